```python
import jax, jax.numpy as jnp
from jax import lax
import numpy as np

D_MODEL = 1024
BATCH = 8
SEQ = 2048
DEPTH = 1

HEAD_DIM = 64
N_FOX_HEADS = 8
N_SWA_HEADS = 8
N_SWA_KV_HEADS = 2
SWA_GROUP = N_SWA_HEADS // N_SWA_KV_HEADS
D_FOX = N_FOX_HEADS * HEAD_DIM
D_SWA = N_SWA_HEADS * HEAD_DIM
D_SWA_KV = N_SWA_KV_HEADS * HEAD_DIM
D_MIX = D_FOX + D_SWA
D_IN = 3 * D_FOX + N_FOX_HEADS + D_SWA + 2 * D_SWA_KV
D_FF = 4 * D_MODEL
D_PLE = 256
WINDOW = 128
Q_BLOCK = 128
N_BUCKETS = 32
MAX_DISTANCE = 128
RMS_EPS = 1e-6

kernel_name = "hymba_fox_swa_sandwich_ple_layer"


def rms_norm(x, g):
    xf = x.astype(jnp.float32)
    y = xf * lax.rsqrt(jnp.mean(xf * xf, axis=-1, keepdims=True) + RMS_EPS)
    return (y * g.astype(jnp.float32)).astype(x.dtype)


def t5_bucket(n):
    max_exact = N_BUCKETS // 2
    large = max_exact + (np.log(np.maximum(n, 1) / max_exact)
                         / np.log(MAX_DISTANCE / max_exact)
                         * (N_BUCKETS - max_exact)).astype(np.int32)
    large = np.minimum(large, N_BUCKETS - 1)
    return np.where(n < max_exact, n, large).astype(np.int32)


def forgetting_attention(q, k, v, log_f):
    B, S, H, D = q.shape
    c = jnp.cumsum(log_f, axis=1).transpose(0, 2, 1)
    scale = D ** -0.5
    outs = []
    for blk in range(S // Q_BLOCK):
        lo, hi = blk * Q_BLOCK, (blk + 1) * Q_BLOCK
        s = jnp.einsum('bqhd,bkhd->bhqk', q[:, lo:hi], k[:, :hi]).astype(jnp.float32) * scale
        s = s + c[:, :, lo:hi, None] - c[:, :, None, :hi]
        causal = np.arange(lo, hi)[:, None] >= np.arange(hi)[None, :]
        s = jnp.where(causal, s, -jnp.inf)
        probs = jax.nn.softmax(s, axis=-1).astype(v.dtype)
        outs.append(jnp.einsum('bhqk,bkhd->bqhd', probs, v[:, :hi]))
    return jnp.concatenate(outs, axis=1).reshape(B, S, H * D)


def sliding_window_attention(q, k, v, sinks, rel_bias):
    B, S, Hq, D = q.shape
    Hkv, G, Q = N_SWA_KV_HEADS, SWA_GROUP, Q_BLOCK
    nb = S // Q
    qb = q.reshape(B, nb, Q, Hkv, G, D)

    def band(t):
        pad = jnp.pad(t, ((0, 0), (Q, 0), (0, 0), (0, 0)))
        prev = pad[:, :S].reshape(B, nb, Q, Hkv, D)
        cur = t.reshape(B, nb, Q, Hkv, D)
        return jnp.concatenate([prev, cur], axis=2)

    kb, vb = band(k), band(v)
    s = jnp.einsum('bnqkgd,bnskd->bnkgqs', qb, kb).astype(jnp.float32) * (D ** -0.5)
    i = np.arange(Q)[:, None]
    j = np.arange(2 * Q)[None, :]
    dist = i + Q - j
    in_window = (dist >= 0) & (dist < WINDOW)
    bucket = t5_bucket(np.clip(dist, 0, None))
    bias = jnp.transpose(rel_bias[bucket], (2, 0, 1)).reshape(Hkv, G, Q, 2 * Q)
    s = s + bias.astype(jnp.float32)
    key_pos = np.arange(nb)[:, None] * Q - Q + np.arange(2 * Q)[None, :]
    valid = in_window[None, :, :] & (key_pos >= 0)[:, None, :]
    s = jnp.where(valid[None, :, None, None], s, -jnp.inf)
    sink = sinks.astype(jnp.float32).reshape(Hkv, G)[None, None, :, :, None, None]
    m = jnp.maximum(jnp.max(s, axis=-1, keepdims=True), sink)
    e = jnp.exp(s - m)
    denom = jnp.sum(e, axis=-1, keepdims=True) + jnp.exp(sink - m)
    probs = (e / denom).astype(v.dtype)
    o = jnp.einsum('bnkgqs,bnskd->bnqkgd', probs, vb)
    return o.reshape(B, S, Hq * D)


def _fwd_setup_inputs(seed: int = 0) -> dict:
    key = jax.random.key(seed)
    ks = jax.random.split(key, 20)
    f32 = jnp.float32
    nrm = lambda k, shape, s: jax.random.normal(k, shape, f32) * s
    gain = lambda k, shape: 1.0 + 0.05 * jax.random.normal(k, shape, f32)
    return {
        "x": jax.random.normal(ks[0], (BATCH, SEQ, D_MODEL), f32),
        "p": jax.random.normal(ks[1], (DEPTH, BATCH, SEQ, D_PLE), f32),
        "w_in": nrm(ks[2], (DEPTH, D_MODEL, D_IN), D_MODEL ** -0.5),
        "b_forget": 1.0 + 0.5 * jax.random.normal(ks[3], (DEPTH, N_FOX_HEADS), f32),
        "w_out": nrm(ks[4], (DEPTH, D_MIX, D_MODEL), D_MIX ** -0.5),
        "rel_bias": nrm(ks[5], (N_BUCKETS, N_SWA_HEADS), 0.5),
        "swa_sinks": nrm(ks[6], (DEPTH, N_SWA_HEADS), 0.5),
        "g_attn_pre": gain(ks[7], (DEPTH, D_MODEL)),
        "g_attn_post": gain(ks[8], (DEPTH, D_MODEL)),
        "w_ff1": nrm(ks[9], (DEPTH, D_MODEL, D_FF), D_MODEL ** -0.5),
        "w_ff2": nrm(ks[10], (DEPTH, D_FF, D_MODEL), D_FF ** -0.5),
        "g_ff_pre": gain(ks[11], (DEPTH, D_MODEL)),
        "g_ff_post": gain(ks[12], (DEPTH, D_MODEL)),
        "w_ple": nrm(ks[13], (DEPTH, D_PLE, D_MODEL), D_PLE ** -0.5),
        "w_ple_gate": nrm(ks[14], (DEPTH, D_MODEL, D_MODEL), D_MODEL ** -0.5),
        "g_ple_post": gain(ks[15], (DEPTH, D_MODEL)),
    }


def _fwd_reference(x, p, w_in, b_forget, w_out, rel_bias, swa_sinks, g_attn_pre, g_attn_post,
              w_ff1, w_ff2, g_ff_pre, g_ff_post, w_ple, w_ple_gate, g_ple_post):
    B, S, _ = x.shape
    h = x
    splits = np.cumsum([D_FOX, D_FOX, D_FOX, N_FOX_HEADS, D_SWA, D_SWA_KV])
    for i in range(DEPTH):
        a = rms_norm(h, g_attn_pre[i])
        z = a @ w_in[i]
        fq, fk, fv, ff, sq, sk, sv = jnp.split(z, splits, axis=-1)
        log_f = jax.nn.log_sigmoid(ff.astype(jnp.float32) + b_forget[i].astype(jnp.float32))
        fox = forgetting_attention(
            fq.reshape(B, S, N_FOX_HEADS, HEAD_DIM),
            fk.reshape(B, S, N_FOX_HEADS, HEAD_DIM),
            fv.reshape(B, S, N_FOX_HEADS, HEAD_DIM), log_f)
        swa = sliding_window_attention(
            sq.reshape(B, S, N_SWA_HEADS, HEAD_DIM),
            sk.reshape(B, S, N_SWA_KV_HEADS, HEAD_DIM),
            sv.reshape(B, S, N_SWA_KV_HEADS, HEAD_DIM), swa_sinks[i], rel_bias)
        mix = jnp.concatenate([fox, swa], axis=-1) @ w_out[i]
        h = h + rms_norm(mix, g_attn_post[i])
        m = rms_norm(h, g_ff_pre[i])
        y = jnp.square(jax.nn.relu(m @ w_ff1[i])) @ w_ff2[i]
        h = h + rms_norm(y, g_ff_post[i])
        gate = jax.nn.sigmoid(h @ w_ple_gate[i])
        e = (p[i].astype(h.dtype) @ w_ple[i]) * gate
        h = h + rms_norm(e, g_ple_post[i])
    return h


import jax as _jax
import jax.numpy as _jnp

TWIN_FORMAT = 'train_step'
FWD_PARAMS = ['x', 'p', 'w_in', 'b_forget', 'w_out', 'rel_bias', 'swa_sinks', 'g_attn_pre', 'g_attn_post', 'w_ff1', 'w_ff2', 'g_ff_pre', 'g_ff_post', 'w_ple', 'w_ple_gate', 'g_ple_post']
TWIN_WEIGHTS = ['w_in', 'b_forget', 'w_out', 'rel_bias', 'swa_sinks', 'g_attn_pre', 'g_attn_post', 'w_ff1', 'w_ff2', 'g_ff_pre', 'g_ff_post', 'w_ple', 'w_ple_gate', 'g_ple_post']
TWIN_DIFF_INPUT = 'x'
TWIN_INPUTS = ['x', 'p', 'w_in', 'b_forget', 'w_out', 'rel_bias', 'swa_sinks', 'g_attn_pre', 'g_attn_post', 'w_ff1', 'w_ff2', 'g_ff_pre', 'g_ff_post', 'w_ple', 'w_ple_gate', 'g_ple_post', 'loss_target', 'm_w_in', 'm_b_forget', 'm_w_out', 'm_rel_bias', 'm_swa_sinks', 'm_g_attn_pre', 'm_g_attn_post', 'm_w_ff1', 'm_w_ff2', 'm_g_ff_pre', 'm_g_ff_post', 'm_w_ple', 'm_w_ple_gate', 'm_g_ple_post', 'v_w_in', 'v_b_forget', 'v_w_out', 'v_rel_bias', 'v_swa_sinks', 'v_g_attn_pre', 'v_g_attn_post', 'v_w_ff1', 'v_w_ff2', 'v_g_ff_pre', 'v_g_ff_post', 'v_w_ple', 'v_w_ple_gate', 'v_g_ple_post']
TWIN_OUTPUTS = ['loss', 'grad_x', 'grad_w_in', 'grad_b_forget', 'grad_w_out', 'grad_rel_bias', 'grad_swa_sinks', 'grad_g_attn_pre', 'grad_g_attn_post', 'grad_w_ff1', 'grad_w_ff2', 'grad_g_ff_pre', 'grad_g_ff_post', 'grad_w_ple', 'grad_w_ple_gate', 'grad_g_ple_post', 'delta_w_in', 'delta_b_forget', 'delta_w_out', 'delta_rel_bias', 'delta_swa_sinks', 'delta_g_attn_pre', 'delta_g_attn_post', 'delta_w_ff1', 'delta_w_ff2', 'delta_g_ff_pre', 'delta_g_ff_post', 'delta_w_ple', 'delta_w_ple_gate', 'delta_g_ple_post', 'new_m_w_in', 'new_m_b_forget', 'new_m_w_out', 'new_m_rel_bias', 'new_m_swa_sinks', 'new_m_g_attn_pre', 'new_m_g_attn_post', 'new_m_w_ff1', 'new_m_w_ff2', 'new_m_g_ff_pre', 'new_m_g_ff_post', 'new_m_w_ple', 'new_m_w_ple_gate', 'new_m_g_ple_post', 'new_v_w_in', 'new_v_b_forget', 'new_v_w_out', 'new_v_rel_bias', 'new_v_swa_sinks', 'new_v_g_attn_pre', 'new_v_g_attn_post', 'new_v_w_ff1', 'new_v_w_ff2', 'new_v_g_ff_pre', 'new_v_g_ff_post', 'new_v_w_ple', 'new_v_w_ple_gate', 'new_v_g_ple_post']
TWIN_LEAF_KINDS = {'loss': 'loss', 'grad_x': 'grad_x', 'grad_w_in': 'grad_w', 'grad_b_forget': 'grad_w', 'grad_w_out': 'grad_w', 'grad_rel_bias': 'grad_w', 'grad_swa_sinks': 'grad_w', 'grad_g_attn_pre': 'grad_w', 'grad_g_attn_post': 'grad_w', 'grad_w_ff1': 'grad_w', 'grad_w_ff2': 'grad_w', 'grad_g_ff_pre': 'grad_w', 'grad_g_ff_post': 'grad_w', 'grad_w_ple': 'grad_w', 'grad_w_ple_gate': 'grad_w', 'grad_g_ple_post': 'grad_w', 'delta_w_in': 'delta_w', 'delta_b_forget': 'delta_w', 'delta_w_out': 'delta_w', 'delta_rel_bias': 'delta_w', 'delta_swa_sinks': 'delta_w', 'delta_g_attn_pre': 'delta_w', 'delta_g_attn_post': 'delta_w', 'delta_w_ff1': 'delta_w', 'delta_w_ff2': 'delta_w', 'delta_g_ff_pre': 'delta_w', 'delta_g_ff_post': 'delta_w', 'delta_w_ple': 'delta_w', 'delta_w_ple_gate': 'delta_w', 'delta_g_ple_post': 'delta_w', 'new_m_w_in': 'new_m', 'new_m_b_forget': 'new_m', 'new_m_w_out': 'new_m', 'new_m_rel_bias': 'new_m', 'new_m_swa_sinks': 'new_m', 'new_m_g_attn_pre': 'new_m', 'new_m_g_attn_post': 'new_m', 'new_m_w_ff1': 'new_m', 'new_m_w_ff2': 'new_m', 'new_m_g_ff_pre': 'new_m', 'new_m_g_ff_post': 'new_m', 'new_m_w_ple': 'new_m', 'new_m_w_ple_gate': 'new_m', 'new_m_g_ple_post': 'new_m', 'new_v_w_in': 'new_v', 'new_v_b_forget': 'new_v', 'new_v_w_out': 'new_v', 'new_v_rel_bias': 'new_v', 'new_v_swa_sinks': 'new_v', 'new_v_g_attn_pre': 'new_v', 'new_v_g_attn_post': 'new_v', 'new_v_w_ff1': 'new_v', 'new_v_w_ff2': 'new_v', 'new_v_g_ff_pre': 'new_v', 'new_v_g_ff_post': 'new_v', 'new_v_w_ple': 'new_v', 'new_v_w_ple_gate': 'new_v', 'new_v_g_ple_post': 'new_v'}


def _forward(args):
    return _fwd_reference(*[args[k] for k in FWD_PARAMS])


def _output_shape():
    out = _jax.eval_shape(lambda: _forward(_fwd_setup_inputs(0)))
    return out.shape, out.dtype

N_MICROBATCH = 1
ADAM_LR = 0.001
ADAM_B1 = 0.9
ADAM_B2 = 0.999
ADAM_EPS = 1e-08
ADAM_WD = 0.01
ADAM_STEP = 10
PER_EXAMPLE_BATCH_AXIS = {'x': 0, 'p': 1, 'loss_target': 0}
SHARED_INPUTS = []
_WEIGHT_DTYPES = {'w_in': _jnp.float32, 'b_forget': _jnp.float32, 'w_out': _jnp.float32, 'rel_bias': _jnp.float32, 'swa_sinks': _jnp.float32, 'g_attn_pre': _jnp.float32, 'g_attn_post': _jnp.float32, 'w_ff1': _jnp.float32, 'w_ff2': _jnp.float32, 'g_ff_pre': _jnp.float32, 'g_ff_post': _jnp.float32, 'w_ple': _jnp.float32, 'w_ple_gate': _jnp.float32, 'g_ple_post': _jnp.float32}
MOMENT_SCALE = {'w_in': 3.776825e-01, 'b_forget': 1.868188e+00, 'w_out': 5.735791e-01, 'rel_bias': 2.208291e-01, 'swa_sinks': 8.746545e-02, 'g_attn_pre': 6.409847e-01, 'g_attn_post': 1.596088e+01, 'w_ff1': 2.740302e-01, 'w_ff2': 6.007091e-01, 'g_ff_pre': 5.310703e-01, 'g_ff_post': 1.644773e+01, 'w_ple': 2.581812e-01, 'w_ple_gate': 1.740711e-01, 'g_ple_post': 1.629377e+01}


def _to_microbatches(a, axis):
    t = _jnp.moveaxis(a, axis, 0)
    t = t.reshape((N_MICROBATCH, t.shape[0] // N_MICROBATCH) + t.shape[1:])
    return _jnp.moveaxis(t, 1, axis + 1)


def setup_inputs(seed: int = 0) -> dict:
    inp = _fwd_setup_inputs(seed)
    key = _jax.random.fold_in(_jax.random.key(seed), 7919)
    shape, _ = _output_shape()
    out = dict(inp)
    out["loss_target"] = _jax.random.normal(_jax.random.fold_in(key, 0), shape, _jnp.float32)
    for i, name in enumerate(TWIN_WEIGHTS):
        w = inp[name].astype(_jnp.float32)
        if MOMENT_SCALE is None:
            s = _jnp.sqrt(_jnp.mean(_jnp.square(w)) + 1e-30)
        else:
            s = MOMENT_SCALE[name]
        km, kv = _jax.random.split(_jax.random.fold_in(key, i + 1))
        out[name] = w
        out["m_" + name] = s * _jax.random.normal(km, w.shape, _jnp.float32)
        out["v_" + name] = (s * s) * _jax.random.uniform(kv, w.shape, _jnp.float32, 0.5, 1.5)
    if N_MICROBATCH > 1:
        for name, axis in PER_EXAMPLE_BATCH_AXIS.items():
            out[name] = _to_microbatches(out[name], axis)
    return {'x': out['x'], 'p': out['p'], 'w_in': out['w_in'], 'b_forget': out['b_forget'], 'w_out': out['w_out'], 'rel_bias': out['rel_bias'], 'swa_sinks': out['swa_sinks'], 'g_attn_pre': out['g_attn_pre'], 'g_attn_post': out['g_attn_post'], 'w_ff1': out['w_ff1'], 'w_ff2': out['w_ff2'], 'g_ff_pre': out['g_ff_pre'], 'g_ff_post': out['g_ff_post'], 'w_ple': out['w_ple'], 'w_ple_gate': out['w_ple_gate'], 'g_ple_post': out['g_ple_post'], 'loss_target': out['loss_target'], 'm_w_in': out['m_w_in'], 'm_b_forget': out['m_b_forget'], 'm_w_out': out['m_w_out'], 'm_rel_bias': out['m_rel_bias'], 'm_swa_sinks': out['m_swa_sinks'], 'm_g_attn_pre': out['m_g_attn_pre'], 'm_g_attn_post': out['m_g_attn_post'], 'm_w_ff1': out['m_w_ff1'], 'm_w_ff2': out['m_w_ff2'], 'm_g_ff_pre': out['m_g_ff_pre'], 'm_g_ff_post': out['m_g_ff_post'], 'm_w_ple': out['m_w_ple'], 'm_w_ple_gate': out['m_w_ple_gate'], 'm_g_ple_post': out['m_g_ple_post'], 'v_w_in': out['v_w_in'], 'v_b_forget': out['v_b_forget'], 'v_w_out': out['v_w_out'], 'v_rel_bias': out['v_rel_bias'], 'v_swa_sinks': out['v_swa_sinks'], 'v_g_attn_pre': out['v_g_attn_pre'], 'v_g_attn_post': out['v_g_attn_post'], 'v_w_ff1': out['v_w_ff1'], 'v_w_ff2': out['v_w_ff2'], 'v_g_ff_pre': out['v_g_ff_pre'], 'v_g_ff_post': out['v_g_ff_post'], 'v_w_ple': out['v_w_ple'], 'v_w_ple_gate': out['v_w_ple_gate'], 'v_g_ple_post': out['v_g_ple_post']}


def _loss(weights, diff, rest, loss_target):
    with _jax.named_scope("forward"):
        args = {**rest, TWIN_DIFF_INPUT: diff, **{k: w.astype(_WEIGHT_DTYPES[k]) for k, w in weights.items()}}
        y = _forward(args)
    with _jax.named_scope("loss_head"):
        err = _jnp.square(y.astype(_jnp.float32) - loss_target)
        return 0.5 * _jnp.sum(_jnp.mean(err, axis=-1)) if err.ndim else 0.5 * err


def _adamw(w, g, m, v):
    m = ADAM_B1 * m + (1.0 - ADAM_B1) * g
    v = ADAM_B2 * v + (1.0 - ADAM_B2) * _jnp.square(g)
    m_hat = m / (1.0 - ADAM_B1 ** ADAM_STEP)
    v_hat = v / (1.0 - ADAM_B2 ** ADAM_STEP)
    delta = -ADAM_LR * (m_hat / (_jnp.sqrt(v_hat) + ADAM_EPS) + ADAM_WD * w)
    return delta, m, v


def reference(x, p, w_in, b_forget, w_out, rel_bias, swa_sinks, g_attn_pre, g_attn_post, w_ff1, w_ff2, g_ff_pre, g_ff_post, w_ple, w_ple_gate, g_ple_post, loss_target, m_w_in, m_b_forget, m_w_out, m_rel_bias, m_swa_sinks, m_g_attn_pre, m_g_attn_post, m_w_ff1, m_w_ff2, m_g_ff_pre, m_g_ff_post, m_w_ple, m_w_ple_gate, m_g_ple_post, v_w_in, v_b_forget, v_w_out, v_rel_bias, v_swa_sinks, v_g_attn_pre, v_g_attn_post, v_w_ff1, v_w_ff2, v_g_ff_pre, v_g_ff_post, v_w_ple, v_w_ple_gate, v_g_ple_post):
    given = dict(x=x, p=p, w_in=w_in, b_forget=b_forget, w_out=w_out, rel_bias=rel_bias, swa_sinks=swa_sinks, g_attn_pre=g_attn_pre, g_attn_post=g_attn_post, w_ff1=w_ff1, w_ff2=w_ff2, g_ff_pre=g_ff_pre, g_ff_post=g_ff_post, w_ple=w_ple, w_ple_gate=w_ple_gate, g_ple_post=g_ple_post, loss_target=loss_target, m_w_in=m_w_in, m_b_forget=m_b_forget, m_w_out=m_w_out, m_rel_bias=m_rel_bias, m_swa_sinks=m_swa_sinks, m_g_attn_pre=m_g_attn_pre, m_g_attn_post=m_g_attn_post, m_w_ff1=m_w_ff1, m_w_ff2=m_w_ff2, m_g_ff_pre=m_g_ff_pre, m_g_ff_post=m_g_ff_post, m_w_ple=m_w_ple, m_w_ple_gate=m_w_ple_gate, m_g_ple_post=m_g_ple_post, v_w_in=v_w_in, v_b_forget=v_b_forget, v_w_out=v_w_out, v_rel_bias=v_rel_bias, v_swa_sinks=v_swa_sinks, v_g_attn_pre=v_g_attn_pre, v_g_attn_post=v_g_attn_post, v_w_ff1=v_w_ff1, v_w_ff2=v_w_ff2, v_g_ff_pre=v_g_ff_pre, v_g_ff_post=v_g_ff_post, v_w_ple=v_w_ple, v_w_ple_gate=v_w_ple_gate, v_g_ple_post=v_g_ple_post)
    weights = {n: given[n] for n in TWIN_WEIGHTS}
    shared = {n: given[n] for n in SHARED_INPUTS}
    per_example = {n: given[n] for n in ['x', 'p']}
    grad_fn = _jax.value_and_grad(_loss, argnums=(0, 1))

    def one_microbatch(ex, loss_target):
        ex = dict(ex)
        diff = ex.pop(TWIN_DIFF_INPUT)
        return grad_fn(weights, diff, {**shared, **ex}, loss_target)

    if N_MICROBATCH == 1:
        loss, (grad_w, grad_x) = one_microbatch(per_example, given["loss_target"])
    else:
        def body(carry, xs):
            loss_sum, grad_sum = carry
            l_k, (gw_k, gx_k) = one_microbatch(xs[0], xs[1])
            with _jax.named_scope("update"):
                return (loss_sum + l_k, _jax.tree.map(_jnp.add, grad_sum, gw_k)), gx_k

        init = (_jnp.zeros((), _jnp.float32), _jax.tree.map(_jnp.zeros_like, weights))
        (loss, grad_w), grad_x = _jax.lax.scan(body, init, (per_example, given["loss_target"]))
    with _jax.named_scope("update"):
        delta_w, new_m, new_v = {}, {}, {}
        for n in TWIN_WEIGHTS:
            delta_w[n], new_m[n], new_v[n] = _adamw(weights[n], grad_w[n], given["m_" + n], given["v_" + n])
    return (loss, grad_x, *[grad_w[n] for n in TWIN_WEIGHTS], *[delta_w[n] for n in TWIN_WEIGHTS],
            *[new_m[n] for n in TWIN_WEIGHTS], *[new_v[n] for n in TWIN_WEIGHTS])
```

```python
import functools

import numpy as np
import jax
import jax.numpy as jnp
from jax import lax
from jax.experimental import pallas as pl
from jax.experimental.pallas import tpu as pltpu

F32 = jnp.float32
MM = jnp.bfloat16

D_MODEL = 1024
HEAD_DIM = 64
N_HEADS = 8
D_ATT = N_HEADS * HEAD_DIM
D_KV = 128
D_FF = 4096
D_PLE = 256
D_IN = 3 * D_ATT + N_HEADS + D_ATT + 2 * D_KV
N_DEV = 8
FF_CHUNK = D_FF // N_DEV
WINDOW = 128
N_BUCKETS = 32
MAX_DISTANCE = 128
RMS_EPS = 1e-6
Q_SCALE = HEAD_DIM ** -0.5
NEG = -1e30

ADAM_LR = 0.001
ADAM_B1 = 0.9
ADAM_B2 = 0.999
ADAM_EPS = 1e-08
ADAM_WD = 0.01
ADAM_STEP = 10

SLOT_HEAD = (0, 4, 1, 5, 2, 6, 3, 7)
HEAD_SLOT = (0, 2, 4, 6, 1, 3, 5, 7)

VMEM_LIMIT = 56 * 1024 * 1024
MESH = pl.DeviceIdType.MESH

NT = (((1,), (1,)), ((), ()))
TN = (((0,), (0,)), ((), ()))


def _params(*semantics):
    return pltpu.CompilerParams(dimension_semantics=semantics, vmem_limit_bytes=VMEM_LIMIT)


def _resident():
    return pl.BlockSpec(memory_space=pltpu.VMEM)


def _rows(tm, width):
    return pl.BlockSpec((tm, width), lambda i: (i, 0))


def _const(shape):
    return pl.BlockSpec(shape, lambda i: (0,) * len(shape))


def _dot(a, b):
    return jnp.dot(a, b, preferred_element_type=F32)


def _dot_nt(a, b):
    return lax.dot_general(a, b, NT, preferred_element_type=F32)


def _dot_tn(a, b):
    return lax.dot_general(a, b, TN, preferred_element_type=F32)


def _rms(xf):
    r = lax.rsqrt(jnp.mean(xf * xf, axis=-1, keepdims=True) + RMS_EPS)
    return xf * r, r


def _rms_bwd(dout, n, r, g):
    dg = jnp.sum(dout * n, axis=0, keepdims=True)
    dn = dout * g
    dx = r * (dn - n * jnp.mean(dn * n, axis=-1, keepdims=True))
    return dx, dg


def _accumulate(ref, value, step):
    @pl.when(step == 0)
    def _():
        ref[...] = value

    @pl.when(step != 0)
    def _():
        ref[...] += value


def _t5_bucket(n):
    max_exact = N_BUCKETS // 2
    large = max_exact + (np.log(np.maximum(n, 1) / max_exact) / np.log(MAX_DISTANCE / max_exact)
                         * (N_BUCKETS - max_exact)).astype(np.int32)
    large = np.minimum(large, N_BUCKETS - 1)
    return np.where(n < max_exact, n, large).astype(np.int32)


def _swa_bucket_map():
    i = np.arange(WINDOW)[:, None]
    j = np.arange(2 * WINDOW)[None, :]
    dist = i + WINDOW - j
    ok = (dist >= 0) & (dist < WINDOW)
    return np.where(ok, _t5_bucket(np.clip(dist, 0, None)), -1).astype(np.int32)


def _pre_attn(x, g1, wqkv, wswa, wfft, tm):
    S = x.shape[0]

    def body(x_ref, g_ref, wqkv_ref, wswa_ref, wff_ref, a_ref, fqkv_ref, sqkv_ref, fft_ref):
        n, _ = _rms(x_ref[...])
        a = (n * g_ref[...]).astype(MM)
        a_ref[...] = a
        fqkv_ref[:, :D_ATT] = (_dot(a, wqkv_ref[:, :D_ATT]) * Q_SCALE).astype(MM)
        fqkv_ref[:, D_ATT:] = _dot(a, wqkv_ref[:, D_ATT:]).astype(MM)
        sqkv_ref[:, :D_ATT] = (_dot(a, wswa_ref[:, :D_ATT]) * Q_SCALE).astype(MM)
        sqkv_ref[:, D_ATT:] = _dot(a, wswa_ref[:, D_ATT:]).astype(MM)
        fft_ref[...] = _dot_nt(wff_ref[...], a)

    return pl.pallas_call(
        body, name="pre_attn", grid=(S // tm,),
        in_specs=[_rows(tm, D_MODEL), _const((1, D_MODEL)), _resident(), _resident(), _resident()],
        out_specs=[_rows(tm, D_MODEL), _rows(tm, 3 * D_ATT), _rows(tm, D_ATT + 2 * D_KV),
                   pl.BlockSpec((16, tm), lambda i: (0, i))],
        out_shape=[jax.ShapeDtypeStruct((S, D_MODEL), MM), jax.ShapeDtypeStruct((S, 3 * D_ATT), MM),
                   jax.ShapeDtypeStruct((S, D_ATT + 2 * D_KV), MM), jax.ShapeDtypeStruct((16, S), F32)],
        compiler_params=_params("parallel"),
    )(x, g1, wqkv, wswa, wfft)


def _lane_scan(v, reverse):
    S = v.shape[1]
    lane = lax.broadcasted_iota(jnp.int32, v.shape, 1)
    k = 1
    while k < S:
        if reverse:
            v = v + jnp.where(lane < S - k, pltpu.roll(v, S - k, axis=1), 0.0)
        else:
            v = v + jnp.where(lane >= k, pltpu.roll(v, k, axis=1), 0.0)
        k *= 2
    return v


def _forget_cumsum(fft, bcol):
    def body(f_ref, b_ref, c_ref):
        z = f_ref[...] + b_ref[...]
        log_f = jnp.minimum(z, 0.0) - jnp.log1p(jnp.exp(-jnp.abs(z)))
        c_ref[...] = _lane_scan(log_f, reverse=False)

    return pl.pallas_call(
        body, name="forget_cumsum", out_shape=jax.ShapeDtypeStruct(fft.shape, F32),
        in_specs=[_resident(), _resident()], out_specs=_resident(),
    )(fft, bcol)


def _forget_bwd(dc_row, fft, bcol):
    def body(dc_ref, f_ref, b_ref, dff_ref, db_ref):
        z = f_ref[...] + b_ref[...]
        dlog_f = _lane_scan(dc_ref[...], reverse=True)
        dff = dlog_f * (1.0 / (1.0 + jnp.exp(z)))
        dff_ref[...] = dff
        db_ref[...] = jnp.sum(dff, axis=1, keepdims=True)

    return pl.pallas_call(
        body, name="forget_bwd",
        out_shape=[jax.ShapeDtypeStruct(fft.shape, F32), jax.ShapeDtypeStruct((fft.shape[0], 1), F32)],
        in_specs=[_resident()] * 3, out_specs=[_resident()] * 2,
    )(dc_row, fft, bcol)


def _head_select(shape, upper):
    lane = lax.broadcasted_iota(jnp.int32, shape, 1)
    return lane >= HEAD_DIM if upper else lane < HEAD_DIM


def _fox_fwd(fqkv, c_col, c_row3, t):
    S = fqkv.shape[0]

    def body(q_ref, k_ref, v_ref, cq_ref, ck_ref, o_ref, lse_ref):
        qi = pl.program_id(0)
        row = lax.broadcasted_iota(jnp.int32, (t, t), 0)
        col = lax.broadcasted_iota(jnp.int32, (t, t), 1)
        for pr in range(N_HEADS // 2):
            lanes = slice(pr * 128, (pr + 1) * 128)
            q2 = q_ref[:, lanes]
            res = []
            for hh in range(2):
                h = 2 * pr + hh
                qh = jnp.where(_head_select((t, 128), hh), q2, jnp.zeros_like(q2))
                cq = cq_ref[:, h:h + 1]

                def block(kb, carry, diagonal, qh=qh, cq=cq, h=h, lanes=lanes):
                    m, l, acc = carry
                    rows = pl.ds(pl.multiple_of(kb * t, t), t)
                    s = _dot_nt(qh, k_ref[rows, lanes]) + cq - ck_ref[h, pl.ds(kb, 1), :]
                    if diagonal:
                        s = jnp.where(row >= col, s, NEG)
                    m_new = jnp.maximum(m, jnp.max(s, axis=-1, keepdims=True))
                    p = jnp.exp(s - m_new)
                    alpha = jnp.exp(m - m_new)
                    l = alpha * l + jnp.sum(p, axis=-1, keepdims=True)
                    acc = alpha * acc + _dot(p.astype(MM), v_ref[rows, lanes])
                    return m_new, l, acc

                init = (jnp.full((t, 1), NEG, F32), jnp.zeros((t, 1), F32), jnp.zeros((t, 128), F32))
                carry = lax.fori_loop(0, qi, functools.partial(block, diagonal=False), init)
                m, l, acc = block(qi, carry, diagonal=True)
                res.append(acc / l)
                lse_ref[:, h:h + 1] = m + jnp.log(l)
            o_ref[:, lanes] = jnp.where(_head_select((t, 128), 0), res[0], res[1]).astype(MM)

    return pl.pallas_call(
        body, name="fox_fwd", grid=(S // t,),
        in_specs=[pl.BlockSpec((t, D_ATT), lambda i: (i, 0)), pl.BlockSpec((S, D_ATT), lambda i: (0, 1)),
                  pl.BlockSpec((S, D_ATT), lambda i: (0, 2)), _rows(t, N_HEADS), _resident()],
        out_specs=[_rows(t, D_ATT), _rows(t, N_HEADS)],
        out_shape=[jax.ShapeDtypeStruct((S, D_ATT), MM), jax.ShapeDtypeStruct((S, N_HEADS), F32)],
        compiler_params=_params("parallel"),
    )(fqkv, fqkv, fqkv, c_col, c_row3)


def _swa_bias(rel_bias_slot, bucket):
    def body(rb_ref, bk_ref, out_ref):
        bk = bk_ref[...]
        for s in range(N_HEADS):
            acc = jnp.where(bk < 0, NEG, 0.0).astype(F32)
            for b in range(N_BUCKETS):
                acc = jnp.where(bk == b, rb_ref[b, s], acc)
            out_ref[s] = acc

    return pl.pallas_call(
        body, name="swa_bias", out_shape=jax.ShapeDtypeStruct((N_HEADS, WINDOW, 2 * WINDOW), F32),
        in_specs=[pl.BlockSpec(memory_space=pltpu.SMEM), _resident()], out_specs=_resident(),
    )(rel_bias_slot, bucket)


def _swa_specs(S):
    q = pl.BlockSpec((WINDOW, D_ATT), lambda n: (n, 0))
    kp = pl.BlockSpec((WINDOW, D_KV), lambda n: (jnp.maximum(n - 1, 0), 4))
    kc = pl.BlockSpec((WINDOW, D_KV), lambda n: (n, 4))
    vp = pl.BlockSpec((WINDOW, D_KV), lambda n: (jnp.maximum(n - 1, 0), 5))
    vc = pl.BlockSpec((WINDOW, D_KV), lambda n: (n, 5))
    return [q, kp, kc, vp, vc]


def _swa_fwd(sqkv, biasm, sinks_slot):
    S = sqkv.shape[0]

    def body(q_ref, kp_ref, kc_ref, vp_ref, vc_ref, bias_ref, sink_ref, o_ref, lse_ref):
        n = pl.program_id(0)
        no_prev = jnp.where(n > 0, 0.0, NEG)
        for j in range(N_HEADS // 2):
            lanes = slice(j * 128, (j + 1) * 128)
            q2 = q_ref[:, lanes]
            res = []
            for hh in range(2):
                s_ = 2 * j + hh
                qh = jnp.where(_head_select((WINDOW, 128), hh), q2, jnp.zeros_like(q2))
                sp = _dot_nt(qh, kp_ref[...]) + bias_ref[s_, :, :WINDOW] + no_prev
                sc = _dot_nt(qh, kc_ref[...]) + bias_ref[s_, :, WINDOW:]
                sink = sink_ref[s_]
                m = jnp.maximum(jnp.maximum(jnp.max(sp, axis=-1, keepdims=True),
                                            jnp.max(sc, axis=-1, keepdims=True)), sink)
                ep = jnp.exp(sp - m)
                ec = jnp.exp(sc - m)
                den = (jnp.sum(ep, axis=-1, keepdims=True) + jnp.sum(ec, axis=-1, keepdims=True)
                       + jnp.exp(sink - m))
                res.append((_dot(ep.astype(MM), vp_ref[...]) + _dot(ec.astype(MM), vc_ref[...])) / den)
                lse_ref[:, s_:s_ + 1] = m + jnp.log(den)
            o_ref[:, lanes] = jnp.where(_head_select((WINDOW, 128), 0), res[0], res[1]).astype(MM)

    return pl.pallas_call(
        body, name="swa_fwd", grid=(S // WINDOW,),
        in_specs=_swa_specs(S) + [_resident(), pl.BlockSpec(memory_space=pltpu.SMEM)],
        out_specs=[_rows(WINDOW, D_ATT), _rows(WINDOW, N_HEADS)],
        out_shape=[jax.ShapeDtypeStruct((S, D_ATT), MM), jax.ShapeDtypeStruct((S, N_HEADS), F32)],
        compiler_params=_params("parallel"),
    )(sqkv, sqkv, sqkv, sqkv, sqkv, biasm, sinks_slot)


def _post_attn(x, fox_o, swa_o, wout_fox, wout_swa, g2, g3, tm):
    S = x.shape[0]

    def body(x_ref, fo_ref, so_ref, wf_ref, ws_ref, g2_ref, g3_ref, mix_ref, h1_ref, m_ref):
        mix = _dot(fo_ref[...], wf_ref[...]) + _dot(so_ref[...], ws_ref[...])
        mix_ref[...] = mix
        n2, _ = _rms(mix)
        h1 = x_ref[...] + n2 * g2_ref[...]
        h1_ref[...] = h1
        n3, _ = _rms(h1)
        m_ref[...] = (n3 * g3_ref[...]).astype(MM)

    return pl.pallas_call(
        body, name="post_attn", grid=(S // tm,),
        in_specs=[_rows(tm, D_MODEL), _rows(tm, D_ATT), _rows(tm, D_ATT), _resident(), _resident(),
                  _const((1, D_MODEL)), _const((1, D_MODEL))],
        out_specs=[_rows(tm, D_MODEL)] * 3,
        out_shape=[jax.ShapeDtypeStruct((S, D_MODEL), F32), jax.ShapeDtypeStruct((S, D_MODEL), F32),
                   jax.ShapeDtypeStruct((S, D_MODEL), MM)],
        compiler_params=_params("parallel"),
    )(x, fox_o, swa_o, wout_fox, wout_swa, g2, g3)


def _mlp_fwd(m, h1, w1, w2, g4, tm):
    S = m.shape[0]

    def body(m_ref, h1_ref, w1_ref, w2_ref, g4_ref, u_ref, y_ref, h2_ref):
        mb = m_ref[...]
        y = jnp.zeros((tm, D_MODEL), F32)
        for j in range(N_DEV):
            cols = slice(j * FF_CHUNK, (j + 1) * FF_CHUNK)
            u = _dot(mb, w1_ref[j])
            u_ref[:, cols] = u.astype(MM)
            y = y + _dot(jnp.square(jnp.maximum(u, 0.0)).astype(MM), w2_ref[cols, :])
        y_ref[...] = y
        n4, _ = _rms(y)
        h2_ref[...] = h1_ref[...] + n4 * g4_ref[...]

    return pl.pallas_call(
        body, name="mlp_fwd", grid=(S // tm,),
        in_specs=[_rows(tm, D_MODEL), _rows(tm, D_MODEL), _resident(), _resident(), _const((1, D_MODEL))],
        out_specs=[_rows(tm, D_FF), _rows(tm, D_MODEL), _rows(tm, D_MODEL)],
        out_shape=[jax.ShapeDtypeStruct((S, D_FF), MM), jax.ShapeDtypeStruct((S, D_MODEL), F32),
                   jax.ShapeDtypeStruct((S, D_MODEL), F32)],
        compiler_params=_params("parallel"),
    )(m, h1, w1, w2, g4)


def _ple_loss(h2, p, target, wg, wple, g5, tm):
    S = h2.shape[0]

    def body(h2_ref, p_ref, t_ref, wg_ref, wp_ref, g5_ref, dh2_ref, dpe_ref, dgl_ref, dg5_ref, loss_ref):
        i = pl.program_id(0)
        h2 = h2_ref[...]
        gate = jax.nn.sigmoid(_dot(h2.astype(MM), wg_ref[...]))
        pe = _dot(p_ref[...].astype(MM), wp_ref[...])
        n5, r5 = _rms(pe * gate)
        g5 = g5_ref[...]
        diff = h2 + n5 * g5 - t_ref[...]
        per_token = jnp.mean(jnp.square(diff), axis=-1, keepdims=True)
        _accumulate(loss_ref, 0.5 * jnp.sum(per_token, axis=0, keepdims=True), i)
        dh3 = diff * (1.0 / D_MODEL)
        de, dg5 = _rms_bwd(dh3, n5, r5, g5)
        _accumulate(dg5_ref, dg5, i)
        dpe_ref[...] = (de * gate).astype(MM)
        dgl = (de * pe * gate * (1.0 - gate)).astype(MM)
        dgl_ref[...] = dgl
        dh2_ref[...] = dh3 + _dot_nt(dgl, wg_ref[...])

    return pl.pallas_call(
        body, name="ple_loss", grid=(S // tm,),
        in_specs=[_rows(tm, D_MODEL), _rows(tm, D_PLE), _rows(tm, D_MODEL), _resident(), _resident(),
                  _const((1, D_MODEL))],
        out_specs=[_rows(tm, D_MODEL), _rows(tm, D_MODEL), _rows(tm, D_MODEL), _const((1, D_MODEL)), _const((1, 1))],
        out_shape=[jax.ShapeDtypeStruct((S, D_MODEL), F32), jax.ShapeDtypeStruct((S, D_MODEL), MM),
                   jax.ShapeDtypeStruct((S, D_MODEL), MM), jax.ShapeDtypeStruct((1, D_MODEL), F32),
                   jax.ShapeDtypeStruct((1, 1), F32)],
        compiler_params=_params("arbitrary"),
    )(h2, p, target, wg, wple, g5)


def _mlp_bwd(dh2, y, h1, u, w1, w2, g4, g3, tm):
    S = dh2.shape[0]

    def body(dh2_ref, y_ref, h1_ref, u_ref, w1_ref, w2_ref, g4_ref, g3_ref,
             dh1_ref, dy_ref, du_ref, dg4_ref, dg3_ref):
        i = pl.program_id(0)
        dh2 = dh2_ref[...]
        n4, r4 = _rms(y_ref[...])
        dy, dg4 = _rms_bwd(dh2, n4, r4, g4_ref[...])
        _accumulate(dg4_ref, dg4, i)
        dyb = dy.astype(MM)
        dy_ref[...] = dyb
        dm = jnp.zeros((tm, D_MODEL), F32)
        for j in range(N_DEV):
            cols = slice(j * FF_CHUNK, (j + 1) * FF_CHUNK)
            dact = _dot_nt(dyb, w2_ref[cols, :])
            du = (dact * (2.0 * jnp.maximum(u_ref[:, cols].astype(F32), 0.0))).astype(MM)
            du_ref[:, cols] = du
            dm = dm + _dot_nt(du, w1_ref[j])
        n3, r3 = _rms(h1_ref[...])
        dx, dg3 = _rms_bwd(dm, n3, r3, g3_ref[...])
        _accumulate(dg3_ref, dg3, i)
        dh1_ref[...] = dh2 + dx

    return pl.pallas_call(
        body, name="mlp_bwd", grid=(S // tm,),
        in_specs=[_rows(tm, D_MODEL), _rows(tm, D_MODEL), _rows(tm, D_MODEL), _rows(tm, D_FF),
                  _resident(), _resident(), _const((1, D_MODEL)), _const((1, D_MODEL))],
        out_specs=[_rows(tm, D_MODEL), _rows(tm, D_MODEL), _rows(tm, D_FF), _const((1, D_MODEL)),
                   _const((1, D_MODEL))],
        out_shape=[jax.ShapeDtypeStruct((S, D_MODEL), F32), jax.ShapeDtypeStruct((S, D_MODEL), MM),
                   jax.ShapeDtypeStruct((S, D_FF), MM), jax.ShapeDtypeStruct((1, D_MODEL), F32),
                   jax.ShapeDtypeStruct((1, D_MODEL), F32)],
        compiler_params=_params("arbitrary"),
    )(dh2, y, h1, u, w1, w2, g4, g3)


def _attn_out_bwd(dh1, mix, fox_o, swa_o, wout_fox, wout_swa, g2, head_rows, head_cols, tm):
    S = dh1.shape[0]

    def body(dh1_ref, mix_ref, fo_ref, so_ref, wf_ref, ws_ref, g2_ref, er_ref, ec_ref,
             dmix_ref, dcat_ref, drow_ref, dcol_ref, dg2_ref):
        i = pl.program_id(0)
        n2, r2 = _rms(mix_ref[...])
        dmix, dg2 = _rms_bwd(dh1_ref[...], n2, r2, g2_ref[...])
        _accumulate(dg2_ref, dg2, i)
        dmb = dmix.astype(MM)
        dmix_ref[...] = dmb
        dfo = _dot_nt(dmb, wf_ref[...]).astype(MM)
        dso = _dot_nt(dmb, ws_ref[...]).astype(MM)
        dcat_ref[:, :D_ATT] = dfo
        dcat_ref[:, D_ATT:] = dso
        hi = lax.Precision.HIGHEST
        prod_f = dfo.astype(F32) * fo_ref[...].astype(F32)
        prod_s = dso.astype(F32) * so_ref[...].astype(F32)
        drow_ref[...] = lax.dot_general(er_ref[...], prod_f, NT, precision=hi, preferred_element_type=F32)
        dcol_ref[...] = jnp.dot(prod_s, ec_ref[...], precision=hi, preferred_element_type=F32)

    return pl.pallas_call(
        body, name="attn_out_bwd", grid=(S // tm,),
        in_specs=[_rows(tm, D_MODEL), _rows(tm, D_MODEL), _rows(tm, D_ATT), _rows(tm, D_ATT), _resident(),
                  _resident(), _const((1, D_MODEL)), _resident(), _resident()],
        out_specs=[_rows(tm, D_MODEL), _rows(tm, D_MODEL), pl.BlockSpec((N_HEADS, tm), lambda i: (0, i)),
                   _rows(tm, N_HEADS), _const((1, D_MODEL))],
        out_shape=[jax.ShapeDtypeStruct((S, D_MODEL), MM), jax.ShapeDtypeStruct((S, D_MODEL), MM),
                   jax.ShapeDtypeStruct((N_HEADS, S), F32), jax.ShapeDtypeStruct((S, N_HEADS), F32),
                   jax.ShapeDtypeStruct((1, D_MODEL), F32)],
        compiler_params=_params("arbitrary"),
    )(dh1, mix, fox_o, swa_o, wout_fox, wout_swa, g2, head_rows, head_cols)


def _fox_bwd(fqkv, dcat, lse_row3, d_row3, c_row3, c_col, t):
    S = fqkv.shape[0]
    n_blk = S // t

    def body(q_ref, k_ref, v_ref, do_ref, lse_ref, dd_ref, cq_ref, ck_ref, dq_ref, dk_ref, dv_ref, dc_ref, dcq_ref):
        kb = pl.program_id(0)

        @pl.when(kb == 0)
        def _():
            dq_ref[...] = jnp.zeros_like(dq_ref)
            dcq_ref[...] = jnp.zeros_like(dcq_ref)

        key = lax.broadcasted_iota(jnp.int32, (t, t), 0)
        qry = lax.broadcasted_iota(jnp.int32, (t, t), 1)
        for pr in range(N_HEADS // 2):
            lanes = slice(pr * 128, (pr + 1) * 128)
            k2 = k_ref[:, lanes]
            v2 = v_ref[:, lanes]
            grads = []
            for hh in range(2):
                h = 2 * pr + hh
                sel = _head_select((t, 128), hh)
                kh = jnp.where(sel, k2, jnp.zeros_like(k2))
                vh = jnp.where(sel, v2, jnp.zeros_like(v2))
                ck = ck_ref[:, h:h + 1]

                def block(qb, carry, diagonal, kh=kh, vh=vh, ck=ck, h=h, lanes=lanes):
                    dk, dv, dc = carry
                    rows = pl.ds(pl.multiple_of(qb * t, t), t)
                    q2 = q_ref[rows, lanes]
                    do2 = do_ref[rows, lanes]
                    s_t = _dot_nt(kh, q2) + cq_ref[h, pl.ds(qb, 1), :] - ck
                    p_t = jnp.exp(s_t - lse_ref[h, pl.ds(qb, 1), :])
                    if diagonal:
                        p_t = jnp.where(qry >= key, p_t, 0.0)
                    ds_t = p_t * (_dot_nt(vh, do2) - dd_ref[h, pl.ds(qb, 1), :])
                    dsb = ds_t.astype(MM)
                    dv = dv + _dot(p_t.astype(MM), do2)
                    dk = dk + _dot(dsb, q2)
                    dc = dc - jnp.sum(ds_t, axis=1, keepdims=True)
                    dq_ref[rows, lanes] += _dot_tn(dsb, kh)
                    dcq_ref[h, pl.ds(qb, 1), :] += jnp.sum(ds_t, axis=0, keepdims=True)
                    return dk, dv, dc

                init = (jnp.zeros((t, 128), F32), jnp.zeros((t, 128), F32), jnp.zeros((t, 1), F32))
                carry = block(kb, init, diagonal=True)
                dk, dv, dc = lax.fori_loop(kb + 1, n_blk, functools.partial(block, diagonal=False), carry)
                grads.append((dk, dv))
                dc_ref[:, h:h + 1] = dc
            low = _head_select((t, 128), 0)
            dk_ref[:, lanes] = jnp.where(low, grads[0][0], grads[1][0]).astype(MM)
            dv_ref[:, lanes] = jnp.where(low, grads[0][1], grads[1][1]).astype(MM)

        @pl.when(kb == n_blk - 1)
        def _():
            dq_ref[...] = dq_ref[...] * Q_SCALE

    return pl.pallas_call(
        body, name="fox_bwd", grid=(n_blk,),
        in_specs=[pl.BlockSpec((S, D_ATT), lambda i: (0, 0)), pl.BlockSpec((t, D_ATT), lambda i: (i, 1)),
                  pl.BlockSpec((t, D_ATT), lambda i: (i, 2)), pl.BlockSpec((S, D_ATT), lambda i: (0, 0)),
                  _resident(), _resident(), _resident(), _rows(t, N_HEADS)],
        out_specs=[_const((S, D_ATT)), _rows(t, D_ATT), _rows(t, D_ATT), _rows(t, N_HEADS),
                   _const((N_HEADS, n_blk, t))],
        out_shape=[jax.ShapeDtypeStruct((S, D_ATT), F32), jax.ShapeDtypeStruct((S, D_ATT), MM),
                   jax.ShapeDtypeStruct((S, D_ATT), MM), jax.ShapeDtypeStruct((S, N_HEADS), F32),
                   jax.ShapeDtypeStruct((N_HEADS, n_blk, t), F32)],
        compiler_params=_params("arbitrary"),
    )(fqkv, fqkv, fqkv, dcat, lse_row3, d_row3, c_row3, c_col)


def _swa_bwd(sqkv, dcat, biasm, sinks_slot, bucket, lse, d_col):
    S = sqkv.shape[0]
    n_blk = S // WINDOW

    def body(q_ref, kp_ref, kc_ref, vp_ref, vc_ref, do_ref, bias_ref, sink_ref, bk_ref, lse_ref, dd_ref,
             dq_ref, dk_ref, dv_ref, drb_ref, dsink_ref, ds_acc):
        n = pl.program_id(0)

        @pl.when(n == 0)
        def _():
            dk_ref[...] = jnp.zeros_like(dk_ref)
            dv_ref[...] = jnp.zeros_like(dv_ref)
            ds_acc[...] = jnp.zeros_like(ds_acc)
            dsink_ref[...] = jnp.zeros_like(dsink_ref)

        no_prev = jnp.where(n > 0, 0.0, NEG)
        prev = pl.ds(pl.multiple_of(jnp.maximum(n - 1, 0) * WINDOW, WINDOW), WINDOW)
        cur = pl.ds(pl.multiple_of(n * WINDOW, WINDOW), WINDOW)
        lane8 = lax.broadcasted_iota(jnp.int32, (1, N_HEADS), 1)
        dkp = jnp.zeros((WINDOW, D_KV), F32)
        dkc = jnp.zeros((WINDOW, D_KV), F32)
        dvp = jnp.zeros((WINDOW, D_KV), F32)
        dvc = jnp.zeros((WINDOW, D_KV), F32)
        dsink = jnp.zeros((1, N_HEADS), F32)
        for j in range(N_HEADS // 2):
            lanes = slice(j * 128, (j + 1) * 128)
            q2 = q_ref[:, lanes]
            do2 = do_ref[:, lanes]
            dqs = []
            for hh in range(2):
                s_ = 2 * j + hh
                sel = _head_select((WINDOW, 128), hh)
                qh = jnp.where(sel, q2, jnp.zeros_like(q2))
                doh = jnp.where(sel, do2, jnp.zeros_like(do2))
                lse_h = lse_ref[:, s_:s_ + 1]
                dd = dd_ref[:, s_:s_ + 1]
                pp = jnp.exp(_dot_nt(qh, kp_ref[...]) + bias_ref[s_, :, :WINDOW] + no_prev - lse_h)
                pc = jnp.exp(_dot_nt(qh, kc_ref[...]) + bias_ref[s_, :, WINDOW:] - lse_h)
                p_sink = jnp.exp(sink_ref[s_] - lse_h)
                dsp = pp * (_dot_nt(doh, vp_ref[...]) - dd)
                dsc = pc * (_dot_nt(doh, vc_ref[...]) - dd)
                dsink = dsink + jnp.where(lane8 == s_, -jnp.sum(p_sink * dd), 0.0)
                ds_acc[s_, :, :WINDOW] += dsp
                ds_acc[s_, :, WINDOW:] += dsc
                dspb, dscb = dsp.astype(MM), dsc.astype(MM)
                dqs.append(_dot(dspb, kp_ref[...]) + _dot(dscb, kc_ref[...]))
                dkp = dkp + _dot_tn(dspb, qh)
                dkc = dkc + _dot_tn(dscb, qh)
                dvp = dvp + _dot_tn(pp.astype(MM), doh)
                dvc = dvc + _dot_tn(pc.astype(MM), doh)
            dq_ref[:, lanes] = (jnp.where(_head_select((WINDOW, 128), 0), dqs[0], dqs[1]) * Q_SCALE).astype(MM)
        dk_ref[prev, :] += dkp
        dk_ref[cur, :] += dkc
        dv_ref[prev, :] += dvp
        dv_ref[cur, :] += dvc
        dsink_ref[...] += dsink

        @pl.when(n == n_blk - 1)
        def _():
            bk = bk_ref[...]
            rb = lax.broadcasted_iota(jnp.int32, (N_BUCKETS, N_HEADS), 0)
            cb = lax.broadcasted_iota(jnp.int32, (N_BUCKETS, N_HEADS), 1)
            out = jnp.zeros((N_BUCKETS, N_HEADS), F32)
            for s in range(N_HEADS):
                acc = ds_acc[s]
                for b in range(N_BUCKETS):
                    out = out + jnp.where((rb == b) & (cb == s), jnp.sum(jnp.where(bk == b, acc, 0.0)), 0.0)
            drb_ref[...] = out

    do_spec = pl.BlockSpec((WINDOW, D_ATT), lambda n: (n, 1))
    return pl.pallas_call(
        body, name="swa_bwd", grid=(n_blk,),
        in_specs=_swa_specs(S) + [do_spec, _resident(), pl.BlockSpec(memory_space=pltpu.SMEM), _resident(),
                                  _rows(WINDOW, N_HEADS), _rows(WINDOW, N_HEADS)],
        out_specs=[_rows(WINDOW, D_ATT), _const((S, D_KV)), _const((S, D_KV)), _const((N_BUCKETS, N_HEADS)),
                   _const((1, N_HEADS))],
        out_shape=[jax.ShapeDtypeStruct((S, D_ATT), MM), jax.ShapeDtypeStruct((S, D_KV), F32),
                   jax.ShapeDtypeStruct((S, D_KV), F32), jax.ShapeDtypeStruct((N_BUCKETS, N_HEADS), F32),
                   jax.ShapeDtypeStruct((1, N_HEADS), F32)],
        scratch_shapes=[pltpu.VMEM((N_HEADS, WINDOW, 2 * WINDOW), F32)],
        compiler_params=_params("arbitrary"),
    )(sqkv, sqkv, sqkv, sqkv, sqkv, dcat, biasm, sinks_slot, bucket, lse, d_col)


def _pre_attn_bwd(x, dh1, dq_fox, dk_fox, dv_fox, dsq, dsk, dsv, dff_t, wqkv, wswa, wfft, g1, tm):
    S = x.shape[0]

    def body(x_ref, dh1_ref, dq_ref, dk_ref, dv_ref, dsq_ref, dsk_ref, dsv_ref, dff_ref, wqkv_ref, wswa_ref,
             wff_ref, g1_ref, dx_ref, dz_ref, dg1_ref):
        i = pl.program_id(0)
        dq = dq_ref[...].astype(MM)
        dsk = dsk_ref[...].astype(MM)
        dsv = dsv_ref[...].astype(MM)
        dz_ref[:, 0:512] = dq
        dz_ref[:, 512:1024] = dk_ref[...]
        dz_ref[:, 1024:1536] = dv_ref[...]
        dz_ref[:, 1536:2048] = dsq_ref[...]
        dz_ref[:, 2048:2176] = dsk
        dz_ref[:, 2176:2304] = dsv
        da = (_dot_nt(dq, wqkv_ref[:, 0:512]) + _dot_nt(dk_ref[...], wqkv_ref[:, 512:1024])
              + _dot_nt(dv_ref[...], wqkv_ref[:, 1024:1536]) + _dot_nt(dsq_ref[...], wswa_ref[:, 0:512])
              + _dot_nt(dsk, wswa_ref[:, 512:640]) + _dot_nt(dsv, wswa_ref[:, 640:768])
              + _dot_tn(dff_ref[...].astype(MM), wff_ref[...]))
        n1, r1 = _rms(x_ref[...])
        dx, dg1 = _rms_bwd(da, n1, r1, g1_ref[...])
        _accumulate(dg1_ref, dg1, i)
        dx_ref[...] = dh1_ref[...] + dx

    return pl.pallas_call(
        body, name="pre_attn_bwd", grid=(S // tm,),
        in_specs=[_rows(tm, D_MODEL), _rows(tm, D_MODEL), _rows(tm, D_ATT), _rows(tm, D_ATT), _rows(tm, D_ATT),
                  _rows(tm, D_ATT), _rows(tm, D_KV), _rows(tm, D_KV), pl.BlockSpec((16, tm), lambda i: (0, i)),
                  _resident(), _resident(), _resident(), _const((1, D_MODEL))],
        out_specs=[_rows(tm, D_MODEL), _rows(tm, 2304), _const((1, D_MODEL))],
        out_shape=[jax.ShapeDtypeStruct((S, D_MODEL), F32), jax.ShapeDtypeStruct((S, 2304), MM),
                   jax.ShapeDtypeStruct((1, D_MODEL), F32)],
        compiler_params=_params("arbitrary"),
    )(x, dh1, dq_fox, dk_fox, dv_fox, dsq, dsk, dsv, dff_t, wqkv, wswa, wfft, g1)


def _weight_grad(a, b, name, tk, n_chunks=1, relu2=False):
    S, K = a.shape
    N = b.shape[1]
    cn = N // n_chunks

    def body(a_ref, b_ref, out_ref):
        av = a_ref[...]
        if relu2:
            av = jnp.square(jnp.maximum(av.astype(F32), 0.0))
        av = av.astype(MM)
        for j in range(n_chunks):
            val = _dot_tn(av, b_ref[:, j * cn:(j + 1) * cn].astype(MM)).astype(MM)
            if n_chunks > 1:
                out_ref[j] = val
            else:
                out_ref[...] = val

    if n_chunks > 1:
        out_spec = pl.BlockSpec((n_chunks, tk, cn), lambda i: (0, i, 0))
        out_shape = jax.ShapeDtypeStruct((n_chunks, K, cn), MM)
    else:
        out_spec = pl.BlockSpec((tk, N), lambda i: (i, 0))
        out_shape = jax.ShapeDtypeStruct((K, N), MM)
    return pl.pallas_call(
        body, name=name, grid=(K // tk,),
        in_specs=[pl.BlockSpec((S, tk), lambda i: (0, i)), _resident()],
        out_specs=out_spec, out_shape=out_shape, compiler_params=_params("parallel"),
    )(a, b)


def _forget_weight_grad(dff_t, a):
    def body(d_ref, a_ref, out_ref):
        out_ref[...] = _dot(d_ref[...].astype(MM), a_ref[...])

    return pl.pallas_call(
        body, name="forget_weight_grad", out_shape=jax.ShapeDtypeStruct((16, D_MODEL), F32),
        in_specs=[_resident(), _resident()], out_specs=_resident(),
    )(dff_t, a)


def _place():
    return lax.axis_index("x"), lax.axis_index("y"), lax.axis_index("c")


def _all_gather(blocks):
    n = len(blocks)

    def body(*refs):
        ins, outs = refs[:n], refs[n:2 * n]
        send_sems, recv_sems, local_sems = refs[2 * n:]
        x, y, c = _place()
        me, sibling = (x, y, c), (x, y, 1 - c)
        chips = [(1 - x, y), (x, 1 - y), (1 - x, 1 - y)]

        def slot(out, place):
            px, py, pc = place
            return out.at[4 * px + 2 * py + pc]

        def copy(a, k, block, to, src=None):
            dst = slot(outs[a], block)
            return pltpu.make_async_remote_copy(
                src_ref=dst if src is None else src, dst_ref=dst, send_sem=send_sems.at[7 * a + k],
                recv_sem=recv_sems.at[7 * a + k], device_id=to, device_id_type=MESH)

        mine = [pltpu.make_async_copy(ins[a], slot(outs[a], me), local_sems.at[a]) for a in range(n)]
        for cp in mine:
            cp.start()
        first = []
        for a in range(n):
            first.append(copy(a, 0, me, sibling, src=ins[a]))
            first += [copy(a, 1 + j, me, (*chip, c), src=ins[a]) for j, chip in enumerate(chips)]
        for cp in first:
            cp.start()
        passed = []
        for j, chip in enumerate(chips):
            for a in range(n):
                copy(a, 1 + j, (*chip, c), me).wait_recv()
                fwd = copy(a, 4 + j, (*chip, c), sibling)
                fwd.start()
                passed.append(fwd)
        for a in range(n):
            copy(a, 0, sibling, me).wait_recv()
            for j, chip in enumerate(chips):
                copy(a, 4 + j, (*chip, 1 - c), me).wait_recv()
        for cp in first + passed:
            cp.wait_send()
        for cp in mine:
            cp.wait()

    hbm = pl.BlockSpec(memory_space=pl.ANY)
    return pl.pallas_call(
        body, name="all_gather_weights",
        out_shape=[jax.ShapeDtypeStruct((N_DEV,) + b.shape, b.dtype) for b in blocks],
        in_specs=[hbm] * n, out_specs=[hbm] * n,
        scratch_shapes=[pltpu.SemaphoreType.DMA((7 * n,)), pltpu.SemaphoreType.DMA((7 * n,)),
                        pltpu.SemaphoreType.DMA((n,))],
    )(*blocks)


def _exchange_in_chip(grads):
    n = len(grads)

    def body(*refs):
        ins, outs = refs[:n], refs[n:2 * n]
        send_sems, recv_sems = refs[2 * n:]
        x, y, c = _place()
        copies = []
        for a in range(n):
            cp = pltpu.make_async_remote_copy(
                src_ref=ins[a].at[:, 1 - c], dst_ref=outs[a], send_sem=send_sems.at[a], recv_sem=recv_sems.at[a],
                device_id=(x, y, 1 - c), device_id_type=MESH)
            cp.start()
            copies.append(cp)
        for cp in copies:
            cp.wait()

    hbm = pl.BlockSpec(memory_space=pl.ANY)
    views = [g.reshape((4, 2) + g.shape[1:]) for g in grads]
    return pl.pallas_call(
        body, name="exchange_in_chip",
        out_shape=[jax.ShapeDtypeStruct((4,) + g.shape[1:], g.dtype) for g in grads],
        in_specs=[hbm] * n, out_specs=[hbm] * n,
        scratch_shapes=[pltpu.SemaphoreType.DMA((n,)), pltpu.SemaphoreType.DMA((n,))],
    )(*views)


def _chip_sum(grad, other, name):
    _, _, r, cdim = grad.shape
    tr = 256 if r % 256 == 0 else r

    def body(c_ref, g_ref, o_ref, out_ref):
        out_ref[...] = (g_ref[...].astype(F32) + o_ref[...].astype(F32)).astype(out_ref.dtype)

    return pl.pallas_call(
        body, name=name,
        grid_spec=pltpu.PrefetchScalarGridSpec(
            num_scalar_prefetch=1, grid=(4, r // tr),
            in_specs=[pl.BlockSpec((None, None, tr, cdim), lambda k, i, c_ref: (k, c_ref[0], i, 0)),
                      pl.BlockSpec((None, tr, cdim), lambda k, i, c_ref: (k, i, 0))],
            out_specs=pl.BlockSpec((None, tr, cdim), lambda k, i, c_ref: (k, i, 0))),
        out_shape=jax.ShapeDtypeStruct((4, r, cdim), MM),
        compiler_params=_params("parallel", "parallel"),
    )(lax.axis_index("c").astype(jnp.int32).reshape(1), grad, other)


def _exchange_between_chips(sums, small):
    n = len(sums)

    def body(*refs):
        ins, small_in = refs[:n], refs[n]
        outs, small_out = refs[n + 1:2 * n + 1], refs[2 * n + 1]
        send_sems, recv_sems, small_send, small_recv, local_sems = refs[2 * n + 2:]
        x, y, c = _place()
        my_chip = 2 * x + y
        chips = [(1 - x, y), (x, 1 - y), (1 - x, 1 - y)]
        copies = []
        for a in range(n):
            local = pltpu.make_async_copy(ins[a].at[my_chip], outs[a].at[my_chip], local_sems.at[a])
            local.start()
            copies.append(local)
            for j, (px, py) in enumerate(chips):
                cp = pltpu.make_async_remote_copy(
                    src_ref=ins[a].at[2 * px + py], dst_ref=outs[a].at[my_chip], send_sem=send_sems.at[3 * a + j],
                    recv_sem=recv_sems.at[3 * a + j], device_id=(px, py, c), device_id_type=MESH)
                cp.start()
                copies.append(cp)
        me = 4 * x + 2 * y + c
        local = pltpu.make_async_copy(small_in, small_out.at[me], local_sems.at[n])
        local.start()
        copies.append(local)
        k = 0
        for dx in range(2):
            for dy in range(2):
                for dc in range(2):
                    if dx + dy + dc == 0:
                        continue
                    peer = (x ^ dx, y ^ dy, c ^ dc)
                    cp = pltpu.make_async_remote_copy(
                        src_ref=small_in, dst_ref=small_out.at[me], send_sem=small_send.at[k],
                        recv_sem=small_recv.at[k], device_id=peer, device_id_type=MESH)
                    cp.start()
                    copies.append(cp)
                    k += 1
        for cp in copies:
            cp.wait()

    hbm = pl.BlockSpec(memory_space=pl.ANY)
    return pl.pallas_call(
        body, name="exchange_between_chips",
        out_shape=[jax.ShapeDtypeStruct(s.shape, s.dtype) for s in sums]
        + [jax.ShapeDtypeStruct((N_DEV,) + small.shape, small.dtype)],
        in_specs=[hbm] * (n + 1), out_specs=[hbm] * (n + 1),
        scratch_shapes=[pltpu.SemaphoreType.DMA((3 * n,)), pltpu.SemaphoreType.DMA((3 * n,)),
                        pltpu.SemaphoreType.DMA((7,)), pltpu.SemaphoreType.DMA((7,)),
                        pltpu.SemaphoreType.DMA((n + 1,))],
    )(*sums, small)


def _adamw_math(w, g, m, v):
    m = ADAM_B1 * m + (1.0 - ADAM_B1) * g
    v = ADAM_B2 * v + (1.0 - ADAM_B2) * jnp.square(g)
    m_hat = m / (1.0 - ADAM_B1 ** ADAM_STEP)
    v_hat = v / (1.0 - ADAM_B2 ** ADAM_STEP)
    delta = -ADAM_LR * (m_hat / (jnp.sqrt(v_hat) + ADAM_EPS) + ADAM_WD * w)
    return delta, m, v


def _adamw(parts, w, m, v, name):
    n_parts, r, cdim = parts.shape
    tr = 256 if r % 256 == 0 else r

    def body(p_ref, w_ref, m_ref, v_ref, g_out, d_out, m_out, v_out):
        g = p_ref[0].astype(F32)
        for k in range(1, n_parts):
            g = g + p_ref[k].astype(F32)
        delta, m_new, v_new = _adamw_math(w_ref[...], g, m_ref[...], v_ref[...])
        g_out[...] = g
        d_out[...] = delta
        m_out[...] = m_new
        v_out[...] = v_new

    blk = pl.BlockSpec((tr, cdim), lambda i: (i, 0))
    return pl.pallas_call(
        body, name=name, grid=(r // tr,),
        in_specs=[pl.BlockSpec((n_parts, tr, cdim), lambda i: (0, i, 0)), blk, blk, blk],
        out_specs=[blk] * 4, out_shape=[jax.ShapeDtypeStruct((r, cdim), F32)] * 4,
        compiler_params=_params("parallel"),
    )(parts, w, m, v)


def _slot_order(t, axis):
    shp = t.shape
    t = t.reshape(shp[:axis] + (N_HEADS, shp[axis] // N_HEADS) + shp[axis + 1:])
    t = jnp.take(t, np.array(SLOT_HEAD), axis=axis)
    return t.reshape(shp)


def _head_order(t, axis):
    shp = t.shape
    t = t.reshape(shp[:axis] + (N_HEADS, shp[axis] // N_HEADS) + shp[axis + 1:])
    t = jnp.take(t, np.array(HEAD_SLOT), axis=axis)
    return t.reshape(shp)


def _forward_backward(x, p, target, win, wout, w1, w2, wple, wg, b_forget, rel_bias, sinks, g1, g2, g3, g4, g5):
    S = x.shape[0]
    tm = 256
    t = 256
    wqkv = win[:, :3 * D_ATT]
    wfft = jnp.pad(win[:, 3 * D_ATT:3 * D_ATT + N_HEADS].T, ((0, 8), (0, 0)))
    q0 = 3 * D_ATT + N_HEADS
    wswa = jnp.concatenate([_slot_order(win[:, q0:q0 + D_ATT], 1), win[:, q0 + D_ATT:]], axis=1)
    wout_fox = wout[:D_ATT]
    wout_swa = _slot_order(wout[D_ATT:], 0)
    bcol = jnp.pad(b_forget.reshape(N_HEADS, 1), ((0, 8), (0, 0)))
    rel_bias_slot = rel_bias[:, np.array(SLOT_HEAD)]
    sinks_slot = sinks.reshape(N_HEADS)[np.array(SLOT_HEAD)]
    bucket = jnp.asarray(_swa_bucket_map())

    a, fqkv, sqkv, fft = _pre_attn(x, g1, wqkv, wswa, wfft, tm)
    c_row = _forget_cumsum(fft, bcol)
    c_col = c_row[:N_HEADS].T
    c_row3 = c_row[:N_HEADS].reshape(N_HEADS, S // t, t)
    fox_o, fox_lse = _fox_fwd(fqkv, c_col, c_row3, t)
    biasm = _swa_bias(rel_bias_slot, bucket)
    swa_o, swa_lse = _swa_fwd(sqkv, biasm, sinks_slot)
    mix, h1, m = _post_attn(x, fox_o, swa_o, wout_fox, wout_swa, g2, g3, tm)
    u, y, h2 = _mlp_fwd(m, h1, w1, w2, g4, tm)
    dh2, dpe, dgl, dg5, loss = _ple_loss(h2, p, target, wg, wple, g5, tm)

    d_wple = _weight_grad(p, dpe, "grad_w_ple", tk=D_PLE, n_chunks=N_DEV)
    d_wg = _weight_grad(h2, dgl, "grad_w_ple_gate", tk=256)
    dh1, dy, du, dg4, dg3 = _mlp_bwd(dh2, y, h1, u, w1, w2, g4, g3, tm)
    d_w2 = _weight_grad(u, dy, "grad_w_ff2", tk=256, relu2=True)
    d_w1 = _weight_grad(m, du, "grad_w_ff1", tk=256, n_chunks=N_DEV)

    head = np.arange(D_ATT) // HEAD_DIM
    head_rows = jnp.asarray((head[None, :] == np.arange(N_HEADS)[:, None]).astype(np.float32))
    dmix, dcat, d_row, d_col, dg2 = _attn_out_bwd(dh1, mix, fox_o, swa_o, wout_fox, wout_swa, g2,
                                                  head_rows, head_rows.T, tm)
    d_wout_fox = _weight_grad(fox_o, dmix, "grad_w_out_fox", tk=256)
    d_wout_swa = _weight_grad(swa_o, dmix, "grad_w_out_swa", tk=256)

    lse_row3 = fox_lse.T.reshape(N_HEADS, S // t, t)
    d_row3 = d_row.reshape(N_HEADS, S // t, t)
    dq_fox, dk_fox, dv_fox, dc_col, dcq = _fox_bwd(fqkv, dcat, lse_row3, d_row3, c_row3, c_col, t)
    dsq, dsk, dsv, d_rb_slot, d_sink_slot = _swa_bwd(sqkv, dcat, biasm, sinks_slot, bucket, swa_lse, d_col)
    dc_row = jnp.pad(dc_col.T + dcq.reshape(N_HEADS, S), ((0, 8), (0, 0)))
    dff_t, db = _forget_bwd(dc_row, fft, bcol)
    grad_x, dz, dg1 = _pre_attn_bwd(x, dh1, dq_fox, dk_fox, dv_fox, dsq, dsk, dsv, dff_t, wqkv, wswa, wfft, g1, tm)
    d_wmain = _weight_grad(a, dz, "grad_w_in", tk=256)
    d_wff_t = _forget_weight_grad(dff_t, a)

    d_win = jnp.concatenate([
        d_wmain[:, :3 * D_ATT], d_wff_t[:N_HEADS].T.astype(MM), _head_order(d_wmain[:, 3 * D_ATT:4 * D_ATT], 1),
        d_wmain[:, 4 * D_ATT:]], axis=1)
    d_win = d_win.reshape(D_MODEL, N_DEV, D_IN // N_DEV).transpose(1, 0, 2)
    d_wout = jnp.concatenate([d_wout_fox, _head_order(d_wout_swa, 0)], axis=0).reshape(N_DEV, D_MODEL // N_DEV, D_MODEL)
    big = dict(w_in=d_win, w_out=d_wout, w_ff1=d_w1, w_ff2=d_w2.reshape(N_DEV, FF_CHUNK, D_MODEL), w_ple=d_wple,
               w_ple_gate=d_wg.reshape(N_DEV, D_MODEL // N_DEV, D_MODEL))
    small = dict(b_forget=db[:N_HEADS].reshape(1, N_HEADS), rel_bias=d_rb_slot[:, np.array(HEAD_SLOT)],
                 swa_sinks=d_sink_slot[:, np.array(HEAD_SLOT)], g_attn_pre=dg1, g_attn_post=dg2, g_ff_pre=dg3,
                 g_ff_post=dg4, g_ple_post=dg5)
    return loss, grad_x, big, small


BIG = ("w_in", "w_out", "w_ff1", "w_ff2", "w_ple", "w_ple_gate")
SMALL_ROWS = ("g_attn_pre", "g_attn_post", "g_ff_pre", "g_ff_post", "g_ple_post")
WEIGHTS = ("w_in", "b_forget", "w_out", "rel_bias", "swa_sinks", "g_attn_pre", "g_attn_post", "w_ff1", "w_ff2",
           "g_ff_pre", "g_ff_post", "w_ple", "w_ple_gate", "g_ple_post")


def _pack_small(t):
    rows = [t[k].reshape(1, D_MODEL) for k in SMALL_ROWS]
    misc = jnp.concatenate([t["b_forget"].reshape(-1), t["swa_sinks"].reshape(-1), t["rel_bias"].reshape(-1)])
    rows.append(jnp.pad(misc, (0, D_MODEL - misc.shape[0])).reshape(1, D_MODEL))
    rows.append(jnp.pad(t["loss"].reshape(-1), (0, D_MODEL - 1)).reshape(1, D_MODEL))
    rows.append(jnp.zeros((1, D_MODEL), F32))
    return jnp.concatenate(rows, axis=0).astype(F32)


def _unpack_small(blk):
    out = {k: blk[i].reshape(1, D_MODEL) for i, k in enumerate(SMALL_ROWS)}
    misc = blk[len(SMALL_ROWS)]
    out["b_forget"] = misc[:N_HEADS].reshape(1, N_HEADS)
    out["swa_sinks"] = misc[N_HEADS:2 * N_HEADS].reshape(1, N_HEADS)
    out["rel_bias"] = misc[2 * N_HEADS:2 * N_HEADS + N_BUCKETS * N_HEADS].reshape(N_BUCKETS, N_HEADS)
    out["loss"] = blk[len(SMALL_ROWS) + 1, 0]
    return out


def kernel(x, p, w_in, b_forget, w_out, rel_bias, swa_sinks, g_attn_pre, g_attn_post, w_ff1, w_ff2, g_ff_pre, g_ff_post, w_ple, w_ple_gate, g_ple_post, loss_target, m_w_in, m_b_forget, m_w_out, m_rel_bias, m_swa_sinks, m_g_attn_pre, m_g_attn_post, m_w_ff1, m_w_ff2, m_g_ff_pre, m_g_ff_post, m_w_ple, m_w_ple_gate, m_g_ple_post, v_w_in, v_b_forget, v_w_out, v_rel_bias, v_swa_sinks, v_g_attn_pre, v_g_attn_post, v_w_ff1, v_w_ff2, v_g_ff_pre, v_g_ff_post, v_w_ple, v_w_ple_gate, v_g_ple_post):
    w = dict(w_in=w_in, b_forget=b_forget, w_out=w_out, rel_bias=rel_bias, swa_sinks=swa_sinks,
             g_attn_pre=g_attn_pre, g_attn_post=g_attn_post, w_ff1=w_ff1, w_ff2=w_ff2, g_ff_pre=g_ff_pre,
             g_ff_post=g_ff_post, w_ple=w_ple, w_ple_gate=w_ple_gate, g_ple_post=g_ple_post)
    mom = dict(w_in=m_w_in, b_forget=m_b_forget, w_out=m_w_out, rel_bias=m_rel_bias, swa_sinks=m_swa_sinks,
               g_attn_pre=m_g_attn_pre, g_attn_post=m_g_attn_post, w_ff1=m_w_ff1, w_ff2=m_w_ff2,
               g_ff_pre=m_g_ff_pre, g_ff_post=m_g_ff_post, w_ple=m_w_ple, w_ple_gate=m_w_ple_gate,
               g_ple_post=m_g_ple_post)
    var = dict(w_in=v_w_in, b_forget=v_b_forget, w_out=v_w_out, rel_bias=v_rel_bias, swa_sinks=v_swa_sinks,
               g_attn_pre=v_g_attn_pre, g_attn_post=v_g_attn_post, w_ff1=v_w_ff1, w_ff2=v_w_ff2,
               g_ff_pre=v_g_ff_pre, g_ff_post=v_g_ff_post, w_ple=v_w_ple, w_ple_gate=v_w_ple_gate,
               g_ple_post=v_g_ple_post)

    gathered = _all_gather([w[k][0].astype(MM) for k in BIG])
    win_g, wout_g, w1_g, w2_g, wple_g, wg_g = gathered
    win = jnp.moveaxis(win_g, 0, 1).reshape(D_MODEL, D_IN)
    wple_full = jnp.moveaxis(wple_g, 0, 1).reshape(D_PLE, D_MODEL)
    loss, grad_x, big, small = _forward_backward(
        x[0], p[0, 0], loss_target[0], win, wout_g.reshape(D_MODEL, D_MODEL), w1_g, w2_g.reshape(D_FF, D_MODEL),
        wple_full, wg_g.reshape(D_MODEL, D_MODEL), b_forget, rel_bias, swa_sinks, g_attn_pre, g_attn_post,
        g_ff_pre, g_ff_post, g_ple_post)

    grads = [big[k] for k in BIG]
    views = [g.reshape((4, 2) + g.shape[1:]) for g in grads]
    others = _exchange_in_chip(grads)
    sums = [_chip_sum(v_, o, "chip_sum_" + k) for v_, o, k in zip(views, others, BIG)]
    small["loss"] = loss
    *parts, small_all = _exchange_between_chips(sums, _pack_small(small))

    out_g, out_d, out_m, out_v = {}, {}, {}, {}
    for k, part in zip(BIG, parts):
        g, d, m_new, v_new = _adamw(part, w[k][0], mom[k][0], var[k][0], "adamw_" + k)
        out_g[k], out_d[k], out_m[k], out_v[k] = g[None], d[None], m_new[None], v_new[None]
    rep = {k: w[k] for k in w if k not in BIG}
    rep["loss"] = jnp.zeros((), F32)
    rep_m = {k: mom[k] for k in mom if k not in BIG}
    rep_m["loss"] = jnp.zeros((), F32)
    rep_v = {k: var[k] for k in var if k not in BIG}
    rep_v["loss"] = jnp.ones((), F32)
    g_s, d_s, m_s, v_s = _adamw(small_all, _pack_small(rep), _pack_small(rep_m), _pack_small(rep_v), "adamw_small")
    g_s, d_s, m_s, v_s = _unpack_small(g_s), _unpack_small(d_s), _unpack_small(m_s), _unpack_small(v_s)
    for k in w:
        if k not in BIG:
            out_g[k], out_d[k], out_m[k], out_v[k] = g_s[k], d_s[k], m_s[k], v_s[k]
    return (g_s["loss"], grad_x[None], *[out_g[k] for k in WEIGHTS], *[out_d[k] for k in WEIGHTS],
            *[out_m[k] for k in WEIGHTS], *[out_v[k] for k in WEIGHTS])
```

```python
import functools

import numpy as np
import jax
import jax.numpy as jnp
from jax import lax
from jax.experimental import pallas as pl
from jax.experimental.pallas import tpu as pltpu

F32 = jnp.float32
MM = jnp.bfloat16

D_MODEL = 1024
HEAD_DIM = 64
N_HEADS = 8
D_ATT = N_HEADS * HEAD_DIM
D_KV = 128
D_FF = 4096
D_PLE = 256
D_IN = 3 * D_ATT + N_HEADS + D_ATT + 2 * D_KV
N_DEV = 8
FF_CHUNK = D_FF // N_DEV
WINDOW = 128
N_BUCKETS = 32
MAX_DISTANCE = 128
RMS_EPS = 1e-6
Q_SCALE = HEAD_DIM ** -0.5
NEG = -1e30

ADAM_LR = 0.001
ADAM_B1 = 0.9
ADAM_B2 = 0.999
ADAM_EPS = 1e-08
ADAM_WD = 0.01
ADAM_STEP = 10

SLOT_HEAD = (0, 4, 1, 5, 2, 6, 3, 7)
HEAD_SLOT = (0, 2, 4, 6, 1, 3, 5, 7)

VMEM_LIMIT = 56 * 1024 * 1024
MESH = pl.DeviceIdType.MESH

NT = (((1,), (1,)), ((), ()))
TN = (((0,), (0,)), ((), ()))


def _params(*semantics):
    return pltpu.CompilerParams(dimension_semantics=semantics, vmem_limit_bytes=VMEM_LIMIT)


def _resident():
    return pl.BlockSpec(memory_space=pltpu.VMEM)


def _rows(tm, width):
    return pl.BlockSpec((tm, width), lambda i: (i, 0))


def _const(shape):
    return pl.BlockSpec(shape, lambda i: (0,) * len(shape))


def _dot(a, b):
    return jnp.dot(a, b, preferred_element_type=F32)


def _dot_nt(a, b):
    return lax.dot_general(a, b, NT, preferred_element_type=F32)


def _dot_tn(a, b):
    return lax.dot_general(a, b, TN, preferred_element_type=F32)


def _rms(xf):
    r = lax.rsqrt(jnp.mean(xf * xf, axis=-1, keepdims=True) + RMS_EPS)
    return xf * r, r


def _rms_bwd(dout, n, r, g):
    dg = jnp.sum(dout * n, axis=0, keepdims=True)
    dn = dout * g
    dx = r * (dn - n * jnp.mean(dn * n, axis=-1, keepdims=True))
    return dx, dg


def _accumulate(ref, value, step):
    @pl.when(step == 0)
    def _():
        ref[...] = value

    @pl.when(step != 0)
    def _():
        ref[...] += value


def _t5_bucket(n):
    max_exact = N_BUCKETS // 2
    large = max_exact + (np.log(np.maximum(n, 1) / max_exact) / np.log(MAX_DISTANCE / max_exact)
                         * (N_BUCKETS - max_exact)).astype(np.int32)
    large = np.minimum(large, N_BUCKETS - 1)
    return np.where(n < max_exact, n, large).astype(np.int32)


def _swa_bucket_map():
    i = np.arange(WINDOW)[:, None]
    j = np.arange(2 * WINDOW)[None, :]
    dist = i + WINDOW - j
    ok = (dist >= 0) & (dist < WINDOW)
    return np.where(ok, _t5_bucket(np.clip(dist, 0, None)), -1).astype(np.int32)


def _pre_attn(x, g1, wqkv, wswa, wfft, tm):
    S = x.shape[0]

    def body(x_ref, g_ref, wqkv_ref, wswa_ref, wff_ref, a_ref, fqkv_ref, sqkv_ref, fft_ref):
        n, _ = _rms(x_ref[...])
        a = (n * g_ref[...]).astype(MM)
        a_ref[...] = a
        fqkv_ref[:, :D_ATT] = (_dot(a, wqkv_ref[:, :D_ATT]) * Q_SCALE).astype(MM)
        fqkv_ref[:, D_ATT:] = _dot(a, wqkv_ref[:, D_ATT:]).astype(MM)
        sqkv_ref[:, :D_ATT] = (_dot(a, wswa_ref[:, :D_ATT]) * Q_SCALE).astype(MM)
        sqkv_ref[:, D_ATT:] = _dot(a, wswa_ref[:, D_ATT:]).astype(MM)
        fft_ref[...] = _dot_nt(wff_ref[...], a)

    return pl.pallas_call(
        body, name="pre_attn", grid=(S // tm,),
        in_specs=[_rows(tm, D_MODEL), _const((1, D_MODEL)), _resident(), _resident(), _resident()],
        out_specs=[_rows(tm, D_MODEL), _rows(tm, 3 * D_ATT), _rows(tm, D_ATT + 2 * D_KV),
                   pl.BlockSpec((16, tm), lambda i: (0, i))],
        out_shape=[jax.ShapeDtypeStruct((S, D_MODEL), MM), jax.ShapeDtypeStruct((S, 3 * D_ATT), MM),
                   jax.ShapeDtypeStruct((S, D_ATT + 2 * D_KV), MM), jax.ShapeDtypeStruct((16, S), F32)],
        compiler_params=_params("parallel"),
    )(x, g1, wqkv, wswa, wfft)


def _lane_scan(v, reverse):
    S = v.shape[1]
    lane = lax.broadcasted_iota(jnp.int32, v.shape, 1)
    k = 1
    while k < S:
        if reverse:
            v = v + jnp.where(lane < S - k, pltpu.roll(v, S - k, axis=1), 0.0)
        else:
            v = v + jnp.where(lane >= k, pltpu.roll(v, k, axis=1), 0.0)
        k *= 2
    return v


def _forget_cumsum(fft, bcol):
    def body(f_ref, b_ref, c_ref):
        z = f_ref[...] + b_ref[...]
        log_f = jnp.minimum(z, 0.0) - jnp.log1p(jnp.exp(-jnp.abs(z)))
        c_ref[...] = _lane_scan(log_f, reverse=False)

    return pl.pallas_call(
        body, name="forget_cumsum", out_shape=jax.ShapeDtypeStruct(fft.shape, F32),
        in_specs=[_resident(), _resident()], out_specs=_resident(),
    )(fft, bcol)


def _forget_bwd(dc_row, fft, bcol):
    def body(dc_ref, f_ref, b_ref, dff_ref, db_ref):
        z = f_ref[...] + b_ref[...]
        dlog_f = _lane_scan(dc_ref[...], reverse=True)
        dff = dlog_f * (1.0 / (1.0 + jnp.exp(z)))
        dff_ref[...] = dff
        db_ref[...] = jnp.sum(dff, axis=1, keepdims=True)

    return pl.pallas_call(
        body, name="forget_bwd",
        out_shape=[jax.ShapeDtypeStruct(fft.shape, F32), jax.ShapeDtypeStruct((fft.shape[0], 1), F32)],
        in_specs=[_resident()] * 3, out_specs=[_resident()] * 2,
    )(dc_row, fft, bcol)


def _head_select(shape, upper):
    lane = lax.broadcasted_iota(jnp.int32, shape, 1)
    return lane >= HEAD_DIM if upper else lane < HEAD_DIM


def _fox_fwd(fqkv, c_col, c_row3, t):
    S = fqkv.shape[0]

    def body(q_ref, k_ref, v_ref, cq_ref, ck_ref, o_ref, lse_ref):
        qi = pl.program_id(0)
        row = lax.broadcasted_iota(jnp.int32, (t, t), 0)
        col = lax.broadcasted_iota(jnp.int32, (t, t), 1)
        for pr in range(N_HEADS // 2):
            lanes = slice(pr * 128, (pr + 1) * 128)
            q2 = q_ref[:, lanes]
            res = []
            for hh in range(2):
                h = 2 * pr + hh
                qh = jnp.where(_head_select((t, 128), hh), q2, jnp.zeros_like(q2))
                cq = cq_ref[:, h:h + 1]

                def block(kb, carry, diagonal, qh=qh, cq=cq, h=h, lanes=lanes):
                    m, l, acc = carry
                    rows = pl.ds(pl.multiple_of(kb * t, t), t)
                    s = _dot_nt(qh, k_ref[rows, lanes]) + cq - ck_ref[h, pl.ds(kb, 1), :]
                    if diagonal:
                        s = jnp.where(row >= col, s, NEG)
                    m_new = jnp.maximum(m, jnp.max(s, axis=-1, keepdims=True))
                    p = jnp.exp(s - m_new)
                    alpha = jnp.exp(m - m_new)
                    l = alpha * l + jnp.sum(p, axis=-1, keepdims=True)
                    acc = alpha * acc + _dot(p.astype(MM), v_ref[rows, lanes])
                    return m_new, l, acc

                init = (jnp.full((t, 1), NEG, F32), jnp.zeros((t, 1), F32), jnp.zeros((t, 128), F32))
                carry = lax.fori_loop(0, qi, functools.partial(block, diagonal=False), init)
                m, l, acc = block(qi, carry, diagonal=True)
                res.append(acc / l)
                lse_ref[:, h:h + 1] = m + jnp.log(l)
            o_ref[:, lanes] = jnp.where(_head_select((t, 128), 0), res[0], res[1]).astype(MM)

    return pl.pallas_call(
        body, name="fox_fwd", grid=(S // t,),
        in_specs=[pl.BlockSpec((t, D_ATT), lambda i: (i, 0)), pl.BlockSpec((S, D_ATT), lambda i: (0, 1)),
                  pl.BlockSpec((S, D_ATT), lambda i: (0, 2)), _rows(t, N_HEADS), _resident()],
        out_specs=[_rows(t, D_ATT), _rows(t, N_HEADS)],
        out_shape=[jax.ShapeDtypeStruct((S, D_ATT), MM), jax.ShapeDtypeStruct((S, N_HEADS), F32)],
        compiler_params=_params("parallel"),
    )(fqkv, fqkv, fqkv, c_col, c_row3)


def _swa_bias(rel_bias_slot, bucket):
    def body(rb_ref, bk_ref, out_ref):
        bk = bk_ref[...]
        for s in range(N_HEADS):
            acc = jnp.where(bk < 0, NEG, 0.0).astype(F32)
            for b in range(N_BUCKETS):
                acc = jnp.where(bk == b, rb_ref[b, s], acc)
            out_ref[s] = acc

    return pl.pallas_call(
        body, name="swa_bias", out_shape=jax.ShapeDtypeStruct((N_HEADS, WINDOW, 2 * WINDOW), F32),
        in_specs=[pl.BlockSpec(memory_space=pltpu.SMEM), _resident()], out_specs=_resident(),
    )(rel_bias_slot, bucket)


def _swa_specs(S):
    q = pl.BlockSpec((WINDOW, D_ATT), lambda n: (n, 0))
    kp = pl.BlockSpec((WINDOW, D_KV), lambda n: (jnp.maximum(n - 1, 0), 4))
    kc = pl.BlockSpec((WINDOW, D_KV), lambda n: (n, 4))
    vp = pl.BlockSpec((WINDOW, D_KV), lambda n: (jnp.maximum(n - 1, 0), 5))
    vc = pl.BlockSpec((WINDOW, D_KV), lambda n: (n, 5))
    return [q, kp, kc, vp, vc]


def _swa_fwd(sqkv, biasm, sinks_slot):
    S = sqkv.shape[0]

    def body(q_ref, kp_ref, kc_ref, vp_ref, vc_ref, bias_ref, sink_ref, o_ref, lse_ref):
        n = pl.program_id(0)
        no_prev = jnp.where(n > 0, 0.0, NEG)
        for j in range(N_HEADS // 2):
            lanes = slice(j * 128, (j + 1) * 128)
            q2 = q_ref[:, lanes]
            res = []
            for hh in range(2):
                s_ = 2 * j + hh
                qh = jnp.where(_head_select((WINDOW, 128), hh), q2, jnp.zeros_like(q2))
                sp = _dot_nt(qh, kp_ref[...]) + bias_ref[s_, :, :WINDOW] + no_prev
                sc = _dot_nt(qh, kc_ref[...]) + bias_ref[s_, :, WINDOW:]
                sink = sink_ref[s_]
                m = jnp.maximum(jnp.maximum(jnp.max(sp, axis=-1, keepdims=True),
                                            jnp.max(sc, axis=-1, keepdims=True)), sink)
                ep = jnp.exp(sp - m)
                ec = jnp.exp(sc - m)
                den = (jnp.sum(ep, axis=-1, keepdims=True) + jnp.sum(ec, axis=-1, keepdims=True)
                       + jnp.exp(sink - m))
                res.append((_dot(ep.astype(MM), vp_ref[...]) + _dot(ec.astype(MM), vc_ref[...])) / den)
                lse_ref[:, s_:s_ + 1] = m + jnp.log(den)
            o_ref[:, lanes] = jnp.where(_head_select((WINDOW, 128), 0), res[0], res[1]).astype(MM)

    return pl.pallas_call(
        body, name="swa_fwd", grid=(S // WINDOW,),
        in_specs=_swa_specs(S) + [_resident(), pl.BlockSpec(memory_space=pltpu.SMEM)],
        out_specs=[_rows(WINDOW, D_ATT), _rows(WINDOW, N_HEADS)],
        out_shape=[jax.ShapeDtypeStruct((S, D_ATT), MM), jax.ShapeDtypeStruct((S, N_HEADS), F32)],
        compiler_params=_params("parallel"),
    )(sqkv, sqkv, sqkv, sqkv, sqkv, biasm, sinks_slot)


def _post_attn(x, fox_o, swa_o, wout_fox, wout_swa, g2, g3, tm):
    S = x.shape[0]

    def body(x_ref, fo_ref, so_ref, wf_ref, ws_ref, g2_ref, g3_ref, mix_ref, h1_ref, m_ref):
        mix = _dot(fo_ref[...], wf_ref[...]) + _dot(so_ref[...], ws_ref[...])
        mix_ref[...] = mix
        n2, _ = _rms(mix)
        h1 = x_ref[...] + n2 * g2_ref[...]
        h1_ref[...] = h1
        n3, _ = _rms(h1)
        m_ref[...] = (n3 * g3_ref[...]).astype(MM)

    return pl.pallas_call(
        body, name="post_attn", grid=(S // tm,),
        in_specs=[_rows(tm, D_MODEL), _rows(tm, D_ATT), _rows(tm, D_ATT), _resident(), _resident(),
                  _const((1, D_MODEL)), _const((1, D_MODEL))],
        out_specs=[_rows(tm, D_MODEL)] * 3,
        out_shape=[jax.ShapeDtypeStruct((S, D_MODEL), F32), jax.ShapeDtypeStruct((S, D_MODEL), F32),
                   jax.ShapeDtypeStruct((S, D_MODEL), MM)],
        compiler_params=_params("parallel"),
    )(x, fox_o, swa_o, wout_fox, wout_swa, g2, g3)


def _mlp_fwd(m, h1, w1, w2, g4, tm):
    S = m.shape[0]

    def body(m_ref, h1_ref, w1_ref, w2_ref, g4_ref, u_ref, y_ref, h2_ref):
        mb = m_ref[...]
        y = jnp.zeros((tm, D_MODEL), F32)
        for j in range(N_DEV):
            cols = slice(j * FF_CHUNK, (j + 1) * FF_CHUNK)
            u = _dot(mb, w1_ref[j])
            u_ref[:, cols] = u.astype(MM)
            y = y + _dot(jnp.square(jnp.maximum(u, 0.0)).astype(MM), w2_ref[cols, :])
        y_ref[...] = y
        n4, _ = _rms(y)
        h2_ref[...] = h1_ref[...] + n4 * g4_ref[...]

    return pl.pallas_call(
        body, name="mlp_fwd", grid=(S // tm,),
        in_specs=[_rows(tm, D_MODEL), _rows(tm, D_MODEL), _resident(), _resident(), _const((1, D_MODEL))],
        out_specs=[_rows(tm, D_FF), _rows(tm, D_MODEL), _rows(tm, D_MODEL)],
        out_shape=[jax.ShapeDtypeStruct((S, D_FF), MM), jax.ShapeDtypeStruct((S, D_MODEL), F32),
                   jax.ShapeDtypeStruct((S, D_MODEL), F32)],
        compiler_params=_params("parallel"),
    )(m, h1, w1, w2, g4)


def _ple_loss(h2, p, target, wg, wple, g5, tm):
    S = h2.shape[0]

    def body(h2_ref, p_ref, t_ref, wg_ref, wp_ref, g5_ref, dh2_ref, dpe_ref, dgl_ref, dg5_ref, loss_ref):
        i = pl.program_id(0)
        h2 = h2_ref[...]
        gate = jax.nn.sigmoid(_dot(h2.astype(MM), wg_ref[...]))
        pe = _dot(p_ref[...].astype(MM), wp_ref[...])
        n5, r5 = _rms(pe * gate)
        g5 = g5_ref[...]
        diff = h2 + n5 * g5 - t_ref[...]
        per_token = jnp.mean(jnp.square(diff), axis=-1, keepdims=True)
        _accumulate(loss_ref, 0.5 * jnp.sum(per_token, axis=0, keepdims=True), i)
        dh3 = diff * (1.0 / D_MODEL)
        de, dg5 = _rms_bwd(dh3, n5, r5, g5)
        _accumulate(dg5_ref, dg5, i)
        dpe_ref[...] = (de * gate).astype(MM)
        dgl = (de * pe * gate * (1.0 - gate)).astype(MM)
        dgl_ref[...] = dgl
        dh2_ref[...] = dh3 + _dot_nt(dgl, wg_ref[...])

    return pl.pallas_call(
        body, name="ple_loss", grid=(S // tm,),
        in_specs=[_rows(tm, D_MODEL), _rows(tm, D_PLE), _rows(tm, D_MODEL), _resident(), _resident(),
                  _const((1, D_MODEL))],
        out_specs=[_rows(tm, D_MODEL), _rows(tm, D_MODEL), _rows(tm, D_MODEL), _const((1, D_MODEL)), _const((1, 1))],
        out_shape=[jax.ShapeDtypeStruct((S, D_MODEL), F32), jax.ShapeDtypeStruct((S, D_MODEL), MM),
                   jax.ShapeDtypeStruct((S, D_MODEL), MM), jax.ShapeDtypeStruct((1, D_MODEL), F32),
                   jax.ShapeDtypeStruct((1, 1), F32)],
        compiler_params=_params("arbitrary"),
    )(h2, p, target, wg, wple, g5)


def _mlp_bwd(dh2, y, h1, u, w1, w2, g4, g3, tm):
    S = dh2.shape[0]

    def body(dh2_ref, y_ref, h1_ref, u_ref, w1_ref, w2_ref, g4_ref, g3_ref,
             dh1_ref, dy_ref, du_ref, dg4_ref, dg3_ref):
        i = pl.program_id(0)
        dh2 = dh2_ref[...]
        n4, r4 = _rms(y_ref[...])
        dy, dg4 = _rms_bwd(dh2, n4, r4, g4_ref[...])
        _accumulate(dg4_ref, dg4, i)
        dyb = dy.astype(MM)
        dy_ref[...] = dyb
        dm = jnp.zeros((tm, D_MODEL), F32)
        for j in range(N_DEV):
            cols = slice(j * FF_CHUNK, (j + 1) * FF_CHUNK)
            dact = _dot_nt(dyb, w2_ref[cols, :])
            du = (dact * (2.0 * jnp.maximum(u_ref[:, cols].astype(F32), 0.0))).astype(MM)
            du_ref[:, cols] = du
            dm = dm + _dot_nt(du, w1_ref[j])
        n3, r3 = _rms(h1_ref[...])
        dx, dg3 = _rms_bwd(dm, n3, r3, g3_ref[...])
        _accumulate(dg3_ref, dg3, i)
        dh1_ref[...] = dh2 + dx

    return pl.pallas_call(
        body, name="mlp_bwd", grid=(S // tm,),
        in_specs=[_rows(tm, D_MODEL), _rows(tm, D_MODEL), _rows(tm, D_MODEL), _rows(tm, D_FF),
                  _resident(), _resident(), _const((1, D_MODEL)), _const((1, D_MODEL))],
        out_specs=[_rows(tm, D_MODEL), _rows(tm, D_MODEL), _rows(tm, D_FF), _const((1, D_MODEL)),
                   _const((1, D_MODEL))],
        out_shape=[jax.ShapeDtypeStruct((S, D_MODEL), F32), jax.ShapeDtypeStruct((S, D_MODEL), MM),
                   jax.ShapeDtypeStruct((S, D_FF), MM), jax.ShapeDtypeStruct((1, D_MODEL), F32),
                   jax.ShapeDtypeStruct((1, D_MODEL), F32)],
        compiler_params=_params("arbitrary"),
    )(dh2, y, h1, u, w1, w2, g4, g3)


def _attn_out_bwd(dh1, mix, fox_o, swa_o, wout_fox, wout_swa, g2, head_rows, head_cols, tm):
    S = dh1.shape[0]

    def body(dh1_ref, mix_ref, fo_ref, so_ref, wf_ref, ws_ref, g2_ref, er_ref, ec_ref,
             dmix_ref, dcat_ref, drow_ref, dcol_ref, dg2_ref):
        i = pl.program_id(0)
        n2, r2 = _rms(mix_ref[...])
        dmix, dg2 = _rms_bwd(dh1_ref[...], n2, r2, g2_ref[...])
        _accumulate(dg2_ref, dg2, i)
        dmb = dmix.astype(MM)
        dmix_ref[...] = dmb
        dfo = _dot_nt(dmb, wf_ref[...]).astype(MM)
        dso = _dot_nt(dmb, ws_ref[...]).astype(MM)
        dcat_ref[:, :D_ATT] = dfo
        dcat_ref[:, D_ATT:] = dso
        hi = lax.Precision.HIGHEST
        prod_f = dfo.astype(F32) * fo_ref[...].astype(F32)
        prod_s = dso.astype(F32) * so_ref[...].astype(F32)
        drow_ref[...] = lax.dot_general(er_ref[...], prod_f, NT, precision=hi, preferred_element_type=F32)
        dcol_ref[...] = jnp.dot(prod_s, ec_ref[...], precision=hi, preferred_element_type=F32)

    return pl.pallas_call(
        body, name="attn_out_bwd", grid=(S // tm,),
        in_specs=[_rows(tm, D_MODEL), _rows(tm, D_MODEL), _rows(tm, D_ATT), _rows(tm, D_ATT), _resident(),
                  _resident(), _const((1, D_MODEL)), _resident(), _resident()],
        out_specs=[_rows(tm, D_MODEL), _rows(tm, D_MODEL), pl.BlockSpec((N_HEADS, tm), lambda i: (0, i)),
                   _rows(tm, N_HEADS), _const((1, D_MODEL))],
        out_shape=[jax.ShapeDtypeStruct((S, D_MODEL), MM), jax.ShapeDtypeStruct((S, D_MODEL), MM),
                   jax.ShapeDtypeStruct((N_HEADS, S), F32), jax.ShapeDtypeStruct((S, N_HEADS), F32),
                   jax.ShapeDtypeStruct((1, D_MODEL), F32)],
        compiler_params=_params("arbitrary"),
    )(dh1, mix, fox_o, swa_o, wout_fox, wout_swa, g2, head_rows, head_cols)


def _fox_bwd(fqkv, dcat, lse_row3, d_row3, c_row3, c_col, t):
    S = fqkv.shape[0]
    n_blk = S // t

    def body(q_ref, k_ref, v_ref, do_ref, lse_ref, dd_ref, cq_ref, ck_ref, dq_ref, dk_ref, dv_ref, dc_ref, dcq_ref):
        kb = pl.program_id(0)

        @pl.when(kb == 0)
        def _():
            dq_ref[...] = jnp.zeros_like(dq_ref)
            dcq_ref[...] = jnp.zeros_like(dcq_ref)

        key = lax.broadcasted_iota(jnp.int32, (t, t), 0)
        qry = lax.broadcasted_iota(jnp.int32, (t, t), 1)
        for pr in range(N_HEADS // 2):
            lanes = slice(pr * 128, (pr + 1) * 128)
            k2 = k_ref[:, lanes]
            v2 = v_ref[:, lanes]
            grads = []
            for hh in range(2):
                h = 2 * pr + hh
                sel = _head_select((t, 128), hh)
                kh = jnp.where(sel, k2, jnp.zeros_like(k2))
                vh = jnp.where(sel, v2, jnp.zeros_like(v2))
                ck = ck_ref[:, h:h + 1]

                def block(qb, carry, diagonal, kh=kh, vh=vh, ck=ck, h=h, lanes=lanes):
                    dk, dv, dc = carry
                    rows = pl.ds(pl.multiple_of(qb * t, t), t)
                    q2 = q_ref[rows, lanes]
                    do2 = do_ref[rows, lanes]
                    s_t = _dot_nt(kh, q2) + cq_ref[h, pl.ds(qb, 1), :] - ck
                    p_t = jnp.exp(s_t - lse_ref[h, pl.ds(qb, 1), :])
                    if diagonal:
                        p_t = jnp.where(qry >= key, p_t, 0.0)
                    ds_t = p_t * (_dot_nt(vh, do2) - dd_ref[h, pl.ds(qb, 1), :])
                    dsb = ds_t.astype(MM)
                    dv = dv + _dot(p_t.astype(MM), do2)
                    dk = dk + _dot(dsb, q2)
                    dc = dc - jnp.sum(ds_t, axis=1, keepdims=True)
                    dq_ref[rows, lanes] += _dot_tn(dsb, kh)
                    dcq_ref[h, pl.ds(qb, 1), :] += jnp.sum(ds_t, axis=0, keepdims=True)
                    return dk, dv, dc

                init = (jnp.zeros((t, 128), F32), jnp.zeros((t, 128), F32), jnp.zeros((t, 1), F32))
                carry = block(kb, init, diagonal=True)
                dk, dv, dc = lax.fori_loop(kb + 1, n_blk, functools.partial(block, diagonal=False), carry)
                grads.append((dk, dv))
                dc_ref[:, h:h + 1] = dc
            low = _head_select((t, 128), 0)
            dk_ref[:, lanes] = jnp.where(low, grads[0][0], grads[1][0]).astype(MM)
            dv_ref[:, lanes] = jnp.where(low, grads[0][1], grads[1][1]).astype(MM)

        @pl.when(kb == n_blk - 1)
        def _():
            dq_ref[...] = dq_ref[...] * Q_SCALE

    return pl.pallas_call(
        body, name="fox_bwd", grid=(n_blk,),
        in_specs=[pl.BlockSpec((S, D_ATT), lambda i: (0, 0)), pl.BlockSpec((t, D_ATT), lambda i: (i, 1)),
                  pl.BlockSpec((t, D_ATT), lambda i: (i, 2)), pl.BlockSpec((S, D_ATT), lambda i: (0, 0)),
                  _resident(), _resident(), _resident(), _rows(t, N_HEADS)],
        out_specs=[_const((S, D_ATT)), _rows(t, D_ATT), _rows(t, D_ATT), _rows(t, N_HEADS),
                   _const((N_HEADS, n_blk, t))],
        out_shape=[jax.ShapeDtypeStruct((S, D_ATT), F32), jax.ShapeDtypeStruct((S, D_ATT), MM),
                   jax.ShapeDtypeStruct((S, D_ATT), MM), jax.ShapeDtypeStruct((S, N_HEADS), F32),
                   jax.ShapeDtypeStruct((N_HEADS, n_blk, t), F32)],
        compiler_params=_params("arbitrary"),
    )(fqkv, fqkv, fqkv, dcat, lse_row3, d_row3, c_row3, c_col)


def _swa_bwd(sqkv, dcat, biasm, sinks_slot, bucket, lse, d_col):
    S = sqkv.shape[0]
    n_blk = S // WINDOW

    def body(q_ref, kp_ref, kc_ref, vp_ref, vc_ref, do_ref, bias_ref, sink_ref, bk_ref, lse_ref, dd_ref,
             dq_ref, dk_ref, dv_ref, drb_ref, dsink_ref, ds_acc):
        n = pl.program_id(0)

        @pl.when(n == 0)
        def _():
            dk_ref[...] = jnp.zeros_like(dk_ref)
            dv_ref[...] = jnp.zeros_like(dv_ref)
            ds_acc[...] = jnp.zeros_like(ds_acc)
            dsink_ref[...] = jnp.zeros_like(dsink_ref)

        no_prev = jnp.where(n > 0, 0.0, NEG)
        prev = pl.ds(pl.multiple_of(jnp.maximum(n - 1, 0) * WINDOW, WINDOW), WINDOW)
        cur = pl.ds(pl.multiple_of(n * WINDOW, WINDOW), WINDOW)
        lane8 = lax.broadcasted_iota(jnp.int32, (1, N_HEADS), 1)
        dkp = jnp.zeros((WINDOW, D_KV), F32)
        dkc = jnp.zeros((WINDOW, D_KV), F32)
        dvp = jnp.zeros((WINDOW, D_KV), F32)
        dvc = jnp.zeros((WINDOW, D_KV), F32)
        dsink = jnp.zeros((1, N_HEADS), F32)
        for j in range(N_HEADS // 2):
            lanes = slice(j * 128, (j + 1) * 128)
            q2 = q_ref[:, lanes]
            do2 = do_ref[:, lanes]
            dqs = []
            for hh in range(2):
                s_ = 2 * j + hh
                sel = _head_select((WINDOW, 128), hh)
                qh = jnp.where(sel, q2, jnp.zeros_like(q2))
                doh = jnp.where(sel, do2, jnp.zeros_like(do2))
                lse_h = lse_ref[:, s_:s_ + 1]
                dd = dd_ref[:, s_:s_ + 1]
                pp = jnp.exp(_dot_nt(qh, kp_ref[...]) + bias_ref[s_, :, :WINDOW] + no_prev - lse_h)
                pc = jnp.exp(_dot_nt(qh, kc_ref[...]) + bias_ref[s_, :, WINDOW:] - lse_h)
                p_sink = jnp.exp(sink_ref[s_] - lse_h)
                dsp = pp * (_dot_nt(doh, vp_ref[...]) - dd)
                dsc = pc * (_dot_nt(doh, vc_ref[...]) - dd)
                dsink = dsink + jnp.where(lane8 == s_, -jnp.sum(p_sink * dd), 0.0)
                ds_acc[s_, :, :WINDOW] += dsp
                ds_acc[s_, :, WINDOW:] += dsc
                dspb, dscb = dsp.astype(MM), dsc.astype(MM)
                dqs.append(_dot(dspb, kp_ref[...]) + _dot(dscb, kc_ref[...]))
                dkp = dkp + _dot_tn(dspb, qh)
                dkc = dkc + _dot_tn(dscb, qh)
                dvp = dvp + _dot_tn(pp.astype(MM), doh)
                dvc = dvc + _dot_tn(pc.astype(MM), doh)
            dq_ref[:, lanes] = (jnp.where(_head_select((WINDOW, 128), 0), dqs[0], dqs[1]) * Q_SCALE).astype(MM)
        dk_ref[prev, :] += dkp
        dk_ref[cur, :] += dkc
        dv_ref[prev, :] += dvp
        dv_ref[cur, :] += dvc
        dsink_ref[...] += dsink

        @pl.when(n == n_blk - 1)
        def _():
            bk = bk_ref[...]
            rb = lax.broadcasted_iota(jnp.int32, (N_BUCKETS, N_HEADS), 0)
            cb = lax.broadcasted_iota(jnp.int32, (N_BUCKETS, N_HEADS), 1)
            out = jnp.zeros((N_BUCKETS, N_HEADS), F32)
            for s in range(N_HEADS):
                acc = ds_acc[s]
                for b in range(N_BUCKETS):
                    out = out + jnp.where((rb == b) & (cb == s), jnp.sum(jnp.where(bk == b, acc, 0.0)), 0.0)
            drb_ref[...] = out

    do_spec = pl.BlockSpec((WINDOW, D_ATT), lambda n: (n, 1))
    return pl.pallas_call(
        body, name="swa_bwd", grid=(n_blk,),
        in_specs=_swa_specs(S) + [do_spec, _resident(), pl.BlockSpec(memory_space=pltpu.SMEM), _resident(),
                                  _rows(WINDOW, N_HEADS), _rows(WINDOW, N_HEADS)],
        out_specs=[_rows(WINDOW, D_ATT), _const((S, D_KV)), _const((S, D_KV)), _const((N_BUCKETS, N_HEADS)),
                   _const((1, N_HEADS))],
        out_shape=[jax.ShapeDtypeStruct((S, D_ATT), MM), jax.ShapeDtypeStruct((S, D_KV), F32),
                   jax.ShapeDtypeStruct((S, D_KV), F32), jax.ShapeDtypeStruct((N_BUCKETS, N_HEADS), F32),
                   jax.ShapeDtypeStruct((1, N_HEADS), F32)],
        scratch_shapes=[pltpu.VMEM((N_HEADS, WINDOW, 2 * WINDOW), F32)],
        compiler_params=_params("arbitrary"),
    )(sqkv, sqkv, sqkv, sqkv, sqkv, dcat, biasm, sinks_slot, bucket, lse, d_col)


def _pre_attn_bwd(x, dh1, dq_fox, dk_fox, dv_fox, dsq, dsk, dsv, dff_t, wqkv, wswa, wfft, g1, tm):
    S = x.shape[0]

    def body(x_ref, dh1_ref, dq_ref, dk_ref, dv_ref, dsq_ref, dsk_ref, dsv_ref, dff_ref, wqkv_ref, wswa_ref,
             wff_ref, g1_ref, dx_ref, dz_ref, dg1_ref):
        i = pl.program_id(0)
        dq = dq_ref[...].astype(MM)
        dsk = dsk_ref[...].astype(MM)
        dsv = dsv_ref[...].astype(MM)
        dz_ref[:, 0:512] = dq
        dz_ref[:, 512:1024] = dk_ref[...]
        dz_ref[:, 1024:1536] = dv_ref[...]
        dz_ref[:, 1536:2048] = dsq_ref[...]
        dz_ref[:, 2048:2176] = dsk
        dz_ref[:, 2176:2304] = dsv
        da = (_dot_nt(dq, wqkv_ref[:, 0:512]) + _dot_nt(dk_ref[...], wqkv_ref[:, 512:1024])
              + _dot_nt(dv_ref[...], wqkv_ref[:, 1024:1536]) + _dot_nt(dsq_ref[...], wswa_ref[:, 0:512])
              + _dot_nt(dsk, wswa_ref[:, 512:640]) + _dot_nt(dsv, wswa_ref[:, 640:768])
              + _dot_tn(dff_ref[...].astype(MM), wff_ref[...]))
        n1, r1 = _rms(x_ref[...])
        dx, dg1 = _rms_bwd(da, n1, r1, g1_ref[...])
        _accumulate(dg1_ref, dg1, i)
        dx_ref[...] = dh1_ref[...] + dx

    return pl.pallas_call(
        body, name="pre_attn_bwd", grid=(S // tm,),
        in_specs=[_rows(tm, D_MODEL), _rows(tm, D_MODEL), _rows(tm, D_ATT), _rows(tm, D_ATT), _rows(tm, D_ATT),
                  _rows(tm, D_ATT), _rows(tm, D_KV), _rows(tm, D_KV), pl.BlockSpec((16, tm), lambda i: (0, i)),
                  _resident(), _resident(), _resident(), _const((1, D_MODEL))],
        out_specs=[_rows(tm, D_MODEL), _rows(tm, 2304), _const((1, D_MODEL))],
        out_shape=[jax.ShapeDtypeStruct((S, D_MODEL), F32), jax.ShapeDtypeStruct((S, 2304), MM),
                   jax.ShapeDtypeStruct((1, D_MODEL), F32)],
        compiler_params=_params("arbitrary"),
    )(x, dh1, dq_fox, dk_fox, dv_fox, dsq, dsk, dsv, dff_t, wqkv, wswa, wfft, g1)


def _weight_grad(a, b, name, tk, n_chunks=1, relu2=False):
    S, K = a.shape
    N = b.shape[1]
    cn = N // n_chunks

    def body(a_ref, b_ref, out_ref):
        av = a_ref[...]
        if relu2:
            av = jnp.square(jnp.maximum(av.astype(F32), 0.0))
        av = av.astype(MM)
        for j in range(n_chunks):
            val = _dot_tn(av, b_ref[:, j * cn:(j + 1) * cn].astype(MM)).astype(MM)
            if n_chunks > 1:
                out_ref[j] = val
            else:
                out_ref[...] = val

    if n_chunks > 1:
        out_spec = pl.BlockSpec((n_chunks, tk, cn), lambda i: (0, i, 0))
        out_shape = jax.ShapeDtypeStruct((n_chunks, K, cn), MM)
    else:
        out_spec = pl.BlockSpec((tk, N), lambda i: (i, 0))
        out_shape = jax.ShapeDtypeStruct((K, N), MM)
    return pl.pallas_call(
        body, name=name, grid=(K // tk,),
        in_specs=[pl.BlockSpec((S, tk), lambda i: (0, i)), _resident()],
        out_specs=out_spec, out_shape=out_shape, compiler_params=_params("parallel"),
    )(a, b)


def _forget_weight_grad(dff_t, a):
    def body(d_ref, a_ref, out_ref):
        out_ref[...] = _dot(d_ref[...].astype(MM), a_ref[...])

    return pl.pallas_call(
        body, name="forget_weight_grad", out_shape=jax.ShapeDtypeStruct((16, D_MODEL), F32),
        in_specs=[_resident(), _resident()], out_specs=_resident(),
    )(dff_t, a)


def _place():
    return lax.axis_index("x"), lax.axis_index("y"), lax.axis_index("c")


def _all_gather(blocks, n_full):
    n_all = len(blocks)
    n = n_full

    def body(*refs):
        local_sems = refs[-1]
        own = [pltpu.make_async_copy(refs[a], refs[n_all + a].at[4 * lax.axis_index("x") + 2 * lax.axis_index("y")
                                                                 + lax.axis_index("c")], local_sems.at[a])
               for a in range(n, n_all)]
        for cp in own:
            cp.start()
        gather(*refs[:n], *refs[n_all:n_all + n], *refs[2 * n_all:])
        for cp in own:
            cp.wait()

    def gather(*refs):
        ins, outs = refs[:n], refs[n:2 * n]
        send_sems, recv_sems, local_sems = refs[2 * n:]
        x, y, c = _place()
        me, sibling = (x, y, c), (x, y, 1 - c)
        chips = [(1 - x, y), (x, 1 - y), (1 - x, 1 - y)]

        def slot(out, place):
            px, py, pc = place
            return out.at[4 * px + 2 * py + pc]

        def copy(a, k, block, to, src=None):
            dst = slot(outs[a], block)
            return pltpu.make_async_remote_copy(
                src_ref=dst if src is None else src, dst_ref=dst, send_sem=send_sems.at[7 * a + k],
                recv_sem=recv_sems.at[7 * a + k], device_id=to, device_id_type=MESH)

        mine = [pltpu.make_async_copy(ins[a], slot(outs[a], me), local_sems.at[a]) for a in range(n)]
        for cp in mine:
            cp.start()
        first = []
        for a in range(n):
            first.append(copy(a, 0, me, sibling, src=ins[a]))
            first += [copy(a, 1 + j, me, (*chip, c), src=ins[a]) for j, chip in enumerate(chips)]
        for cp in first:
            cp.start()
        passed = []
        for j, chip in enumerate(chips):
            for a in range(n):
                copy(a, 1 + j, (*chip, c), me).wait_recv()
                fwd = copy(a, 4 + j, (*chip, c), sibling)
                fwd.start()
                passed.append(fwd)
        for a in range(n):
            copy(a, 0, sibling, me).wait_recv()
            for j, chip in enumerate(chips):
                copy(a, 4 + j, (*chip, 1 - c), me).wait_recv()
        for cp in first + passed:
            cp.wait_send()
        for cp in mine:
            cp.wait()

    hbm = pl.BlockSpec(memory_space=pl.ANY)
    return pl.pallas_call(
        body, name="all_gather_first",
        out_shape=[jax.ShapeDtypeStruct((N_DEV,) + b.shape, b.dtype) for b in blocks],
        in_specs=[hbm] * n_all, out_specs=[hbm] * n_all,
        scratch_shapes=[pltpu.SemaphoreType.DMA((7 * n,)), pltpu.SemaphoreType.DMA((7 * n,)),
                        pltpu.SemaphoreType.DMA((n_all,))],
    )(*blocks)


def _exchange_in_chip(grads):
    n = len(grads)

    def body(*refs):
        ins, outs = refs[:n], refs[n:2 * n]
        send_sems, recv_sems = refs[2 * n:]
        x, y, c = _place()
        copies = []
        for a in range(n):
            cp = pltpu.make_async_remote_copy(
                src_ref=ins[a].at[:, 1 - c], dst_ref=outs[a], send_sem=send_sems.at[a], recv_sem=recv_sems.at[a],
                device_id=(x, y, 1 - c), device_id_type=MESH)
            cp.start()
            copies.append(cp)
        for cp in copies:
            cp.wait()

    hbm = pl.BlockSpec(memory_space=pl.ANY)
    views = [g.reshape((4, 2) + g.shape[1:]) for g in grads]
    return pl.pallas_call(
        body, name="exchange_in_chip",
        out_shape=[jax.ShapeDtypeStruct((4,) + g.shape[1:], g.dtype) for g in grads],
        in_specs=[hbm] * n, out_specs=[hbm] * n,
        scratch_shapes=[pltpu.SemaphoreType.DMA((n,)), pltpu.SemaphoreType.DMA((n,))],
    )(*views)


def _chip_sum(grad, other, name):
    _, _, r, cdim = grad.shape
    tr = 256 if r % 256 == 0 else r

    def body(c_ref, g_ref, o_ref, out_ref):
        out_ref[...] = (g_ref[...].astype(F32) + o_ref[...].astype(F32)).astype(out_ref.dtype)

    return pl.pallas_call(
        body, name=name,
        grid_spec=pltpu.PrefetchScalarGridSpec(
            num_scalar_prefetch=1, grid=(4, r // tr),
            in_specs=[pl.BlockSpec((None, None, tr, cdim), lambda k, i, c_ref: (k, c_ref[0], i, 0)),
                      pl.BlockSpec((None, tr, cdim), lambda k, i, c_ref: (k, i, 0))],
            out_specs=pl.BlockSpec((None, tr, cdim), lambda k, i, c_ref: (k, i, 0))),
        out_shape=jax.ShapeDtypeStruct((4, r, cdim), MM),
        compiler_params=_params("parallel", "parallel"),
    )(lax.axis_index("c").astype(jnp.int32).reshape(1), grad, other)


def _exchange_between_chips(sums, small):
    n = len(sums)

    def body(*refs):
        ins, small_in = refs[:n], refs[n]
        outs, small_out = refs[n + 1:2 * n + 1], refs[2 * n + 1]
        send_sems, recv_sems, small_send, small_recv, local_sems = refs[2 * n + 2:]
        x, y, c = _place()
        my_chip = 2 * x + y
        chips = [(1 - x, y), (x, 1 - y), (1 - x, 1 - y)]
        copies = []
        for a in range(n):
            local = pltpu.make_async_copy(ins[a].at[my_chip], outs[a].at[my_chip], local_sems.at[a])
            local.start()
            copies.append(local)
            for j, (px, py) in enumerate(chips):
                cp = pltpu.make_async_remote_copy(
                    src_ref=ins[a].at[2 * px + py], dst_ref=outs[a].at[my_chip], send_sem=send_sems.at[3 * a + j],
                    recv_sem=recv_sems.at[3 * a + j], device_id=(px, py, c), device_id_type=MESH)
                cp.start()
                copies.append(cp)
        me = 4 * x + 2 * y + c
        local = pltpu.make_async_copy(small_in, small_out.at[me], local_sems.at[n])
        local.start()
        copies.append(local)
        k = 0
        for dx in range(2):
            for dy in range(2):
                for dc in range(2):
                    if dx + dy + dc == 0:
                        continue
                    peer = (x ^ dx, y ^ dy, c ^ dc)
                    cp = pltpu.make_async_remote_copy(
                        src_ref=small_in, dst_ref=small_out.at[me], send_sem=small_send.at[k],
                        recv_sem=small_recv.at[k], device_id=peer, device_id_type=MESH)
                    cp.start()
                    copies.append(cp)
                    k += 1
        for cp in copies:
            cp.wait()

    hbm = pl.BlockSpec(memory_space=pl.ANY)
    return pl.pallas_call(
        body, name="exchange_between_chips",
        out_shape=[jax.ShapeDtypeStruct(s.shape, s.dtype) for s in sums]
        + [jax.ShapeDtypeStruct((N_DEV,) + small.shape, small.dtype)],
        in_specs=[hbm] * (n + 1), out_specs=[hbm] * (n + 1),
        scratch_shapes=[pltpu.SemaphoreType.DMA((3 * n,)), pltpu.SemaphoreType.DMA((3 * n,)),
                        pltpu.SemaphoreType.DMA((7,)), pltpu.SemaphoreType.DMA((7,)),
                        pltpu.SemaphoreType.DMA((n + 1,))],
    )(*sums, small)


HBM_SPEC = pl.BlockSpec(memory_space=pltpu.HBM)
SEM_SPEC = pl.BlockSpec(memory_space=pltpu.SEMAPHORE)
DATAFLOW = pltpu.SideEffectType.DATAFLOW_SIDE_EFFECTING


def _exchange_start(name, arrays, n_copies, plan):
    n = len(arrays)

    def body(*refs):
        send_sems, recv_sems, token = refs[n], refs[n + 1], refs[2 * n + 2]
        for cp in plan(refs[:n], send_sems, recv_sems):
            cp.start()
        token[...] = jnp.zeros_like(token)

    out = pl.pallas_call(
        body, name=name,
        out_shape=(pltpu.SemaphoreType.DMA((n_copies,)), pltpu.SemaphoreType.DMA((n_copies,)),
                   *[pltpu.HBM(a.shape, a.dtype) for a in arrays], jax.ShapeDtypeStruct((1, D_MODEL), F32)),
        in_specs=[HBM_SPEC] * n,
        out_specs=(SEM_SPEC, SEM_SPEC, *[HBM_SPEC] * n, pl.BlockSpec(memory_space=pltpu.VMEM)),
        input_output_aliases={i: 2 + i for i in range(n)},
        compiler_params=pltpu.CompilerParams(has_side_effects=DATAFLOW),
    )(*[pltpu.with_memory_space_constraint(a, pltpu.HBM) for a in arrays])
    return (out[0], out[1]), list(out[2:2 + n]), out[2 + n]


def _exchange_wait(name, arrays, sems, after, plan):
    n = len(arrays)

    def body(*refs):
        send_sems, recv_sems = refs[n], refs[n + 1]
        for cp in plan(refs[:n], send_sems, recv_sems):
            cp.wait_send()
            cp.wait_recv()

    out = pl.pallas_call(
        body, name=name, out_shape=[pltpu.HBM(a.shape, a.dtype) for a in arrays],
        in_specs=[HBM_SPEC] * n + [SEM_SPEC, SEM_SPEC, pl.BlockSpec(memory_space=pl.ANY)],
        out_specs=[HBM_SPEC] * n, input_output_aliases={i: i for i in range(n)},
        compiler_params=pltpu.CompilerParams(has_side_effects=DATAFLOW),
    )(*arrays, sems[0], sems[1], after)
    return list(out)


def _remote(src, dst, send_sems, recv_sems, k, to):
    return pltpu.make_async_remote_copy(src_ref=src, dst_ref=dst, send_sem=send_sems.at[k], recv_sem=recv_sems.at[k],
                                        device_id=to, device_id_type=MESH)


def _plan_gather_direct(refs, send_sems, recv_sems):
    x, y, c = _place()
    me = 4 * x + 2 * y + c
    peers = [(x, y, 1 - c), (1 - x, y, c), (x, 1 - y, c), (1 - x, 1 - y, c)]
    return [_remote(ref.at[me], ref.at[me], send_sems, recv_sems, 4 * a + k, peer)
            for a, ref in enumerate(refs) for k, peer in enumerate(peers)]


def _plan_gather_pass_on(refs, send_sems, recv_sems):
    x, y, c = _place()
    chips = [(1 - x, y), (x, 1 - y), (1 - x, 1 - y)]
    return [_remote(ref.at[4 * px + 2 * py + c], ref.at[4 * px + 2 * py + c], send_sems, recv_sems, 3 * a + k,
                    (x, y, 1 - c))
            for a, ref in enumerate(refs) for k, (px, py) in enumerate(chips)]


def _plan_in_chip(refs, send_sems, recv_sems):
    n = len(refs) // 2
    x, y, c = _place()
    return [_remote(refs[a].at[:, 1 - c], refs[n + a], send_sems, recv_sems, a, (x, y, 1 - c)) for a in range(n)]


def _plan_between_chips(refs, send_sems, recv_sems):
    n = len(refs) // 2
    x, y, c = _place()
    chips = [(1 - x, y), (x, 1 - y), (1 - x, 1 - y)]
    return [_remote(refs[a].at[2 * px + py], refs[n + a].at[2 * x + y], send_sems, recv_sems, 3 * a + k, (px, py, c))
            for a in range(n) for k, (px, py) in enumerate(chips)]


def _adamw_math(w, g, m, v):
    m = ADAM_B1 * m + (1.0 - ADAM_B1) * g
    v = ADAM_B2 * v + (1.0 - ADAM_B2) * jnp.square(g)
    m_hat = m / (1.0 - ADAM_B1 ** ADAM_STEP)
    v_hat = v / (1.0 - ADAM_B2 ** ADAM_STEP)
    delta = -ADAM_LR * (m_hat / (jnp.sqrt(v_hat) + ADAM_EPS) + ADAM_WD * w)
    return delta, m, v


def _adamw(parts, w, m, v, name):
    n_parts, r, cdim = parts.shape
    tr = 256 if r % 256 == 0 else r

    def body(p_ref, w_ref, m_ref, v_ref, g_out, d_out, m_out, v_out):
        g = p_ref[0].astype(F32)
        for k in range(1, n_parts):
            g = g + p_ref[k].astype(F32)
        delta, m_new, v_new = _adamw_math(w_ref[...], g, m_ref[...], v_ref[...])
        g_out[...] = g
        d_out[...] = delta
        m_out[...] = m_new
        v_out[...] = v_new

    blk = pl.BlockSpec((tr, cdim), lambda i: (i, 0))
    return pl.pallas_call(
        body, name=name, grid=(r // tr,),
        in_specs=[pl.BlockSpec((n_parts, tr, cdim), lambda i: (0, i, 0)), blk, blk, blk],
        out_specs=[blk] * 4, out_shape=[jax.ShapeDtypeStruct((r, cdim), F32)] * 4,
        compiler_params=_params("parallel"),
    )(parts, w, m, v)


def _adamw_chips(parts, sums, w, m, v, name):
    _, r, cdim = parts.shape
    tr = 256 if r % 256 == 0 else r

    def body(chip_ref, p_ref, own_ref, w_ref, m_ref, v_ref, g_out, d_out, m_out, v_out):
        g = None
        for k in range(4):
            term = jnp.where(chip_ref[0] == k, own_ref[...], p_ref[k]).astype(F32)
            g = term if g is None else g + term
        delta, m_new, v_new = _adamw_math(w_ref[...], g, m_ref[...], v_ref[...])
        g_out[...] = g
        d_out[...] = delta
        m_out[...] = m_new
        v_out[...] = v_new

    blk = pl.BlockSpec((tr, cdim), lambda i, chip: (i, 0))
    my_chip = (2 * lax.axis_index("x") + lax.axis_index("y")).astype(jnp.int32).reshape(1)
    return pl.pallas_call(
        body, name=name,
        grid_spec=pltpu.PrefetchScalarGridSpec(
            num_scalar_prefetch=1, grid=(r // tr,),
            in_specs=[pl.BlockSpec((4, tr, cdim), lambda i, chip: (0, i, 0)),
                      pl.BlockSpec((None, tr, cdim), lambda i, chip: (chip[0], i, 0)), blk, blk, blk],
            out_specs=[blk] * 4),
        out_shape=[jax.ShapeDtypeStruct((r, cdim), F32)] * 4,
        compiler_params=_params("parallel"),
    )(my_chip, parts, sums, w, m, v)


class _NoExchange:
    def __init__(self, weights):
        self.weights = weights

    def before_pre_attn(self, g1):
        return g1

    def after_fox_fwd(self, fox_o, sinks_slot):
        return sinks_slot

    def after_attention(self, swa_o):
        return self.weights

    def after_mlp_grads(self, grads, g2):
        return g2

    def after_attn_out_bwd(self, dmix, d_row3):
        return d_row3


def _slot_order(t, axis):
    shp = t.shape
    t = t.reshape(shp[:axis] + (N_HEADS, shp[axis] // N_HEADS) + shp[axis + 1:])
    t = jnp.take(t, np.array(SLOT_HEAD), axis=axis)
    return t.reshape(shp)


def _head_order(t, axis):
    shp = t.shape
    t = t.reshape(shp[:axis] + (N_HEADS, shp[axis] // N_HEADS) + shp[axis + 1:])
    t = jnp.take(t, np.array(HEAD_SLOT), axis=axis)
    return t.reshape(shp)


def _forward_backward(x, p, target, win, hooks, b_forget, rel_bias, sinks, g1, g2, g3, g4, g5):
    S = x.shape[0]
    tm = 256
    t = 256
    wqkv = win[:, :3 * D_ATT]
    wfft = jnp.pad(win[:, 3 * D_ATT:3 * D_ATT + N_HEADS].T, ((0, 8), (0, 0)))
    q0 = 3 * D_ATT + N_HEADS
    wswa = jnp.concatenate([_slot_order(win[:, q0:q0 + D_ATT], 1), win[:, q0 + D_ATT:]], axis=1)
    bcol = jnp.pad(b_forget.reshape(N_HEADS, 1), ((0, 8), (0, 0)))
    rel_bias_slot = rel_bias[:, np.array(SLOT_HEAD)]
    sinks_slot = sinks.reshape(N_HEADS)[np.array(SLOT_HEAD)]
    bucket = jnp.asarray(_swa_bucket_map())

    a, fqkv, sqkv, fft = _pre_attn(x, hooks.before_pre_attn(g1), wqkv, wswa, wfft, tm)
    c_row = _forget_cumsum(fft, bcol)
    c_col = c_row[:N_HEADS].T
    c_row3 = c_row[:N_HEADS].reshape(N_HEADS, S // t, t)
    fox_o, fox_lse = _fox_fwd(fqkv, c_col, c_row3, t)
    biasm = _swa_bias(rel_bias_slot, bucket)
    sinks_slot = hooks.after_fox_fwd(fox_o, sinks_slot)
    swa_o, swa_lse = _swa_fwd(sqkv, biasm, sinks_slot)
    wout, w1, w2, wple, wg = hooks.after_attention(swa_o)
    wout_fox = wout[:D_ATT]
    wout_swa = _slot_order(wout[D_ATT:], 0)
    mix, h1, m = _post_attn(x, fox_o, swa_o, wout_fox, wout_swa, g2, g3, tm)
    u, y, h2 = _mlp_fwd(m, h1, w1, w2, g4, tm)
    dh2, dpe, dgl, dg5, loss = _ple_loss(h2, p, target, wg, wple, g5, tm)

    d_wple = _weight_grad(p, dpe, "grad_w_ple", tk=D_PLE, n_chunks=N_DEV)
    d_wg = _weight_grad(h2, dgl, "grad_w_ple_gate", tk=256)
    dh1, dy, du, dg4, dg3 = _mlp_bwd(dh2, y, h1, u, w1, w2, g4, g3, tm)
    d_w2 = _weight_grad(u, dy, "grad_w_ff2", tk=256, relu2=True)
    d_w1 = _weight_grad(m, du, "grad_w_ff1", tk=256, n_chunks=N_DEV)
    early = dict(w_ff1=d_w1, w_ff2=d_w2.reshape(N_DEV, FF_CHUNK, D_MODEL), w_ple=d_wple,
                 w_ple_gate=d_wg.reshape(N_DEV, D_MODEL // N_DEV, D_MODEL))
    g2_late = hooks.after_mlp_grads(early, g2)

    head = np.arange(D_ATT) // HEAD_DIM
    head_rows = jnp.asarray((head[None, :] == np.arange(N_HEADS)[:, None]).astype(np.float32))
    dmix, dcat, d_row, d_col, dg2 = _attn_out_bwd(dh1, mix, fox_o, swa_o, wout_fox, wout_swa, g2_late,
                                                  head_rows, head_rows.T, tm)
    d_wout_fox = _weight_grad(fox_o, dmix, "grad_w_out_fox", tk=256)
    d_wout_swa = _weight_grad(swa_o, dmix, "grad_w_out_swa", tk=256)

    lse_row3 = fox_lse.T.reshape(N_HEADS, S // t, t)
    d_row3 = hooks.after_attn_out_bwd(dmix, d_row.reshape(N_HEADS, S // t, t))
    dq_fox, dk_fox, dv_fox, dc_col, dcq = _fox_bwd(fqkv, dcat, lse_row3, d_row3, c_row3, c_col, t)
    dsq, dsk, dsv, d_rb_slot, d_sink_slot = _swa_bwd(sqkv, dcat, biasm, sinks_slot, bucket, swa_lse, d_col)
    dc_row = jnp.pad(dc_col.T + dcq.reshape(N_HEADS, S), ((0, 8), (0, 0)))
    dff_t, db = _forget_bwd(dc_row, fft, bcol)
    grad_x, dz, dg1 = _pre_attn_bwd(x, dh1, dq_fox, dk_fox, dv_fox, dsq, dsk, dsv, dff_t, wqkv, wswa, wfft, g1, tm)
    d_wmain = _weight_grad(a, dz, "grad_w_in", tk=256)
    d_wff_t = _forget_weight_grad(dff_t, a)

    d_win = jnp.concatenate([
        d_wmain[:, :3 * D_ATT], d_wff_t[:N_HEADS].T.astype(MM), _head_order(d_wmain[:, 3 * D_ATT:4 * D_ATT], 1),
        d_wmain[:, 4 * D_ATT:]], axis=1)
    d_win = d_win.reshape(D_MODEL, N_DEV, D_IN // N_DEV).transpose(1, 0, 2)
    d_wout = jnp.concatenate([d_wout_fox, _head_order(d_wout_swa, 0)], axis=0).reshape(N_DEV, D_MODEL // N_DEV, D_MODEL)
    big = dict(early, w_in=d_win, w_out=d_wout)
    small = dict(b_forget=db[:N_HEADS].reshape(1, N_HEADS), rel_bias=d_rb_slot[:, np.array(HEAD_SLOT)],
                 swa_sinks=d_sink_slot[:, np.array(HEAD_SLOT)], g_attn_pre=dg1, g_attn_post=dg2, g_ff_pre=dg3,
                 g_ff_post=dg4, g_ple_post=dg5)
    return loss, grad_x, big, small, dz


BIG = ("w_in", "w_out", "w_ff1", "w_ff2", "w_ple", "w_ple_gate")
SMALL_ROWS = ("g_attn_pre", "g_attn_post", "g_ff_pre", "g_ff_post", "g_ple_post")
WEIGHTS = ("w_in", "b_forget", "w_out", "rel_bias", "swa_sinks", "g_attn_pre", "g_attn_post", "w_ff1", "w_ff2",
           "g_ff_pre", "g_ff_post", "w_ple", "w_ple_gate", "g_ple_post")


EARLY = ("w_ff1", "w_ff2", "w_ple", "w_ple_gate")
LATE = ("w_in", "w_out")


class _Overlap:
    def __init__(self, later):
        self.later = later

    def before_pre_attn(self, g1):
        self.gather_sems, self.later, token = _exchange_start("gather_rest_start", self.later, 4 * 5, _plan_gather_direct)
        return g1 + token

    def after_fox_fwd(self, fox_o, sinks_slot):
        later = _exchange_wait("gather_rest_wait", self.later, self.gather_sems, fox_o, _plan_gather_direct)
        self.pass_sems, self.later, token = _exchange_start("gather_pass_on_start", later, 3 * 5, _plan_gather_pass_on)
        return sinks_slot + token[0, :N_HEADS]

    def after_attention(self, swa_o):
        wout_g, w1_g, w2_g, wple_g, wg_g = _exchange_wait("gather_pass_on_wait", self.later, self.pass_sems, swa_o,
                                                         _plan_gather_pass_on)
        return (wout_g.reshape(D_MODEL, D_MODEL), w1_g, w2_g.reshape(D_FF, D_MODEL),
                jnp.moveaxis(wple_g, 0, 1).reshape(D_PLE, D_MODEL), wg_g.reshape(D_MODEL, D_MODEL))

    def after_mlp_grads(self, grads, g2):
        views = [grads[k].reshape((4, 2) + grads[k].shape[1:]) for k in EARLY]
        lands = [lax.empty((4,) + grads[k].shape[1:], MM) for k in EARLY]
        self.in_chip_sems, self.in_chip, token = _exchange_start("grads_in_chip_start", views + lands, len(EARLY),
                                                                 _plan_in_chip)
        return g2 + token

    def after_attn_out_bwd(self, dmix, d_row3):
        arrays = _exchange_wait("grads_in_chip_wait", self.in_chip, self.in_chip_sems, dmix, _plan_in_chip)
        n = len(EARLY)
        sums = [_chip_sum(arrays[a], arrays[n + a], "chip_sum_" + k) for a, k in enumerate(EARLY)]
        lands = [lax.empty(s.shape, s.dtype) for s in sums]
        self.between_sems, self.between, token = _exchange_start("grads_between_chips_start", sums + lands, 3 * n,
                                                                 _plan_between_chips)
        return d_row3 + token[0, 0]

    def finish(self, after):
        arrays = _exchange_wait("grads_between_chips_wait", self.between, self.between_sems, after,
                                _plan_between_chips)
        n = len(EARLY)
        self.sums = arrays[:n]
        return arrays[n:]


def _pack_small(t):
    rows = [t[k].reshape(1, D_MODEL) for k in SMALL_ROWS]
    misc = jnp.concatenate([t["b_forget"].reshape(-1), t["swa_sinks"].reshape(-1), t["rel_bias"].reshape(-1)])
    rows.append(jnp.pad(misc, (0, D_MODEL - misc.shape[0])).reshape(1, D_MODEL))
    rows.append(jnp.pad(t["loss"].reshape(-1), (0, D_MODEL - 1)).reshape(1, D_MODEL))
    rows.append(jnp.zeros((1, D_MODEL), F32))
    return jnp.concatenate(rows, axis=0).astype(F32)


def _unpack_small(blk):
    out = {k: blk[i].reshape(1, D_MODEL) for i, k in enumerate(SMALL_ROWS)}
    misc = blk[len(SMALL_ROWS)]
    out["b_forget"] = misc[:N_HEADS].reshape(1, N_HEADS)
    out["swa_sinks"] = misc[N_HEADS:2 * N_HEADS].reshape(1, N_HEADS)
    out["rel_bias"] = misc[2 * N_HEADS:2 * N_HEADS + N_BUCKETS * N_HEADS].reshape(N_BUCKETS, N_HEADS)
    out["loss"] = blk[len(SMALL_ROWS) + 1, 0]
    return out


def kernel(x, p, w_in, b_forget, w_out, rel_bias, swa_sinks, g_attn_pre, g_attn_post, w_ff1, w_ff2, g_ff_pre, g_ff_post, w_ple, w_ple_gate, g_ple_post, loss_target, m_w_in, m_b_forget, m_w_out, m_rel_bias, m_swa_sinks, m_g_attn_pre, m_g_attn_post, m_w_ff1, m_w_ff2, m_g_ff_pre, m_g_ff_post, m_w_ple, m_w_ple_gate, m_g_ple_post, v_w_in, v_b_forget, v_w_out, v_rel_bias, v_swa_sinks, v_g_attn_pre, v_g_attn_post, v_w_ff1, v_w_ff2, v_g_ff_pre, v_g_ff_post, v_w_ple, v_w_ple_gate, v_g_ple_post):
    w = dict(w_in=w_in, b_forget=b_forget, w_out=w_out, rel_bias=rel_bias, swa_sinks=swa_sinks,
             g_attn_pre=g_attn_pre, g_attn_post=g_attn_post, w_ff1=w_ff1, w_ff2=w_ff2, g_ff_pre=g_ff_pre,
             g_ff_post=g_ff_post, w_ple=w_ple, w_ple_gate=w_ple_gate, g_ple_post=g_ple_post)
    mom = dict(w_in=m_w_in, b_forget=m_b_forget, w_out=m_w_out, rel_bias=m_rel_bias, swa_sinks=m_swa_sinks,
               g_attn_pre=m_g_attn_pre, g_attn_post=m_g_attn_post, w_ff1=m_w_ff1, w_ff2=m_w_ff2,
               g_ff_pre=m_g_ff_pre, g_ff_post=m_g_ff_post, w_ple=m_w_ple, w_ple_gate=m_w_ple_gate,
               g_ple_post=m_g_ple_post)
    var = dict(w_in=v_w_in, b_forget=v_b_forget, w_out=v_w_out, rel_bias=v_rel_bias, swa_sinks=v_swa_sinks,
               g_attn_pre=v_g_attn_pre, g_attn_post=v_g_attn_post, w_ff1=v_w_ff1, w_ff2=v_w_ff2,
               g_ff_pre=v_g_ff_pre, g_ff_post=v_g_ff_post, w_ple=v_w_ple, w_ple_gate=v_w_ple_gate,
               g_ple_post=v_g_ple_post)

    win_g, *later = _all_gather([w[k][0].astype(MM) for k in BIG], n_full=1)
    win = jnp.moveaxis(win_g, 0, 1).reshape(D_MODEL, D_IN)
    hooks = _Overlap(later)
    loss, grad_x, big, small, last = _forward_backward(
        x[0], p[0, 0], loss_target[0], win, hooks, b_forget, rel_bias, swa_sinks, g_attn_pre, g_attn_post,
        g_ff_pre, g_ff_post, g_ple_post)
    early_parts = hooks.finish(last)

    grads = [big[k] for k in LATE]
    views = [g.reshape((4, 2) + g.shape[1:]) for g in grads]
    others = _exchange_in_chip(grads)
    sums = [_chip_sum(v_, o, "chip_sum_" + k) for v_, o, k in zip(views, others, LATE)]
    small["loss"] = loss
    *parts, small_all = _exchange_between_chips(sums, _pack_small(small))

    out_g, out_d, out_m, out_v = {}, {}, {}, {}
    for k, part in zip(LATE, parts):
        g, d, m_new, v_new = _adamw(part, w[k][0], mom[k][0], var[k][0], "adamw_" + k)
        out_g[k], out_d[k], out_m[k], out_v[k] = g[None], d[None], m_new[None], v_new[None]
    for k, part, own in zip(EARLY, early_parts, hooks.sums):
        g, d, m_new, v_new = _adamw_chips(part, own, w[k][0], mom[k][0], var[k][0], "adamw_" + k)
        out_g[k], out_d[k], out_m[k], out_v[k] = g[None], d[None], m_new[None], v_new[None]
    rep = {k: w[k] for k in w if k not in BIG}
    rep["loss"] = jnp.zeros((), F32)
    rep_m = {k: mom[k] for k in mom if k not in BIG}
    rep_m["loss"] = jnp.zeros((), F32)
    rep_v = {k: var[k] for k in var if k not in BIG}
    rep_v["loss"] = jnp.ones((), F32)
    g_s, d_s, m_s, v_s = _adamw(small_all, _pack_small(rep), _pack_small(rep_m), _pack_small(rep_v), "adamw_small")
    g_s, d_s, m_s, v_s = _unpack_small(g_s), _unpack_small(d_s), _unpack_small(m_s), _unpack_small(v_s)
    for k in w:
        if k not in BIG:
            out_g[k], out_d[k], out_m[k], out_v[k] = g_s[k], d_s[k], m_s[k], v_s[k]
    return (g_s["loss"], grad_x[None], *[out_g[k] for k in WEIGHTS], *[out_d[k] for k in WEIGHTS],
            *[out_m[k] for k in WEIGHTS], *[out_v[k] for k in WEIGHTS])
```

```python
import functools

import numpy as np
import jax
import jax.numpy as jnp
from jax import lax
from jax.experimental import pallas as pl
from jax.experimental.pallas import tpu as pltpu

F32 = jnp.float32
MM = jnp.bfloat16

D_MODEL = 1024
HEAD_DIM = 64
N_HEADS = 8
D_ATT = N_HEADS * HEAD_DIM
D_KV = 128
D_FF = 4096
D_PLE = 256
D_IN = 3 * D_ATT + N_HEADS + D_ATT + 2 * D_KV
N_DEV = 8
W_IN_FLAT = (D_MODEL * D_IN // N_DEV // 128, 128)
FF_CHUNK = D_FF // N_DEV
WINDOW = 128
N_BUCKETS = 32
MAX_DISTANCE = 128
RMS_EPS = 1e-6
Q_SCALE = HEAD_DIM ** -0.5
NEG = -1e30

ADAM_LR = 0.001
ADAM_B1 = 0.9
ADAM_B2 = 0.999
ADAM_EPS = 1e-08
ADAM_WD = 0.01
ADAM_STEP = 10

SLOT_HEAD = (0, 4, 1, 5, 2, 6, 3, 7)
HEAD_SLOT = (0, 2, 4, 6, 1, 3, 5, 7)

VMEM_LIMIT = 56 * 1024 * 1024
MESH = pl.DeviceIdType.MESH

NT = (((1,), (1,)), ((), ()))
TN = (((0,), (0,)), ((), ()))


def _params(*semantics):
    return pltpu.CompilerParams(dimension_semantics=semantics, vmem_limit_bytes=VMEM_LIMIT)


def _resident():
    return pl.BlockSpec(memory_space=pltpu.VMEM)


def _rows(tm, width):
    return pl.BlockSpec((tm, width), lambda i: (i, 0))


def _const(shape):
    return pl.BlockSpec(shape, lambda i: (0,) * len(shape))


def _dot(a, b):
    return jnp.dot(a, b, preferred_element_type=F32)


def _dot_nt(a, b):
    return lax.dot_general(a, b, NT, preferred_element_type=F32)


def _dot_tn(a, b):
    return lax.dot_general(a, b, TN, preferred_element_type=F32)


def _rms(xf):
    r = lax.rsqrt(jnp.mean(xf * xf, axis=-1, keepdims=True) + RMS_EPS)
    return xf * r, r


def _rms_bwd(dout, n, r, g):
    dg = jnp.sum(dout * n, axis=0, keepdims=True)
    dn = dout * g
    dx = r * (dn - n * jnp.mean(dn * n, axis=-1, keepdims=True))
    return dx, dg


def _accumulate(ref, value, step):
    @pl.when(step == 0)
    def _():
        ref[...] = value

    @pl.when(step != 0)
    def _():
        ref[...] += value


def _t5_bucket(n):
    max_exact = N_BUCKETS // 2
    large = max_exact + (np.log(np.maximum(n, 1) / max_exact) / np.log(MAX_DISTANCE / max_exact)
                         * (N_BUCKETS - max_exact)).astype(np.int32)
    large = np.minimum(large, N_BUCKETS - 1)
    return np.where(n < max_exact, n, large).astype(np.int32)


def _swa_bucket_map():
    i = np.arange(WINDOW)[:, None]
    j = np.arange(2 * WINDOW)[None, :]
    dist = i + WINDOW - j
    ok = (dist >= 0) & (dist < WINDOW)
    return np.where(ok, _t5_bucket(np.clip(dist, 0, None)), -1).astype(np.int32)


def _pre_attn(x, g1, wqkv, wswa, wfft, tm):
    S = x.shape[0]

    def body(x_ref, g_ref, wqkv_ref, wswa_ref, wff_ref, a_ref, fqkv_ref, sqkv_ref, fft_ref):
        n, _ = _rms(x_ref[...])
        a = (n * g_ref[...]).astype(MM)
        a_ref[...] = a
        fqkv_ref[:, :D_ATT] = (_dot(a, wqkv_ref[:, :D_ATT]) * Q_SCALE).astype(MM)
        fqkv_ref[:, D_ATT:] = _dot(a, wqkv_ref[:, D_ATT:]).astype(MM)
        sqkv_ref[:, :D_ATT] = (_dot(a, wswa_ref[:, :D_ATT]) * Q_SCALE).astype(MM)
        sqkv_ref[:, D_ATT:] = _dot(a, wswa_ref[:, D_ATT:]).astype(MM)
        fft_ref[...] = _dot_nt(wff_ref[...], a)

    return pl.pallas_call(
        body, name="pre_attn", grid=(S // tm,),
        in_specs=[_rows(tm, D_MODEL), _const((1, D_MODEL)), _resident(), _resident(), _resident()],
        out_specs=[_rows(tm, D_MODEL), _rows(tm, 3 * D_ATT), _rows(tm, D_ATT + 2 * D_KV),
                   pl.BlockSpec((16, tm), lambda i: (0, i))],
        out_shape=[jax.ShapeDtypeStruct((S, D_MODEL), MM), jax.ShapeDtypeStruct((S, 3 * D_ATT), MM),
                   jax.ShapeDtypeStruct((S, D_ATT + 2 * D_KV), MM), jax.ShapeDtypeStruct((16, S), F32)],
        compiler_params=_params("parallel"),
    )(x, g1, wqkv, wswa, wfft)


def _lane_scan(v, reverse):
    S = v.shape[1]
    lane = lax.broadcasted_iota(jnp.int32, v.shape, 1)
    k = 1
    while k < S:
        if reverse:
            v = v + jnp.where(lane < S - k, pltpu.roll(v, S - k, axis=1), 0.0)
        else:
            v = v + jnp.where(lane >= k, pltpu.roll(v, k, axis=1), 0.0)
        k *= 2
    return v


def _forget_cumsum(fft, bcol):
    def body(f_ref, b_ref, c_ref):
        z = f_ref[...] + b_ref[...]
        log_f = jnp.minimum(z, 0.0) - jnp.log1p(jnp.exp(-jnp.abs(z)))
        c_ref[...] = _lane_scan(log_f, reverse=False)

    return pl.pallas_call(
        body, name="forget_cumsum", out_shape=jax.ShapeDtypeStruct(fft.shape, F32),
        in_specs=[_resident(), _resident()], out_specs=_resident(),
    )(fft, bcol)


def _forget_bwd(dc_row, fft, bcol):
    def body(dc_ref, f_ref, b_ref, dff_ref, db_ref):
        z = f_ref[...] + b_ref[...]
        dlog_f = _lane_scan(dc_ref[...], reverse=True)
        dff = dlog_f * (1.0 / (1.0 + jnp.exp(z)))
        dff_ref[...] = dff
        db_ref[...] = jnp.sum(dff, axis=1, keepdims=True)

    return pl.pallas_call(
        body, name="forget_bwd",
        out_shape=[jax.ShapeDtypeStruct(fft.shape, F32), jax.ShapeDtypeStruct((fft.shape[0], 1), F32)],
        in_specs=[_resident()] * 3, out_specs=[_resident()] * 2,
    )(dc_row, fft, bcol)


def _head_select(shape, upper):
    lane = lax.broadcasted_iota(jnp.int32, shape, 1)
    return lane >= HEAD_DIM if upper else lane < HEAD_DIM


def _fox_fwd(fqkv, c_row3, tq, tk, pairs_per_loop=2, row_chunks=1):
    S = fqkv.shape[0]
    rq = tq // row_chunks
    n_band = tq // tk

    def body(q_ref, k_ref, v_ref, ck_ref, o_ref, lse_ref):
        qi = pl.program_id(0)
        row = lax.broadcasted_iota(jnp.int32, (rq, tk), 0)
        col = lax.broadcasted_iota(jnp.int32, (rq, tk), 1)
        low = _head_select((rq, 128), 0)
        for first in range(0, N_HEADS // 2, pairs_per_loop):
            pairs = range(first, first + pairs_per_loop)
            chains = [(pr, hh, rc) for pr in pairs for hh in range(2) for rc in range(row_chunks)]
            qh = {}
            for pr in pairs:
                for rc in range(row_chunks):
                    q2 = q_ref[rc * rq:(rc + 1) * rq, pr * 128:(pr + 1) * 128]
                    qh[pr, 0, rc] = jnp.where(low, q2, jnp.zeros_like(q2))
                    qh[pr, 1, rc] = jnp.where(low, jnp.zeros_like(q2), q2)

            def block(kb, carry, band, chains=chains, qh=qh):
                rows = pl.ds(pl.multiple_of(kb * tk, tk), tk)
                out = []
                for (pr, hh, rc), (m, l, acc) in zip(chains, carry):
                    if band is not None and (rc + 1) * rq <= band * tk:
                        out.append((m, l, acc))
                        continue
                    lanes = slice(pr * 128, (pr + 1) * 128)
                    s = _dot_nt(qh[pr, hh, rc], k_ref[rows, lanes]) - ck_ref[2 * pr + hh, pl.ds(kb, 1), :]
                    if band is not None:
                        s = jnp.where(row + rc * rq >= col + band * tk, s, NEG)
                    m_new = jnp.maximum(m, jnp.max(s, axis=-1, keepdims=True))
                    p = jnp.exp(s - m_new)
                    alpha = jnp.exp(m - m_new)
                    l = alpha * l + jnp.sum(p, axis=-1, keepdims=True)
                    acc = alpha * acc + _dot(p.astype(MM), v_ref[rows, lanes])
                    out.append((m_new, l, acc))
                return tuple(out)

            carry = tuple((jnp.full((rq, 1), NEG, F32), jnp.zeros((rq, 1), F32), jnp.zeros((rq, 128), F32))
                          for _ in chains)
            carry = lax.fori_loop(0, qi * n_band, functools.partial(block, band=None), carry)
            for band in range(n_band):
                carry = block(qi * n_band + band, carry, band=band)
            res = {}
            for (pr, hh, rc), (m, l, acc) in zip(chains, carry):
                res[pr, hh, rc] = acc / l
                lse_ref[rc * rq:(rc + 1) * rq, 2 * pr + hh:2 * pr + hh + 1] = m + jnp.log(l)
            for pr in pairs:
                for rc in range(row_chunks):
                    o_ref[rc * rq:(rc + 1) * rq, pr * 128:(pr + 1) * 128] = jnp.where(
                        low, res[pr, 0, rc], res[pr, 1, rc]).astype(MM)

    return pl.pallas_call(
        body, name="fox_fwd", grid=(S // tq,),
        in_specs=[pl.BlockSpec((tq, D_ATT), lambda i: (i, 0)), pl.BlockSpec((S, D_ATT), lambda i: (0, 1)),
                  pl.BlockSpec((S, D_ATT), lambda i: (0, 2)), _resident()],
        out_specs=[_rows(tq, D_ATT), _rows(tq, N_HEADS)],
        out_shape=[jax.ShapeDtypeStruct((S, D_ATT), MM), jax.ShapeDtypeStruct((S, N_HEADS), F32)],
        compiler_params=_params("parallel"),
    )(fqkv, fqkv, fqkv, c_row3)


def _swa_bias(rel_bias_slot, bucket):
    def body(rb_ref, bk_ref, out_ref):
        bk = bk_ref[...]
        for s in range(N_HEADS):
            acc = jnp.where(bk < 0, NEG, 0.0).astype(F32)
            for b in range(N_BUCKETS):
                acc = jnp.where(bk == b, rb_ref[b, s], acc)
            out_ref[s] = acc

    return pl.pallas_call(
        body, name="swa_bias", out_shape=jax.ShapeDtypeStruct((N_HEADS, WINDOW, 2 * WINDOW), F32),
        in_specs=[pl.BlockSpec(memory_space=pltpu.SMEM), _resident()], out_specs=_resident(),
    )(rel_bias_slot, bucket)


def _swa_specs(S):
    q = pl.BlockSpec((WINDOW, D_ATT), lambda n: (n, 0))
    kp = pl.BlockSpec((WINDOW, D_KV), lambda n: (jnp.maximum(n - 1, 0), 4))
    kc = pl.BlockSpec((WINDOW, D_KV), lambda n: (n, 4))
    vp = pl.BlockSpec((WINDOW, D_KV), lambda n: (jnp.maximum(n - 1, 0), 5))
    vc = pl.BlockSpec((WINDOW, D_KV), lambda n: (n, 5))
    return [q, kp, kc, vp, vc]


def _swa_fwd(sqkv, biasm, sinks_slot):
    S = sqkv.shape[0]

    def body(q_ref, kp_ref, kc_ref, vp_ref, vc_ref, bias_ref, sink_ref, o_ref, lse_ref):
        n = pl.program_id(0)
        no_prev = jnp.where(n > 0, 0.0, NEG)
        for j in range(N_HEADS // 2):
            lanes = slice(j * 128, (j + 1) * 128)
            q2 = q_ref[:, lanes]
            res = []
            for hh in range(2):
                s_ = 2 * j + hh
                qh = jnp.where(_head_select((WINDOW, 128), hh), q2, jnp.zeros_like(q2))
                sp = _dot_nt(qh, kp_ref[...]) + bias_ref[s_, :, :WINDOW] + no_prev
                sc = _dot_nt(qh, kc_ref[...]) + bias_ref[s_, :, WINDOW:]
                sink = sink_ref[s_]
                m = jnp.maximum(jnp.maximum(jnp.max(sp, axis=-1, keepdims=True),
                                            jnp.max(sc, axis=-1, keepdims=True)), sink)
                ep = jnp.exp(sp - m)
                ec = jnp.exp(sc - m)
                den = (jnp.sum(ep, axis=-1, keepdims=True) + jnp.sum(ec, axis=-1, keepdims=True)
                       + jnp.exp(sink - m))
                res.append((_dot(ep.astype(MM), vp_ref[...]) + _dot(ec.astype(MM), vc_ref[...])) / den)
                lse_ref[:, s_:s_ + 1] = m + jnp.log(den)
            o_ref[:, lanes] = jnp.where(_head_select((WINDOW, 128), 0), res[0], res[1]).astype(MM)

    return pl.pallas_call(
        body, name="swa_fwd", grid=(S // WINDOW,),
        in_specs=_swa_specs(S) + [_resident(), pl.BlockSpec(memory_space=pltpu.SMEM)],
        out_specs=[_rows(WINDOW, D_ATT), _rows(WINDOW, N_HEADS)],
        out_shape=[jax.ShapeDtypeStruct((S, D_ATT), MM), jax.ShapeDtypeStruct((S, N_HEADS), F32)],
        compiler_params=_params("parallel"),
    )(sqkv, sqkv, sqkv, sqkv, sqkv, biasm, sinks_slot)


def _post_attn(x, fox_o, swa_o, wout_fox, wout_swa, g2, g3, tm):
    S = x.shape[0]

    def body(x_ref, fo_ref, so_ref, wf_ref, ws_ref, g2_ref, g3_ref, mix_ref, h1_ref, m_ref):
        mix = _dot(fo_ref[...], wf_ref[...]) + _dot(so_ref[...], ws_ref[...])
        mix_ref[...] = mix
        n2, _ = _rms(mix)
        h1 = x_ref[...] + n2 * g2_ref[...]
        h1_ref[...] = h1
        n3, _ = _rms(h1)
        m_ref[...] = (n3 * g3_ref[...]).astype(MM)

    return pl.pallas_call(
        body, name="post_attn", grid=(S // tm,),
        in_specs=[_rows(tm, D_MODEL), _rows(tm, D_ATT), _rows(tm, D_ATT), _resident(), _resident(),
                  _const((1, D_MODEL)), _const((1, D_MODEL))],
        out_specs=[_rows(tm, D_MODEL)] * 3,
        out_shape=[jax.ShapeDtypeStruct((S, D_MODEL), F32), jax.ShapeDtypeStruct((S, D_MODEL), F32),
                   jax.ShapeDtypeStruct((S, D_MODEL), MM)],
        compiler_params=_params("parallel"),
    )(x, fox_o, swa_o, wout_fox, wout_swa, g2, g3)


def _mlp_fwd(m, h1, w1, w2, g4, tm):
    S = m.shape[0]

    def body(m_ref, h1_ref, w1_ref, w2_ref, g4_ref, u_ref, y_ref, h2_ref):
        mb = m_ref[...]
        y = jnp.zeros((tm, D_MODEL), F32)
        for j in range(N_DEV):
            cols = slice(j * FF_CHUNK, (j + 1) * FF_CHUNK)
            u = _dot(mb, w1_ref[j])
            u_ref[:, cols] = u.astype(MM)
            y = y + _dot(jnp.square(jnp.maximum(u, 0.0)).astype(MM), w2_ref[cols, :])
        y_ref[...] = y
        n4, _ = _rms(y)
        h2_ref[...] = h1_ref[...] + n4 * g4_ref[...]

    return pl.pallas_call(
        body, name="mlp_fwd", grid=(S // tm,),
        in_specs=[_rows(tm, D_MODEL), _rows(tm, D_MODEL), _resident(), _resident(), _const((1, D_MODEL))],
        out_specs=[_rows(tm, D_FF), _rows(tm, D_MODEL), _rows(tm, D_MODEL)],
        out_shape=[jax.ShapeDtypeStruct((S, D_FF), MM), jax.ShapeDtypeStruct((S, D_MODEL), F32),
                   jax.ShapeDtypeStruct((S, D_MODEL), F32)],
        compiler_params=_params("parallel"),
    )(m, h1, w1, w2, g4)


def _ple_loss(h2, p, target, wg, wple, g5, tm):
    S = h2.shape[0]

    def body(h2_ref, p_ref, t_ref, wg_ref, wp_ref, g5_ref, dh2_ref, dpe_ref, dgl_ref, dg5_ref, loss_ref):
        i = pl.program_id(0)
        h2 = h2_ref[...]
        gate = jax.nn.sigmoid(_dot(h2.astype(MM), wg_ref[...]))
        pe = _dot(p_ref[...].astype(MM), wp_ref[...])
        n5, r5 = _rms(pe * gate)
        g5 = g5_ref[...]
        diff = h2 + n5 * g5 - t_ref[...]
        per_token = jnp.mean(jnp.square(diff), axis=-1, keepdims=True)
        _accumulate(loss_ref, 0.5 * jnp.sum(per_token, axis=0, keepdims=True), i)
        dh3 = diff * (1.0 / D_MODEL)
        de, dg5 = _rms_bwd(dh3, n5, r5, g5)
        _accumulate(dg5_ref, dg5, i)
        dpe_ref[...] = (de * gate).astype(MM)
        dgl = (de * pe * gate * (1.0 - gate)).astype(MM)
        dgl_ref[...] = dgl
        dh2_ref[...] = dh3 + _dot_nt(dgl, wg_ref[...])

    return pl.pallas_call(
        body, name="ple_loss", grid=(S // tm,),
        in_specs=[_rows(tm, D_MODEL), _rows(tm, D_PLE), _rows(tm, D_MODEL), _resident(), _resident(),
                  _const((1, D_MODEL))],
        out_specs=[_rows(tm, D_MODEL), _rows(tm, D_MODEL), _rows(tm, D_MODEL), _const((1, D_MODEL)), _const((1, 1))],
        out_shape=[jax.ShapeDtypeStruct((S, D_MODEL), F32), jax.ShapeDtypeStruct((S, D_MODEL), MM),
                   jax.ShapeDtypeStruct((S, D_MODEL), MM), jax.ShapeDtypeStruct((1, D_MODEL), F32),
                   jax.ShapeDtypeStruct((1, 1), F32)],
        compiler_params=_params("arbitrary"),
    )(h2, p, target, wg, wple, g5)


def _mlp_bwd(dh2, y, h1, u, w1, w2, g4, g3, tm):
    S = dh2.shape[0]

    def body(dh2_ref, y_ref, h1_ref, u_ref, w1_ref, w2_ref, g4_ref, g3_ref,
             dh1_ref, dy_ref, du_ref, dg4_ref, dg3_ref):
        i = pl.program_id(0)
        dh2 = dh2_ref[...]
        n4, r4 = _rms(y_ref[...])
        dy, dg4 = _rms_bwd(dh2, n4, r4, g4_ref[...])
        _accumulate(dg4_ref, dg4, i)
        dyb = dy.astype(MM)
        dy_ref[...] = dyb
        dm = jnp.zeros((tm, D_MODEL), F32)
        for j in range(N_DEV):
            cols = slice(j * FF_CHUNK, (j + 1) * FF_CHUNK)
            dact = _dot_nt(dyb, w2_ref[cols, :])
            du = (dact * (2.0 * jnp.maximum(u_ref[:, cols].astype(F32), 0.0))).astype(MM)
            du_ref[:, cols] = du
            dm = dm + _dot_nt(du, w1_ref[j])
        n3, r3 = _rms(h1_ref[...])
        dx, dg3 = _rms_bwd(dm, n3, r3, g3_ref[...])
        _accumulate(dg3_ref, dg3, i)
        dh1_ref[...] = dh2 + dx

    return pl.pallas_call(
        body, name="mlp_bwd", grid=(S // tm,),
        in_specs=[_rows(tm, D_MODEL), _rows(tm, D_MODEL), _rows(tm, D_MODEL), _rows(tm, D_FF),
                  _resident(), _resident(), _const((1, D_MODEL)), _const((1, D_MODEL))],
        out_specs=[_rows(tm, D_MODEL), _rows(tm, D_MODEL), _rows(tm, D_FF), _const((1, D_MODEL)),
                   _const((1, D_MODEL))],
        out_shape=[jax.ShapeDtypeStruct((S, D_MODEL), F32), jax.ShapeDtypeStruct((S, D_MODEL), MM),
                   jax.ShapeDtypeStruct((S, D_FF), MM), jax.ShapeDtypeStruct((1, D_MODEL), F32),
                   jax.ShapeDtypeStruct((1, D_MODEL), F32)],
        compiler_params=_params("arbitrary"),
    )(dh2, y, h1, u, w1, w2, g4, g3)


def _attn_out_bwd(dh1, mix, fox_o, swa_o, wout_fox, wout_swa, g2, head_rows, head_cols, tm):
    S = dh1.shape[0]

    def body(dh1_ref, mix_ref, fo_ref, so_ref, wf_ref, ws_ref, g2_ref, er_ref, ec_ref,
             dmix_ref, dcat_ref, drow_ref, dcol_ref, dg2_ref):
        i = pl.program_id(0)
        n2, r2 = _rms(mix_ref[...])
        dmix, dg2 = _rms_bwd(dh1_ref[...], n2, r2, g2_ref[...])
        _accumulate(dg2_ref, dg2, i)
        dmb = dmix.astype(MM)
        dmix_ref[...] = dmb
        dfo = _dot_nt(dmb, wf_ref[...]).astype(MM)
        dso = _dot_nt(dmb, ws_ref[...]).astype(MM)
        dcat_ref[:, :D_ATT] = dfo
        dcat_ref[:, D_ATT:] = dso
        hi = lax.Precision.HIGHEST
        prod_f = dfo.astype(F32) * fo_ref[...].astype(F32)
        prod_s = dso.astype(F32) * so_ref[...].astype(F32)
        drow_ref[...] = lax.dot_general(er_ref[...], prod_f, NT, precision=hi, preferred_element_type=F32)
        dcol_ref[...] = jnp.dot(prod_s, ec_ref[...], precision=hi, preferred_element_type=F32)

    return pl.pallas_call(
        body, name="attn_out_bwd", grid=(S // tm,),
        in_specs=[_rows(tm, D_MODEL), _rows(tm, D_MODEL), _rows(tm, D_ATT), _rows(tm, D_ATT), _resident(),
                  _resident(), _const((1, D_MODEL)), _resident(), _resident()],
        out_specs=[_rows(tm, D_MODEL), _rows(tm, D_MODEL), pl.BlockSpec((N_HEADS, tm), lambda i: (0, i)),
                   _rows(tm, N_HEADS), _const((1, D_MODEL))],
        out_shape=[jax.ShapeDtypeStruct((S, D_MODEL), MM), jax.ShapeDtypeStruct((S, D_MODEL), MM),
                   jax.ShapeDtypeStruct((N_HEADS, S), F32), jax.ShapeDtypeStruct((S, N_HEADS), F32),
                   jax.ShapeDtypeStruct((1, D_MODEL), F32)],
        compiler_params=_params("arbitrary"),
    )(dh1, mix, fox_o, swa_o, wout_fox, wout_swa, g2, head_rows, head_cols)


def _fox_bwd(fqkv, dcat, lse_row3, d_row3, c_col, tq, tk, pairs_per_loop=2):
    S = fqkv.shape[0]
    n_blk = S // tk
    n_qblk = S // tq
    n_band = tk // tq

    def body(q_ref, k_ref, v_ref, do_ref, lse_ref, dd_ref, ck_ref, dq_ref, dk_ref, dv_ref, dc_ref, dcq_ref):
        kb = pl.program_id(0)

        @pl.when(kb == 0)
        def _():
            dq_ref[...] = jnp.zeros_like(dq_ref)
            dcq_ref[...] = jnp.zeros_like(dcq_ref)

        key = lax.broadcasted_iota(jnp.int32, (tk, tq), 0)
        qry = lax.broadcasted_iota(jnp.int32, (tk, tq), 1)
        low = _head_select((tk, 128), 0)
        for first in range(0, N_HEADS // 2, pairs_per_loop):
            pairs = range(first, first + pairs_per_loop)
            heads = [(pr, hh) for pr in pairs for hh in range(2)]
            kh, vh, ck = {}, {}, {}
            for pr in pairs:
                k2 = k_ref[:, pr * 128:(pr + 1) * 128]
                v2 = v_ref[:, pr * 128:(pr + 1) * 128]
                zero = jnp.zeros_like(k2)
                kh[pr, 0], kh[pr, 1] = jnp.where(low, k2, zero), jnp.where(low, zero, k2)
                vh[pr, 0], vh[pr, 1] = jnp.where(low, v2, zero), jnp.where(low, zero, v2)
                for hh in range(2):
                    ck[pr, hh] = ck_ref[:, 2 * pr + hh:2 * pr + hh + 1]

            def block(qb, carry, band, pairs=pairs, kh=kh, vh=vh, ck=ck):
                rows = pl.ds(pl.multiple_of(qb * tq, tq), tq)
                out = []
                it = iter(carry)
                for pr in pairs:
                    lanes = slice(pr * 128, (pr + 1) * 128)
                    q2 = q_ref[rows, lanes]
                    do2 = do_ref[rows, lanes]
                    dq = None
                    for hh in range(2):
                        h = 2 * pr + hh
                        dk, dv, dc = next(it)
                        s_t = _dot_nt(kh[pr, hh], q2) - ck[pr, hh]
                        p_t = jnp.exp(s_t - lse_ref[h, pl.ds(qb, 1), :])
                        if band is not None:
                            p_t = jnp.where(qry + band * tq >= key, p_t, 0.0)
                        ds_t = p_t * (_dot_nt(vh[pr, hh], do2) - dd_ref[h, pl.ds(qb, 1), :])
                        dsb = ds_t.astype(MM)
                        dv = dv + _dot(p_t.astype(MM), do2)
                        dk = dk + _dot(dsb, q2)
                        dc = dc - jnp.sum(ds_t, axis=1, keepdims=True)
                        part = _dot_tn(dsb, kh[pr, hh])
                        dq = part if dq is None else dq + part
                        dcq_ref[h, pl.ds(qb, 1), :] += jnp.sum(ds_t, axis=0, keepdims=True)
                        out.append((dk, dv, dc))
                    dq_ref[rows, lanes] += dq
                return tuple(out)

            carry = tuple((jnp.zeros((tk, 128), F32), jnp.zeros((tk, 128), F32), jnp.zeros((tk, 1), F32))
                          for _ in heads)
            for band in range(n_band):
                carry = block(kb * n_band + band, carry, band=band)
            carry = lax.fori_loop((kb + 1) * n_band, n_qblk, functools.partial(block, band=None), carry)
            grads = dict(zip(heads, carry))
            for pr in pairs:
                lanes = slice(pr * 128, (pr + 1) * 128)
                dk_ref[:, lanes] = jnp.where(low, grads[pr, 0][0], grads[pr, 1][0]).astype(MM)
                dv_ref[:, lanes] = jnp.where(low, grads[pr, 0][1], grads[pr, 1][1]).astype(MM)
                for hh in range(2):
                    dc_ref[:, 2 * pr + hh:2 * pr + hh + 1] = grads[pr, hh][2]

        @pl.when(kb == n_blk - 1)
        def _():
            dq_ref[...] = dq_ref[...] * Q_SCALE

    return pl.pallas_call(
        body, name="fox_bwd", grid=(n_blk,),
        in_specs=[pl.BlockSpec((S, D_ATT), lambda i: (0, 0)), pl.BlockSpec((tk, D_ATT), lambda i: (i, 1)),
                  pl.BlockSpec((tk, D_ATT), lambda i: (i, 2)), pl.BlockSpec((S, D_ATT), lambda i: (0, 0)),
                  _resident(), _resident(), _rows(tk, N_HEADS)],
        out_specs=[_const((S, D_ATT)), _rows(tk, D_ATT), _rows(tk, D_ATT), _rows(tk, N_HEADS),
                   _const((N_HEADS, n_qblk, tq))],
        out_shape=[jax.ShapeDtypeStruct((S, D_ATT), F32), jax.ShapeDtypeStruct((S, D_ATT), MM),
                   jax.ShapeDtypeStruct((S, D_ATT), MM), jax.ShapeDtypeStruct((S, N_HEADS), F32),
                   jax.ShapeDtypeStruct((N_HEADS, n_qblk, tq), F32)],
        compiler_params=_params("arbitrary"),
    )(fqkv, fqkv, fqkv, dcat, lse_row3, d_row3, c_col)


def _swa_bwd(sqkv, dcat, biasm, sinks_slot, bucket, lse, d_col):
    S = sqkv.shape[0]
    n_blk = S // WINDOW

    def body(q_ref, kp_ref, kc_ref, vp_ref, vc_ref, do_ref, bias_ref, sink_ref, bk_ref, lse_ref, dd_ref,
             dq_ref, dk_ref, dv_ref, drb_ref, dsink_ref, ds_acc):
        n = pl.program_id(0)

        @pl.when(n == 0)
        def _():
            dk_ref[...] = jnp.zeros_like(dk_ref)
            dv_ref[...] = jnp.zeros_like(dv_ref)
            ds_acc[...] = jnp.zeros_like(ds_acc)
            dsink_ref[...] = jnp.zeros_like(dsink_ref)

        no_prev = jnp.where(n > 0, 0.0, NEG)
        prev = pl.ds(pl.multiple_of(jnp.maximum(n - 1, 0) * WINDOW, WINDOW), WINDOW)
        cur = pl.ds(pl.multiple_of(n * WINDOW, WINDOW), WINDOW)
        lane8 = lax.broadcasted_iota(jnp.int32, (1, N_HEADS), 1)
        dkp = jnp.zeros((WINDOW, D_KV), F32)
        dkc = jnp.zeros((WINDOW, D_KV), F32)
        dvp = jnp.zeros((WINDOW, D_KV), F32)
        dvc = jnp.zeros((WINDOW, D_KV), F32)
        dsink = jnp.zeros((1, N_HEADS), F32)
        for j in range(N_HEADS // 2):
            lanes = slice(j * 128, (j + 1) * 128)
            q2 = q_ref[:, lanes]
            do2 = do_ref[:, lanes]
            dqs = []
            for hh in range(2):
                s_ = 2 * j + hh
                sel = _head_select((WINDOW, 128), hh)
                qh = jnp.where(sel, q2, jnp.zeros_like(q2))
                doh = jnp.where(sel, do2, jnp.zeros_like(do2))
                lse_h = lse_ref[:, s_:s_ + 1]
                dd = dd_ref[:, s_:s_ + 1]
                pp = jnp.exp(_dot_nt(qh, kp_ref[...]) + bias_ref[s_, :, :WINDOW] + no_prev - lse_h)
                pc = jnp.exp(_dot_nt(qh, kc_ref[...]) + bias_ref[s_, :, WINDOW:] - lse_h)
                p_sink = jnp.exp(sink_ref[s_] - lse_h)
                dsp = pp * (_dot_nt(doh, vp_ref[...]) - dd)
                dsc = pc * (_dot_nt(doh, vc_ref[...]) - dd)
                dsink = dsink + jnp.where(lane8 == s_, -jnp.sum(p_sink * dd), 0.0)
                ds_acc[s_, :, :WINDOW] += dsp
                ds_acc[s_, :, WINDOW:] += dsc
                dspb, dscb = dsp.astype(MM), dsc.astype(MM)
                dqs.append(_dot(dspb, kp_ref[...]) + _dot(dscb, kc_ref[...]))
                dkp = dkp + _dot_tn(dspb, qh)
                dkc = dkc + _dot_tn(dscb, qh)
                dvp = dvp + _dot_tn(pp.astype(MM), doh)
                dvc = dvc + _dot_tn(pc.astype(MM), doh)
            dq_ref[:, lanes] = (jnp.where(_head_select((WINDOW, 128), 0), dqs[0], dqs[1]) * Q_SCALE).astype(MM)
        dk_ref[prev, :] += dkp
        dk_ref[cur, :] += dkc
        dv_ref[prev, :] += dvp
        dv_ref[cur, :] += dvc
        dsink_ref[...] += dsink

        @pl.when(n == n_blk - 1)
        def _():
            bk = bk_ref[...]
            rb = lax.broadcasted_iota(jnp.int32, (N_BUCKETS, N_HEADS), 0)
            cb = lax.broadcasted_iota(jnp.int32, (N_BUCKETS, N_HEADS), 1)
            out = jnp.zeros((N_BUCKETS, N_HEADS), F32)
            for s in range(N_HEADS):
                acc = ds_acc[s]
                for b in range(N_BUCKETS):
                    out = out + jnp.where((rb == b) & (cb == s), jnp.sum(jnp.where(bk == b, acc, 0.0)), 0.0)
            drb_ref[...] = out

    do_spec = pl.BlockSpec((WINDOW, D_ATT), lambda n: (n, 1))
    return pl.pallas_call(
        body, name="swa_bwd", grid=(n_blk,),
        in_specs=_swa_specs(S) + [do_spec, _resident(), pl.BlockSpec(memory_space=pltpu.SMEM), _resident(),
                                  _rows(WINDOW, N_HEADS), _rows(WINDOW, N_HEADS)],
        out_specs=[_rows(WINDOW, D_ATT), _const((S, D_KV)), _const((S, D_KV)), _const((N_BUCKETS, N_HEADS)),
                   _const((1, N_HEADS))],
        out_shape=[jax.ShapeDtypeStruct((S, D_ATT), MM), jax.ShapeDtypeStruct((S, D_KV), F32),
                   jax.ShapeDtypeStruct((S, D_KV), F32), jax.ShapeDtypeStruct((N_BUCKETS, N_HEADS), F32),
                   jax.ShapeDtypeStruct((1, N_HEADS), F32)],
        scratch_shapes=[pltpu.VMEM((N_HEADS, WINDOW, 2 * WINDOW), F32)],
        compiler_params=_params("arbitrary"),
    )(sqkv, sqkv, sqkv, sqkv, sqkv, dcat, biasm, sinks_slot, bucket, lse, d_col)


def _pre_attn_bwd(x, dh1, dq_fox, dk_fox, dv_fox, dsq, dsk, dsv, dff_t, wqkv, wswa, wfft, g1, tm):
    S = x.shape[0]

    def body(x_ref, dh1_ref, dq_ref, dk_ref, dv_ref, dsq_ref, dsk_ref, dsv_ref, dff_ref, wqkv_ref, wswa_ref,
             wff_ref, g1_ref, dx_ref, dz_ref, dg1_ref):
        i = pl.program_id(0)
        dq = dq_ref[...].astype(MM)
        dsk = dsk_ref[...].astype(MM)
        dsv = dsv_ref[...].astype(MM)
        dz_ref[:, 0:512] = dq
        dz_ref[:, 512:1024] = dk_ref[...]
        dz_ref[:, 1024:1536] = dv_ref[...]
        dz_ref[:, 1536:2048] = dsq_ref[...]
        dz_ref[:, 2048:2176] = dsk
        dz_ref[:, 2176:2304] = dsv
        da = (_dot_nt(dq, wqkv_ref[:, 0:512]) + _dot_nt(dk_ref[...], wqkv_ref[:, 512:1024])
              + _dot_nt(dv_ref[...], wqkv_ref[:, 1024:1536]) + _dot_nt(dsq_ref[...], wswa_ref[:, 0:512])
              + _dot_nt(dsk, wswa_ref[:, 512:640]) + _dot_nt(dsv, wswa_ref[:, 640:768])
              + _dot_tn(dff_ref[...].astype(MM), wff_ref[...]))
        n1, r1 = _rms(x_ref[...])
        dx, dg1 = _rms_bwd(da, n1, r1, g1_ref[...])
        _accumulate(dg1_ref, dg1, i)
        dx_ref[...] = dh1_ref[...] + dx

    return pl.pallas_call(
        body, name="pre_attn_bwd", grid=(S // tm,),
        in_specs=[_rows(tm, D_MODEL), _rows(tm, D_MODEL), _rows(tm, D_ATT), _rows(tm, D_ATT), _rows(tm, D_ATT),
                  _rows(tm, D_ATT), _rows(tm, D_KV), _rows(tm, D_KV), pl.BlockSpec((16, tm), lambda i: (0, i)),
                  _resident(), _resident(), _resident(), _const((1, D_MODEL))],
        out_specs=[_rows(tm, D_MODEL), _rows(tm, 2304), _const((1, D_MODEL))],
        out_shape=[jax.ShapeDtypeStruct((S, D_MODEL), F32), jax.ShapeDtypeStruct((S, 2304), MM),
                   jax.ShapeDtypeStruct((1, D_MODEL), F32)],
        compiler_params=_params("arbitrary"),
    )(x, dh1, dq_fox, dk_fox, dv_fox, dsq, dsk, dsv, dff_t, wqkv, wswa, wfft, g1)


def _weight_grad(a, b, name, tk, n_chunks=1, relu2=False):
    S, K = a.shape
    N = b.shape[1]
    cn = N // n_chunks

    def body(a_ref, b_ref, out_ref):
        av = a_ref[...]
        if relu2:
            av = jnp.square(jnp.maximum(av.astype(F32), 0.0))
        av = av.astype(MM)
        for j in range(n_chunks):
            val = _dot_tn(av, b_ref[:, j * cn:(j + 1) * cn].astype(MM)).astype(MM)
            if n_chunks > 1:
                out_ref[j] = val
            else:
                out_ref[...] = val

    if n_chunks > 1:
        out_spec = pl.BlockSpec((n_chunks, tk, cn), lambda i: (0, i, 0))
        out_shape = jax.ShapeDtypeStruct((n_chunks, K, cn), MM)
    else:
        out_spec = pl.BlockSpec((tk, N), lambda i: (i, 0))
        out_shape = jax.ShapeDtypeStruct((K, N), MM)
    return pl.pallas_call(
        body, name=name, grid=(K // tk,),
        in_specs=[pl.BlockSpec((S, tk), lambda i: (0, i)), _resident()],
        out_specs=out_spec, out_shape=out_shape, compiler_params=_params("parallel"),
    )(a, b)


def _forget_weight_grad(dff_t, a):
    def body(d_ref, a_ref, out_ref):
        out_ref[...] = _dot(d_ref[...].astype(MM), a_ref[...])

    return pl.pallas_call(
        body, name="forget_weight_grad", out_shape=jax.ShapeDtypeStruct((16, D_MODEL), F32),
        in_specs=[_resident(), _resident()], out_specs=_resident(),
    )(dff_t, a)


def _place():
    return lax.axis_index("x"), lax.axis_index("y"), lax.axis_index("c")


def _all_gather(blocks, n_full):
    n_all = len(blocks)
    n = n_full

    def body(*refs):
        local_sems = refs[-1]
        own = [pltpu.make_async_copy(refs[a], refs[n_all + a].at[4 * lax.axis_index("x") + 2 * lax.axis_index("y")
                                                                 + lax.axis_index("c")], local_sems.at[a])
               for a in range(n, n_all)]
        for cp in own:
            cp.start()
        gather(*refs[:n], *refs[n_all:n_all + n], *refs[2 * n_all:])
        for cp in own:
            cp.wait()

    def gather(*refs):
        ins, outs = refs[:n], refs[n:2 * n]
        send_sems, recv_sems, local_sems = refs[2 * n:]
        x, y, c = _place()
        me, sibling = (x, y, c), (x, y, 1 - c)
        chips = [(1 - x, y), (x, 1 - y), (1 - x, 1 - y)]

        def slot(out, place):
            px, py, pc = place
            return out.at[4 * px + 2 * py + pc]

        def copy(a, k, block, to, src=None):
            dst = slot(outs[a], block)
            return pltpu.make_async_remote_copy(
                src_ref=dst if src is None else src, dst_ref=dst, send_sem=send_sems.at[7 * a + k],
                recv_sem=recv_sems.at[7 * a + k], device_id=to, device_id_type=MESH)

        mine = [pltpu.make_async_copy(ins[a], slot(outs[a], me), local_sems.at[a]) for a in range(n)]
        for cp in mine:
            cp.start()
        first = []
        for a in range(n):
            first.append(copy(a, 0, me, sibling, src=ins[a]))
            first += [copy(a, 1 + j, me, (*chip, c), src=ins[a]) for j, chip in enumerate(chips)]
        for cp in first:
            cp.start()
        passed = []
        for j, chip in enumerate(chips):
            for a in range(n):
                copy(a, 1 + j, (*chip, c), me).wait_recv()
                fwd = copy(a, 4 + j, (*chip, c), sibling)
                fwd.start()
                passed.append(fwd)
        for a in range(n):
            copy(a, 0, sibling, me).wait_recv()
            for j, chip in enumerate(chips):
                copy(a, 4 + j, (*chip, 1 - c), me).wait_recv()
        for cp in first + passed:
            cp.wait_send()
        for cp in mine:
            cp.wait()

    hbm = pl.BlockSpec(memory_space=pl.ANY)
    return pl.pallas_call(
        body, name="all_gather_first",
        out_shape=[jax.ShapeDtypeStruct((N_DEV,) + b.shape, b.dtype) for b in blocks],
        in_specs=[hbm] * n_all, out_specs=[hbm] * n_all,
        scratch_shapes=[pltpu.SemaphoreType.DMA((7 * n,)), pltpu.SemaphoreType.DMA((7 * n,)),
                        pltpu.SemaphoreType.DMA((n_all,))],
    )(*blocks)


def _exchange_in_chip(grads):
    n = len(grads)

    def body(*refs):
        ins, outs = refs[:n], refs[n:2 * n]
        send_sems, recv_sems = refs[2 * n:]
        x, y, c = _place()
        copies = []
        for a in range(n):
            cp = pltpu.make_async_remote_copy(
                src_ref=ins[a].at[:, 1 - c], dst_ref=outs[a], send_sem=send_sems.at[a], recv_sem=recv_sems.at[a],
                device_id=(x, y, 1 - c), device_id_type=MESH)
            cp.start()
            copies.append(cp)
        for cp in copies:
            cp.wait()

    hbm = pl.BlockSpec(memory_space=pl.ANY)
    views = [g.reshape((4, 2) + g.shape[1:]) for g in grads]
    return pl.pallas_call(
        body, name="exchange_in_chip",
        out_shape=[jax.ShapeDtypeStruct((4,) + g.shape[1:], g.dtype) for g in grads],
        in_specs=[hbm] * n, out_specs=[hbm] * n,
        scratch_shapes=[pltpu.SemaphoreType.DMA((n,)), pltpu.SemaphoreType.DMA((n,))],
    )(*views)


def _chip_sum(grad, other, name):
    _, _, r, cdim = grad.shape
    tr = 256 if r % 256 == 0 else r

    def body(c_ref, g_ref, o_ref, out_ref):
        out_ref[...] = (g_ref[...].astype(F32) + o_ref[...].astype(F32)).astype(out_ref.dtype)

    return pl.pallas_call(
        body, name=name,
        grid_spec=pltpu.PrefetchScalarGridSpec(
            num_scalar_prefetch=1, grid=(4, r // tr),
            in_specs=[pl.BlockSpec((None, None, tr, cdim), lambda k, i, c_ref: (k, c_ref[0], i, 0)),
                      pl.BlockSpec((None, tr, cdim), lambda k, i, c_ref: (k, i, 0))],
            out_specs=pl.BlockSpec((None, tr, cdim), lambda k, i, c_ref: (k, i, 0))),
        out_shape=jax.ShapeDtypeStruct((4, r, cdim), MM),
        compiler_params=_params("parallel", "parallel"),
    )(lax.axis_index("c").astype(jnp.int32).reshape(1), grad, other)


def _exchange_between_chips(sums, small):
    n = len(sums)

    def body(*refs):
        ins, small_in = refs[:n], refs[n]
        outs, small_out = refs[n + 1:2 * n + 1], refs[2 * n + 1]
        send_sems, recv_sems, small_send, small_recv, local_sems = refs[2 * n + 2:]
        x, y, c = _place()
        my_chip = 2 * x + y
        chips = [(1 - x, y), (x, 1 - y), (1 - x, 1 - y)]
        copies = []
        for a in range(n):
            local = pltpu.make_async_copy(ins[a].at[my_chip], outs[a].at[my_chip], local_sems.at[a])
            local.start()
            copies.append(local)
            for j, (px, py) in enumerate(chips):
                cp = pltpu.make_async_remote_copy(
                    src_ref=ins[a].at[2 * px + py], dst_ref=outs[a].at[my_chip], send_sem=send_sems.at[3 * a + j],
                    recv_sem=recv_sems.at[3 * a + j], device_id=(px, py, c), device_id_type=MESH)
                cp.start()
                copies.append(cp)
        me = 4 * x + 2 * y + c
        local = pltpu.make_async_copy(small_in, small_out.at[me], local_sems.at[n])
        local.start()
        copies.append(local)
        k = 0
        for dx in range(2):
            for dy in range(2):
                for dc in range(2):
                    if dx + dy + dc == 0:
                        continue
                    peer = (x ^ dx, y ^ dy, c ^ dc)
                    cp = pltpu.make_async_remote_copy(
                        src_ref=small_in, dst_ref=small_out.at[me], send_sem=small_send.at[k],
                        recv_sem=small_recv.at[k], device_id=peer, device_id_type=MESH)
                    cp.start()
                    copies.append(cp)
                    k += 1
        for cp in copies:
            cp.wait()

    hbm = pl.BlockSpec(memory_space=pl.ANY)
    return pl.pallas_call(
        body, name="exchange_between_chips",
        out_shape=[jax.ShapeDtypeStruct(s.shape, s.dtype) for s in sums]
        + [jax.ShapeDtypeStruct((N_DEV,) + small.shape, small.dtype)],
        in_specs=[hbm] * (n + 1), out_specs=[hbm] * (n + 1),
        scratch_shapes=[pltpu.SemaphoreType.DMA((3 * n,)), pltpu.SemaphoreType.DMA((3 * n,)),
                        pltpu.SemaphoreType.DMA((7,)), pltpu.SemaphoreType.DMA((7,)),
                        pltpu.SemaphoreType.DMA((n + 1,))],
    )(*sums, small)


HBM_SPEC = pl.BlockSpec(memory_space=pltpu.HBM)
SEM_SPEC = pl.BlockSpec(memory_space=pltpu.SEMAPHORE)
DATAFLOW = pltpu.SideEffectType.DATAFLOW_SIDE_EFFECTING


def _exchange_start(name, arrays, n_copies, plan):
    n = len(arrays)

    def body(*refs):
        send_sems, recv_sems, token = refs[n], refs[n + 1], refs[2 * n + 2]
        for cp in plan(refs[:n], send_sems, recv_sems):
            cp.start()
        token[...] = jnp.zeros_like(token)

    out = pl.pallas_call(
        body, name=name,
        out_shape=(pltpu.SemaphoreType.DMA((n_copies,)), pltpu.SemaphoreType.DMA((n_copies,)),
                   *[pltpu.HBM(a.shape, a.dtype) for a in arrays], jax.ShapeDtypeStruct((1, D_MODEL), F32)),
        in_specs=[HBM_SPEC] * n,
        out_specs=(SEM_SPEC, SEM_SPEC, *[HBM_SPEC] * n, pl.BlockSpec(memory_space=pltpu.VMEM)),
        input_output_aliases={i: 2 + i for i in range(n)},
        compiler_params=pltpu.CompilerParams(has_side_effects=DATAFLOW),
    )(*[pltpu.with_memory_space_constraint(a, pltpu.HBM) for a in arrays])
    return (out[0], out[1]), list(out[2:2 + n]), out[2 + n]


def _exchange_wait(name, arrays, sems, after, plan):
    n = len(arrays)

    def body(*refs):
        send_sems, recv_sems = refs[n], refs[n + 1]
        for cp in plan(refs[:n], send_sems, recv_sems):
            cp.wait_send()
            cp.wait_recv()

    out = pl.pallas_call(
        body, name=name, out_shape=[pltpu.HBM(a.shape, a.dtype) for a in arrays],
        in_specs=[HBM_SPEC] * n + [SEM_SPEC, SEM_SPEC, pl.BlockSpec(memory_space=pl.ANY)],
        out_specs=[HBM_SPEC] * n, input_output_aliases={i: i for i in range(n)},
        compiler_params=pltpu.CompilerParams(has_side_effects=DATAFLOW),
    )(*arrays, sems[0], sems[1], after)
    return list(out)


def _remote(src, dst, send_sems, recv_sems, k, to):
    return pltpu.make_async_remote_copy(src_ref=src, dst_ref=dst, send_sem=send_sems.at[k], recv_sem=recv_sems.at[k],
                                        device_id=to, device_id_type=MESH)


def _plan_gather_direct(refs, send_sems, recv_sems):
    x, y, c = _place()
    me = 4 * x + 2 * y + c
    peers = [(x, y, 1 - c), (1 - x, y, c), (x, 1 - y, c), (1 - x, 1 - y, c)]
    return [_remote(ref.at[me], ref.at[me], send_sems, recv_sems, 4 * a + k, peer)
            for a, ref in enumerate(refs) for k, peer in enumerate(peers)]


def _plan_gather_pass_on(refs, send_sems, recv_sems):
    x, y, c = _place()
    chips = [(1 - x, y), (x, 1 - y), (1 - x, 1 - y)]
    return [_remote(ref.at[4 * px + 2 * py + c], ref.at[4 * px + 2 * py + c], send_sems, recv_sems, 3 * a + k,
                    (x, y, 1 - c))
            for a, ref in enumerate(refs) for k, (px, py) in enumerate(chips)]


def _plan_in_chip(refs, send_sems, recv_sems):
    n = len(refs) // 2
    x, y, c = _place()
    return [_remote(refs[a].at[:, 1 - c], refs[n + a], send_sems, recv_sems, a, (x, y, 1 - c)) for a in range(n)]


def _plan_between_chips(refs, send_sems, recv_sems):
    n = len(refs) // 2
    x, y, c = _place()
    chips = [(1 - x, y), (x, 1 - y), (1 - x, 1 - y)]
    return [_remote(refs[a].at[2 * px + py], refs[n + a].at[2 * x + y], send_sems, recv_sems, 3 * a + k, (px, py, c))
            for a in range(n) for k, (px, py) in enumerate(chips)]


def _adamw_math(w, g, m, v):
    m = ADAM_B1 * m + (1.0 - ADAM_B1) * g
    v = ADAM_B2 * v + (1.0 - ADAM_B2) * jnp.square(g)
    m_hat = m / (1.0 - ADAM_B1 ** ADAM_STEP)
    v_hat = v / (1.0 - ADAM_B2 ** ADAM_STEP)
    delta = -ADAM_LR * (m_hat / (jnp.sqrt(v_hat) + ADAM_EPS) + ADAM_WD * w)
    return delta, m, v


def _adamw(parts, w, m, v, name):
    n_parts, r, cdim = parts.shape
    tr = 256 if r % 256 == 0 else r

    def body(p_ref, w_ref, m_ref, v_ref, g_out, d_out, m_out, v_out):
        g = p_ref[0].astype(F32)
        for k in range(1, n_parts):
            g = g + p_ref[k].astype(F32)
        delta, m_new, v_new = _adamw_math(w_ref[...], g, m_ref[...], v_ref[...])
        g_out[...] = g
        d_out[...] = delta
        m_out[...] = m_new
        v_out[...] = v_new

    blk = pl.BlockSpec((tr, cdim), lambda i: (i, 0))
    return pl.pallas_call(
        body, name=name, grid=(r // tr,),
        in_specs=[pl.BlockSpec((n_parts, tr, cdim), lambda i: (0, i, 0)), blk, blk, blk],
        out_specs=[blk] * 4, out_shape=[jax.ShapeDtypeStruct((r, cdim), F32)] * 4,
        compiler_params=_params("parallel"),
    )(parts, w, m, v)


def _adamw_chips(parts, sums, w, m, v, name):
    _, r, cdim = parts.shape
    tr = 256 if r % 256 == 0 else r

    def body(chip_ref, p_ref, own_ref, w_ref, m_ref, v_ref, g_out, d_out, m_out, v_out):
        g = None
        for k in range(4):
            term = jnp.where(chip_ref[0] == k, own_ref[...], p_ref[k]).astype(F32)
            g = term if g is None else g + term
        delta, m_new, v_new = _adamw_math(w_ref[...], g, m_ref[...], v_ref[...])
        g_out[...] = g
        d_out[...] = delta
        m_out[...] = m_new
        v_out[...] = v_new

    blk = pl.BlockSpec((tr, cdim), lambda i, chip: (i, 0))
    my_chip = (2 * lax.axis_index("x") + lax.axis_index("y")).astype(jnp.int32).reshape(1)
    return pl.pallas_call(
        body, name=name,
        grid_spec=pltpu.PrefetchScalarGridSpec(
            num_scalar_prefetch=1, grid=(r // tr,),
            in_specs=[pl.BlockSpec((4, tr, cdim), lambda i, chip: (0, i, 0)),
                      pl.BlockSpec((None, tr, cdim), lambda i, chip: (chip[0], i, 0)), blk, blk, blk],
            out_specs=[blk] * 4),
        out_shape=[jax.ShapeDtypeStruct((r, cdim), F32)] * 4,
        compiler_params=_params("parallel"),
    )(my_chip, parts, sums, w, m, v)


class _NoExchange:
    def __init__(self, weights):
        self.weights = weights

    def before_pre_attn(self, g1):
        return g1

    def after_fox_fwd(self, fox_o, sinks_slot):
        return sinks_slot

    def after_attention(self, swa_o):
        return self.weights

    def after_mlp_grads(self, grads, g2):
        return g2

    def after_attn_out_bwd(self, dmix, d_row3):
        return d_row3


def _slot_order(t, axis):
    shp = t.shape
    t = t.reshape(shp[:axis] + (N_HEADS, shp[axis] // N_HEADS) + shp[axis + 1:])
    t = jnp.take(t, np.array(SLOT_HEAD), axis=axis)
    return t.reshape(shp)


def _head_order(t, axis):
    shp = t.shape
    t = t.reshape(shp[:axis] + (N_HEADS, shp[axis] // N_HEADS) + shp[axis + 1:])
    t = jnp.take(t, np.array(HEAD_SLOT), axis=axis)
    return t.reshape(shp)


def _forward_backward(x, p, target, win, hooks, b_forget, rel_bias, sinks, g1, g2, g3, g4, g5):
    S = x.shape[0]
    tm = 256
    t = 256
    wqkv = win[:, :3 * D_ATT]
    wfft = jnp.pad(win[:, 3 * D_ATT:3 * D_ATT + N_HEADS].T, ((0, 8), (0, 0)))
    q0 = 3 * D_ATT + N_HEADS
    wswa = jnp.concatenate([_slot_order(win[:, q0:q0 + D_ATT], 1), win[:, q0 + D_ATT:]], axis=1)
    bcol = jnp.pad(b_forget.reshape(N_HEADS, 1), ((0, 8), (0, 0)))
    rel_bias_slot = rel_bias[:, np.array(SLOT_HEAD)]
    sinks_slot = sinks.reshape(N_HEADS)[np.array(SLOT_HEAD)]
    bucket = jnp.asarray(_swa_bucket_map())

    a, fqkv, sqkv, fft = _pre_attn(x, hooks.before_pre_attn(g1), wqkv, wswa, wfft, tm)
    c_row = _forget_cumsum(fft, bcol)
    c_col = c_row[:N_HEADS].T
    c_row3 = c_row[:N_HEADS].reshape(N_HEADS, S // t, t)
    fox_o, fox_lse = _fox_fwd(fqkv, c_row3, tq=512, tk=t)
    biasm = _swa_bias(rel_bias_slot, bucket)
    sinks_slot = hooks.after_fox_fwd(fox_o, sinks_slot)
    swa_o, swa_lse = _swa_fwd(sqkv, biasm, sinks_slot)
    wout, w1, w2, wple, wg = hooks.after_attention(swa_o)
    wout_fox = wout[:D_ATT]
    wout_swa = _slot_order(wout[D_ATT:], 0)
    mix, h1, m = _post_attn(x, fox_o, swa_o, wout_fox, wout_swa, g2, g3, tm)
    u, y, h2 = _mlp_fwd(m, h1, w1, w2, g4, tm)
    dh2, dpe, dgl, dg5, loss = _ple_loss(h2, p, target, wg, wple, g5, tm)

    d_wple = _weight_grad(p, dpe, "grad_w_ple", tk=D_PLE, n_chunks=N_DEV)
    d_wg = _weight_grad(h2, dgl, "grad_w_ple_gate", tk=256)
    dh1, dy, du, dg4, dg3 = _mlp_bwd(dh2, y, h1, u, w1, w2, g4, g3, tm)
    d_w2 = _weight_grad(u, dy, "grad_w_ff2", tk=256, relu2=True)
    d_w1 = _weight_grad(m, du, "grad_w_ff1", tk=256, n_chunks=N_DEV)
    early = dict(w_ff1=d_w1, w_ff2=d_w2.reshape(N_DEV, FF_CHUNK, D_MODEL), w_ple=d_wple,
                 w_ple_gate=d_wg.reshape(N_DEV, D_MODEL // N_DEV, D_MODEL))
    g2_late = hooks.after_mlp_grads(early, g2)

    head = np.arange(D_ATT) // HEAD_DIM
    head_rows = jnp.asarray((head[None, :] == np.arange(N_HEADS)[:, None]).astype(np.float32))
    dmix, dcat, d_row, d_col, dg2 = _attn_out_bwd(dh1, mix, fox_o, swa_o, wout_fox, wout_swa, g2_late,
                                                  head_rows, head_rows.T, tm)
    d_wout_fox = _weight_grad(fox_o, dmix, "grad_w_out_fox", tk=256)
    d_wout_swa = _weight_grad(swa_o, dmix, "grad_w_out_swa", tk=256)

    lse_row3 = fox_lse.T.reshape(N_HEADS, S // t, t)
    d_row3 = hooks.after_attn_out_bwd(dmix, d_row.reshape(N_HEADS, S // t, t))
    dq_fox, dk_fox, dv_fox, dc_col, dcq = _fox_bwd(fqkv, dcat, lse_row3, d_row3, c_col, tq=t, tk=512)
    dsq, dsk, dsv, d_rb_slot, d_sink_slot = _swa_bwd(sqkv, dcat, biasm, sinks_slot, bucket, swa_lse, d_col)
    dc_row = jnp.pad(dc_col.T + dcq.reshape(N_HEADS, S), ((0, 8), (0, 0)))
    dff_t, db = _forget_bwd(dc_row, fft, bcol)
    grad_x, dz, dg1 = _pre_attn_bwd(x, dh1, dq_fox, dk_fox, dv_fox, dsq, dsk, dsv, dff_t, wqkv, wswa, wfft, g1, tm)
    d_wmain = _weight_grad(a, dz, "grad_w_in", tk=256)
    d_wff_t = _forget_weight_grad(dff_t, a)

    d_win = jnp.concatenate([
        d_wmain[:, :3 * D_ATT], d_wff_t[:N_HEADS].T.astype(MM), _head_order(d_wmain[:, 3 * D_ATT:4 * D_ATT], 1),
        d_wmain[:, 4 * D_ATT:]], axis=1)
    d_win = d_win.reshape(D_MODEL, N_DEV, D_IN // N_DEV).transpose(1, 0, 2).reshape((N_DEV,) + W_IN_FLAT)
    d_wout = jnp.concatenate([d_wout_fox, _head_order(d_wout_swa, 0)], axis=0).reshape(N_DEV, D_MODEL // N_DEV, D_MODEL)
    big = dict(early, w_in=d_win, w_out=d_wout)
    small = dict(b_forget=db[:N_HEADS].reshape(1, N_HEADS), rel_bias=d_rb_slot[:, np.array(HEAD_SLOT)],
                 swa_sinks=d_sink_slot[:, np.array(HEAD_SLOT)], g_attn_pre=dg1, g_attn_post=dg2, g_ff_pre=dg3,
                 g_ff_post=dg4, g_ple_post=dg5)
    return loss, grad_x, big, small, dz


BIG = ("w_in", "w_out", "w_ff1", "w_ff2", "w_ple", "w_ple_gate")
SMALL_ROWS = ("g_attn_pre", "g_attn_post", "g_ff_pre", "g_ff_post", "g_ple_post")
WEIGHTS = ("w_in", "b_forget", "w_out", "rel_bias", "swa_sinks", "g_attn_pre", "g_attn_post", "w_ff1", "w_ff2",
           "g_ff_pre", "g_ff_post", "w_ple", "w_ple_gate", "g_ple_post")


EARLY = ("w_ff1", "w_ff2", "w_ple", "w_ple_gate")
LATE = ("w_in", "w_out")


class _Overlap:
    def __init__(self, later):
        self.later = later

    def before_pre_attn(self, g1):
        self.gather_sems, self.later, token = _exchange_start("gather_rest_start", self.later, 4 * 5, _plan_gather_direct)
        return g1 + token

    def after_fox_fwd(self, fox_o, sinks_slot):
        later = _exchange_wait("gather_rest_wait", self.later, self.gather_sems, fox_o, _plan_gather_direct)
        self.pass_sems, self.later, token = _exchange_start("gather_pass_on_start", later, 3 * 5, _plan_gather_pass_on)
        return sinks_slot + token[0, :N_HEADS]

    def after_attention(self, swa_o):
        wout_g, w1_g, w2_g, wple_g, wg_g = _exchange_wait("gather_pass_on_wait", self.later, self.pass_sems, swa_o,
                                                         _plan_gather_pass_on)
        return (wout_g.reshape(D_MODEL, D_MODEL), w1_g, w2_g.reshape(D_FF, D_MODEL),
                jnp.moveaxis(wple_g, 0, 1).reshape(D_PLE, D_MODEL), wg_g.reshape(D_MODEL, D_MODEL))

    def after_mlp_grads(self, grads, g2):
        views = [grads[k].reshape((4, 2) + grads[k].shape[1:]) for k in EARLY]
        lands = [lax.empty((4,) + grads[k].shape[1:], MM) for k in EARLY]
        self.in_chip_sems, self.in_chip, token = _exchange_start("grads_in_chip_start", views + lands, len(EARLY),
                                                                 _plan_in_chip)
        return g2 + token

    def after_attn_out_bwd(self, dmix, d_row3):
        arrays = _exchange_wait("grads_in_chip_wait", self.in_chip, self.in_chip_sems, dmix, _plan_in_chip)
        n = len(EARLY)
        sums = [_chip_sum(arrays[a], arrays[n + a], "chip_sum_" + k) for a, k in enumerate(EARLY)]
        lands = [lax.empty(s.shape, s.dtype) for s in sums]
        self.between_sems, self.between, token = _exchange_start("grads_between_chips_start", sums + lands, 3 * n,
                                                                 _plan_between_chips)
        return d_row3 + token[0, 0]

    def finish(self, after):
        arrays = _exchange_wait("grads_between_chips_wait", self.between, self.between_sems, after,
                                _plan_between_chips)
        n = len(EARLY)
        self.sums = arrays[:n]
        return arrays[n:]


def _pack_small(t):
    rows = [t[k].reshape(1, D_MODEL) for k in SMALL_ROWS]
    misc = jnp.concatenate([t["b_forget"].reshape(-1), t["swa_sinks"].reshape(-1), t["rel_bias"].reshape(-1)])
    rows.append(jnp.pad(misc, (0, D_MODEL - misc.shape[0])).reshape(1, D_MODEL))
    rows.append(jnp.pad(t["loss"].reshape(-1), (0, D_MODEL - 1)).reshape(1, D_MODEL))
    rows.append(jnp.zeros((1, D_MODEL), F32))
    return jnp.concatenate(rows, axis=0).astype(F32)


def _unpack_small(blk):
    out = {k: blk[i].reshape(1, D_MODEL) for i, k in enumerate(SMALL_ROWS)}
    misc = blk[len(SMALL_ROWS)]
    out["b_forget"] = misc[:N_HEADS].reshape(1, N_HEADS)
    out["swa_sinks"] = misc[N_HEADS:2 * N_HEADS].reshape(1, N_HEADS)
    out["rel_bias"] = misc[2 * N_HEADS:2 * N_HEADS + N_BUCKETS * N_HEADS].reshape(N_BUCKETS, N_HEADS)
    out["loss"] = blk[len(SMALL_ROWS) + 1, 0]
    return out


def kernel(x, p, w_in, b_forget, w_out, rel_bias, swa_sinks, g_attn_pre, g_attn_post, w_ff1, w_ff2, g_ff_pre, g_ff_post, w_ple, w_ple_gate, g_ple_post, loss_target, m_w_in, m_b_forget, m_w_out, m_rel_bias, m_swa_sinks, m_g_attn_pre, m_g_attn_post, m_w_ff1, m_w_ff2, m_g_ff_pre, m_g_ff_post, m_w_ple, m_w_ple_gate, m_g_ple_post, v_w_in, v_b_forget, v_w_out, v_rel_bias, v_swa_sinks, v_g_attn_pre, v_g_attn_post, v_w_ff1, v_w_ff2, v_g_ff_pre, v_g_ff_post, v_w_ple, v_w_ple_gate, v_g_ple_post):
    w = dict(w_in=w_in, b_forget=b_forget, w_out=w_out, rel_bias=rel_bias, swa_sinks=swa_sinks,
             g_attn_pre=g_attn_pre, g_attn_post=g_attn_post, w_ff1=w_ff1, w_ff2=w_ff2, g_ff_pre=g_ff_pre,
             g_ff_post=g_ff_post, w_ple=w_ple, w_ple_gate=w_ple_gate, g_ple_post=g_ple_post)
    mom = dict(w_in=m_w_in, b_forget=m_b_forget, w_out=m_w_out, rel_bias=m_rel_bias, swa_sinks=m_swa_sinks,
               g_attn_pre=m_g_attn_pre, g_attn_post=m_g_attn_post, w_ff1=m_w_ff1, w_ff2=m_w_ff2,
               g_ff_pre=m_g_ff_pre, g_ff_post=m_g_ff_post, w_ple=m_w_ple, w_ple_gate=m_w_ple_gate,
               g_ple_post=m_g_ple_post)
    var = dict(w_in=v_w_in, b_forget=v_b_forget, w_out=v_w_out, rel_bias=v_rel_bias, swa_sinks=v_swa_sinks,
               g_attn_pre=v_g_attn_pre, g_attn_post=v_g_attn_post, w_ff1=v_w_ff1, w_ff2=v_w_ff2,
               g_ff_pre=v_g_ff_pre, g_ff_post=v_g_ff_post, w_ple=v_w_ple, w_ple_gate=v_w_ple_gate,
               g_ple_post=v_g_ple_post)

    flat = lambda t, k: t.reshape(W_IN_FLAT) if k == "w_in" else t
    win_g, *later = _all_gather([flat(w[k][0].astype(MM), k) for k in BIG], n_full=1)
    win = jnp.moveaxis(win_g.reshape(N_DEV, D_MODEL, D_IN // N_DEV), 0, 1).reshape(D_MODEL, D_IN)
    hooks = _Overlap(later)
    loss, grad_x, big, small, last = _forward_backward(
        x[0], p[0, 0], loss_target[0], win, hooks, b_forget, rel_bias, swa_sinks, g_attn_pre, g_attn_post,
        g_ff_pre, g_ff_post, g_ple_post)
    early_parts = hooks.finish(last)

    grads = [big[k] for k in LATE]
    views = [g.reshape((4, 2) + g.shape[1:]) for g in grads]
    others = _exchange_in_chip(grads)
    sums = [_chip_sum(v_, o, "chip_sum_" + k) for v_, o, k in zip(views, others, LATE)]
    small["loss"] = loss
    *parts, small_all = _exchange_between_chips(sums, _pack_small(small))

    out_g, out_d, out_m, out_v = {}, {}, {}, {}
    for k, part in zip(LATE, parts):
        shape = w[k].shape
        g, d, m_new, v_new = _adamw(part, flat(w[k][0], k), flat(mom[k][0], k), flat(var[k][0], k), "adamw_" + k)
        out_g[k], out_d[k], out_m[k], out_v[k] = g.reshape(shape), d.reshape(shape), m_new.reshape(shape), v_new.reshape(shape)
    for k, part, own in zip(EARLY, early_parts, hooks.sums):
        g, d, m_new, v_new = _adamw_chips(part, own, w[k][0], mom[k][0], var[k][0], "adamw_" + k)
        out_g[k], out_d[k], out_m[k], out_v[k] = g[None], d[None], m_new[None], v_new[None]
    rep = {k: w[k] for k in w if k not in BIG}
    rep["loss"] = jnp.zeros((), F32)
    rep_m = {k: mom[k] for k in mom if k not in BIG}
    rep_m["loss"] = jnp.zeros((), F32)
    rep_v = {k: var[k] for k in var if k not in BIG}
    rep_v["loss"] = jnp.ones((), F32)
    g_s, d_s, m_s, v_s = _adamw(small_all, _pack_small(rep), _pack_small(rep_m), _pack_small(rep_v), "adamw_small")
    g_s, d_s, m_s, v_s = _unpack_small(g_s), _unpack_small(d_s), _unpack_small(m_s), _unpack_small(v_s)
    for k in w:
        if k not in BIG:
            out_g[k], out_d[k], out_m[k], out_v[k] = g_s[k], d_s[k], m_s[k], v_s[k]
    return (g_s["loss"], grad_x[None], *[out_g[k] for k in WEIGHTS], *[out_d[k] for k in WEIGHTS],
            *[out_m[k] for k in WEIGHTS], *[out_v[k] for k in WEIGHTS])
```

```python
import functools

import numpy as np
import jax
import jax.numpy as jnp
from jax import lax
from jax.experimental import pallas as pl
from jax.experimental.pallas import tpu as pltpu

F32 = jnp.float32
MM = jnp.bfloat16

D_MODEL = 1024
HEAD_DIM = 64
N_HEADS = 8
D_ATT = N_HEADS * HEAD_DIM
D_KV = 128
D_FF = 4096
D_PLE = 256
D_IN = 3 * D_ATT + N_HEADS + D_ATT + 2 * D_KV
N_DEV = 8
W_IN_FLAT = (D_MODEL * D_IN // N_DEV // 128, 128)
FF_CHUNK = D_FF // N_DEV
WINDOW = 128
N_BUCKETS = 32
MAX_DISTANCE = 128
RMS_EPS = 1e-6
Q_SCALE = HEAD_DIM ** -0.5
NEG = -1e30

ADAM_LR = 0.001
ADAM_B1 = 0.9
ADAM_B2 = 0.999
ADAM_EPS = 1e-08
ADAM_WD = 0.01
ADAM_STEP = 10

SLOT_HEAD = (0, 4, 1, 5, 2, 6, 3, 7)
HEAD_SLOT = (0, 2, 4, 6, 1, 3, 5, 7)

VMEM_LIMIT = 56 * 1024 * 1024
MESH = pl.DeviceIdType.MESH

NT = (((1,), (1,)), ((), ()))
TN = (((0,), (0,)), ((), ()))


def _params(*semantics):
    return pltpu.CompilerParams(dimension_semantics=semantics, vmem_limit_bytes=VMEM_LIMIT)


def _resident():
    return pl.BlockSpec(memory_space=pltpu.VMEM)


def _rows(tm, width):
    return pl.BlockSpec((tm, width), lambda i: (i, 0))


def _const(shape):
    return pl.BlockSpec(shape, lambda i: (0,) * len(shape))


def _dot(a, b):
    return jnp.dot(a, b, preferred_element_type=F32)


def _dot_nt(a, b):
    return lax.dot_general(a, b, NT, preferred_element_type=F32)


def _dot_tn(a, b):
    return lax.dot_general(a, b, TN, preferred_element_type=F32)


def _rms(xf):
    r = lax.rsqrt(jnp.mean(xf * xf, axis=-1, keepdims=True) + RMS_EPS)
    return xf * r, r


def _rms_bwd(dout, n, r, g):
    dg = jnp.sum(dout * n, axis=0, keepdims=True)
    dn = dout * g
    dx = r * (dn - n * jnp.mean(dn * n, axis=-1, keepdims=True))
    return dx, dg


def _accumulate(ref, value, step):
    @pl.when(step == 0)
    def _():
        ref[...] = value

    @pl.when(step != 0)
    def _():
        ref[...] += value


def _t5_bucket(n):
    max_exact = N_BUCKETS // 2
    large = max_exact + (np.log(np.maximum(n, 1) / max_exact) / np.log(MAX_DISTANCE / max_exact)
                         * (N_BUCKETS - max_exact)).astype(np.int32)
    large = np.minimum(large, N_BUCKETS - 1)
    return np.where(n < max_exact, n, large).astype(np.int32)


def _swa_bucket_map():
    i = np.arange(WINDOW)[:, None]
    j = np.arange(2 * WINDOW)[None, :]
    dist = i + WINDOW - j
    ok = (dist >= 0) & (dist < WINDOW)
    return np.where(ok, _t5_bucket(np.clip(dist, 0, None)), -1).astype(np.int32)


def _pre_attn(x, g1, wqkv, wswa, wfft, tm):
    S = x.shape[0]

    def body(x_ref, g_ref, wqkv_ref, wswa_ref, wff_ref, a_ref, fqkv_ref, sqkv_ref, fft_ref):
        n, _ = _rms(x_ref[...])
        a = (n * g_ref[...]).astype(MM)
        a_ref[...] = a
        fqkv_ref[:, :D_ATT] = (_dot(a, wqkv_ref[:, :D_ATT]) * Q_SCALE).astype(MM)
        fqkv_ref[:, D_ATT:] = _dot(a, wqkv_ref[:, D_ATT:]).astype(MM)
        sqkv_ref[:, :D_ATT] = (_dot(a, wswa_ref[:, :D_ATT]) * Q_SCALE).astype(MM)
        sqkv_ref[:, D_ATT:] = _dot(a, wswa_ref[:, D_ATT:]).astype(MM)
        fft_ref[...] = _dot_nt(wff_ref[...], a)

    return pl.pallas_call(
        body, name="pre_attn", grid=(S // tm,),
        in_specs=[_rows(tm, D_MODEL), _const((1, D_MODEL)), _resident(), _resident(), _resident()],
        out_specs=[_rows(tm, D_MODEL), _rows(tm, 3 * D_ATT), _rows(tm, D_ATT + 2 * D_KV),
                   pl.BlockSpec((16, tm), lambda i: (0, i))],
        out_shape=[jax.ShapeDtypeStruct((S, D_MODEL), MM), jax.ShapeDtypeStruct((S, 3 * D_ATT), MM),
                   jax.ShapeDtypeStruct((S, D_ATT + 2 * D_KV), MM), jax.ShapeDtypeStruct((16, S), F32)],
        compiler_params=_params("parallel"),
    )(x, g1, wqkv, wswa, wfft)


def _lane_scan(v, reverse):
    S = v.shape[1]
    lane = lax.broadcasted_iota(jnp.int32, v.shape, 1)
    k = 1
    while k < S:
        if reverse:
            v = v + jnp.where(lane < S - k, pltpu.roll(v, S - k, axis=1), 0.0)
        else:
            v = v + jnp.where(lane >= k, pltpu.roll(v, k, axis=1), 0.0)
        k *= 2
    return v


def _forget_cumsum(fft, bcol):
    def body(f_ref, b_ref, c_ref):
        z = f_ref[...] + b_ref[...]
        log_f = jnp.minimum(z, 0.0) - jnp.log1p(jnp.exp(-jnp.abs(z)))
        c_ref[...] = _lane_scan(log_f, reverse=False)

    return pl.pallas_call(
        body, name="forget_cumsum", out_shape=jax.ShapeDtypeStruct(fft.shape, F32),
        in_specs=[_resident(), _resident()], out_specs=_resident(),
    )(fft, bcol)


def _forget_bwd(dc_row, fft, bcol):
    def body(dc_ref, f_ref, b_ref, dff_ref, db_ref):
        z = f_ref[...] + b_ref[...]
        dlog_f = _lane_scan(dc_ref[...], reverse=True)
        dff = dlog_f * (1.0 / (1.0 + jnp.exp(z)))
        dff_ref[...] = dff
        db_ref[...] = jnp.sum(dff, axis=1, keepdims=True)

    return pl.pallas_call(
        body, name="forget_bwd",
        out_shape=[jax.ShapeDtypeStruct(fft.shape, F32), jax.ShapeDtypeStruct((fft.shape[0], 1), F32)],
        in_specs=[_resident()] * 3, out_specs=[_resident()] * 2,
    )(dc_row, fft, bcol)


def _head_select(shape, upper):
    lane = lax.broadcasted_iota(jnp.int32, shape, 1)
    return lane >= HEAD_DIM if upper else lane < HEAD_DIM


def _fox_fwd(fqkv, c_row3, tq, tk, pairs_per_loop=2, row_chunks=1):
    S = fqkv.shape[0]
    rq = tq // row_chunks
    n_band = tq // tk

    def body(q_ref, k_ref, v_ref, ck_ref, o_ref, lse_ref):
        qi = pl.program_id(0)
        row = lax.broadcasted_iota(jnp.int32, (rq, tk), 0)
        col = lax.broadcasted_iota(jnp.int32, (rq, tk), 1)
        low = _head_select((rq, 128), 0)
        for first in range(0, N_HEADS // 2, pairs_per_loop):
            pairs = range(first, first + pairs_per_loop)
            chains = [(pr, hh, rc) for pr in pairs for hh in range(2) for rc in range(row_chunks)]
            qh = {}
            for pr in pairs:
                for rc in range(row_chunks):
                    q2 = q_ref[rc * rq:(rc + 1) * rq, pr * 128:(pr + 1) * 128]
                    qh[pr, 0, rc] = jnp.where(low, q2, jnp.zeros_like(q2))
                    qh[pr, 1, rc] = jnp.where(low, jnp.zeros_like(q2), q2)

            def block(kb, carry, band, chains=chains, qh=qh):
                rows = pl.ds(pl.multiple_of(kb * tk, tk), tk)
                out = []
                for (pr, hh, rc), (m, l, acc) in zip(chains, carry):
                    if band is not None and (rc + 1) * rq <= band * tk:
                        out.append((m, l, acc))
                        continue
                    lanes = slice(pr * 128, (pr + 1) * 128)
                    s = _dot_nt(qh[pr, hh, rc], k_ref[rows, lanes]) - ck_ref[2 * pr + hh, pl.ds(kb, 1), :]
                    if band is not None:
                        s = jnp.where(row + rc * rq >= col + band * tk, s, NEG)
                    m_new = jnp.maximum(m, jnp.max(s, axis=-1, keepdims=True))
                    p = jnp.exp(s - m_new)
                    alpha = jnp.exp(m - m_new)
                    l = alpha * l + jnp.sum(p, axis=-1, keepdims=True)
                    acc = alpha * acc + _dot(p.astype(MM), v_ref[rows, lanes])
                    out.append((m_new, l, acc))
                return tuple(out)

            carry = tuple((jnp.full((rq, 1), NEG, F32), jnp.zeros((rq, 1), F32), jnp.zeros((rq, 128), F32))
                          for _ in chains)
            carry = lax.fori_loop(0, qi * n_band, functools.partial(block, band=None), carry)
            for band in range(n_band):
                carry = block(qi * n_band + band, carry, band=band)
            res = {}
            for (pr, hh, rc), (m, l, acc) in zip(chains, carry):
                res[pr, hh, rc] = acc / l
                lse_ref[rc * rq:(rc + 1) * rq, 2 * pr + hh:2 * pr + hh + 1] = m + jnp.log(l)
            for pr in pairs:
                for rc in range(row_chunks):
                    o_ref[rc * rq:(rc + 1) * rq, pr * 128:(pr + 1) * 128] = jnp.where(
                        low, res[pr, 0, rc], res[pr, 1, rc]).astype(MM)

    return pl.pallas_call(
        body, name="fox_fwd", grid=(S // tq,),
        in_specs=[pl.BlockSpec((tq, D_ATT), lambda i: (i, 0)), pl.BlockSpec((S, D_ATT), lambda i: (0, 1)),
                  pl.BlockSpec((S, D_ATT), lambda i: (0, 2)), _resident()],
        out_specs=[_rows(tq, D_ATT), _rows(tq, N_HEADS)],
        out_shape=[jax.ShapeDtypeStruct((S, D_ATT), MM), jax.ShapeDtypeStruct((S, N_HEADS), F32)],
        compiler_params=_params("parallel"),
    )(fqkv, fqkv, fqkv, c_row3)


def _swa_bias(rel_bias_slot, bucket):
    def body(rb_ref, bk_ref, out_ref):
        bk = bk_ref[...]
        for s in range(N_HEADS):
            acc = jnp.where(bk < 0, NEG, 0.0).astype(F32)
            for b in range(N_BUCKETS):
                acc = jnp.where(bk == b, rb_ref[b, s], acc)
            out_ref[s] = acc

    return pl.pallas_call(
        body, name="swa_bias", out_shape=jax.ShapeDtypeStruct((N_HEADS, WINDOW, 2 * WINDOW), F32),
        in_specs=[pl.BlockSpec(memory_space=pltpu.SMEM), _resident()], out_specs=_resident(),
    )(rel_bias_slot, bucket)


def _swa_specs(S):
    q = pl.BlockSpec((WINDOW, D_ATT), lambda n: (n, 0))
    kp = pl.BlockSpec((WINDOW, D_KV), lambda n: (jnp.maximum(n - 1, 0), 4))
    kc = pl.BlockSpec((WINDOW, D_KV), lambda n: (n, 4))
    vp = pl.BlockSpec((WINDOW, D_KV), lambda n: (jnp.maximum(n - 1, 0), 5))
    vc = pl.BlockSpec((WINDOW, D_KV), lambda n: (n, 5))
    return [q, kp, kc, vp, vc]


def _swa_fwd(sqkv, biasm, sinks_slot):
    S = sqkv.shape[0]

    def body(q_ref, kp_ref, kc_ref, vp_ref, vc_ref, bias_ref, sink_ref, o_ref, lse_ref):
        n = pl.program_id(0)
        no_prev = jnp.where(n > 0, 0.0, NEG)
        for j in range(N_HEADS // 2):
            lanes = slice(j * 128, (j + 1) * 128)
            q2 = q_ref[:, lanes]
            res = []
            for hh in range(2):
                s_ = 2 * j + hh
                qh = jnp.where(_head_select((WINDOW, 128), hh), q2, jnp.zeros_like(q2))
                sp = _dot_nt(qh, kp_ref[...]) + bias_ref[s_, :, :WINDOW] + no_prev
                sc = _dot_nt(qh, kc_ref[...]) + bias_ref[s_, :, WINDOW:]
                sink = sink_ref[s_]
                m = jnp.maximum(jnp.maximum(jnp.max(sp, axis=-1, keepdims=True),
                                            jnp.max(sc, axis=-1, keepdims=True)), sink)
                ep = jnp.exp(sp - m)
                ec = jnp.exp(sc - m)
                den = (jnp.sum(ep, axis=-1, keepdims=True) + jnp.sum(ec, axis=-1, keepdims=True)
                       + jnp.exp(sink - m))
                res.append((_dot(ep.astype(MM), vp_ref[...]) + _dot(ec.astype(MM), vc_ref[...])) / den)
                lse_ref[:, s_:s_ + 1] = m + jnp.log(den)
            o_ref[:, lanes] = jnp.where(_head_select((WINDOW, 128), 0), res[0], res[1]).astype(MM)

    return pl.pallas_call(
        body, name="swa_fwd", grid=(S // WINDOW,),
        in_specs=_swa_specs(S) + [_resident(), pl.BlockSpec(memory_space=pltpu.SMEM)],
        out_specs=[_rows(WINDOW, D_ATT), _rows(WINDOW, N_HEADS)],
        out_shape=[jax.ShapeDtypeStruct((S, D_ATT), MM), jax.ShapeDtypeStruct((S, N_HEADS), F32)],
        compiler_params=_params("parallel"),
    )(sqkv, sqkv, sqkv, sqkv, sqkv, biasm, sinks_slot)


def _post_attn(x, fox_o, swa_o, wout_fox, wout_swa, g2, g3, tm):
    S = x.shape[0]

    def body(x_ref, fo_ref, so_ref, wf_ref, ws_ref, g2_ref, g3_ref, mix_ref, h1_ref, m_ref):
        mix = _dot(fo_ref[...], wf_ref[...]) + _dot(so_ref[...], ws_ref[...])
        mix_ref[...] = mix
        n2, _ = _rms(mix)
        h1 = x_ref[...] + n2 * g2_ref[...]
        h1_ref[...] = h1
        n3, _ = _rms(h1)
        m_ref[...] = (n3 * g3_ref[...]).astype(MM)

    return pl.pallas_call(
        body, name="post_attn", grid=(S // tm,),
        in_specs=[_rows(tm, D_MODEL), _rows(tm, D_ATT), _rows(tm, D_ATT), _resident(), _resident(),
                  _const((1, D_MODEL)), _const((1, D_MODEL))],
        out_specs=[_rows(tm, D_MODEL)] * 3,
        out_shape=[jax.ShapeDtypeStruct((S, D_MODEL), F32), jax.ShapeDtypeStruct((S, D_MODEL), F32),
                   jax.ShapeDtypeStruct((S, D_MODEL), MM)],
        compiler_params=_params("parallel"),
    )(x, fox_o, swa_o, wout_fox, wout_swa, g2, g3)


def _mlp_fwd(m, h1, w1, w2, g4, tm):
    S = m.shape[0]

    def body(m_ref, h1_ref, w1_ref, w2_ref, g4_ref, u_ref, y_ref, h2_ref):
        mb = m_ref[...]
        y = jnp.zeros((tm, D_MODEL), F32)
        for j in range(N_DEV):
            cols = slice(j * FF_CHUNK, (j + 1) * FF_CHUNK)
            u = _dot(mb, w1_ref[j])
            u_ref[:, cols] = u.astype(MM)
            y = y + _dot(jnp.square(jnp.maximum(u, 0.0)).astype(MM), w2_ref[cols, :])
        y_ref[...] = y
        n4, _ = _rms(y)
        h2_ref[...] = h1_ref[...] + n4 * g4_ref[...]

    return pl.pallas_call(
        body, name="mlp_fwd", grid=(S // tm,),
        in_specs=[_rows(tm, D_MODEL), _rows(tm, D_MODEL), _resident(), _resident(), _const((1, D_MODEL))],
        out_specs=[_rows(tm, D_FF), _rows(tm, D_MODEL), _rows(tm, D_MODEL)],
        out_shape=[jax.ShapeDtypeStruct((S, D_FF), MM), jax.ShapeDtypeStruct((S, D_MODEL), F32),
                   jax.ShapeDtypeStruct((S, D_MODEL), F32)],
        compiler_params=_params("parallel"),
    )(m, h1, w1, w2, g4)


def _ple_loss(h2, p, target, wg, wple, g5, tm):
    S = h2.shape[0]

    def body(h2_ref, p_ref, t_ref, wg_ref, wp_ref, g5_ref, dh2_ref, dpe_ref, dgl_ref, dg5_ref, loss_ref):
        i = pl.program_id(0)
        h2 = h2_ref[...]
        gate = jax.nn.sigmoid(_dot(h2.astype(MM), wg_ref[...]))
        pe = _dot(p_ref[...].astype(MM), wp_ref[...])
        n5, r5 = _rms(pe * gate)
        g5 = g5_ref[...]
        diff = h2 + n5 * g5 - t_ref[...]
        per_token = jnp.mean(jnp.square(diff), axis=-1, keepdims=True)
        _accumulate(loss_ref, 0.5 * jnp.sum(per_token, axis=0, keepdims=True), i)
        dh3 = diff * (1.0 / D_MODEL)
        de, dg5 = _rms_bwd(dh3, n5, r5, g5)
        _accumulate(dg5_ref, dg5, i)
        dpe_ref[...] = (de * gate).astype(MM)
        dgl = (de * pe * gate * (1.0 - gate)).astype(MM)
        dgl_ref[...] = dgl
        dh2_ref[...] = dh3 + _dot_nt(dgl, wg_ref[...])

    return pl.pallas_call(
        body, name="ple_loss", grid=(S // tm,),
        in_specs=[_rows(tm, D_MODEL), _rows(tm, D_PLE), _rows(tm, D_MODEL), _resident(), _resident(),
                  _const((1, D_MODEL))],
        out_specs=[_rows(tm, D_MODEL), _rows(tm, D_MODEL), _rows(tm, D_MODEL), _const((1, D_MODEL)), _const((1, 1))],
        out_shape=[jax.ShapeDtypeStruct((S, D_MODEL), F32), jax.ShapeDtypeStruct((S, D_MODEL), MM),
                   jax.ShapeDtypeStruct((S, D_MODEL), MM), jax.ShapeDtypeStruct((1, D_MODEL), F32),
                   jax.ShapeDtypeStruct((1, 1), F32)],
        compiler_params=_params("arbitrary"),
    )(h2, p, target, wg, wple, g5)


def _mlp_bwd(dh2, y, h1, u, w1, w2, g4, g3, tm):
    S = dh2.shape[0]

    def body(dh2_ref, y_ref, h1_ref, u_ref, w1_ref, w2_ref, g4_ref, g3_ref,
             dh1_ref, dy_ref, du_ref, dg4_ref, dg3_ref):
        i = pl.program_id(0)
        dh2 = dh2_ref[...]
        n4, r4 = _rms(y_ref[...])
        dy, dg4 = _rms_bwd(dh2, n4, r4, g4_ref[...])
        _accumulate(dg4_ref, dg4, i)
        dyb = dy.astype(MM)
        dy_ref[...] = dyb
        dm = jnp.zeros((tm, D_MODEL), F32)
        for j in range(N_DEV):
            cols = slice(j * FF_CHUNK, (j + 1) * FF_CHUNK)
            dact = _dot_nt(dyb, w2_ref[cols, :])
            du = (dact * (2.0 * jnp.maximum(u_ref[:, cols].astype(F32), 0.0))).astype(MM)
            du_ref[:, cols] = du
            dm = dm + _dot_nt(du, w1_ref[j])
        n3, r3 = _rms(h1_ref[...])
        dx, dg3 = _rms_bwd(dm, n3, r3, g3_ref[...])
        _accumulate(dg3_ref, dg3, i)
        dh1_ref[...] = dh2 + dx

    return pl.pallas_call(
        body, name="mlp_bwd", grid=(S // tm,),
        in_specs=[_rows(tm, D_MODEL), _rows(tm, D_MODEL), _rows(tm, D_MODEL), _rows(tm, D_FF),
                  _resident(), _resident(), _const((1, D_MODEL)), _const((1, D_MODEL))],
        out_specs=[_rows(tm, D_MODEL), _rows(tm, D_MODEL), _rows(tm, D_FF), _const((1, D_MODEL)),
                   _const((1, D_MODEL))],
        out_shape=[jax.ShapeDtypeStruct((S, D_MODEL), F32), jax.ShapeDtypeStruct((S, D_MODEL), MM),
                   jax.ShapeDtypeStruct((S, D_FF), MM), jax.ShapeDtypeStruct((1, D_MODEL), F32),
                   jax.ShapeDtypeStruct((1, D_MODEL), F32)],
        compiler_params=_params("arbitrary"),
    )(dh2, y, h1, u, w1, w2, g4, g3)


def _attn_out_bwd(dh1, mix, fox_o, swa_o, wout_fox, wout_swa, g2, head_rows, head_cols, tm):
    S = dh1.shape[0]

    def body(dh1_ref, mix_ref, fo_ref, so_ref, wf_ref, ws_ref, g2_ref, er_ref, ec_ref,
             dmix_ref, dcat_ref, drow_ref, dcol_ref, dg2_ref):
        i = pl.program_id(0)
        n2, r2 = _rms(mix_ref[...])
        dmix, dg2 = _rms_bwd(dh1_ref[...], n2, r2, g2_ref[...])
        _accumulate(dg2_ref, dg2, i)
        dmb = dmix.astype(MM)
        dmix_ref[...] = dmb
        dfo = _dot_nt(dmb, wf_ref[...]).astype(MM)
        dso = _dot_nt(dmb, ws_ref[...]).astype(MM)
        dcat_ref[:, :D_ATT] = dfo
        dcat_ref[:, D_ATT:] = dso
        hi = lax.Precision.HIGHEST
        prod_f = dfo.astype(F32) * fo_ref[...].astype(F32)
        prod_s = dso.astype(F32) * so_ref[...].astype(F32)
        drow_ref[...] = lax.dot_general(er_ref[...], prod_f, NT, precision=hi, preferred_element_type=F32)
        dcol_ref[...] = jnp.dot(prod_s, ec_ref[...], precision=hi, preferred_element_type=F32)

    return pl.pallas_call(
        body, name="attn_out_bwd", grid=(S // tm,),
        in_specs=[_rows(tm, D_MODEL), _rows(tm, D_MODEL), _rows(tm, D_ATT), _rows(tm, D_ATT), _resident(),
                  _resident(), _const((1, D_MODEL)), _resident(), _resident()],
        out_specs=[_rows(tm, D_MODEL), _rows(tm, D_MODEL), pl.BlockSpec((N_HEADS, tm), lambda i: (0, i)),
                   _rows(tm, N_HEADS), _const((1, D_MODEL))],
        out_shape=[jax.ShapeDtypeStruct((S, D_MODEL), MM), jax.ShapeDtypeStruct((S, D_MODEL), MM),
                   jax.ShapeDtypeStruct((N_HEADS, S), F32), jax.ShapeDtypeStruct((S, N_HEADS), F32),
                   jax.ShapeDtypeStruct((1, D_MODEL), F32)],
        compiler_params=_params("arbitrary"),
    )(dh1, mix, fox_o, swa_o, wout_fox, wout_swa, g2, head_rows, head_cols)


def _fox_bwd(fqkv, dcat, lse_row3, d_row3, c_col, tq, tk, pairs_per_loop=2):
    S = fqkv.shape[0]
    n_blk = S // tk
    n_qblk = S // tq
    n_band = tk // tq

    def body(q_ref, k_ref, v_ref, do_ref, lse_ref, dd_ref, ck_ref, dq_ref, dk_ref, dv_ref, dc_ref, dcq_ref):
        kb = pl.program_id(0)

        @pl.when(kb == 0)
        def _():
            dq_ref[...] = jnp.zeros_like(dq_ref)
            dcq_ref[...] = jnp.zeros_like(dcq_ref)

        key = lax.broadcasted_iota(jnp.int32, (tk, tq), 0)
        qry = lax.broadcasted_iota(jnp.int32, (tk, tq), 1)
        low = _head_select((tk, 128), 0)
        for first in range(0, N_HEADS // 2, pairs_per_loop):
            pairs = range(first, first + pairs_per_loop)
            heads = [(pr, hh) for pr in pairs for hh in range(2)]
            kh, vh, ck = {}, {}, {}
            for pr in pairs:
                k2 = k_ref[:, pr * 128:(pr + 1) * 128]
                v2 = v_ref[:, pr * 128:(pr + 1) * 128]
                zero = jnp.zeros_like(k2)
                kh[pr, 0], kh[pr, 1] = jnp.where(low, k2, zero), jnp.where(low, zero, k2)
                vh[pr, 0], vh[pr, 1] = jnp.where(low, v2, zero), jnp.where(low, zero, v2)
                for hh in range(2):
                    ck[pr, hh] = ck_ref[:, 2 * pr + hh:2 * pr + hh + 1]

            def block(qb, carry, band, pairs=pairs, kh=kh, vh=vh, ck=ck):
                rows = pl.ds(pl.multiple_of(qb * tq, tq), tq)
                out = []
                it = iter(carry)
                for pr in pairs:
                    lanes = slice(pr * 128, (pr + 1) * 128)
                    q2 = q_ref[rows, lanes]
                    do2 = do_ref[rows, lanes]
                    dq = None
                    for hh in range(2):
                        h = 2 * pr + hh
                        dk, dv, dc = next(it)
                        s_t = _dot_nt(kh[pr, hh], q2) - ck[pr, hh]
                        p_t = jnp.exp(s_t - lse_ref[h, pl.ds(qb, 1), :])
                        if band is not None:
                            p_t = jnp.where(qry + band * tq >= key, p_t, 0.0)
                        ds_t = p_t * (_dot_nt(vh[pr, hh], do2) - dd_ref[h, pl.ds(qb, 1), :])
                        dsb = ds_t.astype(MM)
                        dv = dv + _dot(p_t.astype(MM), do2)
                        dk = dk + _dot(dsb, q2)
                        dc = dc - jnp.sum(ds_t, axis=1, keepdims=True)
                        part = _dot_tn(dsb, kh[pr, hh])
                        dq = part if dq is None else dq + part
                        dcq_ref[h, pl.ds(qb, 1), :] += jnp.sum(ds_t, axis=0, keepdims=True)
                        out.append((dk, dv, dc))
                    dq_ref[rows, lanes] += dq
                return tuple(out)

            carry = tuple((jnp.zeros((tk, 128), F32), jnp.zeros((tk, 128), F32), jnp.zeros((tk, 1), F32))
                          for _ in heads)
            for band in range(n_band):
                carry = block(kb * n_band + band, carry, band=band)
            carry = lax.fori_loop((kb + 1) * n_band, n_qblk, functools.partial(block, band=None), carry)
            grads = dict(zip(heads, carry))
            for pr in pairs:
                lanes = slice(pr * 128, (pr + 1) * 128)
                dk_ref[:, lanes] = jnp.where(low, grads[pr, 0][0], grads[pr, 1][0]).astype(MM)
                dv_ref[:, lanes] = jnp.where(low, grads[pr, 0][1], grads[pr, 1][1]).astype(MM)
                for hh in range(2):
                    dc_ref[:, 2 * pr + hh:2 * pr + hh + 1] = grads[pr, hh][2]

        @pl.when(kb == n_blk - 1)
        def _():
            dq_ref[...] = dq_ref[...] * Q_SCALE

    return pl.pallas_call(
        body, name="fox_bwd", grid=(n_blk,),
        in_specs=[pl.BlockSpec((S, D_ATT), lambda i: (0, 0)), pl.BlockSpec((tk, D_ATT), lambda i: (i, 1)),
                  pl.BlockSpec((tk, D_ATT), lambda i: (i, 2)), pl.BlockSpec((S, D_ATT), lambda i: (0, 0)),
                  _resident(), _resident(), _rows(tk, N_HEADS)],
        out_specs=[_const((S, D_ATT)), _rows(tk, D_ATT), _rows(tk, D_ATT), _rows(tk, N_HEADS),
                   _const((N_HEADS, n_qblk, tq))],
        out_shape=[jax.ShapeDtypeStruct((S, D_ATT), F32), jax.ShapeDtypeStruct((S, D_ATT), MM),
                   jax.ShapeDtypeStruct((S, D_ATT), MM), jax.ShapeDtypeStruct((S, N_HEADS), F32),
                   jax.ShapeDtypeStruct((N_HEADS, n_qblk, tq), F32)],
        compiler_params=_params("arbitrary"),
    )(fqkv, fqkv, fqkv, dcat, lse_row3, d_row3, c_col)


def _swa_bwd(sqkv, dcat, biasm, sinks_slot, bucket, lse, d_col):
    S = sqkv.shape[0]
    n_blk = S // WINDOW

    def body(q_ref, kp_ref, kc_ref, vp_ref, vc_ref, do_ref, bias_ref, sink_ref, bk_ref, lse_ref, dd_ref,
             dq_ref, dk_ref, dv_ref, drb_ref, dsink_ref, ds_acc):
        n = pl.program_id(0)

        @pl.when(n == 0)
        def _():
            dk_ref[...] = jnp.zeros_like(dk_ref)
            dv_ref[...] = jnp.zeros_like(dv_ref)
            ds_acc[...] = jnp.zeros_like(ds_acc)
            dsink_ref[...] = jnp.zeros_like(dsink_ref)

        no_prev = jnp.where(n > 0, 0.0, NEG)
        prev = pl.ds(pl.multiple_of(jnp.maximum(n - 1, 0) * WINDOW, WINDOW), WINDOW)
        cur = pl.ds(pl.multiple_of(n * WINDOW, WINDOW), WINDOW)
        lane8 = lax.broadcasted_iota(jnp.int32, (1, N_HEADS), 1)
        dkp = jnp.zeros((WINDOW, D_KV), F32)
        dkc = jnp.zeros((WINDOW, D_KV), F32)
        dvp = jnp.zeros((WINDOW, D_KV), F32)
        dvc = jnp.zeros((WINDOW, D_KV), F32)
        dsink = jnp.zeros((1, N_HEADS), F32)
        for j in range(N_HEADS // 2):
            lanes = slice(j * 128, (j + 1) * 128)
            q2 = q_ref[:, lanes]
            do2 = do_ref[:, lanes]
            dqs = []
            for hh in range(2):
                s_ = 2 * j + hh
                sel = _head_select((WINDOW, 128), hh)
                qh = jnp.where(sel, q2, jnp.zeros_like(q2))
                doh = jnp.where(sel, do2, jnp.zeros_like(do2))
                lse_h = lse_ref[:, s_:s_ + 1]
                dd = dd_ref[:, s_:s_ + 1]
                pp = jnp.exp(_dot_nt(qh, kp_ref[...]) + bias_ref[s_, :, :WINDOW] + no_prev - lse_h)
                pc = jnp.exp(_dot_nt(qh, kc_ref[...]) + bias_ref[s_, :, WINDOW:] - lse_h)
                p_sink = jnp.exp(sink_ref[s_] - lse_h)
                dsp = pp * (_dot_nt(doh, vp_ref[...]) - dd)
                dsc = pc * (_dot_nt(doh, vc_ref[...]) - dd)
                dsink = dsink + jnp.where(lane8 == s_, -jnp.sum(p_sink * dd), 0.0)
                ds_acc[s_, :, :WINDOW] += dsp
                ds_acc[s_, :, WINDOW:] += dsc
                dspb, dscb = dsp.astype(MM), dsc.astype(MM)
                dqs.append(_dot(dspb, kp_ref[...]) + _dot(dscb, kc_ref[...]))
                dkp = dkp + _dot_tn(dspb, qh)
                dkc = dkc + _dot_tn(dscb, qh)
                dvp = dvp + _dot_tn(pp.astype(MM), doh)
                dvc = dvc + _dot_tn(pc.astype(MM), doh)
            dq_ref[:, lanes] = (jnp.where(_head_select((WINDOW, 128), 0), dqs[0], dqs[1]) * Q_SCALE).astype(MM)
        dk_ref[prev, :] += dkp
        dk_ref[cur, :] += dkc
        dv_ref[prev, :] += dvp
        dv_ref[cur, :] += dvc
        dsink_ref[...] += dsink

        @pl.when(n == n_blk - 1)
        def _():
            bk = bk_ref[...]
            rb = lax.broadcasted_iota(jnp.int32, (N_BUCKETS, N_HEADS), 0)
            cb = lax.broadcasted_iota(jnp.int32, (N_BUCKETS, N_HEADS), 1)
            out = jnp.zeros((N_BUCKETS, N_HEADS), F32)
            for s in range(N_HEADS):
                acc = ds_acc[s]
                for b in range(N_BUCKETS):
                    out = out + jnp.where((rb == b) & (cb == s), jnp.sum(jnp.where(bk == b, acc, 0.0)), 0.0)
            drb_ref[...] = out

    do_spec = pl.BlockSpec((WINDOW, D_ATT), lambda n: (n, 1))
    return pl.pallas_call(
        body, name="swa_bwd", grid=(n_blk,),
        in_specs=_swa_specs(S) + [do_spec, _resident(), pl.BlockSpec(memory_space=pltpu.SMEM), _resident(),
                                  _rows(WINDOW, N_HEADS), _rows(WINDOW, N_HEADS)],
        out_specs=[_rows(WINDOW, D_ATT), _const((S, D_KV)), _const((S, D_KV)), _const((N_BUCKETS, N_HEADS)),
                   _const((1, N_HEADS))],
        out_shape=[jax.ShapeDtypeStruct((S, D_ATT), MM), jax.ShapeDtypeStruct((S, D_KV), F32),
                   jax.ShapeDtypeStruct((S, D_KV), F32), jax.ShapeDtypeStruct((N_BUCKETS, N_HEADS), F32),
                   jax.ShapeDtypeStruct((1, N_HEADS), F32)],
        scratch_shapes=[pltpu.VMEM((N_HEADS, WINDOW, 2 * WINDOW), F32)],
        compiler_params=_params("arbitrary"),
    )(sqkv, sqkv, sqkv, sqkv, sqkv, dcat, biasm, sinks_slot, bucket, lse, d_col)


def _pre_attn_bwd(x, dh1, dq_fox, dk_fox, dv_fox, dsq, dsk, dsv, dff_t, wqkv, wswa, wfft, g1, tm):
    S = x.shape[0]

    def body(x_ref, dh1_ref, dq_ref, dk_ref, dv_ref, dsq_ref, dsk_ref, dsv_ref, dff_ref, wqkv_ref, wswa_ref,
             wff_ref, g1_ref, dx_ref, dz_ref, dg1_ref):
        i = pl.program_id(0)
        dq = dq_ref[...].astype(MM)
        dsk = dsk_ref[...].astype(MM)
        dsv = dsv_ref[...].astype(MM)
        dz_ref[:, 0:512] = dq
        dz_ref[:, 512:1024] = dk_ref[...]
        dz_ref[:, 1024:1536] = dv_ref[...]
        dz_ref[:, 1536:2048] = dsq_ref[...]
        dz_ref[:, 2048:2176] = dsk
        dz_ref[:, 2176:2304] = dsv
        da = (_dot_nt(dq, wqkv_ref[:, 0:512]) + _dot_nt(dk_ref[...], wqkv_ref[:, 512:1024])
              + _dot_nt(dv_ref[...], wqkv_ref[:, 1024:1536]) + _dot_nt(dsq_ref[...], wswa_ref[:, 0:512])
              + _dot_nt(dsk, wswa_ref[:, 512:640]) + _dot_nt(dsv, wswa_ref[:, 640:768])
              + _dot_tn(dff_ref[...].astype(MM), wff_ref[...]))
        n1, r1 = _rms(x_ref[...])
        dx, dg1 = _rms_bwd(da, n1, r1, g1_ref[...])
        _accumulate(dg1_ref, dg1, i)
        dx_ref[...] = dh1_ref[...] + dx

    return pl.pallas_call(
        body, name="pre_attn_bwd", grid=(S // tm,),
        in_specs=[_rows(tm, D_MODEL), _rows(tm, D_MODEL), _rows(tm, D_ATT), _rows(tm, D_ATT), _rows(tm, D_ATT),
                  _rows(tm, D_ATT), _rows(tm, D_KV), _rows(tm, D_KV), pl.BlockSpec((16, tm), lambda i: (0, i)),
                  _resident(), _resident(), _resident(), _const((1, D_MODEL))],
        out_specs=[_rows(tm, D_MODEL), _rows(tm, 2304), _const((1, D_MODEL))],
        out_shape=[jax.ShapeDtypeStruct((S, D_MODEL), F32), jax.ShapeDtypeStruct((S, 2304), MM),
                   jax.ShapeDtypeStruct((1, D_MODEL), F32)],
        compiler_params=_params("arbitrary"),
    )(x, dh1, dq_fox, dk_fox, dv_fox, dsq, dsk, dsv, dff_t, wqkv, wswa, wfft, g1)


def _weight_grad(a, b, name, tk, n_chunks=1, relu2=False):
    S, K = a.shape
    N = b.shape[1]
    cn = N // n_chunks

    def body(a_ref, b_ref, out_ref):
        av = a_ref[...]
        if relu2:
            av = jnp.square(jnp.maximum(av.astype(F32), 0.0))
        av = av.astype(MM)
        for j in range(n_chunks):
            val = _dot_tn(av, b_ref[:, j * cn:(j + 1) * cn].astype(MM)).astype(MM)
            if n_chunks > 1:
                out_ref[j] = val
            else:
                out_ref[...] = val

    if n_chunks > 1:
        out_spec = pl.BlockSpec((n_chunks, tk, cn), lambda i: (0, i, 0))
        out_shape = jax.ShapeDtypeStruct((n_chunks, K, cn), MM)
    else:
        out_spec = pl.BlockSpec((tk, N), lambda i: (i, 0))
        out_shape = jax.ShapeDtypeStruct((K, N), MM)
    return pl.pallas_call(
        body, name=name, grid=(K // tk,),
        in_specs=[pl.BlockSpec((S, tk), lambda i: (0, i)), _resident()],
        out_specs=out_spec, out_shape=out_shape, compiler_params=_params("parallel"),
    )(a, b)


def _forget_weight_grad(dff_t, a):
    def body(d_ref, a_ref, out_ref):
        out_ref[...] = _dot(d_ref[...].astype(MM), a_ref[...])

    return pl.pallas_call(
        body, name="forget_weight_grad", out_shape=jax.ShapeDtypeStruct((16, D_MODEL), F32),
        in_specs=[_resident(), _resident()], out_specs=_resident(),
    )(dff_t, a)


def _place():
    return lax.axis_index("x"), lax.axis_index("y"), lax.axis_index("c")


def _all_gather(stacks):
    n = len(stacks)

    def body(*refs):
        outs = refs[n:2 * n]
        send_sems, recv_sems = refs[2 * n:]
        x, y, c = _place()
        me, sibling = (x, y, c), (x, y, 1 - c)
        chips = [(1 - x, y), (x, 1 - y), (1 - x, 1 - y)]

        def slot(out, place):
            px, py, pc = place
            return out.at[4 * px + 2 * py + pc]

        def copy(a, k, block, to):
            dst = slot(outs[a], block)
            return pltpu.make_async_remote_copy(
                src_ref=dst, dst_ref=dst, send_sem=send_sems.at[7 * a + k],
                recv_sem=recv_sems.at[7 * a + k], device_id=to, device_id_type=MESH)

        first = []
        for a in range(n):
            first.append(copy(a, 0, me, sibling))
            first += [copy(a, 1 + j, me, (*chip, c)) for j, chip in enumerate(chips)]
        for cp in first:
            cp.start()
        passed = []
        for j, chip in enumerate(chips):
            for a in range(n):
                copy(a, 1 + j, (*chip, c), me).wait_recv()
                fwd = copy(a, 4 + j, (*chip, c), sibling)
                fwd.start()
                passed.append(fwd)
        for a in range(n):
            copy(a, 0, sibling, me).wait_recv()
            for j, chip in enumerate(chips):
                copy(a, 4 + j, (*chip, 1 - c), me).wait_recv()
        for cp in first + passed:
            cp.wait_send()

    hbm = pl.BlockSpec(memory_space=pl.ANY)
    return pl.pallas_call(
        body, name="all_gather_first",
        out_shape=[jax.ShapeDtypeStruct(s.shape, s.dtype) for s in stacks],
        in_specs=[hbm] * n, out_specs=[hbm] * n, input_output_aliases={a: a for a in range(n)},
        scratch_shapes=[pltpu.SemaphoreType.DMA((7 * n,)), pltpu.SemaphoreType.DMA((7 * n,))],
    )(*stacks)


def _exchange_in_chip(grads):
    n = len(grads)

    def body(*refs):
        ins, outs = refs[:n], refs[n:2 * n]
        send_sems, recv_sems = refs[2 * n:]
        x, y, c = _place()
        copies = []
        for a in range(n):
            cp = pltpu.make_async_remote_copy(
                src_ref=ins[a].at[:, 1 - c], dst_ref=outs[a], send_sem=send_sems.at[a], recv_sem=recv_sems.at[a],
                device_id=(x, y, 1 - c), device_id_type=MESH)
            cp.start()
            copies.append(cp)
        for cp in copies:
            cp.wait()

    hbm = pl.BlockSpec(memory_space=pl.ANY)
    views = [g.reshape((4, 2) + g.shape[1:]) for g in grads]
    return pl.pallas_call(
        body, name="exchange_in_chip",
        out_shape=[jax.ShapeDtypeStruct((4,) + g.shape[1:], g.dtype) for g in grads],
        in_specs=[hbm] * n, out_specs=[hbm] * n,
        scratch_shapes=[pltpu.SemaphoreType.DMA((n,)), pltpu.SemaphoreType.DMA((n,))],
    )(*views)


def _chip_sum(grad, other, name):
    _, _, r, cdim = grad.shape
    tr = 256 if r % 256 == 0 else r

    def body(c_ref, g_ref, o_ref, out_ref):
        out_ref[...] = (g_ref[...].astype(F32) + o_ref[...].astype(F32)).astype(out_ref.dtype)

    return pl.pallas_call(
        body, name=name,
        grid_spec=pltpu.PrefetchScalarGridSpec(
            num_scalar_prefetch=1, grid=(4, r // tr),
            in_specs=[pl.BlockSpec((None, None, tr, cdim), lambda k, i, c_ref: (k, c_ref[0], i, 0)),
                      pl.BlockSpec((None, tr, cdim), lambda k, i, c_ref: (k, i, 0))],
            out_specs=pl.BlockSpec((None, tr, cdim), lambda k, i, c_ref: (k, i, 0))),
        out_shape=jax.ShapeDtypeStruct((4, r, cdim), MM),
        compiler_params=_params("parallel", "parallel"),
    )(lax.axis_index("c").astype(jnp.int32).reshape(1), grad, other)


def _exchange_between_chips(sums, small):
    n = len(sums)

    def body(*refs):
        ins, small_in = refs[:n], refs[n]
        outs, small_out = refs[n + 1:2 * n + 1], refs[2 * n + 1]
        send_sems, recv_sems, small_send, small_recv, local_sems = refs[2 * n + 2:]
        x, y, c = _place()
        my_chip = 2 * x + y
        chips = [(1 - x, y), (x, 1 - y), (1 - x, 1 - y)]
        copies = []
        for a in range(n):
            for j, (px, py) in enumerate(chips):
                cp = pltpu.make_async_remote_copy(
                    src_ref=ins[a].at[2 * px + py], dst_ref=outs[a].at[my_chip], send_sem=send_sems.at[3 * a + j],
                    recv_sem=recv_sems.at[3 * a + j], device_id=(px, py, c), device_id_type=MESH)
                cp.start()
                copies.append(cp)
        me = 4 * x + 2 * y + c
        local = pltpu.make_async_copy(small_in, small_out.at[me], local_sems.at[0])
        local.start()
        copies.append(local)
        k = 0
        for dx in range(2):
            for dy in range(2):
                for dc in range(2):
                    if dx + dy + dc == 0:
                        continue
                    peer = (x ^ dx, y ^ dy, c ^ dc)
                    cp = pltpu.make_async_remote_copy(
                        src_ref=small_in, dst_ref=small_out.at[me], send_sem=small_send.at[k],
                        recv_sem=small_recv.at[k], device_id=peer, device_id_type=MESH)
                    cp.start()
                    copies.append(cp)
                    k += 1
        for cp in copies:
            cp.wait()

    hbm = pl.BlockSpec(memory_space=pl.ANY)
    return pl.pallas_call(
        body, name="exchange_between_chips",
        out_shape=[jax.ShapeDtypeStruct(s.shape, s.dtype) for s in sums]
        + [jax.ShapeDtypeStruct((N_DEV,) + small.shape, small.dtype)],
        in_specs=[hbm] * (n + 1), out_specs=[hbm] * (n + 1),
        scratch_shapes=[pltpu.SemaphoreType.DMA((3 * n,)), pltpu.SemaphoreType.DMA((3 * n,)),
                        pltpu.SemaphoreType.DMA((7,)), pltpu.SemaphoreType.DMA((7,)),
                        pltpu.SemaphoreType.DMA((1,))],
    )(*sums, small)


HBM_SPEC = pl.BlockSpec(memory_space=pltpu.HBM)
SEM_SPEC = pl.BlockSpec(memory_space=pltpu.SEMAPHORE)
DATAFLOW = pltpu.SideEffectType.DATAFLOW_SIDE_EFFECTING


def _exchange_start(name, arrays, n_copies, plan):
    n = len(arrays)

    def body(*refs):
        send_sems, recv_sems, token = refs[n], refs[n + 1], refs[2 * n + 2]
        for cp in plan(refs[:n], send_sems, recv_sems):
            cp.start()
        token[...] = jnp.zeros_like(token)

    out = pl.pallas_call(
        body, name=name,
        out_shape=(pltpu.SemaphoreType.DMA((n_copies,)), pltpu.SemaphoreType.DMA((n_copies,)),
                   *[pltpu.HBM(a.shape, a.dtype) for a in arrays], jax.ShapeDtypeStruct((1, D_MODEL), F32)),
        in_specs=[HBM_SPEC] * n,
        out_specs=(SEM_SPEC, SEM_SPEC, *[HBM_SPEC] * n, pl.BlockSpec(memory_space=pltpu.VMEM)),
        input_output_aliases={i: 2 + i for i in range(n)},
        compiler_params=pltpu.CompilerParams(has_side_effects=DATAFLOW),
    )(*[pltpu.with_memory_space_constraint(a, pltpu.HBM) for a in arrays])
    return (out[0], out[1]), list(out[2:2 + n]), out[2 + n]


def _exchange_wait(name, arrays, sems, after, plan):
    n = len(arrays)

    def body(*refs):
        send_sems, recv_sems = refs[n], refs[n + 1]
        for cp in plan(refs[:n], send_sems, recv_sems):
            cp.wait_send()
            cp.wait_recv()

    out = pl.pallas_call(
        body, name=name, out_shape=[pltpu.HBM(a.shape, a.dtype) for a in arrays],
        in_specs=[HBM_SPEC] * n + [SEM_SPEC, SEM_SPEC, pl.BlockSpec(memory_space=pl.ANY)],
        out_specs=[HBM_SPEC] * n, input_output_aliases={i: i for i in range(n)},
        compiler_params=pltpu.CompilerParams(has_side_effects=DATAFLOW),
    )(*arrays, sems[0], sems[1], after)
    return list(out)


def _remote(src, dst, send_sems, recv_sems, k, to):
    return pltpu.make_async_remote_copy(src_ref=src, dst_ref=dst, send_sem=send_sems.at[k], recv_sem=recv_sems.at[k],
                                        device_id=to, device_id_type=MESH)


def _plan_gather_direct(refs, send_sems, recv_sems):
    x, y, c = _place()
    me = 4 * x + 2 * y + c
    peers = [(x, y, 1 - c), (1 - x, y, c), (x, 1 - y, c), (1 - x, 1 - y, c)]
    return [_remote(ref.at[me], ref.at[me], send_sems, recv_sems, 4 * a + k, peer)
            for a, ref in enumerate(refs) for k, peer in enumerate(peers)]


def _plan_gather_pass_on(refs, send_sems, recv_sems):
    x, y, c = _place()
    chips = [(1 - x, y), (x, 1 - y), (1 - x, 1 - y)]
    return [_remote(ref.at[4 * px + 2 * py + c], ref.at[4 * px + 2 * py + c], send_sems, recv_sems, 3 * a + k,
                    (x, y, 1 - c))
            for a, ref in enumerate(refs) for k, (px, py) in enumerate(chips)]


def _plan_in_chip(refs, send_sems, recv_sems):
    n = len(refs) // 2
    x, y, c = _place()
    return [_remote(refs[a].at[:, 1 - c], refs[n + a], send_sems, recv_sems, a, (x, y, 1 - c)) for a in range(n)]


def _plan_between_chips(refs, send_sems, recv_sems):
    n = len(refs) // 2
    x, y, c = _place()
    chips = [(1 - x, y), (x, 1 - y), (1 - x, 1 - y)]
    return [_remote(refs[a].at[2 * px + py], refs[n + a].at[2 * x + y], send_sems, recv_sems, 3 * a + k, (px, py, c))
            for a in range(n) for k, (px, py) in enumerate(chips)]


def _adamw_math(w, g, m, v):
    m = ADAM_B1 * m + (1.0 - ADAM_B1) * g
    v = ADAM_B2 * v + (1.0 - ADAM_B2) * jnp.square(g)
    m_hat = m / (1.0 - ADAM_B1 ** ADAM_STEP)
    v_hat = v / (1.0 - ADAM_B2 ** ADAM_STEP)
    delta = -ADAM_LR * (m_hat / (jnp.sqrt(v_hat) + ADAM_EPS) + ADAM_WD * w)
    return delta, m, v


def _adamw(parts, w, m, v, name):
    n_parts, r, cdim = parts.shape
    tr = 256 if r % 256 == 0 else r

    def body(p_ref, w_ref, m_ref, v_ref, g_out, d_out, m_out, v_out):
        g = p_ref[0].astype(F32)
        for k in range(1, n_parts):
            g = g + p_ref[k].astype(F32)
        delta, m_new, v_new = _adamw_math(w_ref[...], g, m_ref[...], v_ref[...])
        g_out[...] = g
        d_out[...] = delta
        m_out[...] = m_new
        v_out[...] = v_new

    blk = pl.BlockSpec((tr, cdim), lambda i: (i, 0))
    return pl.pallas_call(
        body, name=name, grid=(r // tr,),
        in_specs=[pl.BlockSpec((n_parts, tr, cdim), lambda i: (0, i, 0)), blk, blk, blk],
        out_specs=[blk] * 4, out_shape=[jax.ShapeDtypeStruct((r, cdim), F32)] * 4,
        compiler_params=_params("parallel"),
    )(parts, w, m, v)


def _adamw_chips(parts, sums, w, m, v, name):
    _, r, cdim = parts.shape
    tr = 256 if r % 256 == 0 else r

    def body(chip_ref, p_ref, own_ref, w_ref, m_ref, v_ref, g_out, d_out, m_out, v_out):
        g = None
        for k in range(4):
            term = jnp.where(chip_ref[0] == k, own_ref[...], p_ref[k]).astype(F32)
            g = term if g is None else g + term
        delta, m_new, v_new = _adamw_math(w_ref[...], g, m_ref[...], v_ref[...])
        g_out[...] = g
        d_out[...] = delta
        m_out[...] = m_new
        v_out[...] = v_new

    blk = pl.BlockSpec((tr, cdim), lambda i, chip: (i, 0))
    my_chip = (2 * lax.axis_index("x") + lax.axis_index("y")).astype(jnp.int32).reshape(1)
    return pl.pallas_call(
        body, name=name,
        grid_spec=pltpu.PrefetchScalarGridSpec(
            num_scalar_prefetch=1, grid=(r // tr,),
            in_specs=[pl.BlockSpec((4, tr, cdim), lambda i, chip: (0, i, 0)),
                      pl.BlockSpec((None, tr, cdim), lambda i, chip: (chip[0], i, 0)), blk, blk, blk],
            out_specs=[blk] * 4),
        out_shape=[jax.ShapeDtypeStruct((r, cdim), F32)] * 4,
        compiler_params=_params("parallel"),
    )(my_chip, parts, sums, w, m, v)


class _NoExchange:
    def __init__(self, weights):
        self.weights = weights

    def before_pre_attn(self, g1):
        return g1

    def after_fox_fwd(self, fox_o, sinks_slot):
        return sinks_slot

    def after_attention(self, swa_o):
        return self.weights

    def after_mlp_grads(self, grads, g2):
        return g2

    def after_attn_out_bwd(self, dmix, d_row3):
        return d_row3


def _slot_order(t, axis):
    shp = t.shape
    t = t.reshape(shp[:axis] + (N_HEADS, shp[axis] // N_HEADS) + shp[axis + 1:])
    t = jnp.take(t, np.array(SLOT_HEAD), axis=axis)
    return t.reshape(shp)


def _head_order(t, axis):
    shp = t.shape
    t = t.reshape(shp[:axis] + (N_HEADS, shp[axis] // N_HEADS) + shp[axis + 1:])
    t = jnp.take(t, np.array(HEAD_SLOT), axis=axis)
    return t.reshape(shp)


def _forward_backward(x, p, target, win, hooks, b_forget, rel_bias, sinks, g1, g2, g3, g4, g5):
    S = x.shape[0]
    tm = 256
    t = 256
    wqkv = win[:, :3 * D_ATT]
    wfft = jnp.pad(win[:, 3 * D_ATT:3 * D_ATT + N_HEADS].T, ((0, 8), (0, 0)))
    q0 = 3 * D_ATT + N_HEADS
    wswa = jnp.concatenate([_slot_order(win[:, q0:q0 + D_ATT], 1), win[:, q0 + D_ATT:]], axis=1)
    bcol = jnp.pad(b_forget.reshape(N_HEADS, 1), ((0, 8), (0, 0)))
    rel_bias_slot = rel_bias[:, np.array(SLOT_HEAD)]
    sinks_slot = sinks.reshape(N_HEADS)[np.array(SLOT_HEAD)]
    bucket = jnp.asarray(_swa_bucket_map())

    a, fqkv, sqkv, fft = _pre_attn(x, hooks.before_pre_attn(g1), wqkv, wswa, wfft, tm)
    c_row = _forget_cumsum(fft, bcol)
    c_col = c_row[:N_HEADS].T
    c_row3 = c_row[:N_HEADS].reshape(N_HEADS, S // t, t)
    fox_o, fox_lse = _fox_fwd(fqkv, c_row3, tq=512, tk=t)
    biasm = _swa_bias(rel_bias_slot, bucket)
    sinks_slot = hooks.after_fox_fwd(fox_o, sinks_slot)
    swa_o, swa_lse = _swa_fwd(sqkv, biasm, sinks_slot)
    wout, w1, w2, wple, wg = hooks.after_attention(swa_o)
    wout_fox = wout[:D_ATT]
    wout_swa = _slot_order(wout[D_ATT:], 0)
    mix, h1, m = _post_attn(x, fox_o, swa_o, wout_fox, wout_swa, g2, g3, tm)
    u, y, h2 = _mlp_fwd(m, h1, w1, w2, g4, tm)
    dh2, dpe, dgl, dg5, loss = _ple_loss(h2, p, target, wg, wple, g5, tm)

    d_wple = _weight_grad(p, dpe, "grad_w_ple", tk=D_PLE, n_chunks=N_DEV)
    d_wg = _weight_grad(h2, dgl, "grad_w_ple_gate", tk=256)
    dh1, dy, du, dg4, dg3 = _mlp_bwd(dh2, y, h1, u, w1, w2, g4, g3, tm)
    d_w2 = _weight_grad(u, dy, "grad_w_ff2", tk=256, relu2=True)
    d_w1 = _weight_grad(m, du, "grad_w_ff1", tk=256, n_chunks=N_DEV)
    early = dict(w_ff1=d_w1, w_ff2=d_w2.reshape(N_DEV, FF_CHUNK, D_MODEL), w_ple=d_wple,
                 w_ple_gate=d_wg.reshape(N_DEV, D_MODEL // N_DEV, D_MODEL))
    g2_late = hooks.after_mlp_grads(early, g2)

    head = np.arange(D_ATT) // HEAD_DIM
    head_rows = jnp.asarray((head[None, :] == np.arange(N_HEADS)[:, None]).astype(np.float32))
    dmix, dcat, d_row, d_col, dg2 = _attn_out_bwd(dh1, mix, fox_o, swa_o, wout_fox, wout_swa, g2_late,
                                                  head_rows, head_rows.T, tm)
    d_wout_fox = _weight_grad(fox_o, dmix, "grad_w_out_fox", tk=256)
    d_wout_swa = _weight_grad(swa_o, dmix, "grad_w_out_swa", tk=256)

    lse_row3 = fox_lse.T.reshape(N_HEADS, S // t, t)
    d_row3 = hooks.after_attn_out_bwd(dmix, d_row.reshape(N_HEADS, S // t, t))
    dq_fox, dk_fox, dv_fox, dc_col, dcq = _fox_bwd(fqkv, dcat, lse_row3, d_row3, c_col, tq=t, tk=512)
    dsq, dsk, dsv, d_rb_slot, d_sink_slot = _swa_bwd(sqkv, dcat, biasm, sinks_slot, bucket, swa_lse, d_col)
    dc_row = jnp.pad(dc_col.T + dcq.reshape(N_HEADS, S), ((0, 8), (0, 0)))
    dff_t, db = _forget_bwd(dc_row, fft, bcol)
    grad_x, dz, dg1 = _pre_attn_bwd(x, dh1, dq_fox, dk_fox, dv_fox, dsq, dsk, dsv, dff_t, wqkv, wswa, wfft, g1, tm)
    d_wmain = _weight_grad(a, dz, "grad_w_in", tk=256)
    d_wff_t = _forget_weight_grad(dff_t, a)

    d_win = jnp.concatenate([
        d_wmain[:, :3 * D_ATT], d_wff_t[:N_HEADS].T.astype(MM), _head_order(d_wmain[:, 3 * D_ATT:4 * D_ATT], 1),
        d_wmain[:, 4 * D_ATT:]], axis=1)
    d_win = d_win.reshape(D_MODEL, N_DEV, D_IN // N_DEV).transpose(1, 0, 2).reshape((N_DEV,) + W_IN_FLAT)
    d_wout = jnp.concatenate([d_wout_fox, _head_order(d_wout_swa, 0)], axis=0).reshape(N_DEV, D_MODEL // N_DEV, D_MODEL)
    big = dict(early, w_in=d_win, w_out=d_wout)
    small = dict(b_forget=db[:N_HEADS].reshape(1, N_HEADS), rel_bias=d_rb_slot[:, np.array(HEAD_SLOT)],
                 swa_sinks=d_sink_slot[:, np.array(HEAD_SLOT)], g_attn_pre=dg1, g_attn_post=dg2, g_ff_pre=dg3,
                 g_ff_post=dg4, g_ple_post=dg5)
    return loss, grad_x, big, small, dz


BIG = ("w_in", "w_out", "w_ff1", "w_ff2", "w_ple", "w_ple_gate")
SMALL_ROWS = ("g_attn_pre", "g_attn_post", "g_ff_pre", "g_ff_post", "g_ple_post")
WEIGHTS = ("w_in", "b_forget", "w_out", "rel_bias", "swa_sinks", "g_attn_pre", "g_attn_post", "w_ff1", "w_ff2",
           "g_ff_pre", "g_ff_post", "w_ple", "w_ple_gate", "g_ple_post")


EARLY = ("w_ff1", "w_ff2", "w_ple", "w_ple_gate")
LATE = ("w_in", "w_out")


class _Overlap:
    def __init__(self, later):
        self.later = later

    def before_pre_attn(self, g1):
        self.gather_sems, self.later, token = _exchange_start("gather_rest_start", self.later, 4 * 5, _plan_gather_direct)
        return g1 + token

    def after_fox_fwd(self, fox_o, sinks_slot):
        later = _exchange_wait("gather_rest_wait", self.later, self.gather_sems, fox_o, _plan_gather_direct)
        self.pass_sems, self.later, token = _exchange_start("gather_pass_on_start", later, 3 * 5, _plan_gather_pass_on)
        return sinks_slot + token[0, :N_HEADS]

    def after_attention(self, swa_o):
        wout_g, w1_g, w2_g, wple_g, wg_g = _exchange_wait("gather_pass_on_wait", self.later, self.pass_sems, swa_o,
                                                         _plan_gather_pass_on)
        return (wout_g.reshape(D_MODEL, D_MODEL), w1_g, w2_g.reshape(D_FF, D_MODEL),
                jnp.moveaxis(wple_g, 0, 1).reshape(D_PLE, D_MODEL), wg_g.reshape(D_MODEL, D_MODEL))

    def after_mlp_grads(self, grads, g2):
        views = [grads[k].reshape((4, 2) + grads[k].shape[1:]) for k in EARLY]
        lands = [lax.empty((4,) + grads[k].shape[1:], MM) for k in EARLY]
        self.in_chip_sems, self.in_chip, token = _exchange_start("grads_in_chip_start", views + lands, len(EARLY),
                                                                 _plan_in_chip)
        return g2 + token

    def after_attn_out_bwd(self, dmix, d_row3):
        arrays = _exchange_wait("grads_in_chip_wait", self.in_chip, self.in_chip_sems, dmix, _plan_in_chip)
        n = len(EARLY)
        sums = [_chip_sum(arrays[a], arrays[n + a], "chip_sum_" + k) for a, k in enumerate(EARLY)]
        lands = [lax.empty(s.shape, s.dtype) for s in sums]
        self.between_sems, self.between, token = _exchange_start("grads_between_chips_start", sums + lands, 3 * n,
                                                                 _plan_between_chips)
        return d_row3 + token[0, 0]

    def finish(self, after):
        arrays = _exchange_wait("grads_between_chips_wait", self.between, self.between_sems, after,
                                _plan_between_chips)
        n = len(EARLY)
        self.sums = arrays[:n]
        return arrays[n:]


def _pack_small(t):
    rows = [t[k].reshape(1, D_MODEL) for k in SMALL_ROWS]
    misc = jnp.concatenate([t["b_forget"].reshape(-1), t["swa_sinks"].reshape(-1), t["rel_bias"].reshape(-1)])
    rows.append(jnp.pad(misc, (0, D_MODEL - misc.shape[0])).reshape(1, D_MODEL))
    rows.append(jnp.pad(t["loss"].reshape(-1), (0, D_MODEL - 1)).reshape(1, D_MODEL))
    rows.append(jnp.zeros((1, D_MODEL), F32))
    return jnp.concatenate(rows, axis=0).astype(F32)


def _unpack_small(blk):
    out = {k: blk[i].reshape(1, D_MODEL) for i, k in enumerate(SMALL_ROWS)}
    misc = blk[len(SMALL_ROWS)]
    out["b_forget"] = misc[:N_HEADS].reshape(1, N_HEADS)
    out["swa_sinks"] = misc[N_HEADS:2 * N_HEADS].reshape(1, N_HEADS)
    out["rel_bias"] = misc[2 * N_HEADS:2 * N_HEADS + N_BUCKETS * N_HEADS].reshape(N_BUCKETS, N_HEADS)
    out["loss"] = blk[len(SMALL_ROWS) + 1, 0]
    return out


def kernel(x, p, w_in, b_forget, w_out, rel_bias, swa_sinks, g_attn_pre, g_attn_post, w_ff1, w_ff2, g_ff_pre, g_ff_post, w_ple, w_ple_gate, g_ple_post, loss_target, m_w_in, m_b_forget, m_w_out, m_rel_bias, m_swa_sinks, m_g_attn_pre, m_g_attn_post, m_w_ff1, m_w_ff2, m_g_ff_pre, m_g_ff_post, m_w_ple, m_w_ple_gate, m_g_ple_post, v_w_in, v_b_forget, v_w_out, v_rel_bias, v_swa_sinks, v_g_attn_pre, v_g_attn_post, v_w_ff1, v_w_ff2, v_g_ff_pre, v_g_ff_post, v_w_ple, v_w_ple_gate, v_g_ple_post):
    w = dict(w_in=w_in, b_forget=b_forget, w_out=w_out, rel_bias=rel_bias, swa_sinks=swa_sinks,
             g_attn_pre=g_attn_pre, g_attn_post=g_attn_post, w_ff1=w_ff1, w_ff2=w_ff2, g_ff_pre=g_ff_pre,
             g_ff_post=g_ff_post, w_ple=w_ple, w_ple_gate=w_ple_gate, g_ple_post=g_ple_post)
    mom = dict(w_in=m_w_in, b_forget=m_b_forget, w_out=m_w_out, rel_bias=m_rel_bias, swa_sinks=m_swa_sinks,
               g_attn_pre=m_g_attn_pre, g_attn_post=m_g_attn_post, w_ff1=m_w_ff1, w_ff2=m_w_ff2,
               g_ff_pre=m_g_ff_pre, g_ff_post=m_g_ff_post, w_ple=m_w_ple, w_ple_gate=m_w_ple_gate,
               g_ple_post=m_g_ple_post)
    var = dict(w_in=v_w_in, b_forget=v_b_forget, w_out=v_w_out, rel_bias=v_rel_bias, swa_sinks=v_swa_sinks,
               g_attn_pre=v_g_attn_pre, g_attn_post=v_g_attn_post, w_ff1=v_w_ff1, w_ff2=v_w_ff2,
               g_ff_pre=v_g_ff_pre, g_ff_post=v_g_ff_post, w_ple=v_w_ple, w_ple_gate=v_w_ple_gate,
               g_ple_post=v_g_ple_post)

    flat = lambda t, k: t.reshape(W_IN_FLAT) if k == "w_in" else t
    me = 4 * lax.axis_index("x") + 2 * lax.axis_index("y") + lax.axis_index("c")

    def stack(block):
        return lax.dynamic_update_slice_in_dim(lax.empty((N_DEV,) + block.shape, block.dtype), block[None], me, 0)

    stacks = [stack(flat(w[k][0].astype(MM), k)) for k in BIG]
    (win_g,), later = _all_gather(stacks[:1]), stacks[1:]
    win = jnp.moveaxis(win_g.reshape(N_DEV, D_MODEL, D_IN // N_DEV), 0, 1).reshape(D_MODEL, D_IN)
    hooks = _Overlap(later)
    loss, grad_x, big, small, last = _forward_backward(
        x[0], p[0, 0], loss_target[0], win, hooks, b_forget, rel_bias, swa_sinks, g_attn_pre, g_attn_post,
        g_ff_pre, g_ff_post, g_ple_post)
    early_parts = hooks.finish(last)

    grads = [big[k] for k in LATE]
    views = [g.reshape((4, 2) + g.shape[1:]) for g in grads]
    others = _exchange_in_chip(grads)
    sums = [_chip_sum(v_, o, "chip_sum_" + k) for v_, o, k in zip(views, others, LATE)]
    small["loss"] = loss
    *parts, small_all = _exchange_between_chips(sums, _pack_small(small))

    out_g, out_d, out_m, out_v = {}, {}, {}, {}
    for k, part, own in zip(LATE, parts, sums):
        shape = w[k].shape
        g, d, m_new, v_new = _adamw_chips(part, own, flat(w[k][0], k), flat(mom[k][0], k), flat(var[k][0], k),
                                          "adamw_" + k)
        out_g[k], out_d[k], out_m[k], out_v[k] = g.reshape(shape), d.reshape(shape), m_new.reshape(shape), v_new.reshape(shape)
    for k, part, own in zip(EARLY, early_parts, hooks.sums):
        g, d, m_new, v_new = _adamw_chips(part, own, w[k][0], mom[k][0], var[k][0], "adamw_" + k)
        out_g[k], out_d[k], out_m[k], out_v[k] = g[None], d[None], m_new[None], v_new[None]
    rep = {k: w[k] for k in w if k not in BIG}
    rep["loss"] = jnp.zeros((), F32)
    rep_m = {k: mom[k] for k in mom if k not in BIG}
    rep_m["loss"] = jnp.zeros((), F32)
    rep_v = {k: var[k] for k in var if k not in BIG}
    rep_v["loss"] = jnp.ones((), F32)
    g_s, d_s, m_s, v_s = _adamw(small_all, _pack_small(rep), _pack_small(rep_m), _pack_small(rep_v), "adamw_small")
    g_s, d_s, m_s, v_s = _unpack_small(g_s), _unpack_small(d_s), _unpack_small(m_s), _unpack_small(v_s)
    for k in w:
        if k not in BIG:
            out_g[k], out_d[k], out_m[k], out_v[k] = g_s[k], d_s[k], m_s[k], v_s[k]
    return (g_s["loss"], grad_x[None], *[out_g[k] for k in WEIGHTS], *[out_d[k] for k in WEIGHTS],
            *[out_m[k] for k in WEIGHTS], *[out_v[k] for k in WEIGHTS])
```

```python
import functools

import numpy as np
import jax
import jax.numpy as jnp
from jax import lax
from jax.experimental import pallas as pl
from jax.experimental.pallas import tpu as pltpu
from jax.experimental.pallas import tpu_sc as plsc

F32 = jnp.float32
MM = jnp.bfloat16

D_MODEL = 1024
HEAD_DIM = 64
N_HEADS = 8
D_ATT = N_HEADS * HEAD_DIM
D_KV = 128
D_FF = 4096
D_PLE = 256
D_IN = 3 * D_ATT + N_HEADS + D_ATT + 2 * D_KV
N_DEV = 8
W_IN_FLAT = (D_MODEL * D_IN // N_DEV // 128, 128)
FF_CHUNK = D_FF // N_DEV
WINDOW = 128
N_BUCKETS = 32
MAX_DISTANCE = 128
RMS_EPS = 1e-6
Q_SCALE = HEAD_DIM ** -0.5
NEG = -1e30

ADAM_LR = 0.001
ADAM_B1 = 0.9
ADAM_B2 = 0.999
ADAM_EPS = 1e-08
ADAM_WD = 0.01
ADAM_STEP = 10

SLOT_HEAD = (0, 4, 1, 5, 2, 6, 3, 7)
HEAD_SLOT = (0, 2, 4, 6, 1, 3, 5, 7)

VMEM_LIMIT = 56 * 1024 * 1024
MESH = pl.DeviceIdType.MESH

NT = (((1,), (1,)), ((), ()))
TN = (((0,), (0,)), ((), ()))


def _params(*semantics):
    return pltpu.CompilerParams(dimension_semantics=semantics, vmem_limit_bytes=VMEM_LIMIT)


def _resident():
    return pl.BlockSpec(memory_space=pltpu.VMEM)


def _rows(tm, width):
    return pl.BlockSpec((tm, width), lambda i: (i, 0))


def _const(shape):
    return pl.BlockSpec(shape, lambda i: (0,) * len(shape))


def _dot(a, b):
    return jnp.dot(a, b, preferred_element_type=F32)


def _dot_nt(a, b):
    return lax.dot_general(a, b, NT, preferred_element_type=F32)


def _dot_tn(a, b):
    return lax.dot_general(a, b, TN, preferred_element_type=F32)


def _rms(xf):
    r = lax.rsqrt(jnp.mean(xf * xf, axis=-1, keepdims=True) + RMS_EPS)
    return xf * r, r


def _rms_bwd(dout, n, r, g):
    dg = jnp.sum(dout * n, axis=0, keepdims=True)
    dn = dout * g
    dx = r * (dn - n * jnp.mean(dn * n, axis=-1, keepdims=True))
    return dx, dg


def _accumulate(ref, value, step):
    @pl.when(step == 0)
    def _():
        ref[...] = value

    @pl.when(step != 0)
    def _():
        ref[...] += value


def _t5_bucket(n):
    max_exact = N_BUCKETS // 2
    large = max_exact + (np.log(np.maximum(n, 1) / max_exact) / np.log(MAX_DISTANCE / max_exact)
                         * (N_BUCKETS - max_exact)).astype(np.int32)
    large = np.minimum(large, N_BUCKETS - 1)
    return np.where(n < max_exact, n, large).astype(np.int32)


def _swa_bucket_map():
    i = np.arange(WINDOW)[:, None]
    j = np.arange(2 * WINDOW)[None, :]
    dist = i + WINDOW - j
    ok = (dist >= 0) & (dist < WINDOW)
    return np.where(ok, _t5_bucket(np.clip(dist, 0, None)), -1).astype(np.int32)


def _pre_attn(x, g1, wqkv, wswa, wfft, tm):
    S = x.shape[0]

    def body(x_ref, g_ref, wqkv_ref, wswa_ref, wff_ref, a_ref, fqkv_ref, sqkv_ref, fft_ref):
        n, _ = _rms(x_ref[...])
        a = (n * g_ref[...]).astype(MM)
        a_ref[...] = a
        fqkv_ref[:, :D_ATT] = (_dot(a, wqkv_ref[:, :D_ATT]) * Q_SCALE).astype(MM)
        fqkv_ref[:, D_ATT:] = _dot(a, wqkv_ref[:, D_ATT:]).astype(MM)
        sqkv_ref[:, :D_ATT] = (_dot(a, wswa_ref[:, :D_ATT]) * Q_SCALE).astype(MM)
        sqkv_ref[:, D_ATT:] = _dot(a, wswa_ref[:, D_ATT:]).astype(MM)
        fft_ref[...] = _dot_nt(wff_ref[...], a)

    return pl.pallas_call(
        body, name="pre_attn", grid=(S // tm,),
        in_specs=[_rows(tm, D_MODEL), _const((1, D_MODEL)), _resident(), _resident(), _resident()],
        out_specs=[_rows(tm, D_MODEL), _rows(tm, 3 * D_ATT), _rows(tm, D_ATT + 2 * D_KV),
                   pl.BlockSpec((16, tm), lambda i: (0, i))],
        out_shape=[jax.ShapeDtypeStruct((S, D_MODEL), MM), jax.ShapeDtypeStruct((S, 3 * D_ATT), MM),
                   jax.ShapeDtypeStruct((S, D_ATT + 2 * D_KV), MM), jax.ShapeDtypeStruct((16, S), F32)],
        compiler_params=_params("parallel"),
    )(x, g1, wqkv, wswa, wfft)


def _lane_scan(v, reverse):
    S = v.shape[1]
    lane = lax.broadcasted_iota(jnp.int32, v.shape, 1)
    k = 1
    while k < S:
        if reverse:
            v = v + jnp.where(lane < S - k, pltpu.roll(v, S - k, axis=1), 0.0)
        else:
            v = v + jnp.where(lane >= k, pltpu.roll(v, k, axis=1), 0.0)
        k *= 2
    return v


def _forget_cumsum(fft, bcol):
    def body(f_ref, b_ref, c_ref):
        z = f_ref[...] + b_ref[...]
        log_f = jnp.minimum(z, 0.0) - jnp.log1p(jnp.exp(-jnp.abs(z)))
        c_ref[...] = _lane_scan(log_f, reverse=False)

    return pl.pallas_call(
        body, name="forget_cumsum", out_shape=jax.ShapeDtypeStruct(fft.shape, F32),
        in_specs=[_resident(), _resident()], out_specs=_resident(),
    )(fft, bcol)


def _forget_bwd(dc_row, fft, bcol):
    def body(dc_ref, f_ref, b_ref, dff_ref, db_ref):
        z = f_ref[...] + b_ref[...]
        dlog_f = _lane_scan(dc_ref[...], reverse=True)
        dff = dlog_f * (1.0 / (1.0 + jnp.exp(z)))
        dff_ref[...] = dff
        db_ref[...] = jnp.sum(dff, axis=1, keepdims=True)

    return pl.pallas_call(
        body, name="forget_bwd",
        out_shape=[jax.ShapeDtypeStruct(fft.shape, F32), jax.ShapeDtypeStruct((fft.shape[0], 1), F32)],
        in_specs=[_resident()] * 3, out_specs=[_resident()] * 2,
    )(dc_row, fft, bcol)


def _head_select(shape, upper):
    lane = lax.broadcasted_iota(jnp.int32, shape, 1)
    return lane >= HEAD_DIM if upper else lane < HEAD_DIM


def _fox_fwd(fqkv, c_row3, tq, tk, pairs_per_loop=2, row_chunks=1):
    S = fqkv.shape[0]
    rq = tq // row_chunks
    n_band = tq // tk

    def body(q_ref, k_ref, v_ref, ck_ref, o_ref, lse_ref):
        qi = pl.program_id(0)
        row = lax.broadcasted_iota(jnp.int32, (rq, tk), 0)
        col = lax.broadcasted_iota(jnp.int32, (rq, tk), 1)
        low = _head_select((rq, 128), 0)
        for first in range(0, N_HEADS // 2, pairs_per_loop):
            pairs = range(first, first + pairs_per_loop)
            chains = [(pr, hh, rc) for pr in pairs for hh in range(2) for rc in range(row_chunks)]
            qh = {}
            for pr in pairs:
                for rc in range(row_chunks):
                    q2 = q_ref[rc * rq:(rc + 1) * rq, pr * 128:(pr + 1) * 128]
                    qh[pr, 0, rc] = jnp.where(low, q2, jnp.zeros_like(q2))
                    qh[pr, 1, rc] = jnp.where(low, jnp.zeros_like(q2), q2)

            def block(kb, carry, band, chains=chains, qh=qh):
                rows = pl.ds(pl.multiple_of(kb * tk, tk), tk)
                out = []
                for (pr, hh, rc), (m, l, acc) in zip(chains, carry):
                    if band is not None and (rc + 1) * rq <= band * tk:
                        out.append((m, l, acc))
                        continue
                    lanes = slice(pr * 128, (pr + 1) * 128)
                    s = _dot_nt(qh[pr, hh, rc], k_ref[rows, lanes]) - ck_ref[2 * pr + hh, pl.ds(kb, 1), :]
                    if band is not None:
                        s = jnp.where(row + rc * rq >= col + band * tk, s, NEG)
                    m_new = jnp.maximum(m, jnp.max(s, axis=-1, keepdims=True))
                    p = jnp.exp(s - m_new)
                    alpha = jnp.exp(m - m_new)
                    l = alpha * l + jnp.sum(p, axis=-1, keepdims=True)
                    acc = alpha * acc + _dot(p.astype(MM), v_ref[rows, lanes])
                    out.append((m_new, l, acc))
                return tuple(out)

            carry = tuple((jnp.full((rq, 1), NEG, F32), jnp.zeros((rq, 1), F32), jnp.zeros((rq, 128), F32))
                          for _ in chains)
            carry = lax.fori_loop(0, qi * n_band, functools.partial(block, band=None), carry)
            for band in range(n_band):
                carry = block(qi * n_band + band, carry, band=band)
            res = {}
            for (pr, hh, rc), (m, l, acc) in zip(chains, carry):
                res[pr, hh, rc] = acc / l
                lse_ref[rc * rq:(rc + 1) * rq, 2 * pr + hh:2 * pr + hh + 1] = m + jnp.log(l)
            for pr in pairs:
                for rc in range(row_chunks):
                    o_ref[rc * rq:(rc + 1) * rq, pr * 128:(pr + 1) * 128] = jnp.where(
                        low, res[pr, 0, rc], res[pr, 1, rc]).astype(MM)

    return pl.pallas_call(
        body, name="fox_fwd", grid=(S // tq,),
        in_specs=[pl.BlockSpec((tq, D_ATT), lambda i: (i, 0)), pl.BlockSpec((S, D_ATT), lambda i: (0, 1)),
                  pl.BlockSpec((S, D_ATT), lambda i: (0, 2)), _resident()],
        out_specs=[_rows(tq, D_ATT), _rows(tq, N_HEADS)],
        out_shape=[jax.ShapeDtypeStruct((S, D_ATT), MM), jax.ShapeDtypeStruct((S, N_HEADS), F32)],
        compiler_params=_params("parallel"),
    )(fqkv, fqkv, fqkv, c_row3)


def _swa_bias(rel_bias_slot, bucket):
    def body(rb_ref, bk_ref, out_ref):
        bk = bk_ref[...]
        for s in range(N_HEADS):
            acc = jnp.where(bk < 0, NEG, 0.0).astype(F32)
            for b in range(N_BUCKETS):
                acc = jnp.where(bk == b, rb_ref[b, s], acc)
            out_ref[s] = acc

    return pl.pallas_call(
        body, name="swa_bias", out_shape=jax.ShapeDtypeStruct((N_HEADS, WINDOW, 2 * WINDOW), F32),
        in_specs=[pl.BlockSpec(memory_space=pltpu.SMEM), _resident()], out_specs=_resident(),
    )(rel_bias_slot, bucket)


def _swa_specs(S):
    q = pl.BlockSpec((WINDOW, D_ATT), lambda n: (n, 0))
    kp = pl.BlockSpec((WINDOW, D_KV), lambda n: (jnp.maximum(n - 1, 0), 4))
    kc = pl.BlockSpec((WINDOW, D_KV), lambda n: (n, 4))
    vp = pl.BlockSpec((WINDOW, D_KV), lambda n: (jnp.maximum(n - 1, 0), 5))
    vc = pl.BlockSpec((WINDOW, D_KV), lambda n: (n, 5))
    return [q, kp, kc, vp, vc]


def _swa_fwd(sqkv, biasm, sinks_slot):
    S = sqkv.shape[0]

    def body(q_ref, kp_ref, kc_ref, vp_ref, vc_ref, bias_ref, sink_ref, o_ref, lse_ref):
        n = pl.program_id(0)
        no_prev = jnp.where(n > 0, 0.0, NEG)
        for j in range(N_HEADS // 2):
            lanes = slice(j * 128, (j + 1) * 128)
            q2 = q_ref[:, lanes]
            res = []
            for hh in range(2):
                s_ = 2 * j + hh
                qh = jnp.where(_head_select((WINDOW, 128), hh), q2, jnp.zeros_like(q2))
                sp = _dot_nt(qh, kp_ref[...]) + bias_ref[s_, :, :WINDOW] + no_prev
                sc = _dot_nt(qh, kc_ref[...]) + bias_ref[s_, :, WINDOW:]
                sink = sink_ref[s_]
                m = jnp.maximum(jnp.maximum(jnp.max(sp, axis=-1, keepdims=True),
                                            jnp.max(sc, axis=-1, keepdims=True)), sink)
                ep = jnp.exp(sp - m)
                ec = jnp.exp(sc - m)
                den = (jnp.sum(ep, axis=-1, keepdims=True) + jnp.sum(ec, axis=-1, keepdims=True)
                       + jnp.exp(sink - m))
                res.append((_dot(ep.astype(MM), vp_ref[...]) + _dot(ec.astype(MM), vc_ref[...])) / den)
                lse_ref[:, s_:s_ + 1] = m + jnp.log(den)
            o_ref[:, lanes] = jnp.where(_head_select((WINDOW, 128), 0), res[0], res[1]).astype(MM)

    return pl.pallas_call(
        body, name="swa_fwd", grid=(S // WINDOW,),
        in_specs=_swa_specs(S) + [_resident(), pl.BlockSpec(memory_space=pltpu.SMEM)],
        out_specs=[_rows(WINDOW, D_ATT), _rows(WINDOW, N_HEADS)],
        out_shape=[jax.ShapeDtypeStruct((S, D_ATT), MM), jax.ShapeDtypeStruct((S, N_HEADS), F32)],
        compiler_params=_params("parallel"),
    )(sqkv, sqkv, sqkv, sqkv, sqkv, biasm, sinks_slot)


def _post_attn(x, fox_o, swa_o, wout_fox, wout_swa, g2, g3, tm):
    S = x.shape[0]

    def body(x_ref, fo_ref, so_ref, wf_ref, ws_ref, g2_ref, g3_ref, mix_ref, h1_ref, m_ref):
        mix = _dot(fo_ref[...], wf_ref[...]) + _dot(so_ref[...], ws_ref[...])
        mix_ref[...] = mix
        n2, _ = _rms(mix)
        h1 = x_ref[...] + n2 * g2_ref[...]
        h1_ref[...] = h1
        n3, _ = _rms(h1)
        m_ref[...] = (n3 * g3_ref[...]).astype(MM)

    return pl.pallas_call(
        body, name="post_attn", grid=(S // tm,),
        in_specs=[_rows(tm, D_MODEL), _rows(tm, D_ATT), _rows(tm, D_ATT), _resident(), _resident(),
                  _const((1, D_MODEL)), _const((1, D_MODEL))],
        out_specs=[_rows(tm, D_MODEL)] * 3,
        out_shape=[jax.ShapeDtypeStruct((S, D_MODEL), F32), jax.ShapeDtypeStruct((S, D_MODEL), F32),
                   jax.ShapeDtypeStruct((S, D_MODEL), MM)],
        compiler_params=_params("parallel"),
    )(x, fox_o, swa_o, wout_fox, wout_swa, g2, g3)


def _mlp_fwd(m, h1, w1, w2, g4, tm):
    S = m.shape[0]

    def body(m_ref, h1_ref, w1_ref, w2_ref, g4_ref, u_ref, y_ref, h2_ref):
        mb = m_ref[...]
        y = jnp.zeros((tm, D_MODEL), F32)
        for j in range(N_DEV):
            cols = slice(j * FF_CHUNK, (j + 1) * FF_CHUNK)
            u = _dot(mb, w1_ref[j])
            u_ref[:, cols] = u.astype(MM)
            y = y + _dot(jnp.square(jnp.maximum(u, 0.0)).astype(MM), w2_ref[cols, :])
        y_ref[...] = y
        n4, _ = _rms(y)
        h2_ref[...] = h1_ref[...] + n4 * g4_ref[...]

    return pl.pallas_call(
        body, name="mlp_fwd", grid=(S // tm,),
        in_specs=[_rows(tm, D_MODEL), _rows(tm, D_MODEL), _resident(), _resident(), _const((1, D_MODEL))],
        out_specs=[_rows(tm, D_FF), _rows(tm, D_MODEL), _rows(tm, D_MODEL)],
        out_shape=[jax.ShapeDtypeStruct((S, D_FF), MM), jax.ShapeDtypeStruct((S, D_MODEL), F32),
                   jax.ShapeDtypeStruct((S, D_MODEL), F32)],
        compiler_params=_params("parallel"),
    )(m, h1, w1, w2, g4)


def _ple_loss(h2, p, target, wg, wple, g5, tm):
    S = h2.shape[0]

    def body(h2_ref, p_ref, t_ref, wg_ref, wp_ref, g5_ref, dh2_ref, dpe_ref, dgl_ref, dg5_ref, loss_ref):
        i = pl.program_id(0)
        h2 = h2_ref[...]
        gate = jax.nn.sigmoid(_dot(h2.astype(MM), wg_ref[...]))
        pe = _dot(p_ref[...].astype(MM), wp_ref[...])
        n5, r5 = _rms(pe * gate)
        g5 = g5_ref[...]
        diff = h2 + n5 * g5 - t_ref[...]
        per_token = jnp.mean(jnp.square(diff), axis=-1, keepdims=True)
        _accumulate(loss_ref, 0.5 * jnp.sum(per_token, axis=0, keepdims=True), i)
        dh3 = diff * (1.0 / D_MODEL)
        de, dg5 = _rms_bwd(dh3, n5, r5, g5)
        _accumulate(dg5_ref, dg5, i)
        dpe_ref[...] = (de * gate).astype(MM)
        dgl = (de * pe * gate * (1.0 - gate)).astype(MM)
        dgl_ref[...] = dgl
        dh2_ref[...] = dh3 + _dot_nt(dgl, wg_ref[...])

    return pl.pallas_call(
        body, name="ple_loss", grid=(S // tm,),
        in_specs=[_rows(tm, D_MODEL), _rows(tm, D_PLE), _rows(tm, D_MODEL), _resident(), _resident(),
                  _const((1, D_MODEL))],
        out_specs=[_rows(tm, D_MODEL), _rows(tm, D_MODEL), _rows(tm, D_MODEL), _const((1, D_MODEL)), _const((1, 1))],
        out_shape=[jax.ShapeDtypeStruct((S, D_MODEL), F32), jax.ShapeDtypeStruct((S, D_MODEL), MM),
                   jax.ShapeDtypeStruct((S, D_MODEL), MM), jax.ShapeDtypeStruct((1, D_MODEL), F32),
                   jax.ShapeDtypeStruct((1, 1), F32)],
        compiler_params=_params("arbitrary"),
    )(h2, p, target, wg, wple, g5)


def _mlp_bwd(dh2, y, h1, u, w1, w2, g4, g3, tm):
    S = dh2.shape[0]

    def body(dh2_ref, y_ref, h1_ref, u_ref, w1_ref, w2_ref, g4_ref, g3_ref,
             dh1_ref, dy_ref, du_ref, dg4_ref, dg3_ref):
        i = pl.program_id(0)
        dh2 = dh2_ref[...]
        n4, r4 = _rms(y_ref[...])
        dy, dg4 = _rms_bwd(dh2, n4, r4, g4_ref[...])
        _accumulate(dg4_ref, dg4, i)
        dyb = dy.astype(MM)
        dy_ref[...] = dyb
        dm = jnp.zeros((tm, D_MODEL), F32)
        for j in range(N_DEV):
            cols = slice(j * FF_CHUNK, (j + 1) * FF_CHUNK)
            dact = _dot_nt(dyb, w2_ref[cols, :])
            du = (dact * (2.0 * jnp.maximum(u_ref[:, cols].astype(F32), 0.0))).astype(MM)
            du_ref[:, cols] = du
            dm = dm + _dot_nt(du, w1_ref[j])
        n3, r3 = _rms(h1_ref[...])
        dx, dg3 = _rms_bwd(dm, n3, r3, g3_ref[...])
        _accumulate(dg3_ref, dg3, i)
        dh1_ref[...] = dh2 + dx

    return pl.pallas_call(
        body, name="mlp_bwd", grid=(S // tm,),
        in_specs=[_rows(tm, D_MODEL), _rows(tm, D_MODEL), _rows(tm, D_MODEL), _rows(tm, D_FF),
                  _resident(), _resident(), _const((1, D_MODEL)), _const((1, D_MODEL))],
        out_specs=[_rows(tm, D_MODEL), _rows(tm, D_MODEL), _rows(tm, D_FF), _const((1, D_MODEL)),
                   _const((1, D_MODEL))],
        out_shape=[jax.ShapeDtypeStruct((S, D_MODEL), F32), jax.ShapeDtypeStruct((S, D_MODEL), MM),
                   jax.ShapeDtypeStruct((S, D_FF), MM), jax.ShapeDtypeStruct((1, D_MODEL), F32),
                   jax.ShapeDtypeStruct((1, D_MODEL), F32)],
        compiler_params=_params("arbitrary"),
    )(dh2, y, h1, u, w1, w2, g4, g3)


def _attn_out_bwd(dh1, mix, fox_o, swa_o, wout_fox, wout_swa, g2, head_rows, head_cols, tm):
    S = dh1.shape[0]

    def body(dh1_ref, mix_ref, fo_ref, so_ref, wf_ref, ws_ref, g2_ref, er_ref, ec_ref,
             dmix_ref, dcat_ref, drow_ref, dcol_ref, dg2_ref):
        i = pl.program_id(0)
        n2, r2 = _rms(mix_ref[...])
        dmix, dg2 = _rms_bwd(dh1_ref[...], n2, r2, g2_ref[...])
        _accumulate(dg2_ref, dg2, i)
        dmb = dmix.astype(MM)
        dmix_ref[...] = dmb
        dfo = _dot_nt(dmb, wf_ref[...]).astype(MM)
        dso = _dot_nt(dmb, ws_ref[...]).astype(MM)
        dcat_ref[:, :D_ATT] = dfo
        dcat_ref[:, D_ATT:] = dso
        hi = lax.Precision.HIGHEST
        prod_f = dfo.astype(F32) * fo_ref[...].astype(F32)
        prod_s = dso.astype(F32) * so_ref[...].astype(F32)
        drow_ref[...] = lax.dot_general(er_ref[...], prod_f, NT, precision=hi, preferred_element_type=F32)
        dcol_ref[...] = jnp.dot(prod_s, ec_ref[...], precision=hi, preferred_element_type=F32)

    return pl.pallas_call(
        body, name="attn_out_bwd", grid=(S // tm,),
        in_specs=[_rows(tm, D_MODEL), _rows(tm, D_MODEL), _rows(tm, D_ATT), _rows(tm, D_ATT), _resident(),
                  _resident(), _const((1, D_MODEL)), _resident(), _resident()],
        out_specs=[_rows(tm, D_MODEL), _rows(tm, D_MODEL), pl.BlockSpec((N_HEADS, tm), lambda i: (0, i)),
                   _rows(tm, N_HEADS), _const((1, D_MODEL))],
        out_shape=[jax.ShapeDtypeStruct((S, D_MODEL), MM), jax.ShapeDtypeStruct((S, D_MODEL), MM),
                   jax.ShapeDtypeStruct((N_HEADS, S), F32), jax.ShapeDtypeStruct((S, N_HEADS), F32),
                   jax.ShapeDtypeStruct((1, D_MODEL), F32)],
        compiler_params=_params("arbitrary"),
    )(dh1, mix, fox_o, swa_o, wout_fox, wout_swa, g2, head_rows, head_cols)


def _fox_bwd(fqkv, dcat, lse_row3, d_row3, c_col, tq, tk, pairs_per_loop=2):
    S = fqkv.shape[0]
    n_blk = S // tk
    n_qblk = S // tq
    n_band = tk // tq

    def body(q_ref, k_ref, v_ref, do_ref, lse_ref, dd_ref, ck_ref, dq_ref, dk_ref, dv_ref, dc_ref, dcq_ref):
        kb = pl.program_id(0)

        @pl.when(kb == 0)
        def _():
            dq_ref[...] = jnp.zeros_like(dq_ref)
            dcq_ref[...] = jnp.zeros_like(dcq_ref)

        key = lax.broadcasted_iota(jnp.int32, (tk, tq), 0)
        qry = lax.broadcasted_iota(jnp.int32, (tk, tq), 1)
        low = _head_select((tk, 128), 0)
        for first in range(0, N_HEADS // 2, pairs_per_loop):
            pairs = range(first, first + pairs_per_loop)
            heads = [(pr, hh) for pr in pairs for hh in range(2)]
            kh, vh, ck = {}, {}, {}
            for pr in pairs:
                k2 = k_ref[:, pr * 128:(pr + 1) * 128]
                v2 = v_ref[:, pr * 128:(pr + 1) * 128]
                zero = jnp.zeros_like(k2)
                kh[pr, 0], kh[pr, 1] = jnp.where(low, k2, zero), jnp.where(low, zero, k2)
                vh[pr, 0], vh[pr, 1] = jnp.where(low, v2, zero), jnp.where(low, zero, v2)
                for hh in range(2):
                    ck[pr, hh] = ck_ref[:, 2 * pr + hh:2 * pr + hh + 1]

            def block(qb, carry, band, pairs=pairs, kh=kh, vh=vh, ck=ck):
                rows = pl.ds(pl.multiple_of(qb * tq, tq), tq)
                out = []
                it = iter(carry)
                for pr in pairs:
                    lanes = slice(pr * 128, (pr + 1) * 128)
                    q2 = q_ref[rows, lanes]
                    do2 = do_ref[rows, lanes]
                    dq = None
                    for hh in range(2):
                        h = 2 * pr + hh
                        dk, dv, dc = next(it)
                        s_t = _dot_nt(kh[pr, hh], q2) - ck[pr, hh]
                        p_t = jnp.exp(s_t - lse_ref[h, pl.ds(qb, 1), :])
                        if band is not None:
                            p_t = jnp.where(qry + band * tq >= key, p_t, 0.0)
                        ds_t = p_t * (_dot_nt(vh[pr, hh], do2) - dd_ref[h, pl.ds(qb, 1), :])
                        dsb = ds_t.astype(MM)
                        dv = dv + _dot(p_t.astype(MM), do2)
                        dk = dk + _dot(dsb, q2)
                        dc = dc - jnp.sum(ds_t, axis=1, keepdims=True)
                        part = _dot_tn(dsb, kh[pr, hh])
                        dq = part if dq is None else dq + part
                        dcq_ref[h, pl.ds(qb, 1), :] += jnp.sum(ds_t, axis=0, keepdims=True)
                        out.append((dk, dv, dc))
                    dq_ref[rows, lanes] += dq
                return tuple(out)

            carry = tuple((jnp.zeros((tk, 128), F32), jnp.zeros((tk, 128), F32), jnp.zeros((tk, 1), F32))
                          for _ in heads)
            for band in range(n_band):
                carry = block(kb * n_band + band, carry, band=band)
            carry = lax.fori_loop((kb + 1) * n_band, n_qblk, functools.partial(block, band=None), carry)
            grads = dict(zip(heads, carry))
            for pr in pairs:
                lanes = slice(pr * 128, (pr + 1) * 128)
                dk_ref[:, lanes] = jnp.where(low, grads[pr, 0][0], grads[pr, 1][0]).astype(MM)
                dv_ref[:, lanes] = jnp.where(low, grads[pr, 0][1], grads[pr, 1][1]).astype(MM)
                for hh in range(2):
                    dc_ref[:, 2 * pr + hh:2 * pr + hh + 1] = grads[pr, hh][2]

        @pl.when(kb == n_blk - 1)
        def _():
            dq_ref[...] = dq_ref[...] * Q_SCALE

    return pl.pallas_call(
        body, name="fox_bwd", grid=(n_blk,),
        in_specs=[pl.BlockSpec((S, D_ATT), lambda i: (0, 0)), pl.BlockSpec((tk, D_ATT), lambda i: (i, 1)),
                  pl.BlockSpec((tk, D_ATT), lambda i: (i, 2)), pl.BlockSpec((S, D_ATT), lambda i: (0, 0)),
                  _resident(), _resident(), _rows(tk, N_HEADS)],
        out_specs=[_const((S, D_ATT)), _rows(tk, D_ATT), _rows(tk, D_ATT), _rows(tk, N_HEADS),
                   _const((N_HEADS, n_qblk, tq))],
        out_shape=[jax.ShapeDtypeStruct((S, D_ATT), F32), jax.ShapeDtypeStruct((S, D_ATT), MM),
                   jax.ShapeDtypeStruct((S, D_ATT), MM), jax.ShapeDtypeStruct((S, N_HEADS), F32),
                   jax.ShapeDtypeStruct((N_HEADS, n_qblk, tq), F32)],
        compiler_params=_params("arbitrary"),
    )(fqkv, fqkv, fqkv, dcat, lse_row3, d_row3, c_col)


def _swa_bwd(sqkv, dcat, biasm, sinks_slot, bucket, lse, d_col):
    S = sqkv.shape[0]
    n_blk = S // WINDOW

    def body(q_ref, kp_ref, kc_ref, vp_ref, vc_ref, do_ref, bias_ref, sink_ref, bk_ref, lse_ref, dd_ref,
             dq_ref, dk_ref, dv_ref, drb_ref, dsink_ref, ds_acc):
        n = pl.program_id(0)

        @pl.when(n == 0)
        def _():
            dk_ref[...] = jnp.zeros_like(dk_ref)
            dv_ref[...] = jnp.zeros_like(dv_ref)
            ds_acc[...] = jnp.zeros_like(ds_acc)
            dsink_ref[...] = jnp.zeros_like(dsink_ref)

        no_prev = jnp.where(n > 0, 0.0, NEG)
        prev = pl.ds(pl.multiple_of(jnp.maximum(n - 1, 0) * WINDOW, WINDOW), WINDOW)
        cur = pl.ds(pl.multiple_of(n * WINDOW, WINDOW), WINDOW)
        lane8 = lax.broadcasted_iota(jnp.int32, (1, N_HEADS), 1)
        dkp = jnp.zeros((WINDOW, D_KV), F32)
        dkc = jnp.zeros((WINDOW, D_KV), F32)
        dvp = jnp.zeros((WINDOW, D_KV), F32)
        dvc = jnp.zeros((WINDOW, D_KV), F32)
        dsink = jnp.zeros((1, N_HEADS), F32)
        for j in range(N_HEADS // 2):
            lanes = slice(j * 128, (j + 1) * 128)
            q2 = q_ref[:, lanes]
            do2 = do_ref[:, lanes]
            dqs = []
            for hh in range(2):
                s_ = 2 * j + hh
                sel = _head_select((WINDOW, 128), hh)
                qh = jnp.where(sel, q2, jnp.zeros_like(q2))
                doh = jnp.where(sel, do2, jnp.zeros_like(do2))
                lse_h = lse_ref[:, s_:s_ + 1]
                dd = dd_ref[:, s_:s_ + 1]
                pp = jnp.exp(_dot_nt(qh, kp_ref[...]) + bias_ref[s_, :, :WINDOW] + no_prev - lse_h)
                pc = jnp.exp(_dot_nt(qh, kc_ref[...]) + bias_ref[s_, :, WINDOW:] - lse_h)
                p_sink = jnp.exp(sink_ref[s_] - lse_h)
                dsp = pp * (_dot_nt(doh, vp_ref[...]) - dd)
                dsc = pc * (_dot_nt(doh, vc_ref[...]) - dd)
                dsink = dsink + jnp.where(lane8 == s_, -jnp.sum(p_sink * dd), 0.0)
                ds_acc[s_, :, :WINDOW] += dsp
                ds_acc[s_, :, WINDOW:] += dsc
                dspb, dscb = dsp.astype(MM), dsc.astype(MM)
                dqs.append(_dot(dspb, kp_ref[...]) + _dot(dscb, kc_ref[...]))
                dkp = dkp + _dot_tn(dspb, qh)
                dkc = dkc + _dot_tn(dscb, qh)
                dvp = dvp + _dot_tn(pp.astype(MM), doh)
                dvc = dvc + _dot_tn(pc.astype(MM), doh)
            dq_ref[:, lanes] = (jnp.where(_head_select((WINDOW, 128), 0), dqs[0], dqs[1]) * Q_SCALE).astype(MM)
        dk_ref[prev, :] += dkp
        dk_ref[cur, :] += dkc
        dv_ref[prev, :] += dvp
        dv_ref[cur, :] += dvc
        dsink_ref[...] += dsink

        @pl.when(n == n_blk - 1)
        def _():
            bk = bk_ref[...]
            rb = lax.broadcasted_iota(jnp.int32, (N_BUCKETS, N_HEADS), 0)
            cb = lax.broadcasted_iota(jnp.int32, (N_BUCKETS, N_HEADS), 1)
            out = jnp.zeros((N_BUCKETS, N_HEADS), F32)
            for s in range(N_HEADS):
                acc = ds_acc[s]
                for b in range(N_BUCKETS):
                    out = out + jnp.where((rb == b) & (cb == s), jnp.sum(jnp.where(bk == b, acc, 0.0)), 0.0)
            drb_ref[...] = out

    do_spec = pl.BlockSpec((WINDOW, D_ATT), lambda n: (n, 1))
    return pl.pallas_call(
        body, name="swa_bwd", grid=(n_blk,),
        in_specs=_swa_specs(S) + [do_spec, _resident(), pl.BlockSpec(memory_space=pltpu.SMEM), _resident(),
                                  _rows(WINDOW, N_HEADS), _rows(WINDOW, N_HEADS)],
        out_specs=[_rows(WINDOW, D_ATT), _const((S, D_KV)), _const((S, D_KV)), _const((N_BUCKETS, N_HEADS)),
                   _const((1, N_HEADS))],
        out_shape=[jax.ShapeDtypeStruct((S, D_ATT), MM), jax.ShapeDtypeStruct((S, D_KV), F32),
                   jax.ShapeDtypeStruct((S, D_KV), F32), jax.ShapeDtypeStruct((N_BUCKETS, N_HEADS), F32),
                   jax.ShapeDtypeStruct((1, N_HEADS), F32)],
        scratch_shapes=[pltpu.VMEM((N_HEADS, WINDOW, 2 * WINDOW), F32)],
        compiler_params=_params("arbitrary"),
    )(sqkv, sqkv, sqkv, sqkv, sqkv, dcat, biasm, sinks_slot, bucket, lse, d_col)


def _pre_attn_bwd(x, dh1, dq_fox, dk_fox, dv_fox, dsq, dsk, dsv, dff_t, wqkv, wswa, wfft, g1, tm):
    S = x.shape[0]

    def body(x_ref, dh1_ref, dq_ref, dk_ref, dv_ref, dsq_ref, dsk_ref, dsv_ref, dff_ref, wqkv_ref, wswa_ref,
             wff_ref, g1_ref, dx_ref, dz_ref, dg1_ref):
        i = pl.program_id(0)
        dq = dq_ref[...].astype(MM)
        dsk = dsk_ref[...].astype(MM)
        dsv = dsv_ref[...].astype(MM)
        dz_ref[:, 0:512] = dq
        dz_ref[:, 512:1024] = dk_ref[...]
        dz_ref[:, 1024:1536] = dv_ref[...]
        dz_ref[:, 1536:2048] = dsq_ref[...]
        dz_ref[:, 2048:2176] = dsk
        dz_ref[:, 2176:2304] = dsv
        da = (_dot_nt(dq, wqkv_ref[:, 0:512]) + _dot_nt(dk_ref[...], wqkv_ref[:, 512:1024])
              + _dot_nt(dv_ref[...], wqkv_ref[:, 1024:1536]) + _dot_nt(dsq_ref[...], wswa_ref[:, 0:512])
              + _dot_nt(dsk, wswa_ref[:, 512:640]) + _dot_nt(dsv, wswa_ref[:, 640:768])
              + _dot_tn(dff_ref[...].astype(MM), wff_ref[...]))
        n1, r1 = _rms(x_ref[...])
        dx, dg1 = _rms_bwd(da, n1, r1, g1_ref[...])
        _accumulate(dg1_ref, dg1, i)
        dx_ref[...] = dh1_ref[...] + dx

    return pl.pallas_call(
        body, name="pre_attn_bwd", grid=(S // tm,),
        in_specs=[_rows(tm, D_MODEL), _rows(tm, D_MODEL), _rows(tm, D_ATT), _rows(tm, D_ATT), _rows(tm, D_ATT),
                  _rows(tm, D_ATT), _rows(tm, D_KV), _rows(tm, D_KV), pl.BlockSpec((16, tm), lambda i: (0, i)),
                  _resident(), _resident(), _resident(), _const((1, D_MODEL))],
        out_specs=[_rows(tm, D_MODEL), _rows(tm, 2304), _const((1, D_MODEL))],
        out_shape=[jax.ShapeDtypeStruct((S, D_MODEL), F32), jax.ShapeDtypeStruct((S, 2304), MM),
                   jax.ShapeDtypeStruct((1, D_MODEL), F32)],
        compiler_params=_params("arbitrary"),
    )(x, dh1, dq_fox, dk_fox, dv_fox, dsq, dsk, dsv, dff_t, wqkv, wswa, wfft, g1)


def _weight_grad(a, b, name, tk, n_chunks=1, relu2=False):
    S, K = a.shape
    N = b.shape[1]
    cn = N // n_chunks

    def body(a_ref, b_ref, out_ref):
        av = a_ref[...]
        if relu2:
            av = jnp.square(jnp.maximum(av.astype(F32), 0.0))
        av = av.astype(MM)
        for j in range(n_chunks):
            val = _dot_tn(av, b_ref[:, j * cn:(j + 1) * cn].astype(MM)).astype(MM)
            if n_chunks > 1:
                out_ref[j] = val
            else:
                out_ref[...] = val

    if n_chunks > 1:
        out_spec = pl.BlockSpec((n_chunks, tk, cn), lambda i: (0, i, 0))
        out_shape = jax.ShapeDtypeStruct((n_chunks, K, cn), MM)
    else:
        out_spec = pl.BlockSpec((tk, N), lambda i: (i, 0))
        out_shape = jax.ShapeDtypeStruct((K, N), MM)
    return pl.pallas_call(
        body, name=name, grid=(K // tk,),
        in_specs=[pl.BlockSpec((S, tk), lambda i: (0, i)), _resident()],
        out_specs=out_spec, out_shape=out_shape, compiler_params=_params("parallel"),
    )(a, b)


def _forget_weight_grad(dff_t, a):
    def body(d_ref, a_ref, out_ref):
        out_ref[...] = _dot(d_ref[...].astype(MM), a_ref[...])

    return pl.pallas_call(
        body, name="forget_weight_grad", out_shape=jax.ShapeDtypeStruct((16, D_MODEL), F32),
        in_specs=[_resident(), _resident()], out_specs=_resident(),
    )(dff_t, a)


def _place():
    return lax.axis_index("x"), lax.axis_index("y"), lax.axis_index("c")


def _all_gather(stacks, name="all_gather_first"):
    n = len(stacks)

    def body(*refs):
        outs = refs[n:2 * n]
        send_sems, recv_sems = refs[2 * n:]
        x, y, c = _place()
        me = 4 * x + 2 * y + c
        copies = []
        for a in range(n):
            k = 0
            for dx in range(2):
                for dy in range(2):
                    for dc in range(2):
                        if dx + dy + dc:
                            copies.append(_remote(outs[a].at[me], outs[a].at[me], send_sems, recv_sems, 7 * a + k,
                                                  (x ^ dx, y ^ dy, c ^ dc)))
                            k += 1
        for cp in copies:
            cp.start()
        for cp in copies:
            cp.wait()

    return pl.pallas_call(
        body, name=name,
        out_shape=[jax.ShapeDtypeStruct(s.shape, s.dtype) for s in stacks],
        in_specs=[HBM_SPEC] * n, out_specs=[HBM_SPEC] * n, input_output_aliases={a: a for a in range(n)},
        scratch_shapes=[pltpu.SemaphoreType.DMA((7 * n,)), pltpu.SemaphoreType.DMA((7 * n,))],
    )(*stacks)


def _all_gather_sequencer(stack):
    ref = jax.new_ref(stack, memory_space=pltpu.MemorySpace.HBM)

    @pl.kernel(mesh=plsc.ScalarSubcoreMesh(axis_name="sequencer", num_cores=1), name="all_gather_sequencer",
               scratch_types=(pltpu.SemaphoreType.DMA((7,)), pltpu.SemaphoreType.DMA((7,))),
               compiler_params=pltpu.CompilerParams(collective_id=1))
    def launch(send_sems, recv_sems):
        x, y, c = _place()
        me = 4 * x + 2 * y + c
        peers = [(x ^ dx, y ^ dy, c ^ dc) for dx in range(2) for dy in range(2) for dc in range(2) if dx + dy + dc]
        barrier = pltpu.get_barrier_semaphore()
        for peer in peers:
            pl.semaphore_signal(barrier, inc=1, device_id=peer, device_id_type=MESH)
        pl.semaphore_wait(barrier, 7)
        copies = [_remote(ref.at[me], ref.at[me], send_sems, recv_sems, k, peer) for k, peer in enumerate(peers)]
        for cp in copies:
            cp.start()
        for cp in copies:
            cp.wait()

    launch()
    return ref[...]


def _exchange_in_chip(grads):
    n = len(grads)

    def body(*refs):
        ins, outs = refs[:n], refs[n:2 * n]
        send_sems, recv_sems = refs[2 * n:]
        x, y, c = _place()
        copies = []
        for a in range(n):
            cp = pltpu.make_async_remote_copy(
                src_ref=ins[a].at[:, 1 - c], dst_ref=outs[a], send_sem=send_sems.at[a], recv_sem=recv_sems.at[a],
                device_id=(x, y, 1 - c), device_id_type=MESH)
            cp.start()
            copies.append(cp)
        for cp in copies:
            cp.wait()

    hbm = pl.BlockSpec(memory_space=pl.ANY)
    views = [g.reshape((4, 2) + g.shape[1:]) for g in grads]
    return pl.pallas_call(
        body, name="exchange_in_chip",
        out_shape=[jax.ShapeDtypeStruct((4,) + g.shape[1:], g.dtype) for g in grads],
        in_specs=[hbm] * n, out_specs=[hbm] * n,
        scratch_shapes=[pltpu.SemaphoreType.DMA((n,)), pltpu.SemaphoreType.DMA((n,))],
    )(*views)


def _chip_sum(grad, other, name):
    _, _, r, cdim = grad.shape
    tr = 256 if r % 256 == 0 else r

    def body(c_ref, g_ref, o_ref, out_ref):
        out_ref[...] = (g_ref[...].astype(F32) + o_ref[...].astype(F32)).astype(out_ref.dtype)

    return pl.pallas_call(
        body, name=name,
        grid_spec=pltpu.PrefetchScalarGridSpec(
            num_scalar_prefetch=1, grid=(4, r // tr),
            in_specs=[pl.BlockSpec((None, None, tr, cdim), lambda k, i, c_ref: (k, c_ref[0], i, 0)),
                      pl.BlockSpec((None, tr, cdim), lambda k, i, c_ref: (k, i, 0))],
            out_specs=pl.BlockSpec((None, tr, cdim), lambda k, i, c_ref: (k, i, 0))),
        out_shape=jax.ShapeDtypeStruct((4, r, cdim), MM),
        compiler_params=_params("parallel", "parallel"),
    )(lax.axis_index("c").astype(jnp.int32).reshape(1), grad, other)


def _exchange_between_chips(sums, small):
    n = len(sums)

    def body(*refs):
        ins, small_in = refs[:n], refs[n]
        outs, small_out = refs[n + 1:2 * n + 1], refs[2 * n + 1]
        send_sems, recv_sems, small_send, small_recv, local_sems = refs[2 * n + 2:]
        x, y, c = _place()
        my_chip = 2 * x + y
        chips = [(1 - x, y), (x, 1 - y), (1 - x, 1 - y)]
        copies = []
        for a in range(n):
            for j, (px, py) in enumerate(chips):
                cp = pltpu.make_async_remote_copy(
                    src_ref=ins[a].at[2 * px + py], dst_ref=outs[a].at[my_chip], send_sem=send_sems.at[3 * a + j],
                    recv_sem=recv_sems.at[3 * a + j], device_id=(px, py, c), device_id_type=MESH)
                cp.start()
                copies.append(cp)
        me = 4 * x + 2 * y + c
        local = pltpu.make_async_copy(small_in, small_out.at[me], local_sems.at[0])
        local.start()
        copies.append(local)
        k = 0
        for dx in range(2):
            for dy in range(2):
                for dc in range(2):
                    if dx + dy + dc == 0:
                        continue
                    peer = (x ^ dx, y ^ dy, c ^ dc)
                    cp = pltpu.make_async_remote_copy(
                        src_ref=small_in, dst_ref=small_out.at[me], send_sem=small_send.at[k],
                        recv_sem=small_recv.at[k], device_id=peer, device_id_type=MESH)
                    cp.start()
                    copies.append(cp)
                    k += 1
        for cp in copies:
            cp.wait()

    hbm = pl.BlockSpec(memory_space=pl.ANY)
    return pl.pallas_call(
        body, name="exchange_between_chips",
        out_shape=[jax.ShapeDtypeStruct(s.shape, s.dtype) for s in sums]
        + [jax.ShapeDtypeStruct((N_DEV,) + small.shape, small.dtype)],
        in_specs=[hbm] * (n + 1), out_specs=[hbm] * (n + 1),
        scratch_shapes=[pltpu.SemaphoreType.DMA((3 * n,)), pltpu.SemaphoreType.DMA((3 * n,)),
                        pltpu.SemaphoreType.DMA((7,)), pltpu.SemaphoreType.DMA((7,)),
                        pltpu.SemaphoreType.DMA((1,))],
    )(*sums, small)


HBM_SPEC = pl.BlockSpec(memory_space=pltpu.HBM)
SEM_SPEC = pl.BlockSpec(memory_space=pltpu.SEMAPHORE)
DATAFLOW = pltpu.SideEffectType.DATAFLOW_SIDE_EFFECTING


def _exchange_start(name, arrays, n_copies, plan):
    n = len(arrays)

    def body(*refs):
        send_sems, recv_sems, token = refs[n], refs[n + 1], refs[2 * n + 2]
        for cp in plan(refs[:n], send_sems, recv_sems):
            cp.start()
        token[...] = jnp.zeros_like(token)

    out = pl.pallas_call(
        body, name=name,
        out_shape=(pltpu.SemaphoreType.DMA((n_copies,)), pltpu.SemaphoreType.DMA((n_copies,)),
                   *[pltpu.HBM(a.shape, a.dtype) for a in arrays], jax.ShapeDtypeStruct((1, D_MODEL), F32)),
        in_specs=[HBM_SPEC] * n,
        out_specs=(SEM_SPEC, SEM_SPEC, *[HBM_SPEC] * n, pl.BlockSpec(memory_space=pltpu.VMEM)),
        input_output_aliases={i: 2 + i for i in range(n)},
        compiler_params=pltpu.CompilerParams(has_side_effects=DATAFLOW),
    )(*[pltpu.with_memory_space_constraint(a, pltpu.HBM) for a in arrays])
    return (out[0], out[1]), list(out[2:2 + n]), out[2 + n]


def _exchange_wait(name, arrays, sems, after, plan):
    n = len(arrays)

    def body(*refs):
        send_sems, recv_sems = refs[n], refs[n + 1]
        for cp in plan(refs[:n], send_sems, recv_sems):
            cp.wait_send()
            cp.wait_recv()

    out = pl.pallas_call(
        body, name=name, out_shape=[pltpu.HBM(a.shape, a.dtype) for a in arrays],
        in_specs=[HBM_SPEC] * n + [SEM_SPEC, SEM_SPEC, pl.BlockSpec(memory_space=pl.ANY)],
        out_specs=[HBM_SPEC] * n, input_output_aliases={i: i for i in range(n)},
        compiler_params=pltpu.CompilerParams(has_side_effects=DATAFLOW),
    )(*arrays, sems[0], sems[1], after)
    return list(out)


def _remote(src, dst, send_sems, recv_sems, k, to):
    return pltpu.make_async_remote_copy(src_ref=src, dst_ref=dst, send_sem=send_sems.at[k], recv_sem=recv_sems.at[k],
                                        device_id=to, device_id_type=MESH)


def _plan_gather_direct(refs, send_sems, recv_sems):
    x, y, c = _place()
    me = 4 * x + 2 * y + c
    peers = [(x, y, 1 - c), (1 - x, y, c), (x, 1 - y, c), (1 - x, 1 - y, c)]
    return [_remote(ref.at[me], ref.at[me], send_sems, recv_sems, 4 * a + k, peer)
            for a, ref in enumerate(refs) for k, peer in enumerate(peers)]


def _plan_gather_pass_on(refs, send_sems, recv_sems):
    x, y, c = _place()
    chips = [(1 - x, y), (x, 1 - y), (1 - x, 1 - y)]
    return [_remote(ref.at[4 * px + 2 * py + c], ref.at[4 * px + 2 * py + c], send_sems, recv_sems, 3 * a + k,
                    (x, y, 1 - c))
            for a, ref in enumerate(refs) for k, (px, py) in enumerate(chips)]


def _plan_in_chip(refs, send_sems, recv_sems):
    n = len(refs) // 2
    x, y, c = _place()
    return [_remote(refs[a].at[:, 1 - c], refs[n + a], send_sems, recv_sems, a, (x, y, 1 - c)) for a in range(n)]


def _plan_between_chips(refs, send_sems, recv_sems):
    n = len(refs) // 2
    x, y, c = _place()
    chips = [(1 - x, y), (x, 1 - y), (1 - x, 1 - y)]
    return [_remote(refs[a].at[2 * px + py], refs[n + a].at[2 * x + y], send_sems, recv_sems, 3 * a + k, (px, py, c))
            for a in range(n) for k, (px, py) in enumerate(chips)]


def _adamw_math(w, g, m, v):
    m = ADAM_B1 * m + (1.0 - ADAM_B1) * g
    v = ADAM_B2 * v + (1.0 - ADAM_B2) * jnp.square(g)
    m_hat = m / (1.0 - ADAM_B1 ** ADAM_STEP)
    v_hat = v / (1.0 - ADAM_B2 ** ADAM_STEP)
    delta = -ADAM_LR * (m_hat / (jnp.sqrt(v_hat) + ADAM_EPS) + ADAM_WD * w)
    return delta, m, v


def _adamw(parts, w, m, v, name):
    n_parts, r, cdim = parts.shape
    tr = 256 if r % 256 == 0 else r

    def body(p_ref, w_ref, m_ref, v_ref, g_out, d_out, m_out, v_out):
        g = p_ref[0].astype(F32)
        for k in range(1, n_parts):
            g = g + p_ref[k].astype(F32)
        delta, m_new, v_new = _adamw_math(w_ref[...], g, m_ref[...], v_ref[...])
        g_out[...] = g
        d_out[...] = delta
        m_out[...] = m_new
        v_out[...] = v_new

    blk = pl.BlockSpec((tr, cdim), lambda i: (i, 0))
    return pl.pallas_call(
        body, name=name, grid=(r // tr,),
        in_specs=[pl.BlockSpec((n_parts, tr, cdim), lambda i: (0, i, 0)), blk, blk, blk],
        out_specs=[blk] * 4, out_shape=[jax.ShapeDtypeStruct((r, cdim), F32)] * 4,
        compiler_params=_params("parallel"),
    )(parts, w, m, v)


def _adamw_chips(parts, sums, w, m, v, name):
    _, r, cdim = parts.shape
    tr = 256 if r % 256 == 0 else r

    def body(chip_ref, p_ref, own_ref, w_ref, m_ref, v_ref, g_out, d_out, m_out, v_out):
        g = None
        for k in range(4):
            term = jnp.where(chip_ref[0] == k, own_ref[...], p_ref[k]).astype(F32)
            g = term if g is None else g + term
        delta, m_new, v_new = _adamw_math(w_ref[...], g, m_ref[...], v_ref[...])
        g_out[...] = g
        d_out[...] = delta
        m_out[...] = m_new
        v_out[...] = v_new

    blk = pl.BlockSpec((tr, cdim), lambda i, chip: (i, 0))
    my_chip = (2 * lax.axis_index("x") + lax.axis_index("y")).astype(jnp.int32).reshape(1)
    return pl.pallas_call(
        body, name=name,
        grid_spec=pltpu.PrefetchScalarGridSpec(
            num_scalar_prefetch=1, grid=(r // tr,),
            in_specs=[pl.BlockSpec((4, tr, cdim), lambda i, chip: (0, i, 0)),
                      pl.BlockSpec((None, tr, cdim), lambda i, chip: (chip[0], i, 0)), blk, blk, blk],
            out_specs=[blk] * 4),
        out_shape=[jax.ShapeDtypeStruct((r, cdim), F32)] * 4,
        compiler_params=_params("parallel"),
    )(my_chip, parts, sums, w, m, v)


class _NoExchange:
    def __init__(self, weights):
        self.weights = weights

    def before_pre_attn(self, g1):
        return g1

    def after_fox_fwd(self, fox_o, sinks_slot):
        return sinks_slot

    def after_attention(self, swa_o):
        return self.weights

    def after_mlp_grads(self, grads, g2):
        return g2

    def after_attn_out_bwd(self, dmix, d_row3):
        return d_row3


def _slot_order(t, axis):
    shp = t.shape
    t = t.reshape(shp[:axis] + (N_HEADS, shp[axis] // N_HEADS) + shp[axis + 1:])
    t = jnp.take(t, np.array(SLOT_HEAD), axis=axis)
    return t.reshape(shp)


def _head_order(t, axis):
    shp = t.shape
    t = t.reshape(shp[:axis] + (N_HEADS, shp[axis] // N_HEADS) + shp[axis + 1:])
    t = jnp.take(t, np.array(HEAD_SLOT), axis=axis)
    return t.reshape(shp)


def _forward_backward(x, p, target, win, hooks, b_forget, rel_bias, sinks, g1, g2, g3, g4, g5):
    S = x.shape[0]
    tm = 256
    t = 256
    wqkv = win[:, :3 * D_ATT]
    wfft = jnp.pad(win[:, 3 * D_ATT:3 * D_ATT + N_HEADS].T, ((0, 8), (0, 0)))
    q0 = 3 * D_ATT + N_HEADS
    wswa = jnp.concatenate([_slot_order(win[:, q0:q0 + D_ATT], 1), win[:, q0 + D_ATT:]], axis=1)
    bcol = jnp.pad(b_forget.reshape(N_HEADS, 1), ((0, 8), (0, 0)))
    rel_bias_slot = rel_bias[:, np.array(SLOT_HEAD)]
    sinks_slot = sinks.reshape(N_HEADS)[np.array(SLOT_HEAD)]
    bucket = jnp.asarray(_swa_bucket_map())

    a, fqkv, sqkv, fft = _pre_attn(x, hooks.before_pre_attn(g1), wqkv, wswa, wfft, tm)
    c_row = _forget_cumsum(fft, bcol)
    c_col = c_row[:N_HEADS].T
    c_row3 = c_row[:N_HEADS].reshape(N_HEADS, S // t, t)
    fox_o, fox_lse = _fox_fwd(fqkv, c_row3, tq=512, tk=t)
    biasm = _swa_bias(rel_bias_slot, bucket)
    sinks_slot = hooks.after_fox_fwd(fox_o, sinks_slot)
    swa_o, swa_lse = _swa_fwd(sqkv, biasm, sinks_slot)
    wout, w1, w2, wple, wg = hooks.after_attention(swa_o)
    wout_fox = wout[:D_ATT]
    wout_swa = _slot_order(wout[D_ATT:], 0)
    mix, h1, m = _post_attn(x, fox_o, swa_o, wout_fox, wout_swa, g2, g3, tm)
    u, y, h2 = _mlp_fwd(m, h1, w1, w2, g4, tm)
    dh2, dpe, dgl, dg5, loss = _ple_loss(h2, p, target, wg, wple, g5, tm)

    d_wple = _weight_grad(p, dpe, "grad_w_ple", tk=D_PLE, n_chunks=N_DEV)
    d_wg = _weight_grad(h2, dgl, "grad_w_ple_gate", tk=256)
    dh1, dy, du, dg4, dg3 = _mlp_bwd(dh2, y, h1, u, w1, w2, g4, g3, tm)
    d_w2 = _weight_grad(u, dy, "grad_w_ff2", tk=256, relu2=True)
    d_w1 = _weight_grad(m, du, "grad_w_ff1", tk=256, n_chunks=N_DEV)
    early = dict(w_ff1=d_w1, w_ff2=d_w2.reshape(N_DEV, FF_CHUNK, D_MODEL), w_ple=d_wple,
                 w_ple_gate=d_wg.reshape(N_DEV, D_MODEL // N_DEV, D_MODEL))
    g2_late = hooks.after_mlp_grads(early, g2)

    head = np.arange(D_ATT) // HEAD_DIM
    head_rows = jnp.asarray((head[None, :] == np.arange(N_HEADS)[:, None]).astype(np.float32))
    dmix, dcat, d_row, d_col, dg2 = _attn_out_bwd(dh1, mix, fox_o, swa_o, wout_fox, wout_swa, g2_late,
                                                  head_rows, head_rows.T, tm)
    d_wout_fox = _weight_grad(fox_o, dmix, "grad_w_out_fox", tk=256)
    d_wout_swa = _weight_grad(swa_o, dmix, "grad_w_out_swa", tk=256)

    lse_row3 = fox_lse.T.reshape(N_HEADS, S // t, t)
    d_row3 = hooks.after_attn_out_bwd(dmix, d_row.reshape(N_HEADS, S // t, t))
    dq_fox, dk_fox, dv_fox, dc_col, dcq = _fox_bwd(fqkv, dcat, lse_row3, d_row3, c_col, tq=t, tk=512)
    dsq, dsk, dsv, d_rb_slot, d_sink_slot = _swa_bwd(sqkv, dcat, biasm, sinks_slot, bucket, swa_lse, d_col)
    dc_row = jnp.pad(dc_col.T + dcq.reshape(N_HEADS, S), ((0, 8), (0, 0)))
    dff_t, db = _forget_bwd(dc_row, fft, bcol)
    grad_x, dz, dg1 = _pre_attn_bwd(x, dh1, dq_fox, dk_fox, dv_fox, dsq, dsk, dsv, dff_t, wqkv, wswa, wfft, g1, tm)
    d_wmain = _weight_grad(a, dz, "grad_w_in", tk=256)
    d_wff_t = _forget_weight_grad(dff_t, a)

    d_win = jnp.concatenate([
        d_wmain[:, :3 * D_ATT], d_wff_t[:N_HEADS].T.astype(MM), _head_order(d_wmain[:, 3 * D_ATT:4 * D_ATT], 1),
        d_wmain[:, 4 * D_ATT:]], axis=1)
    d_win = d_win.reshape(D_MODEL, N_DEV, D_IN // N_DEV).transpose(1, 0, 2).reshape((N_DEV,) + W_IN_FLAT)
    d_wout = jnp.concatenate([d_wout_fox, _head_order(d_wout_swa, 0)], axis=0).reshape(N_DEV, D_MODEL // N_DEV, D_MODEL)
    big = dict(early, w_in=d_win, w_out=d_wout)
    small = dict(b_forget=db[:N_HEADS].reshape(1, N_HEADS), rel_bias=d_rb_slot[:, np.array(HEAD_SLOT)],
                 swa_sinks=d_sink_slot[:, np.array(HEAD_SLOT)], g_attn_pre=dg1, g_attn_post=dg2, g_ff_pre=dg3,
                 g_ff_post=dg4, g_ple_post=dg5)
    return loss, grad_x, big, small, dz


BIG = ("w_in", "w_out", "w_ff1", "w_ff2", "w_ple", "w_ple_gate")
SMALL_ROWS = ("g_attn_pre", "g_attn_post", "g_ff_pre", "g_ff_post", "g_ple_post")
WEIGHTS = ("w_in", "b_forget", "w_out", "rel_bias", "swa_sinks", "g_attn_pre", "g_attn_post", "w_ff1", "w_ff2",
           "g_ff_pre", "g_ff_post", "w_ple", "w_ple_gate", "g_ple_post")


EARLY = ("w_ff1", "w_ff2", "w_ple", "w_ple_gate")
LATE = ("w_in", "w_out")


class _Overlap:
    def __init__(self, later):
        self.later = later

    def before_pre_attn(self, g1):
        self.gather_sems, self.later, token = _exchange_start("gather_rest_start", self.later, 4 * 5, _plan_gather_direct)
        return g1 + token

    def after_fox_fwd(self, fox_o, sinks_slot):
        later = _exchange_wait("gather_rest_wait", self.later, self.gather_sems, fox_o, _plan_gather_direct)
        self.pass_sems, self.later, token = _exchange_start("gather_pass_on_start", later, 3 * 5, _plan_gather_pass_on)
        return sinks_slot + token[0, :N_HEADS]

    def after_attention(self, swa_o):
        wout_g, w1_g, w2_g, wple_g, wg_g = _exchange_wait("gather_pass_on_wait", self.later, self.pass_sems, swa_o,
                                                         _plan_gather_pass_on)
        return (wout_g.reshape(D_MODEL, D_MODEL), w1_g, w2_g.reshape(D_FF, D_MODEL),
                jnp.moveaxis(wple_g, 0, 1).reshape(D_PLE, D_MODEL), wg_g.reshape(D_MODEL, D_MODEL))

    def after_mlp_grads(self, grads, g2):
        views = [grads[k].reshape((4, 2) + grads[k].shape[1:]) for k in EARLY]
        lands = [lax.empty((4,) + grads[k].shape[1:], MM) for k in EARLY]
        self.in_chip_sems, self.in_chip, token = _exchange_start("grads_in_chip_start", views + lands, len(EARLY),
                                                                 _plan_in_chip)
        return g2 + token

    def after_attn_out_bwd(self, dmix, d_row3):
        arrays = _exchange_wait("grads_in_chip_wait", self.in_chip, self.in_chip_sems, dmix, _plan_in_chip)
        n = len(EARLY)
        sums = [_chip_sum(arrays[a], arrays[n + a], "chip_sum_" + k) for a, k in enumerate(EARLY)]
        lands = [lax.empty(s.shape, s.dtype) for s in sums]
        self.between_sems, self.between, token = _exchange_start("grads_between_chips_start", sums + lands, 3 * n,
                                                                 _plan_between_chips)
        return d_row3 + token[0, 0]

    def finish(self, after):
        arrays = _exchange_wait("grads_between_chips_wait", self.between, self.between_sems, after,
                                _plan_between_chips)
        n = len(EARLY)
        self.sums = arrays[:n]
        return arrays[n:]


def _pack_small(t):
    rows = [t[k].reshape(1, D_MODEL) for k in SMALL_ROWS]
    misc = jnp.concatenate([t["b_forget"].reshape(-1), t["swa_sinks"].reshape(-1), t["rel_bias"].reshape(-1)])
    rows.append(jnp.pad(misc, (0, D_MODEL - misc.shape[0])).reshape(1, D_MODEL))
    rows.append(jnp.pad(t["loss"].reshape(-1), (0, D_MODEL - 1)).reshape(1, D_MODEL))
    rows.append(jnp.zeros((1, D_MODEL), F32))
    return jnp.concatenate(rows, axis=0).astype(F32)


def _unpack_small(blk):
    out = {k: blk[i].reshape(1, D_MODEL) for i, k in enumerate(SMALL_ROWS)}
    misc = blk[len(SMALL_ROWS)]
    out["b_forget"] = misc[:N_HEADS].reshape(1, N_HEADS)
    out["swa_sinks"] = misc[N_HEADS:2 * N_HEADS].reshape(1, N_HEADS)
    out["rel_bias"] = misc[2 * N_HEADS:2 * N_HEADS + N_BUCKETS * N_HEADS].reshape(N_BUCKETS, N_HEADS)
    out["loss"] = blk[len(SMALL_ROWS) + 1, 0]
    return out


def kernel(x, p, w_in, b_forget, w_out, rel_bias, swa_sinks, g_attn_pre, g_attn_post, w_ff1, w_ff2, g_ff_pre, g_ff_post, w_ple, w_ple_gate, g_ple_post, loss_target, m_w_in, m_b_forget, m_w_out, m_rel_bias, m_swa_sinks, m_g_attn_pre, m_g_attn_post, m_w_ff1, m_w_ff2, m_g_ff_pre, m_g_ff_post, m_w_ple, m_w_ple_gate, m_g_ple_post, v_w_in, v_b_forget, v_w_out, v_rel_bias, v_swa_sinks, v_g_attn_pre, v_g_attn_post, v_w_ff1, v_w_ff2, v_g_ff_pre, v_g_ff_post, v_w_ple, v_w_ple_gate, v_g_ple_post):
    w = dict(w_in=w_in, b_forget=b_forget, w_out=w_out, rel_bias=rel_bias, swa_sinks=swa_sinks,
             g_attn_pre=g_attn_pre, g_attn_post=g_attn_post, w_ff1=w_ff1, w_ff2=w_ff2, g_ff_pre=g_ff_pre,
             g_ff_post=g_ff_post, w_ple=w_ple, w_ple_gate=w_ple_gate, g_ple_post=g_ple_post)
    mom = dict(w_in=m_w_in, b_forget=m_b_forget, w_out=m_w_out, rel_bias=m_rel_bias, swa_sinks=m_swa_sinks,
               g_attn_pre=m_g_attn_pre, g_attn_post=m_g_attn_post, w_ff1=m_w_ff1, w_ff2=m_w_ff2,
               g_ff_pre=m_g_ff_pre, g_ff_post=m_g_ff_post, w_ple=m_w_ple, w_ple_gate=m_w_ple_gate,
               g_ple_post=m_g_ple_post)
    var = dict(w_in=v_w_in, b_forget=v_b_forget, w_out=v_w_out, rel_bias=v_rel_bias, swa_sinks=v_swa_sinks,
               g_attn_pre=v_g_attn_pre, g_attn_post=v_g_attn_post, w_ff1=v_w_ff1, w_ff2=v_w_ff2,
               g_ff_pre=v_g_ff_pre, g_ff_post=v_g_ff_post, w_ple=v_w_ple, w_ple_gate=v_w_ple_gate,
               g_ple_post=v_g_ple_post)

    flat = lambda t, k: t.reshape(W_IN_FLAT) if k == "w_in" else t
    me = 4 * lax.axis_index("x") + 2 * lax.axis_index("y") + lax.axis_index("c")

    def stack(block):
        return lax.dynamic_update_slice_in_dim(lax.empty((N_DEV,) + block.shape, block.dtype), block[None], me, 0)

    stacks = [stack(flat(w[k][0].astype(MM), k)) for k in BIG]
    win_g, later = _all_gather_sequencer(stacks[0]), stacks[1:]
    win = jnp.moveaxis(win_g.reshape(N_DEV, D_MODEL, D_IN // N_DEV), 0, 1).reshape(D_MODEL, D_IN)
    hooks = _Overlap(later)
    loss, grad_x, big, small, last = _forward_backward(
        x[0], p[0, 0], loss_target[0], win, hooks, b_forget, rel_bias, swa_sinks, g_attn_pre, g_attn_post,
        g_ff_pre, g_ff_post, g_ple_post)
    early_parts = hooks.finish(last)

    grads = [big[k] for k in LATE]
    views = [g.reshape((4, 2) + g.shape[1:]) for g in grads]
    others = _exchange_in_chip(grads)
    sums = [_chip_sum(v_, o, "chip_sum_" + k) for v_, o, k in zip(views, others, LATE)]
    small["loss"] = loss
    *parts, small_all = _exchange_between_chips(sums, _pack_small(small))

    out_g, out_d, out_m, out_v = {}, {}, {}, {}
    for k, part, own in zip(LATE, parts, sums):
        shape = w[k].shape
        g, d, m_new, v_new = _adamw_chips(part, own, flat(w[k][0], k), flat(mom[k][0], k), flat(var[k][0], k),
                                          "adamw_" + k)
        out_g[k], out_d[k], out_m[k], out_v[k] = g.reshape(shape), d.reshape(shape), m_new.reshape(shape), v_new.reshape(shape)
    for k, part, own in zip(EARLY, early_parts, hooks.sums):
        g, d, m_new, v_new = _adamw_chips(part, own, w[k][0], mom[k][0], var[k][0], "adamw_" + k)
        out_g[k], out_d[k], out_m[k], out_v[k] = g[None], d[None], m_new[None], v_new[None]
    rep = {k: w[k] for k in w if k not in BIG}
    rep["loss"] = jnp.zeros((), F32)
    rep_m = {k: mom[k] for k in mom if k not in BIG}
    rep_m["loss"] = jnp.zeros((), F32)
    rep_v = {k: var[k] for k in var if k not in BIG}
    rep_v["loss"] = jnp.ones((), F32)
    g_s, d_s, m_s, v_s = _adamw(small_all, _pack_small(rep), _pack_small(rep_m), _pack_small(rep_v), "adamw_small")
    g_s, d_s, m_s, v_s = _unpack_small(g_s), _unpack_small(d_s), _unpack_small(m_s), _unpack_small(v_s)
    for k in w:
        if k not in BIG:
            out_g[k], out_d[k], out_m[k], out_v[k] = g_s[k], d_s[k], m_s[k], v_s[k]
    return (g_s["loss"], grad_x[None], *[out_g[k] for k in WEIGHTS], *[out_d[k] for k in WEIGHTS],
            *[out_m[k] for k in WEIGHTS], *[out_v[k] for k in WEIGHTS])
```

```python
import functools

import numpy as np
import jax
import jax.numpy as jnp
from jax import lax
from jax.experimental import pallas as pl
from jax.experimental.pallas import tpu as pltpu
from jax.experimental.pallas import tpu_sc as plsc

F32 = jnp.float32
MM = jnp.bfloat16

D_MODEL = 1024
HEAD_DIM = 64
N_HEADS = 8
D_ATT = N_HEADS * HEAD_DIM
D_KV = 128
D_FF = 4096
D_PLE = 256
D_IN = 3 * D_ATT + N_HEADS + D_ATT + 2 * D_KV
N_DEV = 8
FF_CHUNK = D_FF // N_DEV
WINDOW = 128
N_BUCKETS = 32
MAX_DISTANCE = 128
RMS_EPS = 1e-6
Q_SCALE = HEAD_DIM ** -0.5
NEG = -1e30

ADAM_LR = 0.001
ADAM_B1 = 0.9
ADAM_B2 = 0.999
ADAM_EPS = 1e-08
ADAM_WD = 0.01
ADAM_STEP = 10

SLOT_HEAD = (0, 4, 1, 5, 2, 6, 3, 7)
HEAD_SLOT = (0, 2, 4, 6, 1, 3, 5, 7)

VMEM_LIMIT = 56 * 1024 * 1024
MESH = pl.DeviceIdType.MESH

NT = (((1,), (1,)), ((), ()))
TN = (((0,), (0,)), ((), ()))


def _params(*semantics):
    return pltpu.CompilerParams(dimension_semantics=semantics, vmem_limit_bytes=VMEM_LIMIT)


def _resident():
    return pl.BlockSpec(memory_space=pltpu.VMEM)


def _rows(tm, width):
    return pl.BlockSpec((tm, width), lambda i: (i, 0))


def _const(shape):
    return pl.BlockSpec(shape, lambda i: (0,) * len(shape))


def _dot(a, b):
    return jnp.dot(a, b, preferred_element_type=F32)


def _dot_nt(a, b):
    return lax.dot_general(a, b, NT, preferred_element_type=F32)


def _dot_tn(a, b):
    return lax.dot_general(a, b, TN, preferred_element_type=F32)


def _rms(xf):
    r = lax.rsqrt(jnp.mean(xf * xf, axis=-1, keepdims=True) + RMS_EPS)
    return xf * r, r


def _rms_bwd(dout, n, r, g):
    dg = jnp.sum(dout * n, axis=0, keepdims=True)
    dn = dout * g
    dx = r * (dn - n * jnp.mean(dn * n, axis=-1, keepdims=True))
    return dx, dg


def _accumulate(ref, value, step):
    @pl.when(step == 0)
    def _():
        ref[...] = value

    @pl.when(step != 0)
    def _():
        ref[...] += value


def _t5_bucket(n):
    max_exact = N_BUCKETS // 2
    large = max_exact + (np.log(np.maximum(n, 1) / max_exact) / np.log(MAX_DISTANCE / max_exact)
                         * (N_BUCKETS - max_exact)).astype(np.int32)
    large = np.minimum(large, N_BUCKETS - 1)
    return np.where(n < max_exact, n, large).astype(np.int32)


def _swa_bucket_map():
    i = np.arange(WINDOW)[:, None]
    j = np.arange(2 * WINDOW)[None, :]
    dist = i + WINDOW - j
    ok = (dist >= 0) & (dist < WINDOW)
    return np.where(ok, _t5_bucket(np.clip(dist, 0, None)), -1).astype(np.int32)


WT_FOX = 0
WT_FF = 3 * D_ATT
WT_SQ = WT_FF + 16
WT_SKV = WT_SQ + D_ATT
WT_ROWS = WT_SKV + 2 * D_KV


def _pre_attn(x, g1, wt, tm):
    S = x.shape[0]

    def body(x_ref, g_ref, wt_ref, a_ref, fqkv_ref, sqkv_ref, fft_ref):
        n, _ = _rms(x_ref[...])
        a = (n * g_ref[...]).astype(MM)
        a_ref[...] = a
        fqkv_ref[:, :D_ATT] = (_dot_nt(a, wt_ref[WT_FOX:WT_FOX + D_ATT]) * Q_SCALE).astype(MM)
        fqkv_ref[:, D_ATT:] = _dot_nt(a, wt_ref[WT_FOX + D_ATT:WT_FF]).astype(MM)
        sqkv_ref[:, :D_ATT] = (_dot_nt(a, wt_ref[WT_SQ:WT_SKV]) * Q_SCALE).astype(MM)
        sqkv_ref[:, D_ATT:] = _dot_nt(a, wt_ref[WT_SKV:WT_ROWS]).astype(MM)
        fft_ref[...] = _dot_nt(wt_ref[WT_FF:WT_SQ], a)

    return pl.pallas_call(
        body, name="pre_attn", grid=(S // tm,),
        in_specs=[_rows(tm, D_MODEL), _const((1, D_MODEL)), _resident()],
        out_specs=[_rows(tm, D_MODEL), _rows(tm, 3 * D_ATT), _rows(tm, D_ATT + 2 * D_KV),
                   pl.BlockSpec((16, tm), lambda i: (0, i))],
        out_shape=[jax.ShapeDtypeStruct((S, D_MODEL), MM), jax.ShapeDtypeStruct((S, 3 * D_ATT), MM),
                   jax.ShapeDtypeStruct((S, D_ATT + 2 * D_KV), MM), jax.ShapeDtypeStruct((16, S), F32)],
        compiler_params=_params("parallel"),
    )(x, g1, wt)


def _lane_scan(v, reverse):
    S = v.shape[1]
    lane = lax.broadcasted_iota(jnp.int32, v.shape, 1)
    k = 1
    while k < S:
        if reverse:
            v = v + jnp.where(lane < S - k, pltpu.roll(v, S - k, axis=1), 0.0)
        else:
            v = v + jnp.where(lane >= k, pltpu.roll(v, k, axis=1), 0.0)
        k *= 2
    return v


def _forget_cumsum(fft, bcol):
    def body(f_ref, b_ref, c_ref):
        z = f_ref[...] + b_ref[...]
        log_f = jnp.minimum(z, 0.0) - jnp.log1p(jnp.exp(-jnp.abs(z)))
        c_ref[...] = _lane_scan(log_f, reverse=False)

    return pl.pallas_call(
        body, name="forget_cumsum", out_shape=jax.ShapeDtypeStruct(fft.shape, F32),
        in_specs=[_resident(), _resident()], out_specs=_resident(),
    )(fft, bcol)


def _forget_bwd(dc_row, fft, bcol):
    def body(dc_ref, f_ref, b_ref, dff_ref, db_ref):
        z = f_ref[...] + b_ref[...]
        dlog_f = _lane_scan(dc_ref[...], reverse=True)
        dff = dlog_f * (1.0 / (1.0 + jnp.exp(z)))
        dff_ref[...] = dff
        db_ref[...] = jnp.sum(dff, axis=1, keepdims=True)

    return pl.pallas_call(
        body, name="forget_bwd",
        out_shape=[jax.ShapeDtypeStruct(fft.shape, F32), jax.ShapeDtypeStruct((fft.shape[0], 1), F32)],
        in_specs=[_resident()] * 3, out_specs=[_resident()] * 2,
    )(dc_row, fft, bcol)


def _head_select(shape, upper):
    lane = lax.broadcasted_iota(jnp.int32, shape, 1)
    return lane >= HEAD_DIM if upper else lane < HEAD_DIM


def _fox_fwd(fqkv, c_row3, tq, tk, pairs_per_loop=2, row_chunks=1):
    S = fqkv.shape[0]
    rq = tq // row_chunks
    n_band = tq // tk

    def body(q_ref, k_ref, v_ref, ck_ref, o_ref, lse_ref):
        qi = pl.program_id(0)
        row = lax.broadcasted_iota(jnp.int32, (rq, tk), 0)
        col = lax.broadcasted_iota(jnp.int32, (rq, tk), 1)
        low = _head_select((rq, 128), 0)
        for first in range(0, N_HEADS // 2, pairs_per_loop):
            pairs = range(first, first + pairs_per_loop)
            chains = [(pr, hh, rc) for pr in pairs for hh in range(2) for rc in range(row_chunks)]
            qh = {}
            for pr in pairs:
                for rc in range(row_chunks):
                    q2 = q_ref[rc * rq:(rc + 1) * rq, pr * 128:(pr + 1) * 128]
                    qh[pr, 0, rc] = jnp.where(low, q2, jnp.zeros_like(q2))
                    qh[pr, 1, rc] = jnp.where(low, jnp.zeros_like(q2), q2)

            def block(kb, carry, band, chains=chains, qh=qh):
                rows = pl.ds(pl.multiple_of(kb * tk, tk), tk)
                out = []
                for (pr, hh, rc), (m, l, acc) in zip(chains, carry):
                    if band is not None and (rc + 1) * rq <= band * tk:
                        out.append((m, l, acc))
                        continue
                    lanes = slice(pr * 128, (pr + 1) * 128)
                    s = _dot_nt(qh[pr, hh, rc], k_ref[rows, lanes]) - ck_ref[2 * pr + hh, pl.ds(kb, 1), :]
                    if band is not None:
                        s = jnp.where(row + rc * rq >= col + band * tk, s, NEG)
                    m_new = jnp.maximum(m, jnp.max(s, axis=-1, keepdims=True))
                    p = jnp.exp(s - m_new)
                    alpha = jnp.exp(m - m_new)
                    l = alpha * l + jnp.sum(p, axis=-1, keepdims=True)
                    acc = alpha * acc + _dot(p.astype(MM), v_ref[rows, lanes])
                    out.append((m_new, l, acc))
                return tuple(out)

            carry = tuple((jnp.full((rq, 1), NEG, F32), jnp.zeros((rq, 1), F32), jnp.zeros((rq, 128), F32))
                          for _ in chains)
            carry = lax.fori_loop(0, qi * n_band, functools.partial(block, band=None), carry)
            for band in range(n_band):
                carry = block(qi * n_band + band, carry, band=band)
            res = {}
            for (pr, hh, rc), (m, l, acc) in zip(chains, carry):
                res[pr, hh, rc] = acc / l
                lse_ref[rc * rq:(rc + 1) * rq, 2 * pr + hh:2 * pr + hh + 1] = m + jnp.log(l)
            for pr in pairs:
                for rc in range(row_chunks):
                    o_ref[rc * rq:(rc + 1) * rq, pr * 128:(pr + 1) * 128] = jnp.where(
                        low, res[pr, 0, rc], res[pr, 1, rc]).astype(MM)

    return pl.pallas_call(
        body, name="fox_fwd", grid=(S // tq,),
        in_specs=[pl.BlockSpec((tq, D_ATT), lambda i: (i, 0)), pl.BlockSpec((S, D_ATT), lambda i: (0, 1)),
                  pl.BlockSpec((S, D_ATT), lambda i: (0, 2)), _resident()],
        out_specs=[_rows(tq, D_ATT), _rows(tq, N_HEADS)],
        out_shape=[jax.ShapeDtypeStruct((S, D_ATT), MM), jax.ShapeDtypeStruct((S, N_HEADS), F32)],
        compiler_params=_params("parallel"),
    )(fqkv, fqkv, fqkv, c_row3)


def _swa_bias(rel_bias_slot, bucket):
    def body(rb_ref, bk_ref, out_ref):
        bk = bk_ref[...]
        for s in range(N_HEADS):
            acc = jnp.where(bk < 0, NEG, 0.0).astype(F32)
            for b in range(N_BUCKETS):
                acc = jnp.where(bk == b, rb_ref[b, s], acc)
            out_ref[s] = acc

    return pl.pallas_call(
        body, name="swa_bias", out_shape=jax.ShapeDtypeStruct((N_HEADS, WINDOW, 2 * WINDOW), F32),
        in_specs=[pl.BlockSpec(memory_space=pltpu.SMEM), _resident()], out_specs=_resident(),
    )(rel_bias_slot, bucket)


def _swa_specs(S):
    q = pl.BlockSpec((WINDOW, D_ATT), lambda n: (n, 0))
    kp = pl.BlockSpec((WINDOW, D_KV), lambda n: (jnp.maximum(n - 1, 0), 4))
    kc = pl.BlockSpec((WINDOW, D_KV), lambda n: (n, 4))
    vp = pl.BlockSpec((WINDOW, D_KV), lambda n: (jnp.maximum(n - 1, 0), 5))
    vc = pl.BlockSpec((WINDOW, D_KV), lambda n: (n, 5))
    return [q, kp, kc, vp, vc]


def _swa_fwd(sqkv, biasm, sinks_slot):
    S = sqkv.shape[0]

    def body(q_ref, kp_ref, kc_ref, vp_ref, vc_ref, bias_ref, sink_ref, o_ref, lse_ref):
        n = pl.program_id(0)
        no_prev = jnp.where(n > 0, 0.0, NEG)
        for j in range(N_HEADS // 2):
            lanes = slice(j * 128, (j + 1) * 128)
            q2 = q_ref[:, lanes]
            res = []
            for hh in range(2):
                s_ = 2 * j + hh
                qh = jnp.where(_head_select((WINDOW, 128), hh), q2, jnp.zeros_like(q2))
                sp = _dot_nt(qh, kp_ref[...]) + bias_ref[s_, :, :WINDOW] + no_prev
                sc = _dot_nt(qh, kc_ref[...]) + bias_ref[s_, :, WINDOW:]
                sink = sink_ref[s_]
                m = jnp.maximum(jnp.maximum(jnp.max(sp, axis=-1, keepdims=True),
                                            jnp.max(sc, axis=-1, keepdims=True)), sink)
                ep = jnp.exp(sp - m)
                ec = jnp.exp(sc - m)
                den = (jnp.sum(ep, axis=-1, keepdims=True) + jnp.sum(ec, axis=-1, keepdims=True)
                       + jnp.exp(sink - m))
                res.append((_dot(ep.astype(MM), vp_ref[...]) + _dot(ec.astype(MM), vc_ref[...])) / den)
                lse_ref[:, s_:s_ + 1] = m + jnp.log(den)
            o_ref[:, lanes] = jnp.where(_head_select((WINDOW, 128), 0), res[0], res[1]).astype(MM)

    return pl.pallas_call(
        body, name="swa_fwd", grid=(S // WINDOW,),
        in_specs=_swa_specs(S) + [_resident(), pl.BlockSpec(memory_space=pltpu.SMEM)],
        out_specs=[_rows(WINDOW, D_ATT), _rows(WINDOW, N_HEADS)],
        out_shape=[jax.ShapeDtypeStruct((S, D_ATT), MM), jax.ShapeDtypeStruct((S, N_HEADS), F32)],
        compiler_params=_params("parallel"),
    )(sqkv, sqkv, sqkv, sqkv, sqkv, biasm, sinks_slot)


def _post_attn(x, fox_o, swa_o, wout_fox, wout_swa, g2, g3, tm):
    S = x.shape[0]

    def body(x_ref, fo_ref, so_ref, wf_ref, ws_ref, g2_ref, g3_ref, mix_ref, h1_ref, m_ref):
        mix = _dot(fo_ref[...], wf_ref[...]) + _dot(so_ref[...], ws_ref[...])
        mix_ref[...] = mix
        n2, _ = _rms(mix)
        h1 = x_ref[...] + n2 * g2_ref[...]
        h1_ref[...] = h1
        n3, _ = _rms(h1)
        m_ref[...] = (n3 * g3_ref[...]).astype(MM)

    return pl.pallas_call(
        body, name="post_attn", grid=(S // tm,),
        in_specs=[_rows(tm, D_MODEL), _rows(tm, D_ATT), _rows(tm, D_ATT), _resident(), _resident(),
                  _const((1, D_MODEL)), _const((1, D_MODEL))],
        out_specs=[_rows(tm, D_MODEL)] * 3,
        out_shape=[jax.ShapeDtypeStruct((S, D_MODEL), F32), jax.ShapeDtypeStruct((S, D_MODEL), F32),
                   jax.ShapeDtypeStruct((S, D_MODEL), MM)],
        compiler_params=_params("parallel"),
    )(x, fox_o, swa_o, wout_fox, wout_swa, g2, g3)


def _mlp_fwd(m, h1, w1, w2, g4, tm):
    S = m.shape[0]

    def body(m_ref, h1_ref, w1_ref, w2_ref, g4_ref, u_ref, y_ref, h2_ref):
        mb = m_ref[...]
        y = jnp.zeros((tm, D_MODEL), F32)
        for j in range(N_DEV):
            cols = slice(j * FF_CHUNK, (j + 1) * FF_CHUNK)
            u = _dot(mb, w1_ref[j])
            u_ref[:, cols] = u.astype(MM)
            y = y + _dot(jnp.square(jnp.maximum(u, 0.0)).astype(MM), w2_ref[cols, :])
        y_ref[...] = y
        n4, _ = _rms(y)
        h2_ref[...] = h1_ref[...] + n4 * g4_ref[...]

    return pl.pallas_call(
        body, name="mlp_fwd", grid=(S // tm,),
        in_specs=[_rows(tm, D_MODEL), _rows(tm, D_MODEL), _resident(), _resident(), _const((1, D_MODEL))],
        out_specs=[_rows(tm, D_FF), _rows(tm, D_MODEL), _rows(tm, D_MODEL)],
        out_shape=[jax.ShapeDtypeStruct((S, D_FF), MM), jax.ShapeDtypeStruct((S, D_MODEL), F32),
                   jax.ShapeDtypeStruct((S, D_MODEL), F32)],
        compiler_params=_params("parallel"),
    )(m, h1, w1, w2, g4)


def _ple_loss(h2, p, target, wg, wple, g5, tm):
    S = h2.shape[0]

    def body(h2_ref, p_ref, t_ref, wg_ref, wp_ref, g5_ref, dh2_ref, dpe_ref, dgl_ref, dg5_ref, loss_ref):
        i = pl.program_id(0)
        h2 = h2_ref[...]
        gate = jax.nn.sigmoid(_dot(h2.astype(MM), wg_ref[...]))
        pe = _dot(p_ref[...].astype(MM), wp_ref[...])
        n5, r5 = _rms(pe * gate)
        g5 = g5_ref[...]
        diff = h2 + n5 * g5 - t_ref[...]
        per_token = jnp.mean(jnp.square(diff), axis=-1, keepdims=True)
        _accumulate(loss_ref, 0.5 * jnp.sum(per_token, axis=0, keepdims=True), i)
        dh3 = diff * (1.0 / D_MODEL)
        de, dg5 = _rms_bwd(dh3, n5, r5, g5)
        _accumulate(dg5_ref, dg5, i)
        dpe_ref[...] = (de * gate).astype(MM)
        dgl = (de * pe * gate * (1.0 - gate)).astype(MM)
        dgl_ref[...] = dgl
        dh2_ref[...] = dh3 + _dot_nt(dgl, wg_ref[...])

    return pl.pallas_call(
        body, name="ple_loss", grid=(S // tm,),
        in_specs=[_rows(tm, D_MODEL), _rows(tm, D_PLE), _rows(tm, D_MODEL), _resident(), _resident(),
                  _const((1, D_MODEL))],
        out_specs=[_rows(tm, D_MODEL), _rows(tm, D_MODEL), _rows(tm, D_MODEL), _const((1, D_MODEL)), _const((1, 1))],
        out_shape=[jax.ShapeDtypeStruct((S, D_MODEL), F32), jax.ShapeDtypeStruct((S, D_MODEL), MM),
                   jax.ShapeDtypeStruct((S, D_MODEL), MM), jax.ShapeDtypeStruct((1, D_MODEL), F32),
                   jax.ShapeDtypeStruct((1, 1), F32)],
        compiler_params=_params("arbitrary"),
    )(h2, p, target, wg, wple, g5)


def _mlp_bwd(dh2, y, h1, u, w1, w2, g4, g3, tm):
    S = dh2.shape[0]

    def body(dh2_ref, y_ref, h1_ref, u_ref, w1_ref, w2_ref, g4_ref, g3_ref,
             dh1_ref, dy_ref, du_ref, dg4_ref, dg3_ref):
        i = pl.program_id(0)
        dh2 = dh2_ref[...]
        n4, r4 = _rms(y_ref[...])
        dy, dg4 = _rms_bwd(dh2, n4, r4, g4_ref[...])
        _accumulate(dg4_ref, dg4, i)
        dyb = dy.astype(MM)
        dy_ref[...] = dyb
        dm = jnp.zeros((tm, D_MODEL), F32)
        for j in range(N_DEV):
            cols = slice(j * FF_CHUNK, (j + 1) * FF_CHUNK)
            dact = _dot_nt(dyb, w2_ref[cols, :])
            du = (dact * (2.0 * jnp.maximum(u_ref[:, cols].astype(F32), 0.0))).astype(MM)
            du_ref[:, cols] = du
            dm = dm + _dot_nt(du, w1_ref[j])
        n3, r3 = _rms(h1_ref[...])
        dx, dg3 = _rms_bwd(dm, n3, r3, g3_ref[...])
        _accumulate(dg3_ref, dg3, i)
        dh1_ref[...] = dh2 + dx

    return pl.pallas_call(
        body, name="mlp_bwd", grid=(S // tm,),
        in_specs=[_rows(tm, D_MODEL), _rows(tm, D_MODEL), _rows(tm, D_MODEL), _rows(tm, D_FF),
                  _resident(), _resident(), _const((1, D_MODEL)), _const((1, D_MODEL))],
        out_specs=[_rows(tm, D_MODEL), _rows(tm, D_MODEL), _rows(tm, D_FF), _const((1, D_MODEL)),
                   _const((1, D_MODEL))],
        out_shape=[jax.ShapeDtypeStruct((S, D_MODEL), F32), jax.ShapeDtypeStruct((S, D_MODEL), MM),
                   jax.ShapeDtypeStruct((S, D_FF), MM), jax.ShapeDtypeStruct((1, D_MODEL), F32),
                   jax.ShapeDtypeStruct((1, D_MODEL), F32)],
        compiler_params=_params("arbitrary"),
    )(dh2, y, h1, u, w1, w2, g4, g3)


def _attn_out_bwd(dh1, mix, fox_o, swa_o, wout_fox, wout_swa, g2, head_rows, head_cols, tm):
    S = dh1.shape[0]

    def body(dh1_ref, mix_ref, fo_ref, so_ref, wf_ref, ws_ref, g2_ref, er_ref, ec_ref,
             dmix_ref, dcat_ref, drow_ref, dcol_ref, dg2_ref):
        i = pl.program_id(0)
        n2, r2 = _rms(mix_ref[...])
        dmix, dg2 = _rms_bwd(dh1_ref[...], n2, r2, g2_ref[...])
        _accumulate(dg2_ref, dg2, i)
        dmb = dmix.astype(MM)
        dmix_ref[...] = dmb
        dfo = _dot_nt(dmb, wf_ref[...]).astype(MM)
        dso = _dot_nt(dmb, ws_ref[...]).astype(MM)
        dcat_ref[:, :D_ATT] = dfo
        dcat_ref[:, D_ATT:] = dso
        hi = lax.Precision.HIGHEST
        prod_f = dfo.astype(F32) * fo_ref[...].astype(F32)
        prod_s = dso.astype(F32) * so_ref[...].astype(F32)
        drow_ref[...] = lax.dot_general(er_ref[...], prod_f, NT, precision=hi, preferred_element_type=F32)
        dcol_ref[...] = jnp.dot(prod_s, ec_ref[...], precision=hi, preferred_element_type=F32)

    return pl.pallas_call(
        body, name="attn_out_bwd", grid=(S // tm,),
        in_specs=[_rows(tm, D_MODEL), _rows(tm, D_MODEL), _rows(tm, D_ATT), _rows(tm, D_ATT), _resident(),
                  _resident(), _const((1, D_MODEL)), _resident(), _resident()],
        out_specs=[_rows(tm, D_MODEL), _rows(tm, D_MODEL), pl.BlockSpec((N_HEADS, tm), lambda i: (0, i)),
                   _rows(tm, N_HEADS), _const((1, D_MODEL))],
        out_shape=[jax.ShapeDtypeStruct((S, D_MODEL), MM), jax.ShapeDtypeStruct((S, D_MODEL), MM),
                   jax.ShapeDtypeStruct((N_HEADS, S), F32), jax.ShapeDtypeStruct((S, N_HEADS), F32),
                   jax.ShapeDtypeStruct((1, D_MODEL), F32)],
        compiler_params=_params("arbitrary"),
    )(dh1, mix, fox_o, swa_o, wout_fox, wout_swa, g2, head_rows, head_cols)


def _fox_bwd(fqkv, dcat, lse_row3, d_row3, c_col, tq, tk, pairs_per_loop=2):
    S = fqkv.shape[0]
    n_blk = S // tk
    n_qblk = S // tq
    n_band = tk // tq

    def body(q_ref, k_ref, v_ref, do_ref, lse_ref, dd_ref, ck_ref, dq_ref, dk_ref, dv_ref, dc_ref, dcq_ref):
        kb = pl.program_id(0)

        @pl.when(kb == 0)
        def _():
            dq_ref[...] = jnp.zeros_like(dq_ref)
            dcq_ref[...] = jnp.zeros_like(dcq_ref)

        key = lax.broadcasted_iota(jnp.int32, (tk, tq), 0)
        qry = lax.broadcasted_iota(jnp.int32, (tk, tq), 1)
        low = _head_select((tk, 128), 0)
        for first in range(0, N_HEADS // 2, pairs_per_loop):
            pairs = range(first, first + pairs_per_loop)
            heads = [(pr, hh) for pr in pairs for hh in range(2)]
            kh, vh, ck = {}, {}, {}
            for pr in pairs:
                k2 = k_ref[:, pr * 128:(pr + 1) * 128]
                v2 = v_ref[:, pr * 128:(pr + 1) * 128]
                zero = jnp.zeros_like(k2)
                kh[pr, 0], kh[pr, 1] = jnp.where(low, k2, zero), jnp.where(low, zero, k2)
                vh[pr, 0], vh[pr, 1] = jnp.where(low, v2, zero), jnp.where(low, zero, v2)
                for hh in range(2):
                    ck[pr, hh] = ck_ref[:, 2 * pr + hh:2 * pr + hh + 1]

            def block(qb, carry, band, pairs=pairs, kh=kh, vh=vh, ck=ck):
                rows = pl.ds(pl.multiple_of(qb * tq, tq), tq)
                out = []
                it = iter(carry)
                for pr in pairs:
                    lanes = slice(pr * 128, (pr + 1) * 128)
                    q2 = q_ref[rows, lanes]
                    do2 = do_ref[rows, lanes]
                    dq = None
                    for hh in range(2):
                        h = 2 * pr + hh
                        dk, dv, dc = next(it)
                        s_t = _dot_nt(kh[pr, hh], q2) - ck[pr, hh]
                        p_t = jnp.exp(s_t - lse_ref[h, pl.ds(qb, 1), :])
                        if band is not None:
                            p_t = jnp.where(qry + band * tq >= key, p_t, 0.0)
                        ds_t = p_t * (_dot_nt(vh[pr, hh], do2) - dd_ref[h, pl.ds(qb, 1), :])
                        dsb = ds_t.astype(MM)
                        dv = dv + _dot(p_t.astype(MM), do2)
                        dk = dk + _dot(dsb, q2)
                        dc = dc - jnp.sum(ds_t, axis=1, keepdims=True)
                        part = _dot_tn(dsb, kh[pr, hh])
                        dq = part if dq is None else dq + part
                        dcq_ref[h, pl.ds(qb, 1), :] += jnp.sum(ds_t, axis=0, keepdims=True)
                        out.append((dk, dv, dc))
                    dq_ref[rows, lanes] += dq
                return tuple(out)

            carry = tuple((jnp.zeros((tk, 128), F32), jnp.zeros((tk, 128), F32), jnp.zeros((tk, 1), F32))
                          for _ in heads)
            for band in range(n_band):
                carry = block(kb * n_band + band, carry, band=band)
            carry = lax.fori_loop((kb + 1) * n_band, n_qblk, functools.partial(block, band=None), carry)
            grads = dict(zip(heads, carry))
            for pr in pairs:
                lanes = slice(pr * 128, (pr + 1) * 128)
                dk_ref[:, lanes] = jnp.where(low, grads[pr, 0][0], grads[pr, 1][0]).astype(MM)
                dv_ref[:, lanes] = jnp.where(low, grads[pr, 0][1], grads[pr, 1][1]).astype(MM)
                for hh in range(2):
                    dc_ref[:, 2 * pr + hh:2 * pr + hh + 1] = grads[pr, hh][2]

        @pl.when(kb == n_blk - 1)
        def _():
            dq_ref[...] = dq_ref[...] * Q_SCALE

    return pl.pallas_call(
        body, name="fox_bwd", grid=(n_blk,),
        in_specs=[pl.BlockSpec((S, D_ATT), lambda i: (0, 0)), pl.BlockSpec((tk, D_ATT), lambda i: (i, 1)),
                  pl.BlockSpec((tk, D_ATT), lambda i: (i, 2)), pl.BlockSpec((S, D_ATT), lambda i: (0, 0)),
                  _resident(), _resident(), _rows(tk, N_HEADS)],
        out_specs=[_const((S, D_ATT)), _rows(tk, D_ATT), _rows(tk, D_ATT), _rows(tk, N_HEADS),
                   _const((N_HEADS, n_qblk, tq))],
        out_shape=[jax.ShapeDtypeStruct((S, D_ATT), F32), jax.ShapeDtypeStruct((S, D_ATT), MM),
                   jax.ShapeDtypeStruct((S, D_ATT), MM), jax.ShapeDtypeStruct((S, N_HEADS), F32),
                   jax.ShapeDtypeStruct((N_HEADS, n_qblk, tq), F32)],
        compiler_params=_params("arbitrary"),
    )(fqkv, fqkv, fqkv, dcat, lse_row3, d_row3, c_col)


def _swa_bwd(sqkv, dcat, biasm, sinks_slot, bucket, lse, d_col):
    S = sqkv.shape[0]
    n_blk = S // WINDOW

    def body(q_ref, kp_ref, kc_ref, vp_ref, vc_ref, do_ref, bias_ref, sink_ref, bk_ref, lse_ref, dd_ref,
             dq_ref, dk_ref, dv_ref, drb_ref, dsink_ref, ds_acc):
        n = pl.program_id(0)

        @pl.when(n == 0)
        def _():
            dk_ref[...] = jnp.zeros_like(dk_ref)
            dv_ref[...] = jnp.zeros_like(dv_ref)
            ds_acc[...] = jnp.zeros_like(ds_acc)
            dsink_ref[...] = jnp.zeros_like(dsink_ref)

        no_prev = jnp.where(n > 0, 0.0, NEG)
        prev = pl.ds(pl.multiple_of(jnp.maximum(n - 1, 0) * WINDOW, WINDOW), WINDOW)
        cur = pl.ds(pl.multiple_of(n * WINDOW, WINDOW), WINDOW)
        lane8 = lax.broadcasted_iota(jnp.int32, (1, N_HEADS), 1)
        dkp = jnp.zeros((WINDOW, D_KV), F32)
        dkc = jnp.zeros((WINDOW, D_KV), F32)
        dvp = jnp.zeros((WINDOW, D_KV), F32)
        dvc = jnp.zeros((WINDOW, D_KV), F32)
        dsink = jnp.zeros((1, N_HEADS), F32)
        for j in range(N_HEADS // 2):
            lanes = slice(j * 128, (j + 1) * 128)
            q2 = q_ref[:, lanes]
            do2 = do_ref[:, lanes]
            dqs = []
            for hh in range(2):
                s_ = 2 * j + hh
                sel = _head_select((WINDOW, 128), hh)
                qh = jnp.where(sel, q2, jnp.zeros_like(q2))
                doh = jnp.where(sel, do2, jnp.zeros_like(do2))
                lse_h = lse_ref[:, s_:s_ + 1]
                dd = dd_ref[:, s_:s_ + 1]
                pp = jnp.exp(_dot_nt(qh, kp_ref[...]) + bias_ref[s_, :, :WINDOW] + no_prev - lse_h)
                pc = jnp.exp(_dot_nt(qh, kc_ref[...]) + bias_ref[s_, :, WINDOW:] - lse_h)
                p_sink = jnp.exp(sink_ref[s_] - lse_h)
                dsp = pp * (_dot_nt(doh, vp_ref[...]) - dd)
                dsc = pc * (_dot_nt(doh, vc_ref[...]) - dd)
                dsink = dsink + jnp.where(lane8 == s_, -jnp.sum(p_sink * dd), 0.0)
                ds_acc[s_, :, :WINDOW] += dsp
                ds_acc[s_, :, WINDOW:] += dsc
                dspb, dscb = dsp.astype(MM), dsc.astype(MM)
                dqs.append(_dot(dspb, kp_ref[...]) + _dot(dscb, kc_ref[...]))
                dkp = dkp + _dot_tn(dspb, qh)
                dkc = dkc + _dot_tn(dscb, qh)
                dvp = dvp + _dot_tn(pp.astype(MM), doh)
                dvc = dvc + _dot_tn(pc.astype(MM), doh)
            dq_ref[:, lanes] = (jnp.where(_head_select((WINDOW, 128), 0), dqs[0], dqs[1]) * Q_SCALE).astype(MM)
        dk_ref[prev, :] += dkp
        dk_ref[cur, :] += dkc
        dv_ref[prev, :] += dvp
        dv_ref[cur, :] += dvc
        dsink_ref[...] += dsink

        @pl.when(n == n_blk - 1)
        def _():
            bk = bk_ref[...]
            rb = lax.broadcasted_iota(jnp.int32, (N_BUCKETS, N_HEADS), 0)
            cb = lax.broadcasted_iota(jnp.int32, (N_BUCKETS, N_HEADS), 1)
            out = jnp.zeros((N_BUCKETS, N_HEADS), F32)
            for s in range(N_HEADS):
                acc = ds_acc[s]
                for b in range(N_BUCKETS):
                    out = out + jnp.where((rb == b) & (cb == s), jnp.sum(jnp.where(bk == b, acc, 0.0)), 0.0)
            drb_ref[...] = out

    do_spec = pl.BlockSpec((WINDOW, D_ATT), lambda n: (n, 1))
    return pl.pallas_call(
        body, name="swa_bwd", grid=(n_blk,),
        in_specs=_swa_specs(S) + [do_spec, _resident(), pl.BlockSpec(memory_space=pltpu.SMEM), _resident(),
                                  _rows(WINDOW, N_HEADS), _rows(WINDOW, N_HEADS)],
        out_specs=[_rows(WINDOW, D_ATT), _const((S, D_KV)), _const((S, D_KV)), _const((N_BUCKETS, N_HEADS)),
                   _const((1, N_HEADS))],
        out_shape=[jax.ShapeDtypeStruct((S, D_ATT), MM), jax.ShapeDtypeStruct((S, D_KV), F32),
                   jax.ShapeDtypeStruct((S, D_KV), F32), jax.ShapeDtypeStruct((N_BUCKETS, N_HEADS), F32),
                   jax.ShapeDtypeStruct((1, N_HEADS), F32)],
        scratch_shapes=[pltpu.VMEM((N_HEADS, WINDOW, 2 * WINDOW), F32)],
        compiler_params=_params("arbitrary"),
    )(sqkv, sqkv, sqkv, sqkv, sqkv, dcat, biasm, sinks_slot, bucket, lse, d_col)


def _pre_attn_bwd(x, dh1, dq_fox, dk_fox, dv_fox, dsq, dsk, dsv, dff_t, wt, g1, tm):
    S = x.shape[0]

    def body(x_ref, dh1_ref, dq_ref, dk_ref, dv_ref, dsq_ref, dsk_ref, dsv_ref, dff_ref, wt_ref, g1_ref,
             dx_ref, dz_ref, dg1_ref):
        i = pl.program_id(0)
        dq = dq_ref[...].astype(MM)
        dsk = dsk_ref[...].astype(MM)
        dsv = dsv_ref[...].astype(MM)
        dz_ref[:, 0:512] = dq
        dz_ref[:, 512:1024] = dk_ref[...]
        dz_ref[:, 1024:1536] = dv_ref[...]
        dz_ref[:, 1536:2048] = dsq_ref[...]
        dz_ref[:, 2048:2176] = dsk
        dz_ref[:, 2176:2304] = dsv
        da = (_dot(dq, wt_ref[0:512]) + _dot(dk_ref[...], wt_ref[512:1024]) + _dot(dv_ref[...], wt_ref[1024:WT_FF])
              + _dot(dsq_ref[...], wt_ref[WT_SQ:WT_SKV]) + _dot(dsk, wt_ref[WT_SKV:WT_SKV + D_KV])
              + _dot(dsv, wt_ref[WT_SKV + D_KV:WT_ROWS]) + _dot_tn(dff_ref[...].astype(MM), wt_ref[WT_FF:WT_SQ]))
        n1, r1 = _rms(x_ref[...])
        dx, dg1 = _rms_bwd(da, n1, r1, g1_ref[...])
        _accumulate(dg1_ref, dg1, i)
        dx_ref[...] = dh1_ref[...] + dx

    return pl.pallas_call(
        body, name="pre_attn_bwd", grid=(S // tm,),
        in_specs=[_rows(tm, D_MODEL), _rows(tm, D_MODEL), _rows(tm, D_ATT), _rows(tm, D_ATT), _rows(tm, D_ATT),
                  _rows(tm, D_ATT), _rows(tm, D_KV), _rows(tm, D_KV), pl.BlockSpec((16, tm), lambda i: (0, i)),
                  _resident(), _const((1, D_MODEL))],
        out_specs=[_rows(tm, D_MODEL), _rows(tm, 2304), _const((1, D_MODEL))],
        out_shape=[jax.ShapeDtypeStruct((S, D_MODEL), F32), jax.ShapeDtypeStruct((S, 2304), MM),
                   jax.ShapeDtypeStruct((1, D_MODEL), F32)],
        compiler_params=_params("arbitrary"),
    )(x, dh1, dq_fox, dk_fox, dv_fox, dsq, dsk, dsv, dff_t, wt, g1)


def _weight_grad(a, b, name, tk, n_chunks=1, relu2=False):
    S, K = a.shape
    N = b.shape[1]
    cn = N // n_chunks

    def body(a_ref, b_ref, out_ref):
        av = a_ref[...]
        if relu2:
            av = jnp.square(jnp.maximum(av.astype(F32), 0.0))
        av = av.astype(MM)
        for j in range(n_chunks):
            val = _dot_tn(av, b_ref[:, j * cn:(j + 1) * cn].astype(MM)).astype(MM)
            if n_chunks > 1:
                out_ref[j] = val
            else:
                out_ref[...] = val

    if n_chunks > 1:
        out_spec = pl.BlockSpec((n_chunks, tk, cn), lambda i: (0, i, 0))
        out_shape = jax.ShapeDtypeStruct((n_chunks, K, cn), MM)
    else:
        out_spec = pl.BlockSpec((tk, N), lambda i: (i, 0))
        out_shape = jax.ShapeDtypeStruct((K, N), MM)
    return pl.pallas_call(
        body, name=name, grid=(K // tk,),
        in_specs=[pl.BlockSpec((S, tk), lambda i: (0, i)), _resident()],
        out_specs=out_spec, out_shape=out_shape, compiler_params=_params("parallel"),
    )(a, b)


def _forget_weight_grad(dff_t, a):
    def body(d_ref, a_ref, out_ref):
        out_ref[...] = _dot(d_ref[...].astype(MM), a_ref[...])

    return pl.pallas_call(
        body, name="forget_weight_grad", out_shape=jax.ShapeDtypeStruct((16, D_MODEL), F32),
        in_specs=[_resident(), _resident()], out_specs=_resident(),
    )(dff_t, a)


def _place():
    return lax.axis_index("x"), lax.axis_index("y"), lax.axis_index("c")


def _all_gather_sequencer(stack):
    ref = jax.new_ref(stack, memory_space=pltpu.MemorySpace.HBM)

    @pl.kernel(mesh=plsc.ScalarSubcoreMesh(axis_name="sequencer", num_cores=1), name="all_gather_sequencer",
               scratch_types=(pltpu.SemaphoreType.DMA((7,)), pltpu.SemaphoreType.DMA((7,))),
               compiler_params=pltpu.CompilerParams(collective_id=1))
    def launch(send_sems, recv_sems):
        x, y, c = _place()
        me = 4 * x + 2 * y + c
        peers = [(x ^ dx, y ^ dy, c ^ dc) for dx in range(2) for dy in range(2) for dc in range(2) if dx + dy + dc]
        barrier = pltpu.get_barrier_semaphore()
        for peer in peers:
            pl.semaphore_signal(barrier, inc=1, device_id=peer, device_id_type=MESH)
        pl.semaphore_wait(barrier, 7)
        copies = [_remote(ref.at[me], ref.at[me], send_sems, recv_sems, k, peer) for k, peer in enumerate(peers)]
        for cp in copies:
            cp.start()
        for cp in copies:
            cp.wait()

    launch()
    return ref[...]


def _chip_sum(grad, other, name):
    _, _, r, cdim = grad.shape
    tr = 256 if r % 256 == 0 else r

    def body(c_ref, g_ref, o_ref, out_ref):
        out_ref[...] = (g_ref[...].astype(F32) + o_ref[...].astype(F32)).astype(out_ref.dtype)

    return pl.pallas_call(
        body, name=name,
        grid_spec=pltpu.PrefetchScalarGridSpec(
            num_scalar_prefetch=1, grid=(4, r // tr),
            in_specs=[pl.BlockSpec((None, None, tr, cdim), lambda k, i, c_ref: (k, c_ref[0], i, 0)),
                      pl.BlockSpec((None, tr, cdim), lambda k, i, c_ref: (k, i, 0))],
            out_specs=pl.BlockSpec((None, tr, cdim), lambda k, i, c_ref: (k, i, 0))),
        out_shape=jax.ShapeDtypeStruct((4, r, cdim), MM),
        compiler_params=_params("parallel", "parallel"),
    )(lax.axis_index("c").astype(jnp.int32).reshape(1), grad, other)


HBM_SPEC = pl.BlockSpec(memory_space=pltpu.HBM)
SEM_SPEC = pl.BlockSpec(memory_space=pltpu.SEMAPHORE)
DATAFLOW = pltpu.SideEffectType.DATAFLOW_SIDE_EFFECTING


def _exchange_start(name, arrays, n_copies, plan):
    n = len(arrays)

    def body(*refs):
        send_sems, recv_sems, token = refs[n], refs[n + 1], refs[2 * n + 2]
        for cp in plan(refs[:n], send_sems, recv_sems):
            cp.start()
        token[...] = jnp.zeros_like(token)

    out = pl.pallas_call(
        body, name=name,
        out_shape=(pltpu.SemaphoreType.DMA((n_copies,)), pltpu.SemaphoreType.DMA((n_copies,)),
                   *[pltpu.HBM(a.shape, a.dtype) for a in arrays], jax.ShapeDtypeStruct((1, D_MODEL), F32)),
        in_specs=[HBM_SPEC] * n,
        out_specs=(SEM_SPEC, SEM_SPEC, *[HBM_SPEC] * n, pl.BlockSpec(memory_space=pltpu.VMEM)),
        input_output_aliases={i: 2 + i for i in range(n)},
        compiler_params=pltpu.CompilerParams(has_side_effects=DATAFLOW),
    )(*[pltpu.with_memory_space_constraint(a, pltpu.HBM) for a in arrays])
    return (out[0], out[1]), list(out[2:2 + n]), out[2 + n]


def _exchange_wait(name, arrays, sems, after, plan):
    n = len(arrays)

    def body(*refs):
        send_sems, recv_sems = refs[n], refs[n + 1]
        for cp in plan(refs[:n], send_sems, recv_sems):
            cp.wait_send()
            cp.wait_recv()

    out = pl.pallas_call(
        body, name=name, out_shape=[pltpu.HBM(a.shape, a.dtype) for a in arrays],
        in_specs=[HBM_SPEC] * n + [SEM_SPEC, SEM_SPEC, pl.BlockSpec(memory_space=pl.ANY)],
        out_specs=[HBM_SPEC] * n, input_output_aliases={i: i for i in range(n)},
        compiler_params=pltpu.CompilerParams(has_side_effects=DATAFLOW),
    )(*arrays, sems[0], sems[1], after)
    return list(out)


def _remote(src, dst, send_sems, recv_sems, k, to):
    return pltpu.make_async_remote_copy(src_ref=src, dst_ref=dst, send_sem=send_sems.at[k], recv_sem=recv_sems.at[k],
                                        device_id=to, device_id_type=MESH)


def _plan_gather_direct(refs, send_sems, recv_sems):
    x, y, c = _place()
    me = 4 * x + 2 * y + c
    peers = [(x, y, 1 - c), (1 - x, y, c), (x, 1 - y, c), (1 - x, 1 - y, c)]
    return [_remote(ref.at[me], ref.at[me], send_sems, recv_sems, 4 * a + k, peer)
            for a, ref in enumerate(refs) for k, peer in enumerate(peers)]


def _plan_gather_pass_on(refs, send_sems, recv_sems):
    x, y, c = _place()
    chips = [(1 - x, y), (x, 1 - y), (1 - x, 1 - y)]
    return [_remote(ref.at[4 * px + 2 * py + c], ref.at[4 * px + 2 * py + c], send_sems, recv_sems, 3 * a + k,
                    (x, y, 1 - c))
            for a, ref in enumerate(refs) for k, (px, py) in enumerate(chips)]


def _plan_in_chip(refs, send_sems, recv_sems):
    n = len(refs) // 2
    x, y, c = _place()
    return [_remote(refs[a].at[:, 1 - c], refs[n + a], send_sems, recv_sems, a, (x, y, 1 - c)) for a in range(n)]


def _plan_between_chips(refs, send_sems, recv_sems):
    n = len(refs) // 2
    x, y, c = _place()
    chips = [(1 - x, y), (x, 1 - y), (1 - x, 1 - y)]
    return [_remote(refs[a].at[2 * px + py], refs[n + a].at[2 * x + y], send_sems, recv_sems, 3 * a + k, (px, py, c))
            for a in range(n) for k, (px, py) in enumerate(chips)]


def _plan_late_between(refs, send_sems, recv_sems):
    sums, land, small = refs
    x, y, c = _place()
    me = 4 * x + 2 * y + c
    copies = _plan_between_chips([sums, land], send_sems, recv_sems)
    peers = [(x ^ dx, y ^ dy, c ^ dc) for dx in range(2) for dy in range(2) for dc in range(2) if dx + dy + dc]
    return copies + [_remote(small.at[me], small.at[me], send_sems, recv_sems, 3 + k, peer)
                     for k, peer in enumerate(peers)]


def _adamw_math(w, g, m, v):
    m = ADAM_B1 * m + (1.0 - ADAM_B1) * g
    v = ADAM_B2 * v + (1.0 - ADAM_B2) * jnp.square(g)
    m_hat = m / (1.0 - ADAM_B1 ** ADAM_STEP)
    v_hat = v / (1.0 - ADAM_B2 ** ADAM_STEP)
    delta = -ADAM_LR * (m_hat / (jnp.sqrt(v_hat) + ADAM_EPS) + ADAM_WD * w)
    return delta, m, v


def _adamw(parts, w, m, v, name):
    n_parts, r, cdim = parts.shape
    tr = 256 if r % 256 == 0 else r

    def body(p_ref, w_ref, m_ref, v_ref, g_out, d_out, m_out, v_out):
        g = p_ref[0].astype(F32)
        for k in range(1, n_parts):
            g = g + p_ref[k].astype(F32)
        delta, m_new, v_new = _adamw_math(w_ref[...], g, m_ref[...], v_ref[...])
        g_out[...] = g
        d_out[...] = delta
        m_out[...] = m_new
        v_out[...] = v_new

    blk = pl.BlockSpec((tr, cdim), lambda i: (i, 0))
    return pl.pallas_call(
        body, name=name, grid=(r // tr,),
        in_specs=[pl.BlockSpec((n_parts, tr, cdim), lambda i: (0, i, 0)), blk, blk, blk],
        out_specs=[blk] * 4, out_shape=[jax.ShapeDtypeStruct((r, cdim), F32)] * 4,
        compiler_params=_params("parallel"),
    )(parts, w, m, v)


def _adamw_chips(parts, sums, w, m, v, name):
    _, r, cdim = parts.shape
    tr = 256 if r % 256 == 0 else r

    def body(chip_ref, p_ref, own_ref, w_ref, m_ref, v_ref, g_out, d_out, m_out, v_out):
        g = None
        for k in range(4):
            term = jnp.where(chip_ref[0] == k, own_ref[...], p_ref[k]).astype(F32)
            g = term if g is None else g + term
        delta, m_new, v_new = _adamw_math(w_ref[...], g, m_ref[...], v_ref[...])
        g_out[...] = g
        d_out[...] = delta
        m_out[...] = m_new
        v_out[...] = v_new

    blk = pl.BlockSpec((tr, cdim), lambda i, chip: (i, 0))
    my_chip = (2 * lax.axis_index("x") + lax.axis_index("y")).astype(jnp.int32).reshape(1)
    return pl.pallas_call(
        body, name=name,
        grid_spec=pltpu.PrefetchScalarGridSpec(
            num_scalar_prefetch=1, grid=(r // tr,),
            in_specs=[pl.BlockSpec((4, tr, cdim), lambda i, chip: (0, i, 0)),
                      pl.BlockSpec((None, tr, cdim), lambda i, chip: (chip[0], i, 0)), blk, blk, blk],
            out_specs=[blk] * 4),
        out_shape=[jax.ShapeDtypeStruct((r, cdim), F32)] * 4,
        compiler_params=_params("parallel"),
    )(my_chip, parts, sums, w, m, v)


class _NoExchange:
    def __init__(self, weights):
        self.weights = weights

    def before_pre_attn(self, g1):
        return g1

    def after_fox_fwd(self, fox_o, sinks_slot):
        return sinks_slot

    def after_attention(self, swa_o):
        return self.weights

    def after_early_grads(self, grads, d_col):
        return d_col

    def after_swa_bwd(self, dsq, d_row3):
        return d_row3


def _slot_order(t, axis):
    shp = t.shape
    t = t.reshape(shp[:axis] + (N_HEADS, shp[axis] // N_HEADS) + shp[axis + 1:])
    t = jnp.take(t, np.array(SLOT_HEAD), axis=axis)
    return t.reshape(shp)


def _head_order(t, axis):
    shp = t.shape
    t = t.reshape(shp[:axis] + (N_HEADS, shp[axis] // N_HEADS) + shp[axis + 1:])
    t = jnp.take(t, np.array(HEAD_SLOT), axis=axis)
    return t.reshape(shp)


def _forward_backward(x, p, target, win_t, hooks, b_forget, rel_bias, sinks, g1, g2, g3, g4, g5):
    S = x.shape[0]
    tm = 256
    t = 256
    q0 = 3 * D_ATT + N_HEADS
    wt = jnp.concatenate(
        [win_t[:q0], jnp.zeros((8, D_MODEL), MM)]
        + [win_t[q0 + HEAD_DIM * h:q0 + HEAD_DIM * (h + 1)] for h in SLOT_HEAD] + [win_t[q0 + D_ATT:]], axis=0)
    bcol =jnp.pad(b_forget.reshape(N_HEADS, 1), ((0, 8), (0, 0)))
    rel_bias_slot = rel_bias[:, np.array(SLOT_HEAD)]
    sinks_slot = sinks.reshape(N_HEADS)[np.array(SLOT_HEAD)]
    bucket = jnp.asarray(_swa_bucket_map())

    a, fqkv, sqkv, fft = _pre_attn(x, hooks.before_pre_attn(g1), wt, tm)
    c_row = _forget_cumsum(fft, bcol)
    c_col = c_row[:N_HEADS].T
    c_row3 = c_row[:N_HEADS].reshape(N_HEADS, S // t, t)
    fox_o, fox_lse = _fox_fwd(fqkv, c_row3, tq=512, tk=t)
    biasm = _swa_bias(rel_bias_slot, bucket)
    sinks_slot = hooks.after_fox_fwd(fox_o, sinks_slot)
    swa_o, swa_lse = _swa_fwd(sqkv, biasm, sinks_slot)
    wout, w1, w2, wple, wg = hooks.after_attention(swa_o)
    wout_fox = wout[:D_ATT]
    wout_swa = _slot_order(wout[D_ATT:], 0)
    mix, h1, m = _post_attn(x, fox_o, swa_o, wout_fox, wout_swa, g2, g3, tm)
    u, y, h2 = _mlp_fwd(m, h1, w1, w2, g4, tm)
    dh2, dpe, dgl, dg5, loss = _ple_loss(h2, p, target, wg, wple, g5, tm)

    d_wple = _weight_grad(p, dpe, "grad_w_ple", tk=D_PLE, n_chunks=N_DEV)
    d_wg = _weight_grad(h2, dgl, "grad_w_ple_gate", tk=256)
    dh1, dy, du, dg4, dg3 = _mlp_bwd(dh2, y, h1, u, w1, w2, g4, g3, tm)
    d_w2 = _weight_grad(u, dy, "grad_w_ff2", tk=256, relu2=True)
    d_w1 = _weight_grad(m, du, "grad_w_ff1", tk=256, n_chunks=N_DEV)
    head = np.arange(D_ATT) // HEAD_DIM
    head_rows = jnp.asarray((head[None, :] == np.arange(N_HEADS)[:, None]).astype(np.float32))
    dmix, dcat, d_row, d_col, dg2 = _attn_out_bwd(dh1, mix, fox_o, swa_o, wout_fox, wout_swa, g2,
                                                  head_rows, head_rows.T, tm)
    d_wout_fox = _weight_grad(fox_o, dmix, "grad_w_out_fox", tk=256)
    d_wout_swa = _weight_grad(swa_o, dmix, "grad_w_out_swa", tk=256)
    d_wout = jnp.concatenate([d_wout_fox, _head_order(d_wout_swa, 0)], axis=0).reshape(N_DEV, D_MODEL // N_DEV, D_MODEL)
    early = dict(w_ff1=d_w1, w_ff2=d_w2.reshape(N_DEV, FF_CHUNK, D_MODEL), w_ple=d_wple,
                 w_ple_gate=d_wg.reshape(N_DEV, D_MODEL // N_DEV, D_MODEL), w_out=d_wout)

    d_col = hooks.after_early_grads(early, d_col)
    dsq, dsk, dsv, d_rb_slot, d_sink_slot = _swa_bwd(sqkv, dcat, biasm, sinks_slot, bucket, swa_lse, d_col)
    lse_row3 = fox_lse.T.reshape(N_HEADS, S // t, t)
    d_row3 = hooks.after_swa_bwd(dsq, d_row.reshape(N_HEADS, S // t, t))
    dq_fox, dk_fox, dv_fox, dc_col, dcq = _fox_bwd(fqkv, dcat, lse_row3, d_row3, c_col, tq=t, tk=512)
    dc_row = jnp.pad(dc_col.T + dcq.reshape(N_HEADS, S), ((0, 8), (0, 0)))
    dff_t, db = _forget_bwd(dc_row, fft, bcol)
    grad_x, dz, dg1 = _pre_attn_bwd(x, dh1, dq_fox, dk_fox, dv_fox, dsq, dsk, dsv, dff_t, wt, g1, tm)
    d_wmain = _weight_grad(dz, a, "grad_w_in", tk=256)
    d_wff_t = _forget_weight_grad(dff_t, a)

    sq0 = 3 * D_ATT
    d_win = jnp.concatenate(
        [d_wmain[:sq0], d_wff_t[:N_HEADS].astype(MM)]
        + [d_wmain[sq0 + HEAD_DIM * s:sq0 + HEAD_DIM * (s + 1)] for s in HEAD_SLOT] + [d_wmain[sq0 + D_ATT:]], axis=0)
    d_win = d_win.reshape(N_DEV, D_IN // N_DEV, D_MODEL)
    big = dict(early, w_in=d_win)
    small = dict(b_forget=db[:N_HEADS].reshape(1, N_HEADS), rel_bias=d_rb_slot[:, np.array(HEAD_SLOT)],
                 swa_sinks=d_sink_slot[:, np.array(HEAD_SLOT)], g_attn_pre=dg1, g_attn_post=dg2, g_ff_pre=dg3,
                 g_ff_post=dg4, g_ple_post=dg5)
    return loss, grad_x, big, small, dz


BIG = ("w_in", "w_out", "w_ff1", "w_ff2", "w_ple", "w_ple_gate")
SMALL_ROWS = ("g_attn_pre", "g_attn_post", "g_ff_pre", "g_ff_post", "g_ple_post")
WEIGHTS = ("w_in", "b_forget", "w_out", "rel_bias", "swa_sinks", "g_attn_pre", "g_attn_post", "w_ff1", "w_ff2",
           "g_ff_pre", "g_ff_post", "w_ple", "w_ple_gate", "g_ple_post")


EARLY = ("w_ff1", "w_ff2", "w_ple", "w_ple_gate", "w_out")


class _Overlap:
    def __init__(self, later):
        self.later = later

    def before_pre_attn(self, g1):
        self.gather_sems, self.later, token = _exchange_start("gather_rest_start", self.later, 4 * 5, _plan_gather_direct)
        return g1 + token

    def after_fox_fwd(self, fox_o, sinks_slot):
        later = _exchange_wait("gather_rest_wait", self.later, self.gather_sems, fox_o, _plan_gather_direct)
        self.pass_sems, self.later, token = _exchange_start("gather_pass_on_start", later, 3 * 5, _plan_gather_pass_on)
        return sinks_slot + token[0, :N_HEADS]

    def after_attention(self, swa_o):
        wout_g, w1_g, w2_g, wple_g, wg_g = _exchange_wait("gather_pass_on_wait", self.later, self.pass_sems, swa_o,
                                                         _plan_gather_pass_on)
        return (wout_g.reshape(D_MODEL, D_MODEL), w1_g, w2_g.reshape(D_FF, D_MODEL),
                jnp.moveaxis(wple_g, 0, 1).reshape(D_PLE, D_MODEL), wg_g.reshape(D_MODEL, D_MODEL))

    def after_early_grads(self, grads, d_col):
        views = [grads[k].reshape((4, 2) + grads[k].shape[1:]) for k in EARLY]
        lands = [lax.empty((4,) + grads[k].shape[1:], MM) for k in EARLY]
        self.in_chip_sems, self.in_chip, token = _exchange_start("grads_in_chip_start", views + lands, len(EARLY),
                                                                 _plan_in_chip)
        return d_col + token[0, 0]

    def after_swa_bwd(self, dsq, d_row3):
        arrays = _exchange_wait("grads_in_chip_wait", self.in_chip, self.in_chip_sems, dsq, _plan_in_chip)
        n = len(EARLY)
        sums = [_chip_sum(arrays[a], arrays[n + a], "chip_sum_" + k) for a, k in enumerate(EARLY)]
        lands = [lax.empty(s.shape, s.dtype) for s in sums]
        self.between_sems, self.between, token = _exchange_start("grads_between_chips_start", sums + lands, 3 * n,
                                                                 _plan_between_chips)
        return d_row3 + token[0, 0]

    def finish(self, after):
        arrays = _exchange_wait("grads_between_chips_wait", self.between, self.between_sems, after,
                                _plan_between_chips)
        n = len(EARLY)
        self.sums = arrays[:n]
        return arrays[n:]


def _pack_small(t):
    rows = [t[k].reshape(1, D_MODEL) for k in SMALL_ROWS]
    misc = jnp.concatenate([t["b_forget"].reshape(-1), t["swa_sinks"].reshape(-1), t["rel_bias"].reshape(-1)])
    rows.append(jnp.pad(misc, (0, D_MODEL - misc.shape[0])).reshape(1, D_MODEL))
    rows.append(jnp.pad(t["loss"].reshape(-1), (0, D_MODEL - 1)).reshape(1, D_MODEL))
    rows.append(jnp.zeros((1, D_MODEL), F32))
    return jnp.concatenate(rows, axis=0).astype(F32)


def _unpack_small(blk):
    out = {k: blk[i].reshape(1, D_MODEL) for i, k in enumerate(SMALL_ROWS)}
    misc = blk[len(SMALL_ROWS)]
    out["b_forget"] = misc[:N_HEADS].reshape(1, N_HEADS)
    out["swa_sinks"] = misc[N_HEADS:2 * N_HEADS].reshape(1, N_HEADS)
    out["rel_bias"] = misc[2 * N_HEADS:2 * N_HEADS + N_BUCKETS * N_HEADS].reshape(N_BUCKETS, N_HEADS)
    out["loss"] = blk[len(SMALL_ROWS) + 1, 0]
    return out


def kernel(x, p, w_in, b_forget, w_out, rel_bias, swa_sinks, g_attn_pre, g_attn_post, w_ff1, w_ff2, g_ff_pre, g_ff_post, w_ple, w_ple_gate, g_ple_post, loss_target, m_w_in, m_b_forget, m_w_out, m_rel_bias, m_swa_sinks, m_g_attn_pre, m_g_attn_post, m_w_ff1, m_w_ff2, m_g_ff_pre, m_g_ff_post, m_w_ple, m_w_ple_gate, m_g_ple_post, v_w_in, v_b_forget, v_w_out, v_rel_bias, v_swa_sinks, v_g_attn_pre, v_g_attn_post, v_w_ff1, v_w_ff2, v_g_ff_pre, v_g_ff_post, v_w_ple, v_w_ple_gate, v_g_ple_post):
    w = dict(w_in=w_in, b_forget=b_forget, w_out=w_out, rel_bias=rel_bias, swa_sinks=swa_sinks,
             g_attn_pre=g_attn_pre, g_attn_post=g_attn_post, w_ff1=w_ff1, w_ff2=w_ff2, g_ff_pre=g_ff_pre,
             g_ff_post=g_ff_post, w_ple=w_ple, w_ple_gate=w_ple_gate, g_ple_post=g_ple_post)
    mom = dict(w_in=m_w_in, b_forget=m_b_forget, w_out=m_w_out, rel_bias=m_rel_bias, swa_sinks=m_swa_sinks,
               g_attn_pre=m_g_attn_pre, g_attn_post=m_g_attn_post, w_ff1=m_w_ff1, w_ff2=m_w_ff2,
               g_ff_pre=m_g_ff_pre, g_ff_post=m_g_ff_post, w_ple=m_w_ple, w_ple_gate=m_w_ple_gate,
               g_ple_post=m_g_ple_post)
    var = dict(w_in=v_w_in, b_forget=v_b_forget, w_out=v_w_out, rel_bias=v_rel_bias, swa_sinks=v_swa_sinks,
               g_attn_pre=v_g_attn_pre, g_attn_post=v_g_attn_post, w_ff1=v_w_ff1, w_ff2=v_w_ff2,
               g_ff_pre=v_g_ff_pre, g_ff_post=v_g_ff_post, w_ple=v_w_ple, w_ple_gate=v_w_ple_gate,
               g_ple_post=v_g_ple_post)

    turn = lambda t, k: t.T if k == "w_in" else t
    me = 4 * lax.axis_index("x") + 2 * lax.axis_index("y") + lax.axis_index("c")

    def stack(block):
        return lax.dynamic_update_slice_in_dim(lax.empty((N_DEV,) + block.shape, block.dtype), block[None], me, 0)

    stacks = [stack(turn(w[k][0], k).astype(MM)) for k in BIG]
    win_g, later = _all_gather_sequencer(stacks[0]), stacks[1:]
    hooks = _Overlap(later)
    loss, grad_x, big, small, last = _forward_backward(
        x[0], p[0, 0], loss_target[0], win_g.reshape(D_IN, D_MODEL), hooks, b_forget, rel_bias, swa_sinks,
        g_attn_pre, g_attn_post, g_ff_pre, g_ff_post, g_ple_post)
    out_g, out_d, out_m, out_v = {}, {}, {}, {}

    def update(k, part, own):
        g, d, m_new, v_new = _adamw_chips(part, own, turn(w[k][0], k), turn(mom[k][0], k), turn(var[k][0], k),
                                          "adamw_" + k)
        out_g[k], out_d[k], out_m[k], out_v[k] = turn(g, k)[None], turn(d, k)[None], turn(m_new, k)[None], turn(v_new, k)[None]
        return d

    d_win = big["w_in"]
    in_chip_sems, in_chip, token = _exchange_start(
        "late_in_chip_start", [d_win.reshape((4, 2) + d_win.shape[1:]), lax.empty((4,) + d_win.shape[1:], MM)], 1,
        _plan_in_chip)
    early_parts = hooks.finish(token)
    done = [update(k, part, own) for k, part, own in list(zip(EARLY, early_parts, hooks.sums))[:2]]
    view, other = _exchange_wait("late_in_chip_wait", in_chip, in_chip_sems, done[-1], _plan_in_chip)
    chip_sum = _chip_sum(view, other, "chip_sum_w_in")
    small["loss"] = loss
    between_sems, between, token = _exchange_start(
        "late_between_chips_start", [chip_sum, lax.empty(chip_sum.shape, MM), stack(_pack_small(small))], 3 + 7,
        _plan_late_between)
    done = [update(k, part, own) for k, part, own in list(zip(EARLY, early_parts, hooks.sums))[2:]]
    chip_sum, part, small_all = _exchange_wait("late_between_chips_wait", between, between_sems, done[-1],
                                               _plan_late_between)
    update("w_in", part, chip_sum)
    rep ={k: w[k] for k in w if k not in BIG}
    rep["loss"] = jnp.zeros((), F32)
    rep_m = {k: mom[k] for k in mom if k not in BIG}
    rep_m["loss"] = jnp.zeros((), F32)
    rep_v = {k: var[k] for k in var if k not in BIG}
    rep_v["loss"] = jnp.ones((), F32)
    g_s, d_s, m_s, v_s = _adamw(small_all, _pack_small(rep), _pack_small(rep_m), _pack_small(rep_v), "adamw_small")
    g_s, d_s, m_s, v_s = _unpack_small(g_s), _unpack_small(d_s), _unpack_small(m_s), _unpack_small(v_s)
    for k in w:
        if k not in BIG:
            out_g[k], out_d[k], out_m[k], out_v[k] = g_s[k], d_s[k], m_s[k], v_s[k]
    return (g_s["loss"], grad_x[None], *[out_g[k] for k in WEIGHTS], *[out_d[k] for k in WEIGHTS],
            *[out_m[k] for k in WEIGHTS], *[out_v[k] for k in WEIGHTS])
```

```python
import functools

import numpy as np
import jax
import jax.numpy as jnp
from jax import lax
from jax.experimental import pallas as pl
from jax.experimental.pallas import tpu as pltpu
from jax.experimental.pallas import tpu_sc as plsc

F32 = jnp.float32
MM = jnp.bfloat16

D_MODEL = 1024
HEAD_DIM = 64
N_HEADS = 8
D_ATT = N_HEADS * HEAD_DIM
D_KV = 128
D_FF = 4096
D_PLE = 256
D_IN = 3 * D_ATT + N_HEADS + D_ATT + 2 * D_KV
N_DEV = 8
FF_CHUNK = D_FF // N_DEV
WINDOW = 128
N_BUCKETS = 32
MAX_DISTANCE = 128
RMS_EPS = 1e-6
Q_SCALE = HEAD_DIM ** -0.5
NEG = -1e30

ADAM_LR = 0.001
ADAM_B1 = 0.9
ADAM_B2 = 0.999
ADAM_EPS = 1e-08
ADAM_WD = 0.01
ADAM_STEP = 10

SLOT_HEAD = (0, 4, 1, 5, 2, 6, 3, 7)
HEAD_SLOT = (0, 2, 4, 6, 1, 3, 5, 7)

VMEM_LIMIT = 56 * 1024 * 1024
MESH = pl.DeviceIdType.MESH

NT = (((1,), (1,)), ((), ()))
TN = (((0,), (0,)), ((), ()))


def _params(*semantics):
    return pltpu.CompilerParams(dimension_semantics=semantics, vmem_limit_bytes=VMEM_LIMIT)


def _resident():
    return pl.BlockSpec(memory_space=pltpu.VMEM)


def _rows(tm, width):
    return pl.BlockSpec((tm, width), lambda i: (i, 0))


def _const(shape):
    return pl.BlockSpec(shape, lambda i: (0,) * len(shape))


def _dot(a, b):
    return jnp.dot(a, b, preferred_element_type=F32)


def _dot_nt(a, b):
    return lax.dot_general(a, b, NT, preferred_element_type=F32)


def _dot_tn(a, b):
    return lax.dot_general(a, b, TN, preferred_element_type=F32)


def _rms(xf):
    r = lax.rsqrt(jnp.mean(xf * xf, axis=-1, keepdims=True) + RMS_EPS)
    return xf * r, r


def _rms_bwd(dout, n, r, g):
    dg = jnp.sum(dout * n, axis=0, keepdims=True)
    dn = dout * g
    dx = r * (dn - n * jnp.mean(dn * n, axis=-1, keepdims=True))
    return dx, dg


def _accumulate(ref, value, step):
    @pl.when(step == 0)
    def _():
        ref[...] = value

    @pl.when(step != 0)
    def _():
        ref[...] += value


def _t5_bucket(n):
    max_exact = N_BUCKETS // 2
    large = max_exact + (np.log(np.maximum(n, 1) / max_exact) / np.log(MAX_DISTANCE / max_exact)
                         * (N_BUCKETS - max_exact)).astype(np.int32)
    large = np.minimum(large, N_BUCKETS - 1)
    return np.where(n < max_exact, n, large).astype(np.int32)


def _swa_bucket_map():
    i = np.arange(WINDOW)[:, None]
    j = np.arange(2 * WINDOW)[None, :]
    dist = i + WINDOW - j
    ok = (dist >= 0) & (dist < WINDOW)
    return np.where(ok, _t5_bucket(np.clip(dist, 0, None)), -1).astype(np.int32)


WT_FOX = 0
WT_FF = 3 * D_ATT
WT_SQ = WT_FF + 16
WT_SKV = WT_SQ + D_ATT
WT_ROWS = WT_SKV + 2 * D_KV


def _pre_attn(x, g1, wt, tm):
    S = x.shape[0]

    def body(x_ref, g_ref, wt_ref, a_ref, fqkv_ref, sqkv_ref, fft_ref):
        n, _ = _rms(x_ref[...])
        a = (n * g_ref[...]).astype(MM)
        a_ref[...] = a
        fqkv_ref[:, :D_ATT] = (_dot_nt(a, wt_ref[WT_FOX:WT_FOX + D_ATT]) * Q_SCALE).astype(MM)
        fqkv_ref[:, D_ATT:] = _dot_nt(a, wt_ref[WT_FOX + D_ATT:WT_FF]).astype(MM)
        sqkv_ref[:, :D_ATT] = (_dot_nt(a, wt_ref[WT_SQ:WT_SKV]) * Q_SCALE).astype(MM)
        sqkv_ref[:, D_ATT:] = _dot_nt(a, wt_ref[WT_SKV:WT_ROWS]).astype(MM)
        fft_ref[...] = _dot_nt(wt_ref[WT_FF:WT_SQ], a)

    return pl.pallas_call(
        body, name="pre_attn", grid=(S // tm,),
        in_specs=[_rows(tm, D_MODEL), _const((1, D_MODEL)), _resident()],
        out_specs=[_rows(tm, D_MODEL), _rows(tm, 3 * D_ATT), _rows(tm, D_ATT + 2 * D_KV),
                   pl.BlockSpec((16, tm), lambda i: (0, i))],
        out_shape=[jax.ShapeDtypeStruct((S, D_MODEL), MM), jax.ShapeDtypeStruct((S, 3 * D_ATT), MM),
                   jax.ShapeDtypeStruct((S, D_ATT + 2 * D_KV), MM), jax.ShapeDtypeStruct((16, S), F32)],
        compiler_params=_params("parallel"),
    )(x, g1, wt)


def _lane_scan(v, reverse):
    S = v.shape[1]
    lane = lax.broadcasted_iota(jnp.int32, v.shape, 1)
    k = 1
    while k < S:
        if reverse:
            v = v + jnp.where(lane < S - k, pltpu.roll(v, S - k, axis=1), 0.0)
        else:
            v = v + jnp.where(lane >= k, pltpu.roll(v, k, axis=1), 0.0)
        k *= 2
    return v


def _forget_cumsum(fft, bcol):
    def body(f_ref, b_ref, c_ref):
        z = f_ref[...] + b_ref[...]
        log_f = jnp.minimum(z, 0.0) - jnp.log1p(jnp.exp(-jnp.abs(z)))
        c_ref[...] = _lane_scan(log_f, reverse=False)

    return pl.pallas_call(
        body, name="forget_cumsum", out_shape=jax.ShapeDtypeStruct(fft.shape, F32),
        in_specs=[_resident(), _resident()], out_specs=_resident(),
    )(fft, bcol)


def _forget_bwd(dc_row, fft, bcol):
    def body(dc_ref, f_ref, b_ref, dff_ref, db_ref):
        z = f_ref[...] + b_ref[...]
        dlog_f = _lane_scan(dc_ref[...], reverse=True)
        dff = dlog_f * (1.0 / (1.0 + jnp.exp(z)))
        dff_ref[...] = dff
        db_ref[...] = jnp.sum(dff, axis=1, keepdims=True)

    return pl.pallas_call(
        body, name="forget_bwd",
        out_shape=[jax.ShapeDtypeStruct(fft.shape, F32), jax.ShapeDtypeStruct((fft.shape[0], 1), F32)],
        in_specs=[_resident()] * 3, out_specs=[_resident()] * 2,
    )(dc_row, fft, bcol)


def _head_select(shape, upper):
    lane = lax.broadcasted_iota(jnp.int32, shape, 1)
    return lane >= HEAD_DIM if upper else lane < HEAD_DIM


def _fox_fwd(fqkv, c_row3, tq, tk, pairs_per_loop=2, row_chunks=1):
    S = fqkv.shape[0]
    rq = tq // row_chunks
    n_band = tq // tk

    def body(q_ref, k_ref, v_ref, ck_ref, o_ref, lse_ref):
        qi = pl.program_id(0)
        row = lax.broadcasted_iota(jnp.int32, (rq, tk), 0)
        col = lax.broadcasted_iota(jnp.int32, (rq, tk), 1)
        low = _head_select((rq, 128), 0)
        for first in range(0, N_HEADS // 2, pairs_per_loop):
            pairs = range(first, first + pairs_per_loop)
            chains = [(pr, hh, rc) for pr in pairs for hh in range(2) for rc in range(row_chunks)]
            qh = {}
            for pr in pairs:
                for rc in range(row_chunks):
                    q2 = q_ref[rc * rq:(rc + 1) * rq, pr * 128:(pr + 1) * 128]
                    qh[pr, 0, rc] = jnp.where(low, q2, jnp.zeros_like(q2))
                    qh[pr, 1, rc] = jnp.where(low, jnp.zeros_like(q2), q2)

            def block(kb, carry, band, chains=chains, qh=qh):
                rows = pl.ds(pl.multiple_of(kb * tk, tk), tk)
                out = []
                for (pr, hh, rc), (m, l, acc) in zip(chains, carry):
                    if band is not None and (rc + 1) * rq <= band * tk:
                        out.append((m, l, acc))
                        continue
                    lanes = slice(pr * 128, (pr + 1) * 128)
                    s = _dot_nt(qh[pr, hh, rc], k_ref[rows, lanes]) - ck_ref[2 * pr + hh, pl.ds(kb, 1), :]
                    if band is not None:
                        s = jnp.where(row + rc * rq >= col + band * tk, s, NEG)
                    m_new = jnp.maximum(m, jnp.max(s, axis=-1, keepdims=True))
                    p = jnp.exp(s - m_new)
                    alpha = jnp.exp(m - m_new)
                    l = alpha * l + jnp.sum(p, axis=-1, keepdims=True)
                    acc = alpha * acc + _dot(p.astype(MM), v_ref[rows, lanes])
                    out.append((m_new, l, acc))
                return tuple(out)

            carry = tuple((jnp.full((rq, 1), NEG, F32), jnp.zeros((rq, 1), F32), jnp.zeros((rq, 128), F32))
                          for _ in chains)
            carry = lax.fori_loop(0, qi * n_band, functools.partial(block, band=None), carry)
            for band in range(n_band):
                carry = block(qi * n_band + band, carry, band=band)
            res = {}
            for (pr, hh, rc), (m, l, acc) in zip(chains, carry):
                res[pr, hh, rc] = acc / l
                lse_ref[rc * rq:(rc + 1) * rq, 2 * pr + hh:2 * pr + hh + 1] = m + jnp.log(l)
            for pr in pairs:
                for rc in range(row_chunks):
                    o_ref[rc * rq:(rc + 1) * rq, pr * 128:(pr + 1) * 128] = jnp.where(
                        low, res[pr, 0, rc], res[pr, 1, rc]).astype(MM)

    return pl.pallas_call(
        body, name="fox_fwd", grid=(S // tq,),
        in_specs=[pl.BlockSpec((tq, D_ATT), lambda i: (i, 0)), pl.BlockSpec((S, D_ATT), lambda i: (0, 1)),
                  pl.BlockSpec((S, D_ATT), lambda i: (0, 2)), _resident()],
        out_specs=[_rows(tq, D_ATT), _rows(tq, N_HEADS)],
        out_shape=[jax.ShapeDtypeStruct((S, D_ATT), MM), jax.ShapeDtypeStruct((S, N_HEADS), F32)],
        compiler_params=_params("parallel"),
    )(fqkv, fqkv, fqkv, c_row3)


def _swa_bias(rel_bias_slot, bucket):
    def body(rb_ref, bk_ref, out_ref):
        bk = bk_ref[...]
        for s in range(N_HEADS):
            acc = jnp.where(bk < 0, NEG, 0.0).astype(F32)
            for b in range(N_BUCKETS):
                acc = jnp.where(bk == b, rb_ref[b, s], acc)
            out_ref[s] = acc

    return pl.pallas_call(
        body, name="swa_bias", out_shape=jax.ShapeDtypeStruct((N_HEADS, WINDOW, 2 * WINDOW), F32),
        in_specs=[pl.BlockSpec(memory_space=pltpu.SMEM), _resident()], out_specs=_resident(),
    )(rel_bias_slot, bucket)


def _swa_specs(S):
    q = pl.BlockSpec((WINDOW, D_ATT), lambda n: (n, 0))
    kp = pl.BlockSpec((WINDOW, D_KV), lambda n: (jnp.maximum(n - 1, 0), 4))
    kc = pl.BlockSpec((WINDOW, D_KV), lambda n: (n, 4))
    vp = pl.BlockSpec((WINDOW, D_KV), lambda n: (jnp.maximum(n - 1, 0), 5))
    vc = pl.BlockSpec((WINDOW, D_KV), lambda n: (n, 5))
    return [q, kp, kc, vp, vc]


def _swa_fwd(sqkv, biasm, sinks_slot):
    S = sqkv.shape[0]

    def body(q_ref, kp_ref, kc_ref, vp_ref, vc_ref, bias_ref, sink_ref, o_ref, lse_ref):
        n = pl.program_id(0)
        no_prev = jnp.where(n > 0, 0.0, NEG)
        for j in range(N_HEADS // 2):
            lanes = slice(j * 128, (j + 1) * 128)
            q2 = q_ref[:, lanes]
            res = []
            for hh in range(2):
                s_ = 2 * j + hh
                qh = jnp.where(_head_select((WINDOW, 128), hh), q2, jnp.zeros_like(q2))
                sp = _dot_nt(qh, kp_ref[...]) + bias_ref[s_, :, :WINDOW] + no_prev
                sc = _dot_nt(qh, kc_ref[...]) + bias_ref[s_, :, WINDOW:]
                sink = sink_ref[s_]
                m = jnp.maximum(jnp.maximum(jnp.max(sp, axis=-1, keepdims=True),
                                            jnp.max(sc, axis=-1, keepdims=True)), sink)
                ep = jnp.exp(sp - m)
                ec = jnp.exp(sc - m)
                den = (jnp.sum(ep, axis=-1, keepdims=True) + jnp.sum(ec, axis=-1, keepdims=True)
                       + jnp.exp(sink - m))
                res.append((_dot(ep.astype(MM), vp_ref[...]) + _dot(ec.astype(MM), vc_ref[...])) / den)
                lse_ref[:, s_:s_ + 1] = m + jnp.log(den)
            o_ref[:, lanes] = jnp.where(_head_select((WINDOW, 128), 0), res[0], res[1]).astype(MM)

    return pl.pallas_call(
        body, name="swa_fwd", grid=(S // WINDOW,),
        in_specs=_swa_specs(S) + [_resident(), pl.BlockSpec(memory_space=pltpu.SMEM)],
        out_specs=[_rows(WINDOW, D_ATT), _rows(WINDOW, N_HEADS)],
        out_shape=[jax.ShapeDtypeStruct((S, D_ATT), MM), jax.ShapeDtypeStruct((S, N_HEADS), F32)],
        compiler_params=_params("parallel"),
    )(sqkv, sqkv, sqkv, sqkv, sqkv, biasm, sinks_slot)


def _post_attn(x, fox_o, swa_o, wout_fox, wout_swa, g2, g3, tm):
    S = x.shape[0]

    def body(x_ref, fo_ref, so_ref, wf_ref, ws_ref, g2_ref, g3_ref, mix_ref, h1_ref, m_ref):
        mix = _dot(fo_ref[...], wf_ref[...]) + _dot(so_ref[...], ws_ref[...])
        mix_ref[...] = mix
        n2, _ = _rms(mix)
        h1 = x_ref[...] + n2 * g2_ref[...]
        h1_ref[...] = h1
        n3, _ = _rms(h1)
        m_ref[...] = (n3 * g3_ref[...]).astype(MM)

    return pl.pallas_call(
        body, name="post_attn", grid=(S // tm,),
        in_specs=[_rows(tm, D_MODEL), _rows(tm, D_ATT), _rows(tm, D_ATT), _resident(), _resident(),
                  _const((1, D_MODEL)), _const((1, D_MODEL))],
        out_specs=[_rows(tm, D_MODEL)] * 3,
        out_shape=[jax.ShapeDtypeStruct((S, D_MODEL), F32), jax.ShapeDtypeStruct((S, D_MODEL), F32),
                   jax.ShapeDtypeStruct((S, D_MODEL), MM)],
        compiler_params=_params("parallel"),
    )(x, fox_o, swa_o, wout_fox, wout_swa, g2, g3)


def _mlp_fwd(m, h1, w1, w2, g4, tm):
    S = m.shape[0]

    def body(m_ref, h1_ref, w1_ref, w2_ref, g4_ref, u_ref, y_ref, h2_ref):
        mb = m_ref[...]
        y = jnp.zeros((tm, D_MODEL), F32)
        for j in range(N_DEV):
            cols = slice(j * FF_CHUNK, (j + 1) * FF_CHUNK)
            u = _dot(mb, w1_ref[j])
            u_ref[:, cols] = u.astype(MM)
            y = y + _dot(jnp.square(jnp.maximum(u, 0.0)).astype(MM), w2_ref[cols, :])
        y_ref[...] = y
        n4, _ = _rms(y)
        h2_ref[...] = h1_ref[...] + n4 * g4_ref[...]

    return pl.pallas_call(
        body, name="mlp_fwd", grid=(S // tm,),
        in_specs=[_rows(tm, D_MODEL), _rows(tm, D_MODEL), _resident(), _resident(), _const((1, D_MODEL))],
        out_specs=[_rows(tm, D_FF), _rows(tm, D_MODEL), _rows(tm, D_MODEL)],
        out_shape=[jax.ShapeDtypeStruct((S, D_FF), MM), jax.ShapeDtypeStruct((S, D_MODEL), F32),
                   jax.ShapeDtypeStruct((S, D_MODEL), F32)],
        compiler_params=_params("parallel"),
    )(m, h1, w1, w2, g4)


def _ple_loss(h2, p, target, wg, wple, g5, tm):
    S = h2.shape[0]

    def body(h2_ref, p_ref, t_ref, wg_ref, wp_ref, g5_ref, dh2_ref, dpe_ref, dgl_ref, dg5_ref, loss_ref):
        i = pl.program_id(0)
        h2 = h2_ref[...]
        gate = jax.nn.sigmoid(_dot(h2.astype(MM), wg_ref[...]))
        pe = _dot(p_ref[...].astype(MM), wp_ref[...])
        n5, r5 = _rms(pe * gate)
        g5 = g5_ref[...]
        diff = h2 + n5 * g5 - t_ref[...]
        per_token = jnp.mean(jnp.square(diff), axis=-1, keepdims=True)
        _accumulate(loss_ref, 0.5 * jnp.sum(per_token, axis=0, keepdims=True), i)
        dh3 = diff * (1.0 / D_MODEL)
        de, dg5 = _rms_bwd(dh3, n5, r5, g5)
        _accumulate(dg5_ref, dg5, i)
        dpe_ref[...] = (de * gate).astype(MM)
        dgl = (de * pe * gate * (1.0 - gate)).astype(MM)
        dgl_ref[...] = dgl
        dh2_ref[...] = dh3 + _dot_nt(dgl, wg_ref[...])

    return pl.pallas_call(
        body, name="ple_loss", grid=(S // tm,),
        in_specs=[_rows(tm, D_MODEL), _rows(tm, D_PLE), _rows(tm, D_MODEL), _resident(), _resident(),
                  _const((1, D_MODEL))],
        out_specs=[_rows(tm, D_MODEL), _rows(tm, D_MODEL), _rows(tm, D_MODEL), _const((1, D_MODEL)), _const((1, 1))],
        out_shape=[jax.ShapeDtypeStruct((S, D_MODEL), F32), jax.ShapeDtypeStruct((S, D_MODEL), MM),
                   jax.ShapeDtypeStruct((S, D_MODEL), MM), jax.ShapeDtypeStruct((1, D_MODEL), F32),
                   jax.ShapeDtypeStruct((1, 1), F32)],
        compiler_params=_params("arbitrary"),
    )(h2, p, target, wg, wple, g5)


def _mlp_bwd(dh2, y, h1, u, w1, w2, g4, g3, tm):
    S = dh2.shape[0]

    def body(dh2_ref, y_ref, h1_ref, u_ref, w1_ref, w2_ref, g4_ref, g3_ref,
             dh1_ref, dy_ref, du_ref, dg4_ref, dg3_ref):
        i = pl.program_id(0)
        dh2 = dh2_ref[...]
        n4, r4 = _rms(y_ref[...])
        dy, dg4 = _rms_bwd(dh2, n4, r4, g4_ref[...])
        _accumulate(dg4_ref, dg4, i)
        dyb = dy.astype(MM)
        dy_ref[...] = dyb
        dm = jnp.zeros((tm, D_MODEL), F32)
        for j in range(N_DEV):
            cols = slice(j * FF_CHUNK, (j + 1) * FF_CHUNK)
            dact = _dot_nt(dyb, w2_ref[cols, :])
            du = (dact * (2.0 * jnp.maximum(u_ref[:, cols].astype(F32), 0.0))).astype(MM)
            du_ref[:, cols] = du
            dm = dm + _dot_nt(du, w1_ref[j])
        n3, r3 = _rms(h1_ref[...])
        dx, dg3 = _rms_bwd(dm, n3, r3, g3_ref[...])
        _accumulate(dg3_ref, dg3, i)
        dh1_ref[...] = dh2 + dx

    return pl.pallas_call(
        body, name="mlp_bwd", grid=(S // tm,),
        in_specs=[_rows(tm, D_MODEL), _rows(tm, D_MODEL), _rows(tm, D_MODEL), _rows(tm, D_FF),
                  _resident(), _resident(), _const((1, D_MODEL)), _const((1, D_MODEL))],
        out_specs=[_rows(tm, D_MODEL), _rows(tm, D_MODEL), _rows(tm, D_FF), _const((1, D_MODEL)),
                   _const((1, D_MODEL))],
        out_shape=[jax.ShapeDtypeStruct((S, D_MODEL), F32), jax.ShapeDtypeStruct((S, D_MODEL), MM),
                   jax.ShapeDtypeStruct((S, D_FF), MM), jax.ShapeDtypeStruct((1, D_MODEL), F32),
                   jax.ShapeDtypeStruct((1, D_MODEL), F32)],
        compiler_params=_params("arbitrary"),
    )(dh2, y, h1, u, w1, w2, g4, g3)


def _attn_out_bwd(dh1, mix, fox_o, swa_o, wout_fox, wout_swa, g2, head_rows, head_cols, tm):
    S = dh1.shape[0]

    def body(dh1_ref, mix_ref, fo_ref, so_ref, wf_ref, ws_ref, g2_ref, er_ref, ec_ref,
             dmix_ref, dcat_ref, drow_ref, dcol_ref, dg2_ref):
        i = pl.program_id(0)
        n2, r2 = _rms(mix_ref[...])
        dmix, dg2 = _rms_bwd(dh1_ref[...], n2, r2, g2_ref[...])
        _accumulate(dg2_ref, dg2, i)
        dmb = dmix.astype(MM)
        dmix_ref[...] = dmb
        dfo = _dot_nt(dmb, wf_ref[...]).astype(MM)
        dso = _dot_nt(dmb, ws_ref[...]).astype(MM)
        dcat_ref[:, :D_ATT] = dfo
        dcat_ref[:, D_ATT:] = dso
        hi = lax.Precision.HIGHEST
        prod_f = dfo.astype(F32) * fo_ref[...].astype(F32)
        prod_s = dso.astype(F32) * so_ref[...].astype(F32)
        drow_ref[...] = lax.dot_general(er_ref[...], prod_f, NT, precision=hi, preferred_element_type=F32)
        dcol_ref[...] = jnp.dot(prod_s, ec_ref[...], precision=hi, preferred_element_type=F32)

    return pl.pallas_call(
        body, name="attn_out_bwd", grid=(S // tm,),
        in_specs=[_rows(tm, D_MODEL), _rows(tm, D_MODEL), _rows(tm, D_ATT), _rows(tm, D_ATT), _resident(),
                  _resident(), _const((1, D_MODEL)), _resident(), _resident()],
        out_specs=[_rows(tm, D_MODEL), _rows(tm, D_MODEL), pl.BlockSpec((N_HEADS, tm), lambda i: (0, i)),
                   _rows(tm, N_HEADS), _const((1, D_MODEL))],
        out_shape=[jax.ShapeDtypeStruct((S, D_MODEL), MM), jax.ShapeDtypeStruct((S, D_MODEL), MM),
                   jax.ShapeDtypeStruct((N_HEADS, S), F32), jax.ShapeDtypeStruct((S, N_HEADS), F32),
                   jax.ShapeDtypeStruct((1, D_MODEL), F32)],
        compiler_params=_params("arbitrary"),
    )(dh1, mix, fox_o, swa_o, wout_fox, wout_swa, g2, head_rows, head_cols)


def _fox_bwd(fqkv, dcat, lse_row3, d_row3, c_col, tq, tk, pairs_per_loop=2):
    S = fqkv.shape[0]
    n_blk = S // tk
    n_qblk = S // tq
    n_band = tk // tq

    def body(q_ref, k_ref, v_ref, do_ref, lse_ref, dd_ref, ck_ref, dq_ref, dk_ref, dv_ref, dc_ref, dcq_ref):
        kb = pl.program_id(0)

        @pl.when(kb == 0)
        def _():
            dq_ref[...] = jnp.zeros_like(dq_ref)
            dcq_ref[...] = jnp.zeros_like(dcq_ref)

        key = lax.broadcasted_iota(jnp.int32, (tk, tq), 0)
        qry = lax.broadcasted_iota(jnp.int32, (tk, tq), 1)
        low = _head_select((tk, 128), 0)
        for first in range(0, N_HEADS // 2, pairs_per_loop):
            pairs = range(first, first + pairs_per_loop)
            heads = [(pr, hh) for pr in pairs for hh in range(2)]
            kh, vh, ck = {}, {}, {}
            for pr in pairs:
                k2 = k_ref[:, pr * 128:(pr + 1) * 128]
                v2 = v_ref[:, pr * 128:(pr + 1) * 128]
                zero = jnp.zeros_like(k2)
                kh[pr, 0], kh[pr, 1] = jnp.where(low, k2, zero), jnp.where(low, zero, k2)
                vh[pr, 0], vh[pr, 1] = jnp.where(low, v2, zero), jnp.where(low, zero, v2)
                for hh in range(2):
                    ck[pr, hh] = ck_ref[:, 2 * pr + hh:2 * pr + hh + 1]

            def block(qb, carry, band, pairs=pairs, kh=kh, vh=vh, ck=ck):
                rows = pl.ds(pl.multiple_of(qb * tq, tq), tq)
                out = []
                it = iter(carry)
                for pr in pairs:
                    lanes = slice(pr * 128, (pr + 1) * 128)
                    q2 = q_ref[rows, lanes]
                    do2 = do_ref[rows, lanes]
                    dq = None
                    for hh in range(2):
                        h = 2 * pr + hh
                        dk, dv, dc = next(it)
                        s_t = _dot_nt(kh[pr, hh], q2) - ck[pr, hh]
                        p_t = jnp.exp(s_t - lse_ref[h, pl.ds(qb, 1), :])
                        if band is not None:
                            p_t = jnp.where(qry + band * tq >= key, p_t, 0.0)
                        ds_t = p_t * (_dot_nt(vh[pr, hh], do2) - dd_ref[h, pl.ds(qb, 1), :])
                        dsb = ds_t.astype(MM)
                        dv = dv + _dot(p_t.astype(MM), do2)
                        dk = dk + _dot(dsb, q2)
                        dc = dc - jnp.sum(ds_t, axis=1, keepdims=True)
                        part = _dot_tn(dsb, kh[pr, hh])
                        dq = part if dq is None else dq + part
                        dcq_ref[h, pl.ds(qb, 1), :] += jnp.sum(ds_t, axis=0, keepdims=True)
                        out.append((dk, dv, dc))
                    dq_ref[rows, lanes] += dq
                return tuple(out)

            carry = tuple((jnp.zeros((tk, 128), F32), jnp.zeros((tk, 128), F32), jnp.zeros((tk, 1), F32))
                          for _ in heads)
            for band in range(n_band):
                carry = block(kb * n_band + band, carry, band=band)
            carry = lax.fori_loop((kb + 1) * n_band, n_qblk, functools.partial(block, band=None), carry)
            grads = dict(zip(heads, carry))
            for pr in pairs:
                lanes = slice(pr * 128, (pr + 1) * 128)
                dk_ref[:, lanes] = jnp.where(low, grads[pr, 0][0], grads[pr, 1][0]).astype(MM)
                dv_ref[:, lanes] = jnp.where(low, grads[pr, 0][1], grads[pr, 1][1]).astype(MM)
                for hh in range(2):
                    dc_ref[:, 2 * pr + hh:2 * pr + hh + 1] = grads[pr, hh][2]

        @pl.when(kb == n_blk - 1)
        def _():
            dq_ref[...] = dq_ref[...] * Q_SCALE

    return pl.pallas_call(
        body, name="fox_bwd", grid=(n_blk,),
        in_specs=[pl.BlockSpec((S, D_ATT), lambda i: (0, 0)), pl.BlockSpec((tk, D_ATT), lambda i: (i, 1)),
                  pl.BlockSpec((tk, D_ATT), lambda i: (i, 2)), pl.BlockSpec((S, D_ATT), lambda i: (0, 0)),
                  _resident(), _resident(), _rows(tk, N_HEADS)],
        out_specs=[_const((S, D_ATT)), _rows(tk, D_ATT), _rows(tk, D_ATT), _rows(tk, N_HEADS),
                   _const((N_HEADS, n_qblk, tq))],
        out_shape=[jax.ShapeDtypeStruct((S, D_ATT), F32), jax.ShapeDtypeStruct((S, D_ATT), MM),
                   jax.ShapeDtypeStruct((S, D_ATT), MM), jax.ShapeDtypeStruct((S, N_HEADS), F32),
                   jax.ShapeDtypeStruct((N_HEADS, n_qblk, tq), F32)],
        compiler_params=_params("arbitrary"),
    )(fqkv, fqkv, fqkv, dcat, lse_row3, d_row3, c_col)


def _swa_bwd(sqkv, dcat, biasm, sinks_slot, bucket, lse, d_col):
    S = sqkv.shape[0]
    n_blk = S // WINDOW

    def body(q_ref, kp_ref, kc_ref, vp_ref, vc_ref, do_ref, bias_ref, sink_ref, bk_ref, lse_ref, dd_ref,
             dq_ref, dk_ref, dv_ref, drb_ref, dsink_ref, ds_acc):
        n = pl.program_id(0)

        @pl.when(n == 0)
        def _():
            dk_ref[...] = jnp.zeros_like(dk_ref)
            dv_ref[...] = jnp.zeros_like(dv_ref)
            ds_acc[...] = jnp.zeros_like(ds_acc)
            dsink_ref[...] = jnp.zeros_like(dsink_ref)

        no_prev = jnp.where(n > 0, 0.0, NEG)
        prev = pl.ds(pl.multiple_of(jnp.maximum(n - 1, 0) * WINDOW, WINDOW), WINDOW)
        cur = pl.ds(pl.multiple_of(n * WINDOW, WINDOW), WINDOW)
        lane8 = lax.broadcasted_iota(jnp.int32, (1, N_HEADS), 1)
        dkp = jnp.zeros((WINDOW, D_KV), F32)
        dkc = jnp.zeros((WINDOW, D_KV), F32)
        dvp = jnp.zeros((WINDOW, D_KV), F32)
        dvc = jnp.zeros((WINDOW, D_KV), F32)
        dsink = jnp.zeros((1, N_HEADS), F32)
        for j in range(N_HEADS // 2):
            lanes = slice(j * 128, (j + 1) * 128)
            q2 = q_ref[:, lanes]
            do2 = do_ref[:, lanes]
            dqs = []
            for hh in range(2):
                s_ = 2 * j + hh
                sel = _head_select((WINDOW, 128), hh)
                qh = jnp.where(sel, q2, jnp.zeros_like(q2))
                doh = jnp.where(sel, do2, jnp.zeros_like(do2))
                lse_h = lse_ref[:, s_:s_ + 1]
                dd = dd_ref[:, s_:s_ + 1]
                pp = jnp.exp(_dot_nt(qh, kp_ref[...]) + bias_ref[s_, :, :WINDOW] + no_prev - lse_h)
                pc = jnp.exp(_dot_nt(qh, kc_ref[...]) + bias_ref[s_, :, WINDOW:] - lse_h)
                p_sink = jnp.exp(sink_ref[s_] - lse_h)
                dsp = pp * (_dot_nt(doh, vp_ref[...]) - dd)
                dsc = pc * (_dot_nt(doh, vc_ref[...]) - dd)
                dsink = dsink + jnp.where(lane8 == s_, -jnp.sum(p_sink * dd), 0.0)
                ds_acc[s_, :, :WINDOW] += dsp
                ds_acc[s_, :, WINDOW:] += dsc
                dspb, dscb = dsp.astype(MM), dsc.astype(MM)
                dqs.append(_dot(dspb, kp_ref[...]) + _dot(dscb, kc_ref[...]))
                dkp = dkp + _dot_tn(dspb, qh)
                dkc = dkc + _dot_tn(dscb, qh)
                dvp = dvp + _dot_tn(pp.astype(MM), doh)
                dvc = dvc + _dot_tn(pc.astype(MM), doh)
            dq_ref[:, lanes] = (jnp.where(_head_select((WINDOW, 128), 0), dqs[0], dqs[1]) * Q_SCALE).astype(MM)
        dk_ref[prev, :] += dkp
        dk_ref[cur, :] += dkc
        dv_ref[prev, :] += dvp
        dv_ref[cur, :] += dvc
        dsink_ref[...] += dsink

        @pl.when(n == n_blk - 1)
        def _():
            bk = bk_ref[...]
            rb = lax.broadcasted_iota(jnp.int32, (N_BUCKETS, N_HEADS), 0)
            cb = lax.broadcasted_iota(jnp.int32, (N_BUCKETS, N_HEADS), 1)
            out = jnp.zeros((N_BUCKETS, N_HEADS), F32)
            for s in range(N_HEADS):
                acc = ds_acc[s]
                for b in range(N_BUCKETS):
                    out = out + jnp.where((rb == b) & (cb == s), jnp.sum(jnp.where(bk == b, acc, 0.0)), 0.0)
            drb_ref[...] = out

    do_spec = pl.BlockSpec((WINDOW, D_ATT), lambda n: (n, 1))
    return pl.pallas_call(
        body, name="swa_bwd", grid=(n_blk,),
        in_specs=_swa_specs(S) + [do_spec, _resident(), pl.BlockSpec(memory_space=pltpu.SMEM), _resident(),
                                  _rows(WINDOW, N_HEADS), _rows(WINDOW, N_HEADS)],
        out_specs=[_rows(WINDOW, D_ATT), _const((S, D_KV)), _const((S, D_KV)), _const((N_BUCKETS, N_HEADS)),
                   _const((1, N_HEADS))],
        out_shape=[jax.ShapeDtypeStruct((S, D_ATT), MM), jax.ShapeDtypeStruct((S, D_KV), F32),
                   jax.ShapeDtypeStruct((S, D_KV), F32), jax.ShapeDtypeStruct((N_BUCKETS, N_HEADS), F32),
                   jax.ShapeDtypeStruct((1, N_HEADS), F32)],
        scratch_shapes=[pltpu.VMEM((N_HEADS, WINDOW, 2 * WINDOW), F32)],
        compiler_params=_params("arbitrary"),
    )(sqkv, sqkv, sqkv, sqkv, sqkv, dcat, biasm, sinks_slot, bucket, lse, d_col)


def _pre_attn_bwd(x, dh1, dq_fox, dk_fox, dv_fox, dsq, dsk, dsv, dff_t, wt, g1, tm):
    S = x.shape[0]

    def body(x_ref, dh1_ref, dq_ref, dk_ref, dv_ref, dsq_ref, dsk_ref, dsv_ref, dff_ref, wt_ref, g1_ref,
             dx_ref, dz_ref, dg1_ref):
        i = pl.program_id(0)
        dq = dq_ref[...].astype(MM)
        dsk = dsk_ref[...].astype(MM)
        dsv = dsv_ref[...].astype(MM)
        dz_ref[:, 0:512] = dq
        dz_ref[:, 512:1024] = dk_ref[...]
        dz_ref[:, 1024:1536] = dv_ref[...]
        dz_ref[:, 1536:2048] = dsq_ref[...]
        dz_ref[:, 2048:2176] = dsk
        dz_ref[:, 2176:2304] = dsv
        da = (_dot(dq, wt_ref[0:512]) + _dot(dk_ref[...], wt_ref[512:1024]) + _dot(dv_ref[...], wt_ref[1024:WT_FF])
              + _dot(dsq_ref[...], wt_ref[WT_SQ:WT_SKV]) + _dot(dsk, wt_ref[WT_SKV:WT_SKV + D_KV])
              + _dot(dsv, wt_ref[WT_SKV + D_KV:WT_ROWS]) + _dot_tn(dff_ref[...].astype(MM), wt_ref[WT_FF:WT_SQ]))
        n1, r1 = _rms(x_ref[...])
        dx, dg1 = _rms_bwd(da, n1, r1, g1_ref[...])
        _accumulate(dg1_ref, dg1, i)
        dx_ref[...] = dh1_ref[...] + dx

    return pl.pallas_call(
        body, name="pre_attn_bwd", grid=(S // tm,),
        in_specs=[_rows(tm, D_MODEL), _rows(tm, D_MODEL), _rows(tm, D_ATT), _rows(tm, D_ATT), _rows(tm, D_ATT),
                  _rows(tm, D_ATT), _rows(tm, D_KV), _rows(tm, D_KV), pl.BlockSpec((16, tm), lambda i: (0, i)),
                  _resident(), _const((1, D_MODEL))],
        out_specs=[_rows(tm, D_MODEL), _rows(tm, 2304), _const((1, D_MODEL))],
        out_shape=[jax.ShapeDtypeStruct((S, D_MODEL), F32), jax.ShapeDtypeStruct((S, 2304), MM),
                   jax.ShapeDtypeStruct((1, D_MODEL), F32)],
        compiler_params=_params("arbitrary"),
    )(x, dh1, dq_fox, dk_fox, dv_fox, dsq, dsk, dsv, dff_t, wt, g1)


def _weight_grad(a, b, name, tk, n_chunks=1, relu2=False):
    S, K = a.shape
    N = b.shape[1]
    cn = N // n_chunks

    def body(a_ref, b_ref, out_ref):
        av = a_ref[...]
        if relu2:
            av = jnp.square(jnp.maximum(av.astype(F32), 0.0))
        av = av.astype(MM)
        for j in range(n_chunks):
            val = _dot_tn(av, b_ref[:, j * cn:(j + 1) * cn].astype(MM)).astype(MM)
            if n_chunks > 1:
                out_ref[j] = val
            else:
                out_ref[...] = val

    if n_chunks > 1:
        out_spec = pl.BlockSpec((n_chunks, tk, cn), lambda i: (0, i, 0))
        out_shape = jax.ShapeDtypeStruct((n_chunks, K, cn), MM)
    else:
        out_spec = pl.BlockSpec((tk, N), lambda i: (i, 0))
        out_shape = jax.ShapeDtypeStruct((K, N), MM)
    return pl.pallas_call(
        body, name=name, grid=(K // tk,),
        in_specs=[pl.BlockSpec((S, tk), lambda i: (0, i)), _resident()],
        out_specs=out_spec, out_shape=out_shape, compiler_params=_params("parallel"),
    )(a, b)


def _forget_weight_grad(dff_t, a):
    def body(d_ref, a_ref, out_ref):
        out_ref[...] = _dot(d_ref[...].astype(MM), a_ref[...])

    return pl.pallas_call(
        body, name="forget_weight_grad", out_shape=jax.ShapeDtypeStruct((16, D_MODEL), F32),
        in_specs=[_resident(), _resident()], out_specs=_resident(),
    )(dff_t, a)


def _place():
    return lax.axis_index("x"), lax.axis_index("y"), lax.axis_index("c")


def _all_gather_sequencer(stack):
    ref = jax.new_ref(stack, memory_space=pltpu.MemorySpace.HBM)

    @pl.kernel(mesh=plsc.ScalarSubcoreMesh(axis_name="sequencer", num_cores=1), name="all_gather_sequencer",
               scratch_types=(pltpu.SemaphoreType.DMA((7,)), pltpu.SemaphoreType.DMA((7,))),
               compiler_params=pltpu.CompilerParams(collective_id=1))
    def launch(send_sems, recv_sems):
        x, y, c = _place()
        sibling = (x, y, 1 - c)
        chips = [(1 - x, y), (x, 1 - y), (1 - x, 1 - y)]
        peers = [sibling] + [(px, py, c) for px, py in chips]
        barrier = pltpu.get_barrier_semaphore()
        for peer in peers:
            pl.semaphore_signal(barrier, inc=1, device_id=peer, device_id_type=MESH)
        pl.semaphore_wait(barrier, len(peers))

        def copy(k, block, to):
            px, py, pc = block
            slot = ref.at[4 * px + 2 * py + pc]
            return _remote(slot, slot, send_sems, recv_sems, k, to)

        first = [copy(k, (x, y, c), peer) for k, peer in enumerate(peers)]
        for cp in first:
            cp.start()
        passed = []
        for j, (px, py) in enumerate(chips):
            copy(1 + j, (px, py, c), sibling).wait_recv()
            passed.append(copy(4 + j, (px, py, c), sibling))
            passed[-1].start()
        copy(0, (x, y, 1 - c), sibling).wait_recv()
        for j, (px, py) in enumerate(chips):
            copy(4 + j, (px, py, 1 - c), sibling).wait_recv()
        for cp in first + passed:
            cp.wait_send()

    launch()
    return ref[...]


def _chip_sum(grad, other, name):
    _, _, r, cdim = grad.shape
    tr = 256 if r % 256 == 0 else r

    def body(c_ref, g_ref, o_ref, out_ref):
        out_ref[...] = (g_ref[...].astype(F32) + o_ref[...].astype(F32)).astype(out_ref.dtype)

    return pl.pallas_call(
        body, name=name,
        grid_spec=pltpu.PrefetchScalarGridSpec(
            num_scalar_prefetch=1, grid=(4, r // tr),
            in_specs=[pl.BlockSpec((None, None, tr, cdim), lambda k, i, c_ref: (k, c_ref[0], i, 0)),
                      pl.BlockSpec((None, tr, cdim), lambda k, i, c_ref: (k, i, 0))],
            out_specs=pl.BlockSpec((None, tr, cdim), lambda k, i, c_ref: (k, i, 0))),
        out_shape=jax.ShapeDtypeStruct((4, r, cdim), MM),
        compiler_params=_params("parallel", "parallel"),
    )(lax.axis_index("c").astype(jnp.int32).reshape(1), grad, other)


HBM_SPEC = pl.BlockSpec(memory_space=pltpu.HBM)
SEM_SPEC = pl.BlockSpec(memory_space=pltpu.SEMAPHORE)
DATAFLOW = pltpu.SideEffectType.DATAFLOW_SIDE_EFFECTING


def _exchange_start(name, arrays, n_copies, plan):
    n = len(arrays)

    def body(*refs):
        send_sems, recv_sems, token = refs[n], refs[n + 1], refs[2 * n + 2]
        for cp in plan(refs[:n], send_sems, recv_sems):
            cp.start()
        token[...] = jnp.zeros_like(token)

    out = pl.pallas_call(
        body, name=name,
        out_shape=(pltpu.SemaphoreType.DMA((n_copies,)), pltpu.SemaphoreType.DMA((n_copies,)),
                   *[pltpu.HBM(a.shape, a.dtype) for a in arrays], jax.ShapeDtypeStruct((1, D_MODEL), F32)),
        in_specs=[HBM_SPEC] * n,
        out_specs=(SEM_SPEC, SEM_SPEC, *[HBM_SPEC] * n, pl.BlockSpec(memory_space=pltpu.VMEM)),
        input_output_aliases={i: 2 + i for i in range(n)},
        compiler_params=pltpu.CompilerParams(has_side_effects=DATAFLOW),
    )(*[pltpu.with_memory_space_constraint(a, pltpu.HBM) for a in arrays])
    return (out[0], out[1]), list(out[2:2 + n]), out[2 + n]


def _exchange_wait(name, arrays, sems, after, plan):
    n = len(arrays)
    after = list(after) if isinstance(after, (list, tuple)) else [after]

    def body(*refs):
        send_sems, recv_sems = refs[n], refs[n + 1]
        for cp in plan(refs[:n], send_sems, recv_sems):
            cp.wait_send()
            cp.wait_recv()

    out = pl.pallas_call(
        body, name=name, out_shape=[pltpu.HBM(a.shape, a.dtype) for a in arrays],
        in_specs=[HBM_SPEC] * n + [SEM_SPEC, SEM_SPEC] + [pl.BlockSpec(memory_space=pl.ANY)] * len(after),
        out_specs=[HBM_SPEC] * n, input_output_aliases={i: i for i in range(n)},
        compiler_params=pltpu.CompilerParams(has_side_effects=DATAFLOW),
    )(*arrays, sems[0], sems[1], *after)
    return list(out)


def _remote(src, dst, send_sems, recv_sems, k, to):
    return pltpu.make_async_remote_copy(src_ref=src, dst_ref=dst, send_sem=send_sems.at[k], recv_sem=recv_sems.at[k],
                                        device_id=to, device_id_type=MESH)


def _plan_gather_direct(refs, send_sems, recv_sems):
    x, y, c = _place()
    me = 4 * x + 2 * y + c
    peers = [(x, y, 1 - c), (1 - x, y, c), (x, 1 - y, c), (1 - x, 1 - y, c)]
    return [_remote(ref.at[me], ref.at[me], send_sems, recv_sems, 4 * a + k, peer)
            for a, ref in enumerate(refs) for k, peer in enumerate(peers)]


def _plan_gather_pass_on(refs, send_sems, recv_sems):
    x, y, c = _place()
    chips = [(1 - x, y), (x, 1 - y), (1 - x, 1 - y)]
    return [_remote(ref.at[4 * px + 2 * py + c], ref.at[4 * px + 2 * py + c], send_sems, recv_sems, 3 * a + k,
                    (x, y, 1 - c))
            for a, ref in enumerate(refs) for k, (px, py) in enumerate(chips)]


def _plan_in_chip(refs, send_sems, recv_sems):
    n = len(refs) // 2
    x, y, c = _place()
    return [_remote(refs[a].at[:, 1 - c], refs[n + a], send_sems, recv_sems, a, (x, y, 1 - c)) for a in range(n)]


def _plan_between_chips(refs, send_sems, recv_sems):
    n = len(refs) // 2
    x, y, c = _place()
    chips = [(1 - x, y), (x, 1 - y), (1 - x, 1 - y)]
    return [_remote(refs[a].at[2 * px + py], refs[n + a].at[2 * x + y], send_sems, recv_sems, 3 * a + k, (px, py, c))
            for a in range(n) for k, (px, py) in enumerate(chips)]


def _plan_late_between(refs, send_sems, recv_sems):
    sums, land, small = refs
    x, y, c = _place()
    me = 4 * x + 2 * y + c
    copies = _plan_between_chips([sums, land], send_sems, recv_sems)
    peers = [(x ^ dx, y ^ dy, c ^ dc) for dx in range(2) for dy in range(2) for dc in range(2) if dx + dy + dc]
    return copies + [_remote(small.at[me], small.at[me], send_sems, recv_sems, 3 + k, peer)
                     for k, peer in enumerate(peers)]


def _adamw_math(w, g, m, v):
    m = ADAM_B1 * m + (1.0 - ADAM_B1) * g
    v = ADAM_B2 * v + (1.0 - ADAM_B2) * jnp.square(g)
    m_hat = m / (1.0 - ADAM_B1 ** ADAM_STEP)
    v_hat = v / (1.0 - ADAM_B2 ** ADAM_STEP)
    delta = -ADAM_LR * (m_hat / (jnp.sqrt(v_hat) + ADAM_EPS) + ADAM_WD * w)
    return delta, m, v


def _adamw(parts, w, m, v, name):
    n_parts, r, cdim = parts.shape
    tr = 256 if r % 256 == 0 else r

    def body(p_ref, w_ref, m_ref, v_ref, g_out, d_out, m_out, v_out):
        g = p_ref[0].astype(F32)
        for k in range(1, n_parts):
            g = g + p_ref[k].astype(F32)
        delta, m_new, v_new = _adamw_math(w_ref[...], g, m_ref[...], v_ref[...])
        g_out[...] = g
        d_out[...] = delta
        m_out[...] = m_new
        v_out[...] = v_new

    blk = pl.BlockSpec((tr, cdim), lambda i: (i, 0))
    return pl.pallas_call(
        body, name=name, grid=(r // tr,),
        in_specs=[pl.BlockSpec((n_parts, tr, cdim), lambda i: (0, i, 0)), blk, blk, blk],
        out_specs=[blk] * 4, out_shape=[jax.ShapeDtypeStruct((r, cdim), F32)] * 4,
        compiler_params=_params("parallel"),
    )(parts, w, m, v)


def _adamw_chips(parts, sums, w, m, v, name):
    _, r, cdim = parts.shape
    tr = 256 if r % 256 == 0 else r

    def body(chip_ref, p_ref, own_ref, w_ref, m_ref, v_ref, g_out, d_out, m_out, v_out):
        g = None
        for k in range(4):
            term = jnp.where(chip_ref[0] == k, own_ref[...], p_ref[k]).astype(F32)
            g = term if g is None else g + term
        delta, m_new, v_new = _adamw_math(w_ref[...], g, m_ref[...], v_ref[...])
        g_out[...] = g
        d_out[...] = delta
        m_out[...] = m_new
        v_out[...] = v_new

    blk = pl.BlockSpec((tr, cdim), lambda i, chip: (i, 0))
    my_chip = (2 * lax.axis_index("x") + lax.axis_index("y")).astype(jnp.int32).reshape(1)
    return pl.pallas_call(
        body, name=name,
        grid_spec=pltpu.PrefetchScalarGridSpec(
            num_scalar_prefetch=1, grid=(r // tr,),
            in_specs=[pl.BlockSpec((4, tr, cdim), lambda i, chip: (0, i, 0)),
                      pl.BlockSpec((None, tr, cdim), lambda i, chip: (chip[0], i, 0)), blk, blk, blk],
            out_specs=[blk] * 4),
        out_shape=[jax.ShapeDtypeStruct((r, cdim), F32)] * 4,
        compiler_params=_params("parallel"),
    )(my_chip, parts, sums, w, m, v)


class _NoExchange:
    def __init__(self, weights):
        self.weights = weights

    def before_pre_attn(self, g1):
        return g1

    def after_fox_fwd(self, fox_o, sinks_slot):
        return sinks_slot

    def after_attention(self, swa_o):
        return self.weights

    def after_early_grads(self, grads, d_col):
        return d_col

    def after_swa_bwd(self, dsq, d_row3):
        return d_row3


def _slot_order(t, axis):
    shp = t.shape
    t = t.reshape(shp[:axis] + (N_HEADS, shp[axis] // N_HEADS) + shp[axis + 1:])
    t = jnp.take(t, np.array(SLOT_HEAD), axis=axis)
    return t.reshape(shp)


def _head_order(t, axis):
    shp = t.shape
    t = t.reshape(shp[:axis] + (N_HEADS, shp[axis] // N_HEADS) + shp[axis + 1:])
    t = jnp.take(t, np.array(HEAD_SLOT), axis=axis)
    return t.reshape(shp)


def _forward_backward(x, p, target, win_t, hooks, b_forget, rel_bias, sinks, g1, g2, g3, g4, g5):
    S = x.shape[0]
    tm = 256
    t = 256
    q0 = 3 * D_ATT + N_HEADS
    wt = jnp.concatenate(
        [win_t[:q0], jnp.zeros((8, D_MODEL), MM)]
        + [win_t[q0 + HEAD_DIM * h:q0 + HEAD_DIM * (h + 1)] for h in SLOT_HEAD] + [win_t[q0 + D_ATT:]], axis=0)
    bcol =jnp.pad(b_forget.reshape(N_HEADS, 1), ((0, 8), (0, 0)))
    rel_bias_slot = rel_bias[:, np.array(SLOT_HEAD)]
    sinks_slot = sinks.reshape(N_HEADS)[np.array(SLOT_HEAD)]
    bucket = jnp.asarray(_swa_bucket_map())

    a, fqkv, sqkv, fft = _pre_attn(x, hooks.before_pre_attn(g1), wt, tm)
    c_row = _forget_cumsum(fft, bcol)
    c_col = c_row[:N_HEADS].T
    c_row3 = c_row[:N_HEADS].reshape(N_HEADS, S // t, t)
    fox_o, fox_lse = _fox_fwd(fqkv, c_row3, tq=512, tk=t)
    biasm = _swa_bias(rel_bias_slot, bucket)
    sinks_slot = hooks.after_fox_fwd(fox_o, sinks_slot)
    swa_o, swa_lse = _swa_fwd(sqkv, biasm, sinks_slot)
    wout, w1, w2, wple, wg = hooks.after_attention(swa_o)
    wout_fox = wout[:D_ATT]
    wout_swa = _slot_order(wout[D_ATT:], 0)
    mix, h1, m = _post_attn(x, fox_o, swa_o, wout_fox, wout_swa, g2, g3, tm)
    u, y, h2 = _mlp_fwd(m, h1, w1, w2, g4, tm)
    dh2, dpe, dgl, dg5, loss = _ple_loss(h2, p, target, wg, wple, g5, tm)

    d_wple = _weight_grad(p, dpe, "grad_w_ple", tk=D_PLE, n_chunks=N_DEV)
    d_wg = _weight_grad(h2, dgl, "grad_w_ple_gate", tk=256)
    dh1, dy, du, dg4, dg3 = _mlp_bwd(dh2, y, h1, u, w1, w2, g4, g3, tm)
    d_w2 = _weight_grad(u, dy, "grad_w_ff2", tk=256, relu2=True)
    d_w1 = _weight_grad(m, du, "grad_w_ff1", tk=256, n_chunks=N_DEV)
    head = np.arange(D_ATT) // HEAD_DIM
    head_rows = jnp.asarray((head[None, :] == np.arange(N_HEADS)[:, None]).astype(np.float32))
    dmix, dcat, d_row, d_col, dg2 = _attn_out_bwd(dh1, mix, fox_o, swa_o, wout_fox, wout_swa, g2,
                                                  head_rows, head_rows.T, tm)
    d_wout_fox = _weight_grad(fox_o, dmix, "grad_w_out_fox", tk=256)
    d_wout_swa = _weight_grad(swa_o, dmix, "grad_w_out_swa", tk=256)
    d_wout = jnp.concatenate([d_wout_fox, _head_order(d_wout_swa, 0)], axis=0).reshape(N_DEV, D_MODEL // N_DEV, D_MODEL)
    early = dict(w_ff1=d_w1, w_ff2=d_w2.reshape(N_DEV, FF_CHUNK, D_MODEL), w_ple=d_wple,
                 w_ple_gate=d_wg.reshape(N_DEV, D_MODEL // N_DEV, D_MODEL), w_out=d_wout)

    d_col = hooks.after_early_grads(early, d_col)
    dsq, dsk, dsv, d_rb_slot, d_sink_slot = _swa_bwd(sqkv, dcat, biasm, sinks_slot, bucket, swa_lse, d_col)
    lse_row3 = fox_lse.T.reshape(N_HEADS, S // t, t)
    d_row3 = hooks.after_swa_bwd(dsq, d_row.reshape(N_HEADS, S // t, t))
    dq_fox, dk_fox, dv_fox, dc_col, dcq = _fox_bwd(fqkv, dcat, lse_row3, d_row3, c_col, tq=t, tk=512)
    dc_row = jnp.pad(dc_col.T + dcq.reshape(N_HEADS, S), ((0, 8), (0, 0)))
    dff_t, db = _forget_bwd(dc_row, fft, bcol)
    grad_x, dz, dg1 = _pre_attn_bwd(x, dh1, dq_fox, dk_fox, dv_fox, dsq, dsk, dsv, dff_t, wt, g1, tm)
    d_wmain = _weight_grad(dz, a, "grad_w_in", tk=256)
    d_wff_t = _forget_weight_grad(dff_t, a)

    sq0 = 3 * D_ATT
    d_win = jnp.concatenate(
        [d_wmain[:sq0], d_wff_t[:N_HEADS].astype(MM)]
        + [d_wmain[sq0 + HEAD_DIM * s:sq0 + HEAD_DIM * (s + 1)] for s in HEAD_SLOT] + [d_wmain[sq0 + D_ATT:]], axis=0)
    d_win = d_win.reshape(N_DEV, D_IN // N_DEV, D_MODEL)
    big = dict(early, w_in=d_win)
    small = dict(b_forget=db[:N_HEADS].reshape(1, N_HEADS), rel_bias=d_rb_slot[:, np.array(HEAD_SLOT)],
                 swa_sinks=d_sink_slot[:, np.array(HEAD_SLOT)], g_attn_pre=dg1, g_attn_post=dg2, g_ff_pre=dg3,
                 g_ff_post=dg4, g_ple_post=dg5)
    return loss, grad_x, big, small, dz


BIG = ("w_in", "w_out", "w_ff1", "w_ff2", "w_ple", "w_ple_gate")
SMALL_ROWS = ("g_attn_pre", "g_attn_post", "g_ff_pre", "g_ff_post", "g_ple_post")
WEIGHTS = ("w_in", "b_forget", "w_out", "rel_bias", "swa_sinks", "g_attn_pre", "g_attn_post", "w_ff1", "w_ff2",
           "g_ff_pre", "g_ff_post", "w_ple", "w_ple_gate", "g_ple_post")


EARLY = ("w_ff1", "w_ff2", "w_ple", "w_ple_gate", "w_out")


class _Overlap:
    def __init__(self, later):
        self.later = later

    def before_pre_attn(self, g1):
        self.gather_sems, self.later, token = _exchange_start("gather_rest_start", self.later, 4 * 5, _plan_gather_direct)
        return g1 + token

    def after_fox_fwd(self, fox_o, sinks_slot):
        later = _exchange_wait("gather_rest_wait", self.later, self.gather_sems, fox_o, _plan_gather_direct)
        self.pass_sems, self.later, token = _exchange_start("gather_pass_on_start", later, 3 * 5, _plan_gather_pass_on)
        return sinks_slot + token[0, :N_HEADS]

    def after_attention(self, swa_o):
        wout_g, w1_g, w2_g, wple_g, wg_g = _exchange_wait("gather_pass_on_wait", self.later, self.pass_sems, swa_o,
                                                         _plan_gather_pass_on)
        return (wout_g.reshape(D_MODEL, D_MODEL), w1_g, w2_g.reshape(D_FF, D_MODEL),
                jnp.moveaxis(wple_g, 0, 1).reshape(D_PLE, D_MODEL), wg_g.reshape(D_MODEL, D_MODEL))

    def after_early_grads(self, grads, d_col):
        views = [grads[k].reshape((4, 2) + grads[k].shape[1:]) for k in EARLY]
        lands = [lax.empty((4,) + grads[k].shape[1:], MM) for k in EARLY]
        self.in_chip_sems, self.in_chip, token = _exchange_start("grads_in_chip_start", views + lands, len(EARLY),
                                                                 _plan_in_chip)
        return d_col + token[0, 0]

    def after_swa_bwd(self, dsq, d_row3):
        arrays = _exchange_wait("grads_in_chip_wait", self.in_chip, self.in_chip_sems, dsq, _plan_in_chip)
        n = len(EARLY)
        sums = [_chip_sum(arrays[a], arrays[n + a], "chip_sum_" + k) for a, k in enumerate(EARLY)]
        lands = [lax.empty(s.shape, s.dtype) for s in sums]
        self.between_sems, self.between, token = _exchange_start("grads_between_chips_start", sums + lands, 3 * n,
                                                                 _plan_between_chips)
        return d_row3 + token[0, 0]

    def finish(self, after):
        arrays = _exchange_wait("grads_between_chips_wait", self.between, self.between_sems, after,
                                _plan_between_chips)
        n = len(EARLY)
        self.sums = arrays[:n]
        return arrays[n:]


def _pack_small(t):
    rows = [t[k].reshape(1, D_MODEL) for k in SMALL_ROWS]
    misc = jnp.concatenate([t["b_forget"].reshape(-1), t["swa_sinks"].reshape(-1), t["rel_bias"].reshape(-1)])
    rows.append(jnp.pad(misc, (0, D_MODEL - misc.shape[0])).reshape(1, D_MODEL))
    rows.append(jnp.pad(t["loss"].reshape(-1), (0, D_MODEL - 1)).reshape(1, D_MODEL))
    rows.append(jnp.zeros((1, D_MODEL), F32))
    return jnp.concatenate(rows, axis=0).astype(F32)


def _unpack_small(blk):
    out = {k: blk[i].reshape(1, D_MODEL) for i, k in enumerate(SMALL_ROWS)}
    misc = blk[len(SMALL_ROWS)]
    out["b_forget"] = misc[:N_HEADS].reshape(1, N_HEADS)
    out["swa_sinks"] = misc[N_HEADS:2 * N_HEADS].reshape(1, N_HEADS)
    out["rel_bias"] = misc[2 * N_HEADS:2 * N_HEADS + N_BUCKETS * N_HEADS].reshape(N_BUCKETS, N_HEADS)
    out["loss"] = blk[len(SMALL_ROWS) + 1, 0]
    return out


def kernel(x, p, w_in, b_forget, w_out, rel_bias, swa_sinks, g_attn_pre, g_attn_post, w_ff1, w_ff2, g_ff_pre, g_ff_post, w_ple, w_ple_gate, g_ple_post, loss_target, m_w_in, m_b_forget, m_w_out, m_rel_bias, m_swa_sinks, m_g_attn_pre, m_g_attn_post, m_w_ff1, m_w_ff2, m_g_ff_pre, m_g_ff_post, m_w_ple, m_w_ple_gate, m_g_ple_post, v_w_in, v_b_forget, v_w_out, v_rel_bias, v_swa_sinks, v_g_attn_pre, v_g_attn_post, v_w_ff1, v_w_ff2, v_g_ff_pre, v_g_ff_post, v_w_ple, v_w_ple_gate, v_g_ple_post):
    w = dict(w_in=w_in, b_forget=b_forget, w_out=w_out, rel_bias=rel_bias, swa_sinks=swa_sinks,
             g_attn_pre=g_attn_pre, g_attn_post=g_attn_post, w_ff1=w_ff1, w_ff2=w_ff2, g_ff_pre=g_ff_pre,
             g_ff_post=g_ff_post, w_ple=w_ple, w_ple_gate=w_ple_gate, g_ple_post=g_ple_post)
    mom = dict(w_in=m_w_in, b_forget=m_b_forget, w_out=m_w_out, rel_bias=m_rel_bias, swa_sinks=m_swa_sinks,
               g_attn_pre=m_g_attn_pre, g_attn_post=m_g_attn_post, w_ff1=m_w_ff1, w_ff2=m_w_ff2,
               g_ff_pre=m_g_ff_pre, g_ff_post=m_g_ff_post, w_ple=m_w_ple, w_ple_gate=m_w_ple_gate,
               g_ple_post=m_g_ple_post)
    var = dict(w_in=v_w_in, b_forget=v_b_forget, w_out=v_w_out, rel_bias=v_rel_bias, swa_sinks=v_swa_sinks,
               g_attn_pre=v_g_attn_pre, g_attn_post=v_g_attn_post, w_ff1=v_w_ff1, w_ff2=v_w_ff2,
               g_ff_pre=v_g_ff_pre, g_ff_post=v_g_ff_post, w_ple=v_w_ple, w_ple_gate=v_w_ple_gate,
               g_ple_post=v_g_ple_post)

    turn = lambda t, k: t.T if k == "w_in" else t
    me = 4 * lax.axis_index("x") + 2 * lax.axis_index("y") + lax.axis_index("c")

    def stack(block):
        return lax.dynamic_update_slice_in_dim(lax.empty((N_DEV,) + block.shape, block.dtype), block[None], me, 0)

    stacks = [stack(turn(w[k][0], k).astype(MM)) for k in BIG]
    win_g, later = _all_gather_sequencer(stacks[0]), stacks[1:]
    hooks = _Overlap(later)
    loss, grad_x, big, small, last = _forward_backward(
        x[0], p[0, 0], loss_target[0], win_g.reshape(D_IN, D_MODEL), hooks, b_forget, rel_bias, swa_sinks,
        g_attn_pre, g_attn_post, g_ff_pre, g_ff_post, g_ple_post)
    out_g, out_d, out_m, out_v = {}, {}, {}, {}

    def update(k, part, own):
        g, d, m_new, v_new = _adamw_chips(part, own, turn(w[k][0], k), turn(mom[k][0], k), turn(var[k][0], k),
                                          "adamw_" + k)
        out_g[k], out_d[k], out_m[k], out_v[k] = turn(g, k)[None], turn(d, k)[None], turn(m_new, k)[None], turn(v_new, k)[None]
        return d

    d_win = big["w_in"]
    in_chip_sems, in_chip, token = _exchange_start(
        "late_in_chip_start", [d_win.reshape((4, 2) + d_win.shape[1:]), lax.empty((4,) + d_win.shape[1:], MM)], 1,
        _plan_in_chip)
    early_parts = hooks.finish(token)
    done = [update(k, part, own) for k, part, own in list(zip(EARLY, early_parts, hooks.sums))[:2]]
    view, other = _exchange_wait("late_in_chip_wait", in_chip, in_chip_sems, done, _plan_in_chip)
    chip_sum = _chip_sum(view, other, "chip_sum_w_in")
    small["loss"] = loss
    between_sems, between, token = _exchange_start(
        "late_between_chips_start", [chip_sum, lax.empty(chip_sum.shape, MM), stack(_pack_small(small))], 3 + 7,
        _plan_late_between)
    done = [update(k, part, own) for k, part, own in list(zip(EARLY, early_parts, hooks.sums))[2:]]
    chip_sum, part, small_all = _exchange_wait("late_between_chips_wait", between, between_sems, done,
                                               _plan_late_between)
    update("w_in", part, chip_sum)
    rep ={k: w[k] for k in w if k not in BIG}
    rep["loss"] = jnp.zeros((), F32)
    rep_m = {k: mom[k] for k in mom if k not in BIG}
    rep_m["loss"] = jnp.zeros((), F32)
    rep_v = {k: var[k] for k in var if k not in BIG}
    rep_v["loss"] = jnp.ones((), F32)
    g_s, d_s, m_s, v_s = _adamw(small_all, _pack_small(rep), _pack_small(rep_m), _pack_small(rep_v), "adamw_small")
    g_s, d_s, m_s, v_s = _unpack_small(g_s), _unpack_small(d_s), _unpack_small(m_s), _unpack_small(v_s)
    for k in w:
        if k not in BIG:
            out_g[k], out_d[k], out_m[k], out_v[k] = g_s[k], d_s[k], m_s[k], v_s[k]
    return (g_s["loss"], grad_x[None], *[out_g[k] for k in WEIGHTS], *[out_d[k] for k in WEIGHTS],
            *[out_m[k] for k in WEIGHTS], *[out_v[k] for k in WEIGHTS])
```

```python
import functools

import numpy as np
import jax
import jax.numpy as jnp
from jax import lax
from jax.experimental import pallas as pl
from jax.experimental.pallas import tpu as pltpu
from jax.experimental.pallas import tpu_sc as plsc

F32 = jnp.float32
MM = jnp.bfloat16

D_MODEL = 1024
HEAD_DIM = 64
N_HEADS = 8
D_ATT = N_HEADS * HEAD_DIM
D_KV = 128
D_FF = 4096
D_PLE = 256
D_IN = 3 * D_ATT + N_HEADS + D_ATT + 2 * D_KV
N_DEV = 8
FF_CHUNK = D_FF // N_DEV
WINDOW = 128
N_BUCKETS = 32
MAX_DISTANCE = 128
RMS_EPS = 1e-6
Q_SCALE = HEAD_DIM ** -0.5
NEG = -1e30

ADAM_LR = 0.001
ADAM_B1 = 0.9
ADAM_B2 = 0.999
ADAM_EPS = 1e-08
ADAM_WD = 0.01
ADAM_STEP = 10

SLOT_HEAD = (0, 4, 1, 5, 2, 6, 3, 7)
HEAD_SLOT = (0, 2, 4, 6, 1, 3, 5, 7)

VMEM_LIMIT = 56 * 1024 * 1024
MESH = pl.DeviceIdType.MESH

NT = (((1,), (1,)), ((), ()))
TN = (((0,), (0,)), ((), ()))


def _params(*semantics):
    return pltpu.CompilerParams(dimension_semantics=semantics, vmem_limit_bytes=VMEM_LIMIT)


def _resident():
    return pl.BlockSpec(memory_space=pltpu.VMEM)


def _rows(tm, width):
    return pl.BlockSpec((tm, width), lambda i: (i, 0))


def _const(shape):
    return pl.BlockSpec(shape, lambda i: (0,) * len(shape))


def _dot(a, b):
    return jnp.dot(a, b, preferred_element_type=F32)


def _dot_nt(a, b):
    return lax.dot_general(a, b, NT, preferred_element_type=F32)


def _dot_tn(a, b):
    return lax.dot_general(a, b, TN, preferred_element_type=F32)


def _rms(xf):
    r = lax.rsqrt(jnp.mean(xf * xf, axis=-1, keepdims=True) + RMS_EPS)
    return xf * r, r


def _rms_bwd(dout, n, r, g):
    dg = jnp.sum(dout * n, axis=0, keepdims=True)
    dn = dout * g
    dx = r * (dn - n * jnp.mean(dn * n, axis=-1, keepdims=True))
    return dx, dg


def _accumulate(ref, value, step):
    @pl.when(step == 0)
    def _():
        ref[...] = value

    @pl.when(step != 0)
    def _():
        ref[...] += value


def _t5_bucket(n):
    max_exact = N_BUCKETS // 2
    large = max_exact + (np.log(np.maximum(n, 1) / max_exact) / np.log(MAX_DISTANCE / max_exact)
                         * (N_BUCKETS - max_exact)).astype(np.int32)
    large = np.minimum(large, N_BUCKETS - 1)
    return np.where(n < max_exact, n, large).astype(np.int32)


def _swa_bucket_map():
    i = np.arange(WINDOW)[:, None]
    j = np.arange(2 * WINDOW)[None, :]
    dist = i + WINDOW - j
    ok = (dist >= 0) & (dist < WINDOW)
    return np.where(ok, _t5_bucket(np.clip(dist, 0, None)), -1).astype(np.int32)


WT_FOX = 0
WT_FF = 3 * D_ATT
WT_SQ = WT_FF + 16
WT_SKV = WT_SQ + D_ATT
WT_ROWS = WT_SKV + 2 * D_KV


def _pre_attn(x, g1, wt, tm):
    S = x.shape[0]

    def body(x_ref, g_ref, wt_ref, a_ref, fqkv_ref, sqkv_ref, fft_ref):
        n, _ = _rms(x_ref[...])
        a = (n * g_ref[...]).astype(MM)
        a_ref[...] = a
        fqkv_ref[:, :D_ATT] = (_dot_nt(a, wt_ref[WT_FOX:WT_FOX + D_ATT]) * Q_SCALE).astype(MM)
        fqkv_ref[:, D_ATT:] = _dot_nt(a, wt_ref[WT_FOX + D_ATT:WT_FF]).astype(MM)
        sqkv_ref[:, :D_ATT] = (_dot_nt(a, wt_ref[WT_SQ:WT_SKV]) * Q_SCALE).astype(MM)
        sqkv_ref[:, D_ATT:] = _dot_nt(a, wt_ref[WT_SKV:WT_ROWS]).astype(MM)
        fft_ref[...] = _dot_nt(wt_ref[WT_FF:WT_SQ], a)

    return pl.pallas_call(
        body, name="pre_attn", grid=(S // tm,),
        in_specs=[_rows(tm, D_MODEL), _const((1, D_MODEL)), _resident()],
        out_specs=[_rows(tm, D_MODEL), _rows(tm, 3 * D_ATT), _rows(tm, D_ATT + 2 * D_KV),
                   pl.BlockSpec((16, tm), lambda i: (0, i))],
        out_shape=[jax.ShapeDtypeStruct((S, D_MODEL), MM), jax.ShapeDtypeStruct((S, 3 * D_ATT), MM),
                   jax.ShapeDtypeStruct((S, D_ATT + 2 * D_KV), MM), jax.ShapeDtypeStruct((16, S), F32)],
        compiler_params=_params("parallel"),
    )(x, g1, wt)


def _lane_scan(v, reverse):
    S = v.shape[1]
    lane = lax.broadcasted_iota(jnp.int32, v.shape, 1)
    k = 1
    while k < S:
        if reverse:
            v = v + jnp.where(lane < S - k, pltpu.roll(v, S - k, axis=1), 0.0)
        else:
            v = v + jnp.where(lane >= k, pltpu.roll(v, k, axis=1), 0.0)
        k *= 2
    return v


def _forget_cumsum(fft, bcol):
    def body(f_ref, b_ref, c_ref):
        z = f_ref[...] + b_ref[...]
        log_f = jnp.minimum(z, 0.0) - jnp.log1p(jnp.exp(-jnp.abs(z)))
        c_ref[...] = _lane_scan(log_f, reverse=False)

    return pl.pallas_call(
        body, name="forget_cumsum", out_shape=jax.ShapeDtypeStruct(fft.shape, F32),
        in_specs=[_resident(), _resident()], out_specs=_resident(),
    )(fft, bcol)


def _forget_bwd(dc_row, fft, bcol):
    def body(dc_ref, f_ref, b_ref, dff_ref, db_ref):
        z = f_ref[...] + b_ref[...]
        dlog_f = _lane_scan(dc_ref[...], reverse=True)
        dff = dlog_f * (1.0 / (1.0 + jnp.exp(z)))
        dff_ref[...] = dff
        db_ref[...] = jnp.sum(dff, axis=1, keepdims=True)

    return pl.pallas_call(
        body, name="forget_bwd",
        out_shape=[jax.ShapeDtypeStruct(fft.shape, F32), jax.ShapeDtypeStruct((fft.shape[0], 1), F32)],
        in_specs=[_resident()] * 3, out_specs=[_resident()] * 2,
    )(dc_row, fft, bcol)


def _head_select(shape, upper):
    lane = lax.broadcasted_iota(jnp.int32, shape, 1)
    return lane >= HEAD_DIM if upper else lane < HEAD_DIM


def _fox_fwd(fqkv, c_row3, tq, tk, pairs_per_loop=2, row_chunks=1):
    S = fqkv.shape[0]
    rq = tq // row_chunks
    n_band = tq // tk

    def body(q_ref, k_ref, v_ref, ck_ref, o_ref, lse_ref):
        qi = pl.program_id(0)
        row = lax.broadcasted_iota(jnp.int32, (rq, tk), 0)
        col = lax.broadcasted_iota(jnp.int32, (rq, tk), 1)
        low = _head_select((rq, 128), 0)
        for first in range(0, N_HEADS // 2, pairs_per_loop):
            pairs = range(first, first + pairs_per_loop)
            chains = [(pr, hh, rc) for pr in pairs for hh in range(2) for rc in range(row_chunks)]
            qh = {}
            for pr in pairs:
                for rc in range(row_chunks):
                    q2 = q_ref[rc * rq:(rc + 1) * rq, pr * 128:(pr + 1) * 128]
                    qh[pr, 0, rc] = jnp.where(low, q2, jnp.zeros_like(q2))
                    qh[pr, 1, rc] = jnp.where(low, jnp.zeros_like(q2), q2)

            def block(kb, carry, band, chains=chains, qh=qh):
                rows = pl.ds(pl.multiple_of(kb * tk, tk), tk)
                out = []
                for (pr, hh, rc), (m, l, acc) in zip(chains, carry):
                    if band is not None and (rc + 1) * rq <= band * tk:
                        out.append((m, l, acc))
                        continue
                    lanes = slice(pr * 128, (pr + 1) * 128)
                    s = _dot_nt(qh[pr, hh, rc], k_ref[rows, lanes]) - ck_ref[2 * pr + hh, pl.ds(kb, 1), :]
                    if band is not None:
                        s = jnp.where(row + rc * rq >= col + band * tk, s, NEG)
                    m_new = jnp.maximum(m, jnp.max(s, axis=-1, keepdims=True))
                    p = jnp.exp(s - m_new)
                    alpha = jnp.exp(m - m_new)
                    l = alpha * l + jnp.sum(p, axis=-1, keepdims=True)
                    acc = alpha * acc + _dot(p.astype(MM), v_ref[rows, lanes])
                    out.append((m_new, l, acc))
                return tuple(out)

            carry = tuple((jnp.full((rq, 1), NEG, F32), jnp.zeros((rq, 1), F32), jnp.zeros((rq, 128), F32))
                          for _ in chains)
            carry = lax.fori_loop(0, qi * n_band, functools.partial(block, band=None), carry)
            for band in range(n_band):
                carry = block(qi * n_band + band, carry, band=band)
            res = {}
            for (pr, hh, rc), (m, l, acc) in zip(chains, carry):
                res[pr, hh, rc] = acc / l
                lse_ref[rc * rq:(rc + 1) * rq, 2 * pr + hh:2 * pr + hh + 1] = m + jnp.log(l)
            for pr in pairs:
                for rc in range(row_chunks):
                    o_ref[rc * rq:(rc + 1) * rq, pr * 128:(pr + 1) * 128] = jnp.where(
                        low, res[pr, 0, rc], res[pr, 1, rc]).astype(MM)

    return pl.pallas_call(
        body, name="fox_fwd", grid=(S // tq,),
        in_specs=[pl.BlockSpec((tq, D_ATT), lambda i: (i, 0)), pl.BlockSpec((S, D_ATT), lambda i: (0, 1)),
                  pl.BlockSpec((S, D_ATT), lambda i: (0, 2)), _resident()],
        out_specs=[_rows(tq, D_ATT), _rows(tq, N_HEADS)],
        out_shape=[jax.ShapeDtypeStruct((S, D_ATT), MM), jax.ShapeDtypeStruct((S, N_HEADS), F32)],
        compiler_params=_params("parallel"),
    )(fqkv, fqkv, fqkv, c_row3)


def _swa_bias(rel_bias_slot, bucket):
    def body(rb_ref, bk_ref, out_ref):
        bk = bk_ref[...]
        for s in range(N_HEADS):
            acc = jnp.where(bk < 0, NEG, 0.0).astype(F32)
            for b in range(N_BUCKETS):
                acc = jnp.where(bk == b, rb_ref[b, s], acc)
            out_ref[s] = acc

    return pl.pallas_call(
        body, name="swa_bias", out_shape=jax.ShapeDtypeStruct((N_HEADS, WINDOW, 2 * WINDOW), F32),
        in_specs=[pl.BlockSpec(memory_space=pltpu.SMEM), _resident()], out_specs=_resident(),
    )(rel_bias_slot, bucket)


def _stack4(piece):
    return jnp.concatenate([piece(j) for j in range(4)], axis=0)


def _swa_specs(S):
    q = pl.BlockSpec((WINDOW, D_ATT), lambda n: (n, 0))
    kp = pl.BlockSpec((WINDOW, D_KV), lambda n: (jnp.maximum(n - 1, 0), 4))
    kc = pl.BlockSpec((WINDOW, D_KV), lambda n: (n, 4))
    vp = pl.BlockSpec((WINDOW, D_KV), lambda n: (jnp.maximum(n - 1, 0), 5))
    vc = pl.BlockSpec((WINDOW, D_KV), lambda n: (n, 5))
    return [q, kp, kc, vp, vc]


def _swa_fwd(sqkv, biasm, sinks_slot):
    S = sqkv.shape[0]

    def body(q_ref, kp_ref, kc_ref, vp_ref, vc_ref, bias_ref, sink_ref, o_ref, lse_ref):
        n = pl.program_id(0)
        no_prev = jnp.where(n > 0, 0.0, NEG)
        low = _head_select((WINDOW, 128), 0)
        res = []
        for g in range(2):
            sel = low if g == 0 else jnp.logical_not(low)
            qg = _stack4(lambda j: jnp.where(sel, q_ref[:, j * 128:(j + 1) * 128], jnp.zeros((WINDOW, 128), MM)))
            sink = _stack4(lambda j: jnp.full((WINDOW, 1), sink_ref[2 * j + g], F32))
            sp = _dot_nt(qg, kp_ref[...]) + _stack4(lambda j: bias_ref[2 * j + g, :, :WINDOW]) + no_prev
            sc = _dot_nt(qg, kc_ref[...]) + _stack4(lambda j: bias_ref[2 * j + g, :, WINDOW:])
            m = jnp.maximum(jnp.maximum(jnp.max(sp, axis=-1, keepdims=True),
                                        jnp.max(sc, axis=-1, keepdims=True)), sink)
            ep = jnp.exp(sp - m)
            ec = jnp.exp(sc - m)
            den = jnp.sum(ep, axis=-1, keepdims=True) + jnp.sum(ec, axis=-1, keepdims=True) + jnp.exp(sink - m)
            res.append((_dot(ep.astype(MM), vp_ref[...]) + _dot(ec.astype(MM), vc_ref[...])) / den)
            lse = m + jnp.log(den)
            for j in range(4):
                lse_ref[:, 2 * j + g:2 * j + g + 1] = lse[j * WINDOW:(j + 1) * WINDOW]
        for j in range(4):
            rows = slice(j * WINDOW, (j + 1) * WINDOW)
            o_ref[:, j * 128:(j + 1) * 128] = jnp.where(low, res[0][rows], res[1][rows]).astype(MM)

    return pl.pallas_call(
        body, name="swa_fwd", grid=(S // WINDOW,),
        in_specs=_swa_specs(S) + [_resident(), pl.BlockSpec(memory_space=pltpu.SMEM)],
        out_specs=[_rows(WINDOW, D_ATT), _rows(WINDOW, N_HEADS)],
        out_shape=[jax.ShapeDtypeStruct((S, D_ATT), MM), jax.ShapeDtypeStruct((S, N_HEADS), F32)],
        compiler_params=_params("parallel"),
    )(sqkv, sqkv, sqkv, sqkv, sqkv, biasm, sinks_slot)


def _post_attn(x, fox_o, swa_o, wout_fox, wout_swa, g2, g3, tm):
    S = x.shape[0]

    def body(x_ref, fo_ref, so_ref, wf_ref, ws_ref, g2_ref, g3_ref, mix_ref, h1_ref, m_ref):
        mix = _dot(fo_ref[...], wf_ref[...]) + _dot(so_ref[...], ws_ref[...])
        mix_ref[...] = mix
        n2, _ = _rms(mix)
        h1 = x_ref[...] + n2 * g2_ref[...]
        h1_ref[...] = h1
        n3, _ = _rms(h1)
        m_ref[...] = (n3 * g3_ref[...]).astype(MM)

    return pl.pallas_call(
        body, name="post_attn", grid=(S // tm,),
        in_specs=[_rows(tm, D_MODEL), _rows(tm, D_ATT), _rows(tm, D_ATT), _resident(), _resident(),
                  _const((1, D_MODEL)), _const((1, D_MODEL))],
        out_specs=[_rows(tm, D_MODEL)] * 3,
        out_shape=[jax.ShapeDtypeStruct((S, D_MODEL), F32), jax.ShapeDtypeStruct((S, D_MODEL), F32),
                   jax.ShapeDtypeStruct((S, D_MODEL), MM)],
        compiler_params=_params("parallel"),
    )(x, fox_o, swa_o, wout_fox, wout_swa, g2, g3)


def _mlp_fwd(m, h1, w1, w2, g4, tm):
    S = m.shape[0]

    def body(m_ref, h1_ref, w1_ref, w2_ref, g4_ref, u_ref, y_ref, h2_ref):
        mb = m_ref[...]
        y = jnp.zeros((tm, D_MODEL), F32)
        for j in range(N_DEV):
            cols = slice(j * FF_CHUNK, (j + 1) * FF_CHUNK)
            u = _dot(mb, w1_ref[j])
            u_ref[:, cols] = u.astype(MM)
            y = y + _dot(jnp.square(jnp.maximum(u, 0.0)).astype(MM), w2_ref[cols, :])
        y_ref[...] = y
        n4, _ = _rms(y)
        h2_ref[...] = h1_ref[...] + n4 * g4_ref[...]

    return pl.pallas_call(
        body, name="mlp_fwd", grid=(S // tm,),
        in_specs=[_rows(tm, D_MODEL), _rows(tm, D_MODEL), _resident(), _resident(), _const((1, D_MODEL))],
        out_specs=[_rows(tm, D_FF), _rows(tm, D_MODEL), _rows(tm, D_MODEL)],
        out_shape=[jax.ShapeDtypeStruct((S, D_FF), MM), jax.ShapeDtypeStruct((S, D_MODEL), F32),
                   jax.ShapeDtypeStruct((S, D_MODEL), F32)],
        compiler_params=_params("parallel"),
    )(m, h1, w1, w2, g4)


def _ple_loss(h2, p, target, wg, wple, g5, tm):
    S = h2.shape[0]

    def body(h2_ref, p_ref, t_ref, wg_ref, wp_ref, g5_ref, dh2_ref, dpe_ref, dgl_ref, dg5_ref, loss_ref):
        i = pl.program_id(0)
        h2 = h2_ref[...]
        gate = jax.nn.sigmoid(_dot(h2.astype(MM), wg_ref[...]))
        pe = _dot(p_ref[...].astype(MM), wp_ref[...])
        n5, r5 = _rms(pe * gate)
        g5 = g5_ref[...]
        diff = h2 + n5 * g5 - t_ref[...]
        per_token = jnp.mean(jnp.square(diff), axis=-1, keepdims=True)
        _accumulate(loss_ref, 0.5 * jnp.sum(per_token, axis=0, keepdims=True), i)
        dh3 = diff * (1.0 / D_MODEL)
        de, dg5 = _rms_bwd(dh3, n5, r5, g5)
        _accumulate(dg5_ref, dg5, i)
        dpe_ref[...] = (de * gate).astype(MM)
        dgl = (de * pe * gate * (1.0 - gate)).astype(MM)
        dgl_ref[...] = dgl
        dh2_ref[...] = dh3 + _dot_nt(dgl, wg_ref[...])

    return pl.pallas_call(
        body, name="ple_loss", grid=(S // tm,),
        in_specs=[_rows(tm, D_MODEL), _rows(tm, D_PLE), _rows(tm, D_MODEL), _resident(), _resident(),
                  _const((1, D_MODEL))],
        out_specs=[_rows(tm, D_MODEL), _rows(tm, D_MODEL), _rows(tm, D_MODEL), _const((1, D_MODEL)), _const((1, 1))],
        out_shape=[jax.ShapeDtypeStruct((S, D_MODEL), F32), jax.ShapeDtypeStruct((S, D_MODEL), MM),
                   jax.ShapeDtypeStruct((S, D_MODEL), MM), jax.ShapeDtypeStruct((1, D_MODEL), F32),
                   jax.ShapeDtypeStruct((1, 1), F32)],
        compiler_params=_params("arbitrary"),
    )(h2, p, target, wg, wple, g5)


def _mlp_bwd(dh2, y, h1, u, w1, w2, g4, g3, tm):
    S = dh2.shape[0]

    def body(dh2_ref, y_ref, h1_ref, u_ref, w1_ref, w2_ref, g4_ref, g3_ref,
             dh1_ref, dy_ref, du_ref, dg4_ref, dg3_ref):
        i = pl.program_id(0)
        dh2 = dh2_ref[...]
        n4, r4 = _rms(y_ref[...])
        dy, dg4 = _rms_bwd(dh2, n4, r4, g4_ref[...])
        _accumulate(dg4_ref, dg4, i)
        dyb = dy.astype(MM)
        dy_ref[...] = dyb
        dm = jnp.zeros((tm, D_MODEL), F32)
        for j in range(N_DEV):
            cols = slice(j * FF_CHUNK, (j + 1) * FF_CHUNK)
            dact = _dot_nt(dyb, w2_ref[cols, :])
            du = (dact * (2.0 * jnp.maximum(u_ref[:, cols].astype(F32), 0.0))).astype(MM)
            du_ref[:, cols] = du
            dm = dm + _dot_nt(du, w1_ref[j])
        n3, r3 = _rms(h1_ref[...])
        dx, dg3 = _rms_bwd(dm, n3, r3, g3_ref[...])
        _accumulate(dg3_ref, dg3, i)
        dh1_ref[...] = dh2 + dx

    return pl.pallas_call(
        body, name="mlp_bwd", grid=(S // tm,),
        in_specs=[_rows(tm, D_MODEL), _rows(tm, D_MODEL), _rows(tm, D_MODEL), _rows(tm, D_FF),
                  _resident(), _resident(), _const((1, D_MODEL)), _const((1, D_MODEL))],
        out_specs=[_rows(tm, D_MODEL), _rows(tm, D_MODEL), _rows(tm, D_FF), _const((1, D_MODEL)),
                   _const((1, D_MODEL))],
        out_shape=[jax.ShapeDtypeStruct((S, D_MODEL), F32), jax.ShapeDtypeStruct((S, D_MODEL), MM),
                   jax.ShapeDtypeStruct((S, D_FF), MM), jax.ShapeDtypeStruct((1, D_MODEL), F32),
                   jax.ShapeDtypeStruct((1, D_MODEL), F32)],
        compiler_params=_params("arbitrary"),
    )(dh2, y, h1, u, w1, w2, g4, g3)


def _attn_out_bwd(dh1, mix, fox_o, swa_o, wout_fox, wout_swa, g2, head_rows, head_cols, tm):
    S = dh1.shape[0]

    def body(dh1_ref, mix_ref, fo_ref, so_ref, wf_ref, ws_ref, g2_ref, er_ref, ec_ref,
             dmix_ref, dcat_ref, drow_ref, dcol_ref, dg2_ref):
        i = pl.program_id(0)
        n2, r2 = _rms(mix_ref[...])
        dmix, dg2 = _rms_bwd(dh1_ref[...], n2, r2, g2_ref[...])
        _accumulate(dg2_ref, dg2, i)
        dmb = dmix.astype(MM)
        dmix_ref[...] = dmb
        dfo = _dot_nt(dmb, wf_ref[...]).astype(MM)
        dso = _dot_nt(dmb, ws_ref[...]).astype(MM)
        dcat_ref[:, :D_ATT] = dfo
        dcat_ref[:, D_ATT:] = dso
        hi = lax.Precision.HIGHEST
        prod_f = dfo.astype(F32) * fo_ref[...].astype(F32)
        prod_s = dso.astype(F32) * so_ref[...].astype(F32)
        drow_ref[...] = lax.dot_general(er_ref[...], prod_f, NT, precision=hi, preferred_element_type=F32)
        dcol_ref[...] = jnp.dot(prod_s, ec_ref[...], precision=hi, preferred_element_type=F32)

    return pl.pallas_call(
        body, name="attn_out_bwd", grid=(S // tm,),
        in_specs=[_rows(tm, D_MODEL), _rows(tm, D_MODEL), _rows(tm, D_ATT), _rows(tm, D_ATT), _resident(),
                  _resident(), _const((1, D_MODEL)), _resident(), _resident()],
        out_specs=[_rows(tm, D_MODEL), _rows(tm, D_MODEL), pl.BlockSpec((N_HEADS, tm), lambda i: (0, i)),
                   _rows(tm, N_HEADS), _const((1, D_MODEL))],
        out_shape=[jax.ShapeDtypeStruct((S, D_MODEL), MM), jax.ShapeDtypeStruct((S, D_MODEL), MM),
                   jax.ShapeDtypeStruct((N_HEADS, S), F32), jax.ShapeDtypeStruct((S, N_HEADS), F32),
                   jax.ShapeDtypeStruct((1, D_MODEL), F32)],
        compiler_params=_params("arbitrary"),
    )(dh1, mix, fox_o, swa_o, wout_fox, wout_swa, g2, head_rows, head_cols)


def _fox_bwd(fqkv, dcat, lse_row3, d_row3, c_col, tq, tk, pairs_per_loop=2):
    S = fqkv.shape[0]
    n_blk = S // tk
    n_qblk = S // tq
    n_band = tk // tq

    def body(q_ref, k_ref, v_ref, do_ref, lse_ref, dd_ref, ck_ref, dq_ref, dk_ref, dv_ref, dc_ref, dcq_ref):
        kb = pl.program_id(0)

        @pl.when(kb == 0)
        def _():
            dq_ref[...] = jnp.zeros_like(dq_ref)
            dcq_ref[...] = jnp.zeros_like(dcq_ref)

        key = lax.broadcasted_iota(jnp.int32, (tk, tq), 0)
        qry = lax.broadcasted_iota(jnp.int32, (tk, tq), 1)
        low = _head_select((tk, 128), 0)
        for first in range(0, N_HEADS // 2, pairs_per_loop):
            pairs = range(first, first + pairs_per_loop)
            heads = [(pr, hh) for pr in pairs for hh in range(2)]
            kh, vh, ck = {}, {}, {}
            for pr in pairs:
                k2 = k_ref[:, pr * 128:(pr + 1) * 128]
                v2 = v_ref[:, pr * 128:(pr + 1) * 128]
                zero = jnp.zeros_like(k2)
                kh[pr, 0], kh[pr, 1] = jnp.where(low, k2, zero), jnp.where(low, zero, k2)
                vh[pr, 0], vh[pr, 1] = jnp.where(low, v2, zero), jnp.where(low, zero, v2)
                for hh in range(2):
                    ck[pr, hh] = ck_ref[:, 2 * pr + hh:2 * pr + hh + 1]

            def block(qb, carry, band, pairs=pairs, kh=kh, vh=vh, ck=ck):
                rows = pl.ds(pl.multiple_of(qb * tq, tq), tq)
                out = []
                it = iter(carry)
                for pr in pairs:
                    lanes = slice(pr * 128, (pr + 1) * 128)
                    q2 = q_ref[rows, lanes]
                    do2 = do_ref[rows, lanes]
                    dq = None
                    for hh in range(2):
                        h = 2 * pr + hh
                        dk, dv, dc = next(it)
                        s_t = _dot_nt(kh[pr, hh], q2) - ck[pr, hh]
                        p_t = jnp.exp(s_t - lse_ref[h, pl.ds(qb, 1), :])
                        if band is not None:
                            p_t = jnp.where(qry + band * tq >= key, p_t, 0.0)
                        ds_t = p_t * (_dot_nt(vh[pr, hh], do2) - dd_ref[h, pl.ds(qb, 1), :])
                        dsb = ds_t.astype(MM)
                        dv = dv + _dot(p_t.astype(MM), do2)
                        dk = dk + _dot(dsb, q2)
                        dc = dc - jnp.sum(ds_t, axis=1, keepdims=True)
                        part = _dot_tn(dsb, kh[pr, hh])
                        dq = part if dq is None else dq + part
                        dcq_ref[h, pl.ds(qb, 1), :] += jnp.sum(ds_t, axis=0, keepdims=True)
                        out.append((dk, dv, dc))
                    dq_ref[rows, lanes] += dq
                return tuple(out)

            carry = tuple((jnp.zeros((tk, 128), F32), jnp.zeros((tk, 128), F32), jnp.zeros((tk, 1), F32))
                          for _ in heads)
            for band in range(n_band):
                carry = block(kb * n_band + band, carry, band=band)
            carry = lax.fori_loop((kb + 1) * n_band, n_qblk, functools.partial(block, band=None), carry)
            grads = dict(zip(heads, carry))
            for pr in pairs:
                lanes = slice(pr * 128, (pr + 1) * 128)
                dk_ref[:, lanes] = jnp.where(low, grads[pr, 0][0], grads[pr, 1][0]).astype(MM)
                dv_ref[:, lanes] = jnp.where(low, grads[pr, 0][1], grads[pr, 1][1]).astype(MM)
                for hh in range(2):
                    dc_ref[:, 2 * pr + hh:2 * pr + hh + 1] = grads[pr, hh][2]

        @pl.when(kb == n_blk - 1)
        def _():
            dq_ref[...] = dq_ref[...] * Q_SCALE

    return pl.pallas_call(
        body, name="fox_bwd", grid=(n_blk,),
        in_specs=[pl.BlockSpec((S, D_ATT), lambda i: (0, 0)), pl.BlockSpec((tk, D_ATT), lambda i: (i, 1)),
                  pl.BlockSpec((tk, D_ATT), lambda i: (i, 2)), pl.BlockSpec((S, D_ATT), lambda i: (0, 0)),
                  _resident(), _resident(), _rows(tk, N_HEADS)],
        out_specs=[_const((S, D_ATT)), _rows(tk, D_ATT), _rows(tk, D_ATT), _rows(tk, N_HEADS),
                   _const((N_HEADS, n_qblk, tq))],
        out_shape=[jax.ShapeDtypeStruct((S, D_ATT), F32), jax.ShapeDtypeStruct((S, D_ATT), MM),
                   jax.ShapeDtypeStruct((S, D_ATT), MM), jax.ShapeDtypeStruct((S, N_HEADS), F32),
                   jax.ShapeDtypeStruct((N_HEADS, n_qblk, tq), F32)],
        compiler_params=_params("arbitrary"),
    )(fqkv, fqkv, fqkv, dcat, lse_row3, d_row3, c_col)


def _swa_bwd(sqkv, dcat, biasm, sinks_slot, bucket, lse, d_col):
    S = sqkv.shape[0]
    n_blk = S // WINDOW

    def body(q_ref, kp_ref, kc_ref, vp_ref, vc_ref, do_ref, bias_ref, sink_ref, bk_ref, lse_ref, dd_ref,
             dq_ref, dk_ref, dv_ref, drb_ref, dsink_ref, ds_acc):
        n = pl.program_id(0)

        @pl.when(n == 0)
        def _():
            dk_ref[...] = jnp.zeros_like(dk_ref)
            dv_ref[...] = jnp.zeros_like(dv_ref)
            ds_acc[...] = jnp.zeros_like(ds_acc)
            dsink_ref[...] = jnp.zeros_like(dsink_ref)

        no_prev = jnp.where(n > 0, 0.0, NEG)
        prev = pl.ds(pl.multiple_of(jnp.maximum(n - 1, 0) * WINDOW, WINDOW), WINDOW)
        cur = pl.ds(pl.multiple_of(n * WINDOW, WINDOW), WINDOW)
        lane8 = lax.broadcasted_iota(jnp.int32, (1, N_HEADS), 1)
        dkp = jnp.zeros((WINDOW, D_KV), F32)
        dkc = jnp.zeros((WINDOW, D_KV), F32)
        dvp = jnp.zeros((WINDOW, D_KV), F32)
        dvc = jnp.zeros((WINDOW, D_KV), F32)
        dsink = jnp.zeros((1, N_HEADS), F32)
        low = _head_select((WINDOW, 128), 0)
        zero = jnp.zeros((WINDOW, 128), MM)
        dqs = []
        for g in range(2):
            sel = low if g == 0 else jnp.logical_not(low)
            qg = _stack4(lambda j: jnp.where(sel, q_ref[:, j * 128:(j + 1) * 128], zero))
            dog = _stack4(lambda j: jnp.where(sel, do_ref[:, j * 128:(j + 1) * 128], zero))
            lse_g = _stack4(lambda j: lse_ref[:, 2 * j + g:2 * j + g + 1])
            dd = _stack4(lambda j: dd_ref[:, 2 * j + g:2 * j + g + 1])
            sink = _stack4(lambda j: jnp.full((WINDOW, 1), sink_ref[2 * j + g], F32))
            pp = jnp.exp(_dot_nt(qg, kp_ref[...]) + _stack4(lambda j: bias_ref[2 * j + g, :, :WINDOW]) + no_prev - lse_g)
            pc = jnp.exp(_dot_nt(qg, kc_ref[...]) + _stack4(lambda j: bias_ref[2 * j + g, :, WINDOW:]) - lse_g)
            sink_term = jnp.exp(sink - lse_g) * dd
            dsp = pp * (_dot_nt(dog, vp_ref[...]) - dd)
            dsc = pc * (_dot_nt(dog, vc_ref[...]) - dd)
            for j in range(4):
                rows = slice(j * WINDOW, (j + 1) * WINDOW)
                dsink = dsink + jnp.where(lane8 == 2 * j + g, -jnp.sum(sink_term[rows]), 0.0)
                ds_acc[2 * j + g, :, :WINDOW] += dsp[rows]
                ds_acc[2 * j + g, :, WINDOW:] += dsc[rows]
            dspb, dscb = dsp.astype(MM), dsc.astype(MM)
            dqs.append(_dot(dspb, kp_ref[...]) + _dot(dscb, kc_ref[...]))
            dkp = dkp + _dot_tn(dspb, qg)
            dkc = dkc + _dot_tn(dscb, qg)
            dvp = dvp + _dot_tn(pp.astype(MM), dog)
            dvc = dvc + _dot_tn(pc.astype(MM), dog)
        for j in range(4):
            rows = slice(j * WINDOW, (j + 1) * WINDOW)
            dq_ref[:, j * 128:(j + 1) * 128] = (jnp.where(low, dqs[0][rows], dqs[1][rows]) * Q_SCALE).astype(MM)
        dk_ref[prev, :] += dkp
        dk_ref[cur, :] += dkc
        dv_ref[prev, :] += dvp
        dv_ref[cur, :] += dvc
        dsink_ref[...] += dsink

        @pl.when(n == n_blk - 1)
        def _():
            bk = bk_ref[...]
            rb = lax.broadcasted_iota(jnp.int32, (N_BUCKETS, N_HEADS), 0)
            cb = lax.broadcasted_iota(jnp.int32, (N_BUCKETS, N_HEADS), 1)
            out = jnp.zeros((N_BUCKETS, N_HEADS), F32)
            for s in range(N_HEADS):
                acc = ds_acc[s]
                for b in range(N_BUCKETS):
                    out = out + jnp.where((rb == b) & (cb == s), jnp.sum(jnp.where(bk == b, acc, 0.0)), 0.0)
            drb_ref[...] = out

    do_spec = pl.BlockSpec((WINDOW, D_ATT), lambda n: (n, 1))
    return pl.pallas_call(
        body, name="swa_bwd", grid=(n_blk,),
        in_specs=_swa_specs(S) + [do_spec, _resident(), pl.BlockSpec(memory_space=pltpu.SMEM), _resident(),
                                  _rows(WINDOW, N_HEADS), _rows(WINDOW, N_HEADS)],
        out_specs=[_rows(WINDOW, D_ATT), _const((S, D_KV)), _const((S, D_KV)), _const((N_BUCKETS, N_HEADS)),
                   _const((1, N_HEADS))],
        out_shape=[jax.ShapeDtypeStruct((S, D_ATT), MM), jax.ShapeDtypeStruct((S, D_KV), F32),
                   jax.ShapeDtypeStruct((S, D_KV), F32), jax.ShapeDtypeStruct((N_BUCKETS, N_HEADS), F32),
                   jax.ShapeDtypeStruct((1, N_HEADS), F32)],
        scratch_shapes=[pltpu.VMEM((N_HEADS, WINDOW, 2 * WINDOW), F32)],
        compiler_params=_params("arbitrary"),
    )(sqkv, sqkv, sqkv, sqkv, sqkv, dcat, biasm, sinks_slot, bucket, lse, d_col)


def _pre_attn_bwd(x, dh1, dq_fox, dk_fox, dv_fox, dsq, dsk, dsv, dff_t, wt, g1, tm):
    S = x.shape[0]

    def body(x_ref, dh1_ref, dq_ref, dk_ref, dv_ref, dsq_ref, dsk_ref, dsv_ref, dff_ref, wt_ref, g1_ref,
             dx_ref, dz_ref, dg1_ref):
        i = pl.program_id(0)
        dq = dq_ref[...].astype(MM)
        dsk = dsk_ref[...].astype(MM)
        dsv = dsv_ref[...].astype(MM)
        dz_ref[:, 0:512] = dq
        dz_ref[:, 512:1024] = dk_ref[...]
        dz_ref[:, 1024:1536] = dv_ref[...]
        dz_ref[:, 1536:2048] = dsq_ref[...]
        dz_ref[:, 2048:2176] = dsk
        dz_ref[:, 2176:2304] = dsv
        da = (_dot(dq, wt_ref[0:512]) + _dot(dk_ref[...], wt_ref[512:1024]) + _dot(dv_ref[...], wt_ref[1024:WT_FF])
              + _dot(dsq_ref[...], wt_ref[WT_SQ:WT_SKV]) + _dot(dsk, wt_ref[WT_SKV:WT_SKV + D_KV])
              + _dot(dsv, wt_ref[WT_SKV + D_KV:WT_ROWS]) + _dot_tn(dff_ref[...].astype(MM), wt_ref[WT_FF:WT_SQ]))
        n1, r1 = _rms(x_ref[...])
        dx, dg1 = _rms_bwd(da, n1, r1, g1_ref[...])
        _accumulate(dg1_ref, dg1, i)
        dx_ref[...] = dh1_ref[...] + dx

    return pl.pallas_call(
        body, name="pre_attn_bwd", grid=(S // tm,),
        in_specs=[_rows(tm, D_MODEL), _rows(tm, D_MODEL), _rows(tm, D_ATT), _rows(tm, D_ATT), _rows(tm, D_ATT),
                  _rows(tm, D_ATT), _rows(tm, D_KV), _rows(tm, D_KV), pl.BlockSpec((16, tm), lambda i: (0, i)),
                  _resident(), _const((1, D_MODEL))],
        out_specs=[_rows(tm, D_MODEL), _rows(tm, 2304), _const((1, D_MODEL))],
        out_shape=[jax.ShapeDtypeStruct((S, D_MODEL), F32), jax.ShapeDtypeStruct((S, 2304), MM),
                   jax.ShapeDtypeStruct((1, D_MODEL), F32)],
        compiler_params=_params("arbitrary"),
    )(x, dh1, dq_fox, dk_fox, dv_fox, dsq, dsk, dsv, dff_t, wt, g1)


def _weight_grad(a, b, name, tk, n_chunks=1, relu2=False):
    S, K = a.shape
    N = b.shape[1]
    cn = N // n_chunks

    def body(a_ref, b_ref, out_ref):
        av = a_ref[...]
        if relu2:
            av = jnp.square(jnp.maximum(av.astype(F32), 0.0))
        av = av.astype(MM)
        for j in range(n_chunks):
            val = _dot_tn(av, b_ref[:, j * cn:(j + 1) * cn].astype(MM)).astype(MM)
            if n_chunks > 1:
                out_ref[j] = val
            else:
                out_ref[...] = val

    if n_chunks > 1:
        out_spec = pl.BlockSpec((n_chunks, tk, cn), lambda i: (0, i, 0))
        out_shape = jax.ShapeDtypeStruct((n_chunks, K, cn), MM)
    else:
        out_spec = pl.BlockSpec((tk, N), lambda i: (i, 0))
        out_shape = jax.ShapeDtypeStruct((K, N), MM)
    return pl.pallas_call(
        body, name=name, grid=(K // tk,),
        in_specs=[pl.BlockSpec((S, tk), lambda i: (0, i)), _resident()],
        out_specs=out_spec, out_shape=out_shape, compiler_params=_params("parallel"),
    )(a, b)


def _forget_weight_grad(dff_t, a):
    def body(d_ref, a_ref, out_ref):
        out_ref[...] = _dot(d_ref[...].astype(MM), a_ref[...])

    return pl.pallas_call(
        body, name="forget_weight_grad", out_shape=jax.ShapeDtypeStruct((16, D_MODEL), F32),
        in_specs=[_resident(), _resident()], out_specs=_resident(),
    )(dff_t, a)


def _place():
    return lax.axis_index("x"), lax.axis_index("y"), lax.axis_index("c")


def _all_gather_sequencer(stack):
    ref = jax.new_ref(stack, memory_space=pltpu.MemorySpace.HBM)

    @pl.kernel(mesh=plsc.ScalarSubcoreMesh(axis_name="sequencer", num_cores=1), name="all_gather_sequencer",
               scratch_types=(pltpu.SemaphoreType.DMA((7,)), pltpu.SemaphoreType.DMA((7,))),
               compiler_params=pltpu.CompilerParams(collective_id=1))
    def launch(send_sems, recv_sems):
        x, y, c = _place()
        sibling = (x, y, 1 - c)
        chips = [(1 - x, y), (x, 1 - y), (1 - x, 1 - y)]
        peers = [sibling] + [(px, py, c) for px, py in chips]
        barrier = pltpu.get_barrier_semaphore()
        for peer in peers:
            pl.semaphore_signal(barrier, inc=1, device_id=peer, device_id_type=MESH)
        pl.semaphore_wait(barrier, len(peers))

        def copy(k, block, to):
            px, py, pc = block
            slot = ref.at[4 * px + 2 * py + pc]
            return _remote(slot, slot, send_sems, recv_sems, k, to)

        first = [copy(k, (x, y, c), peer) for k, peer in enumerate(peers)]
        for cp in first:
            cp.start()
        passed = []
        for j, (px, py) in enumerate(chips):
            copy(1 + j, (px, py, c), sibling).wait_recv()
            passed.append(copy(4 + j, (px, py, c), sibling))
            passed[-1].start()
        copy(0, (x, y, 1 - c), sibling).wait_recv()
        for j, (px, py) in enumerate(chips):
            copy(4 + j, (px, py, 1 - c), sibling).wait_recv()
        for cp in first + passed:
            cp.wait_send()

    launch()
    return ref[...]


def _chip_sum(grad, other, name):
    _, _, r, cdim = grad.shape
    tr = 256 if r % 256 == 0 else r

    def body(c_ref, g_ref, o_ref, out_ref):
        out_ref[...] = (g_ref[...].astype(F32) + o_ref[...].astype(F32)).astype(out_ref.dtype)

    return pl.pallas_call(
        body, name=name,
        grid_spec=pltpu.PrefetchScalarGridSpec(
            num_scalar_prefetch=1, grid=(4, r // tr),
            in_specs=[pl.BlockSpec((None, None, tr, cdim), lambda k, i, c_ref: (k, c_ref[0], i, 0)),
                      pl.BlockSpec((None, tr, cdim), lambda k, i, c_ref: (k, i, 0))],
            out_specs=pl.BlockSpec((None, tr, cdim), lambda k, i, c_ref: (k, i, 0))),
        out_shape=jax.ShapeDtypeStruct((4, r, cdim), MM),
        compiler_params=_params("parallel", "parallel"),
    )(lax.axis_index("c").astype(jnp.int32).reshape(1), grad, other)


HBM_SPEC = pl.BlockSpec(memory_space=pltpu.HBM)
SEM_SPEC = pl.BlockSpec(memory_space=pltpu.SEMAPHORE)
DATAFLOW = pltpu.SideEffectType.DATAFLOW_SIDE_EFFECTING


def _exchange_start(name, arrays, n_copies, plan):
    n = len(arrays)

    def body(*refs):
        send_sems, recv_sems, token = refs[n], refs[n + 1], refs[2 * n + 2]
        for cp in plan(refs[:n], send_sems, recv_sems):
            cp.start()
        token[...] = jnp.zeros_like(token)

    out = pl.pallas_call(
        body, name=name,
        out_shape=(pltpu.SemaphoreType.DMA((n_copies,)), pltpu.SemaphoreType.DMA((n_copies,)),
                   *[pltpu.HBM(a.shape, a.dtype) for a in arrays], jax.ShapeDtypeStruct((1, D_MODEL), F32)),
        in_specs=[HBM_SPEC] * n,
        out_specs=(SEM_SPEC, SEM_SPEC, *[HBM_SPEC] * n, pl.BlockSpec(memory_space=pltpu.VMEM)),
        input_output_aliases={i: 2 + i for i in range(n)},
        compiler_params=pltpu.CompilerParams(has_side_effects=DATAFLOW),
    )(*[pltpu.with_memory_space_constraint(a, pltpu.HBM) for a in arrays])
    return (out[0], out[1]), list(out[2:2 + n]), out[2 + n]


def _exchange_wait(name, arrays, sems, after, plan):
    n = len(arrays)
    after = list(after) if isinstance(after, (list, tuple)) else [after]

    def body(*refs):
        send_sems, recv_sems = refs[n], refs[n + 1]
        for cp in plan(refs[:n], send_sems, recv_sems):
            cp.wait_send()
            cp.wait_recv()

    out = pl.pallas_call(
        body, name=name, out_shape=[pltpu.HBM(a.shape, a.dtype) for a in arrays],
        in_specs=[HBM_SPEC] * n + [SEM_SPEC, SEM_SPEC] + [pl.BlockSpec(memory_space=pl.ANY)] * len(after),
        out_specs=[HBM_SPEC] * n, input_output_aliases={i: i for i in range(n)},
        compiler_params=pltpu.CompilerParams(has_side_effects=DATAFLOW),
    )(*arrays, sems[0], sems[1], *after)
    return list(out)


def _remote(src, dst, send_sems, recv_sems, k, to):
    return pltpu.make_async_remote_copy(src_ref=src, dst_ref=dst, send_sem=send_sems.at[k], recv_sem=recv_sems.at[k],
                                        device_id=to, device_id_type=MESH)


def _plan_gather_direct(refs, send_sems, recv_sems):
    x, y, c = _place()
    me = 4 * x + 2 * y + c
    peers = [(x, y, 1 - c), (1 - x, y, c), (x, 1 - y, c), (1 - x, 1 - y, c)]
    return [_remote(ref.at[me], ref.at[me], send_sems, recv_sems, 4 * a + k, peer)
            for a, ref in enumerate(refs) for k, peer in enumerate(peers)]


def _plan_gather_pass_on(refs, send_sems, recv_sems):
    x, y, c = _place()
    chips = [(1 - x, y), (x, 1 - y), (1 - x, 1 - y)]
    return [_remote(ref.at[4 * px + 2 * py + c], ref.at[4 * px + 2 * py + c], send_sems, recv_sems, 3 * a + k,
                    (x, y, 1 - c))
            for a, ref in enumerate(refs) for k, (px, py) in enumerate(chips)]


def _plan_in_chip(refs, send_sems, recv_sems):
    n = len(refs) // 2
    x, y, c = _place()
    return [_remote(refs[a].at[:, 1 - c], refs[n + a], send_sems, recv_sems, a, (x, y, 1 - c)) for a in range(n)]


def _plan_between_chips(refs, send_sems, recv_sems):
    n = len(refs) // 2
    x, y, c = _place()
    chips = [(1 - x, y), (x, 1 - y), (1 - x, 1 - y)]
    return [_remote(refs[a].at[2 * px + py], refs[n + a].at[2 * x + y], send_sems, recv_sems, 3 * a + k, (px, py, c))
            for a in range(n) for k, (px, py) in enumerate(chips)]


def _plan_late_between(refs, send_sems, recv_sems):
    sums, land, small = refs
    x, y, c = _place()
    me = 4 * x + 2 * y + c
    copies = _plan_between_chips([sums, land], send_sems, recv_sems)
    peers = [(x ^ dx, y ^ dy, c ^ dc) for dx in range(2) for dy in range(2) for dc in range(2) if dx + dy + dc]
    return copies + [_remote(small.at[me], small.at[me], send_sems, recv_sems, 3 + k, peer)
                     for k, peer in enumerate(peers)]


def _adamw_math(w, g, m, v):
    m = ADAM_B1 * m + (1.0 - ADAM_B1) * g
    v = ADAM_B2 * v + (1.0 - ADAM_B2) * jnp.square(g)
    m_hat = m / (1.0 - ADAM_B1 ** ADAM_STEP)
    v_hat = v / (1.0 - ADAM_B2 ** ADAM_STEP)
    delta = -ADAM_LR * (m_hat / (jnp.sqrt(v_hat) + ADAM_EPS) + ADAM_WD * w)
    return delta, m, v


def _adamw(parts, w, m, v, name):
    n_parts, r, cdim = parts.shape
    tr = 256 if r % 256 == 0 else r

    def body(p_ref, w_ref, m_ref, v_ref, g_out, d_out, m_out, v_out):
        g = p_ref[0].astype(F32)
        for k in range(1, n_parts):
            g = g + p_ref[k].astype(F32)
        delta, m_new, v_new = _adamw_math(w_ref[...], g, m_ref[...], v_ref[...])
        g_out[...] = g
        d_out[...] = delta
        m_out[...] = m_new
        v_out[...] = v_new

    blk = pl.BlockSpec((tr, cdim), lambda i: (i, 0))
    return pl.pallas_call(
        body, name=name, grid=(r // tr,),
        in_specs=[pl.BlockSpec((n_parts, tr, cdim), lambda i: (0, i, 0)), blk, blk, blk],
        out_specs=[blk] * 4, out_shape=[jax.ShapeDtypeStruct((r, cdim), F32)] * 4,
        compiler_params=_params("parallel"),
    )(parts, w, m, v)


def _adamw_chips(parts, sums, w, m, v, name):
    _, r, cdim = parts.shape
    tr = 256 if r % 256 == 0 else r

    def body(chip_ref, p_ref, own_ref, w_ref, m_ref, v_ref, g_out, d_out, m_out, v_out):
        g = None
        for k in range(4):
            term = jnp.where(chip_ref[0] == k, own_ref[...], p_ref[k]).astype(F32)
            g = term if g is None else g + term
        delta, m_new, v_new = _adamw_math(w_ref[...], g, m_ref[...], v_ref[...])
        g_out[...] = g
        d_out[...] = delta
        m_out[...] = m_new
        v_out[...] = v_new

    blk = pl.BlockSpec((tr, cdim), lambda i, chip: (i, 0))
    my_chip = (2 * lax.axis_index("x") + lax.axis_index("y")).astype(jnp.int32).reshape(1)
    return pl.pallas_call(
        body, name=name,
        grid_spec=pltpu.PrefetchScalarGridSpec(
            num_scalar_prefetch=1, grid=(r // tr,),
            in_specs=[pl.BlockSpec((4, tr, cdim), lambda i, chip: (0, i, 0)),
                      pl.BlockSpec((None, tr, cdim), lambda i, chip: (chip[0], i, 0)), blk, blk, blk],
            out_specs=[blk] * 4),
        out_shape=[jax.ShapeDtypeStruct((r, cdim), F32)] * 4,
        compiler_params=_params("parallel"),
    )(my_chip, parts, sums, w, m, v)


class _NoExchange:
    def __init__(self, weights):
        self.weights = weights

    def before_pre_attn(self, g1):
        return g1

    def after_fox_fwd(self, fox_o, sinks_slot):
        return sinks_slot

    def after_attention(self, swa_o):
        return self.weights

    def after_early_grads(self, grads, d_col):
        return d_col

    def after_swa_bwd(self, dsq, d_row3):
        return d_row3


def _slot_order(t, axis):
    shp = t.shape
    t = t.reshape(shp[:axis] + (N_HEADS, shp[axis] // N_HEADS) + shp[axis + 1:])
    t = jnp.take(t, np.array(SLOT_HEAD), axis=axis)
    return t.reshape(shp)


def _head_order(t, axis):
    shp = t.shape
    t = t.reshape(shp[:axis] + (N_HEADS, shp[axis] // N_HEADS) + shp[axis + 1:])
    t = jnp.take(t, np.array(HEAD_SLOT), axis=axis)
    return t.reshape(shp)


def _forward_backward(x, p, target, win_t, hooks, b_forget, rel_bias, sinks, g1, g2, g3, g4, g5):
    S = x.shape[0]
    tm = 256
    t = 256
    q0 = 3 * D_ATT + N_HEADS
    wt = jnp.concatenate(
        [win_t[:q0], jnp.zeros((8, D_MODEL), MM)]
        + [win_t[q0 + HEAD_DIM * h:q0 + HEAD_DIM * (h + 1)] for h in SLOT_HEAD] + [win_t[q0 + D_ATT:]], axis=0)
    bcol =jnp.pad(b_forget.reshape(N_HEADS, 1), ((0, 8), (0, 0)))
    rel_bias_slot = rel_bias[:, np.array(SLOT_HEAD)]
    sinks_slot = sinks.reshape(N_HEADS)[np.array(SLOT_HEAD)]
    bucket = jnp.asarray(_swa_bucket_map())

    a, fqkv, sqkv, fft = _pre_attn(x, hooks.before_pre_attn(g1), wt, tm)
    c_row = _forget_cumsum(fft, bcol)
    c_col = c_row[:N_HEADS].T
    c_row3 = c_row[:N_HEADS].reshape(N_HEADS, S // t, t)
    fox_o, fox_lse = _fox_fwd(fqkv, c_row3, tq=512, tk=t)
    biasm = _swa_bias(rel_bias_slot, bucket)
    sinks_slot = hooks.after_fox_fwd(fox_o, sinks_slot)
    swa_o, swa_lse = _swa_fwd(sqkv, biasm, sinks_slot)
    wout, w1, w2, wple, wg = hooks.after_attention(swa_o)
    wout_fox = wout[:D_ATT]
    wout_swa = _slot_order(wout[D_ATT:], 0)
    mix, h1, m = _post_attn(x, fox_o, swa_o, wout_fox, wout_swa, g2, g3, tm)
    u, y, h2 = _mlp_fwd(m, h1, w1, w2, g4, tm)
    dh2, dpe, dgl, dg5, loss = _ple_loss(h2, p, target, wg, wple, g5, tm)

    d_wple = _weight_grad(p, dpe, "grad_w_ple", tk=D_PLE, n_chunks=N_DEV)
    d_wg = _weight_grad(h2, dgl, "grad_w_ple_gate", tk=256)
    dh1, dy, du, dg4, dg3 = _mlp_bwd(dh2, y, h1, u, w1, w2, g4, g3, tm)
    d_w2 = _weight_grad(u, dy, "grad_w_ff2", tk=256, relu2=True)
    d_w1 = _weight_grad(m, du, "grad_w_ff1", tk=256, n_chunks=N_DEV)
    head = np.arange(D_ATT) // HEAD_DIM
    head_rows = jnp.asarray((head[None, :] == np.arange(N_HEADS)[:, None]).astype(np.float32))
    dmix, dcat, d_row, d_col, dg2 = _attn_out_bwd(dh1, mix, fox_o, swa_o, wout_fox, wout_swa, g2,
                                                  head_rows, head_rows.T, tm)
    d_wout_fox = _weight_grad(fox_o, dmix, "grad_w_out_fox", tk=256)
    d_wout_swa = _weight_grad(swa_o, dmix, "grad_w_out_swa", tk=256)
    d_wout = jnp.concatenate([d_wout_fox, _head_order(d_wout_swa, 0)], axis=0).reshape(N_DEV, D_MODEL // N_DEV, D_MODEL)
    early = dict(w_ff1=d_w1, w_ff2=d_w2.reshape(N_DEV, FF_CHUNK, D_MODEL), w_ple=d_wple,
                 w_ple_gate=d_wg.reshape(N_DEV, D_MODEL // N_DEV, D_MODEL), w_out=d_wout)

    d_col = hooks.after_early_grads(early, d_col)
    dsq, dsk, dsv, d_rb_slot, d_sink_slot = _swa_bwd(sqkv, dcat, biasm, sinks_slot, bucket, swa_lse, d_col)
    lse_row3 = fox_lse.T.reshape(N_HEADS, S // t, t)
    d_row3 = hooks.after_swa_bwd(dsq, d_row.reshape(N_HEADS, S // t, t))
    dq_fox, dk_fox, dv_fox, dc_col, dcq = _fox_bwd(fqkv, dcat, lse_row3, d_row3, c_col, tq=t, tk=512)
    dc_row = jnp.pad(dc_col.T + dcq.reshape(N_HEADS, S), ((0, 8), (0, 0)))
    dff_t, db = _forget_bwd(dc_row, fft, bcol)
    grad_x, dz, dg1 = _pre_attn_bwd(x, dh1, dq_fox, dk_fox, dv_fox, dsq, dsk, dsv, dff_t, wt, g1, tm)
    d_wmain = _weight_grad(dz, a, "grad_w_in", tk=256)
    d_wff_t = _forget_weight_grad(dff_t, a)

    sq0 = 3 * D_ATT
    d_win = jnp.concatenate(
        [d_wmain[:sq0], d_wff_t[:N_HEADS].astype(MM)]
        + [d_wmain[sq0 + HEAD_DIM * s:sq0 + HEAD_DIM * (s + 1)] for s in HEAD_SLOT] + [d_wmain[sq0 + D_ATT:]], axis=0)
    d_win = d_win.reshape(N_DEV, D_IN // N_DEV, D_MODEL)
    big = dict(early, w_in=d_win)
    small = dict(b_forget=db[:N_HEADS].reshape(1, N_HEADS), rel_bias=d_rb_slot[:, np.array(HEAD_SLOT)],
                 swa_sinks=d_sink_slot[:, np.array(HEAD_SLOT)], g_attn_pre=dg1, g_attn_post=dg2, g_ff_pre=dg3,
                 g_ff_post=dg4, g_ple_post=dg5)
    return loss, grad_x, big, small, dz


BIG = ("w_in", "w_out", "w_ff1", "w_ff2", "w_ple", "w_ple_gate")
SMALL_ROWS = ("g_attn_pre", "g_attn_post", "g_ff_pre", "g_ff_post", "g_ple_post")
WEIGHTS = ("w_in", "b_forget", "w_out", "rel_bias", "swa_sinks", "g_attn_pre", "g_attn_post", "w_ff1", "w_ff2",
           "g_ff_pre", "g_ff_post", "w_ple", "w_ple_gate", "g_ple_post")


EARLY = ("w_ff1", "w_ff2", "w_ple", "w_ple_gate", "w_out")


class _Overlap:
    def __init__(self, later):
        self.later = later

    def before_pre_attn(self, g1):
        self.gather_sems, self.later, token = _exchange_start("gather_rest_start", self.later, 4 * 5, _plan_gather_direct)
        return g1 + token

    def after_fox_fwd(self, fox_o, sinks_slot):
        later = _exchange_wait("gather_rest_wait", self.later, self.gather_sems, fox_o, _plan_gather_direct)
        self.pass_sems, self.later, token = _exchange_start("gather_pass_on_start", later, 3 * 5, _plan_gather_pass_on)
        return sinks_slot + token[0, :N_HEADS]

    def after_attention(self, swa_o):
        wout_g, w1_g, w2_g, wple_g, wg_g = _exchange_wait("gather_pass_on_wait", self.later, self.pass_sems, swa_o,
                                                         _plan_gather_pass_on)
        return (wout_g.reshape(D_MODEL, D_MODEL), w1_g, w2_g.reshape(D_FF, D_MODEL),
                jnp.moveaxis(wple_g, 0, 1).reshape(D_PLE, D_MODEL), wg_g.reshape(D_MODEL, D_MODEL))

    def after_early_grads(self, grads, d_col):
        views = [grads[k].reshape((4, 2) + grads[k].shape[1:]) for k in EARLY]
        lands = [lax.empty((4,) + grads[k].shape[1:], MM) for k in EARLY]
        self.in_chip_sems, self.in_chip, token = _exchange_start("grads_in_chip_start", views + lands, len(EARLY),
                                                                 _plan_in_chip)
        return d_col + token[0, 0]

    def after_swa_bwd(self, dsq, d_row3):
        arrays = _exchange_wait("grads_in_chip_wait", self.in_chip, self.in_chip_sems, dsq, _plan_in_chip)
        n = len(EARLY)
        sums = [_chip_sum(arrays[a], arrays[n + a], "chip_sum_" + k) for a, k in enumerate(EARLY)]
        lands = [lax.empty(s.shape, s.dtype) for s in sums]
        self.between_sems, self.between, token = _exchange_start("grads_between_chips_start", sums + lands, 3 * n,
                                                                 _plan_between_chips)
        return d_row3 + token[0, 0]

    def finish(self, after):
        arrays = _exchange_wait("grads_between_chips_wait", self.between, self.between_sems, after,
                                _plan_between_chips)
        n = len(EARLY)
        self.sums = arrays[:n]
        return arrays[n:]


def _pack_small(t):
    rows = [t[k].reshape(1, D_MODEL) for k in SMALL_ROWS]
    misc = jnp.concatenate([t["b_forget"].reshape(-1), t["swa_sinks"].reshape(-1), t["rel_bias"].reshape(-1)])
    rows.append(jnp.pad(misc, (0, D_MODEL - misc.shape[0])).reshape(1, D_MODEL))
    rows.append(jnp.pad(t["loss"].reshape(-1), (0, D_MODEL - 1)).reshape(1, D_MODEL))
    rows.append(jnp.zeros((1, D_MODEL), F32))
    return jnp.concatenate(rows, axis=0).astype(F32)


def _unpack_small(blk):
    out = {k: blk[i].reshape(1, D_MODEL) for i, k in enumerate(SMALL_ROWS)}
    misc = blk[len(SMALL_ROWS)]
    out["b_forget"] = misc[:N_HEADS].reshape(1, N_HEADS)
    out["swa_sinks"] = misc[N_HEADS:2 * N_HEADS].reshape(1, N_HEADS)
    out["rel_bias"] = misc[2 * N_HEADS:2 * N_HEADS + N_BUCKETS * N_HEADS].reshape(N_BUCKETS, N_HEADS)
    out["loss"] = blk[len(SMALL_ROWS) + 1, 0]
    return out


def kernel(x, p, w_in, b_forget, w_out, rel_bias, swa_sinks, g_attn_pre, g_attn_post, w_ff1, w_ff2, g_ff_pre, g_ff_post, w_ple, w_ple_gate, g_ple_post, loss_target, m_w_in, m_b_forget, m_w_out, m_rel_bias, m_swa_sinks, m_g_attn_pre, m_g_attn_post, m_w_ff1, m_w_ff2, m_g_ff_pre, m_g_ff_post, m_w_ple, m_w_ple_gate, m_g_ple_post, v_w_in, v_b_forget, v_w_out, v_rel_bias, v_swa_sinks, v_g_attn_pre, v_g_attn_post, v_w_ff1, v_w_ff2, v_g_ff_pre, v_g_ff_post, v_w_ple, v_w_ple_gate, v_g_ple_post):
    w = dict(w_in=w_in, b_forget=b_forget, w_out=w_out, rel_bias=rel_bias, swa_sinks=swa_sinks,
             g_attn_pre=g_attn_pre, g_attn_post=g_attn_post, w_ff1=w_ff1, w_ff2=w_ff2, g_ff_pre=g_ff_pre,
             g_ff_post=g_ff_post, w_ple=w_ple, w_ple_gate=w_ple_gate, g_ple_post=g_ple_post)
    mom = dict(w_in=m_w_in, b_forget=m_b_forget, w_out=m_w_out, rel_bias=m_rel_bias, swa_sinks=m_swa_sinks,
               g_attn_pre=m_g_attn_pre, g_attn_post=m_g_attn_post, w_ff1=m_w_ff1, w_ff2=m_w_ff2,
               g_ff_pre=m_g_ff_pre, g_ff_post=m_g_ff_post, w_ple=m_w_ple, w_ple_gate=m_w_ple_gate,
               g_ple_post=m_g_ple_post)
    var = dict(w_in=v_w_in, b_forget=v_b_forget, w_out=v_w_out, rel_bias=v_rel_bias, swa_sinks=v_swa_sinks,
               g_attn_pre=v_g_attn_pre, g_attn_post=v_g_attn_post, w_ff1=v_w_ff1, w_ff2=v_w_ff2,
               g_ff_pre=v_g_ff_pre, g_ff_post=v_g_ff_post, w_ple=v_w_ple, w_ple_gate=v_w_ple_gate,
               g_ple_post=v_g_ple_post)

    turn = lambda t, k: t.T if k == "w_in" else t
    me = 4 * lax.axis_index("x") + 2 * lax.axis_index("y") + lax.axis_index("c")

    def stack(block):
        return lax.dynamic_update_slice_in_dim(lax.empty((N_DEV,) + block.shape, block.dtype), block[None], me, 0)

    stacks = [stack(turn(w[k][0], k).astype(MM)) for k in BIG]
    win_g, later = _all_gather_sequencer(stacks[0]), stacks[1:]
    hooks = _Overlap(later)
    loss, grad_x, big, small, last = _forward_backward(
        x[0], p[0, 0], loss_target[0], win_g.reshape(D_IN, D_MODEL), hooks, b_forget, rel_bias, swa_sinks,
        g_attn_pre, g_attn_post, g_ff_pre, g_ff_post, g_ple_post)
    out_g, out_d, out_m, out_v = {}, {}, {}, {}

    def update(k, part, own):
        g, d, m_new, v_new = _adamw_chips(part, own, turn(w[k][0], k), turn(mom[k][0], k), turn(var[k][0], k),
                                          "adamw_" + k)
        out_g[k], out_d[k], out_m[k], out_v[k] = turn(g, k)[None], turn(d, k)[None], turn(m_new, k)[None], turn(v_new, k)[None]
        return d

    d_win = big["w_in"]
    in_chip_sems, in_chip, token = _exchange_start(
        "late_in_chip_start", [d_win.reshape((4, 2) + d_win.shape[1:]), lax.empty((4,) + d_win.shape[1:], MM)], 1,
        _plan_in_chip)
    early_parts = hooks.finish(token)
    done = [update(k, part, own) for k, part, own in list(zip(EARLY, early_parts, hooks.sums))[:2]]
    view, other = _exchange_wait("late_in_chip_wait", in_chip, in_chip_sems, done, _plan_in_chip)
    chip_sum = _chip_sum(view, other, "chip_sum_w_in")
    small["loss"] = loss
    between_sems, between, token = _exchange_start(
        "late_between_chips_start", [chip_sum, lax.empty(chip_sum.shape, MM), stack(_pack_small(small))], 3 + 7,
        _plan_late_between)
    done = [update(k, part, own) for k, part, own in list(zip(EARLY, early_parts, hooks.sums))[2:]]
    chip_sum, part, small_all = _exchange_wait("late_between_chips_wait", between, between_sems, done,
                                               _plan_late_between)
    update("w_in", part, chip_sum)
    rep ={k: w[k] for k in w if k not in BIG}
    rep["loss"] = jnp.zeros((), F32)
    rep_m = {k: mom[k] for k in mom if k not in BIG}
    rep_m["loss"] = jnp.zeros((), F32)
    rep_v = {k: var[k] for k in var if k not in BIG}
    rep_v["loss"] = jnp.ones((), F32)
    g_s, d_s, m_s, v_s = _adamw(small_all, _pack_small(rep), _pack_small(rep_m), _pack_small(rep_v), "adamw_small")
    g_s, d_s, m_s, v_s = _unpack_small(g_s), _unpack_small(d_s), _unpack_small(m_s), _unpack_small(v_s)
    for k in w:
        if k not in BIG:
            out_g[k], out_d[k], out_m[k], out_v[k] = g_s[k], d_s[k], m_s[k], v_s[k]
    return (g_s["loss"], grad_x[None], *[out_g[k] for k in WEIGHTS], *[out_d[k] for k in WEIGHTS],
            *[out_m[k] for k in WEIGHTS], *[out_v[k] for k in WEIGHTS])
```

```python
import functools

import numpy as np
import jax
import jax.numpy as jnp
from jax import lax
from jax.experimental import pallas as pl
from jax.experimental.pallas import tpu as pltpu
from jax.experimental.pallas import tpu_sc as plsc

F32 = jnp.float32
MM = jnp.bfloat16

D_MODEL = 1024
HEAD_DIM = 64
N_HEADS = 8
D_ATT = N_HEADS * HEAD_DIM
D_KV = 128
D_FF = 4096
D_PLE = 256
D_IN = 3 * D_ATT + N_HEADS + D_ATT + 2 * D_KV
N_DEV = 8
FF_CHUNK = D_FF // N_DEV
WINDOW = 128
N_BUCKETS = 32
MAX_DISTANCE = 128
RMS_EPS = 1e-6
Q_SCALE = HEAD_DIM ** -0.5
NEG = -1e30

ADAM_LR = 0.001
ADAM_B1 = 0.9
ADAM_B2 = 0.999
ADAM_EPS = 1e-08
ADAM_WD = 0.01
ADAM_STEP = 10

SLOT_HEAD = (0, 4, 1, 5, 2, 6, 3, 7)
HEAD_SLOT = (0, 2, 4, 6, 1, 3, 5, 7)

VMEM_LIMIT = 56 * 1024 * 1024
MESH = pl.DeviceIdType.MESH

NT = (((1,), (1,)), ((), ()))
TN = (((0,), (0,)), ((), ()))


def _params(*semantics):
    return pltpu.CompilerParams(dimension_semantics=semantics, vmem_limit_bytes=VMEM_LIMIT)


def _resident():
    return pl.BlockSpec(memory_space=pltpu.VMEM)


def _rows(tm, width):
    return pl.BlockSpec((tm, width), lambda i: (i, 0))


def _const(shape):
    return pl.BlockSpec(shape, lambda i: (0,) * len(shape))


def _dot(a, b):
    return jnp.dot(a, b, preferred_element_type=F32)


def _dot_nt(a, b):
    return lax.dot_general(a, b, NT, preferred_element_type=F32)


def _dot_tn(a, b):
    return lax.dot_general(a, b, TN, preferred_element_type=F32)


def _rms(xf):
    r = lax.rsqrt(jnp.mean(xf * xf, axis=-1, keepdims=True) + RMS_EPS)
    return xf * r, r


def _rms_bwd(dout, n, r, g):
    dg = jnp.sum(dout * n, axis=0, keepdims=True)
    dn = dout * g
    dx = r * (dn - n * jnp.mean(dn * n, axis=-1, keepdims=True))
    return dx, dg


def _accumulate(ref, value, step):
    @pl.when(step == 0)
    def _():
        ref[...] = value

    @pl.when(step != 0)
    def _():
        ref[...] += value


def _t5_bucket(n):
    max_exact = N_BUCKETS // 2
    large = max_exact + (np.log(np.maximum(n, 1) / max_exact) / np.log(MAX_DISTANCE / max_exact)
                         * (N_BUCKETS - max_exact)).astype(np.int32)
    large = np.minimum(large, N_BUCKETS - 1)
    return np.where(n < max_exact, n, large).astype(np.int32)


def _swa_bucket_map():
    i = np.arange(WINDOW)[:, None]
    j = np.arange(2 * WINDOW)[None, :]
    dist = i + WINDOW - j
    ok = (dist >= 0) & (dist < WINDOW)
    return np.where(ok, _t5_bucket(np.clip(dist, 0, None)), -1).astype(np.int32)


WT_FOX = 0
WT_FF = 3 * D_ATT
WT_SQ = WT_FF + 16
WT_SKV = WT_SQ + D_ATT
WT_ROWS = WT_SKV + 2 * D_KV


def _pre_attn(x, g1, wt, tm):
    S = x.shape[0]

    def body(x_ref, g_ref, wt_ref, a_ref, fqkv_ref, sqkv_ref, fft_ref):
        n, _ = _rms(x_ref[...])
        a = (n * g_ref[...]).astype(MM)
        a_ref[...] = a
        fqkv_ref[:, :D_ATT] = (_dot_nt(a, wt_ref[WT_FOX:WT_FOX + D_ATT]) * Q_SCALE).astype(MM)
        fqkv_ref[:, D_ATT:] = _dot_nt(a, wt_ref[WT_FOX + D_ATT:WT_FF]).astype(MM)
        sqkv_ref[:, :D_ATT] = (_dot_nt(a, wt_ref[WT_SQ:WT_SKV]) * Q_SCALE).astype(MM)
        sqkv_ref[:, D_ATT:] = _dot_nt(a, wt_ref[WT_SKV:WT_ROWS]).astype(MM)
        fft_ref[...] = _dot_nt(wt_ref[WT_FF:WT_SQ], a)

    return pl.pallas_call(
        body, name="pre_attn", grid=(S // tm,),
        in_specs=[_rows(tm, D_MODEL), _const((1, D_MODEL)), _resident()],
        out_specs=[_rows(tm, D_MODEL), _rows(tm, 3 * D_ATT), _rows(tm, D_ATT + 2 * D_KV),
                   pl.BlockSpec((16, tm), lambda i: (0, i))],
        out_shape=[jax.ShapeDtypeStruct((S, D_MODEL), MM), jax.ShapeDtypeStruct((S, 3 * D_ATT), MM),
                   jax.ShapeDtypeStruct((S, D_ATT + 2 * D_KV), MM), jax.ShapeDtypeStruct((16, S), F32)],
        compiler_params=_params("parallel"),
    )(x, g1, wt)


def _lane_scan(v, reverse):
    S = v.shape[1]
    lane = lax.broadcasted_iota(jnp.int32, v.shape, 1)
    k = 1
    while k < S:
        if reverse:
            v = v + jnp.where(lane < S - k, pltpu.roll(v, S - k, axis=1), 0.0)
        else:
            v = v + jnp.where(lane >= k, pltpu.roll(v, k, axis=1), 0.0)
        k *= 2
    return v


def _forget_cumsum(fft, bcol):
    def body(f_ref, b_ref, c_ref):
        z = f_ref[...] + b_ref[...]
        log_f = jnp.minimum(z, 0.0) - jnp.log1p(jnp.exp(-jnp.abs(z)))
        c_ref[...] = _lane_scan(log_f, reverse=False)

    return pl.pallas_call(
        body, name="forget_cumsum", out_shape=jax.ShapeDtypeStruct(fft.shape, F32),
        in_specs=[_resident(), _resident()], out_specs=_resident(),
    )(fft, bcol)


def _forget_bwd(dc_row, fft, bcol):
    def body(dc_ref, f_ref, b_ref, dff_ref, db_ref):
        z = f_ref[...] + b_ref[...]
        dlog_f = _lane_scan(dc_ref[...], reverse=True)
        dff = dlog_f * (1.0 / (1.0 + jnp.exp(z)))
        dff_ref[...] = dff
        db_ref[...] = jnp.sum(dff, axis=1, keepdims=True)

    return pl.pallas_call(
        body, name="forget_bwd",
        out_shape=[jax.ShapeDtypeStruct(fft.shape, F32), jax.ShapeDtypeStruct((fft.shape[0], 1), F32)],
        in_specs=[_resident()] * 3, out_specs=[_resident()] * 2,
    )(dc_row, fft, bcol)


def _head_select(shape, upper):
    lane = lax.broadcasted_iota(jnp.int32, shape, 1)
    return lane >= HEAD_DIM if upper else lane < HEAD_DIM


def _fox_fwd(fqkv, c_row3, tq, tk, pairs_per_loop=2, row_chunks=1):
    S = fqkv.shape[0]
    rq = tq // row_chunks
    n_band = tq // tk

    def body(q_ref, k_ref, v_ref, ck_ref, o_ref, lse_ref):
        qi = pl.program_id(0)
        row = lax.broadcasted_iota(jnp.int32, (rq, tk), 0)
        col = lax.broadcasted_iota(jnp.int32, (rq, tk), 1)
        low = _head_select((rq, 128), 0)
        for first in range(0, N_HEADS // 2, pairs_per_loop):
            pairs = range(first, first + pairs_per_loop)
            chains = [(pr, hh, rc) for pr in pairs for hh in range(2) for rc in range(row_chunks)]
            qh = {}
            for pr in pairs:
                for rc in range(row_chunks):
                    q2 = q_ref[rc * rq:(rc + 1) * rq, pr * 128:(pr + 1) * 128]
                    qh[pr, 0, rc] = jnp.where(low, q2, jnp.zeros_like(q2))
                    qh[pr, 1, rc] = jnp.where(low, jnp.zeros_like(q2), q2)

            def block(kb, carry, band, chains=chains, qh=qh):
                rows = pl.ds(pl.multiple_of(kb * tk, tk), tk)
                out = []
                for (pr, hh, rc), (m, l, acc) in zip(chains, carry):
                    if band is not None and (rc + 1) * rq <= band * tk:
                        out.append((m, l, acc))
                        continue
                    lanes = slice(pr * 128, (pr + 1) * 128)
                    s = _dot_nt(qh[pr, hh, rc], k_ref[rows, lanes]) - ck_ref[2 * pr + hh, pl.ds(kb, 1), :]
                    if band is not None:
                        s = jnp.where(row + rc * rq >= col + band * tk, s, NEG)
                    m_new = jnp.maximum(m, jnp.max(s, axis=-1, keepdims=True))
                    p = jnp.exp(s - m_new)
                    alpha = jnp.exp(m - m_new)
                    l = alpha * l + jnp.sum(p, axis=-1, keepdims=True)
                    acc = alpha * acc + _dot(p.astype(MM), v_ref[rows, lanes])
                    out.append((m_new, l, acc))
                return tuple(out)

            carry = tuple((jnp.full((rq, 1), NEG, F32), jnp.zeros((rq, 1), F32), jnp.zeros((rq, 128), F32))
                          for _ in chains)
            carry = lax.fori_loop(0, qi * n_band, functools.partial(block, band=None), carry)
            for band in range(n_band):
                carry = block(qi * n_band + band, carry, band=band)
            res = {}
            for (pr, hh, rc), (m, l, acc) in zip(chains, carry):
                res[pr, hh, rc] = acc / l
                lse_ref[rc * rq:(rc + 1) * rq, 2 * pr + hh:2 * pr + hh + 1] = m + jnp.log(l)
            for pr in pairs:
                for rc in range(row_chunks):
                    o_ref[rc * rq:(rc + 1) * rq, pr * 128:(pr + 1) * 128] = jnp.where(
                        low, res[pr, 0, rc], res[pr, 1, rc]).astype(MM)

    return pl.pallas_call(
        body, name="fox_fwd", grid=(S // tq,),
        in_specs=[pl.BlockSpec((tq, D_ATT), lambda i: (i, 0)), pl.BlockSpec((S, D_ATT), lambda i: (0, 1)),
                  pl.BlockSpec((S, D_ATT), lambda i: (0, 2)), _resident()],
        out_specs=[_rows(tq, D_ATT), _rows(tq, N_HEADS)],
        out_shape=[jax.ShapeDtypeStruct((S, D_ATT), MM), jax.ShapeDtypeStruct((S, N_HEADS), F32)],
        compiler_params=_params("parallel"),
    )(fqkv, fqkv, fqkv, c_row3)


def _swa_bias(rel_bias_slot, bucket):
    def body(rb_ref, bk_ref, out_ref):
        bk = bk_ref[...]
        for s in range(N_HEADS):
            acc = jnp.where(bk < 0, NEG, 0.0).astype(F32)
            for b in range(N_BUCKETS):
                acc = jnp.where(bk == b, rb_ref[b, s], acc)
            out_ref[s] = acc

    return pl.pallas_call(
        body, name="swa_bias", out_shape=jax.ShapeDtypeStruct((N_HEADS, WINDOW, 2 * WINDOW), F32),
        in_specs=[pl.BlockSpec(memory_space=pltpu.SMEM), _resident()], out_specs=_resident(),
    )(rel_bias_slot, bucket)


def _stack4(piece):
    return jnp.concatenate([piece(j) for j in range(4)], axis=0)


def _swa_specs(S):
    q = pl.BlockSpec((WINDOW, D_ATT), lambda n: (n, 0))
    kp = pl.BlockSpec((WINDOW, D_KV), lambda n: (jnp.maximum(n - 1, 0), 4))
    kc = pl.BlockSpec((WINDOW, D_KV), lambda n: (n, 4))
    vp = pl.BlockSpec((WINDOW, D_KV), lambda n: (jnp.maximum(n - 1, 0), 5))
    vc = pl.BlockSpec((WINDOW, D_KV), lambda n: (n, 5))
    return [q, kp, kc, vp, vc]


def _swa_fwd(sqkv, biasm, sinks_slot):
    S = sqkv.shape[0]

    def body(q_ref, kp_ref, kc_ref, vp_ref, vc_ref, bias_ref, sink_ref, o_ref, lse_ref):
        n = pl.program_id(0)
        no_prev = jnp.where(n > 0, 0.0, NEG)
        low = _head_select((WINDOW, 128), 0)
        res = []
        for g in range(2):
            sel = low if g == 0 else jnp.logical_not(low)
            qg = _stack4(lambda j: jnp.where(sel, q_ref[:, j * 128:(j + 1) * 128], jnp.zeros((WINDOW, 128), MM)))
            sink = _stack4(lambda j: jnp.full((WINDOW, 1), sink_ref[2 * j + g], F32))
            sp = _dot_nt(qg, kp_ref[...]) + _stack4(lambda j: bias_ref[2 * j + g, :, :WINDOW]) + no_prev
            sc = _dot_nt(qg, kc_ref[...]) + _stack4(lambda j: bias_ref[2 * j + g, :, WINDOW:])
            m = jnp.maximum(jnp.maximum(jnp.max(sp, axis=-1, keepdims=True),
                                        jnp.max(sc, axis=-1, keepdims=True)), sink)
            ep = jnp.exp(sp - m)
            ec = jnp.exp(sc - m)
            den = jnp.sum(ep, axis=-1, keepdims=True) + jnp.sum(ec, axis=-1, keepdims=True) + jnp.exp(sink - m)
            res.append((_dot(ep.astype(MM), vp_ref[...]) + _dot(ec.astype(MM), vc_ref[...])) / den)
            lse = m + jnp.log(den)
            for j in range(4):
                lse_ref[:, 2 * j + g:2 * j + g + 1] = lse[j * WINDOW:(j + 1) * WINDOW]
        for j in range(4):
            rows = slice(j * WINDOW, (j + 1) * WINDOW)
            o_ref[:, j * 128:(j + 1) * 128] = jnp.where(low, res[0][rows], res[1][rows]).astype(MM)

    return pl.pallas_call(
        body, name="swa_fwd", grid=(S // WINDOW,),
        in_specs=_swa_specs(S) + [_resident(), pl.BlockSpec(memory_space=pltpu.SMEM)],
        out_specs=[_rows(WINDOW, D_ATT), _rows(WINDOW, N_HEADS)],
        out_shape=[jax.ShapeDtypeStruct((S, D_ATT), MM), jax.ShapeDtypeStruct((S, N_HEADS), F32)],
        compiler_params=_params("parallel"),
    )(sqkv, sqkv, sqkv, sqkv, sqkv, biasm, sinks_slot)


def _post_attn(x, fox_o, swa_o, wout_fox, wout_swa, g2, g3, tm):
    S = x.shape[0]

    def body(x_ref, fo_ref, so_ref, wf_ref, ws_ref, g2_ref, g3_ref, mix_ref, h1_ref, m_ref):
        mix = _dot(fo_ref[...], wf_ref[...]) + _dot(so_ref[...], ws_ref[...])
        mix_ref[...] = mix
        n2, _ = _rms(mix)
        h1 = x_ref[...] + n2 * g2_ref[...]
        h1_ref[...] = h1
        n3, _ = _rms(h1)
        m_ref[...] = (n3 * g3_ref[...]).astype(MM)

    return pl.pallas_call(
        body, name="post_attn", grid=(S // tm,),
        in_specs=[_rows(tm, D_MODEL), _rows(tm, D_ATT), _rows(tm, D_ATT), _resident(), _resident(),
                  _const((1, D_MODEL)), _const((1, D_MODEL))],
        out_specs=[_rows(tm, D_MODEL)] * 3,
        out_shape=[jax.ShapeDtypeStruct((S, D_MODEL), F32), jax.ShapeDtypeStruct((S, D_MODEL), F32),
                   jax.ShapeDtypeStruct((S, D_MODEL), MM)],
        compiler_params=_params("parallel"),
    )(x, fox_o, swa_o, wout_fox, wout_swa, g2, g3)


def _mlp_fwd(m, h1, w1, w2, g4, tm):
    S = m.shape[0]

    def body(m_ref, h1_ref, w1_ref, w2_ref, g4_ref, u_ref, y_ref, h2_ref):
        mb = m_ref[...]
        y = jnp.zeros((tm, D_MODEL), F32)
        for j in range(N_DEV):
            cols = slice(j * FF_CHUNK, (j + 1) * FF_CHUNK)
            u = _dot(mb, w1_ref[j])
            u_ref[:, cols] = u.astype(MM)
            y = y + _dot(jnp.square(jnp.maximum(u, 0.0)).astype(MM), w2_ref[cols, :])
        y_ref[...] = y
        n4, _ = _rms(y)
        h2_ref[...] = h1_ref[...] + n4 * g4_ref[...]

    return pl.pallas_call(
        body, name="mlp_fwd", grid=(S // tm,),
        in_specs=[_rows(tm, D_MODEL), _rows(tm, D_MODEL), _resident(), _resident(), _const((1, D_MODEL))],
        out_specs=[_rows(tm, D_FF), _rows(tm, D_MODEL), _rows(tm, D_MODEL)],
        out_shape=[jax.ShapeDtypeStruct((S, D_FF), MM), jax.ShapeDtypeStruct((S, D_MODEL), F32),
                   jax.ShapeDtypeStruct((S, D_MODEL), F32)],
        compiler_params=_params("parallel"),
    )(m, h1, w1, w2, g4)


def _ple_loss(h2, p, target, wg, wple, g5, tm):
    S = h2.shape[0]

    def body(h2_ref, p_ref, t_ref, wg_ref, wp_ref, g5_ref, dh2_ref, dpe_ref, dgl_ref, dg5_ref, loss_ref):
        i = pl.program_id(0)
        h2 = h2_ref[...]
        gate = jax.nn.sigmoid(_dot(h2.astype(MM), wg_ref[...]))
        pe = _dot(p_ref[...].astype(MM), wp_ref[...])
        n5, r5 = _rms(pe * gate)
        g5 = g5_ref[...]
        diff = h2 + n5 * g5 - t_ref[...]
        per_token = jnp.mean(jnp.square(diff), axis=-1, keepdims=True)
        _accumulate(loss_ref, 0.5 * jnp.sum(per_token, axis=0, keepdims=True), i)
        dh3 = diff * (1.0 / D_MODEL)
        de, dg5 = _rms_bwd(dh3, n5, r5, g5)
        _accumulate(dg5_ref, dg5, i)
        dpe_ref[...] = (de * gate).astype(MM)
        dgl = (de * pe * gate * (1.0 - gate)).astype(MM)
        dgl_ref[...] = dgl
        dh2_ref[...] = dh3 + _dot_nt(dgl, wg_ref[...])

    return pl.pallas_call(
        body, name="ple_loss", grid=(S // tm,),
        in_specs=[_rows(tm, D_MODEL), _rows(tm, D_PLE), _rows(tm, D_MODEL), _resident(), _resident(),
                  _const((1, D_MODEL))],
        out_specs=[_rows(tm, D_MODEL), _rows(tm, D_MODEL), _rows(tm, D_MODEL), _const((1, D_MODEL)), _const((1, 1))],
        out_shape=[jax.ShapeDtypeStruct((S, D_MODEL), F32), jax.ShapeDtypeStruct((S, D_MODEL), MM),
                   jax.ShapeDtypeStruct((S, D_MODEL), MM), jax.ShapeDtypeStruct((1, D_MODEL), F32),
                   jax.ShapeDtypeStruct((1, 1), F32)],
        compiler_params=_params("arbitrary"),
    )(h2, p, target, wg, wple, g5)


def _mlp_bwd(dh2, y, h1, u, w1, w2, g4, g3, tm):
    S = dh2.shape[0]

    def body(dh2_ref, y_ref, h1_ref, u_ref, w1_ref, w2_ref, g4_ref, g3_ref,
             dh1_ref, dy_ref, du_ref, dg4_ref, dg3_ref):
        i = pl.program_id(0)
        dh2 = dh2_ref[...]
        n4, r4 = _rms(y_ref[...])
        dy, dg4 = _rms_bwd(dh2, n4, r4, g4_ref[...])
        _accumulate(dg4_ref, dg4, i)
        dyb = dy.astype(MM)
        dy_ref[...] = dyb
        dm = jnp.zeros((tm, D_MODEL), F32)
        for j in range(N_DEV):
            cols = slice(j * FF_CHUNK, (j + 1) * FF_CHUNK)
            dact = _dot_nt(dyb, w2_ref[cols, :])
            du = (dact * (2.0 * jnp.maximum(u_ref[:, cols].astype(F32), 0.0))).astype(MM)
            du_ref[:, cols] = du
            dm = dm + _dot_nt(du, w1_ref[j])
        n3, r3 = _rms(h1_ref[...])
        dx, dg3 = _rms_bwd(dm, n3, r3, g3_ref[...])
        _accumulate(dg3_ref, dg3, i)
        dh1_ref[...] = dh2 + dx

    return pl.pallas_call(
        body, name="mlp_bwd", grid=(S // tm,),
        in_specs=[_rows(tm, D_MODEL), _rows(tm, D_MODEL), _rows(tm, D_MODEL), _rows(tm, D_FF),
                  _resident(), _resident(), _const((1, D_MODEL)), _const((1, D_MODEL))],
        out_specs=[_rows(tm, D_MODEL), _rows(tm, D_MODEL), _rows(tm, D_FF), _const((1, D_MODEL)),
                   _const((1, D_MODEL))],
        out_shape=[jax.ShapeDtypeStruct((S, D_MODEL), F32), jax.ShapeDtypeStruct((S, D_MODEL), MM),
                   jax.ShapeDtypeStruct((S, D_FF), MM), jax.ShapeDtypeStruct((1, D_MODEL), F32),
                   jax.ShapeDtypeStruct((1, D_MODEL), F32)],
        compiler_params=_params("arbitrary"),
    )(dh2, y, h1, u, w1, w2, g4, g3)


def _attn_out_bwd(dh1, mix, fox_o, swa_o, wout_fox, wout_swa, g2, head_rows, head_cols, tm):
    S = dh1.shape[0]

    def body(dh1_ref, mix_ref, fo_ref, so_ref, wf_ref, ws_ref, g2_ref, er_ref, ec_ref,
             dmix_ref, dcat_ref, drow_ref, dcol_ref, dg2_ref):
        i = pl.program_id(0)
        n2, r2 = _rms(mix_ref[...])
        dmix, dg2 = _rms_bwd(dh1_ref[...], n2, r2, g2_ref[...])
        _accumulate(dg2_ref, dg2, i)
        dmb = dmix.astype(MM)
        dmix_ref[...] = dmb
        dfo = _dot_nt(dmb, wf_ref[...]).astype(MM)
        dso = _dot_nt(dmb, ws_ref[...]).astype(MM)
        dcat_ref[:, :D_ATT] = dfo
        dcat_ref[:, D_ATT:] = dso
        hi = lax.Precision.HIGHEST
        prod_f = dfo.astype(F32) * fo_ref[...].astype(F32)
        prod_s = dso.astype(F32) * so_ref[...].astype(F32)
        drow_ref[...] = lax.dot_general(er_ref[...], prod_f, NT, precision=hi, preferred_element_type=F32)
        dcol_ref[...] = jnp.dot(prod_s, ec_ref[...], precision=hi, preferred_element_type=F32)

    return pl.pallas_call(
        body, name="attn_out_bwd", grid=(S // tm,),
        in_specs=[_rows(tm, D_MODEL), _rows(tm, D_MODEL), _rows(tm, D_ATT), _rows(tm, D_ATT), _resident(),
                  _resident(), _const((1, D_MODEL)), _resident(), _resident()],
        out_specs=[_rows(tm, D_MODEL), _rows(tm, D_MODEL), pl.BlockSpec((N_HEADS, tm), lambda i: (0, i)),
                   _rows(tm, N_HEADS), _const((1, D_MODEL))],
        out_shape=[jax.ShapeDtypeStruct((S, D_MODEL), MM), jax.ShapeDtypeStruct((S, D_MODEL), MM),
                   jax.ShapeDtypeStruct((N_HEADS, S), F32), jax.ShapeDtypeStruct((S, N_HEADS), F32),
                   jax.ShapeDtypeStruct((1, D_MODEL), F32)],
        compiler_params=_params("arbitrary"),
    )(dh1, mix, fox_o, swa_o, wout_fox, wout_swa, g2, head_rows, head_cols)


def _fox_bwd(fqkv, dcat, lse_row3, d_row3, c_col, tq, tk, pairs_per_loop=2):
    S = fqkv.shape[0]
    n_blk = S // tk
    n_qblk = S // tq
    n_band = tk // tq

    def body(q_ref, k_ref, v_ref, do_ref, lse_ref, dd_ref, ck_ref, dq_ref, dk_ref, dv_ref, dc_ref, dcq_ref):
        kb = pl.program_id(0)

        @pl.when(kb == 0)
        def _():
            dq_ref[...] = jnp.zeros_like(dq_ref)
            dcq_ref[...] = jnp.zeros_like(dcq_ref)

        key = lax.broadcasted_iota(jnp.int32, (tk, tq), 0)
        qry = lax.broadcasted_iota(jnp.int32, (tk, tq), 1)
        low = _head_select((tk, 128), 0)
        for first in range(0, N_HEADS // 2, pairs_per_loop):
            pairs = range(first, first + pairs_per_loop)
            heads = [(pr, hh) for pr in pairs for hh in range(2)]
            kh, vh, ck = {}, {}, {}
            for pr in pairs:
                k2 = k_ref[:, pr * 128:(pr + 1) * 128]
                v2 = v_ref[:, pr * 128:(pr + 1) * 128]
                zero = jnp.zeros_like(k2)
                kh[pr, 0], kh[pr, 1] = jnp.where(low, k2, zero), jnp.where(low, zero, k2)
                vh[pr, 0], vh[pr, 1] = jnp.where(low, v2, zero), jnp.where(low, zero, v2)
                for hh in range(2):
                    ck[pr, hh] = ck_ref[:, 2 * pr + hh:2 * pr + hh + 1]

            def block(qb, carry, band, pairs=pairs, kh=kh, vh=vh, ck=ck):
                rows = pl.ds(pl.multiple_of(qb * tq, tq), tq)
                out = []
                it = iter(carry)
                for pr in pairs:
                    lanes = slice(pr * 128, (pr + 1) * 128)
                    q2 = q_ref[rows, lanes]
                    do2 = do_ref[rows, lanes]
                    dq = None
                    for hh in range(2):
                        h = 2 * pr + hh
                        dk, dv, dc = next(it)
                        s_t = _dot_nt(kh[pr, hh], q2) - ck[pr, hh]
                        p_t = jnp.exp(s_t - lse_ref[h, pl.ds(qb, 1), :])
                        if band is not None:
                            p_t = jnp.where(qry + band * tq >= key, p_t, 0.0)
                        ds_t = p_t * (_dot_nt(vh[pr, hh], do2) - dd_ref[h, pl.ds(qb, 1), :])
                        dsb = ds_t.astype(MM)
                        dv = dv + _dot(p_t.astype(MM), do2)
                        dk = dk + _dot(dsb, q2)
                        dc = dc - jnp.sum(ds_t, axis=1, keepdims=True)
                        part = _dot_tn(dsb, kh[pr, hh])
                        dq = part if dq is None else dq + part
                        dcq_ref[h, pl.ds(qb, 1), :] += jnp.sum(ds_t, axis=0, keepdims=True)
                        out.append((dk, dv, dc))
                    dq_ref[rows, lanes] += dq
                return tuple(out)

            carry = tuple((jnp.zeros((tk, 128), F32), jnp.zeros((tk, 128), F32), jnp.zeros((tk, 1), F32))
                          for _ in heads)
            for band in range(n_band):
                carry = block(kb * n_band + band, carry, band=band)
            carry = lax.fori_loop((kb + 1) * n_band, n_qblk, functools.partial(block, band=None), carry)
            grads = dict(zip(heads, carry))
            for pr in pairs:
                lanes = slice(pr * 128, (pr + 1) * 128)
                dk_ref[:, lanes] = jnp.where(low, grads[pr, 0][0], grads[pr, 1][0]).astype(MM)
                dv_ref[:, lanes] = jnp.where(low, grads[pr, 0][1], grads[pr, 1][1]).astype(MM)
                for hh in range(2):
                    dc_ref[:, 2 * pr + hh:2 * pr + hh + 1] = grads[pr, hh][2]

        @pl.when(kb == n_blk - 1)
        def _():
            dq_ref[...] = dq_ref[...] * Q_SCALE

    return pl.pallas_call(
        body, name="fox_bwd", grid=(n_blk,),
        in_specs=[pl.BlockSpec((S, D_ATT), lambda i: (0, 0)), pl.BlockSpec((tk, D_ATT), lambda i: (i, 1)),
                  pl.BlockSpec((tk, D_ATT), lambda i: (i, 2)), pl.BlockSpec((S, D_ATT), lambda i: (0, 0)),
                  _resident(), _resident(), _rows(tk, N_HEADS)],
        out_specs=[_const((S, D_ATT)), _rows(tk, D_ATT), _rows(tk, D_ATT), _rows(tk, N_HEADS),
                   _const((N_HEADS, n_qblk, tq))],
        out_shape=[jax.ShapeDtypeStruct((S, D_ATT), F32), jax.ShapeDtypeStruct((S, D_ATT), MM),
                   jax.ShapeDtypeStruct((S, D_ATT), MM), jax.ShapeDtypeStruct((S, N_HEADS), F32),
                   jax.ShapeDtypeStruct((N_HEADS, n_qblk, tq), F32)],
        compiler_params=_params("arbitrary"),
    )(fqkv, fqkv, fqkv, dcat, lse_row3, d_row3, c_col)


def _swa_bwd(sqkv, dcat, biasm, sinks_slot, bucket, lse, d_col):
    S = sqkv.shape[0]
    n_blk = S // WINDOW

    def body(q_ref, kp_ref, kc_ref, vp_ref, vc_ref, do_ref, bias_ref, sink_ref, bk_ref, lse_ref, dd_ref,
             dq_ref, dk_ref, dv_ref, drb_ref, dsink_ref, ds_acc):
        n = pl.program_id(0)

        @pl.when(n == 0)
        def _():
            dk_ref[...] = jnp.zeros_like(dk_ref)
            dv_ref[...] = jnp.zeros_like(dv_ref)
            ds_acc[...] = jnp.zeros_like(ds_acc)
            dsink_ref[...] = jnp.zeros_like(dsink_ref)

        no_prev = jnp.where(n > 0, 0.0, NEG)
        prev = pl.ds(pl.multiple_of(jnp.maximum(n - 1, 0) * WINDOW, WINDOW), WINDOW)
        cur = pl.ds(pl.multiple_of(n * WINDOW, WINDOW), WINDOW)
        lane8 = lax.broadcasted_iota(jnp.int32, (1, N_HEADS), 1)
        dkp = jnp.zeros((WINDOW, D_KV), F32)
        dkc = jnp.zeros((WINDOW, D_KV), F32)
        dvp = jnp.zeros((WINDOW, D_KV), F32)
        dvc = jnp.zeros((WINDOW, D_KV), F32)
        dsink = jnp.zeros((1, N_HEADS), F32)
        low = _head_select((WINDOW, 128), 0)
        zero = jnp.zeros((WINDOW, 128), MM)
        dqs = []
        for g in range(2):
            sel = low if g == 0 else jnp.logical_not(low)
            qg = _stack4(lambda j: jnp.where(sel, q_ref[:, j * 128:(j + 1) * 128], zero))
            dog = _stack4(lambda j: jnp.where(sel, do_ref[:, j * 128:(j + 1) * 128], zero))
            lse_g = _stack4(lambda j: lse_ref[:, 2 * j + g:2 * j + g + 1])
            dd = _stack4(lambda j: dd_ref[:, 2 * j + g:2 * j + g + 1])
            sink = _stack4(lambda j: jnp.full((WINDOW, 1), sink_ref[2 * j + g], F32))
            pp = jnp.exp(_dot_nt(qg, kp_ref[...]) + _stack4(lambda j: bias_ref[2 * j + g, :, :WINDOW]) + no_prev - lse_g)
            pc = jnp.exp(_dot_nt(qg, kc_ref[...]) + _stack4(lambda j: bias_ref[2 * j + g, :, WINDOW:]) - lse_g)
            sink_term = jnp.exp(sink - lse_g) * dd
            dsp = pp * (_dot_nt(dog, vp_ref[...]) - dd)
            dsc = pc * (_dot_nt(dog, vc_ref[...]) - dd)
            for j in range(4):
                rows = slice(j * WINDOW, (j + 1) * WINDOW)
                dsink = dsink + jnp.where(lane8 == 2 * j + g, -jnp.sum(sink_term[rows]), 0.0)
                ds_acc[2 * j + g, :, :WINDOW] += dsp[rows]
                ds_acc[2 * j + g, :, WINDOW:] += dsc[rows]
            dspb, dscb = dsp.astype(MM), dsc.astype(MM)
            dqs.append(_dot(dspb, kp_ref[...]) + _dot(dscb, kc_ref[...]))
            dkp = dkp + _dot_tn(dspb, qg)
            dkc = dkc + _dot_tn(dscb, qg)
            dvp = dvp + _dot_tn(pp.astype(MM), dog)
            dvc = dvc + _dot_tn(pc.astype(MM), dog)
        for j in range(4):
            rows = slice(j * WINDOW, (j + 1) * WINDOW)
            dq_ref[:, j * 128:(j + 1) * 128] = (jnp.where(low, dqs[0][rows], dqs[1][rows]) * Q_SCALE).astype(MM)
        dk_ref[prev, :] += dkp
        dk_ref[cur, :] += dkc
        dv_ref[prev, :] += dvp
        dv_ref[cur, :] += dvc
        dsink_ref[...] += dsink

        @pl.when(n == n_blk - 1)
        def _():
            bk = bk_ref[...]
            rb = lax.broadcasted_iota(jnp.int32, (N_BUCKETS, N_HEADS), 0)
            cb = lax.broadcasted_iota(jnp.int32, (N_BUCKETS, N_HEADS), 1)
            out = jnp.zeros((N_BUCKETS, N_HEADS), F32)
            for s in range(N_HEADS):
                acc = ds_acc[s]
                for b in range(N_BUCKETS):
                    out = out + jnp.where((rb == b) & (cb == s), jnp.sum(jnp.where(bk == b, acc, 0.0)), 0.0)
            drb_ref[...] = out

    do_spec = pl.BlockSpec((WINDOW, D_ATT), lambda n: (n, 1))
    return pl.pallas_call(
        body, name="swa_bwd", grid=(n_blk,),
        in_specs=_swa_specs(S) + [do_spec, _resident(), pl.BlockSpec(memory_space=pltpu.SMEM), _resident(),
                                  _rows(WINDOW, N_HEADS), _rows(WINDOW, N_HEADS)],
        out_specs=[_rows(WINDOW, D_ATT), _const((S, D_KV)), _const((S, D_KV)), _const((N_BUCKETS, N_HEADS)),
                   _const((1, N_HEADS))],
        out_shape=[jax.ShapeDtypeStruct((S, D_ATT), MM), jax.ShapeDtypeStruct((S, D_KV), F32),
                   jax.ShapeDtypeStruct((S, D_KV), F32), jax.ShapeDtypeStruct((N_BUCKETS, N_HEADS), F32),
                   jax.ShapeDtypeStruct((1, N_HEADS), F32)],
        scratch_shapes=[pltpu.VMEM((N_HEADS, WINDOW, 2 * WINDOW), F32)],
        compiler_params=_params("arbitrary"),
    )(sqkv, sqkv, sqkv, sqkv, sqkv, dcat, biasm, sinks_slot, bucket, lse, d_col)


D_Z = 3 * D_ATT + D_ATT + 2 * D_KV


def _pack_dz(dq_fox, dk_fox, dv_fox, dsq, dsk, dsv, tm):
    S = dq_fox.shape[0]

    def body(dq_ref, dk_ref, dv_ref, dsq_ref, dsk_ref, dsv_ref, dz_ref):
        dz_ref[:, 0:512] = dq_ref[...].astype(MM)
        dz_ref[:, 512:1024] = dk_ref[...]
        dz_ref[:, 1024:1536] = dv_ref[...]
        dz_ref[:, 1536:2048] = dsq_ref[...]
        dz_ref[:, 2048:2176] = dsk_ref[...].astype(MM)
        dz_ref[:, 2176:2304] = dsv_ref[...].astype(MM)

    return pl.pallas_call(
        body, name="pack_dz", grid=(S // tm,),
        in_specs=[_rows(tm, D_ATT), _rows(tm, D_ATT), _rows(tm, D_ATT), _rows(tm, D_ATT), _rows(tm, D_KV),
                  _rows(tm, D_KV)],
        out_specs=_rows(tm, D_Z), out_shape=jax.ShapeDtypeStruct((S, D_Z), MM),
        compiler_params=_params("parallel"),
    )(dq_fox, dk_fox, dv_fox, dsq, dsk, dsv)


def _pre_attn_bwd(x, dh1, dz, dff_t, wt, g1, tm):
    S = x.shape[0]

    def body(x_ref, dh1_ref, dz_ref, dff_ref, wt_ref, g1_ref, dx_ref, dg1_ref):
        i = pl.program_id(0)
        da = (_dot(dz_ref[:, 0:WT_FF], wt_ref[0:WT_FF]) + _dot(dz_ref[:, WT_FF:D_Z], wt_ref[WT_SQ:WT_ROWS])
              + _dot_tn(dff_ref[...].astype(MM), wt_ref[WT_FF:WT_SQ]))
        n1, r1 = _rms(x_ref[...])
        dx, dg1 = _rms_bwd(da, n1, r1, g1_ref[...])
        _accumulate(dg1_ref, dg1, i)
        dx_ref[...] = dh1_ref[...] + dx

    return pl.pallas_call(
        body, name="pre_attn_bwd", grid=(S // tm,),
        in_specs=[_rows(tm, D_MODEL), _rows(tm, D_MODEL), _rows(tm, D_Z), pl.BlockSpec((16, tm), lambda i: (0, i)),
                  _resident(), _const((1, D_MODEL))],
        out_specs=[_rows(tm, D_MODEL), _const((1, D_MODEL))],
        out_shape=[jax.ShapeDtypeStruct((S, D_MODEL), F32), jax.ShapeDtypeStruct((1, D_MODEL), F32)],
        compiler_params=_params("arbitrary"),
    )(x, dh1, dz, dff_t, wt, g1)


def _weight_grad(a, b, name, tk, n_chunks=1, relu2=False):
    S, K = a.shape
    N = b.shape[1]
    cn = N // n_chunks

    def body(a_ref, b_ref, out_ref):
        av = a_ref[...]
        if relu2:
            av = jnp.square(jnp.maximum(av.astype(F32), 0.0))
        av = av.astype(MM)
        for j in range(n_chunks):
            val = _dot_tn(av, b_ref[:, j * cn:(j + 1) * cn].astype(MM)).astype(MM)
            if n_chunks > 1:
                out_ref[j] = val
            else:
                out_ref[...] = val

    if n_chunks > 1:
        out_spec = pl.BlockSpec((n_chunks, tk, cn), lambda i: (0, i, 0))
        out_shape = jax.ShapeDtypeStruct((n_chunks, K, cn), MM)
    else:
        out_spec = pl.BlockSpec((tk, N), lambda i: (i, 0))
        out_shape = jax.ShapeDtypeStruct((K, N), MM)
    return pl.pallas_call(
        body, name=name, grid=(K // tk,),
        in_specs=[pl.BlockSpec((S, tk), lambda i: (0, i)), _resident()],
        out_specs=out_spec, out_shape=out_shape, compiler_params=_params("parallel"),
    )(a, b)


def _forget_weight_grad(dff_t, a):
    def body(d_ref, a_ref, out_ref):
        out_ref[...] = _dot(d_ref[...].astype(MM), a_ref[...])

    return pl.pallas_call(
        body, name="forget_weight_grad", out_shape=jax.ShapeDtypeStruct((16, D_MODEL), F32),
        in_specs=[_resident(), _resident()], out_specs=_resident(),
    )(dff_t, a)


def _place():
    return lax.axis_index("x"), lax.axis_index("y"), lax.axis_index("c")


def _all_gather_sequencer(stack):
    ref = jax.new_ref(stack, memory_space=pltpu.MemorySpace.HBM)

    @pl.kernel(mesh=plsc.ScalarSubcoreMesh(axis_name="sequencer", num_cores=1), name="all_gather_sequencer",
               scratch_types=(pltpu.SemaphoreType.DMA((7,)), pltpu.SemaphoreType.DMA((7,))),
               compiler_params=pltpu.CompilerParams(collective_id=1))
    def launch(send_sems, recv_sems):
        x, y, c = _place()
        sibling = (x, y, 1 - c)
        chips = [(1 - x, y), (x, 1 - y), (1 - x, 1 - y)]
        peers = [sibling] + [(px, py, c) for px, py in chips]
        barrier = pltpu.get_barrier_semaphore()
        for peer in peers:
            pl.semaphore_signal(barrier, inc=1, device_id=peer, device_id_type=MESH)
        pl.semaphore_wait(barrier, len(peers))

        def copy(k, block, to):
            px, py, pc = block
            slot = ref.at[4 * px + 2 * py + pc]
            return _remote(slot, slot, send_sems, recv_sems, k, to)

        first = [copy(k, (x, y, c), peer) for k, peer in enumerate(peers)]
        for cp in first:
            cp.start()
        passed = []
        for j, (px, py) in enumerate(chips):
            copy(1 + j, (px, py, c), sibling).wait_recv()
            passed.append(copy(4 + j, (px, py, c), sibling))
            passed[-1].start()
        copy(0, (x, y, 1 - c), sibling).wait_recv()
        for j, (px, py) in enumerate(chips):
            copy(4 + j, (px, py, 1 - c), sibling).wait_recv()
        for cp in first + passed:
            cp.wait_send()

    launch()
    return ref[...]


def _chip_sum(grad, other, name):
    _, _, r, cdim = grad.shape
    tr = 512 if r % 512 == 0 else r

    def body(c_ref, g_ref, o_ref, out_ref):
        out_ref[...] = (g_ref[...].astype(F32) + o_ref[...].astype(F32)).astype(out_ref.dtype)

    return pl.pallas_call(
        body, name=name,
        grid_spec=pltpu.PrefetchScalarGridSpec(
            num_scalar_prefetch=1, grid=(4, r // tr),
            in_specs=[pl.BlockSpec((None, None, tr, cdim), lambda k, i, c_ref: (k, c_ref[0], i, 0)),
                      pl.BlockSpec((None, tr, cdim), lambda k, i, c_ref: (k, i, 0))],
            out_specs=pl.BlockSpec((None, tr, cdim), lambda k, i, c_ref: (k, i, 0))),
        out_shape=jax.ShapeDtypeStruct((4, r, cdim), MM),
        compiler_params=_params("parallel", "parallel"),
    )(lax.axis_index("c").astype(jnp.int32).reshape(1), grad, other)


HBM_SPEC = pl.BlockSpec(memory_space=pltpu.HBM)
SEM_SPEC = pl.BlockSpec(memory_space=pltpu.SEMAPHORE)
DATAFLOW = pltpu.SideEffectType.DATAFLOW_SIDE_EFFECTING


def _exchange_start(name, arrays, n_copies, plan):
    n = len(arrays)

    def body(*refs):
        send_sems, recv_sems, token = refs[n], refs[n + 1], refs[2 * n + 2]
        for cp in plan(refs[:n], send_sems, recv_sems):
            cp.start()
        token[...] = jnp.zeros_like(token)

    out = pl.pallas_call(
        body, name=name,
        out_shape=(pltpu.SemaphoreType.DMA((n_copies,)), pltpu.SemaphoreType.DMA((n_copies,)),
                   *[pltpu.HBM(a.shape, a.dtype) for a in arrays], jax.ShapeDtypeStruct((1, D_MODEL), F32)),
        in_specs=[HBM_SPEC] * n,
        out_specs=(SEM_SPEC, SEM_SPEC, *[HBM_SPEC] * n, pl.BlockSpec(memory_space=pltpu.VMEM)),
        input_output_aliases={i: 2 + i for i in range(n)},
        compiler_params=pltpu.CompilerParams(has_side_effects=DATAFLOW),
    )(*[pltpu.with_memory_space_constraint(a, pltpu.HBM) for a in arrays])
    return (out[0], out[1]), list(out[2:2 + n]), out[2 + n]


def _exchange_wait(name, arrays, sems, after, plan):
    n = len(arrays)
    after = list(after) if isinstance(after, (list, tuple)) else [after]

    def body(*refs):
        send_sems, recv_sems = refs[n], refs[n + 1]
        for cp in plan(refs[:n], send_sems, recv_sems):
            cp.wait_send()
            cp.wait_recv()

    out = pl.pallas_call(
        body, name=name, out_shape=[pltpu.HBM(a.shape, a.dtype) for a in arrays],
        in_specs=[HBM_SPEC] * n + [SEM_SPEC, SEM_SPEC] + [pl.BlockSpec(memory_space=pl.ANY)] * len(after),
        out_specs=[HBM_SPEC] * n, input_output_aliases={i: i for i in range(n)},
        compiler_params=pltpu.CompilerParams(has_side_effects=DATAFLOW),
    )(*arrays, sems[0], sems[1], *after)
    return list(out)


def _remote(src, dst, send_sems, recv_sems, k, to):
    return pltpu.make_async_remote_copy(src_ref=src, dst_ref=dst, send_sem=send_sems.at[k], recv_sem=recv_sems.at[k],
                                        device_id=to, device_id_type=MESH)


def _plan_gather_direct(refs, send_sems, recv_sems):
    x, y, c = _place()
    me = 4 * x + 2 * y + c
    peers = [(x, y, 1 - c), (1 - x, y, c), (x, 1 - y, c), (1 - x, 1 - y, c)]
    return [_remote(ref.at[me], ref.at[me], send_sems, recv_sems, 4 * a + k, peer)
            for a, ref in enumerate(refs) for k, peer in enumerate(peers)]


def _plan_gather_pass_on(refs, send_sems, recv_sems):
    x, y, c = _place()
    chips = [(1 - x, y), (x, 1 - y), (1 - x, 1 - y)]
    return [_remote(ref.at[4 * px + 2 * py + c], ref.at[4 * px + 2 * py + c], send_sems, recv_sems, 3 * a + k,
                    (x, y, 1 - c))
            for a, ref in enumerate(refs) for k, (px, py) in enumerate(chips)]


def _plan_in_chip(refs, send_sems, recv_sems):
    n = len(refs) // 2
    x, y, c = _place()
    return [_remote(refs[a].at[:, 1 - c], refs[n + a], send_sems, recv_sems, a, (x, y, 1 - c)) for a in range(n)]


def _plan_between_chips(refs, send_sems, recv_sems):
    n = len(refs) // 2
    x, y, c = _place()
    chips = [(1 - x, y), (x, 1 - y), (1 - x, 1 - y)]
    return [_remote(refs[a].at[2 * px + py], refs[n + a].at[2 * x + y], send_sems, recv_sems, 3 * a + k, (px, py, c))
            for a in range(n) for k, (px, py) in enumerate(chips)]


def _plan_late_between(refs, send_sems, recv_sems):
    sums, land, small = refs
    x, y, c = _place()
    me = 4 * x + 2 * y + c
    copies = _plan_between_chips([sums, land], send_sems, recv_sems)
    peers = [(x ^ dx, y ^ dy, c ^ dc) for dx in range(2) for dy in range(2) for dc in range(2) if dx + dy + dc]
    return copies + [_remote(small.at[me], small.at[me], send_sems, recv_sems, 3 + k, peer)
                     for k, peer in enumerate(peers)]


def _adamw_math(w, g, m, v):
    m = ADAM_B1 * m + (1.0 - ADAM_B1) * g
    v = ADAM_B2 * v + (1.0 - ADAM_B2) * jnp.square(g)
    m_hat = m / (1.0 - ADAM_B1 ** ADAM_STEP)
    v_hat = v / (1.0 - ADAM_B2 ** ADAM_STEP)
    delta = -ADAM_LR * (m_hat / (jnp.sqrt(v_hat) + ADAM_EPS) + ADAM_WD * w)
    return delta, m, v


def _adamw(parts, w, m, v, name):
    n_parts, r, cdim = parts.shape
    tr = 256 if r % 256 == 0 else r

    def body(p_ref, w_ref, m_ref, v_ref, g_out, d_out, m_out, v_out):
        g = p_ref[0].astype(F32)
        for k in range(1, n_parts):
            g = g + p_ref[k].astype(F32)
        delta, m_new, v_new = _adamw_math(w_ref[...], g, m_ref[...], v_ref[...])
        g_out[...] = g
        d_out[...] = delta
        m_out[...] = m_new
        v_out[...] = v_new

    blk = pl.BlockSpec((tr, cdim), lambda i: (i, 0))
    return pl.pallas_call(
        body, name=name, grid=(r // tr,),
        in_specs=[pl.BlockSpec((n_parts, tr, cdim), lambda i: (0, i, 0)), blk, blk, blk],
        out_specs=[blk] * 4, out_shape=[jax.ShapeDtypeStruct((r, cdim), F32)] * 4,
        compiler_params=_params("parallel"),
    )(parts, w, m, v)


def _adamw_chips(parts, sums, w, m, v, name):
    _, r, cdim = parts.shape
    tr = 256 if r % 256 == 0 else r

    def body(chip_ref, p_ref, own_ref, w_ref, m_ref, v_ref, g_out, d_out, m_out, v_out):
        g = None
        for k in range(4):
            term = jnp.where(chip_ref[0] == k, own_ref[...], p_ref[k]).astype(F32)
            g = term if g is None else g + term
        delta, m_new, v_new = _adamw_math(w_ref[...], g, m_ref[...], v_ref[...])
        g_out[...] = g
        d_out[...] = delta
        m_out[...] = m_new
        v_out[...] = v_new

    blk = pl.BlockSpec((tr, cdim), lambda i, chip: (i, 0))
    my_chip = (2 * lax.axis_index("x") + lax.axis_index("y")).astype(jnp.int32).reshape(1)
    return pl.pallas_call(
        body, name=name,
        grid_spec=pltpu.PrefetchScalarGridSpec(
            num_scalar_prefetch=1, grid=(r // tr,),
            in_specs=[pl.BlockSpec((4, tr, cdim), lambda i, chip: (0, i, 0)),
                      pl.BlockSpec((None, tr, cdim), lambda i, chip: (chip[0], i, 0)), blk, blk, blk],
            out_specs=[blk] * 4),
        out_shape=[jax.ShapeDtypeStruct((r, cdim), F32)] * 4,
        compiler_params=_params("parallel"),
    )(my_chip, parts, sums, w, m, v)


class _NoExchange:
    def __init__(self, weights):
        self.weights = weights

    def before_pre_attn(self, g1):
        return g1

    def after_fox_fwd(self, fox_o, sinks_slot):
        return sinks_slot

    def after_attention(self, swa_o):
        return self.weights

    def after_early_grads(self, grads, d_col):
        return d_col

    def after_swa_bwd(self, dsq, d_row3):
        return d_row3

    def after_w_in_grad(self, d_win, g1):
        return g1


def _slot_order(t, axis):
    shp = t.shape
    t = t.reshape(shp[:axis] + (N_HEADS, shp[axis] // N_HEADS) + shp[axis + 1:])
    t = jnp.take(t, np.array(SLOT_HEAD), axis=axis)
    return t.reshape(shp)


def _head_order(t, axis):
    shp = t.shape
    t = t.reshape(shp[:axis] + (N_HEADS, shp[axis] // N_HEADS) + shp[axis + 1:])
    t = jnp.take(t, np.array(HEAD_SLOT), axis=axis)
    return t.reshape(shp)


def _wt_rows():
    q0 = 3 * D_ATT + N_HEADS
    heads = [np.arange(q0 + HEAD_DIM * h, q0 + HEAD_DIM * (h + 1)) for h in SLOT_HEAD]
    return np.concatenate([np.arange(q0), np.full(8, -1)] + heads + [np.arange(q0 + D_ATT, D_IN)])


def _forward_backward(x, p, target, win_t, hooks, b_forget, rel_bias, sinks, g1, g2, g3, g4, g5):
    S = x.shape[0]
    tm = 256
    tm_mlp = 512
    t = 256
    rows = _wt_rows()
    per_dev = D_IN // N_DEV
    wt = jnp.where((rows >= 0)[:, None], win_t[np.maximum(rows, 0) // per_dev, np.maximum(rows, 0) % per_dev], 0)
    bcol = jnp.pad(b_forget.reshape(N_HEADS, 1), ((0, 8), (0, 0)))
    rel_bias_slot = rel_bias[:, np.array(SLOT_HEAD)]
    sinks_slot = sinks.reshape(N_HEADS)[np.array(SLOT_HEAD)]
    bucket = jnp.asarray(_swa_bucket_map())

    a, fqkv, sqkv, fft = _pre_attn(x, hooks.before_pre_attn(g1), wt, tm)
    c_row = _forget_cumsum(fft, bcol)
    c_col = c_row[:N_HEADS].T
    c_row3 = c_row[:N_HEADS].reshape(N_HEADS, S // t, t)
    fox_o, fox_lse = _fox_fwd(fqkv, c_row3, tq=512, tk=t)
    biasm = _swa_bias(rel_bias_slot, bucket)
    sinks_slot = hooks.after_fox_fwd(fox_o, sinks_slot)
    swa_o, swa_lse = _swa_fwd(sqkv, biasm, sinks_slot)
    wout, w1, w2, wple, wg = hooks.after_attention(swa_o)
    wout_fox = wout[:D_ATT]
    wout_swa = _slot_order(wout[D_ATT:], 0)
    mix, h1, m = _post_attn(x, fox_o, swa_o, wout_fox, wout_swa, g2, g3, tm)
    u, y, h2 = _mlp_fwd(m, h1, w1, w2, g4, tm_mlp)
    dh2, dpe, dgl, dg5, loss = _ple_loss(h2, p, target, wg, wple, g5, tm)

    d_wple = _weight_grad(p, dpe, "grad_w_ple", tk=D_PLE, n_chunks=N_DEV)
    d_wg = _weight_grad(h2, dgl, "grad_w_ple_gate", tk=256)
    dh1, dy, du, dg4, dg3 = _mlp_bwd(dh2, y, h1, u, w1, w2, g4, g3, tm)
    d_w2 = _weight_grad(u, dy, "grad_w_ff2", tk=256, relu2=True)
    d_w1 = _weight_grad(m, du, "grad_w_ff1", tk=256, n_chunks=N_DEV)
    head = np.arange(D_ATT) // HEAD_DIM
    head_rows = jnp.asarray((head[None, :] == np.arange(N_HEADS)[:, None]).astype(np.float32))
    dmix, dcat, d_row, d_col, dg2 = _attn_out_bwd(dh1, mix, fox_o, swa_o, wout_fox, wout_swa, g2,
                                                  head_rows, head_rows.T, tm)
    d_wout_fox = _weight_grad(fox_o, dmix, "grad_w_out_fox", tk=256)
    d_wout_swa = _weight_grad(swa_o, dmix, "grad_w_out_swa", tk=256)
    d_wout = jnp.concatenate([d_wout_fox, _head_order(d_wout_swa, 0)], axis=0).reshape(N_DEV, D_MODEL // N_DEV, D_MODEL)
    early = dict(w_ff1=d_w1, w_ff2=d_w2.reshape(N_DEV, FF_CHUNK, D_MODEL), w_ple=d_wple,
                 w_ple_gate=d_wg.reshape(N_DEV, D_MODEL // N_DEV, D_MODEL), w_out=d_wout)

    d_col = hooks.after_early_grads(early, d_col)
    dsq, dsk, dsv, d_rb_slot, d_sink_slot = _swa_bwd(sqkv, dcat, biasm, sinks_slot, bucket, swa_lse, d_col)
    lse_row3 = fox_lse.T.reshape(N_HEADS, S // t, t)
    d_row3 = hooks.after_swa_bwd(dsq, d_row.reshape(N_HEADS, S // t, t))
    dq_fox, dk_fox, dv_fox, dc_col, dcq = _fox_bwd(fqkv, dcat, lse_row3, d_row3, c_col, tq=t, tk=512)
    dc_row = jnp.pad(dc_col.T + dcq.reshape(N_HEADS, S), ((0, 8), (0, 0)))
    dff_t, db = _forget_bwd(dc_row, fft, bcol)
    dz = _pack_dz(dq_fox, dk_fox, dv_fox, dsq, dsk, dsv, 512)
    d_wmain = _weight_grad(dz, a, "grad_w_in", tk=256)
    d_wff_t = _forget_weight_grad(dff_t, a)

    sq0 = 3 * D_ATT
    src = np.concatenate([np.arange(sq0), np.zeros(N_HEADS, np.int64)]
                         + [np.arange(sq0 + HEAD_DIM * s, sq0 + HEAD_DIM * (s + 1)) for s in HEAD_SLOT]
                         + [np.arange(sq0 + D_ATT, D_Z)]).reshape(N_DEV, per_dev)
    row = np.arange(D_IN).reshape(N_DEV, per_dev)
    is_ff = (row >= sq0) & (row < sq0 + N_HEADS)
    d_win = jnp.where(is_ff[:, :, None], d_wff_t.astype(MM)[np.clip(row - sq0, 0, N_HEADS - 1)], d_wmain[src])
    grad_x, dg1 = _pre_attn_bwd(x, dh1, dz, dff_t, wt, hooks.after_w_in_grad(d_win, g1), tm)
    big = dict(early, w_in=d_win)
    small = dict(b_forget=db[:N_HEADS].reshape(1, N_HEADS), rel_bias=d_rb_slot[:, np.array(HEAD_SLOT)],
                 swa_sinks=d_sink_slot[:, np.array(HEAD_SLOT)], g_attn_pre=dg1, g_attn_post=dg2, g_ff_pre=dg3,
                 g_ff_post=dg4, g_ple_post=dg5)
    return loss, grad_x, big, small


BIG = ("w_in", "w_out", "w_ff1", "w_ff2", "w_ple", "w_ple_gate")
SMALL_ROWS = ("g_attn_pre", "g_attn_post", "g_ff_pre", "g_ff_post", "g_ple_post")
WEIGHTS = ("w_in", "b_forget", "w_out", "rel_bias", "swa_sinks", "g_attn_pre", "g_attn_post", "w_ff1", "w_ff2",
           "g_ff_pre", "g_ff_post", "w_ple", "w_ple_gate", "g_ple_post")


EARLY = ("w_ff1", "w_ff2", "w_ple", "w_ple_gate", "w_out")


class _Overlap:
    def __init__(self, later):
        self.later = later

    def before_pre_attn(self, g1):
        self.gather_sems, self.later, token = _exchange_start("gather_rest_start", self.later, 4 * 5, _plan_gather_direct)
        return g1 + token

    def after_fox_fwd(self, fox_o, sinks_slot):
        later = _exchange_wait("gather_rest_wait", self.later, self.gather_sems, fox_o, _plan_gather_direct)
        self.pass_sems, self.later, token = _exchange_start("gather_pass_on_start", later, 3 * 5, _plan_gather_pass_on)
        return sinks_slot + token[0, :N_HEADS]

    def after_attention(self, swa_o):
        wout_g, w1_g, w2_g, wple_g, wg_g = _exchange_wait("gather_pass_on_wait", self.later, self.pass_sems, swa_o,
                                                         _plan_gather_pass_on)
        return (wout_g.reshape(D_MODEL, D_MODEL), w1_g, w2_g.reshape(D_FF, D_MODEL),
                jnp.moveaxis(wple_g, 0, 1).reshape(D_PLE, D_MODEL), wg_g.reshape(D_MODEL, D_MODEL))

    def after_early_grads(self, grads, d_col):
        views = [grads[k].reshape((4, 2) + grads[k].shape[1:]) for k in EARLY]
        lands = [lax.empty((4,) + grads[k].shape[1:], MM) for k in EARLY]
        self.in_chip_sems, self.in_chip, token = _exchange_start("grads_in_chip_start", views + lands, len(EARLY),
                                                                 _plan_in_chip)
        return d_col + token[0, 0]

    def after_swa_bwd(self, dsq, d_row3):
        arrays = _exchange_wait("grads_in_chip_wait", self.in_chip, self.in_chip_sems, dsq, _plan_in_chip)
        n = len(EARLY)
        sums = [_chip_sum(arrays[a], arrays[n + a], "chip_sum_" + k) for a, k in enumerate(EARLY)]
        lands = [lax.empty(s.shape, s.dtype) for s in sums]
        self.between_sems, self.between, token = _exchange_start("grads_between_chips_start", sums + lands, 3 * n,
                                                                 _plan_between_chips)
        return d_row3 + token[0, 0]

    def after_w_in_grad(self, d_win, g1):
        self.late_in_chip_sems, self.late_in_chip, token = _exchange_start(
            "late_in_chip_start", [d_win.reshape((4, 2) + d_win.shape[1:]), lax.empty((4,) + d_win.shape[1:], MM)],
            1, _plan_in_chip)
        return g1 + token

    def finish(self, after):
        arrays = _exchange_wait("grads_between_chips_wait", self.between, self.between_sems, after,
                                _plan_between_chips)
        n = len(EARLY)
        self.sums = arrays[:n]
        return arrays[n:]


def _pack_small(t):
    rows = [t[k].reshape(1, D_MODEL) for k in SMALL_ROWS]
    misc = jnp.concatenate([t["b_forget"].reshape(-1), t["swa_sinks"].reshape(-1), t["rel_bias"].reshape(-1)])
    rows.append(jnp.pad(misc, (0, D_MODEL - misc.shape[0])).reshape(1, D_MODEL))
    rows.append(jnp.pad(t["loss"].reshape(-1), (0, D_MODEL - 1)).reshape(1, D_MODEL))
    rows.append(jnp.zeros((1, D_MODEL), F32))
    return jnp.concatenate(rows, axis=0).astype(F32)


def _unpack_small(blk):
    out = {k: blk[i].reshape(1, D_MODEL) for i, k in enumerate(SMALL_ROWS)}
    misc = blk[len(SMALL_ROWS)]
    out["b_forget"] = misc[:N_HEADS].reshape(1, N_HEADS)
    out["swa_sinks"] = misc[N_HEADS:2 * N_HEADS].reshape(1, N_HEADS)
    out["rel_bias"] = misc[2 * N_HEADS:2 * N_HEADS + N_BUCKETS * N_HEADS].reshape(N_BUCKETS, N_HEADS)
    out["loss"] = blk[len(SMALL_ROWS) + 1, 0]
    return out


def kernel(x, p, w_in, b_forget, w_out, rel_bias, swa_sinks, g_attn_pre, g_attn_post, w_ff1, w_ff2, g_ff_pre, g_ff_post, w_ple, w_ple_gate, g_ple_post, loss_target, m_w_in, m_b_forget, m_w_out, m_rel_bias, m_swa_sinks, m_g_attn_pre, m_g_attn_post, m_w_ff1, m_w_ff2, m_g_ff_pre, m_g_ff_post, m_w_ple, m_w_ple_gate, m_g_ple_post, v_w_in, v_b_forget, v_w_out, v_rel_bias, v_swa_sinks, v_g_attn_pre, v_g_attn_post, v_w_ff1, v_w_ff2, v_g_ff_pre, v_g_ff_post, v_w_ple, v_w_ple_gate, v_g_ple_post):
    w = dict(w_in=w_in, b_forget=b_forget, w_out=w_out, rel_bias=rel_bias, swa_sinks=swa_sinks,
             g_attn_pre=g_attn_pre, g_attn_post=g_attn_post, w_ff1=w_ff1, w_ff2=w_ff2, g_ff_pre=g_ff_pre,
             g_ff_post=g_ff_post, w_ple=w_ple, w_ple_gate=w_ple_gate, g_ple_post=g_ple_post)
    mom = dict(w_in=m_w_in, b_forget=m_b_forget, w_out=m_w_out, rel_bias=m_rel_bias, swa_sinks=m_swa_sinks,
               g_attn_pre=m_g_attn_pre, g_attn_post=m_g_attn_post, w_ff1=m_w_ff1, w_ff2=m_w_ff2,
               g_ff_pre=m_g_ff_pre, g_ff_post=m_g_ff_post, w_ple=m_w_ple, w_ple_gate=m_w_ple_gate,
               g_ple_post=m_g_ple_post)
    var = dict(w_in=v_w_in, b_forget=v_b_forget, w_out=v_w_out, rel_bias=v_rel_bias, swa_sinks=v_swa_sinks,
               g_attn_pre=v_g_attn_pre, g_attn_post=v_g_attn_post, w_ff1=v_w_ff1, w_ff2=v_w_ff2,
               g_ff_pre=v_g_ff_pre, g_ff_post=v_g_ff_post, w_ple=v_w_ple, w_ple_gate=v_w_ple_gate,
               g_ple_post=v_g_ple_post)

    turn = lambda t, k: t.T if k == "w_in" else t
    me = 4 * lax.axis_index("x") + 2 * lax.axis_index("y") + lax.axis_index("c")

    def stack(block):
        return lax.dynamic_update_slice_in_dim(lax.empty((N_DEV,) + block.shape, block.dtype), block[None], me, 0)

    stacks = [stack(turn(w[k][0], k).astype(MM)) for k in BIG]
    win_g, later = _all_gather_sequencer(stacks[0]), stacks[1:]
    hooks = _Overlap(later)
    loss, grad_x, big, small = _forward_backward(
        x[0], p[0, 0], loss_target[0], win_g, hooks, b_forget, rel_bias, swa_sinks,
        g_attn_pre, g_attn_post, g_ff_pre, g_ff_post, g_ple_post)
    out_g, out_d, out_m, out_v = {}, {}, {}, {}

    def update(k, part, own):
        g, d, m_new, v_new = _adamw_chips(part, own, turn(w[k][0], k), turn(mom[k][0], k), turn(var[k][0], k),
                                          "adamw_" + k)
        out_g[k], out_d[k], out_m[k], out_v[k] = turn(g, k)[None], turn(d, k)[None], turn(m_new, k)[None], turn(v_new, k)[None]
        return d

    view, other = _exchange_wait("late_in_chip_wait", hooks.late_in_chip, hooks.late_in_chip_sems, grad_x,
                                 _plan_in_chip)
    chip_sum = _chip_sum(view, other, "chip_sum_w_in")
    small["loss"] = loss
    between_sems, between, token = _exchange_start(
        "late_between_chips_start", [chip_sum, lax.empty(chip_sum.shape, MM), stack(_pack_small(small))], 3 + 7,
        _plan_late_between)
    early_parts = hooks.finish(token)
    done = [update(k, part, own) for k, part, own in zip(EARLY, early_parts, hooks.sums)]
    chip_sum, part, small_all = _exchange_wait("late_between_chips_wait", between, between_sems, done,
                                               _plan_late_between)
    update("w_in", part, chip_sum)
    rep ={k: w[k] for k in w if k not in BIG}
    rep["loss"] = jnp.zeros((), F32)
    rep_m = {k: mom[k] for k in mom if k not in BIG}
    rep_m["loss"] = jnp.zeros((), F32)
    rep_v = {k: var[k] for k in var if k not in BIG}
    rep_v["loss"] = jnp.ones((), F32)
    g_s, d_s, m_s, v_s = _adamw(small_all, _pack_small(rep), _pack_small(rep_m), _pack_small(rep_v), "adamw_small")
    g_s, d_s, m_s, v_s = _unpack_small(g_s), _unpack_small(d_s), _unpack_small(m_s), _unpack_small(v_s)
    for k in w:
        if k not in BIG:
            out_g[k], out_d[k], out_m[k], out_v[k] = g_s[k], d_s[k], m_s[k], v_s[k]
    return (g_s["loss"], grad_x[None], *[out_g[k] for k in WEIGHTS], *[out_d[k] for k in WEIGHTS],
            *[out_m[k] for k in WEIGHTS], *[out_v[k] for k in WEIGHTS])
```

```python
import functools

import numpy as np
import jax
import jax.numpy as jnp
from jax import lax
from jax.experimental import pallas as pl
from jax.experimental.pallas import tpu as pltpu
from jax.experimental.pallas import tpu_sc as plsc

F32 = jnp.float32
MM = jnp.bfloat16

D_MODEL = 1024
HEAD_DIM = 64
N_HEADS = 8
D_ATT = N_HEADS * HEAD_DIM
D_KV = 128
D_FF = 4096
D_PLE = 256
D_IN = 3 * D_ATT + N_HEADS + D_ATT + 2 * D_KV
N_DEV = 8
FF_CHUNK = D_FF // N_DEV
WINDOW = 128
N_BUCKETS = 32
MAX_DISTANCE = 128
RMS_EPS = 1e-6
Q_SCALE = HEAD_DIM ** -0.5
NEG = -1e30

ADAM_LR = 0.001
ADAM_B1 = 0.9
ADAM_B2 = 0.999
ADAM_EPS = 1e-08
ADAM_WD = 0.01
ADAM_STEP = 10

SLOT_HEAD = (0, 4, 1, 5, 2, 6, 3, 7)
HEAD_SLOT = (0, 2, 4, 6, 1, 3, 5, 7)

VMEM_LIMIT = 56 * 1024 * 1024
MESH = pl.DeviceIdType.MESH

NT = (((1,), (1,)), ((), ()))
TN = (((0,), (0,)), ((), ()))


def _params(*semantics):
    return pltpu.CompilerParams(dimension_semantics=semantics, vmem_limit_bytes=VMEM_LIMIT)


def _resident():
    return pl.BlockSpec(memory_space=pltpu.VMEM)


def _rows(tm, width):
    return pl.BlockSpec((tm, width), lambda i: (i, 0))


def _const(shape):
    return pl.BlockSpec(shape, lambda i: (0,) * len(shape))


def _dot(a, b):
    return jnp.dot(a, b, preferred_element_type=F32)


def _dot_nt(a, b):
    return lax.dot_general(a, b, NT, preferred_element_type=F32)


def _dot_tn(a, b):
    return lax.dot_general(a, b, TN, preferred_element_type=F32)


def _rms(xf):
    r = lax.rsqrt(jnp.mean(xf * xf, axis=-1, keepdims=True) + RMS_EPS)
    return xf * r, r


def _rms_bwd(dout, n, r, g):
    dg = jnp.sum(dout * n, axis=0, keepdims=True)
    dn = dout * g
    dx = r * (dn - n * jnp.mean(dn * n, axis=-1, keepdims=True))
    return dx, dg


def _accumulate(ref, value, step):
    @pl.when(step == 0)
    def _():
        ref[...] = value

    @pl.when(step != 0)
    def _():
        ref[...] += value


def _t5_bucket(n):
    max_exact = N_BUCKETS // 2
    large = max_exact + (np.log(np.maximum(n, 1) / max_exact) / np.log(MAX_DISTANCE / max_exact)
                         * (N_BUCKETS - max_exact)).astype(np.int32)
    large = np.minimum(large, N_BUCKETS - 1)
    return np.where(n < max_exact, n, large).astype(np.int32)


def _swa_bucket_map():
    i = np.arange(WINDOW)[:, None]
    j = np.arange(2 * WINDOW)[None, :]
    dist = i + WINDOW - j
    ok = (dist >= 0) & (dist < WINDOW)
    return np.where(ok, _t5_bucket(np.clip(dist, 0, None)), -1).astype(np.int32)


WT_FOX = 0
WT_FF = 3 * D_ATT
WT_SQ = WT_FF + 16
WT_SKV = WT_SQ + D_ATT
WT_ROWS = WT_SKV + 2 * D_KV


def _pre_attn(x, g1, wt, tm):
    S = x.shape[0]

    def body(x_ref, g_ref, wt_ref, a_ref, fqkv_ref, sqkv_ref, fft_ref):
        n, _ = _rms(x_ref[...])
        a = (n * g_ref[...]).astype(MM)
        a_ref[...] = a
        fqkv_ref[:, :D_ATT] = (_dot_nt(a, wt_ref[WT_FOX:WT_FOX + D_ATT]) * Q_SCALE).astype(MM)
        fqkv_ref[:, D_ATT:] = _dot_nt(a, wt_ref[WT_FOX + D_ATT:WT_FF]).astype(MM)
        sqkv_ref[:, :D_ATT] = (_dot_nt(a, wt_ref[WT_SQ:WT_SKV]) * Q_SCALE).astype(MM)
        sqkv_ref[:, D_ATT:] = _dot_nt(a, wt_ref[WT_SKV:WT_ROWS]).astype(MM)
        fft_ref[...] = _dot_nt(wt_ref[WT_FF:WT_SQ], a)

    return pl.pallas_call(
        body, name="pre_attn", grid=(S // tm,),
        in_specs=[_rows(tm, D_MODEL), _const((1, D_MODEL)), _resident()],
        out_specs=[_rows(tm, D_MODEL), _rows(tm, 3 * D_ATT), _rows(tm, D_ATT + 2 * D_KV),
                   pl.BlockSpec((16, tm), lambda i: (0, i))],
        out_shape=[jax.ShapeDtypeStruct((S, D_MODEL), MM), jax.ShapeDtypeStruct((S, 3 * D_ATT), MM),
                   jax.ShapeDtypeStruct((S, D_ATT + 2 * D_KV), MM), jax.ShapeDtypeStruct((16, S), F32)],
        compiler_params=_params("parallel"),
    )(x, g1, wt)


def _lane_scan(v, reverse):
    S = v.shape[1]
    lane = lax.broadcasted_iota(jnp.int32, v.shape, 1)
    k = 1
    while k < S:
        if reverse:
            v = v + jnp.where(lane < S - k, pltpu.roll(v, S - k, axis=1), 0.0)
        else:
            v = v + jnp.where(lane >= k, pltpu.roll(v, k, axis=1), 0.0)
        k *= 2
    return v


def _forget_cumsum(fft, bcol):
    def body(f_ref, b_ref, c_ref):
        z = f_ref[...] + b_ref[...]
        log_f = jnp.minimum(z, 0.0) - jnp.log1p(jnp.exp(-jnp.abs(z)))
        c_ref[...] = _lane_scan(log_f, reverse=False)

    return pl.pallas_call(
        body, name="forget_cumsum", out_shape=jax.ShapeDtypeStruct(fft.shape, F32),
        in_specs=[_resident(), _resident()], out_specs=_resident(),
    )(fft, bcol)


def _forget_bwd(dc_row, fft, bcol):
    def body(dc_ref, f_ref, b_ref, dff_ref, db_ref):
        z = f_ref[...] + b_ref[...]
        dlog_f = _lane_scan(dc_ref[...], reverse=True)
        dff = dlog_f * (1.0 / (1.0 + jnp.exp(z)))
        dff_ref[...] = dff
        db_ref[...] = jnp.sum(dff, axis=1, keepdims=True)

    return pl.pallas_call(
        body, name="forget_bwd",
        out_shape=[jax.ShapeDtypeStruct(fft.shape, F32), jax.ShapeDtypeStruct((fft.shape[0], 1), F32)],
        in_specs=[_resident()] * 3, out_specs=[_resident()] * 2,
    )(dc_row, fft, bcol)


def _head_select(shape, upper):
    lane = lax.broadcasted_iota(jnp.int32, shape, 1)
    return lane >= HEAD_DIM if upper else lane < HEAD_DIM


def _fox_fwd(fqkv, c_row3, tq, tk, pairs_per_loop=2, row_chunks=1):
    S = fqkv.shape[0]
    rq = tq // row_chunks
    n_band = tq // tk

    def body(q_ref, k_ref, v_ref, ck_ref, o_ref, lse_ref):
        qi = pl.program_id(0)
        row = lax.broadcasted_iota(jnp.int32, (rq, tk), 0)
        col = lax.broadcasted_iota(jnp.int32, (rq, tk), 1)
        low = _head_select((rq, 128), 0)
        for first in range(0, N_HEADS // 2, pairs_per_loop):
            pairs = range(first, first + pairs_per_loop)
            chains = [(pr, hh, rc) for pr in pairs for hh in range(2) for rc in range(row_chunks)]
            qh = {}
            for pr in pairs:
                for rc in range(row_chunks):
                    q2 = q_ref[rc * rq:(rc + 1) * rq, pr * 128:(pr + 1) * 128]
                    qh[pr, 0, rc] = jnp.where(low, q2, jnp.zeros_like(q2))
                    qh[pr, 1, rc] = jnp.where(low, jnp.zeros_like(q2), q2)

            def block(kb, carry, band, chains=chains, qh=qh):
                rows = pl.ds(pl.multiple_of(kb * tk, tk), tk)
                out = []
                for (pr, hh, rc), (m, l, acc) in zip(chains, carry):
                    if band is not None and (rc + 1) * rq <= band * tk:
                        out.append((m, l, acc))
                        continue
                    lanes = slice(pr * 128, (pr + 1) * 128)
                    s = _dot_nt(qh[pr, hh, rc], k_ref[rows, lanes]) - ck_ref[2 * pr + hh, pl.ds(kb, 1), :]
                    if band is not None:
                        s = jnp.where(row + rc * rq >= col + band * tk, s, NEG)
                    m_new = jnp.maximum(m, jnp.max(s, axis=-1, keepdims=True))
                    p = jnp.exp(s - m_new)
                    alpha = jnp.exp(m - m_new)
                    l = alpha * l + jnp.sum(p, axis=-1, keepdims=True)
                    acc = alpha * acc + _dot(p.astype(MM), v_ref[rows, lanes])
                    out.append((m_new, l, acc))
                return tuple(out)

            carry = tuple((jnp.full((rq, 1), NEG, F32), jnp.zeros((rq, 1), F32), jnp.zeros((rq, 128), F32))
                          for _ in chains)
            carry = lax.fori_loop(0, qi * n_band, functools.partial(block, band=None), carry)
            for band in range(n_band):
                carry = block(qi * n_band + band, carry, band=band)
            res = {}
            for (pr, hh, rc), (m, l, acc) in zip(chains, carry):
                res[pr, hh, rc] = acc / l
                lse_ref[rc * rq:(rc + 1) * rq, 2 * pr + hh:2 * pr + hh + 1] = m + jnp.log(l)
            for pr in pairs:
                for rc in range(row_chunks):
                    o_ref[rc * rq:(rc + 1) * rq, pr * 128:(pr + 1) * 128] = jnp.where(
                        low, res[pr, 0, rc], res[pr, 1, rc]).astype(MM)

    return pl.pallas_call(
        body, name="fox_fwd", grid=(S // tq,),
        in_specs=[pl.BlockSpec((tq, D_ATT), lambda i: (i, 0)), pl.BlockSpec((S, D_ATT), lambda i: (0, 1)),
                  pl.BlockSpec((S, D_ATT), lambda i: (0, 2)), _resident()],
        out_specs=[_rows(tq, D_ATT), _rows(tq, N_HEADS)],
        out_shape=[jax.ShapeDtypeStruct((S, D_ATT), MM), jax.ShapeDtypeStruct((S, N_HEADS), F32)],
        compiler_params=_params("parallel"),
    )(fqkv, fqkv, fqkv, c_row3)


def _swa_bias(rel_bias_slot, bucket):
    def body(rb_ref, bk_ref, out_ref):
        bk = bk_ref[...]
        for s in range(N_HEADS):
            acc = jnp.where(bk < 0, NEG, 0.0).astype(F32)
            for b in range(N_BUCKETS):
                acc = jnp.where(bk == b, rb_ref[b, s], acc)
            out_ref[s] = acc

    return pl.pallas_call(
        body, name="swa_bias", out_shape=jax.ShapeDtypeStruct((N_HEADS, WINDOW, 2 * WINDOW), F32),
        in_specs=[pl.BlockSpec(memory_space=pltpu.SMEM), _resident()], out_specs=_resident(),
    )(rel_bias_slot, bucket)


def _stack4(piece):
    return jnp.concatenate([piece(j) for j in range(4)], axis=0)


def _swa_specs(S):
    q = pl.BlockSpec((WINDOW, D_ATT), lambda n: (n, 0))
    kp = pl.BlockSpec((WINDOW, D_KV), lambda n: (jnp.maximum(n - 1, 0), 4))
    kc = pl.BlockSpec((WINDOW, D_KV), lambda n: (n, 4))
    vp = pl.BlockSpec((WINDOW, D_KV), lambda n: (jnp.maximum(n - 1, 0), 5))
    vc = pl.BlockSpec((WINDOW, D_KV), lambda n: (n, 5))
    return [q, kp, kc, vp, vc]


def _swa_fwd(sqkv, biasm, sinks_slot):
    S = sqkv.shape[0]

    def body(q_ref, kp_ref, kc_ref, vp_ref, vc_ref, bias_ref, sink_ref, o_ref, lse_ref):
        n = pl.program_id(0)
        no_prev = jnp.where(n > 0, 0.0, NEG)
        low = _head_select((WINDOW, 128), 0)
        res = []
        for g in range(2):
            sel = low if g == 0 else jnp.logical_not(low)
            qg = _stack4(lambda j: jnp.where(sel, q_ref[:, j * 128:(j + 1) * 128], jnp.zeros((WINDOW, 128), MM)))
            sink = _stack4(lambda j: jnp.full((WINDOW, 1), sink_ref[2 * j + g], F32))
            sp = _dot_nt(qg, kp_ref[...]) + _stack4(lambda j: bias_ref[2 * j + g, :, :WINDOW]) + no_prev
            sc = _dot_nt(qg, kc_ref[...]) + _stack4(lambda j: bias_ref[2 * j + g, :, WINDOW:])
            m = jnp.maximum(jnp.maximum(jnp.max(sp, axis=-1, keepdims=True),
                                        jnp.max(sc, axis=-1, keepdims=True)), sink)
            ep = jnp.exp(sp - m)
            ec = jnp.exp(sc - m)
            den = jnp.sum(ep, axis=-1, keepdims=True) + jnp.sum(ec, axis=-1, keepdims=True) + jnp.exp(sink - m)
            res.append((_dot(ep.astype(MM), vp_ref[...]) + _dot(ec.astype(MM), vc_ref[...])) / den)
            lse = m + jnp.log(den)
            for j in range(4):
                lse_ref[:, 2 * j + g:2 * j + g + 1] = lse[j * WINDOW:(j + 1) * WINDOW]
        for j in range(4):
            rows = slice(j * WINDOW, (j + 1) * WINDOW)
            o_ref[:, j * 128:(j + 1) * 128] = jnp.where(low, res[0][rows], res[1][rows]).astype(MM)

    return pl.pallas_call(
        body, name="swa_fwd", grid=(S // WINDOW,),
        in_specs=_swa_specs(S) + [_resident(), pl.BlockSpec(memory_space=pltpu.SMEM)],
        out_specs=[_rows(WINDOW, D_ATT), _rows(WINDOW, N_HEADS)],
        out_shape=[jax.ShapeDtypeStruct((S, D_ATT), MM), jax.ShapeDtypeStruct((S, N_HEADS), F32)],
        compiler_params=_params("parallel"),
    )(sqkv, sqkv, sqkv, sqkv, sqkv, biasm, sinks_slot)


def _post_attn(x, fox_o, swa_o, wout_fox, wout_swa, g2, g3, tm):
    S = x.shape[0]

    def body(x_ref, fo_ref, so_ref, wf_ref, ws_ref, g2_ref, g3_ref, mix_ref, h1_ref, m_ref):
        mix = _dot(fo_ref[...], wf_ref[...]) + _dot(so_ref[...], ws_ref[...])
        mix_ref[...] = mix
        n2, _ = _rms(mix)
        h1 = x_ref[...] + n2 * g2_ref[...]
        h1_ref[...] = h1
        n3, _ = _rms(h1)
        m_ref[...] = (n3 * g3_ref[...]).astype(MM)

    return pl.pallas_call(
        body, name="post_attn", grid=(S // tm,),
        in_specs=[_rows(tm, D_MODEL), _rows(tm, D_ATT), _rows(tm, D_ATT), _resident(), _resident(),
                  _const((1, D_MODEL)), _const((1, D_MODEL))],
        out_specs=[_rows(tm, D_MODEL)] * 3,
        out_shape=[jax.ShapeDtypeStruct((S, D_MODEL), F32), jax.ShapeDtypeStruct((S, D_MODEL), F32),
                   jax.ShapeDtypeStruct((S, D_MODEL), MM)],
        compiler_params=_params("parallel"),
    )(x, fox_o, swa_o, wout_fox, wout_swa, g2, g3)


def _mlp_fwd(m, h1, w1, w2, g4, tm):
    S = m.shape[0]

    def body(m_ref, h1_ref, w1_ref, w2_ref, g4_ref, u_ref, y_ref, h2_ref):
        mb = m_ref[...]
        y = jnp.zeros((tm, D_MODEL), F32)
        for j in range(N_DEV):
            cols = slice(j * FF_CHUNK, (j + 1) * FF_CHUNK)
            u = _dot(mb, w1_ref[j])
            u_ref[:, cols] = u.astype(MM)
            y = y + _dot(jnp.square(jnp.maximum(u, 0.0)).astype(MM), w2_ref[cols, :])
        y_ref[...] = y
        n4, _ = _rms(y)
        h2_ref[...] = h1_ref[...] + n4 * g4_ref[...]

    return pl.pallas_call(
        body, name="mlp_fwd", grid=(S // tm,),
        in_specs=[_rows(tm, D_MODEL), _rows(tm, D_MODEL), _resident(), _resident(), _const((1, D_MODEL))],
        out_specs=[_rows(tm, D_FF), _rows(tm, D_MODEL), _rows(tm, D_MODEL)],
        out_shape=[jax.ShapeDtypeStruct((S, D_FF), MM), jax.ShapeDtypeStruct((S, D_MODEL), F32),
                   jax.ShapeDtypeStruct((S, D_MODEL), F32)],
        compiler_params=_params("parallel"),
    )(m, h1, w1, w2, g4)


def _ple_loss(h2, p, target, wg, wple, g5, tm):
    S = h2.shape[0]

    def body(h2_ref, p_ref, t_ref, wg_ref, wp_ref, g5_ref, dh2_ref, dpe_ref, dgl_ref, dg5_ref, loss_ref):
        i = pl.program_id(0)
        h2 = h2_ref[...]
        gate = jax.nn.sigmoid(_dot(h2.astype(MM), wg_ref[...]))
        pe = _dot(p_ref[...].astype(MM), wp_ref[...])
        n5, r5 = _rms(pe * gate)
        g5 = g5_ref[...]
        diff = h2 + n5 * g5 - t_ref[...]
        per_token = jnp.mean(jnp.square(diff), axis=-1, keepdims=True)
        _accumulate(loss_ref, 0.5 * jnp.sum(per_token, axis=0, keepdims=True), i)
        dh3 = diff * (1.0 / D_MODEL)
        de, dg5 = _rms_bwd(dh3, n5, r5, g5)
        _accumulate(dg5_ref, dg5, i)
        dpe_ref[...] = (de * gate).astype(MM)
        dgl = (de * pe * gate * (1.0 - gate)).astype(MM)
        dgl_ref[...] = dgl
        dh2_ref[...] = dh3 + _dot_nt(dgl, wg_ref[...])

    return pl.pallas_call(
        body, name="ple_loss", grid=(S // tm,),
        in_specs=[_rows(tm, D_MODEL), _rows(tm, D_PLE), _rows(tm, D_MODEL), _resident(), _resident(),
                  _const((1, D_MODEL))],
        out_specs=[_rows(tm, D_MODEL), _rows(tm, D_MODEL), _rows(tm, D_MODEL), _const((1, D_MODEL)), _const((1, 1))],
        out_shape=[jax.ShapeDtypeStruct((S, D_MODEL), F32), jax.ShapeDtypeStruct((S, D_MODEL), MM),
                   jax.ShapeDtypeStruct((S, D_MODEL), MM), jax.ShapeDtypeStruct((1, D_MODEL), F32),
                   jax.ShapeDtypeStruct((1, 1), F32)],
        compiler_params=_params("arbitrary"),
    )(h2, p, target, wg, wple, g5)


def _mlp_bwd(dh2, y, h1, u, w1, w2, g4, g3, tm):
    S = dh2.shape[0]

    def body(dh2_ref, y_ref, h1_ref, u_ref, w1_ref, w2_ref, g4_ref, g3_ref,
             dh1_ref, dy_ref, du_ref, dg4_ref, dg3_ref):
        i = pl.program_id(0)
        dh2 = dh2_ref[...]
        n4, r4 = _rms(y_ref[...])
        dy, dg4 = _rms_bwd(dh2, n4, r4, g4_ref[...])
        _accumulate(dg4_ref, dg4, i)
        dyb = dy.astype(MM)
        dy_ref[...] = dyb
        dm = jnp.zeros((tm, D_MODEL), F32)
        for j in range(N_DEV):
            cols = slice(j * FF_CHUNK, (j + 1) * FF_CHUNK)
            dact = _dot_nt(dyb, w2_ref[cols, :])
            du = (dact * (2.0 * jnp.maximum(u_ref[:, cols].astype(F32), 0.0))).astype(MM)
            du_ref[:, cols] = du
            dm = dm + _dot_nt(du, w1_ref[j])
        n3, r3 = _rms(h1_ref[...])
        dx, dg3 = _rms_bwd(dm, n3, r3, g3_ref[...])
        _accumulate(dg3_ref, dg3, i)
        dh1_ref[...] = dh2 + dx

    return pl.pallas_call(
        body, name="mlp_bwd", grid=(S // tm,),
        in_specs=[_rows(tm, D_MODEL), _rows(tm, D_MODEL), _rows(tm, D_MODEL), _rows(tm, D_FF),
                  _resident(), _resident(), _const((1, D_MODEL)), _const((1, D_MODEL))],
        out_specs=[_rows(tm, D_MODEL), _rows(tm, D_MODEL), _rows(tm, D_FF), _const((1, D_MODEL)),
                   _const((1, D_MODEL))],
        out_shape=[jax.ShapeDtypeStruct((S, D_MODEL), F32), jax.ShapeDtypeStruct((S, D_MODEL), MM),
                   jax.ShapeDtypeStruct((S, D_FF), MM), jax.ShapeDtypeStruct((1, D_MODEL), F32),
                   jax.ShapeDtypeStruct((1, D_MODEL), F32)],
        compiler_params=_params("arbitrary"),
    )(dh2, y, h1, u, w1, w2, g4, g3)


def _attn_out_bwd(dh1, mix, fox_o, swa_o, wout_fox, wout_swa, g2, head_rows, head_cols, tm):
    S = dh1.shape[0]

    def body(dh1_ref, mix_ref, fo_ref, so_ref, wf_ref, ws_ref, g2_ref, er_ref, ec_ref,
             dmix_ref, dcat_ref, drow_ref, dcol_ref, dg2_ref):
        i = pl.program_id(0)
        n2, r2 = _rms(mix_ref[...])
        dmix, dg2 = _rms_bwd(dh1_ref[...], n2, r2, g2_ref[...])
        _accumulate(dg2_ref, dg2, i)
        dmb = dmix.astype(MM)
        dmix_ref[...] = dmb
        dfo = _dot_nt(dmb, wf_ref[...]).astype(MM)
        dso = _dot_nt(dmb, ws_ref[...]).astype(MM)
        dcat_ref[:, :D_ATT] = dfo
        dcat_ref[:, D_ATT:] = dso
        hi = lax.Precision.HIGHEST
        prod_f = dfo.astype(F32) * fo_ref[...].astype(F32)
        prod_s = dso.astype(F32) * so_ref[...].astype(F32)
        drow_ref[...] = lax.dot_general(er_ref[...], prod_f, NT, precision=hi, preferred_element_type=F32)
        dcol_ref[...] = jnp.dot(prod_s, ec_ref[...], precision=hi, preferred_element_type=F32)

    return pl.pallas_call(
        body, name="attn_out_bwd", grid=(S // tm,),
        in_specs=[_rows(tm, D_MODEL), _rows(tm, D_MODEL), _rows(tm, D_ATT), _rows(tm, D_ATT), _resident(),
                  _resident(), _const((1, D_MODEL)), _resident(), _resident()],
        out_specs=[_rows(tm, D_MODEL), _rows(tm, D_MODEL), pl.BlockSpec((N_HEADS, tm), lambda i: (0, i)),
                   _rows(tm, N_HEADS), _const((1, D_MODEL))],
        out_shape=[jax.ShapeDtypeStruct((S, D_MODEL), MM), jax.ShapeDtypeStruct((S, D_MODEL), MM),
                   jax.ShapeDtypeStruct((N_HEADS, S), F32), jax.ShapeDtypeStruct((S, N_HEADS), F32),
                   jax.ShapeDtypeStruct((1, D_MODEL), F32)],
        compiler_params=_params("arbitrary"),
    )(dh1, mix, fox_o, swa_o, wout_fox, wout_swa, g2, head_rows, head_cols)


def _fox_bwd(fqkv, dcat, lse_row3, d_row3, c_col, tq, tk, pairs_per_loop=2):
    S = fqkv.shape[0]
    n_blk = S // tk
    n_qblk = S // tq
    n_band = tk // tq

    def body(q_ref, k_ref, v_ref, do_ref, lse_ref, dd_ref, ck_ref, dq_ref, dk_ref, dv_ref, dc_ref, dcq_ref):
        kb = pl.program_id(0)

        @pl.when(kb == 0)
        def _():
            dq_ref[...] = jnp.zeros_like(dq_ref)
            dcq_ref[...] = jnp.zeros_like(dcq_ref)

        key = lax.broadcasted_iota(jnp.int32, (tk, tq), 0)
        qry = lax.broadcasted_iota(jnp.int32, (tk, tq), 1)
        low = _head_select((tk, 128), 0)
        for first in range(0, N_HEADS // 2, pairs_per_loop):
            pairs = range(first, first + pairs_per_loop)
            heads = [(pr, hh) for pr in pairs for hh in range(2)]
            kh, vh, ck = {}, {}, {}
            for pr in pairs:
                k2 = k_ref[:, pr * 128:(pr + 1) * 128]
                v2 = v_ref[:, pr * 128:(pr + 1) * 128]
                zero = jnp.zeros_like(k2)
                kh[pr, 0], kh[pr, 1] = jnp.where(low, k2, zero), jnp.where(low, zero, k2)
                vh[pr, 0], vh[pr, 1] = jnp.where(low, v2, zero), jnp.where(low, zero, v2)
                for hh in range(2):
                    ck[pr, hh] = ck_ref[:, 2 * pr + hh:2 * pr + hh + 1]

            def block(qb, carry, band, pairs=pairs, kh=kh, vh=vh, ck=ck):
                rows = pl.ds(pl.multiple_of(qb * tq, tq), tq)
                out = []
                it = iter(carry)
                for pr in pairs:
                    lanes = slice(pr * 128, (pr + 1) * 128)
                    q2 = q_ref[rows, lanes]
                    do2 = do_ref[rows, lanes]
                    dq = None
                    for hh in range(2):
                        h = 2 * pr + hh
                        dk, dv, dc = next(it)
                        s_t = _dot_nt(kh[pr, hh], q2) - ck[pr, hh]
                        p_t = jnp.exp(s_t - lse_ref[h, pl.ds(qb, 1), :])
                        if band is not None:
                            p_t = jnp.where(qry + band * tq >= key, p_t, 0.0)
                        ds_t = p_t * (_dot_nt(vh[pr, hh], do2) - dd_ref[h, pl.ds(qb, 1), :])
                        dsb = ds_t.astype(MM)
                        dv = dv + _dot(p_t.astype(MM), do2)
                        dk = dk + _dot(dsb, q2)
                        dc = dc - jnp.sum(ds_t, axis=1, keepdims=True)
                        part = _dot_tn(dsb, kh[pr, hh])
                        dq = part if dq is None else dq + part
                        dcq_ref[h, pl.ds(qb, 1), :] += jnp.sum(ds_t, axis=0, keepdims=True)
                        out.append((dk, dv, dc))
                    dq_ref[rows, lanes] += dq
                return tuple(out)

            carry = tuple((jnp.zeros((tk, 128), F32), jnp.zeros((tk, 128), F32), jnp.zeros((tk, 1), F32))
                          for _ in heads)
            for band in range(n_band):
                carry = block(kb * n_band + band, carry, band=band)
            carry = lax.fori_loop((kb + 1) * n_band, n_qblk, functools.partial(block, band=None), carry)
            grads = dict(zip(heads, carry))
            for pr in pairs:
                lanes = slice(pr * 128, (pr + 1) * 128)
                dk_ref[:, lanes] = jnp.where(low, grads[pr, 0][0], grads[pr, 1][0]).astype(MM)
                dv_ref[:, lanes] = jnp.where(low, grads[pr, 0][1], grads[pr, 1][1]).astype(MM)
                for hh in range(2):
                    dc_ref[:, 2 * pr + hh:2 * pr + hh + 1] = grads[pr, hh][2]

        @pl.when(kb == n_blk - 1)
        def _():
            dq_ref[...] = dq_ref[...] * Q_SCALE

    return pl.pallas_call(
        body, name="fox_bwd", grid=(n_blk,),
        in_specs=[pl.BlockSpec((S, D_ATT), lambda i: (0, 0)), pl.BlockSpec((tk, D_ATT), lambda i: (i, 1)),
                  pl.BlockSpec((tk, D_ATT), lambda i: (i, 2)), pl.BlockSpec((S, D_ATT), lambda i: (0, 0)),
                  _resident(), _resident(), _rows(tk, N_HEADS)],
        out_specs=[_const((S, D_ATT)), _rows(tk, D_ATT), _rows(tk, D_ATT), _rows(tk, N_HEADS),
                   _const((N_HEADS, n_qblk, tq))],
        out_shape=[jax.ShapeDtypeStruct((S, D_ATT), F32), jax.ShapeDtypeStruct((S, D_ATT), MM),
                   jax.ShapeDtypeStruct((S, D_ATT), MM), jax.ShapeDtypeStruct((S, N_HEADS), F32),
                   jax.ShapeDtypeStruct((N_HEADS, n_qblk, tq), F32)],
        compiler_params=_params("arbitrary"),
    )(fqkv, fqkv, fqkv, dcat, lse_row3, d_row3, c_col)


def _swa_bwd(sqkv, dcat, biasm, sinks_slot, bucket, lse, d_col):
    S = sqkv.shape[0]
    n_blk = S // WINDOW

    def body(q_ref, kp_ref, kc_ref, vp_ref, vc_ref, do_ref, bias_ref, sink_ref, bk_ref, lse_ref, dd_ref,
             dq_ref, dk_ref, dv_ref, drb_ref, dsink_ref, ds_acc):
        n = pl.program_id(0)

        @pl.when(n == 0)
        def _():
            dk_ref[...] = jnp.zeros_like(dk_ref)
            dv_ref[...] = jnp.zeros_like(dv_ref)
            ds_acc[...] = jnp.zeros_like(ds_acc)
            dsink_ref[...] = jnp.zeros_like(dsink_ref)

        no_prev = jnp.where(n > 0, 0.0, NEG)
        prev = pl.ds(pl.multiple_of(jnp.maximum(n - 1, 0) * WINDOW, WINDOW), WINDOW)
        cur = pl.ds(pl.multiple_of(n * WINDOW, WINDOW), WINDOW)
        lane8 = lax.broadcasted_iota(jnp.int32, (1, N_HEADS), 1)
        dkp = jnp.zeros((WINDOW, D_KV), F32)
        dkc = jnp.zeros((WINDOW, D_KV), F32)
        dvp = jnp.zeros((WINDOW, D_KV), F32)
        dvc = jnp.zeros((WINDOW, D_KV), F32)
        dsink = jnp.zeros((1, N_HEADS), F32)
        low = _head_select((WINDOW, 128), 0)
        zero = jnp.zeros((WINDOW, 128), MM)
        dqs = []
        for g in range(2):
            sel = low if g == 0 else jnp.logical_not(low)
            qg = _stack4(lambda j: jnp.where(sel, q_ref[:, j * 128:(j + 1) * 128], zero))
            dog = _stack4(lambda j: jnp.where(sel, do_ref[:, j * 128:(j + 1) * 128], zero))
            lse_g = _stack4(lambda j: lse_ref[:, 2 * j + g:2 * j + g + 1])
            dd = _stack4(lambda j: dd_ref[:, 2 * j + g:2 * j + g + 1])
            sink = _stack4(lambda j: jnp.full((WINDOW, 1), sink_ref[2 * j + g], F32))
            pp = jnp.exp(_dot_nt(qg, kp_ref[...]) + _stack4(lambda j: bias_ref[2 * j + g, :, :WINDOW]) + no_prev - lse_g)
            pc = jnp.exp(_dot_nt(qg, kc_ref[...]) + _stack4(lambda j: bias_ref[2 * j + g, :, WINDOW:]) - lse_g)
            sink_term = jnp.exp(sink - lse_g) * dd
            dsp = pp * (_dot_nt(dog, vp_ref[...]) - dd)
            dsc = pc * (_dot_nt(dog, vc_ref[...]) - dd)
            for j in range(4):
                rows = slice(j * WINDOW, (j + 1) * WINDOW)
                dsink = dsink + jnp.where(lane8 == 2 * j + g, -jnp.sum(sink_term[rows]), 0.0)
                ds_acc[2 * j + g, :, :WINDOW] += dsp[rows]
                ds_acc[2 * j + g, :, WINDOW:] += dsc[rows]
            dspb, dscb = dsp.astype(MM), dsc.astype(MM)
            dqs.append(_dot(dspb, kp_ref[...]) + _dot(dscb, kc_ref[...]))
            dkp = dkp + _dot_tn(dspb, qg)
            dkc = dkc + _dot_tn(dscb, qg)
            dvp = dvp + _dot_tn(pp.astype(MM), dog)
            dvc = dvc + _dot_tn(pc.astype(MM), dog)
        for j in range(4):
            rows = slice(j * WINDOW, (j + 1) * WINDOW)
            dq_ref[:, j * 128:(j + 1) * 128] = (jnp.where(low, dqs[0][rows], dqs[1][rows]) * Q_SCALE).astype(MM)
        dk_ref[prev, :] += dkp
        dk_ref[cur, :] += dkc
        dv_ref[prev, :] += dvp
        dv_ref[cur, :] += dvc
        dsink_ref[...] += dsink

        @pl.when(n == n_blk - 1)
        def _():
            bk = bk_ref[...]
            rb = lax.broadcasted_iota(jnp.int32, (N_BUCKETS, N_HEADS), 0)
            cb = lax.broadcasted_iota(jnp.int32, (N_BUCKETS, N_HEADS), 1)
            out = jnp.zeros((N_BUCKETS, N_HEADS), F32)
            for s in range(N_HEADS):
                acc = ds_acc[s]
                for b in range(N_BUCKETS):
                    out = out + jnp.where((rb == b) & (cb == s), jnp.sum(jnp.where(bk == b, acc, 0.0)), 0.0)
            drb_ref[...] = out

    do_spec = pl.BlockSpec((WINDOW, D_ATT), lambda n: (n, 1))
    return pl.pallas_call(
        body, name="swa_bwd", grid=(n_blk,),
        in_specs=_swa_specs(S) + [do_spec, _resident(), pl.BlockSpec(memory_space=pltpu.SMEM), _resident(),
                                  _rows(WINDOW, N_HEADS), _rows(WINDOW, N_HEADS)],
        out_specs=[_rows(WINDOW, D_ATT), _const((S, D_KV)), _const((S, D_KV)), _const((N_BUCKETS, N_HEADS)),
                   _const((1, N_HEADS))],
        out_shape=[jax.ShapeDtypeStruct((S, D_ATT), MM), jax.ShapeDtypeStruct((S, D_KV), F32),
                   jax.ShapeDtypeStruct((S, D_KV), F32), jax.ShapeDtypeStruct((N_BUCKETS, N_HEADS), F32),
                   jax.ShapeDtypeStruct((1, N_HEADS), F32)],
        scratch_shapes=[pltpu.VMEM((N_HEADS, WINDOW, 2 * WINDOW), F32)],
        compiler_params=_params("arbitrary"),
    )(sqkv, sqkv, sqkv, sqkv, sqkv, dcat, biasm, sinks_slot, bucket, lse, d_col)


D_Z = 3 * D_ATT + D_ATT + 2 * D_KV


def _pack_dz(dq_fox, dk_fox, dv_fox, dsq, dsk, dsv, tm):
    S = dq_fox.shape[0]

    def body(dq_ref, dk_ref, dv_ref, dsq_ref, dsk_ref, dsv_ref, dz_ref):
        dz_ref[:, 0:512] = dq_ref[...].astype(MM)
        dz_ref[:, 512:1024] = dk_ref[...]
        dz_ref[:, 1024:1536] = dv_ref[...]
        dz_ref[:, 1536:2048] = dsq_ref[...]
        dz_ref[:, 2048:2176] = dsk_ref[...].astype(MM)
        dz_ref[:, 2176:2304] = dsv_ref[...].astype(MM)

    return pl.pallas_call(
        body, name="pack_dz", grid=(S // tm,),
        in_specs=[_rows(tm, D_ATT), _rows(tm, D_ATT), _rows(tm, D_ATT), _rows(tm, D_ATT), _rows(tm, D_KV),
                  _rows(tm, D_KV)],
        out_specs=_rows(tm, D_Z), out_shape=jax.ShapeDtypeStruct((S, D_Z), MM),
        compiler_params=_params("parallel"),
    )(dq_fox, dk_fox, dv_fox, dsq, dsk, dsv)


def _pre_attn_bwd(x, dh1, dz, dff_t, wt, g1, tm):
    S = x.shape[0]

    def body(x_ref, dh1_ref, dz_ref, dff_ref, wt_ref, g1_ref, dx_ref, dg1_ref):
        i = pl.program_id(0)
        da = (_dot(dz_ref[:, 0:WT_FF], wt_ref[0:WT_FF]) + _dot(dz_ref[:, WT_FF:D_Z], wt_ref[WT_SQ:WT_ROWS])
              + _dot_tn(dff_ref[...].astype(MM), wt_ref[WT_FF:WT_SQ]))
        n1, r1 = _rms(x_ref[...])
        dx, dg1 = _rms_bwd(da, n1, r1, g1_ref[...])
        _accumulate(dg1_ref, dg1, i)
        dx_ref[...] = dh1_ref[...] + dx

    return pl.pallas_call(
        body, name="pre_attn_bwd", grid=(S // tm,),
        in_specs=[_rows(tm, D_MODEL), _rows(tm, D_MODEL), _rows(tm, D_Z), pl.BlockSpec((16, tm), lambda i: (0, i)),
                  _resident(), _const((1, D_MODEL))],
        out_specs=[_rows(tm, D_MODEL), _const((1, D_MODEL))],
        out_shape=[jax.ShapeDtypeStruct((S, D_MODEL), F32), jax.ShapeDtypeStruct((1, D_MODEL), F32)],
        compiler_params=_params("arbitrary"),
    )(x, dh1, dz, dff_t, wt, g1)


def _weight_grad(a, b, name, tk, n_chunks=1, relu2=False):
    S, K = a.shape
    N = b.shape[1]
    cn = N // n_chunks

    def body(a_ref, b_ref, out_ref):
        av = a_ref[...]
        if relu2:
            av = jnp.square(jnp.maximum(av.astype(F32), 0.0))
        av = av.astype(MM)
        for j in range(n_chunks):
            val = _dot_tn(av, b_ref[:, j * cn:(j + 1) * cn].astype(MM)).astype(MM)
            if n_chunks > 1:
                out_ref[j] = val
            else:
                out_ref[...] = val

    if n_chunks > 1:
        out_spec = pl.BlockSpec((n_chunks, tk, cn), lambda i: (0, i, 0))
        out_shape = jax.ShapeDtypeStruct((n_chunks, K, cn), MM)
    else:
        out_spec = pl.BlockSpec((tk, N), lambda i: (i, 0))
        out_shape = jax.ShapeDtypeStruct((K, N), MM)
    return pl.pallas_call(
        body, name=name, grid=(K // tk,),
        in_specs=[pl.BlockSpec((S, tk), lambda i: (0, i)), _resident()],
        out_specs=out_spec, out_shape=out_shape, compiler_params=_params("parallel"),
    )(a, b)


def _forget_weight_grad(dff_t, a):
    def body(d_ref, a_ref, out_ref):
        out_ref[...] = _dot(d_ref[...].astype(MM), a_ref[...])

    return pl.pallas_call(
        body, name="forget_weight_grad", out_shape=jax.ShapeDtypeStruct((16, D_MODEL), F32),
        in_specs=[_resident(), _resident()], out_specs=_resident(),
    )(dff_t, a)


def _place():
    return lax.axis_index("x"), lax.axis_index("y"), lax.axis_index("c")


def _all_gather_sequencer(stack):
    ref = jax.new_ref(stack, memory_space=pltpu.MemorySpace.HBM)

    @pl.kernel(mesh=plsc.ScalarSubcoreMesh(axis_name="sequencer", num_cores=1), name="all_gather_sequencer",
               scratch_types=(pltpu.SemaphoreType.DMA((7,)), pltpu.SemaphoreType.DMA((7,))),
               compiler_params=pltpu.CompilerParams(collective_id=1))
    def launch(send_sems, recv_sems):
        x, y, c = _place()
        sibling = (x, y, 1 - c)
        chips = [(1 - x, y), (x, 1 - y), (1 - x, 1 - y)]
        peers = [sibling] + [(px, py, c) for px, py in chips]
        barrier = pltpu.get_barrier_semaphore()
        for peer in peers:
            pl.semaphore_signal(barrier, inc=1, device_id=peer, device_id_type=MESH)
        pl.semaphore_wait(barrier, len(peers))

        def copy(k, block, to):
            px, py, pc = block
            slot = ref.at[4 * px + 2 * py + pc]
            return _remote(slot, slot, send_sems, recv_sems, k, to)

        first = [copy(k, (x, y, c), peer) for k, peer in enumerate(peers)]
        for cp in first:
            cp.start()
        passed = []
        for j, (px, py) in enumerate(chips):
            copy(1 + j, (px, py, c), sibling).wait_recv()
            passed.append(copy(4 + j, (px, py, c), sibling))
            passed[-1].start()
        copy(0, (x, y, 1 - c), sibling).wait_recv()
        for j, (px, py) in enumerate(chips):
            copy(4 + j, (px, py, 1 - c), sibling).wait_recv()
        for cp in first + passed:
            cp.wait_send()

    launch()
    return ref[...]


def _chip_sum(grad, other, name):
    _, _, r, cdim = grad.shape
    tr = 512 if r % 512 == 0 else r

    def body(c_ref, g_ref, o_ref, out_ref):
        out_ref[...] = (g_ref[...].astype(F32) + o_ref[...].astype(F32)).astype(out_ref.dtype)

    return pl.pallas_call(
        body, name=name,
        grid_spec=pltpu.PrefetchScalarGridSpec(
            num_scalar_prefetch=1, grid=(4, r // tr),
            in_specs=[pl.BlockSpec((None, None, tr, cdim), lambda k, i, c_ref: (k, c_ref[0], i, 0)),
                      pl.BlockSpec((None, tr, cdim), lambda k, i, c_ref: (k, i, 0))],
            out_specs=pl.BlockSpec((None, tr, cdim), lambda k, i, c_ref: (k, i, 0))),
        out_shape=jax.ShapeDtypeStruct((4, r, cdim), MM),
        compiler_params=_params("parallel", "parallel"),
    )(lax.axis_index("c").astype(jnp.int32).reshape(1), grad, other)


HBM_SPEC = pl.BlockSpec(memory_space=pltpu.HBM)
SEM_SPEC = pl.BlockSpec(memory_space=pltpu.SEMAPHORE)
DATAFLOW = pltpu.SideEffectType.DATAFLOW_SIDE_EFFECTING


def _exchange_start(name, arrays, n_copies, plan):
    n = len(arrays)

    def body(*refs):
        send_sems, recv_sems, token = refs[n], refs[n + 1], refs[2 * n + 2]
        for cp in plan(refs[:n], send_sems, recv_sems):
            cp.start()
        token[...] = jnp.zeros_like(token)

    out = pl.pallas_call(
        body, name=name,
        out_shape=(pltpu.SemaphoreType.DMA((n_copies,)), pltpu.SemaphoreType.DMA((n_copies,)),
                   *[pltpu.HBM(a.shape, a.dtype) for a in arrays], jax.ShapeDtypeStruct((1, D_MODEL), F32)),
        in_specs=[HBM_SPEC] * n,
        out_specs=(SEM_SPEC, SEM_SPEC, *[HBM_SPEC] * n, pl.BlockSpec(memory_space=pltpu.VMEM)),
        input_output_aliases={i: 2 + i for i in range(n)},
        compiler_params=pltpu.CompilerParams(has_side_effects=DATAFLOW),
    )(*[pltpu.with_memory_space_constraint(a, pltpu.HBM) for a in arrays])
    return (out[0], out[1]), list(out[2:2 + n]), out[2 + n]


def _exchange_wait(name, arrays, sems, after, plan):
    n = len(arrays)
    after = list(after) if isinstance(after, (list, tuple)) else [after]

    def body(*refs):
        send_sems, recv_sems = refs[n], refs[n + 1]
        for cp in plan(refs[:n], send_sems, recv_sems):
            cp.wait_send()
            cp.wait_recv()

    out = pl.pallas_call(
        body, name=name, out_shape=[pltpu.HBM(a.shape, a.dtype) for a in arrays],
        in_specs=[HBM_SPEC] * n + [SEM_SPEC, SEM_SPEC] + [pl.BlockSpec(memory_space=pl.ANY)] * len(after),
        out_specs=[HBM_SPEC] * n, input_output_aliases={i: i for i in range(n)},
        compiler_params=pltpu.CompilerParams(has_side_effects=DATAFLOW),
    )(*arrays, sems[0], sems[1], *after)
    return list(out)


def _remote(src, dst, send_sems, recv_sems, k, to):
    return pltpu.make_async_remote_copy(src_ref=src, dst_ref=dst, send_sem=send_sems.at[k], recv_sem=recv_sems.at[k],
                                        device_id=to, device_id_type=MESH)


def _plan_gather_direct(refs, send_sems, recv_sems):
    x, y, c = _place()
    me = 4 * x + 2 * y + c
    peers = [(x, y, 1 - c), (1 - x, y, c), (x, 1 - y, c), (1 - x, 1 - y, c)]
    return [_remote(ref.at[me], ref.at[me], send_sems, recv_sems, 4 * a + k, peer)
            for a, ref in enumerate(refs) for k, peer in enumerate(peers)]


def _plan_gather_pass_on(refs, send_sems, recv_sems):
    x, y, c = _place()
    chips = [(1 - x, y), (x, 1 - y), (1 - x, 1 - y)]
    return [_remote(ref.at[4 * px + 2 * py + c], ref.at[4 * px + 2 * py + c], send_sems, recv_sems, 3 * a + k,
                    (x, y, 1 - c))
            for a, ref in enumerate(refs) for k, (px, py) in enumerate(chips)]


def _plan_in_chip(refs, send_sems, recv_sems):
    n = len(refs) // 2
    x, y, c = _place()
    return [_remote(refs[a].at[:, 1 - c], refs[n + a], send_sems, recv_sems, a, (x, y, 1 - c)) for a in range(n)]


def _plan_between_chips(refs, send_sems, recv_sems):
    n = len(refs) // 2
    x, y, c = _place()
    chips = [(1 - x, y), (x, 1 - y), (1 - x, 1 - y)]
    return [_remote(refs[a].at[2 * px + py], refs[n + a].at[2 * x + y], send_sems, recv_sems, 3 * a + k, (px, py, c))
            for a in range(n) for k, (px, py) in enumerate(chips)]


def _plan_late_between(refs, send_sems, recv_sems):
    sums, land, small = refs
    x, y, c = _place()
    me = 4 * x + 2 * y + c
    copies = _plan_between_chips([sums, land], send_sems, recv_sems)
    peers = [(x ^ dx, y ^ dy, c ^ dc) for dx in range(2) for dy in range(2) for dc in range(2) if dx + dy + dc]
    return copies + [_remote(small.at[me], small.at[me], send_sems, recv_sems, 3 + k, peer)
                     for k, peer in enumerate(peers)]


def _adamw_math(w, g, m, v):
    m = ADAM_B1 * m + (1.0 - ADAM_B1) * g
    v = ADAM_B2 * v + (1.0 - ADAM_B2) * jnp.square(g)
    m_hat = m / (1.0 - ADAM_B1 ** ADAM_STEP)
    v_hat = v / (1.0 - ADAM_B2 ** ADAM_STEP)
    delta = -ADAM_LR * (m_hat / (jnp.sqrt(v_hat) + ADAM_EPS) + ADAM_WD * w)
    return delta, m, v


def _adamw(parts, w, m, v, name):
    n_parts, r, cdim = parts.shape
    tr = 256 if r % 256 == 0 else r

    def body(p_ref, w_ref, m_ref, v_ref, g_out, d_out, m_out, v_out):
        g = p_ref[0].astype(F32)
        for k in range(1, n_parts):
            g = g + p_ref[k].astype(F32)
        delta, m_new, v_new = _adamw_math(w_ref[...], g, m_ref[...], v_ref[...])
        g_out[...] = g
        d_out[...] = delta
        m_out[...] = m_new
        v_out[...] = v_new

    blk = pl.BlockSpec((tr, cdim), lambda i: (i, 0))
    return pl.pallas_call(
        body, name=name, grid=(r // tr,),
        in_specs=[pl.BlockSpec((n_parts, tr, cdim), lambda i: (0, i, 0)), blk, blk, blk],
        out_specs=[blk] * 4, out_shape=[jax.ShapeDtypeStruct((r, cdim), F32)] * 4,
        compiler_params=_params("parallel"),
    )(parts, w, m, v)


def _adamw_chips(parts, sums, w, m, v, name):
    _, r, cdim = parts.shape
    tr = 256 if r % 256 == 0 else r

    def body(chip_ref, p_ref, own_ref, w_ref, m_ref, v_ref, g_out, d_out, m_out, v_out):
        g = None
        for k in range(4):
            term = jnp.where(chip_ref[0] == k, own_ref[...], p_ref[k]).astype(F32)
            g = term if g is None else g + term
        delta, m_new, v_new = _adamw_math(w_ref[...], g, m_ref[...], v_ref[...])
        g_out[...] = g
        d_out[...] = delta
        m_out[...] = m_new
        v_out[...] = v_new

    blk = pl.BlockSpec((tr, cdim), lambda i, chip: (i, 0))
    my_chip = (2 * lax.axis_index("x") + lax.axis_index("y")).astype(jnp.int32).reshape(1)
    return pl.pallas_call(
        body, name=name,
        grid_spec=pltpu.PrefetchScalarGridSpec(
            num_scalar_prefetch=1, grid=(r // tr,),
            in_specs=[pl.BlockSpec((4, tr, cdim), lambda i, chip: (0, i, 0)),
                      pl.BlockSpec((None, tr, cdim), lambda i, chip: (chip[0], i, 0)), blk, blk, blk],
            out_specs=[blk] * 4),
        out_shape=[jax.ShapeDtypeStruct((r, cdim), F32)] * 4,
        compiler_params=_params("parallel"),
    )(my_chip, parts, sums, w, m, v)


class _NoExchange:
    def __init__(self, weights):
        self.weights = weights

    def before_pre_attn(self, g1):
        return g1

    def after_fox_fwd(self, fox_o, sinks_slot):
        return sinks_slot

    def after_attention(self, swa_o):
        return self.weights

    def after_early_grads(self, grads, d_col):
        return d_col

    def after_swa_bwd(self, dsq, d_row3):
        return d_row3

    def after_w_in_grad(self, d_win, g1):
        return g1


def _slot_order(t, axis):
    shp = t.shape
    t = t.reshape(shp[:axis] + (N_HEADS, shp[axis] // N_HEADS) + shp[axis + 1:])
    t = jnp.take(t, np.array(SLOT_HEAD), axis=axis)
    return t.reshape(shp)


def _head_order(t, axis):
    shp = t.shape
    t = t.reshape(shp[:axis] + (N_HEADS, shp[axis] // N_HEADS) + shp[axis + 1:])
    t = jnp.take(t, np.array(HEAD_SLOT), axis=axis)
    return t.reshape(shp)


def _forward_backward(x, p, target, win_t, hooks, b_forget, rel_bias, sinks, g1, g2, g3, g4, g5):
    S = x.shape[0]
    tm = 256
    tm_mlp = 512
    t = 256
    q0 = 3 * D_ATT + N_HEADS
    win_t = win_t.reshape(D_IN, D_MODEL)
    wt = jnp.concatenate(
        [win_t[:q0], jnp.zeros((8, D_MODEL), MM)]
        + [win_t[q0 + HEAD_DIM * h:q0 + HEAD_DIM * (h + 1)] for h in SLOT_HEAD] + [win_t[q0 + D_ATT:]], axis=0)
    bcol = jnp.pad(b_forget.reshape(N_HEADS, 1), ((0, 8), (0, 0)))
    rel_bias_slot = rel_bias[:, np.array(SLOT_HEAD)]
    sinks_slot = sinks.reshape(N_HEADS)[np.array(SLOT_HEAD)]
    bucket = jnp.asarray(_swa_bucket_map())

    a, fqkv, sqkv, fft = _pre_attn(x, hooks.before_pre_attn(g1), wt, tm)
    c_row = _forget_cumsum(fft, bcol)
    c_col = c_row[:N_HEADS].T
    c_row3 = c_row[:N_HEADS].reshape(N_HEADS, S // t, t)
    fox_o, fox_lse = _fox_fwd(fqkv, c_row3, tq=512, tk=t)
    biasm = _swa_bias(rel_bias_slot, bucket)
    sinks_slot = hooks.after_fox_fwd(fox_o, sinks_slot)
    swa_o, swa_lse = _swa_fwd(sqkv, biasm, sinks_slot)
    wout, w1, w2, wple, wg = hooks.after_attention(swa_o)
    wout_fox = wout[:D_ATT]
    wout_swa = _slot_order(wout[D_ATT:], 0)
    mix, h1, m = _post_attn(x, fox_o, swa_o, wout_fox, wout_swa, g2, g3, tm)
    u, y, h2 = _mlp_fwd(m, h1, w1, w2, g4, tm_mlp)
    dh2, dpe, dgl, dg5, loss = _ple_loss(h2, p, target, wg, wple, g5, tm)

    d_wple = _weight_grad(p, dpe, "grad_w_ple", tk=D_PLE, n_chunks=N_DEV)
    d_wg = _weight_grad(h2, dgl, "grad_w_ple_gate", tk=256)
    dh1, dy, du, dg4, dg3 = _mlp_bwd(dh2, y, h1, u, w1, w2, g4, g3, tm)
    d_w2 = _weight_grad(u, dy, "grad_w_ff2", tk=256, relu2=True)
    d_w1 = _weight_grad(m, du, "grad_w_ff1", tk=256, n_chunks=N_DEV)
    head = np.arange(D_ATT) // HEAD_DIM
    head_rows = jnp.asarray((head[None, :] == np.arange(N_HEADS)[:, None]).astype(np.float32))
    dmix, dcat, d_row, d_col, dg2 = _attn_out_bwd(dh1, mix, fox_o, swa_o, wout_fox, wout_swa, g2,
                                                  head_rows, head_rows.T, tm)
    d_wout_fox = _weight_grad(fox_o, dmix, "grad_w_out_fox", tk=256)
    d_wout_swa = _weight_grad(swa_o, dmix, "grad_w_out_swa", tk=256)
    d_wout = jnp.concatenate([d_wout_fox, _head_order(d_wout_swa, 0)], axis=0).reshape(N_DEV, D_MODEL // N_DEV, D_MODEL)
    early = dict(w_ff1=d_w1, w_ff2=d_w2.reshape(N_DEV, FF_CHUNK, D_MODEL), w_ple=d_wple,
                 w_ple_gate=d_wg.reshape(N_DEV, D_MODEL // N_DEV, D_MODEL), w_out=d_wout)

    d_col = hooks.after_early_grads(early, d_col)
    dsq, dsk, dsv, d_rb_slot, d_sink_slot = _swa_bwd(sqkv, dcat, biasm, sinks_slot, bucket, swa_lse, d_col)
    lse_row3 = fox_lse.T.reshape(N_HEADS, S // t, t)
    d_row3 = hooks.after_swa_bwd(dsq, d_row.reshape(N_HEADS, S // t, t))
    dq_fox, dk_fox, dv_fox, dc_col, dcq = _fox_bwd(fqkv, dcat, lse_row3, d_row3, c_col, tq=t, tk=512)
    dc_row = jnp.pad(dc_col.T + dcq.reshape(N_HEADS, S), ((0, 8), (0, 0)))
    dff_t, db = _forget_bwd(dc_row, fft, bcol)
    dz = _pack_dz(dq_fox, dk_fox, dv_fox, dsq, dsk, dsv, 512)
    d_wmain = _weight_grad(dz, a, "grad_w_in", tk=256)
    d_wff_t = _forget_weight_grad(dff_t, a)

    sq0 = 3 * D_ATT
    d_win = jnp.concatenate(
        [d_wmain[:sq0], d_wff_t[:N_HEADS].astype(MM)]
        + [d_wmain[sq0 + HEAD_DIM * s:sq0 + HEAD_DIM * (s + 1)] for s in HEAD_SLOT] + [d_wmain[sq0 + D_ATT:]], axis=0)
    d_win = d_win.reshape(N_DEV, D_IN // N_DEV, D_MODEL)
    grad_x, dg1 = _pre_attn_bwd(x, dh1, dz, dff_t, wt, hooks.after_w_in_grad(d_win, g1), tm)
    big = dict(early, w_in=d_win)
    small = dict(b_forget=db[:N_HEADS].reshape(1, N_HEADS), rel_bias=d_rb_slot[:, np.array(HEAD_SLOT)],
                 swa_sinks=d_sink_slot[:, np.array(HEAD_SLOT)], g_attn_pre=dg1, g_attn_post=dg2, g_ff_pre=dg3,
                 g_ff_post=dg4, g_ple_post=dg5)
    return loss, grad_x, big, small


BIG = ("w_in", "w_out", "w_ff1", "w_ff2", "w_ple", "w_ple_gate")
SMALL_ROWS = ("g_attn_pre", "g_attn_post", "g_ff_pre", "g_ff_post", "g_ple_post")
WEIGHTS = ("w_in", "b_forget", "w_out", "rel_bias", "swa_sinks", "g_attn_pre", "g_attn_post", "w_ff1", "w_ff2",
           "g_ff_pre", "g_ff_post", "w_ple", "w_ple_gate", "g_ple_post")


EARLY = ("w_ff1", "w_ff2", "w_ple", "w_ple_gate", "w_out")


class _Overlap:
    def __init__(self, later):
        self.later = later

    def before_pre_attn(self, g1):
        self.gather_sems, self.later, token = _exchange_start("gather_rest_start", self.later, 4 * 5, _plan_gather_direct)
        return g1 + token

    def after_fox_fwd(self, fox_o, sinks_slot):
        later = _exchange_wait("gather_rest_wait", self.later, self.gather_sems, fox_o, _plan_gather_direct)
        self.pass_sems, self.later, token = _exchange_start("gather_pass_on_start", later, 3 * 5, _plan_gather_pass_on)
        return sinks_slot + token[0, :N_HEADS]

    def after_attention(self, swa_o):
        wout_g, w1_g, w2_g, wple_g, wg_g = _exchange_wait("gather_pass_on_wait", self.later, self.pass_sems, swa_o,
                                                         _plan_gather_pass_on)
        return (wout_g.reshape(D_MODEL, D_MODEL), w1_g, w2_g.reshape(D_FF, D_MODEL),
                jnp.moveaxis(wple_g, 0, 1).reshape(D_PLE, D_MODEL), wg_g.reshape(D_MODEL, D_MODEL))

    def after_early_grads(self, grads, d_col):
        views = [grads[k].reshape((4, 2) + grads[k].shape[1:]) for k in EARLY]
        lands = [lax.empty((4,) + grads[k].shape[1:], MM) for k in EARLY]
        self.in_chip_sems, self.in_chip, token = _exchange_start("grads_in_chip_start", views + lands, len(EARLY),
                                                                 _plan_in_chip)
        return d_col + token[0, 0]

    def after_swa_bwd(self, dsq, d_row3):
        arrays = _exchange_wait("grads_in_chip_wait", self.in_chip, self.in_chip_sems, dsq, _plan_in_chip)
        n = len(EARLY)
        sums = [_chip_sum(arrays[a], arrays[n + a], "chip_sum_" + k) for a, k in enumerate(EARLY)]
        lands = [lax.empty(s.shape, s.dtype) for s in sums]
        self.between_sems, self.between, token = _exchange_start("grads_between_chips_start", sums + lands, 3 * n,
                                                                 _plan_between_chips)
        return d_row3 + token[0, 0]

    def after_w_in_grad(self, d_win, g1):
        self.late_in_chip_sems, self.late_in_chip, token = _exchange_start(
            "late_in_chip_start", [d_win.reshape((4, 2) + d_win.shape[1:]), lax.empty((4,) + d_win.shape[1:], MM)],
            1, _plan_in_chip)
        return g1 + token

    def finish(self, after):
        arrays = _exchange_wait("grads_between_chips_wait", self.between, self.between_sems, after,
                                _plan_between_chips)
        n = len(EARLY)
        self.sums = arrays[:n]
        return arrays[n:]


def _pack_small(t):
    rows = [t[k].reshape(1, D_MODEL) for k in SMALL_ROWS]
    misc = jnp.concatenate([t["b_forget"].reshape(-1), t["swa_sinks"].reshape(-1), t["rel_bias"].reshape(-1)])
    rows.append(jnp.pad(misc, (0, D_MODEL - misc.shape[0])).reshape(1, D_MODEL))
    rows.append(jnp.pad(t["loss"].reshape(-1), (0, D_MODEL - 1)).reshape(1, D_MODEL))
    rows.append(jnp.zeros((1, D_MODEL), F32))
    return jnp.concatenate(rows, axis=0).astype(F32)


def _unpack_small(blk):
    out = {k: blk[i].reshape(1, D_MODEL) for i, k in enumerate(SMALL_ROWS)}
    misc = blk[len(SMALL_ROWS)]
    out["b_forget"] = misc[:N_HEADS].reshape(1, N_HEADS)
    out["swa_sinks"] = misc[N_HEADS:2 * N_HEADS].reshape(1, N_HEADS)
    out["rel_bias"] = misc[2 * N_HEADS:2 * N_HEADS + N_BUCKETS * N_HEADS].reshape(N_BUCKETS, N_HEADS)
    out["loss"] = blk[len(SMALL_ROWS) + 1, 0]
    return out


def kernel(x, p, w_in, b_forget, w_out, rel_bias, swa_sinks, g_attn_pre, g_attn_post, w_ff1, w_ff2, g_ff_pre, g_ff_post, w_ple, w_ple_gate, g_ple_post, loss_target, m_w_in, m_b_forget, m_w_out, m_rel_bias, m_swa_sinks, m_g_attn_pre, m_g_attn_post, m_w_ff1, m_w_ff2, m_g_ff_pre, m_g_ff_post, m_w_ple, m_w_ple_gate, m_g_ple_post, v_w_in, v_b_forget, v_w_out, v_rel_bias, v_swa_sinks, v_g_attn_pre, v_g_attn_post, v_w_ff1, v_w_ff2, v_g_ff_pre, v_g_ff_post, v_w_ple, v_w_ple_gate, v_g_ple_post):
    w = dict(w_in=w_in, b_forget=b_forget, w_out=w_out, rel_bias=rel_bias, swa_sinks=swa_sinks,
             g_attn_pre=g_attn_pre, g_attn_post=g_attn_post, w_ff1=w_ff1, w_ff2=w_ff2, g_ff_pre=g_ff_pre,
             g_ff_post=g_ff_post, w_ple=w_ple, w_ple_gate=w_ple_gate, g_ple_post=g_ple_post)
    mom = dict(w_in=m_w_in, b_forget=m_b_forget, w_out=m_w_out, rel_bias=m_rel_bias, swa_sinks=m_swa_sinks,
               g_attn_pre=m_g_attn_pre, g_attn_post=m_g_attn_post, w_ff1=m_w_ff1, w_ff2=m_w_ff2,
               g_ff_pre=m_g_ff_pre, g_ff_post=m_g_ff_post, w_ple=m_w_ple, w_ple_gate=m_w_ple_gate,
               g_ple_post=m_g_ple_post)
    var = dict(w_in=v_w_in, b_forget=v_b_forget, w_out=v_w_out, rel_bias=v_rel_bias, swa_sinks=v_swa_sinks,
               g_attn_pre=v_g_attn_pre, g_attn_post=v_g_attn_post, w_ff1=v_w_ff1, w_ff2=v_w_ff2,
               g_ff_pre=v_g_ff_pre, g_ff_post=v_g_ff_post, w_ple=v_w_ple, w_ple_gate=v_w_ple_gate,
               g_ple_post=v_g_ple_post)

    turn = lambda t, k: t.T if k == "w_in" else t
    me = 4 * lax.axis_index("x") + 2 * lax.axis_index("y") + lax.axis_index("c")

    def stack(block):
        return lax.dynamic_update_slice_in_dim(lax.empty((N_DEV,) + block.shape, block.dtype), block[None], me, 0)

    stacks = [stack(turn(w[k][0], k).astype(MM)) for k in BIG]
    win_g, later = _all_gather_sequencer(stacks[0]), stacks[1:]
    hooks = _Overlap(later)
    loss, grad_x, big, small = _forward_backward(
        x[0], p[0, 0], loss_target[0], win_g, hooks, b_forget, rel_bias, swa_sinks,
        g_attn_pre, g_attn_post, g_ff_pre, g_ff_post, g_ple_post)
    out_g, out_d, out_m, out_v = {}, {}, {}, {}

    def update(k, part, own):
        g, d, m_new, v_new = _adamw_chips(part, own, turn(w[k][0], k), turn(mom[k][0], k), turn(var[k][0], k),
                                          "adamw_" + k)
        out_g[k], out_d[k], out_m[k], out_v[k] = turn(g, k)[None], turn(d, k)[None], turn(m_new, k)[None], turn(v_new, k)[None]
        return d

    view, other = _exchange_wait("late_in_chip_wait", hooks.late_in_chip, hooks.late_in_chip_sems, grad_x,
                                 _plan_in_chip)
    chip_sum = _chip_sum(view, other, "chip_sum_w_in")
    small["loss"] = loss
    between_sems, between, token = _exchange_start(
        "late_between_chips_start", [chip_sum, lax.empty(chip_sum.shape, MM), stack(_pack_small(small))], 3 + 7,
        _plan_late_between)
    early_parts = hooks.finish(token)
    done = [update(k, part, own) for k, part, own in zip(EARLY, early_parts, hooks.sums)]
    chip_sum, part, small_all = _exchange_wait("late_between_chips_wait", between, between_sems, done,
                                               _plan_late_between)
    update("w_in", part, chip_sum)
    rep ={k: w[k] for k in w if k not in BIG}
    rep["loss"] = jnp.zeros((), F32)
    rep_m = {k: mom[k] for k in mom if k not in BIG}
    rep_m["loss"] = jnp.zeros((), F32)
    rep_v = {k: var[k] for k in var if k not in BIG}
    rep_v["loss"] = jnp.ones((), F32)
    g_s, d_s, m_s, v_s = _adamw(small_all, _pack_small(rep), _pack_small(rep_m), _pack_small(rep_v), "adamw_small")
    g_s, d_s, m_s, v_s = _unpack_small(g_s), _unpack_small(d_s), _unpack_small(m_s), _unpack_small(v_s)
    for k in w:
        if k not in BIG:
            out_g[k], out_d[k], out_m[k], out_v[k] = g_s[k], d_s[k], m_s[k], v_s[k]
    return (g_s["loss"], grad_x[None], *[out_g[k] for k in WEIGHTS], *[out_d[k] for k in WEIGHTS],
            *[out_m[k] for k in WEIGHTS], *[out_v[k] for k in WEIGHTS])
```

```python
import functools

import numpy as np
import jax
import jax.numpy as jnp
from jax import lax
from jax.experimental import pallas as pl
from jax.experimental.pallas import tpu as pltpu
from jax.experimental.pallas import tpu_sc as plsc

F32 = jnp.float32
MM = jnp.bfloat16

D_MODEL = 1024
HEAD_DIM = 64
N_HEADS = 8
D_ATT = N_HEADS * HEAD_DIM
D_KV = 128
D_FF = 4096
D_PLE = 256
D_IN = 3 * D_ATT + N_HEADS + D_ATT + 2 * D_KV
N_DEV = 8
FF_CHUNK = D_FF // N_DEV
WINDOW = 128
N_BUCKETS = 32
MAX_DISTANCE = 128
RMS_EPS = 1e-6
Q_SCALE = HEAD_DIM ** -0.5
NEG = -1e30

ADAM_LR = 0.001
ADAM_B1 = 0.9
ADAM_B2 = 0.999
ADAM_EPS = 1e-08
ADAM_WD = 0.01
ADAM_STEP = 10

SLOT_HEAD = (0, 4, 1, 5, 2, 6, 3, 7)
HEAD_SLOT = (0, 2, 4, 6, 1, 3, 5, 7)

VMEM_LIMIT = 56 * 1024 * 1024
MESH = pl.DeviceIdType.MESH

NT = (((1,), (1,)), ((), ()))
TN = (((0,), (0,)), ((), ()))


def _params(*semantics):
    return pltpu.CompilerParams(dimension_semantics=semantics, vmem_limit_bytes=VMEM_LIMIT)


def _resident():
    return pl.BlockSpec(memory_space=pltpu.VMEM)


def _rows(tm, width):
    return pl.BlockSpec((tm, width), lambda i: (i, 0))


def _const(shape):
    return pl.BlockSpec(shape, lambda i: (0,) * len(shape))


def _dot(a, b):
    return jnp.dot(a, b, preferred_element_type=F32)


def _dot_nt(a, b):
    return lax.dot_general(a, b, NT, preferred_element_type=F32)


def _dot_tn(a, b):
    return lax.dot_general(a, b, TN, preferred_element_type=F32)


def _rms(xf):
    r = lax.rsqrt(jnp.mean(xf * xf, axis=-1, keepdims=True) + RMS_EPS)
    return xf * r, r


def _rms_bwd(dout, n, r, g):
    dg = jnp.sum(dout * n, axis=0, keepdims=True)
    dn = dout * g
    dx = r * (dn - n * jnp.mean(dn * n, axis=-1, keepdims=True))
    return dx, dg


def _accumulate(ref, value, step):
    @pl.when(step == 0)
    def _():
        ref[...] = value

    @pl.when(step != 0)
    def _():
        ref[...] += value


def _t5_bucket(n):
    max_exact = N_BUCKETS // 2
    large = max_exact + (np.log(np.maximum(n, 1) / max_exact) / np.log(MAX_DISTANCE / max_exact)
                         * (N_BUCKETS - max_exact)).astype(np.int32)
    large = np.minimum(large, N_BUCKETS - 1)
    return np.where(n < max_exact, n, large).astype(np.int32)


def _swa_bucket_map():
    i = np.arange(WINDOW)[:, None]
    j = np.arange(2 * WINDOW)[None, :]
    dist = i + WINDOW - j
    ok = (dist >= 0) & (dist < WINDOW)
    return np.where(ok, _t5_bucket(np.clip(dist, 0, None)), -1).astype(np.int32)


WT_FOX = 0
WT_FF = 3 * D_ATT
WT_SQ = WT_FF + 16
WT_SKV = WT_SQ + D_ATT
WT_ROWS = WT_SKV + 2 * D_KV


def _pre_attn(x, g1, wt, tm):
    S = x.shape[0]

    def body(x_ref, g_ref, wt_ref, a_ref, fqkv_ref, sqkv_ref, fft_ref):
        n, _ = _rms(x_ref[...])
        a = (n * g_ref[...]).astype(MM)
        a_ref[...] = a
        fqkv_ref[:, :D_ATT] = (_dot_nt(a, wt_ref[WT_FOX:WT_FOX + D_ATT]) * Q_SCALE).astype(MM)
        fqkv_ref[:, D_ATT:] = _dot_nt(a, wt_ref[WT_FOX + D_ATT:WT_FF]).astype(MM)
        sqkv_ref[:, :D_ATT] = (_dot_nt(a, wt_ref[WT_SQ:WT_SKV]) * Q_SCALE).astype(MM)
        sqkv_ref[:, D_ATT:] = _dot_nt(a, wt_ref[WT_SKV:WT_ROWS]).astype(MM)
        fft_ref[...] = _dot_nt(wt_ref[WT_FF:WT_SQ], a)

    return pl.pallas_call(
        body, name="pre_attn", grid=(S // tm,),
        in_specs=[_rows(tm, D_MODEL), _const((1, D_MODEL)), _resident()],
        out_specs=[_rows(tm, D_MODEL), _rows(tm, 3 * D_ATT), _rows(tm, D_ATT + 2 * D_KV),
                   pl.BlockSpec((16, tm), lambda i: (0, i))],
        out_shape=[jax.ShapeDtypeStruct((S, D_MODEL), MM), jax.ShapeDtypeStruct((S, 3 * D_ATT), MM),
                   jax.ShapeDtypeStruct((S, D_ATT + 2 * D_KV), MM), jax.ShapeDtypeStruct((16, S), F32)],
        compiler_params=_params("parallel"),
    )(x, g1, wt)


def _lane_scan(v, reverse):
    S = v.shape[1]
    lane = lax.broadcasted_iota(jnp.int32, v.shape, 1)
    k = 1
    while k < S:
        if reverse:
            v = v + jnp.where(lane < S - k, pltpu.roll(v, S - k, axis=1), 0.0)
        else:
            v = v + jnp.where(lane >= k, pltpu.roll(v, k, axis=1), 0.0)
        k *= 2
    return v


def _forget_cumsum(fft, bcol):
    def body(f_ref, b_ref, c_ref):
        z = f_ref[...] + b_ref[...]
        log_f = jnp.minimum(z, 0.0) - jnp.log1p(jnp.exp(-jnp.abs(z)))
        c_ref[...] = _lane_scan(log_f, reverse=False)

    return pl.pallas_call(
        body, name="forget_cumsum", out_shape=jax.ShapeDtypeStruct(fft.shape, F32),
        in_specs=[_resident(), _resident()], out_specs=_resident(),
    )(fft, bcol)


def _forget_bwd(dc_row, fft, bcol):
    def body(dc_ref, f_ref, b_ref, dff_ref, db_ref):
        z = f_ref[...] + b_ref[...]
        dlog_f = _lane_scan(dc_ref[...], reverse=True)
        dff = dlog_f * (1.0 / (1.0 + jnp.exp(z)))
        dff_ref[...] = dff
        db_ref[...] = jnp.sum(dff, axis=1, keepdims=True)

    return pl.pallas_call(
        body, name="forget_bwd",
        out_shape=[jax.ShapeDtypeStruct(fft.shape, F32), jax.ShapeDtypeStruct((fft.shape[0], 1), F32)],
        in_specs=[_resident()] * 3, out_specs=[_resident()] * 2,
    )(dc_row, fft, bcol)


def _head_select(shape, upper):
    lane = lax.broadcasted_iota(jnp.int32, shape, 1)
    return lane >= HEAD_DIM if upper else lane < HEAD_DIM


def _fox_fwd(fqkv, c_row3, tq, tk, pairs_per_loop=2, row_chunks=1):
    S = fqkv.shape[0]
    rq = tq // row_chunks
    n_band = tq // tk

    def body(q_ref, k_ref, v_ref, ck_ref, o_ref, lse_ref):
        qi = pl.program_id(0)
        row = lax.broadcasted_iota(jnp.int32, (rq, tk), 0)
        col = lax.broadcasted_iota(jnp.int32, (rq, tk), 1)
        low = _head_select((rq, 128), 0)
        for first in range(0, N_HEADS // 2, pairs_per_loop):
            pairs = range(first, first + pairs_per_loop)
            chains = [(pr, hh, rc) for pr in pairs for hh in range(2) for rc in range(row_chunks)]
            qh = {}
            for pr in pairs:
                for rc in range(row_chunks):
                    q2 = q_ref[rc * rq:(rc + 1) * rq, pr * 128:(pr + 1) * 128]
                    qh[pr, 0, rc] = jnp.where(low, q2, jnp.zeros_like(q2))
                    qh[pr, 1, rc] = jnp.where(low, jnp.zeros_like(q2), q2)

            def block(kb, carry, band, chains=chains, qh=qh):
                rows = pl.ds(pl.multiple_of(kb * tk, tk), tk)
                out = []
                for (pr, hh, rc), (m, l, acc) in zip(chains, carry):
                    if band is not None and (rc + 1) * rq <= band * tk:
                        out.append((m, l, acc))
                        continue
                    lanes = slice(pr * 128, (pr + 1) * 128)
                    s = _dot_nt(qh[pr, hh, rc], k_ref[rows, lanes]) - ck_ref[2 * pr + hh, pl.ds(kb, 1), :]
                    if band is not None:
                        s = jnp.where(row + rc * rq >= col + band * tk, s, NEG)
                    m_new = jnp.maximum(m, jnp.max(s, axis=-1, keepdims=True))
                    p = jnp.exp(s - m_new)
                    alpha = jnp.exp(m - m_new)
                    l = alpha * l + jnp.sum(p, axis=-1, keepdims=True)
                    acc = alpha * acc + _dot(p.astype(MM), v_ref[rows, lanes])
                    out.append((m_new, l, acc))
                return tuple(out)

            carry = tuple((jnp.full((rq, 1), NEG, F32), jnp.zeros((rq, 1), F32), jnp.zeros((rq, 128), F32))
                          for _ in chains)
            carry = lax.fori_loop(0, qi * n_band, functools.partial(block, band=None), carry)
            for band in range(n_band):
                carry = block(qi * n_band + band, carry, band=band)
            res = {}
            for (pr, hh, rc), (m, l, acc) in zip(chains, carry):
                res[pr, hh, rc] = acc / l
                lse_ref[rc * rq:(rc + 1) * rq, 2 * pr + hh:2 * pr + hh + 1] = m + jnp.log(l)
            for pr in pairs:
                for rc in range(row_chunks):
                    o_ref[rc * rq:(rc + 1) * rq, pr * 128:(pr + 1) * 128] = jnp.where(
                        low, res[pr, 0, rc], res[pr, 1, rc]).astype(MM)

    return pl.pallas_call(
        body, name="fox_fwd", grid=(S // tq,),
        in_specs=[pl.BlockSpec((tq, D_ATT), lambda i: (i, 0)), pl.BlockSpec((S, D_ATT), lambda i: (0, 1)),
                  pl.BlockSpec((S, D_ATT), lambda i: (0, 2)), _resident()],
        out_specs=[_rows(tq, D_ATT), _rows(tq, N_HEADS)],
        out_shape=[jax.ShapeDtypeStruct((S, D_ATT), MM), jax.ShapeDtypeStruct((S, N_HEADS), F32)],
        compiler_params=_params("parallel"),
    )(fqkv, fqkv, fqkv, c_row3)


def _swa_bias(rel_bias_slot, bucket):
    def body(rb_ref, bk_ref, out_ref):
        bk = bk_ref[...]
        for s in range(N_HEADS):
            acc = jnp.where(bk < 0, NEG, 0.0).astype(F32)
            for b in range(N_BUCKETS):
                acc = jnp.where(bk == b, rb_ref[b, s], acc)
            out_ref[s] = acc

    return pl.pallas_call(
        body, name="swa_bias", out_shape=jax.ShapeDtypeStruct((N_HEADS, WINDOW, 2 * WINDOW), F32),
        in_specs=[pl.BlockSpec(memory_space=pltpu.SMEM), _resident()], out_specs=_resident(),
    )(rel_bias_slot, bucket)


def _stack4(piece):
    return jnp.concatenate([piece(j) for j in range(4)], axis=0)


def _swa_specs(S):
    q = pl.BlockSpec((WINDOW, D_ATT), lambda n: (n, 0))
    kp = pl.BlockSpec((WINDOW, D_KV), lambda n: (jnp.maximum(n - 1, 0), 4))
    kc = pl.BlockSpec((WINDOW, D_KV), lambda n: (n, 4))
    vp = pl.BlockSpec((WINDOW, D_KV), lambda n: (jnp.maximum(n - 1, 0), 5))
    vc = pl.BlockSpec((WINDOW, D_KV), lambda n: (n, 5))
    return [q, kp, kc, vp, vc]


def _swa_fwd(sqkv, biasm, sinks_slot):
    S = sqkv.shape[0]

    def body(q_ref, kp_ref, kc_ref, vp_ref, vc_ref, bias_ref, sink_ref, o_ref, lse_ref):
        n = pl.program_id(0)
        no_prev = jnp.where(n > 0, 0.0, NEG)
        low = _head_select((WINDOW, 128), 0)
        res = []
        for g in range(2):
            sel = low if g == 0 else jnp.logical_not(low)
            qg = _stack4(lambda j: jnp.where(sel, q_ref[:, j * 128:(j + 1) * 128], jnp.zeros((WINDOW, 128), MM)))
            sink = _stack4(lambda j: jnp.full((WINDOW, 1), sink_ref[2 * j + g], F32))
            sp = _dot_nt(qg, kp_ref[...]) + _stack4(lambda j: bias_ref[2 * j + g, :, :WINDOW]) + no_prev
            sc = _dot_nt(qg, kc_ref[...]) + _stack4(lambda j: bias_ref[2 * j + g, :, WINDOW:])
            m = jnp.maximum(jnp.maximum(jnp.max(sp, axis=-1, keepdims=True),
                                        jnp.max(sc, axis=-1, keepdims=True)), sink)
            ep = jnp.exp(sp - m)
            ec = jnp.exp(sc - m)
            den = jnp.sum(ep, axis=-1, keepdims=True) + jnp.sum(ec, axis=-1, keepdims=True) + jnp.exp(sink - m)
            res.append((_dot(ep.astype(MM), vp_ref[...]) + _dot(ec.astype(MM), vc_ref[...])) / den)
            lse = m + jnp.log(den)
            for j in range(4):
                lse_ref[:, 2 * j + g:2 * j + g + 1] = lse[j * WINDOW:(j + 1) * WINDOW]
        for j in range(4):
            rows = slice(j * WINDOW, (j + 1) * WINDOW)
            o_ref[:, j * 128:(j + 1) * 128] = jnp.where(low, res[0][rows], res[1][rows]).astype(MM)

    return pl.pallas_call(
        body, name="swa_fwd", grid=(S // WINDOW,),
        in_specs=_swa_specs(S) + [_resident(), pl.BlockSpec(memory_space=pltpu.SMEM)],
        out_specs=[_rows(WINDOW, D_ATT), _rows(WINDOW, N_HEADS)],
        out_shape=[jax.ShapeDtypeStruct((S, D_ATT), MM), jax.ShapeDtypeStruct((S, N_HEADS), F32)],
        compiler_params=_params("parallel"),
    )(sqkv, sqkv, sqkv, sqkv, sqkv, biasm, sinks_slot)


def _post_attn(x, fox_o, swa_o, wout_fox, wout_swa, g2, g3, tm):
    S = x.shape[0]

    def body(x_ref, fo_ref, so_ref, wf_ref, ws_ref, g2_ref, g3_ref, mix_ref, h1_ref, m_ref):
        mix = _dot(fo_ref[...], wf_ref[...]) + _dot(so_ref[...], ws_ref[...])
        mix_ref[...] = mix
        n2, _ = _rms(mix)
        h1 = x_ref[...] + n2 * g2_ref[...]
        h1_ref[...] = h1
        n3, _ = _rms(h1)
        m_ref[...] = (n3 * g3_ref[...]).astype(MM)

    return pl.pallas_call(
        body, name="post_attn", grid=(S // tm,),
        in_specs=[_rows(tm, D_MODEL), _rows(tm, D_ATT), _rows(tm, D_ATT), _resident(), _resident(),
                  _const((1, D_MODEL)), _const((1, D_MODEL))],
        out_specs=[_rows(tm, D_MODEL)] * 3,
        out_shape=[jax.ShapeDtypeStruct((S, D_MODEL), F32), jax.ShapeDtypeStruct((S, D_MODEL), F32),
                   jax.ShapeDtypeStruct((S, D_MODEL), MM)],
        compiler_params=_params("parallel"),
    )(x, fox_o, swa_o, wout_fox, wout_swa, g2, g3)


def _mlp_fwd(m, h1, w1, w2, g4, tm):
    S = m.shape[0]

    def body(m_ref, h1_ref, w1_ref, w2_ref, g4_ref, u_ref, y_ref, h2_ref):
        mb = m_ref[...]
        y = jnp.zeros((tm, D_MODEL), F32)
        for j in range(N_DEV):
            cols = slice(j * FF_CHUNK, (j + 1) * FF_CHUNK)
            u = _dot(mb, w1_ref[j])
            u_ref[:, cols] = u.astype(MM)
            y = y + _dot(jnp.square(jnp.maximum(u, 0.0)).astype(MM), w2_ref[cols, :])
        y_ref[...] = y
        n4, _ = _rms(y)
        h2_ref[...] = h1_ref[...] + n4 * g4_ref[...]

    return pl.pallas_call(
        body, name="mlp_fwd", grid=(S // tm,),
        in_specs=[_rows(tm, D_MODEL), _rows(tm, D_MODEL), _resident(), _resident(), _const((1, D_MODEL))],
        out_specs=[_rows(tm, D_FF), _rows(tm, D_MODEL), _rows(tm, D_MODEL)],
        out_shape=[jax.ShapeDtypeStruct((S, D_FF), MM), jax.ShapeDtypeStruct((S, D_MODEL), F32),
                   jax.ShapeDtypeStruct((S, D_MODEL), F32)],
        compiler_params=_params("parallel"),
    )(m, h1, w1, w2, g4)


def _ple_loss(h2, p, target, wg, wple, g5, tm):
    S = h2.shape[0]

    def body(h2_ref, p_ref, t_ref, wg_ref, wp_ref, g5_ref, dh2_ref, dpe_ref, dgl_ref, dg5_ref, loss_ref):
        i = pl.program_id(0)
        h2 = h2_ref[...]
        gate = jax.nn.sigmoid(_dot(h2.astype(MM), wg_ref[...]))
        pe = _dot(p_ref[...].astype(MM), wp_ref[...])
        n5, r5 = _rms(pe * gate)
        g5 = g5_ref[...]
        diff = h2 + n5 * g5 - t_ref[...]
        per_token = jnp.mean(jnp.square(diff), axis=-1, keepdims=True)
        _accumulate(loss_ref, 0.5 * jnp.sum(per_token, axis=0, keepdims=True), i)
        dh3 = diff * (1.0 / D_MODEL)
        de, dg5 = _rms_bwd(dh3, n5, r5, g5)
        _accumulate(dg5_ref, dg5, i)
        dpe_ref[...] = (de * gate).astype(MM)
        dgl = (de * pe * gate * (1.0 - gate)).astype(MM)
        dgl_ref[...] = dgl
        dh2_ref[...] = dh3 + _dot_nt(dgl, wg_ref[...])

    return pl.pallas_call(
        body, name="ple_loss", grid=(S // tm,),
        in_specs=[_rows(tm, D_MODEL), _rows(tm, D_PLE), _rows(tm, D_MODEL), _resident(), _resident(),
                  _const((1, D_MODEL))],
        out_specs=[_rows(tm, D_MODEL), _rows(tm, D_MODEL), _rows(tm, D_MODEL), _const((1, D_MODEL)), _const((1, 1))],
        out_shape=[jax.ShapeDtypeStruct((S, D_MODEL), F32), jax.ShapeDtypeStruct((S, D_MODEL), MM),
                   jax.ShapeDtypeStruct((S, D_MODEL), MM), jax.ShapeDtypeStruct((1, D_MODEL), F32),
                   jax.ShapeDtypeStruct((1, 1), F32)],
        compiler_params=_params("arbitrary"),
    )(h2, p, target, wg, wple, g5)


def _mlp_bwd(dh2, y, h1, u, w1, w2, g4, g3, tm):
    S = dh2.shape[0]

    def body(dh2_ref, y_ref, h1_ref, u_ref, w1_ref, w2_ref, g4_ref, g3_ref,
             dh1_ref, dy_ref, du_ref, dg4_ref, dg3_ref):
        i = pl.program_id(0)
        dh2 = dh2_ref[...]
        n4, r4 = _rms(y_ref[...])
        dy, dg4 = _rms_bwd(dh2, n4, r4, g4_ref[...])
        _accumulate(dg4_ref, dg4, i)
        dyb = dy.astype(MM)
        dy_ref[...] = dyb
        dm = jnp.zeros((tm, D_MODEL), F32)
        for j in range(N_DEV):
            cols = slice(j * FF_CHUNK, (j + 1) * FF_CHUNK)
            dact = _dot_nt(dyb, w2_ref[cols, :])
            du = (dact * (2.0 * jnp.maximum(u_ref[:, cols].astype(F32), 0.0))).astype(MM)
            du_ref[:, cols] = du
            dm = dm + _dot_nt(du, w1_ref[j])
        n3, r3 = _rms(h1_ref[...])
        dx, dg3 = _rms_bwd(dm, n3, r3, g3_ref[...])
        _accumulate(dg3_ref, dg3, i)
        dh1_ref[...] = dh2 + dx

    return pl.pallas_call(
        body, name="mlp_bwd", grid=(S // tm,),
        in_specs=[_rows(tm, D_MODEL), _rows(tm, D_MODEL), _rows(tm, D_MODEL), _rows(tm, D_FF),
                  _resident(), _resident(), _const((1, D_MODEL)), _const((1, D_MODEL))],
        out_specs=[_rows(tm, D_MODEL), _rows(tm, D_MODEL), _rows(tm, D_FF), _const((1, D_MODEL)),
                   _const((1, D_MODEL))],
        out_shape=[jax.ShapeDtypeStruct((S, D_MODEL), F32), jax.ShapeDtypeStruct((S, D_MODEL), MM),
                   jax.ShapeDtypeStruct((S, D_FF), MM), jax.ShapeDtypeStruct((1, D_MODEL), F32),
                   jax.ShapeDtypeStruct((1, D_MODEL), F32)],
        compiler_params=_params("arbitrary"),
    )(dh2, y, h1, u, w1, w2, g4, g3)


def _attn_out_bwd(dh1, mix, fox_o, swa_o, wout_fox, wout_swa, g2, head_rows, tm):
    S = dh1.shape[0]

    def body(dh1_ref, mix_ref, fo_ref, so_ref, wf_ref, ws_ref, g2_ref, er_ref,
             dmix_ref, dcat_ref, drow_ref, dswa_ref, dg2_ref):
        i = pl.program_id(0)
        n2, r2 = _rms(mix_ref[...])
        dmix, dg2 = _rms_bwd(dh1_ref[...], n2, r2, g2_ref[...])
        _accumulate(dg2_ref, dg2, i)
        dmb = dmix.astype(MM)
        dmix_ref[...] = dmb
        dfo = _dot_nt(dmb, wf_ref[...]).astype(MM)
        dso = _dot_nt(dmb, ws_ref[...]).astype(MM)
        dcat_ref[:, :D_ATT] = dfo
        dcat_ref[:, D_ATT:] = dso
        hi = lax.Precision.HIGHEST
        prod_f = dfo.astype(F32) * fo_ref[...].astype(F32)
        prod_s = dso.astype(F32) * so_ref[...].astype(F32)
        drow_ref[...] = lax.dot_general(er_ref[...], prod_f, NT, precision=hi, preferred_element_type=F32)
        dswa_ref[...] = lax.dot_general(er_ref[...], prod_s, NT, precision=hi, preferred_element_type=F32)

    return pl.pallas_call(
        body, name="attn_out_bwd", grid=(S // tm,),
        in_specs=[_rows(tm, D_MODEL), _rows(tm, D_MODEL), _rows(tm, D_ATT), _rows(tm, D_ATT), _resident(),
                  _resident(), _const((1, D_MODEL)), _resident()],
        out_specs=[_rows(tm, D_MODEL), _rows(tm, D_MODEL), pl.BlockSpec((N_HEADS, tm), lambda i: (0, i)),
                   pl.BlockSpec((N_HEADS, tm), lambda i: (0, i)), _const((1, D_MODEL))],
        out_shape=[jax.ShapeDtypeStruct((S, D_MODEL), MM), jax.ShapeDtypeStruct((S, D_MODEL), MM),
                   jax.ShapeDtypeStruct((N_HEADS, S), F32), jax.ShapeDtypeStruct((N_HEADS, S), F32),
                   jax.ShapeDtypeStruct((1, D_MODEL), F32)],
        compiler_params=_params("arbitrary"),
    )(dh1, mix, fox_o, swa_o, wout_fox, wout_swa, g2, head_rows)


def _fox_bwd(fqkv, dcat, lse_row3, d_row3, c_col, tq, tk, pairs_per_loop=2):
    S = fqkv.shape[0]
    n_blk = S // tk
    n_qblk = S // tq
    n_band = tk // tq

    def body(q_ref, k_ref, v_ref, do_ref, lse_ref, dd_ref, ck_ref, dq_ref, dk_ref, dv_ref, dc_ref, dcq_ref):
        kb = pl.program_id(0)

        @pl.when(kb == 0)
        def _():
            dq_ref[...] = jnp.zeros_like(dq_ref)
            dcq_ref[...] = jnp.zeros_like(dcq_ref)

        key = lax.broadcasted_iota(jnp.int32, (tk, tq), 0)
        qry = lax.broadcasted_iota(jnp.int32, (tk, tq), 1)
        low = _head_select((tk, 128), 0)
        for first in range(0, N_HEADS // 2, pairs_per_loop):
            pairs = range(first, first + pairs_per_loop)
            heads = [(pr, hh) for pr in pairs for hh in range(2)]
            kh, vh, ck = {}, {}, {}
            for pr in pairs:
                k2 = k_ref[:, pr * 128:(pr + 1) * 128]
                v2 = v_ref[:, pr * 128:(pr + 1) * 128]
                zero = jnp.zeros_like(k2)
                kh[pr, 0], kh[pr, 1] = jnp.where(low, k2, zero), jnp.where(low, zero, k2)
                vh[pr, 0], vh[pr, 1] = jnp.where(low, v2, zero), jnp.where(low, zero, v2)
                for hh in range(2):
                    ck[pr, hh] = ck_ref[:, 2 * pr + hh:2 * pr + hh + 1]

            def block(qb, carry, band, pairs=pairs, kh=kh, vh=vh, ck=ck):
                rows = pl.ds(pl.multiple_of(qb * tq, tq), tq)
                out = []
                it = iter(carry)
                for pr in pairs:
                    lanes = slice(pr * 128, (pr + 1) * 128)
                    q2 = q_ref[rows, lanes]
                    do2 = do_ref[rows, lanes]
                    dq = None
                    for hh in range(2):
                        h = 2 * pr + hh
                        dk, dv, dc = next(it)
                        s_t = _dot_nt(kh[pr, hh], q2) - ck[pr, hh]
                        p_t = jnp.exp(s_t - lse_ref[h, pl.ds(qb, 1), :])
                        if band is not None:
                            p_t = jnp.where(qry + band * tq >= key, p_t, 0.0)
                        ds_t = p_t * (_dot_nt(vh[pr, hh], do2) - dd_ref[h, pl.ds(qb, 1), :])
                        dsb = ds_t.astype(MM)
                        dv = dv + _dot(p_t.astype(MM), do2)
                        dk = dk + _dot(dsb, q2)
                        dc = dc - jnp.sum(ds_t, axis=1, keepdims=True)
                        part = _dot_tn(dsb, kh[pr, hh])
                        dq = part if dq is None else dq + part
                        dcq_ref[h, pl.ds(qb, 1), :] += jnp.sum(ds_t, axis=0, keepdims=True)
                        out.append((dk, dv, dc))
                    dq_ref[rows, lanes] += dq
                return tuple(out)

            carry = tuple((jnp.zeros((tk, 128), F32), jnp.zeros((tk, 128), F32), jnp.zeros((tk, 1), F32))
                          for _ in heads)
            for band in range(n_band):
                carry = block(kb * n_band + band, carry, band=band)
            carry = lax.fori_loop((kb + 1) * n_band, n_qblk, functools.partial(block, band=None), carry)
            grads = dict(zip(heads, carry))
            for pr in pairs:
                lanes = slice(pr * 128, (pr + 1) * 128)
                dk_ref[:, lanes] = jnp.where(low, grads[pr, 0][0], grads[pr, 1][0]).astype(MM)
                dv_ref[:, lanes] = jnp.where(low, grads[pr, 0][1], grads[pr, 1][1]).astype(MM)
                for hh in range(2):
                    dc_ref[:, 2 * pr + hh:2 * pr + hh + 1] = grads[pr, hh][2]

        @pl.when(kb == n_blk - 1)
        def _():
            dq_ref[...] = dq_ref[...] * Q_SCALE

    return pl.pallas_call(
        body, name="fox_bwd", grid=(n_blk,),
        in_specs=[pl.BlockSpec((S, D_ATT), lambda i: (0, 0)), pl.BlockSpec((tk, D_ATT), lambda i: (i, 1)),
                  pl.BlockSpec((tk, D_ATT), lambda i: (i, 2)), pl.BlockSpec((S, D_ATT), lambda i: (0, 0)),
                  _resident(), _resident(), _rows(tk, N_HEADS)],
        out_specs=[_const((S, D_ATT)), _rows(tk, D_ATT), _rows(tk, D_ATT), _rows(tk, N_HEADS),
                   _const((N_HEADS, n_qblk, tq))],
        out_shape=[jax.ShapeDtypeStruct((S, D_ATT), F32), jax.ShapeDtypeStruct((S, D_ATT), MM),
                   jax.ShapeDtypeStruct((S, D_ATT), MM), jax.ShapeDtypeStruct((S, N_HEADS), F32),
                   jax.ShapeDtypeStruct((N_HEADS, n_qblk, tq), F32)],
        compiler_params=_params("arbitrary"),
    )(fqkv, fqkv, fqkv, dcat, lse_row3, d_row3, c_col)


def _swa_bwd(sqkv, dcat, biasm, sinks_slot, bucket, lse, d_col):
    S = sqkv.shape[0]
    n_blk = S // WINDOW

    def body(q_ref, kp_ref, kc_ref, vp_ref, vc_ref, do_ref, bias_ref, sink_ref, bk_ref, lse_ref, dd_ref,
             dq_ref, dk_ref, dv_ref, drb_ref, dsink_ref, ds_acc):
        n = pl.program_id(0)

        @pl.when(n == 0)
        def _():
            dk_ref[...] = jnp.zeros_like(dk_ref)
            dv_ref[...] = jnp.zeros_like(dv_ref)
            ds_acc[...] = jnp.zeros_like(ds_acc)
            dsink_ref[...] = jnp.zeros_like(dsink_ref)

        no_prev = jnp.where(n > 0, 0.0, NEG)
        prev = pl.ds(pl.multiple_of(jnp.maximum(n - 1, 0) * WINDOW, WINDOW), WINDOW)
        cur = pl.ds(pl.multiple_of(n * WINDOW, WINDOW), WINDOW)
        lane8 = lax.broadcasted_iota(jnp.int32, (1, N_HEADS), 1)
        dkp = jnp.zeros((WINDOW, D_KV), F32)
        dkc = jnp.zeros((WINDOW, D_KV), F32)
        dvp = jnp.zeros((WINDOW, D_KV), F32)
        dvc = jnp.zeros((WINDOW, D_KV), F32)
        dsink = jnp.zeros((1, N_HEADS), F32)
        low = _head_select((WINDOW, 128), 0)
        zero = jnp.zeros((WINDOW, 128), MM)
        dqs = []
        for g in range(2):
            sel = low if g == 0 else jnp.logical_not(low)
            qg = _stack4(lambda j: jnp.where(sel, q_ref[:, j * 128:(j + 1) * 128], zero))
            dog = _stack4(lambda j: jnp.where(sel, do_ref[:, j * 128:(j + 1) * 128], zero))
            lse_g = _stack4(lambda j: lse_ref[:, 2 * j + g:2 * j + g + 1])
            dd = _stack4(lambda j: dd_ref[:, 2 * j + g:2 * j + g + 1])
            sink = _stack4(lambda j: jnp.full((WINDOW, 1), sink_ref[2 * j + g], F32))
            pp = jnp.exp(_dot_nt(qg, kp_ref[...]) + _stack4(lambda j: bias_ref[2 * j + g, :, :WINDOW]) + no_prev - lse_g)
            pc = jnp.exp(_dot_nt(qg, kc_ref[...]) + _stack4(lambda j: bias_ref[2 * j + g, :, WINDOW:]) - lse_g)
            sink_term = jnp.exp(sink - lse_g) * dd
            dsp = pp * (_dot_nt(dog, vp_ref[...]) - dd)
            dsc = pc * (_dot_nt(dog, vc_ref[...]) - dd)
            for j in range(4):
                rows = slice(j * WINDOW, (j + 1) * WINDOW)
                dsink = dsink + jnp.where(lane8 == 2 * j + g, -jnp.sum(sink_term[rows]), 0.0)
                ds_acc[2 * j + g, :, :WINDOW] += dsp[rows]
                ds_acc[2 * j + g, :, WINDOW:] += dsc[rows]
            dspb, dscb = dsp.astype(MM), dsc.astype(MM)
            dqs.append(_dot(dspb, kp_ref[...]) + _dot(dscb, kc_ref[...]))
            dkp = dkp + _dot_tn(dspb, qg)
            dkc = dkc + _dot_tn(dscb, qg)
            dvp = dvp + _dot_tn(pp.astype(MM), dog)
            dvc = dvc + _dot_tn(pc.astype(MM), dog)
        for j in range(4):
            rows = slice(j * WINDOW, (j + 1) * WINDOW)
            dq_ref[:, j * 128:(j + 1) * 128] = (jnp.where(low, dqs[0][rows], dqs[1][rows]) * Q_SCALE).astype(MM)
        dk_ref[prev, :] += dkp
        dk_ref[cur, :] += dkc
        dv_ref[prev, :] += dvp
        dv_ref[cur, :] += dvc
        dsink_ref[...] += dsink

        @pl.when(n == n_blk - 1)
        def _():
            bk = bk_ref[...]
            rb = lax.broadcasted_iota(jnp.int32, (N_BUCKETS, N_HEADS), 0)
            cb = lax.broadcasted_iota(jnp.int32, (N_BUCKETS, N_HEADS), 1)
            out = jnp.zeros((N_BUCKETS, N_HEADS), F32)
            for s in range(N_HEADS):
                acc = ds_acc[s]
                for b in range(N_BUCKETS):
                    out = out + jnp.where((rb == b) & (cb == s), jnp.sum(jnp.where(bk == b, acc, 0.0)), 0.0)
            drb_ref[...] = out

    do_spec = pl.BlockSpec((WINDOW, D_ATT), lambda n: (n, 1))
    return pl.pallas_call(
        body, name="swa_bwd", grid=(n_blk,),
        in_specs=_swa_specs(S) + [do_spec, _resident(), pl.BlockSpec(memory_space=pltpu.SMEM), _resident(),
                                  _rows(WINDOW, N_HEADS), _rows(WINDOW, N_HEADS)],
        out_specs=[_rows(WINDOW, D_ATT), _const((S, D_KV)), _const((S, D_KV)), _const((N_BUCKETS, N_HEADS)),
                   _const((1, N_HEADS))],
        out_shape=[jax.ShapeDtypeStruct((S, D_ATT), MM), jax.ShapeDtypeStruct((S, D_KV), F32),
                   jax.ShapeDtypeStruct((S, D_KV), F32), jax.ShapeDtypeStruct((N_BUCKETS, N_HEADS), F32),
                   jax.ShapeDtypeStruct((1, N_HEADS), F32)],
        scratch_shapes=[pltpu.VMEM((N_HEADS, WINDOW, 2 * WINDOW), F32)],
        compiler_params=_params("arbitrary"),
    )(sqkv, sqkv, sqkv, sqkv, sqkv, dcat, biasm, sinks_slot, bucket, lse, d_col)


D_Z = 3 * D_ATT + D_ATT + 2 * D_KV


def _pack_dz(dq_fox, dk_fox, dv_fox, dsq, dsk, dsv, tm):
    S = dq_fox.shape[0]

    def body(dq_ref, dk_ref, dv_ref, dsq_ref, dsk_ref, dsv_ref, dz_ref):
        dz_ref[:, 0:512] = dq_ref[...].astype(MM)
        dz_ref[:, 512:1024] = dk_ref[...]
        dz_ref[:, 1024:1536] = dv_ref[...]
        dz_ref[:, 1536:2048] = dsq_ref[...]
        dz_ref[:, 2048:2176] = dsk_ref[...].astype(MM)
        dz_ref[:, 2176:2304] = dsv_ref[...].astype(MM)

    return pl.pallas_call(
        body, name="pack_dz", grid=(S // tm,),
        in_specs=[_rows(tm, D_ATT), _rows(tm, D_ATT), _rows(tm, D_ATT), _rows(tm, D_ATT), _rows(tm, D_KV),
                  _rows(tm, D_KV)],
        out_specs=_rows(tm, D_Z), out_shape=jax.ShapeDtypeStruct((S, D_Z), MM),
        compiler_params=_params("parallel"),
    )(dq_fox, dk_fox, dv_fox, dsq, dsk, dsv)


def _pre_attn_bwd(x, dh1, dz, dff_t, wt, g1, tm):
    S = x.shape[0]

    def body(x_ref, dh1_ref, dz_ref, dff_ref, wt_ref, g1_ref, dx_ref, dg1_ref):
        i = pl.program_id(0)
        da = (_dot(dz_ref[:, 0:WT_FF], wt_ref[0:WT_FF]) + _dot(dz_ref[:, WT_FF:D_Z], wt_ref[WT_SQ:WT_ROWS])
              + _dot_tn(dff_ref[...].astype(MM), wt_ref[WT_FF:WT_SQ]))
        n1, r1 = _rms(x_ref[...])
        dx, dg1 = _rms_bwd(da, n1, r1, g1_ref[...])
        _accumulate(dg1_ref, dg1, i)
        dx_ref[...] = dh1_ref[...] + dx

    return pl.pallas_call(
        body, name="pre_attn_bwd", grid=(S // tm,),
        in_specs=[_rows(tm, D_MODEL), _rows(tm, D_MODEL), _rows(tm, D_Z), pl.BlockSpec((16, tm), lambda i: (0, i)),
                  _resident(), _const((1, D_MODEL))],
        out_specs=[_rows(tm, D_MODEL), _const((1, D_MODEL))],
        out_shape=[jax.ShapeDtypeStruct((S, D_MODEL), F32), jax.ShapeDtypeStruct((1, D_MODEL), F32)],
        compiler_params=_params("arbitrary"),
    )(x, dh1, dz, dff_t, wt, g1)


def _weight_grad(a, b, name, tk, n_chunks=1, relu2=False):
    S, K = a.shape
    N = b.shape[1]
    cn = N // n_chunks

    def body(a_ref, b_ref, out_ref):
        av = a_ref[...]
        if relu2:
            av = jnp.square(jnp.maximum(av.astype(F32), 0.0))
        av = av.astype(MM)
        for j in range(n_chunks):
            val = _dot_tn(av, b_ref[:, j * cn:(j + 1) * cn].astype(MM)).astype(MM)
            if n_chunks > 1:
                out_ref[j] = val
            else:
                out_ref[...] = val

    if n_chunks > 1:
        out_spec = pl.BlockSpec((n_chunks, tk, cn), lambda i: (0, i, 0))
        out_shape = jax.ShapeDtypeStruct((n_chunks, K, cn), MM)
    else:
        out_spec = pl.BlockSpec((tk, N), lambda i: (i, 0))
        out_shape = jax.ShapeDtypeStruct((K, N), MM)
    return pl.pallas_call(
        body, name=name, grid=(K // tk,),
        in_specs=[pl.BlockSpec((S, tk), lambda i: (0, i)), _resident()],
        out_specs=out_spec, out_shape=out_shape, compiler_params=_params("parallel"),
    )(a, b)


def _forget_weight_grad(dff_t, a):
    def body(d_ref, a_ref, out_ref):
        out_ref[...] = _dot(d_ref[...].astype(MM), a_ref[...])

    return pl.pallas_call(
        body, name="forget_weight_grad", out_shape=jax.ShapeDtypeStruct((16, D_MODEL), F32),
        in_specs=[_resident(), _resident()], out_specs=_resident(),
    )(dff_t, a)


def _place():
    return lax.axis_index("x"), lax.axis_index("y"), lax.axis_index("c")


def _all_gather_sequencer(stack):
    ref = jax.new_ref(stack, memory_space=pltpu.MemorySpace.HBM)

    @pl.kernel(mesh=plsc.ScalarSubcoreMesh(axis_name="sequencer", num_cores=1), name="all_gather_sequencer",
               scratch_types=(pltpu.SemaphoreType.DMA((7,)), pltpu.SemaphoreType.DMA((7,))),
               compiler_params=pltpu.CompilerParams(collective_id=1))
    def launch(send_sems, recv_sems):
        x, y, c = _place()
        sibling = (x, y, 1 - c)
        chips = [(1 - x, y), (x, 1 - y), (1 - x, 1 - y)]
        peers = [sibling] + [(px, py, c) for px, py in chips]
        barrier = pltpu.get_barrier_semaphore()
        for peer in peers:
            pl.semaphore_signal(barrier, inc=1, device_id=peer, device_id_type=MESH)
        pl.semaphore_wait(barrier, len(peers))

        def copy(k, block, to):
            px, py, pc = block
            slot = ref.at[4 * px + 2 * py + pc]
            return _remote(slot, slot, send_sems, recv_sems, k, to)

        first = [copy(k, (x, y, c), peer) for k, peer in enumerate(peers)]
        for cp in first:
            cp.start()
        passed = []
        for j, (px, py) in enumerate(chips):
            copy(1 + j, (px, py, c), sibling).wait_recv()
            passed.append(copy(4 + j, (px, py, c), sibling))
            passed[-1].start()
        copy(0, (x, y, 1 - c), sibling).wait_recv()
        for j, (px, py) in enumerate(chips):
            copy(4 + j, (px, py, 1 - c), sibling).wait_recv()
        for cp in first + passed:
            cp.wait_send()

    launch()
    return ref[...]


def _chip_sum(grad, other, name):
    _, _, r, cdim = grad.shape
    tr = 512 if r % 512 == 0 else r

    def body(c_ref, g_ref, o_ref, out_ref):
        out_ref[...] = (g_ref[...].astype(F32) + o_ref[...].astype(F32)).astype(out_ref.dtype)

    return pl.pallas_call(
        body, name=name,
        grid_spec=pltpu.PrefetchScalarGridSpec(
            num_scalar_prefetch=1, grid=(4, r // tr),
            in_specs=[pl.BlockSpec((None, None, tr, cdim), lambda k, i, c_ref: (k, c_ref[0], i, 0)),
                      pl.BlockSpec((None, tr, cdim), lambda k, i, c_ref: (k, i, 0))],
            out_specs=pl.BlockSpec((None, tr, cdim), lambda k, i, c_ref: (k, i, 0))),
        out_shape=jax.ShapeDtypeStruct((4, r, cdim), MM),
        compiler_params=_params("parallel", "parallel"),
    )(lax.axis_index("c").astype(jnp.int32).reshape(1), grad, other)


HBM_SPEC = pl.BlockSpec(memory_space=pltpu.HBM)
SEM_SPEC = pl.BlockSpec(memory_space=pltpu.SEMAPHORE)
DATAFLOW = pltpu.SideEffectType.DATAFLOW_SIDE_EFFECTING


def _exchange_start(name, arrays, n_copies, plan):
    n = len(arrays)

    def body(*refs):
        send_sems, recv_sems, token = refs[n], refs[n + 1], refs[2 * n + 2]
        for cp in plan(refs[:n], send_sems, recv_sems):
            cp.start()
        token[...] = jnp.zeros_like(token)

    out = pl.pallas_call(
        body, name=name,
        out_shape=(pltpu.SemaphoreType.DMA((n_copies,)), pltpu.SemaphoreType.DMA((n_copies,)),
                   *[pltpu.HBM(a.shape, a.dtype) for a in arrays], jax.ShapeDtypeStruct((1, D_MODEL), F32)),
        in_specs=[HBM_SPEC] * n,
        out_specs=(SEM_SPEC, SEM_SPEC, *[HBM_SPEC] * n, pl.BlockSpec(memory_space=pltpu.VMEM)),
        input_output_aliases={i: 2 + i for i in range(n)},
        compiler_params=pltpu.CompilerParams(has_side_effects=DATAFLOW),
    )(*[pltpu.with_memory_space_constraint(a, pltpu.HBM) for a in arrays])
    return (out[0], out[1]), list(out[2:2 + n]), out[2 + n]


def _exchange_wait(name, arrays, sems, after, plan):
    n = len(arrays)
    after = list(after) if isinstance(after, (list, tuple)) else [after]

    def body(*refs):
        send_sems, recv_sems = refs[n], refs[n + 1]
        for cp in plan(refs[:n], send_sems, recv_sems):
            cp.wait_send()
            cp.wait_recv()

    out = pl.pallas_call(
        body, name=name, out_shape=[pltpu.HBM(a.shape, a.dtype) for a in arrays],
        in_specs=[HBM_SPEC] * n + [SEM_SPEC, SEM_SPEC] + [pl.BlockSpec(memory_space=pl.ANY)] * len(after),
        out_specs=[HBM_SPEC] * n, input_output_aliases={i: i for i in range(n)},
        compiler_params=pltpu.CompilerParams(has_side_effects=DATAFLOW),
    )(*arrays, sems[0], sems[1], *after)
    return list(out)


def _remote(src, dst, send_sems, recv_sems, k, to):
    return pltpu.make_async_remote_copy(src_ref=src, dst_ref=dst, send_sem=send_sems.at[k], recv_sem=recv_sems.at[k],
                                        device_id=to, device_id_type=MESH)


def _plan_gather_direct(refs, send_sems, recv_sems):
    x, y, c = _place()
    me = 4 * x + 2 * y + c
    peers = [(x, y, 1 - c), (1 - x, y, c), (x, 1 - y, c), (1 - x, 1 - y, c)]
    return [_remote(ref.at[me], ref.at[me], send_sems, recv_sems, 4 * a + k, peer)
            for a, ref in enumerate(refs) for k, peer in enumerate(peers)]


def _plan_gather_pass_on(refs, send_sems, recv_sems):
    x, y, c = _place()
    chips = [(1 - x, y), (x, 1 - y), (1 - x, 1 - y)]
    return [_remote(ref.at[4 * px + 2 * py + c], ref.at[4 * px + 2 * py + c], send_sems, recv_sems, 3 * a + k,
                    (x, y, 1 - c))
            for a, ref in enumerate(refs) for k, (px, py) in enumerate(chips)]


def _plan_in_chip(refs, send_sems, recv_sems):
    n = len(refs) // 2
    x, y, c = _place()
    return [_remote(refs[a].at[:, 1 - c], refs[n + a], send_sems, recv_sems, a, (x, y, 1 - c)) for a in range(n)]


def _plan_between_chips(refs, send_sems, recv_sems):
    n = len(refs) // 2
    x, y, c = _place()
    chips = [(1 - x, y), (x, 1 - y), (1 - x, 1 - y)]
    return [_remote(refs[a].at[2 * px + py], refs[n + a].at[2 * x + y], send_sems, recv_sems, 3 * a + k, (px, py, c))
            for a in range(n) for k, (px, py) in enumerate(chips)]


def _plan_late_between(refs, send_sems, recv_sems):
    sums, land, small = refs
    x, y, c = _place()
    me = 4 * x + 2 * y + c
    copies = _plan_between_chips([sums, land], send_sems, recv_sems)
    peers = [(x ^ dx, y ^ dy, c ^ dc) for dx in range(2) for dy in range(2) for dc in range(2) if dx + dy + dc]
    return copies + [_remote(small.at[me], small.at[me], send_sems, recv_sems, 3 + k, peer)
                     for k, peer in enumerate(peers)]


def _adamw_math(w, g, m, v):
    m = ADAM_B1 * m + (1.0 - ADAM_B1) * g
    v = ADAM_B2 * v + (1.0 - ADAM_B2) * jnp.square(g)
    m_hat = m / (1.0 - ADAM_B1 ** ADAM_STEP)
    v_hat = v / (1.0 - ADAM_B2 ** ADAM_STEP)
    delta = -ADAM_LR * (m_hat / (jnp.sqrt(v_hat) + ADAM_EPS) + ADAM_WD * w)
    return delta, m, v


def _adamw(parts, w, m, v, name):
    n_parts, r, cdim = parts.shape
    tr = 256 if r % 256 == 0 else r

    def body(p_ref, w_ref, m_ref, v_ref, g_out, d_out, m_out, v_out):
        g = p_ref[0].astype(F32)
        for k in range(1, n_parts):
            g = g + p_ref[k].astype(F32)
        delta, m_new, v_new = _adamw_math(w_ref[...], g, m_ref[...], v_ref[...])
        g_out[...] = g
        d_out[...] = delta
        m_out[...] = m_new
        v_out[...] = v_new

    blk = pl.BlockSpec((tr, cdim), lambda i: (i, 0))
    return pl.pallas_call(
        body, name=name, grid=(r // tr,),
        in_specs=[pl.BlockSpec((n_parts, tr, cdim), lambda i: (0, i, 0)), blk, blk, blk],
        out_specs=[blk] * 4, out_shape=[jax.ShapeDtypeStruct((r, cdim), F32)] * 4,
        compiler_params=_params("parallel"),
    )(parts, w, m, v)


def _adamw_chips(parts, sums, w, m, v, name):
    _, r, cdim = parts.shape
    tr = 256 if r % 256 == 0 else r

    def body(chip_ref, p_ref, own_ref, w_ref, m_ref, v_ref, g_out, d_out, m_out, v_out):
        g = None
        for k in range(4):
            term = jnp.where(chip_ref[0] == k, own_ref[...], p_ref[k]).astype(F32)
            g = term if g is None else g + term
        delta, m_new, v_new = _adamw_math(w_ref[...], g, m_ref[...], v_ref[...])
        g_out[...] = g
        d_out[...] = delta
        m_out[...] = m_new
        v_out[...] = v_new

    blk = pl.BlockSpec((tr, cdim), lambda i, chip: (i, 0))
    my_chip = (2 * lax.axis_index("x") + lax.axis_index("y")).astype(jnp.int32).reshape(1)
    return pl.pallas_call(
        body, name=name,
        grid_spec=pltpu.PrefetchScalarGridSpec(
            num_scalar_prefetch=1, grid=(r // tr,),
            in_specs=[pl.BlockSpec((4, tr, cdim), lambda i, chip: (0, i, 0)),
                      pl.BlockSpec((None, tr, cdim), lambda i, chip: (chip[0], i, 0)), blk, blk, blk],
            out_specs=[blk] * 4),
        out_shape=[jax.ShapeDtypeStruct((r, cdim), F32)] * 4,
        compiler_params=_params("parallel"),
    )(my_chip, parts, sums, w, m, v)


class _NoExchange:
    def __init__(self, weights):
        self.weights = weights

    def before_pre_attn(self, g1):
        return g1

    def after_fox_fwd(self, fox_o, sinks_slot):
        return sinks_slot

    def after_attention(self, swa_o):
        return self.weights

    def after_early_grads(self, grads, d_col):
        return d_col

    def after_swa_bwd(self, dsq, d_row3):
        return d_row3

    def after_w_in_grad(self, d_win, g1):
        return g1


def _slot_order(t, axis):
    shp = t.shape
    t = t.reshape(shp[:axis] + (2, 4, shp[axis] // N_HEADS) + shp[axis + 1:])
    return jnp.swapaxes(t, axis, axis + 1).reshape(shp)


def _head_order(t, axis):
    shp = t.shape
    t = t.reshape(shp[:axis] + (4, 2, shp[axis] // N_HEADS) + shp[axis + 1:])
    return jnp.swapaxes(t, axis, axis + 1).reshape(shp)


def _forward_backward(x, p, target, win_t, hooks, b_forget, rel_bias, sinks, g1, g2, g3, g4, g5):
    S = x.shape[0]
    tm = 256
    tm_mlp = 512
    t = 256
    q0 = 3 * D_ATT + N_HEADS
    win_t = win_t.reshape(D_IN, D_MODEL)
    wt = jnp.concatenate(
        [win_t[:q0], jnp.zeros((8, D_MODEL), MM), _slot_order(win_t[q0:q0 + D_ATT], 0), win_t[q0 + D_ATT:]], axis=0)
    bcol = jnp.pad(b_forget.reshape(N_HEADS, 1), ((0, 8), (0, 0)))
    rel_bias_slot = rel_bias[:, np.array(SLOT_HEAD)]
    sinks_slot = sinks.reshape(N_HEADS)[np.array(SLOT_HEAD)]
    bucket = jnp.asarray(_swa_bucket_map())

    a, fqkv, sqkv, fft = _pre_attn(x, hooks.before_pre_attn(g1), wt, tm)
    c_row = _forget_cumsum(fft, bcol)
    c_col = c_row[:N_HEADS].T
    c_row3 = c_row[:N_HEADS].reshape(N_HEADS, S // t, t)
    fox_o, fox_lse = _fox_fwd(fqkv, c_row3, tq=512, tk=t)
    biasm = _swa_bias(rel_bias_slot, bucket)
    sinks_slot = hooks.after_fox_fwd(fox_o, sinks_slot)
    swa_o, swa_lse = _swa_fwd(sqkv, biasm, sinks_slot)
    wout, w1, w2, wple, wg = hooks.after_attention(swa_o)
    wout_fox = wout[:D_ATT]
    wout_swa = _slot_order(wout[D_ATT:], 0)
    mix, h1, m = _post_attn(x, fox_o, swa_o, wout_fox, wout_swa, g2, g3, tm)
    u, y, h2 = _mlp_fwd(m, h1, w1, w2, g4, tm_mlp)
    dh2, dpe, dgl, dg5, loss = _ple_loss(h2, p, target, wg, wple, g5, tm)

    d_wple = _weight_grad(p, dpe, "grad_w_ple", tk=D_PLE, n_chunks=N_DEV)
    d_wg = _weight_grad(h2, dgl, "grad_w_ple_gate", tk=256)
    dh1, dy, du, dg4, dg3 = _mlp_bwd(dh2, y, h1, u, w1, w2, g4, g3, tm)
    d_w2 = _weight_grad(u, dy, "grad_w_ff2", tk=256, relu2=True)
    d_w1 = _weight_grad(m, du, "grad_w_ff1", tk=256, n_chunks=N_DEV)
    head = np.arange(D_ATT) // HEAD_DIM
    head_rows = jnp.asarray((head[None, :] == np.arange(N_HEADS)[:, None]).astype(np.float32))
    dmix, dcat, d_row, d_swa, dg2 = _attn_out_bwd(dh1, mix, fox_o, swa_o, wout_fox, wout_swa, g2, head_rows, tm)
    d_col = d_swa.T
    d_wout_fox = _weight_grad(fox_o, dmix, "grad_w_out_fox", tk=256)
    d_wout_swa = _weight_grad(swa_o, dmix, "grad_w_out_swa", tk=256)
    d_wout = jnp.concatenate([d_wout_fox, _head_order(d_wout_swa, 0)], axis=0).reshape(N_DEV, D_MODEL // N_DEV, D_MODEL)
    early = dict(w_ff1=d_w1, w_ff2=d_w2.reshape(N_DEV, FF_CHUNK, D_MODEL), w_ple=d_wple,
                 w_ple_gate=d_wg.reshape(N_DEV, D_MODEL // N_DEV, D_MODEL), w_out=d_wout)

    d_col = hooks.after_early_grads(early, d_col)
    dsq, dsk, dsv, d_rb_slot, d_sink_slot = _swa_bwd(sqkv, dcat, biasm, sinks_slot, bucket, swa_lse, d_col)
    lse_row3 = fox_lse.T.reshape(N_HEADS, S // t, t)
    d_row3 = hooks.after_swa_bwd(dsq, d_row.reshape(N_HEADS, S // t, t))
    dq_fox, dk_fox, dv_fox, dc_col, dcq = _fox_bwd(fqkv, dcat, lse_row3, d_row3, c_col, tq=t, tk=512)
    dc_row = jnp.pad(dc_col.T + dcq.reshape(N_HEADS, S), ((0, 8), (0, 0)))
    dff_t, db = _forget_bwd(dc_row, fft, bcol)
    dz = _pack_dz(dq_fox, dk_fox, dv_fox, dsq, dsk, dsv, 512)
    d_wmain = _weight_grad(dz, a, "grad_w_in", tk=256)
    d_wff_t = _forget_weight_grad(dff_t, a)

    sq0 = 3 * D_ATT
    d_win = jnp.concatenate(
        [d_wmain[:sq0], d_wff_t[:N_HEADS].astype(MM), _head_order(d_wmain[sq0:sq0 + D_ATT], 0),
         d_wmain[sq0 + D_ATT:]], axis=0)
    d_win = d_win.reshape(N_DEV, D_IN // N_DEV, D_MODEL)
    grad_x, dg1 = _pre_attn_bwd(x, dh1, dz, dff_t, wt, hooks.after_w_in_grad(d_win, g1), tm)
    big = dict(early, w_in=d_win)
    small = dict(b_forget=db[:N_HEADS].reshape(1, N_HEADS), rel_bias=d_rb_slot[:, np.array(HEAD_SLOT)],
                 swa_sinks=d_sink_slot[:, np.array(HEAD_SLOT)], g_attn_pre=dg1, g_attn_post=dg2, g_ff_pre=dg3,
                 g_ff_post=dg4, g_ple_post=dg5)
    return loss, grad_x, big, small


BIG = ("w_in", "w_out", "w_ff1", "w_ff2", "w_ple", "w_ple_gate")
SMALL_ROWS = ("g_attn_pre", "g_attn_post", "g_ff_pre", "g_ff_post", "g_ple_post")
WEIGHTS = ("w_in", "b_forget", "w_out", "rel_bias", "swa_sinks", "g_attn_pre", "g_attn_post", "w_ff1", "w_ff2",
           "g_ff_pre", "g_ff_post", "w_ple", "w_ple_gate", "g_ple_post")


EARLY = ("w_ff1", "w_ff2", "w_ple", "w_ple_gate", "w_out")


class _Overlap:
    def __init__(self, later):
        self.later = later

    def before_pre_attn(self, g1):
        self.gather_sems, self.later, token = _exchange_start("gather_rest_start", self.later, 4 * 5, _plan_gather_direct)
        return g1 + token

    def after_fox_fwd(self, fox_o, sinks_slot):
        later = _exchange_wait("gather_rest_wait", self.later, self.gather_sems, fox_o, _plan_gather_direct)
        self.pass_sems, self.later, token = _exchange_start("gather_pass_on_start", later, 3 * 5, _plan_gather_pass_on)
        return sinks_slot + token[0, :N_HEADS]

    def after_attention(self, swa_o):
        wout_g, w1_g, w2_g, wple_g, wg_g = _exchange_wait("gather_pass_on_wait", self.later, self.pass_sems, swa_o,
                                                         _plan_gather_pass_on)
        return (wout_g.reshape(D_MODEL, D_MODEL), w1_g, w2_g.reshape(D_FF, D_MODEL),
                jnp.moveaxis(wple_g, 0, 1).reshape(D_PLE, D_MODEL), wg_g.reshape(D_MODEL, D_MODEL))

    def after_early_grads(self, grads, d_col):
        views = [grads[k].reshape((4, 2) + grads[k].shape[1:]) for k in EARLY]
        lands = [lax.empty((4,) + grads[k].shape[1:], MM) for k in EARLY]
        self.in_chip_sems, self.in_chip, token = _exchange_start("grads_in_chip_start", views + lands, len(EARLY),
                                                                 _plan_in_chip)
        return d_col + token[0, 0]

    def after_swa_bwd(self, dsq, d_row3):
        arrays = _exchange_wait("grads_in_chip_wait", self.in_chip, self.in_chip_sems, dsq, _plan_in_chip)
        n = len(EARLY)
        sums = [_chip_sum(arrays[a], arrays[n + a], "chip_sum_" + k) for a, k in enumerate(EARLY)]
        lands = [lax.empty(s.shape, s.dtype) for s in sums]
        self.between_sems, self.between, token = _exchange_start("grads_between_chips_start", sums + lands, 3 * n,
                                                                 _plan_between_chips)
        return d_row3 + token[0, 0]

    def after_w_in_grad(self, d_win, g1):
        self.late_in_chip_sems, self.late_in_chip, token = _exchange_start(
            "late_in_chip_start", [d_win.reshape((4, 2) + d_win.shape[1:]), lax.empty((4,) + d_win.shape[1:], MM)],
            1, _plan_in_chip)
        return g1 + token

    def finish(self, after):
        arrays = _exchange_wait("grads_between_chips_wait", self.between, self.between_sems, after,
                                _plan_between_chips)
        n = len(EARLY)
        self.sums = arrays[:n]
        return arrays[n:]


def _pack_small(t):
    rows = [t[k].reshape(1, D_MODEL) for k in SMALL_ROWS]
    misc = jnp.concatenate([t["b_forget"].reshape(-1), t["swa_sinks"].reshape(-1), t["rel_bias"].reshape(-1)])
    rows.append(jnp.pad(misc, (0, D_MODEL - misc.shape[0])).reshape(1, D_MODEL))
    rows.append(jnp.pad(t["loss"].reshape(-1), (0, D_MODEL - 1)).reshape(1, D_MODEL))
    rows.append(jnp.zeros((1, D_MODEL), F32))
    return jnp.concatenate(rows, axis=0).astype(F32)


def _unpack_small(blk):
    out = {k: blk[i].reshape(1, D_MODEL) for i, k in enumerate(SMALL_ROWS)}
    misc = blk[len(SMALL_ROWS)]
    out["b_forget"] = misc[:N_HEADS].reshape(1, N_HEADS)
    out["swa_sinks"] = misc[N_HEADS:2 * N_HEADS].reshape(1, N_HEADS)
    out["rel_bias"] = misc[2 * N_HEADS:2 * N_HEADS + N_BUCKETS * N_HEADS].reshape(N_BUCKETS, N_HEADS)
    out["loss"] = blk[len(SMALL_ROWS) + 1, 0]
    return out


def kernel(x, p, w_in, b_forget, w_out, rel_bias, swa_sinks, g_attn_pre, g_attn_post, w_ff1, w_ff2, g_ff_pre, g_ff_post, w_ple, w_ple_gate, g_ple_post, loss_target, m_w_in, m_b_forget, m_w_out, m_rel_bias, m_swa_sinks, m_g_attn_pre, m_g_attn_post, m_w_ff1, m_w_ff2, m_g_ff_pre, m_g_ff_post, m_w_ple, m_w_ple_gate, m_g_ple_post, v_w_in, v_b_forget, v_w_out, v_rel_bias, v_swa_sinks, v_g_attn_pre, v_g_attn_post, v_w_ff1, v_w_ff2, v_g_ff_pre, v_g_ff_post, v_w_ple, v_w_ple_gate, v_g_ple_post):
    w = dict(w_in=w_in, b_forget=b_forget, w_out=w_out, rel_bias=rel_bias, swa_sinks=swa_sinks,
             g_attn_pre=g_attn_pre, g_attn_post=g_attn_post, w_ff1=w_ff1, w_ff2=w_ff2, g_ff_pre=g_ff_pre,
             g_ff_post=g_ff_post, w_ple=w_ple, w_ple_gate=w_ple_gate, g_ple_post=g_ple_post)
    mom = dict(w_in=m_w_in, b_forget=m_b_forget, w_out=m_w_out, rel_bias=m_rel_bias, swa_sinks=m_swa_sinks,
               g_attn_pre=m_g_attn_pre, g_attn_post=m_g_attn_post, w_ff1=m_w_ff1, w_ff2=m_w_ff2,
               g_ff_pre=m_g_ff_pre, g_ff_post=m_g_ff_post, w_ple=m_w_ple, w_ple_gate=m_w_ple_gate,
               g_ple_post=m_g_ple_post)
    var = dict(w_in=v_w_in, b_forget=v_b_forget, w_out=v_w_out, rel_bias=v_rel_bias, swa_sinks=v_swa_sinks,
               g_attn_pre=v_g_attn_pre, g_attn_post=v_g_attn_post, w_ff1=v_w_ff1, w_ff2=v_w_ff2,
               g_ff_pre=v_g_ff_pre, g_ff_post=v_g_ff_post, w_ple=v_w_ple, w_ple_gate=v_w_ple_gate,
               g_ple_post=v_g_ple_post)

    turn = lambda t, k: t.T if k == "w_in" else t
    me = 4 * lax.axis_index("x") + 2 * lax.axis_index("y") + lax.axis_index("c")

    def stack(block):
        return lax.dynamic_update_slice_in_dim(lax.empty((N_DEV,) + block.shape, block.dtype), block[None], me, 0)

    stacks = [stack(turn(w[k][0], k).astype(MM)) for k in BIG]
    win_g, later = _all_gather_sequencer(stacks[0]), stacks[1:]
    hooks = _Overlap(later)
    loss, grad_x, big, small = _forward_backward(
        x[0], p[0, 0], loss_target[0], win_g, hooks, b_forget, rel_bias, swa_sinks,
        g_attn_pre, g_attn_post, g_ff_pre, g_ff_post, g_ple_post)
    out_g, out_d, out_m, out_v = {}, {}, {}, {}

    def update(k, part, own):
        g, d, m_new, v_new = _adamw_chips(part, own, turn(w[k][0], k), turn(mom[k][0], k), turn(var[k][0], k),
                                          "adamw_" + k)
        out_g[k], out_d[k], out_m[k], out_v[k] = turn(g, k)[None], turn(d, k)[None], turn(m_new, k)[None], turn(v_new, k)[None]
        return d

    view, other = _exchange_wait("late_in_chip_wait", hooks.late_in_chip, hooks.late_in_chip_sems, grad_x,
                                 _plan_in_chip)
    chip_sum = _chip_sum(view, other, "chip_sum_w_in")
    small["loss"] = loss
    between_sems, between, token = _exchange_start(
        "late_between_chips_start", [chip_sum, lax.empty(chip_sum.shape, MM), stack(_pack_small(small))], 3 + 7,
        _plan_late_between)
    early_parts = hooks.finish(token)
    done = [update(k, part, own) for k, part, own in zip(EARLY, early_parts, hooks.sums)]
    chip_sum, part, small_all = _exchange_wait("late_between_chips_wait", between, between_sems, done,
                                               _plan_late_between)
    update("w_in", part, chip_sum)
    rep ={k: w[k] for k in w if k not in BIG}
    rep["loss"] = jnp.zeros((), F32)
    rep_m = {k: mom[k] for k in mom if k not in BIG}
    rep_m["loss"] = jnp.zeros((), F32)
    rep_v = {k: var[k] for k in var if k not in BIG}
    rep_v["loss"] = jnp.ones((), F32)
    g_s, d_s, m_s, v_s = _adamw(small_all, _pack_small(rep), _pack_small(rep_m), _pack_small(rep_v), "adamw_small")
    g_s, d_s, m_s, v_s = _unpack_small(g_s), _unpack_small(d_s), _unpack_small(m_s), _unpack_small(v_s)
    for k in w:
        if k not in BIG:
            out_g[k], out_d[k], out_m[k], out_v[k] = g_s[k], d_s[k], m_s[k], v_s[k]
    return (g_s["loss"], grad_x[None], *[out_g[k] for k in WEIGHTS], *[out_d[k] for k in WEIGHTS],
            *[out_m[k] for k in WEIGHTS], *[out_v[k] for k in WEIGHTS])
```

```python
import functools

import numpy as np
import jax
import jax.numpy as jnp
from jax import lax
from jax.experimental import pallas as pl
from jax.experimental.pallas import tpu as pltpu
from jax.experimental.pallas import tpu_sc as plsc

F32 = jnp.float32
MM = jnp.bfloat16

D_MODEL = 1024
HEAD_DIM = 64
N_HEADS = 8
D_ATT = N_HEADS * HEAD_DIM
D_KV = 128
D_FF = 4096
D_PLE = 256
D_IN = 3 * D_ATT + N_HEADS + D_ATT + 2 * D_KV
N_DEV = 8
FF_CHUNK = D_FF // N_DEV
WINDOW = 128
N_BUCKETS = 32
MAX_DISTANCE = 128
RMS_EPS = 1e-6
Q_SCALE = HEAD_DIM ** -0.5
NEG = -1e30

ADAM_LR = 0.001
ADAM_B1 = 0.9
ADAM_B2 = 0.999
ADAM_EPS = 1e-08
ADAM_WD = 0.01
ADAM_STEP = 10

SLOT_HEAD = (0, 4, 1, 5, 2, 6, 3, 7)
HEAD_SLOT = (0, 2, 4, 6, 1, 3, 5, 7)

VMEM_LIMIT = 56 * 1024 * 1024
MESH = pl.DeviceIdType.MESH

NT = (((1,), (1,)), ((), ()))
TN = (((0,), (0,)), ((), ()))


def _params(*semantics):
    return pltpu.CompilerParams(dimension_semantics=semantics, vmem_limit_bytes=VMEM_LIMIT)


def _resident():
    return pl.BlockSpec(memory_space=pltpu.VMEM)


def _rows(tm, width):
    return pl.BlockSpec((tm, width), lambda i: (i, 0))


def _const(shape):
    return pl.BlockSpec(shape, lambda i: (0,) * len(shape))


def _dot(a, b):
    return jnp.dot(a, b, preferred_element_type=F32)


def _dot_nt(a, b):
    return lax.dot_general(a, b, NT, preferred_element_type=F32)


def _dot_tn(a, b):
    return lax.dot_general(a, b, TN, preferred_element_type=F32)


def _rms(xf):
    r = lax.rsqrt(jnp.mean(xf * xf, axis=-1, keepdims=True) + RMS_EPS)
    return xf * r, r


def _rms_bwd(dout, n, r, g):
    dg = jnp.sum(dout * n, axis=0, keepdims=True)
    dn = dout * g
    dx = r * (dn - n * jnp.mean(dn * n, axis=-1, keepdims=True))
    return dx, dg


def _accumulate(ref, value, step):
    @pl.when(step == 0)
    def _():
        ref[...] = value

    @pl.when(step != 0)
    def _():
        ref[...] += value


def _t5_bucket(n):
    max_exact = N_BUCKETS // 2
    large = max_exact + (np.log(np.maximum(n, 1) / max_exact) / np.log(MAX_DISTANCE / max_exact)
                         * (N_BUCKETS - max_exact)).astype(np.int32)
    large = np.minimum(large, N_BUCKETS - 1)
    return np.where(n < max_exact, n, large).astype(np.int32)


def _swa_bucket_map():
    i = np.arange(WINDOW)[:, None]
    j = np.arange(2 * WINDOW)[None, :]
    dist = i + WINDOW - j
    ok = (dist >= 0) & (dist < WINDOW)
    return np.where(ok, _t5_bucket(np.clip(dist, 0, None)), -1).astype(np.int32)


WT_FOX = 0
WT_FF = 3 * D_ATT
WT_SQ = WT_FF + 16
WT_SKV = WT_SQ + D_ATT
WT_ROWS = WT_SKV + 2 * D_KV


def _pre_attn(x, g1, wt, tm):
    S = x.shape[0]

    def body(x_ref, g_ref, wt_ref, a_ref, fqkv_ref, sqkv_ref, fft_ref):
        n, _ = _rms(x_ref[...])
        a = (n * g_ref[...]).astype(MM)
        a_ref[...] = a
        fqkv_ref[:, :D_ATT] = (_dot_nt(a, wt_ref[WT_FOX:WT_FOX + D_ATT]) * Q_SCALE).astype(MM)
        fqkv_ref[:, D_ATT:] = _dot_nt(a, wt_ref[WT_FOX + D_ATT:WT_FF]).astype(MM)
        sqkv_ref[:, :D_ATT] = (_dot_nt(a, wt_ref[WT_SQ:WT_SKV]) * Q_SCALE).astype(MM)
        sqkv_ref[:, D_ATT:] = _dot_nt(a, wt_ref[WT_SKV:WT_ROWS]).astype(MM)
        fft_ref[...] = _dot_nt(wt_ref[WT_FF:WT_SQ], a)

    return pl.pallas_call(
        body, name="pre_attn", grid=(S // tm,),
        in_specs=[_rows(tm, D_MODEL), _const((1, D_MODEL)), _resident()],
        out_specs=[_rows(tm, D_MODEL), _rows(tm, 3 * D_ATT), _rows(tm, D_ATT + 2 * D_KV),
                   pl.BlockSpec((16, tm), lambda i: (0, i))],
        out_shape=[jax.ShapeDtypeStruct((S, D_MODEL), MM), jax.ShapeDtypeStruct((S, 3 * D_ATT), MM),
                   jax.ShapeDtypeStruct((S, D_ATT + 2 * D_KV), MM), jax.ShapeDtypeStruct((16, S), F32)],
        compiler_params=_params("parallel"),
    )(x, g1, wt)


def _lane_scan(v, reverse):
    S = v.shape[1]
    lane = lax.broadcasted_iota(jnp.int32, v.shape, 1)
    k = 1
    while k < S:
        if reverse:
            v = v + jnp.where(lane < S - k, pltpu.roll(v, S - k, axis=1), 0.0)
        else:
            v = v + jnp.where(lane >= k, pltpu.roll(v, k, axis=1), 0.0)
        k *= 2
    return v


def _forget_cumsum(fft, bcol):
    def body(f_ref, b_ref, c_ref):
        z = f_ref[...] + b_ref[...]
        log_f = jnp.minimum(z, 0.0) - jnp.log1p(jnp.exp(-jnp.abs(z)))
        c_ref[...] = _lane_scan(log_f, reverse=False)

    return pl.pallas_call(
        body, name="forget_cumsum", out_shape=jax.ShapeDtypeStruct(fft.shape, F32),
        in_specs=[_resident(), _resident()], out_specs=_resident(),
    )(fft, bcol)


def _forget_bwd(dc_row, fft, bcol):
    def body(dc_ref, f_ref, b_ref, dff_ref, db_ref):
        z = f_ref[...] + b_ref[...]
        dlog_f = _lane_scan(dc_ref[...], reverse=True)
        dff = dlog_f * (1.0 / (1.0 + jnp.exp(z)))
        dff_ref[...] = dff
        db_ref[...] = jnp.sum(dff, axis=1, keepdims=True)

    return pl.pallas_call(
        body, name="forget_bwd",
        out_shape=[jax.ShapeDtypeStruct(fft.shape, F32), jax.ShapeDtypeStruct((fft.shape[0], 1), F32)],
        in_specs=[_resident()] * 3, out_specs=[_resident()] * 2,
    )(dc_row, fft, bcol)


def _head_select(shape, upper):
    lane = lax.broadcasted_iota(jnp.int32, shape, 1)
    return lane >= HEAD_DIM if upper else lane < HEAD_DIM


def _fox_fwd(fqkv, c_row3, tq, tk, pairs_per_loop=2, row_chunks=1):
    S = fqkv.shape[0]
    rq = tq // row_chunks
    n_band = tq // tk

    def body(q_ref, k_ref, v_ref, ck_ref, o_ref, lse_ref):
        qi = pl.program_id(0)
        row = lax.broadcasted_iota(jnp.int32, (rq, tk), 0)
        col = lax.broadcasted_iota(jnp.int32, (rq, tk), 1)
        low = _head_select((rq, 128), 0)
        for first in range(0, N_HEADS // 2, pairs_per_loop):
            pairs = range(first, first + pairs_per_loop)
            chains = [(pr, hh, rc) for pr in pairs for hh in range(2) for rc in range(row_chunks)]
            qh = {}
            for pr in pairs:
                for rc in range(row_chunks):
                    q2 = q_ref[rc * rq:(rc + 1) * rq, pr * 128:(pr + 1) * 128]
                    qh[pr, 0, rc] = jnp.where(low, q2, jnp.zeros_like(q2))
                    qh[pr, 1, rc] = jnp.where(low, jnp.zeros_like(q2), q2)

            def block(kb, carry, band, chains=chains, qh=qh):
                rows = pl.ds(pl.multiple_of(kb * tk, tk), tk)
                out = []
                for (pr, hh, rc), (m, l, acc) in zip(chains, carry):
                    if band is not None and (rc + 1) * rq <= band * tk:
                        out.append((m, l, acc))
                        continue
                    lanes = slice(pr * 128, (pr + 1) * 128)
                    s = _dot_nt(qh[pr, hh, rc], k_ref[rows, lanes]) - ck_ref[2 * pr + hh, pl.ds(kb, 1), :]
                    if band is not None:
                        s = jnp.where(row + rc * rq >= col + band * tk, s, NEG)
                    m_new = jnp.maximum(m, jnp.max(s, axis=-1, keepdims=True))
                    p = jnp.exp(s - m_new)
                    alpha = jnp.exp(m - m_new)
                    l = alpha * l + jnp.sum(p, axis=-1, keepdims=True)
                    acc = alpha * acc + _dot(p.astype(MM), v_ref[rows, lanes])
                    out.append((m_new, l, acc))
                return tuple(out)

            carry = tuple((jnp.full((rq, 1), NEG, F32), jnp.zeros((rq, 1), F32), jnp.zeros((rq, 128), F32))
                          for _ in chains)
            carry = lax.fori_loop(0, qi * n_band, functools.partial(block, band=None), carry)
            for band in range(n_band):
                carry = block(qi * n_band + band, carry, band=band)
            res = {}
            for (pr, hh, rc), (m, l, acc) in zip(chains, carry):
                res[pr, hh, rc] = acc / l
                lse_ref[rc * rq:(rc + 1) * rq, 2 * pr + hh:2 * pr + hh + 1] = m + jnp.log(l)
            for pr in pairs:
                for rc in range(row_chunks):
                    o_ref[rc * rq:(rc + 1) * rq, pr * 128:(pr + 1) * 128] = jnp.where(
                        low, res[pr, 0, rc], res[pr, 1, rc]).astype(MM)

    return pl.pallas_call(
        body, name="fox_fwd", grid=(S // tq,),
        in_specs=[pl.BlockSpec((tq, D_ATT), lambda i: (i, 0)), pl.BlockSpec((S, D_ATT), lambda i: (0, 1)),
                  pl.BlockSpec((S, D_ATT), lambda i: (0, 2)), _resident()],
        out_specs=[_rows(tq, D_ATT), _rows(tq, N_HEADS)],
        out_shape=[jax.ShapeDtypeStruct((S, D_ATT), MM), jax.ShapeDtypeStruct((S, N_HEADS), F32)],
        compiler_params=_params("parallel"),
    )(fqkv, fqkv, fqkv, c_row3)


def _swa_bias(rel_bias_slot, bucket):
    def body(rb_ref, bk_ref, out_ref):
        bk = bk_ref[...]
        for s in range(N_HEADS):
            acc = jnp.where(bk < 0, NEG, 0.0).astype(F32)
            for b in range(N_BUCKETS):
                acc = jnp.where(bk == b, rb_ref[b, s], acc)
            out_ref[s] = acc

    return pl.pallas_call(
        body, name="swa_bias", out_shape=jax.ShapeDtypeStruct((N_HEADS, WINDOW, 2 * WINDOW), F32),
        in_specs=[pl.BlockSpec(memory_space=pltpu.SMEM), _resident()], out_specs=_resident(),
    )(rel_bias_slot, bucket)


def _stack4(piece):
    return jnp.concatenate([piece(j) for j in range(4)], axis=0)


def _swa_specs(S):
    q = pl.BlockSpec((WINDOW, D_ATT), lambda n: (n, 0))
    kp = pl.BlockSpec((WINDOW, D_KV), lambda n: (jnp.maximum(n - 1, 0), 4))
    kc = pl.BlockSpec((WINDOW, D_KV), lambda n: (n, 4))
    vp = pl.BlockSpec((WINDOW, D_KV), lambda n: (jnp.maximum(n - 1, 0), 5))
    vc = pl.BlockSpec((WINDOW, D_KV), lambda n: (n, 5))
    return [q, kp, kc, vp, vc]


def _swa_fwd(sqkv, biasm, sinks_slot):
    S = sqkv.shape[0]

    def body(q_ref, kp_ref, kc_ref, vp_ref, vc_ref, bias_ref, sink_ref, o_ref, lse_ref):
        n = pl.program_id(0)
        no_prev = jnp.where(n > 0, 0.0, NEG)
        low = _head_select((WINDOW, 128), 0)
        res = []
        for g in range(2):
            sel = low if g == 0 else jnp.logical_not(low)
            qg = _stack4(lambda j: jnp.where(sel, q_ref[:, j * 128:(j + 1) * 128], jnp.zeros((WINDOW, 128), MM)))
            sink = _stack4(lambda j: jnp.full((WINDOW, 1), sink_ref[2 * j + g], F32))
            sp = _dot_nt(qg, kp_ref[...]) + _stack4(lambda j: bias_ref[2 * j + g, :, :WINDOW]) + no_prev
            sc = _dot_nt(qg, kc_ref[...]) + _stack4(lambda j: bias_ref[2 * j + g, :, WINDOW:])
            m = jnp.maximum(jnp.maximum(jnp.max(sp, axis=-1, keepdims=True),
                                        jnp.max(sc, axis=-1, keepdims=True)), sink)
            ep = jnp.exp(sp - m)
            ec = jnp.exp(sc - m)
            den = jnp.sum(ep, axis=-1, keepdims=True) + jnp.sum(ec, axis=-1, keepdims=True) + jnp.exp(sink - m)
            res.append((_dot(ep.astype(MM), vp_ref[...]) + _dot(ec.astype(MM), vc_ref[...])) / den)
            lse = m + jnp.log(den)
            for j in range(4):
                lse_ref[:, 2 * j + g:2 * j + g + 1] = lse[j * WINDOW:(j + 1) * WINDOW]
        for j in range(4):
            rows = slice(j * WINDOW, (j + 1) * WINDOW)
            o_ref[:, j * 128:(j + 1) * 128] = jnp.where(low, res[0][rows], res[1][rows]).astype(MM)

    return pl.pallas_call(
        body, name="swa_fwd", grid=(S // WINDOW,),
        in_specs=_swa_specs(S) + [_resident(), pl.BlockSpec(memory_space=pltpu.SMEM)],
        out_specs=[_rows(WINDOW, D_ATT), _rows(WINDOW, N_HEADS)],
        out_shape=[jax.ShapeDtypeStruct((S, D_ATT), MM), jax.ShapeDtypeStruct((S, N_HEADS), F32)],
        compiler_params=_params("parallel"),
    )(sqkv, sqkv, sqkv, sqkv, sqkv, biasm, sinks_slot)


def _post_attn(x, fox_o, swa_o, wout_fox, wout_swa, g2, g3, tm):
    S = x.shape[0]

    def body(x_ref, fo_ref, so_ref, wf_ref, ws_ref, g2_ref, g3_ref, mix_ref, h1_ref, m_ref):
        mix = _dot(fo_ref[...], wf_ref[...]) + _dot(so_ref[...], ws_ref[...])
        mix_ref[...] = mix
        n2, _ = _rms(mix)
        h1 = x_ref[...] + n2 * g2_ref[...]
        h1_ref[...] = h1
        n3, _ = _rms(h1)
        m_ref[...] = (n3 * g3_ref[...]).astype(MM)

    return pl.pallas_call(
        body, name="post_attn", grid=(S // tm,),
        in_specs=[_rows(tm, D_MODEL), _rows(tm, D_ATT), _rows(tm, D_ATT), _resident(), _resident(),
                  _const((1, D_MODEL)), _const((1, D_MODEL))],
        out_specs=[_rows(tm, D_MODEL)] * 3,
        out_shape=[jax.ShapeDtypeStruct((S, D_MODEL), F32), jax.ShapeDtypeStruct((S, D_MODEL), F32),
                   jax.ShapeDtypeStruct((S, D_MODEL), MM)],
        compiler_params=_params("parallel"),
    )(x, fox_o, swa_o, wout_fox, wout_swa, g2, g3)


def _mlp_fwd(m, h1, w1, w2, g4, tm):
    S = m.shape[0]

    def body(m_ref, h1_ref, w1_ref, w2_ref, g4_ref, u_ref, y_ref, h2_ref):
        mb = m_ref[...]
        y = jnp.zeros((tm, D_MODEL), F32)
        for j in range(N_DEV):
            cols = slice(j * FF_CHUNK, (j + 1) * FF_CHUNK)
            u = _dot(mb, w1_ref[j])
            u_ref[:, cols] = u.astype(MM)
            y = y + _dot(jnp.square(jnp.maximum(u, 0.0)).astype(MM), w2_ref[cols, :])
        y_ref[...] = y
        n4, _ = _rms(y)
        h2_ref[...] = h1_ref[...] + n4 * g4_ref[...]

    return pl.pallas_call(
        body, name="mlp_fwd", grid=(S // tm,),
        in_specs=[_rows(tm, D_MODEL), _rows(tm, D_MODEL), _resident(), _resident(), _const((1, D_MODEL))],
        out_specs=[_rows(tm, D_FF), _rows(tm, D_MODEL), _rows(tm, D_MODEL)],
        out_shape=[jax.ShapeDtypeStruct((S, D_FF), MM), jax.ShapeDtypeStruct((S, D_MODEL), F32),
                   jax.ShapeDtypeStruct((S, D_MODEL), F32)],
        compiler_params=_params("parallel"),
    )(m, h1, w1, w2, g4)


def _ple_loss(h2, p, target, wg, wple, g5, tm):
    S = h2.shape[0]

    def body(h2_ref, p_ref, t_ref, wg_ref, wp_ref, g5_ref, dh2_ref, dpe_ref, dgl_ref, dg5_ref, loss_ref):
        i = pl.program_id(0)
        h2 = h2_ref[...]
        gate = jax.nn.sigmoid(_dot(h2.astype(MM), wg_ref[...]))
        pe = _dot(p_ref[...].astype(MM), wp_ref[...])
        n5, r5 = _rms(pe * gate)
        g5 = g5_ref[...]
        diff = h2 + n5 * g5 - t_ref[...]
        per_token = jnp.mean(jnp.square(diff), axis=-1, keepdims=True)
        _accumulate(loss_ref, 0.5 * jnp.sum(per_token, axis=0, keepdims=True), i)
        dh3 = diff * (1.0 / D_MODEL)
        de, dg5 = _rms_bwd(dh3, n5, r5, g5)
        _accumulate(dg5_ref, dg5, i)
        dpe_ref[...] = (de * gate).astype(MM)
        dgl = (de * pe * gate * (1.0 - gate)).astype(MM)
        dgl_ref[...] = dgl
        dh2_ref[...] = dh3 + _dot_nt(dgl, wg_ref[...])

    return pl.pallas_call(
        body, name="ple_loss", grid=(S // tm,),
        in_specs=[_rows(tm, D_MODEL), _rows(tm, D_PLE), _rows(tm, D_MODEL), _resident(), _resident(),
                  _const((1, D_MODEL))],
        out_specs=[_rows(tm, D_MODEL), _rows(tm, D_MODEL), _rows(tm, D_MODEL), _const((1, D_MODEL)), _const((1, 1))],
        out_shape=[jax.ShapeDtypeStruct((S, D_MODEL), F32), jax.ShapeDtypeStruct((S, D_MODEL), MM),
                   jax.ShapeDtypeStruct((S, D_MODEL), MM), jax.ShapeDtypeStruct((1, D_MODEL), F32),
                   jax.ShapeDtypeStruct((1, 1), F32)],
        compiler_params=_params("arbitrary"),
    )(h2, p, target, wg, wple, g5)


def _mlp_bwd(dh2, y, h1, u, w1, w2, g4, g3, tm):
    S = dh2.shape[0]

    def body(dh2_ref, y_ref, h1_ref, u_ref, w1_ref, w2_ref, g4_ref, g3_ref,
             dh1_ref, dy_ref, du_ref, dg4_ref, dg3_ref):
        i = pl.program_id(0)
        dh2 = dh2_ref[...]
        n4, r4 = _rms(y_ref[...])
        dy, dg4 = _rms_bwd(dh2, n4, r4, g4_ref[...])
        _accumulate(dg4_ref, dg4, i)
        dyb = dy.astype(MM)
        dy_ref[...] = dyb
        dm = jnp.zeros((tm, D_MODEL), F32)
        for j in range(N_DEV):
            cols = slice(j * FF_CHUNK, (j + 1) * FF_CHUNK)
            dact = _dot_nt(dyb, w2_ref[cols, :])
            du = (dact * (2.0 * jnp.maximum(u_ref[:, cols].astype(F32), 0.0))).astype(MM)
            du_ref[:, cols] = du
            dm = dm + _dot_nt(du, w1_ref[j])
        n3, r3 = _rms(h1_ref[...])
        dx, dg3 = _rms_bwd(dm, n3, r3, g3_ref[...])
        _accumulate(dg3_ref, dg3, i)
        dh1_ref[...] = dh2 + dx

    return pl.pallas_call(
        body, name="mlp_bwd", grid=(S // tm,),
        in_specs=[_rows(tm, D_MODEL), _rows(tm, D_MODEL), _rows(tm, D_MODEL), _rows(tm, D_FF),
                  _resident(), _resident(), _const((1, D_MODEL)), _const((1, D_MODEL))],
        out_specs=[_rows(tm, D_MODEL), _rows(tm, D_MODEL), _rows(tm, D_FF), _const((1, D_MODEL)),
                   _const((1, D_MODEL))],
        out_shape=[jax.ShapeDtypeStruct((S, D_MODEL), F32), jax.ShapeDtypeStruct((S, D_MODEL), MM),
                   jax.ShapeDtypeStruct((S, D_FF), MM), jax.ShapeDtypeStruct((1, D_MODEL), F32),
                   jax.ShapeDtypeStruct((1, D_MODEL), F32)],
        compiler_params=_params("arbitrary"),
    )(dh2, y, h1, u, w1, w2, g4, g3)


def _attn_out_bwd(dh1, mix, fox_o, swa_o, wout_fox, wout_swa, g2, head_rows, tm):
    S = dh1.shape[0]

    def body(dh1_ref, mix_ref, fo_ref, so_ref, wf_ref, ws_ref, g2_ref, er_ref,
             dmix_ref, dcat_ref, drow_ref, dswa_ref, dg2_ref):
        i = pl.program_id(0)
        n2, r2 = _rms(mix_ref[...])
        dmix, dg2 = _rms_bwd(dh1_ref[...], n2, r2, g2_ref[...])
        _accumulate(dg2_ref, dg2, i)
        dmb = dmix.astype(MM)
        dmix_ref[...] = dmb
        dfo = _dot_nt(dmb, wf_ref[...]).astype(MM)
        dso = _dot_nt(dmb, ws_ref[...]).astype(MM)
        dcat_ref[:, :D_ATT] = dfo
        dcat_ref[:, D_ATT:] = dso
        hi = lax.Precision.HIGHEST
        prod_f = dfo.astype(F32) * fo_ref[...].astype(F32)
        prod_s = dso.astype(F32) * so_ref[...].astype(F32)
        drow_ref[...] = lax.dot_general(er_ref[...], prod_f, NT, precision=hi, preferred_element_type=F32)
        dswa_ref[...] = lax.dot_general(er_ref[...], prod_s, NT, precision=hi, preferred_element_type=F32)

    return pl.pallas_call(
        body, name="attn_out_bwd", grid=(S // tm,),
        in_specs=[_rows(tm, D_MODEL), _rows(tm, D_MODEL), _rows(tm, D_ATT), _rows(tm, D_ATT), _resident(),
                  _resident(), _const((1, D_MODEL)), _resident()],
        out_specs=[_rows(tm, D_MODEL), _rows(tm, D_MODEL), pl.BlockSpec((N_HEADS, tm), lambda i: (0, i)),
                   pl.BlockSpec((N_HEADS, tm), lambda i: (0, i)), _const((1, D_MODEL))],
        out_shape=[jax.ShapeDtypeStruct((S, D_MODEL), MM), jax.ShapeDtypeStruct((S, D_MODEL), MM),
                   jax.ShapeDtypeStruct((N_HEADS, S), F32), jax.ShapeDtypeStruct((N_HEADS, S), F32),
                   jax.ShapeDtypeStruct((1, D_MODEL), F32)],
        compiler_params=_params("arbitrary"),
    )(dh1, mix, fox_o, swa_o, wout_fox, wout_swa, g2, head_rows)


def _fox_bwd(fqkv, dcat, lse_row3, d_row3, c_col, tq, tk, pairs_per_loop=2):
    S = fqkv.shape[0]
    n_blk = S // tk
    n_qblk = S // tq
    n_band = tk // tq

    def body(q_ref, k_ref, v_ref, do_ref, lse_ref, dd_ref, ck_ref, dq_ref, dk_ref, dv_ref, dc_ref, dcq_ref):
        kb = pl.program_id(0)

        @pl.when(kb == 0)
        def _():
            dq_ref[...] = jnp.zeros_like(dq_ref)
            dcq_ref[...] = jnp.zeros_like(dcq_ref)

        key = lax.broadcasted_iota(jnp.int32, (tk, tq), 0)
        qry = lax.broadcasted_iota(jnp.int32, (tk, tq), 1)
        low = _head_select((tk, 128), 0)
        for first in range(0, N_HEADS // 2, pairs_per_loop):
            pairs = range(first, first + pairs_per_loop)
            heads = [(pr, hh) for pr in pairs for hh in range(2)]
            kh, vh, ck = {}, {}, {}
            for pr in pairs:
                k2 = k_ref[:, pr * 128:(pr + 1) * 128]
                v2 = v_ref[:, pr * 128:(pr + 1) * 128]
                zero = jnp.zeros_like(k2)
                kh[pr, 0], kh[pr, 1] = jnp.where(low, k2, zero), jnp.where(low, zero, k2)
                vh[pr, 0], vh[pr, 1] = jnp.where(low, v2, zero), jnp.where(low, zero, v2)
                for hh in range(2):
                    ck[pr, hh] = ck_ref[:, 2 * pr + hh:2 * pr + hh + 1]

            def block(qb, carry, band, pairs=pairs, kh=kh, vh=vh, ck=ck):
                rows = pl.ds(pl.multiple_of(qb * tq, tq), tq)
                out = []
                it = iter(carry)
                for pr in pairs:
                    lanes = slice(pr * 128, (pr + 1) * 128)
                    q2 = q_ref[rows, lanes]
                    do2 = do_ref[rows, lanes]
                    dq = None
                    for hh in range(2):
                        h = 2 * pr + hh
                        dk, dv, dc = next(it)
                        s_t = _dot_nt(kh[pr, hh], q2) - ck[pr, hh]
                        p_t = jnp.exp(s_t - lse_ref[h, pl.ds(qb, 1), :])
                        if band is not None:
                            p_t = jnp.where(qry + band * tq >= key, p_t, 0.0)
                        ds_t = p_t * (_dot_nt(vh[pr, hh], do2) - dd_ref[h, pl.ds(qb, 1), :])
                        dsb = ds_t.astype(MM)
                        dv = dv + _dot(p_t.astype(MM), do2)
                        dk = dk + _dot(dsb, q2)
                        dc = dc - jnp.sum(ds_t, axis=1, keepdims=True)
                        part = _dot_tn(dsb, kh[pr, hh])
                        dq = part if dq is None else dq + part
                        dcq_ref[h, pl.ds(qb, 1), :] += jnp.sum(ds_t, axis=0, keepdims=True)
                        out.append((dk, dv, dc))
                    dq_ref[rows, lanes] += dq
                return tuple(out)

            carry = tuple((jnp.zeros((tk, 128), F32), jnp.zeros((tk, 128), F32), jnp.zeros((tk, 1), F32))
                          for _ in heads)
            for band in range(n_band):
                carry = block(kb * n_band + band, carry, band=band)
            carry = lax.fori_loop((kb + 1) * n_band, n_qblk, functools.partial(block, band=None), carry)
            grads = dict(zip(heads, carry))
            for pr in pairs:
                lanes = slice(pr * 128, (pr + 1) * 128)
                dk_ref[:, lanes] = jnp.where(low, grads[pr, 0][0], grads[pr, 1][0]).astype(MM)
                dv_ref[:, lanes] = jnp.where(low, grads[pr, 0][1], grads[pr, 1][1]).astype(MM)
                for hh in range(2):
                    dc_ref[:, 2 * pr + hh:2 * pr + hh + 1] = grads[pr, hh][2]

        @pl.when(kb == n_blk - 1)
        def _():
            dq_ref[...] = dq_ref[...] * Q_SCALE

    return pl.pallas_call(
        body, name="fox_bwd", grid=(n_blk,),
        in_specs=[pl.BlockSpec((S, D_ATT), lambda i: (0, 0)), pl.BlockSpec((tk, D_ATT), lambda i: (i, 1)),
                  pl.BlockSpec((tk, D_ATT), lambda i: (i, 2)), pl.BlockSpec((S, D_ATT), lambda i: (0, 0)),
                  _resident(), _resident(), _rows(tk, N_HEADS)],
        out_specs=[_const((S, D_ATT)), _rows(tk, D_ATT), _rows(tk, D_ATT), _rows(tk, N_HEADS),
                   _const((N_HEADS, n_qblk, tq))],
        out_shape=[jax.ShapeDtypeStruct((S, D_ATT), F32), jax.ShapeDtypeStruct((S, D_ATT), MM),
                   jax.ShapeDtypeStruct((S, D_ATT), MM), jax.ShapeDtypeStruct((S, N_HEADS), F32),
                   jax.ShapeDtypeStruct((N_HEADS, n_qblk, tq), F32)],
        compiler_params=_params("arbitrary"),
    )(fqkv, fqkv, fqkv, dcat, lse_row3, d_row3, c_col)


def _swa_bwd(sqkv, dcat, biasm, sinks_slot, bucket, lse, d_col):
    S = sqkv.shape[0]
    n_blk = S // WINDOW

    def body(q_ref, kp_ref, kc_ref, vp_ref, vc_ref, do_ref, bias_ref, sink_ref, bk_ref, lse_ref, dd_ref,
             dq_ref, dk_ref, dv_ref, drb_ref, dsink_ref, ds_acc):
        n = pl.program_id(0)

        @pl.when(n == 0)
        def _():
            dk_ref[...] = jnp.zeros_like(dk_ref)
            dv_ref[...] = jnp.zeros_like(dv_ref)
            ds_acc[...] = jnp.zeros_like(ds_acc)
            dsink_ref[...] = jnp.zeros_like(dsink_ref)

        no_prev = jnp.where(n > 0, 0.0, NEG)
        prev = pl.ds(pl.multiple_of(jnp.maximum(n - 1, 0) * WINDOW, WINDOW), WINDOW)
        cur = pl.ds(pl.multiple_of(n * WINDOW, WINDOW), WINDOW)
        lane8 = lax.broadcasted_iota(jnp.int32, (1, N_HEADS), 1)
        dkp = jnp.zeros((WINDOW, D_KV), F32)
        dkc = jnp.zeros((WINDOW, D_KV), F32)
        dvp = jnp.zeros((WINDOW, D_KV), F32)
        dvc = jnp.zeros((WINDOW, D_KV), F32)
        dsink = jnp.zeros((1, N_HEADS), F32)
        low = _head_select((WINDOW, 128), 0)
        zero = jnp.zeros((WINDOW, 128), MM)
        dqs = []
        for g in range(2):
            sel = low if g == 0 else jnp.logical_not(low)
            qg = _stack4(lambda j: jnp.where(sel, q_ref[:, j * 128:(j + 1) * 128], zero))
            dog = _stack4(lambda j: jnp.where(sel, do_ref[:, j * 128:(j + 1) * 128], zero))
            lse_g = _stack4(lambda j: lse_ref[:, 2 * j + g:2 * j + g + 1])
            dd = _stack4(lambda j: dd_ref[:, 2 * j + g:2 * j + g + 1])
            sink = _stack4(lambda j: jnp.full((WINDOW, 1), sink_ref[2 * j + g], F32))
            pp = jnp.exp(_dot_nt(qg, kp_ref[...]) + _stack4(lambda j: bias_ref[2 * j + g, :, :WINDOW]) + no_prev - lse_g)
            pc = jnp.exp(_dot_nt(qg, kc_ref[...]) + _stack4(lambda j: bias_ref[2 * j + g, :, WINDOW:]) - lse_g)
            sink_term = jnp.exp(sink - lse_g) * dd
            dsp = pp * (_dot_nt(dog, vp_ref[...]) - dd)
            dsc = pc * (_dot_nt(dog, vc_ref[...]) - dd)
            for j in range(4):
                rows = slice(j * WINDOW, (j + 1) * WINDOW)
                dsink = dsink + jnp.where(lane8 == 2 * j + g, -jnp.sum(sink_term[rows]), 0.0)
                ds_acc[2 * j + g, :, :WINDOW] += dsp[rows]
                ds_acc[2 * j + g, :, WINDOW:] += dsc[rows]
            dspb, dscb = dsp.astype(MM), dsc.astype(MM)
            dqs.append(_dot(dspb, kp_ref[...]) + _dot(dscb, kc_ref[...]))
            dkp = dkp + _dot_tn(dspb, qg)
            dkc = dkc + _dot_tn(dscb, qg)
            dvp = dvp + _dot_tn(pp.astype(MM), dog)
            dvc = dvc + _dot_tn(pc.astype(MM), dog)
        for j in range(4):
            rows = slice(j * WINDOW, (j + 1) * WINDOW)
            dq_ref[:, j * 128:(j + 1) * 128] = (jnp.where(low, dqs[0][rows], dqs[1][rows]) * Q_SCALE).astype(MM)
        dk_ref[prev, :] += dkp
        dk_ref[cur, :] += dkc
        dv_ref[prev, :] += dvp
        dv_ref[cur, :] += dvc
        dsink_ref[...] += dsink

        @pl.when(n == n_blk - 1)
        def _():
            bk = bk_ref[...]
            rb = lax.broadcasted_iota(jnp.int32, (N_BUCKETS, N_HEADS), 0)
            cb = lax.broadcasted_iota(jnp.int32, (N_BUCKETS, N_HEADS), 1)
            out = jnp.zeros((N_BUCKETS, N_HEADS), F32)
            for s in range(N_HEADS):
                acc = ds_acc[s]
                for b in range(N_BUCKETS):
                    out = out + jnp.where((rb == b) & (cb == s), jnp.sum(jnp.where(bk == b, acc, 0.0)), 0.0)
            drb_ref[...] = out

    do_spec = pl.BlockSpec((WINDOW, D_ATT), lambda n: (n, 1))
    return pl.pallas_call(
        body, name="swa_bwd", grid=(n_blk,),
        in_specs=_swa_specs(S) + [do_spec, _resident(), pl.BlockSpec(memory_space=pltpu.SMEM), _resident(),
                                  _rows(WINDOW, N_HEADS), _rows(WINDOW, N_HEADS)],
        out_specs=[_rows(WINDOW, D_ATT), _const((S, D_KV)), _const((S, D_KV)), _const((N_BUCKETS, N_HEADS)),
                   _const((1, N_HEADS))],
        out_shape=[jax.ShapeDtypeStruct((S, D_ATT), MM), jax.ShapeDtypeStruct((S, D_KV), F32),
                   jax.ShapeDtypeStruct((S, D_KV), F32), jax.ShapeDtypeStruct((N_BUCKETS, N_HEADS), F32),
                   jax.ShapeDtypeStruct((1, N_HEADS), F32)],
        scratch_shapes=[pltpu.VMEM((N_HEADS, WINDOW, 2 * WINDOW), F32)],
        compiler_params=_params("arbitrary"),
    )(sqkv, sqkv, sqkv, sqkv, sqkv, dcat, biasm, sinks_slot, bucket, lse, d_col)


D_Z = 3 * D_ATT + D_ATT + 2 * D_KV


def _pack_dz(dq_fox, dk_fox, dv_fox, dsq, dsk, dsv, tm):
    S = dq_fox.shape[0]

    def body(dq_ref, dk_ref, dv_ref, dsq_ref, dsk_ref, dsv_ref, dz_ref):
        dz_ref[:, 0:512] = dq_ref[...].astype(MM)
        dz_ref[:, 512:1024] = dk_ref[...]
        dz_ref[:, 1024:1536] = dv_ref[...]
        dz_ref[:, 1536:2048] = dsq_ref[...]
        dz_ref[:, 2048:2176] = dsk_ref[...].astype(MM)
        dz_ref[:, 2176:2304] = dsv_ref[...].astype(MM)

    return pl.pallas_call(
        body, name="pack_dz", grid=(S // tm,),
        in_specs=[_rows(tm, D_ATT), _rows(tm, D_ATT), _rows(tm, D_ATT), _rows(tm, D_ATT), _rows(tm, D_KV),
                  _rows(tm, D_KV)],
        out_specs=_rows(tm, D_Z), out_shape=jax.ShapeDtypeStruct((S, D_Z), MM),
        compiler_params=_params("parallel"),
    )(dq_fox, dk_fox, dv_fox, dsq, dsk, dsv)


def _pre_attn_bwd(x, dh1, dz, dff_t, wt, g1, tm):
    S = x.shape[0]

    def body(x_ref, dh1_ref, dz_ref, dff_ref, wt_ref, g1_ref, dx_ref, dg1_ref):
        i = pl.program_id(0)
        da = (_dot(dz_ref[:, 0:WT_FF], wt_ref[0:WT_FF]) + _dot(dz_ref[:, WT_FF:D_Z], wt_ref[WT_SQ:WT_ROWS])
              + _dot_tn(dff_ref[...].astype(MM), wt_ref[WT_FF:WT_SQ]))
        n1, r1 = _rms(x_ref[...])
        dx, dg1 = _rms_bwd(da, n1, r1, g1_ref[...])
        _accumulate(dg1_ref, dg1, i)
        dx_ref[...] = dh1_ref[...] + dx

    return pl.pallas_call(
        body, name="pre_attn_bwd", grid=(S // tm,),
        in_specs=[_rows(tm, D_MODEL), _rows(tm, D_MODEL), _rows(tm, D_Z), pl.BlockSpec((16, tm), lambda i: (0, i)),
                  _resident(), _const((1, D_MODEL))],
        out_specs=[_rows(tm, D_MODEL), _const((1, D_MODEL))],
        out_shape=[jax.ShapeDtypeStruct((S, D_MODEL), F32), jax.ShapeDtypeStruct((1, D_MODEL), F32)],
        compiler_params=_params("arbitrary"),
    )(x, dh1, dz, dff_t, wt, g1)


def _weight_grad(a, b, name, tk, n_chunks=1, relu2=False):
    S, K = a.shape
    N = b.shape[1]
    cn = N // n_chunks

    def body(a_ref, b_ref, out_ref):
        av = a_ref[...]
        if relu2:
            av = jnp.square(jnp.maximum(av.astype(F32), 0.0))
        av = av.astype(MM)
        for j in range(n_chunks):
            val = _dot_tn(av, b_ref[:, j * cn:(j + 1) * cn].astype(MM)).astype(MM)
            if n_chunks > 1:
                out_ref[j] = val
            else:
                out_ref[...] = val

    if n_chunks > 1:
        out_spec = pl.BlockSpec((n_chunks, tk, cn), lambda i: (0, i, 0))
        out_shape = jax.ShapeDtypeStruct((n_chunks, K, cn), MM)
    else:
        out_spec = pl.BlockSpec((tk, N), lambda i: (i, 0))
        out_shape = jax.ShapeDtypeStruct((K, N), MM)
    return pl.pallas_call(
        body, name=name, grid=(K // tk,),
        in_specs=[pl.BlockSpec((S, tk), lambda i: (0, i)), _resident()],
        out_specs=out_spec, out_shape=out_shape, compiler_params=_params("parallel"),
    )(a, b)


def _forget_weight_grad(dff_t, a):
    def body(d_ref, a_ref, out_ref):
        out_ref[...] = _dot(d_ref[...].astype(MM), a_ref[...])

    return pl.pallas_call(
        body, name="forget_weight_grad", out_shape=jax.ShapeDtypeStruct((16, D_MODEL), F32),
        in_specs=[_resident(), _resident()], out_specs=_resident(),
    )(dff_t, a)


def _place():
    return lax.axis_index("x"), lax.axis_index("y"), lax.axis_index("c")


def _all_gather_sequencer(stack):
    ref = jax.new_ref(stack, memory_space=pltpu.MemorySpace.HBM)

    @pl.kernel(mesh=plsc.ScalarSubcoreMesh(axis_name="sequencer", num_cores=1), name="all_gather_sequencer",
               scratch_types=(pltpu.SemaphoreType.DMA((7,)), pltpu.SemaphoreType.DMA((7,))),
               compiler_params=pltpu.CompilerParams(collective_id=1))
    def launch(send_sems, recv_sems):
        x, y, c = _place()
        sibling = (x, y, 1 - c)
        chips = [(1 - x, y), (x, 1 - y), (1 - x, 1 - y)]
        peers = [sibling] + [(px, py, c) for px, py in chips]
        barrier = pltpu.get_barrier_semaphore()
        for peer in peers:
            pl.semaphore_signal(barrier, inc=1, device_id=peer, device_id_type=MESH)
        pl.semaphore_wait(barrier, len(peers))

        def copy(k, block, to):
            px, py, pc = block
            slot = ref.at[4 * px + 2 * py + pc]
            return _remote(slot, slot, send_sems, recv_sems, k, to)

        first = [copy(k, (x, y, c), peer) for k, peer in enumerate(peers)]
        for cp in first:
            cp.start()
        passed = []
        for j, (px, py) in enumerate(chips):
            copy(1 + j, (px, py, c), sibling).wait_recv()
            passed.append(copy(4 + j, (px, py, c), sibling))
            passed[-1].start()
        copy(0, (x, y, 1 - c), sibling).wait_recv()
        for j, (px, py) in enumerate(chips):
            copy(4 + j, (px, py, 1 - c), sibling).wait_recv()
        for cp in first + passed:
            cp.wait_send()

    launch()
    return ref[...]


def _chip_sum(grad, other, name):
    _, _, r, cdim = grad.shape
    tr = 512 if r % 512 == 0 else r

    def body(c_ref, g_ref, o_ref, out_ref):
        out_ref[...] = (g_ref[...].astype(F32) + o_ref[...].astype(F32)).astype(out_ref.dtype)

    return pl.pallas_call(
        body, name=name,
        grid_spec=pltpu.PrefetchScalarGridSpec(
            num_scalar_prefetch=1, grid=(4, r // tr),
            in_specs=[pl.BlockSpec((None, None, tr, cdim), lambda k, i, c_ref: (k, c_ref[0], i, 0)),
                      pl.BlockSpec((None, tr, cdim), lambda k, i, c_ref: (k, i, 0))],
            out_specs=pl.BlockSpec((None, tr, cdim), lambda k, i, c_ref: (k, i, 0))),
        out_shape=jax.ShapeDtypeStruct((4, r, cdim), MM),
        compiler_params=_params("parallel", "parallel"),
    )(lax.axis_index("c").astype(jnp.int32).reshape(1), grad, other)


HBM_SPEC = pl.BlockSpec(memory_space=pltpu.HBM)
SEM_SPEC = pl.BlockSpec(memory_space=pltpu.SEMAPHORE)
DATAFLOW = pltpu.SideEffectType.DATAFLOW_SIDE_EFFECTING


def _exchange_start(name, arrays, n_copies, plan):
    n = len(arrays)

    def body(*refs):
        send_sems, recv_sems, token = refs[n], refs[n + 1], refs[2 * n + 2]
        for cp in plan(refs[:n], send_sems, recv_sems):
            cp.start()
        token[...] = jnp.zeros_like(token)

    out = pl.pallas_call(
        body, name=name,
        out_shape=(pltpu.SemaphoreType.DMA((n_copies,)), pltpu.SemaphoreType.DMA((n_copies,)),
                   *[pltpu.HBM(a.shape, a.dtype) for a in arrays], jax.ShapeDtypeStruct((1, D_MODEL), F32)),
        in_specs=[HBM_SPEC] * n,
        out_specs=(SEM_SPEC, SEM_SPEC, *[HBM_SPEC] * n, pl.BlockSpec(memory_space=pltpu.VMEM)),
        input_output_aliases={i: 2 + i for i in range(n)},
        compiler_params=pltpu.CompilerParams(has_side_effects=DATAFLOW),
    )(*[pltpu.with_memory_space_constraint(a, pltpu.HBM) for a in arrays])
    return (out[0], out[1]), list(out[2:2 + n]), out[2 + n]


def _exchange_wait(name, arrays, sems, after, plan):
    n = len(arrays)
    after = list(after) if isinstance(after, (list, tuple)) else [after]

    def body(*refs):
        send_sems, recv_sems = refs[n], refs[n + 1]
        for cp in plan(refs[:n], send_sems, recv_sems):
            cp.wait_send()
            cp.wait_recv()

    out = pl.pallas_call(
        body, name=name, out_shape=[pltpu.HBM(a.shape, a.dtype) for a in arrays],
        in_specs=[HBM_SPEC] * n + [SEM_SPEC, SEM_SPEC] + [pl.BlockSpec(memory_space=pl.ANY)] * len(after),
        out_specs=[HBM_SPEC] * n, input_output_aliases={i: i for i in range(n)},
        compiler_params=pltpu.CompilerParams(has_side_effects=DATAFLOW),
    )(*arrays, sems[0], sems[1], *after)
    return list(out)


def _remote(src, dst, send_sems, recv_sems, k, to):
    return pltpu.make_async_remote_copy(src_ref=src, dst_ref=dst, send_sem=send_sems.at[k], recv_sem=recv_sems.at[k],
                                        device_id=to, device_id_type=MESH)


def _plan_gather_direct(refs, send_sems, recv_sems):
    x, y, c = _place()
    me = 4 * x + 2 * y + c
    peers = [(x, y, 1 - c), (1 - x, y, c), (x, 1 - y, c), (1 - x, 1 - y, c)]
    return [_remote(ref.at[me], ref.at[me], send_sems, recv_sems, 4 * a + k, peer)
            for a, ref in enumerate(refs) for k, peer in enumerate(peers)]


def _plan_gather_pass_on(refs, send_sems, recv_sems):
    x, y, c = _place()
    chips = [(1 - x, y), (x, 1 - y), (1 - x, 1 - y)]
    return [_remote(ref.at[4 * px + 2 * py + c], ref.at[4 * px + 2 * py + c], send_sems, recv_sems, 3 * a + k,
                    (x, y, 1 - c))
            for a, ref in enumerate(refs) for k, (px, py) in enumerate(chips)]


def _plan_in_chip(refs, send_sems, recv_sems):
    n = len(refs) // 2
    x, y, c = _place()
    return [_remote(refs[a].at[:, 1 - c], refs[n + a], send_sems, recv_sems, a, (x, y, 1 - c)) for a in range(n)]


def _plan_between_chips(refs, send_sems, recv_sems):
    n = len(refs) // 2
    x, y, c = _place()
    chips = [(1 - x, y), (x, 1 - y), (1 - x, 1 - y)]
    return [_remote(refs[a].at[2 * px + py], refs[n + a].at[2 * x + y], send_sems, recv_sems, 3 * a + k, (px, py, c))
            for a in range(n) for k, (px, py) in enumerate(chips)]


def _plan_late_between(refs, send_sems, recv_sems):
    sums, land, small = refs
    x, y, c = _place()
    me = 4 * x + 2 * y + c
    copies = _plan_between_chips([sums, land], send_sems, recv_sems)
    peers = [(x ^ dx, y ^ dy, c ^ dc) for dx in range(2) for dy in range(2) for dc in range(2) if dx + dy + dc]
    return copies + [_remote(small.at[me], small.at[me], send_sems, recv_sems, 3 + k, peer)
                     for k, peer in enumerate(peers)]


def _adamw_math(w, g, m, v):
    m = ADAM_B1 * m + (1.0 - ADAM_B1) * g
    v = ADAM_B2 * v + (1.0 - ADAM_B2) * jnp.square(g)
    m_hat = m / (1.0 - ADAM_B1 ** ADAM_STEP)
    v_hat = v / (1.0 - ADAM_B2 ** ADAM_STEP)
    delta = -ADAM_LR * (m_hat / (jnp.sqrt(v_hat) + ADAM_EPS) + ADAM_WD * w)
    return delta, m, v


def _adamw(g, w, m, v, name):
    r, cdim = g.shape
    tr = 256 if r % 256 == 0 else r

    def body(g_ref, w_ref, m_ref, v_ref, d_out, m_out, v_out):
        d_out[...], m_out[...], v_out[...] = _adamw_math(w_ref[...], g_ref[...], m_ref[...], v_ref[...])

    blk = pl.BlockSpec((tr, cdim), lambda i: (i, 0))
    return pl.pallas_call(
        body, name=name, grid=(r // tr,), in_specs=[blk] * 4, out_specs=[blk] * 3,
        out_shape=[jax.ShapeDtypeStruct((r, cdim), F32)] * 3, compiler_params=_params("parallel"),
    )(g, w, m, v)


SMALL = ("b_forget", "rel_bias", "swa_sinks", "g_attn_pre", "g_attn_post", "g_ff_pre", "g_ff_post", "g_ple_post")
SMALL_ROW = dict(g_attn_pre=0, g_attn_post=1, g_ff_pre=2, g_ff_post=3, g_ple_post=4, b_forget=5, swa_sinks=6,
                 loss=7, rel_bias=8)


def _pack_small(t):
    wide = lambda a: jnp.pad(a.astype(F32), ((0, 0), (0, D_MODEL - a.shape[1])))
    order = sorted(SMALL_ROW, key=SMALL_ROW.get)
    return jnp.concatenate([wide(t[k].reshape(-1, t[k].shape[-1]) if t[k].ndim else t[k].reshape(1, 1))
                            for k in order], axis=0)


def _adamw_small(blocks, w, m, v):
    n = len(SMALL)

    def body(blk_ref, *refs):
        ins, outs = refs[:3 * n], refs[3 * n:]
        total = blk_ref[0]
        for d in range(1, N_DEV):
            total = total + blk_ref[d]
        outs[0][...] = total[SMALL_ROW["loss"]:SMALL_ROW["loss"] + 1, 0:1]
        for i, k in enumerate(SMALL):
            rows, cols = ins[3 * i].shape
            g = total[SMALL_ROW[k]:SMALL_ROW[k] + rows, 0:cols]
            delta, m_new, v_new = _adamw_math(ins[3 * i][...], g, ins[3 * i + 1][...], ins[3 * i + 2][...])
            for out, val in zip(outs[1 + 4 * i:5 + 4 * i], (g, delta, m_new, v_new)):
                out[...] = val

    shapes = [jax.ShapeDtypeStruct(w[k].shape, F32) for k in SMALL for _ in range(4)]
    out = pl.pallas_call(
        body, name="adamw_small", out_shape=[jax.ShapeDtypeStruct((1, 1), F32)] + shapes,
        in_specs=[_resident()] * (1 + 3 * n), out_specs=[_resident()] * (1 + 4 * n),
    )(blocks, *[t[k] for k in SMALL for t in (w, m, v)])
    kinds = [{k: out[1 + 4 * i + j] for i, k in enumerate(SMALL)} for j in range(4)]
    return (out[0], *kinds)


def _sum_chips(parts, sums, name):
    _, r, cdim = parts.shape

    def body(chip_ref, p_ref, own_ref, g_out):
        g = None
        for k in range(4):
            term = jnp.where(chip_ref[0] == k, own_ref[k], p_ref[k]).astype(F32)
            g = term if g is None else g + term
        g_out[...] = g

    my_chip = (2 * lax.axis_index("x") + lax.axis_index("y")).astype(jnp.int32).reshape(1)
    return pl.pallas_call(
        body, name=name, out_shape=jax.ShapeDtypeStruct((r, cdim), F32),
        in_specs=[pl.BlockSpec(memory_space=pltpu.SMEM), _resident(), _resident()], out_specs=_resident(),
    )(my_chip, parts, sums)


def _adamw_chips(parts, sums, w, m, v, name):
    _, r, cdim = parts.shape
    tr = 256 if r % 256 == 0 else r

    def body(chip_ref, p_ref, own_ref, w_ref, m_ref, v_ref, g_out, d_out, m_out, v_out):
        g = None
        for k in range(4):
            term = jnp.where(chip_ref[0] == k, own_ref[...], p_ref[k]).astype(F32)
            g = term if g is None else g + term
        delta, m_new, v_new = _adamw_math(w_ref[...], g, m_ref[...], v_ref[...])
        g_out[...] = g
        d_out[...] = delta
        m_out[...] = m_new
        v_out[...] = v_new

    blk = pl.BlockSpec((tr, cdim), lambda i, chip: (i, 0))
    my_chip = (2 * lax.axis_index("x") + lax.axis_index("y")).astype(jnp.int32).reshape(1)
    return pl.pallas_call(
        body, name=name,
        grid_spec=pltpu.PrefetchScalarGridSpec(
            num_scalar_prefetch=1, grid=(r // tr,),
            in_specs=[pl.BlockSpec((4, tr, cdim), lambda i, chip: (0, i, 0)),
                      pl.BlockSpec((None, tr, cdim), lambda i, chip: (chip[0], i, 0)), blk, blk, blk],
            out_specs=[blk] * 4),
        out_shape=[jax.ShapeDtypeStruct((r, cdim), F32)] * 4,
        compiler_params=_params("parallel"),
    )(my_chip, parts, sums, w, m, v)


class _NoExchange:
    def __init__(self, weights):
        self.weights = weights

    def before_pre_attn(self, g1):
        return g1

    def after_fox_fwd(self, fox_o, sinks_slot):
        return sinks_slot

    def after_attention(self, swa_o):
        return self.weights

    def after_early_grads(self, grads, d_col):
        return d_col

    def after_swa_bwd(self, dsq, d_row3):
        return d_row3

    def after_w_in_grad(self, d_win, g1):
        return g1


def _slot_order(t, axis):
    shp = t.shape
    t = t.reshape(shp[:axis] + (2, 4, shp[axis] // N_HEADS) + shp[axis + 1:])
    return jnp.swapaxes(t, axis, axis + 1).reshape(shp)


def _head_order(t, axis):
    shp = t.shape
    t = t.reshape(shp[:axis] + (4, 2, shp[axis] // N_HEADS) + shp[axis + 1:])
    return jnp.swapaxes(t, axis, axis + 1).reshape(shp)


def _forward_backward(x, p, target, win_t, hooks, b_forget, rel_bias, sinks, g1, g2, g3, g4, g5):
    S = x.shape[0]
    tm = 256
    tm_mlp = 512
    t = 256
    q0 = 3 * D_ATT + N_HEADS
    win_t = win_t.reshape(D_IN, D_MODEL)
    wt = jnp.concatenate(
        [win_t[:q0], jnp.zeros((8, D_MODEL), MM), _slot_order(win_t[q0:q0 + D_ATT], 0), win_t[q0 + D_ATT:]], axis=0)
    bcol = jnp.pad(b_forget.reshape(N_HEADS, 1), ((0, 8), (0, 0)))
    rel_bias_slot = rel_bias[:, np.array(SLOT_HEAD)]
    sinks_slot = sinks.reshape(N_HEADS)[np.array(SLOT_HEAD)]
    bucket = jnp.asarray(_swa_bucket_map())

    a, fqkv, sqkv, fft = _pre_attn(x, hooks.before_pre_attn(g1), wt, tm)
    c_row = _forget_cumsum(fft, bcol)
    c_col = c_row[:N_HEADS].T
    c_row3 = c_row[:N_HEADS].reshape(N_HEADS, S // t, t)
    fox_o, fox_lse = _fox_fwd(fqkv, c_row3, tq=512, tk=t)
    biasm = _swa_bias(rel_bias_slot, bucket)
    sinks_slot = hooks.after_fox_fwd(fox_o, sinks_slot)
    swa_o, swa_lse = _swa_fwd(sqkv, biasm, sinks_slot)
    wout, w1, w2, wple, wg = hooks.after_attention(swa_o)
    wout_fox = wout[:D_ATT]
    wout_swa = _slot_order(wout[D_ATT:], 0)
    mix, h1, m = _post_attn(x, fox_o, swa_o, wout_fox, wout_swa, g2, g3, tm)
    u, y, h2 = _mlp_fwd(m, h1, w1, w2, g4, tm_mlp)
    dh2, dpe, dgl, dg5, loss = _ple_loss(h2, p, target, wg, wple, g5, tm)

    d_wple = _weight_grad(p, dpe, "grad_w_ple", tk=D_PLE, n_chunks=N_DEV)
    d_wg = _weight_grad(h2, dgl, "grad_w_ple_gate", tk=256)
    dh1, dy, du, dg4, dg3 = _mlp_bwd(dh2, y, h1, u, w1, w2, g4, g3, tm)
    d_w2 = _weight_grad(u, dy, "grad_w_ff2", tk=256, relu2=True)
    d_w1 = _weight_grad(m, du, "grad_w_ff1", tk=256, n_chunks=N_DEV)
    head = np.arange(D_ATT) // HEAD_DIM
    head_rows = jnp.asarray((head[None, :] == np.arange(N_HEADS)[:, None]).astype(np.float32))
    dmix, dcat, d_row, d_swa, dg2 = _attn_out_bwd(dh1, mix, fox_o, swa_o, wout_fox, wout_swa, g2, head_rows, tm)
    d_col = d_swa.T
    d_wout_fox = _weight_grad(fox_o, dmix, "grad_w_out_fox", tk=256)
    d_wout_swa = _weight_grad(swa_o, dmix, "grad_w_out_swa", tk=256)
    d_wout = jnp.concatenate([d_wout_fox, _head_order(d_wout_swa, 0)], axis=0).reshape(N_DEV, D_MODEL // N_DEV, D_MODEL)
    early = dict(w_ff1=d_w1, w_ff2=d_w2.reshape(N_DEV, FF_CHUNK, D_MODEL), w_ple=d_wple,
                 w_ple_gate=d_wg.reshape(N_DEV, D_MODEL // N_DEV, D_MODEL), w_out=d_wout)

    d_col = hooks.after_early_grads(early, d_col)
    dsq, dsk, dsv, d_rb_slot, d_sink_slot = _swa_bwd(sqkv, dcat, biasm, sinks_slot, bucket, swa_lse, d_col)
    lse_row3 = fox_lse.T.reshape(N_HEADS, S // t, t)
    d_row3 = hooks.after_swa_bwd(dsq, d_row.reshape(N_HEADS, S // t, t))
    dq_fox, dk_fox, dv_fox, dc_col, dcq = _fox_bwd(fqkv, dcat, lse_row3, d_row3, c_col, tq=t, tk=512)
    dc_row = jnp.pad(dc_col.T + dcq.reshape(N_HEADS, S), ((0, 8), (0, 0)))
    dff_t, db = _forget_bwd(dc_row, fft, bcol)
    dz = _pack_dz(dq_fox, dk_fox, dv_fox, dsq, dsk, dsv, 512)
    d_wmain = _weight_grad(dz, a, "grad_w_in", tk=256)
    d_wff_t = _forget_weight_grad(dff_t, a)

    sq0 = 3 * D_ATT
    d_win = jnp.concatenate(
        [d_wmain[:sq0], d_wff_t[:N_HEADS].astype(MM), _head_order(d_wmain[sq0:sq0 + D_ATT], 0),
         d_wmain[sq0 + D_ATT:]], axis=0)
    d_win = d_win.reshape(N_DEV, D_IN // N_DEV, D_MODEL)
    grad_x, dg1 = _pre_attn_bwd(x, dh1, dz, dff_t, wt, hooks.after_w_in_grad(d_win, g1), tm)
    big = dict(early, w_in=d_win)
    small = dict(b_forget=db[:N_HEADS].reshape(1, N_HEADS), rel_bias=d_rb_slot[:, np.array(HEAD_SLOT)],
                 swa_sinks=d_sink_slot[:, np.array(HEAD_SLOT)], g_attn_pre=dg1, g_attn_post=dg2, g_ff_pre=dg3,
                 g_ff_post=dg4, g_ple_post=dg5)
    return loss, grad_x, big, small


BIG = ("w_in", "w_out", "w_ff1", "w_ff2", "w_ple", "w_ple_gate")
WEIGHTS = ("w_in", "b_forget", "w_out", "rel_bias", "swa_sinks", "g_attn_pre", "g_attn_post", "w_ff1", "w_ff2",
           "g_ff_pre", "g_ff_post", "w_ple", "w_ple_gate", "g_ple_post")


EARLY = ("w_ff1", "w_ff2", "w_ple", "w_ple_gate", "w_out")


class _Overlap:
    def __init__(self, later):
        self.later = later

    def before_pre_attn(self, g1):
        self.gather_sems, self.later, token = _exchange_start("gather_rest_start", self.later, 4 * 5, _plan_gather_direct)
        return g1 + token

    def after_fox_fwd(self, fox_o, sinks_slot):
        later = _exchange_wait("gather_rest_wait", self.later, self.gather_sems, fox_o, _plan_gather_direct)
        self.pass_sems, self.later, token = _exchange_start("gather_pass_on_start", later, 3 * 5, _plan_gather_pass_on)
        return sinks_slot + token[0, :N_HEADS]

    def after_attention(self, swa_o):
        wout_g, w1_g, w2_g, wple_g, wg_g = _exchange_wait("gather_pass_on_wait", self.later, self.pass_sems, swa_o,
                                                         _plan_gather_pass_on)
        return (wout_g.reshape(D_MODEL, D_MODEL), w1_g, w2_g.reshape(D_FF, D_MODEL),
                jnp.moveaxis(wple_g, 0, 1).reshape(D_PLE, D_MODEL), wg_g.reshape(D_MODEL, D_MODEL))

    def after_early_grads(self, grads, d_col):
        views = [grads[k].reshape((4, 2) + grads[k].shape[1:]) for k in EARLY]
        lands = [lax.empty((4,) + grads[k].shape[1:], MM) for k in EARLY]
        self.in_chip_sems, self.in_chip, token = _exchange_start("grads_in_chip_start", views + lands, len(EARLY),
                                                                 _plan_in_chip)
        return d_col + token[0, 0]

    def after_swa_bwd(self, dsq, d_row3):
        arrays = _exchange_wait("grads_in_chip_wait", self.in_chip, self.in_chip_sems, dsq, _plan_in_chip)
        n = len(EARLY)
        sums = [_chip_sum(arrays[a], arrays[n + a], "chip_sum_" + k) for a, k in enumerate(EARLY)]
        lands = [lax.empty(s.shape, s.dtype) for s in sums]
        self.between_sems, self.between, token = _exchange_start("grads_between_chips_start", sums + lands, 3 * n,
                                                                 _plan_between_chips)
        return d_row3 + token[0, 0]

    def after_w_in_grad(self, d_win, g1):
        self.late_in_chip_sems, self.late_in_chip, token = _exchange_start(
            "late_in_chip_start", [d_win.reshape((4, 2) + d_win.shape[1:]), lax.empty((4,) + d_win.shape[1:], MM)],
            1, _plan_in_chip)
        return g1 + token

    def finish(self, after):
        arrays = _exchange_wait("grads_between_chips_wait", self.between, self.between_sems, after,
                                _plan_between_chips)
        n = len(EARLY)
        self.sums = arrays[:n]
        return arrays[n:]


def kernel(x, p, w_in, b_forget, w_out, rel_bias, swa_sinks, g_attn_pre, g_attn_post, w_ff1, w_ff2, g_ff_pre, g_ff_post, w_ple, w_ple_gate, g_ple_post, loss_target, m_w_in, m_b_forget, m_w_out, m_rel_bias, m_swa_sinks, m_g_attn_pre, m_g_attn_post, m_w_ff1, m_w_ff2, m_g_ff_pre, m_g_ff_post, m_w_ple, m_w_ple_gate, m_g_ple_post, v_w_in, v_b_forget, v_w_out, v_rel_bias, v_swa_sinks, v_g_attn_pre, v_g_attn_post, v_w_ff1, v_w_ff2, v_g_ff_pre, v_g_ff_post, v_w_ple, v_w_ple_gate, v_g_ple_post):
    w = dict(w_in=w_in, b_forget=b_forget, w_out=w_out, rel_bias=rel_bias, swa_sinks=swa_sinks,
             g_attn_pre=g_attn_pre, g_attn_post=g_attn_post, w_ff1=w_ff1, w_ff2=w_ff2, g_ff_pre=g_ff_pre,
             g_ff_post=g_ff_post, w_ple=w_ple, w_ple_gate=w_ple_gate, g_ple_post=g_ple_post)
    mom = dict(w_in=m_w_in, b_forget=m_b_forget, w_out=m_w_out, rel_bias=m_rel_bias, swa_sinks=m_swa_sinks,
               g_attn_pre=m_g_attn_pre, g_attn_post=m_g_attn_post, w_ff1=m_w_ff1, w_ff2=m_w_ff2,
               g_ff_pre=m_g_ff_pre, g_ff_post=m_g_ff_post, w_ple=m_w_ple, w_ple_gate=m_w_ple_gate,
               g_ple_post=m_g_ple_post)
    var = dict(w_in=v_w_in, b_forget=v_b_forget, w_out=v_w_out, rel_bias=v_rel_bias, swa_sinks=v_swa_sinks,
               g_attn_pre=v_g_attn_pre, g_attn_post=v_g_attn_post, w_ff1=v_w_ff1, w_ff2=v_w_ff2,
               g_ff_pre=v_g_ff_pre, g_ff_post=v_g_ff_post, w_ple=v_w_ple, w_ple_gate=v_w_ple_gate,
               g_ple_post=v_g_ple_post)

    turn = lambda t, k: t.T if k == "w_in" else t
    me = 4 * lax.axis_index("x") + 2 * lax.axis_index("y") + lax.axis_index("c")

    def stack(block):
        return lax.dynamic_update_slice_in_dim(lax.empty((N_DEV,) + block.shape, block.dtype), block[None], me, 0)

    stacks = [stack(turn(w[k][0], k).astype(MM)) for k in BIG]
    win_g, later = _all_gather_sequencer(stacks[0]), stacks[1:]
    hooks = _Overlap(later)
    loss, grad_x, big, small = _forward_backward(
        x[0], p[0, 0], loss_target[0], win_g, hooks, b_forget, rel_bias, swa_sinks,
        g_attn_pre, g_attn_post, g_ff_pre, g_ff_post, g_ple_post)
    out_g, out_d, out_m, out_v = {}, {}, {}, {}

    def update(k, part, own):
        g, d, m_new, v_new = _adamw_chips(part, own, w[k][0], mom[k][0], var[k][0], "adamw_" + k)
        out_g[k], out_d[k], out_m[k], out_v[k] = g[None], d[None], m_new[None], v_new[None]
        return d

    view, other = _exchange_wait("late_in_chip_wait", hooks.late_in_chip, hooks.late_in_chip_sems, grad_x,
                                 _plan_in_chip)
    chip_sum = _chip_sum(view, other, "chip_sum_w_in")
    small["loss"] = loss
    between_sems, between, token = _exchange_start(
        "late_between_chips_start", [chip_sum, lax.empty(chip_sum.shape, MM), stack(_pack_small(small))], 3 + 7,
        _plan_late_between)
    early_parts = hooks.finish(token)
    done = [update(k, part, own) for k, part, own in zip(EARLY, early_parts, hooks.sums)]
    chip_sum, part, small_all = _exchange_wait("late_between_chips_wait", between, between_sems, done,
                                               _plan_late_between)
    g_win = _sum_chips(part, chip_sum, "sum_w_in").T
    d, m_new, v_new = _adamw(g_win, w_in[0], m_w_in[0], v_w_in[0], "adamw_w_in")
    out_g["w_in"], out_d["w_in"], out_m["w_in"], out_v["w_in"] = g_win[None], d[None], m_new[None], v_new[None]
    total_loss, g_s, d_s, m_s, v_s = _adamw_small(small_all, w, mom, var)
    for k in SMALL:
        out_g[k], out_d[k], out_m[k], out_v[k] = g_s[k], d_s[k], m_s[k], v_s[k]
    return (total_loss.reshape(()), grad_x[None], *[out_g[k] for k in WEIGHTS], *[out_d[k] for k in WEIGHTS],
            *[out_m[k] for k in WEIGHTS], *[out_v[k] for k in WEIGHTS])
```

```python
import functools

import numpy as np
import jax
import jax.numpy as jnp
from jax import lax
from jax.experimental import pallas as pl
from jax.experimental.pallas import tpu as pltpu
from jax.experimental.pallas import tpu_sc as plsc

F32 = jnp.float32
MM = jnp.bfloat16

D_MODEL = 1024
HEAD_DIM = 64
N_HEADS = 8
D_ATT = N_HEADS * HEAD_DIM
D_KV = 128
D_FF = 4096
D_PLE = 256
D_IN = 3 * D_ATT + N_HEADS + D_ATT + 2 * D_KV
N_DEV = 8
FF_CHUNK = D_FF // N_DEV
WINDOW = 128
N_BUCKETS = 32
MAX_DISTANCE = 128
RMS_EPS = 1e-6
Q_SCALE = HEAD_DIM ** -0.5
NEG = -1e30

ADAM_LR = 0.001
ADAM_B1 = 0.9
ADAM_B2 = 0.999
ADAM_EPS = 1e-08
ADAM_WD = 0.01
ADAM_STEP = 10

SLOT_HEAD = (0, 4, 1, 5, 2, 6, 3, 7)
HEAD_SLOT = (0, 2, 4, 6, 1, 3, 5, 7)

VMEM_LIMIT = 60 * 1024 * 1024
MESH = pl.DeviceIdType.MESH

NT = (((1,), (1,)), ((), ()))
TN = (((0,), (0,)), ((), ()))


def _params(*semantics):
    return pltpu.CompilerParams(dimension_semantics=semantics, vmem_limit_bytes=VMEM_LIMIT)


def _resident():
    return pl.BlockSpec(memory_space=pltpu.VMEM)


def _rows(tm, width):
    return pl.BlockSpec((tm, width), lambda i: (i, 0))


def _const(shape):
    return pl.BlockSpec(shape, lambda i: (0,) * len(shape))


def _dot(a, b):
    return jnp.dot(a, b, preferred_element_type=F32)


def _dot_nt(a, b):
    return lax.dot_general(a, b, NT, preferred_element_type=F32)


def _dot_tn(a, b):
    return lax.dot_general(a, b, TN, preferred_element_type=F32)


def _rms(xf):
    r = lax.rsqrt(jnp.mean(xf * xf, axis=-1, keepdims=True) + RMS_EPS)
    return xf * r, r


def _rms_bwd(dout, n, r, g):
    dg = jnp.sum(dout * n, axis=0, keepdims=True)
    dn = dout * g
    dx = r * (dn - n * jnp.mean(dn * n, axis=-1, keepdims=True))
    return dx, dg


def _accumulate(ref, value, step):
    @pl.when(step == 0)
    def _():
        ref[...] = value

    @pl.when(step != 0)
    def _():
        ref[...] += value


def _t5_bucket(n):
    max_exact = N_BUCKETS // 2
    large = max_exact + (np.log(np.maximum(n, 1) / max_exact) / np.log(MAX_DISTANCE / max_exact)
                         * (N_BUCKETS - max_exact)).astype(np.int32)
    large = np.minimum(large, N_BUCKETS - 1)
    return np.where(n < max_exact, n, large).astype(np.int32)


def _swa_bucket_map():
    i = np.arange(WINDOW)[:, None]
    j = np.arange(2 * WINDOW)[None, :]
    dist = i + WINDOW - j
    ok = (dist >= 0) & (dist < WINDOW)
    return np.where(ok, _t5_bucket(np.clip(dist, 0, None)), -1).astype(np.int32)


WT_FOX = 0
WT_FF = 3 * D_ATT
WT_SQ = WT_FF + 16
WT_SKV = WT_SQ + D_ATT
WT_ROWS = WT_SKV + 2 * D_KV


def _pre_attn(x, g1, wt, tm):
    S = x.shape[0]

    def body(x_ref, g_ref, wt_ref, a_ref, fqkv_ref, sqkv_ref, fft_ref):
        n, _ = _rms(x_ref[...])
        a = (n * g_ref[...]).astype(MM)
        a_ref[...] = a
        fqkv_ref[:, :D_ATT] = (_dot_nt(a, wt_ref[WT_FOX:WT_FOX + D_ATT]) * Q_SCALE).astype(MM)
        fqkv_ref[:, D_ATT:] = _dot_nt(a, wt_ref[WT_FOX + D_ATT:WT_FF]).astype(MM)
        sqkv_ref[:, :D_ATT] = (_dot_nt(a, wt_ref[WT_SQ:WT_SKV]) * Q_SCALE).astype(MM)
        sqkv_ref[:, D_ATT:] = _dot_nt(a, wt_ref[WT_SKV:WT_ROWS]).astype(MM)
        fft_ref[...] = _dot_nt(wt_ref[WT_FF:WT_SQ], a)

    return pl.pallas_call(
        body, name="pre_attn", grid=(S // tm,),
        in_specs=[_rows(tm, D_MODEL), _const((1, D_MODEL)), _resident()],
        out_specs=[_rows(tm, D_MODEL), _rows(tm, 3 * D_ATT), _rows(tm, D_ATT + 2 * D_KV),
                   pl.BlockSpec((16, tm), lambda i: (0, i))],
        out_shape=[jax.ShapeDtypeStruct((S, D_MODEL), MM), jax.ShapeDtypeStruct((S, 3 * D_ATT), MM),
                   jax.ShapeDtypeStruct((S, D_ATT + 2 * D_KV), MM), jax.ShapeDtypeStruct((16, S), F32)],
        compiler_params=_params("parallel"),
    )(x, g1, wt)


def _lane_scan(v, reverse):
    S = v.shape[1]
    lane = lax.broadcasted_iota(jnp.int32, v.shape, 1)
    k = 1
    while k < S:
        if reverse:
            v = v + jnp.where(lane < S - k, pltpu.roll(v, S - k, axis=1), 0.0)
        else:
            v = v + jnp.where(lane >= k, pltpu.roll(v, k, axis=1), 0.0)
        k *= 2
    return v


def _forget_cumsum(fft, bcol):
    def body(f_ref, b_ref, c_ref):
        z = f_ref[...] + b_ref[...]
        log_f = jnp.minimum(z, 0.0) - jnp.log1p(jnp.exp(-jnp.abs(z)))
        c_ref[...] = _lane_scan(log_f, reverse=False)

    return pl.pallas_call(
        body, name="forget_cumsum", out_shape=jax.ShapeDtypeStruct(fft.shape, F32),
        in_specs=[_resident(), _resident()], out_specs=_resident(),
    )(fft, bcol)


def _forget_bwd(dc_row, fft, bcol):
    def body(dc_ref, f_ref, b_ref, dff_ref, db_ref):
        z = f_ref[...] + b_ref[...]
        dlog_f = _lane_scan(dc_ref[...], reverse=True)
        dff = dlog_f * (1.0 / (1.0 + jnp.exp(z)))
        dff_ref[...] = dff
        db_ref[...] = jnp.sum(dff, axis=1, keepdims=True)

    return pl.pallas_call(
        body, name="forget_bwd",
        out_shape=[jax.ShapeDtypeStruct(fft.shape, F32), jax.ShapeDtypeStruct((fft.shape[0], 1), F32)],
        in_specs=[_resident()] * 3, out_specs=[_resident()] * 2,
    )(dc_row, fft, bcol)


def _head_select(shape, upper):
    lane = lax.broadcasted_iota(jnp.int32, shape, 1)
    return lane >= HEAD_DIM if upper else lane < HEAD_DIM


def _fox_fwd(fqkv, c_row3, tq, tk, pairs_per_loop=2, row_chunks=1):
    S = fqkv.shape[0]
    rq = tq // row_chunks
    n_band = tq // tk

    def body(q_ref, k_ref, v_ref, ck_ref, o_ref, lse_ref):
        qi = pl.program_id(0)
        row = lax.broadcasted_iota(jnp.int32, (rq, tk), 0)
        col = lax.broadcasted_iota(jnp.int32, (rq, tk), 1)
        low = _head_select((rq, 128), 0)
        for first in range(0, N_HEADS // 2, pairs_per_loop):
            pairs = range(first, first + pairs_per_loop)
            chains = [(pr, hh, rc) for pr in pairs for hh in range(2) for rc in range(row_chunks)]
            qh = {}
            for pr in pairs:
                for rc in range(row_chunks):
                    q2 = q_ref[rc * rq:(rc + 1) * rq, pr * 128:(pr + 1) * 128]
                    qh[pr, 0, rc] = jnp.where(low, q2, jnp.zeros_like(q2))
                    qh[pr, 1, rc] = jnp.where(low, jnp.zeros_like(q2), q2)

            def block(kb, carry, band, chains=chains, qh=qh):
                rows = pl.ds(pl.multiple_of(kb * tk, tk), tk)
                out = []
                for (pr, hh, rc), (m, l, acc) in zip(chains, carry):
                    if band is not None and (rc + 1) * rq <= band * tk:
                        out.append((m, l, acc))
                        continue
                    lanes = slice(pr * 128, (pr + 1) * 128)
                    s = _dot_nt(qh[pr, hh, rc], k_ref[rows, lanes]) - ck_ref[2 * pr + hh, pl.ds(kb, 1), :]
                    if band is not None:
                        s = jnp.where(row + rc * rq >= col + band * tk, s, NEG)
                    m_new = jnp.maximum(m, jnp.max(s, axis=-1, keepdims=True))
                    p = jnp.exp(s - m_new)
                    alpha = jnp.exp(m - m_new)
                    l = alpha * l + jnp.sum(p, axis=-1, keepdims=True)
                    acc = alpha * acc + _dot(p.astype(MM), v_ref[rows, lanes])
                    out.append((m_new, l, acc))
                return tuple(out)

            carry = tuple((jnp.full((rq, 1), NEG, F32), jnp.zeros((rq, 1), F32), jnp.zeros((rq, 128), F32))
                          for _ in chains)
            carry = lax.fori_loop(0, qi * n_band, functools.partial(block, band=None), carry)
            for band in range(n_band):
                carry = block(qi * n_band + band, carry, band=band)
            res = {}
            for (pr, hh, rc), (m, l, acc) in zip(chains, carry):
                res[pr, hh, rc] = acc / l
                lse_ref[rc * rq:(rc + 1) * rq, 2 * pr + hh:2 * pr + hh + 1] = m + jnp.log(l)
            for pr in pairs:
                for rc in range(row_chunks):
                    o_ref[rc * rq:(rc + 1) * rq, pr * 128:(pr + 1) * 128] = jnp.where(
                        low, res[pr, 0, rc], res[pr, 1, rc]).astype(MM)

    return pl.pallas_call(
        body, name="fox_fwd", grid=(S // tq,),
        in_specs=[pl.BlockSpec((tq, D_ATT), lambda i: (i, 0)), pl.BlockSpec((S, D_ATT), lambda i: (0, 1)),
                  pl.BlockSpec((S, D_ATT), lambda i: (0, 2)), _resident()],
        out_specs=[_rows(tq, D_ATT), _rows(tq, N_HEADS)],
        out_shape=[jax.ShapeDtypeStruct((S, D_ATT), MM), jax.ShapeDtypeStruct((S, N_HEADS), F32)],
        compiler_params=_params("parallel"),
    )(fqkv, fqkv, fqkv, c_row3)


def _swa_bias(rel_bias_slot, bucket):
    def body(rb_ref, bk_ref, out_ref):
        bk = bk_ref[...]
        for s in range(N_HEADS):
            acc = jnp.where(bk < 0, NEG, 0.0).astype(F32)
            for b in range(N_BUCKETS):
                acc = jnp.where(bk == b, rb_ref[b, s], acc)
            out_ref[s] = acc

    return pl.pallas_call(
        body, name="swa_bias", out_shape=jax.ShapeDtypeStruct((N_HEADS, WINDOW, 2 * WINDOW), F32),
        in_specs=[pl.BlockSpec(memory_space=pltpu.SMEM), _resident()], out_specs=_resident(),
    )(rel_bias_slot, bucket)


def _stack4(piece):
    return jnp.concatenate([piece(j) for j in range(4)], axis=0)


def _swa_specs(S):
    q = pl.BlockSpec((WINDOW, D_ATT), lambda n: (n, 0))
    kp = pl.BlockSpec((WINDOW, D_KV), lambda n: (jnp.maximum(n - 1, 0), 4))
    kc = pl.BlockSpec((WINDOW, D_KV), lambda n: (n, 4))
    vp = pl.BlockSpec((WINDOW, D_KV), lambda n: (jnp.maximum(n - 1, 0), 5))
    vc = pl.BlockSpec((WINDOW, D_KV), lambda n: (n, 5))
    return [q, kp, kc, vp, vc]


def _swa_fwd(sqkv, biasm, sinks_slot):
    S = sqkv.shape[0]

    def body(q_ref, kp_ref, kc_ref, vp_ref, vc_ref, bias_ref, sink_ref, o_ref, lse_ref):
        n = pl.program_id(0)
        no_prev = jnp.where(n > 0, 0.0, NEG)
        low = _head_select((WINDOW, 128), 0)
        res = []
        for g in range(2):
            sel = low if g == 0 else jnp.logical_not(low)
            qg = _stack4(lambda j: jnp.where(sel, q_ref[:, j * 128:(j + 1) * 128], jnp.zeros((WINDOW, 128), MM)))
            sink = _stack4(lambda j: jnp.full((WINDOW, 1), sink_ref[2 * j + g], F32))
            sp = _dot_nt(qg, kp_ref[...]) + _stack4(lambda j: bias_ref[2 * j + g, :, :WINDOW]) + no_prev
            sc = _dot_nt(qg, kc_ref[...]) + _stack4(lambda j: bias_ref[2 * j + g, :, WINDOW:])
            m = jnp.maximum(jnp.maximum(jnp.max(sp, axis=-1, keepdims=True),
                                        jnp.max(sc, axis=-1, keepdims=True)), sink)
            ep = jnp.exp(sp - m)
            ec = jnp.exp(sc - m)
            den = jnp.sum(ep, axis=-1, keepdims=True) + jnp.sum(ec, axis=-1, keepdims=True) + jnp.exp(sink - m)
            res.append((_dot(ep.astype(MM), vp_ref[...]) + _dot(ec.astype(MM), vc_ref[...])) / den)
            lse = m + jnp.log(den)
            for j in range(4):
                lse_ref[:, 2 * j + g:2 * j + g + 1] = lse[j * WINDOW:(j + 1) * WINDOW]
        for j in range(4):
            rows = slice(j * WINDOW, (j + 1) * WINDOW)
            o_ref[:, j * 128:(j + 1) * 128] = jnp.where(low, res[0][rows], res[1][rows]).astype(MM)

    return pl.pallas_call(
        body, name="swa_fwd", grid=(S // WINDOW,),
        in_specs=_swa_specs(S) + [_resident(), pl.BlockSpec(memory_space=pltpu.SMEM)],
        out_specs=[_rows(WINDOW, D_ATT), _rows(WINDOW, N_HEADS)],
        out_shape=[jax.ShapeDtypeStruct((S, D_ATT), MM), jax.ShapeDtypeStruct((S, N_HEADS), F32)],
        compiler_params=_params("parallel"),
    )(sqkv, sqkv, sqkv, sqkv, sqkv, biasm, sinks_slot)


def _post_attn(x, fox_o, swa_o, wout_fox, wout_swa, g2, g3, tm):
    S = x.shape[0]

    def body(x_ref, fo_ref, so_ref, wf_ref, ws_ref, g2_ref, g3_ref, mix_ref, h1_ref, m_ref):
        mix = _dot(fo_ref[...], wf_ref[...]) + _dot(so_ref[...], ws_ref[...])
        mix_ref[...] = mix
        n2, _ = _rms(mix)
        h1 = x_ref[...] + n2 * g2_ref[...]
        h1_ref[...] = h1
        n3, _ = _rms(h1)
        m_ref[...] = (n3 * g3_ref[...]).astype(MM)

    return pl.pallas_call(
        body, name="post_attn", grid=(S // tm,),
        in_specs=[_rows(tm, D_MODEL), _rows(tm, D_ATT), _rows(tm, D_ATT), _resident(), _resident(),
                  _const((1, D_MODEL)), _const((1, D_MODEL))],
        out_specs=[_rows(tm, D_MODEL)] * 3,
        out_shape=[jax.ShapeDtypeStruct((S, D_MODEL), F32), jax.ShapeDtypeStruct((S, D_MODEL), F32),
                   jax.ShapeDtypeStruct((S, D_MODEL), MM)],
        compiler_params=_params("parallel"),
    )(x, fox_o, swa_o, wout_fox, wout_swa, g2, g3)


def _mlp_fwd(m, h1, w1, w2, g4, tm):
    S = m.shape[0]

    def body(m_ref, h1_ref, w1_ref, w2_ref, g4_ref, u_ref, y_ref, h2_ref):
        mb = m_ref[...]
        y = jnp.zeros((tm, D_MODEL), F32)
        for j in range(N_DEV):
            cols = slice(j * FF_CHUNK, (j + 1) * FF_CHUNK)
            u = _dot(mb, w1_ref[j])
            u_ref[:, cols] = u.astype(MM)
            y = y + _dot(jnp.square(jnp.maximum(u, 0.0)).astype(MM), w2_ref[cols, :])
        y_ref[...] = y
        n4, _ = _rms(y)
        h2_ref[...] = h1_ref[...] + n4 * g4_ref[...]

    return pl.pallas_call(
        body, name="mlp_fwd", grid=(S // tm,),
        in_specs=[_rows(tm, D_MODEL), _rows(tm, D_MODEL), _resident(), _resident(), _const((1, D_MODEL))],
        out_specs=[_rows(tm, D_FF), _rows(tm, D_MODEL), _rows(tm, D_MODEL)],
        out_shape=[jax.ShapeDtypeStruct((S, D_FF), MM), jax.ShapeDtypeStruct((S, D_MODEL), F32),
                   jax.ShapeDtypeStruct((S, D_MODEL), F32)],
        compiler_params=_params("parallel"),
    )(m, h1, w1, w2, g4)


def _ple_loss(h2, p, target, wg, wple, g5, tm):
    S = h2.shape[0]

    def body(h2_ref, p_ref, t_ref, wg_ref, wp_ref, g5_ref, dh2_ref, dpe_ref, dgl_ref, dg5_ref, loss_ref):
        i = pl.program_id(0)
        h2 = h2_ref[...]
        gate = jax.nn.sigmoid(_dot(h2.astype(MM), wg_ref[...]))
        pe = _dot(p_ref[...].astype(MM), wp_ref[...])
        n5, r5 = _rms(pe * gate)
        g5 = g5_ref[...]
        diff = h2 + n5 * g5 - t_ref[...]
        per_token = jnp.mean(jnp.square(diff), axis=-1, keepdims=True)
        _accumulate(loss_ref, 0.5 * jnp.sum(per_token, axis=0, keepdims=True), i)
        dh3 = diff * (1.0 / D_MODEL)
        de, dg5 = _rms_bwd(dh3, n5, r5, g5)
        _accumulate(dg5_ref, dg5, i)
        dpe_ref[...] = (de * gate).astype(MM)
        dgl = (de * pe * gate * (1.0 - gate)).astype(MM)
        dgl_ref[...] = dgl
        dh2_ref[...] = dh3 + _dot_nt(dgl, wg_ref[...])

    return pl.pallas_call(
        body, name="ple_loss", grid=(S // tm,),
        in_specs=[_rows(tm, D_MODEL), _rows(tm, D_PLE), _rows(tm, D_MODEL), _resident(), _resident(),
                  _const((1, D_MODEL))],
        out_specs=[_rows(tm, D_MODEL), _rows(tm, D_MODEL), _rows(tm, D_MODEL), _const((1, D_MODEL)), _const((1, 1))],
        out_shape=[jax.ShapeDtypeStruct((S, D_MODEL), F32), jax.ShapeDtypeStruct((S, D_MODEL), MM),
                   jax.ShapeDtypeStruct((S, D_MODEL), MM), jax.ShapeDtypeStruct((1, D_MODEL), F32),
                   jax.ShapeDtypeStruct((1, 1), F32)],
        compiler_params=_params("arbitrary"),
    )(h2, p, target, wg, wple, g5)


def _mlp_bwd(dh2, y, h1, u, w1, w2, g4, g3, tm):
    S = dh2.shape[0]

    def body(dh2_ref, y_ref, h1_ref, u_ref, w1_ref, w2_ref, g4_ref, g3_ref,
             dh1_ref, dy_ref, du_ref, dg4_ref, dg3_ref):
        i = pl.program_id(0)
        dh2 = dh2_ref[...]
        n4, r4 = _rms(y_ref[...])
        dy, dg4 = _rms_bwd(dh2, n4, r4, g4_ref[...])
        _accumulate(dg4_ref, dg4, i)
        dyb = dy.astype(MM)
        dy_ref[...] = dyb
        dm = jnp.zeros((tm, D_MODEL), F32)
        for j in range(N_DEV):
            cols = slice(j * FF_CHUNK, (j + 1) * FF_CHUNK)
            dact = _dot_nt(dyb, w2_ref[cols, :])
            du = (dact * (2.0 * jnp.maximum(u_ref[:, cols].astype(F32), 0.0))).astype(MM)
            du_ref[:, cols] = du
            dm = dm + _dot_nt(du, w1_ref[j])
        n3, r3 = _rms(h1_ref[...])
        dx, dg3 = _rms_bwd(dm, n3, r3, g3_ref[...])
        _accumulate(dg3_ref, dg3, i)
        dh1_ref[...] = dh2 + dx

    return pl.pallas_call(
        body, name="mlp_bwd", grid=(S // tm,),
        in_specs=[_rows(tm, D_MODEL), _rows(tm, D_MODEL), _rows(tm, D_MODEL), _rows(tm, D_FF),
                  _resident(), _resident(), _const((1, D_MODEL)), _const((1, D_MODEL))],
        out_specs=[_rows(tm, D_MODEL), _rows(tm, D_MODEL), _rows(tm, D_FF), _const((1, D_MODEL)),
                   _const((1, D_MODEL))],
        out_shape=[jax.ShapeDtypeStruct((S, D_MODEL), F32), jax.ShapeDtypeStruct((S, D_MODEL), MM),
                   jax.ShapeDtypeStruct((S, D_FF), MM), jax.ShapeDtypeStruct((1, D_MODEL), F32),
                   jax.ShapeDtypeStruct((1, D_MODEL), F32)],
        compiler_params=_params("arbitrary"),
    )(dh2, y, h1, u, w1, w2, g4, g3)


def _attn_out_bwd(dh1, mix, fox_o, swa_o, wout_fox, wout_swa, g2, head_rows, tm):
    S = dh1.shape[0]

    def body(dh1_ref, mix_ref, fo_ref, so_ref, wf_ref, ws_ref, g2_ref, er_ref,
             dmix_ref, dcat_ref, drow_ref, dswa_ref, dg2_ref):
        i = pl.program_id(0)
        n2, r2 = _rms(mix_ref[...])
        dmix, dg2 = _rms_bwd(dh1_ref[...], n2, r2, g2_ref[...])
        _accumulate(dg2_ref, dg2, i)
        dmb = dmix.astype(MM)
        dmix_ref[...] = dmb
        dfo = _dot_nt(dmb, wf_ref[...]).astype(MM)
        dso = _dot_nt(dmb, ws_ref[...]).astype(MM)
        dcat_ref[:, :D_ATT] = dfo
        dcat_ref[:, D_ATT:] = dso
        hi = lax.Precision.HIGHEST
        prod_f = dfo.astype(F32) * fo_ref[...].astype(F32)
        prod_s = dso.astype(F32) * so_ref[...].astype(F32)
        drow_ref[...] = lax.dot_general(er_ref[...], prod_f, NT, precision=hi, preferred_element_type=F32)
        dswa_ref[...] = lax.dot_general(er_ref[...], prod_s, NT, precision=hi, preferred_element_type=F32)

    return pl.pallas_call(
        body, name="attn_out_bwd", grid=(S // tm,),
        in_specs=[_rows(tm, D_MODEL), _rows(tm, D_MODEL), _rows(tm, D_ATT), _rows(tm, D_ATT), _resident(),
                  _resident(), _const((1, D_MODEL)), _resident()],
        out_specs=[_rows(tm, D_MODEL), _rows(tm, D_MODEL), pl.BlockSpec((N_HEADS, tm), lambda i: (0, i)),
                   pl.BlockSpec((N_HEADS, tm), lambda i: (0, i)), _const((1, D_MODEL))],
        out_shape=[jax.ShapeDtypeStruct((S, D_MODEL), MM), jax.ShapeDtypeStruct((S, D_MODEL), MM),
                   jax.ShapeDtypeStruct((N_HEADS, S), F32), jax.ShapeDtypeStruct((N_HEADS, S), F32),
                   jax.ShapeDtypeStruct((1, D_MODEL), F32)],
        compiler_params=_params("arbitrary"),
    )(dh1, mix, fox_o, swa_o, wout_fox, wout_swa, g2, head_rows)


def _fox_bwd(fqkv, dcat, lse_row3, d_row3, c_col, tq, tk, pairs_per_loop=2):
    S = fqkv.shape[0]
    n_blk = S // tk
    n_qblk = S // tq
    n_band = tk // tq

    def body(q_ref, k_ref, v_ref, do_ref, lse_ref, dd_ref, ck_ref, dq_ref, dk_ref, dv_ref, dc_ref, dcq_ref):
        kb = pl.program_id(0)

        @pl.when(kb == 0)
        def _():
            dq_ref[...] = jnp.zeros_like(dq_ref)
            dcq_ref[...] = jnp.zeros_like(dcq_ref)

        key = lax.broadcasted_iota(jnp.int32, (tk, tq), 0)
        qry = lax.broadcasted_iota(jnp.int32, (tk, tq), 1)
        low = _head_select((tk, 128), 0)
        for first in range(0, N_HEADS // 2, pairs_per_loop):
            pairs = range(first, first + pairs_per_loop)
            heads = [(pr, hh) for pr in pairs for hh in range(2)]
            kh, vh, ck = {}, {}, {}
            for pr in pairs:
                k2 = k_ref[:, pr * 128:(pr + 1) * 128]
                v2 = v_ref[:, pr * 128:(pr + 1) * 128]
                zero = jnp.zeros_like(k2)
                kh[pr, 0], kh[pr, 1] = jnp.where(low, k2, zero), jnp.where(low, zero, k2)
                vh[pr, 0], vh[pr, 1] = jnp.where(low, v2, zero), jnp.where(low, zero, v2)
                for hh in range(2):
                    ck[pr, hh] = ck_ref[:, 2 * pr + hh:2 * pr + hh + 1]

            def block(qb, carry, band, pairs=pairs, kh=kh, vh=vh, ck=ck):
                rows = pl.ds(pl.multiple_of(qb * tq, tq), tq)
                out = []
                it = iter(carry)
                for pr in pairs:
                    lanes = slice(pr * 128, (pr + 1) * 128)
                    q2 = q_ref[rows, lanes]
                    do2 = do_ref[rows, lanes]
                    dq = None
                    for hh in range(2):
                        h = 2 * pr + hh
                        dk, dv, dc = next(it)
                        s_t = _dot_nt(kh[pr, hh], q2) - ck[pr, hh]
                        p_t = jnp.exp(s_t - lse_ref[h, pl.ds(qb, 1), :])
                        if band is not None:
                            p_t = jnp.where(qry + band * tq >= key, p_t, 0.0)
                        ds_t = p_t * (_dot_nt(vh[pr, hh], do2) - dd_ref[h, pl.ds(qb, 1), :])
                        dsb = ds_t.astype(MM)
                        dv = dv + _dot(p_t.astype(MM), do2)
                        dk = dk + _dot(dsb, q2)
                        dc = dc - jnp.sum(ds_t, axis=1, keepdims=True)
                        part = _dot_tn(dsb, kh[pr, hh])
                        dq = part if dq is None else dq + part
                        dcq_ref[h, pl.ds(qb, 1), :] += jnp.sum(ds_t, axis=0, keepdims=True)
                        out.append((dk, dv, dc))
                    dq_ref[rows, lanes] += dq
                return tuple(out)

            carry = tuple((jnp.zeros((tk, 128), F32), jnp.zeros((tk, 128), F32), jnp.zeros((tk, 1), F32))
                          for _ in heads)
            for band in range(n_band):
                carry = block(kb * n_band + band, carry, band=band)
            carry = lax.fori_loop((kb + 1) * n_band, n_qblk, functools.partial(block, band=None), carry)
            grads = dict(zip(heads, carry))
            for pr in pairs:
                lanes = slice(pr * 128, (pr + 1) * 128)
                dk_ref[:, lanes] = jnp.where(low, grads[pr, 0][0], grads[pr, 1][0]).astype(MM)
                dv_ref[:, lanes] = jnp.where(low, grads[pr, 0][1], grads[pr, 1][1]).astype(MM)
                for hh in range(2):
                    dc_ref[:, 2 * pr + hh:2 * pr + hh + 1] = grads[pr, hh][2]

        @pl.when(kb == n_blk - 1)
        def _():
            dq_ref[...] = dq_ref[...] * Q_SCALE

    return pl.pallas_call(
        body, name="fox_bwd", grid=(n_blk,),
        in_specs=[pl.BlockSpec((S, D_ATT), lambda i: (0, 0)), pl.BlockSpec((tk, D_ATT), lambda i: (i, 1)),
                  pl.BlockSpec((tk, D_ATT), lambda i: (i, 2)), pl.BlockSpec((S, D_ATT), lambda i: (0, 0)),
                  _resident(), _resident(), _rows(tk, N_HEADS)],
        out_specs=[_const((S, D_ATT)), _rows(tk, D_ATT), _rows(tk, D_ATT), _rows(tk, N_HEADS),
                   _const((N_HEADS, n_qblk, tq))],
        out_shape=[jax.ShapeDtypeStruct((S, D_ATT), F32), jax.ShapeDtypeStruct((S, D_ATT), MM),
                   jax.ShapeDtypeStruct((S, D_ATT), MM), jax.ShapeDtypeStruct((S, N_HEADS), F32),
                   jax.ShapeDtypeStruct((N_HEADS, n_qblk, tq), F32)],
        compiler_params=_params("arbitrary"),
    )(fqkv, fqkv, fqkv, dcat, lse_row3, d_row3, c_col)


def _swa_bwd(sqkv, dcat, biasm, sinks_slot, bucket, lse, d_col):
    S = sqkv.shape[0]
    n_blk = S // WINDOW

    def body(q_ref, kp_ref, kc_ref, vp_ref, vc_ref, do_ref, bias_ref, sink_ref, bk_ref, lse_ref, dd_ref,
             dq_ref, dk_ref, dv_ref, drb_ref, dsink_ref, ds_acc):
        n = pl.program_id(0)

        @pl.when(n == 0)
        def _():
            dk_ref[...] = jnp.zeros_like(dk_ref)
            dv_ref[...] = jnp.zeros_like(dv_ref)
            ds_acc[...] = jnp.zeros_like(ds_acc)
            dsink_ref[...] = jnp.zeros_like(dsink_ref)

        no_prev = jnp.where(n > 0, 0.0, NEG)
        prev = pl.ds(pl.multiple_of(jnp.maximum(n - 1, 0) * WINDOW, WINDOW), WINDOW)
        cur = pl.ds(pl.multiple_of(n * WINDOW, WINDOW), WINDOW)
        lane8 = lax.broadcasted_iota(jnp.int32, (1, N_HEADS), 1)
        dkp = jnp.zeros((WINDOW, D_KV), F32)
        dkc = jnp.zeros((WINDOW, D_KV), F32)
        dvp = jnp.zeros((WINDOW, D_KV), F32)
        dvc = jnp.zeros((WINDOW, D_KV), F32)
        dsink = jnp.zeros((1, N_HEADS), F32)
        low = _head_select((WINDOW, 128), 0)
        zero = jnp.zeros((WINDOW, 128), MM)
        dqs = []
        for g in range(2):
            sel = low if g == 0 else jnp.logical_not(low)
            qg = _stack4(lambda j: jnp.where(sel, q_ref[:, j * 128:(j + 1) * 128], zero))
            dog = _stack4(lambda j: jnp.where(sel, do_ref[:, j * 128:(j + 1) * 128], zero))
            lse_g = _stack4(lambda j: lse_ref[:, 2 * j + g:2 * j + g + 1])
            dd = _stack4(lambda j: dd_ref[:, 2 * j + g:2 * j + g + 1])
            sink = _stack4(lambda j: jnp.full((WINDOW, 1), sink_ref[2 * j + g], F32))
            pp = jnp.exp(_dot_nt(qg, kp_ref[...]) + _stack4(lambda j: bias_ref[2 * j + g, :, :WINDOW]) + no_prev - lse_g)
            pc = jnp.exp(_dot_nt(qg, kc_ref[...]) + _stack4(lambda j: bias_ref[2 * j + g, :, WINDOW:]) - lse_g)
            sink_term = jnp.exp(sink - lse_g) * dd
            dsp = pp * (_dot_nt(dog, vp_ref[...]) - dd)
            dsc = pc * (_dot_nt(dog, vc_ref[...]) - dd)
            for j in range(4):
                rows = slice(j * WINDOW, (j + 1) * WINDOW)
                dsink = dsink + jnp.where(lane8 == 2 * j + g, -jnp.sum(sink_term[rows]), 0.0)
                ds_acc[2 * j + g, :, :WINDOW] += dsp[rows]
                ds_acc[2 * j + g, :, WINDOW:] += dsc[rows]
            dspb, dscb = dsp.astype(MM), dsc.astype(MM)
            dqs.append(_dot(dspb, kp_ref[...]) + _dot(dscb, kc_ref[...]))
            dkp = dkp + _dot_tn(dspb, qg)
            dkc = dkc + _dot_tn(dscb, qg)
            dvp = dvp + _dot_tn(pp.astype(MM), dog)
            dvc = dvc + _dot_tn(pc.astype(MM), dog)
        for j in range(4):
            rows = slice(j * WINDOW, (j + 1) * WINDOW)
            dq_ref[:, j * 128:(j + 1) * 128] = (jnp.where(low, dqs[0][rows], dqs[1][rows]) * Q_SCALE).astype(MM)
        dk_ref[prev, :] += dkp
        dk_ref[cur, :] += dkc
        dv_ref[prev, :] += dvp
        dv_ref[cur, :] += dvc
        dsink_ref[...] += dsink

        @pl.when(n == n_blk - 1)
        def _():
            bk = bk_ref[...]
            rb = lax.broadcasted_iota(jnp.int32, (N_BUCKETS, N_HEADS), 0)
            cb = lax.broadcasted_iota(jnp.int32, (N_BUCKETS, N_HEADS), 1)
            out = jnp.zeros((N_BUCKETS, N_HEADS), F32)
            for s in range(N_HEADS):
                acc = ds_acc[s]
                for b in range(N_BUCKETS):
                    out = out + jnp.where((rb == b) & (cb == s), jnp.sum(jnp.where(bk == b, acc, 0.0)), 0.0)
            drb_ref[...] = out

    do_spec = pl.BlockSpec((WINDOW, D_ATT), lambda n: (n, 1))
    return pl.pallas_call(
        body, name="swa_bwd", grid=(n_blk,),
        in_specs=_swa_specs(S) + [do_spec, _resident(), pl.BlockSpec(memory_space=pltpu.SMEM), _resident(),
                                  _rows(WINDOW, N_HEADS), _rows(WINDOW, N_HEADS)],
        out_specs=[_rows(WINDOW, D_ATT), _const((S, D_KV)), _const((S, D_KV)), _const((N_BUCKETS, N_HEADS)),
                   _const((1, N_HEADS))],
        out_shape=[jax.ShapeDtypeStruct((S, D_ATT), MM), jax.ShapeDtypeStruct((S, D_KV), F32),
                   jax.ShapeDtypeStruct((S, D_KV), F32), jax.ShapeDtypeStruct((N_BUCKETS, N_HEADS), F32),
                   jax.ShapeDtypeStruct((1, N_HEADS), F32)],
        scratch_shapes=[pltpu.VMEM((N_HEADS, WINDOW, 2 * WINDOW), F32)],
        compiler_params=_params("arbitrary"),
    )(sqkv, sqkv, sqkv, sqkv, sqkv, dcat, biasm, sinks_slot, bucket, lse, d_col)


D_Z = 3 * D_ATT + D_ATT + 2 * D_KV


def _pack_dz(dq_fox, dk_fox, dv_fox, dsq, dsk, dsv, tm):
    S = dq_fox.shape[0]

    def body(dq_ref, dk_ref, dv_ref, dsq_ref, dsk_ref, dsv_ref, dz_ref):
        dz_ref[:, 0:512] = dq_ref[...].astype(MM)
        dz_ref[:, 512:1024] = dk_ref[...]
        dz_ref[:, 1024:1536] = dv_ref[...]
        dz_ref[:, 1536:2048] = dsq_ref[...]
        dz_ref[:, 2048:2176] = dsk_ref[...].astype(MM)
        dz_ref[:, 2176:2304] = dsv_ref[...].astype(MM)

    return pl.pallas_call(
        body, name="pack_dz", grid=(S // tm,),
        in_specs=[_rows(tm, D_ATT), _rows(tm, D_ATT), _rows(tm, D_ATT), _rows(tm, D_ATT), _rows(tm, D_KV),
                  _rows(tm, D_KV)],
        out_specs=_rows(tm, D_Z), out_shape=jax.ShapeDtypeStruct((S, D_Z), MM),
        compiler_params=_params("parallel"),
    )(dq_fox, dk_fox, dv_fox, dsq, dsk, dsv)


def _pre_attn_bwd(x, dh1, dz, dff_t, wt, g1, tm):
    S = x.shape[0]

    def body(x_ref, dh1_ref, dz_ref, dff_ref, wt_ref, g1_ref, dx_ref, dg1_ref):
        i = pl.program_id(0)
        da = (_dot(dz_ref[:, 0:WT_FF], wt_ref[0:WT_FF]) + _dot(dz_ref[:, WT_FF:D_Z], wt_ref[WT_SQ:WT_ROWS])
              + _dot_tn(dff_ref[...].astype(MM), wt_ref[WT_FF:WT_SQ]))
        n1, r1 = _rms(x_ref[...])
        dx, dg1 = _rms_bwd(da, n1, r1, g1_ref[...])
        _accumulate(dg1_ref, dg1, i)
        dx_ref[...] = dh1_ref[...] + dx

    return pl.pallas_call(
        body, name="pre_attn_bwd", grid=(S // tm,),
        in_specs=[_rows(tm, D_MODEL), _rows(tm, D_MODEL), _rows(tm, D_Z), pl.BlockSpec((16, tm), lambda i: (0, i)),
                  _resident(), _const((1, D_MODEL))],
        out_specs=[_rows(tm, D_MODEL), _const((1, D_MODEL))],
        out_shape=[jax.ShapeDtypeStruct((S, D_MODEL), F32), jax.ShapeDtypeStruct((1, D_MODEL), F32)],
        compiler_params=_params("arbitrary"),
    )(x, dh1, dz, dff_t, wt, g1)


def _weight_grad(a, b, name, tk, n_chunks=1, relu2=False):
    S, K = a.shape
    N = b.shape[1]
    cn = N // n_chunks

    def body(a_ref, b_ref, out_ref):
        av = a_ref[...]
        if relu2:
            av = jnp.square(jnp.maximum(av.astype(F32), 0.0))
        av = av.astype(MM)
        for j in range(n_chunks):
            val = _dot_tn(av, b_ref[:, j * cn:(j + 1) * cn].astype(MM)).astype(MM)
            if n_chunks > 1:
                out_ref[j] = val
            else:
                out_ref[...] = val

    if n_chunks > 1:
        out_spec = pl.BlockSpec((n_chunks, tk, cn), lambda i: (0, i, 0))
        out_shape = jax.ShapeDtypeStruct((n_chunks, K, cn), MM)
    else:
        out_spec = pl.BlockSpec((tk, N), lambda i: (i, 0))
        out_shape = jax.ShapeDtypeStruct((K, N), MM)
    return pl.pallas_call(
        body, name=name, grid=(K // tk,),
        in_specs=[pl.BlockSpec((S, tk), lambda i: (0, i)), _resident()],
        out_specs=out_spec, out_shape=out_shape, compiler_params=_params("parallel"),
    )(a, b)


def _forget_weight_grad(dff_t, a):
    def body(d_ref, a_ref, out_ref):
        out_ref[...] = _dot(d_ref[...].astype(MM), a_ref[...])

    return pl.pallas_call(
        body, name="forget_weight_grad", out_shape=jax.ShapeDtypeStruct((16, D_MODEL), F32),
        in_specs=[_resident(), _resident()], out_specs=_resident(),
    )(dff_t, a)


def _place():
    return lax.axis_index("x"), lax.axis_index("y"), lax.axis_index("c")


def _all_gather_sequencer(stack):
    ref = jax.new_ref(stack, memory_space=pltpu.MemorySpace.HBM)

    @pl.kernel(mesh=plsc.ScalarSubcoreMesh(axis_name="sequencer", num_cores=1), name="all_gather_sequencer",
               scratch_types=(pltpu.SemaphoreType.DMA((7,)), pltpu.SemaphoreType.DMA((7,))),
               compiler_params=pltpu.CompilerParams(collective_id=1))
    def launch(send_sems, recv_sems):
        x, y, c = _place()
        sibling = (x, y, 1 - c)
        chips = [(1 - x, y), (x, 1 - y), (1 - x, 1 - y)]
        peers = [sibling] + [(px, py, c) for px, py in chips]
        barrier = pltpu.get_barrier_semaphore()
        for peer in peers:
            pl.semaphore_signal(barrier, inc=1, device_id=peer, device_id_type=MESH)
        pl.semaphore_wait(barrier, len(peers))

        def copy(k, block, to):
            px, py, pc = block
            slot = ref.at[4 * px + 2 * py + pc]
            return _remote(slot, slot, send_sems, recv_sems, k, to)

        first = [copy(k, (x, y, c), peer) for k, peer in enumerate(peers)]
        for cp in first:
            cp.start()
        passed = []
        for j, (px, py) in enumerate(chips):
            copy(1 + j, (px, py, c), sibling).wait_recv()
            passed.append(copy(4 + j, (px, py, c), sibling))
            passed[-1].start()
        copy(0, (x, y, 1 - c), sibling).wait_recv()
        for j, (px, py) in enumerate(chips):
            copy(4 + j, (px, py, 1 - c), sibling).wait_recv()
        for cp in first + passed:
            cp.wait_send()

    launch()
    return ref[...]


def _chip_sum(grad, other, name):
    _, _, r, cdim = grad.shape
    tr = 512 if r % 512 == 0 else r

    def body(c_ref, g_ref, o_ref, out_ref):
        out_ref[...] = (g_ref[...].astype(F32) + o_ref[...].astype(F32)).astype(out_ref.dtype)

    return pl.pallas_call(
        body, name=name,
        grid_spec=pltpu.PrefetchScalarGridSpec(
            num_scalar_prefetch=1, grid=(4, r // tr),
            in_specs=[pl.BlockSpec((None, None, tr, cdim), lambda k, i, c_ref: (k, c_ref[0], i, 0)),
                      pl.BlockSpec((None, tr, cdim), lambda k, i, c_ref: (k, i, 0))],
            out_specs=pl.BlockSpec((None, tr, cdim), lambda k, i, c_ref: (k, i, 0))),
        out_shape=jax.ShapeDtypeStruct((4, r, cdim), MM),
        compiler_params=_params("parallel", "parallel"),
    )(lax.axis_index("c").astype(jnp.int32).reshape(1), grad, other)


HBM_SPEC = pl.BlockSpec(memory_space=pltpu.HBM)
SEM_SPEC = pl.BlockSpec(memory_space=pltpu.SEMAPHORE)
DATAFLOW = pltpu.SideEffectType.DATAFLOW_SIDE_EFFECTING


def _exchange_start(name, arrays, n_copies, plan):
    n = len(arrays)

    def body(*refs):
        send_sems, recv_sems, token = refs[n], refs[n + 1], refs[2 * n + 2]
        for cp in plan(refs[:n], send_sems, recv_sems):
            cp.start()
        token[...] = jnp.zeros_like(token)

    out = pl.pallas_call(
        body, name=name,
        out_shape=(pltpu.SemaphoreType.DMA((n_copies,)), pltpu.SemaphoreType.DMA((n_copies,)),
                   *[pltpu.HBM(a.shape, a.dtype) for a in arrays], jax.ShapeDtypeStruct((1, D_MODEL), F32)),
        in_specs=[HBM_SPEC] * n,
        out_specs=(SEM_SPEC, SEM_SPEC, *[HBM_SPEC] * n, pl.BlockSpec(memory_space=pltpu.VMEM)),
        input_output_aliases={i: 2 + i for i in range(n)},
        compiler_params=pltpu.CompilerParams(has_side_effects=DATAFLOW),
    )(*[pltpu.with_memory_space_constraint(a, pltpu.HBM) for a in arrays])
    return (out[0], out[1]), list(out[2:2 + n]), out[2 + n]


def _exchange_wait(name, arrays, sems, after, plan):
    n = len(arrays)
    after = list(after) if isinstance(after, (list, tuple)) else [after]

    def body(*refs):
        send_sems, recv_sems = refs[n], refs[n + 1]
        for cp in plan(refs[:n], send_sems, recv_sems):
            cp.wait_send()
            cp.wait_recv()

    out = pl.pallas_call(
        body, name=name, out_shape=[pltpu.HBM(a.shape, a.dtype) for a in arrays],
        in_specs=[HBM_SPEC] * n + [SEM_SPEC, SEM_SPEC] + [pl.BlockSpec(memory_space=pl.ANY)] * len(after),
        out_specs=[HBM_SPEC] * n, input_output_aliases={i: i for i in range(n)},
        compiler_params=pltpu.CompilerParams(has_side_effects=DATAFLOW),
    )(*arrays, sems[0], sems[1], *after)
    return list(out)


def _remote(src, dst, send_sems, recv_sems, k, to):
    return pltpu.make_async_remote_copy(src_ref=src, dst_ref=dst, send_sem=send_sems.at[k], recv_sem=recv_sems.at[k],
                                        device_id=to, device_id_type=MESH)


def _plan_gather_direct(refs, send_sems, recv_sems):
    x, y, c = _place()
    me = 4 * x + 2 * y + c
    peers = [(x, y, 1 - c), (1 - x, y, c), (x, 1 - y, c), (1 - x, 1 - y, c)]
    return [_remote(ref.at[me], ref.at[me], send_sems, recv_sems, 4 * a + k, peer)
            for a, ref in enumerate(refs) for k, peer in enumerate(peers)]


def _plan_gather_pass_on(refs, send_sems, recv_sems):
    x, y, c = _place()
    chips = [(1 - x, y), (x, 1 - y), (1 - x, 1 - y)]
    return [_remote(ref.at[4 * px + 2 * py + c], ref.at[4 * px + 2 * py + c], send_sems, recv_sems, 3 * a + k,
                    (x, y, 1 - c))
            for a, ref in enumerate(refs) for k, (px, py) in enumerate(chips)]


def _plan_in_chip(refs, send_sems, recv_sems):
    n = len(refs) // 2
    x, y, c = _place()
    return [_remote(refs[a].at[:, 1 - c], refs[n + a], send_sems, recv_sems, a, (x, y, 1 - c)) for a in range(n)]


def _plan_between_chips(refs, send_sems, recv_sems):
    n = len(refs) // 2
    x, y, c = _place()
    chips = [(1 - x, y), (x, 1 - y), (1 - x, 1 - y)]
    return [_remote(refs[a].at[2 * px + py], refs[n + a].at[2 * x + y], send_sems, recv_sems, 3 * a + k, (px, py, c))
            for a in range(n) for k, (px, py) in enumerate(chips)]


def _plan_late_between(refs, send_sems, recv_sems):
    sums, land, small = refs
    x, y, c = _place()
    me = 4 * x + 2 * y + c
    copies = _plan_between_chips([sums, land], send_sems, recv_sems)
    peers = [(x ^ dx, y ^ dy, c ^ dc) for dx in range(2) for dy in range(2) for dc in range(2) if dx + dy + dc]
    return copies + [_remote(small.at[me], small.at[me], send_sems, recv_sems, 3 + k, peer)
                     for k, peer in enumerate(peers)]


def _adamw_math(w, g, m, v):
    m = ADAM_B1 * m + (1.0 - ADAM_B1) * g
    v = ADAM_B2 * v + (1.0 - ADAM_B2) * jnp.square(g)
    m_hat = m / (1.0 - ADAM_B1 ** ADAM_STEP)
    v_hat = v / (1.0 - ADAM_B2 ** ADAM_STEP)
    delta = -ADAM_LR * (m_hat / (jnp.sqrt(v_hat) + ADAM_EPS) + ADAM_WD * w)
    return delta, m, v


def _adamw(parts, w, m, v, name):
    n_parts, r, cdim = parts.shape
    tr = 256 if r % 256 == 0 else r

    def body(p_ref, w_ref, m_ref, v_ref, g_out, d_out, m_out, v_out):
        g = p_ref[0].astype(F32)
        for k in range(1, n_parts):
            g = g + p_ref[k].astype(F32)
        delta, m_new, v_new = _adamw_math(w_ref[...], g, m_ref[...], v_ref[...])
        g_out[...] = g
        d_out[...] = delta
        m_out[...] = m_new
        v_out[...] = v_new

    blk = pl.BlockSpec((tr, cdim), lambda i: (i, 0))
    return pl.pallas_call(
        body, name=name, grid=(r // tr,),
        in_specs=[pl.BlockSpec((n_parts, tr, cdim), lambda i: (0, i, 0)), blk, blk, blk],
        out_specs=[blk] * 4, out_shape=[jax.ShapeDtypeStruct((r, cdim), F32)] * 4,
        compiler_params=_params("parallel"),
    )(parts, w, m, v)


def _adamw_chips(parts, sums, w, m, v, name):
    _, r, cdim = parts.shape
    tr = 256 if r % 256 == 0 else r

    def body(chip_ref, p_ref, own_ref, w_ref, m_ref, v_ref, g_out, d_out, m_out, v_out):
        g = None
        for k in range(4):
            term = jnp.where(chip_ref[0] == k, own_ref[...], p_ref[k]).astype(F32)
            g = term if g is None else g + term
        delta, m_new, v_new = _adamw_math(w_ref[...], g, m_ref[...], v_ref[...])
        g_out[...] = g
        d_out[...] = delta
        m_out[...] = m_new
        v_out[...] = v_new

    blk = pl.BlockSpec((tr, cdim), lambda i, chip: (i, 0))
    my_chip = (2 * lax.axis_index("x") + lax.axis_index("y")).astype(jnp.int32).reshape(1)
    return pl.pallas_call(
        body, name=name,
        grid_spec=pltpu.PrefetchScalarGridSpec(
            num_scalar_prefetch=1, grid=(r // tr,),
            in_specs=[pl.BlockSpec((4, tr, cdim), lambda i, chip: (0, i, 0)),
                      pl.BlockSpec((None, tr, cdim), lambda i, chip: (chip[0], i, 0)), blk, blk, blk],
            out_specs=[blk] * 4),
        out_shape=[jax.ShapeDtypeStruct((r, cdim), F32)] * 4,
        compiler_params=_params("parallel"),
    )(my_chip, parts, sums, w, m, v)


class _NoExchange:
    def __init__(self, weights):
        self.weights = weights

    def before_pre_attn(self, g1):
        return g1

    def after_fox_fwd(self, fox_o, sinks_slot):
        return sinks_slot

    def after_attention(self, swa_o):
        return self.weights

    def after_early_grads(self, grads, d_col):
        return d_col

    def after_swa_bwd(self, dsq, d_row3):
        return d_row3

    def after_w_in_grad(self, d_win, g1):
        return g1


def _slot_order(t, axis):
    shp = t.shape
    t = t.reshape(shp[:axis] + (2, 4, shp[axis] // N_HEADS) + shp[axis + 1:])
    return jnp.swapaxes(t, axis, axis + 1).reshape(shp)


def _head_order(t, axis):
    shp = t.shape
    t = t.reshape(shp[:axis] + (4, 2, shp[axis] // N_HEADS) + shp[axis + 1:])
    return jnp.swapaxes(t, axis, axis + 1).reshape(shp)


def _forward_backward(x, p, target, win_t, hooks, b_forget, rel_bias, sinks, g1, g2, g3, g4, g5):
    S = x.shape[0]
    tm = 512
    tm_mlp = 512
    t = 256
    q0 = 3 * D_ATT + N_HEADS
    win_t = win_t.reshape(D_IN, D_MODEL)
    wt = jnp.concatenate(
        [win_t[:q0], jnp.zeros((8, D_MODEL), MM), _slot_order(win_t[q0:q0 + D_ATT], 0), win_t[q0 + D_ATT:]], axis=0)
    bcol = jnp.pad(b_forget.reshape(N_HEADS, 1), ((0, 8), (0, 0)))
    rel_bias_slot = rel_bias[:, np.array(SLOT_HEAD)]
    sinks_slot = sinks.reshape(N_HEADS)[np.array(SLOT_HEAD)]
    bucket = jnp.asarray(_swa_bucket_map())

    a, fqkv, sqkv, fft = _pre_attn(x, hooks.before_pre_attn(g1), wt, tm)
    c_row = _forget_cumsum(fft, bcol)
    c_col = c_row[:N_HEADS].T
    c_row3 = c_row[:N_HEADS].reshape(N_HEADS, S // t, t)
    fox_o, fox_lse = _fox_fwd(fqkv, c_row3, tq=512, tk=t)
    biasm = _swa_bias(rel_bias_slot, bucket)
    sinks_slot = hooks.after_fox_fwd(fox_o, sinks_slot)
    swa_o, swa_lse = _swa_fwd(sqkv, biasm, sinks_slot)
    wout, w1, w2, wple, wg = hooks.after_attention(swa_o)
    wout_fox = wout[:D_ATT]
    wout_swa = _slot_order(wout[D_ATT:], 0)
    mix, h1, m = _post_attn(x, fox_o, swa_o, wout_fox, wout_swa, g2, g3, tm)
    u, y, h2 = _mlp_fwd(m, h1, w1, w2, g4, tm_mlp)
    dh2, dpe, dgl, dg5, loss = _ple_loss(h2, p, target, wg, wple, g5, tm)

    d_wple = _weight_grad(p, dpe, "grad_w_ple", tk=D_PLE, n_chunks=N_DEV)
    d_wg = _weight_grad(h2, dgl, "grad_w_ple_gate", tk=256)
    dh1, dy, du, dg4, dg3 = _mlp_bwd(dh2, y, h1, u, w1, w2, g4, g3, tm)
    d_w2 = _weight_grad(u, dy, "grad_w_ff2", tk=256, relu2=True)
    d_w1 = _weight_grad(m, du, "grad_w_ff1", tk=256, n_chunks=N_DEV)
    head = np.arange(D_ATT) // HEAD_DIM
    head_rows = jnp.asarray((head[None, :] == np.arange(N_HEADS)[:, None]).astype(np.float32))
    dmix, dcat, d_row, d_swa, dg2 = _attn_out_bwd(dh1, mix, fox_o, swa_o, wout_fox, wout_swa, g2, head_rows, tm)
    d_col = d_swa.T
    d_wout_fox = _weight_grad(fox_o, dmix, "grad_w_out_fox", tk=256)
    d_wout_swa = _weight_grad(swa_o, dmix, "grad_w_out_swa", tk=256)
    d_wout = jnp.concatenate([d_wout_fox, _head_order(d_wout_swa, 0)], axis=0).reshape(N_DEV, D_MODEL // N_DEV, D_MODEL)
    early = dict(w_ff1=d_w1, w_ff2=d_w2.reshape(N_DEV, FF_CHUNK, D_MODEL), w_ple=d_wple,
                 w_ple_gate=d_wg.reshape(N_DEV, D_MODEL // N_DEV, D_MODEL), w_out=d_wout)

    d_col = hooks.after_early_grads(early, d_col)
    dsq, dsk, dsv, d_rb_slot, d_sink_slot = _swa_bwd(sqkv, dcat, biasm, sinks_slot, bucket, swa_lse, d_col)
    lse_row3 = fox_lse.T.reshape(N_HEADS, S // t, t)
    d_row3 = hooks.after_swa_bwd(dsq, d_row.reshape(N_HEADS, S // t, t))
    dq_fox, dk_fox, dv_fox, dc_col, dcq = _fox_bwd(fqkv, dcat, lse_row3, d_row3, c_col, tq=t, tk=512)
    dc_row = jnp.pad(dc_col.T + dcq.reshape(N_HEADS, S), ((0, 8), (0, 0)))
    dff_t, db = _forget_bwd(dc_row, fft, bcol)
    dz = _pack_dz(dq_fox, dk_fox, dv_fox, dsq, dsk, dsv, 512)
    d_wmain = _weight_grad(dz, a, "grad_w_in", tk=256)
    d_wff_t = _forget_weight_grad(dff_t, a)

    sq0 = 3 * D_ATT
    d_win = jnp.concatenate(
        [d_wmain[:sq0], d_wff_t[:N_HEADS].astype(MM), _head_order(d_wmain[sq0:sq0 + D_ATT], 0),
         d_wmain[sq0 + D_ATT:]], axis=0)
    d_win = d_win.reshape(N_DEV, D_IN // N_DEV, D_MODEL)
    grad_x, dg1 = _pre_attn_bwd(x, dh1, dz, dff_t, wt, hooks.after_w_in_grad(d_win, g1), tm)
    big = dict(early, w_in=d_win)
    small = dict(b_forget=db[:N_HEADS].reshape(1, N_HEADS), rel_bias=d_rb_slot[:, np.array(HEAD_SLOT)],
                 swa_sinks=d_sink_slot[:, np.array(HEAD_SLOT)], g_attn_pre=dg1, g_attn_post=dg2, g_ff_pre=dg3,
                 g_ff_post=dg4, g_ple_post=dg5)
    return loss, grad_x, big, small


BIG = ("w_in", "w_out", "w_ff1", "w_ff2", "w_ple", "w_ple_gate")
SMALL_ROWS = ("g_attn_pre", "g_attn_post", "g_ff_pre", "g_ff_post", "g_ple_post")
WEIGHTS =("w_in", "b_forget", "w_out", "rel_bias", "swa_sinks", "g_attn_pre", "g_attn_post", "w_ff1", "w_ff2",
           "g_ff_pre", "g_ff_post", "w_ple", "w_ple_gate", "g_ple_post")


EARLY = ("w_ff1", "w_ff2", "w_ple", "w_ple_gate", "w_out")


class _Overlap:
    def __init__(self, later):
        self.later = later

    def before_pre_attn(self, g1):
        self.gather_sems, self.later, token = _exchange_start("gather_rest_start", self.later, 4 * 5, _plan_gather_direct)
        return g1 + token

    def after_fox_fwd(self, fox_o, sinks_slot):
        later = _exchange_wait("gather_rest_wait", self.later, self.gather_sems, fox_o, _plan_gather_direct)
        self.pass_sems, self.later, token = _exchange_start("gather_pass_on_start", later, 3 * 5, _plan_gather_pass_on)
        return sinks_slot + token[0, :N_HEADS]

    def after_attention(self, swa_o):
        wout_g, w1_g, w2_g, wple_g, wg_g = _exchange_wait("gather_pass_on_wait", self.later, self.pass_sems, swa_o,
                                                         _plan_gather_pass_on)
        return (wout_g.reshape(D_MODEL, D_MODEL), w1_g, w2_g.reshape(D_FF, D_MODEL),
                jnp.moveaxis(wple_g, 0, 1).reshape(D_PLE, D_MODEL), wg_g.reshape(D_MODEL, D_MODEL))

    def after_early_grads(self, grads, d_col):
        views = [grads[k].reshape((4, 2) + grads[k].shape[1:]) for k in EARLY]
        lands = [lax.empty((4,) + grads[k].shape[1:], MM) for k in EARLY]
        self.in_chip_sems, self.in_chip, token = _exchange_start("grads_in_chip_start", views + lands, len(EARLY),
                                                                 _plan_in_chip)
        return d_col + token[0, 0]

    def after_swa_bwd(self, dsq, d_row3):
        arrays = _exchange_wait("grads_in_chip_wait", self.in_chip, self.in_chip_sems, dsq, _plan_in_chip)
        n = len(EARLY)
        sums = [_chip_sum(arrays[a], arrays[n + a], "chip_sum_" + k) for a, k in enumerate(EARLY)]
        lands = [lax.empty(s.shape, s.dtype) for s in sums]
        self.between_sems, self.between, token = _exchange_start("grads_between_chips_start", sums + lands, 3 * n,
                                                                 _plan_between_chips)
        return d_row3 + token[0, 0]

    def after_w_in_grad(self, d_win, g1):
        self.late_in_chip_sems, self.late_in_chip, token = _exchange_start(
            "late_in_chip_start", [d_win.reshape((4, 2) + d_win.shape[1:]), lax.empty((4,) + d_win.shape[1:], MM)],
            1, _plan_in_chip)
        return g1 + token

    def finish(self, after):
        arrays = _exchange_wait("grads_between_chips_wait", self.between, self.between_sems, after,
                                _plan_between_chips)
        n = len(EARLY)
        self.sums = arrays[:n]
        return arrays[n:]


def _pack_small(t):
    rows = [t[k].reshape(1, D_MODEL) for k in SMALL_ROWS]
    misc = jnp.concatenate([t["b_forget"].reshape(-1), t["swa_sinks"].reshape(-1), t["rel_bias"].reshape(-1)])
    rows.append(jnp.pad(misc, (0, D_MODEL - misc.shape[0])).reshape(1, D_MODEL))
    rows.append(jnp.pad(t["loss"].reshape(-1), (0, D_MODEL - 1)).reshape(1, D_MODEL))
    rows.append(jnp.zeros((1, D_MODEL), F32))
    return jnp.concatenate(rows, axis=0).astype(F32)


def _unpack_small(blk):
    out = {k: blk[i].reshape(1, D_MODEL) for i, k in enumerate(SMALL_ROWS)}
    misc = blk[len(SMALL_ROWS)]
    out["b_forget"] = misc[:N_HEADS].reshape(1, N_HEADS)
    out["swa_sinks"] = misc[N_HEADS:2 * N_HEADS].reshape(1, N_HEADS)
    out["rel_bias"] = misc[2 * N_HEADS:2 * N_HEADS + N_BUCKETS * N_HEADS].reshape(N_BUCKETS, N_HEADS)
    out["loss"] = blk[len(SMALL_ROWS) + 1, 0]
    return out


def kernel(x, p, w_in, b_forget, w_out, rel_bias, swa_sinks, g_attn_pre, g_attn_post, w_ff1, w_ff2, g_ff_pre, g_ff_post, w_ple, w_ple_gate, g_ple_post, loss_target, m_w_in, m_b_forget, m_w_out, m_rel_bias, m_swa_sinks, m_g_attn_pre, m_g_attn_post, m_w_ff1, m_w_ff2, m_g_ff_pre, m_g_ff_post, m_w_ple, m_w_ple_gate, m_g_ple_post, v_w_in, v_b_forget, v_w_out, v_rel_bias, v_swa_sinks, v_g_attn_pre, v_g_attn_post, v_w_ff1, v_w_ff2, v_g_ff_pre, v_g_ff_post, v_w_ple, v_w_ple_gate, v_g_ple_post):
    w = dict(w_in=w_in, b_forget=b_forget, w_out=w_out, rel_bias=rel_bias, swa_sinks=swa_sinks,
             g_attn_pre=g_attn_pre, g_attn_post=g_attn_post, w_ff1=w_ff1, w_ff2=w_ff2, g_ff_pre=g_ff_pre,
             g_ff_post=g_ff_post, w_ple=w_ple, w_ple_gate=w_ple_gate, g_ple_post=g_ple_post)
    mom = dict(w_in=m_w_in, b_forget=m_b_forget, w_out=m_w_out, rel_bias=m_rel_bias, swa_sinks=m_swa_sinks,
               g_attn_pre=m_g_attn_pre, g_attn_post=m_g_attn_post, w_ff1=m_w_ff1, w_ff2=m_w_ff2,
               g_ff_pre=m_g_ff_pre, g_ff_post=m_g_ff_post, w_ple=m_w_ple, w_ple_gate=m_w_ple_gate,
               g_ple_post=m_g_ple_post)
    var = dict(w_in=v_w_in, b_forget=v_b_forget, w_out=v_w_out, rel_bias=v_rel_bias, swa_sinks=v_swa_sinks,
               g_attn_pre=v_g_attn_pre, g_attn_post=v_g_attn_post, w_ff1=v_w_ff1, w_ff2=v_w_ff2,
               g_ff_pre=v_g_ff_pre, g_ff_post=v_g_ff_post, w_ple=v_w_ple, w_ple_gate=v_w_ple_gate,
               g_ple_post=v_g_ple_post)

    turn = lambda t, k: t.T if k == "w_in" else t
    me = 4 * lax.axis_index("x") + 2 * lax.axis_index("y") + lax.axis_index("c")

    def stack(block):
        return lax.dynamic_update_slice_in_dim(lax.empty((N_DEV,) + block.shape, block.dtype), block[None], me, 0)

    stacks = [stack(turn(w[k][0], k).astype(MM)) for k in BIG]
    win_g, later = _all_gather_sequencer(stacks[0]), stacks[1:]
    hooks = _Overlap(later)
    loss, grad_x, big, small = _forward_backward(
        x[0], p[0, 0], loss_target[0], win_g, hooks, b_forget, rel_bias, swa_sinks,
        g_attn_pre, g_attn_post, g_ff_pre, g_ff_post, g_ple_post)
    out_g, out_d, out_m, out_v = {}, {}, {}, {}

    def update(k, part, own):
        g, d, m_new, v_new = _adamw_chips(part, own, turn(w[k][0], k), turn(mom[k][0], k), turn(var[k][0], k),
                                          "adamw_" + k)
        out_g[k], out_d[k], out_m[k], out_v[k] = turn(g, k)[None], turn(d, k)[None], turn(m_new, k)[None], turn(v_new, k)[None]
        return d

    view, other = _exchange_wait("late_in_chip_wait", hooks.late_in_chip, hooks.late_in_chip_sems, grad_x,
                                 _plan_in_chip)
    chip_sum = _chip_sum(view, other, "chip_sum_w_in")
    small["loss"] = loss
    between_sems, between, token = _exchange_start(
        "late_between_chips_start", [chip_sum, lax.empty(chip_sum.shape, MM), stack(_pack_small(small))], 3 + 7,
        _plan_late_between)
    early_parts = hooks.finish(token)
    done = [update(k, part, own) for k, part, own in zip(EARLY, early_parts, hooks.sums)]
    chip_sum, part, small_all = _exchange_wait("late_between_chips_wait", between, between_sems, done,
                                               _plan_late_between)
    update("w_in", part, chip_sum)
    rep = {k: w[k] for k in w if k not in BIG}
    rep["loss"] = jnp.zeros((), F32)
    rep_m = {k: mom[k] for k in mom if k not in BIG}
    rep_m["loss"] = jnp.zeros((), F32)
    rep_v = {k: var[k] for k in var if k not in BIG}
    rep_v["loss"] = jnp.ones((), F32)
    g_s, d_s, m_s, v_s = _adamw(small_all, _pack_small(rep), _pack_small(rep_m), _pack_small(rep_v), "adamw_small")
    g_s, d_s, m_s, v_s = _unpack_small(g_s), _unpack_small(d_s), _unpack_small(m_s), _unpack_small(v_s)
    for k in w:
        if k not in BIG:
            out_g[k], out_d[k], out_m[k], out_v[k] = g_s[k], d_s[k], m_s[k], v_s[k]
    return (g_s["loss"], grad_x[None], *[out_g[k] for k in WEIGHTS], *[out_d[k] for k in WEIGHTS],
            *[out_m[k] for k in WEIGHTS], *[out_v[k] for k in WEIGHTS])
```

```python
import functools

import numpy as np
import jax
import jax.numpy as jnp
from jax import lax
from jax.experimental import pallas as pl
from jax.experimental.pallas import tpu as pltpu
from jax.experimental.pallas import tpu_sc as plsc

F32 = jnp.float32
MM = jnp.bfloat16

D_MODEL = 1024
HEAD_DIM = 64
N_HEADS = 8
D_ATT = N_HEADS * HEAD_DIM
D_KV = 128
D_FF = 4096
D_PLE = 256
D_IN = 3 * D_ATT + N_HEADS + D_ATT + 2 * D_KV
N_DEV = 8
FF_CHUNK = D_FF // N_DEV
WINDOW = 128
N_BUCKETS = 32
MAX_DISTANCE = 128
RMS_EPS = 1e-6
Q_SCALE = HEAD_DIM ** -0.5
NEG = -1e30

ADAM_LR = 0.001
ADAM_B1 = 0.9
ADAM_B2 = 0.999
ADAM_EPS = 1e-08
ADAM_WD = 0.01
ADAM_STEP = 10

SLOT_HEAD = (0, 4, 1, 5, 2, 6, 3, 7)
HEAD_SLOT = (0, 2, 4, 6, 1, 3, 5, 7)

VMEM_LIMIT = 60 * 1024 * 1024
MESH = pl.DeviceIdType.MESH

NT = (((1,), (1,)), ((), ()))
TN = (((0,), (0,)), ((), ()))


def _params(*semantics):
    return pltpu.CompilerParams(dimension_semantics=semantics, vmem_limit_bytes=VMEM_LIMIT)


def _resident():
    return pl.BlockSpec(memory_space=pltpu.VMEM)


def _rows(tm, width):
    return pl.BlockSpec((tm, width), lambda i: (i, 0))


def _const(shape):
    return pl.BlockSpec(shape, lambda i: (0,) * len(shape))


def _dot(a, b):
    return jnp.dot(a, b, preferred_element_type=F32)


def _dot_nt(a, b):
    return lax.dot_general(a, b, NT, preferred_element_type=F32)


def _dot_tn(a, b):
    return lax.dot_general(a, b, TN, preferred_element_type=F32)


def _rms(xf):
    r = lax.rsqrt(jnp.mean(xf * xf, axis=-1, keepdims=True) + RMS_EPS)
    return xf * r, r


def _rms_bwd(dout, n, r, g):
    dg = jnp.sum(dout * n, axis=0, keepdims=True)
    dn = dout * g
    dx = r * (dn - n * jnp.mean(dn * n, axis=-1, keepdims=True))
    return dx, dg


def _accumulate(ref, value, step):
    @pl.when(step == 0)
    def _():
        ref[...] = value

    @pl.when(step != 0)
    def _():
        ref[...] += value


def _t5_bucket(n):
    max_exact = N_BUCKETS // 2
    large = max_exact + (np.log(np.maximum(n, 1) / max_exact) / np.log(MAX_DISTANCE / max_exact)
                         * (N_BUCKETS - max_exact)).astype(np.int32)
    large = np.minimum(large, N_BUCKETS - 1)
    return np.where(n < max_exact, n, large).astype(np.int32)


def _swa_bucket_map():
    i = np.arange(WINDOW)[:, None]
    j = np.arange(2 * WINDOW)[None, :]
    dist = i + WINDOW - j
    ok = (dist >= 0) & (dist < WINDOW)
    return np.where(ok, _t5_bucket(np.clip(dist, 0, None)), -1).astype(np.int32)


WT_FOX = 0
WT_FF = 3 * D_ATT
WT_SQ = WT_FF + 16
WT_SKV = WT_SQ + D_ATT
WT_ROWS = WT_SKV + 2 * D_KV


def _pre_attn(x, g1, wt, tm):
    S = x.shape[0]

    def body(x_ref, g_ref, wt_ref, a_ref, fqkv_ref, sqkv_ref, fft_ref):
        n, _ = _rms(x_ref[...])
        a = (n * g_ref[...]).astype(MM)
        a_ref[...] = a
        fqkv_ref[:, :D_ATT] = (_dot_nt(a, wt_ref[WT_FOX:WT_FOX + D_ATT]) * Q_SCALE).astype(MM)
        fqkv_ref[:, D_ATT:] = _dot_nt(a, wt_ref[WT_FOX + D_ATT:WT_FF]).astype(MM)
        sqkv_ref[:, :D_ATT] = (_dot_nt(a, wt_ref[WT_SQ:WT_SKV]) * Q_SCALE).astype(MM)
        sqkv_ref[:, D_ATT:] = _dot_nt(a, wt_ref[WT_SKV:WT_ROWS]).astype(MM)
        fft_ref[...] = _dot_nt(wt_ref[WT_FF:WT_SQ], a)

    return pl.pallas_call(
        body, name="pre_attn", grid=(S // tm,),
        in_specs=[_rows(tm, D_MODEL), _const((1, D_MODEL)), _resident()],
        out_specs=[_rows(tm, D_MODEL), _rows(tm, 3 * D_ATT), _rows(tm, D_ATT + 2 * D_KV),
                   pl.BlockSpec((16, tm), lambda i: (0, i))],
        out_shape=[jax.ShapeDtypeStruct((S, D_MODEL), MM), jax.ShapeDtypeStruct((S, 3 * D_ATT), MM),
                   jax.ShapeDtypeStruct((S, D_ATT + 2 * D_KV), MM), jax.ShapeDtypeStruct((16, S), F32)],
        compiler_params=_params("parallel"),
    )(x, g1, wt)


def _lane_scan(v, reverse):
    S = v.shape[1]
    lane = lax.broadcasted_iota(jnp.int32, v.shape, 1)
    k = 1
    while k < S:
        if reverse:
            v = v + jnp.where(lane < S - k, pltpu.roll(v, S - k, axis=1), 0.0)
        else:
            v = v + jnp.where(lane >= k, pltpu.roll(v, k, axis=1), 0.0)
        k *= 2
    return v


def _forget_cumsum(fft, bcol):
    def body(f_ref, b_ref, c_ref):
        z = f_ref[...] + b_ref[...]
        log_f = jnp.minimum(z, 0.0) - jnp.log1p(jnp.exp(-jnp.abs(z)))
        c_ref[...] = _lane_scan(log_f, reverse=False)

    return pl.pallas_call(
        body, name="forget_cumsum", out_shape=jax.ShapeDtypeStruct(fft.shape, F32),
        in_specs=[_resident(), _resident()], out_specs=_resident(),
    )(fft, bcol)


def _forget_bwd(dc_row, fft, bcol):
    def body(dc_ref, f_ref, b_ref, dff_ref, db_ref):
        z = f_ref[...] + b_ref[...]
        dlog_f = _lane_scan(dc_ref[...], reverse=True)
        dff = dlog_f * (1.0 / (1.0 + jnp.exp(z)))
        dff_ref[...] = dff
        db_ref[...] = jnp.sum(dff, axis=1, keepdims=True)

    return pl.pallas_call(
        body, name="forget_bwd",
        out_shape=[jax.ShapeDtypeStruct(fft.shape, F32), jax.ShapeDtypeStruct((fft.shape[0], 1), F32)],
        in_specs=[_resident()] * 3, out_specs=[_resident()] * 2,
    )(dc_row, fft, bcol)


def _head_select(shape, upper):
    lane = lax.broadcasted_iota(jnp.int32, shape, 1)
    return lane >= HEAD_DIM if upper else lane < HEAD_DIM


def _fox_fwd(fqkv, c_row3, tq, tk, pairs_per_loop=2, row_chunks=1):
    S = fqkv.shape[0]
    rq = tq // row_chunks
    n_band = tq // tk

    def body(q_ref, k_ref, v_ref, ck_ref, o_ref, lse_ref):
        qi = pl.program_id(0)
        row = lax.broadcasted_iota(jnp.int32, (rq, tk), 0)
        col = lax.broadcasted_iota(jnp.int32, (rq, tk), 1)
        low = _head_select((rq, 128), 0)
        for first in range(0, N_HEADS // 2, pairs_per_loop):
            pairs = range(first, first + pairs_per_loop)
            chains = [(pr, hh, rc) for pr in pairs for hh in range(2) for rc in range(row_chunks)]
            qh = {}
            for pr in pairs:
                for rc in range(row_chunks):
                    q2 = q_ref[rc * rq:(rc + 1) * rq, pr * 128:(pr + 1) * 128]
                    qh[pr, 0, rc] = jnp.where(low, q2, jnp.zeros_like(q2))
                    qh[pr, 1, rc] = jnp.where(low, jnp.zeros_like(q2), q2)

            def block(kb, carry, band, chains=chains, qh=qh):
                rows = pl.ds(pl.multiple_of(kb * tk, tk), tk)
                out = []
                for (pr, hh, rc), (m, l, acc) in zip(chains, carry):
                    if band is not None and (rc + 1) * rq <= band * tk:
                        out.append((m, l, acc))
                        continue
                    lanes = slice(pr * 128, (pr + 1) * 128)
                    s = _dot_nt(qh[pr, hh, rc], k_ref[rows, lanes]) - ck_ref[2 * pr + hh, pl.ds(kb, 1), :]
                    if band is not None:
                        s = jnp.where(row + rc * rq >= col + band * tk, s, NEG)
                    m_new = jnp.maximum(m, jnp.max(s, axis=-1, keepdims=True))
                    p = jnp.exp(s - m_new)
                    alpha = jnp.exp(m - m_new)
                    l = alpha * l + jnp.sum(p, axis=-1, keepdims=True)
                    acc = alpha * acc + _dot(p.astype(MM), v_ref[rows, lanes])
                    out.append((m_new, l, acc))
                return tuple(out)

            carry = tuple((jnp.full((rq, 1), NEG, F32), jnp.zeros((rq, 1), F32), jnp.zeros((rq, 128), F32))
                          for _ in chains)
            carry = lax.fori_loop(0, qi * n_band, functools.partial(block, band=None), carry)
            for band in range(n_band):
                carry = block(qi * n_band + band, carry, band=band)
            res = {}
            for (pr, hh, rc), (m, l, acc) in zip(chains, carry):
                res[pr, hh, rc] = acc / l
                lse_ref[rc * rq:(rc + 1) * rq, 2 * pr + hh:2 * pr + hh + 1] = m + jnp.log(l)
            for pr in pairs:
                for rc in range(row_chunks):
                    o_ref[rc * rq:(rc + 1) * rq, pr * 128:(pr + 1) * 128] = jnp.where(
                        low, res[pr, 0, rc], res[pr, 1, rc]).astype(MM)

    return pl.pallas_call(
        body, name="fox_fwd", grid=(S // tq,),
        in_specs=[pl.BlockSpec((tq, D_ATT), lambda i: (i, 0)), pl.BlockSpec((S, D_ATT), lambda i: (0, 1)),
                  pl.BlockSpec((S, D_ATT), lambda i: (0, 2)), _resident()],
        out_specs=[_rows(tq, D_ATT), _rows(tq, N_HEADS)],
        out_shape=[jax.ShapeDtypeStruct((S, D_ATT), MM), jax.ShapeDtypeStruct((S, N_HEADS), F32)],
        compiler_params=_params("parallel"),
    )(fqkv, fqkv, fqkv, c_row3)


def _swa_bias(rel_bias_slot, bucket):
    def body(rb_ref, bk_ref, out_ref):
        bk = bk_ref[...]
        for s in range(N_HEADS):
            acc = jnp.where(bk < 0, NEG, 0.0).astype(F32)
            for b in range(N_BUCKETS):
                acc = jnp.where(bk == b, rb_ref[b, s], acc)
            out_ref[s] = acc

    return pl.pallas_call(
        body, name="swa_bias", out_shape=jax.ShapeDtypeStruct((N_HEADS, WINDOW, 2 * WINDOW), F32),
        in_specs=[pl.BlockSpec(memory_space=pltpu.SMEM), _resident()], out_specs=_resident(),
    )(rel_bias_slot, bucket)


def _stack4(piece):
    return jnp.concatenate([piece(j) for j in range(4)], axis=0)


def _swa_specs(S):
    q = pl.BlockSpec((WINDOW, D_ATT), lambda n: (n, 0))
    kp = pl.BlockSpec((WINDOW, D_KV), lambda n: (jnp.maximum(n - 1, 0), 4))
    kc = pl.BlockSpec((WINDOW, D_KV), lambda n: (n, 4))
    vp = pl.BlockSpec((WINDOW, D_KV), lambda n: (jnp.maximum(n - 1, 0), 5))
    vc = pl.BlockSpec((WINDOW, D_KV), lambda n: (n, 5))
    return [q, kp, kc, vp, vc]


def _swa_fwd(sqkv, biasm, sinks_slot):
    S = sqkv.shape[0]

    def body(q_ref, kp_ref, kc_ref, vp_ref, vc_ref, bias_ref, sink_ref, o_ref, lse_ref):
        n = pl.program_id(0)
        no_prev = jnp.where(n > 0, 0.0, NEG)
        low = _head_select((WINDOW, 128), 0)
        res = []
        for g in range(2):
            sel = low if g == 0 else jnp.logical_not(low)
            qg = _stack4(lambda j: jnp.where(sel, q_ref[:, j * 128:(j + 1) * 128], jnp.zeros((WINDOW, 128), MM)))
            sink = _stack4(lambda j: jnp.full((WINDOW, 1), sink_ref[2 * j + g], F32))
            sp = _dot_nt(qg, kp_ref[...]) + _stack4(lambda j: bias_ref[2 * j + g, :, :WINDOW]) + no_prev
            sc = _dot_nt(qg, kc_ref[...]) + _stack4(lambda j: bias_ref[2 * j + g, :, WINDOW:])
            m = jnp.maximum(jnp.maximum(jnp.max(sp, axis=-1, keepdims=True),
                                        jnp.max(sc, axis=-1, keepdims=True)), sink)
            ep = jnp.exp(sp - m)
            ec = jnp.exp(sc - m)
            den = jnp.sum(ep, axis=-1, keepdims=True) + jnp.sum(ec, axis=-1, keepdims=True) + jnp.exp(sink - m)
            res.append((_dot(ep.astype(MM), vp_ref[...]) + _dot(ec.astype(MM), vc_ref[...])) / den)
            lse = m + jnp.log(den)
            for j in range(4):
                lse_ref[:, 2 * j + g:2 * j + g + 1] = lse[j * WINDOW:(j + 1) * WINDOW]
        for j in range(4):
            rows = slice(j * WINDOW, (j + 1) * WINDOW)
            o_ref[:, j * 128:(j + 1) * 128] = jnp.where(low, res[0][rows], res[1][rows]).astype(MM)

    return pl.pallas_call(
        body, name="swa_fwd", grid=(S // WINDOW,),
        in_specs=_swa_specs(S) + [_resident(), pl.BlockSpec(memory_space=pltpu.SMEM)],
        out_specs=[_rows(WINDOW, D_ATT), _rows(WINDOW, N_HEADS)],
        out_shape=[jax.ShapeDtypeStruct((S, D_ATT), MM), jax.ShapeDtypeStruct((S, N_HEADS), F32)],
        compiler_params=_params("parallel"),
    )(sqkv, sqkv, sqkv, sqkv, sqkv, biasm, sinks_slot)


def _post_attn(x, fox_o, swa_o, wout_fox, wout_swa, g2, g3, tm):
    S = x.shape[0]

    def body(x_ref, fo_ref, so_ref, wf_ref, ws_ref, g2_ref, g3_ref, mix_ref, h1_ref, m_ref):
        mix = _dot(fo_ref[...], wf_ref[...]) + _dot(so_ref[...], ws_ref[...])
        mix_ref[...] = mix
        n2, _ = _rms(mix)
        h1 = x_ref[...] + n2 * g2_ref[...]
        h1_ref[...] = h1
        n3, _ = _rms(h1)
        m_ref[...] = (n3 * g3_ref[...]).astype(MM)

    return pl.pallas_call(
        body, name="post_attn", grid=(S // tm,),
        in_specs=[_rows(tm, D_MODEL), _rows(tm, D_ATT), _rows(tm, D_ATT), _resident(), _resident(),
                  _const((1, D_MODEL)), _const((1, D_MODEL))],
        out_specs=[_rows(tm, D_MODEL)] * 3,
        out_shape=[jax.ShapeDtypeStruct((S, D_MODEL), F32), jax.ShapeDtypeStruct((S, D_MODEL), F32),
                   jax.ShapeDtypeStruct((S, D_MODEL), MM)],
        compiler_params=_params("parallel"),
    )(x, fox_o, swa_o, wout_fox, wout_swa, g2, g3)


def _mlp_fwd(m, h1, w1, w2, g4, tm):
    S = m.shape[0]

    def body(m_ref, h1_ref, w1_ref, w2_ref, g4_ref, u_ref, y_ref, h2_ref):
        mb = m_ref[...]
        y = jnp.zeros((tm, D_MODEL), F32)
        for j in range(N_DEV):
            cols = slice(j * FF_CHUNK, (j + 1) * FF_CHUNK)
            u = _dot(mb, w1_ref[j])
            u_ref[:, cols] = u.astype(MM)
            y = y + _dot(jnp.square(jnp.maximum(u, 0.0)).astype(MM), w2_ref[cols, :])
        y_ref[...] = y
        n4, _ = _rms(y)
        h2_ref[...] = h1_ref[...] + n4 * g4_ref[...]

    return pl.pallas_call(
        body, name="mlp_fwd", grid=(S // tm,),
        in_specs=[_rows(tm, D_MODEL), _rows(tm, D_MODEL), _resident(), _resident(), _const((1, D_MODEL))],
        out_specs=[_rows(tm, D_FF), _rows(tm, D_MODEL), _rows(tm, D_MODEL)],
        out_shape=[jax.ShapeDtypeStruct((S, D_FF), MM), jax.ShapeDtypeStruct((S, D_MODEL), F32),
                   jax.ShapeDtypeStruct((S, D_MODEL), F32)],
        compiler_params=_params("parallel"),
    )(m, h1, w1, w2, g4)


def _ple_loss(h2, p, target, wg, wple, g5, tm):
    S = h2.shape[0]

    def body(h2_ref, p_ref, t_ref, wg_ref, wp_ref, g5_ref, dh2_ref, dpe_ref, dgl_ref, dg5_ref, loss_ref):
        i = pl.program_id(0)
        h2 = h2_ref[...]
        gate = jax.nn.sigmoid(_dot(h2.astype(MM), wg_ref[...]))
        pe = _dot(p_ref[...].astype(MM), wp_ref[...])
        n5, r5 = _rms(pe * gate)
        g5 = g5_ref[...]
        diff = h2 + n5 * g5 - t_ref[...]
        per_token = jnp.mean(jnp.square(diff), axis=-1, keepdims=True)
        _accumulate(loss_ref, 0.5 * jnp.sum(per_token, axis=0, keepdims=True), i)
        dh3 = diff * (1.0 / D_MODEL)
        de, dg5 = _rms_bwd(dh3, n5, r5, g5)
        _accumulate(dg5_ref, dg5, i)
        dpe_ref[...] = (de * gate).astype(MM)
        dgl = (de * pe * gate * (1.0 - gate)).astype(MM)
        dgl_ref[...] = dgl
        dh2_ref[...] = dh3 + _dot_nt(dgl, wg_ref[...])

    return pl.pallas_call(
        body, name="ple_loss", grid=(S // tm,),
        in_specs=[_rows(tm, D_MODEL), _rows(tm, D_PLE), _rows(tm, D_MODEL), _resident(), _resident(),
                  _const((1, D_MODEL))],
        out_specs=[_rows(tm, D_MODEL), _rows(tm, D_MODEL), _rows(tm, D_MODEL), _const((1, D_MODEL)), _const((1, 1))],
        out_shape=[jax.ShapeDtypeStruct((S, D_MODEL), F32), jax.ShapeDtypeStruct((S, D_MODEL), MM),
                   jax.ShapeDtypeStruct((S, D_MODEL), MM), jax.ShapeDtypeStruct((1, D_MODEL), F32),
                   jax.ShapeDtypeStruct((1, 1), F32)],
        compiler_params=_params("arbitrary"),
    )(h2, p, target, wg, wple, g5)


def _mlp_bwd(dh2, y, h1, u, w1, w2, g4, g3, tm):
    S = dh2.shape[0]

    def body(dh2_ref, y_ref, h1_ref, u_ref, w1_ref, w2_ref, g4_ref, g3_ref,
             dh1_ref, dy_ref, du_ref, dg4_ref, dg3_ref):
        i = pl.program_id(0)
        dh2 = dh2_ref[...]
        n4, r4 = _rms(y_ref[...])
        dy, dg4 = _rms_bwd(dh2, n4, r4, g4_ref[...])
        _accumulate(dg4_ref, dg4, i)
        dyb = dy.astype(MM)
        dy_ref[...] = dyb
        dm = jnp.zeros((tm, D_MODEL), F32)
        for j in range(N_DEV):
            cols = slice(j * FF_CHUNK, (j + 1) * FF_CHUNK)
            dact = _dot_nt(dyb, w2_ref[cols, :])
            du = (dact * (2.0 * jnp.maximum(u_ref[:, cols].astype(F32), 0.0))).astype(MM)
            du_ref[:, cols] = du
            dm = dm + _dot_nt(du, w1_ref[j])
        n3, r3 = _rms(h1_ref[...])
        dx, dg3 = _rms_bwd(dm, n3, r3, g3_ref[...])
        _accumulate(dg3_ref, dg3, i)
        dh1_ref[...] = dh2 + dx

    return pl.pallas_call(
        body, name="mlp_bwd", grid=(S // tm,),
        in_specs=[_rows(tm, D_MODEL), _rows(tm, D_MODEL), _rows(tm, D_MODEL), _rows(tm, D_FF),
                  _resident(), _resident(), _const((1, D_MODEL)), _const((1, D_MODEL))],
        out_specs=[_rows(tm, D_MODEL), _rows(tm, D_MODEL), _rows(tm, D_FF), _const((1, D_MODEL)),
                   _const((1, D_MODEL))],
        out_shape=[jax.ShapeDtypeStruct((S, D_MODEL), F32), jax.ShapeDtypeStruct((S, D_MODEL), MM),
                   jax.ShapeDtypeStruct((S, D_FF), MM), jax.ShapeDtypeStruct((1, D_MODEL), F32),
                   jax.ShapeDtypeStruct((1, D_MODEL), F32)],
        compiler_params=_params("arbitrary"),
    )(dh2, y, h1, u, w1, w2, g4, g3)


def _attn_out_bwd(dh1, mix, fox_o, swa_o, wout_fox, wout_swa, g2, head_rows, tm):
    S = dh1.shape[0]

    def body(dh1_ref, mix_ref, fo_ref, so_ref, wf_ref, ws_ref, g2_ref, er_ref,
             dmix_ref, dcat_ref, drow_ref, dswa_ref, dg2_ref):
        i = pl.program_id(0)
        n2, r2 = _rms(mix_ref[...])
        dmix, dg2 = _rms_bwd(dh1_ref[...], n2, r2, g2_ref[...])
        _accumulate(dg2_ref, dg2, i)
        dmb = dmix.astype(MM)
        dmix_ref[...] = dmb
        dfo = _dot_nt(dmb, wf_ref[...]).astype(MM)
        dso = _dot_nt(dmb, ws_ref[...]).astype(MM)
        dcat_ref[:, :D_ATT] = dfo
        dcat_ref[:, D_ATT:] = dso
        hi = lax.Precision.HIGHEST
        prod_f = dfo.astype(F32) * fo_ref[...].astype(F32)
        prod_s = dso.astype(F32) * so_ref[...].astype(F32)
        drow_ref[...] = lax.dot_general(er_ref[...], prod_f, NT, precision=hi, preferred_element_type=F32)
        dswa_ref[...] = lax.dot_general(er_ref[...], prod_s, NT, precision=hi, preferred_element_type=F32)

    return pl.pallas_call(
        body, name="attn_out_bwd", grid=(S // tm,),
        in_specs=[_rows(tm, D_MODEL), _rows(tm, D_MODEL), _rows(tm, D_ATT), _rows(tm, D_ATT), _resident(),
                  _resident(), _const((1, D_MODEL)), _resident()],
        out_specs=[_rows(tm, D_MODEL), _rows(tm, D_MODEL), pl.BlockSpec((N_HEADS, tm), lambda i: (0, i)),
                   pl.BlockSpec((N_HEADS, tm), lambda i: (0, i)), _const((1, D_MODEL))],
        out_shape=[jax.ShapeDtypeStruct((S, D_MODEL), MM), jax.ShapeDtypeStruct((S, D_MODEL), MM),
                   jax.ShapeDtypeStruct((N_HEADS, S), F32), jax.ShapeDtypeStruct((N_HEADS, S), F32),
                   jax.ShapeDtypeStruct((1, D_MODEL), F32)],
        compiler_params=_params("arbitrary"),
    )(dh1, mix, fox_o, swa_o, wout_fox, wout_swa, g2, head_rows)


def _fox_bwd(fqkv, dcat, lse_row3, d_row3, c_col, tq, tk, pairs_per_loop=2):
    S = fqkv.shape[0]
    n_blk = S // tk
    n_qblk = S // tq
    n_band = tk // tq

    def body(q_ref, k_ref, v_ref, do_ref, lse_ref, dd_ref, ck_ref, dq_ref, dk_ref, dv_ref, dc_ref, dcq_ref):
        kb = pl.program_id(0)

        @pl.when(kb == 0)
        def _():
            dq_ref[...] = jnp.zeros_like(dq_ref)
            dcq_ref[...] = jnp.zeros_like(dcq_ref)

        key = lax.broadcasted_iota(jnp.int32, (tk, tq), 0)
        qry = lax.broadcasted_iota(jnp.int32, (tk, tq), 1)
        low = _head_select((tk, 128), 0)
        for first in range(0, N_HEADS // 2, pairs_per_loop):
            pairs = range(first, first + pairs_per_loop)
            heads = [(pr, hh) for pr in pairs for hh in range(2)]
            kh, vh, ck = {}, {}, {}
            for pr in pairs:
                k2 = k_ref[:, pr * 128:(pr + 1) * 128]
                v2 = v_ref[:, pr * 128:(pr + 1) * 128]
                zero = jnp.zeros_like(k2)
                kh[pr, 0], kh[pr, 1] = jnp.where(low, k2, zero), jnp.where(low, zero, k2)
                vh[pr, 0], vh[pr, 1] = jnp.where(low, v2, zero), jnp.where(low, zero, v2)
                for hh in range(2):
                    ck[pr, hh] = ck_ref[:, 2 * pr + hh:2 * pr + hh + 1]

            def block(qb, carry, band, pairs=pairs, kh=kh, vh=vh, ck=ck):
                rows = pl.ds(pl.multiple_of(qb * tq, tq), tq)
                k1 = tk if band is None else (band + 1) * tq
                out = []
                it = iter(carry)
                for pr in pairs:
                    lanes = slice(pr * 128, (pr + 1) * 128)
                    q2 = q_ref[rows, lanes]
                    do2 = do_ref[rows, lanes]
                    dq = None
                    for hh in range(2):
                        h = 2 * pr + hh
                        dk, dv, dc = next(it)
                        s_t = _dot_nt(kh[pr, hh][:k1], q2) - ck[pr, hh][:k1]
                        p_t = jnp.exp(s_t - lse_ref[h, pl.ds(qb, 1), :])
                        if band is not None:
                            p_t = jnp.where(qry[:k1] + band * tq >= key[:k1], p_t, 0.0)
                        ds_t = p_t * (_dot_nt(vh[pr, hh][:k1], do2) - dd_ref[h, pl.ds(qb, 1), :])
                        dsb = ds_t.astype(MM)
                        dv_new = dv[:k1] + _dot(p_t.astype(MM), do2)
                        dk_new = dk[:k1] + _dot(dsb, q2)
                        dc_new = dc[:k1] - jnp.sum(ds_t, axis=1, keepdims=True)
                        if k1 < tk:
                            dv_new = jnp.concatenate([dv_new, dv[k1:]], axis=0)
                            dk_new = jnp.concatenate([dk_new, dk[k1:]], axis=0)
                            dc_new = jnp.concatenate([dc_new, dc[k1:]], axis=0)
                        part = _dot_tn(dsb, kh[pr, hh][:k1])
                        dq = part if dq is None else dq + part
                        dcq_ref[h, pl.ds(qb, 1), :] += jnp.sum(ds_t, axis=0, keepdims=True)
                        out.append((dk_new, dv_new, dc_new))
                    dq_ref[rows, lanes] += dq
                return tuple(out)

            carry = tuple((jnp.zeros((tk, 128), F32), jnp.zeros((tk, 128), F32), jnp.zeros((tk, 1), F32))
                          for _ in heads)
            for band in range(n_band):
                carry = block(kb * n_band + band, carry, band=band)
            carry = lax.fori_loop((kb + 1) * n_band, n_qblk, functools.partial(block, band=None), carry)
            grads = dict(zip(heads, carry))
            for pr in pairs:
                lanes = slice(pr * 128, (pr + 1) * 128)
                dk_ref[:, lanes] = jnp.where(low, grads[pr, 0][0], grads[pr, 1][0]).astype(MM)
                dv_ref[:, lanes] = jnp.where(low, grads[pr, 0][1], grads[pr, 1][1]).astype(MM)
                for hh in range(2):
                    dc_ref[:, 2 * pr + hh:2 * pr + hh + 1] = grads[pr, hh][2]

        @pl.when(kb == n_blk - 1)
        def _():
            dq_ref[...] = dq_ref[...] * Q_SCALE

    return pl.pallas_call(
        body, name="fox_bwd", grid=(n_blk,),
        in_specs=[pl.BlockSpec((S, D_ATT), lambda i: (0, 0)), pl.BlockSpec((tk, D_ATT), lambda i: (i, 1)),
                  pl.BlockSpec((tk, D_ATT), lambda i: (i, 2)), pl.BlockSpec((S, D_ATT), lambda i: (0, 0)),
                  _resident(), _resident(), _rows(tk, N_HEADS)],
        out_specs=[_const((S, D_ATT)), _rows(tk, D_ATT), _rows(tk, D_ATT), _rows(tk, N_HEADS),
                   _const((N_HEADS, n_qblk, tq))],
        out_shape=[jax.ShapeDtypeStruct((S, D_ATT), F32), jax.ShapeDtypeStruct((S, D_ATT), MM),
                   jax.ShapeDtypeStruct((S, D_ATT), MM), jax.ShapeDtypeStruct((S, N_HEADS), F32),
                   jax.ShapeDtypeStruct((N_HEADS, n_qblk, tq), F32)],
        compiler_params=_params("arbitrary"),
    )(fqkv, fqkv, fqkv, dcat, lse_row3, d_row3, c_col)


def _swa_bwd(sqkv, dcat, biasm, sinks_slot, bucket, lse, d_col):
    S = sqkv.shape[0]
    n_blk = S // WINDOW

    def body(q_ref, kp_ref, kc_ref, vp_ref, vc_ref, do_ref, bias_ref, sink_ref, bk_ref, lse_ref, dd_ref,
             dq_ref, dk_ref, dv_ref, drb_ref, dsink_ref, ds_acc):
        n = pl.program_id(0)

        @pl.when(n == 0)
        def _():
            dk_ref[...] = jnp.zeros_like(dk_ref)
            dv_ref[...] = jnp.zeros_like(dv_ref)
            ds_acc[...] = jnp.zeros_like(ds_acc)
            dsink_ref[...] = jnp.zeros_like(dsink_ref)

        no_prev = jnp.where(n > 0, 0.0, NEG)
        prev = pl.ds(pl.multiple_of(jnp.maximum(n - 1, 0) * WINDOW, WINDOW), WINDOW)
        cur = pl.ds(pl.multiple_of(n * WINDOW, WINDOW), WINDOW)
        lane8 = lax.broadcasted_iota(jnp.int32, (1, N_HEADS), 1)
        dkp = jnp.zeros((WINDOW, D_KV), F32)
        dkc = jnp.zeros((WINDOW, D_KV), F32)
        dvp = jnp.zeros((WINDOW, D_KV), F32)
        dvc = jnp.zeros((WINDOW, D_KV), F32)
        dsink = jnp.zeros((1, N_HEADS), F32)
        low = _head_select((WINDOW, 128), 0)
        zero = jnp.zeros((WINDOW, 128), MM)
        dqs = []
        for g in range(2):
            sel = low if g == 0 else jnp.logical_not(low)
            qg = _stack4(lambda j: jnp.where(sel, q_ref[:, j * 128:(j + 1) * 128], zero))
            dog = _stack4(lambda j: jnp.where(sel, do_ref[:, j * 128:(j + 1) * 128], zero))
            lse_g = _stack4(lambda j: lse_ref[:, 2 * j + g:2 * j + g + 1])
            dd = _stack4(lambda j: dd_ref[:, 2 * j + g:2 * j + g + 1])
            sink = _stack4(lambda j: jnp.full((WINDOW, 1), sink_ref[2 * j + g], F32))
            pp = jnp.exp(_dot_nt(qg, kp_ref[...]) + _stack4(lambda j: bias_ref[2 * j + g, :, :WINDOW]) + no_prev - lse_g)
            pc = jnp.exp(_dot_nt(qg, kc_ref[...]) + _stack4(lambda j: bias_ref[2 * j + g, :, WINDOW:]) - lse_g)
            sink_term = jnp.exp(sink - lse_g) * dd
            dsp = pp * (_dot_nt(dog, vp_ref[...]) - dd)
            dsc = pc * (_dot_nt(dog, vc_ref[...]) - dd)
            for j in range(4):
                rows = slice(j * WINDOW, (j + 1) * WINDOW)
                dsink = dsink + jnp.where(lane8 == 2 * j + g, -jnp.sum(sink_term[rows]), 0.0)
                ds_acc[2 * j + g, :, :WINDOW] += dsp[rows]
                ds_acc[2 * j + g, :, WINDOW:] += dsc[rows]
            dspb, dscb = dsp.astype(MM), dsc.astype(MM)
            dqs.append(_dot(dspb, kp_ref[...]) + _dot(dscb, kc_ref[...]))
            dkp = dkp + _dot_tn(dspb, qg)
            dkc = dkc + _dot_tn(dscb, qg)
            dvp = dvp + _dot_tn(pp.astype(MM), dog)
            dvc = dvc + _dot_tn(pc.astype(MM), dog)
        for j in range(4):
            rows = slice(j * WINDOW, (j + 1) * WINDOW)
            dq_ref[:, j * 128:(j + 1) * 128] = (jnp.where(low, dqs[0][rows], dqs[1][rows]) * Q_SCALE).astype(MM)
        dk_ref[prev, :] += dkp
        dk_ref[cur, :] += dkc
        dv_ref[prev, :] += dvp
        dv_ref[cur, :] += dvc
        dsink_ref[...] += dsink

        @pl.when(n == n_blk - 1)
        def _():
            bk = bk_ref[...]
            rb = lax.broadcasted_iota(jnp.int32, (N_BUCKETS, N_HEADS), 0)
            cb = lax.broadcasted_iota(jnp.int32, (N_BUCKETS, N_HEADS), 1)
            out = jnp.zeros((N_BUCKETS, N_HEADS), F32)
            for s in range(N_HEADS):
                acc = ds_acc[s]
                for b in range(N_BUCKETS):
                    out = out + jnp.where((rb == b) & (cb == s), jnp.sum(jnp.where(bk == b, acc, 0.0)), 0.0)
            drb_ref[...] = out

    do_spec = pl.BlockSpec((WINDOW, D_ATT), lambda n: (n, 1))
    return pl.pallas_call(
        body, name="swa_bwd", grid=(n_blk,),
        in_specs=_swa_specs(S) + [do_spec, _resident(), pl.BlockSpec(memory_space=pltpu.SMEM), _resident(),
                                  _rows(WINDOW, N_HEADS), _rows(WINDOW, N_HEADS)],
        out_specs=[_rows(WINDOW, D_ATT), _const((S, D_KV)), _const((S, D_KV)), _const((N_BUCKETS, N_HEADS)),
                   _const((1, N_HEADS))],
        out_shape=[jax.ShapeDtypeStruct((S, D_ATT), MM), jax.ShapeDtypeStruct((S, D_KV), F32),
                   jax.ShapeDtypeStruct((S, D_KV), F32), jax.ShapeDtypeStruct((N_BUCKETS, N_HEADS), F32),
                   jax.ShapeDtypeStruct((1, N_HEADS), F32)],
        scratch_shapes=[pltpu.VMEM((N_HEADS, WINDOW, 2 * WINDOW), F32)],
        compiler_params=_params("arbitrary"),
    )(sqkv, sqkv, sqkv, sqkv, sqkv, dcat, biasm, sinks_slot, bucket, lse, d_col)


D_Z = 3 * D_ATT + D_ATT + 2 * D_KV


def _pack_dz(dq_fox, dk_fox, dv_fox, dsq, dsk, dsv, tm):
    S = dq_fox.shape[0]

    def body(dq_ref, dk_ref, dv_ref, dsq_ref, dsk_ref, dsv_ref, dz_ref):
        dz_ref[:, 0:512] = dq_ref[...].astype(MM)
        dz_ref[:, 512:1024] = dk_ref[...]
        dz_ref[:, 1024:1536] = dv_ref[...]
        dz_ref[:, 1536:2048] = dsq_ref[...]
        dz_ref[:, 2048:2176] = dsk_ref[...].astype(MM)
        dz_ref[:, 2176:2304] = dsv_ref[...].astype(MM)

    return pl.pallas_call(
        body, name="pack_dz", grid=(S // tm,),
        in_specs=[_rows(tm, D_ATT), _rows(tm, D_ATT), _rows(tm, D_ATT), _rows(tm, D_ATT), _rows(tm, D_KV),
                  _rows(tm, D_KV)],
        out_specs=_rows(tm, D_Z), out_shape=jax.ShapeDtypeStruct((S, D_Z), MM),
        compiler_params=_params("parallel"),
    )(dq_fox, dk_fox, dv_fox, dsq, dsk, dsv)


def _pre_attn_bwd(x, dh1, dz, dff_t, wt, g1, tm):
    S = x.shape[0]

    def body(x_ref, dh1_ref, dz_ref, dff_ref, wt_ref, g1_ref, dx_ref, dg1_ref):
        i = pl.program_id(0)
        da = (_dot(dz_ref[:, 0:WT_FF], wt_ref[0:WT_FF]) + _dot(dz_ref[:, WT_FF:D_Z], wt_ref[WT_SQ:WT_ROWS])
              + _dot_tn(dff_ref[...].astype(MM), wt_ref[WT_FF:WT_SQ]))
        n1, r1 = _rms(x_ref[...])
        dx, dg1 = _rms_bwd(da, n1, r1, g1_ref[...])
        _accumulate(dg1_ref, dg1, i)
        dx_ref[...] = dh1_ref[...] + dx

    return pl.pallas_call(
        body, name="pre_attn_bwd", grid=(S // tm,),
        in_specs=[_rows(tm, D_MODEL), _rows(tm, D_MODEL), _rows(tm, D_Z), pl.BlockSpec((16, tm), lambda i: (0, i)),
                  _resident(), _const((1, D_MODEL))],
        out_specs=[_rows(tm, D_MODEL), _const((1, D_MODEL))],
        out_shape=[jax.ShapeDtypeStruct((S, D_MODEL), F32), jax.ShapeDtypeStruct((1, D_MODEL), F32)],
        compiler_params=_params("arbitrary"),
    )(x, dh1, dz, dff_t, wt, g1)


def _weight_grad(a, b, name, tk, n_chunks=1, relu2=False):
    S, K = a.shape
    N = b.shape[1]
    cn = N // n_chunks

    def body(a_ref, b_ref, out_ref):
        av = a_ref[...]
        if relu2:
            av = jnp.square(jnp.maximum(av.astype(F32), 0.0))
        av = av.astype(MM)
        for j in range(n_chunks):
            val = _dot_tn(av, b_ref[:, j * cn:(j + 1) * cn].astype(MM)).astype(MM)
            if n_chunks > 1:
                out_ref[j] = val
            else:
                out_ref[...] = val

    if n_chunks > 1:
        out_spec = pl.BlockSpec((n_chunks, tk, cn), lambda i: (0, i, 0))
        out_shape = jax.ShapeDtypeStruct((n_chunks, K, cn), MM)
    else:
        out_spec = pl.BlockSpec((tk, N), lambda i: (i, 0))
        out_shape = jax.ShapeDtypeStruct((K, N), MM)
    return pl.pallas_call(
        body, name=name, grid=(K // tk,),
        in_specs=[pl.BlockSpec((S, tk), lambda i: (0, i)), _resident()],
        out_specs=out_spec, out_shape=out_shape, compiler_params=_params("parallel"),
    )(a, b)


def _forget_weight_grad(dff_t, a):
    def body(d_ref, a_ref, out_ref):
        out_ref[...] = _dot(d_ref[...].astype(MM), a_ref[...])

    return pl.pallas_call(
        body, name="forget_weight_grad", out_shape=jax.ShapeDtypeStruct((16, D_MODEL), F32),
        in_specs=[_resident(), _resident()], out_specs=_resident(),
    )(dff_t, a)


def _place():
    return lax.axis_index("x"), lax.axis_index("y"), lax.axis_index("c")


def _all_gather_sequencer(stacks, name, collective_id):
    refs = [jax.new_ref(s, memory_space=pltpu.MemorySpace.HBM) for s in stacks]
    n = len(refs)

    @pl.kernel(mesh=plsc.ScalarSubcoreMesh(axis_name="sequencer", num_cores=1), name=name,
               scratch_types=(pltpu.SemaphoreType.DMA((7 * n,)), pltpu.SemaphoreType.DMA((7 * n,))),
               compiler_params=pltpu.CompilerParams(collective_id=collective_id))
    def launch(send_sems, recv_sems):
        x, y, c = _place()
        sibling = (x, y, 1 - c)
        chips = [(1 - x, y), (x, 1 - y), (1 - x, 1 - y)]
        peers = [sibling] + [(px, py, c) for px, py in chips]
        barrier = pltpu.get_barrier_semaphore()
        for peer in peers:
            pl.semaphore_signal(barrier, inc=1, device_id=peer, device_id_type=MESH)
        pl.semaphore_wait(barrier, len(peers))

        def copy(a, k, block, to):
            px, py, pc = block
            slot = refs[a].at[4 * px + 2 * py + pc]
            return _remote(slot, slot, send_sems, recv_sems, 7 * a + k, to)

        first = [copy(a, k, (x, y, c), peer) for a in range(n) for k, peer in enumerate(peers)]
        for cp in first:
            cp.start()
        passed = []
        for j, (px, py) in enumerate(chips):
            for a in range(n):
                copy(a, 1 + j, (px, py, c), sibling).wait_recv()
                passed.append(copy(a, 4 + j, (px, py, c), sibling))
                passed[-1].start()
        for a in range(n):
            copy(a, 0, (x, y, 1 - c), sibling).wait_recv()
            for j, (px, py) in enumerate(chips):
                copy(a, 4 + j, (px, py, 1 - c), sibling).wait_recv()
        for cp in first + passed:
            cp.wait_send()

    launch()
    return [ref[...] for ref in refs]


def _chip_sum(grad, other, name):
    _, _, r, cdim = grad.shape
    tr = 512 if r % 512 == 0 else r

    def body(c_ref, g_ref, o_ref, out_ref):
        out_ref[...] = (g_ref[...].astype(F32) + o_ref[...].astype(F32)).astype(out_ref.dtype)

    return pl.pallas_call(
        body, name=name,
        grid_spec=pltpu.PrefetchScalarGridSpec(
            num_scalar_prefetch=1, grid=(4, r // tr),
            in_specs=[pl.BlockSpec((None, None, tr, cdim), lambda k, i, c_ref: (k, c_ref[0], i, 0)),
                      pl.BlockSpec((None, tr, cdim), lambda k, i, c_ref: (k, i, 0))],
            out_specs=pl.BlockSpec((None, tr, cdim), lambda k, i, c_ref: (k, i, 0))),
        out_shape=jax.ShapeDtypeStruct((4, r, cdim), MM),
        compiler_params=_params("parallel", "parallel"),
    )(lax.axis_index("c").astype(jnp.int32).reshape(1), grad, other)


HBM_SPEC = pl.BlockSpec(memory_space=pltpu.HBM)
SEM_SPEC = pl.BlockSpec(memory_space=pltpu.SEMAPHORE)
DATAFLOW = pltpu.SideEffectType.DATAFLOW_SIDE_EFFECTING


def _exchange_start(name, arrays, n_copies, plan):
    n = len(arrays)

    def body(*refs):
        send_sems, recv_sems, token = refs[n], refs[n + 1], refs[2 * n + 2]
        for cp in plan(refs[:n], send_sems, recv_sems):
            cp.start()
        token[...] = jnp.zeros_like(token)

    out = pl.pallas_call(
        body, name=name,
        out_shape=(pltpu.SemaphoreType.DMA((n_copies,)), pltpu.SemaphoreType.DMA((n_copies,)),
                   *[pltpu.HBM(a.shape, a.dtype) for a in arrays], jax.ShapeDtypeStruct((1, D_MODEL), F32)),
        in_specs=[HBM_SPEC] * n,
        out_specs=(SEM_SPEC, SEM_SPEC, *[HBM_SPEC] * n, pl.BlockSpec(memory_space=pltpu.VMEM)),
        input_output_aliases={i: 2 + i for i in range(n)},
        compiler_params=pltpu.CompilerParams(has_side_effects=DATAFLOW),
    )(*[pltpu.with_memory_space_constraint(a, pltpu.HBM) for a in arrays])
    return (out[0], out[1]), list(out[2:2 + n]), out[2 + n]


def _exchange_wait(name, arrays, sems, after, plan):
    n = len(arrays)
    after = list(after) if isinstance(after, (list, tuple)) else [after]

    def body(*refs):
        send_sems, recv_sems = refs[n], refs[n + 1]
        for cp in plan(refs[:n], send_sems, recv_sems):
            cp.wait_send()
            cp.wait_recv()

    out = pl.pallas_call(
        body, name=name, out_shape=[pltpu.HBM(a.shape, a.dtype) for a in arrays],
        in_specs=[HBM_SPEC] * n + [SEM_SPEC, SEM_SPEC] + [pl.BlockSpec(memory_space=pl.ANY)] * len(after),
        out_specs=[HBM_SPEC] * n, input_output_aliases={i: i for i in range(n)},
        compiler_params=pltpu.CompilerParams(has_side_effects=DATAFLOW),
    )(*arrays, sems[0], sems[1], *after)
    return list(out)


def _remote(src, dst, send_sems, recv_sems, k, to):
    return pltpu.make_async_remote_copy(src_ref=src, dst_ref=dst, send_sem=send_sems.at[k], recv_sem=recv_sems.at[k],
                                        device_id=to, device_id_type=MESH)


def _plan_gather_direct(refs, send_sems, recv_sems):
    x, y, c = _place()
    me = 4 * x + 2 * y + c
    peers = [(x, y, 1 - c), (1 - x, y, c), (x, 1 - y, c), (1 - x, 1 - y, c)]
    return [_remote(ref.at[me], ref.at[me], send_sems, recv_sems, 4 * a + k, peer)
            for a, ref in enumerate(refs) for k, peer in enumerate(peers)]


def _plan_gather_pass_on(refs, send_sems, recv_sems):
    x, y, c = _place()
    chips = [(1 - x, y), (x, 1 - y), (1 - x, 1 - y)]
    return [_remote(ref.at[4 * px + 2 * py + c], ref.at[4 * px + 2 * py + c], send_sems, recv_sems, 3 * a + k,
                    (x, y, 1 - c))
            for a, ref in enumerate(refs) for k, (px, py) in enumerate(chips)]


def _plan_in_chip(refs, send_sems, recv_sems):
    n = len(refs) // 2
    x, y, c = _place()
    return [_remote(refs[a].at[:, 1 - c], refs[n + a], send_sems, recv_sems, a, (x, y, 1 - c)) for a in range(n)]


def _plan_between_chips(refs, send_sems, recv_sems):
    n = len(refs) // 2
    x, y, c = _place()
    chips = [(1 - x, y), (x, 1 - y), (1 - x, 1 - y)]
    return [_remote(refs[a].at[2 * px + py], refs[n + a].at[2 * x + y], send_sems, recv_sems, 3 * a + k, (px, py, c))
            for a in range(n) for k, (px, py) in enumerate(chips)]


def _plan_late_between(refs, send_sems, recv_sems):
    sums, land, small = refs
    x, y, c = _place()
    me = 4 * x + 2 * y + c
    copies = _plan_between_chips([sums, land], send_sems, recv_sems)
    peers = [(x ^ dx, y ^ dy, c ^ dc) for dx in range(2) for dy in range(2) for dc in range(2) if dx + dy + dc]
    return copies + [_remote(small.at[me], small.at[me], send_sems, recv_sems, 3 + k, peer)
                     for k, peer in enumerate(peers)]


def _adamw_math(w, g, m, v):
    m = ADAM_B1 * m + (1.0 - ADAM_B1) * g
    v = ADAM_B2 * v + (1.0 - ADAM_B2) * jnp.square(g)
    m_hat = m / (1.0 - ADAM_B1 ** ADAM_STEP)
    v_hat = v / (1.0 - ADAM_B2 ** ADAM_STEP)
    delta = -ADAM_LR * (m_hat / (jnp.sqrt(v_hat) + ADAM_EPS) + ADAM_WD * w)
    return delta, m, v


def _adamw(parts, w, m, v, name):
    n_parts, r, cdim = parts.shape
    tr = 256 if r % 256 == 0 else r

    def body(p_ref, w_ref, m_ref, v_ref, g_out, d_out, m_out, v_out):
        g = p_ref[0].astype(F32)
        for k in range(1, n_parts):
            g = g + p_ref[k].astype(F32)
        delta, m_new, v_new = _adamw_math(w_ref[...], g, m_ref[...], v_ref[...])
        g_out[...] = g
        d_out[...] = delta
        m_out[...] = m_new
        v_out[...] = v_new

    blk = pl.BlockSpec((tr, cdim), lambda i: (i, 0))
    return pl.pallas_call(
        body, name=name, grid=(r // tr,),
        in_specs=[pl.BlockSpec((n_parts, tr, cdim), lambda i: (0, i, 0)), blk, blk, blk],
        out_specs=[blk] * 4, out_shape=[jax.ShapeDtypeStruct((r, cdim), F32)] * 4,
        compiler_params=_params("parallel"),
    )(parts, w, m, v)


def _adamw_chips(parts, sums, w, m, v, name):
    _, r, cdim = parts.shape
    tr = 256 if r % 256 == 0 else r

    def body(chip_ref, p_ref, own_ref, w_ref, m_ref, v_ref, g_out, d_out, m_out, v_out):
        g = None
        for k in range(4):
            term = jnp.where(chip_ref[0] == k, own_ref[...], p_ref[k]).astype(F32)
            g = term if g is None else g + term
        delta, m_new, v_new = _adamw_math(w_ref[...], g, m_ref[...], v_ref[...])
        g_out[...] = g
        d_out[...] = delta
        m_out[...] = m_new
        v_out[...] = v_new

    blk = pl.BlockSpec((tr, cdim), lambda i, chip: (i, 0))
    my_chip = (2 * lax.axis_index("x") + lax.axis_index("y")).astype(jnp.int32).reshape(1)
    return pl.pallas_call(
        body, name=name,
        grid_spec=pltpu.PrefetchScalarGridSpec(
            num_scalar_prefetch=1, grid=(r // tr,),
            in_specs=[pl.BlockSpec((4, tr, cdim), lambda i, chip: (0, i, 0)),
                      pl.BlockSpec((None, tr, cdim), lambda i, chip: (chip[0], i, 0)), blk, blk, blk],
            out_specs=[blk] * 4),
        out_shape=[jax.ShapeDtypeStruct((r, cdim), F32)] * 4,
        compiler_params=_params("parallel"),
    )(my_chip, parts, sums, w, m, v)


class _NoExchange:
    def __init__(self, weights):
        self.weights = weights

    def before_pre_attn(self, g1):
        return g1

    def after_fox_fwd(self, fox_o, sinks_slot):
        return sinks_slot

    def after_attention(self, swa_o):
        return self.weights

    def after_early_grads(self, grads, d_col):
        return d_col

    def after_swa_bwd(self, dsq, d_row3):
        return d_row3

    def after_w_in_grad(self, d_win, g1):
        return g1


def _slot_order(t, axis):
    shp = t.shape
    t = t.reshape(shp[:axis] + (2, 4, shp[axis] // N_HEADS) + shp[axis + 1:])
    return jnp.swapaxes(t, axis, axis + 1).reshape(shp)


def _head_order(t, axis):
    shp = t.shape
    t = t.reshape(shp[:axis] + (4, 2, shp[axis] // N_HEADS) + shp[axis + 1:])
    return jnp.swapaxes(t, axis, axis + 1).reshape(shp)


def _forward_backward(x, p, target, win_t, hooks, b_forget, rel_bias, sinks, g1, g2, g3, g4, g5):
    S = x.shape[0]
    tm = 512
    tm_mlp = 512
    t = 256
    q0 = 3 * D_ATT + N_HEADS
    win_t = win_t.reshape(D_IN, D_MODEL)
    wt = jnp.concatenate(
        [win_t[:q0], jnp.zeros((8, D_MODEL), MM), _slot_order(win_t[q0:q0 + D_ATT], 0), win_t[q0 + D_ATT:]], axis=0)
    bcol = jnp.pad(b_forget.reshape(N_HEADS, 1), ((0, 8), (0, 0)))
    rel_bias_slot = rel_bias[:, np.array(SLOT_HEAD)]
    sinks_slot = sinks.reshape(N_HEADS)[np.array(SLOT_HEAD)]
    bucket = jnp.asarray(_swa_bucket_map())

    a, fqkv, sqkv, fft = _pre_attn(x, hooks.before_pre_attn(g1), wt, tm)
    c_row = _forget_cumsum(fft, bcol)
    c_col = c_row[:N_HEADS].T
    c_row3 = c_row[:N_HEADS].reshape(N_HEADS, S // t, t)
    fox_o, fox_lse = _fox_fwd(fqkv, c_row3, tq=512, tk=t)
    biasm = _swa_bias(rel_bias_slot, bucket)
    sinks_slot = hooks.after_fox_fwd(fox_o, sinks_slot)
    swa_o, swa_lse = _swa_fwd(sqkv, biasm, sinks_slot)
    wout, w1, w2, wple, wg = hooks.after_attention(swa_o)
    wout_fox = wout[:D_ATT]
    wout_swa = _slot_order(wout[D_ATT:], 0)
    mix, h1, m = _post_attn(x, fox_o, swa_o, wout_fox, wout_swa, g2, g3, tm)
    u, y, h2 = _mlp_fwd(m, h1, w1, w2, g4, tm_mlp)
    dh2, dpe, dgl, dg5, loss = _ple_loss(h2, p, target, wg, wple, g5, tm)

    d_wple = _weight_grad(p, dpe, "grad_w_ple", tk=D_PLE, n_chunks=N_DEV)
    d_wg = _weight_grad(h2, dgl, "grad_w_ple_gate", tk=256)
    dh1, dy, du, dg4, dg3 = _mlp_bwd(dh2, y, h1, u, w1, w2, g4, g3, tm)
    d_w2 = _weight_grad(u, dy, "grad_w_ff2", tk=256, relu2=True)
    d_w1 = _weight_grad(m, du, "grad_w_ff1", tk=256, n_chunks=N_DEV)
    head = np.arange(D_ATT) // HEAD_DIM
    head_rows = jnp.asarray((head[None, :] == np.arange(N_HEADS)[:, None]).astype(np.float32))
    dmix, dcat, d_row, d_swa, dg2 = _attn_out_bwd(dh1, mix, fox_o, swa_o, wout_fox, wout_swa, g2, head_rows, tm)
    d_col = d_swa.T
    d_wout_fox = _weight_grad(fox_o, dmix, "grad_w_out_fox", tk=256)
    d_wout_swa = _weight_grad(swa_o, dmix, "grad_w_out_swa", tk=256)
    d_wout = jnp.concatenate([d_wout_fox, _head_order(d_wout_swa, 0)], axis=0).reshape(N_DEV, D_MODEL // N_DEV, D_MODEL)
    early = dict(w_ff1=d_w1, w_ff2=d_w2.reshape(N_DEV, FF_CHUNK, D_MODEL), w_ple=d_wple,
                 w_ple_gate=d_wg.reshape(N_DEV, D_MODEL // N_DEV, D_MODEL), w_out=d_wout)

    d_col = hooks.after_early_grads(early, d_col)
    dsq, dsk, dsv, d_rb_slot, d_sink_slot = _swa_bwd(sqkv, dcat, biasm, sinks_slot, bucket, swa_lse, d_col)
    lse_row3 = fox_lse.T.reshape(N_HEADS, S // t, t)
    d_row3 = hooks.after_swa_bwd(dsq, d_row.reshape(N_HEADS, S // t, t))
    dq_fox, dk_fox, dv_fox, dc_col, dcq = _fox_bwd(fqkv, dcat, lse_row3, d_row3, c_col, tq=t, tk=512)
    dc_row = jnp.pad(dc_col.T + dcq.reshape(N_HEADS, S), ((0, 8), (0, 0)))
    dff_t, db = _forget_bwd(dc_row, fft, bcol)
    dz = _pack_dz(dq_fox, dk_fox, dv_fox, dsq, dsk, dsv, 512)
    d_wmain = _weight_grad(dz, a, "grad_w_in", tk=256)
    d_wff_t = _forget_weight_grad(dff_t, a)

    sq0 = 3 * D_ATT
    d_win = jnp.concatenate(
        [d_wmain[:sq0], d_wff_t[:N_HEADS].astype(MM), _head_order(d_wmain[sq0:sq0 + D_ATT], 0),
         d_wmain[sq0 + D_ATT:]], axis=0)
    d_win = d_win.reshape(N_DEV, D_IN // N_DEV, D_MODEL)
    grad_x, dg1 = _pre_attn_bwd(x, dh1, dz, dff_t, wt, hooks.after_w_in_grad(d_win, g1), tm)
    big = dict(early, w_in=d_win)
    small = dict(b_forget=db[:N_HEADS].reshape(1, N_HEADS), rel_bias=d_rb_slot[:, np.array(HEAD_SLOT)],
                 swa_sinks=d_sink_slot[:, np.array(HEAD_SLOT)], g_attn_pre=dg1, g_attn_post=dg2, g_ff_pre=dg3,
                 g_ff_post=dg4, g_ple_post=dg5)
    return loss, grad_x, big, small


BIG = ("w_in", "w_out", "w_ff1", "w_ff2", "w_ple", "w_ple_gate")
SMALL_ROWS = ("g_attn_pre", "g_attn_post", "g_ff_pre", "g_ff_post", "g_ple_post")
WEIGHTS =("w_in", "b_forget", "w_out", "rel_bias", "swa_sinks", "g_attn_pre", "g_attn_post", "w_ff1", "w_ff2",
           "g_ff_pre", "g_ff_post", "w_ple", "w_ple_gate", "g_ple_post")


EARLY = ("w_ff1", "w_ff2", "w_ple", "w_ple_gate", "w_out")


class _Overlap:
    def __init__(self, later):
        self.later = later

    def before_pre_attn(self, g1):
        self.gather_sems, self.later, token = _exchange_start("gather_rest_start", self.later, 4 * 5, _plan_gather_direct)
        return g1 + token

    def after_fox_fwd(self, fox_o, sinks_slot):
        later = _exchange_wait("gather_rest_wait", self.later, self.gather_sems, fox_o, _plan_gather_direct)
        self.pass_sems, self.later, token = _exchange_start("gather_pass_on_start", later, 3 * 5, _plan_gather_pass_on)
        return sinks_slot + token[0, :N_HEADS]

    def after_attention(self, swa_o):
        wout_g, w1_g, w2_g, wple_g, wg_g = _exchange_wait("gather_pass_on_wait", self.later, self.pass_sems, swa_o,
                                                         _plan_gather_pass_on)
        return (wout_g.reshape(D_MODEL, D_MODEL), w1_g, w2_g.reshape(D_FF, D_MODEL),
                jnp.moveaxis(wple_g, 0, 1).reshape(D_PLE, D_MODEL), wg_g.reshape(D_MODEL, D_MODEL))

    def after_early_grads(self, grads, d_col):
        views = [grads[k].reshape((4, 2) + grads[k].shape[1:]) for k in EARLY]
        lands = [lax.empty((4,) + grads[k].shape[1:], MM) for k in EARLY]
        self.in_chip_sems, self.in_chip, token = _exchange_start("grads_in_chip_start", views + lands, len(EARLY),
                                                                 _plan_in_chip)
        return d_col + token[0, 0]

    def after_swa_bwd(self, dsq, d_row3):
        arrays = _exchange_wait("grads_in_chip_wait", self.in_chip, self.in_chip_sems, dsq, _plan_in_chip)
        n = len(EARLY)
        sums = [_chip_sum(arrays[a], arrays[n + a], "chip_sum_" + k) for a, k in enumerate(EARLY)]
        lands = [lax.empty(s.shape, s.dtype) for s in sums]
        self.between_sems, self.between, token = _exchange_start("grads_between_chips_start", sums + lands, 3 * n,
                                                                 _plan_between_chips)
        return d_row3 + token[0, 0]

    def after_w_in_grad(self, d_win, g1):
        self.late_in_chip_sems, self.late_in_chip, token = _exchange_start(
            "late_in_chip_start", [d_win.reshape((4, 2) + d_win.shape[1:]), lax.empty((4,) + d_win.shape[1:], MM)],
            1, _plan_in_chip)
        return g1 + token

    def finish(self, after):
        arrays = _exchange_wait("grads_between_chips_wait", self.between, self.between_sems, after,
                                _plan_between_chips)
        n = len(EARLY)
        self.sums = arrays[:n]
        return arrays[n:]


def _pack_small(t):
    rows = [t[k].reshape(1, D_MODEL) for k in SMALL_ROWS]
    misc = jnp.concatenate([t["b_forget"].reshape(-1), t["swa_sinks"].reshape(-1), t["rel_bias"].reshape(-1)])
    rows.append(jnp.pad(misc, (0, D_MODEL - misc.shape[0])).reshape(1, D_MODEL))
    rows.append(jnp.pad(t["loss"].reshape(-1), (0, D_MODEL - 1)).reshape(1, D_MODEL))
    rows.append(jnp.zeros((1, D_MODEL), F32))
    return jnp.concatenate(rows, axis=0).astype(F32)


def _unpack_small(blk):
    out = {k: blk[i].reshape(1, D_MODEL) for i, k in enumerate(SMALL_ROWS)}
    misc = blk[len(SMALL_ROWS)]
    out["b_forget"] = misc[:N_HEADS].reshape(1, N_HEADS)
    out["swa_sinks"] = misc[N_HEADS:2 * N_HEADS].reshape(1, N_HEADS)
    out["rel_bias"] = misc[2 * N_HEADS:2 * N_HEADS + N_BUCKETS * N_HEADS].reshape(N_BUCKETS, N_HEADS)
    out["loss"] = blk[len(SMALL_ROWS) + 1, 0]
    return out


def kernel(x, p, w_in, b_forget, w_out, rel_bias, swa_sinks, g_attn_pre, g_attn_post, w_ff1, w_ff2, g_ff_pre, g_ff_post, w_ple, w_ple_gate, g_ple_post, loss_target, m_w_in, m_b_forget, m_w_out, m_rel_bias, m_swa_sinks, m_g_attn_pre, m_g_attn_post, m_w_ff1, m_w_ff2, m_g_ff_pre, m_g_ff_post, m_w_ple, m_w_ple_gate, m_g_ple_post, v_w_in, v_b_forget, v_w_out, v_rel_bias, v_swa_sinks, v_g_attn_pre, v_g_attn_post, v_w_ff1, v_w_ff2, v_g_ff_pre, v_g_ff_post, v_w_ple, v_w_ple_gate, v_g_ple_post):
    w = dict(w_in=w_in, b_forget=b_forget, w_out=w_out, rel_bias=rel_bias, swa_sinks=swa_sinks,
             g_attn_pre=g_attn_pre, g_attn_post=g_attn_post, w_ff1=w_ff1, w_ff2=w_ff2, g_ff_pre=g_ff_pre,
             g_ff_post=g_ff_post, w_ple=w_ple, w_ple_gate=w_ple_gate, g_ple_post=g_ple_post)
    mom = dict(w_in=m_w_in, b_forget=m_b_forget, w_out=m_w_out, rel_bias=m_rel_bias, swa_sinks=m_swa_sinks,
               g_attn_pre=m_g_attn_pre, g_attn_post=m_g_attn_post, w_ff1=m_w_ff1, w_ff2=m_w_ff2,
               g_ff_pre=m_g_ff_pre, g_ff_post=m_g_ff_post, w_ple=m_w_ple, w_ple_gate=m_w_ple_gate,
               g_ple_post=m_g_ple_post)
    var = dict(w_in=v_w_in, b_forget=v_b_forget, w_out=v_w_out, rel_bias=v_rel_bias, swa_sinks=v_swa_sinks,
               g_attn_pre=v_g_attn_pre, g_attn_post=v_g_attn_post, w_ff1=v_w_ff1, w_ff2=v_w_ff2,
               g_ff_pre=v_g_ff_pre, g_ff_post=v_g_ff_post, w_ple=v_w_ple, w_ple_gate=v_w_ple_gate,
               g_ple_post=v_g_ple_post)

    turn = lambda t, k: t.T if k == "w_in" else t
    me = 4 * lax.axis_index("x") + 2 * lax.axis_index("y") + lax.axis_index("c")

    def stack(block):
        return lax.dynamic_update_slice_in_dim(lax.empty((N_DEV,) + block.shape, block.dtype), block[None], me, 0)

    stacks = [stack(turn(w[k][0], k).astype(MM)) for k in BIG]
    (win_g,), later = _all_gather_sequencer(stacks[:1], "all_gather_sequencer", 1), stacks[1:]
    hooks = _Overlap(later)
    loss, grad_x, big, small = _forward_backward(
        x[0], p[0, 0], loss_target[0], win_g, hooks, b_forget, rel_bias, swa_sinks,
        g_attn_pre, g_attn_post, g_ff_pre, g_ff_post, g_ple_post)
    out_g, out_d, out_m, out_v = {}, {}, {}, {}

    def update(k, part, own):
        g, d, m_new, v_new = _adamw_chips(part, own, turn(w[k][0], k), turn(mom[k][0], k), turn(var[k][0], k),
                                          "adamw_" + k)
        out_g[k], out_d[k], out_m[k], out_v[k] = turn(g, k)[None], turn(d, k)[None], turn(m_new, k)[None], turn(v_new, k)[None]
        return d

    view, other = _exchange_wait("late_in_chip_wait", hooks.late_in_chip, hooks.late_in_chip_sems, grad_x,
                                 _plan_in_chip)
    chip_sum = _chip_sum(view, other, "chip_sum_w_in")
    small["loss"] = loss
    between_sems, between, token = _exchange_start(
        "late_between_chips_start", [chip_sum, lax.empty(chip_sum.shape, MM), stack(_pack_small(small))], 3 + 7,
        _plan_late_between)
    early_parts = hooks.finish(token)
    done = [update(k, part, own) for k, part, own in zip(EARLY, early_parts, hooks.sums)]
    chip_sum, part, small_all = _exchange_wait("late_between_chips_wait", between, between_sems, done,
                                               _plan_late_between)
    update("w_in", part, chip_sum)
    rep = {k: w[k] for k in w if k not in BIG}
    rep["loss"] = jnp.zeros((), F32)
    rep_m = {k: mom[k] for k in mom if k not in BIG}
    rep_m["loss"] = jnp.zeros((), F32)
    rep_v = {k: var[k] for k in var if k not in BIG}
    rep_v["loss"] = jnp.ones((), F32)
    g_s, d_s, m_s, v_s = _adamw(small_all, _pack_small(rep), _pack_small(rep_m), _pack_small(rep_v), "adamw_small")
    g_s, d_s, m_s, v_s = _unpack_small(g_s), _unpack_small(d_s), _unpack_small(m_s), _unpack_small(v_s)
    for k in w:
        if k not in BIG:
            out_g[k], out_d[k], out_m[k], out_v[k] = g_s[k], d_s[k], m_s[k], v_s[k]
    return (g_s["loss"], grad_x[None], *[out_g[k] for k in WEIGHTS], *[out_d[k] for k in WEIGHTS],
            *[out_m[k] for k in WEIGHTS], *[out_v[k] for k in WEIGHTS])
```

```python
import functools

import numpy as np
import jax
import jax.numpy as jnp
from jax import lax
from jax.experimental import pallas as pl
from jax.experimental.pallas import tpu as pltpu
from jax.experimental.pallas import tpu_sc as plsc

F32 = jnp.float32
MM = jnp.bfloat16

D_MODEL = 1024
HEAD_DIM = 64
N_HEADS = 8
D_ATT = N_HEADS * HEAD_DIM
D_KV = 128
D_FF = 4096
D_PLE = 256
D_IN = 3 * D_ATT + N_HEADS + D_ATT + 2 * D_KV
N_DEV = 8
FF_CHUNK = D_FF // N_DEV
WINDOW = 128
N_BUCKETS = 32
MAX_DISTANCE = 128
RMS_EPS = 1e-6
Q_SCALE = HEAD_DIM ** -0.5
NEG = -1e30

ADAM_LR = 0.001
ADAM_B1 = 0.9
ADAM_B2 = 0.999
ADAM_EPS = 1e-08
ADAM_WD = 0.01
ADAM_STEP = 10

SLOT_HEAD = (0, 4, 1, 5, 2, 6, 3, 7)
HEAD_SLOT = (0, 2, 4, 6, 1, 3, 5, 7)

VMEM_LIMIT = 60 * 1024 * 1024
MESH = pl.DeviceIdType.MESH

NT = (((1,), (1,)), ((), ()))
TN = (((0,), (0,)), ((), ()))


def _params(*semantics):
    return pltpu.CompilerParams(dimension_semantics=semantics, vmem_limit_bytes=VMEM_LIMIT)


def _resident():
    return pl.BlockSpec(memory_space=pltpu.VMEM)


def _rows(tm, width):
    return pl.BlockSpec((tm, width), lambda i: (i, 0))


def _const(shape):
    return pl.BlockSpec(shape, lambda i: (0,) * len(shape))


def _dot(a, b):
    return jnp.dot(a, b, preferred_element_type=F32)


def _dot_nt(a, b):
    return lax.dot_general(a, b, NT, preferred_element_type=F32)


def _dot_tn(a, b):
    return lax.dot_general(a, b, TN, preferred_element_type=F32)


def _rms(xf):
    r = lax.rsqrt(jnp.mean(xf * xf, axis=-1, keepdims=True) + RMS_EPS)
    return xf * r, r


def _rms_bwd(dout, n, r, g):
    dg = jnp.sum(dout * n, axis=0, keepdims=True)
    dn = dout * g
    dx = r * (dn - n * jnp.mean(dn * n, axis=-1, keepdims=True))
    return dx, dg


def _accumulate(ref, value, step):
    @pl.when(step == 0)
    def _():
        ref[...] = value

    @pl.when(step != 0)
    def _():
        ref[...] += value


def _t5_bucket(n):
    max_exact = N_BUCKETS // 2
    large = max_exact + (np.log(np.maximum(n, 1) / max_exact) / np.log(MAX_DISTANCE / max_exact)
                         * (N_BUCKETS - max_exact)).astype(np.int32)
    large = np.minimum(large, N_BUCKETS - 1)
    return np.where(n < max_exact, n, large).astype(np.int32)


def _swa_bucket_map():
    i = np.arange(WINDOW)[:, None]
    j = np.arange(2 * WINDOW)[None, :]
    dist = i + WINDOW - j
    ok = (dist >= 0) & (dist < WINDOW)
    return np.where(ok, _t5_bucket(np.clip(dist, 0, None)), -1).astype(np.int32)


WT_FOX = 0
WT_FF = 3 * D_ATT
WT_SQ = WT_FF + 16
WT_SKV = WT_SQ + D_ATT
WT_ROWS = WT_SKV + 2 * D_KV


def _pre_attn(x, g1, wt, tm):
    S = x.shape[0]

    def body(x_ref, g_ref, wt_ref, a_ref, fqkv_ref, sqkv_ref, fft_ref):
        n, _ = _rms(x_ref[...])
        a = (n * g_ref[...]).astype(MM)
        a_ref[...] = a
        fqkv_ref[:, :D_ATT] = (_dot_nt(a, wt_ref[WT_FOX:WT_FOX + D_ATT]) * Q_SCALE).astype(MM)
        fqkv_ref[:, D_ATT:] = _dot_nt(a, wt_ref[WT_FOX + D_ATT:WT_FF]).astype(MM)
        sqkv_ref[:, :D_ATT] = (_dot_nt(a, wt_ref[WT_SQ:WT_SKV]) * Q_SCALE).astype(MM)
        sqkv_ref[:, D_ATT:] = _dot_nt(a, wt_ref[WT_SKV:WT_ROWS]).astype(MM)
        fft_ref[...] = _dot_nt(wt_ref[WT_FF:WT_SQ], a)

    return pl.pallas_call(
        body, name="pre_attn", grid=(S // tm,),
        in_specs=[_rows(tm, D_MODEL), _const((1, D_MODEL)), _resident()],
        out_specs=[_rows(tm, D_MODEL), _rows(tm, 3 * D_ATT), _rows(tm, D_ATT + 2 * D_KV),
                   pl.BlockSpec((16, tm), lambda i: (0, i))],
        out_shape=[jax.ShapeDtypeStruct((S, D_MODEL), MM), jax.ShapeDtypeStruct((S, 3 * D_ATT), MM),
                   jax.ShapeDtypeStruct((S, D_ATT + 2 * D_KV), MM), jax.ShapeDtypeStruct((16, S), F32)],
        compiler_params=_params("parallel"),
    )(x, g1, wt)


def _lane_scan(v, reverse):
    S = v.shape[1]
    lane = lax.broadcasted_iota(jnp.int32, v.shape, 1)
    k = 1
    while k < S:
        if reverse:
            v = v + jnp.where(lane < S - k, pltpu.roll(v, S - k, axis=1), 0.0)
        else:
            v = v + jnp.where(lane >= k, pltpu.roll(v, k, axis=1), 0.0)
        k *= 2
    return v


def _forget_cumsum(fft, bcol):
    def body(f_ref, b_ref, c_ref):
        z = f_ref[...] + b_ref[...]
        log_f = jnp.minimum(z, 0.0) - jnp.log1p(jnp.exp(-jnp.abs(z)))
        c_ref[...] = _lane_scan(log_f, reverse=False)

    return pl.pallas_call(
        body, name="forget_cumsum", out_shape=jax.ShapeDtypeStruct(fft.shape, F32),
        in_specs=[_resident(), _resident()], out_specs=_resident(),
    )(fft, bcol)


def _forget_bwd(dc_row, fft, bcol):
    def body(dc_ref, f_ref, b_ref, dff_ref, db_ref):
        z = f_ref[...] + b_ref[...]
        dlog_f = _lane_scan(dc_ref[...], reverse=True)
        dff = dlog_f * (1.0 / (1.0 + jnp.exp(z)))
        dff_ref[...] = dff
        db_ref[...] = jnp.sum(dff, axis=1, keepdims=True)

    return pl.pallas_call(
        body, name="forget_bwd",
        out_shape=[jax.ShapeDtypeStruct(fft.shape, F32), jax.ShapeDtypeStruct((fft.shape[0], 1), F32)],
        in_specs=[_resident()] * 3, out_specs=[_resident()] * 2,
    )(dc_row, fft, bcol)


def _head_select(shape, upper):
    lane = lax.broadcasted_iota(jnp.int32, shape, 1)
    return lane >= HEAD_DIM if upper else lane < HEAD_DIM


def _fox_fwd(fqkv, c_row3, tq, tk, pairs_per_loop=2, row_chunks=1):
    S = fqkv.shape[0]
    rq = tq // row_chunks
    n_band = tq // tk

    def body(q_ref, k_ref, v_ref, ck_ref, o_ref, lse_ref):
        qi = pl.program_id(0)
        row = lax.broadcasted_iota(jnp.int32, (rq, tk), 0)
        col = lax.broadcasted_iota(jnp.int32, (rq, tk), 1)
        low = _head_select((rq, 128), 0)
        for first in range(0, N_HEADS // 2, pairs_per_loop):
            pairs = range(first, first + pairs_per_loop)
            chains = [(pr, hh, rc) for pr in pairs for hh in range(2) for rc in range(row_chunks)]
            qh = {}
            for pr in pairs:
                for rc in range(row_chunks):
                    q2 = q_ref[rc * rq:(rc + 1) * rq, pr * 128:(pr + 1) * 128]
                    qh[pr, 0, rc] = jnp.where(low, q2, jnp.zeros_like(q2))
                    qh[pr, 1, rc] = jnp.where(low, jnp.zeros_like(q2), q2)

            def block(kb, carry, band, chains=chains, qh=qh):
                rows = pl.ds(pl.multiple_of(kb * tk, tk), tk)
                out = []
                for (pr, hh, rc), (m, l, acc) in zip(chains, carry):
                    if band is not None and (rc + 1) * rq <= band * tk:
                        out.append((m, l, acc))
                        continue
                    lanes = slice(pr * 128, (pr + 1) * 128)
                    s = _dot_nt(qh[pr, hh, rc], k_ref[rows, lanes]) - ck_ref[2 * pr + hh, pl.ds(kb, 1), :]
                    if band is not None:
                        s = jnp.where(row + rc * rq >= col + band * tk, s, NEG)
                    m_new = jnp.maximum(m, jnp.max(s, axis=-1, keepdims=True))
                    p = jnp.exp(s - m_new)
                    alpha = jnp.exp(m - m_new)
                    l = alpha * l + jnp.sum(p, axis=-1, keepdims=True)
                    acc = alpha * acc + _dot(p.astype(MM), v_ref[rows, lanes])
                    out.append((m_new, l, acc))
                return tuple(out)

            carry = tuple((jnp.full((rq, 1), NEG, F32), jnp.zeros((rq, 1), F32), jnp.zeros((rq, 128), F32))
                          for _ in chains)
            carry = lax.fori_loop(0, qi * n_band, functools.partial(block, band=None), carry)
            for band in range(n_band):
                carry = block(qi * n_band + band, carry, band=band)
            res = {}
            for (pr, hh, rc), (m, l, acc) in zip(chains, carry):
                res[pr, hh, rc] = acc / l
                lse_ref[rc * rq:(rc + 1) * rq, 2 * pr + hh:2 * pr + hh + 1] = m + jnp.log(l)
            for pr in pairs:
                for rc in range(row_chunks):
                    o_ref[rc * rq:(rc + 1) * rq, pr * 128:(pr + 1) * 128] = jnp.where(
                        low, res[pr, 0, rc], res[pr, 1, rc]).astype(MM)

    return pl.pallas_call(
        body, name="fox_fwd", grid=(S // tq,),
        in_specs=[pl.BlockSpec((tq, D_ATT), lambda i: (i, 0)), pl.BlockSpec((S, D_ATT), lambda i: (0, 1)),
                  pl.BlockSpec((S, D_ATT), lambda i: (0, 2)), _resident()],
        out_specs=[_rows(tq, D_ATT), _rows(tq, N_HEADS)],
        out_shape=[jax.ShapeDtypeStruct((S, D_ATT), MM), jax.ShapeDtypeStruct((S, N_HEADS), F32)],
        compiler_params=_params("parallel"),
    )(fqkv, fqkv, fqkv, c_row3)


def _swa_bias(rel_bias_slot, bucket):
    def body(rb_ref, bk_ref, out_ref):
        bk = bk_ref[...]
        for s in range(N_HEADS):
            acc = jnp.where(bk < 0, NEG, 0.0).astype(F32)
            for b in range(N_BUCKETS):
                acc = jnp.where(bk == b, rb_ref[b, s], acc)
            out_ref[s] = acc

    return pl.pallas_call(
        body, name="swa_bias", out_shape=jax.ShapeDtypeStruct((N_HEADS, WINDOW, 2 * WINDOW), F32),
        in_specs=[pl.BlockSpec(memory_space=pltpu.SMEM), _resident()], out_specs=_resident(),
    )(rel_bias_slot, bucket)


def _stack4(piece):
    return jnp.concatenate([piece(j) for j in range(4)], axis=0)


def _swa_specs(S):
    q = pl.BlockSpec((WINDOW, D_ATT), lambda n: (n, 0))
    kp = pl.BlockSpec((WINDOW, D_KV), lambda n: (jnp.maximum(n - 1, 0), 4))
    kc = pl.BlockSpec((WINDOW, D_KV), lambda n: (n, 4))
    vp = pl.BlockSpec((WINDOW, D_KV), lambda n: (jnp.maximum(n - 1, 0), 5))
    vc = pl.BlockSpec((WINDOW, D_KV), lambda n: (n, 5))
    return [q, kp, kc, vp, vc]


def _swa_fwd(sqkv, biasm, sinks_slot):
    S = sqkv.shape[0]

    def body(q_ref, kp_ref, kc_ref, vp_ref, vc_ref, bias_ref, sink_ref, o_ref, lse_ref):
        n = pl.program_id(0)
        no_prev = jnp.where(n > 0, 0.0, NEG)
        low = _head_select((WINDOW, 128), 0)
        res = []
        for g in range(2):
            sel = low if g == 0 else jnp.logical_not(low)
            qg = _stack4(lambda j: jnp.where(sel, q_ref[:, j * 128:(j + 1) * 128], jnp.zeros((WINDOW, 128), MM)))
            sink = _stack4(lambda j: jnp.full((WINDOW, 1), sink_ref[2 * j + g], F32))
            sp = _dot_nt(qg, kp_ref[...]) + _stack4(lambda j: bias_ref[2 * j + g, :, :WINDOW]) + no_prev
            sc = _dot_nt(qg, kc_ref[...]) + _stack4(lambda j: bias_ref[2 * j + g, :, WINDOW:])
            m = jnp.maximum(jnp.maximum(jnp.max(sp, axis=-1, keepdims=True),
                                        jnp.max(sc, axis=-1, keepdims=True)), sink)
            ep = jnp.exp(sp - m)
            ec = jnp.exp(sc - m)
            den = jnp.sum(ep, axis=-1, keepdims=True) + jnp.sum(ec, axis=-1, keepdims=True) + jnp.exp(sink - m)
            res.append((_dot(ep.astype(MM), vp_ref[...]) + _dot(ec.astype(MM), vc_ref[...])) / den)
            lse = m + jnp.log(den)
            for j in range(4):
                lse_ref[:, 2 * j + g:2 * j + g + 1] = lse[j * WINDOW:(j + 1) * WINDOW]
        for j in range(4):
            rows = slice(j * WINDOW, (j + 1) * WINDOW)
            o_ref[:, j * 128:(j + 1) * 128] = jnp.where(low, res[0][rows], res[1][rows]).astype(MM)

    return pl.pallas_call(
        body, name="swa_fwd", grid=(S // WINDOW,),
        in_specs=_swa_specs(S) + [_resident(), pl.BlockSpec(memory_space=pltpu.SMEM)],
        out_specs=[_rows(WINDOW, D_ATT), _rows(WINDOW, N_HEADS)],
        out_shape=[jax.ShapeDtypeStruct((S, D_ATT), MM), jax.ShapeDtypeStruct((S, N_HEADS), F32)],
        compiler_params=_params("parallel"),
    )(sqkv, sqkv, sqkv, sqkv, sqkv, biasm, sinks_slot)


def _post_attn(x, fox_o, swa_o, wout_fox, wout_swa, g2, g3, tm):
    S = x.shape[0]

    def body(x_ref, fo_ref, so_ref, wf_ref, ws_ref, g2_ref, g3_ref, mix_ref, h1_ref, m_ref):
        mix = _dot(fo_ref[...], wf_ref[...]) + _dot(so_ref[...], ws_ref[...])
        mix_ref[...] = mix
        n2, _ = _rms(mix)
        h1 = x_ref[...] + n2 * g2_ref[...]
        h1_ref[...] = h1
        n3, _ = _rms(h1)
        m_ref[...] = (n3 * g3_ref[...]).astype(MM)

    return pl.pallas_call(
        body, name="post_attn", grid=(S // tm,),
        in_specs=[_rows(tm, D_MODEL), _rows(tm, D_ATT), _rows(tm, D_ATT), _resident(), _resident(),
                  _const((1, D_MODEL)), _const((1, D_MODEL))],
        out_specs=[_rows(tm, D_MODEL)] * 3,
        out_shape=[jax.ShapeDtypeStruct((S, D_MODEL), F32), jax.ShapeDtypeStruct((S, D_MODEL), F32),
                   jax.ShapeDtypeStruct((S, D_MODEL), MM)],
        compiler_params=_params("parallel"),
    )(x, fox_o, swa_o, wout_fox, wout_swa, g2, g3)


def _mlp_fwd(m, h1, w1, w2, g4, tm):
    S = m.shape[0]

    def body(m_ref, h1_ref, w1_ref, w2_ref, g4_ref, u_ref, y_ref, h2_ref):
        mb = m_ref[...]
        y = jnp.zeros((tm, D_MODEL), F32)
        for j in range(N_DEV):
            cols = slice(j * FF_CHUNK, (j + 1) * FF_CHUNK)
            u = _dot(mb, w1_ref[j])
            u_ref[:, cols] = u.astype(MM)
            y = y + _dot(jnp.square(jnp.maximum(u, 0.0)).astype(MM), w2_ref[cols, :])
        y_ref[...] = y
        n4, _ = _rms(y)
        h2_ref[...] = h1_ref[...] + n4 * g4_ref[...]

    return pl.pallas_call(
        body, name="mlp_fwd", grid=(S // tm,),
        in_specs=[_rows(tm, D_MODEL), _rows(tm, D_MODEL), _resident(), _resident(), _const((1, D_MODEL))],
        out_specs=[_rows(tm, D_FF), _rows(tm, D_MODEL), _rows(tm, D_MODEL)],
        out_shape=[jax.ShapeDtypeStruct((S, D_FF), MM), jax.ShapeDtypeStruct((S, D_MODEL), F32),
                   jax.ShapeDtypeStruct((S, D_MODEL), F32)],
        compiler_params=_params("parallel"),
    )(m, h1, w1, w2, g4)


def _ple_loss(h2, p, target, wg, wple, g5, tm):
    S = h2.shape[0]

    def body(h2_ref, p_ref, t_ref, wg_ref, wp_ref, g5_ref, dh2_ref, dpe_ref, dgl_ref, dg5_ref, loss_ref):
        i = pl.program_id(0)
        h2 = h2_ref[...]
        gate = jax.nn.sigmoid(_dot(h2.astype(MM), wg_ref[...]))
        pe = _dot(p_ref[...].astype(MM), wp_ref[...])
        n5, r5 = _rms(pe * gate)
        g5 = g5_ref[...]
        diff = h2 + n5 * g5 - t_ref[...]
        per_token = jnp.mean(jnp.square(diff), axis=-1, keepdims=True)
        _accumulate(loss_ref, 0.5 * jnp.sum(per_token, axis=0, keepdims=True), i)
        dh3 = diff * (1.0 / D_MODEL)
        de, dg5 = _rms_bwd(dh3, n5, r5, g5)
        _accumulate(dg5_ref, dg5, i)
        dpe_ref[...] = (de * gate).astype(MM)
        dgl = (de * pe * gate * (1.0 - gate)).astype(MM)
        dgl_ref[...] = dgl
        dh2_ref[...] = dh3 + _dot_nt(dgl, wg_ref[...])

    return pl.pallas_call(
        body, name="ple_loss", grid=(S // tm,),
        in_specs=[_rows(tm, D_MODEL), _rows(tm, D_PLE), _rows(tm, D_MODEL), _resident(), _resident(),
                  _const((1, D_MODEL))],
        out_specs=[_rows(tm, D_MODEL), _rows(tm, D_MODEL), _rows(tm, D_MODEL), _const((1, D_MODEL)), _const((1, 1))],
        out_shape=[jax.ShapeDtypeStruct((S, D_MODEL), F32), jax.ShapeDtypeStruct((S, D_MODEL), MM),
                   jax.ShapeDtypeStruct((S, D_MODEL), MM), jax.ShapeDtypeStruct((1, D_MODEL), F32),
                   jax.ShapeDtypeStruct((1, 1), F32)],
        compiler_params=_params("arbitrary"),
    )(h2, p, target, wg, wple, g5)


def _mlp_bwd(dh2, y, h1, u, w1, w2, g4, g3, tm):
    S = dh2.shape[0]

    def body(dh2_ref, y_ref, h1_ref, u_ref, w1_ref, w2_ref, g4_ref, g3_ref,
             dh1_ref, dy_ref, du_ref, dg4_ref, dg3_ref):
        i = pl.program_id(0)
        dh2 = dh2_ref[...]
        n4, r4 = _rms(y_ref[...])
        dy, dg4 = _rms_bwd(dh2, n4, r4, g4_ref[...])
        _accumulate(dg4_ref, dg4, i)
        dyb = dy.astype(MM)
        dy_ref[...] = dyb
        dm = jnp.zeros((tm, D_MODEL), F32)
        for j in range(N_DEV):
            cols = slice(j * FF_CHUNK, (j + 1) * FF_CHUNK)
            dact = _dot_nt(dyb, w2_ref[cols, :])
            du = (dact * (2.0 * jnp.maximum(u_ref[:, cols].astype(F32), 0.0))).astype(MM)
            du_ref[:, cols] = du
            dm = dm + _dot_nt(du, w1_ref[j])
        n3, r3 = _rms(h1_ref[...])
        dx, dg3 = _rms_bwd(dm, n3, r3, g3_ref[...])
        _accumulate(dg3_ref, dg3, i)
        dh1_ref[...] = dh2 + dx

    return pl.pallas_call(
        body, name="mlp_bwd", grid=(S // tm,),
        in_specs=[_rows(tm, D_MODEL), _rows(tm, D_MODEL), _rows(tm, D_MODEL), _rows(tm, D_FF),
                  _resident(), _resident(), _const((1, D_MODEL)), _const((1, D_MODEL))],
        out_specs=[_rows(tm, D_MODEL), _rows(tm, D_MODEL), _rows(tm, D_FF), _const((1, D_MODEL)),
                   _const((1, D_MODEL))],
        out_shape=[jax.ShapeDtypeStruct((S, D_MODEL), F32), jax.ShapeDtypeStruct((S, D_MODEL), MM),
                   jax.ShapeDtypeStruct((S, D_FF), MM), jax.ShapeDtypeStruct((1, D_MODEL), F32),
                   jax.ShapeDtypeStruct((1, D_MODEL), F32)],
        compiler_params=_params("arbitrary"),
    )(dh2, y, h1, u, w1, w2, g4, g3)


def _attn_out_bwd(dh1, mix, fox_o, swa_o, wout_fox, wout_swa, g2, head_rows, tm):
    S = dh1.shape[0]

    def body(dh1_ref, mix_ref, fo_ref, so_ref, wf_ref, ws_ref, g2_ref, er_ref,
             dmix_ref, dcat_ref, drow_ref, dswa_ref, dg2_ref):
        i = pl.program_id(0)
        n2, r2 = _rms(mix_ref[...])
        dmix, dg2 = _rms_bwd(dh1_ref[...], n2, r2, g2_ref[...])
        _accumulate(dg2_ref, dg2, i)
        dmb = dmix.astype(MM)
        dmix_ref[...] = dmb
        dfo = _dot_nt(dmb, wf_ref[...]).astype(MM)
        dso = _dot_nt(dmb, ws_ref[...]).astype(MM)
        dcat_ref[:, :D_ATT] = dfo
        dcat_ref[:, D_ATT:] = dso
        hi = lax.Precision.HIGHEST
        prod_f = dfo.astype(F32) * fo_ref[...].astype(F32)
        prod_s = dso.astype(F32) * so_ref[...].astype(F32)
        drow_ref[...] = lax.dot_general(er_ref[...], prod_f, NT, precision=hi, preferred_element_type=F32)
        dswa_ref[...] = lax.dot_general(er_ref[...], prod_s, NT, precision=hi, preferred_element_type=F32)

    return pl.pallas_call(
        body, name="attn_out_bwd", grid=(S // tm,),
        in_specs=[_rows(tm, D_MODEL), _rows(tm, D_MODEL), _rows(tm, D_ATT), _rows(tm, D_ATT), _resident(),
                  _resident(), _const((1, D_MODEL)), _resident()],
        out_specs=[_rows(tm, D_MODEL), _rows(tm, D_MODEL), pl.BlockSpec((N_HEADS, tm), lambda i: (0, i)),
                   pl.BlockSpec((N_HEADS, tm), lambda i: (0, i)), _const((1, D_MODEL))],
        out_shape=[jax.ShapeDtypeStruct((S, D_MODEL), MM), jax.ShapeDtypeStruct((S, D_MODEL), MM),
                   jax.ShapeDtypeStruct((N_HEADS, S), F32), jax.ShapeDtypeStruct((N_HEADS, S), F32),
                   jax.ShapeDtypeStruct((1, D_MODEL), F32)],
        compiler_params=_params("arbitrary"),
    )(dh1, mix, fox_o, swa_o, wout_fox, wout_swa, g2, head_rows)


def _fox_bwd(fqkv, dcat, lse_row3, d_row3, c_col, tq, tk, pairs_per_loop=2):
    S = fqkv.shape[0]
    n_blk = S // tk
    n_qblk = S // tq
    n_band = tk // tq

    def body(q_ref, k_ref, v_ref, do_ref, lse_ref, dd_ref, ck_ref, dq_ref, dk_ref, dv_ref, dc_ref, dcq_ref):
        kb = pl.program_id(0)

        @pl.when(kb == 0)
        def _():
            dq_ref[...] = jnp.zeros_like(dq_ref)
            dcq_ref[...] = jnp.zeros_like(dcq_ref)

        key = lax.broadcasted_iota(jnp.int32, (tk, tq), 0)
        qry = lax.broadcasted_iota(jnp.int32, (tk, tq), 1)
        low = _head_select((tk, 128), 0)
        for first in range(0, N_HEADS // 2, pairs_per_loop):
            pairs = range(first, first + pairs_per_loop)
            heads = [(pr, hh) for pr in pairs for hh in range(2)]
            kh, vh, ck = {}, {}, {}
            for pr in pairs:
                k2 = k_ref[:, pr * 128:(pr + 1) * 128]
                v2 = v_ref[:, pr * 128:(pr + 1) * 128]
                zero = jnp.zeros_like(k2)
                kh[pr, 0], kh[pr, 1] = jnp.where(low, k2, zero), jnp.where(low, zero, k2)
                vh[pr, 0], vh[pr, 1] = jnp.where(low, v2, zero), jnp.where(low, zero, v2)
                for hh in range(2):
                    ck[pr, hh] = ck_ref[:, 2 * pr + hh:2 * pr + hh + 1]

            def block(qb, carry, band, pairs=pairs, kh=kh, vh=vh, ck=ck):
                rows = pl.ds(pl.multiple_of(qb * tq, tq), tq)
                k1 = tk if band is None else (band + 1) * tq
                out = []
                it = iter(carry)
                for pr in pairs:
                    lanes = slice(pr * 128, (pr + 1) * 128)
                    q2 = q_ref[rows, lanes]
                    do2 = do_ref[rows, lanes]
                    dq = None
                    for hh in range(2):
                        h = 2 * pr + hh
                        dk, dv, dc = next(it)
                        s_t = _dot_nt(kh[pr, hh][:k1], q2) - ck[pr, hh][:k1]
                        p_t = jnp.exp(s_t - lse_ref[h, pl.ds(qb, 1), :])
                        if band is not None:
                            p_t = jnp.where(qry[:k1] + band * tq >= key[:k1], p_t, 0.0)
                        ds_t = p_t * (_dot_nt(vh[pr, hh][:k1], do2) - dd_ref[h, pl.ds(qb, 1), :])
                        dsb = ds_t.astype(MM)
                        dv_new = dv[:k1] + _dot(p_t.astype(MM), do2)
                        dk_new = dk[:k1] + _dot(dsb, q2)
                        dc_new = dc[:k1] - jnp.sum(ds_t, axis=1, keepdims=True)
                        if k1 < tk:
                            dv_new = jnp.concatenate([dv_new, dv[k1:]], axis=0)
                            dk_new = jnp.concatenate([dk_new, dk[k1:]], axis=0)
                            dc_new = jnp.concatenate([dc_new, dc[k1:]], axis=0)
                        part = _dot_tn(dsb, kh[pr, hh][:k1])
                        dq = part if dq is None else dq + part
                        dcq_ref[h, pl.ds(qb, 1), :] += jnp.sum(ds_t, axis=0, keepdims=True)
                        out.append((dk_new, dv_new, dc_new))
                    dq_ref[rows, lanes] += dq
                return tuple(out)

            carry = tuple((jnp.zeros((tk, 128), F32), jnp.zeros((tk, 128), F32), jnp.zeros((tk, 1), F32))
                          for _ in heads)
            for band in range(n_band):
                carry = block(kb * n_band + band, carry, band=band)
            carry = lax.fori_loop((kb + 1) * n_band, n_qblk, functools.partial(block, band=None), carry)
            grads = dict(zip(heads, carry))
            for pr in pairs:
                lanes = slice(pr * 128, (pr + 1) * 128)
                dk_ref[:, lanes] = jnp.where(low, grads[pr, 0][0], grads[pr, 1][0]).astype(MM)
                dv_ref[:, lanes] = jnp.where(low, grads[pr, 0][1], grads[pr, 1][1]).astype(MM)
                for hh in range(2):
                    dc_ref[:, 2 * pr + hh:2 * pr + hh + 1] = grads[pr, hh][2]

        @pl.when(kb == n_blk - 1)
        def _():
            dq_ref[...] = dq_ref[...] * Q_SCALE

    return pl.pallas_call(
        body, name="fox_bwd", grid=(n_blk,),
        in_specs=[pl.BlockSpec((S, D_ATT), lambda i: (0, 0)), pl.BlockSpec((tk, D_ATT), lambda i: (i, 1)),
                  pl.BlockSpec((tk, D_ATT), lambda i: (i, 2)), pl.BlockSpec((S, D_ATT), lambda i: (0, 0)),
                  _resident(), _resident(), _rows(tk, N_HEADS)],
        out_specs=[_const((S, D_ATT)), _rows(tk, D_ATT), _rows(tk, D_ATT), _rows(tk, N_HEADS),
                   _const((N_HEADS, n_qblk, tq))],
        out_shape=[jax.ShapeDtypeStruct((S, D_ATT), F32), jax.ShapeDtypeStruct((S, D_ATT), MM),
                   jax.ShapeDtypeStruct((S, D_ATT), MM), jax.ShapeDtypeStruct((S, N_HEADS), F32),
                   jax.ShapeDtypeStruct((N_HEADS, n_qblk, tq), F32)],
        compiler_params=_params("arbitrary"),
    )(fqkv, fqkv, fqkv, dcat, lse_row3, d_row3, c_col)


def _swa_bwd(sqkv, dcat, biasm, sinks_slot, bucket, lse, d_col):
    S = sqkv.shape[0]
    n_blk = S // WINDOW

    def body(q_ref, kp_ref, kc_ref, vp_ref, vc_ref, do_ref, bias_ref, sink_ref, bk_ref, lse_ref, dd_ref,
             dq_ref, dk_ref, dv_ref, drb_ref, dsink_ref, ds_acc):
        n = pl.program_id(0)

        @pl.when(n == 0)
        def _():
            dk_ref[...] = jnp.zeros_like(dk_ref)
            dv_ref[...] = jnp.zeros_like(dv_ref)
            ds_acc[...] = jnp.zeros_like(ds_acc)
            dsink_ref[...] = jnp.zeros_like(dsink_ref)

        no_prev = jnp.where(n > 0, 0.0, NEG)
        prev = pl.ds(pl.multiple_of(jnp.maximum(n - 1, 0) * WINDOW, WINDOW), WINDOW)
        cur = pl.ds(pl.multiple_of(n * WINDOW, WINDOW), WINDOW)
        lane8 = lax.broadcasted_iota(jnp.int32, (1, N_HEADS), 1)
        dkp = jnp.zeros((WINDOW, D_KV), F32)
        dkc = jnp.zeros((WINDOW, D_KV), F32)
        dvp = jnp.zeros((WINDOW, D_KV), F32)
        dvc = jnp.zeros((WINDOW, D_KV), F32)
        dsink = jnp.zeros((1, N_HEADS), F32)
        low = _head_select((WINDOW, 128), 0)
        zero = jnp.zeros((WINDOW, 128), MM)
        dqs = []
        for g in range(2):
            sel = low if g == 0 else jnp.logical_not(low)
            qg = _stack4(lambda j: jnp.where(sel, q_ref[:, j * 128:(j + 1) * 128], zero))
            dog = _stack4(lambda j: jnp.where(sel, do_ref[:, j * 128:(j + 1) * 128], zero))
            lse_g = _stack4(lambda j: lse_ref[:, 2 * j + g:2 * j + g + 1])
            dd = _stack4(lambda j: dd_ref[:, 2 * j + g:2 * j + g + 1])
            sink = _stack4(lambda j: jnp.full((WINDOW, 1), sink_ref[2 * j + g], F32))
            pp = jnp.exp(_dot_nt(qg, kp_ref[...]) + _stack4(lambda j: bias_ref[2 * j + g, :, :WINDOW]) + no_prev - lse_g)
            pc = jnp.exp(_dot_nt(qg, kc_ref[...]) + _stack4(lambda j: bias_ref[2 * j + g, :, WINDOW:]) - lse_g)
            sink_term = jnp.exp(sink - lse_g) * dd
            dsp = pp * (_dot_nt(dog, vp_ref[...]) - dd)
            dsc = pc * (_dot_nt(dog, vc_ref[...]) - dd)
            for j in range(4):
                rows = slice(j * WINDOW, (j + 1) * WINDOW)
                dsink = dsink + jnp.where(lane8 == 2 * j + g, -jnp.sum(sink_term[rows]), 0.0)
                ds_acc[2 * j + g, :, :WINDOW] += dsp[rows]
                ds_acc[2 * j + g, :, WINDOW:] += dsc[rows]
            dspb, dscb = dsp.astype(MM), dsc.astype(MM)
            dqs.append(_dot(dspb, kp_ref[...]) + _dot(dscb, kc_ref[...]))
            dkp = dkp + _dot_tn(dspb, qg)
            dkc = dkc + _dot_tn(dscb, qg)
            dvp = dvp + _dot_tn(pp.astype(MM), dog)
            dvc = dvc + _dot_tn(pc.astype(MM), dog)
        for j in range(4):
            rows = slice(j * WINDOW, (j + 1) * WINDOW)
            dq_ref[:, j * 128:(j + 1) * 128] = (jnp.where(low, dqs[0][rows], dqs[1][rows]) * Q_SCALE).astype(MM)
        dk_ref[prev, :] += dkp
        dk_ref[cur, :] += dkc
        dv_ref[prev, :] += dvp
        dv_ref[cur, :] += dvc
        dsink_ref[...] += dsink

        @pl.when(n == n_blk - 1)
        def _():
            bk = bk_ref[...]
            rb = lax.broadcasted_iota(jnp.int32, (N_BUCKETS, N_HEADS), 0)
            cb = lax.broadcasted_iota(jnp.int32, (N_BUCKETS, N_HEADS), 1)
            out = jnp.zeros((N_BUCKETS, N_HEADS), F32)
            for s in range(N_HEADS):
                acc = ds_acc[s]
                for b in range(N_BUCKETS):
                    out = out + jnp.where((rb == b) & (cb == s), jnp.sum(jnp.where(bk == b, acc, 0.0)), 0.0)
            drb_ref[...] = out

    do_spec = pl.BlockSpec((WINDOW, D_ATT), lambda n: (n, 1))
    return pl.pallas_call(
        body, name="swa_bwd", grid=(n_blk,),
        in_specs=_swa_specs(S) + [do_spec, _resident(), pl.BlockSpec(memory_space=pltpu.SMEM), _resident(),
                                  _rows(WINDOW, N_HEADS), _rows(WINDOW, N_HEADS)],
        out_specs=[_rows(WINDOW, D_ATT), _const((S, D_KV)), _const((S, D_KV)), _const((N_BUCKETS, N_HEADS)),
                   _const((1, N_HEADS))],
        out_shape=[jax.ShapeDtypeStruct((S, D_ATT), MM), jax.ShapeDtypeStruct((S, D_KV), F32),
                   jax.ShapeDtypeStruct((S, D_KV), F32), jax.ShapeDtypeStruct((N_BUCKETS, N_HEADS), F32),
                   jax.ShapeDtypeStruct((1, N_HEADS), F32)],
        scratch_shapes=[pltpu.VMEM((N_HEADS, WINDOW, 2 * WINDOW), F32)],
        compiler_params=_params("arbitrary"),
    )(sqkv, sqkv, sqkv, sqkv, sqkv, dcat, biasm, sinks_slot, bucket, lse, d_col)


D_Z = 3 * D_ATT + D_ATT + 2 * D_KV


def _pack_dz(dq_fox, dk_fox, dv_fox, dsq, dsk, dsv, tm):
    S = dq_fox.shape[0]

    def body(dq_ref, dk_ref, dv_ref, dsq_ref, dsk_ref, dsv_ref, dz_ref):
        dz_ref[:, 0:512] = dq_ref[...].astype(MM)
        dz_ref[:, 512:1024] = dk_ref[...]
        dz_ref[:, 1024:1536] = dv_ref[...]
        dz_ref[:, 1536:2048] = dsq_ref[...]
        dz_ref[:, 2048:2176] = dsk_ref[...].astype(MM)
        dz_ref[:, 2176:2304] = dsv_ref[...].astype(MM)

    return pl.pallas_call(
        body, name="pack_dz", grid=(S // tm,),
        in_specs=[_rows(tm, D_ATT), _rows(tm, D_ATT), _rows(tm, D_ATT), _rows(tm, D_ATT), _rows(tm, D_KV),
                  _rows(tm, D_KV)],
        out_specs=_rows(tm, D_Z), out_shape=jax.ShapeDtypeStruct((S, D_Z), MM),
        compiler_params=_params("parallel"),
    )(dq_fox, dk_fox, dv_fox, dsq, dsk, dsv)


def _pre_attn_bwd(x, dh1, dz, dff_t, wt, g1, tm):
    S = x.shape[0]

    def body(x_ref, dh1_ref, dz_ref, dff_ref, wt_ref, g1_ref, dx_ref, dg1_ref):
        i = pl.program_id(0)
        da = (_dot(dz_ref[:, 0:WT_FF], wt_ref[0:WT_FF]) + _dot(dz_ref[:, WT_FF:D_Z], wt_ref[WT_SQ:WT_ROWS])
              + _dot_tn(dff_ref[...].astype(MM), wt_ref[WT_FF:WT_SQ]))
        n1, r1 = _rms(x_ref[...])
        dx, dg1 = _rms_bwd(da, n1, r1, g1_ref[...])
        _accumulate(dg1_ref, dg1, i)
        dx_ref[...] = dh1_ref[...] + dx

    return pl.pallas_call(
        body, name="pre_attn_bwd", grid=(S // tm,),
        in_specs=[_rows(tm, D_MODEL), _rows(tm, D_MODEL), _rows(tm, D_Z), pl.BlockSpec((16, tm), lambda i: (0, i)),
                  _resident(), _const((1, D_MODEL))],
        out_specs=[_rows(tm, D_MODEL), _const((1, D_MODEL))],
        out_shape=[jax.ShapeDtypeStruct((S, D_MODEL), F32), jax.ShapeDtypeStruct((1, D_MODEL), F32)],
        compiler_params=_params("arbitrary"),
    )(x, dh1, dz, dff_t, wt, g1)


def _weight_grad(a, b, name, tk, n_chunks=1, relu2=False):
    S, K = a.shape
    N = b.shape[1]
    cn = N // n_chunks

    def body(a_ref, b_ref, out_ref):
        av = a_ref[...]
        if relu2:
            av = jnp.square(jnp.maximum(av.astype(F32), 0.0))
        av = av.astype(MM)
        for j in range(n_chunks):
            val = _dot_tn(av, b_ref[:, j * cn:(j + 1) * cn].astype(MM)).astype(MM)
            if n_chunks > 1:
                out_ref[j] = val
            else:
                out_ref[...] = val

    if n_chunks > 1:
        out_spec = pl.BlockSpec((n_chunks, tk, cn), lambda i: (0, i, 0))
        out_shape = jax.ShapeDtypeStruct((n_chunks, K, cn), MM)
    else:
        out_spec = pl.BlockSpec((tk, N), lambda i: (i, 0))
        out_shape = jax.ShapeDtypeStruct((K, N), MM)
    return pl.pallas_call(
        body, name=name, grid=(K // tk,),
        in_specs=[pl.BlockSpec((S, tk), lambda i: (0, i)), _resident()],
        out_specs=out_spec, out_shape=out_shape, compiler_params=_params("parallel"),
    )(a, b)


def _forget_weight_grad(dff_t, a):
    def body(d_ref, a_ref, out_ref):
        out_ref[...] = _dot(d_ref[...].astype(MM), a_ref[...])

    return pl.pallas_call(
        body, name="forget_weight_grad", out_shape=jax.ShapeDtypeStruct((16, D_MODEL), F32),
        in_specs=[_resident(), _resident()], out_specs=_resident(),
    )(dff_t, a)


def _place():
    return lax.axis_index("x"), lax.axis_index("y"), lax.axis_index("c")


def _all_gather_sequencer(stacks, name, collective_id):
    refs = [jax.new_ref(s, memory_space=pltpu.MemorySpace.HBM) for s in stacks]
    n = len(refs)

    @pl.kernel(mesh=plsc.ScalarSubcoreMesh(axis_name="sequencer", num_cores=1), name=name,
               scratch_types=(pltpu.SemaphoreType.DMA((7 * n,)), pltpu.SemaphoreType.DMA((7 * n,))),
               compiler_params=pltpu.CompilerParams(collective_id=collective_id))
    def launch(send_sems, recv_sems):
        x, y, c = _place()
        sibling = (x, y, 1 - c)
        chips = [(1 - x, y), (x, 1 - y), (1 - x, 1 - y)]
        peers = [sibling] + [(px, py, c) for px, py in chips]
        barrier = pltpu.get_barrier_semaphore()
        for peer in peers:
            pl.semaphore_signal(barrier, inc=1, device_id=peer, device_id_type=MESH)
        pl.semaphore_wait(barrier, len(peers))

        def copy(a, k, block, to):
            px, py, pc = block
            slot = refs[a].at[4 * px + 2 * py + pc]
            return _remote(slot, slot, send_sems, recv_sems, 7 * a + k, to)

        first = [copy(a, k, (x, y, c), peer) for a in range(n) for k, peer in enumerate(peers)]
        for cp in first:
            cp.start()
        passed = []
        for j, (px, py) in enumerate(chips):
            for a in range(n):
                copy(a, 1 + j, (px, py, c), sibling).wait_recv()
                passed.append(copy(a, 4 + j, (px, py, c), sibling))
                passed[-1].start()
        for a in range(n):
            copy(a, 0, (x, y, 1 - c), sibling).wait_recv()
            for j, (px, py) in enumerate(chips):
                copy(a, 4 + j, (px, py, 1 - c), sibling).wait_recv()
        for cp in first + passed:
            cp.wait_send()

    launch()
    return [ref[...] for ref in refs]


def _chip_sums(grads, others, name):
    n = len(grads)

    def body(c_ref, *refs):
        for g_ref, o_ref, out_ref in zip(refs[:n], refs[n:2 * n], refs[2 * n:]):
            out_ref[...] = (g_ref[...].astype(F32) + o_ref[...].astype(F32)).astype(out_ref.dtype)

    own = [pl.BlockSpec((None, None) + g.shape[2:], lambda k, c_ref: (k, c_ref[0], 0, 0)) for g in grads]
    chip = [pl.BlockSpec((None,) + g.shape[2:], lambda k, c_ref: (k, 0, 0)) for g in grads]
    return pl.pallas_call(
        body, name=name,
        grid_spec=pltpu.PrefetchScalarGridSpec(num_scalar_prefetch=1, grid=(4,), in_specs=own + chip, out_specs=chip),
        out_shape=[jax.ShapeDtypeStruct((4,) + g.shape[2:], MM) for g in grads],
        compiler_params=_params("parallel"),
    )(lax.axis_index("c").astype(jnp.int32).reshape(1), *grads, *others)


HBM_SPEC = pl.BlockSpec(memory_space=pltpu.HBM)
SEM_SPEC = pl.BlockSpec(memory_space=pltpu.SEMAPHORE)
DATAFLOW = pltpu.SideEffectType.DATAFLOW_SIDE_EFFECTING


def _exchange_start(name, arrays, n_copies, plan):
    n = len(arrays)

    def body(*refs):
        send_sems, recv_sems, token = refs[n], refs[n + 1], refs[2 * n + 2]
        for cp in plan(refs[:n], send_sems, recv_sems):
            cp.start()
        token[...] = jnp.zeros_like(token)

    out = pl.pallas_call(
        body, name=name,
        out_shape=(pltpu.SemaphoreType.DMA((n_copies,)), pltpu.SemaphoreType.DMA((n_copies,)),
                   *[pltpu.HBM(a.shape, a.dtype) for a in arrays], jax.ShapeDtypeStruct((1, D_MODEL), F32)),
        in_specs=[HBM_SPEC] * n,
        out_specs=(SEM_SPEC, SEM_SPEC, *[HBM_SPEC] * n, pl.BlockSpec(memory_space=pltpu.VMEM)),
        input_output_aliases={i: 2 + i for i in range(n)},
        compiler_params=pltpu.CompilerParams(has_side_effects=DATAFLOW),
    )(*[pltpu.with_memory_space_constraint(a, pltpu.HBM) for a in arrays])
    return (out[0], out[1]), list(out[2:2 + n]), out[2 + n]


def _exchange_wait(name, arrays, sems, after, plan):
    n = len(arrays)
    after = list(after) if isinstance(after, (list, tuple)) else [after]

    def body(*refs):
        send_sems, recv_sems = refs[n], refs[n + 1]
        for cp in plan(refs[:n], send_sems, recv_sems):
            cp.wait_send()
            cp.wait_recv()

    out = pl.pallas_call(
        body, name=name, out_shape=[pltpu.HBM(a.shape, a.dtype) for a in arrays],
        in_specs=[HBM_SPEC] * n + [SEM_SPEC, SEM_SPEC] + [pl.BlockSpec(memory_space=pl.ANY)] * len(after),
        out_specs=[HBM_SPEC] * n, input_output_aliases={i: i for i in range(n)},
        compiler_params=pltpu.CompilerParams(has_side_effects=DATAFLOW),
    )(*arrays, sems[0], sems[1], *after)
    return list(out)


def _remote(src, dst, send_sems, recv_sems, k, to):
    return pltpu.make_async_remote_copy(src_ref=src, dst_ref=dst, send_sem=send_sems.at[k], recv_sem=recv_sems.at[k],
                                        device_id=to, device_id_type=MESH)


def _plan_gather_direct(refs, send_sems, recv_sems):
    x, y, c = _place()
    me = 4 * x + 2 * y + c
    peers = [(x, y, 1 - c), (1 - x, y, c), (x, 1 - y, c), (1 - x, 1 - y, c)]
    return [_remote(ref.at[me], ref.at[me], send_sems, recv_sems, 4 * a + k, peer)
            for a, ref in enumerate(refs) for k, peer in enumerate(peers)]


def _plan_gather_pass_on(refs, send_sems, recv_sems):
    x, y, c = _place()
    chips = [(1 - x, y), (x, 1 - y), (1 - x, 1 - y)]
    return [_remote(ref.at[4 * px + 2 * py + c], ref.at[4 * px + 2 * py + c], send_sems, recv_sems, 3 * a + k,
                    (x, y, 1 - c))
            for a, ref in enumerate(refs) for k, (px, py) in enumerate(chips)]


def _plan_in_chip(refs, send_sems, recv_sems):
    n = len(refs) // 2
    x, y, c = _place()
    return [_remote(refs[a].at[:, 1 - c], refs[n + a], send_sems, recv_sems, a, (x, y, 1 - c)) for a in range(n)]


def _plan_between_chips(refs, send_sems, recv_sems):
    n = len(refs) // 2
    x, y, c = _place()
    chips = [(1 - x, y), (x, 1 - y), (1 - x, 1 - y)]
    return [_remote(refs[a].at[2 * px + py], refs[n + a].at[2 * x + y], send_sems, recv_sems, 3 * a + k, (px, py, c))
            for a in range(n) for k, (px, py) in enumerate(chips)]


def _plan_late_between(refs, send_sems, recv_sems):
    sums, land, small = refs
    x, y, c = _place()
    me = 4 * x + 2 * y + c
    copies = _plan_between_chips([sums, land], send_sems, recv_sems)
    peers = [(x ^ dx, y ^ dy, c ^ dc) for dx in range(2) for dy in range(2) for dc in range(2) if dx + dy + dc]
    return copies + [_remote(small.at[me], small.at[me], send_sems, recv_sems, 3 + k, peer)
                     for k, peer in enumerate(peers)]


def _adamw_math(w, g, m, v):
    m = ADAM_B1 * m + (1.0 - ADAM_B1) * g
    v = ADAM_B2 * v + (1.0 - ADAM_B2) * jnp.square(g)
    m_hat = m / (1.0 - ADAM_B1 ** ADAM_STEP)
    v_hat = v / (1.0 - ADAM_B2 ** ADAM_STEP)
    delta = -ADAM_LR * (m_hat / (jnp.sqrt(v_hat) + ADAM_EPS) + ADAM_WD * w)
    return delta, m, v


def _adamw(parts, w, m, v, name):
    n_parts, r, cdim = parts.shape
    tr = 256 if r % 256 == 0 else r

    def body(p_ref, w_ref, m_ref, v_ref, g_out, d_out, m_out, v_out):
        g = p_ref[0].astype(F32)
        for k in range(1, n_parts):
            g = g + p_ref[k].astype(F32)
        delta, m_new, v_new = _adamw_math(w_ref[...], g, m_ref[...], v_ref[...])
        g_out[...] = g
        d_out[...] = delta
        m_out[...] = m_new
        v_out[...] = v_new

    blk = pl.BlockSpec((tr, cdim), lambda i: (i, 0))
    return pl.pallas_call(
        body, name=name, grid=(r // tr,),
        in_specs=[pl.BlockSpec((n_parts, tr, cdim), lambda i: (0, i, 0)), blk, blk, blk],
        out_specs=[blk] * 4, out_shape=[jax.ShapeDtypeStruct((r, cdim), F32)] * 4,
        compiler_params=_params("parallel"),
    )(parts, w, m, v)


def _adamw_chips(parts, sums, w, m, v, name):
    _, r, cdim = parts.shape
    tr = 256 if r % 256 == 0 else r

    def body(chip_ref, p_ref, own_ref, w_ref, m_ref, v_ref, g_out, d_out, m_out, v_out):
        g = None
        for k in range(4):
            term = jnp.where(chip_ref[0] == k, own_ref[...], p_ref[k]).astype(F32)
            g = term if g is None else g + term
        delta, m_new, v_new = _adamw_math(w_ref[...], g, m_ref[...], v_ref[...])
        g_out[...] = g
        d_out[...] = delta
        m_out[...] = m_new
        v_out[...] = v_new

    blk = pl.BlockSpec((tr, cdim), lambda i, chip: (i, 0))
    my_chip = (2 * lax.axis_index("x") + lax.axis_index("y")).astype(jnp.int32).reshape(1)
    return pl.pallas_call(
        body, name=name,
        grid_spec=pltpu.PrefetchScalarGridSpec(
            num_scalar_prefetch=1, grid=(r // tr,),
            in_specs=[pl.BlockSpec((4, tr, cdim), lambda i, chip: (0, i, 0)),
                      pl.BlockSpec((None, tr, cdim), lambda i, chip: (chip[0], i, 0)), blk, blk, blk],
            out_specs=[blk] * 4),
        out_shape=[jax.ShapeDtypeStruct((r, cdim), F32)] * 4,
        compiler_params=_params("parallel"),
    )(my_chip, parts, sums, w, m, v)


class _NoExchange:
    def __init__(self, weights):
        self.weights = weights

    def before_pre_attn(self, g1):
        return g1

    def after_fox_fwd(self, fox_o, sinks_slot):
        return sinks_slot

    def after_attention(self, swa_o):
        return self.weights

    def after_early_grads(self, grads, d_col):
        return d_col

    def after_swa_bwd(self, dsq, d_row3):
        return d_row3

    def after_w_in_grad(self, d_win, g1):
        return g1


def _slot_order(t, axis):
    shp = t.shape
    t = t.reshape(shp[:axis] + (2, 4, shp[axis] // N_HEADS) + shp[axis + 1:])
    return jnp.swapaxes(t, axis, axis + 1).reshape(shp)


def _head_order(t, axis):
    shp = t.shape
    t = t.reshape(shp[:axis] + (4, 2, shp[axis] // N_HEADS) + shp[axis + 1:])
    return jnp.swapaxes(t, axis, axis + 1).reshape(shp)


def _forward_backward(x, p, target, win_t, hooks, b_forget, rel_bias, sinks, g1, g2, g3, g4, g5):
    S = x.shape[0]
    tm = 512
    tm_mlp = 512
    t = 256
    q0 = 3 * D_ATT + N_HEADS
    win_t = win_t.reshape(D_IN, D_MODEL)
    wt = jnp.concatenate(
        [win_t[:q0], jnp.zeros((8, D_MODEL), MM), _slot_order(win_t[q0:q0 + D_ATT], 0), win_t[q0 + D_ATT:]], axis=0)
    bcol = jnp.pad(b_forget.reshape(N_HEADS, 1), ((0, 8), (0, 0)))
    rel_bias_slot = rel_bias[:, np.array(SLOT_HEAD)]
    sinks_slot = sinks.reshape(N_HEADS)[np.array(SLOT_HEAD)]
    bucket = jnp.asarray(_swa_bucket_map())

    a, fqkv, sqkv, fft = _pre_attn(x, hooks.before_pre_attn(g1), wt, tm)
    c_row = _forget_cumsum(fft, bcol)
    c_col = c_row[:N_HEADS].T
    c_row3 = c_row[:N_HEADS].reshape(N_HEADS, S // t, t)
    fox_o, fox_lse = _fox_fwd(fqkv, c_row3, tq=512, tk=t)
    biasm = _swa_bias(rel_bias_slot, bucket)
    sinks_slot = hooks.after_fox_fwd(fox_o, sinks_slot)
    swa_o, swa_lse = _swa_fwd(sqkv, biasm, sinks_slot)
    wout, w1, w2, wple, wg = hooks.after_attention(swa_o)
    wout_fox = wout[:D_ATT]
    wout_swa = _slot_order(wout[D_ATT:], 0)
    mix, h1, m = _post_attn(x, fox_o, swa_o, wout_fox, wout_swa, g2, g3, tm)
    u, y, h2 = _mlp_fwd(m, h1, w1, w2, g4, tm_mlp)
    dh2, dpe, dgl, dg5, loss = _ple_loss(h2, p, target, wg, wple, g5, tm)

    d_wple = _weight_grad(p, dpe, "grad_w_ple", tk=D_PLE, n_chunks=N_DEV)
    d_wg = _weight_grad(h2, dgl, "grad_w_ple_gate", tk=256)
    dh1, dy, du, dg4, dg3 = _mlp_bwd(dh2, y, h1, u, w1, w2, g4, g3, tm)
    d_w2 = _weight_grad(u, dy, "grad_w_ff2", tk=256, relu2=True)
    d_w1 = _weight_grad(m, du, "grad_w_ff1", tk=256, n_chunks=N_DEV)
    head = np.arange(D_ATT) // HEAD_DIM
    head_rows = jnp.asarray((head[None, :] == np.arange(N_HEADS)[:, None]).astype(np.float32))
    dmix, dcat, d_row, d_swa, dg2 = _attn_out_bwd(dh1, mix, fox_o, swa_o, wout_fox, wout_swa, g2, head_rows, tm)
    d_col = d_swa.T
    d_wout_fox = _weight_grad(fox_o, dmix, "grad_w_out_fox", tk=256)
    d_wout_swa = _weight_grad(swa_o, dmix, "grad_w_out_swa", tk=256)
    d_wout = jnp.concatenate([d_wout_fox, _head_order(d_wout_swa, 0)], axis=0).reshape(N_DEV, D_MODEL // N_DEV, D_MODEL)
    early = dict(w_ff1=d_w1, w_ff2=d_w2.reshape(N_DEV, FF_CHUNK, D_MODEL), w_ple=d_wple,
                 w_ple_gate=d_wg.reshape(N_DEV, D_MODEL // N_DEV, D_MODEL), w_out=d_wout)

    d_col = hooks.after_early_grads(early, d_col)
    dsq, dsk, dsv, d_rb_slot, d_sink_slot = _swa_bwd(sqkv, dcat, biasm, sinks_slot, bucket, swa_lse, d_col)
    lse_row3 = fox_lse.T.reshape(N_HEADS, S // t, t)
    d_row3 = hooks.after_swa_bwd(dsq, d_row.reshape(N_HEADS, S // t, t))
    dq_fox, dk_fox, dv_fox, dc_col, dcq = _fox_bwd(fqkv, dcat, lse_row3, d_row3, c_col, tq=t, tk=512)
    dc_row = jnp.pad(dc_col.T + dcq.reshape(N_HEADS, S), ((0, 8), (0, 0)))
    dff_t, db = _forget_bwd(dc_row, fft, bcol)
    dz = _pack_dz(dq_fox, dk_fox, dv_fox, dsq, dsk, dsv, 512)
    d_wmain = _weight_grad(dz, a, "grad_w_in", tk=256)
    d_wff_t = _forget_weight_grad(dff_t, a)

    sq0 = 3 * D_ATT
    d_win = jnp.concatenate(
        [d_wmain[:sq0], d_wff_t[:N_HEADS].astype(MM), _head_order(d_wmain[sq0:sq0 + D_ATT], 0),
         d_wmain[sq0 + D_ATT:]], axis=0)
    d_win = d_win.reshape(N_DEV, D_IN // N_DEV, D_MODEL)
    grad_x, dg1 = _pre_attn_bwd(x, dh1, dz, dff_t, wt, hooks.after_w_in_grad(d_win, g1), tm)
    big = dict(early, w_in=d_win)
    small = dict(b_forget=db[:N_HEADS].reshape(1, N_HEADS), rel_bias=d_rb_slot[:, np.array(HEAD_SLOT)],
                 swa_sinks=d_sink_slot[:, np.array(HEAD_SLOT)], g_attn_pre=dg1, g_attn_post=dg2, g_ff_pre=dg3,
                 g_ff_post=dg4, g_ple_post=dg5)
    return loss, grad_x, big, small


BIG = ("w_in", "w_out", "w_ff1", "w_ff2", "w_ple", "w_ple_gate")
SMALL_ROWS = ("g_attn_pre", "g_attn_post", "g_ff_pre", "g_ff_post", "g_ple_post")
WEIGHTS =("w_in", "b_forget", "w_out", "rel_bias", "swa_sinks", "g_attn_pre", "g_attn_post", "w_ff1", "w_ff2",
           "g_ff_pre", "g_ff_post", "w_ple", "w_ple_gate", "g_ple_post")


EARLY = ("w_ff1", "w_ff2", "w_ple", "w_ple_gate", "w_out")


class _Overlap:
    def __init__(self, later):
        self.later = later

    def before_pre_attn(self, g1):
        self.gather_sems, self.later, token = _exchange_start("gather_rest_start", self.later, 4 * 5, _plan_gather_direct)
        return g1 + token

    def after_fox_fwd(self, fox_o, sinks_slot):
        later = _exchange_wait("gather_rest_wait", self.later, self.gather_sems, fox_o, _plan_gather_direct)
        self.pass_sems, self.later, token = _exchange_start("gather_pass_on_start", later, 3 * 5, _plan_gather_pass_on)
        return sinks_slot + token[0, :N_HEADS]

    def after_attention(self, swa_o):
        wout_g, w1_g, w2_g, wple_g, wg_g = _exchange_wait("gather_pass_on_wait", self.later, self.pass_sems, swa_o,
                                                         _plan_gather_pass_on)
        return (wout_g.reshape(D_MODEL, D_MODEL), w1_g, w2_g.reshape(D_FF, D_MODEL),
                jnp.moveaxis(wple_g, 0, 1).reshape(D_PLE, D_MODEL), wg_g.reshape(D_MODEL, D_MODEL))

    def after_early_grads(self, grads, d_col):
        views = [grads[k].reshape((4, 2) + grads[k].shape[1:]) for k in EARLY]
        lands = [lax.empty((4,) + grads[k].shape[1:], MM) for k in EARLY]
        self.in_chip_sems, self.in_chip, token = _exchange_start("grads_in_chip_start", views + lands, len(EARLY),
                                                                 _plan_in_chip)
        return d_col + token[0, 0]

    def after_swa_bwd(self, dsq, d_row3):
        arrays = _exchange_wait("grads_in_chip_wait", self.in_chip, self.in_chip_sems, dsq, _plan_in_chip)
        n = len(EARLY)
        sums = list(_chip_sums(arrays[:n], arrays[n:], "chip_sums_early"))
        lands = [lax.empty(s.shape, s.dtype) for s in sums]
        self.between_sems, self.between, token = _exchange_start("grads_between_chips_start", sums + lands, 3 * n,
                                                                 _plan_between_chips)
        return d_row3 + token[0, 0]

    def after_w_in_grad(self, d_win, g1):
        self.late_in_chip_sems, self.late_in_chip, token = _exchange_start(
            "late_in_chip_start", [d_win.reshape((4, 2) + d_win.shape[1:]), lax.empty((4,) + d_win.shape[1:], MM)],
            1, _plan_in_chip)
        return g1 + token

    def finish(self, after):
        arrays = _exchange_wait("grads_between_chips_wait", self.between, self.between_sems, after,
                                _plan_between_chips)
        n = len(EARLY)
        self.sums = arrays[:n]
        return arrays[n:]


def _pack_small(t):
    rows = [t[k].reshape(1, D_MODEL) for k in SMALL_ROWS]
    misc = jnp.concatenate([t["b_forget"].reshape(-1), t["swa_sinks"].reshape(-1), t["rel_bias"].reshape(-1)])
    rows.append(jnp.pad(misc, (0, D_MODEL - misc.shape[0])).reshape(1, D_MODEL))
    rows.append(jnp.pad(t["loss"].reshape(-1), (0, D_MODEL - 1)).reshape(1, D_MODEL))
    rows.append(jnp.zeros((1, D_MODEL), F32))
    return jnp.concatenate(rows, axis=0).astype(F32)


def _unpack_small(blk):
    out = {k: blk[i].reshape(1, D_MODEL) for i, k in enumerate(SMALL_ROWS)}
    misc = blk[len(SMALL_ROWS)]
    out["b_forget"] = misc[:N_HEADS].reshape(1, N_HEADS)
    out["swa_sinks"] = misc[N_HEADS:2 * N_HEADS].reshape(1, N_HEADS)
    out["rel_bias"] = misc[2 * N_HEADS:2 * N_HEADS + N_BUCKETS * N_HEADS].reshape(N_BUCKETS, N_HEADS)
    out["loss"] = blk[len(SMALL_ROWS) + 1, 0]
    return out


def kernel(x, p, w_in, b_forget, w_out, rel_bias, swa_sinks, g_attn_pre, g_attn_post, w_ff1, w_ff2, g_ff_pre, g_ff_post, w_ple, w_ple_gate, g_ple_post, loss_target, m_w_in, m_b_forget, m_w_out, m_rel_bias, m_swa_sinks, m_g_attn_pre, m_g_attn_post, m_w_ff1, m_w_ff2, m_g_ff_pre, m_g_ff_post, m_w_ple, m_w_ple_gate, m_g_ple_post, v_w_in, v_b_forget, v_w_out, v_rel_bias, v_swa_sinks, v_g_attn_pre, v_g_attn_post, v_w_ff1, v_w_ff2, v_g_ff_pre, v_g_ff_post, v_w_ple, v_w_ple_gate, v_g_ple_post):
    w = dict(w_in=w_in, b_forget=b_forget, w_out=w_out, rel_bias=rel_bias, swa_sinks=swa_sinks,
             g_attn_pre=g_attn_pre, g_attn_post=g_attn_post, w_ff1=w_ff1, w_ff2=w_ff2, g_ff_pre=g_ff_pre,
             g_ff_post=g_ff_post, w_ple=w_ple, w_ple_gate=w_ple_gate, g_ple_post=g_ple_post)
    mom = dict(w_in=m_w_in, b_forget=m_b_forget, w_out=m_w_out, rel_bias=m_rel_bias, swa_sinks=m_swa_sinks,
               g_attn_pre=m_g_attn_pre, g_attn_post=m_g_attn_post, w_ff1=m_w_ff1, w_ff2=m_w_ff2,
               g_ff_pre=m_g_ff_pre, g_ff_post=m_g_ff_post, w_ple=m_w_ple, w_ple_gate=m_w_ple_gate,
               g_ple_post=m_g_ple_post)
    var = dict(w_in=v_w_in, b_forget=v_b_forget, w_out=v_w_out, rel_bias=v_rel_bias, swa_sinks=v_swa_sinks,
               g_attn_pre=v_g_attn_pre, g_attn_post=v_g_attn_post, w_ff1=v_w_ff1, w_ff2=v_w_ff2,
               g_ff_pre=v_g_ff_pre, g_ff_post=v_g_ff_post, w_ple=v_w_ple, w_ple_gate=v_w_ple_gate,
               g_ple_post=v_g_ple_post)

    turn = lambda t, k: t.T if k == "w_in" else t
    me = 4 * lax.axis_index("x") + 2 * lax.axis_index("y") + lax.axis_index("c")

    def stack(block):
        return lax.dynamic_update_slice_in_dim(lax.empty((N_DEV,) + block.shape, block.dtype), block[None], me, 0)

    stacks = [stack(turn(w[k][0], k).astype(MM)) for k in BIG]
    (win_g,), later = _all_gather_sequencer(stacks[:1], "all_gather_sequencer", 1), stacks[1:]
    hooks = _Overlap(later)
    loss, grad_x, big, small = _forward_backward(
        x[0], p[0, 0], loss_target[0], win_g, hooks, b_forget, rel_bias, swa_sinks,
        g_attn_pre, g_attn_post, g_ff_pre, g_ff_post, g_ple_post)
    out_g, out_d, out_m, out_v = {}, {}, {}, {}

    def update(k, part, own):
        g, d, m_new, v_new = _adamw_chips(part, own, turn(w[k][0], k), turn(mom[k][0], k), turn(var[k][0], k),
                                          "adamw_" + k)
        out_g[k], out_d[k], out_m[k], out_v[k] = turn(g, k)[None], turn(d, k)[None], turn(m_new, k)[None], turn(v_new, k)[None]
        return d

    view, other = _exchange_wait("late_in_chip_wait", hooks.late_in_chip, hooks.late_in_chip_sems, grad_x,
                                 _plan_in_chip)
    (chip_sum,) = _chip_sums([view], [other], "chip_sum_w_in")
    small["loss"] = loss
    between_sems, between, token = _exchange_start(
        "late_between_chips_start", [chip_sum, lax.empty(chip_sum.shape, MM), stack(_pack_small(small))], 3 + 7,
        _plan_late_between)
    early_parts = hooks.finish(token)
    done = [update(k, part, own) for k, part, own in zip(EARLY, early_parts, hooks.sums)]
    chip_sum, part, small_all = _exchange_wait("late_between_chips_wait", between, between_sems, done,
                                               _plan_late_between)
    update("w_in", part, chip_sum)
    rep = {k: w[k] for k in w if k not in BIG}
    rep["loss"] = jnp.zeros((), F32)
    rep_m = {k: mom[k] for k in mom if k not in BIG}
    rep_m["loss"] = jnp.zeros((), F32)
    rep_v = {k: var[k] for k in var if k not in BIG}
    rep_v["loss"] = jnp.ones((), F32)
    g_s, d_s, m_s, v_s = _adamw(small_all, _pack_small(rep), _pack_small(rep_m), _pack_small(rep_v), "adamw_small")
    g_s, d_s, m_s, v_s = _unpack_small(g_s), _unpack_small(d_s), _unpack_small(m_s), _unpack_small(v_s)
    for k in w:
        if k not in BIG:
            out_g[k], out_d[k], out_m[k], out_v[k] = g_s[k], d_s[k], m_s[k], v_s[k]
    return (g_s["loss"], grad_x[None], *[out_g[k] for k in WEIGHTS], *[out_d[k] for k in WEIGHTS],
            *[out_m[k] for k in WEIGHTS], *[out_v[k] for k in WEIGHTS])
```

```python
import functools

import numpy as np
import jax
import jax.numpy as jnp
from jax import lax
from jax.experimental import pallas as pl
from jax.experimental.pallas import tpu as pltpu
from jax.experimental.pallas import tpu_sc as plsc

F32 = jnp.float32
MM = jnp.bfloat16

D_MODEL = 1024
HEAD_DIM = 64
N_HEADS = 8
D_ATT = N_HEADS * HEAD_DIM
D_KV = 128
D_FF = 4096
D_PLE = 256
D_IN = 3 * D_ATT + N_HEADS + D_ATT + 2 * D_KV
N_DEV = 8
FF_CHUNK = D_FF // N_DEV
WINDOW = 128
N_BUCKETS = 32
MAX_DISTANCE = 128
RMS_EPS = 1e-6
Q_SCALE = HEAD_DIM ** -0.5
NEG = -1e30

ADAM_LR = 0.001
ADAM_B1 = 0.9
ADAM_B2 = 0.999
ADAM_EPS = 1e-08
ADAM_WD = 0.01
ADAM_STEP = 10

SLOT_HEAD = (0, 4, 1, 5, 2, 6, 3, 7)
HEAD_SLOT = (0, 2, 4, 6, 1, 3, 5, 7)

VMEM_LIMIT = 60 * 1024 * 1024
MESH = pl.DeviceIdType.MESH

NT = (((1,), (1,)), ((), ()))
TN = (((0,), (0,)), ((), ()))


def _params(*semantics):
    return pltpu.CompilerParams(dimension_semantics=semantics, vmem_limit_bytes=VMEM_LIMIT)


def _resident():
    return pl.BlockSpec(memory_space=pltpu.VMEM)


def _rows(tm, width):
    return pl.BlockSpec((tm, width), lambda i: (i, 0))


def _const(shape):
    return pl.BlockSpec(shape, lambda i: (0,) * len(shape))


def _dot(a, b):
    return jnp.dot(a, b, preferred_element_type=F32)


def _dot_nt(a, b):
    return lax.dot_general(a, b, NT, preferred_element_type=F32)


def _dot_tn(a, b):
    return lax.dot_general(a, b, TN, preferred_element_type=F32)


def _rms(xf):
    r = lax.rsqrt(jnp.mean(xf * xf, axis=-1, keepdims=True) + RMS_EPS)
    return xf * r, r


def _rms_bwd(dout, n, r, g):
    dg = jnp.sum(dout * n, axis=0, keepdims=True)
    dn = dout * g
    dx = r * (dn - n * jnp.mean(dn * n, axis=-1, keepdims=True))
    return dx, dg


def _accumulate(ref, value, step):
    @pl.when(step == 0)
    def _():
        ref[...] = value

    @pl.when(step != 0)
    def _():
        ref[...] += value


def _t5_bucket(n):
    max_exact = N_BUCKETS // 2
    large = max_exact + (np.log(np.maximum(n, 1) / max_exact) / np.log(MAX_DISTANCE / max_exact)
                         * (N_BUCKETS - max_exact)).astype(np.int32)
    large = np.minimum(large, N_BUCKETS - 1)
    return np.where(n < max_exact, n, large).astype(np.int32)


def _swa_bucket_map():
    i = np.arange(WINDOW)[:, None]
    j = np.arange(2 * WINDOW)[None, :]
    dist = i + WINDOW - j
    ok = (dist >= 0) & (dist < WINDOW)
    return np.where(ok, _t5_bucket(np.clip(dist, 0, None)), -1).astype(np.int32)


WT_FOX = 0
WT_FF = 3 * D_ATT
WT_SQ = WT_FF + 16
WT_SKV = WT_SQ + D_ATT
WT_ROWS = WT_SKV + 2 * D_KV


def _pre_attn(x, g1, wt, tm):
    S = x.shape[0]

    def body(x_ref, g_ref, wt_ref, a_ref, fqkv_ref, sqkv_ref, fft_ref):
        n, _ = _rms(x_ref[...])
        a = (n * g_ref[...]).astype(MM)
        a_ref[...] = a
        fqkv_ref[:, :D_ATT] = (_dot_nt(a, wt_ref[WT_FOX:WT_FOX + D_ATT]) * Q_SCALE).astype(MM)
        fqkv_ref[:, D_ATT:] = _dot_nt(a, wt_ref[WT_FOX + D_ATT:WT_FF]).astype(MM)
        sqkv_ref[:, :D_ATT] = (_dot_nt(a, wt_ref[WT_SQ:WT_SKV]) * Q_SCALE).astype(MM)
        sqkv_ref[:, D_ATT:] = _dot_nt(a, wt_ref[WT_SKV:WT_ROWS]).astype(MM)
        fft_ref[...] = _dot_nt(wt_ref[WT_FF:WT_SQ], a)

    return pl.pallas_call(
        body, name="pre_attn", grid=(S // tm,),
        in_specs=[_rows(tm, D_MODEL), _const((1, D_MODEL)), _resident()],
        out_specs=[_rows(tm, D_MODEL), _rows(tm, 3 * D_ATT), _rows(tm, D_ATT + 2 * D_KV),
                   pl.BlockSpec((16, tm), lambda i: (0, i))],
        out_shape=[jax.ShapeDtypeStruct((S, D_MODEL), MM), jax.ShapeDtypeStruct((S, 3 * D_ATT), MM),
                   jax.ShapeDtypeStruct((S, D_ATT + 2 * D_KV), MM), jax.ShapeDtypeStruct((16, S), F32)],
        compiler_params=_params("parallel"),
    )(x, g1, wt)


def _lane_scan(v, reverse):
    S = v.shape[1]
    lane = lax.broadcasted_iota(jnp.int32, v.shape, 1)
    k = 1
    while k < S:
        if reverse:
            v = v + jnp.where(lane < S - k, pltpu.roll(v, S - k, axis=1), 0.0)
        else:
            v = v + jnp.where(lane >= k, pltpu.roll(v, k, axis=1), 0.0)
        k *= 2
    return v


def _forget_cumsum(fft, bcol):
    def body(f_ref, b_ref, c_ref):
        z = f_ref[...] + b_ref[...]
        log_f = jnp.minimum(z, 0.0) - jnp.log1p(jnp.exp(-jnp.abs(z)))
        c_ref[...] = _lane_scan(log_f, reverse=False)

    return pl.pallas_call(
        body, name="forget_cumsum", out_shape=jax.ShapeDtypeStruct(fft.shape, F32),
        in_specs=[_resident(), _resident()], out_specs=_resident(),
    )(fft, bcol)


def _forget_bwd(dc_row, fft, bcol, a):
    def body(dc_ref, f_ref, b_ref, a_ref, dff_ref, db_ref, dw_ref):
        z = f_ref[...] + b_ref[...]
        dlog_f = _lane_scan(dc_ref[...], reverse=True)
        dff = dlog_f * (1.0 / (1.0 + jnp.exp(z)))
        dff_ref[...] = dff
        db_ref[...] = jnp.sum(dff, axis=1, keepdims=True)
        dw_ref[...] = _dot(dff.astype(MM), a_ref[...])

    return pl.pallas_call(
        body, name="forget_bwd",
        out_shape=[jax.ShapeDtypeStruct(fft.shape, F32), jax.ShapeDtypeStruct((fft.shape[0], 1), F32),
                   jax.ShapeDtypeStruct((fft.shape[0], D_MODEL), F32)],
        in_specs=[_resident()] * 4, out_specs=[_resident()] * 3,
    )(dc_row, fft, bcol, a)


def _head_select(shape, upper):
    lane = lax.broadcasted_iota(jnp.int32, shape, 1)
    return lane >= HEAD_DIM if upper else lane < HEAD_DIM


def _fox_fwd(fqkv, c_row3, tq, tk, pairs_per_loop=2, row_chunks=1):
    S = fqkv.shape[0]
    rq = tq // row_chunks
    n_band = tq // tk

    def body(q_ref, k_ref, v_ref, ck_ref, o_ref, lse_ref):
        qi = pl.program_id(0)
        row = lax.broadcasted_iota(jnp.int32, (rq, tk), 0)
        col = lax.broadcasted_iota(jnp.int32, (rq, tk), 1)
        low = _head_select((rq, 128), 0)
        for first in range(0, N_HEADS // 2, pairs_per_loop):
            pairs = range(first, first + pairs_per_loop)
            chains = [(pr, hh, rc) for pr in pairs for hh in range(2) for rc in range(row_chunks)]
            qh = {}
            for pr in pairs:
                for rc in range(row_chunks):
                    q2 = q_ref[rc * rq:(rc + 1) * rq, pr * 128:(pr + 1) * 128]
                    qh[pr, 0, rc] = jnp.where(low, q2, jnp.zeros_like(q2))
                    qh[pr, 1, rc] = jnp.where(low, jnp.zeros_like(q2), q2)

            def block(kb, carry, band, chains=chains, qh=qh):
                rows = pl.ds(pl.multiple_of(kb * tk, tk), tk)
                out = []
                for (pr, hh, rc), (m, l, acc) in zip(chains, carry):
                    if band is not None and (rc + 1) * rq <= band * tk:
                        out.append((m, l, acc))
                        continue
                    lanes = slice(pr * 128, (pr + 1) * 128)
                    s = _dot_nt(qh[pr, hh, rc], k_ref[rows, lanes]) - ck_ref[2 * pr + hh, pl.ds(kb, 1), :]
                    if band is not None:
                        s = jnp.where(row + rc * rq >= col + band * tk, s, NEG)
                    m_new = jnp.maximum(m, jnp.max(s, axis=-1, keepdims=True))
                    p = jnp.exp(s - m_new)
                    alpha = jnp.exp(m - m_new)
                    l = alpha * l + jnp.sum(p, axis=-1, keepdims=True)
                    acc = alpha * acc + _dot(p.astype(MM), v_ref[rows, lanes])
                    out.append((m_new, l, acc))
                return tuple(out)

            carry = tuple((jnp.full((rq, 1), NEG, F32), jnp.zeros((rq, 1), F32), jnp.zeros((rq, 128), F32))
                          for _ in chains)
            carry = lax.fori_loop(0, qi * n_band, functools.partial(block, band=None), carry)
            for band in range(n_band):
                carry = block(qi * n_band + band, carry, band=band)
            res = {}
            for (pr, hh, rc), (m, l, acc) in zip(chains, carry):
                res[pr, hh, rc] = acc / l
                lse_ref[rc * rq:(rc + 1) * rq, 2 * pr + hh:2 * pr + hh + 1] = m + jnp.log(l)
            for pr in pairs:
                for rc in range(row_chunks):
                    o_ref[rc * rq:(rc + 1) * rq, pr * 128:(pr + 1) * 128] = jnp.where(
                        low, res[pr, 0, rc], res[pr, 1, rc]).astype(MM)

    return pl.pallas_call(
        body, name="fox_fwd", grid=(S // tq,),
        in_specs=[pl.BlockSpec((tq, D_ATT), lambda i: (i, 0)), pl.BlockSpec((S, D_ATT), lambda i: (0, 1)),
                  pl.BlockSpec((S, D_ATT), lambda i: (0, 2)), _resident()],
        out_specs=[_rows(tq, D_ATT), _rows(tq, N_HEADS)],
        out_shape=[jax.ShapeDtypeStruct((S, D_ATT), MM), jax.ShapeDtypeStruct((S, N_HEADS), F32)],
        compiler_params=_params("parallel"),
    )(fqkv, fqkv, fqkv, c_row3)


def _swa_bias(rel_bias_slot, bucket):
    def body(rb_ref, bk_ref, out_ref):
        bk = bk_ref[...]
        for s in range(N_HEADS):
            acc = jnp.where(bk < 0, NEG, 0.0).astype(F32)
            for b in range(N_BUCKETS):
                acc = jnp.where(bk == b, rb_ref[b, s], acc)
            out_ref[s] = acc

    return pl.pallas_call(
        body, name="swa_bias", out_shape=jax.ShapeDtypeStruct((N_HEADS, WINDOW, 2 * WINDOW), F32),
        in_specs=[pl.BlockSpec(memory_space=pltpu.SMEM), _resident()], out_specs=_resident(),
    )(rel_bias_slot, bucket)


def _stack4(piece):
    return jnp.concatenate([piece(j) for j in range(4)], axis=0)


def _swa_specs(S):
    q = pl.BlockSpec((WINDOW, D_ATT), lambda n: (n, 0))
    kp = pl.BlockSpec((WINDOW, D_KV), lambda n: (jnp.maximum(n - 1, 0), 4))
    kc = pl.BlockSpec((WINDOW, D_KV), lambda n: (n, 4))
    vp = pl.BlockSpec((WINDOW, D_KV), lambda n: (jnp.maximum(n - 1, 0), 5))
    vc = pl.BlockSpec((WINDOW, D_KV), lambda n: (n, 5))
    return [q, kp, kc, vp, vc]


def _swa_fwd(sqkv, biasm, sinks_slot):
    S = sqkv.shape[0]

    def body(q_ref, kp_ref, kc_ref, vp_ref, vc_ref, bias_ref, sink_ref, o_ref, lse_ref):
        n = pl.program_id(0)
        no_prev = jnp.where(n > 0, 0.0, NEG)
        low = _head_select((WINDOW, 128), 0)
        res = []
        for g in range(2):
            sel = low if g == 0 else jnp.logical_not(low)
            qg = _stack4(lambda j: jnp.where(sel, q_ref[:, j * 128:(j + 1) * 128], jnp.zeros((WINDOW, 128), MM)))
            sink = _stack4(lambda j: jnp.full((WINDOW, 1), sink_ref[2 * j + g], F32))
            sp = _dot_nt(qg, kp_ref[...]) + _stack4(lambda j: bias_ref[2 * j + g, :, :WINDOW]) + no_prev
            sc = _dot_nt(qg, kc_ref[...]) + _stack4(lambda j: bias_ref[2 * j + g, :, WINDOW:])
            m = jnp.maximum(jnp.maximum(jnp.max(sp, axis=-1, keepdims=True),
                                        jnp.max(sc, axis=-1, keepdims=True)), sink)
            ep = jnp.exp(sp - m)
            ec = jnp.exp(sc - m)
            den = jnp.sum(ep, axis=-1, keepdims=True) + jnp.sum(ec, axis=-1, keepdims=True) + jnp.exp(sink - m)
            res.append((_dot(ep.astype(MM), vp_ref[...]) + _dot(ec.astype(MM), vc_ref[...])) / den)
            lse = m + jnp.log(den)
            for j in range(4):
                lse_ref[:, 2 * j + g:2 * j + g + 1] = lse[j * WINDOW:(j + 1) * WINDOW]
        for j in range(4):
            rows = slice(j * WINDOW, (j + 1) * WINDOW)
            o_ref[:, j * 128:(j + 1) * 128] = jnp.where(low, res[0][rows], res[1][rows]).astype(MM)

    return pl.pallas_call(
        body, name="swa_fwd", grid=(S // WINDOW,),
        in_specs=_swa_specs(S) + [_resident(), pl.BlockSpec(memory_space=pltpu.SMEM)],
        out_specs=[_rows(WINDOW, D_ATT), _rows(WINDOW, N_HEADS)],
        out_shape=[jax.ShapeDtypeStruct((S, D_ATT), MM), jax.ShapeDtypeStruct((S, N_HEADS), F32)],
        compiler_params=_params("parallel"),
    )(sqkv, sqkv, sqkv, sqkv, sqkv, biasm, sinks_slot)


def _post_attn(x, fox_o, swa_o, wout_fox, wout_swa, g2, g3, tm):
    S = x.shape[0]

    def body(x_ref, fo_ref, so_ref, wf_ref, ws_ref, g2_ref, g3_ref, mix_ref, h1_ref, m_ref):
        mix = _dot(fo_ref[...], wf_ref[...]) + _dot(so_ref[...], ws_ref[...])
        mix_ref[...] = mix
        n2, _ = _rms(mix)
        h1 = x_ref[...] + n2 * g2_ref[...]
        h1_ref[...] = h1
        n3, _ = _rms(h1)
        m_ref[...] = (n3 * g3_ref[...]).astype(MM)

    return pl.pallas_call(
        body, name="post_attn", grid=(S // tm,),
        in_specs=[_rows(tm, D_MODEL), _rows(tm, D_ATT), _rows(tm, D_ATT), _resident(), _resident(),
                  _const((1, D_MODEL)), _const((1, D_MODEL))],
        out_specs=[_rows(tm, D_MODEL)] * 3,
        out_shape=[jax.ShapeDtypeStruct((S, D_MODEL), F32), jax.ShapeDtypeStruct((S, D_MODEL), F32),
                   jax.ShapeDtypeStruct((S, D_MODEL), MM)],
        compiler_params=_params("parallel"),
    )(x, fox_o, swa_o, wout_fox, wout_swa, g2, g3)


def _mlp_fwd(m, h1, w1, w2, g4, tm):
    S = m.shape[0]

    def body(m_ref, h1_ref, w1_ref, w2_ref, g4_ref, u_ref, y_ref, h2_ref):
        mb = m_ref[...]
        y = jnp.zeros((tm, D_MODEL), F32)
        for j in range(N_DEV):
            cols = slice(j * FF_CHUNK, (j + 1) * FF_CHUNK)
            u = _dot(mb, w1_ref[j])
            u_ref[:, cols] = u.astype(MM)
            y = y + _dot(jnp.square(jnp.maximum(u, 0.0)).astype(MM), w2_ref[cols, :])
        y_ref[...] = y
        n4, _ = _rms(y)
        h2_ref[...] = h1_ref[...] + n4 * g4_ref[...]

    return pl.pallas_call(
        body, name="mlp_fwd", grid=(S // tm,),
        in_specs=[_rows(tm, D_MODEL), _rows(tm, D_MODEL), _resident(), _resident(), _const((1, D_MODEL))],
        out_specs=[_rows(tm, D_FF), _rows(tm, D_MODEL), _rows(tm, D_MODEL)],
        out_shape=[jax.ShapeDtypeStruct((S, D_FF), MM), jax.ShapeDtypeStruct((S, D_MODEL), F32),
                   jax.ShapeDtypeStruct((S, D_MODEL), F32)],
        compiler_params=_params("parallel"),
    )(m, h1, w1, w2, g4)


def _ple_loss(h2, p, target, wg, wple, g5, tm):
    S = h2.shape[0]

    def body(h2_ref, p_ref, t_ref, wg_ref, wp_ref, g5_ref, dh2_ref, dpe_ref, dgl_ref, dg5_ref, loss_ref):
        i = pl.program_id(0)
        h2 = h2_ref[...]
        gate = jax.nn.sigmoid(_dot(h2.astype(MM), wg_ref[...]))
        pe = _dot(p_ref[...].astype(MM), wp_ref[...])
        n5, r5 = _rms(pe * gate)
        g5 = g5_ref[...]
        diff = h2 + n5 * g5 - t_ref[...]
        per_token = jnp.mean(jnp.square(diff), axis=-1, keepdims=True)
        _accumulate(loss_ref, 0.5 * jnp.sum(per_token, axis=0, keepdims=True), i)
        dh3 = diff * (1.0 / D_MODEL)
        de, dg5 = _rms_bwd(dh3, n5, r5, g5)
        _accumulate(dg5_ref, dg5, i)
        dpe_ref[...] = (de * gate).astype(MM)
        dgl = (de * pe * gate * (1.0 - gate)).astype(MM)
        dgl_ref[...] = dgl
        dh2_ref[...] = dh3 + _dot_nt(dgl, wg_ref[...])

    return pl.pallas_call(
        body, name="ple_loss", grid=(S // tm,),
        in_specs=[_rows(tm, D_MODEL), _rows(tm, D_PLE), _rows(tm, D_MODEL), _resident(), _resident(),
                  _const((1, D_MODEL))],
        out_specs=[_rows(tm, D_MODEL), _rows(tm, D_MODEL), _rows(tm, D_MODEL), _const((1, D_MODEL)), _const((1, 1))],
        out_shape=[jax.ShapeDtypeStruct((S, D_MODEL), F32), jax.ShapeDtypeStruct((S, D_MODEL), MM),
                   jax.ShapeDtypeStruct((S, D_MODEL), MM), jax.ShapeDtypeStruct((1, D_MODEL), F32),
                   jax.ShapeDtypeStruct((1, 1), F32)],
        compiler_params=_params("arbitrary"),
    )(h2, p, target, wg, wple, g5)


def _mlp_bwd(dh2, y, h1, u, w1, w2, g4, g3, tm):
    S = dh2.shape[0]

    def body(dh2_ref, y_ref, h1_ref, u_ref, w1_ref, w2_ref, g4_ref, g3_ref,
             dh1_ref, dy_ref, du_ref, dg4_ref, dg3_ref):
        i = pl.program_id(0)
        dh2 = dh2_ref[...]
        n4, r4 = _rms(y_ref[...])
        dy, dg4 = _rms_bwd(dh2, n4, r4, g4_ref[...])
        _accumulate(dg4_ref, dg4, i)
        dyb = dy.astype(MM)
        dy_ref[...] = dyb
        dm = jnp.zeros((tm, D_MODEL), F32)
        for j in range(N_DEV):
            cols = slice(j * FF_CHUNK, (j + 1) * FF_CHUNK)
            dact = _dot_nt(dyb, w2_ref[cols, :])
            du = (dact * (2.0 * jnp.maximum(u_ref[:, cols].astype(F32), 0.0))).astype(MM)
            du_ref[:, cols] = du
            dm = dm + _dot_nt(du, w1_ref[j])
        n3, r3 = _rms(h1_ref[...])
        dx, dg3 = _rms_bwd(dm, n3, r3, g3_ref[...])
        _accumulate(dg3_ref, dg3, i)
        dh1_ref[...] = dh2 + dx

    return pl.pallas_call(
        body, name="mlp_bwd", grid=(S // tm,),
        in_specs=[_rows(tm, D_MODEL), _rows(tm, D_MODEL), _rows(tm, D_MODEL), _rows(tm, D_FF),
                  _resident(), _resident(), _const((1, D_MODEL)), _const((1, D_MODEL))],
        out_specs=[_rows(tm, D_MODEL), _rows(tm, D_MODEL), _rows(tm, D_FF), _const((1, D_MODEL)),
                   _const((1, D_MODEL))],
        out_shape=[jax.ShapeDtypeStruct((S, D_MODEL), F32), jax.ShapeDtypeStruct((S, D_MODEL), MM),
                   jax.ShapeDtypeStruct((S, D_FF), MM), jax.ShapeDtypeStruct((1, D_MODEL), F32),
                   jax.ShapeDtypeStruct((1, D_MODEL), F32)],
        compiler_params=_params("arbitrary"),
    )(dh2, y, h1, u, w1, w2, g4, g3)


def _attn_out_bwd(dh1, mix, fox_o, swa_o, wout_fox, wout_swa, g2, head_rows, tm):
    S = dh1.shape[0]

    def body(dh1_ref, mix_ref, fo_ref, so_ref, wf_ref, ws_ref, g2_ref, er_ref,
             dmix_ref, dcat_ref, drow_ref, dswa_ref, dg2_ref):
        i = pl.program_id(0)
        n2, r2 = _rms(mix_ref[...])
        dmix, dg2 = _rms_bwd(dh1_ref[...], n2, r2, g2_ref[...])
        _accumulate(dg2_ref, dg2, i)
        dmb = dmix.astype(MM)
        dmix_ref[...] = dmb
        dfo = _dot_nt(dmb, wf_ref[...]).astype(MM)
        dso = _dot_nt(dmb, ws_ref[...]).astype(MM)
        dcat_ref[:, :D_ATT] = dfo
        dcat_ref[:, D_ATT:] = dso
        hi = lax.Precision.HIGHEST
        prod_f = dfo.astype(F32) * fo_ref[...].astype(F32)
        prod_s = dso.astype(F32) * so_ref[...].astype(F32)
        drow_ref[...] = lax.dot_general(er_ref[...], prod_f, NT, precision=hi, preferred_element_type=F32)
        dswa_ref[...] = lax.dot_general(er_ref[...], prod_s, NT, precision=hi, preferred_element_type=F32)

    return pl.pallas_call(
        body, name="attn_out_bwd", grid=(S // tm,),
        in_specs=[_rows(tm, D_MODEL), _rows(tm, D_MODEL), _rows(tm, D_ATT), _rows(tm, D_ATT), _resident(),
                  _resident(), _const((1, D_MODEL)), _resident()],
        out_specs=[_rows(tm, D_MODEL), _rows(tm, D_MODEL), pl.BlockSpec((N_HEADS, tm), lambda i: (0, i)),
                   pl.BlockSpec((N_HEADS, tm), lambda i: (0, i)), _const((1, D_MODEL))],
        out_shape=[jax.ShapeDtypeStruct((S, D_MODEL), MM), jax.ShapeDtypeStruct((S, D_MODEL), MM),
                   jax.ShapeDtypeStruct((N_HEADS, S), F32), jax.ShapeDtypeStruct((N_HEADS, S), F32),
                   jax.ShapeDtypeStruct((1, D_MODEL), F32)],
        compiler_params=_params("arbitrary"),
    )(dh1, mix, fox_o, swa_o, wout_fox, wout_swa, g2, head_rows)


def _fox_bwd(fqkv, dcat, lse_row3, d_row3, c_col, tq, tk, pairs_per_loop=2):
    S = fqkv.shape[0]
    n_blk = S // tk
    n_qblk = S // tq
    n_band = tk // tq

    def body(q_ref, k_ref, v_ref, do_ref, lse_ref, dd_ref, ck_ref, dq_ref, dk_ref, dv_ref, dc_ref, dcq_ref):
        kb = pl.program_id(0)

        @pl.when(kb == 0)
        def _():
            dq_ref[...] = jnp.zeros_like(dq_ref)
            dcq_ref[...] = jnp.zeros_like(dcq_ref)

        key = lax.broadcasted_iota(jnp.int32, (tk, tq), 0)
        qry = lax.broadcasted_iota(jnp.int32, (tk, tq), 1)
        low = _head_select((tk, 128), 0)
        for first in range(0, N_HEADS // 2, pairs_per_loop):
            pairs = range(first, first + pairs_per_loop)
            heads = [(pr, hh) for pr in pairs for hh in range(2)]
            kh, vh, ck = {}, {}, {}
            for pr in pairs:
                k2 = k_ref[:, pr * 128:(pr + 1) * 128]
                v2 = v_ref[:, pr * 128:(pr + 1) * 128]
                zero = jnp.zeros_like(k2)
                kh[pr, 0], kh[pr, 1] = jnp.where(low, k2, zero), jnp.where(low, zero, k2)
                vh[pr, 0], vh[pr, 1] = jnp.where(low, v2, zero), jnp.where(low, zero, v2)
                for hh in range(2):
                    ck[pr, hh] = ck_ref[:, 2 * pr + hh:2 * pr + hh + 1]

            def block(qb, carry, band, pairs=pairs, kh=kh, vh=vh, ck=ck):
                rows = pl.ds(pl.multiple_of(qb * tq, tq), tq)
                k1 = tk if band is None else (band + 1) * tq
                out = []
                it = iter(carry)
                for pr in pairs:
                    lanes = slice(pr * 128, (pr + 1) * 128)
                    q2 = q_ref[rows, lanes]
                    do2 = do_ref[rows, lanes]
                    dq = None
                    for hh in range(2):
                        h = 2 * pr + hh
                        dk, dv, dc = next(it)
                        s_t = _dot_nt(kh[pr, hh][:k1], q2) - ck[pr, hh][:k1]
                        p_t = jnp.exp(s_t - lse_ref[h, pl.ds(qb, 1), :])
                        if band is not None:
                            p_t = jnp.where(qry[:k1] + band * tq >= key[:k1], p_t, 0.0)
                        ds_t = p_t * (_dot_nt(vh[pr, hh][:k1], do2) - dd_ref[h, pl.ds(qb, 1), :])
                        dsb = ds_t.astype(MM)
                        dv_new = dv[:k1] + _dot(p_t.astype(MM), do2)
                        dk_new = dk[:k1] + _dot(dsb, q2)
                        dc_new = dc[:k1] - jnp.sum(ds_t, axis=1, keepdims=True)
                        if k1 < tk:
                            dv_new = jnp.concatenate([dv_new, dv[k1:]], axis=0)
                            dk_new = jnp.concatenate([dk_new, dk[k1:]], axis=0)
                            dc_new = jnp.concatenate([dc_new, dc[k1:]], axis=0)
                        part = _dot_tn(dsb, kh[pr, hh][:k1])
                        dq = part if dq is None else dq + part
                        dcq_ref[h, pl.ds(qb, 1), :] += jnp.sum(ds_t, axis=0, keepdims=True)
                        out.append((dk_new, dv_new, dc_new))
                    dq_ref[rows, lanes] += dq
                return tuple(out)

            carry = tuple((jnp.zeros((tk, 128), F32), jnp.zeros((tk, 128), F32), jnp.zeros((tk, 1), F32))
                          for _ in heads)
            for band in range(n_band):
                carry = block(kb * n_band + band, carry, band=band)
            carry = lax.fori_loop((kb + 1) * n_band, n_qblk, functools.partial(block, band=None), carry)
            grads = dict(zip(heads, carry))
            for pr in pairs:
                lanes = slice(pr * 128, (pr + 1) * 128)
                dk_ref[:, lanes] = jnp.where(low, grads[pr, 0][0], grads[pr, 1][0]).astype(MM)
                dv_ref[:, lanes] = jnp.where(low, grads[pr, 0][1], grads[pr, 1][1]).astype(MM)
                for hh in range(2):
                    dc_ref[:, 2 * pr + hh:2 * pr + hh + 1] = grads[pr, hh][2]

        @pl.when(kb == n_blk - 1)
        def _():
            dq_ref[...] = dq_ref[...] * Q_SCALE

    return pl.pallas_call(
        body, name="fox_bwd", grid=(n_blk,),
        in_specs=[pl.BlockSpec((S, D_ATT), lambda i: (0, 0)), pl.BlockSpec((tk, D_ATT), lambda i: (i, 1)),
                  pl.BlockSpec((tk, D_ATT), lambda i: (i, 2)), pl.BlockSpec((S, D_ATT), lambda i: (0, 0)),
                  _resident(), _resident(), _rows(tk, N_HEADS)],
        out_specs=[_const((S, D_ATT)), _rows(tk, D_ATT), _rows(tk, D_ATT), _rows(tk, N_HEADS),
                   _const((N_HEADS, n_qblk, tq))],
        out_shape=[jax.ShapeDtypeStruct((S, D_ATT), F32), jax.ShapeDtypeStruct((S, D_ATT), MM),
                   jax.ShapeDtypeStruct((S, D_ATT), MM), jax.ShapeDtypeStruct((S, N_HEADS), F32),
                   jax.ShapeDtypeStruct((N_HEADS, n_qblk, tq), F32)],
        compiler_params=_params("arbitrary"),
    )(fqkv, fqkv, fqkv, dcat, lse_row3, d_row3, c_col)


def _swa_bwd(sqkv, dcat, biasm, sinks_slot, bucket, lse, d_col):
    S = sqkv.shape[0]
    n_blk = S // WINDOW

    def body(q_ref, kp_ref, kc_ref, vp_ref, vc_ref, do_ref, bias_ref, sink_ref, bk_ref, lse_ref, dd_ref,
             dq_ref, dk_ref, dv_ref, drb_ref, dsink_ref, ds_acc):
        n = pl.program_id(0)

        @pl.when(n == 0)
        def _():
            dk_ref[...] = jnp.zeros_like(dk_ref)
            dv_ref[...] = jnp.zeros_like(dv_ref)
            ds_acc[...] = jnp.zeros_like(ds_acc)
            dsink_ref[...] = jnp.zeros_like(dsink_ref)

        no_prev = jnp.where(n > 0, 0.0, NEG)
        prev = pl.ds(pl.multiple_of(jnp.maximum(n - 1, 0) * WINDOW, WINDOW), WINDOW)
        cur = pl.ds(pl.multiple_of(n * WINDOW, WINDOW), WINDOW)
        lane8 = lax.broadcasted_iota(jnp.int32, (1, N_HEADS), 1)
        dkp = jnp.zeros((WINDOW, D_KV), F32)
        dkc = jnp.zeros((WINDOW, D_KV), F32)
        dvp = jnp.zeros((WINDOW, D_KV), F32)
        dvc = jnp.zeros((WINDOW, D_KV), F32)
        dsink = jnp.zeros((1, N_HEADS), F32)
        low = _head_select((WINDOW, 128), 0)
        zero = jnp.zeros((WINDOW, 128), MM)
        dqs = []
        for g in range(2):
            sel = low if g == 0 else jnp.logical_not(low)
            qg = _stack4(lambda j: jnp.where(sel, q_ref[:, j * 128:(j + 1) * 128], zero))
            dog = _stack4(lambda j: jnp.where(sel, do_ref[:, j * 128:(j + 1) * 128], zero))
            lse_g = _stack4(lambda j: lse_ref[:, 2 * j + g:2 * j + g + 1])
            dd = _stack4(lambda j: dd_ref[:, 2 * j + g:2 * j + g + 1])
            sink = _stack4(lambda j: jnp.full((WINDOW, 1), sink_ref[2 * j + g], F32))
            pp = jnp.exp(_dot_nt(qg, kp_ref[...]) + _stack4(lambda j: bias_ref[2 * j + g, :, :WINDOW]) + no_prev - lse_g)
            pc = jnp.exp(_dot_nt(qg, kc_ref[...]) + _stack4(lambda j: bias_ref[2 * j + g, :, WINDOW:]) - lse_g)
            sink_term = jnp.exp(sink - lse_g) * dd
            dsp = pp * (_dot_nt(dog, vp_ref[...]) - dd)
            dsc = pc * (_dot_nt(dog, vc_ref[...]) - dd)
            for j in range(4):
                rows = slice(j * WINDOW, (j + 1) * WINDOW)
                dsink = dsink + jnp.where(lane8 == 2 * j + g, -jnp.sum(sink_term[rows]), 0.0)
                ds_acc[2 * j + g, :, :WINDOW] += dsp[rows]
                ds_acc[2 * j + g, :, WINDOW:] += dsc[rows]
            dspb, dscb = dsp.astype(MM), dsc.astype(MM)
            dqs.append(_dot(dspb, kp_ref[...]) + _dot(dscb, kc_ref[...]))
            dkp = dkp + _dot_tn(dspb, qg)
            dkc = dkc + _dot_tn(dscb, qg)
            dvp = dvp + _dot_tn(pp.astype(MM), dog)
            dvc = dvc + _dot_tn(pc.astype(MM), dog)
        for j in range(4):
            rows = slice(j * WINDOW, (j + 1) * WINDOW)
            dq_ref[:, j * 128:(j + 1) * 128] = (jnp.where(low, dqs[0][rows], dqs[1][rows]) * Q_SCALE).astype(MM)
        dk_ref[prev, :] += dkp
        dk_ref[cur, :] += dkc
        dv_ref[prev, :] += dvp
        dv_ref[cur, :] += dvc
        dsink_ref[...] += dsink

        @pl.when(n == n_blk - 1)
        def _():
            bk = bk_ref[...]
            rb = lax.broadcasted_iota(jnp.int32, (N_BUCKETS, N_HEADS), 0)
            cb = lax.broadcasted_iota(jnp.int32, (N_BUCKETS, N_HEADS), 1)
            out = jnp.zeros((N_BUCKETS, N_HEADS), F32)
            for s in range(N_HEADS):
                acc = ds_acc[s]
                for b in range(N_BUCKETS):
                    out = out + jnp.where((rb == b) & (cb == s), jnp.sum(jnp.where(bk == b, acc, 0.0)), 0.0)
            drb_ref[...] = out

    do_spec = pl.BlockSpec((WINDOW, D_ATT), lambda n: (n, 1))
    return pl.pallas_call(
        body, name="swa_bwd", grid=(n_blk,),
        in_specs=_swa_specs(S) + [do_spec, _resident(), pl.BlockSpec(memory_space=pltpu.SMEM), _resident(),
                                  _rows(WINDOW, N_HEADS), _rows(WINDOW, N_HEADS)],
        out_specs=[_rows(WINDOW, D_ATT), _const((S, D_KV)), _const((S, D_KV)), _const((N_BUCKETS, N_HEADS)),
                   _const((1, N_HEADS))],
        out_shape=[jax.ShapeDtypeStruct((S, D_ATT), MM), jax.ShapeDtypeStruct((S, D_KV), F32),
                   jax.ShapeDtypeStruct((S, D_KV), F32), jax.ShapeDtypeStruct((N_BUCKETS, N_HEADS), F32),
                   jax.ShapeDtypeStruct((1, N_HEADS), F32)],
        scratch_shapes=[pltpu.VMEM((N_HEADS, WINDOW, 2 * WINDOW), F32)],
        compiler_params=_params("arbitrary"),
    )(sqkv, sqkv, sqkv, sqkv, sqkv, dcat, biasm, sinks_slot, bucket, lse, d_col)


D_Z = 3 * D_ATT + D_ATT + 2 * D_KV


def _pack_dz(dq_fox, dk_fox, dv_fox, dsq, dsk, dsv, tm):
    S = dq_fox.shape[0]

    def body(dq_ref, dk_ref, dv_ref, dsq_ref, dsk_ref, dsv_ref, dz_ref):
        dz_ref[:, 0:512] = dq_ref[...].astype(MM)
        dz_ref[:, 512:1024] = dk_ref[...]
        dz_ref[:, 1024:1536] = dv_ref[...]
        dz_ref[:, 1536:2048] = dsq_ref[...]
        dz_ref[:, 2048:2176] = dsk_ref[...].astype(MM)
        dz_ref[:, 2176:2304] = dsv_ref[...].astype(MM)

    return pl.pallas_call(
        body, name="pack_dz", grid=(S // tm,),
        in_specs=[_rows(tm, D_ATT), _rows(tm, D_ATT), _rows(tm, D_ATT), _rows(tm, D_ATT), _rows(tm, D_KV),
                  _rows(tm, D_KV)],
        out_specs=_rows(tm, D_Z), out_shape=jax.ShapeDtypeStruct((S, D_Z), MM),
        compiler_params=_params("parallel"),
    )(dq_fox, dk_fox, dv_fox, dsq, dsk, dsv)


def _pre_attn_bwd(x, dh1, dz, dff_t, wt, g1, tm):
    S = x.shape[0]

    def body(x_ref, dh1_ref, dz_ref, dff_ref, wt_ref, g1_ref, dx_ref, dg1_ref):
        i = pl.program_id(0)
        da = (_dot(dz_ref[:, 0:WT_FF], wt_ref[0:WT_FF]) + _dot(dz_ref[:, WT_FF:D_Z], wt_ref[WT_SQ:WT_ROWS])
              + _dot_tn(dff_ref[...].astype(MM), wt_ref[WT_FF:WT_SQ]))
        n1, r1 = _rms(x_ref[...])
        dx, dg1 = _rms_bwd(da, n1, r1, g1_ref[...])
        _accumulate(dg1_ref, dg1, i)
        dx_ref[...] = dh1_ref[...] + dx

    return pl.pallas_call(
        body, name="pre_attn_bwd", grid=(S // tm,),
        in_specs=[_rows(tm, D_MODEL), _rows(tm, D_MODEL), _rows(tm, D_Z), pl.BlockSpec((16, tm), lambda i: (0, i)),
                  _resident(), _const((1, D_MODEL))],
        out_specs=[_rows(tm, D_MODEL), _const((1, D_MODEL))],
        out_shape=[jax.ShapeDtypeStruct((S, D_MODEL), F32), jax.ShapeDtypeStruct((1, D_MODEL), F32)],
        compiler_params=_params("arbitrary"),
    )(x, dh1, dz, dff_t, wt, g1)


def _weight_grad(a, b, name, tk, n_chunks=1, relu2=False):
    S, K = a.shape
    N = b.shape[1]
    cn = N // n_chunks

    def body(a_ref, b_ref, out_ref):
        av = a_ref[...]
        if relu2:
            av = jnp.square(jnp.maximum(av.astype(F32), 0.0))
        av = av.astype(MM)
        for j in range(n_chunks):
            val = _dot_tn(av, b_ref[:, j * cn:(j + 1) * cn].astype(MM)).astype(MM)
            if n_chunks > 1:
                out_ref[j] = val
            else:
                out_ref[...] = val

    if n_chunks > 1:
        out_spec = pl.BlockSpec((n_chunks, tk, cn), lambda i: (0, i, 0))
        out_shape = jax.ShapeDtypeStruct((n_chunks, K, cn), MM)
    else:
        out_spec = pl.BlockSpec((tk, N), lambda i: (i, 0))
        out_shape = jax.ShapeDtypeStruct((K, N), MM)
    return pl.pallas_call(
        body, name=name, grid=(K // tk,),
        in_specs=[pl.BlockSpec((S, tk), lambda i: (0, i)), _resident()],
        out_specs=out_spec, out_shape=out_shape, compiler_params=_params("parallel"),
    )(a, b)


def _weight_grad_two(a1, a2, b, name, tk):
    S, K1 = a1.shape
    K2 = a2.shape[1]
    N = b.shape[1]
    n1 = K1 // tk

    def body(a1_ref, a2_ref, b_ref, out_ref):
        av = jnp.where(pl.program_id(0) < n1, a1_ref[...], a2_ref[...])
        out_ref[...] = _dot_tn(av, b_ref[...]).astype(MM)

    return pl.pallas_call(
        body, name=name, grid=((K1 + K2) // tk,),
        in_specs=[pl.BlockSpec((S, tk), lambda i: (0, jnp.minimum(i, n1 - 1))),
                  pl.BlockSpec((S, tk), lambda i: (0, jnp.maximum(i - n1, 0))), _resident()],
        out_specs=pl.BlockSpec((tk, N), lambda i: (i, 0)), out_shape=jax.ShapeDtypeStruct((K1 + K2, N), MM),
        compiler_params=_params("parallel"),
    )(a1, a2, b)


def _place():
    return lax.axis_index("x"), lax.axis_index("y"), lax.axis_index("c")


def _all_gather_sequencer(stacks, name, collective_id):
    refs = [jax.new_ref(s, memory_space=pltpu.MemorySpace.HBM) for s in stacks]
    n = len(refs)

    @pl.kernel(mesh=plsc.ScalarSubcoreMesh(axis_name="sequencer", num_cores=1), name=name,
               scratch_types=(pltpu.SemaphoreType.DMA((7 * n,)), pltpu.SemaphoreType.DMA((7 * n,))),
               compiler_params=pltpu.CompilerParams(collective_id=collective_id))
    def launch(send_sems, recv_sems):
        x, y, c = _place()
        sibling = (x, y, 1 - c)
        chips = [(1 - x, y), (x, 1 - y), (1 - x, 1 - y)]
        peers = [sibling] + [(px, py, c) for px, py in chips]
        barrier = pltpu.get_barrier_semaphore()
        for peer in peers:
            pl.semaphore_signal(barrier, inc=1, device_id=peer, device_id_type=MESH)
        pl.semaphore_wait(barrier, len(peers))

        def copy(a, k, block, to):
            px, py, pc = block
            slot = refs[a].at[4 * px + 2 * py + pc]
            return _remote(slot, slot, send_sems, recv_sems, 7 * a + k, to)

        first = [copy(a, k, (x, y, c), peer) for a in range(n) for k, peer in enumerate(peers)]
        for cp in first:
            cp.start()
        passed = []
        for j, (px, py) in enumerate(chips):
            for a in range(n):
                copy(a, 1 + j, (px, py, c), sibling).wait_recv()
                passed.append(copy(a, 4 + j, (px, py, c), sibling))
                passed[-1].start()
        for a in range(n):
            copy(a, 0, (x, y, 1 - c), sibling).wait_recv()
            for j, (px, py) in enumerate(chips):
                copy(a, 4 + j, (px, py, 1 - c), sibling).wait_recv()
        for cp in first + passed:
            cp.wait_send()

    launch()
    return [ref[...] for ref in refs]


def _chip_sums(grads, others, name):
    n = len(grads)

    def body(c_ref, *refs):
        for g_ref, o_ref, out_ref in zip(refs[:n], refs[n:2 * n], refs[2 * n:]):
            out_ref[...] = (g_ref[...].astype(F32) + o_ref[...].astype(F32)).astype(out_ref.dtype)

    own = [pl.BlockSpec((None, None) + g.shape[2:], lambda k, c_ref: (k, c_ref[0], 0, 0)) for g in grads]
    chip = [pl.BlockSpec((None,) + g.shape[2:], lambda k, c_ref: (k, 0, 0)) for g in grads]
    return pl.pallas_call(
        body, name=name,
        grid_spec=pltpu.PrefetchScalarGridSpec(num_scalar_prefetch=1, grid=(4,), in_specs=own + chip, out_specs=chip),
        out_shape=[jax.ShapeDtypeStruct((4,) + g.shape[2:], MM) for g in grads],
        compiler_params=_params("parallel"),
    )(lax.axis_index("c").astype(jnp.int32).reshape(1), *grads, *others)


HBM_SPEC = pl.BlockSpec(memory_space=pltpu.HBM)
SEM_SPEC = pl.BlockSpec(memory_space=pltpu.SEMAPHORE)
DATAFLOW = pltpu.SideEffectType.DATAFLOW_SIDE_EFFECTING


def _exchange_start(name, arrays, n_copies, plan):
    n = len(arrays)

    def body(*refs):
        send_sems, recv_sems, token = refs[n], refs[n + 1], refs[2 * n + 2]
        for cp in plan(refs[:n], send_sems, recv_sems):
            cp.start()
        token[...] = jnp.zeros_like(token)

    out = pl.pallas_call(
        body, name=name,
        out_shape=(pltpu.SemaphoreType.DMA((n_copies,)), pltpu.SemaphoreType.DMA((n_copies,)),
                   *[pltpu.HBM(a.shape, a.dtype) for a in arrays], jax.ShapeDtypeStruct((1, D_MODEL), F32)),
        in_specs=[HBM_SPEC] * n,
        out_specs=(SEM_SPEC, SEM_SPEC, *[HBM_SPEC] * n, pl.BlockSpec(memory_space=pltpu.VMEM)),
        input_output_aliases={i: 2 + i for i in range(n)},
        compiler_params=pltpu.CompilerParams(has_side_effects=DATAFLOW),
    )(*[pltpu.with_memory_space_constraint(a, pltpu.HBM) for a in arrays])
    return (out[0], out[1]), list(out[2:2 + n]), out[2 + n]


def _exchange_wait(name, arrays, sems, after, plan):
    n = len(arrays)
    after = list(after) if isinstance(after, (list, tuple)) else [after]

    def body(*refs):
        send_sems, recv_sems = refs[n], refs[n + 1]
        for cp in plan(refs[:n], send_sems, recv_sems):
            cp.wait_send()
            cp.wait_recv()

    out = pl.pallas_call(
        body, name=name, out_shape=[pltpu.HBM(a.shape, a.dtype) for a in arrays],
        in_specs=[HBM_SPEC] * n + [SEM_SPEC, SEM_SPEC] + [pl.BlockSpec(memory_space=pl.ANY)] * len(after),
        out_specs=[HBM_SPEC] * n, input_output_aliases={i: i for i in range(n)},
        compiler_params=pltpu.CompilerParams(has_side_effects=DATAFLOW),
    )(*arrays, sems[0], sems[1], *after)
    return list(out)


def _remote(src, dst, send_sems, recv_sems, k, to):
    return pltpu.make_async_remote_copy(src_ref=src, dst_ref=dst, send_sem=send_sems.at[k], recv_sem=recv_sems.at[k],
                                        device_id=to, device_id_type=MESH)


def _plan_gather_direct(refs, send_sems, recv_sems):
    x, y, c = _place()
    me = 4 * x + 2 * y + c
    peers = [(x, y, 1 - c), (1 - x, y, c), (x, 1 - y, c), (1 - x, 1 - y, c)]
    return [_remote(ref.at[me], ref.at[me], send_sems, recv_sems, 4 * a + k, peer)
            for a, ref in enumerate(refs) for k, peer in enumerate(peers)]


def _plan_gather_pass_on(refs, send_sems, recv_sems):
    x, y, c = _place()
    chips = [(1 - x, y), (x, 1 - y), (1 - x, 1 - y)]
    return [_remote(ref.at[4 * px + 2 * py + c], ref.at[4 * px + 2 * py + c], send_sems, recv_sems, 3 * a + k,
                    (x, y, 1 - c))
            for a, ref in enumerate(refs) for k, (px, py) in enumerate(chips)]


def _plan_in_chip(refs, send_sems, recv_sems):
    n = len(refs) // 2
    x, y, c = _place()
    return [_remote(refs[a].at[:, 1 - c], refs[n + a], send_sems, recv_sems, a, (x, y, 1 - c)) for a in range(n)]


def _plan_between_chips(refs, send_sems, recv_sems):
    n = len(refs) // 2
    x, y, c = _place()
    chips = [(1 - x, y), (x, 1 - y), (1 - x, 1 - y)]
    return [_remote(refs[a].at[2 * px + py], refs[n + a].at[2 * x + y], send_sems, recv_sems, 3 * a + k, (px, py, c))
            for a in range(n) for k, (px, py) in enumerate(chips)]


def _plan_late_between(refs, send_sems, recv_sems):
    sums, land, small = refs
    x, y, c = _place()
    me = 4 * x + 2 * y + c
    copies = _plan_between_chips([sums, land], send_sems, recv_sems)
    peers = [(x ^ dx, y ^ dy, c ^ dc) for dx in range(2) for dy in range(2) for dc in range(2) if dx + dy + dc]
    return copies + [_remote(small.at[me], small.at[me], send_sems, recv_sems, 3 + k, peer)
                     for k, peer in enumerate(peers)]


def _adamw_math(w, g, m, v):
    m = ADAM_B1 * m + (1.0 - ADAM_B1) * g
    v = ADAM_B2 * v + (1.0 - ADAM_B2) * jnp.square(g)
    m_hat = m / (1.0 - ADAM_B1 ** ADAM_STEP)
    v_hat = v / (1.0 - ADAM_B2 ** ADAM_STEP)
    delta = -ADAM_LR * (m_hat / (jnp.sqrt(v_hat) + ADAM_EPS) + ADAM_WD * w)
    return delta, m, v


def _adamw(parts, w, m, v, name):
    n_parts, r, cdim = parts.shape
    tr = 256 if r % 256 == 0 else r

    def body(p_ref, w_ref, m_ref, v_ref, g_out, d_out, m_out, v_out):
        g = p_ref[0].astype(F32)
        for k in range(1, n_parts):
            g = g + p_ref[k].astype(F32)
        delta, m_new, v_new = _adamw_math(w_ref[...], g, m_ref[...], v_ref[...])
        g_out[...] = g
        d_out[...] = delta
        m_out[...] = m_new
        v_out[...] = v_new

    blk = pl.BlockSpec((tr, cdim), lambda i: (i, 0))
    return pl.pallas_call(
        body, name=name, grid=(r // tr,),
        in_specs=[pl.BlockSpec((n_parts, tr, cdim), lambda i: (0, i, 0)), blk, blk, blk],
        out_specs=[blk] * 4, out_shape=[jax.ShapeDtypeStruct((r, cdim), F32)] * 4,
        compiler_params=_params("parallel"),
    )(parts, w, m, v)


def _adamw_chips(parts, sums, w, m, v, name):
    _, r, cdim = parts.shape
    tr = 256 if r % 256 == 0 else r

    def body(chip_ref, p_ref, own_ref, w_ref, m_ref, v_ref, g_out, d_out, m_out, v_out):
        g = None
        for k in range(4):
            term = jnp.where(chip_ref[0] == k, own_ref[...], p_ref[k]).astype(F32)
            g = term if g is None else g + term
        delta, m_new, v_new = _adamw_math(w_ref[...], g, m_ref[...], v_ref[...])
        g_out[...] = g
        d_out[...] = delta
        m_out[...] = m_new
        v_out[...] = v_new

    blk = pl.BlockSpec((tr, cdim), lambda i, chip: (i, 0))
    my_chip = (2 * lax.axis_index("x") + lax.axis_index("y")).astype(jnp.int32).reshape(1)
    return pl.pallas_call(
        body, name=name,
        grid_spec=pltpu.PrefetchScalarGridSpec(
            num_scalar_prefetch=1, grid=(r // tr,),
            in_specs=[pl.BlockSpec((4, tr, cdim), lambda i, chip: (0, i, 0)),
                      pl.BlockSpec((None, tr, cdim), lambda i, chip: (chip[0], i, 0)), blk, blk, blk],
            out_specs=[blk] * 4),
        out_shape=[jax.ShapeDtypeStruct((r, cdim), F32)] * 4,
        compiler_params=_params("parallel"),
    )(my_chip, parts, sums, w, m, v)


class _NoExchange:
    def __init__(self, weights):
        self.weights = weights

    def before_pre_attn(self, g1):
        return g1

    def after_fox_fwd(self, fox_o, sinks_slot):
        return sinks_slot

    def after_attention(self, swa_o):
        return self.weights

    def after_early_grads(self, grads, d_col):
        return d_col

    def after_swa_bwd(self, dsq, d_row3):
        return d_row3

    def after_w_in_grad(self, d_win, g1):
        return g1


def _slot_order(t, axis):
    shp = t.shape
    t = t.reshape(shp[:axis] + (2, 4, shp[axis] // N_HEADS) + shp[axis + 1:])
    return jnp.swapaxes(t, axis, axis + 1).reshape(shp)


def _head_order(t, axis):
    shp = t.shape
    t = t.reshape(shp[:axis] + (4, 2, shp[axis] // N_HEADS) + shp[axis + 1:])
    return jnp.swapaxes(t, axis, axis + 1).reshape(shp)


def _forward_backward(x, p, target, win_t, hooks, b_forget, rel_bias, sinks, g1, g2, g3, g4, g5):
    S = x.shape[0]
    tm = 512
    tm_mlp = 512
    t = 256
    q0 = 3 * D_ATT + N_HEADS
    win_t = win_t.reshape(D_IN, D_MODEL)
    wt = jnp.concatenate(
        [win_t[:q0], jnp.zeros((8, D_MODEL), MM), _slot_order(win_t[q0:q0 + D_ATT], 0), win_t[q0 + D_ATT:]], axis=0)
    bcol = jnp.pad(b_forget.reshape(N_HEADS, 1), ((0, 8), (0, 0)))
    rel_bias_slot = rel_bias[:, np.array(SLOT_HEAD)]
    sinks_slot = sinks.reshape(N_HEADS)[np.array(SLOT_HEAD)]
    bucket = jnp.asarray(_swa_bucket_map())

    a, fqkv, sqkv, fft = _pre_attn(x, hooks.before_pre_attn(g1), wt, tm)
    c_row = _forget_cumsum(fft, bcol)
    c_col = c_row[:N_HEADS].T
    c_row3 = c_row[:N_HEADS].reshape(N_HEADS, S // t, t)
    fox_o, fox_lse = _fox_fwd(fqkv, c_row3, tq=512, tk=t)
    biasm = _swa_bias(rel_bias_slot, bucket)
    sinks_slot = hooks.after_fox_fwd(fox_o, sinks_slot)
    swa_o, swa_lse = _swa_fwd(sqkv, biasm, sinks_slot)
    wout, w1, w2, wple, wg = hooks.after_attention(swa_o)
    wout_fox = wout[:D_ATT]
    wout_swa = _slot_order(wout[D_ATT:], 0)
    mix, h1, m = _post_attn(x, fox_o, swa_o, wout_fox, wout_swa, g2, g3, tm)
    u, y, h2 = _mlp_fwd(m, h1, w1, w2, g4, tm_mlp)
    dh2, dpe, dgl, dg5, loss = _ple_loss(h2, p, target, wg, wple, g5, tm)

    d_wple = _weight_grad(p, dpe, "grad_w_ple", tk=D_PLE, n_chunks=N_DEV)
    d_wg = _weight_grad(h2, dgl, "grad_w_ple_gate", tk=256)
    dh1, dy, du, dg4, dg3 = _mlp_bwd(dh2, y, h1, u, w1, w2, g4, g3, tm)
    d_w2 = _weight_grad(u, dy, "grad_w_ff2", tk=256, relu2=True)
    d_w1 = _weight_grad(m, du, "grad_w_ff1", tk=256, n_chunks=N_DEV)
    head = np.arange(D_ATT) // HEAD_DIM
    head_rows = jnp.asarray((head[None, :] == np.arange(N_HEADS)[:, None]).astype(np.float32))
    dmix, dcat, d_row, d_swa, dg2 = _attn_out_bwd(dh1, mix, fox_o, swa_o, wout_fox, wout_swa, g2, head_rows, tm)
    d_col = d_swa.T
    d_wout = _weight_grad_two(fox_o, swa_o, dmix, "grad_w_out", tk=256)
    d_wout = jnp.concatenate([d_wout[:D_ATT], _head_order(d_wout[D_ATT:], 0)], axis=0)
    d_wout = d_wout.reshape(N_DEV, D_MODEL // N_DEV, D_MODEL)
    early = dict(w_ff1=d_w1, w_ff2=d_w2.reshape(N_DEV, FF_CHUNK, D_MODEL), w_ple=d_wple,
                 w_ple_gate=d_wg.reshape(N_DEV, D_MODEL // N_DEV, D_MODEL), w_out=d_wout)

    d_col = hooks.after_early_grads(early, d_col)
    dsq, dsk, dsv, d_rb_slot, d_sink_slot = _swa_bwd(sqkv, dcat, biasm, sinks_slot, bucket, swa_lse, d_col)
    lse_row3 = fox_lse.T.reshape(N_HEADS, S // t, t)
    d_row3 = hooks.after_swa_bwd(dsq, d_row.reshape(N_HEADS, S // t, t))
    dq_fox, dk_fox, dv_fox, dc_col, dcq = _fox_bwd(fqkv, dcat, lse_row3, d_row3, c_col, tq=t, tk=512)
    dc_row = jnp.pad(dc_col.T + dcq.reshape(N_HEADS, S), ((0, 8), (0, 0)))
    dff_t, db, d_wff_t = _forget_bwd(dc_row, fft, bcol, a)
    dz = _pack_dz(dq_fox, dk_fox, dv_fox, dsq, dsk, dsv, 512)
    d_wmain = _weight_grad(dz, a, "grad_w_in", tk=256)

    sq0 = 3 * D_ATT
    d_win = jnp.concatenate(
        [d_wmain[:sq0], d_wff_t[:N_HEADS].astype(MM), _head_order(d_wmain[sq0:sq0 + D_ATT], 0),
         d_wmain[sq0 + D_ATT:]], axis=0)
    d_win = d_win.reshape(N_DEV, D_IN // N_DEV, D_MODEL)
    grad_x, dg1 = _pre_attn_bwd(x, dh1, dz, dff_t, wt, hooks.after_w_in_grad(d_win, g1), tm)
    big = dict(early, w_in=d_win)
    small = dict(b_forget=db[:N_HEADS].reshape(1, N_HEADS), rel_bias=d_rb_slot[:, np.array(HEAD_SLOT)],
                 swa_sinks=d_sink_slot[:, np.array(HEAD_SLOT)], g_attn_pre=dg1, g_attn_post=dg2, g_ff_pre=dg3,
                 g_ff_post=dg4, g_ple_post=dg5)
    return loss, grad_x, big, small


BIG = ("w_in", "w_out", "w_ff1", "w_ff2", "w_ple", "w_ple_gate")
SMALL_ROWS = ("g_attn_pre", "g_attn_post", "g_ff_pre", "g_ff_post", "g_ple_post")
WEIGHTS =("w_in", "b_forget", "w_out", "rel_bias", "swa_sinks", "g_attn_pre", "g_attn_post", "w_ff1", "w_ff2",
           "g_ff_pre", "g_ff_post", "w_ple", "w_ple_gate", "g_ple_post")


EARLY = ("w_ff1", "w_ff2", "w_ple", "w_ple_gate", "w_out")


class _Overlap:
    def __init__(self, later):
        self.later = later

    def before_pre_attn(self, g1):
        self.gather_sems, self.later, token = _exchange_start("gather_rest_start", self.later, 4 * 5, _plan_gather_direct)
        return g1 + token

    def after_fox_fwd(self, fox_o, sinks_slot):
        later = _exchange_wait("gather_rest_wait", self.later, self.gather_sems, fox_o, _plan_gather_direct)
        self.pass_sems, self.later, token = _exchange_start("gather_pass_on_start", later, 3 * 5, _plan_gather_pass_on)
        return sinks_slot + token[0, :N_HEADS]

    def after_attention(self, swa_o):
        wout_g, w1_g, w2_g, wple_g, wg_g = _exchange_wait("gather_pass_on_wait", self.later, self.pass_sems, swa_o,
                                                         _plan_gather_pass_on)
        return (wout_g.reshape(D_MODEL, D_MODEL), w1_g, w2_g.reshape(D_FF, D_MODEL),
                jnp.moveaxis(wple_g, 0, 1).reshape(D_PLE, D_MODEL), wg_g.reshape(D_MODEL, D_MODEL))

    def after_early_grads(self, grads, d_col):
        views = [grads[k].reshape((4, 2) + grads[k].shape[1:]) for k in EARLY]
        lands = [lax.empty((4,) + grads[k].shape[1:], MM) for k in EARLY]
        self.in_chip_sems, self.in_chip, token = _exchange_start("grads_in_chip_start", views + lands, len(EARLY),
                                                                 _plan_in_chip)
        return d_col + token[0, 0]

    def after_swa_bwd(self, dsq, d_row3):
        arrays = _exchange_wait("grads_in_chip_wait", self.in_chip, self.in_chip_sems, dsq, _plan_in_chip)
        n = len(EARLY)
        sums = list(_chip_sums(arrays[:n], arrays[n:], "chip_sums_early"))
        lands = [lax.empty(s.shape, s.dtype) for s in sums]
        self.between_sems, self.between, token = _exchange_start("grads_between_chips_start", sums + lands, 3 * n,
                                                                 _plan_between_chips)
        return d_row3 + token[0, 0]

    def after_w_in_grad(self, d_win, g1):
        self.late_in_chip_sems, self.late_in_chip, token = _exchange_start(
            "late_in_chip_start", [d_win.reshape((4, 2) + d_win.shape[1:]), lax.empty((4,) + d_win.shape[1:], MM)],
            1, _plan_in_chip)
        return g1 + token

    def finish(self, after):
        arrays = _exchange_wait("grads_between_chips_wait", self.between, self.between_sems, after,
                                _plan_between_chips)
        n = len(EARLY)
        self.sums = arrays[:n]
        return arrays[n:]


def _pack_small(t):
    rows = [t[k].reshape(1, D_MODEL) for k in SMALL_ROWS]
    misc = jnp.concatenate([t["b_forget"].reshape(-1), t["swa_sinks"].reshape(-1), t["rel_bias"].reshape(-1)])
    rows.append(jnp.pad(misc, (0, D_MODEL - misc.shape[0])).reshape(1, D_MODEL))
    rows.append(jnp.pad(t["loss"].reshape(-1), (0, D_MODEL - 1)).reshape(1, D_MODEL))
    rows.append(jnp.zeros((1, D_MODEL), F32))
    return jnp.concatenate(rows, axis=0).astype(F32)


def _unpack_small(blk):
    out = {k: blk[i].reshape(1, D_MODEL) for i, k in enumerate(SMALL_ROWS)}
    misc = blk[len(SMALL_ROWS)]
    out["b_forget"] = misc[:N_HEADS].reshape(1, N_HEADS)
    out["swa_sinks"] = misc[N_HEADS:2 * N_HEADS].reshape(1, N_HEADS)
    out["rel_bias"] = misc[2 * N_HEADS:2 * N_HEADS + N_BUCKETS * N_HEADS].reshape(N_BUCKETS, N_HEADS)
    out["loss"] = blk[len(SMALL_ROWS) + 1, 0]
    return out


def kernel(x, p, w_in, b_forget, w_out, rel_bias, swa_sinks, g_attn_pre, g_attn_post, w_ff1, w_ff2, g_ff_pre, g_ff_post, w_ple, w_ple_gate, g_ple_post, loss_target, m_w_in, m_b_forget, m_w_out, m_rel_bias, m_swa_sinks, m_g_attn_pre, m_g_attn_post, m_w_ff1, m_w_ff2, m_g_ff_pre, m_g_ff_post, m_w_ple, m_w_ple_gate, m_g_ple_post, v_w_in, v_b_forget, v_w_out, v_rel_bias, v_swa_sinks, v_g_attn_pre, v_g_attn_post, v_w_ff1, v_w_ff2, v_g_ff_pre, v_g_ff_post, v_w_ple, v_w_ple_gate, v_g_ple_post):
    w = dict(w_in=w_in, b_forget=b_forget, w_out=w_out, rel_bias=rel_bias, swa_sinks=swa_sinks,
             g_attn_pre=g_attn_pre, g_attn_post=g_attn_post, w_ff1=w_ff1, w_ff2=w_ff2, g_ff_pre=g_ff_pre,
             g_ff_post=g_ff_post, w_ple=w_ple, w_ple_gate=w_ple_gate, g_ple_post=g_ple_post)
    mom = dict(w_in=m_w_in, b_forget=m_b_forget, w_out=m_w_out, rel_bias=m_rel_bias, swa_sinks=m_swa_sinks,
               g_attn_pre=m_g_attn_pre, g_attn_post=m_g_attn_post, w_ff1=m_w_ff1, w_ff2=m_w_ff2,
               g_ff_pre=m_g_ff_pre, g_ff_post=m_g_ff_post, w_ple=m_w_ple, w_ple_gate=m_w_ple_gate,
               g_ple_post=m_g_ple_post)
    var = dict(w_in=v_w_in, b_forget=v_b_forget, w_out=v_w_out, rel_bias=v_rel_bias, swa_sinks=v_swa_sinks,
               g_attn_pre=v_g_attn_pre, g_attn_post=v_g_attn_post, w_ff1=v_w_ff1, w_ff2=v_w_ff2,
               g_ff_pre=v_g_ff_pre, g_ff_post=v_g_ff_post, w_ple=v_w_ple, w_ple_gate=v_w_ple_gate,
               g_ple_post=v_g_ple_post)

    turn = lambda t, k: t.T if k == "w_in" else t
    me = 4 * lax.axis_index("x") + 2 * lax.axis_index("y") + lax.axis_index("c")

    def stack(block):
        return lax.dynamic_update_slice_in_dim(lax.empty((N_DEV,) + block.shape, block.dtype), block[None], me, 0)

    stacks = [stack(turn(w[k][0], k).astype(MM)) for k in BIG]
    (win_g,), later = _all_gather_sequencer(stacks[:1], "all_gather_sequencer", 1), stacks[1:]
    hooks = _Overlap(later)
    loss, grad_x, big, small = _forward_backward(
        x[0], p[0, 0], loss_target[0], win_g, hooks, b_forget, rel_bias, swa_sinks,
        g_attn_pre, g_attn_post, g_ff_pre, g_ff_post, g_ple_post)
    out_g, out_d, out_m, out_v = {}, {}, {}, {}

    def update(k, part, own):
        g, d, m_new, v_new = _adamw_chips(part, own, turn(w[k][0], k), turn(mom[k][0], k), turn(var[k][0], k),
                                          "adamw_" + k)
        out_g[k], out_d[k], out_m[k], out_v[k] = turn(g, k)[None], turn(d, k)[None], turn(m_new, k)[None], turn(v_new, k)[None]
        return d

    view, other = _exchange_wait("late_in_chip_wait", hooks.late_in_chip, hooks.late_in_chip_sems, grad_x,
                                 _plan_in_chip)
    (chip_sum,) = _chip_sums([view], [other], "chip_sum_w_in")
    small["loss"] = loss
    between_sems, between, token = _exchange_start(
        "late_between_chips_start", [chip_sum, lax.empty(chip_sum.shape, MM), stack(_pack_small(small))], 3 + 7,
        _plan_late_between)
    early_parts = hooks.finish(token)
    done = [update(k, part, own) for k, part, own in zip(EARLY, early_parts, hooks.sums)]
    chip_sum, part, small_all = _exchange_wait("late_between_chips_wait", between, between_sems, done,
                                               _plan_late_between)
    update("w_in", part, chip_sum)
    rep = {k: w[k] for k in w if k not in BIG}
    rep["loss"] = jnp.zeros((), F32)
    rep_m = {k: mom[k] for k in mom if k not in BIG}
    rep_m["loss"] = jnp.zeros((), F32)
    rep_v = {k: var[k] for k in var if k not in BIG}
    rep_v["loss"] = jnp.ones((), F32)
    g_s, d_s, m_s, v_s = _adamw(small_all, _pack_small(rep), _pack_small(rep_m), _pack_small(rep_v), "adamw_small")
    g_s, d_s, m_s, v_s = _unpack_small(g_s), _unpack_small(d_s), _unpack_small(m_s), _unpack_small(v_s)
    for k in w:
        if k not in BIG:
            out_g[k], out_d[k], out_m[k], out_v[k] = g_s[k], d_s[k], m_s[k], v_s[k]
    return (g_s["loss"], grad_x[None], *[out_g[k] for k in WEIGHTS], *[out_d[k] for k in WEIGHTS],
            *[out_m[k] for k in WEIGHTS], *[out_v[k] for k in WEIGHTS])
```

```python
import functools

import numpy as np
import jax
import jax.numpy as jnp
from jax import lax
from jax.experimental import pallas as pl
from jax.experimental.pallas import tpu as pltpu
from jax.experimental.pallas import tpu_sc as plsc

F32 = jnp.float32
MM = jnp.bfloat16

D_MODEL = 1024
HEAD_DIM = 64
N_HEADS = 8
D_ATT = N_HEADS * HEAD_DIM
D_KV = 128
D_FF = 4096
D_PLE = 256
D_IN = 3 * D_ATT + N_HEADS + D_ATT + 2 * D_KV
N_DEV = 8
FF_CHUNK = D_FF // N_DEV
WINDOW = 128
N_BUCKETS = 32
MAX_DISTANCE = 128
RMS_EPS = 1e-6
Q_SCALE = HEAD_DIM ** -0.5
NEG = -1e30

ADAM_LR = 0.001
ADAM_B1 = 0.9
ADAM_B2 = 0.999
ADAM_EPS = 1e-08
ADAM_WD = 0.01
ADAM_STEP = 10

SLOT_HEAD = (0, 4, 1, 5, 2, 6, 3, 7)
HEAD_SLOT = (0, 2, 4, 6, 1, 3, 5, 7)

VMEM_LIMIT = 60 * 1024 * 1024
MESH = pl.DeviceIdType.MESH

NT = (((1,), (1,)), ((), ()))
TN = (((0,), (0,)), ((), ()))


def _params(*semantics):
    return pltpu.CompilerParams(dimension_semantics=semantics, vmem_limit_bytes=VMEM_LIMIT)


def _resident():
    return pl.BlockSpec(memory_space=pltpu.VMEM)


def _rows(tm, width):
    return pl.BlockSpec((tm, width), lambda i: (i, 0))


def _const(shape):
    return pl.BlockSpec(shape, lambda i: (0,) * len(shape))


def _dot(a, b):
    return jnp.dot(a, b, preferred_element_type=F32)


def _dot_nt(a, b):
    return lax.dot_general(a, b, NT, preferred_element_type=F32)


def _dot_tn(a, b):
    return lax.dot_general(a, b, TN, preferred_element_type=F32)


def _rms(xf):
    r = lax.rsqrt(jnp.mean(xf * xf, axis=-1, keepdims=True) + RMS_EPS)
    return xf * r, r


def _rms_bwd(dout, n, r, g):
    dg = jnp.sum(dout * n, axis=0, keepdims=True)
    dn = dout * g
    dx = r * (dn - n * jnp.mean(dn * n, axis=-1, keepdims=True))
    return dx, dg


def _run_after(after, body, in_specs, operands):
    if after is None:
        return body, list(in_specs), tuple(operands)
    n = len(operands)
    return ((lambda *refs: body(*refs[:n], *refs[n + 1:])), list(in_specs) + [pl.BlockSpec(memory_space=pl.ANY)],
            tuple(operands) + (after,))


def _accumulate(ref, value, step):
    @pl.when(step == 0)
    def _():
        ref[...] = value

    @pl.when(step != 0)
    def _():
        ref[...] += value


def _t5_bucket(n):
    max_exact = N_BUCKETS // 2
    large = max_exact + (np.log(np.maximum(n, 1) / max_exact) / np.log(MAX_DISTANCE / max_exact)
                         * (N_BUCKETS - max_exact)).astype(np.int32)
    large = np.minimum(large, N_BUCKETS - 1)
    return np.where(n < max_exact, n, large).astype(np.int32)


def _swa_bucket_map():
    i = np.arange(WINDOW)[:, None]
    j = np.arange(2 * WINDOW)[None, :]
    dist = i + WINDOW - j
    ok = (dist >= 0) & (dist < WINDOW)
    return np.where(ok, _t5_bucket(np.clip(dist, 0, None)), -1).astype(np.int32)


WT_FOX = 0
WT_FF = 3 * D_ATT
WT_SQ = WT_FF + 16
WT_SKV = WT_SQ + D_ATT
WT_ROWS = WT_SKV + 2 * D_KV


def _pre_attn(x, g1, wt, tm, after=None):
    S = x.shape[0]

    def body(x_ref, g_ref, wt_ref, a_ref, fqkv_ref, sqkv_ref, fft_ref):
        n, _ = _rms(x_ref[...])
        a = (n * g_ref[...]).astype(MM)
        a_ref[...] = a
        fqkv_ref[:, :D_ATT] = (_dot_nt(a, wt_ref[WT_FOX:WT_FOX + D_ATT]) * Q_SCALE).astype(MM)
        fqkv_ref[:, D_ATT:] = _dot_nt(a, wt_ref[WT_FOX + D_ATT:WT_FF]).astype(MM)
        sqkv_ref[:, :D_ATT] = (_dot_nt(a, wt_ref[WT_SQ:WT_SKV]) * Q_SCALE).astype(MM)
        sqkv_ref[:, D_ATT:] = _dot_nt(a, wt_ref[WT_SKV:WT_ROWS]).astype(MM)
        fft_ref[...] = _dot_nt(wt_ref[WT_FF:WT_SQ], a)

    body, in_specs, operands = _run_after(after, body, [_rows(tm, D_MODEL), _const((1, D_MODEL)), _resident()],
                                          (x, g1, wt))
    return pl.pallas_call(
        body, name="pre_attn", grid=(S // tm,), in_specs=in_specs,
        out_specs=[_rows(tm, D_MODEL), _rows(tm, 3 * D_ATT), _rows(tm, D_ATT + 2 * D_KV),
                   pl.BlockSpec((16, tm), lambda i: (0, i))],
        out_shape=[jax.ShapeDtypeStruct((S, D_MODEL), MM), jax.ShapeDtypeStruct((S, 3 * D_ATT), MM),
                   jax.ShapeDtypeStruct((S, D_ATT + 2 * D_KV), MM), jax.ShapeDtypeStruct((16, S), F32)],
        compiler_params=_params("parallel"),
    )(*operands)


def _lane_scan(v, reverse):
    S = v.shape[1]
    lane = lax.broadcasted_iota(jnp.int32, v.shape, 1)
    k = 1
    while k < S:
        if reverse:
            v = v + jnp.where(lane < S - k, pltpu.roll(v, S - k, axis=1), 0.0)
        else:
            v = v + jnp.where(lane >= k, pltpu.roll(v, k, axis=1), 0.0)
        k *= 2
    return v


def _forget_cumsum(fft, bcol):
    def body(f_ref, b_ref, c_ref):
        z = f_ref[...] + b_ref[...]
        log_f = jnp.minimum(z, 0.0) - jnp.log1p(jnp.exp(-jnp.abs(z)))
        c_ref[...] = _lane_scan(log_f, reverse=False)

    return pl.pallas_call(
        body, name="forget_cumsum", out_shape=jax.ShapeDtypeStruct(fft.shape, F32),
        in_specs=[_resident(), _resident()], out_specs=_resident(),
    )(fft, bcol)


def _forget_bwd(dc_row, fft, bcol, a):
    def body(dc_ref, f_ref, b_ref, a_ref, dff_ref, db_ref, dw_ref):
        z = f_ref[...] + b_ref[...]
        dlog_f = _lane_scan(dc_ref[...], reverse=True)
        dff = dlog_f * (1.0 / (1.0 + jnp.exp(z)))
        dff_ref[...] = dff
        db_ref[...] = jnp.sum(dff, axis=1, keepdims=True)
        dw_ref[...] = _dot(dff.astype(MM), a_ref[...])

    return pl.pallas_call(
        body, name="forget_bwd",
        out_shape=[jax.ShapeDtypeStruct(fft.shape, F32), jax.ShapeDtypeStruct((fft.shape[0], 1), F32),
                   jax.ShapeDtypeStruct((fft.shape[0], D_MODEL), F32)],
        in_specs=[_resident()] * 4, out_specs=[_resident()] * 3,
    )(dc_row, fft, bcol, a)


def _head_select(shape, upper):
    lane = lax.broadcasted_iota(jnp.int32, shape, 1)
    return lane >= HEAD_DIM if upper else lane < HEAD_DIM


def _fox_fwd(fqkv, c_row3, tq, tk, pairs_per_loop=2, row_chunks=1):
    S = fqkv.shape[0]
    rq = tq // row_chunks
    n_band = tq // tk

    def body(q_ref, k_ref, v_ref, ck_ref, o_ref, lse_ref):
        qi = pl.program_id(0)
        row = lax.broadcasted_iota(jnp.int32, (rq, tk), 0)
        col = lax.broadcasted_iota(jnp.int32, (rq, tk), 1)
        low = _head_select((rq, 128), 0)
        for first in range(0, N_HEADS // 2, pairs_per_loop):
            pairs = range(first, first + pairs_per_loop)
            chains = [(pr, hh, rc) for pr in pairs for hh in range(2) for rc in range(row_chunks)]
            qh = {}
            for pr in pairs:
                for rc in range(row_chunks):
                    q2 = q_ref[rc * rq:(rc + 1) * rq, pr * 128:(pr + 1) * 128]
                    qh[pr, 0, rc] = jnp.where(low, q2, jnp.zeros_like(q2))
                    qh[pr, 1, rc] = jnp.where(low, jnp.zeros_like(q2), q2)

            def block(kb, carry, band, chains=chains, qh=qh):
                rows = pl.ds(pl.multiple_of(kb * tk, tk), tk)
                out = []
                for (pr, hh, rc), (m, l, acc) in zip(chains, carry):
                    if band is not None and (rc + 1) * rq <= band * tk:
                        out.append((m, l, acc))
                        continue
                    lanes = slice(pr * 128, (pr + 1) * 128)
                    s = _dot_nt(qh[pr, hh, rc], k_ref[rows, lanes]) - ck_ref[2 * pr + hh, pl.ds(kb, 1), :]
                    if band is not None:
                        s = jnp.where(row + rc * rq >= col + band * tk, s, NEG)
                    m_new = jnp.maximum(m, jnp.max(s, axis=-1, keepdims=True))
                    p = jnp.exp(s - m_new)
                    alpha = jnp.exp(m - m_new)
                    l = alpha * l + jnp.sum(p, axis=-1, keepdims=True)
                    acc = alpha * acc + _dot(p.astype(MM), v_ref[rows, lanes])
                    out.append((m_new, l, acc))
                return tuple(out)

            carry = tuple((jnp.full((rq, 1), NEG, F32), jnp.zeros((rq, 1), F32), jnp.zeros((rq, 128), F32))
                          for _ in chains)
            carry = lax.fori_loop(0, qi * n_band, functools.partial(block, band=None), carry)
            for band in range(n_band):
                carry = block(qi * n_band + band, carry, band=band)
            res = {}
            for (pr, hh, rc), (m, l, acc) in zip(chains, carry):
                res[pr, hh, rc] = acc / l
                lse_ref[rc * rq:(rc + 1) * rq, 2 * pr + hh:2 * pr + hh + 1] = m + jnp.log(l)
            for pr in pairs:
                for rc in range(row_chunks):
                    o_ref[rc * rq:(rc + 1) * rq, pr * 128:(pr + 1) * 128] = jnp.where(
                        low, res[pr, 0, rc], res[pr, 1, rc]).astype(MM)

    return pl.pallas_call(
        body, name="fox_fwd", grid=(S // tq,),
        in_specs=[pl.BlockSpec((tq, D_ATT), lambda i: (i, 0)), pl.BlockSpec((S, D_ATT), lambda i: (0, 1)),
                  pl.BlockSpec((S, D_ATT), lambda i: (0, 2)), _resident()],
        out_specs=[_rows(tq, D_ATT), _rows(tq, N_HEADS)],
        out_shape=[jax.ShapeDtypeStruct((S, D_ATT), MM), jax.ShapeDtypeStruct((S, N_HEADS), F32)],
        compiler_params=_params("parallel"),
    )(fqkv, fqkv, fqkv, c_row3)


def _swa_bias(rel_bias_slot, bucket):
    def body(rb_ref, bk_ref, out_ref):
        bk = bk_ref[...]
        for s in range(N_HEADS):
            acc = jnp.where(bk < 0, NEG, 0.0).astype(F32)
            for b in range(N_BUCKETS):
                acc = jnp.where(bk == b, rb_ref[b, s], acc)
            out_ref[s] = acc

    return pl.pallas_call(
        body, name="swa_bias", out_shape=jax.ShapeDtypeStruct((N_HEADS, WINDOW, 2 * WINDOW), F32),
        in_specs=[pl.BlockSpec(memory_space=pltpu.SMEM), _resident()], out_specs=_resident(),
    )(rel_bias_slot, bucket)


def _stack4(piece):
    return jnp.concatenate([piece(j) for j in range(4)], axis=0)


def _swa_specs(S):
    q = pl.BlockSpec((WINDOW, D_ATT), lambda n: (n, 0))
    kp = pl.BlockSpec((WINDOW, D_KV), lambda n: (jnp.maximum(n - 1, 0), 4))
    kc = pl.BlockSpec((WINDOW, D_KV), lambda n: (n, 4))
    vp = pl.BlockSpec((WINDOW, D_KV), lambda n: (jnp.maximum(n - 1, 0), 5))
    vc = pl.BlockSpec((WINDOW, D_KV), lambda n: (n, 5))
    return [q, kp, kc, vp, vc]


def _swa_fwd(sqkv, biasm, sinks_slot, after=None):
    S = sqkv.shape[0]

    def body(q_ref, kp_ref, kc_ref, vp_ref, vc_ref, bias_ref, sink_ref, o_ref, lse_ref):
        n = pl.program_id(0)
        no_prev = jnp.where(n > 0, 0.0, NEG)
        low = _head_select((WINDOW, 128), 0)
        res = []
        for g in range(2):
            sel = low if g == 0 else jnp.logical_not(low)
            qg = _stack4(lambda j: jnp.where(sel, q_ref[:, j * 128:(j + 1) * 128], jnp.zeros((WINDOW, 128), MM)))
            sink = _stack4(lambda j: jnp.full((WINDOW, 1), sink_ref[2 * j + g], F32))
            sp = _dot_nt(qg, kp_ref[...]) + _stack4(lambda j: bias_ref[2 * j + g, :, :WINDOW]) + no_prev
            sc = _dot_nt(qg, kc_ref[...]) + _stack4(lambda j: bias_ref[2 * j + g, :, WINDOW:])
            m = jnp.maximum(jnp.maximum(jnp.max(sp, axis=-1, keepdims=True),
                                        jnp.max(sc, axis=-1, keepdims=True)), sink)
            ep = jnp.exp(sp - m)
            ec = jnp.exp(sc - m)
            den = jnp.sum(ep, axis=-1, keepdims=True) + jnp.sum(ec, axis=-1, keepdims=True) + jnp.exp(sink - m)
            res.append((_dot(ep.astype(MM), vp_ref[...]) + _dot(ec.astype(MM), vc_ref[...])) / den)
            lse = m + jnp.log(den)
            for j in range(4):
                lse_ref[:, 2 * j + g:2 * j + g + 1] = lse[j * WINDOW:(j + 1) * WINDOW]
        for j in range(4):
            rows = slice(j * WINDOW, (j + 1) * WINDOW)
            o_ref[:, j * 128:(j + 1) * 128] = jnp.where(low, res[0][rows], res[1][rows]).astype(MM)

    body, in_specs, operands = _run_after(
        after, body, _swa_specs(S) + [_resident(), pl.BlockSpec(memory_space=pltpu.SMEM)],
        (sqkv, sqkv, sqkv, sqkv, sqkv, biasm, sinks_slot))
    return pl.pallas_call(
        body, name="swa_fwd", grid=(S // WINDOW,), in_specs=in_specs,
        out_specs=[_rows(WINDOW, D_ATT), _rows(WINDOW, N_HEADS)],
        out_shape=[jax.ShapeDtypeStruct((S, D_ATT), MM), jax.ShapeDtypeStruct((S, N_HEADS), F32)],
        compiler_params=_params("parallel"),
    )(*operands)


def _post_attn(x, fox_o, swa_o, wout_fox, wout_swa, g2, g3, tm):
    S = x.shape[0]

    def body(x_ref, fo_ref, so_ref, wf_ref, ws_ref, g2_ref, g3_ref, mix_ref, h1_ref, m_ref):
        mix = _dot(fo_ref[...], wf_ref[...]) + _dot(so_ref[...], ws_ref[...])
        mix_ref[...] = mix
        n2, _ = _rms(mix)
        h1 = x_ref[...] + n2 * g2_ref[...]
        h1_ref[...] = h1
        n3, _ = _rms(h1)
        m_ref[...] = (n3 * g3_ref[...]).astype(MM)

    return pl.pallas_call(
        body, name="post_attn", grid=(S // tm,),
        in_specs=[_rows(tm, D_MODEL), _rows(tm, D_ATT), _rows(tm, D_ATT), _resident(), _resident(),
                  _const((1, D_MODEL)), _const((1, D_MODEL))],
        out_specs=[_rows(tm, D_MODEL)] * 3,
        out_shape=[jax.ShapeDtypeStruct((S, D_MODEL), F32), jax.ShapeDtypeStruct((S, D_MODEL), F32),
                   jax.ShapeDtypeStruct((S, D_MODEL), MM)],
        compiler_params=_params("parallel"),
    )(x, fox_o, swa_o, wout_fox, wout_swa, g2, g3)


def _mlp_fwd(m, h1, w1, w2, g4, tm):
    S = m.shape[0]

    def body(m_ref, h1_ref, w1_ref, w2_ref, g4_ref, u_ref, y_ref, h2_ref):
        mb = m_ref[...]
        y = jnp.zeros((tm, D_MODEL), F32)
        for j in range(N_DEV):
            cols = slice(j * FF_CHUNK, (j + 1) * FF_CHUNK)
            u = _dot(mb, w1_ref[j])
            u_ref[:, cols] = u.astype(MM)
            y = y + _dot(jnp.square(jnp.maximum(u, 0.0)).astype(MM), w2_ref[cols, :])
        y_ref[...] = y
        n4, _ = _rms(y)
        h2_ref[...] = h1_ref[...] + n4 * g4_ref[...]

    return pl.pallas_call(
        body, name="mlp_fwd", grid=(S // tm,),
        in_specs=[_rows(tm, D_MODEL), _rows(tm, D_MODEL), _resident(), _resident(), _const((1, D_MODEL))],
        out_specs=[_rows(tm, D_FF), _rows(tm, D_MODEL), _rows(tm, D_MODEL)],
        out_shape=[jax.ShapeDtypeStruct((S, D_FF), MM), jax.ShapeDtypeStruct((S, D_MODEL), F32),
                   jax.ShapeDtypeStruct((S, D_MODEL), F32)],
        compiler_params=_params("parallel"),
    )(m, h1, w1, w2, g4)


def _ple_loss(h2, p, target, wg, wple, g5, tm):
    S = h2.shape[0]

    def body(h2_ref, p_ref, t_ref, wg_ref, wp_ref, g5_ref, dh2_ref, dpe_ref, dgl_ref, dg5_ref, loss_ref):
        i = pl.program_id(0)
        h2 = h2_ref[...]
        gate = jax.nn.sigmoid(_dot(h2.astype(MM), wg_ref[...]))
        pe = _dot(p_ref[...].astype(MM), wp_ref[...])
        n5, r5 = _rms(pe * gate)
        g5 = g5_ref[...]
        diff = h2 + n5 * g5 - t_ref[...]
        per_token = jnp.mean(jnp.square(diff), axis=-1, keepdims=True)
        _accumulate(loss_ref, 0.5 * jnp.sum(per_token, axis=0, keepdims=True), i)
        dh3 = diff * (1.0 / D_MODEL)
        de, dg5 = _rms_bwd(dh3, n5, r5, g5)
        _accumulate(dg5_ref, dg5, i)
        dpe_ref[...] = (de * gate).astype(MM)
        dgl = (de * pe * gate * (1.0 - gate)).astype(MM)
        dgl_ref[...] = dgl
        dh2_ref[...] = dh3 + _dot_nt(dgl, wg_ref[...])

    return pl.pallas_call(
        body, name="ple_loss", grid=(S // tm,),
        in_specs=[_rows(tm, D_MODEL), _rows(tm, D_PLE), _rows(tm, D_MODEL), _resident(), _resident(),
                  _const((1, D_MODEL))],
        out_specs=[_rows(tm, D_MODEL), _rows(tm, D_MODEL), _rows(tm, D_MODEL), _const((1, D_MODEL)), _const((1, 1))],
        out_shape=[jax.ShapeDtypeStruct((S, D_MODEL), F32), jax.ShapeDtypeStruct((S, D_MODEL), MM),
                   jax.ShapeDtypeStruct((S, D_MODEL), MM), jax.ShapeDtypeStruct((1, D_MODEL), F32),
                   jax.ShapeDtypeStruct((1, 1), F32)],
        compiler_params=_params("arbitrary"),
    )(h2, p, target, wg, wple, g5)


def _mlp_bwd(dh2, y, h1, u, w1, w2, g4, g3, tm):
    S = dh2.shape[0]

    def body(dh2_ref, y_ref, h1_ref, u_ref, w1_ref, w2_ref, g4_ref, g3_ref,
             dh1_ref, dy_ref, du_ref, dg4_ref, dg3_ref):
        i = pl.program_id(0)
        dh2 = dh2_ref[...]
        n4, r4 = _rms(y_ref[...])
        dy, dg4 = _rms_bwd(dh2, n4, r4, g4_ref[...])
        _accumulate(dg4_ref, dg4, i)
        dyb = dy.astype(MM)
        dy_ref[...] = dyb
        dm = jnp.zeros((tm, D_MODEL), F32)
        for j in range(N_DEV):
            cols = slice(j * FF_CHUNK, (j + 1) * FF_CHUNK)
            dact = _dot_nt(dyb, w2_ref[cols, :])
            du = (dact * (2.0 * jnp.maximum(u_ref[:, cols].astype(F32), 0.0))).astype(MM)
            du_ref[:, cols] = du
            dm = dm + _dot_nt(du, w1_ref[j])
        n3, r3 = _rms(h1_ref[...])
        dx, dg3 = _rms_bwd(dm, n3, r3, g3_ref[...])
        _accumulate(dg3_ref, dg3, i)
        dh1_ref[...] = dh2 + dx

    return pl.pallas_call(
        body, name="mlp_bwd", grid=(S // tm,),
        in_specs=[_rows(tm, D_MODEL), _rows(tm, D_MODEL), _rows(tm, D_MODEL), _rows(tm, D_FF),
                  _resident(), _resident(), _const((1, D_MODEL)), _const((1, D_MODEL))],
        out_specs=[_rows(tm, D_MODEL), _rows(tm, D_MODEL), _rows(tm, D_FF), _const((1, D_MODEL)),
                   _const((1, D_MODEL))],
        out_shape=[jax.ShapeDtypeStruct((S, D_MODEL), F32), jax.ShapeDtypeStruct((S, D_MODEL), MM),
                   jax.ShapeDtypeStruct((S, D_FF), MM), jax.ShapeDtypeStruct((1, D_MODEL), F32),
                   jax.ShapeDtypeStruct((1, D_MODEL), F32)],
        compiler_params=_params("arbitrary"),
    )(dh2, y, h1, u, w1, w2, g4, g3)


def _attn_out_bwd(dh1, mix, fox_o, swa_o, wout_fox, wout_swa, g2, head_rows, tm):
    S = dh1.shape[0]

    def body(dh1_ref, mix_ref, fo_ref, so_ref, wf_ref, ws_ref, g2_ref, er_ref,
             dmix_ref, dcat_ref, drow_ref, dswa_ref, dg2_ref):
        i = pl.program_id(0)
        n2, r2 = _rms(mix_ref[...])
        dmix, dg2 = _rms_bwd(dh1_ref[...], n2, r2, g2_ref[...])
        _accumulate(dg2_ref, dg2, i)
        dmb = dmix.astype(MM)
        dmix_ref[...] = dmb
        dfo = _dot_nt(dmb, wf_ref[...]).astype(MM)
        dso = _dot_nt(dmb, ws_ref[...]).astype(MM)
        dcat_ref[:, :D_ATT] = dfo
        dcat_ref[:, D_ATT:] = dso
        hi = lax.Precision.HIGHEST
        prod_f = dfo.astype(F32) * fo_ref[...].astype(F32)
        prod_s = dso.astype(F32) * so_ref[...].astype(F32)
        drow_ref[...] = lax.dot_general(er_ref[...], prod_f, NT, precision=hi, preferred_element_type=F32)
        dswa_ref[...] = lax.dot_general(er_ref[...], prod_s, NT, precision=hi, preferred_element_type=F32)

    return pl.pallas_call(
        body, name="attn_out_bwd", grid=(S // tm,),
        in_specs=[_rows(tm, D_MODEL), _rows(tm, D_MODEL), _rows(tm, D_ATT), _rows(tm, D_ATT), _resident(),
                  _resident(), _const((1, D_MODEL)), _resident()],
        out_specs=[_rows(tm, D_MODEL), _rows(tm, D_MODEL), pl.BlockSpec((N_HEADS, tm), lambda i: (0, i)),
                   pl.BlockSpec((N_HEADS, tm), lambda i: (0, i)), _const((1, D_MODEL))],
        out_shape=[jax.ShapeDtypeStruct((S, D_MODEL), MM), jax.ShapeDtypeStruct((S, D_MODEL), MM),
                   jax.ShapeDtypeStruct((N_HEADS, S), F32), jax.ShapeDtypeStruct((N_HEADS, S), F32),
                   jax.ShapeDtypeStruct((1, D_MODEL), F32)],
        compiler_params=_params("arbitrary"),
    )(dh1, mix, fox_o, swa_o, wout_fox, wout_swa, g2, head_rows)


def _fox_bwd(fqkv, dcat, lse_row3, d_row3, c_col, tq, tk, pairs_per_loop=2, after=None):
    S = fqkv.shape[0]
    n_blk = S // tk
    n_qblk = S // tq
    n_band = tk // tq

    def body(q_ref, k_ref, v_ref, do_ref, lse_ref, dd_ref, ck_ref, dq_ref, dk_ref, dv_ref, dc_ref, dcq_ref):
        kb = pl.program_id(0)

        @pl.when(kb == 0)
        def _():
            dq_ref[...] = jnp.zeros_like(dq_ref)
            dcq_ref[...] = jnp.zeros_like(dcq_ref)

        key = lax.broadcasted_iota(jnp.int32, (tk, tq), 0)
        qry = lax.broadcasted_iota(jnp.int32, (tk, tq), 1)
        low = _head_select((tk, 128), 0)
        for first in range(0, N_HEADS // 2, pairs_per_loop):
            pairs = range(first, first + pairs_per_loop)
            heads = [(pr, hh) for pr in pairs for hh in range(2)]
            kh, vh, ck = {}, {}, {}
            for pr in pairs:
                k2 = k_ref[:, pr * 128:(pr + 1) * 128]
                v2 = v_ref[:, pr * 128:(pr + 1) * 128]
                zero = jnp.zeros_like(k2)
                kh[pr, 0], kh[pr, 1] = jnp.where(low, k2, zero), jnp.where(low, zero, k2)
                vh[pr, 0], vh[pr, 1] = jnp.where(low, v2, zero), jnp.where(low, zero, v2)
                for hh in range(2):
                    ck[pr, hh] = ck_ref[:, 2 * pr + hh:2 * pr + hh + 1]

            def block(qb, carry, band, pairs=pairs, kh=kh, vh=vh, ck=ck):
                rows = pl.ds(pl.multiple_of(qb * tq, tq), tq)
                k1 = tk if band is None else (band + 1) * tq
                out = []
                it = iter(carry)
                for pr in pairs:
                    lanes = slice(pr * 128, (pr + 1) * 128)
                    q2 = q_ref[rows, lanes]
                    do2 = do_ref[rows, lanes]
                    dq = None
                    for hh in range(2):
                        h = 2 * pr + hh
                        dk, dv, dc = next(it)
                        s_t = _dot_nt(kh[pr, hh][:k1], q2) - ck[pr, hh][:k1]
                        p_t = jnp.exp(s_t - lse_ref[h, pl.ds(qb, 1), :])
                        if band is not None:
                            p_t = jnp.where(qry[:k1] + band * tq >= key[:k1], p_t, 0.0)
                        ds_t = p_t * (_dot_nt(vh[pr, hh][:k1], do2) - dd_ref[h, pl.ds(qb, 1), :])
                        dsb = ds_t.astype(MM)
                        dv_new = dv[:k1] + _dot(p_t.astype(MM), do2)
                        dk_new = dk[:k1] + _dot(dsb, q2)
                        dc_new = dc[:k1] - jnp.sum(ds_t, axis=1, keepdims=True)
                        if k1 < tk:
                            dv_new = jnp.concatenate([dv_new, dv[k1:]], axis=0)
                            dk_new = jnp.concatenate([dk_new, dk[k1:]], axis=0)
                            dc_new = jnp.concatenate([dc_new, dc[k1:]], axis=0)
                        part = _dot_tn(dsb, kh[pr, hh][:k1])
                        dq = part if dq is None else dq + part
                        dcq_ref[h, pl.ds(qb, 1), :] += jnp.sum(ds_t, axis=0, keepdims=True)
                        out.append((dk_new, dv_new, dc_new))
                    dq_ref[rows, lanes] += dq
                return tuple(out)

            carry = tuple((jnp.zeros((tk, 128), F32), jnp.zeros((tk, 128), F32), jnp.zeros((tk, 1), F32))
                          for _ in heads)
            for band in range(n_band):
                carry = block(kb * n_band + band, carry, band=band)
            carry = lax.fori_loop((kb + 1) * n_band, n_qblk, functools.partial(block, band=None), carry)
            grads = dict(zip(heads, carry))
            for pr in pairs:
                lanes = slice(pr * 128, (pr + 1) * 128)
                dk_ref[:, lanes] = jnp.where(low, grads[pr, 0][0], grads[pr, 1][0]).astype(MM)
                dv_ref[:, lanes] = jnp.where(low, grads[pr, 0][1], grads[pr, 1][1]).astype(MM)
                for hh in range(2):
                    dc_ref[:, 2 * pr + hh:2 * pr + hh + 1] = grads[pr, hh][2]

        @pl.when(kb == n_blk - 1)
        def _():
            dq_ref[...] = dq_ref[...] * Q_SCALE

    body, in_specs, operands = _run_after(
        after, body,
        [pl.BlockSpec((S, D_ATT), lambda i: (0, 0)), pl.BlockSpec((tk, D_ATT), lambda i: (i, 1)),
         pl.BlockSpec((tk, D_ATT), lambda i: (i, 2)), pl.BlockSpec((S, D_ATT), lambda i: (0, 0)),
         _resident(), _resident(), _rows(tk, N_HEADS)],
        (fqkv, fqkv, fqkv, dcat, lse_row3, d_row3, c_col))
    return pl.pallas_call(
        body, name="fox_bwd", grid=(n_blk,), in_specs=in_specs,
        out_specs=[_const((S, D_ATT)), _rows(tk, D_ATT), _rows(tk, D_ATT), _rows(tk, N_HEADS),
                   _const((N_HEADS, n_qblk, tq))],
        out_shape=[jax.ShapeDtypeStruct((S, D_ATT), F32), jax.ShapeDtypeStruct((S, D_ATT), MM),
                   jax.ShapeDtypeStruct((S, D_ATT), MM), jax.ShapeDtypeStruct((S, N_HEADS), F32),
                   jax.ShapeDtypeStruct((N_HEADS, n_qblk, tq), F32)],
        compiler_params=_params("arbitrary"),
    )(*operands)


def _swa_bwd(sqkv, dcat, biasm, sinks_slot, bucket, lse, d_col, after=None):
    S = sqkv.shape[0]
    n_blk = S // WINDOW

    def body(q_ref, kp_ref, kc_ref, vp_ref, vc_ref, do_ref, bias_ref, sink_ref, bk_ref, lse_ref, dd_ref,
             dq_ref, dk_ref, dv_ref, drb_ref, dsink_ref, ds_acc):
        n = pl.program_id(0)

        @pl.when(n == 0)
        def _():
            dk_ref[...] = jnp.zeros_like(dk_ref)
            dv_ref[...] = jnp.zeros_like(dv_ref)
            ds_acc[...] = jnp.zeros_like(ds_acc)
            dsink_ref[...] = jnp.zeros_like(dsink_ref)

        no_prev = jnp.where(n > 0, 0.0, NEG)
        prev = pl.ds(pl.multiple_of(jnp.maximum(n - 1, 0) * WINDOW, WINDOW), WINDOW)
        cur = pl.ds(pl.multiple_of(n * WINDOW, WINDOW), WINDOW)
        lane8 = lax.broadcasted_iota(jnp.int32, (1, N_HEADS), 1)
        dkp = jnp.zeros((WINDOW, D_KV), F32)
        dkc = jnp.zeros((WINDOW, D_KV), F32)
        dvp = jnp.zeros((WINDOW, D_KV), F32)
        dvc = jnp.zeros((WINDOW, D_KV), F32)
        dsink = jnp.zeros((1, N_HEADS), F32)
        low = _head_select((WINDOW, 128), 0)
        zero = jnp.zeros((WINDOW, 128), MM)
        dqs = []
        for g in range(2):
            sel = low if g == 0 else jnp.logical_not(low)
            qg = _stack4(lambda j: jnp.where(sel, q_ref[:, j * 128:(j + 1) * 128], zero))
            dog = _stack4(lambda j: jnp.where(sel, do_ref[:, j * 128:(j + 1) * 128], zero))
            lse_g = _stack4(lambda j: lse_ref[:, 2 * j + g:2 * j + g + 1])
            dd = _stack4(lambda j: dd_ref[:, 2 * j + g:2 * j + g + 1])
            sink = _stack4(lambda j: jnp.full((WINDOW, 1), sink_ref[2 * j + g], F32))
            pp = jnp.exp(_dot_nt(qg, kp_ref[...]) + _stack4(lambda j: bias_ref[2 * j + g, :, :WINDOW]) + no_prev - lse_g)
            pc = jnp.exp(_dot_nt(qg, kc_ref[...]) + _stack4(lambda j: bias_ref[2 * j + g, :, WINDOW:]) - lse_g)
            sink_term = jnp.exp(sink - lse_g) * dd
            dsp = pp * (_dot_nt(dog, vp_ref[...]) - dd)
            dsc = pc * (_dot_nt(dog, vc_ref[...]) - dd)
            for j in range(4):
                rows = slice(j * WINDOW, (j + 1) * WINDOW)
                dsink = dsink + jnp.where(lane8 == 2 * j + g, -jnp.sum(sink_term[rows]), 0.0)
                ds_acc[2 * j + g, :, :WINDOW] += dsp[rows]
                ds_acc[2 * j + g, :, WINDOW:] += dsc[rows]
            dspb, dscb = dsp.astype(MM), dsc.astype(MM)
            dqs.append(_dot(dspb, kp_ref[...]) + _dot(dscb, kc_ref[...]))
            dkp = dkp + _dot_tn(dspb, qg)
            dkc = dkc + _dot_tn(dscb, qg)
            dvp = dvp + _dot_tn(pp.astype(MM), dog)
            dvc = dvc + _dot_tn(pc.astype(MM), dog)
        for j in range(4):
            rows = slice(j * WINDOW, (j + 1) * WINDOW)
            dq_ref[:, j * 128:(j + 1) * 128] = (jnp.where(low, dqs[0][rows], dqs[1][rows]) * Q_SCALE).astype(MM)
        dk_ref[prev, :] += dkp
        dk_ref[cur, :] += dkc
        dv_ref[prev, :] += dvp
        dv_ref[cur, :] += dvc
        dsink_ref[...] += dsink

        @pl.when(n == n_blk - 1)
        def _():
            bk = bk_ref[...]
            rb = lax.broadcasted_iota(jnp.int32, (N_BUCKETS, N_HEADS), 0)
            cb = lax.broadcasted_iota(jnp.int32, (N_BUCKETS, N_HEADS), 1)
            out = jnp.zeros((N_BUCKETS, N_HEADS), F32)
            for s in range(N_HEADS):
                acc = ds_acc[s]
                for b in range(N_BUCKETS):
                    out = out + jnp.where((rb == b) & (cb == s), jnp.sum(jnp.where(bk == b, acc, 0.0)), 0.0)
            drb_ref[...] = out

    do_spec = pl.BlockSpec((WINDOW, D_ATT), lambda n: (n, 1))
    body, in_specs, operands = _run_after(
        after, body, _swa_specs(S) + [do_spec, _resident(), pl.BlockSpec(memory_space=pltpu.SMEM), _resident(),
                                      _rows(WINDOW, N_HEADS), _rows(WINDOW, N_HEADS)],
        (sqkv, sqkv, sqkv, sqkv, sqkv, dcat, biasm, sinks_slot, bucket, lse, d_col))
    return pl.pallas_call(
        body, name="swa_bwd", grid=(n_blk,), in_specs=in_specs,
        out_specs=[_rows(WINDOW, D_ATT), _const((S, D_KV)), _const((S, D_KV)), _const((N_BUCKETS, N_HEADS)),
                   _const((1, N_HEADS))],
        out_shape=[jax.ShapeDtypeStruct((S, D_ATT), MM), jax.ShapeDtypeStruct((S, D_KV), F32),
                   jax.ShapeDtypeStruct((S, D_KV), F32), jax.ShapeDtypeStruct((N_BUCKETS, N_HEADS), F32),
                   jax.ShapeDtypeStruct((1, N_HEADS), F32)],
        scratch_shapes=[pltpu.VMEM((N_HEADS, WINDOW, 2 * WINDOW), F32)],
        compiler_params=_params("arbitrary"),
    )(*operands)


D_Z = 3 * D_ATT + D_ATT + 2 * D_KV


def _pack_dz(dq_fox, dk_fox, dv_fox, dsq, dsk, dsv, tm):
    S = dq_fox.shape[0]

    def body(dq_ref, dk_ref, dv_ref, dsq_ref, dsk_ref, dsv_ref, dz_ref):
        dz_ref[:, 0:512] = dq_ref[...].astype(MM)
        dz_ref[:, 512:1024] = dk_ref[...]
        dz_ref[:, 1024:1536] = dv_ref[...]
        dz_ref[:, 1536:2048] = dsq_ref[...]
        dz_ref[:, 2048:2176] = dsk_ref[...].astype(MM)
        dz_ref[:, 2176:2304] = dsv_ref[...].astype(MM)

    return pl.pallas_call(
        body, name="pack_dz", grid=(S // tm,),
        in_specs=[_rows(tm, D_ATT), _rows(tm, D_ATT), _rows(tm, D_ATT), _rows(tm, D_ATT), _rows(tm, D_KV),
                  _rows(tm, D_KV)],
        out_specs=_rows(tm, D_Z), out_shape=jax.ShapeDtypeStruct((S, D_Z), MM),
        compiler_params=_params("parallel"),
    )(dq_fox, dk_fox, dv_fox, dsq, dsk, dsv)


def _pre_attn_bwd(x, dh1, dz, dff_t, wt, g1, tm, after=None):
    S = x.shape[0]

    def body(x_ref, dh1_ref, dz_ref, dff_ref, wt_ref, g1_ref, dx_ref, dg1_ref):
        i = pl.program_id(0)
        da = (_dot(dz_ref[:, 0:WT_FF], wt_ref[0:WT_FF]) + _dot(dz_ref[:, WT_FF:D_Z], wt_ref[WT_SQ:WT_ROWS])
              + _dot_tn(dff_ref[...].astype(MM), wt_ref[WT_FF:WT_SQ]))
        n1, r1 = _rms(x_ref[...])
        dx, dg1 = _rms_bwd(da, n1, r1, g1_ref[...])
        _accumulate(dg1_ref, dg1, i)
        dx_ref[...] = dh1_ref[...] + dx

    body, in_specs, operands = _run_after(
        after, body,
        [_rows(tm, D_MODEL), _rows(tm, D_MODEL), _rows(tm, D_Z), pl.BlockSpec((16, tm), lambda i: (0, i)),
         _resident(), _const((1, D_MODEL))], (x, dh1, dz, dff_t, wt, g1))
    return pl.pallas_call(
        body, name="pre_attn_bwd", grid=(S // tm,), in_specs=in_specs,
        out_specs=[_rows(tm, D_MODEL), _const((1, D_MODEL))],
        out_shape=[jax.ShapeDtypeStruct((S, D_MODEL), F32), jax.ShapeDtypeStruct((1, D_MODEL), F32)],
        compiler_params=_params("arbitrary"),
    )(*operands)


def _weight_grad(a, b, name, tk, n_chunks=1, relu2=False):
    S, K = a.shape
    N = b.shape[1]
    cn = N // n_chunks

    def body(a_ref, b_ref, out_ref):
        av = a_ref[...]
        if relu2:
            av = jnp.square(jnp.maximum(av.astype(F32), 0.0))
        av = av.astype(MM)
        for j in range(n_chunks):
            val = _dot_tn(av, b_ref[:, j * cn:(j + 1) * cn].astype(MM)).astype(MM)
            if n_chunks > 1:
                out_ref[j] = val
            else:
                out_ref[...] = val

    if n_chunks > 1:
        out_spec = pl.BlockSpec((n_chunks, tk, cn), lambda i: (0, i, 0))
        out_shape = jax.ShapeDtypeStruct((n_chunks, K, cn), MM)
    else:
        out_spec = pl.BlockSpec((tk, N), lambda i: (i, 0))
        out_shape = jax.ShapeDtypeStruct((K, N), MM)
    return pl.pallas_call(
        body, name=name, grid=(K // tk,),
        in_specs=[pl.BlockSpec((S, tk), lambda i: (0, i)), _resident()],
        out_specs=out_spec, out_shape=out_shape, compiler_params=_params("parallel"),
    )(a, b)


def _weight_grad_two(a1, a2, b, name, tk):
    S, K1 = a1.shape
    K2 = a2.shape[1]
    N = b.shape[1]
    n1 = K1 // tk

    def body(a1_ref, a2_ref, b_ref, out_ref):
        av = jnp.where(pl.program_id(0) < n1, a1_ref[...], a2_ref[...])
        out_ref[...] = _dot_tn(av, b_ref[...]).astype(MM)

    return pl.pallas_call(
        body, name=name, grid=((K1 + K2) // tk,),
        in_specs=[pl.BlockSpec((S, tk), lambda i: (0, jnp.minimum(i, n1 - 1))),
                  pl.BlockSpec((S, tk), lambda i: (0, jnp.maximum(i - n1, 0))), _resident()],
        out_specs=pl.BlockSpec((tk, N), lambda i: (i, 0)), out_shape=jax.ShapeDtypeStruct((K1 + K2, N), MM),
        compiler_params=_params("parallel"),
    )(a1, a2, b)


def _place():
    return lax.axis_index("x"), lax.axis_index("y"), lax.axis_index("c")


def _all_gather_sequencer(stacks, name, collective_id):
    refs = [jax.new_ref(s, memory_space=pltpu.MemorySpace.HBM) for s in stacks]
    n = len(refs)

    @pl.kernel(mesh=plsc.ScalarSubcoreMesh(axis_name="sequencer", num_cores=1), name=name,
               scratch_types=(pltpu.SemaphoreType.DMA((7 * n,)), pltpu.SemaphoreType.DMA((7 * n,))),
               compiler_params=pltpu.CompilerParams(collective_id=collective_id))
    def launch(send_sems, recv_sems):
        x, y, c = _place()
        sibling = (x, y, 1 - c)
        chips = [(1 - x, y), (x, 1 - y), (1 - x, 1 - y)]
        peers = [sibling] + [(px, py, c) for px, py in chips]
        barrier = pltpu.get_barrier_semaphore()
        for peer in peers:
            pl.semaphore_signal(barrier, inc=1, device_id=peer, device_id_type=MESH)
        pl.semaphore_wait(barrier, len(peers))

        def copy(a, k, block, to):
            px, py, pc = block
            slot = refs[a].at[4 * px + 2 * py + pc]
            return _remote(slot, slot, send_sems, recv_sems, 7 * a + k, to)

        first = [copy(a, k, (x, y, c), peer) for a in range(n) for k, peer in enumerate(peers)]
        for cp in first:
            cp.start()
        passed = []
        for j, (px, py) in enumerate(chips):
            for a in range(n):
                copy(a, 1 + j, (px, py, c), sibling).wait_recv()
                passed.append(copy(a, 4 + j, (px, py, c), sibling))
                passed[-1].start()
        for a in range(n):
            copy(a, 0, (x, y, 1 - c), sibling).wait_recv()
            for j, (px, py) in enumerate(chips):
                copy(a, 4 + j, (px, py, 1 - c), sibling).wait_recv()
        for cp in first + passed:
            cp.wait_send()

    launch()
    return [ref[...] for ref in refs]


def _chip_sums(grads, others, name):
    n = len(grads)

    def body(c_ref, *refs):
        for g_ref, o_ref, out_ref in zip(refs[:n], refs[n:2 * n], refs[2 * n:]):
            out_ref[...] = (g_ref[...].astype(F32) + o_ref[...].astype(F32)).astype(out_ref.dtype)

    own = [pl.BlockSpec((None, None) + g.shape[2:], lambda k, c_ref: (k, c_ref[0], 0, 0)) for g in grads]
    chip = [pl.BlockSpec((None,) + g.shape[2:], lambda k, c_ref: (k, 0, 0)) for g in grads]
    return pl.pallas_call(
        body, name=name,
        grid_spec=pltpu.PrefetchScalarGridSpec(num_scalar_prefetch=1, grid=(4,), in_specs=own + chip, out_specs=chip),
        out_shape=[jax.ShapeDtypeStruct((4,) + g.shape[2:], MM) for g in grads],
        compiler_params=_params("parallel"),
    )(lax.axis_index("c").astype(jnp.int32).reshape(1), *grads, *others)


HBM_SPEC = pl.BlockSpec(memory_space=pltpu.HBM)
SEM_SPEC = pl.BlockSpec(memory_space=pltpu.SEMAPHORE)
DATAFLOW = pltpu.SideEffectType.DATAFLOW_SIDE_EFFECTING


def _exchange_start(name, arrays, n_copies, plan):
    n = len(arrays)

    def body(*refs):
        send_sems, recv_sems, token = refs[n], refs[n + 1], refs[2 * n + 2]
        for cp in plan(refs[:n], send_sems, recv_sems):
            cp.start()
        token[...] = jnp.zeros_like(token)

    out = pl.pallas_call(
        body, name=name,
        out_shape=(pltpu.SemaphoreType.DMA((n_copies,)), pltpu.SemaphoreType.DMA((n_copies,)),
                   *[pltpu.HBM(a.shape, a.dtype) for a in arrays], jax.ShapeDtypeStruct((1, D_MODEL), F32)),
        in_specs=[HBM_SPEC] * n,
        out_specs=(SEM_SPEC, SEM_SPEC, *[HBM_SPEC] * n, pl.BlockSpec(memory_space=pltpu.VMEM)),
        input_output_aliases={i: 2 + i for i in range(n)},
        compiler_params=pltpu.CompilerParams(has_side_effects=DATAFLOW),
    )(*[pltpu.with_memory_space_constraint(a, pltpu.HBM) for a in arrays])
    return (out[0], out[1]), list(out[2:2 + n]), out[2 + n]


def _exchange_wait(name, arrays, sems, after, plan):
    n = len(arrays)
    after = list(after) if isinstance(after, (list, tuple)) else [after]

    def body(*refs):
        send_sems, recv_sems = refs[n], refs[n + 1]
        for cp in plan(refs[:n], send_sems, recv_sems):
            cp.wait_send()
            cp.wait_recv()

    out = pl.pallas_call(
        body, name=name, out_shape=[pltpu.HBM(a.shape, a.dtype) for a in arrays],
        in_specs=[HBM_SPEC] * n + [SEM_SPEC, SEM_SPEC] + [pl.BlockSpec(memory_space=pl.ANY)] * len(after),
        out_specs=[HBM_SPEC] * n, input_output_aliases={i: i for i in range(n)},
        compiler_params=pltpu.CompilerParams(has_side_effects=DATAFLOW),
    )(*arrays, sems[0], sems[1], *after)
    return list(out)


def _remote(src, dst, send_sems, recv_sems, k, to):
    return pltpu.make_async_remote_copy(src_ref=src, dst_ref=dst, send_sem=send_sems.at[k], recv_sem=recv_sems.at[k],
                                        device_id=to, device_id_type=MESH)


def _plan_gather_direct(refs, send_sems, recv_sems):
    x, y, c = _place()
    me = 4 * x + 2 * y + c
    peers = [(x, y, 1 - c), (1 - x, y, c), (x, 1 - y, c), (1 - x, 1 - y, c)]
    return [_remote(ref.at[me], ref.at[me], send_sems, recv_sems, 4 * a + k, peer)
            for a, ref in enumerate(refs) for k, peer in enumerate(peers)]


def _plan_gather_pass_on(refs, send_sems, recv_sems):
    x, y, c = _place()
    chips = [(1 - x, y), (x, 1 - y), (1 - x, 1 - y)]
    return [_remote(ref.at[4 * px + 2 * py + c], ref.at[4 * px + 2 * py + c], send_sems, recv_sems, 3 * a + k,
                    (x, y, 1 - c))
            for a, ref in enumerate(refs) for k, (px, py) in enumerate(chips)]


def _plan_in_chip(refs, send_sems, recv_sems):
    n = len(refs) // 2
    x, y, c = _place()
    return [_remote(refs[a].at[:, 1 - c], refs[n + a], send_sems, recv_sems, a, (x, y, 1 - c)) for a in range(n)]


def _plan_between_chips(refs, send_sems, recv_sems):
    n = len(refs) // 2
    x, y, c = _place()
    chips = [(1 - x, y), (x, 1 - y), (1 - x, 1 - y)]
    return [_remote(refs[a].at[2 * px + py], refs[n + a].at[2 * x + y], send_sems, recv_sems, 3 * a + k, (px, py, c))
            for a in range(n) for k, (px, py) in enumerate(chips)]


def _plan_late_between(refs, send_sems, recv_sems):
    sums, land, small = refs
    x, y, c = _place()
    me = 4 * x + 2 * y + c
    copies = _plan_between_chips([sums, land], send_sems, recv_sems)
    peers = [(x ^ dx, y ^ dy, c ^ dc) for dx in range(2) for dy in range(2) for dc in range(2) if dx + dy + dc]
    return copies + [_remote(small.at[me], small.at[me], send_sems, recv_sems, 3 + k, peer)
                     for k, peer in enumerate(peers)]


def _adamw_math(w, g, m, v):
    m = ADAM_B1 * m + (1.0 - ADAM_B1) * g
    v = ADAM_B2 * v + (1.0 - ADAM_B2) * jnp.square(g)
    m_hat = m / (1.0 - ADAM_B1 ** ADAM_STEP)
    v_hat = v / (1.0 - ADAM_B2 ** ADAM_STEP)
    delta = -ADAM_LR * (m_hat / (jnp.sqrt(v_hat) + ADAM_EPS) + ADAM_WD * w)
    return delta, m, v


def _adamw(parts, w, m, v, name):
    n_parts, r, cdim = parts.shape
    tr = 256 if r % 256 == 0 else r

    def body(p_ref, w_ref, m_ref, v_ref, g_out, d_out, m_out, v_out):
        g = p_ref[0].astype(F32)
        for k in range(1, n_parts):
            g = g + p_ref[k].astype(F32)
        delta, m_new, v_new = _adamw_math(w_ref[...], g, m_ref[...], v_ref[...])
        g_out[...] = g
        d_out[...] = delta
        m_out[...] = m_new
        v_out[...] = v_new

    blk = pl.BlockSpec((tr, cdim), lambda i: (i, 0))
    return pl.pallas_call(
        body, name=name, grid=(r // tr,),
        in_specs=[pl.BlockSpec((n_parts, tr, cdim), lambda i: (0, i, 0)), blk, blk, blk],
        out_specs=[blk] * 4, out_shape=[jax.ShapeDtypeStruct((r, cdim), F32)] * 4,
        compiler_params=_params("parallel"),
    )(parts, w, m, v)


def _adamw_chips(parts, sums, w, m, v, name):
    _, r, cdim = parts.shape
    tr = 256 if r % 256 == 0 else r

    def body(chip_ref, p_ref, own_ref, w_ref, m_ref, v_ref, g_out, d_out, m_out, v_out):
        g = None
        for k in range(4):
            term = jnp.where(chip_ref[0] == k, own_ref[...], p_ref[k]).astype(F32)
            g = term if g is None else g + term
        delta, m_new, v_new = _adamw_math(w_ref[...], g, m_ref[...], v_ref[...])
        g_out[...] = g
        d_out[...] = delta
        m_out[...] = m_new
        v_out[...] = v_new

    blk = pl.BlockSpec((tr, cdim), lambda i, chip: (i, 0))
    my_chip = (2 * lax.axis_index("x") + lax.axis_index("y")).astype(jnp.int32).reshape(1)
    return pl.pallas_call(
        body, name=name,
        grid_spec=pltpu.PrefetchScalarGridSpec(
            num_scalar_prefetch=1, grid=(r // tr,),
            in_specs=[pl.BlockSpec((4, tr, cdim), lambda i, chip: (0, i, 0)),
                      pl.BlockSpec((None, tr, cdim), lambda i, chip: (chip[0], i, 0)), blk, blk, blk],
            out_specs=[blk] * 4),
        out_shape=[jax.ShapeDtypeStruct((r, cdim), F32)] * 4,
        compiler_params=_params("parallel"),
    )(my_chip, parts, sums, w, m, v)


class _NoExchange:
    def __init__(self, weights):
        self.weights = weights

    def before_pre_attn(self):
        return None

    def after_fox_fwd(self, fox_o):
        return None

    def after_attention(self, swa_o):
        return self.weights

    def after_early_grads(self, grads):
        return None

    def after_swa_bwd(self, dsq):
        return None

    def after_w_in_grad(self, d_win):
        return None


def _slot_order(t, axis):
    shp = t.shape
    t = t.reshape(shp[:axis] + (2, 4, shp[axis] // N_HEADS) + shp[axis + 1:])
    return jnp.swapaxes(t, axis, axis + 1).reshape(shp)


def _head_order(t, axis):
    shp = t.shape
    t = t.reshape(shp[:axis] + (4, 2, shp[axis] // N_HEADS) + shp[axis + 1:])
    return jnp.swapaxes(t, axis, axis + 1).reshape(shp)


def _forward_backward(x, p, target, win_t, hooks, b_forget, rel_bias, sinks, g1, g2, g3, g4, g5):
    S = x.shape[0]
    tm = 512
    tm_mlp = 512
    t = 256
    q0 = 3 * D_ATT + N_HEADS
    win_t = win_t.reshape(D_IN, D_MODEL)
    wt = jnp.concatenate(
        [win_t[:q0], jnp.zeros((8, D_MODEL), MM), _slot_order(win_t[q0:q0 + D_ATT], 0), win_t[q0 + D_ATT:]], axis=0)
    bcol = jnp.pad(b_forget.reshape(N_HEADS, 1), ((0, 8), (0, 0)))
    rel_bias_slot = rel_bias[:, np.array(SLOT_HEAD)]
    sinks_slot = sinks.reshape(N_HEADS)[np.array(SLOT_HEAD)]
    bucket = jnp.asarray(_swa_bucket_map())

    a, fqkv, sqkv, fft = _pre_attn(x, g1, wt, tm, after=hooks.before_pre_attn())
    c_row = _forget_cumsum(fft, bcol)
    c_col = c_row[:N_HEADS].T
    c_row3 = c_row[:N_HEADS].reshape(N_HEADS, S // t, t)
    fox_o, fox_lse = _fox_fwd(fqkv, c_row3, tq=512, tk=t)
    biasm = _swa_bias(rel_bias_slot, bucket)
    swa_o, swa_lse = _swa_fwd(sqkv, biasm, sinks_slot, after=hooks.after_fox_fwd(fox_o))
    wout, w1, w2, wple, wg = hooks.after_attention(swa_o)
    wout_fox = wout[:D_ATT]
    wout_swa = _slot_order(wout[D_ATT:], 0)
    mix, h1, m = _post_attn(x, fox_o, swa_o, wout_fox, wout_swa, g2, g3, tm)
    u, y, h2 = _mlp_fwd(m, h1, w1, w2, g4, tm_mlp)
    dh2, dpe, dgl, dg5, loss = _ple_loss(h2, p, target, wg, wple, g5, tm)

    d_wple = _weight_grad(p, dpe, "grad_w_ple", tk=D_PLE, n_chunks=N_DEV)
    d_wg = _weight_grad(h2, dgl, "grad_w_ple_gate", tk=256)
    dh1, dy, du, dg4, dg3 = _mlp_bwd(dh2, y, h1, u, w1, w2, g4, g3, tm)
    d_w2 = _weight_grad(u, dy, "grad_w_ff2", tk=256, relu2=True)
    d_w1 = _weight_grad(m, du, "grad_w_ff1", tk=256, n_chunks=N_DEV)
    head = np.arange(D_ATT) // HEAD_DIM
    head_rows = jnp.asarray((head[None, :] == np.arange(N_HEADS)[:, None]).astype(np.float32))
    dmix, dcat, d_row, d_swa, dg2 = _attn_out_bwd(dh1, mix, fox_o, swa_o, wout_fox, wout_swa, g2, head_rows, tm)
    d_col = d_swa.T
    d_wout = _weight_grad_two(fox_o, swa_o, dmix, "grad_w_out", tk=256)
    d_wout = jnp.concatenate([d_wout[:D_ATT], _head_order(d_wout[D_ATT:], 0)], axis=0)
    d_wout = d_wout.reshape(N_DEV, D_MODEL // N_DEV, D_MODEL)
    early = dict(w_ff1=d_w1, w_ff2=d_w2.reshape(N_DEV, FF_CHUNK, D_MODEL), w_ple=d_wple,
                 w_ple_gate=d_wg.reshape(N_DEV, D_MODEL // N_DEV, D_MODEL), w_out=d_wout)

    dsq, dsk, dsv, d_rb_slot, d_sink_slot = _swa_bwd(sqkv, dcat, biasm, sinks_slot, bucket, swa_lse, d_col,
                                                     after=hooks.after_early_grads(early))
    lse_row3 = fox_lse.T.reshape(N_HEADS, S // t, t)
    d_row3 = d_row.reshape(N_HEADS, S // t, t)
    dq_fox, dk_fox, dv_fox, dc_col, dcq = _fox_bwd(fqkv, dcat, lse_row3, d_row3, c_col, tq=t, tk=512,
                                                  after=hooks.after_swa_bwd(dsq))
    dc_row = jnp.pad(dc_col.T + dcq.reshape(N_HEADS, S), ((0, 8), (0, 0)))
    dff_t, db, d_wff_t = _forget_bwd(dc_row, fft, bcol, a)
    dz = _pack_dz(dq_fox, dk_fox, dv_fox, dsq, dsk, dsv, 512)
    d_wmain = _weight_grad(dz, a, "grad_w_in", tk=256)

    sq0 = 3 * D_ATT
    d_win = jnp.concatenate(
        [d_wmain[:sq0], d_wff_t[:N_HEADS].astype(MM), _head_order(d_wmain[sq0:sq0 + D_ATT], 0),
         d_wmain[sq0 + D_ATT:]], axis=0)
    d_win = d_win.reshape(N_DEV, D_IN // N_DEV, D_MODEL)
    grad_x, dg1 = _pre_attn_bwd(x, dh1, dz, dff_t, wt, g1, tm, after=hooks.after_w_in_grad(d_win))
    big = dict(early, w_in=d_win)
    small = dict(b_forget=db[:N_HEADS].reshape(1, N_HEADS), rel_bias=d_rb_slot[:, np.array(HEAD_SLOT)],
                 swa_sinks=d_sink_slot[:, np.array(HEAD_SLOT)], g_attn_pre=dg1, g_attn_post=dg2, g_ff_pre=dg3,
                 g_ff_post=dg4, g_ple_post=dg5)
    return loss, grad_x, big, small


BIG = ("w_in", "w_out", "w_ff1", "w_ff2", "w_ple", "w_ple_gate")
SMALL_ROWS = ("g_attn_pre", "g_attn_post", "g_ff_pre", "g_ff_post", "g_ple_post")
WEIGHTS =("w_in", "b_forget", "w_out", "rel_bias", "swa_sinks", "g_attn_pre", "g_attn_post", "w_ff1", "w_ff2",
           "g_ff_pre", "g_ff_post", "w_ple", "w_ple_gate", "g_ple_post")


EARLY = ("w_ff1", "w_ff2", "w_ple", "w_ple_gate", "w_out")


class _Overlap:
    def __init__(self, later):
        self.later = later

    def before_pre_attn(self):
        self.gather_sems, self.later, token = _exchange_start("gather_rest_start", self.later, 4 * 5, _plan_gather_direct)
        return token

    def after_fox_fwd(self, fox_o):
        later = _exchange_wait("gather_rest_wait", self.later, self.gather_sems, fox_o, _plan_gather_direct)
        self.pass_sems, self.later, token = _exchange_start("gather_pass_on_start", later, 3 * 5, _plan_gather_pass_on)
        return token

    def after_attention(self, swa_o):
        wout_g, w1_g, w2_g, wple_g, wg_g = _exchange_wait("gather_pass_on_wait", self.later, self.pass_sems, swa_o,
                                                         _plan_gather_pass_on)
        return (wout_g.reshape(D_MODEL, D_MODEL), w1_g, w2_g.reshape(D_FF, D_MODEL),
                jnp.moveaxis(wple_g, 0, 1).reshape(D_PLE, D_MODEL), wg_g.reshape(D_MODEL, D_MODEL))

    def after_early_grads(self, grads):
        views = [grads[k].reshape((4, 2) + grads[k].shape[1:]) for k in EARLY]
        lands = [lax.empty((4,) + grads[k].shape[1:], MM) for k in EARLY]
        self.in_chip_sems, self.in_chip, token = _exchange_start("grads_in_chip_start", views + lands, len(EARLY),
                                                                 _plan_in_chip)
        return token

    def after_swa_bwd(self, dsq):
        arrays = _exchange_wait("grads_in_chip_wait", self.in_chip, self.in_chip_sems, dsq, _plan_in_chip)
        n = len(EARLY)
        sums = list(_chip_sums(arrays[:n], arrays[n:], "chip_sums_early"))
        lands = [lax.empty(s.shape, s.dtype) for s in sums]
        self.between_sems, self.between, token = _exchange_start("grads_between_chips_start", sums + lands, 3 * n,
                                                                 _plan_between_chips)
        return token

    def after_w_in_grad(self, d_win):
        self.late_in_chip_sems, self.late_in_chip, token = _exchange_start(
            "late_in_chip_start", [d_win.reshape((4, 2) + d_win.shape[1:]), lax.empty((4,) + d_win.shape[1:], MM)],
            1, _plan_in_chip)
        return token

    def finish(self, after):
        arrays = _exchange_wait("grads_between_chips_wait", self.between, self.between_sems, after,
                                _plan_between_chips)
        n = len(EARLY)
        self.sums = arrays[:n]
        return arrays[n:]


def _pack_small(t):
    rows = [t[k].reshape(1, D_MODEL) for k in SMALL_ROWS]
    misc = jnp.concatenate([t["b_forget"].reshape(-1), t["swa_sinks"].reshape(-1), t["rel_bias"].reshape(-1)])
    rows.append(jnp.pad(misc, (0, D_MODEL - misc.shape[0])).reshape(1, D_MODEL))
    rows.append(jnp.pad(t["loss"].reshape(-1), (0, D_MODEL - 1)).reshape(1, D_MODEL))
    rows.append(jnp.zeros((1, D_MODEL), F32))
    return jnp.concatenate(rows, axis=0).astype(F32)


def _unpack_small(blk):
    out = {k: blk[i].reshape(1, D_MODEL) for i, k in enumerate(SMALL_ROWS)}
    misc = blk[len(SMALL_ROWS)]
    out["b_forget"] = misc[:N_HEADS].reshape(1, N_HEADS)
    out["swa_sinks"] = misc[N_HEADS:2 * N_HEADS].reshape(1, N_HEADS)
    out["rel_bias"] = misc[2 * N_HEADS:2 * N_HEADS + N_BUCKETS * N_HEADS].reshape(N_BUCKETS, N_HEADS)
    out["loss"] = blk[len(SMALL_ROWS) + 1, 0]
    return out


def kernel(x, p, w_in, b_forget, w_out, rel_bias, swa_sinks, g_attn_pre, g_attn_post, w_ff1, w_ff2, g_ff_pre, g_ff_post, w_ple, w_ple_gate, g_ple_post, loss_target, m_w_in, m_b_forget, m_w_out, m_rel_bias, m_swa_sinks, m_g_attn_pre, m_g_attn_post, m_w_ff1, m_w_ff2, m_g_ff_pre, m_g_ff_post, m_w_ple, m_w_ple_gate, m_g_ple_post, v_w_in, v_b_forget, v_w_out, v_rel_bias, v_swa_sinks, v_g_attn_pre, v_g_attn_post, v_w_ff1, v_w_ff2, v_g_ff_pre, v_g_ff_post, v_w_ple, v_w_ple_gate, v_g_ple_post):
    w = dict(w_in=w_in, b_forget=b_forget, w_out=w_out, rel_bias=rel_bias, swa_sinks=swa_sinks,
             g_attn_pre=g_attn_pre, g_attn_post=g_attn_post, w_ff1=w_ff1, w_ff2=w_ff2, g_ff_pre=g_ff_pre,
             g_ff_post=g_ff_post, w_ple=w_ple, w_ple_gate=w_ple_gate, g_ple_post=g_ple_post)
    mom = dict(w_in=m_w_in, b_forget=m_b_forget, w_out=m_w_out, rel_bias=m_rel_bias, swa_sinks=m_swa_sinks,
               g_attn_pre=m_g_attn_pre, g_attn_post=m_g_attn_post, w_ff1=m_w_ff1, w_ff2=m_w_ff2,
               g_ff_pre=m_g_ff_pre, g_ff_post=m_g_ff_post, w_ple=m_w_ple, w_ple_gate=m_w_ple_gate,
               g_ple_post=m_g_ple_post)
    var = dict(w_in=v_w_in, b_forget=v_b_forget, w_out=v_w_out, rel_bias=v_rel_bias, swa_sinks=v_swa_sinks,
               g_attn_pre=v_g_attn_pre, g_attn_post=v_g_attn_post, w_ff1=v_w_ff1, w_ff2=v_w_ff2,
               g_ff_pre=v_g_ff_pre, g_ff_post=v_g_ff_post, w_ple=v_w_ple, w_ple_gate=v_w_ple_gate,
               g_ple_post=v_g_ple_post)

    turn = lambda t, k: t.T if k == "w_in" else t
    me = 4 * lax.axis_index("x") + 2 * lax.axis_index("y") + lax.axis_index("c")

    def stack(block):
        return lax.dynamic_update_slice_in_dim(lax.empty((N_DEV,) + block.shape, block.dtype), block[None], me, 0)

    stacks = [stack(turn(w[k][0], k).astype(MM)) for k in BIG]
    (win_g,), later = _all_gather_sequencer(stacks[:1], "all_gather_sequencer", 1), stacks[1:]
    hooks = _Overlap(later)
    loss, grad_x, big, small = _forward_backward(
        x[0], p[0, 0], loss_target[0], win_g, hooks, b_forget, rel_bias, swa_sinks,
        g_attn_pre, g_attn_post, g_ff_pre, g_ff_post, g_ple_post)
    out_g, out_d, out_m, out_v = {}, {}, {}, {}

    def update(k, part, own):
        g, d, m_new, v_new = _adamw_chips(part, own, turn(w[k][0], k), turn(mom[k][0], k), turn(var[k][0], k),
                                          "adamw_" + k)
        out_g[k], out_d[k], out_m[k], out_v[k] = turn(g, k)[None], turn(d, k)[None], turn(m_new, k)[None], turn(v_new, k)[None]
        return d

    view, other = _exchange_wait("late_in_chip_wait", hooks.late_in_chip, hooks.late_in_chip_sems, grad_x,
                                 _plan_in_chip)
    (chip_sum,) = _chip_sums([view], [other], "chip_sum_w_in")
    small["loss"] = loss
    between_sems, between, token = _exchange_start(
        "late_between_chips_start", [chip_sum, lax.empty(chip_sum.shape, MM), stack(_pack_small(small))], 3 + 7,
        _plan_late_between)
    early_parts = hooks.finish(token)
    done = [update(k, part, own) for k, part, own in zip(EARLY, early_parts, hooks.sums)]
    chip_sum, part, small_all = _exchange_wait("late_between_chips_wait", between, between_sems, done,
                                               _plan_late_between)
    update("w_in", part, chip_sum)
    rep = {k: w[k] for k in w if k not in BIG}
    rep["loss"] = jnp.zeros((), F32)
    rep_m = {k: mom[k] for k in mom if k not in BIG}
    rep_m["loss"] = jnp.zeros((), F32)
    rep_v = {k: var[k] for k in var if k not in BIG}
    rep_v["loss"] = jnp.ones((), F32)
    g_s, d_s, m_s, v_s = _adamw(small_all, _pack_small(rep), _pack_small(rep_m), _pack_small(rep_v), "adamw_small")
    g_s, d_s, m_s, v_s = _unpack_small(g_s), _unpack_small(d_s), _unpack_small(m_s), _unpack_small(v_s)
    for k in w:
        if k not in BIG:
            out_g[k], out_d[k], out_m[k], out_v[k] = g_s[k], d_s[k], m_s[k], v_s[k]
    return (g_s["loss"], grad_x[None], *[out_g[k] for k in WEIGHTS], *[out_d[k] for k in WEIGHTS],
            *[out_m[k] for k in WEIGHTS], *[out_v[k] for k in WEIGHTS])
```

```python
import functools

import numpy as np
import jax
import jax.numpy as jnp
from jax import lax
from jax.experimental import pallas as pl
from jax.experimental.pallas import tpu as pltpu
from jax.experimental.pallas import tpu_sc as plsc

F32 = jnp.float32
MM = jnp.bfloat16

D_MODEL = 1024
HEAD_DIM = 64
N_HEADS = 8
D_ATT = N_HEADS * HEAD_DIM
D_KV = 128
D_FF = 4096
D_PLE = 256
D_IN = 3 * D_ATT + N_HEADS + D_ATT + 2 * D_KV
N_DEV = 8
FF_CHUNK = D_FF // N_DEV
WINDOW = 128
N_BUCKETS = 32
MAX_DISTANCE = 128
RMS_EPS = 1e-6
Q_SCALE = HEAD_DIM ** -0.5
NEG = -1e30

ADAM_LR = 0.001
ADAM_B1 = 0.9
ADAM_B2 = 0.999
ADAM_EPS = 1e-08
ADAM_WD = 0.01
ADAM_STEP = 10

SLOT_HEAD = (0, 4, 1, 5, 2, 6, 3, 7)
HEAD_SLOT = (0, 2, 4, 6, 1, 3, 5, 7)

VMEM_LIMIT = 60 * 1024 * 1024
MESH = pl.DeviceIdType.MESH

NT = (((1,), (1,)), ((), ()))
TN = (((0,), (0,)), ((), ()))


def _params(*semantics):
    return pltpu.CompilerParams(dimension_semantics=semantics, vmem_limit_bytes=VMEM_LIMIT)


def _resident():
    return pl.BlockSpec(memory_space=pltpu.VMEM)


def _rows(tm, width):
    return pl.BlockSpec((tm, width), lambda i: (i, 0))


def _const(shape):
    return pl.BlockSpec(shape, lambda i: (0,) * len(shape))


def _dot(a, b):
    return jnp.dot(a, b, preferred_element_type=F32)


def _dot_nt(a, b):
    return lax.dot_general(a, b, NT, preferred_element_type=F32)


def _dot_tn(a, b):
    return lax.dot_general(a, b, TN, preferred_element_type=F32)


def _rms(xf):
    r = lax.rsqrt(jnp.mean(xf * xf, axis=-1, keepdims=True) + RMS_EPS)
    return xf * r, r


def _rms_bwd(dout, n, r, g):
    dg = jnp.sum(dout * n, axis=0, keepdims=True)
    dn = dout * g
    dx = r * (dn - n * jnp.mean(dn * n, axis=-1, keepdims=True))
    return dx, dg


def _run_after(after, body, in_specs, operands):
    if after is None:
        return body, list(in_specs), tuple(operands)
    n = len(operands)
    return ((lambda *refs: body(*refs[:n], *refs[n + 1:])), list(in_specs) + [pl.BlockSpec(memory_space=pl.ANY)],
            tuple(operands) + (after,))


def _accumulate(ref, value, step):
    @pl.when(step == 0)
    def _():
        ref[...] = value

    @pl.when(step != 0)
    def _():
        ref[...] += value


def _t5_bucket(n):
    max_exact = N_BUCKETS // 2
    large = max_exact + (np.log(np.maximum(n, 1) / max_exact) / np.log(MAX_DISTANCE / max_exact)
                         * (N_BUCKETS - max_exact)).astype(np.int32)
    large = np.minimum(large, N_BUCKETS - 1)
    return np.where(n < max_exact, n, large).astype(np.int32)


def _swa_bucket_map():
    i = np.arange(WINDOW)[:, None]
    j = np.arange(2 * WINDOW)[None, :]
    dist = i + WINDOW - j
    ok = (dist >= 0) & (dist < WINDOW)
    return np.where(ok, _t5_bucket(np.clip(dist, 0, None)), -1).astype(np.int32)


WT_FOX = 3 * D_ATT
WT_SQ = 16
WT_SKV = WT_SQ + D_ATT
WT_REST = WT_SKV + 2 * D_KV


def _pre_attn(x, g1, win_t, wt_rest, tm, after=None):
    S = x.shape[0]

    def body(x_ref, g_ref, wf_ref, wr_ref, a_ref, fqkv_ref, sqkv_ref, fft_ref):
        n, _ = _rms(x_ref[...])
        a = (n * g_ref[...]).astype(MM)
        a_ref[...] = a
        fqkv_ref[:, :D_ATT] = (_dot_nt(a, wf_ref[0:D_ATT]) * Q_SCALE).astype(MM)
        fqkv_ref[:, D_ATT:] = _dot_nt(a, wf_ref[D_ATT:WT_FOX]).astype(MM)
        sqkv_ref[:, :D_ATT] = (_dot_nt(a, wr_ref[WT_SQ:WT_SKV]) * Q_SCALE).astype(MM)
        sqkv_ref[:, D_ATT:] = _dot_nt(a, wr_ref[WT_SKV:WT_REST]).astype(MM)
        fft_ref[...] = _dot_nt(wr_ref[0:WT_SQ], a)

    body, in_specs, operands = _run_after(
        after, body, [_rows(tm, D_MODEL), _const((1, D_MODEL)), _const((WT_FOX, D_MODEL)), _resident()],
        (x, g1, win_t, wt_rest))
    return pl.pallas_call(
        body, name="pre_attn", grid=(S // tm,), in_specs=in_specs,
        out_specs=[_rows(tm, D_MODEL), _rows(tm, 3 * D_ATT), _rows(tm, D_ATT + 2 * D_KV),
                   pl.BlockSpec((16, tm), lambda i: (0, i))],
        out_shape=[jax.ShapeDtypeStruct((S, D_MODEL), MM), jax.ShapeDtypeStruct((S, 3 * D_ATT), MM),
                   jax.ShapeDtypeStruct((S, D_ATT + 2 * D_KV), MM), jax.ShapeDtypeStruct((16, S), F32)],
        compiler_params=_params("parallel"),
    )(*operands)


def _lane_scan(v, reverse):
    S = v.shape[1]
    lane = lax.broadcasted_iota(jnp.int32, v.shape, 1)
    k = 1
    while k < S:
        if reverse:
            v = v + jnp.where(lane < S - k, pltpu.roll(v, S - k, axis=1), 0.0)
        else:
            v = v + jnp.where(lane >= k, pltpu.roll(v, k, axis=1), 0.0)
        k *= 2
    return v


def _forget_cumsum(fft, bcol):
    def body(f_ref, b_ref, c_ref):
        z = f_ref[...] + b_ref[...]
        log_f = jnp.minimum(z, 0.0) - jnp.log1p(jnp.exp(-jnp.abs(z)))
        c_ref[...] = _lane_scan(log_f, reverse=False)

    return pl.pallas_call(
        body, name="forget_cumsum", out_shape=jax.ShapeDtypeStruct(fft.shape, F32),
        in_specs=[_resident(), _resident()], out_specs=_resident(),
    )(fft, bcol)


def _forget_bwd(dc_row, fft, bcol, a):
    def body(dc_ref, f_ref, b_ref, a_ref, dff_ref, db_ref, dw_ref):
        z = f_ref[...] + b_ref[...]
        dlog_f = _lane_scan(dc_ref[...], reverse=True)
        dff = dlog_f * (1.0 / (1.0 + jnp.exp(z)))
        dff_ref[...] = dff
        db_ref[...] = jnp.sum(dff, axis=1, keepdims=True)
        dw_ref[...] = _dot(dff.astype(MM), a_ref[...])

    return pl.pallas_call(
        body, name="forget_bwd",
        out_shape=[jax.ShapeDtypeStruct(fft.shape, F32), jax.ShapeDtypeStruct((fft.shape[0], 1), F32),
                   jax.ShapeDtypeStruct((fft.shape[0], D_MODEL), F32)],
        in_specs=[_resident()] * 4, out_specs=[_resident()] * 3,
    )(dc_row, fft, bcol, a)


def _head_select(shape, upper):
    lane = lax.broadcasted_iota(jnp.int32, shape, 1)
    return lane >= HEAD_DIM if upper else lane < HEAD_DIM


def _fox_fwd(fqkv, c_row3, tq, tk, pairs_per_loop=2, row_chunks=1):
    S = fqkv.shape[0]
    rq = tq // row_chunks
    n_band = tq // tk

    def body(q_ref, k_ref, v_ref, ck_ref, o_ref, lse_ref):
        qi = pl.program_id(0)
        row = lax.broadcasted_iota(jnp.int32, (rq, tk), 0)
        col = lax.broadcasted_iota(jnp.int32, (rq, tk), 1)
        low = _head_select((rq, 128), 0)
        for first in range(0, N_HEADS // 2, pairs_per_loop):
            pairs = range(first, first + pairs_per_loop)
            chains = [(pr, hh, rc) for pr in pairs for hh in range(2) for rc in range(row_chunks)]
            qh = {}
            for pr in pairs:
                for rc in range(row_chunks):
                    q2 = q_ref[rc * rq:(rc + 1) * rq, pr * 128:(pr + 1) * 128]
                    qh[pr, 0, rc] = jnp.where(low, q2, jnp.zeros_like(q2))
                    qh[pr, 1, rc] = jnp.where(low, jnp.zeros_like(q2), q2)

            def block(kb, carry, band, chains=chains, qh=qh):
                rows = pl.ds(pl.multiple_of(kb * tk, tk), tk)
                out = []
                for (pr, hh, rc), (m, l, acc) in zip(chains, carry):
                    if band is not None and (rc + 1) * rq <= band * tk:
                        out.append((m, l, acc))
                        continue
                    lanes = slice(pr * 128, (pr + 1) * 128)
                    s = _dot_nt(qh[pr, hh, rc], k_ref[rows, lanes]) - ck_ref[2 * pr + hh, pl.ds(kb, 1), :]
                    if band is not None:
                        s = jnp.where(row + rc * rq >= col + band * tk, s, NEG)
                    m_new = jnp.maximum(m, jnp.max(s, axis=-1, keepdims=True))
                    p = jnp.exp(s - m_new)
                    alpha = jnp.exp(m - m_new)
                    l = alpha * l + jnp.sum(p, axis=-1, keepdims=True)
                    acc = alpha * acc + _dot(p.astype(MM), v_ref[rows, lanes])
                    out.append((m_new, l, acc))
                return tuple(out)

            carry = tuple((jnp.full((rq, 1), NEG, F32), jnp.zeros((rq, 1), F32), jnp.zeros((rq, 128), F32))
                          for _ in chains)
            carry = lax.fori_loop(0, qi * n_band, functools.partial(block, band=None), carry)
            for band in range(n_band):
                carry = block(qi * n_band + band, carry, band=band)
            res = {}
            for (pr, hh, rc), (m, l, acc) in zip(chains, carry):
                res[pr, hh, rc] = acc / l
                lse_ref[rc * rq:(rc + 1) * rq, 2 * pr + hh:2 * pr + hh + 1] = m + jnp.log(l)
            for pr in pairs:
                for rc in range(row_chunks):
                    o_ref[rc * rq:(rc + 1) * rq, pr * 128:(pr + 1) * 128] = jnp.where(
                        low, res[pr, 0, rc], res[pr, 1, rc]).astype(MM)

    return pl.pallas_call(
        body, name="fox_fwd", grid=(S // tq,),
        in_specs=[pl.BlockSpec((tq, D_ATT), lambda i: (i, 0)), pl.BlockSpec((S, D_ATT), lambda i: (0, 1)),
                  pl.BlockSpec((S, D_ATT), lambda i: (0, 2)), _resident()],
        out_specs=[_rows(tq, D_ATT), _rows(tq, N_HEADS)],
        out_shape=[jax.ShapeDtypeStruct((S, D_ATT), MM), jax.ShapeDtypeStruct((S, N_HEADS), F32)],
        compiler_params=_params("parallel"),
    )(fqkv, fqkv, fqkv, c_row3)


def _swa_bias(rel_bias_slot, bucket):
    def body(rb_ref, bk_ref, out_ref):
        bk = bk_ref[...]
        for s in range(N_HEADS):
            acc = jnp.where(bk < 0, NEG, 0.0).astype(F32)
            for b in range(N_BUCKETS):
                acc = jnp.where(bk == b, rb_ref[b, s], acc)
            out_ref[s] = acc

    return pl.pallas_call(
        body, name="swa_bias", out_shape=jax.ShapeDtypeStruct((N_HEADS, WINDOW, 2 * WINDOW), F32),
        in_specs=[pl.BlockSpec(memory_space=pltpu.SMEM), _resident()], out_specs=_resident(),
    )(rel_bias_slot, bucket)


def _stack4(piece):
    return jnp.concatenate([piece(j) for j in range(4)], axis=0)


def _swa_specs(S):
    q = pl.BlockSpec((WINDOW, D_ATT), lambda n: (n, 0))
    kp = pl.BlockSpec((WINDOW, D_KV), lambda n: (jnp.maximum(n - 1, 0), 4))
    kc = pl.BlockSpec((WINDOW, D_KV), lambda n: (n, 4))
    vp = pl.BlockSpec((WINDOW, D_KV), lambda n: (jnp.maximum(n - 1, 0), 5))
    vc = pl.BlockSpec((WINDOW, D_KV), lambda n: (n, 5))
    return [q, kp, kc, vp, vc]


def _swa_fwd(sqkv, biasm, sinks_slot, after=None):
    S = sqkv.shape[0]

    def body(q_ref, kp_ref, kc_ref, vp_ref, vc_ref, bias_ref, sink_ref, o_ref, lse_ref):
        n = pl.program_id(0)
        no_prev = jnp.where(n > 0, 0.0, NEG)
        low = _head_select((WINDOW, 128), 0)
        res = []
        for g in range(2):
            sel = low if g == 0 else jnp.logical_not(low)
            qg = _stack4(lambda j: jnp.where(sel, q_ref[:, j * 128:(j + 1) * 128], jnp.zeros((WINDOW, 128), MM)))
            sink = _stack4(lambda j: jnp.full((WINDOW, 1), sink_ref[2 * j + g], F32))
            sp = _dot_nt(qg, kp_ref[...]) + _stack4(lambda j: bias_ref[2 * j + g, :, :WINDOW]) + no_prev
            sc = _dot_nt(qg, kc_ref[...]) + _stack4(lambda j: bias_ref[2 * j + g, :, WINDOW:])
            m = jnp.maximum(jnp.maximum(jnp.max(sp, axis=-1, keepdims=True),
                                        jnp.max(sc, axis=-1, keepdims=True)), sink)
            ep = jnp.exp(sp - m)
            ec = jnp.exp(sc - m)
            den = jnp.sum(ep, axis=-1, keepdims=True) + jnp.sum(ec, axis=-1, keepdims=True) + jnp.exp(sink - m)
            res.append((_dot(ep.astype(MM), vp_ref[...]) + _dot(ec.astype(MM), vc_ref[...])) / den)
            lse = m + jnp.log(den)
            for j in range(4):
                lse_ref[:, 2 * j + g:2 * j + g + 1] = lse[j * WINDOW:(j + 1) * WINDOW]
        for j in range(4):
            rows = slice(j * WINDOW, (j + 1) * WINDOW)
            o_ref[:, j * 128:(j + 1) * 128] = jnp.where(low, res[0][rows], res[1][rows]).astype(MM)

    body, in_specs, operands = _run_after(
        after, body, _swa_specs(S) + [_resident(), pl.BlockSpec(memory_space=pltpu.SMEM)],
        (sqkv, sqkv, sqkv, sqkv, sqkv, biasm, sinks_slot))
    return pl.pallas_call(
        body, name="swa_fwd", grid=(S // WINDOW,), in_specs=in_specs,
        out_specs=[_rows(WINDOW, D_ATT), _rows(WINDOW, N_HEADS)],
        out_shape=[jax.ShapeDtypeStruct((S, D_ATT), MM), jax.ShapeDtypeStruct((S, N_HEADS), F32)],
        compiler_params=_params("parallel"),
    )(*operands)


def _post_attn(x, fox_o, swa_o, wout_fox, wout_swa, g2, g3, tm):
    S = x.shape[0]

    def body(x_ref, fo_ref, so_ref, wf_ref, ws_ref, g2_ref, g3_ref, mix_ref, h1_ref, m_ref):
        mix = _dot(fo_ref[...], wf_ref[...]) + _dot(so_ref[...], ws_ref[...])
        mix_ref[...] = mix
        n2, _ = _rms(mix)
        h1 = x_ref[...] + n2 * g2_ref[...]
        h1_ref[...] = h1
        n3, _ = _rms(h1)
        m_ref[...] = (n3 * g3_ref[...]).astype(MM)

    return pl.pallas_call(
        body, name="post_attn", grid=(S // tm,),
        in_specs=[_rows(tm, D_MODEL), _rows(tm, D_ATT), _rows(tm, D_ATT), _resident(), _resident(),
                  _const((1, D_MODEL)), _const((1, D_MODEL))],
        out_specs=[_rows(tm, D_MODEL)] * 3,
        out_shape=[jax.ShapeDtypeStruct((S, D_MODEL), F32), jax.ShapeDtypeStruct((S, D_MODEL), F32),
                   jax.ShapeDtypeStruct((S, D_MODEL), MM)],
        compiler_params=_params("parallel"),
    )(x, fox_o, swa_o, wout_fox, wout_swa, g2, g3)


def _mlp_fwd(m, h1, w1, w2, g4, tm):
    S = m.shape[0]

    def body(m_ref, h1_ref, w1_ref, w2_ref, g4_ref, u_ref, y_ref, h2_ref):
        mb = m_ref[...]
        y = jnp.zeros((tm, D_MODEL), F32)
        for j in range(N_DEV):
            cols = slice(j * FF_CHUNK, (j + 1) * FF_CHUNK)
            u = _dot(mb, w1_ref[j])
            u_ref[:, cols] = u.astype(MM)
            y = y + _dot(jnp.square(jnp.maximum(u, 0.0)).astype(MM), w2_ref[cols, :])
        y_ref[...] = y
        n4, _ = _rms(y)
        h2_ref[...] = h1_ref[...] + n4 * g4_ref[...]

    return pl.pallas_call(
        body, name="mlp_fwd", grid=(S // tm,),
        in_specs=[_rows(tm, D_MODEL), _rows(tm, D_MODEL), _resident(), _resident(), _const((1, D_MODEL))],
        out_specs=[_rows(tm, D_FF), _rows(tm, D_MODEL), _rows(tm, D_MODEL)],
        out_shape=[jax.ShapeDtypeStruct((S, D_FF), MM), jax.ShapeDtypeStruct((S, D_MODEL), F32),
                   jax.ShapeDtypeStruct((S, D_MODEL), F32)],
        compiler_params=_params("parallel"),
    )(m, h1, w1, w2, g4)


def _ple_loss(h2, p, target, wg, wple, g5, tm):
    S = h2.shape[0]

    def body(h2_ref, p_ref, t_ref, wg_ref, wp_ref, g5_ref, dh2_ref, dpe_ref, dgl_ref, dg5_ref, loss_ref):
        i = pl.program_id(0)
        h2 = h2_ref[...]
        gate = jax.nn.sigmoid(_dot(h2.astype(MM), wg_ref[...]))
        pe = _dot(p_ref[...].astype(MM), wp_ref[...])
        n5, r5 = _rms(pe * gate)
        g5 = g5_ref[...]
        diff = h2 + n5 * g5 - t_ref[...]
        per_token = jnp.mean(jnp.square(diff), axis=-1, keepdims=True)
        _accumulate(loss_ref, 0.5 * jnp.sum(per_token, axis=0, keepdims=True), i)
        dh3 = diff * (1.0 / D_MODEL)
        de, dg5 = _rms_bwd(dh3, n5, r5, g5)
        _accumulate(dg5_ref, dg5, i)
        dpe_ref[...] = (de * gate).astype(MM)
        dgl = (de * pe * gate * (1.0 - gate)).astype(MM)
        dgl_ref[...] = dgl
        dh2_ref[...] = dh3 + _dot_nt(dgl, wg_ref[...])

    return pl.pallas_call(
        body, name="ple_loss", grid=(S // tm,),
        in_specs=[_rows(tm, D_MODEL), _rows(tm, D_PLE), _rows(tm, D_MODEL), _resident(), _resident(),
                  _const((1, D_MODEL))],
        out_specs=[_rows(tm, D_MODEL), _rows(tm, D_MODEL), _rows(tm, D_MODEL), _const((1, D_MODEL)), _const((1, 1))],
        out_shape=[jax.ShapeDtypeStruct((S, D_MODEL), F32), jax.ShapeDtypeStruct((S, D_MODEL), MM),
                   jax.ShapeDtypeStruct((S, D_MODEL), MM), jax.ShapeDtypeStruct((1, D_MODEL), F32),
                   jax.ShapeDtypeStruct((1, 1), F32)],
        compiler_params=_params("arbitrary"),
    )(h2, p, target, wg, wple, g5)


def _mlp_bwd(dh2, y, h1, u, w1, w2, g4, g3, tm):
    S = dh2.shape[0]

    def body(dh2_ref, y_ref, h1_ref, u_ref, w1_ref, w2_ref, g4_ref, g3_ref,
             dh1_ref, dy_ref, du_ref, dg4_ref, dg3_ref):
        i = pl.program_id(0)
        dh2 = dh2_ref[...]
        n4, r4 = _rms(y_ref[...])
        dy, dg4 = _rms_bwd(dh2, n4, r4, g4_ref[...])
        _accumulate(dg4_ref, dg4, i)
        dyb = dy.astype(MM)
        dy_ref[...] = dyb
        dm = jnp.zeros((tm, D_MODEL), F32)
        for j in range(N_DEV):
            cols = slice(j * FF_CHUNK, (j + 1) * FF_CHUNK)
            dact = _dot_nt(dyb, w2_ref[cols, :])
            du = (dact * (2.0 * jnp.maximum(u_ref[:, cols].astype(F32), 0.0))).astype(MM)
            du_ref[:, cols] = du
            dm = dm + _dot_nt(du, w1_ref[j])
        n3, r3 = _rms(h1_ref[...])
        dx, dg3 = _rms_bwd(dm, n3, r3, g3_ref[...])
        _accumulate(dg3_ref, dg3, i)
        dh1_ref[...] = dh2 + dx

    return pl.pallas_call(
        body, name="mlp_bwd", grid=(S // tm,),
        in_specs=[_rows(tm, D_MODEL), _rows(tm, D_MODEL), _rows(tm, D_MODEL), _rows(tm, D_FF),
                  _resident(), _resident(), _const((1, D_MODEL)), _const((1, D_MODEL))],
        out_specs=[_rows(tm, D_MODEL), _rows(tm, D_MODEL), _rows(tm, D_FF), _const((1, D_MODEL)),
                   _const((1, D_MODEL))],
        out_shape=[jax.ShapeDtypeStruct((S, D_MODEL), F32), jax.ShapeDtypeStruct((S, D_MODEL), MM),
                   jax.ShapeDtypeStruct((S, D_FF), MM), jax.ShapeDtypeStruct((1, D_MODEL), F32),
                   jax.ShapeDtypeStruct((1, D_MODEL), F32)],
        compiler_params=_params("arbitrary"),
    )(dh2, y, h1, u, w1, w2, g4, g3)


def _attn_out_bwd(dh1, mix, fox_o, swa_o, wout_fox, wout_swa, g2, head_rows, tm):
    S = dh1.shape[0]

    def body(dh1_ref, mix_ref, fo_ref, so_ref, wf_ref, ws_ref, g2_ref, er_ref,
             dmix_ref, dcat_ref, drow_ref, dswa_ref, dg2_ref):
        i = pl.program_id(0)
        n2, r2 = _rms(mix_ref[...])
        dmix, dg2 = _rms_bwd(dh1_ref[...], n2, r2, g2_ref[...])
        _accumulate(dg2_ref, dg2, i)
        dmb = dmix.astype(MM)
        dmix_ref[...] = dmb
        dfo = _dot_nt(dmb, wf_ref[...]).astype(MM)
        dso = _dot_nt(dmb, ws_ref[...]).astype(MM)
        dcat_ref[:, :D_ATT] = dfo
        dcat_ref[:, D_ATT:] = dso
        hi = lax.Precision.HIGHEST
        prod_f = dfo.astype(F32) * fo_ref[...].astype(F32)
        prod_s = dso.astype(F32) * so_ref[...].astype(F32)
        drow_ref[...] = lax.dot_general(er_ref[...], prod_f, NT, precision=hi, preferred_element_type=F32)
        dswa_ref[...] = lax.dot_general(er_ref[...], prod_s, NT, precision=hi, preferred_element_type=F32)

    return pl.pallas_call(
        body, name="attn_out_bwd", grid=(S // tm,),
        in_specs=[_rows(tm, D_MODEL), _rows(tm, D_MODEL), _rows(tm, D_ATT), _rows(tm, D_ATT), _resident(),
                  _resident(), _const((1, D_MODEL)), _resident()],
        out_specs=[_rows(tm, D_MODEL), _rows(tm, D_MODEL), pl.BlockSpec((N_HEADS, tm), lambda i: (0, i)),
                   pl.BlockSpec((N_HEADS, tm), lambda i: (0, i)), _const((1, D_MODEL))],
        out_shape=[jax.ShapeDtypeStruct((S, D_MODEL), MM), jax.ShapeDtypeStruct((S, D_MODEL), MM),
                   jax.ShapeDtypeStruct((N_HEADS, S), F32), jax.ShapeDtypeStruct((N_HEADS, S), F32),
                   jax.ShapeDtypeStruct((1, D_MODEL), F32)],
        compiler_params=_params("arbitrary"),
    )(dh1, mix, fox_o, swa_o, wout_fox, wout_swa, g2, head_rows)


def _fox_bwd(fqkv, dcat, lse_row3, d_row3, c_col, tq, tk, pairs_per_loop=2, after=None):
    S = fqkv.shape[0]
    n_blk = S // tk
    n_qblk = S // tq
    n_band = tk // tq

    def body(q_ref, k_ref, v_ref, do_ref, lse_ref, dd_ref, ck_ref, dq_ref, dk_ref, dv_ref, dc_ref, dcq_ref):
        kb = pl.program_id(0)

        @pl.when(kb == 0)
        def _():
            dq_ref[...] = jnp.zeros_like(dq_ref)
            dcq_ref[...] = jnp.zeros_like(dcq_ref)

        key = lax.broadcasted_iota(jnp.int32, (tk, tq), 0)
        qry = lax.broadcasted_iota(jnp.int32, (tk, tq), 1)
        low = _head_select((tk, 128), 0)
        for first in range(0, N_HEADS // 2, pairs_per_loop):
            pairs = range(first, first + pairs_per_loop)
            heads = [(pr, hh) for pr in pairs for hh in range(2)]
            kh, vh, ck = {}, {}, {}
            for pr in pairs:
                k2 = k_ref[:, pr * 128:(pr + 1) * 128]
                v2 = v_ref[:, pr * 128:(pr + 1) * 128]
                zero = jnp.zeros_like(k2)
                kh[pr, 0], kh[pr, 1] = jnp.where(low, k2, zero), jnp.where(low, zero, k2)
                vh[pr, 0], vh[pr, 1] = jnp.where(low, v2, zero), jnp.where(low, zero, v2)
                for hh in range(2):
                    ck[pr, hh] = ck_ref[:, 2 * pr + hh:2 * pr + hh + 1]

            def block(qb, carry, band, pairs=pairs, kh=kh, vh=vh, ck=ck):
                rows = pl.ds(pl.multiple_of(qb * tq, tq), tq)
                k1 = tk if band is None else (band + 1) * tq
                out = []
                it = iter(carry)
                for pr in pairs:
                    lanes = slice(pr * 128, (pr + 1) * 128)
                    q2 = q_ref[rows, lanes]
                    do2 = do_ref[rows, lanes]
                    dq = None
                    for hh in range(2):
                        h = 2 * pr + hh
                        dk, dv, dc = next(it)
                        s_t = _dot_nt(kh[pr, hh][:k1], q2) - ck[pr, hh][:k1]
                        p_t = jnp.exp(s_t - lse_ref[h, pl.ds(qb, 1), :])
                        if band is not None:
                            p_t = jnp.where(qry[:k1] + band * tq >= key[:k1], p_t, 0.0)
                        ds_t = p_t * (_dot_nt(vh[pr, hh][:k1], do2) - dd_ref[h, pl.ds(qb, 1), :])
                        dsb = ds_t.astype(MM)
                        dv_new = dv[:k1] + _dot(p_t.astype(MM), do2)
                        dk_new = dk[:k1] + _dot(dsb, q2)
                        dc_new = dc[:k1] - jnp.sum(ds_t, axis=1, keepdims=True)
                        if k1 < tk:
                            dv_new = jnp.concatenate([dv_new, dv[k1:]], axis=0)
                            dk_new = jnp.concatenate([dk_new, dk[k1:]], axis=0)
                            dc_new = jnp.concatenate([dc_new, dc[k1:]], axis=0)
                        part = _dot_tn(dsb, kh[pr, hh][:k1])
                        dq = part if dq is None else dq + part
                        dcq_ref[h, pl.ds(qb, 1), :] += jnp.sum(ds_t, axis=0, keepdims=True)
                        out.append((dk_new, dv_new, dc_new))
                    dq_ref[rows, lanes] += dq
                return tuple(out)

            carry = tuple((jnp.zeros((tk, 128), F32), jnp.zeros((tk, 128), F32), jnp.zeros((tk, 1), F32))
                          for _ in heads)
            for band in range(n_band):
                carry = block(kb * n_band + band, carry, band=band)
            carry = lax.fori_loop((kb + 1) * n_band, n_qblk, functools.partial(block, band=None), carry)
            grads = dict(zip(heads, carry))
            for pr in pairs:
                lanes = slice(pr * 128, (pr + 1) * 128)
                dk_ref[:, lanes] = jnp.where(low, grads[pr, 0][0], grads[pr, 1][0]).astype(MM)
                dv_ref[:, lanes] = jnp.where(low, grads[pr, 0][1], grads[pr, 1][1]).astype(MM)
                for hh in range(2):
                    dc_ref[:, 2 * pr + hh:2 * pr + hh + 1] = grads[pr, hh][2]

        @pl.when(kb == n_blk - 1)
        def _():
            dq_ref[...] = dq_ref[...] * Q_SCALE

    body, in_specs, operands = _run_after(
        after, body,
        [pl.BlockSpec((S, D_ATT), lambda i: (0, 0)), pl.BlockSpec((tk, D_ATT), lambda i: (i, 1)),
         pl.BlockSpec((tk, D_ATT), lambda i: (i, 2)), pl.BlockSpec((S, D_ATT), lambda i: (0, 0)),
         _resident(), _resident(), _rows(tk, N_HEADS)],
        (fqkv, fqkv, fqkv, dcat, lse_row3, d_row3, c_col))
    return pl.pallas_call(
        body, name="fox_bwd", grid=(n_blk,), in_specs=in_specs,
        out_specs=[_const((S, D_ATT)), _rows(tk, D_ATT), _rows(tk, D_ATT), _rows(tk, N_HEADS),
                   _const((N_HEADS, n_qblk, tq))],
        out_shape=[jax.ShapeDtypeStruct((S, D_ATT), F32), jax.ShapeDtypeStruct((S, D_ATT), MM),
                   jax.ShapeDtypeStruct((S, D_ATT), MM), jax.ShapeDtypeStruct((S, N_HEADS), F32),
                   jax.ShapeDtypeStruct((N_HEADS, n_qblk, tq), F32)],
        compiler_params=_params("arbitrary"),
    )(*operands)


def _swa_bwd(sqkv, dcat, biasm, sinks_slot, bucket, lse, d_col, after=None):
    S = sqkv.shape[0]
    n_blk = S // WINDOW

    def body(q_ref, kp_ref, kc_ref, vp_ref, vc_ref, do_ref, bias_ref, sink_ref, bk_ref, lse_ref, dd_ref,
             dq_ref, dk_ref, dv_ref, drb_ref, dsink_ref, ds_acc):
        n = pl.program_id(0)

        @pl.when(n == 0)
        def _():
            dk_ref[...] = jnp.zeros_like(dk_ref)
            dv_ref[...] = jnp.zeros_like(dv_ref)
            ds_acc[...] = jnp.zeros_like(ds_acc)
            dsink_ref[...] = jnp.zeros_like(dsink_ref)

        no_prev = jnp.where(n > 0, 0.0, NEG)
        prev = pl.ds(pl.multiple_of(jnp.maximum(n - 1, 0) * WINDOW, WINDOW), WINDOW)
        cur = pl.ds(pl.multiple_of(n * WINDOW, WINDOW), WINDOW)
        lane8 = lax.broadcasted_iota(jnp.int32, (1, N_HEADS), 1)
        dkp = jnp.zeros((WINDOW, D_KV), F32)
        dkc = jnp.zeros((WINDOW, D_KV), F32)
        dvp = jnp.zeros((WINDOW, D_KV), F32)
        dvc = jnp.zeros((WINDOW, D_KV), F32)
        dsink = jnp.zeros((1, N_HEADS), F32)
        low = _head_select((WINDOW, 128), 0)
        zero = jnp.zeros((WINDOW, 128), MM)
        dqs = []
        for g in range(2):
            sel = low if g == 0 else jnp.logical_not(low)
            qg = _stack4(lambda j: jnp.where(sel, q_ref[:, j * 128:(j + 1) * 128], zero))
            dog = _stack4(lambda j: jnp.where(sel, do_ref[:, j * 128:(j + 1) * 128], zero))
            lse_g = _stack4(lambda j: lse_ref[:, 2 * j + g:2 * j + g + 1])
            dd = _stack4(lambda j: dd_ref[:, 2 * j + g:2 * j + g + 1])
            sink = _stack4(lambda j: jnp.full((WINDOW, 1), sink_ref[2 * j + g], F32))
            pp = jnp.exp(_dot_nt(qg, kp_ref[...]) + _stack4(lambda j: bias_ref[2 * j + g, :, :WINDOW]) + no_prev - lse_g)
            pc = jnp.exp(_dot_nt(qg, kc_ref[...]) + _stack4(lambda j: bias_ref[2 * j + g, :, WINDOW:]) - lse_g)
            sink_term = jnp.exp(sink - lse_g) * dd
            dsp = pp * (_dot_nt(dog, vp_ref[...]) - dd)
            dsc = pc * (_dot_nt(dog, vc_ref[...]) - dd)
            for j in range(4):
                rows = slice(j * WINDOW, (j + 1) * WINDOW)
                dsink = dsink + jnp.where(lane8 == 2 * j + g, -jnp.sum(sink_term[rows]), 0.0)
                ds_acc[2 * j + g, :, :WINDOW] += dsp[rows]
                ds_acc[2 * j + g, :, WINDOW:] += dsc[rows]
            dspb, dscb = dsp.astype(MM), dsc.astype(MM)
            dqs.append(_dot(dspb, kp_ref[...]) + _dot(dscb, kc_ref[...]))
            dkp = dkp + _dot_tn(dspb, qg)
            dkc = dkc + _dot_tn(dscb, qg)
            dvp = dvp + _dot_tn(pp.astype(MM), dog)
            dvc = dvc + _dot_tn(pc.astype(MM), dog)
        for j in range(4):
            rows = slice(j * WINDOW, (j + 1) * WINDOW)
            dq_ref[:, j * 128:(j + 1) * 128] = (jnp.where(low, dqs[0][rows], dqs[1][rows]) * Q_SCALE).astype(MM)
        dk_ref[prev, :] += dkp
        dk_ref[cur, :] += dkc
        dv_ref[prev, :] += dvp
        dv_ref[cur, :] += dvc
        dsink_ref[...] += dsink

        @pl.when(n == n_blk - 1)
        def _():
            bk = bk_ref[...]
            rb = lax.broadcasted_iota(jnp.int32, (N_BUCKETS, N_HEADS), 0)
            cb = lax.broadcasted_iota(jnp.int32, (N_BUCKETS, N_HEADS), 1)
            out = jnp.zeros((N_BUCKETS, N_HEADS), F32)
            for s in range(N_HEADS):
                acc = ds_acc[s]
                for b in range(N_BUCKETS):
                    out = out + jnp.where((rb == b) & (cb == s), jnp.sum(jnp.where(bk == b, acc, 0.0)), 0.0)
            drb_ref[...] = out

    do_spec = pl.BlockSpec((WINDOW, D_ATT), lambda n: (n, 1))
    body, in_specs, operands = _run_after(
        after, body, _swa_specs(S) + [do_spec, _resident(), pl.BlockSpec(memory_space=pltpu.SMEM), _resident(),
                                      _rows(WINDOW, N_HEADS), _rows(WINDOW, N_HEADS)],
        (sqkv, sqkv, sqkv, sqkv, sqkv, dcat, biasm, sinks_slot, bucket, lse, d_col))
    return pl.pallas_call(
        body, name="swa_bwd", grid=(n_blk,), in_specs=in_specs,
        out_specs=[_rows(WINDOW, D_ATT), _const((S, D_KV)), _const((S, D_KV)), _const((N_BUCKETS, N_HEADS)),
                   _const((1, N_HEADS))],
        out_shape=[jax.ShapeDtypeStruct((S, D_ATT), MM), jax.ShapeDtypeStruct((S, D_KV), F32),
                   jax.ShapeDtypeStruct((S, D_KV), F32), jax.ShapeDtypeStruct((N_BUCKETS, N_HEADS), F32),
                   jax.ShapeDtypeStruct((1, N_HEADS), F32)],
        scratch_shapes=[pltpu.VMEM((N_HEADS, WINDOW, 2 * WINDOW), F32)],
        compiler_params=_params("arbitrary"),
    )(*operands)


D_Z = 3 * D_ATT + D_ATT + 2 * D_KV


def _pack_dz(dq_fox, dk_fox, dv_fox, dsq, dsk, dsv, tm):
    S = dq_fox.shape[0]

    def body(dq_ref, dk_ref, dv_ref, dsq_ref, dsk_ref, dsv_ref, dz_ref):
        dz_ref[:, 0:512] = dq_ref[...].astype(MM)
        dz_ref[:, 512:1024] = dk_ref[...]
        dz_ref[:, 1024:1536] = dv_ref[...]
        dz_ref[:, 1536:2048] = dsq_ref[...]
        dz_ref[:, 2048:2176] = dsk_ref[...].astype(MM)
        dz_ref[:, 2176:2304] = dsv_ref[...].astype(MM)

    return pl.pallas_call(
        body, name="pack_dz", grid=(S // tm,),
        in_specs=[_rows(tm, D_ATT), _rows(tm, D_ATT), _rows(tm, D_ATT), _rows(tm, D_ATT), _rows(tm, D_KV),
                  _rows(tm, D_KV)],
        out_specs=_rows(tm, D_Z), out_shape=jax.ShapeDtypeStruct((S, D_Z), MM),
        compiler_params=_params("parallel"),
    )(dq_fox, dk_fox, dv_fox, dsq, dsk, dsv)


def _pre_attn_bwd(x, dh1, dz, dff_t, win_t, wt_rest, g1, tm, after=None):
    S = x.shape[0]

    def body(x_ref, dh1_ref, dz_ref, dff_ref, wf_ref, wr_ref, g1_ref, dx_ref, dg1_ref):
        i = pl.program_id(0)
        da = (_dot(dz_ref[:, 0:WT_FOX], wf_ref[...]) + _dot(dz_ref[:, WT_FOX:D_Z], wr_ref[WT_SQ:WT_REST])
              + _dot_tn(dff_ref[...].astype(MM), wr_ref[0:WT_SQ]))
        n1, r1 = _rms(x_ref[...])
        dx, dg1 = _rms_bwd(da, n1, r1, g1_ref[...])
        _accumulate(dg1_ref, dg1, i)
        dx_ref[...] = dh1_ref[...] + dx

    body, in_specs, operands = _run_after(
        after, body,
        [_rows(tm, D_MODEL), _rows(tm, D_MODEL), _rows(tm, D_Z), pl.BlockSpec((16, tm), lambda i: (0, i)),
         _const((WT_FOX, D_MODEL)), _resident(), _const((1, D_MODEL))], (x, dh1, dz, dff_t, win_t, wt_rest, g1))
    return pl.pallas_call(
        body, name="pre_attn_bwd", grid=(S // tm,), in_specs=in_specs,
        out_specs=[_rows(tm, D_MODEL), _const((1, D_MODEL))],
        out_shape=[jax.ShapeDtypeStruct((S, D_MODEL), F32), jax.ShapeDtypeStruct((1, D_MODEL), F32)],
        compiler_params=_params("arbitrary"),
    )(*operands)


def _weight_grad(a, b, name, tk, n_chunks=1, relu2=False):
    S, K = a.shape
    N = b.shape[1]
    cn = N // n_chunks

    def body(a_ref, b_ref, out_ref):
        av = a_ref[...]
        if relu2:
            av = jnp.square(jnp.maximum(av.astype(F32), 0.0))
        av = av.astype(MM)
        for j in range(n_chunks):
            val = _dot_tn(av, b_ref[:, j * cn:(j + 1) * cn].astype(MM)).astype(MM)
            if n_chunks > 1:
                out_ref[j] = val
            else:
                out_ref[...] = val

    if n_chunks > 1:
        out_spec = pl.BlockSpec((n_chunks, tk, cn), lambda i: (0, i, 0))
        out_shape = jax.ShapeDtypeStruct((n_chunks, K, cn), MM)
    else:
        out_spec = pl.BlockSpec((tk, N), lambda i: (i, 0))
        out_shape = jax.ShapeDtypeStruct((K, N), MM)
    return pl.pallas_call(
        body, name=name, grid=(K // tk,),
        in_specs=[pl.BlockSpec((S, tk), lambda i: (0, i)), _resident()],
        out_specs=out_spec, out_shape=out_shape, compiler_params=_params("parallel"),
    )(a, b)


def _weight_grad_two(a1, a2, b, name, tk):
    S, K1 = a1.shape
    K2 = a2.shape[1]
    N = b.shape[1]
    n1 = K1 // tk

    def body(a1_ref, a2_ref, b_ref, out_ref):
        av = jnp.where(pl.program_id(0) < n1, a1_ref[...], a2_ref[...])
        out_ref[...] = _dot_tn(av, b_ref[...]).astype(MM)

    return pl.pallas_call(
        body, name=name, grid=((K1 + K2) // tk,),
        in_specs=[pl.BlockSpec((S, tk), lambda i: (0, jnp.minimum(i, n1 - 1))),
                  pl.BlockSpec((S, tk), lambda i: (0, jnp.maximum(i - n1, 0))), _resident()],
        out_specs=pl.BlockSpec((tk, N), lambda i: (i, 0)), out_shape=jax.ShapeDtypeStruct((K1 + K2, N), MM),
        compiler_params=_params("parallel"),
    )(a1, a2, b)


def _place():
    return lax.axis_index("x"), lax.axis_index("y"), lax.axis_index("c")


def _all_gather_sequencer(stacks, name, collective_id):
    refs = [jax.new_ref(s, memory_space=pltpu.MemorySpace.HBM) for s in stacks]
    n = len(refs)

    @pl.kernel(mesh=plsc.ScalarSubcoreMesh(axis_name="sequencer", num_cores=1), name=name,
               scratch_types=(pltpu.SemaphoreType.DMA((7 * n,)), pltpu.SemaphoreType.DMA((7 * n,))),
               compiler_params=pltpu.CompilerParams(collective_id=collective_id))
    def launch(send_sems, recv_sems):
        x, y, c = _place()
        sibling = (x, y, 1 - c)
        chips = [(1 - x, y), (x, 1 - y), (1 - x, 1 - y)]
        peers = [sibling] + [(px, py, c) for px, py in chips]
        barrier = pltpu.get_barrier_semaphore()
        for peer in peers:
            pl.semaphore_signal(barrier, inc=1, device_id=peer, device_id_type=MESH)
        pl.semaphore_wait(barrier, len(peers))

        def copy(a, k, block, to):
            px, py, pc = block
            slot = refs[a].at[4 * px + 2 * py + pc]
            return _remote(slot, slot, send_sems, recv_sems, 7 * a + k, to)

        first = [copy(a, k, (x, y, c), peer) for a in range(n) for k, peer in enumerate(peers)]
        for cp in first:
            cp.start()
        passed = []
        for j, (px, py) in enumerate(chips):
            for a in range(n):
                copy(a, 1 + j, (px, py, c), sibling).wait_recv()
                passed.append(copy(a, 4 + j, (px, py, c), sibling))
                passed[-1].start()
        for a in range(n):
            copy(a, 0, (x, y, 1 - c), sibling).wait_recv()
            for j, (px, py) in enumerate(chips):
                copy(a, 4 + j, (px, py, 1 - c), sibling).wait_recv()
        for cp in first + passed:
            cp.wait_send()

    launch()
    return [ref[...] for ref in refs]


def _chip_sums(grads, others, name):
    n = len(grads)

    def body(c_ref, *refs):
        for g_ref, o_ref, out_ref in zip(refs[:n], refs[n:2 * n], refs[2 * n:]):
            out_ref[...] = (g_ref[...].astype(F32) + o_ref[...].astype(F32)).astype(out_ref.dtype)

    own = [pl.BlockSpec((None, None) + g.shape[2:], lambda k, c_ref: (k, c_ref[0], 0, 0)) for g in grads]
    chip = [pl.BlockSpec((None,) + g.shape[2:], lambda k, c_ref: (k, 0, 0)) for g in grads]
    return pl.pallas_call(
        body, name=name,
        grid_spec=pltpu.PrefetchScalarGridSpec(num_scalar_prefetch=1, grid=(4,), in_specs=own + chip, out_specs=chip),
        out_shape=[jax.ShapeDtypeStruct((4,) + g.shape[2:], MM) for g in grads],
        compiler_params=_params("parallel"),
    )(lax.axis_index("c").astype(jnp.int32).reshape(1), *grads, *others)


HBM_SPEC = pl.BlockSpec(memory_space=pltpu.HBM)
SEM_SPEC = pl.BlockSpec(memory_space=pltpu.SEMAPHORE)
DATAFLOW = pltpu.SideEffectType.DATAFLOW_SIDE_EFFECTING


def _exchange_start(name, arrays, n_copies, plan):
    n = len(arrays)

    def body(*refs):
        send_sems, recv_sems, token = refs[n], refs[n + 1], refs[2 * n + 2]
        for cp in plan(refs[:n], send_sems, recv_sems):
            cp.start()
        token[...] = jnp.zeros_like(token)

    out = pl.pallas_call(
        body, name=name,
        out_shape=(pltpu.SemaphoreType.DMA((n_copies,)), pltpu.SemaphoreType.DMA((n_copies,)),
                   *[pltpu.HBM(a.shape, a.dtype) for a in arrays], jax.ShapeDtypeStruct((1, D_MODEL), F32)),
        in_specs=[HBM_SPEC] * n,
        out_specs=(SEM_SPEC, SEM_SPEC, *[HBM_SPEC] * n, pl.BlockSpec(memory_space=pltpu.VMEM)),
        input_output_aliases={i: 2 + i for i in range(n)},
        compiler_params=pltpu.CompilerParams(has_side_effects=DATAFLOW),
    )(*[pltpu.with_memory_space_constraint(a, pltpu.HBM) for a in arrays])
    return (out[0], out[1]), list(out[2:2 + n]), out[2 + n]


def _exchange_wait(name, arrays, sems, after, plan):
    n = len(arrays)
    after = list(after) if isinstance(after, (list, tuple)) else [after]

    def body(*refs):
        send_sems, recv_sems = refs[n], refs[n + 1]
        for cp in plan(refs[:n], send_sems, recv_sems):
            cp.wait_send()
            cp.wait_recv()

    out = pl.pallas_call(
        body, name=name, out_shape=[pltpu.HBM(a.shape, a.dtype) for a in arrays],
        in_specs=[HBM_SPEC] * n + [SEM_SPEC, SEM_SPEC] + [pl.BlockSpec(memory_space=pl.ANY)] * len(after),
        out_specs=[HBM_SPEC] * n, input_output_aliases={i: i for i in range(n)},
        compiler_params=pltpu.CompilerParams(has_side_effects=DATAFLOW),
    )(*arrays, sems[0], sems[1], *after)
    return list(out)


def _remote(src, dst, send_sems, recv_sems, k, to):
    return pltpu.make_async_remote_copy(src_ref=src, dst_ref=dst, send_sem=send_sems.at[k], recv_sem=recv_sems.at[k],
                                        device_id=to, device_id_type=MESH)


def _plan_gather_direct(refs, send_sems, recv_sems):
    x, y, c = _place()
    me = 4 * x + 2 * y + c
    peers = [(x, y, 1 - c), (1 - x, y, c), (x, 1 - y, c), (1 - x, 1 - y, c)]
    return [_remote(ref.at[me], ref.at[me], send_sems, recv_sems, 4 * a + k, peer)
            for a, ref in enumerate(refs) for k, peer in enumerate(peers)]


def _plan_gather_pass_on(refs, send_sems, recv_sems):
    x, y, c = _place()
    chips = [(1 - x, y), (x, 1 - y), (1 - x, 1 - y)]
    return [_remote(ref.at[4 * px + 2 * py + c], ref.at[4 * px + 2 * py + c], send_sems, recv_sems, 3 * a + k,
                    (x, y, 1 - c))
            for a, ref in enumerate(refs) for k, (px, py) in enumerate(chips)]


def _plan_in_chip(refs, send_sems, recv_sems):
    n = len(refs) // 2
    x, y, c = _place()
    return [_remote(refs[a].at[:, 1 - c], refs[n + a], send_sems, recv_sems, a, (x, y, 1 - c)) for a in range(n)]


def _plan_between_chips(refs, send_sems, recv_sems):
    n = len(refs) // 2
    x, y, c = _place()
    chips = [(1 - x, y), (x, 1 - y), (1 - x, 1 - y)]
    return [_remote(refs[a].at[2 * px + py], refs[n + a].at[2 * x + y], send_sems, recv_sems, 3 * a + k, (px, py, c))
            for a in range(n) for k, (px, py) in enumerate(chips)]


def _plan_late_between(refs, send_sems, recv_sems):
    sums, land, small = refs
    x, y, c = _place()
    me = 4 * x + 2 * y + c
    copies = _plan_between_chips([sums, land], send_sems, recv_sems)
    peers = [(x ^ dx, y ^ dy, c ^ dc) for dx in range(2) for dy in range(2) for dc in range(2) if dx + dy + dc]
    return copies + [_remote(small.at[me], small.at[me], send_sems, recv_sems, 3 + k, peer)
                     for k, peer in enumerate(peers)]


def _adamw_math(w, g, m, v):
    m = ADAM_B1 * m + (1.0 - ADAM_B1) * g
    v = ADAM_B2 * v + (1.0 - ADAM_B2) * jnp.square(g)
    m_hat = m / (1.0 - ADAM_B1 ** ADAM_STEP)
    v_hat = v / (1.0 - ADAM_B2 ** ADAM_STEP)
    delta = -ADAM_LR * (m_hat / (jnp.sqrt(v_hat) + ADAM_EPS) + ADAM_WD * w)
    return delta, m, v


def _adamw(parts, w, m, v, name):
    n_parts, r, cdim = parts.shape
    tr = 256 if r % 256 == 0 else r

    def body(p_ref, w_ref, m_ref, v_ref, g_out, d_out, m_out, v_out):
        g = p_ref[0].astype(F32)
        for k in range(1, n_parts):
            g = g + p_ref[k].astype(F32)
        delta, m_new, v_new = _adamw_math(w_ref[...], g, m_ref[...], v_ref[...])
        g_out[...] = g
        d_out[...] = delta
        m_out[...] = m_new
        v_out[...] = v_new

    blk = pl.BlockSpec((tr, cdim), lambda i: (i, 0))
    return pl.pallas_call(
        body, name=name, grid=(r // tr,),
        in_specs=[pl.BlockSpec((n_parts, tr, cdim), lambda i: (0, i, 0)), blk, blk, blk],
        out_specs=[blk] * 4, out_shape=[jax.ShapeDtypeStruct((r, cdim), F32)] * 4,
        compiler_params=_params("parallel"),
    )(parts, w, m, v)


def _adamw_chips(parts, sums, w, m, v, name):
    _, r, cdim = parts.shape
    tr = 256 if r % 256 == 0 else r

    def body(chip_ref, p_ref, own_ref, w_ref, m_ref, v_ref, g_out, d_out, m_out, v_out):
        g = None
        for k in range(4):
            term = jnp.where(chip_ref[0] == k, own_ref[...], p_ref[k]).astype(F32)
            g = term if g is None else g + term
        delta, m_new, v_new = _adamw_math(w_ref[...], g, m_ref[...], v_ref[...])
        g_out[...] = g
        d_out[...] = delta
        m_out[...] = m_new
        v_out[...] = v_new

    blk = pl.BlockSpec((tr, cdim), lambda i, chip: (i, 0))
    my_chip = (2 * lax.axis_index("x") + lax.axis_index("y")).astype(jnp.int32).reshape(1)
    return pl.pallas_call(
        body, name=name,
        grid_spec=pltpu.PrefetchScalarGridSpec(
            num_scalar_prefetch=1, grid=(r // tr,),
            in_specs=[pl.BlockSpec((4, tr, cdim), lambda i, chip: (0, i, 0)),
                      pl.BlockSpec((None, tr, cdim), lambda i, chip: (chip[0], i, 0)), blk, blk, blk],
            out_specs=[blk] * 4),
        out_shape=[jax.ShapeDtypeStruct((r, cdim), F32)] * 4,
        compiler_params=_params("parallel"),
    )(my_chip, parts, sums, w, m, v)


class _NoExchange:
    def __init__(self, weights):
        self.weights = weights

    def before_pre_attn(self):
        return None

    def after_fox_fwd(self, fox_o):
        return None

    def after_attention(self, swa_o):
        return self.weights

    def after_early_grads(self, grads):
        return None

    def after_swa_bwd(self, dsq):
        return None

    def after_w_in_grad(self, d_win):
        return None


def _slot_order(t, axis):
    shp = t.shape
    t = t.reshape(shp[:axis] + (2, 4, shp[axis] // N_HEADS) + shp[axis + 1:])
    return jnp.swapaxes(t, axis, axis + 1).reshape(shp)


def _head_order(t, axis):
    shp = t.shape
    t = t.reshape(shp[:axis] + (4, 2, shp[axis] // N_HEADS) + shp[axis + 1:])
    return jnp.swapaxes(t, axis, axis + 1).reshape(shp)


def _forward_backward(x, p, target, win_t, hooks, b_forget, rel_bias, sinks, g1, g2, g3, g4, g5):
    S = x.shape[0]
    tm = 512
    tm_mlp = 512
    t = 256
    q0 = 3 * D_ATT + N_HEADS
    win_t = win_t.reshape(D_IN, D_MODEL)
    wt_rest = jnp.concatenate(
        [win_t[WT_FOX:q0], jnp.zeros((8, D_MODEL), MM), _slot_order(win_t[q0:q0 + D_ATT], 0), win_t[q0 + D_ATT:]],
        axis=0)
    bcol = jnp.pad(b_forget.reshape(N_HEADS, 1), ((0, 8), (0, 0)))
    rel_bias_slot = rel_bias[:, np.array(SLOT_HEAD)]
    sinks_slot = sinks.reshape(N_HEADS)[np.array(SLOT_HEAD)]
    bucket = jnp.asarray(_swa_bucket_map())

    a, fqkv, sqkv, fft = _pre_attn(x, g1, win_t, wt_rest, tm, after=hooks.before_pre_attn())
    c_row = _forget_cumsum(fft, bcol)
    c_col = c_row[:N_HEADS].T
    c_row3 = c_row[:N_HEADS].reshape(N_HEADS, S // t, t)
    fox_o, fox_lse = _fox_fwd(fqkv, c_row3, tq=512, tk=t)
    biasm = _swa_bias(rel_bias_slot, bucket)
    swa_o, swa_lse = _swa_fwd(sqkv, biasm, sinks_slot, after=hooks.after_fox_fwd(fox_o))
    wout, w1, w2, wple, wg = hooks.after_attention(swa_o)
    wout_fox = wout[:D_ATT]
    wout_swa = _slot_order(wout[D_ATT:], 0)
    mix, h1, m = _post_attn(x, fox_o, swa_o, wout_fox, wout_swa, g2, g3, tm)
    u, y, h2 = _mlp_fwd(m, h1, w1, w2, g4, tm_mlp)
    dh2, dpe, dgl, dg5, loss = _ple_loss(h2, p, target, wg, wple, g5, tm)

    d_wple = _weight_grad(p, dpe, "grad_w_ple", tk=D_PLE, n_chunks=N_DEV)
    d_wg = _weight_grad(h2, dgl, "grad_w_ple_gate", tk=256)
    dh1, dy, du, dg4, dg3 = _mlp_bwd(dh2, y, h1, u, w1, w2, g4, g3, tm)
    d_w2 = _weight_grad(u, dy, "grad_w_ff2", tk=256, relu2=True)
    d_w1 = _weight_grad(m, du, "grad_w_ff1", tk=256, n_chunks=N_DEV)
    head = np.arange(D_ATT) // HEAD_DIM
    head_rows = jnp.asarray((head[None, :] == np.arange(N_HEADS)[:, None]).astype(np.float32))
    dmix, dcat, d_row, d_swa, dg2 = _attn_out_bwd(dh1, mix, fox_o, swa_o, wout_fox, wout_swa, g2, head_rows, tm)
    d_col = d_swa.T
    d_wout = _weight_grad_two(fox_o, swa_o, dmix, "grad_w_out", tk=256)
    d_wout = jnp.concatenate([d_wout[:D_ATT], _head_order(d_wout[D_ATT:], 0)], axis=0)
    d_wout = d_wout.reshape(N_DEV, D_MODEL // N_DEV, D_MODEL)
    early = dict(w_ff1=d_w1, w_ff2=d_w2.reshape(N_DEV, FF_CHUNK, D_MODEL), w_ple=d_wple,
                 w_ple_gate=d_wg.reshape(N_DEV, D_MODEL // N_DEV, D_MODEL), w_out=d_wout)

    dsq, dsk, dsv, d_rb_slot, d_sink_slot = _swa_bwd(sqkv, dcat, biasm, sinks_slot, bucket, swa_lse, d_col,
                                                     after=hooks.after_early_grads(early))
    lse_row3 = fox_lse.T.reshape(N_HEADS, S // t, t)
    d_row3 = d_row.reshape(N_HEADS, S // t, t)
    dq_fox, dk_fox, dv_fox, dc_col, dcq = _fox_bwd(fqkv, dcat, lse_row3, d_row3, c_col, tq=t, tk=512,
                                                  after=hooks.after_swa_bwd(dsq))
    dc_row = jnp.pad(dc_col.T + dcq.reshape(N_HEADS, S), ((0, 8), (0, 0)))
    dff_t, db, d_wff_t = _forget_bwd(dc_row, fft, bcol, a)
    dz = _pack_dz(dq_fox, dk_fox, dv_fox, dsq, dsk, dsv, 512)
    d_wmain = _weight_grad(dz, a, "grad_w_in", tk=256)

    sq0 = 3 * D_ATT
    d_win = jnp.concatenate(
        [d_wmain[:sq0], d_wff_t[:N_HEADS].astype(MM), _head_order(d_wmain[sq0:sq0 + D_ATT], 0),
         d_wmain[sq0 + D_ATT:]], axis=0)
    d_win = d_win.reshape(N_DEV, D_IN // N_DEV, D_MODEL)
    grad_x, dg1 = _pre_attn_bwd(x, dh1, dz, dff_t, win_t, wt_rest, g1, tm, after=hooks.after_w_in_grad(d_win))
    big = dict(early, w_in=d_win)
    small = dict(b_forget=db[:N_HEADS].reshape(1, N_HEADS), rel_bias=d_rb_slot[:, np.array(HEAD_SLOT)],
                 swa_sinks=d_sink_slot[:, np.array(HEAD_SLOT)], g_attn_pre=dg1, g_attn_post=dg2, g_ff_pre=dg3,
                 g_ff_post=dg4, g_ple_post=dg5)
    return loss, grad_x, big, small


BIG = ("w_in", "w_out", "w_ff1", "w_ff2", "w_ple", "w_ple_gate")
SMALL_ROWS = ("g_attn_pre", "g_attn_post", "g_ff_pre", "g_ff_post", "g_ple_post")
WEIGHTS =("w_in", "b_forget", "w_out", "rel_bias", "swa_sinks", "g_attn_pre", "g_attn_post", "w_ff1", "w_ff2",
           "g_ff_pre", "g_ff_post", "w_ple", "w_ple_gate", "g_ple_post")


EARLY = ("w_ff1", "w_ff2", "w_ple", "w_ple_gate", "w_out")


class _Overlap:
    def __init__(self, later):
        self.later = later

    def before_pre_attn(self):
        self.gather_sems, self.later, token = _exchange_start("gather_rest_start", self.later, 4 * 5, _plan_gather_direct)
        return token

    def after_fox_fwd(self, fox_o):
        later = _exchange_wait("gather_rest_wait", self.later, self.gather_sems, fox_o, _plan_gather_direct)
        self.pass_sems, self.later, token = _exchange_start("gather_pass_on_start", later, 3 * 5, _plan_gather_pass_on)
        return token

    def after_attention(self, swa_o):
        wout_g, w1_g, w2_g, wple_g, wg_g = _exchange_wait("gather_pass_on_wait", self.later, self.pass_sems, swa_o,
                                                         _plan_gather_pass_on)
        return (wout_g.reshape(D_MODEL, D_MODEL), w1_g, w2_g.reshape(D_FF, D_MODEL),
                jnp.moveaxis(wple_g, 0, 1).reshape(D_PLE, D_MODEL), wg_g.reshape(D_MODEL, D_MODEL))

    def after_early_grads(self, grads):
        views = [grads[k].reshape((4, 2) + grads[k].shape[1:]) for k in EARLY]
        lands = [lax.empty((4,) + grads[k].shape[1:], MM) for k in EARLY]
        self.in_chip_sems, self.in_chip, token = _exchange_start("grads_in_chip_start", views + lands, len(EARLY),
                                                                 _plan_in_chip)
        return token

    def after_swa_bwd(self, dsq):
        arrays = _exchange_wait("grads_in_chip_wait", self.in_chip, self.in_chip_sems, dsq, _plan_in_chip)
        n = len(EARLY)
        sums = list(_chip_sums(arrays[:n], arrays[n:], "chip_sums_early"))
        lands = [lax.empty(s.shape, s.dtype) for s in sums]
        self.between_sems, self.between, token = _exchange_start("grads_between_chips_start", sums + lands, 3 * n,
                                                                 _plan_between_chips)
        return token

    def after_w_in_grad(self, d_win):
        self.late_in_chip_sems, self.late_in_chip, token = _exchange_start(
            "late_in_chip_start", [d_win.reshape((4, 2) + d_win.shape[1:]), lax.empty((4,) + d_win.shape[1:], MM)],
            1, _plan_in_chip)
        return token

    def finish(self, after):
        arrays = _exchange_wait("grads_between_chips_wait", self.between, self.between_sems, after,
                                _plan_between_chips)
        n = len(EARLY)
        self.sums = arrays[:n]
        return arrays[n:]


def _pack_small(t):
    rows = [t[k].reshape(1, D_MODEL) for k in SMALL_ROWS]
    misc = jnp.concatenate([t["b_forget"].reshape(-1), t["swa_sinks"].reshape(-1), t["rel_bias"].reshape(-1)])
    rows.append(jnp.pad(misc, (0, D_MODEL - misc.shape[0])).reshape(1, D_MODEL))
    rows.append(jnp.pad(t["loss"].reshape(-1), (0, D_MODEL - 1)).reshape(1, D_MODEL))
    rows.append(jnp.zeros((1, D_MODEL), F32))
    return jnp.concatenate(rows, axis=0).astype(F32)


def _unpack_small(blk):
    out = {k: blk[i].reshape(1, D_MODEL) for i, k in enumerate(SMALL_ROWS)}
    misc = blk[len(SMALL_ROWS)]
    out["b_forget"] = misc[:N_HEADS].reshape(1, N_HEADS)
    out["swa_sinks"] = misc[N_HEADS:2 * N_HEADS].reshape(1, N_HEADS)
    out["rel_bias"] = misc[2 * N_HEADS:2 * N_HEADS + N_BUCKETS * N_HEADS].reshape(N_BUCKETS, N_HEADS)
    out["loss"] = blk[len(SMALL_ROWS) + 1, 0]
    return out


def kernel(x, p, w_in, b_forget, w_out, rel_bias, swa_sinks, g_attn_pre, g_attn_post, w_ff1, w_ff2, g_ff_pre, g_ff_post, w_ple, w_ple_gate, g_ple_post, loss_target, m_w_in, m_b_forget, m_w_out, m_rel_bias, m_swa_sinks, m_g_attn_pre, m_g_attn_post, m_w_ff1, m_w_ff2, m_g_ff_pre, m_g_ff_post, m_w_ple, m_w_ple_gate, m_g_ple_post, v_w_in, v_b_forget, v_w_out, v_rel_bias, v_swa_sinks, v_g_attn_pre, v_g_attn_post, v_w_ff1, v_w_ff2, v_g_ff_pre, v_g_ff_post, v_w_ple, v_w_ple_gate, v_g_ple_post):
    w = dict(w_in=w_in, b_forget=b_forget, w_out=w_out, rel_bias=rel_bias, swa_sinks=swa_sinks,
             g_attn_pre=g_attn_pre, g_attn_post=g_attn_post, w_ff1=w_ff1, w_ff2=w_ff2, g_ff_pre=g_ff_pre,
             g_ff_post=g_ff_post, w_ple=w_ple, w_ple_gate=w_ple_gate, g_ple_post=g_ple_post)
    mom = dict(w_in=m_w_in, b_forget=m_b_forget, w_out=m_w_out, rel_bias=m_rel_bias, swa_sinks=m_swa_sinks,
               g_attn_pre=m_g_attn_pre, g_attn_post=m_g_attn_post, w_ff1=m_w_ff1, w_ff2=m_w_ff2,
               g_ff_pre=m_g_ff_pre, g_ff_post=m_g_ff_post, w_ple=m_w_ple, w_ple_gate=m_w_ple_gate,
               g_ple_post=m_g_ple_post)
    var = dict(w_in=v_w_in, b_forget=v_b_forget, w_out=v_w_out, rel_bias=v_rel_bias, swa_sinks=v_swa_sinks,
               g_attn_pre=v_g_attn_pre, g_attn_post=v_g_attn_post, w_ff1=v_w_ff1, w_ff2=v_w_ff2,
               g_ff_pre=v_g_ff_pre, g_ff_post=v_g_ff_post, w_ple=v_w_ple, w_ple_gate=v_w_ple_gate,
               g_ple_post=v_g_ple_post)

    turn = lambda t, k: t.T if k == "w_in" else t
    me = 4 * lax.axis_index("x") + 2 * lax.axis_index("y") + lax.axis_index("c")

    def stack(block):
        return lax.dynamic_update_slice_in_dim(lax.empty((N_DEV,) + block.shape, block.dtype), block[None], me, 0)

    stacks = [stack(turn(w[k][0], k).astype(MM)) for k in BIG]
    (win_g,), later = _all_gather_sequencer(stacks[:1], "all_gather_sequencer", 1), stacks[1:]
    hooks = _Overlap(later)
    loss, grad_x, big, small = _forward_backward(
        x[0], p[0, 0], loss_target[0], win_g, hooks, b_forget, rel_bias, swa_sinks,
        g_attn_pre, g_attn_post, g_ff_pre, g_ff_post, g_ple_post)
    out_g, out_d, out_m, out_v = {}, {}, {}, {}

    def update(k, part, own):
        g, d, m_new, v_new = _adamw_chips(part, own, turn(w[k][0], k), turn(mom[k][0], k), turn(var[k][0], k),
                                          "adamw_" + k)
        out_g[k], out_d[k], out_m[k], out_v[k] = turn(g, k)[None], turn(d, k)[None], turn(m_new, k)[None], turn(v_new, k)[None]
        return d

    view, other = _exchange_wait("late_in_chip_wait", hooks.late_in_chip, hooks.late_in_chip_sems, grad_x,
                                 _plan_in_chip)
    (chip_sum,) = _chip_sums([view], [other], "chip_sum_w_in")
    small["loss"] = loss
    between_sems, between, token = _exchange_start(
        "late_between_chips_start", [chip_sum, lax.empty(chip_sum.shape, MM), stack(_pack_small(small))], 3 + 7,
        _plan_late_between)
    early_parts = hooks.finish(token)
    done = [update(k, part, own) for k, part, own in zip(EARLY, early_parts, hooks.sums)]
    chip_sum, part, small_all = _exchange_wait("late_between_chips_wait", between, between_sems, done,
                                               _plan_late_between)
    update("w_in", part, chip_sum)
    rep = {k: w[k] for k in w if k not in BIG}
    rep["loss"] = jnp.zeros((), F32)
    rep_m = {k: mom[k] for k in mom if k not in BIG}
    rep_m["loss"] = jnp.zeros((), F32)
    rep_v = {k: var[k] for k in var if k not in BIG}
    rep_v["loss"] = jnp.ones((), F32)
    g_s, d_s, m_s, v_s = _adamw(small_all, _pack_small(rep), _pack_small(rep_m), _pack_small(rep_v), "adamw_small")
    g_s, d_s, m_s, v_s = _unpack_small(g_s), _unpack_small(d_s), _unpack_small(m_s), _unpack_small(v_s)
    for k in w:
        if k not in BIG:
            out_g[k], out_d[k], out_m[k], out_v[k] = g_s[k], d_s[k], m_s[k], v_s[k]
    return (g_s["loss"], grad_x[None], *[out_g[k] for k in WEIGHTS], *[out_d[k] for k in WEIGHTS],
            *[out_m[k] for k in WEIGHTS], *[out_v[k] for k in WEIGHTS])
```

```python
import functools

import numpy as np
import jax
import jax.numpy as jnp
from jax import lax
from jax.experimental import pallas as pl
from jax.experimental.pallas import tpu as pltpu
from jax.experimental.pallas import tpu_sc as plsc

F32 = jnp.float32
MM = jnp.bfloat16

D_MODEL = 1024
HEAD_DIM = 64
N_HEADS = 8
D_ATT = N_HEADS * HEAD_DIM
D_KV = 128
D_FF = 4096
D_PLE = 256
D_IN = 3 * D_ATT + N_HEADS + D_ATT + 2 * D_KV
N_DEV = 8
FF_CHUNK = D_FF // N_DEV
WINDOW = 128
N_BUCKETS = 32
MAX_DISTANCE = 128
RMS_EPS = 1e-6
Q_SCALE = HEAD_DIM ** -0.5
NEG = -1e30

ADAM_LR = 0.001
ADAM_B1 = 0.9
ADAM_B2 = 0.999
ADAM_EPS = 1e-08
ADAM_WD = 0.01
ADAM_STEP = 10

SLOT_HEAD = (0, 4, 1, 5, 2, 6, 3, 7)
HEAD_SLOT = (0, 2, 4, 6, 1, 3, 5, 7)

VMEM_LIMIT = 60 * 1024 * 1024
MESH = pl.DeviceIdType.MESH

NT = (((1,), (1,)), ((), ()))
TN = (((0,), (0,)), ((), ()))


def _params(*semantics):
    return pltpu.CompilerParams(dimension_semantics=semantics, vmem_limit_bytes=VMEM_LIMIT)


def _resident():
    return pl.BlockSpec(memory_space=pltpu.VMEM)


def _rows(tm, width):
    return pl.BlockSpec((tm, width), lambda i: (i, 0))


def _const(shape):
    return pl.BlockSpec(shape, lambda i: (0,) * len(shape))


def _dot(a, b):
    return jnp.dot(a, b, preferred_element_type=F32)


def _dot_nt(a, b):
    return lax.dot_general(a, b, NT, preferred_element_type=F32)


def _dot_tn(a, b):
    return lax.dot_general(a, b, TN, preferred_element_type=F32)


def _rms(xf):
    r = lax.rsqrt(jnp.mean(xf * xf, axis=-1, keepdims=True) + RMS_EPS)
    return xf * r, r


def _rms_bwd(dout, n, r, g):
    dg = jnp.sum(dout * n, axis=0, keepdims=True)
    dn = dout * g
    dx = r * (dn - n * jnp.mean(dn * n, axis=-1, keepdims=True))
    return dx, dg


def _run_after(after, body, in_specs, operands):
    if after is None:
        return body, list(in_specs), tuple(operands)
    n = len(operands)
    return ((lambda *refs: body(*refs[:n], *refs[n + 1:])), list(in_specs) + [pl.BlockSpec(memory_space=pl.ANY)],
            tuple(operands) + (after,))


def _accumulate(ref, value, step):
    @pl.when(step == 0)
    def _():
        ref[...] = value

    @pl.when(step != 0)
    def _():
        ref[...] += value


def _t5_bucket(n):
    max_exact = N_BUCKETS // 2
    large = max_exact + (np.log(np.maximum(n, 1) / max_exact) / np.log(MAX_DISTANCE / max_exact)
                         * (N_BUCKETS - max_exact)).astype(np.int32)
    large = np.minimum(large, N_BUCKETS - 1)
    return np.where(n < max_exact, n, large).astype(np.int32)


def _swa_bucket_map():
    i = np.arange(WINDOW)[:, None]
    j = np.arange(2 * WINDOW)[None, :]
    dist = i + WINDOW - j
    ok = (dist >= 0) & (dist < WINDOW)
    return np.where(ok, _t5_bucket(np.clip(dist, 0, None)), -1).astype(np.int32)


WT_FOX = 3 * D_ATT
WT_SQ = 16
WT_SKV = WT_SQ + D_ATT
WT_REST = WT_SKV + 2 * D_KV


def _pre_attn(x, g1, win_t, wt_rest, tm, after=None):
    S = x.shape[0]

    def body(x_ref, g_ref, wf_ref, wr_ref, a_ref, fqkv_ref, sqkv_ref, fft_ref):
        n, _ = _rms(x_ref[...])
        a = (n * g_ref[...]).astype(MM)
        a_ref[...] = a
        fqkv_ref[:, :D_ATT] = (_dot_nt(a, wf_ref[0:D_ATT]) * Q_SCALE).astype(MM)
        fqkv_ref[:, D_ATT:] = _dot_nt(a, wf_ref[D_ATT:WT_FOX]).astype(MM)
        sqkv_ref[:, :D_ATT] = (_dot_nt(a, wr_ref[WT_SQ:WT_SKV]) * Q_SCALE).astype(MM)
        sqkv_ref[:, D_ATT:] = _dot_nt(a, wr_ref[WT_SKV:WT_REST]).astype(MM)
        fft_ref[...] = _dot_nt(wr_ref[0:WT_SQ], a)

    body, in_specs, operands = _run_after(
        after, body, [_rows(tm, D_MODEL), _const((1, D_MODEL)), _const((WT_FOX, D_MODEL)), _resident()],
        (x, g1, win_t, wt_rest))
    return pl.pallas_call(
        body, name="pre_attn", grid=(S // tm,), in_specs=in_specs,
        out_specs=[_rows(tm, D_MODEL), _rows(tm, 3 * D_ATT), _rows(tm, D_ATT + 2 * D_KV),
                   pl.BlockSpec((16, tm), lambda i: (0, i))],
        out_shape=[jax.ShapeDtypeStruct((S, D_MODEL), MM), jax.ShapeDtypeStruct((S, 3 * D_ATT), MM),
                   jax.ShapeDtypeStruct((S, D_ATT + 2 * D_KV), MM), jax.ShapeDtypeStruct((16, S), F32)],
        compiler_params=_params("parallel"),
    )(*operands)


def _lane_scan(v, reverse):
    S = v.shape[1]
    lane = lax.broadcasted_iota(jnp.int32, v.shape, 1)
    k = 1
    while k < S:
        if reverse:
            v = v + jnp.where(lane < S - k, pltpu.roll(v, S - k, axis=1), 0.0)
        else:
            v = v + jnp.where(lane >= k, pltpu.roll(v, k, axis=1), 0.0)
        k *= 2
    return v


def _forget_cumsum(fft, bcol):
    def body(f_ref, b_ref, c_ref):
        z = f_ref[...] + b_ref[...]
        log_f = jnp.minimum(z, 0.0) - jnp.log1p(jnp.exp(-jnp.abs(z)))
        c_ref[...] = _lane_scan(log_f, reverse=False)

    return pl.pallas_call(
        body, name="forget_cumsum", out_shape=jax.ShapeDtypeStruct(fft.shape, F32),
        in_specs=[_resident(), _resident()], out_specs=_resident(),
    )(fft, bcol)


def _forget_bwd(dc_row, fft, bcol, a):
    def body(dc_ref, f_ref, b_ref, a_ref, dff_ref, db_ref, dw_ref):
        z = f_ref[...] + b_ref[...]
        dlog_f = _lane_scan(dc_ref[...], reverse=True)
        dff = dlog_f * (1.0 / (1.0 + jnp.exp(z)))
        dff_ref[...] = dff
        db_ref[...] = jnp.sum(dff, axis=1, keepdims=True)
        dw_ref[...] = _dot(dff.astype(MM), a_ref[...])

    return pl.pallas_call(
        body, name="forget_bwd",
        out_shape=[jax.ShapeDtypeStruct(fft.shape, F32), jax.ShapeDtypeStruct((fft.shape[0], 1), F32),
                   jax.ShapeDtypeStruct((fft.shape[0], D_MODEL), F32)],
        in_specs=[_resident()] * 4, out_specs=[_resident()] * 3,
    )(dc_row, fft, bcol, a)


def _head_select(shape, upper):
    lane = lax.broadcasted_iota(jnp.int32, shape, 1)
    return lane >= HEAD_DIM if upper else lane < HEAD_DIM


def _fox_fwd(fqkv, c_row3, tq, tk, pairs_per_loop=2, row_chunks=1, after=None):
    S = fqkv.shape[0]
    rq = tq // row_chunks
    n_band = tq // tk

    def body(q_ref, k_ref, v_ref, ck_ref, o_ref, lse_ref):
        qi = pl.program_id(0)
        row = lax.broadcasted_iota(jnp.int32, (rq, tk), 0)
        col = lax.broadcasted_iota(jnp.int32, (rq, tk), 1)
        low = _head_select((rq, 128), 0)
        for first in range(0, N_HEADS // 2, pairs_per_loop):
            pairs = range(first, first + pairs_per_loop)
            chains = [(pr, hh, rc) for pr in pairs for hh in range(2) for rc in range(row_chunks)]
            qh = {}
            for pr in pairs:
                for rc in range(row_chunks):
                    q2 = q_ref[rc * rq:(rc + 1) * rq, pr * 128:(pr + 1) * 128]
                    qh[pr, 0, rc] = jnp.where(low, q2, jnp.zeros_like(q2))
                    qh[pr, 1, rc] = jnp.where(low, jnp.zeros_like(q2), q2)

            def block(kb, carry, band, chains=chains, qh=qh):
                rows = pl.ds(pl.multiple_of(kb * tk, tk), tk)
                out = []
                for (pr, hh, rc), (m, l, acc) in zip(chains, carry):
                    if band is not None and (rc + 1) * rq <= band * tk:
                        out.append((m, l, acc))
                        continue
                    lanes = slice(pr * 128, (pr + 1) * 128)
                    s = _dot_nt(qh[pr, hh, rc], k_ref[rows, lanes]) - ck_ref[2 * pr + hh, pl.ds(kb, 1), :]
                    if band is not None:
                        s = jnp.where(row + rc * rq >= col + band * tk, s, NEG)
                    m_new = jnp.maximum(m, jnp.max(s, axis=-1, keepdims=True))
                    p = jnp.exp(s - m_new)
                    alpha = jnp.exp(m - m_new)
                    l = alpha * l + jnp.sum(p, axis=-1, keepdims=True)
                    acc = alpha * acc + _dot(p.astype(MM), v_ref[rows, lanes])
                    out.append((m_new, l, acc))
                return tuple(out)

            carry = tuple((jnp.full((rq, 1), NEG, F32), jnp.zeros((rq, 1), F32), jnp.zeros((rq, 128), F32))
                          for _ in chains)
            carry = lax.fori_loop(0, qi * n_band, functools.partial(block, band=None), carry)
            for band in range(n_band):
                carry = block(qi * n_band + band, carry, band=band)
            res = {}
            for (pr, hh, rc), (m, l, acc) in zip(chains, carry):
                res[pr, hh, rc] = acc / l
                lse_ref[rc * rq:(rc + 1) * rq, 2 * pr + hh:2 * pr + hh + 1] = m + jnp.log(l)
            for pr in pairs:
                for rc in range(row_chunks):
                    o_ref[rc * rq:(rc + 1) * rq, pr * 128:(pr + 1) * 128] = jnp.where(
                        low, res[pr, 0, rc], res[pr, 1, rc]).astype(MM)

    body, in_specs, operands = _run_after(
        after, body, [pl.BlockSpec((tq, D_ATT), lambda i: (i, 0)), pl.BlockSpec((S, D_ATT), lambda i: (0, 1)),
                      pl.BlockSpec((S, D_ATT), lambda i: (0, 2)), _resident()], (fqkv, fqkv, fqkv, c_row3))
    return pl.pallas_call(
        body, name="fox_fwd", grid=(S // tq,), in_specs=in_specs,
        out_specs=[_rows(tq, D_ATT), _rows(tq, N_HEADS)],
        out_shape=[jax.ShapeDtypeStruct((S, D_ATT), MM), jax.ShapeDtypeStruct((S, N_HEADS), F32)],
        compiler_params=_params("parallel"),
    )(*operands)


def _swa_bias(rel_bias_slot, bucket):
    def body(rb_ref, bk_ref, out_ref):
        bk = bk_ref[...]
        for s in range(N_HEADS):
            acc = jnp.where(bk < 0, NEG, 0.0).astype(F32)
            for b in range(N_BUCKETS):
                acc = jnp.where(bk == b, rb_ref[b, s], acc)
            out_ref[s] = acc

    return pl.pallas_call(
        body, name="swa_bias", out_shape=jax.ShapeDtypeStruct((N_HEADS, WINDOW, 2 * WINDOW), F32),
        in_specs=[pl.BlockSpec(memory_space=pltpu.SMEM), _resident()], out_specs=_resident(),
    )(rel_bias_slot, bucket)


def _stack4(piece):
    return jnp.concatenate([piece(j) for j in range(4)], axis=0)


def _swa_specs(S):
    q = pl.BlockSpec((WINDOW, D_ATT), lambda n: (n, 0))
    kp = pl.BlockSpec((WINDOW, D_KV), lambda n: (jnp.maximum(n - 1, 0), 4))
    kc = pl.BlockSpec((WINDOW, D_KV), lambda n: (n, 4))
    vp = pl.BlockSpec((WINDOW, D_KV), lambda n: (jnp.maximum(n - 1, 0), 5))
    vc = pl.BlockSpec((WINDOW, D_KV), lambda n: (n, 5))
    return [q, kp, kc, vp, vc]


def _swa_fwd(sqkv, biasm, sinks_slot, after=None):
    S = sqkv.shape[0]

    def body(q_ref, kp_ref, kc_ref, vp_ref, vc_ref, bias_ref, sink_ref, o_ref, lse_ref):
        n = pl.program_id(0)
        no_prev = jnp.where(n > 0, 0.0, NEG)
        low = _head_select((WINDOW, 128), 0)
        res = []
        for g in range(2):
            sel = low if g == 0 else jnp.logical_not(low)
            qg = _stack4(lambda j: jnp.where(sel, q_ref[:, j * 128:(j + 1) * 128], jnp.zeros((WINDOW, 128), MM)))
            sink = _stack4(lambda j: jnp.full((WINDOW, 1), sink_ref[2 * j + g], F32))
            sp = _dot_nt(qg, kp_ref[...]) + _stack4(lambda j: bias_ref[2 * j + g, :, :WINDOW]) + no_prev
            sc = _dot_nt(qg, kc_ref[...]) + _stack4(lambda j: bias_ref[2 * j + g, :, WINDOW:])
            m = jnp.maximum(jnp.maximum(jnp.max(sp, axis=-1, keepdims=True),
                                        jnp.max(sc, axis=-1, keepdims=True)), sink)
            ep = jnp.exp(sp - m)
            ec = jnp.exp(sc - m)
            den = jnp.sum(ep, axis=-1, keepdims=True) + jnp.sum(ec, axis=-1, keepdims=True) + jnp.exp(sink - m)
            res.append((_dot(ep.astype(MM), vp_ref[...]) + _dot(ec.astype(MM), vc_ref[...])) / den)
            lse = m + jnp.log(den)
            for j in range(4):
                lse_ref[:, 2 * j + g:2 * j + g + 1] = lse[j * WINDOW:(j + 1) * WINDOW]
        for j in range(4):
            rows = slice(j * WINDOW, (j + 1) * WINDOW)
            o_ref[:, j * 128:(j + 1) * 128] = jnp.where(low, res[0][rows], res[1][rows]).astype(MM)

    body, in_specs, operands = _run_after(
        after, body, _swa_specs(S) + [_resident(), pl.BlockSpec(memory_space=pltpu.SMEM)],
        (sqkv, sqkv, sqkv, sqkv, sqkv, biasm, sinks_slot))
    return pl.pallas_call(
        body, name="swa_fwd", grid=(S // WINDOW,), in_specs=in_specs,
        out_specs=[_rows(WINDOW, D_ATT), _rows(WINDOW, N_HEADS)],
        out_shape=[jax.ShapeDtypeStruct((S, D_ATT), MM), jax.ShapeDtypeStruct((S, N_HEADS), F32)],
        compiler_params=_params("parallel"),
    )(*operands)


def _post_attn(x, fox_o, swa_o, wout_fox, wout_swa, g2, g3, tm):
    S = x.shape[0]

    def body(x_ref, fo_ref, so_ref, wf_ref, ws_ref, g2_ref, g3_ref, mix_ref, h1_ref, m_ref):
        mix = _dot(fo_ref[...], wf_ref[...]) + _dot(so_ref[...], ws_ref[...])
        mix_ref[...] = mix
        n2, _ = _rms(mix)
        h1 = x_ref[...] + n2 * g2_ref[...]
        h1_ref[...] = h1
        n3, _ = _rms(h1)
        m_ref[...] = (n3 * g3_ref[...]).astype(MM)

    return pl.pallas_call(
        body, name="post_attn", grid=(S // tm,),
        in_specs=[_rows(tm, D_MODEL), _rows(tm, D_ATT), _rows(tm, D_ATT), _resident(), _resident(),
                  _const((1, D_MODEL)), _const((1, D_MODEL))],
        out_specs=[_rows(tm, D_MODEL)] * 3,
        out_shape=[jax.ShapeDtypeStruct((S, D_MODEL), F32), jax.ShapeDtypeStruct((S, D_MODEL), F32),
                   jax.ShapeDtypeStruct((S, D_MODEL), MM)],
        compiler_params=_params("parallel"),
    )(x, fox_o, swa_o, wout_fox, wout_swa, g2, g3)


def _mlp_fwd(m, h1, w1, w2, g4, tm):
    S = m.shape[0]

    def body(m_ref, h1_ref, w1_ref, w2_ref, g4_ref, u_ref, y_ref, h2_ref):
        mb = m_ref[...]
        y = jnp.zeros((tm, D_MODEL), F32)
        for j in range(N_DEV):
            cols = slice(j * FF_CHUNK, (j + 1) * FF_CHUNK)
            u = _dot(mb, w1_ref[j])
            u_ref[:, cols] = u.astype(MM)
            y = y + _dot(jnp.square(jnp.maximum(u, 0.0)).astype(MM), w2_ref[cols, :])
        y_ref[...] = y
        n4, _ = _rms(y)
        h2_ref[...] = h1_ref[...] + n4 * g4_ref[...]

    return pl.pallas_call(
        body, name="mlp_fwd", grid=(S // tm,),
        in_specs=[_rows(tm, D_MODEL), _rows(tm, D_MODEL), _resident(), _resident(), _const((1, D_MODEL))],
        out_specs=[_rows(tm, D_FF), _rows(tm, D_MODEL), _rows(tm, D_MODEL)],
        out_shape=[jax.ShapeDtypeStruct((S, D_FF), MM), jax.ShapeDtypeStruct((S, D_MODEL), F32),
                   jax.ShapeDtypeStruct((S, D_MODEL), F32)],
        compiler_params=_params("parallel"),
    )(m, h1, w1, w2, g4)


def _ple_loss(h2, p, target, wg, wple, g5, tm):
    S = h2.shape[0]

    def body(h2_ref, p_ref, t_ref, wg_ref, wp_ref, g5_ref, dh2_ref, dpe_ref, dgl_ref, dg5_ref, loss_ref):
        i = pl.program_id(0)
        h2 = h2_ref[...]
        gate = jax.nn.sigmoid(_dot(h2.astype(MM), wg_ref[...]))
        pe = _dot(p_ref[...].astype(MM), wp_ref[...])
        n5, r5 = _rms(pe * gate)
        g5 = g5_ref[...]
        diff = h2 + n5 * g5 - t_ref[...]
        per_token = jnp.mean(jnp.square(diff), axis=-1, keepdims=True)
        _accumulate(loss_ref, 0.5 * jnp.sum(per_token, axis=0, keepdims=True), i)
        dh3 = diff * (1.0 / D_MODEL)
        de, dg5 = _rms_bwd(dh3, n5, r5, g5)
        _accumulate(dg5_ref, dg5, i)
        dpe_ref[...] = (de * gate).astype(MM)
        dgl = (de * pe * gate * (1.0 - gate)).astype(MM)
        dgl_ref[...] = dgl
        dh2_ref[...] = dh3 + _dot_nt(dgl, wg_ref[...])

    return pl.pallas_call(
        body, name="ple_loss", grid=(S // tm,),
        in_specs=[_rows(tm, D_MODEL), _rows(tm, D_PLE), _rows(tm, D_MODEL), _resident(), _resident(),
                  _const((1, D_MODEL))],
        out_specs=[_rows(tm, D_MODEL), _rows(tm, D_MODEL), _rows(tm, D_MODEL), _const((1, D_MODEL)), _const((1, 1))],
        out_shape=[jax.ShapeDtypeStruct((S, D_MODEL), F32), jax.ShapeDtypeStruct((S, D_MODEL), MM),
                   jax.ShapeDtypeStruct((S, D_MODEL), MM), jax.ShapeDtypeStruct((1, D_MODEL), F32),
                   jax.ShapeDtypeStruct((1, 1), F32)],
        compiler_params=_params("arbitrary"),
    )(h2, p, target, wg, wple, g5)


def _mlp_bwd(dh2, y, h1, u, w1, w2, g4, g3, tm):
    S = dh2.shape[0]

    def body(dh2_ref, y_ref, h1_ref, u_ref, w1_ref, w2_ref, g4_ref, g3_ref,
             dh1_ref, dy_ref, du_ref, dg4_ref, dg3_ref):
        i = pl.program_id(0)
        dh2 = dh2_ref[...]
        n4, r4 = _rms(y_ref[...])
        dy, dg4 = _rms_bwd(dh2, n4, r4, g4_ref[...])
        _accumulate(dg4_ref, dg4, i)
        dyb = dy.astype(MM)
        dy_ref[...] = dyb
        dm = jnp.zeros((tm, D_MODEL), F32)
        for j in range(N_DEV):
            cols = slice(j * FF_CHUNK, (j + 1) * FF_CHUNK)
            dact = _dot_nt(dyb, w2_ref[cols, :])
            du = (dact * (2.0 * jnp.maximum(u_ref[:, cols].astype(F32), 0.0))).astype(MM)
            du_ref[:, cols] = du
            dm = dm + _dot_nt(du, w1_ref[j])
        n3, r3 = _rms(h1_ref[...])
        dx, dg3 = _rms_bwd(dm, n3, r3, g3_ref[...])
        _accumulate(dg3_ref, dg3, i)
        dh1_ref[...] = dh2 + dx

    return pl.pallas_call(
        body, name="mlp_bwd", grid=(S // tm,),
        in_specs=[_rows(tm, D_MODEL), _rows(tm, D_MODEL), _rows(tm, D_MODEL), _rows(tm, D_FF),
                  _resident(), _resident(), _const((1, D_MODEL)), _const((1, D_MODEL))],
        out_specs=[_rows(tm, D_MODEL), _rows(tm, D_MODEL), _rows(tm, D_FF), _const((1, D_MODEL)),
                   _const((1, D_MODEL))],
        out_shape=[jax.ShapeDtypeStruct((S, D_MODEL), F32), jax.ShapeDtypeStruct((S, D_MODEL), MM),
                   jax.ShapeDtypeStruct((S, D_FF), MM), jax.ShapeDtypeStruct((1, D_MODEL), F32),
                   jax.ShapeDtypeStruct((1, D_MODEL), F32)],
        compiler_params=_params("arbitrary"),
    )(dh2, y, h1, u, w1, w2, g4, g3)


def _attn_out_bwd(dh1, mix, fox_o, swa_o, wout_fox, wout_swa, g2, head_rows, tm):
    S = dh1.shape[0]

    def body(dh1_ref, mix_ref, fo_ref, so_ref, wf_ref, ws_ref, g2_ref, er_ref,
             dmix_ref, dcat_ref, drow_ref, dswa_ref, dg2_ref):
        i = pl.program_id(0)
        n2, r2 = _rms(mix_ref[...])
        dmix, dg2 = _rms_bwd(dh1_ref[...], n2, r2, g2_ref[...])
        _accumulate(dg2_ref, dg2, i)
        dmb = dmix.astype(MM)
        dmix_ref[...] = dmb
        dfo = _dot_nt(dmb, wf_ref[...]).astype(MM)
        dso = _dot_nt(dmb, ws_ref[...]).astype(MM)
        dcat_ref[:, :D_ATT] = dfo
        dcat_ref[:, D_ATT:] = dso
        hi = lax.Precision.HIGHEST
        prod_f = dfo.astype(F32) * fo_ref[...].astype(F32)
        prod_s = dso.astype(F32) * so_ref[...].astype(F32)
        drow_ref[...] = lax.dot_general(er_ref[...], prod_f, NT, precision=hi, preferred_element_type=F32)
        dswa_ref[...] = lax.dot_general(er_ref[...], prod_s, NT, precision=hi, preferred_element_type=F32)

    return pl.pallas_call(
        body, name="attn_out_bwd", grid=(S // tm,),
        in_specs=[_rows(tm, D_MODEL), _rows(tm, D_MODEL), _rows(tm, D_ATT), _rows(tm, D_ATT), _resident(),
                  _resident(), _const((1, D_MODEL)), _resident()],
        out_specs=[_rows(tm, D_MODEL), _rows(tm, D_MODEL), pl.BlockSpec((N_HEADS, tm), lambda i: (0, i)),
                   pl.BlockSpec((N_HEADS, tm), lambda i: (0, i)), _const((1, D_MODEL))],
        out_shape=[jax.ShapeDtypeStruct((S, D_MODEL), MM), jax.ShapeDtypeStruct((S, D_MODEL), MM),
                   jax.ShapeDtypeStruct((N_HEADS, S), F32), jax.ShapeDtypeStruct((N_HEADS, S), F32),
                   jax.ShapeDtypeStruct((1, D_MODEL), F32)],
        compiler_params=_params("arbitrary"),
    )(dh1, mix, fox_o, swa_o, wout_fox, wout_swa, g2, head_rows)


def _fox_bwd(fqkv, dcat, lse_row3, d_row3, c_col, tq, tk, pairs_per_loop=2, after=None):
    S = fqkv.shape[0]
    n_blk = S // tk
    n_qblk = S // tq
    n_band = tk // tq

    def body(q_ref, k_ref, v_ref, do_ref, lse_ref, dd_ref, ck_ref, dq_ref, dk_ref, dv_ref, dc_ref, dcq_ref):
        kb = pl.program_id(0)

        @pl.when(kb == 0)
        def _():
            dq_ref[...] = jnp.zeros_like(dq_ref)
            dcq_ref[...] = jnp.zeros_like(dcq_ref)

        key = lax.broadcasted_iota(jnp.int32, (tk, tq), 0)
        qry = lax.broadcasted_iota(jnp.int32, (tk, tq), 1)
        low = _head_select((tk, 128), 0)
        for first in range(0, N_HEADS // 2, pairs_per_loop):
            pairs = range(first, first + pairs_per_loop)
            heads = [(pr, hh) for pr in pairs for hh in range(2)]
            kh, vh, ck = {}, {}, {}
            for pr in pairs:
                k2 = k_ref[:, pr * 128:(pr + 1) * 128]
                v2 = v_ref[:, pr * 128:(pr + 1) * 128]
                zero = jnp.zeros_like(k2)
                kh[pr, 0], kh[pr, 1] = jnp.where(low, k2, zero), jnp.where(low, zero, k2)
                vh[pr, 0], vh[pr, 1] = jnp.where(low, v2, zero), jnp.where(low, zero, v2)
                for hh in range(2):
                    ck[pr, hh] = ck_ref[:, 2 * pr + hh:2 * pr + hh + 1]

            def block(qb, carry, band, pairs=pairs, kh=kh, vh=vh, ck=ck):
                rows = pl.ds(pl.multiple_of(qb * tq, tq), tq)
                k1 = tk if band is None else (band + 1) * tq
                out = []
                it = iter(carry)
                for pr in pairs:
                    lanes = slice(pr * 128, (pr + 1) * 128)
                    q2 = q_ref[rows, lanes]
                    do2 = do_ref[rows, lanes]
                    dq = None
                    for hh in range(2):
                        h = 2 * pr + hh
                        dk, dv, dc = next(it)
                        s_t = _dot_nt(kh[pr, hh][:k1], q2) - ck[pr, hh][:k1]
                        p_t = jnp.exp(s_t - lse_ref[h, pl.ds(qb, 1), :])
                        if band is not None:
                            p_t = jnp.where(qry[:k1] + band * tq >= key[:k1], p_t, 0.0)
                        ds_t = p_t * (_dot_nt(vh[pr, hh][:k1], do2) - dd_ref[h, pl.ds(qb, 1), :])
                        dsb = ds_t.astype(MM)
                        dv_new = dv[:k1] + _dot(p_t.astype(MM), do2)
                        dk_new = dk[:k1] + _dot(dsb, q2)
                        dc_new = dc[:k1] - jnp.sum(ds_t, axis=1, keepdims=True)
                        if k1 < tk:
                            dv_new = jnp.concatenate([dv_new, dv[k1:]], axis=0)
                            dk_new = jnp.concatenate([dk_new, dk[k1:]], axis=0)
                            dc_new = jnp.concatenate([dc_new, dc[k1:]], axis=0)
                        part = _dot_tn(dsb, kh[pr, hh][:k1])
                        dq = part if dq is None else dq + part
                        dcq_ref[h, pl.ds(qb, 1), :] += jnp.sum(ds_t, axis=0, keepdims=True)
                        out.append((dk_new, dv_new, dc_new))
                    dq_ref[rows, lanes] += dq
                return tuple(out)

            carry = tuple((jnp.zeros((tk, 128), F32), jnp.zeros((tk, 128), F32), jnp.zeros((tk, 1), F32))
                          for _ in heads)
            for band in range(n_band):
                carry = block(kb * n_band + band, carry, band=band)
            carry = lax.fori_loop((kb + 1) * n_band, n_qblk, functools.partial(block, band=None), carry)
            grads = dict(zip(heads, carry))
            for pr in pairs:
                lanes = slice(pr * 128, (pr + 1) * 128)
                dk_ref[:, lanes] = jnp.where(low, grads[pr, 0][0], grads[pr, 1][0]).astype(MM)
                dv_ref[:, lanes] = jnp.where(low, grads[pr, 0][1], grads[pr, 1][1]).astype(MM)
                for hh in range(2):
                    dc_ref[:, 2 * pr + hh:2 * pr + hh + 1] = grads[pr, hh][2]

        @pl.when(kb == n_blk - 1)
        def _():
            dq_ref[...] = dq_ref[...] * Q_SCALE

    body, in_specs, operands = _run_after(
        after, body,
        [pl.BlockSpec((S, D_ATT), lambda i: (0, 0)), pl.BlockSpec((tk, D_ATT), lambda i: (i, 1)),
         pl.BlockSpec((tk, D_ATT), lambda i: (i, 2)), pl.BlockSpec((S, D_ATT), lambda i: (0, 0)),
         _resident(), _resident(), _rows(tk, N_HEADS)],
        (fqkv, fqkv, fqkv, dcat, lse_row3, d_row3, c_col))
    return pl.pallas_call(
        body, name="fox_bwd", grid=(n_blk,), in_specs=in_specs,
        out_specs=[_const((S, D_ATT)), _rows(tk, D_ATT), _rows(tk, D_ATT), _rows(tk, N_HEADS),
                   _const((N_HEADS, n_qblk, tq))],
        out_shape=[jax.ShapeDtypeStruct((S, D_ATT), F32), jax.ShapeDtypeStruct((S, D_ATT), MM),
                   jax.ShapeDtypeStruct((S, D_ATT), MM), jax.ShapeDtypeStruct((S, N_HEADS), F32),
                   jax.ShapeDtypeStruct((N_HEADS, n_qblk, tq), F32)],
        compiler_params=_params("arbitrary"),
    )(*operands)


def _swa_bwd(sqkv, dcat, biasm, sinks_slot, bucket, lse, d_col, after=None):
    S = sqkv.shape[0]
    n_blk = S // WINDOW

    def body(q_ref, kp_ref, kc_ref, vp_ref, vc_ref, do_ref, bias_ref, sink_ref, bk_ref, lse_ref, dd_ref,
             dq_ref, dk_ref, dv_ref, drb_ref, dsink_ref, ds_acc):
        n = pl.program_id(0)

        @pl.when(n == 0)
        def _():
            dk_ref[...] = jnp.zeros_like(dk_ref)
            dv_ref[...] = jnp.zeros_like(dv_ref)
            ds_acc[...] = jnp.zeros_like(ds_acc)
            dsink_ref[...] = jnp.zeros_like(dsink_ref)

        no_prev = jnp.where(n > 0, 0.0, NEG)
        prev = pl.ds(pl.multiple_of(jnp.maximum(n - 1, 0) * WINDOW, WINDOW), WINDOW)
        cur = pl.ds(pl.multiple_of(n * WINDOW, WINDOW), WINDOW)
        lane8 = lax.broadcasted_iota(jnp.int32, (1, N_HEADS), 1)
        dkp = jnp.zeros((WINDOW, D_KV), F32)
        dkc = jnp.zeros((WINDOW, D_KV), F32)
        dvp = jnp.zeros((WINDOW, D_KV), F32)
        dvc = jnp.zeros((WINDOW, D_KV), F32)
        dsink = jnp.zeros((1, N_HEADS), F32)
        low = _head_select((WINDOW, 128), 0)
        zero = jnp.zeros((WINDOW, 128), MM)
        dqs = []
        for g in range(2):
            sel = low if g == 0 else jnp.logical_not(low)
            qg = _stack4(lambda j: jnp.where(sel, q_ref[:, j * 128:(j + 1) * 128], zero))
            dog = _stack4(lambda j: jnp.where(sel, do_ref[:, j * 128:(j + 1) * 128], zero))
            lse_g = _stack4(lambda j: lse_ref[:, 2 * j + g:2 * j + g + 1])
            dd = _stack4(lambda j: dd_ref[:, 2 * j + g:2 * j + g + 1])
            sink = _stack4(lambda j: jnp.full((WINDOW, 1), sink_ref[2 * j + g], F32))
            pp = jnp.exp(_dot_nt(qg, kp_ref[...]) + _stack4(lambda j: bias_ref[2 * j + g, :, :WINDOW]) + no_prev - lse_g)
            pc = jnp.exp(_dot_nt(qg, kc_ref[...]) + _stack4(lambda j: bias_ref[2 * j + g, :, WINDOW:]) - lse_g)
            sink_term = jnp.exp(sink - lse_g) * dd
            dsp = pp * (_dot_nt(dog, vp_ref[...]) - dd)
            dsc = pc * (_dot_nt(dog, vc_ref[...]) - dd)
            for j in range(4):
                rows = slice(j * WINDOW, (j + 1) * WINDOW)
                dsink = dsink + jnp.where(lane8 == 2 * j + g, -jnp.sum(sink_term[rows]), 0.0)
                ds_acc[2 * j + g, :, :WINDOW] += dsp[rows]
                ds_acc[2 * j + g, :, WINDOW:] += dsc[rows]
            dspb, dscb = dsp.astype(MM), dsc.astype(MM)
            dqs.append(_dot(dspb, kp_ref[...]) + _dot(dscb, kc_ref[...]))
            dkp = dkp + _dot_tn(dspb, qg)
            dkc = dkc + _dot_tn(dscb, qg)
            dvp = dvp + _dot_tn(pp.astype(MM), dog)
            dvc = dvc + _dot_tn(pc.astype(MM), dog)
        for j in range(4):
            rows = slice(j * WINDOW, (j + 1) * WINDOW)
            dq_ref[:, j * 128:(j + 1) * 128] = (jnp.where(low, dqs[0][rows], dqs[1][rows]) * Q_SCALE).astype(MM)
        dk_ref[prev, :] += dkp
        dk_ref[cur, :] += dkc
        dv_ref[prev, :] += dvp
        dv_ref[cur, :] += dvc
        dsink_ref[...] += dsink

        @pl.when(n == n_blk - 1)
        def _():
            bk = bk_ref[...]
            rb = lax.broadcasted_iota(jnp.int32, (N_BUCKETS, N_HEADS), 0)
            cb = lax.broadcasted_iota(jnp.int32, (N_BUCKETS, N_HEADS), 1)
            out = jnp.zeros((N_BUCKETS, N_HEADS), F32)
            for s in range(N_HEADS):
                acc = ds_acc[s]
                for b in range(N_BUCKETS):
                    out = out + jnp.where((rb == b) & (cb == s), jnp.sum(jnp.where(bk == b, acc, 0.0)), 0.0)
            drb_ref[...] = out

    do_spec = pl.BlockSpec((WINDOW, D_ATT), lambda n: (n, 1))
    body, in_specs, operands = _run_after(
        after, body, _swa_specs(S) + [do_spec, _resident(), pl.BlockSpec(memory_space=pltpu.SMEM), _resident(),
                                      _rows(WINDOW, N_HEADS), _rows(WINDOW, N_HEADS)],
        (sqkv, sqkv, sqkv, sqkv, sqkv, dcat, biasm, sinks_slot, bucket, lse, d_col))
    return pl.pallas_call(
        body, name="swa_bwd", grid=(n_blk,), in_specs=in_specs,
        out_specs=[_rows(WINDOW, D_ATT), _const((S, D_KV)), _const((S, D_KV)), _const((N_BUCKETS, N_HEADS)),
                   _const((1, N_HEADS))],
        out_shape=[jax.ShapeDtypeStruct((S, D_ATT), MM), jax.ShapeDtypeStruct((S, D_KV), F32),
                   jax.ShapeDtypeStruct((S, D_KV), F32), jax.ShapeDtypeStruct((N_BUCKETS, N_HEADS), F32),
                   jax.ShapeDtypeStruct((1, N_HEADS), F32)],
        scratch_shapes=[pltpu.VMEM((N_HEADS, WINDOW, 2 * WINDOW), F32)],
        compiler_params=_params("arbitrary"),
    )(*operands)


D_Z = 3 * D_ATT + D_ATT + 2 * D_KV


def _pack_dz(dq_fox, dk_fox, dv_fox, dsq, dsk, dsv, tm):
    S = dq_fox.shape[0]

    def body(dq_ref, dk_ref, dv_ref, dsq_ref, dsk_ref, dsv_ref, dz_ref):
        dz_ref[:, 0:512] = dq_ref[...].astype(MM)
        dz_ref[:, 512:1024] = dk_ref[...]
        dz_ref[:, 1024:1536] = dv_ref[...]
        dz_ref[:, 1536:2048] = dsq_ref[...]
        dz_ref[:, 2048:2176] = dsk_ref[...].astype(MM)
        dz_ref[:, 2176:2304] = dsv_ref[...].astype(MM)

    return pl.pallas_call(
        body, name="pack_dz", grid=(S // tm,),
        in_specs=[_rows(tm, D_ATT), _rows(tm, D_ATT), _rows(tm, D_ATT), _rows(tm, D_ATT), _rows(tm, D_KV),
                  _rows(tm, D_KV)],
        out_specs=_rows(tm, D_Z), out_shape=jax.ShapeDtypeStruct((S, D_Z), MM),
        compiler_params=_params("parallel"),
    )(dq_fox, dk_fox, dv_fox, dsq, dsk, dsv)


def _pre_attn_bwd(x, dh1, dz, dff_t, win_t, wt_rest, g1, tm, after=None):
    S = x.shape[0]

    def body(x_ref, dh1_ref, dz_ref, dff_ref, wf_ref, wr_ref, g1_ref, dx_ref, dg1_ref):
        i = pl.program_id(0)
        da = (_dot(dz_ref[:, 0:WT_FOX], wf_ref[...]) + _dot(dz_ref[:, WT_FOX:D_Z], wr_ref[WT_SQ:WT_REST])
              + _dot_tn(dff_ref[...].astype(MM), wr_ref[0:WT_SQ]))
        n1, r1 = _rms(x_ref[...])
        dx, dg1 = _rms_bwd(da, n1, r1, g1_ref[...])
        _accumulate(dg1_ref, dg1, i)
        dx_ref[...] = dh1_ref[...] + dx

    body, in_specs, operands = _run_after(
        after, body,
        [_rows(tm, D_MODEL), _rows(tm, D_MODEL), _rows(tm, D_Z), pl.BlockSpec((16, tm), lambda i: (0, i)),
         _const((WT_FOX, D_MODEL)), _resident(), _const((1, D_MODEL))], (x, dh1, dz, dff_t, win_t, wt_rest, g1))
    return pl.pallas_call(
        body, name="pre_attn_bwd", grid=(S // tm,), in_specs=in_specs,
        out_specs=[_rows(tm, D_MODEL), _const((1, D_MODEL))],
        out_shape=[jax.ShapeDtypeStruct((S, D_MODEL), F32), jax.ShapeDtypeStruct((1, D_MODEL), F32)],
        compiler_params=_params("arbitrary"),
    )(*operands)


def _weight_grad(a, b, name, tk, n_chunks=1, relu2=False):
    S, K = a.shape
    N = b.shape[1]
    cn = N // n_chunks

    def body(a_ref, b_ref, out_ref):
        av = a_ref[...]
        if relu2:
            av = jnp.square(jnp.maximum(av.astype(F32), 0.0))
        av = av.astype(MM)
        for j in range(n_chunks):
            val = _dot_tn(av, b_ref[:, j * cn:(j + 1) * cn].astype(MM)).astype(MM)
            if n_chunks > 1:
                out_ref[j] = val
            else:
                out_ref[...] = val

    if n_chunks > 1:
        out_spec = pl.BlockSpec((n_chunks, tk, cn), lambda i: (0, i, 0))
        out_shape = jax.ShapeDtypeStruct((n_chunks, K, cn), MM)
    else:
        out_spec = pl.BlockSpec((tk, N), lambda i: (i, 0))
        out_shape = jax.ShapeDtypeStruct((K, N), MM)
    return pl.pallas_call(
        body, name=name, grid=(K // tk,),
        in_specs=[pl.BlockSpec((S, tk), lambda i: (0, i)), _resident()],
        out_specs=out_spec, out_shape=out_shape, compiler_params=_params("parallel"),
    )(a, b)


def _weight_grad_two(a1, a2, b, name, tk):
    S, K1 = a1.shape
    K2 = a2.shape[1]
    N = b.shape[1]
    n1 = K1 // tk

    def body(a1_ref, a2_ref, b_ref, out_ref):
        av = jnp.where(pl.program_id(0) < n1, a1_ref[...], a2_ref[...])
        out_ref[...] = _dot_tn(av, b_ref[...]).astype(MM)

    return pl.pallas_call(
        body, name=name, grid=((K1 + K2) // tk,),
        in_specs=[pl.BlockSpec((S, tk), lambda i: (0, jnp.minimum(i, n1 - 1))),
                  pl.BlockSpec((S, tk), lambda i: (0, jnp.maximum(i - n1, 0))), _resident()],
        out_specs=pl.BlockSpec((tk, N), lambda i: (i, 0)), out_shape=jax.ShapeDtypeStruct((K1 + K2, N), MM),
        compiler_params=_params("parallel"),
    )(a1, a2, b)


def _place():
    return lax.axis_index("x"), lax.axis_index("y"), lax.axis_index("c")


def _all_gather_sequencer(stacks, name, collective_id):
    refs = [jax.new_ref(s, memory_space=pltpu.MemorySpace.HBM) for s in stacks]
    n = len(refs)

    @pl.kernel(mesh=plsc.ScalarSubcoreMesh(axis_name="sequencer", num_cores=1), name=name,
               scratch_types=(pltpu.SemaphoreType.DMA((7 * n,)), pltpu.SemaphoreType.DMA((7 * n,))),
               compiler_params=pltpu.CompilerParams(collective_id=collective_id))
    def launch(send_sems, recv_sems):
        x, y, c = _place()
        sibling = (x, y, 1 - c)
        chips = [(1 - x, y), (x, 1 - y), (1 - x, 1 - y)]
        peers = [sibling] + [(px, py, c) for px, py in chips]
        barrier = pltpu.get_barrier_semaphore()
        for peer in peers:
            pl.semaphore_signal(barrier, inc=1, device_id=peer, device_id_type=MESH)
        pl.semaphore_wait(barrier, len(peers))

        def copy(a, k, block, to):
            px, py, pc = block
            slot = refs[a].at[4 * px + 2 * py + pc]
            return _remote(slot, slot, send_sems, recv_sems, 7 * a + k, to)

        first = [copy(a, k, (x, y, c), peer) for a in range(n) for k, peer in enumerate(peers)]
        for cp in first:
            cp.start()
        passed = []
        for j, (px, py) in enumerate(chips):
            for a in range(n):
                copy(a, 1 + j, (px, py, c), sibling).wait_recv()
                passed.append(copy(a, 4 + j, (px, py, c), sibling))
                passed[-1].start()
        for a in range(n):
            copy(a, 0, (x, y, 1 - c), sibling).wait_recv()
            for j, (px, py) in enumerate(chips):
                copy(a, 4 + j, (px, py, 1 - c), sibling).wait_recv()
        for cp in first + passed:
            cp.wait_send()

    launch()
    return [ref[...] for ref in refs]


def _chip_sums(grads, others, name):
    n = len(grads)

    def body(c_ref, *refs):
        for g_ref, o_ref, out_ref in zip(refs[:n], refs[n:2 * n], refs[2 * n:]):
            out_ref[...] = (g_ref[...].astype(F32) + o_ref[...].astype(F32)).astype(out_ref.dtype)

    own = [pl.BlockSpec((None, None) + g.shape[2:], lambda k, c_ref: (k, c_ref[0], 0, 0)) for g in grads]
    chip = [pl.BlockSpec((None,) + g.shape[2:], lambda k, c_ref: (k, 0, 0)) for g in grads]
    return pl.pallas_call(
        body, name=name,
        grid_spec=pltpu.PrefetchScalarGridSpec(num_scalar_prefetch=1, grid=(4,), in_specs=own + chip, out_specs=chip),
        out_shape=[jax.ShapeDtypeStruct((4,) + g.shape[2:], MM) for g in grads],
        compiler_params=_params("parallel"),
    )(lax.axis_index("c").astype(jnp.int32).reshape(1), *grads, *others)


HBM_SPEC = pl.BlockSpec(memory_space=pltpu.HBM)
SEM_SPEC = pl.BlockSpec(memory_space=pltpu.SEMAPHORE)
DATAFLOW = pltpu.SideEffectType.DATAFLOW_SIDE_EFFECTING


def _exchange_start(name, arrays, n_copies, plan):
    n = len(arrays)

    def body(*refs):
        send_sems, recv_sems, token = refs[n], refs[n + 1], refs[2 * n + 2]
        for cp in plan(refs[:n], send_sems, recv_sems):
            cp.start()
        token[...] = jnp.zeros_like(token)

    out = pl.pallas_call(
        body, name=name,
        out_shape=(pltpu.SemaphoreType.DMA((n_copies,)), pltpu.SemaphoreType.DMA((n_copies,)),
                   *[pltpu.HBM(a.shape, a.dtype) for a in arrays], jax.ShapeDtypeStruct((1, D_MODEL), F32)),
        in_specs=[HBM_SPEC] * n,
        out_specs=(SEM_SPEC, SEM_SPEC, *[HBM_SPEC] * n, pl.BlockSpec(memory_space=pltpu.VMEM)),
        input_output_aliases={i: 2 + i for i in range(n)},
        compiler_params=pltpu.CompilerParams(has_side_effects=DATAFLOW),
    )(*[pltpu.with_memory_space_constraint(a, pltpu.HBM) for a in arrays])
    return (out[0], out[1]), list(out[2:2 + n]), out[2 + n]


def _exchange_wait(name, arrays, sems, after, plan):
    n = len(arrays)
    after = list(after) if isinstance(after, (list, tuple)) else [after]

    def body(*refs):
        send_sems, recv_sems = refs[n], refs[n + 1]
        for cp in plan(refs[:n], send_sems, recv_sems):
            cp.wait_send()
            cp.wait_recv()

    out = pl.pallas_call(
        body, name=name, out_shape=[pltpu.HBM(a.shape, a.dtype) for a in arrays],
        in_specs=[HBM_SPEC] * n + [SEM_SPEC, SEM_SPEC] + [pl.BlockSpec(memory_space=pl.ANY)] * len(after),
        out_specs=[HBM_SPEC] * n, input_output_aliases={i: i for i in range(n)},
        compiler_params=pltpu.CompilerParams(has_side_effects=DATAFLOW),
    )(*arrays, sems[0], sems[1], *after)
    return list(out)


def _remote(src, dst, send_sems, recv_sems, k, to):
    return pltpu.make_async_remote_copy(src_ref=src, dst_ref=dst, send_sem=send_sems.at[k], recv_sem=recv_sems.at[k],
                                        device_id=to, device_id_type=MESH)


def _plan_gather_near(refs, send_sems, recv_sems):
    x, y, c = _place()
    me = 4 * x + 2 * y + c
    peers = [(x, y, 1 - c), (1 - x, y, c), (x, 1 - y, c)]
    return [_remote(ref.at[me], ref.at[me], send_sems, recv_sems, 3 * a + k, peer)
            for a, ref in enumerate(refs) for k, peer in enumerate(peers)]


def _plan_gather_far(refs, send_sems, recv_sems):
    x, y, c = _place()
    south = 1 - c
    from_x, from_y = x + south - 2 * x * south, y + c - 2 * y * c
    to_x, to_y = x + c - 2 * x * c, y + south - 2 * y * south
    carried = 4 * from_x + 2 * from_y + c
    copies = []
    for a, ref in enumerate(refs):
        copies.append(_remote(ref.at[carried], ref.at[carried], send_sems, recv_sems, 3 * a, (to_x, to_y, c)))
        for k, (px, py) in enumerate([(1 - x, y), (x, 1 - y)]):
            block = 4 * px + 2 * py + c
            copies.append(_remote(ref.at[block], ref.at[block], send_sems, recv_sems, 3 * a + 1 + k, (x, y, 1 - c)))
    return copies


def _plan_gather_last(refs, send_sems, recv_sems):
    x, y, c = _place()
    block = 4 * (1 - x) + 2 * (1 - y) + c
    return [_remote(ref.at[block], ref.at[block], send_sems, recv_sems, a, (x, y, 1 - c))
            for a, ref in enumerate(refs)]


def _plan_in_chip(refs, send_sems, recv_sems):
    n = len(refs) // 2
    x, y, c = _place()
    return [_remote(refs[a].at[:, 1 - c], refs[n + a], send_sems, recv_sems, a, (x, y, 1 - c)) for a in range(n)]


def _plan_between_chips(refs, send_sems, recv_sems):
    n = len(refs) // 2
    x, y, c = _place()
    chips = [(1 - x, y), (x, 1 - y), (1 - x, 1 - y)]
    return [_remote(refs[a].at[2 * px + py], refs[n + a].at[2 * x + y], send_sems, recv_sems, 3 * a + k, (px, py, c))
            for a in range(n) for k, (px, py) in enumerate(chips)]


def _plan_late_between(refs, send_sems, recv_sems):
    sums, land, small = refs
    x, y, c = _place()
    me = 4 * x + 2 * y + c
    copies = _plan_between_chips([sums, land], send_sems, recv_sems)
    peers = [(x ^ dx, y ^ dy, c ^ dc) for dx in range(2) for dy in range(2) for dc in range(2) if dx + dy + dc]
    return copies + [_remote(small.at[me], small.at[me], send_sems, recv_sems, 3 + k, peer)
                     for k, peer in enumerate(peers)]


def _adamw_math(w, g, m, v):
    m = ADAM_B1 * m + (1.0 - ADAM_B1) * g
    v = ADAM_B2 * v + (1.0 - ADAM_B2) * jnp.square(g)
    m_hat = m / (1.0 - ADAM_B1 ** ADAM_STEP)
    v_hat = v / (1.0 - ADAM_B2 ** ADAM_STEP)
    delta = -ADAM_LR * (m_hat / (jnp.sqrt(v_hat) + ADAM_EPS) + ADAM_WD * w)
    return delta, m, v


def _adamw(parts, w, m, v, name):
    n_parts, r, cdim = parts.shape
    tr = 256 if r % 256 == 0 else r

    def body(p_ref, w_ref, m_ref, v_ref, g_out, d_out, m_out, v_out):
        g = p_ref[0].astype(F32)
        for k in range(1, n_parts):
            g = g + p_ref[k].astype(F32)
        delta, m_new, v_new = _adamw_math(w_ref[...], g, m_ref[...], v_ref[...])
        g_out[...] = g
        d_out[...] = delta
        m_out[...] = m_new
        v_out[...] = v_new

    blk = pl.BlockSpec((tr, cdim), lambda i: (i, 0))
    return pl.pallas_call(
        body, name=name, grid=(r // tr,),
        in_specs=[pl.BlockSpec((n_parts, tr, cdim), lambda i: (0, i, 0)), blk, blk, blk],
        out_specs=[blk] * 4, out_shape=[jax.ShapeDtypeStruct((r, cdim), F32)] * 4,
        compiler_params=_params("parallel"),
    )(parts, w, m, v)


def _adamw_chips(parts, sums, w, m, v, name):
    _, r, cdim = parts.shape
    tr = 256 if r % 256 == 0 else r

    def body(chip_ref, p_ref, own_ref, w_ref, m_ref, v_ref, g_out, d_out, m_out, v_out):
        g = None
        for k in range(4):
            term = jnp.where(chip_ref[0] == k, own_ref[...], p_ref[k]).astype(F32)
            g = term if g is None else g + term
        delta, m_new, v_new = _adamw_math(w_ref[...], g, m_ref[...], v_ref[...])
        g_out[...] = g
        d_out[...] = delta
        m_out[...] = m_new
        v_out[...] = v_new

    blk = pl.BlockSpec((tr, cdim), lambda i, chip: (i, 0))
    my_chip = (2 * lax.axis_index("x") + lax.axis_index("y")).astype(jnp.int32).reshape(1)
    return pl.pallas_call(
        body, name=name,
        grid_spec=pltpu.PrefetchScalarGridSpec(
            num_scalar_prefetch=1, grid=(r // tr,),
            in_specs=[pl.BlockSpec((4, tr, cdim), lambda i, chip: (0, i, 0)),
                      pl.BlockSpec((None, tr, cdim), lambda i, chip: (chip[0], i, 0)), blk, blk, blk],
            out_specs=[blk] * 4),
        out_shape=[jax.ShapeDtypeStruct((r, cdim), F32)] * 4,
        compiler_params=_params("parallel"),
    )(my_chip, parts, sums, w, m, v)


class _NoExchange:
    def __init__(self, weights):
        self.weights = weights

    def before_pre_attn(self):
        return None

    def after_pre_attn(self, c_row3):
        return None

    def after_fox_fwd(self, fox_o):
        return None

    def after_attention(self, swa_o):
        return self.weights

    def after_early_grads(self, grads):
        return None

    def after_swa_bwd(self, dsq):
        return None

    def after_w_in_grad(self, d_win):
        return None


def _slot_order(t, axis):
    shp = t.shape
    t = t.reshape(shp[:axis] + (2, 4, shp[axis] // N_HEADS) + shp[axis + 1:])
    return jnp.swapaxes(t, axis, axis + 1).reshape(shp)


def _head_order(t, axis):
    shp = t.shape
    t = t.reshape(shp[:axis] + (4, 2, shp[axis] // N_HEADS) + shp[axis + 1:])
    return jnp.swapaxes(t, axis, axis + 1).reshape(shp)


def _forward_backward(x, p, target, win_t, hooks, b_forget, rel_bias, sinks, g1, g2, g3, g4, g5):
    S = x.shape[0]
    tm = 512
    tm_mlp = 512
    t = 256
    q0 = 3 * D_ATT + N_HEADS
    win_t = win_t.reshape(D_IN, D_MODEL)
    wt_rest = jnp.concatenate(
        [win_t[WT_FOX:q0], jnp.zeros((8, D_MODEL), MM), _slot_order(win_t[q0:q0 + D_ATT], 0), win_t[q0 + D_ATT:]],
        axis=0)
    bcol = jnp.pad(b_forget.reshape(N_HEADS, 1), ((0, 8), (0, 0)))
    rel_bias_slot = rel_bias[:, np.array(SLOT_HEAD)]
    sinks_slot = sinks.reshape(N_HEADS)[np.array(SLOT_HEAD)]
    bucket = jnp.asarray(_swa_bucket_map())

    a, fqkv, sqkv, fft = _pre_attn(x, g1, win_t, wt_rest, tm, after=hooks.before_pre_attn())
    c_row = _forget_cumsum(fft, bcol)
    c_col = c_row[:N_HEADS].T
    c_row3 = c_row[:N_HEADS].reshape(N_HEADS, S // t, t)
    fox_o, fox_lse = _fox_fwd(fqkv, c_row3, tq=512, tk=t, after=hooks.after_pre_attn(c_row))
    biasm = _swa_bias(rel_bias_slot, bucket)
    swa_o, swa_lse = _swa_fwd(sqkv, biasm, sinks_slot, after=hooks.after_fox_fwd(fox_o))
    wout, w1, w2, wple, wg = hooks.after_attention(swa_o)
    wout_fox = wout[:D_ATT]
    wout_swa = _slot_order(wout[D_ATT:], 0)
    mix, h1, m = _post_attn(x, fox_o, swa_o, wout_fox, wout_swa, g2, g3, tm)
    u, y, h2 = _mlp_fwd(m, h1, w1, w2, g4, tm_mlp)
    dh2, dpe, dgl, dg5, loss = _ple_loss(h2, p, target, wg, wple, g5, tm)

    d_wple = _weight_grad(p, dpe, "grad_w_ple", tk=D_PLE, n_chunks=N_DEV)
    d_wg = _weight_grad(h2, dgl, "grad_w_ple_gate", tk=256)
    dh1, dy, du, dg4, dg3 = _mlp_bwd(dh2, y, h1, u, w1, w2, g4, g3, tm)
    d_w2 = _weight_grad(u, dy, "grad_w_ff2", tk=256, relu2=True)
    d_w1 = _weight_grad(m, du, "grad_w_ff1", tk=256, n_chunks=N_DEV)
    head = np.arange(D_ATT) // HEAD_DIM
    head_rows = jnp.asarray((head[None, :] == np.arange(N_HEADS)[:, None]).astype(np.float32))
    dmix, dcat, d_row, d_swa, dg2 = _attn_out_bwd(dh1, mix, fox_o, swa_o, wout_fox, wout_swa, g2, head_rows, tm)
    d_col = d_swa.T
    d_wout = _weight_grad_two(fox_o, swa_o, dmix, "grad_w_out", tk=256)
    d_wout = jnp.concatenate([d_wout[:D_ATT], _head_order(d_wout[D_ATT:], 0)], axis=0)
    d_wout = d_wout.reshape(N_DEV, D_MODEL // N_DEV, D_MODEL)
    early = dict(w_ff1=d_w1, w_ff2=d_w2.reshape(N_DEV, FF_CHUNK, D_MODEL), w_ple=d_wple,
                 w_ple_gate=d_wg.reshape(N_DEV, D_MODEL // N_DEV, D_MODEL), w_out=d_wout)

    dsq, dsk, dsv, d_rb_slot, d_sink_slot = _swa_bwd(sqkv, dcat, biasm, sinks_slot, bucket, swa_lse, d_col,
                                                     after=hooks.after_early_grads(early))
    lse_row3 = fox_lse.T.reshape(N_HEADS, S // t, t)
    d_row3 = d_row.reshape(N_HEADS, S // t, t)
    dq_fox, dk_fox, dv_fox, dc_col, dcq = _fox_bwd(fqkv, dcat, lse_row3, d_row3, c_col, tq=t, tk=512,
                                                  after=hooks.after_swa_bwd(dsq))
    dc_row = jnp.pad(dc_col.T + dcq.reshape(N_HEADS, S), ((0, 8), (0, 0)))
    dff_t, db, d_wff_t = _forget_bwd(dc_row, fft, bcol, a)
    dz = _pack_dz(dq_fox, dk_fox, dv_fox, dsq, dsk, dsv, 512)
    d_wmain = _weight_grad(dz, a, "grad_w_in", tk=256)

    sq0 = 3 * D_ATT
    d_win = jnp.concatenate(
        [d_wmain[:sq0], d_wff_t[:N_HEADS].astype(MM), _head_order(d_wmain[sq0:sq0 + D_ATT], 0),
         d_wmain[sq0 + D_ATT:]], axis=0)
    d_win = d_win.reshape(N_DEV, D_IN // N_DEV, D_MODEL)
    grad_x, dg1 = _pre_attn_bwd(x, dh1, dz, dff_t, win_t, wt_rest, g1, tm, after=hooks.after_w_in_grad(d_win))
    big = dict(early, w_in=d_win)
    small = dict(b_forget=db[:N_HEADS].reshape(1, N_HEADS), rel_bias=d_rb_slot[:, np.array(HEAD_SLOT)],
                 swa_sinks=d_sink_slot[:, np.array(HEAD_SLOT)], g_attn_pre=dg1, g_attn_post=dg2, g_ff_pre=dg3,
                 g_ff_post=dg4, g_ple_post=dg5)
    return loss, grad_x, big, small


BIG = ("w_in", "w_out", "w_ff1", "w_ff2", "w_ple", "w_ple_gate")
SMALL_ROWS = ("g_attn_pre", "g_attn_post", "g_ff_pre", "g_ff_post", "g_ple_post")
WEIGHTS =("w_in", "b_forget", "w_out", "rel_bias", "swa_sinks", "g_attn_pre", "g_attn_post", "w_ff1", "w_ff2",
           "g_ff_pre", "g_ff_post", "w_ple", "w_ple_gate", "g_ple_post")


EARLY = ("w_ff1", "w_ff2", "w_ple", "w_ple_gate", "w_out")


class _Overlap:
    def __init__(self, later):
        self.later = later

    def before_pre_attn(self):
        self.near_sems, self.later, token = _exchange_start("gather_near_start", self.later, 3 * 5, _plan_gather_near)
        return token

    def after_pre_attn(self, c_row3):
        later = _exchange_wait("gather_near_wait", self.later, self.near_sems, c_row3, _plan_gather_near)
        self.far_sems, self.later, token = _exchange_start("gather_far_start", later, 3 * 5, _plan_gather_far)
        return token

    def after_fox_fwd(self, fox_o):
        later = _exchange_wait("gather_far_wait", self.later, self.far_sems, fox_o, _plan_gather_far)
        self.last_sems, self.later, token = _exchange_start("gather_last_start", later, 5, _plan_gather_last)
        return token

    def after_attention(self, swa_o):
        wout_g, w1_g, w2_g, wple_g, wg_g = _exchange_wait("gather_last_wait", self.later, self.last_sems, swa_o,
                                                         _plan_gather_last)
        return (wout_g.reshape(D_MODEL, D_MODEL), w1_g, w2_g.reshape(D_FF, D_MODEL),
                jnp.moveaxis(wple_g, 0, 1).reshape(D_PLE, D_MODEL), wg_g.reshape(D_MODEL, D_MODEL))

    def after_early_grads(self, grads):
        views = [grads[k].reshape((4, 2) + grads[k].shape[1:]) for k in EARLY]
        lands = [lax.empty((4,) + grads[k].shape[1:], MM) for k in EARLY]
        self.in_chip_sems, self.in_chip, token = _exchange_start("grads_in_chip_start", views + lands, len(EARLY),
                                                                 _plan_in_chip)
        return token

    def after_swa_bwd(self, dsq):
        arrays = _exchange_wait("grads_in_chip_wait", self.in_chip, self.in_chip_sems, dsq, _plan_in_chip)
        n = len(EARLY)
        sums = list(_chip_sums(arrays[:n], arrays[n:], "chip_sums_early"))
        lands = [lax.empty(s.shape, s.dtype) for s in sums]
        self.between_sems, self.between, token = _exchange_start("grads_between_chips_start", sums + lands, 3 * n,
                                                                 _plan_between_chips)
        return token

    def after_w_in_grad(self, d_win):
        self.late_in_chip_sems, self.late_in_chip, token = _exchange_start(
            "late_in_chip_start", [d_win.reshape((4, 2) + d_win.shape[1:]), lax.empty((4,) + d_win.shape[1:], MM)],
            1, _plan_in_chip)
        return token

    def finish(self, after):
        arrays = _exchange_wait("grads_between_chips_wait", self.between, self.between_sems, after,
                                _plan_between_chips)
        n = len(EARLY)
        self.sums = arrays[:n]
        return arrays[n:]


def _pack_small(t):
    rows = [t[k].reshape(1, D_MODEL) for k in SMALL_ROWS]
    misc = jnp.concatenate([t["b_forget"].reshape(-1), t["swa_sinks"].reshape(-1), t["rel_bias"].reshape(-1)])
    rows.append(jnp.pad(misc, (0, D_MODEL - misc.shape[0])).reshape(1, D_MODEL))
    rows.append(jnp.pad(t["loss"].reshape(-1), (0, D_MODEL - 1)).reshape(1, D_MODEL))
    rows.append(jnp.zeros((1, D_MODEL), F32))
    return jnp.concatenate(rows, axis=0).astype(F32)


def _unpack_small(blk):
    out = {k: blk[i].reshape(1, D_MODEL) for i, k in enumerate(SMALL_ROWS)}
    misc = blk[len(SMALL_ROWS)]
    out["b_forget"] = misc[:N_HEADS].reshape(1, N_HEADS)
    out["swa_sinks"] = misc[N_HEADS:2 * N_HEADS].reshape(1, N_HEADS)
    out["rel_bias"] = misc[2 * N_HEADS:2 * N_HEADS + N_BUCKETS * N_HEADS].reshape(N_BUCKETS, N_HEADS)
    out["loss"] = blk[len(SMALL_ROWS) + 1, 0]
    return out


def kernel(x, p, w_in, b_forget, w_out, rel_bias, swa_sinks, g_attn_pre, g_attn_post, w_ff1, w_ff2, g_ff_pre, g_ff_post, w_ple, w_ple_gate, g_ple_post, loss_target, m_w_in, m_b_forget, m_w_out, m_rel_bias, m_swa_sinks, m_g_attn_pre, m_g_attn_post, m_w_ff1, m_w_ff2, m_g_ff_pre, m_g_ff_post, m_w_ple, m_w_ple_gate, m_g_ple_post, v_w_in, v_b_forget, v_w_out, v_rel_bias, v_swa_sinks, v_g_attn_pre, v_g_attn_post, v_w_ff1, v_w_ff2, v_g_ff_pre, v_g_ff_post, v_w_ple, v_w_ple_gate, v_g_ple_post):
    w = dict(w_in=w_in, b_forget=b_forget, w_out=w_out, rel_bias=rel_bias, swa_sinks=swa_sinks,
             g_attn_pre=g_attn_pre, g_attn_post=g_attn_post, w_ff1=w_ff1, w_ff2=w_ff2, g_ff_pre=g_ff_pre,
             g_ff_post=g_ff_post, w_ple=w_ple, w_ple_gate=w_ple_gate, g_ple_post=g_ple_post)
    mom = dict(w_in=m_w_in, b_forget=m_b_forget, w_out=m_w_out, rel_bias=m_rel_bias, swa_sinks=m_swa_sinks,
               g_attn_pre=m_g_attn_pre, g_attn_post=m_g_attn_post, w_ff1=m_w_ff1, w_ff2=m_w_ff2,
               g_ff_pre=m_g_ff_pre, g_ff_post=m_g_ff_post, w_ple=m_w_ple, w_ple_gate=m_w_ple_gate,
               g_ple_post=m_g_ple_post)
    var = dict(w_in=v_w_in, b_forget=v_b_forget, w_out=v_w_out, rel_bias=v_rel_bias, swa_sinks=v_swa_sinks,
               g_attn_pre=v_g_attn_pre, g_attn_post=v_g_attn_post, w_ff1=v_w_ff1, w_ff2=v_w_ff2,
               g_ff_pre=v_g_ff_pre, g_ff_post=v_g_ff_post, w_ple=v_w_ple, w_ple_gate=v_w_ple_gate,
               g_ple_post=v_g_ple_post)

    turn = lambda t, k: t.T if k == "w_in" else t
    me = 4 * lax.axis_index("x") + 2 * lax.axis_index("y") + lax.axis_index("c")

    def stack(block):
        return lax.dynamic_update_slice_in_dim(lax.empty((N_DEV,) + block.shape, block.dtype), block[None], me, 0)

    stacks = [stack(turn(w[k][0], k).astype(MM)) for k in BIG]
    (win_g,), later = _all_gather_sequencer(stacks[:1], "all_gather_sequencer", 1), stacks[1:]
    hooks = _Overlap(later)
    loss, grad_x, big, small = _forward_backward(
        x[0], p[0, 0], loss_target[0], win_g, hooks, b_forget, rel_bias, swa_sinks,
        g_attn_pre, g_attn_post, g_ff_pre, g_ff_post, g_ple_post)
    out_g, out_d, out_m, out_v = {}, {}, {}, {}

    def update(k, part, own):
        g, d, m_new, v_new = _adamw_chips(part, own, turn(w[k][0], k), turn(mom[k][0], k), turn(var[k][0], k),
                                          "adamw_" + k)
        out_g[k], out_d[k], out_m[k], out_v[k] = turn(g, k)[None], turn(d, k)[None], turn(m_new, k)[None], turn(v_new, k)[None]
        return d

    view, other = _exchange_wait("late_in_chip_wait", hooks.late_in_chip, hooks.late_in_chip_sems, grad_x,
                                 _plan_in_chip)
    (chip_sum,) = _chip_sums([view], [other], "chip_sum_w_in")
    small["loss"] = loss
    between_sems, between, token = _exchange_start(
        "late_between_chips_start", [chip_sum, lax.empty(chip_sum.shape, MM), stack(_pack_small(small))], 3 + 7,
        _plan_late_between)
    early_parts = hooks.finish(token)
    done = [update(k, part, own) for k, part, own in zip(EARLY, early_parts, hooks.sums)]
    chip_sum, part, small_all = _exchange_wait("late_between_chips_wait", between, between_sems, done,
                                               _plan_late_between)
    update("w_in", part, chip_sum)
    rep = {k: w[k] for k in w if k not in BIG}
    rep["loss"] = jnp.zeros((), F32)
    rep_m = {k: mom[k] for k in mom if k not in BIG}
    rep_m["loss"] = jnp.zeros((), F32)
    rep_v = {k: var[k] for k in var if k not in BIG}
    rep_v["loss"] = jnp.ones((), F32)
    g_s, d_s, m_s, v_s = _adamw(small_all, _pack_small(rep), _pack_small(rep_m), _pack_small(rep_v), "adamw_small")
    g_s, d_s, m_s, v_s = _unpack_small(g_s), _unpack_small(d_s), _unpack_small(m_s), _unpack_small(v_s)
    for k in w:
        if k not in BIG:
            out_g[k], out_d[k], out_m[k], out_v[k] = g_s[k], d_s[k], m_s[k], v_s[k]
    return (g_s["loss"], grad_x[None], *[out_g[k] for k in WEIGHTS], *[out_d[k] for k in WEIGHTS],
            *[out_m[k] for k in WEIGHTS], *[out_v[k] for k in WEIGHTS])
```

```python
import functools

import numpy as np
import jax
import jax.numpy as jnp
from jax import lax
from jax.experimental import pallas as pl
from jax.experimental.pallas import tpu as pltpu
from jax.experimental.pallas import tpu_sc as plsc

F32 = jnp.float32
MM = jnp.bfloat16

D_MODEL = 1024
HEAD_DIM = 64
N_HEADS = 8
D_ATT = N_HEADS * HEAD_DIM
D_KV = 128
D_FF = 4096
D_PLE = 256
D_IN = 3 * D_ATT + N_HEADS + D_ATT + 2 * D_KV
N_DEV = 8
FF_CHUNK = D_FF // N_DEV
WINDOW = 128
N_BUCKETS = 32
MAX_DISTANCE = 128
RMS_EPS = 1e-6
Q_SCALE = HEAD_DIM ** -0.5
NEG = -1e30

ADAM_LR = 0.001
ADAM_B1 = 0.9
ADAM_B2 = 0.999
ADAM_EPS = 1e-08
ADAM_WD = 0.01
ADAM_STEP = 10

SLOT_HEAD = (0, 4, 1, 5, 2, 6, 3, 7)
HEAD_SLOT = (0, 2, 4, 6, 1, 3, 5, 7)

VMEM_LIMIT = 60 * 1024 * 1024
MESH = pl.DeviceIdType.MESH

NT = (((1,), (1,)), ((), ()))
TN = (((0,), (0,)), ((), ()))


def _params(*semantics):
    return pltpu.CompilerParams(dimension_semantics=semantics, vmem_limit_bytes=VMEM_LIMIT)


def _resident():
    return pl.BlockSpec(memory_space=pltpu.VMEM)


def _rows(tm, width):
    return pl.BlockSpec((tm, width), lambda i: (i, 0))


def _const(shape):
    return pl.BlockSpec(shape, lambda i: (0,) * len(shape))


def _dot(a, b):
    return jnp.dot(a, b, preferred_element_type=F32)


def _dot_nt(a, b):
    return lax.dot_general(a, b, NT, preferred_element_type=F32)


def _dot_tn(a, b):
    return lax.dot_general(a, b, TN, preferred_element_type=F32)


def _rms(xf):
    r = lax.rsqrt(jnp.mean(xf * xf, axis=-1, keepdims=True) + RMS_EPS)
    return xf * r, r


def _rms_bwd(dout, n, r, g):
    dg = jnp.sum(dout * n, axis=0, keepdims=True)
    dn = dout * g
    dx = r * (dn - n * jnp.mean(dn * n, axis=-1, keepdims=True))
    return dx, dg


def _run_after(after, body, in_specs, operands):
    if after is None:
        return body, list(in_specs), tuple(operands)
    n = len(operands)
    return ((lambda *refs: body(*refs[:n], *refs[n + 1:])), list(in_specs) + [pl.BlockSpec(memory_space=pl.ANY)],
            tuple(operands) + (after,))


def _accumulate(ref, value, step):
    @pl.when(step == 0)
    def _():
        ref[...] = value

    @pl.when(step != 0)
    def _():
        ref[...] += value


def _t5_bucket(n):
    max_exact = N_BUCKETS // 2
    large = max_exact + (np.log(np.maximum(n, 1) / max_exact) / np.log(MAX_DISTANCE / max_exact)
                         * (N_BUCKETS - max_exact)).astype(np.int32)
    large = np.minimum(large, N_BUCKETS - 1)
    return np.where(n < max_exact, n, large).astype(np.int32)


def _swa_bucket_map():
    i = np.arange(WINDOW)[:, None]
    j = np.arange(2 * WINDOW)[None, :]
    dist = i + WINDOW - j
    ok = (dist >= 0) & (dist < WINDOW)
    return np.where(ok, _t5_bucket(np.clip(dist, 0, None)), -1).astype(np.int32)


WT_FOX = 3 * D_ATT
WT_SQ = 16
WT_SKV = WT_SQ + D_ATT
WT_REST = WT_SKV + 2 * D_KV


def _pre_attn(x, g1, win_t, wt_rest, tm, after=None):
    S = x.shape[0]

    def body(x_ref, g_ref, wf_ref, wr_ref, a_ref, fqkv_ref, sqkv_ref, fft_ref):
        n, _ = _rms(x_ref[...])
        a = (n * g_ref[...]).astype(MM)
        a_ref[...] = a
        fqkv_ref[:, :D_ATT] = (_dot_nt(a, wf_ref[0:D_ATT]) * Q_SCALE).astype(MM)
        fqkv_ref[:, D_ATT:] = _dot_nt(a, wf_ref[D_ATT:WT_FOX]).astype(MM)
        sqkv_ref[:, :D_ATT] = (_dot_nt(a, wr_ref[WT_SQ:WT_SKV]) * Q_SCALE).astype(MM)
        sqkv_ref[:, D_ATT:] = _dot_nt(a, wr_ref[WT_SKV:WT_REST]).astype(MM)
        fft_ref[...] = _dot_nt(wr_ref[0:WT_SQ], a)

    body, in_specs, operands = _run_after(
        after, body, [_rows(tm, D_MODEL), _const((1, D_MODEL)), _const((WT_FOX, D_MODEL)), _resident()],
        (x, g1, win_t, wt_rest))
    return pl.pallas_call(
        body, name="pre_attn", grid=(S // tm,), in_specs=in_specs,
        out_specs=[_rows(tm, D_MODEL), _rows(tm, 3 * D_ATT), _rows(tm, D_ATT + 2 * D_KV),
                   pl.BlockSpec((16, tm), lambda i: (0, i))],
        out_shape=[jax.ShapeDtypeStruct((S, D_MODEL), MM), jax.ShapeDtypeStruct((S, 3 * D_ATT), MM),
                   jax.ShapeDtypeStruct((S, D_ATT + 2 * D_KV), MM), jax.ShapeDtypeStruct((16, S), F32)],
        compiler_params=_params("parallel"),
    )(*operands)


def _lane_scan(v, reverse):
    S = v.shape[1]
    lane = lax.broadcasted_iota(jnp.int32, v.shape, 1)
    k = 1
    while k < S:
        if reverse:
            v = v + jnp.where(lane < S - k, pltpu.roll(v, S - k, axis=1), 0.0)
        else:
            v = v + jnp.where(lane >= k, pltpu.roll(v, k, axis=1), 0.0)
        k *= 2
    return v


def _forget_cumsum(fft, bcol):
    def body(f_ref, b_ref, c_ref):
        z = f_ref[...] + b_ref[...]
        log_f = jnp.minimum(z, 0.0) - jnp.log1p(jnp.exp(-jnp.abs(z)))
        c_ref[...] = _lane_scan(log_f, reverse=False)

    return pl.pallas_call(
        body, name="forget_cumsum", out_shape=jax.ShapeDtypeStruct(fft.shape, F32),
        in_specs=[_resident(), _resident()], out_specs=_resident(),
    )(fft, bcol)


def _forget_bwd(dc_row, fft, bcol, a):
    def body(dc_ref, f_ref, b_ref, a_ref, dff_ref, db_ref, dw_ref):
        z = f_ref[...] + b_ref[...]
        dlog_f = _lane_scan(dc_ref[...], reverse=True)
        dff = dlog_f * (1.0 / (1.0 + jnp.exp(z)))
        dff_ref[...] = dff
        db_ref[...] = jnp.sum(dff, axis=1, keepdims=True)
        dw_ref[...] = _dot(dff.astype(MM), a_ref[...])

    return pl.pallas_call(
        body, name="forget_bwd",
        out_shape=[jax.ShapeDtypeStruct(fft.shape, F32), jax.ShapeDtypeStruct((fft.shape[0], 1), F32),
                   jax.ShapeDtypeStruct((fft.shape[0], D_MODEL), F32)],
        in_specs=[_resident()] * 4, out_specs=[_resident()] * 3,
    )(dc_row, fft, bcol, a)


def _head_select(shape, upper):
    lane = lax.broadcasted_iota(jnp.int32, shape, 1)
    return lane >= HEAD_DIM if upper else lane < HEAD_DIM


def _fox_fwd(fqkv, c_row3, tq, tk, pairs_per_loop=2, row_chunks=1):
    S = fqkv.shape[0]
    rq = tq // row_chunks
    n_band = tq // tk

    def body(q_ref, k_ref, v_ref, ck_ref, o_ref, lse_ref):
        qi = pl.program_id(0)
        row = lax.broadcasted_iota(jnp.int32, (rq, tk), 0)
        col = lax.broadcasted_iota(jnp.int32, (rq, tk), 1)
        low = _head_select((rq, 128), 0)
        for first in range(0, N_HEADS // 2, pairs_per_loop):
            pairs = range(first, first + pairs_per_loop)
            chains = [(pr, hh, rc) for pr in pairs for hh in range(2) for rc in range(row_chunks)]
            qh = {}
            for pr in pairs:
                for rc in range(row_chunks):
                    q2 = q_ref[rc * rq:(rc + 1) * rq, pr * 128:(pr + 1) * 128]
                    qh[pr, 0, rc] = jnp.where(low, q2, jnp.zeros_like(q2))
                    qh[pr, 1, rc] = jnp.where(low, jnp.zeros_like(q2), q2)

            def block(kb, carry, band, chains=chains, qh=qh):
                rows = pl.ds(pl.multiple_of(kb * tk, tk), tk)
                out = []
                for (pr, hh, rc), (m, l, acc) in zip(chains, carry):
                    if band is not None and (rc + 1) * rq <= band * tk:
                        out.append((m, l, acc))
                        continue
                    lanes = slice(pr * 128, (pr + 1) * 128)
                    s = _dot_nt(qh[pr, hh, rc], k_ref[rows, lanes]) - ck_ref[2 * pr + hh, pl.ds(kb, 1), :]
                    if band is not None:
                        s = jnp.where(row + rc * rq >= col + band * tk, s, NEG)
                    m_new = jnp.maximum(m, jnp.max(s, axis=-1, keepdims=True))
                    p = jnp.exp(s - m_new)
                    alpha = jnp.exp(m - m_new)
                    l = alpha * l + jnp.sum(p, axis=-1, keepdims=True)
                    acc = alpha * acc + _dot(p.astype(MM), v_ref[rows, lanes])
                    out.append((m_new, l, acc))
                return tuple(out)

            carry = tuple((jnp.full((rq, 1), NEG, F32), jnp.zeros((rq, 1), F32), jnp.zeros((rq, 128), F32))
                          for _ in chains)
            carry = lax.fori_loop(0, qi * n_band, functools.partial(block, band=None), carry)
            for band in range(n_band):
                carry = block(qi * n_band + band, carry, band=band)
            res = {}
            for (pr, hh, rc), (m, l, acc) in zip(chains, carry):
                res[pr, hh, rc] = acc / l
                lse_ref[rc * rq:(rc + 1) * rq, 2 * pr + hh:2 * pr + hh + 1] = m + jnp.log(l)
            for pr in pairs:
                for rc in range(row_chunks):
                    o_ref[rc * rq:(rc + 1) * rq, pr * 128:(pr + 1) * 128] = jnp.where(
                        low, res[pr, 0, rc], res[pr, 1, rc]).astype(MM)

    return pl.pallas_call(
        body, name="fox_fwd", grid=(S // tq,),
        in_specs=[pl.BlockSpec((tq, D_ATT), lambda i: (i, 0)), pl.BlockSpec((S, D_ATT), lambda i: (0, 1)),
                  pl.BlockSpec((S, D_ATT), lambda i: (0, 2)), _resident()],
        out_specs=[_rows(tq, D_ATT), _rows(tq, N_HEADS)],
        out_shape=[jax.ShapeDtypeStruct((S, D_ATT), MM), jax.ShapeDtypeStruct((S, N_HEADS), F32)],
        compiler_params=_params("parallel"),
    )(fqkv, fqkv, fqkv, c_row3)


def _swa_bias(rel_bias_slot, bucket):
    def body(rb_ref, bk_ref, out_ref):
        bk = bk_ref[...]
        for s in range(N_HEADS):
            acc = jnp.where(bk < 0, NEG, 0.0).astype(F32)
            for b in range(N_BUCKETS):
                acc = jnp.where(bk == b, rb_ref[b, s], acc)
            out_ref[s] = acc

    return pl.pallas_call(
        body, name="swa_bias", out_shape=jax.ShapeDtypeStruct((N_HEADS, WINDOW, 2 * WINDOW), F32),
        in_specs=[pl.BlockSpec(memory_space=pltpu.SMEM), _resident()], out_specs=_resident(),
    )(rel_bias_slot, bucket)


def _stack4(piece):
    return jnp.concatenate([piece(j) for j in range(4)], axis=0)


def _swa_specs(S):
    q = pl.BlockSpec((WINDOW, D_ATT), lambda n: (n, 0))
    kp = pl.BlockSpec((WINDOW, D_KV), lambda n: (jnp.maximum(n - 1, 0), 4))
    kc = pl.BlockSpec((WINDOW, D_KV), lambda n: (n, 4))
    vp = pl.BlockSpec((WINDOW, D_KV), lambda n: (jnp.maximum(n - 1, 0), 5))
    vc = pl.BlockSpec((WINDOW, D_KV), lambda n: (n, 5))
    return [q, kp, kc, vp, vc]


def _swa_fwd(sqkv, biasm, sinks_slot, after=None):
    S = sqkv.shape[0]

    def body(q_ref, kp_ref, kc_ref, vp_ref, vc_ref, bias_ref, sink_ref, o_ref, lse_ref):
        n = pl.program_id(0)
        no_prev = jnp.where(n > 0, 0.0, NEG)
        low = _head_select((WINDOW, 128), 0)
        res = []
        for g in range(2):
            sel = low if g == 0 else jnp.logical_not(low)
            qg = _stack4(lambda j: jnp.where(sel, q_ref[:, j * 128:(j + 1) * 128], jnp.zeros((WINDOW, 128), MM)))
            sink = _stack4(lambda j: jnp.full((WINDOW, 1), sink_ref[2 * j + g], F32))
            sp = _dot_nt(qg, kp_ref[...]) + _stack4(lambda j: bias_ref[2 * j + g, :, :WINDOW]) + no_prev
            sc = _dot_nt(qg, kc_ref[...]) + _stack4(lambda j: bias_ref[2 * j + g, :, WINDOW:])
            m = jnp.maximum(jnp.maximum(jnp.max(sp, axis=-1, keepdims=True),
                                        jnp.max(sc, axis=-1, keepdims=True)), sink)
            ep = jnp.exp(sp - m)
            ec = jnp.exp(sc - m)
            den = jnp.sum(ep, axis=-1, keepdims=True) + jnp.sum(ec, axis=-1, keepdims=True) + jnp.exp(sink - m)
            res.append((_dot(ep.astype(MM), vp_ref[...]) + _dot(ec.astype(MM), vc_ref[...])) / den)
            lse = m + jnp.log(den)
            for j in range(4):
                lse_ref[:, 2 * j + g:2 * j + g + 1] = lse[j * WINDOW:(j + 1) * WINDOW]
        for j in range(4):
            rows = slice(j * WINDOW, (j + 1) * WINDOW)
            o_ref[:, j * 128:(j + 1) * 128] = jnp.where(low, res[0][rows], res[1][rows]).astype(MM)

    body, in_specs, operands = _run_after(
        after, body, _swa_specs(S) + [_resident(), pl.BlockSpec(memory_space=pltpu.SMEM)],
        (sqkv, sqkv, sqkv, sqkv, sqkv, biasm, sinks_slot))
    return pl.pallas_call(
        body, name="swa_fwd", grid=(S // WINDOW,), in_specs=in_specs,
        out_specs=[_rows(WINDOW, D_ATT), _rows(WINDOW, N_HEADS)],
        out_shape=[jax.ShapeDtypeStruct((S, D_ATT), MM), jax.ShapeDtypeStruct((S, N_HEADS), F32)],
        compiler_params=_params("parallel"),
    )(*operands)


def _post_attn(x, fox_o, swa_o, wout_fox, wout_swa, g2, g3, tm):
    S = x.shape[0]

    def body(x_ref, fo_ref, so_ref, wf_ref, ws_ref, g2_ref, g3_ref, mix_ref, h1_ref, m_ref):
        mix = _dot(fo_ref[...], wf_ref[...]) + _dot(so_ref[...], ws_ref[...])
        mix_ref[...] = mix
        n2, _ = _rms(mix)
        h1 = x_ref[...] + n2 * g2_ref[...]
        h1_ref[...] = h1
        n3, _ = _rms(h1)
        m_ref[...] = (n3 * g3_ref[...]).astype(MM)

    return pl.pallas_call(
        body, name="post_attn", grid=(S // tm,),
        in_specs=[_rows(tm, D_MODEL), _rows(tm, D_ATT), _rows(tm, D_ATT), _resident(), _resident(),
                  _const((1, D_MODEL)), _const((1, D_MODEL))],
        out_specs=[_rows(tm, D_MODEL)] * 3,
        out_shape=[jax.ShapeDtypeStruct((S, D_MODEL), F32), jax.ShapeDtypeStruct((S, D_MODEL), F32),
                   jax.ShapeDtypeStruct((S, D_MODEL), MM)],
        compiler_params=_params("parallel"),
    )(x, fox_o, swa_o, wout_fox, wout_swa, g2, g3)


def _mlp_fwd(m, h1, w1, w2, g4, tm):
    S = m.shape[0]

    def body(m_ref, h1_ref, w1_ref, w2_ref, g4_ref, u_ref, y_ref, h2_ref):
        mb = m_ref[...]
        y = jnp.zeros((tm, D_MODEL), F32)
        for j in range(N_DEV):
            cols = slice(j * FF_CHUNK, (j + 1) * FF_CHUNK)
            u = _dot(mb, w1_ref[j])
            u_ref[:, cols] = u.astype(MM)
            y = y + _dot(jnp.square(jnp.maximum(u, 0.0)).astype(MM), w2_ref[cols, :])
        y_ref[...] = y
        n4, _ = _rms(y)
        h2_ref[...] = h1_ref[...] + n4 * g4_ref[...]

    return pl.pallas_call(
        body, name="mlp_fwd", grid=(S // tm,),
        in_specs=[_rows(tm, D_MODEL), _rows(tm, D_MODEL), _resident(), _resident(), _const((1, D_MODEL))],
        out_specs=[_rows(tm, D_FF), _rows(tm, D_MODEL), _rows(tm, D_MODEL)],
        out_shape=[jax.ShapeDtypeStruct((S, D_FF), MM), jax.ShapeDtypeStruct((S, D_MODEL), F32),
                   jax.ShapeDtypeStruct((S, D_MODEL), F32)],
        compiler_params=_params("parallel"),
    )(m, h1, w1, w2, g4)


def _ple_loss(h2, p, target, wg, wple, g5, tm):
    S = h2.shape[0]

    def body(h2_ref, p_ref, t_ref, wg_ref, wp_ref, g5_ref, dh2_ref, dpe_ref, dgl_ref, dg5_ref, loss_ref):
        i = pl.program_id(0)
        h2 = h2_ref[...]
        gate = jax.nn.sigmoid(_dot(h2.astype(MM), wg_ref[...]))
        pe = _dot(p_ref[...].astype(MM), wp_ref[...])
        n5, r5 = _rms(pe * gate)
        g5 = g5_ref[...]
        diff = h2 + n5 * g5 - t_ref[...]
        per_token = jnp.mean(jnp.square(diff), axis=-1, keepdims=True)
        _accumulate(loss_ref, 0.5 * jnp.sum(per_token, axis=0, keepdims=True), i)
        dh3 = diff * (1.0 / D_MODEL)
        de, dg5 = _rms_bwd(dh3, n5, r5, g5)
        _accumulate(dg5_ref, dg5, i)
        dpe_ref[...] = (de * gate).astype(MM)
        dgl = (de * pe * gate * (1.0 - gate)).astype(MM)
        dgl_ref[...] = dgl
        dh2_ref[...] = dh3 + _dot_nt(dgl, wg_ref[...])

    return pl.pallas_call(
        body, name="ple_loss", grid=(S // tm,),
        in_specs=[_rows(tm, D_MODEL), _rows(tm, D_PLE), _rows(tm, D_MODEL), _resident(), _resident(),
                  _const((1, D_MODEL))],
        out_specs=[_rows(tm, D_MODEL), _rows(tm, D_MODEL), _rows(tm, D_MODEL), _const((1, D_MODEL)), _const((1, 1))],
        out_shape=[jax.ShapeDtypeStruct((S, D_MODEL), F32), jax.ShapeDtypeStruct((S, D_MODEL), MM),
                   jax.ShapeDtypeStruct((S, D_MODEL), MM), jax.ShapeDtypeStruct((1, D_MODEL), F32),
                   jax.ShapeDtypeStruct((1, 1), F32)],
        compiler_params=_params("arbitrary"),
    )(h2, p, target, wg, wple, g5)


def _mlp_bwd(dh2, y, h1, u, w1, w2, g4, g3, tm):
    S = dh2.shape[0]

    def body(dh2_ref, y_ref, h1_ref, u_ref, w1_ref, w2_ref, g4_ref, g3_ref,
             dh1_ref, dy_ref, du_ref, dg4_ref, dg3_ref):
        i = pl.program_id(0)
        dh2 = dh2_ref[...]
        n4, r4 = _rms(y_ref[...])
        dy, dg4 = _rms_bwd(dh2, n4, r4, g4_ref[...])
        _accumulate(dg4_ref, dg4, i)
        dyb = dy.astype(MM)
        dy_ref[...] = dyb
        dm = jnp.zeros((tm, D_MODEL), F32)
        for j in range(N_DEV):
            cols = slice(j * FF_CHUNK, (j + 1) * FF_CHUNK)
            dact = _dot_nt(dyb, w2_ref[cols, :])
            du = (dact * (2.0 * jnp.maximum(u_ref[:, cols].astype(F32), 0.0))).astype(MM)
            du_ref[:, cols] = du
            dm = dm + _dot_nt(du, w1_ref[j])
        n3, r3 = _rms(h1_ref[...])
        dx, dg3 = _rms_bwd(dm, n3, r3, g3_ref[...])
        _accumulate(dg3_ref, dg3, i)
        dh1_ref[...] = dh2 + dx

    return pl.pallas_call(
        body, name="mlp_bwd", grid=(S // tm,),
        in_specs=[_rows(tm, D_MODEL), _rows(tm, D_MODEL), _rows(tm, D_MODEL), _rows(tm, D_FF),
                  _resident(), _resident(), _const((1, D_MODEL)), _const((1, D_MODEL))],
        out_specs=[_rows(tm, D_MODEL), _rows(tm, D_MODEL), _rows(tm, D_FF), _const((1, D_MODEL)),
                   _const((1, D_MODEL))],
        out_shape=[jax.ShapeDtypeStruct((S, D_MODEL), F32), jax.ShapeDtypeStruct((S, D_MODEL), MM),
                   jax.ShapeDtypeStruct((S, D_FF), MM), jax.ShapeDtypeStruct((1, D_MODEL), F32),
                   jax.ShapeDtypeStruct((1, D_MODEL), F32)],
        compiler_params=_params("arbitrary"),
    )(dh2, y, h1, u, w1, w2, g4, g3)


def _attn_out_bwd(dh1, mix, fox_o, swa_o, wout_fox, wout_swa, g2, head_rows, tm):
    S = dh1.shape[0]

    def body(dh1_ref, mix_ref, fo_ref, so_ref, wf_ref, ws_ref, g2_ref, er_ref,
             dmix_ref, dcat_ref, drow_ref, dswa_ref, dg2_ref):
        i = pl.program_id(0)
        n2, r2 = _rms(mix_ref[...])
        dmix, dg2 = _rms_bwd(dh1_ref[...], n2, r2, g2_ref[...])
        _accumulate(dg2_ref, dg2, i)
        dmb = dmix.astype(MM)
        dmix_ref[...] = dmb
        dfo = _dot_nt(dmb, wf_ref[...]).astype(MM)
        dso = _dot_nt(dmb, ws_ref[...]).astype(MM)
        dcat_ref[:, :D_ATT] = dfo
        dcat_ref[:, D_ATT:] = dso
        hi = lax.Precision.HIGHEST
        prod_f = dfo.astype(F32) * fo_ref[...].astype(F32)
        prod_s = dso.astype(F32) * so_ref[...].astype(F32)
        drow_ref[...] = lax.dot_general(er_ref[...], prod_f, NT, precision=hi, preferred_element_type=F32)
        dswa_ref[...] = lax.dot_general(er_ref[...], prod_s, NT, precision=hi, preferred_element_type=F32)

    return pl.pallas_call(
        body, name="attn_out_bwd", grid=(S // tm,),
        in_specs=[_rows(tm, D_MODEL), _rows(tm, D_MODEL), _rows(tm, D_ATT), _rows(tm, D_ATT), _resident(),
                  _resident(), _const((1, D_MODEL)), _resident()],
        out_specs=[_rows(tm, D_MODEL), _rows(tm, D_MODEL), pl.BlockSpec((N_HEADS, tm), lambda i: (0, i)),
                   pl.BlockSpec((N_HEADS, tm), lambda i: (0, i)), _const((1, D_MODEL))],
        out_shape=[jax.ShapeDtypeStruct((S, D_MODEL), MM), jax.ShapeDtypeStruct((S, D_MODEL), MM),
                   jax.ShapeDtypeStruct((N_HEADS, S), F32), jax.ShapeDtypeStruct((N_HEADS, S), F32),
                   jax.ShapeDtypeStruct((1, D_MODEL), F32)],
        compiler_params=_params("arbitrary"),
    )(dh1, mix, fox_o, swa_o, wout_fox, wout_swa, g2, head_rows)


def _fox_bwd(fqkv, dcat, lse_row3, d_row3, c_col, tq, tk, pairs_per_loop=2, after=None):
    S = fqkv.shape[0]
    n_blk = S // tk
    n_qblk = S // tq
    n_band = tk // tq

    def body(q_ref, k_ref, v_ref, do_ref, lse_ref, dd_ref, ck_ref, dq_ref, dk_ref, dv_ref, dc_ref, dcq_ref):
        kb = pl.program_id(0)

        @pl.when(kb == 0)
        def _():
            dq_ref[...] = jnp.zeros_like(dq_ref)
            dcq_ref[...] = jnp.zeros_like(dcq_ref)

        key = lax.broadcasted_iota(jnp.int32, (tk, tq), 0)
        qry = lax.broadcasted_iota(jnp.int32, (tk, tq), 1)
        low = _head_select((tk, 128), 0)
        for first in range(0, N_HEADS // 2, pairs_per_loop):
            pairs = range(first, first + pairs_per_loop)
            heads = [(pr, hh) for pr in pairs for hh in range(2)]
            kh, vh, ck = {}, {}, {}
            for pr in pairs:
                k2 = k_ref[:, pr * 128:(pr + 1) * 128]
                v2 = v_ref[:, pr * 128:(pr + 1) * 128]
                zero = jnp.zeros_like(k2)
                kh[pr, 0], kh[pr, 1] = jnp.where(low, k2, zero), jnp.where(low, zero, k2)
                vh[pr, 0], vh[pr, 1] = jnp.where(low, v2, zero), jnp.where(low, zero, v2)
                for hh in range(2):
                    ck[pr, hh] = ck_ref[:, 2 * pr + hh:2 * pr + hh + 1]

            def block(qb, carry, band, pairs=pairs, kh=kh, vh=vh, ck=ck):
                rows = pl.ds(pl.multiple_of(qb * tq, tq), tq)
                k1 = tk if band is None else (band + 1) * tq
                out = []
                it = iter(carry)
                for pr in pairs:
                    lanes = slice(pr * 128, (pr + 1) * 128)
                    q2 = q_ref[rows, lanes]
                    do2 = do_ref[rows, lanes]
                    dq = None
                    for hh in range(2):
                        h = 2 * pr + hh
                        dk, dv, dc = next(it)
                        s_t = _dot_nt(kh[pr, hh][:k1], q2) - ck[pr, hh][:k1]
                        p_t = jnp.exp(s_t - lse_ref[h, pl.ds(qb, 1), :])
                        if band is not None:
                            p_t = jnp.where(qry[:k1] + band * tq >= key[:k1], p_t, 0.0)
                        ds_t = p_t * (_dot_nt(vh[pr, hh][:k1], do2) - dd_ref[h, pl.ds(qb, 1), :])
                        dsb = ds_t.astype(MM)
                        dv_new = dv[:k1] + _dot(p_t.astype(MM), do2)
                        dk_new = dk[:k1] + _dot(dsb, q2)
                        dc_new = dc[:k1] - jnp.sum(ds_t, axis=1, keepdims=True)
                        if k1 < tk:
                            dv_new = jnp.concatenate([dv_new, dv[k1:]], axis=0)
                            dk_new = jnp.concatenate([dk_new, dk[k1:]], axis=0)
                            dc_new = jnp.concatenate([dc_new, dc[k1:]], axis=0)
                        part = _dot_tn(dsb, kh[pr, hh][:k1])
                        dq = part if dq is None else dq + part
                        dcq_ref[h, pl.ds(qb, 1), :] += jnp.sum(ds_t, axis=0, keepdims=True)
                        out.append((dk_new, dv_new, dc_new))
                    dq_ref[rows, lanes] += dq
                return tuple(out)

            carry = tuple((jnp.zeros((tk, 128), F32), jnp.zeros((tk, 128), F32), jnp.zeros((tk, 1), F32))
                          for _ in heads)
            for band in range(n_band):
                carry = block(kb * n_band + band, carry, band=band)
            carry = lax.fori_loop((kb + 1) * n_band, n_qblk, functools.partial(block, band=None), carry)
            grads = dict(zip(heads, carry))
            for pr in pairs:
                lanes = slice(pr * 128, (pr + 1) * 128)
                dk_ref[:, lanes] = jnp.where(low, grads[pr, 0][0], grads[pr, 1][0]).astype(MM)
                dv_ref[:, lanes] = jnp.where(low, grads[pr, 0][1], grads[pr, 1][1]).astype(MM)
                for hh in range(2):
                    dc_ref[:, 2 * pr + hh:2 * pr + hh + 1] = grads[pr, hh][2]

        @pl.when(kb == n_blk - 1)
        def _():
            dq_ref[...] = dq_ref[...] * Q_SCALE

    body, in_specs, operands = _run_after(
        after, body,
        [pl.BlockSpec((S, D_ATT), lambda i: (0, 0)), pl.BlockSpec((tk, D_ATT), lambda i: (i, 1)),
         pl.BlockSpec((tk, D_ATT), lambda i: (i, 2)), pl.BlockSpec((S, D_ATT), lambda i: (0, 0)),
         _resident(), _resident(), _rows(tk, N_HEADS)],
        (fqkv, fqkv, fqkv, dcat, lse_row3, d_row3, c_col))
    return pl.pallas_call(
        body, name="fox_bwd", grid=(n_blk,), in_specs=in_specs,
        out_specs=[_const((S, D_ATT)), _rows(tk, D_ATT), _rows(tk, D_ATT), _rows(tk, N_HEADS),
                   _const((N_HEADS, n_qblk, tq))],
        out_shape=[jax.ShapeDtypeStruct((S, D_ATT), F32), jax.ShapeDtypeStruct((S, D_ATT), MM),
                   jax.ShapeDtypeStruct((S, D_ATT), MM), jax.ShapeDtypeStruct((S, N_HEADS), F32),
                   jax.ShapeDtypeStruct((N_HEADS, n_qblk, tq), F32)],
        compiler_params=_params("arbitrary"),
    )(*operands)


def _swa_bwd(sqkv, dcat, biasm, sinks_slot, bucket, lse, d_col, after=None):
    S = sqkv.shape[0]
    n_blk = S // WINDOW

    def body(q_ref, kp_ref, kc_ref, vp_ref, vc_ref, do_ref, bias_ref, sink_ref, bk_ref, lse_ref, dd_ref,
             dq_ref, dk_ref, dv_ref, drb_ref, dsink_ref, ds_acc):
        n = pl.program_id(0)

        @pl.when(n == 0)
        def _():
            dk_ref[...] = jnp.zeros_like(dk_ref)
            dv_ref[...] = jnp.zeros_like(dv_ref)
            ds_acc[...] = jnp.zeros_like(ds_acc)
            dsink_ref[...] = jnp.zeros_like(dsink_ref)

        no_prev = jnp.where(n > 0, 0.0, NEG)
        prev = pl.ds(pl.multiple_of(jnp.maximum(n - 1, 0) * WINDOW, WINDOW), WINDOW)
        cur = pl.ds(pl.multiple_of(n * WINDOW, WINDOW), WINDOW)
        lane8 = lax.broadcasted_iota(jnp.int32, (1, N_HEADS), 1)
        dkp = jnp.zeros((WINDOW, D_KV), F32)
        dkc = jnp.zeros((WINDOW, D_KV), F32)
        dvp = jnp.zeros((WINDOW, D_KV), F32)
        dvc = jnp.zeros((WINDOW, D_KV), F32)
        dsink = jnp.zeros((1, N_HEADS), F32)
        low = _head_select((WINDOW, 128), 0)
        zero = jnp.zeros((WINDOW, 128), MM)
        dqs = []
        for g in range(2):
            sel = low if g == 0 else jnp.logical_not(low)
            qg = _stack4(lambda j: jnp.where(sel, q_ref[:, j * 128:(j + 1) * 128], zero))
            dog = _stack4(lambda j: jnp.where(sel, do_ref[:, j * 128:(j + 1) * 128], zero))
            lse_g = _stack4(lambda j: lse_ref[:, 2 * j + g:2 * j + g + 1])
            dd = _stack4(lambda j: dd_ref[:, 2 * j + g:2 * j + g + 1])
            sink = _stack4(lambda j: jnp.full((WINDOW, 1), sink_ref[2 * j + g], F32))
            pp = jnp.exp(_dot_nt(qg, kp_ref[...]) + _stack4(lambda j: bias_ref[2 * j + g, :, :WINDOW]) + no_prev - lse_g)
            pc = jnp.exp(_dot_nt(qg, kc_ref[...]) + _stack4(lambda j: bias_ref[2 * j + g, :, WINDOW:]) - lse_g)
            sink_term = jnp.exp(sink - lse_g) * dd
            dsp = pp * (_dot_nt(dog, vp_ref[...]) - dd)
            dsc = pc * (_dot_nt(dog, vc_ref[...]) - dd)
            for j in range(4):
                rows = slice(j * WINDOW, (j + 1) * WINDOW)
                dsink = dsink + jnp.where(lane8 == 2 * j + g, -jnp.sum(sink_term[rows]), 0.0)
                ds_acc[2 * j + g, :, :WINDOW] += dsp[rows]
                ds_acc[2 * j + g, :, WINDOW:] += dsc[rows]
            dspb, dscb = dsp.astype(MM), dsc.astype(MM)
            dqs.append(_dot(dspb, kp_ref[...]) + _dot(dscb, kc_ref[...]))
            dkp = dkp + _dot_tn(dspb, qg)
            dkc = dkc + _dot_tn(dscb, qg)
            dvp = dvp + _dot_tn(pp.astype(MM), dog)
            dvc = dvc + _dot_tn(pc.astype(MM), dog)
        for j in range(4):
            rows = slice(j * WINDOW, (j + 1) * WINDOW)
            dq_ref[:, j * 128:(j + 1) * 128] = (jnp.where(low, dqs[0][rows], dqs[1][rows]) * Q_SCALE).astype(MM)
        dk_ref[prev, :] += dkp
        dk_ref[cur, :] += dkc
        dv_ref[prev, :] += dvp
        dv_ref[cur, :] += dvc
        dsink_ref[...] += dsink

        @pl.when(n == n_blk - 1)
        def _():
            bk = bk_ref[...]
            rb = lax.broadcasted_iota(jnp.int32, (N_BUCKETS, N_HEADS), 0)
            cb = lax.broadcasted_iota(jnp.int32, (N_BUCKETS, N_HEADS), 1)
            out = jnp.zeros((N_BUCKETS, N_HEADS), F32)
            for s in range(N_HEADS):
                acc = ds_acc[s]
                for b in range(N_BUCKETS):
                    out = out + jnp.where((rb == b) & (cb == s), jnp.sum(jnp.where(bk == b, acc, 0.0)), 0.0)
            drb_ref[...] = out

    do_spec = pl.BlockSpec((WINDOW, D_ATT), lambda n: (n, 1))
    body, in_specs, operands = _run_after(
        after, body, _swa_specs(S) + [do_spec, _resident(), pl.BlockSpec(memory_space=pltpu.SMEM), _resident(),
                                      _rows(WINDOW, N_HEADS), _rows(WINDOW, N_HEADS)],
        (sqkv, sqkv, sqkv, sqkv, sqkv, dcat, biasm, sinks_slot, bucket, lse, d_col))
    return pl.pallas_call(
        body, name="swa_bwd", grid=(n_blk,), in_specs=in_specs,
        out_specs=[_rows(WINDOW, D_ATT), _const((S, D_KV)), _const((S, D_KV)), _const((N_BUCKETS, N_HEADS)),
                   _const((1, N_HEADS))],
        out_shape=[jax.ShapeDtypeStruct((S, D_ATT), MM), jax.ShapeDtypeStruct((S, D_KV), F32),
                   jax.ShapeDtypeStruct((S, D_KV), F32), jax.ShapeDtypeStruct((N_BUCKETS, N_HEADS), F32),
                   jax.ShapeDtypeStruct((1, N_HEADS), F32)],
        scratch_shapes=[pltpu.VMEM((N_HEADS, WINDOW, 2 * WINDOW), F32)],
        compiler_params=_params("arbitrary"),
    )(*operands)


def _weight_grad_pieces(pieces, b, name, tk):
    S, N = b.shape
    wide = [p for p in pieces if p.shape[1] % tk == 0]
    narrow = pieces[len(wide):]
    assert sum(p.shape[1] for p in narrow) == tk
    counts = [p.shape[1] // tk for p in wide]
    firsts = [sum(counts[:j]) for j in range(len(wide))]
    last = sum(counts)

    def body(*refs):
        b_ref, out_ref = refs[len(pieces)], refs[len(pieces) + 1]
        i = pl.program_id(0)

        def whole_steps(ref, first, count):
            @pl.when((i >= first) & (i < first + count))
            def _():
                out_ref[...] = _dot_tn(ref[...].astype(MM), b_ref[...]).astype(MM)

        for ref, first, count in zip(refs[:len(wide)], firsts, counts):
            whole_steps(ref, first, count)

        @pl.when(i == last)
        def _():
            row = 0
            for ref in refs[len(wide):len(pieces)]:
                k = ref.shape[1]
                out_ref[row:row + k] = _dot_tn(ref[...].astype(MM), b_ref[...]).astype(MM)
                row += k

    def steps_of(first, count):
        return pl.BlockSpec((S, tk), lambda i: (0, jnp.clip(i - first, 0, count - 1)))

    return pl.pallas_call(
        body, name=name, grid=(last + 1,),
        in_specs=[steps_of(first, count) for first, count in zip(firsts, counts)]
        + [pl.BlockSpec((S, p.shape[1]), lambda i: (0, 0)) for p in narrow] + [_resident()],
        out_specs=pl.BlockSpec((tk, N), lambda i: (i, 0)), out_shape=jax.ShapeDtypeStruct(((last + 1) * tk, N), MM),
        compiler_params=_params("parallel"),
    )(*pieces, b)


def _pre_attn_bwd(x, dh1, dz, dff_t, win_t, wt_rest, g1, tm, after=None):
    S = x.shape[0]

    def body(x_ref, dh1_ref, dq_ref, dk_ref, dv_ref, dsq_ref, dsk_ref, dsv_ref, dff_ref, wf_ref, wr_ref, g1_ref,
             dx_ref, dg1_ref):
        i = pl.program_id(0)
        dz_fox = jnp.concatenate([dq_ref[...].astype(MM), dk_ref[...], dv_ref[...]], axis=1)
        dz_swa = jnp.concatenate([dsq_ref[...], dsk_ref[...].astype(MM), dsv_ref[...].astype(MM)], axis=1)
        da = (_dot(dz_fox, wf_ref[...]) + _dot(dz_swa, wr_ref[WT_SQ:WT_REST])
              + _dot_tn(dff_ref[...].astype(MM), wr_ref[0:WT_SQ]))
        n1, r1 = _rms(x_ref[...])
        dx, dg1 = _rms_bwd(da, n1, r1, g1_ref[...])
        _accumulate(dg1_ref, dg1, i)
        dx_ref[...] = dh1_ref[...] + dx

    body, in_specs, operands = _run_after(
        after, body,
        [_rows(tm, D_MODEL), _rows(tm, D_MODEL), *[_rows(tm, d.shape[1]) for d in dz],
         pl.BlockSpec((16, tm), lambda i: (0, i)), _const((WT_FOX, D_MODEL)), _resident(), _const((1, D_MODEL))],
        (x, dh1, *dz, dff_t, win_t, wt_rest, g1))
    return pl.pallas_call(
        body, name="pre_attn_bwd", grid=(S // tm,), in_specs=in_specs,
        out_specs=[_rows(tm, D_MODEL), _const((1, D_MODEL))],
        out_shape=[jax.ShapeDtypeStruct((S, D_MODEL), F32), jax.ShapeDtypeStruct((1, D_MODEL), F32)],
        compiler_params=_params("arbitrary"),
    )(*operands)


def _weight_grad(a, b, name, tk, n_chunks=1, relu2=False):
    S, K = a.shape
    N = b.shape[1]
    cn = N // n_chunks

    def body(a_ref, b_ref, out_ref):
        av = a_ref[...]
        if relu2:
            av = jnp.square(jnp.maximum(av.astype(F32), 0.0))
        av = av.astype(MM)
        for j in range(n_chunks):
            val = _dot_tn(av, b_ref[:, j * cn:(j + 1) * cn].astype(MM)).astype(MM)
            if n_chunks > 1:
                out_ref[j] = val
            else:
                out_ref[...] = val

    if n_chunks > 1:
        out_spec = pl.BlockSpec((n_chunks, tk, cn), lambda i: (0, i, 0))
        out_shape = jax.ShapeDtypeStruct((n_chunks, K, cn), MM)
    else:
        out_spec = pl.BlockSpec((tk, N), lambda i: (i, 0))
        out_shape = jax.ShapeDtypeStruct((K, N), MM)
    return pl.pallas_call(
        body, name=name, grid=(K // tk,),
        in_specs=[pl.BlockSpec((S, tk), lambda i: (0, i)), _resident()],
        out_specs=out_spec, out_shape=out_shape, compiler_params=_params("parallel"),
    )(a, b)


def _weight_grad_two(a1, a2, b, name, tk):
    S, K1 = a1.shape
    K2 = a2.shape[1]
    N = b.shape[1]
    n1 = K1 // tk

    def body(a1_ref, a2_ref, b_ref, out_ref):
        av = jnp.where(pl.program_id(0) < n1, a1_ref[...], a2_ref[...])
        out_ref[...] = _dot_tn(av, b_ref[...]).astype(MM)

    return pl.pallas_call(
        body, name=name, grid=((K1 + K2) // tk,),
        in_specs=[pl.BlockSpec((S, tk), lambda i: (0, jnp.minimum(i, n1 - 1))),
                  pl.BlockSpec((S, tk), lambda i: (0, jnp.maximum(i - n1, 0))), _resident()],
        out_specs=pl.BlockSpec((tk, N), lambda i: (i, 0)), out_shape=jax.ShapeDtypeStruct((K1 + K2, N), MM),
        compiler_params=_params("parallel"),
    )(a1, a2, b)


def _place():
    return lax.axis_index("x"), lax.axis_index("y"), lax.axis_index("c")


def _all_gather_sequencer(stacks, name, collective_id):
    refs = [jax.new_ref(s, memory_space=pltpu.MemorySpace.HBM) for s in stacks]
    n = len(refs)

    @pl.kernel(mesh=plsc.ScalarSubcoreMesh(axis_name="sequencer", num_cores=1), name=name,
               scratch_types=(pltpu.SemaphoreType.DMA((7 * n,)), pltpu.SemaphoreType.DMA((7 * n,))),
               compiler_params=pltpu.CompilerParams(collective_id=collective_id))
    def launch(send_sems, recv_sems):
        x, y, c = _place()
        sibling = (x, y, 1 - c)
        chips = [(1 - x, y), (x, 1 - y), (1 - x, 1 - y)]
        peers = [sibling] + [(px, py, c) for px, py in chips]
        barrier = pltpu.get_barrier_semaphore()
        for peer in peers:
            pl.semaphore_signal(barrier, inc=1, device_id=peer, device_id_type=MESH)
        pl.semaphore_wait(barrier, len(peers))

        def copy(a, k, block, to):
            px, py, pc = block
            slot = refs[a].at[4 * px + 2 * py + pc]
            return _remote(slot, slot, send_sems, recv_sems, 7 * a + k, to)

        first = [copy(a, k, (x, y, c), peer) for a in range(n) for k, peer in enumerate(peers)]
        for cp in first:
            cp.start()
        passed = []
        for j, (px, py) in enumerate(chips):
            for a in range(n):
                copy(a, 1 + j, (px, py, c), sibling).wait_recv()
                passed.append(copy(a, 4 + j, (px, py, c), sibling))
                passed[-1].start()
        for a in range(n):
            copy(a, 0, (x, y, 1 - c), sibling).wait_recv()
            for j, (px, py) in enumerate(chips):
                copy(a, 4 + j, (px, py, 1 - c), sibling).wait_recv()
        for cp in first + passed:
            cp.wait_send()

    launch()
    return [ref[...] for ref in refs]


def _chip_sums(grads, others, name):
    n = len(grads)

    def body(c_ref, *refs):
        for g_ref, o_ref, out_ref in zip(refs[:n], refs[n:2 * n], refs[2 * n:]):
            out_ref[...] = (g_ref[...].astype(F32) + o_ref[...].astype(F32)).astype(out_ref.dtype)

    own = [pl.BlockSpec((None, None) + g.shape[2:], lambda k, c_ref: (k, c_ref[0], 0, 0)) for g in grads]
    chip = [pl.BlockSpec((None,) + g.shape[2:], lambda k, c_ref: (k, 0, 0)) for g in grads]
    return pl.pallas_call(
        body, name=name,
        grid_spec=pltpu.PrefetchScalarGridSpec(num_scalar_prefetch=1, grid=(4,), in_specs=own + chip, out_specs=chip),
        out_shape=[jax.ShapeDtypeStruct((4,) + g.shape[2:], MM) for g in grads],
        compiler_params=_params("parallel"),
    )(lax.axis_index("c").astype(jnp.int32).reshape(1), *grads, *others)


HBM_SPEC = pl.BlockSpec(memory_space=pltpu.HBM)
SEM_SPEC = pl.BlockSpec(memory_space=pltpu.SEMAPHORE)
DATAFLOW = pltpu.SideEffectType.DATAFLOW_SIDE_EFFECTING


def _exchange_start(name, arrays, n_copies, plan):
    n = len(arrays)

    def body(*refs):
        send_sems, recv_sems, token = refs[n], refs[n + 1], refs[2 * n + 2]
        for cp in plan(refs[:n], send_sems, recv_sems):
            cp.start()
        token[...] = jnp.zeros_like(token)

    out = pl.pallas_call(
        body, name=name,
        out_shape=(pltpu.SemaphoreType.DMA((n_copies,)), pltpu.SemaphoreType.DMA((n_copies,)),
                   *[pltpu.HBM(a.shape, a.dtype) for a in arrays], jax.ShapeDtypeStruct((1, D_MODEL), F32)),
        in_specs=[HBM_SPEC] * n,
        out_specs=(SEM_SPEC, SEM_SPEC, *[HBM_SPEC] * n, pl.BlockSpec(memory_space=pltpu.VMEM)),
        input_output_aliases={i: 2 + i for i in range(n)},
        compiler_params=pltpu.CompilerParams(has_side_effects=DATAFLOW),
    )(*[pltpu.with_memory_space_constraint(a, pltpu.HBM) for a in arrays])
    return (out[0], out[1]), list(out[2:2 + n]), out[2 + n]


def _exchange_wait(name, arrays, sems, after, plan):
    n = len(arrays)
    after = list(after) if isinstance(after, (list, tuple)) else [after]

    def body(*refs):
        send_sems, recv_sems = refs[n], refs[n + 1]
        for cp in plan(refs[:n], send_sems, recv_sems):
            cp.wait_send()
            cp.wait_recv()

    out = pl.pallas_call(
        body, name=name, out_shape=[pltpu.HBM(a.shape, a.dtype) for a in arrays],
        in_specs=[HBM_SPEC] * n + [SEM_SPEC, SEM_SPEC] + [pl.BlockSpec(memory_space=pl.ANY)] * len(after),
        out_specs=[HBM_SPEC] * n, input_output_aliases={i: i for i in range(n)},
        compiler_params=pltpu.CompilerParams(has_side_effects=DATAFLOW),
    )(*arrays, sems[0], sems[1], *after)
    return list(out)


def _remote(src, dst, send_sems, recv_sems, k, to):
    return pltpu.make_async_remote_copy(src_ref=src, dst_ref=dst, send_sem=send_sems.at[k], recv_sem=recv_sems.at[k],
                                        device_id=to, device_id_type=MESH)


def _plan_gather_direct(refs, send_sems, recv_sems):
    x, y, c = _place()
    me = 4 * x + 2 * y + c
    peers = [(x, y, 1 - c), (1 - x, y, c), (x, 1 - y, c), (1 - x, 1 - y, c)]
    return [_remote(ref.at[me], ref.at[me], send_sems, recv_sems, 4 * a + k, peer)
            for a, ref in enumerate(refs) for k, peer in enumerate(peers)]


def _plan_gather_pass_on(refs, send_sems, recv_sems):
    x, y, c = _place()
    chips = [(1 - x, y), (x, 1 - y), (1 - x, 1 - y)]
    return [_remote(ref.at[4 * px + 2 * py + c], ref.at[4 * px + 2 * py + c], send_sems, recv_sems, 3 * a + k,
                    (x, y, 1 - c))
            for a, ref in enumerate(refs) for k, (px, py) in enumerate(chips)]


def _plan_in_chip(refs, send_sems, recv_sems):
    n = len(refs) // 2
    x, y, c = _place()
    return [_remote(refs[a].at[:, 1 - c], refs[n + a], send_sems, recv_sems, a, (x, y, 1 - c)) for a in range(n)]


def _plan_between_chips(refs, send_sems, recv_sems):
    n = len(refs) // 2
    x, y, c = _place()
    chips = [(1 - x, y), (x, 1 - y), (1 - x, 1 - y)]
    return [_remote(refs[a].at[2 * px + py], refs[n + a].at[2 * x + y], send_sems, recv_sems, 3 * a + k, (px, py, c))
            for a in range(n) for k, (px, py) in enumerate(chips)]


def _plan_late_between(refs, send_sems, recv_sems):
    sums, land, small = refs
    x, y, c = _place()
    me = 4 * x + 2 * y + c
    copies = _plan_between_chips([sums, land], send_sems, recv_sems)
    peers = [(x ^ dx, y ^ dy, c ^ dc) for dx in range(2) for dy in range(2) for dc in range(2) if dx + dy + dc]
    return copies + [_remote(small.at[me], small.at[me], send_sems, recv_sems, 3 + k, peer)
                     for k, peer in enumerate(peers)]


def _adamw_math(w, g, m, v):
    m = ADAM_B1 * m + (1.0 - ADAM_B1) * g
    v = ADAM_B2 * v + (1.0 - ADAM_B2) * jnp.square(g)
    m_hat = m / (1.0 - ADAM_B1 ** ADAM_STEP)
    v_hat = v / (1.0 - ADAM_B2 ** ADAM_STEP)
    delta = -ADAM_LR * (m_hat / (jnp.sqrt(v_hat) + ADAM_EPS) + ADAM_WD * w)
    return delta, m, v


def _adamw(parts, w, m, v, name):
    n_parts, r, cdim = parts.shape
    tr = 256 if r % 256 == 0 else r

    def body(p_ref, w_ref, m_ref, v_ref, g_out, d_out, m_out, v_out):
        g = p_ref[0].astype(F32)
        for k in range(1, n_parts):
            g = g + p_ref[k].astype(F32)
        delta, m_new, v_new = _adamw_math(w_ref[...], g, m_ref[...], v_ref[...])
        g_out[...] = g
        d_out[...] = delta
        m_out[...] = m_new
        v_out[...] = v_new

    blk = pl.BlockSpec((tr, cdim), lambda i: (i, 0))
    return pl.pallas_call(
        body, name=name, grid=(r // tr,),
        in_specs=[pl.BlockSpec((n_parts, tr, cdim), lambda i: (0, i, 0)), blk, blk, blk],
        out_specs=[blk] * 4, out_shape=[jax.ShapeDtypeStruct((r, cdim), F32)] * 4,
        compiler_params=_params("parallel"),
    )(parts, w, m, v)


def _adamw_chips(parts, sums, w, m, v, name):
    _, r, cdim = parts.shape
    tr = 256 if r % 256 == 0 else r

    def body(chip_ref, p_ref, own_ref, w_ref, m_ref, v_ref, g_out, d_out, m_out, v_out):
        g = None
        for k in range(4):
            term = jnp.where(chip_ref[0] == k, own_ref[...], p_ref[k]).astype(F32)
            g = term if g is None else g + term
        delta, m_new, v_new = _adamw_math(w_ref[...], g, m_ref[...], v_ref[...])
        g_out[...] = g
        d_out[...] = delta
        m_out[...] = m_new
        v_out[...] = v_new

    blk = pl.BlockSpec((tr, cdim), lambda i, chip: (i, 0))
    my_chip = (2 * lax.axis_index("x") + lax.axis_index("y")).astype(jnp.int32).reshape(1)
    return pl.pallas_call(
        body, name=name,
        grid_spec=pltpu.PrefetchScalarGridSpec(
            num_scalar_prefetch=1, grid=(r // tr,),
            in_specs=[pl.BlockSpec((4, tr, cdim), lambda i, chip: (0, i, 0)),
                      pl.BlockSpec((None, tr, cdim), lambda i, chip: (chip[0], i, 0)), blk, blk, blk],
            out_specs=[blk] * 4),
        out_shape=[jax.ShapeDtypeStruct((r, cdim), F32)] * 4,
        compiler_params=_params("parallel"),
    )(my_chip, parts, sums, w, m, v)


class _NoExchange:
    def __init__(self, weights):
        self.weights = weights

    def before_pre_attn(self):
        return None

    def after_fox_fwd(self, fox_o):
        return None

    def after_attention(self, swa_o):
        return self.weights

    def after_early_grads(self, grads):
        return None

    def after_swa_bwd(self, dsq):
        return None

    def after_w_in_grad(self, d_win):
        return None


def _slot_order(t, axis):
    shp = t.shape
    t = t.reshape(shp[:axis] + (2, 4, shp[axis] // N_HEADS) + shp[axis + 1:])
    return jnp.swapaxes(t, axis, axis + 1).reshape(shp)


def _head_order(t, axis):
    shp = t.shape
    t = t.reshape(shp[:axis] + (4, 2, shp[axis] // N_HEADS) + shp[axis + 1:])
    return jnp.swapaxes(t, axis, axis + 1).reshape(shp)


def _forward_backward(x, p, target, win_t, hooks, b_forget, rel_bias, sinks, g1, g2, g3, g4, g5):
    S = x.shape[0]
    tm = 512
    tm_mlp = 512
    t = 256
    q0 = 3 * D_ATT + N_HEADS
    win_t = win_t.reshape(D_IN, D_MODEL)
    wt_rest = jnp.concatenate(
        [win_t[WT_FOX:q0], jnp.zeros((8, D_MODEL), MM), _slot_order(win_t[q0:q0 + D_ATT], 0), win_t[q0 + D_ATT:]],
        axis=0)
    bcol = jnp.pad(b_forget.reshape(N_HEADS, 1), ((0, 8), (0, 0)))
    rel_bias_slot = rel_bias[:, np.array(SLOT_HEAD)]
    sinks_slot = sinks.reshape(N_HEADS)[np.array(SLOT_HEAD)]
    bucket = jnp.asarray(_swa_bucket_map())

    a, fqkv, sqkv, fft = _pre_attn(x, g1, win_t, wt_rest, tm, after=hooks.before_pre_attn())
    c_row = _forget_cumsum(fft, bcol)
    c_col = c_row[:N_HEADS].T
    c_row3 = c_row[:N_HEADS].reshape(N_HEADS, S // t, t)
    fox_o, fox_lse = _fox_fwd(fqkv, c_row3, tq=512, tk=t)
    biasm = _swa_bias(rel_bias_slot, bucket)
    swa_o, swa_lse = _swa_fwd(sqkv, biasm, sinks_slot, after=hooks.after_fox_fwd(fox_o))
    wout, w1, w2, wple, wg = hooks.after_attention(swa_o)
    wout_fox = wout[:D_ATT]
    wout_swa = _slot_order(wout[D_ATT:], 0)
    mix, h1, m = _post_attn(x, fox_o, swa_o, wout_fox, wout_swa, g2, g3, tm)
    u, y, h2 = _mlp_fwd(m, h1, w1, w2, g4, tm_mlp)
    dh2, dpe, dgl, dg5, loss = _ple_loss(h2, p, target, wg, wple, g5, tm)

    d_wple = _weight_grad(p, dpe, "grad_w_ple", tk=D_PLE, n_chunks=N_DEV)
    d_wg = _weight_grad(h2, dgl, "grad_w_ple_gate", tk=256)
    dh1, dy, du, dg4, dg3 = _mlp_bwd(dh2, y, h1, u, w1, w2, g4, g3, tm)
    d_w2 = _weight_grad(u, dy, "grad_w_ff2", tk=256, relu2=True)
    d_w1 = _weight_grad(m, du, "grad_w_ff1", tk=256, n_chunks=N_DEV)
    head = np.arange(D_ATT) // HEAD_DIM
    head_rows = jnp.asarray((head[None, :] == np.arange(N_HEADS)[:, None]).astype(np.float32))
    dmix, dcat, d_row, d_swa, dg2 = _attn_out_bwd(dh1, mix, fox_o, swa_o, wout_fox, wout_swa, g2, head_rows, tm)
    d_col = d_swa.T
    d_wout = _weight_grad_two(fox_o, swa_o, dmix, "grad_w_out", tk=256)
    d_wout = jnp.concatenate([d_wout[:D_ATT], _head_order(d_wout[D_ATT:], 0)], axis=0)
    d_wout = d_wout.reshape(N_DEV, D_MODEL // N_DEV, D_MODEL)
    early = dict(w_ff1=d_w1, w_ff2=d_w2.reshape(N_DEV, FF_CHUNK, D_MODEL), w_ple=d_wple,
                 w_ple_gate=d_wg.reshape(N_DEV, D_MODEL // N_DEV, D_MODEL), w_out=d_wout)

    dsq, dsk, dsv, d_rb_slot, d_sink_slot = _swa_bwd(sqkv, dcat, biasm, sinks_slot, bucket, swa_lse, d_col,
                                                     after=hooks.after_early_grads(early))
    lse_row3 = fox_lse.T.reshape(N_HEADS, S // t, t)
    d_row3 = d_row.reshape(N_HEADS, S // t, t)
    dq_fox, dk_fox, dv_fox, dc_col, dcq = _fox_bwd(fqkv, dcat, lse_row3, d_row3, c_col, tq=t, tk=512,
                                                  after=hooks.after_swa_bwd(dsq))
    dc_row = jnp.pad(dc_col.T + dcq.reshape(N_HEADS, S), ((0, 8), (0, 0)))
    dff_t, db, d_wff_t = _forget_bwd(dc_row, fft, bcol, a)
    dz = [dq_fox, dk_fox, dv_fox, dsq, dsk, dsv]
    d_wmain = _weight_grad_pieces(dz, a, "grad_w_in", tk=256)

    sq0 = 3 * D_ATT
    d_win = jnp.concatenate(
        [d_wmain[:sq0], d_wff_t[:N_HEADS].astype(MM), _head_order(d_wmain[sq0:sq0 + D_ATT], 0),
         d_wmain[sq0 + D_ATT:]], axis=0)
    d_win = d_win.reshape(N_DEV, D_IN // N_DEV, D_MODEL)
    grad_x, dg1 = _pre_attn_bwd(x, dh1, dz, dff_t, win_t, wt_rest, g1, tm, after=hooks.after_w_in_grad(d_win))
    big = dict(early, w_in=d_win)
    small = dict(b_forget=db[:N_HEADS].reshape(1, N_HEADS), rel_bias=d_rb_slot[:, np.array(HEAD_SLOT)],
                 swa_sinks=d_sink_slot[:, np.array(HEAD_SLOT)], g_attn_pre=dg1, g_attn_post=dg2, g_ff_pre=dg3,
                 g_ff_post=dg4, g_ple_post=dg5)
    return loss, grad_x, big, small


BIG = ("w_in", "w_out", "w_ff1", "w_ff2", "w_ple", "w_ple_gate")
SMALL_ROWS = ("g_attn_pre", "g_attn_post", "g_ff_pre", "g_ff_post", "g_ple_post")
WEIGHTS =("w_in", "b_forget", "w_out", "rel_bias", "swa_sinks", "g_attn_pre", "g_attn_post", "w_ff1", "w_ff2",
           "g_ff_pre", "g_ff_post", "w_ple", "w_ple_gate", "g_ple_post")


EARLY = ("w_ff1", "w_ff2", "w_ple", "w_ple_gate", "w_out")


class _Overlap:
    def __init__(self, later):
        self.later = later

    def before_pre_attn(self):
        self.gather_sems, self.later, token = _exchange_start("gather_rest_start", self.later, 4 * 5, _plan_gather_direct)
        return token

    def after_fox_fwd(self, fox_o):
        later = _exchange_wait("gather_rest_wait", self.later, self.gather_sems, fox_o, _plan_gather_direct)
        self.pass_sems, self.later, token = _exchange_start("gather_pass_on_start", later, 3 * 5, _plan_gather_pass_on)
        return token

    def after_attention(self, swa_o):
        wout_g, w1_g, w2_g, wple_g, wg_g = _exchange_wait("gather_pass_on_wait", self.later, self.pass_sems, swa_o,
                                                         _plan_gather_pass_on)
        return (wout_g.reshape(D_MODEL, D_MODEL), w1_g, w2_g.reshape(D_FF, D_MODEL),
                jnp.moveaxis(wple_g, 0, 1).reshape(D_PLE, D_MODEL), wg_g.reshape(D_MODEL, D_MODEL))

    def after_early_grads(self, grads):
        views = [grads[k].reshape((4, 2) + grads[k].shape[1:]) for k in EARLY]
        lands = [lax.empty((4,) + grads[k].shape[1:], MM) for k in EARLY]
        self.in_chip_sems, self.in_chip, token = _exchange_start("grads_in_chip_start", views + lands, len(EARLY),
                                                                 _plan_in_chip)
        return token

    def after_swa_bwd(self, dsq):
        arrays = _exchange_wait("grads_in_chip_wait", self.in_chip, self.in_chip_sems, dsq, _plan_in_chip)
        n = len(EARLY)
        sums = list(_chip_sums(arrays[:n], arrays[n:], "chip_sums_early"))
        lands = [lax.empty(s.shape, s.dtype) for s in sums]
        self.between_sems, self.between, token = _exchange_start("grads_between_chips_start", sums + lands, 3 * n,
                                                                 _plan_between_chips)
        return token

    def after_w_in_grad(self, d_win):
        self.late_in_chip_sems, self.late_in_chip, token = _exchange_start(
            "late_in_chip_start", [d_win.reshape((4, 2) + d_win.shape[1:]), lax.empty((4,) + d_win.shape[1:], MM)],
            1, _plan_in_chip)
        return token

    def finish(self, after):
        arrays = _exchange_wait("grads_between_chips_wait", self.between, self.between_sems, after,
                                _plan_between_chips)
        n = len(EARLY)
        self.sums = arrays[:n]
        return arrays[n:]


def _pack_small(t):
    rows = [t[k].reshape(1, D_MODEL) for k in SMALL_ROWS]
    misc = jnp.concatenate([t["b_forget"].reshape(-1), t["swa_sinks"].reshape(-1), t["rel_bias"].reshape(-1)])
    rows.append(jnp.pad(misc, (0, D_MODEL - misc.shape[0])).reshape(1, D_MODEL))
    rows.append(jnp.pad(t["loss"].reshape(-1), (0, D_MODEL - 1)).reshape(1, D_MODEL))
    rows.append(jnp.zeros((1, D_MODEL), F32))
    return jnp.concatenate(rows, axis=0).astype(F32)


def _unpack_small(blk):
    out = {k: blk[i].reshape(1, D_MODEL) for i, k in enumerate(SMALL_ROWS)}
    misc = blk[len(SMALL_ROWS)]
    out["b_forget"] = misc[:N_HEADS].reshape(1, N_HEADS)
    out["swa_sinks"] = misc[N_HEADS:2 * N_HEADS].reshape(1, N_HEADS)
    out["rel_bias"] = misc[2 * N_HEADS:2 * N_HEADS + N_BUCKETS * N_HEADS].reshape(N_BUCKETS, N_HEADS)
    out["loss"] = blk[len(SMALL_ROWS) + 1, 0]
    return out


def kernel(x, p, w_in, b_forget, w_out, rel_bias, swa_sinks, g_attn_pre, g_attn_post, w_ff1, w_ff2, g_ff_pre, g_ff_post, w_ple, w_ple_gate, g_ple_post, loss_target, m_w_in, m_b_forget, m_w_out, m_rel_bias, m_swa_sinks, m_g_attn_pre, m_g_attn_post, m_w_ff1, m_w_ff2, m_g_ff_pre, m_g_ff_post, m_w_ple, m_w_ple_gate, m_g_ple_post, v_w_in, v_b_forget, v_w_out, v_rel_bias, v_swa_sinks, v_g_attn_pre, v_g_attn_post, v_w_ff1, v_w_ff2, v_g_ff_pre, v_g_ff_post, v_w_ple, v_w_ple_gate, v_g_ple_post):
    w = dict(w_in=w_in, b_forget=b_forget, w_out=w_out, rel_bias=rel_bias, swa_sinks=swa_sinks,
             g_attn_pre=g_attn_pre, g_attn_post=g_attn_post, w_ff1=w_ff1, w_ff2=w_ff2, g_ff_pre=g_ff_pre,
             g_ff_post=g_ff_post, w_ple=w_ple, w_ple_gate=w_ple_gate, g_ple_post=g_ple_post)
    mom = dict(w_in=m_w_in, b_forget=m_b_forget, w_out=m_w_out, rel_bias=m_rel_bias, swa_sinks=m_swa_sinks,
               g_attn_pre=m_g_attn_pre, g_attn_post=m_g_attn_post, w_ff1=m_w_ff1, w_ff2=m_w_ff2,
               g_ff_pre=m_g_ff_pre, g_ff_post=m_g_ff_post, w_ple=m_w_ple, w_ple_gate=m_w_ple_gate,
               g_ple_post=m_g_ple_post)
    var = dict(w_in=v_w_in, b_forget=v_b_forget, w_out=v_w_out, rel_bias=v_rel_bias, swa_sinks=v_swa_sinks,
               g_attn_pre=v_g_attn_pre, g_attn_post=v_g_attn_post, w_ff1=v_w_ff1, w_ff2=v_w_ff2,
               g_ff_pre=v_g_ff_pre, g_ff_post=v_g_ff_post, w_ple=v_w_ple, w_ple_gate=v_w_ple_gate,
               g_ple_post=v_g_ple_post)

    turn = lambda t, k: t.T if k == "w_in" else t
    me = 4 * lax.axis_index("x") + 2 * lax.axis_index("y") + lax.axis_index("c")

    def stack(block):
        return lax.dynamic_update_slice_in_dim(lax.empty((N_DEV,) + block.shape, block.dtype), block[None], me, 0)

    stacks = [stack(turn(w[k][0], k).astype(MM)) for k in BIG]
    (win_g,), later = _all_gather_sequencer(stacks[:1], "all_gather_sequencer", 1), stacks[1:]
    hooks = _Overlap(later)
    loss, grad_x, big, small = _forward_backward(
        x[0], p[0, 0], loss_target[0], win_g, hooks, b_forget, rel_bias, swa_sinks,
        g_attn_pre, g_attn_post, g_ff_pre, g_ff_post, g_ple_post)
    out_g, out_d, out_m, out_v = {}, {}, {}, {}

    def update(k, part, own):
        g, d, m_new, v_new = _adamw_chips(part, own, turn(w[k][0], k), turn(mom[k][0], k), turn(var[k][0], k),
                                          "adamw_" + k)
        out_g[k], out_d[k], out_m[k], out_v[k] = turn(g, k)[None], turn(d, k)[None], turn(m_new, k)[None], turn(v_new, k)[None]
        return d

    view, other = _exchange_wait("late_in_chip_wait", hooks.late_in_chip, hooks.late_in_chip_sems, grad_x,
                                 _plan_in_chip)
    (chip_sum,) = _chip_sums([view], [other], "chip_sum_w_in")
    small["loss"] = loss
    between_sems, between, token = _exchange_start(
        "late_between_chips_start", [chip_sum, lax.empty(chip_sum.shape, MM), stack(_pack_small(small))], 3 + 7,
        _plan_late_between)
    early_parts = hooks.finish(token)
    done = [update(k, part, own) for k, part, own in zip(EARLY, early_parts, hooks.sums)]
    chip_sum, part, small_all = _exchange_wait("late_between_chips_wait", between, between_sems, done,
                                               _plan_late_between)
    update("w_in", part, chip_sum)
    rep = {k: w[k] for k in w if k not in BIG}
    rep["loss"] = jnp.zeros((), F32)
    rep_m = {k: mom[k] for k in mom if k not in BIG}
    rep_m["loss"] = jnp.zeros((), F32)
    rep_v = {k: var[k] for k in var if k not in BIG}
    rep_v["loss"] = jnp.ones((), F32)
    g_s, d_s, m_s, v_s = _adamw(small_all, _pack_small(rep), _pack_small(rep_m), _pack_small(rep_v), "adamw_small")
    g_s, d_s, m_s, v_s = _unpack_small(g_s), _unpack_small(d_s), _unpack_small(m_s), _unpack_small(v_s)
    for k in w:
        if k not in BIG:
            out_g[k], out_d[k], out_m[k], out_v[k] = g_s[k], d_s[k], m_s[k], v_s[k]
    return (g_s["loss"], grad_x[None], *[out_g[k] for k in WEIGHTS], *[out_d[k] for k in WEIGHTS],
            *[out_m[k] for k in WEIGHTS], *[out_v[k] for k in WEIGHTS])
```

```python
import functools

import numpy as np
import jax
import jax.numpy as jnp
from jax import lax
from jax.experimental import pallas as pl
from jax.experimental.pallas import tpu as pltpu
from jax.experimental.pallas import tpu_sc as plsc

F32 = jnp.float32
MM = jnp.bfloat16

D_MODEL = 1024
HEAD_DIM = 64
N_HEADS = 8
D_ATT = N_HEADS * HEAD_DIM
D_KV = 128
D_FF = 4096
D_PLE = 256
D_IN = 3 * D_ATT + N_HEADS + D_ATT + 2 * D_KV
N_DEV = 8
FF_CHUNK = D_FF // N_DEV
WINDOW = 128
N_BUCKETS = 32
MAX_DISTANCE = 128
RMS_EPS = 1e-6
Q_SCALE = HEAD_DIM ** -0.5
NEG = -1e30

ADAM_LR = 0.001
ADAM_B1 = 0.9
ADAM_B2 = 0.999
ADAM_EPS = 1e-08
ADAM_WD = 0.01
ADAM_STEP = 10

SLOT_HEAD = (0, 4, 1, 5, 2, 6, 3, 7)
HEAD_SLOT = (0, 2, 4, 6, 1, 3, 5, 7)

VMEM_LIMIT = 60 * 1024 * 1024
MESH = pl.DeviceIdType.MESH

NT = (((1,), (1,)), ((), ()))
TN = (((0,), (0,)), ((), ()))


def _params(*semantics):
    return pltpu.CompilerParams(dimension_semantics=semantics, vmem_limit_bytes=VMEM_LIMIT)


def _resident():
    return pl.BlockSpec(memory_space=pltpu.VMEM)


def _rows(tm, width):
    return pl.BlockSpec((tm, width), lambda i: (i, 0))


def _const(shape):
    return pl.BlockSpec(shape, lambda i: (0,) * len(shape))


def _dot(a, b):
    return jnp.dot(a, b, preferred_element_type=F32)


def _dot_nt(a, b):
    return lax.dot_general(a, b, NT, preferred_element_type=F32)


def _dot_tn(a, b):
    return lax.dot_general(a, b, TN, preferred_element_type=F32)


def _rms(xf):
    r = lax.rsqrt(jnp.mean(xf * xf, axis=-1, keepdims=True) + RMS_EPS)
    return xf * r, r


def _rms_bwd(dout, n, r, g):
    dg = jnp.sum(dout * n, axis=0, keepdims=True)
    dn = dout * g
    dx = r * (dn - n * jnp.mean(dn * n, axis=-1, keepdims=True))
    return dx, dg


def _run_after(after, body, in_specs, operands):
    if after is None:
        return body, list(in_specs), tuple(operands)
    n = len(operands)
    return ((lambda *refs: body(*refs[:n], *refs[n + 1:])), list(in_specs) + [pl.BlockSpec(memory_space=pl.ANY)],
            tuple(operands) + (after,))


def _accumulate(ref, value, step):
    @pl.when(step == 0)
    def _():
        ref[...] = value

    @pl.when(step != 0)
    def _():
        ref[...] += value


def _t5_bucket(n):
    max_exact = N_BUCKETS // 2
    large = max_exact + (np.log(np.maximum(n, 1) / max_exact) / np.log(MAX_DISTANCE / max_exact)
                         * (N_BUCKETS - max_exact)).astype(np.int32)
    large = np.minimum(large, N_BUCKETS - 1)
    return np.where(n < max_exact, n, large).astype(np.int32)


def _swa_bucket_map():
    i = np.arange(WINDOW)[:, None]
    j = np.arange(2 * WINDOW)[None, :]
    dist = i + WINDOW - j
    ok = (dist >= 0) & (dist < WINDOW)
    return np.where(ok, _t5_bucket(np.clip(dist, 0, None)), -1).astype(np.int32)


WT_FOX = 3 * D_ATT
WT_SQ = 16
WT_SKV = WT_SQ + D_ATT
WT_REST = WT_SKV + 2 * D_KV


def _pre_attn(x, g1, win_t, wt_rest, tm, after=None):
    S = x.shape[0]

    def body(x_ref, g_ref, wf_ref, wr_ref, a_ref, fqkv_ref, sqkv_ref, fft_ref):
        n, _ = _rms(x_ref[...])
        a = (n * g_ref[...]).astype(MM)
        a_ref[...] = a
        fqkv_ref[:, :D_ATT] = (_dot_nt(a, wf_ref[0:D_ATT]) * Q_SCALE).astype(MM)
        fqkv_ref[:, D_ATT:] = _dot_nt(a, wf_ref[D_ATT:WT_FOX]).astype(MM)
        sqkv_ref[:, :D_ATT] = (_dot_nt(a, wr_ref[WT_SQ:WT_SKV]) * Q_SCALE).astype(MM)
        sqkv_ref[:, D_ATT:] = _dot_nt(a, wr_ref[WT_SKV:WT_REST]).astype(MM)
        fft_ref[...] = _dot_nt(wr_ref[0:WT_SQ], a)

    body, in_specs, operands = _run_after(
        after, body, [_rows(tm, D_MODEL), _const((1, D_MODEL)), _const((WT_FOX, D_MODEL)), _resident()],
        (x, g1, win_t, wt_rest))
    return pl.pallas_call(
        body, name="pre_attn", grid=(S // tm,), in_specs=in_specs,
        out_specs=[_rows(tm, D_MODEL), _rows(tm, 3 * D_ATT), _rows(tm, D_ATT + 2 * D_KV),
                   pl.BlockSpec((16, tm), lambda i: (0, i))],
        out_shape=[jax.ShapeDtypeStruct((S, D_MODEL), MM), jax.ShapeDtypeStruct((S, 3 * D_ATT), MM),
                   jax.ShapeDtypeStruct((S, D_ATT + 2 * D_KV), MM), jax.ShapeDtypeStruct((16, S), F32)],
        compiler_params=_params("parallel"),
    )(*operands)


def _lane_scan(v, reverse):
    S = v.shape[1]
    lane = lax.broadcasted_iota(jnp.int32, v.shape, 1)
    k = 1
    while k < S:
        if reverse:
            v = v + jnp.where(lane < S - k, pltpu.roll(v, S - k, axis=1), 0.0)
        else:
            v = v + jnp.where(lane >= k, pltpu.roll(v, k, axis=1), 0.0)
        k *= 2
    return v


def _forget_cumsum(fft, bcol):
    def body(f_ref, b_ref, c_ref):
        z = f_ref[...] + b_ref[...]
        log_f = jnp.minimum(z, 0.0) - jnp.log1p(jnp.exp(-jnp.abs(z)))
        c_ref[...] = _lane_scan(log_f, reverse=False)

    return pl.pallas_call(
        body, name="forget_cumsum", out_shape=jax.ShapeDtypeStruct(fft.shape, F32),
        in_specs=[_resident(), _resident()], out_specs=_resident(),
    )(fft, bcol)


def _forget_bwd(dc_row, fft, bcol, a):
    def body(dc_ref, f_ref, b_ref, a_ref, dff_ref, db_ref, dw_ref):
        z = f_ref[...] + b_ref[...]
        dlog_f = _lane_scan(dc_ref[...], reverse=True)
        dff = dlog_f * (1.0 / (1.0 + jnp.exp(z)))
        dff_ref[...] = dff
        db_ref[...] = jnp.sum(dff, axis=1, keepdims=True)
        dw_ref[...] = _dot(dff.astype(MM), a_ref[...])

    return pl.pallas_call(
        body, name="forget_bwd",
        out_shape=[jax.ShapeDtypeStruct(fft.shape, F32), jax.ShapeDtypeStruct((fft.shape[0], 1), F32),
                   jax.ShapeDtypeStruct((fft.shape[0], D_MODEL), F32)],
        in_specs=[_resident()] * 4, out_specs=[_resident()] * 3,
    )(dc_row, fft, bcol, a)


def _head_select(shape, upper):
    lane = lax.broadcasted_iota(jnp.int32, shape, 1)
    return lane >= HEAD_DIM if upper else lane < HEAD_DIM


def _fox_fwd(fqkv, c_row3, tq, tk, pairs_per_loop=2, row_chunks=1):
    S = fqkv.shape[0]
    rq = tq // row_chunks
    n_band = tq // tk

    def body(q_ref, k_ref, v_ref, ck_ref, o_ref, lse_ref):
        qi = pl.program_id(0)
        row = lax.broadcasted_iota(jnp.int32, (rq, tk), 0)
        col = lax.broadcasted_iota(jnp.int32, (rq, tk), 1)
        low = _head_select((rq, 128), 0)
        for first in range(0, N_HEADS // 2, pairs_per_loop):
            pairs = range(first, first + pairs_per_loop)
            chains = [(pr, hh, rc) for pr in pairs for hh in range(2) for rc in range(row_chunks)]
            qh = {}
            for pr in pairs:
                for rc in range(row_chunks):
                    q2 = q_ref[rc * rq:(rc + 1) * rq, pr * 128:(pr + 1) * 128]
                    qh[pr, 0, rc] = jnp.where(low, q2, jnp.zeros_like(q2))
                    qh[pr, 1, rc] = jnp.where(low, jnp.zeros_like(q2), q2)

            def block(kb, carry, band, chains=chains, qh=qh):
                rows = pl.ds(pl.multiple_of(kb * tk, tk), tk)
                out = []
                for (pr, hh, rc), (m, l, acc) in zip(chains, carry):
                    if band is not None and (rc + 1) * rq <= band * tk:
                        out.append((m, l, acc))
                        continue
                    lanes = slice(pr * 128, (pr + 1) * 128)
                    s = _dot_nt(qh[pr, hh, rc], k_ref[rows, lanes]) - ck_ref[2 * pr + hh, pl.ds(kb, 1), :]
                    if band is not None:
                        s = jnp.where(row + rc * rq >= col + band * tk, s, NEG)
                    m_new = jnp.maximum(m, jnp.max(s, axis=-1, keepdims=True))
                    p = jnp.exp(s - m_new)
                    alpha = jnp.exp(m - m_new)
                    l = alpha * l + jnp.sum(p, axis=-1, keepdims=True)
                    acc = alpha * acc + _dot(p.astype(MM), v_ref[rows, lanes])
                    out.append((m_new, l, acc))
                return tuple(out)

            carry = tuple((jnp.full((rq, 1), NEG, F32), jnp.zeros((rq, 1), F32), jnp.zeros((rq, 128), F32))
                          for _ in chains)
            carry = lax.fori_loop(0, qi * n_band, functools.partial(block, band=None), carry)
            for band in range(n_band):
                carry = block(qi * n_band + band, carry, band=band)
            res = {}
            for (pr, hh, rc), (m, l, acc) in zip(chains, carry):
                res[pr, hh, rc] = acc / l
                lse_ref[rc * rq:(rc + 1) * rq, 2 * pr + hh:2 * pr + hh + 1] = m + jnp.log(l)
            for pr in pairs:
                for rc in range(row_chunks):
                    o_ref[rc * rq:(rc + 1) * rq, pr * 128:(pr + 1) * 128] = jnp.where(
                        low, res[pr, 0, rc], res[pr, 1, rc]).astype(MM)

    return pl.pallas_call(
        body, name="fox_fwd", grid=(S // tq,),
        in_specs=[pl.BlockSpec((tq, D_ATT), lambda i: (i, 0)), pl.BlockSpec((S, D_ATT), lambda i: (0, 1)),
                  pl.BlockSpec((S, D_ATT), lambda i: (0, 2)), _resident()],
        out_specs=[_rows(tq, D_ATT), _rows(tq, N_HEADS)],
        out_shape=[jax.ShapeDtypeStruct((S, D_ATT), MM), jax.ShapeDtypeStruct((S, N_HEADS), F32)],
        compiler_params=_params("parallel"),
    )(fqkv, fqkv, fqkv, c_row3)


def _swa_bias(rel_bias_slot, bucket):
    def body(rb_ref, bk_ref, out_ref):
        bk = bk_ref[...]
        for s in range(N_HEADS):
            acc = jnp.where(bk < 0, NEG, 0.0).astype(F32)
            for b in range(N_BUCKETS):
                acc = jnp.where(bk == b, rb_ref[b, s], acc)
            out_ref[s] = acc

    return pl.pallas_call(
        body, name="swa_bias", out_shape=jax.ShapeDtypeStruct((N_HEADS, WINDOW, 2 * WINDOW), F32),
        in_specs=[pl.BlockSpec(memory_space=pltpu.SMEM), _resident()], out_specs=_resident(),
    )(rel_bias_slot, bucket)


def _stack4(piece):
    return jnp.concatenate([piece(j) for j in range(4)], axis=0)


def _swa_specs(S):
    q = pl.BlockSpec((WINDOW, D_ATT), lambda n: (n, 0))
    kp = pl.BlockSpec((WINDOW, D_KV), lambda n: (jnp.maximum(n - 1, 0), 4))
    kc = pl.BlockSpec((WINDOW, D_KV), lambda n: (n, 4))
    vp = pl.BlockSpec((WINDOW, D_KV), lambda n: (jnp.maximum(n - 1, 0), 5))
    vc = pl.BlockSpec((WINDOW, D_KV), lambda n: (n, 5))
    return [q, kp, kc, vp, vc]


def _swa_fwd(sqkv, biasm, sinks_slot, after=None):
    S = sqkv.shape[0]

    def body(q_ref, kp_ref, kc_ref, vp_ref, vc_ref, bias_ref, sink_ref, o_ref, lse_ref):
        n = pl.program_id(0)
        no_prev = jnp.where(n > 0, 0.0, NEG)
        low = _head_select((WINDOW, 128), 0)
        res = []
        for g in range(2):
            sel = low if g == 0 else jnp.logical_not(low)
            qg = _stack4(lambda j: jnp.where(sel, q_ref[:, j * 128:(j + 1) * 128], jnp.zeros((WINDOW, 128), MM)))
            sink = _stack4(lambda j: jnp.full((WINDOW, 1), sink_ref[2 * j + g], F32))
            sp = _dot_nt(qg, kp_ref[...]) + _stack4(lambda j: bias_ref[2 * j + g, :, :WINDOW]) + no_prev
            sc = _dot_nt(qg, kc_ref[...]) + _stack4(lambda j: bias_ref[2 * j + g, :, WINDOW:])
            m = jnp.maximum(jnp.maximum(jnp.max(sp, axis=-1, keepdims=True),
                                        jnp.max(sc, axis=-1, keepdims=True)), sink)
            ep = jnp.exp(sp - m)
            ec = jnp.exp(sc - m)
            den = jnp.sum(ep, axis=-1, keepdims=True) + jnp.sum(ec, axis=-1, keepdims=True) + jnp.exp(sink - m)
            res.append((_dot(ep.astype(MM), vp_ref[...]) + _dot(ec.astype(MM), vc_ref[...])) / den)
            lse = m + jnp.log(den)
            for j in range(4):
                lse_ref[:, 2 * j + g:2 * j + g + 1] = lse[j * WINDOW:(j + 1) * WINDOW]
        for j in range(4):
            rows = slice(j * WINDOW, (j + 1) * WINDOW)
            o_ref[:, j * 128:(j + 1) * 128] = jnp.where(low, res[0][rows], res[1][rows]).astype(MM)

    body, in_specs, operands = _run_after(
        after, body, _swa_specs(S) + [_resident(), pl.BlockSpec(memory_space=pltpu.SMEM)],
        (sqkv, sqkv, sqkv, sqkv, sqkv, biasm, sinks_slot))
    return pl.pallas_call(
        body, name="swa_fwd", grid=(S // WINDOW,), in_specs=in_specs,
        out_specs=[_rows(WINDOW, D_ATT), _rows(WINDOW, N_HEADS)],
        out_shape=[jax.ShapeDtypeStruct((S, D_ATT), MM), jax.ShapeDtypeStruct((S, N_HEADS), F32)],
        compiler_params=_params("parallel"),
    )(*operands)


def _post_attn(x, fox_o, swa_o, wout_fox, wout_swa, g2, g3, tm):
    S = x.shape[0]

    def body(x_ref, fo_ref, so_ref, wf_ref, ws_ref, g2_ref, g3_ref, mix_ref, h1_ref, m_ref):
        mix = _dot(fo_ref[...], wf_ref[...]) + _dot(so_ref[...], ws_ref[...])
        mix_ref[...] = mix
        n2, _ = _rms(mix)
        h1 = x_ref[...] + n2 * g2_ref[...]
        h1_ref[...] = h1
        n3, _ = _rms(h1)
        m_ref[...] = (n3 * g3_ref[...]).astype(MM)

    return pl.pallas_call(
        body, name="post_attn", grid=(S // tm,),
        in_specs=[_rows(tm, D_MODEL), _rows(tm, D_ATT), _rows(tm, D_ATT), _resident(), _resident(),
                  _const((1, D_MODEL)), _const((1, D_MODEL))],
        out_specs=[_rows(tm, D_MODEL)] * 3,
        out_shape=[jax.ShapeDtypeStruct((S, D_MODEL), F32), jax.ShapeDtypeStruct((S, D_MODEL), F32),
                   jax.ShapeDtypeStruct((S, D_MODEL), MM)],
        compiler_params=_params("parallel"),
    )(x, fox_o, swa_o, wout_fox, wout_swa, g2, g3)


def _mlp_fwd(m, h1, w1, w2, g4, tm):
    S = m.shape[0]

    def body(m_ref, h1_ref, w1_ref, w2_ref, g4_ref, u_ref, y_ref, h2_ref):
        mb = m_ref[...]
        y = jnp.zeros((tm, D_MODEL), F32)
        for j in range(N_DEV):
            cols = slice(j * FF_CHUNK, (j + 1) * FF_CHUNK)
            u = _dot(mb, w1_ref[j])
            u_ref[:, cols] = u.astype(MM)
            y = y + _dot(jnp.square(jnp.maximum(u, 0.0)).astype(MM), w2_ref[cols, :])
        y_ref[...] = y
        n4, _ = _rms(y)
        h2_ref[...] = h1_ref[...] + n4 * g4_ref[...]

    return pl.pallas_call(
        body, name="mlp_fwd", grid=(S // tm,),
        in_specs=[_rows(tm, D_MODEL), _rows(tm, D_MODEL), _resident(), _resident(), _const((1, D_MODEL))],
        out_specs=[_rows(tm, D_FF), _rows(tm, D_MODEL), _rows(tm, D_MODEL)],
        out_shape=[jax.ShapeDtypeStruct((S, D_FF), MM), jax.ShapeDtypeStruct((S, D_MODEL), F32),
                   jax.ShapeDtypeStruct((S, D_MODEL), F32)],
        compiler_params=_params("parallel"),
    )(m, h1, w1, w2, g4)


def _ple_loss(h2, p, target, wg, wple, g5, tm):
    S = h2.shape[0]

    def body(h2_ref, p_ref, t_ref, wg_ref, wp_ref, g5_ref, dh2_ref, dpe_ref, dgl_ref, dg5_ref, loss_ref):
        i = pl.program_id(0)
        h2 = h2_ref[...]
        gate = jax.nn.sigmoid(_dot(h2.astype(MM), wg_ref[...]))
        pe = _dot(p_ref[...].astype(MM), wp_ref[...])
        n5, r5 = _rms(pe * gate)
        g5 = g5_ref[...]
        diff = h2 + n5 * g5 - t_ref[...]
        per_token = jnp.mean(jnp.square(diff), axis=-1, keepdims=True)
        _accumulate(loss_ref, 0.5 * jnp.sum(per_token, axis=0, keepdims=True), i)
        dh3 = diff * (1.0 / D_MODEL)
        de, dg5 = _rms_bwd(dh3, n5, r5, g5)
        _accumulate(dg5_ref, dg5, i)
        dpe_ref[...] = (de * gate).astype(MM)
        dgl = (de * pe * gate * (1.0 - gate)).astype(MM)
        dgl_ref[...] = dgl
        dh2_ref[...] = dh3 + _dot_nt(dgl, wg_ref[...])

    return pl.pallas_call(
        body, name="ple_loss", grid=(S // tm,),
        in_specs=[_rows(tm, D_MODEL), _rows(tm, D_PLE), _rows(tm, D_MODEL), _resident(), _resident(),
                  _const((1, D_MODEL))],
        out_specs=[_rows(tm, D_MODEL), _rows(tm, D_MODEL), _rows(tm, D_MODEL), _const((1, D_MODEL)), _const((1, 1))],
        out_shape=[jax.ShapeDtypeStruct((S, D_MODEL), F32), jax.ShapeDtypeStruct((S, D_MODEL), MM),
                   jax.ShapeDtypeStruct((S, D_MODEL), MM), jax.ShapeDtypeStruct((1, D_MODEL), F32),
                   jax.ShapeDtypeStruct((1, 1), F32)],
        compiler_params=_params("arbitrary"),
    )(h2, p, target, wg, wple, g5)


def _mlp_bwd(dh2, y, h1, u, w1, w2, g4, g3, tm):
    S = dh2.shape[0]

    def body(dh2_ref, y_ref, h1_ref, u_ref, w1_ref, w2_ref, g4_ref, g3_ref,
             dh1_ref, dy_ref, du_ref, dg4_ref, dg3_ref):
        i = pl.program_id(0)
        dh2 = dh2_ref[...]
        n4, r4 = _rms(y_ref[...])
        dy, dg4 = _rms_bwd(dh2, n4, r4, g4_ref[...])
        _accumulate(dg4_ref, dg4, i)
        dyb = dy.astype(MM)
        dy_ref[...] = dyb
        dm = jnp.zeros((tm, D_MODEL), F32)
        for j in range(N_DEV):
            cols = slice(j * FF_CHUNK, (j + 1) * FF_CHUNK)
            dact = _dot_nt(dyb, w2_ref[cols, :])
            du = (dact * (2.0 * jnp.maximum(u_ref[:, cols].astype(F32), 0.0))).astype(MM)
            du_ref[:, cols] = du
            dm = dm + _dot_nt(du, w1_ref[j])
        n3, r3 = _rms(h1_ref[...])
        dx, dg3 = _rms_bwd(dm, n3, r3, g3_ref[...])
        _accumulate(dg3_ref, dg3, i)
        dh1_ref[...] = dh2 + dx

    return pl.pallas_call(
        body, name="mlp_bwd", grid=(S // tm,),
        in_specs=[_rows(tm, D_MODEL), _rows(tm, D_MODEL), _rows(tm, D_MODEL), _rows(tm, D_FF),
                  _resident(), _resident(), _const((1, D_MODEL)), _const((1, D_MODEL))],
        out_specs=[_rows(tm, D_MODEL), _rows(tm, D_MODEL), _rows(tm, D_FF), _const((1, D_MODEL)),
                   _const((1, D_MODEL))],
        out_shape=[jax.ShapeDtypeStruct((S, D_MODEL), F32), jax.ShapeDtypeStruct((S, D_MODEL), MM),
                   jax.ShapeDtypeStruct((S, D_FF), MM), jax.ShapeDtypeStruct((1, D_MODEL), F32),
                   jax.ShapeDtypeStruct((1, D_MODEL), F32)],
        compiler_params=_params("arbitrary"),
    )(dh2, y, h1, u, w1, w2, g4, g3)


def _attn_out_bwd(dh1, mix, fox_o, swa_o, wout_fox, wout_swa, g2, head_rows, tm):
    S = dh1.shape[0]

    def body(dh1_ref, mix_ref, fo_ref, so_ref, wf_ref, ws_ref, g2_ref, er_ref,
             dmix_ref, dcat_ref, drow_ref, dswa_ref, dg2_ref):
        i = pl.program_id(0)
        n2, r2 = _rms(mix_ref[...])
        dmix, dg2 = _rms_bwd(dh1_ref[...], n2, r2, g2_ref[...])
        _accumulate(dg2_ref, dg2, i)
        dmb = dmix.astype(MM)
        dmix_ref[...] = dmb
        dfo = _dot_nt(dmb, wf_ref[...]).astype(MM)
        dso = _dot_nt(dmb, ws_ref[...]).astype(MM)
        dcat_ref[:, :D_ATT] = dfo
        dcat_ref[:, D_ATT:] = dso
        hi = lax.Precision.HIGHEST
        prod_f = dfo.astype(F32) * fo_ref[...].astype(F32)
        prod_s = dso.astype(F32) * so_ref[...].astype(F32)
        drow_ref[...] = lax.dot_general(er_ref[...], prod_f, NT, precision=hi, preferred_element_type=F32)
        dswa_ref[...] = lax.dot_general(er_ref[...], prod_s, NT, precision=hi, preferred_element_type=F32)

    return pl.pallas_call(
        body, name="attn_out_bwd", grid=(S // tm,),
        in_specs=[_rows(tm, D_MODEL), _rows(tm, D_MODEL), _rows(tm, D_ATT), _rows(tm, D_ATT), _resident(),
                  _resident(), _const((1, D_MODEL)), _resident()],
        out_specs=[_rows(tm, D_MODEL), _rows(tm, D_MODEL), pl.BlockSpec((N_HEADS, tm), lambda i: (0, i)),
                   pl.BlockSpec((N_HEADS, tm), lambda i: (0, i)), _const((1, D_MODEL))],
        out_shape=[jax.ShapeDtypeStruct((S, D_MODEL), MM), jax.ShapeDtypeStruct((S, D_MODEL), MM),
                   jax.ShapeDtypeStruct((N_HEADS, S), F32), jax.ShapeDtypeStruct((N_HEADS, S), F32),
                   jax.ShapeDtypeStruct((1, D_MODEL), F32)],
        compiler_params=_params("arbitrary"),
    )(dh1, mix, fox_o, swa_o, wout_fox, wout_swa, g2, head_rows)


def _fox_bwd(fqkv, dcat, lse_row3, d_row3, c_col, tq, tk, pairs_per_loop=2, after=None):
    S = fqkv.shape[0]
    n_blk = S // tk
    n_qblk = S // tq
    n_band = tk // tq

    def body(q_ref, k_ref, v_ref, do_ref, lse_ref, dd_ref, ck_ref, dq_ref, dk_ref, dv_ref, dc_ref, dcq_ref):
        kb = pl.program_id(0)

        @pl.when(kb == 0)
        def _():
            dq_ref[...] = jnp.zeros_like(dq_ref)
            dcq_ref[...] = jnp.zeros_like(dcq_ref)

        key = lax.broadcasted_iota(jnp.int32, (tk, tq), 0)
        qry = lax.broadcasted_iota(jnp.int32, (tk, tq), 1)
        low = _head_select((tk, 128), 0)
        for first in range(0, N_HEADS // 2, pairs_per_loop):
            pairs = range(first, first + pairs_per_loop)
            heads = [(pr, hh) for pr in pairs for hh in range(2)]
            kh, vh, ck = {}, {}, {}
            for pr in pairs:
                k2 = k_ref[:, pr * 128:(pr + 1) * 128]
                v2 = v_ref[:, pr * 128:(pr + 1) * 128]
                zero = jnp.zeros_like(k2)
                kh[pr, 0], kh[pr, 1] = jnp.where(low, k2, zero), jnp.where(low, zero, k2)
                vh[pr, 0], vh[pr, 1] = jnp.where(low, v2, zero), jnp.where(low, zero, v2)
                for hh in range(2):
                    ck[pr, hh] = ck_ref[:, 2 * pr + hh:2 * pr + hh + 1]

            def block(qb, carry, band, pairs=pairs, kh=kh, vh=vh, ck=ck):
                rows = pl.ds(pl.multiple_of(qb * tq, tq), tq)
                k1 = tk if band is None else (band + 1) * tq
                out = []
                it = iter(carry)
                for pr in pairs:
                    lanes = slice(pr * 128, (pr + 1) * 128)
                    q2 = q_ref[rows, lanes]
                    do2 = do_ref[rows, lanes]
                    dq = None
                    for hh in range(2):
                        h = 2 * pr + hh
                        dk, dv, dc = next(it)
                        s_t = _dot_nt(kh[pr, hh][:k1], q2) - ck[pr, hh][:k1]
                        p_t = jnp.exp(s_t - lse_ref[h, pl.ds(qb, 1), :])
                        if band is not None:
                            p_t = jnp.where(qry[:k1] + band * tq >= key[:k1], p_t, 0.0)
                        ds_t = p_t * (_dot_nt(vh[pr, hh][:k1], do2) - dd_ref[h, pl.ds(qb, 1), :])
                        dsb = ds_t.astype(MM)
                        dv_new = dv[:k1] + _dot(p_t.astype(MM), do2)
                        dk_new = dk[:k1] + _dot(dsb, q2)
                        dc_new = dc[:k1] - jnp.sum(ds_t, axis=1, keepdims=True)
                        if k1 < tk:
                            dv_new = jnp.concatenate([dv_new, dv[k1:]], axis=0)
                            dk_new = jnp.concatenate([dk_new, dk[k1:]], axis=0)
                            dc_new = jnp.concatenate([dc_new, dc[k1:]], axis=0)
                        part = _dot_tn(dsb, kh[pr, hh][:k1])
                        dq = part if dq is None else dq + part
                        dcq_ref[h, pl.ds(qb, 1), :] += jnp.sum(ds_t, axis=0, keepdims=True)
                        out.append((dk_new, dv_new, dc_new))
                    dq_ref[rows, lanes] += dq
                return tuple(out)

            carry = tuple((jnp.zeros((tk, 128), F32), jnp.zeros((tk, 128), F32), jnp.zeros((tk, 1), F32))
                          for _ in heads)
            for band in range(n_band):
                carry = block(kb * n_band + band, carry, band=band)
            carry = lax.fori_loop((kb + 1) * n_band, n_qblk, functools.partial(block, band=None), carry)
            grads = dict(zip(heads, carry))
            for pr in pairs:
                lanes = slice(pr * 128, (pr + 1) * 128)
                dk_ref[:, lanes] = jnp.where(low, grads[pr, 0][0], grads[pr, 1][0]).astype(MM)
                dv_ref[:, lanes] = jnp.where(low, grads[pr, 0][1], grads[pr, 1][1]).astype(MM)
                for hh in range(2):
                    dc_ref[:, 2 * pr + hh:2 * pr + hh + 1] = grads[pr, hh][2]

        @pl.when(kb == n_blk - 1)
        def _():
            dq_ref[...] = dq_ref[...] * Q_SCALE

    body, in_specs, operands = _run_after(
        after, body,
        [pl.BlockSpec((S, D_ATT), lambda i: (0, 0)), pl.BlockSpec((tk, D_ATT), lambda i: (i, 1)),
         pl.BlockSpec((tk, D_ATT), lambda i: (i, 2)), pl.BlockSpec((S, D_ATT), lambda i: (0, 0)),
         _resident(), _resident(), _rows(tk, N_HEADS)],
        (fqkv, fqkv, fqkv, dcat, lse_row3, d_row3, c_col))
    return pl.pallas_call(
        body, name="fox_bwd", grid=(n_blk,), in_specs=in_specs,
        out_specs=[_const((S, D_ATT)), _rows(tk, D_ATT), _rows(tk, D_ATT), _rows(tk, N_HEADS),
                   _const((N_HEADS, n_qblk, tq))],
        out_shape=[jax.ShapeDtypeStruct((S, D_ATT), F32), jax.ShapeDtypeStruct((S, D_ATT), MM),
                   jax.ShapeDtypeStruct((S, D_ATT), MM), jax.ShapeDtypeStruct((S, N_HEADS), F32),
                   jax.ShapeDtypeStruct((N_HEADS, n_qblk, tq), F32)],
        compiler_params=_params("arbitrary"),
    )(*operands)


def _swa_bwd(sqkv, dcat, biasm, sinks_slot, bucket, lse, d_col, after=None):
    S = sqkv.shape[0]
    n_blk = S // WINDOW

    def body(q_ref, kp_ref, kc_ref, vp_ref, vc_ref, do_ref, bias_ref, sink_ref, bk_ref, lse_ref, dd_ref,
             dq_ref, dk_ref, dv_ref, drb_ref, dsink_ref, ds_acc):
        n = pl.program_id(0)

        @pl.when(n == 0)
        def _():
            dk_ref[...] = jnp.zeros_like(dk_ref)
            dv_ref[...] = jnp.zeros_like(dv_ref)
            ds_acc[...] = jnp.zeros_like(ds_acc)
            dsink_ref[...] = jnp.zeros_like(dsink_ref)

        no_prev = jnp.where(n > 0, 0.0, NEG)
        prev = pl.ds(pl.multiple_of(jnp.maximum(n - 1, 0) * WINDOW, WINDOW), WINDOW)
        cur = pl.ds(pl.multiple_of(n * WINDOW, WINDOW), WINDOW)
        lane8 = lax.broadcasted_iota(jnp.int32, (1, N_HEADS), 1)
        dkp = jnp.zeros((WINDOW, D_KV), F32)
        dkc = jnp.zeros((WINDOW, D_KV), F32)
        dvp = jnp.zeros((WINDOW, D_KV), F32)
        dvc = jnp.zeros((WINDOW, D_KV), F32)
        dsink = jnp.zeros((1, N_HEADS), F32)
        low = _head_select((WINDOW, 128), 0)
        zero = jnp.zeros((WINDOW, 128), MM)
        dqs = []
        for g in range(2):
            sel = low if g == 0 else jnp.logical_not(low)
            qg = _stack4(lambda j: jnp.where(sel, q_ref[:, j * 128:(j + 1) * 128], zero))
            dog = _stack4(lambda j: jnp.where(sel, do_ref[:, j * 128:(j + 1) * 128], zero))
            lse_g = _stack4(lambda j: lse_ref[:, 2 * j + g:2 * j + g + 1])
            dd = _stack4(lambda j: dd_ref[:, 2 * j + g:2 * j + g + 1])
            sink = _stack4(lambda j: jnp.full((WINDOW, 1), sink_ref[2 * j + g], F32))
            pp = jnp.exp(_dot_nt(qg, kp_ref[...]) + _stack4(lambda j: bias_ref[2 * j + g, :, :WINDOW]) + no_prev - lse_g)
            pc = jnp.exp(_dot_nt(qg, kc_ref[...]) + _stack4(lambda j: bias_ref[2 * j + g, :, WINDOW:]) - lse_g)
            sink_term = jnp.exp(sink - lse_g) * dd
            dsp = pp * (_dot_nt(dog, vp_ref[...]) - dd)
            dsc = pc * (_dot_nt(dog, vc_ref[...]) - dd)
            for j in range(4):
                rows = slice(j * WINDOW, (j + 1) * WINDOW)
                dsink = dsink + jnp.where(lane8 == 2 * j + g, -jnp.sum(sink_term[rows]), 0.0)
                ds_acc[2 * j + g, :, :WINDOW] += dsp[rows]
                ds_acc[2 * j + g, :, WINDOW:] += dsc[rows]
            dspb, dscb = dsp.astype(MM), dsc.astype(MM)
            dqs.append(_dot(dspb, kp_ref[...]) + _dot(dscb, kc_ref[...]))
            dkp = dkp + _dot_tn(dspb, qg)
            dkc = dkc + _dot_tn(dscb, qg)
            dvp = dvp + _dot_tn(pp.astype(MM), dog)
            dvc = dvc + _dot_tn(pc.astype(MM), dog)
        for j in range(4):
            rows = slice(j * WINDOW, (j + 1) * WINDOW)
            dq_ref[:, j * 128:(j + 1) * 128] = (jnp.where(low, dqs[0][rows], dqs[1][rows]) * Q_SCALE).astype(MM)
        dk_ref[prev, :] += dkp
        dk_ref[cur, :] += dkc
        dv_ref[prev, :] += dvp
        dv_ref[cur, :] += dvc
        dsink_ref[...] += dsink

        @pl.when(n == n_blk - 1)
        def _():
            bk = bk_ref[...]
            rb = lax.broadcasted_iota(jnp.int32, (N_BUCKETS, N_HEADS), 0)
            cb = lax.broadcasted_iota(jnp.int32, (N_BUCKETS, N_HEADS), 1)
            out = jnp.zeros((N_BUCKETS, N_HEADS), F32)
            for s in range(N_HEADS):
                acc = ds_acc[s]
                for b in range(N_BUCKETS):
                    out = out + jnp.where((rb == b) & (cb == s), jnp.sum(jnp.where(bk == b, acc, 0.0)), 0.0)
            drb_ref[...] = out

    do_spec = pl.BlockSpec((WINDOW, D_ATT), lambda n: (n, 1))
    body, in_specs, operands = _run_after(
        after, body, _swa_specs(S) + [do_spec, _resident(), pl.BlockSpec(memory_space=pltpu.SMEM), _resident(),
                                      _rows(WINDOW, N_HEADS), _rows(WINDOW, N_HEADS)],
        (sqkv, sqkv, sqkv, sqkv, sqkv, dcat, biasm, sinks_slot, bucket, lse, d_col))
    return pl.pallas_call(
        body, name="swa_bwd", grid=(n_blk,), in_specs=in_specs,
        out_specs=[_rows(WINDOW, D_ATT), _const((S, D_KV)), _const((S, D_KV)), _const((N_BUCKETS, N_HEADS)),
                   _const((1, N_HEADS))],
        out_shape=[jax.ShapeDtypeStruct((S, D_ATT), MM), jax.ShapeDtypeStruct((S, D_KV), F32),
                   jax.ShapeDtypeStruct((S, D_KV), F32), jax.ShapeDtypeStruct((N_BUCKETS, N_HEADS), F32),
                   jax.ShapeDtypeStruct((1, N_HEADS), F32)],
        scratch_shapes=[pltpu.VMEM((N_HEADS, WINDOW, 2 * WINDOW), F32)],
        compiler_params=_params("arbitrary"),
    )(*operands)


def _weight_grad_pieces(pieces, b, name, tk):
    S, N = b.shape
    wide = [p for p in pieces if p.shape[1] % tk == 0]
    narrow = pieces[len(wide):]
    assert sum(p.shape[1] for p in narrow) == tk
    counts = [p.shape[1] // tk for p in wide]
    firsts = [sum(counts[:j]) for j in range(len(wide))]
    last = sum(counts)

    def body(*refs):
        b_ref, out_ref = refs[len(pieces)], refs[len(pieces) + 1]
        i = pl.program_id(0)

        def whole_steps(ref, first, count):
            @pl.when((i >= first) & (i < first + count))
            def _():
                out_ref[...] = _dot_tn(ref[...].astype(MM), b_ref[...]).astype(MM)

        for ref, first, count in zip(refs[:len(wide)], firsts, counts):
            whole_steps(ref, first, count)

        @pl.when(i == last)
        def _():
            row = 0
            for ref in refs[len(wide):len(pieces)]:
                k = ref.shape[1]
                out_ref[row:row + k] = _dot_tn(ref[...].astype(MM), b_ref[...]).astype(MM)
                row += k

    def steps_of(first, count):
        return pl.BlockSpec((S, tk), lambda i: (0, jnp.clip(i - first, 0, count - 1)))

    return pl.pallas_call(
        body, name=name, grid=(last + 1,),
        in_specs=[steps_of(first, count) for first, count in zip(firsts, counts)]
        + [pl.BlockSpec((S, p.shape[1]), lambda i: (0, 0)) for p in narrow] + [_resident()],
        out_specs=pl.BlockSpec((tk, N), lambda i: (i, 0)), out_shape=jax.ShapeDtypeStruct(((last + 1) * tk, N), MM),
        compiler_params=_params("parallel"),
    )(*pieces, b)


def _pre_attn_bwd(x, dh1, dz, dff_t, win_t, wt_rest, g1, tm, after=None):
    S = x.shape[0]

    def body(x_ref, dh1_ref, dq_ref, dk_ref, dv_ref, dsq_ref, dsk_ref, dsv_ref, dff_ref, wf_ref, wr_ref, g1_ref,
             dx_ref, dg1_ref):
        i = pl.program_id(0)
        dz_fox = jnp.concatenate([dq_ref[...].astype(MM), dk_ref[...], dv_ref[...]], axis=1)
        dz_swa = jnp.concatenate([dsq_ref[...], dsk_ref[...].astype(MM), dsv_ref[...].astype(MM)], axis=1)
        da = (_dot(dz_fox, wf_ref[...]) + _dot(dz_swa, wr_ref[WT_SQ:WT_REST])
              + _dot_tn(dff_ref[...].astype(MM), wr_ref[0:WT_SQ]))
        n1, r1 = _rms(x_ref[...])
        dx, dg1 = _rms_bwd(da, n1, r1, g1_ref[...])
        _accumulate(dg1_ref, dg1, i)
        dx_ref[...] = dh1_ref[...] + dx

    body, in_specs, operands = _run_after(
        after, body,
        [_rows(tm, D_MODEL), _rows(tm, D_MODEL), *[_rows(tm, d.shape[1]) for d in dz],
         pl.BlockSpec((16, tm), lambda i: (0, i)), _const((WT_FOX, D_MODEL)), _resident(), _const((1, D_MODEL))],
        (x, dh1, *dz, dff_t, win_t, wt_rest, g1))
    return pl.pallas_call(
        body, name="pre_attn_bwd", grid=(S // tm,), in_specs=in_specs,
        out_specs=[_rows(tm, D_MODEL), _const((1, D_MODEL))],
        out_shape=[jax.ShapeDtypeStruct((S, D_MODEL), F32), jax.ShapeDtypeStruct((1, D_MODEL), F32)],
        compiler_params=_params("arbitrary"),
    )(*operands)


def _weight_grad(a, b, name, tk, n_chunks=1, relu2=False):
    S, K = a.shape
    N = b.shape[1]
    cn = N // n_chunks

    def body(a_ref, b_ref, out_ref):
        av = a_ref[...]
        if relu2:
            av = jnp.square(jnp.maximum(av.astype(F32), 0.0))
        av = av.astype(MM)
        for j in range(n_chunks):
            val = _dot_tn(av, b_ref[:, j * cn:(j + 1) * cn].astype(MM)).astype(MM)
            if n_chunks > 1:
                out_ref[j] = val
            else:
                out_ref[...] = val

    if n_chunks > 1:
        out_spec = pl.BlockSpec((n_chunks, tk, cn), lambda i: (0, i, 0))
        out_shape = jax.ShapeDtypeStruct((n_chunks, K, cn), MM)
    else:
        out_spec = pl.BlockSpec((tk, N), lambda i: (i, 0))
        out_shape = jax.ShapeDtypeStruct((K, N), MM)
    return pl.pallas_call(
        body, name=name, grid=(K // tk,),
        in_specs=[pl.BlockSpec((S, tk), lambda i: (0, i)), _resident()],
        out_specs=out_spec, out_shape=out_shape, compiler_params=_params("parallel"),
    )(a, b)


def _weight_grad_two(a1, a2, b, name, tk):
    S, K1 = a1.shape
    K2 = a2.shape[1]
    N = b.shape[1]
    n1 = K1 // tk

    def body(a1_ref, a2_ref, b_ref, out_ref):
        av = jnp.where(pl.program_id(0) < n1, a1_ref[...], a2_ref[...])
        out_ref[...] = _dot_tn(av, b_ref[...]).astype(MM)

    return pl.pallas_call(
        body, name=name, grid=((K1 + K2) // tk,),
        in_specs=[pl.BlockSpec((S, tk), lambda i: (0, jnp.minimum(i, n1 - 1))),
                  pl.BlockSpec((S, tk), lambda i: (0, jnp.maximum(i - n1, 0))), _resident()],
        out_specs=pl.BlockSpec((tk, N), lambda i: (i, 0)), out_shape=jax.ShapeDtypeStruct((K1 + K2, N), MM),
        compiler_params=_params("parallel"),
    )(a1, a2, b)


def _place():
    return lax.axis_index("x"), lax.axis_index("y"), lax.axis_index("c")


def _all_gather_sequencer(stacks, name, collective_id):
    refs = [jax.new_ref(s, memory_space=pltpu.MemorySpace.HBM) for s in stacks]
    n = len(refs)

    @pl.kernel(mesh=plsc.ScalarSubcoreMesh(axis_name="sequencer", num_cores=1), name=name,
               scratch_types=(pltpu.SemaphoreType.DMA((7 * n,)), pltpu.SemaphoreType.DMA((7 * n,))),
               compiler_params=pltpu.CompilerParams(collective_id=collective_id))
    def launch(send_sems, recv_sems):
        x, y, c = _place()
        sibling = (x, y, 1 - c)
        chips = [(1 - x, y), (x, 1 - y), (1 - x, 1 - y)]
        peers = [sibling] + [(px, py, c) for px, py in chips]
        barrier = pltpu.get_barrier_semaphore()
        for peer in peers:
            pl.semaphore_signal(barrier, inc=1, device_id=peer, device_id_type=MESH)
        pl.semaphore_wait(barrier, len(peers))

        def copy(a, k, block, to):
            px, py, pc = block
            slot = refs[a].at[4 * px + 2 * py + pc]
            return _remote(slot, slot, send_sems, recv_sems, 7 * a + k, to)

        first = [copy(a, k, (x, y, c), peer) for a in range(n) for k, peer in enumerate(peers)]
        for cp in first:
            cp.start()
        passed = []
        for j, (px, py) in enumerate(chips):
            for a in range(n):
                copy(a, 1 + j, (px, py, c), sibling).wait_recv()
                passed.append(copy(a, 4 + j, (px, py, c), sibling))
                passed[-1].start()
        for a in range(n):
            copy(a, 0, (x, y, 1 - c), sibling).wait_recv()
            for j, (px, py) in enumerate(chips):
                copy(a, 4 + j, (px, py, 1 - c), sibling).wait_recv()
        for cp in first + passed:
            cp.wait_send()

    launch()
    return [ref[...] for ref in refs]


def _chip_sums(grads, others, name):
    n = len(grads)

    def body(c_ref, *refs):
        for g_ref, o_ref, out_ref in zip(refs[:n], refs[n:2 * n], refs[2 * n:]):
            out_ref[...] = (g_ref[...].astype(F32) + o_ref[...].astype(F32)).astype(out_ref.dtype)

    own = [pl.BlockSpec((None, None) + g.shape[2:], lambda k, c_ref: (k, c_ref[0], 0, 0)) for g in grads]
    chip = [pl.BlockSpec((None,) + g.shape[2:], lambda k, c_ref: (k, 0, 0)) for g in grads]
    return pl.pallas_call(
        body, name=name,
        grid_spec=pltpu.PrefetchScalarGridSpec(num_scalar_prefetch=1, grid=(4,), in_specs=own + chip, out_specs=chip),
        out_shape=[jax.ShapeDtypeStruct((4,) + g.shape[2:], MM) for g in grads],
        compiler_params=_params("parallel"),
    )(lax.axis_index("c").astype(jnp.int32).reshape(1), *grads, *others)


HBM_SPEC = pl.BlockSpec(memory_space=pltpu.HBM)
SEM_SPEC = pl.BlockSpec(memory_space=pltpu.SEMAPHORE)
DATAFLOW = pltpu.SideEffectType.DATAFLOW_SIDE_EFFECTING


def _exchange_start(name, arrays, n_copies, plan):
    n = len(arrays)

    def body(*refs):
        send_sems, recv_sems, token = refs[n], refs[n + 1], refs[2 * n + 2]
        for cp in plan(refs[:n], send_sems, recv_sems):
            cp.start()
        token[...] = jnp.zeros_like(token)

    out = pl.pallas_call(
        body, name=name,
        out_shape=(pltpu.SemaphoreType.DMA((n_copies,)), pltpu.SemaphoreType.DMA((n_copies,)),
                   *[pltpu.HBM(a.shape, a.dtype) for a in arrays], jax.ShapeDtypeStruct((1, D_MODEL), F32)),
        in_specs=[HBM_SPEC] * n,
        out_specs=(SEM_SPEC, SEM_SPEC, *[HBM_SPEC] * n, pl.BlockSpec(memory_space=pltpu.VMEM)),
        input_output_aliases={i: 2 + i for i in range(n)},
        compiler_params=pltpu.CompilerParams(has_side_effects=DATAFLOW),
    )(*[pltpu.with_memory_space_constraint(a, pltpu.HBM) for a in arrays])
    return (out[0], out[1]), list(out[2:2 + n]), out[2 + n]


def _exchange_wait(name, arrays, sems, after, plan):
    n = len(arrays)
    after = list(after) if isinstance(after, (list, tuple)) else [after]

    def body(*refs):
        send_sems, recv_sems = refs[n], refs[n + 1]
        for cp in plan(refs[:n], send_sems, recv_sems):
            cp.wait_send()
            cp.wait_recv()

    out = pl.pallas_call(
        body, name=name, out_shape=[pltpu.HBM(a.shape, a.dtype) for a in arrays],
        in_specs=[HBM_SPEC] * n + [SEM_SPEC, SEM_SPEC] + [pl.BlockSpec(memory_space=pl.ANY)] * len(after),
        out_specs=[HBM_SPEC] * n, input_output_aliases={i: i for i in range(n)},
        compiler_params=pltpu.CompilerParams(has_side_effects=DATAFLOW),
    )(*arrays, sems[0], sems[1], *after)
    return list(out)


def _remote(src, dst, send_sems, recv_sems, k, to):
    return pltpu.make_async_remote_copy(src_ref=src, dst_ref=dst, send_sem=send_sems.at[k], recv_sem=recv_sems.at[k],
                                        device_id=to, device_id_type=MESH)


def _plan_gather_direct(refs, send_sems, recv_sems):
    x, y, c = _place()
    me = 4 * x + 2 * y + c
    peers = [(x, y, 1 - c), (1 - x, y, c), (x, 1 - y, c), (1 - x, 1 - y, c)]
    return [_remote(ref.at[me], ref.at[me], send_sems, recv_sems, 4 * a + k, peer)
            for a, ref in enumerate(refs) for k, peer in enumerate(peers)]


def _plan_gather_pass_on(refs, send_sems, recv_sems):
    x, y, c = _place()
    chips = [(1 - x, y), (x, 1 - y), (1 - x, 1 - y)]
    return [_remote(ref.at[4 * px + 2 * py + c], ref.at[4 * px + 2 * py + c], send_sems, recv_sems, 3 * a + k,
                    (x, y, 1 - c))
            for a, ref in enumerate(refs) for k, (px, py) in enumerate(chips)]


def _plan_in_chip(refs, send_sems, recv_sems):
    n = len(refs) // 2
    x, y, c = _place()
    return [_remote(refs[a].at[:, 1 - c], refs[n + a], send_sems, recv_sems, a, (x, y, 1 - c)) for a in range(n)]


def _plan_between_chips(refs, send_sems, recv_sems):
    n = len(refs) // 2
    x, y, c = _place()
    chips = [(1 - x, y), (x, 1 - y), (1 - x, 1 - y)]
    return [_remote(refs[a].at[2 * px + py], refs[n + a].at[2 * x + y], send_sems, recv_sems, 3 * a + k, (px, py, c))
            for a in range(n) for k, (px, py) in enumerate(chips)]


def _plan_late_between(refs, send_sems, recv_sems):
    sums, land, small = refs
    x, y, c = _place()
    me = 4 * x + 2 * y + c
    copies = _plan_between_chips([sums, land], send_sems, recv_sems)
    peers = [(x ^ dx, y ^ dy, c ^ dc) for dx in range(2) for dy in range(2) for dc in range(2) if dx + dy + dc]
    return copies + [_remote(small.at[me], small.at[me], send_sems, recv_sems, 3 + k, peer)
                     for k, peer in enumerate(peers)]


def _adamw_math(w, g, m, v):
    m = ADAM_B1 * m + (1.0 - ADAM_B1) * g
    v = ADAM_B2 * v + (1.0 - ADAM_B2) * jnp.square(g)
    m_hat = m / (1.0 - ADAM_B1 ** ADAM_STEP)
    v_hat = v / (1.0 - ADAM_B2 ** ADAM_STEP)
    delta = -ADAM_LR * (m_hat / (jnp.sqrt(v_hat) + ADAM_EPS) + ADAM_WD * w)
    return delta, m, v


def _adamw(parts, w, m, v, name):
    n_parts, r, cdim = parts.shape
    tr = 256 if r % 256 == 0 else r

    def body(p_ref, w_ref, m_ref, v_ref, g_out, d_out, m_out, v_out):
        g = p_ref[0].astype(F32)
        for k in range(1, n_parts):
            g = g + p_ref[k].astype(F32)
        delta, m_new, v_new = _adamw_math(w_ref[...], g, m_ref[...], v_ref[...])
        g_out[...] = g
        d_out[...] = delta
        m_out[...] = m_new
        v_out[...] = v_new

    blk = pl.BlockSpec((tr, cdim), lambda i: (i, 0))
    return pl.pallas_call(
        body, name=name, grid=(r // tr,),
        in_specs=[pl.BlockSpec((n_parts, tr, cdim), lambda i: (0, i, 0)), blk, blk, blk],
        out_specs=[blk] * 4, out_shape=[jax.ShapeDtypeStruct((r, cdim), F32)] * 4,
        compiler_params=_params("parallel"),
    )(parts, w, m, v)


def _adamw_chips(parts, sums, w, m, v, name):
    _, r, cdim = parts.shape
    tr = 256 if r % 256 == 0 else r

    def body(chip_ref, p_ref, own_ref, w_ref, m_ref, v_ref, g_out, d_out, m_out, v_out):
        g = None
        for k in range(4):
            term = jnp.where(chip_ref[0] == k, own_ref[...], p_ref[k]).astype(F32)
            g = term if g is None else g + term
        get = (lambda ref: ref[:, 0, :]) if apart else (lambda ref: ref[...])
        delta, m_new, v_new = _adamw_math(get(w_ref), g, get(m_ref), get(v_ref))
        for ref, val in ((g_out, g), (d_out, delta), (m_out, m_new), (v_out, v_new)):
            if apart:
                ref[:, 0, :] = val
            else:
                ref[...] = val

    apart = w.ndim == 3
    if apart:
        blk = pl.BlockSpec((tr, 1, cdim), lambda i, chip: (i, 0, 0))
        shape = (r, 1, cdim)
    else:
        blk = pl.BlockSpec((tr, cdim), lambda i, chip: (i, 0))
        shape = (r, cdim)
    my_chip = (2 * lax.axis_index("x") + lax.axis_index("y")).astype(jnp.int32).reshape(1)
    return pl.pallas_call(
        body, name=name,
        grid_spec=pltpu.PrefetchScalarGridSpec(
            num_scalar_prefetch=1, grid=(r // tr,),
            in_specs=[pl.BlockSpec((4, tr, cdim), lambda i, chip: (0, i, 0)),
                      pl.BlockSpec((None, tr, cdim), lambda i, chip: (chip[0], i, 0)), blk, blk, blk],
            out_specs=[blk] * 4),
        out_shape=[jax.ShapeDtypeStruct(shape, F32)] * 4,
        compiler_params=_params("parallel"),
    )(my_chip, parts, sums, w, m, v)


class _NoExchange:
    def __init__(self, weights):
        self.weights = weights

    def before_pre_attn(self):
        return None

    def after_fox_fwd(self, fox_o):
        return None

    def after_attention(self, swa_o):
        return self.weights

    def after_early_grads(self, grads):
        return None

    def after_swa_bwd(self, dsq):
        return None

    def after_w_in_grad(self, d_win):
        return None


def _slot_order(t, axis):
    shp = t.shape
    t = t.reshape(shp[:axis] + (2, 4, shp[axis] // N_HEADS) + shp[axis + 1:])
    return jnp.swapaxes(t, axis, axis + 1).reshape(shp)


def _head_order(t, axis):
    shp = t.shape
    t = t.reshape(shp[:axis] + (4, 2, shp[axis] // N_HEADS) + shp[axis + 1:])
    return jnp.swapaxes(t, axis, axis + 1).reshape(shp)


def _forward_backward(x, p, target, win_t, hooks, b_forget, rel_bias, sinks, g1, g2, g3, g4, g5):
    S = x.shape[0]
    tm = 512
    tm_mlp = 512
    t = 256
    q0 = 3 * D_ATT + N_HEADS
    win_t = win_t.reshape(D_IN, D_MODEL)
    wt_rest = jnp.concatenate(
        [win_t[WT_FOX:q0], jnp.zeros((8, D_MODEL), MM), _slot_order(win_t[q0:q0 + D_ATT], 0), win_t[q0 + D_ATT:]],
        axis=0)
    bcol = jnp.pad(b_forget.reshape(N_HEADS, 1), ((0, 8), (0, 0)))
    rel_bias_slot = rel_bias[:, np.array(SLOT_HEAD)]
    sinks_slot = sinks.reshape(N_HEADS)[np.array(SLOT_HEAD)]
    bucket = jnp.asarray(_swa_bucket_map())

    a, fqkv, sqkv, fft = _pre_attn(x, g1, win_t, wt_rest, tm, after=hooks.before_pre_attn())
    c_row = _forget_cumsum(fft, bcol)
    c_col = c_row[:N_HEADS].T
    c_row3 = c_row[:N_HEADS].reshape(N_HEADS, S // t, t)
    fox_o, fox_lse = _fox_fwd(fqkv, c_row3, tq=512, tk=t)
    biasm = _swa_bias(rel_bias_slot, bucket)
    swa_o, swa_lse = _swa_fwd(sqkv, biasm, sinks_slot, after=hooks.after_fox_fwd(fox_o))
    wout, w1, w2, wple, wg = hooks.after_attention(swa_o)
    wout_fox = wout[:D_ATT]
    wout_swa = _slot_order(wout[D_ATT:], 0)
    mix, h1, m = _post_attn(x, fox_o, swa_o, wout_fox, wout_swa, g2, g3, tm)
    u, y, h2 = _mlp_fwd(m, h1, w1, w2, g4, tm_mlp)
    dh2, dpe, dgl, dg5, loss = _ple_loss(h2, p, target, wg, wple, g5, tm)

    d_wple = _weight_grad(p, dpe, "grad_w_ple", tk=D_PLE, n_chunks=N_DEV)
    d_wg = _weight_grad(h2, dgl, "grad_w_ple_gate", tk=256)
    dh1, dy, du, dg4, dg3 = _mlp_bwd(dh2, y, h1, u, w1, w2, g4, g3, tm)
    d_w2 = _weight_grad(u, dy, "grad_w_ff2", tk=256, relu2=True)
    d_w1 = _weight_grad(m, du, "grad_w_ff1", tk=256, n_chunks=N_DEV)
    head = np.arange(D_ATT) // HEAD_DIM
    head_rows = jnp.asarray((head[None, :] == np.arange(N_HEADS)[:, None]).astype(np.float32))
    dmix, dcat, d_row, d_swa, dg2 = _attn_out_bwd(dh1, mix, fox_o, swa_o, wout_fox, wout_swa, g2, head_rows, tm)
    d_col = d_swa.T
    d_wout = _weight_grad_two(fox_o, swa_o, dmix, "grad_w_out", tk=256)
    d_wout = jnp.concatenate([d_wout[:D_ATT], _head_order(d_wout[D_ATT:], 0)], axis=0)
    d_wout = d_wout.reshape(N_DEV, D_MODEL // N_DEV, D_MODEL)
    early = dict(w_ff1=d_w1, w_ff2=d_w2.reshape(N_DEV, FF_CHUNK, D_MODEL), w_ple=d_wple,
                 w_ple_gate=d_wg.reshape(N_DEV, D_MODEL // N_DEV, D_MODEL), w_out=d_wout)

    dsq, dsk, dsv, d_rb_slot, d_sink_slot = _swa_bwd(sqkv, dcat, biasm, sinks_slot, bucket, swa_lse, d_col,
                                                     after=hooks.after_early_grads(early))
    lse_row3 = fox_lse.T.reshape(N_HEADS, S // t, t)
    d_row3 = d_row.reshape(N_HEADS, S // t, t)
    dq_fox, dk_fox, dv_fox, dc_col, dcq = _fox_bwd(fqkv, dcat, lse_row3, d_row3, c_col, tq=t, tk=512,
                                                  after=hooks.after_swa_bwd(dsq))
    dc_row = jnp.pad(dc_col.T + dcq.reshape(N_HEADS, S), ((0, 8), (0, 0)))
    dff_t, db, d_wff_t = _forget_bwd(dc_row, fft, bcol, a)
    dz = [dq_fox, dk_fox, dv_fox, dsq, dsk, dsv]
    d_wmain = _weight_grad_pieces(dz, a, "grad_w_in", tk=256)

    sq0 = 3 * D_ATT
    d_win = jnp.concatenate(
        [d_wmain[:sq0], d_wff_t[:N_HEADS].astype(MM), _head_order(d_wmain[sq0:sq0 + D_ATT], 0),
         d_wmain[sq0 + D_ATT:]], axis=0)
    d_win = d_win.reshape(N_DEV, D_IN // N_DEV, D_MODEL)
    grad_x, dg1 = _pre_attn_bwd(x, dh1, dz, dff_t, win_t, wt_rest, g1, tm, after=hooks.after_w_in_grad(d_win))
    big = dict(early, w_in=d_win)
    small = dict(b_forget=db[:N_HEADS].reshape(1, N_HEADS), rel_bias=d_rb_slot[:, np.array(HEAD_SLOT)],
                 swa_sinks=d_sink_slot[:, np.array(HEAD_SLOT)], g_attn_pre=dg1, g_attn_post=dg2, g_ff_pre=dg3,
                 g_ff_post=dg4, g_ple_post=dg5)
    return loss, grad_x, big, small


BIG = ("w_in", "w_out", "w_ff1", "w_ff2", "w_ple", "w_ple_gate")
SMALL_ROWS = ("g_attn_pre", "g_attn_post", "g_ff_pre", "g_ff_post", "g_ple_post")
WEIGHTS =("w_in", "b_forget", "w_out", "rel_bias", "swa_sinks", "g_attn_pre", "g_attn_post", "w_ff1", "w_ff2",
           "g_ff_pre", "g_ff_post", "w_ple", "w_ple_gate", "g_ple_post")


EARLY = ("w_ff1", "w_ff2", "w_ple", "w_ple_gate", "w_out")


class _Overlap:
    def __init__(self, later):
        self.later = later

    def before_pre_attn(self):
        self.gather_sems, self.later, token = _exchange_start("gather_rest_start", self.later, 4 * 5, _plan_gather_direct)
        return token

    def after_fox_fwd(self, fox_o):
        later = _exchange_wait("gather_rest_wait", self.later, self.gather_sems, fox_o, _plan_gather_direct)
        self.pass_sems, self.later, token = _exchange_start("gather_pass_on_start", later, 3 * 5, _plan_gather_pass_on)
        return token

    def after_attention(self, swa_o):
        wout_g, w1_g, w2_g, wple_g, wg_g = _exchange_wait("gather_pass_on_wait", self.later, self.pass_sems, swa_o,
                                                         _plan_gather_pass_on)
        return (wout_g.reshape(D_MODEL, D_MODEL), w1_g, w2_g.reshape(D_FF, D_MODEL),
                jnp.moveaxis(wple_g, 0, 1).reshape(D_PLE, D_MODEL), wg_g.reshape(D_MODEL, D_MODEL))

    def after_early_grads(self, grads):
        views = [grads[k].reshape((4, 2) + grads[k].shape[1:]) for k in EARLY]
        lands = [lax.empty((4,) + grads[k].shape[1:], MM) for k in EARLY]
        self.in_chip_sems, self.in_chip, token = _exchange_start("grads_in_chip_start", views + lands, len(EARLY),
                                                                 _plan_in_chip)
        return token

    def after_swa_bwd(self, dsq):
        arrays = _exchange_wait("grads_in_chip_wait", self.in_chip, self.in_chip_sems, dsq, _plan_in_chip)
        n = len(EARLY)
        sums = list(_chip_sums(arrays[:n], arrays[n:], "chip_sums_early"))
        lands = [lax.empty(s.shape, s.dtype) for s in sums]
        self.between_sems, self.between, token = _exchange_start("grads_between_chips_start", sums + lands, 3 * n,
                                                                 _plan_between_chips)
        return token

    def after_w_in_grad(self, d_win):
        self.late_in_chip_sems, self.late_in_chip, token = _exchange_start(
            "late_in_chip_start", [d_win.reshape((4, 2) + d_win.shape[1:]), lax.empty((4,) + d_win.shape[1:], MM)],
            1, _plan_in_chip)
        return token

    def finish(self, after):
        arrays = _exchange_wait("grads_between_chips_wait", self.between, self.between_sems, after,
                                _plan_between_chips)
        n = len(EARLY)
        self.sums = arrays[:n]
        return arrays[n:]


def _pack_small(t):
    rows = [t[k].reshape(1, D_MODEL) for k in SMALL_ROWS]
    misc = jnp.concatenate([t["b_forget"].reshape(-1), t["swa_sinks"].reshape(-1), t["rel_bias"].reshape(-1)])
    rows.append(jnp.pad(misc, (0, D_MODEL - misc.shape[0])).reshape(1, D_MODEL))
    rows.append(jnp.pad(t["loss"].reshape(-1), (0, D_MODEL - 1)).reshape(1, D_MODEL))
    rows.append(jnp.zeros((1, D_MODEL), F32))
    return jnp.concatenate(rows, axis=0).astype(F32)


def _unpack_small(blk):
    out = {k: blk[i].reshape(1, D_MODEL) for i, k in enumerate(SMALL_ROWS)}
    misc = blk[len(SMALL_ROWS)]
    out["b_forget"] = misc[:N_HEADS].reshape(1, N_HEADS)
    out["swa_sinks"] = misc[N_HEADS:2 * N_HEADS].reshape(1, N_HEADS)
    out["rel_bias"] = misc[2 * N_HEADS:2 * N_HEADS + N_BUCKETS * N_HEADS].reshape(N_BUCKETS, N_HEADS)
    out["loss"] = blk[len(SMALL_ROWS) + 1, 0]
    return out


def kernel(x, p, w_in, b_forget, w_out, rel_bias, swa_sinks, g_attn_pre, g_attn_post, w_ff1, w_ff2, g_ff_pre, g_ff_post, w_ple, w_ple_gate, g_ple_post, loss_target, m_w_in, m_b_forget, m_w_out, m_rel_bias, m_swa_sinks, m_g_attn_pre, m_g_attn_post, m_w_ff1, m_w_ff2, m_g_ff_pre, m_g_ff_post, m_w_ple, m_w_ple_gate, m_g_ple_post, v_w_in, v_b_forget, v_w_out, v_rel_bias, v_swa_sinks, v_g_attn_pre, v_g_attn_post, v_w_ff1, v_w_ff2, v_g_ff_pre, v_g_ff_post, v_w_ple, v_w_ple_gate, v_g_ple_post):
    w = dict(w_in=w_in, b_forget=b_forget, w_out=w_out, rel_bias=rel_bias, swa_sinks=swa_sinks,
             g_attn_pre=g_attn_pre, g_attn_post=g_attn_post, w_ff1=w_ff1, w_ff2=w_ff2, g_ff_pre=g_ff_pre,
             g_ff_post=g_ff_post, w_ple=w_ple, w_ple_gate=w_ple_gate, g_ple_post=g_ple_post)
    mom = dict(w_in=m_w_in, b_forget=m_b_forget, w_out=m_w_out, rel_bias=m_rel_bias, swa_sinks=m_swa_sinks,
               g_attn_pre=m_g_attn_pre, g_attn_post=m_g_attn_post, w_ff1=m_w_ff1, w_ff2=m_w_ff2,
               g_ff_pre=m_g_ff_pre, g_ff_post=m_g_ff_post, w_ple=m_w_ple, w_ple_gate=m_w_ple_gate,
               g_ple_post=m_g_ple_post)
    var = dict(w_in=v_w_in, b_forget=v_b_forget, w_out=v_w_out, rel_bias=v_rel_bias, swa_sinks=v_swa_sinks,
               g_attn_pre=v_g_attn_pre, g_attn_post=v_g_attn_post, w_ff1=v_w_ff1, w_ff2=v_w_ff2,
               g_ff_pre=v_g_ff_pre, g_ff_post=v_g_ff_post, w_ple=v_w_ple, w_ple_gate=v_w_ple_gate,
               g_ple_post=v_g_ple_post)

    turn = lambda t, k: t.T if k == "w_in" else t
    me = 4 * lax.axis_index("x") + 2 * lax.axis_index("y") + lax.axis_index("c")

    def stack(block):
        return lax.dynamic_update_slice_in_dim(lax.empty((N_DEV,) + block.shape, block.dtype), block[None], me, 0)

    stacks = [stack(turn(w[k][0], k).astype(MM)) for k in BIG]
    (win_g,), later = _all_gather_sequencer(stacks[:1], "all_gather_sequencer", 1), stacks[1:]
    hooks = _Overlap(later)
    loss, grad_x, big, small = _forward_backward(
        x[0], p[0, 0], loss_target[0], win_g, hooks, b_forget, rel_bias, swa_sinks,
        g_attn_pre, g_attn_post, g_ff_pre, g_ff_post, g_ple_post)
    out_g, out_d, out_m, out_v = {}, {}, {}, {}

    def update(k, part, own):
        if k == "w_in":
            there, back = (lambda t: jnp.transpose(t, (2, 0, 1))), (lambda t: jnp.transpose(t, (1, 2, 0)))
        else:
            there, back = (lambda t: t[0]), (lambda t: t[None])
        g, d, m_new, v_new = _adamw_chips(part, own, there(w[k]), there(mom[k]), there(var[k]), "adamw_" + k)
        out_g[k], out_d[k], out_m[k], out_v[k] = back(g), back(d), back(m_new), back(v_new)
        return d

    view, other = _exchange_wait("late_in_chip_wait", hooks.late_in_chip, hooks.late_in_chip_sems, grad_x,
                                 _plan_in_chip)
    (chip_sum,) = _chip_sums([view], [other], "chip_sum_w_in")
    small["loss"] = loss
    between_sems, between, token = _exchange_start(
        "late_between_chips_start", [chip_sum, lax.empty(chip_sum.shape, MM), stack(_pack_small(small))], 3 + 7,
        _plan_late_between)
    early_parts = hooks.finish(token)
    done = [update(k, part, own) for k, part, own in zip(EARLY, early_parts, hooks.sums)]
    chip_sum, part, small_all = _exchange_wait("late_between_chips_wait", between, between_sems, done,
                                               _plan_late_between)
    update("w_in", part, chip_sum)
    rep = {k: w[k] for k in w if k not in BIG}
    rep["loss"] = jnp.zeros((), F32)
    rep_m = {k: mom[k] for k in mom if k not in BIG}
    rep_m["loss"] = jnp.zeros((), F32)
    rep_v = {k: var[k] for k in var if k not in BIG}
    rep_v["loss"] = jnp.ones((), F32)
    g_s, d_s, m_s, v_s = _adamw(small_all, _pack_small(rep), _pack_small(rep_m), _pack_small(rep_v), "adamw_small")
    g_s, d_s, m_s, v_s = _unpack_small(g_s), _unpack_small(d_s), _unpack_small(m_s), _unpack_small(v_s)
    for k in w:
        if k not in BIG:
            out_g[k], out_d[k], out_m[k], out_v[k] = g_s[k], d_s[k], m_s[k], v_s[k]
    return (g_s["loss"], grad_x[None], *[out_g[k] for k in WEIGHTS], *[out_d[k] for k in WEIGHTS],
            *[out_m[k] for k in WEIGHTS], *[out_v[k] for k in WEIGHTS])
```

```python
import functools

import numpy as np
import jax
import jax.numpy as jnp
from jax import lax
from jax.experimental import pallas as pl
from jax.experimental.pallas import tpu as pltpu
from jax.experimental.pallas import tpu_sc as plsc

F32 = jnp.float32
MM = jnp.bfloat16

D_MODEL = 1024
HEAD_DIM = 64
N_HEADS = 8
D_ATT = N_HEADS * HEAD_DIM
D_KV = 128
D_FF = 4096
D_PLE = 256
D_IN = 3 * D_ATT + N_HEADS + D_ATT + 2 * D_KV
N_DEV = 8
FF_CHUNK = D_FF // N_DEV
WINDOW = 128
N_BUCKETS = 32
MAX_DISTANCE = 128
RMS_EPS = 1e-6
Q_SCALE = HEAD_DIM ** -0.5
NEG = -1e30

ADAM_LR = 0.001
ADAM_B1 = 0.9
ADAM_B2 = 0.999
ADAM_EPS = 1e-08
ADAM_WD = 0.01
ADAM_STEP = 10

SLOT_HEAD = (0, 4, 1, 5, 2, 6, 3, 7)
HEAD_SLOT = (0, 2, 4, 6, 1, 3, 5, 7)

VMEM_LIMIT = 60 * 1024 * 1024
MESH = pl.DeviceIdType.MESH

NT = (((1,), (1,)), ((), ()))
TN = (((0,), (0,)), ((), ()))


def _params(*semantics):
    return pltpu.CompilerParams(dimension_semantics=semantics, vmem_limit_bytes=VMEM_LIMIT)


def _resident():
    return pl.BlockSpec(memory_space=pltpu.VMEM)


def _rows(tm, width):
    return pl.BlockSpec((tm, width), lambda i: (i, 0))


def _const(shape):
    return pl.BlockSpec(shape, lambda i: (0,) * len(shape))


def _dot(a, b):
    return jnp.dot(a, b, preferred_element_type=F32)


def _dot_nt(a, b):
    return lax.dot_general(a, b, NT, preferred_element_type=F32)


def _dot_tn(a, b):
    return lax.dot_general(a, b, TN, preferred_element_type=F32)


def _rms(xf):
    r = lax.rsqrt(jnp.mean(xf * xf, axis=-1, keepdims=True) + RMS_EPS)
    return xf * r, r


def _rms_bwd(dout, n, r, g):
    dg = jnp.sum(dout * n, axis=0, keepdims=True)
    dn = dout * g
    dx = r * (dn - n * jnp.mean(dn * n, axis=-1, keepdims=True))
    return dx, dg


def _run_after(after, body, in_specs, operands):
    if after is None:
        return body, list(in_specs), tuple(operands)
    n = len(operands)
    return ((lambda *refs: body(*refs[:n], *refs[n + 1:])), list(in_specs) + [pl.BlockSpec(memory_space=pl.ANY)],
            tuple(operands) + (after,))


def _accumulate(ref, value, step):
    @pl.when(step == 0)
    def _():
        ref[...] = value

    @pl.when(step != 0)
    def _():
        ref[...] += value


def _t5_bucket(n):
    max_exact = N_BUCKETS // 2
    large = max_exact + (np.log(np.maximum(n, 1) / max_exact) / np.log(MAX_DISTANCE / max_exact)
                         * (N_BUCKETS - max_exact)).astype(np.int32)
    large = np.minimum(large, N_BUCKETS - 1)
    return np.where(n < max_exact, n, large).astype(np.int32)


def _swa_bucket_map():
    i = np.arange(WINDOW)[:, None]
    j = np.arange(2 * WINDOW)[None, :]
    dist = i + WINDOW - j
    ok = (dist >= 0) & (dist < WINDOW)
    return np.where(ok, _t5_bucket(np.clip(dist, 0, None)), -1).astype(np.int32)


WT_FOX = 3 * D_ATT
WT_SQ = 16
WT_SKV = WT_SQ + D_ATT
WT_REST = WT_SKV + 2 * D_KV


def _pre_attn(x, g1, win_t, wt_rest, tm, after=None):
    S = x.shape[0]

    def body(x_ref, g_ref, wf_ref, wr_ref, a_ref, fqkv_ref, sqkv_ref, fft_ref):
        n, _ = _rms(x_ref[...])
        a = (n * g_ref[...]).astype(MM)
        a_ref[...] = a
        fqkv_ref[:, :D_ATT] = (_dot_nt(a, wf_ref[0:D_ATT]) * Q_SCALE).astype(MM)
        fqkv_ref[:, D_ATT:] = _dot_nt(a, wf_ref[D_ATT:WT_FOX]).astype(MM)
        sqkv_ref[:, :D_ATT] = (_dot_nt(a, wr_ref[WT_SQ:WT_SKV]) * Q_SCALE).astype(MM)
        sqkv_ref[:, D_ATT:] = _dot_nt(a, wr_ref[WT_SKV:WT_REST]).astype(MM)
        fft_ref[...] = _dot_nt(wr_ref[0:WT_SQ], a)

    body, in_specs, operands = _run_after(
        after, body, [_rows(tm, D_MODEL), _const((1, D_MODEL)), _const((WT_FOX, D_MODEL)), _resident()],
        (x, g1, win_t, wt_rest))
    return pl.pallas_call(
        body, name="pre_attn", grid=(S // tm,), in_specs=in_specs,
        out_specs=[_rows(tm, D_MODEL), _rows(tm, 3 * D_ATT), _rows(tm, D_ATT + 2 * D_KV),
                   pl.BlockSpec((16, tm), lambda i: (0, i))],
        out_shape=[jax.ShapeDtypeStruct((S, D_MODEL), MM), jax.ShapeDtypeStruct((S, 3 * D_ATT), MM),
                   jax.ShapeDtypeStruct((S, D_ATT + 2 * D_KV), MM), jax.ShapeDtypeStruct((16, S), F32)],
        compiler_params=_params("parallel"),
    )(*operands)


def _lane_scan(v, reverse):
    S = v.shape[1]
    lane = lax.broadcasted_iota(jnp.int32, v.shape, 1)
    k = 1
    while k < S:
        if reverse:
            v = v + jnp.where(lane < S - k, pltpu.roll(v, S - k, axis=1), 0.0)
        else:
            v = v + jnp.where(lane >= k, pltpu.roll(v, k, axis=1), 0.0)
        k *= 2
    return v


def _forget_cumsum(fft, bcol):
    def body(f_ref, b_ref, c_ref):
        z = f_ref[...] + b_ref[...]
        log_f = jnp.minimum(z, 0.0) - jnp.log1p(jnp.exp(-jnp.abs(z)))
        c_ref[...] = _lane_scan(log_f, reverse=False)

    return pl.pallas_call(
        body, name="forget_cumsum", out_shape=jax.ShapeDtypeStruct(fft.shape, F32),
        in_specs=[_resident(), _resident()], out_specs=_resident(),
    )(fft, bcol)


def _forget_bwd(dc_row, fft, bcol, a):
    def body(dc_ref, f_ref, b_ref, a_ref, dff_ref, db_ref, dw_ref):
        z = f_ref[...] + b_ref[...]
        dlog_f = _lane_scan(dc_ref[...], reverse=True)
        dff = dlog_f * (1.0 / (1.0 + jnp.exp(z)))
        dff_ref[...] = dff
        db_ref[...] = jnp.sum(dff, axis=1, keepdims=True)
        dw_ref[...] = _dot(dff.astype(MM), a_ref[...])

    return pl.pallas_call(
        body, name="forget_bwd",
        out_shape=[jax.ShapeDtypeStruct(fft.shape, F32), jax.ShapeDtypeStruct((fft.shape[0], 1), F32),
                   jax.ShapeDtypeStruct((fft.shape[0], D_MODEL), F32)],
        in_specs=[_resident()] * 4, out_specs=[_resident()] * 3,
    )(dc_row, fft, bcol, a)


def _head_select(shape, upper):
    lane = lax.broadcasted_iota(jnp.int32, shape, 1)
    return lane >= HEAD_DIM if upper else lane < HEAD_DIM


def _fox_fwd(fqkv, c_row3, tq, tk, pairs_per_loop=2, row_chunks=1):
    S = fqkv.shape[0]
    rq = tq // row_chunks
    n_band = tq // tk

    def body(q_ref, k_ref, v_ref, ck_ref, o_ref, lse_ref):
        qi = pl.program_id(0)
        row = lax.broadcasted_iota(jnp.int32, (rq, tk), 0)
        col = lax.broadcasted_iota(jnp.int32, (rq, tk), 1)
        low = _head_select((rq, 128), 0)
        for first in range(0, N_HEADS // 2, pairs_per_loop):
            pairs = range(first, first + pairs_per_loop)
            chains = [(pr, hh, rc) for pr in pairs for hh in range(2) for rc in range(row_chunks)]
            qh = {}
            for pr in pairs:
                for rc in range(row_chunks):
                    q2 = q_ref[rc * rq:(rc + 1) * rq, pr * 128:(pr + 1) * 128]
                    qh[pr, 0, rc] = jnp.where(low, q2, jnp.zeros_like(q2))
                    qh[pr, 1, rc] = jnp.where(low, jnp.zeros_like(q2), q2)

            def block(kb, carry, band, chains=chains, qh=qh):
                rows = pl.ds(pl.multiple_of(kb * tk, tk), tk)
                out = []
                for (pr, hh, rc), (m, l, acc) in zip(chains, carry):
                    if band is not None and (rc + 1) * rq <= band * tk:
                        out.append((m, l, acc))
                        continue
                    lanes = slice(pr * 128, (pr + 1) * 128)
                    s = _dot_nt(qh[pr, hh, rc], k_ref[rows, lanes]) - ck_ref[2 * pr + hh, pl.ds(kb, 1), :]
                    if band is not None:
                        s = jnp.where(row + rc * rq >= col + band * tk, s, NEG)
                    m_new = jnp.maximum(m, jnp.max(s, axis=-1, keepdims=True))
                    p = jnp.exp(s - m_new)
                    alpha = jnp.exp(m - m_new)
                    l = alpha * l + jnp.sum(p, axis=-1, keepdims=True)
                    acc = alpha * acc + _dot(p.astype(MM), v_ref[rows, lanes])
                    out.append((m_new, l, acc))
                return tuple(out)

            carry = tuple((jnp.full((rq, 1), NEG, F32), jnp.zeros((rq, 1), F32), jnp.zeros((rq, 128), F32))
                          for _ in chains)
            carry = lax.fori_loop(0, qi * n_band, functools.partial(block, band=None), carry)
            for band in range(n_band):
                carry = block(qi * n_band + band, carry, band=band)
            res = {}
            for (pr, hh, rc), (m, l, acc) in zip(chains, carry):
                res[pr, hh, rc] = acc / l
                lse_ref[rc * rq:(rc + 1) * rq, 2 * pr + hh:2 * pr + hh + 1] = m + jnp.log(l)
            for pr in pairs:
                for rc in range(row_chunks):
                    o_ref[rc * rq:(rc + 1) * rq, pr * 128:(pr + 1) * 128] = jnp.where(
                        low, res[pr, 0, rc], res[pr, 1, rc]).astype(MM)

    return pl.pallas_call(
        body, name="fox_fwd", grid=(S // tq,),
        in_specs=[pl.BlockSpec((tq, D_ATT), lambda i: (i, 0)), pl.BlockSpec((S, D_ATT), lambda i: (0, 1)),
                  pl.BlockSpec((S, D_ATT), lambda i: (0, 2)), _resident()],
        out_specs=[_rows(tq, D_ATT), _rows(tq, N_HEADS)],
        out_shape=[jax.ShapeDtypeStruct((S, D_ATT), MM), jax.ShapeDtypeStruct((S, N_HEADS), F32)],
        compiler_params=_params("parallel"),
    )(fqkv, fqkv, fqkv, c_row3)


def _swa_bias(rel_bias_slot, bucket):
    def body(rb_ref, bk_ref, out_ref):
        bk = bk_ref[...]
        for s in range(N_HEADS):
            acc = jnp.where(bk < 0, NEG, 0.0).astype(F32)
            for b in range(N_BUCKETS):
                acc = jnp.where(bk == b, rb_ref[b, s], acc)
            out_ref[s] = acc

    return pl.pallas_call(
        body, name="swa_bias", out_shape=jax.ShapeDtypeStruct((N_HEADS, WINDOW, 2 * WINDOW), F32),
        in_specs=[pl.BlockSpec(memory_space=pltpu.SMEM), _resident()], out_specs=_resident(),
    )(rel_bias_slot, bucket)


def _stack4(piece):
    return jnp.concatenate([piece(j) for j in range(4)], axis=0)


def _swa_specs(S):
    q = pl.BlockSpec((WINDOW, D_ATT), lambda n: (n, 0))
    kp = pl.BlockSpec((WINDOW, D_KV), lambda n: (jnp.maximum(n - 1, 0), 4))
    kc = pl.BlockSpec((WINDOW, D_KV), lambda n: (n, 4))
    vp = pl.BlockSpec((WINDOW, D_KV), lambda n: (jnp.maximum(n - 1, 0), 5))
    vc = pl.BlockSpec((WINDOW, D_KV), lambda n: (n, 5))
    return [q, kp, kc, vp, vc]


def _swa_fwd(sqkv, biasm, sinks_slot, after=None):
    S = sqkv.shape[0]

    def body(q_ref, kp_ref, kc_ref, vp_ref, vc_ref, bias_ref, sink_ref, o_ref, lse_ref):
        n = pl.program_id(0)
        no_prev = jnp.where(n > 0, 0.0, NEG)
        low = _head_select((WINDOW, 128), 0)
        res = []
        for g in range(2):
            sel = low if g == 0 else jnp.logical_not(low)
            qg = _stack4(lambda j: jnp.where(sel, q_ref[:, j * 128:(j + 1) * 128], jnp.zeros((WINDOW, 128), MM)))
            sink = _stack4(lambda j: jnp.full((WINDOW, 1), sink_ref[2 * j + g], F32))
            sp = _dot_nt(qg, kp_ref[...]) + _stack4(lambda j: bias_ref[2 * j + g, :, :WINDOW]) + no_prev
            sc = _dot_nt(qg, kc_ref[...]) + _stack4(lambda j: bias_ref[2 * j + g, :, WINDOW:])
            m = jnp.maximum(jnp.maximum(jnp.max(sp, axis=-1, keepdims=True),
                                        jnp.max(sc, axis=-1, keepdims=True)), sink)
            ep = jnp.exp(sp - m)
            ec = jnp.exp(sc - m)
            den = jnp.sum(ep, axis=-1, keepdims=True) + jnp.sum(ec, axis=-1, keepdims=True) + jnp.exp(sink - m)
            res.append((_dot(ep.astype(MM), vp_ref[...]) + _dot(ec.astype(MM), vc_ref[...])) / den)
            lse = m + jnp.log(den)
            for j in range(4):
                lse_ref[:, 2 * j + g:2 * j + g + 1] = lse[j * WINDOW:(j + 1) * WINDOW]
        for j in range(4):
            rows = slice(j * WINDOW, (j + 1) * WINDOW)
            o_ref[:, j * 128:(j + 1) * 128] = jnp.where(low, res[0][rows], res[1][rows]).astype(MM)

    body, in_specs, operands = _run_after(
        after, body, _swa_specs(S) + [_resident(), pl.BlockSpec(memory_space=pltpu.SMEM)],
        (sqkv, sqkv, sqkv, sqkv, sqkv, biasm, sinks_slot))
    return pl.pallas_call(
        body, name="swa_fwd", grid=(S // WINDOW,), in_specs=in_specs,
        out_specs=[_rows(WINDOW, D_ATT), _rows(WINDOW, N_HEADS)],
        out_shape=[jax.ShapeDtypeStruct((S, D_ATT), MM), jax.ShapeDtypeStruct((S, N_HEADS), F32)],
        compiler_params=_params("parallel"),
    )(*operands)


def _post_attn(x, fox_o, swa_o, wout_fox, wout_swa, g2, g3, tm):
    S = x.shape[0]

    def body(x_ref, fo_ref, so_ref, wf_ref, ws_ref, g2_ref, g3_ref, mix_ref, h1_ref, m_ref):
        mix = _dot(fo_ref[...], wf_ref[...]) + _dot(so_ref[...], ws_ref[...])
        mix_ref[...] = mix
        n2, _ = _rms(mix)
        h1 = x_ref[...] + n2 * g2_ref[...]
        h1_ref[...] = h1
        n3, _ = _rms(h1)
        m_ref[...] = (n3 * g3_ref[...]).astype(MM)

    return pl.pallas_call(
        body, name="post_attn", grid=(S // tm,),
        in_specs=[_rows(tm, D_MODEL), _rows(tm, D_ATT), _rows(tm, D_ATT), _resident(), _resident(),
                  _const((1, D_MODEL)), _const((1, D_MODEL))],
        out_specs=[_rows(tm, D_MODEL)] * 3,
        out_shape=[jax.ShapeDtypeStruct((S, D_MODEL), F32), jax.ShapeDtypeStruct((S, D_MODEL), F32),
                   jax.ShapeDtypeStruct((S, D_MODEL), MM)],
        compiler_params=_params("parallel"),
    )(x, fox_o, swa_o, wout_fox, wout_swa, g2, g3)


def _mlp_fwd(m, h1, w1, w2, g4, tm):
    S = m.shape[0]

    def body(m_ref, h1_ref, w1_ref, w2_ref, g4_ref, u_ref, y_ref, h2_ref):
        mb = m_ref[...]
        y = jnp.zeros((tm, D_MODEL), F32)
        for j in range(N_DEV):
            cols = slice(j * FF_CHUNK, (j + 1) * FF_CHUNK)
            u = _dot(mb, w1_ref[j])
            u_ref[:, cols] = u.astype(MM)
            y = y + _dot(jnp.square(jnp.maximum(u, 0.0)).astype(MM), w2_ref[cols, :])
        y_ref[...] = y
        n4, _ = _rms(y)
        h2_ref[...] = h1_ref[...] + n4 * g4_ref[...]

    return pl.pallas_call(
        body, name="mlp_fwd", grid=(S // tm,),
        in_specs=[_rows(tm, D_MODEL), _rows(tm, D_MODEL), _resident(), _resident(), _const((1, D_MODEL))],
        out_specs=[_rows(tm, D_FF), _rows(tm, D_MODEL), _rows(tm, D_MODEL)],
        out_shape=[jax.ShapeDtypeStruct((S, D_FF), MM), jax.ShapeDtypeStruct((S, D_MODEL), F32),
                   jax.ShapeDtypeStruct((S, D_MODEL), F32)],
        compiler_params=_params("parallel"),
    )(m, h1, w1, w2, g4)


def _ple_loss(h2, p, target, wg, wple, g5, tm):
    S = h2.shape[0]

    def body(h2_ref, p_ref, t_ref, wg_ref, wp_ref, g5_ref, dh2_ref, dpe_ref, dgl_ref, dg5_ref, loss_ref):
        i = pl.program_id(0)
        h2 = h2_ref[...]
        gate = jax.nn.sigmoid(_dot(h2.astype(MM), wg_ref[...]))
        pe = _dot(p_ref[...].astype(MM), wp_ref[...])
        n5, r5 = _rms(pe * gate)
        g5 = g5_ref[...]
        diff = h2 + n5 * g5 - t_ref[...]
        per_token = jnp.mean(jnp.square(diff), axis=-1, keepdims=True)
        _accumulate(loss_ref, 0.5 * jnp.sum(per_token, axis=0, keepdims=True), i)
        dh3 = diff * (1.0 / D_MODEL)
        de, dg5 = _rms_bwd(dh3, n5, r5, g5)
        _accumulate(dg5_ref, dg5, i)
        dpe_ref[...] = (de * gate).astype(MM)
        dgl = (de * pe * gate * (1.0 - gate)).astype(MM)
        dgl_ref[...] = dgl
        dh2_ref[...] = dh3 + _dot_nt(dgl, wg_ref[...])

    return pl.pallas_call(
        body, name="ple_loss", grid=(S // tm,),
        in_specs=[_rows(tm, D_MODEL), _rows(tm, D_PLE), _rows(tm, D_MODEL), _resident(), _resident(),
                  _const((1, D_MODEL))],
        out_specs=[_rows(tm, D_MODEL), _rows(tm, D_MODEL), _rows(tm, D_MODEL), _const((1, D_MODEL)), _const((1, 1))],
        out_shape=[jax.ShapeDtypeStruct((S, D_MODEL), F32), jax.ShapeDtypeStruct((S, D_MODEL), MM),
                   jax.ShapeDtypeStruct((S, D_MODEL), MM), jax.ShapeDtypeStruct((1, D_MODEL), F32),
                   jax.ShapeDtypeStruct((1, 1), F32)],
        compiler_params=_params("arbitrary"),
    )(h2, p, target, wg, wple, g5)


def _mlp_bwd(dh2, y, h1, u, w1, w2, g4, g3, tm):
    S = dh2.shape[0]

    def body(dh2_ref, y_ref, h1_ref, u_ref, w1_ref, w2_ref, g4_ref, g3_ref,
             dh1_ref, dy_ref, du_ref, dg4_ref, dg3_ref):
        i = pl.program_id(0)
        dh2 = dh2_ref[...]
        n4, r4 = _rms(y_ref[...])
        dy, dg4 = _rms_bwd(dh2, n4, r4, g4_ref[...])
        _accumulate(dg4_ref, dg4, i)
        dyb = dy.astype(MM)
        dy_ref[...] = dyb
        dm = jnp.zeros((tm, D_MODEL), F32)
        for j in range(N_DEV):
            cols = slice(j * FF_CHUNK, (j + 1) * FF_CHUNK)
            dact = _dot_nt(dyb, w2_ref[cols, :])
            du = (dact * (2.0 * jnp.maximum(u_ref[:, cols].astype(F32), 0.0))).astype(MM)
            du_ref[:, cols] = du
            dm = dm + _dot_nt(du, w1_ref[j])
        n3, r3 = _rms(h1_ref[...])
        dx, dg3 = _rms_bwd(dm, n3, r3, g3_ref[...])
        _accumulate(dg3_ref, dg3, i)
        dh1_ref[...] = dh2 + dx

    return pl.pallas_call(
        body, name="mlp_bwd", grid=(S // tm,),
        in_specs=[_rows(tm, D_MODEL), _rows(tm, D_MODEL), _rows(tm, D_MODEL), _rows(tm, D_FF),
                  _resident(), _resident(), _const((1, D_MODEL)), _const((1, D_MODEL))],
        out_specs=[_rows(tm, D_MODEL), _rows(tm, D_MODEL), _rows(tm, D_FF), _const((1, D_MODEL)),
                   _const((1, D_MODEL))],
        out_shape=[jax.ShapeDtypeStruct((S, D_MODEL), F32), jax.ShapeDtypeStruct((S, D_MODEL), MM),
                   jax.ShapeDtypeStruct((S, D_FF), MM), jax.ShapeDtypeStruct((1, D_MODEL), F32),
                   jax.ShapeDtypeStruct((1, D_MODEL), F32)],
        compiler_params=_params("arbitrary"),
    )(dh2, y, h1, u, w1, w2, g4, g3)


def _attn_out_bwd(dh1, mix, fox_o, swa_o, wout_fox, wout_swa, g2, head_rows, tm):
    S = dh1.shape[0]

    def body(dh1_ref, mix_ref, fo_ref, so_ref, wf_ref, ws_ref, g2_ref, er_ref,
             dmix_ref, dcat_ref, drow_ref, dswa_ref, dg2_ref):
        i = pl.program_id(0)
        n2, r2 = _rms(mix_ref[...])
        dmix, dg2 = _rms_bwd(dh1_ref[...], n2, r2, g2_ref[...])
        _accumulate(dg2_ref, dg2, i)
        dmb = dmix.astype(MM)
        dmix_ref[...] = dmb
        dfo = _dot_nt(dmb, wf_ref[...]).astype(MM)
        dso = _dot_nt(dmb, ws_ref[...]).astype(MM)
        dcat_ref[:, :D_ATT] = dfo
        dcat_ref[:, D_ATT:] = dso
        hi = lax.Precision.HIGHEST
        prod_f = dfo.astype(F32) * fo_ref[...].astype(F32)
        prod_s = dso.astype(F32) * so_ref[...].astype(F32)
        drow_ref[...] = lax.dot_general(er_ref[...], prod_f, NT, precision=hi, preferred_element_type=F32)
        dswa_ref[...] = lax.dot_general(er_ref[...], prod_s, NT, precision=hi, preferred_element_type=F32)

    return pl.pallas_call(
        body, name="attn_out_bwd", grid=(S // tm,),
        in_specs=[_rows(tm, D_MODEL), _rows(tm, D_MODEL), _rows(tm, D_ATT), _rows(tm, D_ATT), _resident(),
                  _resident(), _const((1, D_MODEL)), _resident()],
        out_specs=[_rows(tm, D_MODEL), _rows(tm, D_MODEL), pl.BlockSpec((N_HEADS, tm), lambda i: (0, i)),
                   pl.BlockSpec((N_HEADS, tm), lambda i: (0, i)), _const((1, D_MODEL))],
        out_shape=[jax.ShapeDtypeStruct((S, D_MODEL), MM), jax.ShapeDtypeStruct((S, D_MODEL), MM),
                   jax.ShapeDtypeStruct((N_HEADS, S), F32), jax.ShapeDtypeStruct((N_HEADS, S), F32),
                   jax.ShapeDtypeStruct((1, D_MODEL), F32)],
        compiler_params=_params("arbitrary"),
    )(dh1, mix, fox_o, swa_o, wout_fox, wout_swa, g2, head_rows)


def _fox_bwd(fqkv, dcat, lse_row3, d_row3, c_col, tq, tk, pairs_per_loop=2, after=None):
    S = fqkv.shape[0]
    n_blk = S // tk
    n_qblk = S // tq
    n_band = tk // tq

    def body(q_ref, k_ref, v_ref, do_ref, lse_ref, dd_ref, ck_ref, dq_ref, dk_ref, dv_ref, dc_ref, dcq_ref):
        kb = pl.program_id(0)

        @pl.when(kb == 0)
        def _():
            dq_ref[...] = jnp.zeros_like(dq_ref)
            dcq_ref[...] = jnp.zeros_like(dcq_ref)

        key = lax.broadcasted_iota(jnp.int32, (tk, tq), 0)
        qry = lax.broadcasted_iota(jnp.int32, (tk, tq), 1)
        low = _head_select((tk, 128), 0)
        for first in range(0, N_HEADS // 2, pairs_per_loop):
            pairs = range(first, first + pairs_per_loop)
            heads = [(pr, hh) for pr in pairs for hh in range(2)]
            kh, vh, ck = {}, {}, {}
            for pr in pairs:
                k2 = k_ref[:, pr * 128:(pr + 1) * 128]
                v2 = v_ref[:, pr * 128:(pr + 1) * 128]
                zero = jnp.zeros_like(k2)
                kh[pr, 0], kh[pr, 1] = jnp.where(low, k2, zero), jnp.where(low, zero, k2)
                vh[pr, 0], vh[pr, 1] = jnp.where(low, v2, zero), jnp.where(low, zero, v2)
                for hh in range(2):
                    ck[pr, hh] = ck_ref[:, 2 * pr + hh:2 * pr + hh + 1]

            def block(qb, carry, band, pairs=pairs, kh=kh, vh=vh, ck=ck):
                rows = pl.ds(pl.multiple_of(qb * tq, tq), tq)
                k1 = tk if band is None else (band + 1) * tq
                out = []
                it = iter(carry)
                for pr in pairs:
                    lanes = slice(pr * 128, (pr + 1) * 128)
                    q2 = q_ref[rows, lanes]
                    do2 = do_ref[rows, lanes]
                    dq = None
                    for hh in range(2):
                        h = 2 * pr + hh
                        dk, dv, dc = next(it)
                        s_t = _dot_nt(kh[pr, hh][:k1], q2) - ck[pr, hh][:k1]
                        p_t = jnp.exp(s_t - lse_ref[h, pl.ds(qb, 1), :])
                        if band is not None:
                            p_t = jnp.where(qry[:k1] + band * tq >= key[:k1], p_t, 0.0)
                        ds_t = p_t * (_dot_nt(vh[pr, hh][:k1], do2) - dd_ref[h, pl.ds(qb, 1), :])
                        dsb = ds_t.astype(MM)
                        dv_new = dv[:k1] + _dot(p_t.astype(MM), do2)
                        dk_new = dk[:k1] + _dot(dsb, q2)
                        dc_new = dc[:k1] - jnp.sum(ds_t, axis=1, keepdims=True)
                        if k1 < tk:
                            dv_new = jnp.concatenate([dv_new, dv[k1:]], axis=0)
                            dk_new = jnp.concatenate([dk_new, dk[k1:]], axis=0)
                            dc_new = jnp.concatenate([dc_new, dc[k1:]], axis=0)
                        part = _dot_tn(dsb, kh[pr, hh][:k1])
                        dq = part if dq is None else dq + part
                        dcq_ref[h, pl.ds(qb, 1), :] += jnp.sum(ds_t, axis=0, keepdims=True)
                        out.append((dk_new, dv_new, dc_new))
                    dq_ref[rows, lanes] += dq
                return tuple(out)

            carry = tuple((jnp.zeros((tk, 128), F32), jnp.zeros((tk, 128), F32), jnp.zeros((tk, 1), F32))
                          for _ in heads)
            for band in range(n_band):
                carry = block(kb * n_band + band, carry, band=band)
            carry = lax.fori_loop((kb + 1) * n_band, n_qblk, functools.partial(block, band=None), carry)
            grads = dict(zip(heads, carry))
            for pr in pairs:
                lanes = slice(pr * 128, (pr + 1) * 128)
                dk_ref[:, lanes] = jnp.where(low, grads[pr, 0][0], grads[pr, 1][0]).astype(MM)
                dv_ref[:, lanes] = jnp.where(low, grads[pr, 0][1], grads[pr, 1][1]).astype(MM)
                for hh in range(2):
                    dc_ref[:, 2 * pr + hh:2 * pr + hh + 1] = grads[pr, hh][2]

        @pl.when(kb == n_blk - 1)
        def _():
            dq_ref[...] = dq_ref[...] * Q_SCALE

    body, in_specs, operands = _run_after(
        after, body,
        [pl.BlockSpec((S, D_ATT), lambda i: (0, 0)), pl.BlockSpec((tk, D_ATT), lambda i: (i, 1)),
         pl.BlockSpec((tk, D_ATT), lambda i: (i, 2)), pl.BlockSpec((S, D_ATT), lambda i: (0, 0)),
         _resident(), _resident(), _rows(tk, N_HEADS)],
        (fqkv, fqkv, fqkv, dcat, lse_row3, d_row3, c_col))
    return pl.pallas_call(
        body, name="fox_bwd", grid=(n_blk,), in_specs=in_specs,
        out_specs=[_const((S, D_ATT)), _rows(tk, D_ATT), _rows(tk, D_ATT), _rows(tk, N_HEADS),
                   _const((N_HEADS, n_qblk, tq))],
        out_shape=[jax.ShapeDtypeStruct((S, D_ATT), F32), jax.ShapeDtypeStruct((S, D_ATT), MM),
                   jax.ShapeDtypeStruct((S, D_ATT), MM), jax.ShapeDtypeStruct((S, N_HEADS), F32),
                   jax.ShapeDtypeStruct((N_HEADS, n_qblk, tq), F32)],
        compiler_params=_params("arbitrary"),
    )(*operands)


def _swa_bwd(sqkv, dcat, biasm, sinks_slot, bucket, lse, d_col, after=None):
    S = sqkv.shape[0]
    n_blk = S // WINDOW

    def body(q_ref, kp_ref, kc_ref, vp_ref, vc_ref, do_ref, bias_ref, sink_ref, bk_ref, lse_ref, dd_ref,
             dq_ref, dk_ref, dv_ref, drb_ref, dsink_ref, ds_acc):
        n = pl.program_id(0)

        @pl.when(n == 0)
        def _():
            dk_ref[...] = jnp.zeros_like(dk_ref)
            dv_ref[...] = jnp.zeros_like(dv_ref)
            ds_acc[...] = jnp.zeros_like(ds_acc)
            dsink_ref[...] = jnp.zeros_like(dsink_ref)

        no_prev = jnp.where(n > 0, 0.0, NEG)
        prev = pl.ds(pl.multiple_of(jnp.maximum(n - 1, 0) * WINDOW, WINDOW), WINDOW)
        cur = pl.ds(pl.multiple_of(n * WINDOW, WINDOW), WINDOW)
        lane8 = lax.broadcasted_iota(jnp.int32, (1, N_HEADS), 1)
        dkp = jnp.zeros((WINDOW, D_KV), F32)
        dkc = jnp.zeros((WINDOW, D_KV), F32)
        dvp = jnp.zeros((WINDOW, D_KV), F32)
        dvc = jnp.zeros((WINDOW, D_KV), F32)
        dsink = jnp.zeros((1, N_HEADS), F32)
        low = _head_select((WINDOW, 128), 0)
        zero = jnp.zeros((WINDOW, 128), MM)
        dqs = []
        for g in range(2):
            sel = low if g == 0 else jnp.logical_not(low)
            qg = _stack4(lambda j: jnp.where(sel, q_ref[:, j * 128:(j + 1) * 128], zero))
            dog = _stack4(lambda j: jnp.where(sel, do_ref[:, j * 128:(j + 1) * 128], zero))
            lse_g = _stack4(lambda j: lse_ref[:, 2 * j + g:2 * j + g + 1])
            dd = _stack4(lambda j: dd_ref[:, 2 * j + g:2 * j + g + 1])
            sink = _stack4(lambda j: jnp.full((WINDOW, 1), sink_ref[2 * j + g], F32))
            pp = jnp.exp(_dot_nt(qg, kp_ref[...]) + _stack4(lambda j: bias_ref[2 * j + g, :, :WINDOW]) + no_prev - lse_g)
            pc = jnp.exp(_dot_nt(qg, kc_ref[...]) + _stack4(lambda j: bias_ref[2 * j + g, :, WINDOW:]) - lse_g)
            sink_term = jnp.exp(sink - lse_g) * dd
            dsp = pp * (_dot_nt(dog, vp_ref[...]) - dd)
            dsc = pc * (_dot_nt(dog, vc_ref[...]) - dd)
            for j in range(4):
                rows = slice(j * WINDOW, (j + 1) * WINDOW)
                dsink = dsink + jnp.where(lane8 == 2 * j + g, -jnp.sum(sink_term[rows]), 0.0)
                ds_acc[2 * j + g, :, :WINDOW] += dsp[rows]
                ds_acc[2 * j + g, :, WINDOW:] += dsc[rows]
            dspb, dscb = dsp.astype(MM), dsc.astype(MM)
            dqs.append(_dot(dspb, kp_ref[...]) + _dot(dscb, kc_ref[...]))
            dkp = dkp + _dot_tn(dspb, qg)
            dkc = dkc + _dot_tn(dscb, qg)
            dvp = dvp + _dot_tn(pp.astype(MM), dog)
            dvc = dvc + _dot_tn(pc.astype(MM), dog)
        for j in range(4):
            rows = slice(j * WINDOW, (j + 1) * WINDOW)
            dq_ref[:, j * 128:(j + 1) * 128] = (jnp.where(low, dqs[0][rows], dqs[1][rows]) * Q_SCALE).astype(MM)
        dk_ref[prev, :] += dkp
        dk_ref[cur, :] += dkc
        dv_ref[prev, :] += dvp
        dv_ref[cur, :] += dvc
        dsink_ref[...] += dsink

        @pl.when(n == n_blk - 1)
        def _():
            bk = bk_ref[...]
            rb = lax.broadcasted_iota(jnp.int32, (N_BUCKETS, N_HEADS), 0)
            cb = lax.broadcasted_iota(jnp.int32, (N_BUCKETS, N_HEADS), 1)
            out = jnp.zeros((N_BUCKETS, N_HEADS), F32)
            for s in range(N_HEADS):
                acc = ds_acc[s]
                for b in range(N_BUCKETS):
                    out = out + jnp.where((rb == b) & (cb == s), jnp.sum(jnp.where(bk == b, acc, 0.0)), 0.0)
            drb_ref[...] = out

    do_spec = pl.BlockSpec((WINDOW, D_ATT), lambda n: (n, 1))
    body, in_specs, operands = _run_after(
        after, body, _swa_specs(S) + [do_spec, _resident(), pl.BlockSpec(memory_space=pltpu.SMEM), _resident(),
                                      _rows(WINDOW, N_HEADS), _rows(WINDOW, N_HEADS)],
        (sqkv, sqkv, sqkv, sqkv, sqkv, dcat, biasm, sinks_slot, bucket, lse, d_col))
    return pl.pallas_call(
        body, name="swa_bwd", grid=(n_blk,), in_specs=in_specs,
        out_specs=[_rows(WINDOW, D_ATT), _const((S, D_KV)), _const((S, D_KV)), _const((N_BUCKETS, N_HEADS)),
                   _const((1, N_HEADS))],
        out_shape=[jax.ShapeDtypeStruct((S, D_ATT), MM), jax.ShapeDtypeStruct((S, D_KV), F32),
                   jax.ShapeDtypeStruct((S, D_KV), F32), jax.ShapeDtypeStruct((N_BUCKETS, N_HEADS), F32),
                   jax.ShapeDtypeStruct((1, N_HEADS), F32)],
        scratch_shapes=[pltpu.VMEM((N_HEADS, WINDOW, 2 * WINDOW), F32)],
        compiler_params=_params("arbitrary"),
    )(*operands)


def _weight_grad_pieces(pieces, b, name, tk):
    S, N = b.shape
    wide = [p for p in pieces if p.shape[1] % tk == 0]
    narrow = pieces[len(wide):]
    assert sum(p.shape[1] for p in narrow) == tk
    counts = [p.shape[1] // tk for p in wide]
    firsts = [sum(counts[:j]) for j in range(len(wide))]
    last = sum(counts)

    def body(*refs):
        b_ref, out_ref = refs[len(pieces)], refs[len(pieces) + 1]
        i = pl.program_id(0)

        def whole_steps(ref, first, count):
            @pl.when((i >= first) & (i < first + count))
            def _():
                out_ref[...] = _dot_tn(ref[...].astype(MM), b_ref[...]).astype(MM)

        for ref, first, count in zip(refs[:len(wide)], firsts, counts):
            whole_steps(ref, first, count)

        @pl.when(i == last)
        def _():
            row = 0
            for ref in refs[len(wide):len(pieces)]:
                k = ref.shape[1]
                out_ref[row:row + k] = _dot_tn(ref[...].astype(MM), b_ref[...]).astype(MM)
                row += k

    def steps_of(first, count):
        return pl.BlockSpec((S, tk), lambda i: (0, jnp.clip(i - first, 0, count - 1)))

    return pl.pallas_call(
        body, name=name, grid=(last + 1,),
        in_specs=[steps_of(first, count) for first, count in zip(firsts, counts)]
        + [pl.BlockSpec((S, p.shape[1]), lambda i: (0, 0)) for p in narrow] + [_resident()],
        out_specs=pl.BlockSpec((tk, N), lambda i: (i, 0)), out_shape=jax.ShapeDtypeStruct(((last + 1) * tk, N), MM),
        compiler_params=_params("parallel"),
    )(*pieces, b)


def _pre_attn_bwd(x, dh1, dz, dff_t, win_t, wt_rest, g1, tm, after=None):
    S = x.shape[0]

    def body(x_ref, dh1_ref, dq_ref, dk_ref, dv_ref, dsq_ref, dsk_ref, dsv_ref, dff_ref, wf_ref, wr_ref, g1_ref,
             dx_ref, dg1_ref):
        i = pl.program_id(0)
        dz_fox = jnp.concatenate([dq_ref[...].astype(MM), dk_ref[...], dv_ref[...]], axis=1)
        dz_swa = jnp.concatenate([dsq_ref[...], dsk_ref[...].astype(MM), dsv_ref[...].astype(MM)], axis=1)
        da = (_dot(dz_fox, wf_ref[...]) + _dot(dz_swa, wr_ref[WT_SQ:WT_REST])
              + _dot_tn(dff_ref[...].astype(MM), wr_ref[0:WT_SQ]))
        n1, r1 = _rms(x_ref[...])
        dx, dg1 = _rms_bwd(da, n1, r1, g1_ref[...])
        _accumulate(dg1_ref, dg1, i)
        dx_ref[...] = dh1_ref[...] + dx

    body, in_specs, operands = _run_after(
        after, body,
        [_rows(tm, D_MODEL), _rows(tm, D_MODEL), *[_rows(tm, d.shape[1]) for d in dz],
         pl.BlockSpec((16, tm), lambda i: (0, i)), _const((WT_FOX, D_MODEL)), _resident(), _const((1, D_MODEL))],
        (x, dh1, *dz, dff_t, win_t, wt_rest, g1))
    return pl.pallas_call(
        body, name="pre_attn_bwd", grid=(S // tm,), in_specs=in_specs,
        out_specs=[_rows(tm, D_MODEL), _const((1, D_MODEL))],
        out_shape=[jax.ShapeDtypeStruct((S, D_MODEL), F32), jax.ShapeDtypeStruct((1, D_MODEL), F32)],
        compiler_params=_params("arbitrary"),
    )(*operands)


def _weight_grad(a, b, name, tk, n_chunks=1, relu2=False):
    S, K = a.shape
    N = b.shape[1]
    cn = N // n_chunks

    def body(a_ref, b_ref, out_ref):
        av = a_ref[...]
        if relu2:
            av = jnp.square(jnp.maximum(av.astype(F32), 0.0))
        av = av.astype(MM)
        for j in range(n_chunks):
            val = _dot_tn(av, b_ref[:, j * cn:(j + 1) * cn].astype(MM)).astype(MM)
            if n_chunks > 1:
                out_ref[j] = val
            else:
                out_ref[...] = val

    if n_chunks > 1:
        out_spec = pl.BlockSpec((n_chunks, tk, cn), lambda i: (0, i, 0))
        out_shape = jax.ShapeDtypeStruct((n_chunks, K, cn), MM)
    else:
        out_spec = pl.BlockSpec((tk, N), lambda i: (i, 0))
        out_shape = jax.ShapeDtypeStruct((K, N), MM)
    return pl.pallas_call(
        body, name=name, grid=(K // tk,),
        in_specs=[pl.BlockSpec((S, tk), lambda i: (0, i)), _resident()],
        out_specs=out_spec, out_shape=out_shape, compiler_params=_params("parallel"),
    )(a, b)


def _weight_grad_two(a1, a2, b, name, tk):
    S, K1 = a1.shape
    K2 = a2.shape[1]
    N = b.shape[1]
    n1 = K1 // tk

    def body(a1_ref, a2_ref, b_ref, out_ref):
        av = jnp.where(pl.program_id(0) < n1, a1_ref[...], a2_ref[...])
        out_ref[...] = _dot_tn(av, b_ref[...]).astype(MM)

    return pl.pallas_call(
        body, name=name, grid=((K1 + K2) // tk,),
        in_specs=[pl.BlockSpec((S, tk), lambda i: (0, jnp.minimum(i, n1 - 1))),
                  pl.BlockSpec((S, tk), lambda i: (0, jnp.maximum(i - n1, 0))), _resident()],
        out_specs=pl.BlockSpec((tk, N), lambda i: (i, 0)), out_shape=jax.ShapeDtypeStruct((K1 + K2, N), MM),
        compiler_params=_params("parallel"),
    )(a1, a2, b)


def _place():
    return lax.axis_index("x"), lax.axis_index("y"), lax.axis_index("c")


def _all_gather_sequencer(stacks, name, collective_id):
    refs = [jax.new_ref(s, memory_space=pltpu.MemorySpace.HBM) for s in stacks]
    n = len(refs)

    @pl.kernel(mesh=plsc.ScalarSubcoreMesh(axis_name="sequencer", num_cores=1), name=name,
               scratch_types=(pltpu.SemaphoreType.DMA((7 * n,)), pltpu.SemaphoreType.DMA((7 * n,))),
               compiler_params=pltpu.CompilerParams(collective_id=collective_id))
    def launch(send_sems, recv_sems):
        x, y, c = _place()
        sibling = (x, y, 1 - c)
        chips = [(1 - x, y), (x, 1 - y), (1 - x, 1 - y)]
        peers = [sibling] + [(px, py, c) for px, py in chips]
        barrier = pltpu.get_barrier_semaphore()
        for peer in peers:
            pl.semaphore_signal(barrier, inc=1, device_id=peer, device_id_type=MESH)
        pl.semaphore_wait(barrier, len(peers))

        def copy(a, k, block, to):
            px, py, pc = block
            slot = refs[a].at[4 * px + 2 * py + pc]
            return _remote(slot, slot, send_sems, recv_sems, 7 * a + k, to)

        first = [copy(a, k, (x, y, c), peer) for a in range(n) for k, peer in enumerate(peers)]
        for cp in first:
            cp.start()
        passed = []
        for j, (px, py) in enumerate(chips):
            for a in range(n):
                copy(a, 1 + j, (px, py, c), sibling).wait_recv()
                passed.append(copy(a, 4 + j, (px, py, c), sibling))
                passed[-1].start()
        for a in range(n):
            copy(a, 0, (x, y, 1 - c), sibling).wait_recv()
            for j, (px, py) in enumerate(chips):
                copy(a, 4 + j, (px, py, 1 - c), sibling).wait_recv()
        for cp in first + passed:
            cp.wait_send()

    launch()
    return [ref[...] for ref in refs]


def _chip_sums(grads, others, name):
    n = len(grads)

    def body(c_ref, *refs):
        for g_ref, o_ref, out_ref in zip(refs[:n], refs[n:2 * n], refs[2 * n:]):
            out_ref[...] = (g_ref[...].astype(F32) + o_ref[...].astype(F32)).astype(out_ref.dtype)

    own = [pl.BlockSpec((None, None) + g.shape[2:], lambda k, c_ref: (k, c_ref[0], 0, 0)) for g in grads]
    chip = [pl.BlockSpec((None,) + g.shape[2:], lambda k, c_ref: (k, 0, 0)) for g in grads]
    return pl.pallas_call(
        body, name=name,
        grid_spec=pltpu.PrefetchScalarGridSpec(num_scalar_prefetch=1, grid=(4,), in_specs=own + chip, out_specs=chip),
        out_shape=[jax.ShapeDtypeStruct((4,) + g.shape[2:], MM) for g in grads],
        compiler_params=_params("parallel"),
    )(lax.axis_index("c").astype(jnp.int32).reshape(1), *grads, *others)


HBM_SPEC = pl.BlockSpec(memory_space=pltpu.HBM)
SEM_SPEC = pl.BlockSpec(memory_space=pltpu.SEMAPHORE)
DATAFLOW = pltpu.SideEffectType.DATAFLOW_SIDE_EFFECTING


def _exchange_start(name, arrays, n_copies, plan):
    n = len(arrays)

    def body(*refs):
        send_sems, recv_sems, token = refs[n], refs[n + 1], refs[2 * n + 2]
        for cp in plan(refs[:n], send_sems, recv_sems):
            cp.start()
        token[...] = jnp.zeros_like(token)

    out = pl.pallas_call(
        body, name=name,
        out_shape=(pltpu.SemaphoreType.DMA((n_copies,)), pltpu.SemaphoreType.DMA((n_copies,)),
                   *[pltpu.HBM(a.shape, a.dtype) for a in arrays], jax.ShapeDtypeStruct((1, D_MODEL), F32)),
        in_specs=[HBM_SPEC] * n,
        out_specs=(SEM_SPEC, SEM_SPEC, *[HBM_SPEC] * n, pl.BlockSpec(memory_space=pltpu.VMEM)),
        input_output_aliases={i: 2 + i for i in range(n)},
        compiler_params=pltpu.CompilerParams(has_side_effects=DATAFLOW),
    )(*[pltpu.with_memory_space_constraint(a, pltpu.HBM) for a in arrays])
    return (out[0], out[1]), list(out[2:2 + n]), out[2 + n]


def _exchange_wait(name, arrays, sems, after, plan):
    n = len(arrays)
    after = list(after) if isinstance(after, (list, tuple)) else [after]

    def body(*refs):
        send_sems, recv_sems = refs[n], refs[n + 1]
        for cp in plan(refs[:n], send_sems, recv_sems):
            cp.wait_send()
            cp.wait_recv()

    out = pl.pallas_call(
        body, name=name, out_shape=[pltpu.HBM(a.shape, a.dtype) for a in arrays],
        in_specs=[HBM_SPEC] * n + [SEM_SPEC, SEM_SPEC] + [pl.BlockSpec(memory_space=pl.ANY)] * len(after),
        out_specs=[HBM_SPEC] * n, input_output_aliases={i: i for i in range(n)},
        compiler_params=pltpu.CompilerParams(has_side_effects=DATAFLOW),
    )(*arrays, sems[0], sems[1], *after)
    return list(out)


def _remote(src, dst, send_sems, recv_sems, k, to):
    return pltpu.make_async_remote_copy(src_ref=src, dst_ref=dst, send_sem=send_sems.at[k], recv_sem=recv_sems.at[k],
                                        device_id=to, device_id_type=MESH)


def _plan_gather_direct(refs, send_sems, recv_sems):
    x, y, c = _place()
    me = 4 * x + 2 * y + c
    peers = [(x, y, 1 - c), (1 - x, y, c), (x, 1 - y, c), (1 - x, 1 - y, c)]
    return [_remote(ref.at[me], ref.at[me], send_sems, recv_sems, 4 * a + k, peer)
            for a, ref in enumerate(refs) for k, peer in enumerate(peers)]


def _plan_gather_pass_on(refs, send_sems, recv_sems):
    x, y, c = _place()
    chips = [(1 - x, y), (x, 1 - y), (1 - x, 1 - y)]
    return [_remote(ref.at[4 * px + 2 * py + c], ref.at[4 * px + 2 * py + c], send_sems, recv_sems, 3 * a + k,
                    (x, y, 1 - c))
            for a, ref in enumerate(refs) for k, (px, py) in enumerate(chips)]


def _plan_in_chip(refs, send_sems, recv_sems):
    n = len(refs) // 2
    x, y, c = _place()
    return [_remote(refs[a].at[:, 1 - c], refs[n + a], send_sems, recv_sems, a, (x, y, 1 - c)) for a in range(n)]


def _plan_between_chips(refs, send_sems, recv_sems):
    n = len(refs) // 2
    x, y, c = _place()
    chips = [(1 - x, y), (x, 1 - y), (1 - x, 1 - y)]
    return [_remote(refs[a].at[2 * px + py], refs[n + a].at[2 * x + y], send_sems, recv_sems, 3 * a + k, (px, py, c))
            for a in range(n) for k, (px, py) in enumerate(chips)]


def _plan_late_between(refs, send_sems, recv_sems):
    sums, land, small = refs
    x, y, c = _place()
    me = 4 * x + 2 * y + c
    copies = _plan_between_chips([sums, land], send_sems, recv_sems)
    peers = [(x ^ dx, y ^ dy, c ^ dc) for dx in range(2) for dy in range(2) for dc in range(2) if dx + dy + dc]
    return copies + [_remote(small.at[me], small.at[me], send_sems, recv_sems, 3 + k, peer)
                     for k, peer in enumerate(peers)]


def _adamw_math(w, g, m, v):
    m = ADAM_B1 * m + (1.0 - ADAM_B1) * g
    v = ADAM_B2 * v + (1.0 - ADAM_B2) * jnp.square(g)
    m_hat = m / (1.0 - ADAM_B1 ** ADAM_STEP)
    v_hat = v / (1.0 - ADAM_B2 ** ADAM_STEP)
    delta = -ADAM_LR * (m_hat / (jnp.sqrt(v_hat) + ADAM_EPS) + ADAM_WD * w)
    return delta, m, v


def _adamw_small(parts, w, m, v):
    n_parts = parts.shape[0]
    n_rows = len(SMALL_ROWS)
    names = SMALL_ROWS + ("b_forget", "swa_sinks", "rel_bias")
    shapes = [(1, D_MODEL)] * n_rows + [(1, N_HEADS), (1, N_HEADS), (N_HEADS, N_BUCKETS)]

    def body(p_ref, w_ref, m_ref, v_ref, *outs):
        g = p_ref[0]
        for k in range(1, n_parts):
            g = g + p_ref[k]
        delta, m_new, v_new = _adamw_math(w_ref[...], g, m_ref[...], v_ref[...])
        for kind, val in enumerate((g, delta, m_new, v_new)):
            o = outs[kind * len(names):(kind + 1) * len(names)]
            for i in range(n_rows):
                o[i][...] = val[i:i + 1]
            misc = val[n_rows:n_rows + 1]
            o[n_rows][...] = misc[:, :N_HEADS]
            o[n_rows + 1][...] = misc[:, N_HEADS:2 * N_HEADS]
            for h in range(N_HEADS):
                first = 2 * N_HEADS + h * N_BUCKETS
                o[n_rows + 2][h:h + 1, :] = misc[:, first:first + N_BUCKETS]
        outs[-1][...] = g[n_rows + 1:n_rows + 2, 0:1]

    out = pl.pallas_call(
        body, name="adamw_small", in_specs=[_resident()] * 4, out_specs=[_resident()] * (4 * len(names) + 1),
        out_shape=[jax.ShapeDtypeStruct(s, F32) for s in shapes * 4] + [jax.ShapeDtypeStruct((1, 1), F32)],
        compiler_params=_params(),
    )(parts, w, m, v)
    kinds =[dict(zip(names, out[kind * len(names):(kind + 1) * len(names)])) for kind in range(4)]
    kinds[0]["loss"] = out[-1]
    return kinds


def _adamw_chips(parts, sums, w, m, v, name):
    _, r, cdim = parts.shape
    tr = 256 if r % 256 == 0 else r
    apart = w.ndim == 3

    def body(chip_ref, p_ref, own_ref, w_ref, m_ref, v_ref, g_out, d_out, m_out, v_out):
        g = None
        for k in range(4):
            term = jnp.where(chip_ref[0] == k, own_ref[...], p_ref[k]).astype(F32)
            g = term if g is None else g + term
        get = (lambda ref: ref[:, 0, :]) if apart else (lambda ref: ref[...])
        delta, m_new, v_new = _adamw_math(get(w_ref), g, get(m_ref), get(v_ref))
        for ref, val in ((g_out, g), (d_out, delta), (m_out, m_new), (v_out, v_new)):
            if apart:
                ref[:, 0, :] = val
            else:
                ref[...] = val

    if apart:
        blk = pl.BlockSpec((tr, 1, cdim), lambda i, chip: (i, 0, 0))
        shape = (r, 1, cdim)
    else:
        blk = pl.BlockSpec((tr, cdim), lambda i, chip: (i, 0))
        shape = (r, cdim)
    my_chip = (2 * lax.axis_index("x") + lax.axis_index("y")).astype(jnp.int32).reshape(1)
    return pl.pallas_call(
        body, name=name,
        grid_spec=pltpu.PrefetchScalarGridSpec(
            num_scalar_prefetch=1, grid=(r // tr,),
            in_specs=[pl.BlockSpec((4, tr, cdim), lambda i, chip: (0, i, 0)),
                      pl.BlockSpec((None, tr, cdim), lambda i, chip: (chip[0], i, 0)), blk, blk, blk],
            out_specs=[blk] * 4),
        out_shape=[jax.ShapeDtypeStruct(shape, F32)] * 4,
        compiler_params=_params("parallel"),
    )(my_chip, parts, sums, w, m, v)


class _NoExchange:
    def __init__(self, weights):
        self.weights = weights

    def before_pre_attn(self):
        return None

    def after_fox_fwd(self, fox_o):
        return None

    def after_attention(self, swa_o):
        return self.weights

    def after_early_grads(self, grads):
        return None

    def after_swa_bwd(self, dsq):
        return None

    def after_w_in_grad(self, d_win):
        return None


def _slot_order(t, axis):
    shp = t.shape
    t = t.reshape(shp[:axis] + (2, 4, shp[axis] // N_HEADS) + shp[axis + 1:])
    return jnp.swapaxes(t, axis, axis + 1).reshape(shp)


def _head_order(t, axis):
    shp = t.shape
    t = t.reshape(shp[:axis] + (4, 2, shp[axis] // N_HEADS) + shp[axis + 1:])
    return jnp.swapaxes(t, axis, axis + 1).reshape(shp)


def _forward_backward(x, p, target, win_t, hooks, b_forget, rel_bias, sinks, g1, g2, g3, g4, g5):
    S = x.shape[0]
    tm = 512
    tm_mlp = 512
    t = 256
    q0 = 3 * D_ATT + N_HEADS
    win_t = win_t.reshape(D_IN, D_MODEL)
    wt_rest = jnp.concatenate(
        [win_t[WT_FOX:q0], jnp.zeros((8, D_MODEL), MM), _slot_order(win_t[q0:q0 + D_ATT], 0), win_t[q0 + D_ATT:]],
        axis=0)
    bcol = jnp.pad(b_forget.reshape(N_HEADS, 1), ((0, 8), (0, 0)))
    rel_bias_slot = rel_bias[:, np.array(SLOT_HEAD)]
    sinks_slot = sinks.reshape(N_HEADS)[np.array(SLOT_HEAD)]
    bucket = jnp.asarray(_swa_bucket_map())

    a, fqkv, sqkv, fft = _pre_attn(x, g1, win_t, wt_rest, tm, after=hooks.before_pre_attn())
    c_row = _forget_cumsum(fft, bcol)
    c_col = c_row[:N_HEADS].T
    c_row3 = c_row[:N_HEADS].reshape(N_HEADS, S // t, t)
    fox_o, fox_lse = _fox_fwd(fqkv, c_row3, tq=512, tk=t)
    biasm = _swa_bias(rel_bias_slot, bucket)
    swa_o, swa_lse = _swa_fwd(sqkv, biasm, sinks_slot, after=hooks.after_fox_fwd(fox_o))
    wout, w1, w2, wple, wg = hooks.after_attention(swa_o)
    wout_fox = wout[:D_ATT]
    wout_swa = _slot_order(wout[D_ATT:], 0)
    mix, h1, m = _post_attn(x, fox_o, swa_o, wout_fox, wout_swa, g2, g3, tm)
    u, y, h2 = _mlp_fwd(m, h1, w1, w2, g4, tm_mlp)
    dh2, dpe, dgl, dg5, loss = _ple_loss(h2, p, target, wg, wple, g5, tm)

    d_wple = _weight_grad(p, dpe, "grad_w_ple", tk=D_PLE, n_chunks=N_DEV)
    d_wg = _weight_grad(h2, dgl, "grad_w_ple_gate", tk=256)
    dh1, dy, du, dg4, dg3 = _mlp_bwd(dh2, y, h1, u, w1, w2, g4, g3, tm)
    d_w2 = _weight_grad(u, dy, "grad_w_ff2", tk=256, relu2=True)
    d_w1 = _weight_grad(m, du, "grad_w_ff1", tk=256, n_chunks=N_DEV)
    head = np.arange(D_ATT) // HEAD_DIM
    head_rows = jnp.asarray((head[None, :] == np.arange(N_HEADS)[:, None]).astype(np.float32))
    dmix, dcat, d_row, d_swa, dg2 = _attn_out_bwd(dh1, mix, fox_o, swa_o, wout_fox, wout_swa, g2, head_rows, tm)
    d_col = d_swa.T
    d_wout = _weight_grad_two(fox_o, swa_o, dmix, "grad_w_out", tk=256)
    d_wout = jnp.concatenate([d_wout[:D_ATT], _head_order(d_wout[D_ATT:], 0)], axis=0)
    d_wout = d_wout.reshape(N_DEV, D_MODEL // N_DEV, D_MODEL)
    early = dict(w_ff1=d_w1, w_ff2=d_w2.reshape(N_DEV, FF_CHUNK, D_MODEL), w_ple=d_wple,
                 w_ple_gate=d_wg.reshape(N_DEV, D_MODEL // N_DEV, D_MODEL), w_out=d_wout)

    dsq, dsk, dsv, d_rb_slot, d_sink_slot = _swa_bwd(sqkv, dcat, biasm, sinks_slot, bucket, swa_lse, d_col,
                                                     after=hooks.after_early_grads(early))
    lse_row3 = fox_lse.T.reshape(N_HEADS, S // t, t)
    d_row3 = d_row.reshape(N_HEADS, S // t, t)
    dq_fox, dk_fox, dv_fox, dc_col, dcq = _fox_bwd(fqkv, dcat, lse_row3, d_row3, c_col, tq=t, tk=512,
                                                  after=hooks.after_swa_bwd(dsq))
    dc_row = jnp.pad(dc_col.T + dcq.reshape(N_HEADS, S), ((0, 8), (0, 0)))
    dff_t, db, d_wff_t = _forget_bwd(dc_row, fft, bcol, a)
    dz = [dq_fox, dk_fox, dv_fox, dsq, dsk, dsv]
    d_wmain = _weight_grad_pieces(dz, a, "grad_w_in", tk=256)

    sq0 = 3 * D_ATT
    d_win = jnp.concatenate(
        [d_wmain[:sq0], d_wff_t[:N_HEADS].astype(MM), _head_order(d_wmain[sq0:sq0 + D_ATT], 0),
         d_wmain[sq0 + D_ATT:]], axis=0)
    d_win = d_win.reshape(N_DEV, D_IN // N_DEV, D_MODEL)
    grad_x, dg1 = _pre_attn_bwd(x, dh1, dz, dff_t, win_t, wt_rest, g1, tm, after=hooks.after_w_in_grad(d_win))
    big = dict(early, w_in=d_win)
    small = dict(b_forget=db[:N_HEADS].reshape(1, N_HEADS), rel_bias=d_rb_slot[:, np.array(HEAD_SLOT)],
                 swa_sinks=d_sink_slot[:, np.array(HEAD_SLOT)], g_attn_pre=dg1, g_attn_post=dg2, g_ff_pre=dg3,
                 g_ff_post=dg4, g_ple_post=dg5)
    return loss, grad_x, big, small


BIG = ("w_in", "w_out", "w_ff1", "w_ff2", "w_ple", "w_ple_gate")
SMALL_ROWS = ("g_attn_pre", "g_attn_post", "g_ff_pre", "g_ff_post", "g_ple_post")
WEIGHTS =("w_in", "b_forget", "w_out", "rel_bias", "swa_sinks", "g_attn_pre", "g_attn_post", "w_ff1", "w_ff2",
           "g_ff_pre", "g_ff_post", "w_ple", "w_ple_gate", "g_ple_post")


EARLY = ("w_ff1", "w_ff2", "w_ple", "w_ple_gate", "w_out")


class _Overlap:
    def __init__(self, later):
        self.later = later

    def before_pre_attn(self):
        self.gather_sems, self.later, token = _exchange_start("gather_rest_start", self.later, 4 * 5, _plan_gather_direct)
        return token

    def after_fox_fwd(self, fox_o):
        later = _exchange_wait("gather_rest_wait", self.later, self.gather_sems, fox_o, _plan_gather_direct)
        self.pass_sems, self.later, token = _exchange_start("gather_pass_on_start", later, 3 * 5, _plan_gather_pass_on)
        return token

    def after_attention(self, swa_o):
        wout_g, w1_g, w2_g, wple_g, wg_g = _exchange_wait("gather_pass_on_wait", self.later, self.pass_sems, swa_o,
                                                         _plan_gather_pass_on)
        return (wout_g.reshape(D_MODEL, D_MODEL), w1_g, w2_g.reshape(D_FF, D_MODEL),
                jnp.moveaxis(wple_g, 0, 1).reshape(D_PLE, D_MODEL), wg_g.reshape(D_MODEL, D_MODEL))

    def after_early_grads(self, grads):
        views = [grads[k].reshape((4, 2) + grads[k].shape[1:]) for k in EARLY]
        lands = [lax.empty((4,) + grads[k].shape[1:], MM) for k in EARLY]
        self.in_chip_sems, self.in_chip, token = _exchange_start("grads_in_chip_start", views + lands, len(EARLY),
                                                                 _plan_in_chip)
        return token

    def after_swa_bwd(self, dsq):
        arrays = _exchange_wait("grads_in_chip_wait", self.in_chip, self.in_chip_sems, dsq, _plan_in_chip)
        n = len(EARLY)
        sums = list(_chip_sums(arrays[:n], arrays[n:], "chip_sums_early"))
        lands = [lax.empty(s.shape, s.dtype) for s in sums]
        self.between_sems, self.between, token = _exchange_start("grads_between_chips_start", sums + lands, 3 * n,
                                                                 _plan_between_chips)
        return token

    def after_w_in_grad(self, d_win):
        self.late_in_chip_sems, self.late_in_chip, token = _exchange_start(
            "late_in_chip_start", [d_win.reshape((4, 2) + d_win.shape[1:]), lax.empty((4,) + d_win.shape[1:], MM)],
            1, _plan_in_chip)
        return token

    def finish(self, after):
        arrays = _exchange_wait("grads_between_chips_wait", self.between, self.between_sems, after,
                                _plan_between_chips)
        n = len(EARLY)
        self.sums = arrays[:n]
        return arrays[n:]


def _pack_small(t):
    rows = [t[k].reshape(1, D_MODEL) for k in SMALL_ROWS]
    misc = jnp.concatenate([t["b_forget"].reshape(-1), t["swa_sinks"].reshape(-1), t["rel_bias"].T.reshape(-1)])
    rows.append(jnp.pad(misc, (0, D_MODEL - misc.shape[0])).reshape(1, D_MODEL))
    rows.append(jnp.pad(t["loss"].reshape(-1), (0, D_MODEL - 1)).reshape(1, D_MODEL))
    rows.append(jnp.zeros((1, D_MODEL), F32))
    return jnp.concatenate(rows, axis=0).astype(F32)


def kernel(x, p, w_in, b_forget, w_out, rel_bias, swa_sinks, g_attn_pre, g_attn_post, w_ff1, w_ff2, g_ff_pre, g_ff_post, w_ple, w_ple_gate, g_ple_post, loss_target, m_w_in, m_b_forget, m_w_out, m_rel_bias, m_swa_sinks, m_g_attn_pre, m_g_attn_post, m_w_ff1, m_w_ff2, m_g_ff_pre, m_g_ff_post, m_w_ple, m_w_ple_gate, m_g_ple_post, v_w_in, v_b_forget, v_w_out, v_rel_bias, v_swa_sinks, v_g_attn_pre, v_g_attn_post, v_w_ff1, v_w_ff2, v_g_ff_pre, v_g_ff_post, v_w_ple, v_w_ple_gate, v_g_ple_post):
    w = dict(w_in=w_in, b_forget=b_forget, w_out=w_out, rel_bias=rel_bias, swa_sinks=swa_sinks,
             g_attn_pre=g_attn_pre, g_attn_post=g_attn_post, w_ff1=w_ff1, w_ff2=w_ff2, g_ff_pre=g_ff_pre,
             g_ff_post=g_ff_post, w_ple=w_ple, w_ple_gate=w_ple_gate, g_ple_post=g_ple_post)
    mom = dict(w_in=m_w_in, b_forget=m_b_forget, w_out=m_w_out, rel_bias=m_rel_bias, swa_sinks=m_swa_sinks,
               g_attn_pre=m_g_attn_pre, g_attn_post=m_g_attn_post, w_ff1=m_w_ff1, w_ff2=m_w_ff2,
               g_ff_pre=m_g_ff_pre, g_ff_post=m_g_ff_post, w_ple=m_w_ple, w_ple_gate=m_w_ple_gate,
               g_ple_post=m_g_ple_post)
    var = dict(w_in=v_w_in, b_forget=v_b_forget, w_out=v_w_out, rel_bias=v_rel_bias, swa_sinks=v_swa_sinks,
               g_attn_pre=v_g_attn_pre, g_attn_post=v_g_attn_post, w_ff1=v_w_ff1, w_ff2=v_w_ff2,
               g_ff_pre=v_g_ff_pre, g_ff_post=v_g_ff_post, w_ple=v_w_ple, w_ple_gate=v_w_ple_gate,
               g_ple_post=v_g_ple_post)

    turn = lambda t, k: t.T if k == "w_in" else t
    me = 4 * lax.axis_index("x") + 2 * lax.axis_index("y") + lax.axis_index("c")

    def stack(block):
        return lax.dynamic_update_slice_in_dim(lax.empty((N_DEV,) + block.shape, block.dtype), block[None], me, 0)

    stacks = [stack(turn(w[k][0], k).astype(MM)) for k in BIG]
    (win_g,), later = _all_gather_sequencer(stacks[:1], "all_gather_sequencer", 1), stacks[1:]
    hooks = _Overlap(later)
    loss, grad_x, big, small = _forward_backward(
        x[0], p[0, 0], loss_target[0], win_g, hooks, b_forget, rel_bias, swa_sinks,
        g_attn_pre, g_attn_post, g_ff_pre, g_ff_post, g_ple_post)
    out_g, out_d, out_m, out_v = {}, {}, {}, {}

    def update(k, part, own):
        if k == "w_in":
            there, back = (lambda t: jnp.transpose(t, (2, 0, 1))), (lambda t: jnp.transpose(t, (1, 2, 0)))
        else:
            there, back = (lambda t: t[0]), (lambda t: t[None])
        g, d, m_new, v_new = _adamw_chips(part, own, there(w[k]), there(mom[k]), there(var[k]), "adamw_" + k)
        out_g[k], out_d[k], out_m[k], out_v[k] = back(g), back(d), back(m_new), back(v_new)
        return d

    view, other = _exchange_wait("late_in_chip_wait", hooks.late_in_chip, hooks.late_in_chip_sems, grad_x,
                                 _plan_in_chip)
    (chip_sum,) = _chip_sums([view], [other], "chip_sum_w_in")
    small["loss"] = loss
    between_sems, between, token = _exchange_start(
        "late_between_chips_start", [chip_sum, lax.empty(chip_sum.shape, MM), stack(_pack_small(small))], 3 + 7,
        _plan_late_between)
    early_parts = hooks.finish(token)
    done = [update(k, part, own) for k, part, own in zip(EARLY, early_parts, hooks.sums)]
    chip_sum, part, small_all = _exchange_wait("late_between_chips_wait", between, between_sems, done,
                                               _plan_late_between)
    update("w_in", part, chip_sum)
    rep = {k: w[k] for k in w if k not in BIG}
    rep["loss"] = jnp.zeros((), F32)
    rep_m = {k: mom[k] for k in mom if k not in BIG}
    rep_m["loss"] = jnp.zeros((), F32)
    rep_v = {k: var[k] for k in var if k not in BIG}
    rep_v["loss"] = jnp.ones((), F32)
    g_s, d_s, m_s, v_s = _adamw_small(small_all, _pack_small(rep), _pack_small(rep_m), _pack_small(rep_v))
    for k in w:
        if k not in BIG:
            natural = (lambda t: t.T) if k == "rel_bias" else (lambda t: t)
            out_g[k], out_d[k], out_m[k], out_v[k] = natural(g_s[k]), natural(d_s[k]), natural(m_s[k]), natural(v_s[k])
    return (g_s["loss"].reshape(()), grad_x[None], *[out_g[k] for k in WEIGHTS], *[out_d[k] for k in WEIGHTS],
            *[out_m[k] for k in WEIGHTS], *[out_v[k] for k in WEIGHTS])
```

```python
import functools

import numpy as np
import jax
import jax.numpy as jnp
from jax import lax
from jax.experimental import pallas as pl
from jax.experimental.pallas import tpu as pltpu
from jax.experimental.pallas import tpu_sc as plsc

F32 = jnp.float32
MM = jnp.bfloat16

D_MODEL = 1024
HEAD_DIM = 64
N_HEADS = 8
D_ATT = N_HEADS * HEAD_DIM
D_KV = 128
D_FF = 4096
D_PLE = 256
D_IN = 3 * D_ATT + N_HEADS + D_ATT + 2 * D_KV
N_DEV = 8
FF_CHUNK = D_FF // N_DEV
WINDOW = 128
N_BUCKETS = 32
MAX_DISTANCE = 128
RMS_EPS = 1e-6
Q_SCALE = HEAD_DIM ** -0.5
NEG = -1e30

ADAM_LR = 0.001
ADAM_B1 = 0.9
ADAM_B2 = 0.999
ADAM_EPS = 1e-08
ADAM_WD = 0.01
ADAM_STEP = 10

SLOT_HEAD = (0, 4, 1, 5, 2, 6, 3, 7)
HEAD_SLOT = (0, 2, 4, 6, 1, 3, 5, 7)

VMEM_LIMIT = 60 * 1024 * 1024
MESH = pl.DeviceIdType.MESH

NT = (((1,), (1,)), ((), ()))
TN = (((0,), (0,)), ((), ()))


def _params(*semantics):
    return pltpu.CompilerParams(dimension_semantics=semantics, vmem_limit_bytes=VMEM_LIMIT)


def _resident():
    return pl.BlockSpec(memory_space=pltpu.VMEM)


def _rows(tm, width):
    return pl.BlockSpec((tm, width), lambda i: (i, 0))


def _const(shape):
    return pl.BlockSpec(shape, lambda i: (0,) * len(shape))


def _dot(a, b):
    return jnp.dot(a, b, preferred_element_type=F32)


def _dot_nt(a, b):
    return lax.dot_general(a, b, NT, preferred_element_type=F32)


def _dot_tn(a, b):
    return lax.dot_general(a, b, TN, preferred_element_type=F32)


def _rms(xf):
    r = lax.rsqrt(jnp.mean(xf * xf, axis=-1, keepdims=True) + RMS_EPS)
    return xf * r, r


def _rms_bwd(dout, n, r, g):
    dg = jnp.sum(dout * n, axis=0, keepdims=True)
    dn = dout * g
    dx = r * (dn - n * jnp.mean(dn * n, axis=-1, keepdims=True))
    return dx, dg


def _run_after(after, body, in_specs, operands):
    if after is None:
        return body, list(in_specs), tuple(operands)
    n = len(operands)
    return ((lambda *refs: body(*refs[:n], *refs[n + 1:])), list(in_specs) + [pl.BlockSpec(memory_space=pl.ANY)],
            tuple(operands) + (after,))


def _accumulate(ref, value, step):
    @pl.when(step == 0)
    def _():
        ref[...] = value

    @pl.when(step != 0)
    def _():
        ref[...] += value


def _t5_bucket(n):
    max_exact = N_BUCKETS // 2
    large = max_exact + (np.log(np.maximum(n, 1) / max_exact) / np.log(MAX_DISTANCE / max_exact)
                         * (N_BUCKETS - max_exact)).astype(np.int32)
    large = np.minimum(large, N_BUCKETS - 1)
    return np.where(n < max_exact, n, large).astype(np.int32)


def _swa_bucket_map():
    i = np.arange(WINDOW)[:, None]
    j = np.arange(2 * WINDOW)[None, :]
    dist = i + WINDOW - j
    ok = (dist >= 0) & (dist < WINDOW)
    return np.where(ok, _t5_bucket(np.clip(dist, 0, None)), -1).astype(np.int32)


WT_FOX = 3 * D_ATT
WT_SQ = 16
WT_SKV = WT_SQ + D_ATT
WT_REST = WT_SKV + 2 * D_KV


def _pre_attn(x, g1, win_t, wt_rest, tm, after=None):
    S = x.shape[0]

    def body(x_ref, g_ref, wf_ref, wr_ref, a_ref, fqkv_ref, sqkv_ref, fft_ref):
        n, _ = _rms(x_ref[...])
        a = (n * g_ref[...]).astype(MM)
        a_ref[...] = a
        fqkv_ref[:, :D_ATT] = (_dot_nt(a, wf_ref[0:D_ATT]) * Q_SCALE).astype(MM)
        fqkv_ref[:, D_ATT:] = _dot_nt(a, wf_ref[D_ATT:WT_FOX]).astype(MM)
        sqkv_ref[:, :D_ATT] = (_dot_nt(a, wr_ref[WT_SQ:WT_SKV]) * Q_SCALE).astype(MM)
        sqkv_ref[:, D_ATT:] = _dot_nt(a, wr_ref[WT_SKV:WT_REST]).astype(MM)
        fft_ref[...] = _dot_nt(wr_ref[0:WT_SQ], a)

    body, in_specs, operands = _run_after(
        after, body, [_rows(tm, D_MODEL), _const((1, D_MODEL)), _const((WT_FOX, D_MODEL)), _resident()],
        (x, g1, win_t, wt_rest))
    return pl.pallas_call(
        body, name="pre_attn", grid=(S // tm,), in_specs=in_specs,
        out_specs=[_rows(tm, D_MODEL), _rows(tm, 3 * D_ATT), _rows(tm, D_ATT + 2 * D_KV),
                   pl.BlockSpec((16, tm), lambda i: (0, i))],
        out_shape=[jax.ShapeDtypeStruct((S, D_MODEL), MM), jax.ShapeDtypeStruct((S, 3 * D_ATT), MM),
                   jax.ShapeDtypeStruct((S, D_ATT + 2 * D_KV), MM), jax.ShapeDtypeStruct((16, S), F32)],
        compiler_params=_params("parallel"),
    )(*operands)


def _lane_scan(v, reverse):
    S = v.shape[1]
    lane = lax.broadcasted_iota(jnp.int32, v.shape, 1)
    k = 1
    while k < S:
        if reverse:
            v = v + jnp.where(lane < S - k, pltpu.roll(v, S - k, axis=1), 0.0)
        else:
            v = v + jnp.where(lane >= k, pltpu.roll(v, k, axis=1), 0.0)
        k *= 2
    return v


def _forget_cumsum(fft, bcol):
    def body(f_ref, b_ref, c_ref):
        z = f_ref[...] + b_ref[...]
        log_f = jnp.minimum(z, 0.0) - jnp.log1p(jnp.exp(-jnp.abs(z)))
        c_ref[...] = _lane_scan(log_f, reverse=False)

    return pl.pallas_call(
        body, name="forget_cumsum", out_shape=jax.ShapeDtypeStruct(fft.shape, F32),
        in_specs=[_resident(), _resident()], out_specs=_resident(),
    )(fft, bcol)


def _forget_bwd(dc_row, fft, bcol, a):
    def body(dc_ref, f_ref, b_ref, a_ref, dff_ref, db_ref, dw_ref):
        z = f_ref[...] + b_ref[...]
        dlog_f = _lane_scan(dc_ref[...], reverse=True)
        dff = dlog_f * (1.0 / (1.0 + jnp.exp(z)))
        dff_ref[...] = dff
        db_ref[...] = jnp.sum(dff, axis=1, keepdims=True)
        dw_ref[...] = _dot(dff.astype(MM), a_ref[...])

    return pl.pallas_call(
        body, name="forget_bwd",
        out_shape=[jax.ShapeDtypeStruct(fft.shape, F32), jax.ShapeDtypeStruct((fft.shape[0], 1), F32),
                   jax.ShapeDtypeStruct((fft.shape[0], D_MODEL), F32)],
        in_specs=[_resident()] * 4, out_specs=[_resident()] * 3,
    )(dc_row, fft, bcol, a)


def _head_select(shape, upper):
    lane = lax.broadcasted_iota(jnp.int32, shape, 1)
    return lane >= HEAD_DIM if upper else lane < HEAD_DIM


def _fox_fwd(fqkv, c_row3, tq, tk, pairs_per_loop=2, row_chunks=1):
    S = fqkv.shape[0]
    rq = tq // row_chunks
    n_band = tq // tk

    def body(q_ref, k_ref, v_ref, ck_ref, o_ref, lse_ref):
        qi = pl.program_id(0)
        row = lax.broadcasted_iota(jnp.int32, (rq, tk), 0)
        col = lax.broadcasted_iota(jnp.int32, (rq, tk), 1)
        low = _head_select((rq, 128), 0)
        for first in range(0, N_HEADS // 2, pairs_per_loop):
            pairs = range(first, first + pairs_per_loop)
            chains = [(pr, hh, rc) for pr in pairs for hh in range(2) for rc in range(row_chunks)]
            qh = {}
            for pr in pairs:
                for rc in range(row_chunks):
                    q2 = q_ref[rc * rq:(rc + 1) * rq, pr * 128:(pr + 1) * 128]
                    qh[pr, 0, rc] = jnp.where(low, q2, jnp.zeros_like(q2))
                    qh[pr, 1, rc] = jnp.where(low, jnp.zeros_like(q2), q2)

            def block(kb, carry, band, chains=chains, qh=qh):
                rows = pl.ds(pl.multiple_of(kb * tk, tk), tk)
                out = []
                for (pr, hh, rc), (m, l, acc) in zip(chains, carry):
                    if band is not None and (rc + 1) * rq <= band * tk:
                        out.append((m, l, acc))
                        continue
                    lanes = slice(pr * 128, (pr + 1) * 128)
                    s = _dot_nt(qh[pr, hh, rc], k_ref[rows, lanes]) - ck_ref[2 * pr + hh, pl.ds(kb, 1), :]
                    if band is not None:
                        s = jnp.where(row + rc * rq >= col + band * tk, s, NEG)
                    m_new = jnp.maximum(m, jnp.max(s, axis=-1, keepdims=True))
                    p = jnp.exp(s - m_new)
                    alpha = jnp.exp(m - m_new)
                    l = alpha * l + jnp.sum(p, axis=-1, keepdims=True)
                    acc = alpha * acc + _dot(p.astype(MM), v_ref[rows, lanes])
                    out.append((m_new, l, acc))
                return tuple(out)

            carry = tuple((jnp.full((rq, 1), NEG, F32), jnp.zeros((rq, 1), F32), jnp.zeros((rq, 128), F32))
                          for _ in chains)
            carry = lax.fori_loop(0, qi * n_band, functools.partial(block, band=None), carry)
            for band in range(n_band):
                carry = block(qi * n_band + band, carry, band=band)
            res = {}
            for (pr, hh, rc), (m, l, acc) in zip(chains, carry):
                res[pr, hh, rc] = acc / l
                lse_ref[rc * rq:(rc + 1) * rq, 2 * pr + hh:2 * pr + hh + 1] = m + jnp.log(l)
            for pr in pairs:
                for rc in range(row_chunks):
                    o_ref[rc * rq:(rc + 1) * rq, pr * 128:(pr + 1) * 128] = jnp.where(
                        low, res[pr, 0, rc], res[pr, 1, rc]).astype(MM)

    return pl.pallas_call(
        body, name="fox_fwd", grid=(S // tq,),
        in_specs=[pl.BlockSpec((tq, D_ATT), lambda i: (i, 0)), pl.BlockSpec((S, D_ATT), lambda i: (0, 1)),
                  pl.BlockSpec((S, D_ATT), lambda i: (0, 2)), _resident()],
        out_specs=[_rows(tq, D_ATT), _rows(tq, N_HEADS)],
        out_shape=[jax.ShapeDtypeStruct((S, D_ATT), MM), jax.ShapeDtypeStruct((S, N_HEADS), F32)],
        compiler_params=_params("parallel"),
    )(fqkv, fqkv, fqkv, c_row3)


def _swa_bias(rel_bias_slot, bucket):
    def body(rb_ref, bk_ref, out_ref):
        bk = bk_ref[...]
        for s in range(N_HEADS):
            acc = jnp.where(bk < 0, NEG, 0.0).astype(F32)
            for b in range(N_BUCKETS):
                acc = jnp.where(bk == b, rb_ref[b, s], acc)
            out_ref[s] = acc

    return pl.pallas_call(
        body, name="swa_bias", out_shape=jax.ShapeDtypeStruct((N_HEADS, WINDOW, 2 * WINDOW), F32),
        in_specs=[pl.BlockSpec(memory_space=pltpu.SMEM), _resident()], out_specs=_resident(),
    )(rel_bias_slot, bucket)


def _stack4(piece):
    return jnp.concatenate([piece(j) for j in range(4)], axis=0)


def _swa_specs(S):
    q = pl.BlockSpec((WINDOW, D_ATT), lambda n: (n, 0))
    kp = pl.BlockSpec((WINDOW, D_KV), lambda n: (jnp.maximum(n - 1, 0), 4))
    kc = pl.BlockSpec((WINDOW, D_KV), lambda n: (n, 4))
    vp = pl.BlockSpec((WINDOW, D_KV), lambda n: (jnp.maximum(n - 1, 0), 5))
    vc = pl.BlockSpec((WINDOW, D_KV), lambda n: (n, 5))
    return [q, kp, kc, vp, vc]


def _swa_fwd(sqkv, biasm, sinks_slot, after=None):
    S = sqkv.shape[0]

    def body(q_ref, kp_ref, kc_ref, vp_ref, vc_ref, bias_ref, sink_ref, o_ref, lse_ref):
        n = pl.program_id(0)
        no_prev = jnp.where(n > 0, 0.0, NEG)
        low = _head_select((WINDOW, 128), 0)
        res = []
        for g in range(2):
            sel = low if g == 0 else jnp.logical_not(low)
            qg = _stack4(lambda j: jnp.where(sel, q_ref[:, j * 128:(j + 1) * 128], jnp.zeros((WINDOW, 128), MM)))
            sink = _stack4(lambda j: jnp.full((WINDOW, 1), sink_ref[2 * j + g], F32))
            sp = _dot_nt(qg, kp_ref[...]) + _stack4(lambda j: bias_ref[2 * j + g, :, :WINDOW]) + no_prev
            sc = _dot_nt(qg, kc_ref[...]) + _stack4(lambda j: bias_ref[2 * j + g, :, WINDOW:])
            m = jnp.maximum(jnp.maximum(jnp.max(sp, axis=-1, keepdims=True),
                                        jnp.max(sc, axis=-1, keepdims=True)), sink)
            ep = jnp.exp(sp - m)
            ec = jnp.exp(sc - m)
            den = jnp.sum(ep, axis=-1, keepdims=True) + jnp.sum(ec, axis=-1, keepdims=True) + jnp.exp(sink - m)
            res.append((_dot(ep.astype(MM), vp_ref[...]) + _dot(ec.astype(MM), vc_ref[...])) / den)
            lse = m + jnp.log(den)
            for j in range(4):
                lse_ref[:, 2 * j + g:2 * j + g + 1] = lse[j * WINDOW:(j + 1) * WINDOW]
        for j in range(4):
            rows = slice(j * WINDOW, (j + 1) * WINDOW)
            o_ref[:, j * 128:(j + 1) * 128] = jnp.where(low, res[0][rows], res[1][rows]).astype(MM)

    body, in_specs, operands = _run_after(
        after, body, _swa_specs(S) + [_resident(), pl.BlockSpec(memory_space=pltpu.SMEM)],
        (sqkv, sqkv, sqkv, sqkv, sqkv, biasm, sinks_slot))
    return pl.pallas_call(
        body, name="swa_fwd", grid=(S // WINDOW,), in_specs=in_specs,
        out_specs=[_rows(WINDOW, D_ATT), _rows(WINDOW, N_HEADS)],
        out_shape=[jax.ShapeDtypeStruct((S, D_ATT), MM), jax.ShapeDtypeStruct((S, N_HEADS), F32)],
        compiler_params=_params("parallel"),
    )(*operands)


def _post_attn(x, fox_o, swa_o, wout_fox, wout_swa, g2, g3, tm):
    S = x.shape[0]

    def body(x_ref, fo_ref, so_ref, wf_ref, ws_ref, g2_ref, g3_ref, mix_ref, h1_ref, m_ref):
        mix = _dot(fo_ref[...], wf_ref[...]) + _dot(so_ref[...], ws_ref[...])
        mix_ref[...] = mix
        n2, _ = _rms(mix)
        h1 = x_ref[...] + n2 * g2_ref[...]
        h1_ref[...] = h1
        n3, _ = _rms(h1)
        m_ref[...] = (n3 * g3_ref[...]).astype(MM)

    return pl.pallas_call(
        body, name="post_attn", grid=(S // tm,),
        in_specs=[_rows(tm, D_MODEL), _rows(tm, D_ATT), _rows(tm, D_ATT), _resident(), _resident(),
                  _const((1, D_MODEL)), _const((1, D_MODEL))],
        out_specs=[_rows(tm, D_MODEL)] * 3,
        out_shape=[jax.ShapeDtypeStruct((S, D_MODEL), F32), jax.ShapeDtypeStruct((S, D_MODEL), F32),
                   jax.ShapeDtypeStruct((S, D_MODEL), MM)],
        compiler_params=_params("parallel"),
    )(x, fox_o, swa_o, wout_fox, wout_swa, g2, g3)


def _mlp_fwd(m, h1, w1, w2, g4, tm):
    S = m.shape[0]

    def body(m_ref, h1_ref, w1_ref, w2_ref, g4_ref, u_ref, y_ref, h2_ref):
        mb = m_ref[...]
        y = jnp.zeros((tm, D_MODEL), F32)
        for j in range(N_DEV):
            cols = slice(j * FF_CHUNK, (j + 1) * FF_CHUNK)
            u = _dot(mb, w1_ref[j])
            u_ref[:, cols] = u.astype(MM)
            y = y + _dot(jnp.square(jnp.maximum(u, 0.0)).astype(MM), w2_ref[cols, :])
        y_ref[...] = y
        n4, _ = _rms(y)
        h2_ref[...] = h1_ref[...] + n4 * g4_ref[...]

    return pl.pallas_call(
        body, name="mlp_fwd", grid=(S // tm,),
        in_specs=[_rows(tm, D_MODEL), _rows(tm, D_MODEL), _resident(), _resident(), _const((1, D_MODEL))],
        out_specs=[_rows(tm, D_FF), _rows(tm, D_MODEL), _rows(tm, D_MODEL)],
        out_shape=[jax.ShapeDtypeStruct((S, D_FF), MM), jax.ShapeDtypeStruct((S, D_MODEL), F32),
                   jax.ShapeDtypeStruct((S, D_MODEL), F32)],
        compiler_params=_params("parallel"),
    )(m, h1, w1, w2, g4)


def _ple_loss(h2, p, target, wg, wple, g5, tm):
    S = h2.shape[0]

    def body(h2_ref, p_ref, t_ref, wg_ref, wp_ref, g5_ref, dh2_ref, dpe_ref, dgl_ref, dg5_ref, loss_ref):
        i = pl.program_id(0)
        h2 = h2_ref[...]
        gate = jax.nn.sigmoid(_dot(h2.astype(MM), wg_ref[...]))
        pe = _dot(p_ref[...].astype(MM), wp_ref[...])
        n5, r5 = _rms(pe * gate)
        g5 = g5_ref[...]
        diff = h2 + n5 * g5 - t_ref[...]
        per_token = jnp.mean(jnp.square(diff), axis=-1, keepdims=True)
        _accumulate(loss_ref, 0.5 * jnp.sum(per_token, axis=0, keepdims=True), i)
        dh3 = diff * (1.0 / D_MODEL)
        de, dg5 = _rms_bwd(dh3, n5, r5, g5)
        _accumulate(dg5_ref, dg5, i)
        dpe_ref[...] = (de * gate).astype(MM)
        dgl = (de * pe * gate * (1.0 - gate)).astype(MM)
        dgl_ref[...] = dgl
        dh2_ref[...] = dh3 + _dot_nt(dgl, wg_ref[...])

    return pl.pallas_call(
        body, name="ple_loss", grid=(S // tm,),
        in_specs=[_rows(tm, D_MODEL), _rows(tm, D_PLE), _rows(tm, D_MODEL), _resident(), _resident(),
                  _const((1, D_MODEL))],
        out_specs=[_rows(tm, D_MODEL), _rows(tm, D_MODEL), _rows(tm, D_MODEL), _const((1, D_MODEL)), _const((1, 1))],
        out_shape=[jax.ShapeDtypeStruct((S, D_MODEL), F32), jax.ShapeDtypeStruct((S, D_MODEL), MM),
                   jax.ShapeDtypeStruct((S, D_MODEL), MM), jax.ShapeDtypeStruct((1, D_MODEL), F32),
                   jax.ShapeDtypeStruct((1, 1), F32)],
        compiler_params=_params("arbitrary"),
    )(h2, p, target, wg, wple, g5)


def _mlp_bwd(dh2, y, h1, u, w1, w2, g4, g3, tm):
    S = dh2.shape[0]

    def body(dh2_ref, y_ref, h1_ref, u_ref, w1_ref, w2_ref, g4_ref, g3_ref,
             dh1_ref, dy_ref, du_ref, dg4_ref, dg3_ref):
        i = pl.program_id(0)
        dh2 = dh2_ref[...]
        n4, r4 = _rms(y_ref[...])
        dy, dg4 = _rms_bwd(dh2, n4, r4, g4_ref[...])
        _accumulate(dg4_ref, dg4, i)
        dyb = dy.astype(MM)
        dy_ref[...] = dyb
        dm = jnp.zeros((tm, D_MODEL), F32)
        for j in range(N_DEV):
            cols = slice(j * FF_CHUNK, (j + 1) * FF_CHUNK)
            dact = _dot_nt(dyb, w2_ref[cols, :])
            du = (dact * (2.0 * jnp.maximum(u_ref[:, cols].astype(F32), 0.0))).astype(MM)
            du_ref[:, cols] = du
            dm = dm + _dot_nt(du, w1_ref[j])
        n3, r3 = _rms(h1_ref[...])
        dx, dg3 = _rms_bwd(dm, n3, r3, g3_ref[...])
        _accumulate(dg3_ref, dg3, i)
        dh1_ref[...] = dh2 + dx

    return pl.pallas_call(
        body, name="mlp_bwd", grid=(S // tm,),
        in_specs=[_rows(tm, D_MODEL), _rows(tm, D_MODEL), _rows(tm, D_MODEL), _rows(tm, D_FF),
                  _resident(), _resident(), _const((1, D_MODEL)), _const((1, D_MODEL))],
        out_specs=[_rows(tm, D_MODEL), _rows(tm, D_MODEL), _rows(tm, D_FF), _const((1, D_MODEL)),
                   _const((1, D_MODEL))],
        out_shape=[jax.ShapeDtypeStruct((S, D_MODEL), F32), jax.ShapeDtypeStruct((S, D_MODEL), MM),
                   jax.ShapeDtypeStruct((S, D_FF), MM), jax.ShapeDtypeStruct((1, D_MODEL), F32),
                   jax.ShapeDtypeStruct((1, D_MODEL), F32)],
        compiler_params=_params("arbitrary"),
    )(dh2, y, h1, u, w1, w2, g4, g3)


def _attn_out_bwd(dh1, mix, fox_o, swa_o, wout_fox, wout_swa, g2, head_rows, tm):
    S = dh1.shape[0]

    def body(dh1_ref, mix_ref, fo_ref, so_ref, wf_ref, ws_ref, g2_ref, er_ref,
             dmix_ref, dcat_ref, drow_ref, dswa_ref, dg2_ref):
        i = pl.program_id(0)
        n2, r2 = _rms(mix_ref[...])
        dmix, dg2 = _rms_bwd(dh1_ref[...], n2, r2, g2_ref[...])
        _accumulate(dg2_ref, dg2, i)
        dmb = dmix.astype(MM)
        dmix_ref[...] = dmb
        dfo = _dot_nt(dmb, wf_ref[...]).astype(MM)
        dso = _dot_nt(dmb, ws_ref[...]).astype(MM)
        dcat_ref[:, :D_ATT] = dfo
        dcat_ref[:, D_ATT:] = dso
        hi = lax.Precision.HIGHEST
        prod_f = dfo.astype(F32) * fo_ref[...].astype(F32)
        prod_s = dso.astype(F32) * so_ref[...].astype(F32)
        drow_ref[...] = lax.dot_general(er_ref[...], prod_f, NT, precision=hi, preferred_element_type=F32)
        dswa_ref[...] = lax.dot_general(er_ref[...], prod_s, NT, precision=hi, preferred_element_type=F32)

    return pl.pallas_call(
        body, name="attn_out_bwd", grid=(S // tm,),
        in_specs=[_rows(tm, D_MODEL), _rows(tm, D_MODEL), _rows(tm, D_ATT), _rows(tm, D_ATT), _resident(),
                  _resident(), _const((1, D_MODEL)), _resident()],
        out_specs=[_rows(tm, D_MODEL), _rows(tm, D_MODEL), pl.BlockSpec((N_HEADS, tm), lambda i: (0, i)),
                   pl.BlockSpec((N_HEADS, tm), lambda i: (0, i)), _const((1, D_MODEL))],
        out_shape=[jax.ShapeDtypeStruct((S, D_MODEL), MM), jax.ShapeDtypeStruct((S, D_MODEL), MM),
                   jax.ShapeDtypeStruct((N_HEADS, S), F32), jax.ShapeDtypeStruct((N_HEADS, S), F32),
                   jax.ShapeDtypeStruct((1, D_MODEL), F32)],
        compiler_params=_params("arbitrary"),
    )(dh1, mix, fox_o, swa_o, wout_fox, wout_swa, g2, head_rows)


def _fox_bwd(fqkv, dcat, lse_row3, d_row3, c_col, tq, tk, pairs_per_loop=2, after=None):
    S = fqkv.shape[0]
    n_blk = S // tk
    n_qblk = S // tq
    n_band = tk // tq

    def body(q_ref, k_ref, v_ref, do_ref, lse_ref, dd_ref, ck_ref, dq_ref, dk_ref, dv_ref, dc_ref, dcq_ref):
        kb = pl.program_id(0)

        @pl.when(kb == 0)
        def _():
            dq_ref[...] = jnp.zeros_like(dq_ref)
            dcq_ref[...] = jnp.zeros_like(dcq_ref)

        key = lax.broadcasted_iota(jnp.int32, (tk, tq), 0)
        qry = lax.broadcasted_iota(jnp.int32, (tk, tq), 1)
        low = _head_select((tk, 128), 0)
        for first in range(0, N_HEADS // 2, pairs_per_loop):
            pairs = range(first, first + pairs_per_loop)
            heads = [(pr, hh) for pr in pairs for hh in range(2)]
            kh, vh, ck = {}, {}, {}
            for pr in pairs:
                k2 = k_ref[:, pr * 128:(pr + 1) * 128]
                v2 = v_ref[:, pr * 128:(pr + 1) * 128]
                zero = jnp.zeros_like(k2)
                kh[pr, 0], kh[pr, 1] = jnp.where(low, k2, zero), jnp.where(low, zero, k2)
                vh[pr, 0], vh[pr, 1] = jnp.where(low, v2, zero), jnp.where(low, zero, v2)
                for hh in range(2):
                    ck[pr, hh] = ck_ref[:, 2 * pr + hh:2 * pr + hh + 1]

            def block(qb, carry, band, pairs=pairs, kh=kh, vh=vh, ck=ck):
                rows = pl.ds(pl.multiple_of(qb * tq, tq), tq)
                k1 = tk if band is None else (band + 1) * tq
                out = []
                it = iter(carry)
                for pr in pairs:
                    lanes = slice(pr * 128, (pr + 1) * 128)
                    q2 = q_ref[rows, lanes]
                    do2 = do_ref[rows, lanes]
                    dq = None
                    for hh in range(2):
                        h = 2 * pr + hh
                        dk, dv, dc = next(it)
                        s_t = _dot_nt(kh[pr, hh][:k1], q2) - ck[pr, hh][:k1]
                        p_t = jnp.exp(s_t - lse_ref[h, pl.ds(qb, 1), :])
                        if band is not None:
                            p_t = jnp.where(qry[:k1] + band * tq >= key[:k1], p_t, 0.0)
                        ds_t = p_t * (_dot_nt(vh[pr, hh][:k1], do2) - dd_ref[h, pl.ds(qb, 1), :])
                        dsb = ds_t.astype(MM)
                        dv_new = dv[:k1] + _dot(p_t.astype(MM), do2)
                        dk_new = dk[:k1] + _dot(dsb, q2)
                        dc_new = dc[:k1] - jnp.sum(ds_t, axis=1, keepdims=True)
                        if k1 < tk:
                            dv_new = jnp.concatenate([dv_new, dv[k1:]], axis=0)
                            dk_new = jnp.concatenate([dk_new, dk[k1:]], axis=0)
                            dc_new = jnp.concatenate([dc_new, dc[k1:]], axis=0)
                        part = _dot_tn(dsb, kh[pr, hh][:k1])
                        dq = part if dq is None else dq + part
                        dcq_ref[h, pl.ds(qb, 1), :] += jnp.sum(ds_t, axis=0, keepdims=True)
                        out.append((dk_new, dv_new, dc_new))
                    dq_ref[rows, lanes] += dq
                return tuple(out)

            carry = tuple((jnp.zeros((tk, 128), F32), jnp.zeros((tk, 128), F32), jnp.zeros((tk, 1), F32))
                          for _ in heads)
            for band in range(n_band):
                carry = block(kb * n_band + band, carry, band=band)
            carry = lax.fori_loop((kb + 1) * n_band, n_qblk, functools.partial(block, band=None), carry)
            grads = dict(zip(heads, carry))
            for pr in pairs:
                lanes = slice(pr * 128, (pr + 1) * 128)
                dk_ref[:, lanes] = jnp.where(low, grads[pr, 0][0], grads[pr, 1][0]).astype(MM)
                dv_ref[:, lanes] = jnp.where(low, grads[pr, 0][1], grads[pr, 1][1]).astype(MM)
                for hh in range(2):
                    dc_ref[:, 2 * pr + hh:2 * pr + hh + 1] = grads[pr, hh][2]

        @pl.when(kb == n_blk - 1)
        def _():
            dq_ref[...] = dq_ref[...] * Q_SCALE

    body, in_specs, operands = _run_after(
        after, body,
        [pl.BlockSpec((S, D_ATT), lambda i: (0, 0)), pl.BlockSpec((tk, D_ATT), lambda i: (i, 1)),
         pl.BlockSpec((tk, D_ATT), lambda i: (i, 2)), pl.BlockSpec((S, D_ATT), lambda i: (0, 0)),
         _resident(), _resident(), _rows(tk, N_HEADS)],
        (fqkv, fqkv, fqkv, dcat, lse_row3, d_row3, c_col))
    return pl.pallas_call(
        body, name="fox_bwd", grid=(n_blk,), in_specs=in_specs,
        out_specs=[_const((S, D_ATT)), _rows(tk, D_ATT), _rows(tk, D_ATT), _rows(tk, N_HEADS),
                   _const((N_HEADS, n_qblk, tq))],
        out_shape=[jax.ShapeDtypeStruct((S, D_ATT), F32), jax.ShapeDtypeStruct((S, D_ATT), MM),
                   jax.ShapeDtypeStruct((S, D_ATT), MM), jax.ShapeDtypeStruct((S, N_HEADS), F32),
                   jax.ShapeDtypeStruct((N_HEADS, n_qblk, tq), F32)],
        compiler_params=_params("arbitrary"),
    )(*operands)


def _swa_bwd(sqkv, dcat, biasm, sinks_slot, bucket, lse, d_col, after=None):
    S = sqkv.shape[0]
    n_blk = S // WINDOW

    def body(q_ref, kp_ref, kc_ref, vp_ref, vc_ref, do_ref, bias_ref, sink_ref, bk_ref, lse_ref, dd_ref,
             dq_ref, dk_ref, dv_ref, drb_ref, dsink_ref, ds_acc):
        n = pl.program_id(0)

        @pl.when(n == 0)
        def _():
            dk_ref[...] = jnp.zeros_like(dk_ref)
            dv_ref[...] = jnp.zeros_like(dv_ref)
            ds_acc[...] = jnp.zeros_like(ds_acc)
            dsink_ref[...] = jnp.zeros_like(dsink_ref)

        no_prev = jnp.where(n > 0, 0.0, NEG)
        prev = pl.ds(pl.multiple_of(jnp.maximum(n - 1, 0) * WINDOW, WINDOW), WINDOW)
        cur = pl.ds(pl.multiple_of(n * WINDOW, WINDOW), WINDOW)
        lane8 = lax.broadcasted_iota(jnp.int32, (1, N_HEADS), 1)
        dkp = jnp.zeros((WINDOW, D_KV), F32)
        dkc = jnp.zeros((WINDOW, D_KV), F32)
        dvp = jnp.zeros((WINDOW, D_KV), F32)
        dvc = jnp.zeros((WINDOW, D_KV), F32)
        dsink = jnp.zeros((1, N_HEADS), F32)
        low = _head_select((WINDOW, 128), 0)
        zero = jnp.zeros((WINDOW, 128), MM)
        dqs = []
        for g in range(2):
            sel = low if g == 0 else jnp.logical_not(low)
            qg = _stack4(lambda j: jnp.where(sel, q_ref[:, j * 128:(j + 1) * 128], zero))
            dog = _stack4(lambda j: jnp.where(sel, do_ref[:, j * 128:(j + 1) * 128], zero))
            lse_g = _stack4(lambda j: lse_ref[:, 2 * j + g:2 * j + g + 1])
            dd = _stack4(lambda j: dd_ref[:, 2 * j + g:2 * j + g + 1])
            sink = _stack4(lambda j: jnp.full((WINDOW, 1), sink_ref[2 * j + g], F32))
            pp = jnp.exp(_dot_nt(qg, kp_ref[...]) + _stack4(lambda j: bias_ref[2 * j + g, :, :WINDOW]) + no_prev - lse_g)
            pc = jnp.exp(_dot_nt(qg, kc_ref[...]) + _stack4(lambda j: bias_ref[2 * j + g, :, WINDOW:]) - lse_g)
            sink_term = jnp.exp(sink - lse_g) * dd
            dsp = pp * (_dot_nt(dog, vp_ref[...]) - dd)
            dsc = pc * (_dot_nt(dog, vc_ref[...]) - dd)
            for j in range(4):
                rows = slice(j * WINDOW, (j + 1) * WINDOW)
                dsink = dsink + jnp.where(lane8 == 2 * j + g, -jnp.sum(sink_term[rows]), 0.0)
                ds_acc[2 * j + g, :, :WINDOW] += dsp[rows]
                ds_acc[2 * j + g, :, WINDOW:] += dsc[rows]
            dspb, dscb = dsp.astype(MM), dsc.astype(MM)
            dqs.append(_dot(dspb, kp_ref[...]) + _dot(dscb, kc_ref[...]))
            dkp = dkp + _dot_tn(dspb, qg)
            dkc = dkc + _dot_tn(dscb, qg)
            dvp = dvp + _dot_tn(pp.astype(MM), dog)
            dvc = dvc + _dot_tn(pc.astype(MM), dog)
        for j in range(4):
            rows = slice(j * WINDOW, (j + 1) * WINDOW)
            dq_ref[:, j * 128:(j + 1) * 128] = (jnp.where(low, dqs[0][rows], dqs[1][rows]) * Q_SCALE).astype(MM)
        dk_ref[prev, :] += dkp
        dk_ref[cur, :] += dkc
        dv_ref[prev, :] += dvp
        dv_ref[cur, :] += dvc
        dsink_ref[...] += dsink

        @pl.when(n == n_blk - 1)
        def _():
            bk = bk_ref[...]
            rb = lax.broadcasted_iota(jnp.int32, (N_BUCKETS, N_HEADS), 0)
            cb = lax.broadcasted_iota(jnp.int32, (N_BUCKETS, N_HEADS), 1)
            out = jnp.zeros((N_BUCKETS, N_HEADS), F32)
            for s in range(N_HEADS):
                acc = ds_acc[s]
                for b in range(N_BUCKETS):
                    out = out + jnp.where((rb == b) & (cb == s), jnp.sum(jnp.where(bk == b, acc, 0.0)), 0.0)
            drb_ref[...] = out

    do_spec = pl.BlockSpec((WINDOW, D_ATT), lambda n: (n, 1))
    body, in_specs, operands = _run_after(
        after, body, _swa_specs(S) + [do_spec, _resident(), pl.BlockSpec(memory_space=pltpu.SMEM), _resident(),
                                      _rows(WINDOW, N_HEADS), _rows(WINDOW, N_HEADS)],
        (sqkv, sqkv, sqkv, sqkv, sqkv, dcat, biasm, sinks_slot, bucket, lse, d_col))
    return pl.pallas_call(
        body, name="swa_bwd", grid=(n_blk,), in_specs=in_specs,
        out_specs=[_rows(WINDOW, D_ATT), _const((S, D_KV)), _const((S, D_KV)), _const((N_BUCKETS, N_HEADS)),
                   _const((1, N_HEADS))],
        out_shape=[jax.ShapeDtypeStruct((S, D_ATT), MM), jax.ShapeDtypeStruct((S, D_KV), F32),
                   jax.ShapeDtypeStruct((S, D_KV), F32), jax.ShapeDtypeStruct((N_BUCKETS, N_HEADS), F32),
                   jax.ShapeDtypeStruct((1, N_HEADS), F32)],
        scratch_shapes=[pltpu.VMEM((N_HEADS, WINDOW, 2 * WINDOW), F32)],
        compiler_params=_params("arbitrary"),
    )(*operands)


def _weight_grad_pieces(pieces, b, name, tk):
    S, N = b.shape
    wide = [p for p in pieces if p.shape[1] % tk == 0]
    narrow = pieces[len(wide):]
    assert sum(p.shape[1] for p in narrow) == tk
    counts = [p.shape[1] // tk for p in wide]
    firsts = [sum(counts[:j]) for j in range(len(wide))]
    last = sum(counts)

    def body(*refs):
        b_ref, out_ref = refs[len(pieces)], refs[len(pieces) + 1]
        i = pl.program_id(0)

        def whole_steps(ref, first, count):
            @pl.when((i >= first) & (i < first + count))
            def _():
                out_ref[...] = _dot_tn(ref[...].astype(MM), b_ref[...]).astype(MM)

        for ref, first, count in zip(refs[:len(wide)], firsts, counts):
            whole_steps(ref, first, count)

        @pl.when(i == last)
        def _():
            row = 0
            for ref in refs[len(wide):len(pieces)]:
                k = ref.shape[1]
                out_ref[row:row + k] = _dot_tn(ref[...].astype(MM), b_ref[...]).astype(MM)
                row += k

    def steps_of(first, count):
        return pl.BlockSpec((S, tk), lambda i: (0, jnp.clip(i - first, 0, count - 1)))

    return pl.pallas_call(
        body, name=name, grid=(last + 1,),
        in_specs=[steps_of(first, count) for first, count in zip(firsts, counts)]
        + [pl.BlockSpec((S, p.shape[1]), lambda i: (0, 0)) for p in narrow] + [_resident()],
        out_specs=pl.BlockSpec((tk, N), lambda i: (i, 0)), out_shape=jax.ShapeDtypeStruct(((last + 1) * tk, N), MM),
        compiler_params=_params("parallel"),
    )(*pieces, b)


def _pre_attn_bwd(x, dh1, dz, dff_t, win_t, wt_rest, g1, tm, after=None):
    S = x.shape[0]

    def body(x_ref, dh1_ref, dq_ref, dk_ref, dv_ref, dsq_ref, dsk_ref, dsv_ref, dff_ref, wf_ref, wr_ref, g1_ref,
             dx_ref, dg1_ref):
        i = pl.program_id(0)
        dz_fox = jnp.concatenate([dq_ref[...].astype(MM), dk_ref[...], dv_ref[...]], axis=1)
        dz_swa = jnp.concatenate([dsq_ref[...], dsk_ref[...].astype(MM), dsv_ref[...].astype(MM)], axis=1)
        da = (_dot(dz_fox, wf_ref[...]) + _dot(dz_swa, wr_ref[WT_SQ:WT_REST])
              + _dot_tn(dff_ref[...].astype(MM), wr_ref[0:WT_SQ]))
        n1, r1 = _rms(x_ref[...])
        dx, dg1 = _rms_bwd(da, n1, r1, g1_ref[...])
        _accumulate(dg1_ref, dg1, i)
        dx_ref[...] = dh1_ref[...] + dx

    body, in_specs, operands = _run_after(
        after, body,
        [_rows(tm, D_MODEL), _rows(tm, D_MODEL), *[_rows(tm, d.shape[1]) for d in dz],
         pl.BlockSpec((16, tm), lambda i: (0, i)), _const((WT_FOX, D_MODEL)), _resident(), _const((1, D_MODEL))],
        (x, dh1, *dz, dff_t, win_t, wt_rest, g1))
    return pl.pallas_call(
        body, name="pre_attn_bwd", grid=(S // tm,), in_specs=in_specs,
        out_specs=[_rows(tm, D_MODEL), _const((1, D_MODEL))],
        out_shape=[jax.ShapeDtypeStruct((S, D_MODEL), F32), jax.ShapeDtypeStruct((1, D_MODEL), F32)],
        compiler_params=_params("arbitrary"),
    )(*operands)


def _weight_grad(a, b, name, tk, n_chunks=1, relu2=False):
    S, K = a.shape
    N = b.shape[1]
    cn = N // n_chunks

    def body(a_ref, b_ref, out_ref):
        av = a_ref[...]
        if relu2:
            av = jnp.square(jnp.maximum(av.astype(F32), 0.0))
        av = av.astype(MM)
        for j in range(n_chunks):
            val = _dot_tn(av, b_ref[:, j * cn:(j + 1) * cn].astype(MM)).astype(MM)
            if n_chunks > 1:
                out_ref[j] = val
            else:
                out_ref[...] = val

    if n_chunks > 1:
        out_spec = pl.BlockSpec((n_chunks, tk, cn), lambda i: (0, i, 0))
        out_shape = jax.ShapeDtypeStruct((n_chunks, K, cn), MM)
    else:
        out_spec = pl.BlockSpec((tk, N), lambda i: (i, 0))
        out_shape = jax.ShapeDtypeStruct((K, N), MM)
    return pl.pallas_call(
        body, name=name, grid=(K // tk,),
        in_specs=[pl.BlockSpec((S, tk), lambda i: (0, i)), _resident()],
        out_specs=out_spec, out_shape=out_shape, compiler_params=_params("parallel"),
    )(a, b)


def _weight_grad_two(a1, a2, b, name, tk):
    S, K1 = a1.shape
    K2 = a2.shape[1]
    N = b.shape[1]
    n1 = K1 // tk

    def body(a1_ref, a2_ref, b_ref, out_ref):
        av = jnp.where(pl.program_id(0) < n1, a1_ref[...], a2_ref[...])
        out_ref[...] = _dot_tn(av, b_ref[...]).astype(MM)

    return pl.pallas_call(
        body, name=name, grid=((K1 + K2) // tk,),
        in_specs=[pl.BlockSpec((S, tk), lambda i: (0, jnp.minimum(i, n1 - 1))),
                  pl.BlockSpec((S, tk), lambda i: (0, jnp.maximum(i - n1, 0))), _resident()],
        out_specs=pl.BlockSpec((tk, N), lambda i: (i, 0)), out_shape=jax.ShapeDtypeStruct((K1 + K2, N), MM),
        compiler_params=_params("parallel"),
    )(a1, a2, b)


def _place():
    return lax.axis_index("x"), lax.axis_index("y"), lax.axis_index("c")


def _all_gather_sequencer(stacks, name, collective_id):
    refs = [jax.new_ref(s, memory_space=pltpu.MemorySpace.HBM) for s in stacks]
    n = len(refs)

    @pl.kernel(mesh=plsc.ScalarSubcoreMesh(axis_name="sequencer", num_cores=1), name=name,
               scratch_types=(pltpu.SemaphoreType.DMA((7 * n,)), pltpu.SemaphoreType.DMA((7 * n,))),
               compiler_params=pltpu.CompilerParams(collective_id=collective_id))
    def launch(send_sems, recv_sems):
        x, y, c = _place()
        sibling = (x, y, 1 - c)
        chips = [(1 - x, y), (x, 1 - y), (1 - x, 1 - y)]
        peers = [sibling] + [(px, py, c) for px, py in chips]
        barrier = pltpu.get_barrier_semaphore()
        for peer in peers:
            pl.semaphore_signal(barrier, inc=1, device_id=peer, device_id_type=MESH)
        pl.semaphore_wait(barrier, len(peers))

        def copy(a, k, block, to):
            px, py, pc = block
            slot = refs[a].at[4 * px + 2 * py + pc]
            return _remote(slot, slot, send_sems, recv_sems, 7 * a + k, to)

        first = [copy(a, k, (x, y, c), peer) for a in range(n) for k, peer in enumerate(peers)]
        for cp in first:
            cp.start()
        passed = []
        for j, (px, py) in enumerate(chips):
            for a in range(n):
                copy(a, 1 + j, (px, py, c), sibling).wait_recv()
                passed.append(copy(a, 4 + j, (px, py, c), sibling))
                passed[-1].start()
        for a in range(n):
            copy(a, 0, (x, y, 1 - c), sibling).wait_recv()
            for j, (px, py) in enumerate(chips):
                copy(a, 4 + j, (px, py, 1 - c), sibling).wait_recv()
        for cp in first + passed:
            cp.wait_send()

    launch()
    return [ref[...] for ref in refs]


def _chip_sums(grads, others, name):
    n = len(grads)

    def body(c_ref, *refs):
        for g_ref, o_ref, out_ref in zip(refs[:n], refs[n:2 * n], refs[2 * n:]):
            out_ref[...] = (g_ref[...].astype(F32) + o_ref[...].astype(F32)).astype(out_ref.dtype)

    own = [pl.BlockSpec((None, None) + g.shape[2:], lambda k, c_ref: (k, c_ref[0], 0, 0)) for g in grads]
    chip = [pl.BlockSpec((None,) + g.shape[2:], lambda k, c_ref: (k, 0, 0)) for g in grads]
    return pl.pallas_call(
        body, name=name,
        grid_spec=pltpu.PrefetchScalarGridSpec(num_scalar_prefetch=1, grid=(4,), in_specs=own + chip, out_specs=chip),
        out_shape=[jax.ShapeDtypeStruct((4,) + g.shape[2:], MM) for g in grads],
        compiler_params=_params("parallel"),
    )(lax.axis_index("c").astype(jnp.int32).reshape(1), *grads, *others)


HBM_SPEC = pl.BlockSpec(memory_space=pltpu.HBM)
SEM_SPEC = pl.BlockSpec(memory_space=pltpu.SEMAPHORE)
DATAFLOW = pltpu.SideEffectType.DATAFLOW_SIDE_EFFECTING


def _exchange_start(name, arrays, n_copies, plan):
    n = len(arrays)

    def body(*refs):
        send_sems, recv_sems, token = refs[n], refs[n + 1], refs[2 * n + 2]
        for cp in plan(refs[:n], send_sems, recv_sems):
            cp.start()
        token[...] = jnp.zeros_like(token)

    out = pl.pallas_call(
        body, name=name,
        out_shape=(pltpu.SemaphoreType.DMA((n_copies,)), pltpu.SemaphoreType.DMA((n_copies,)),
                   *[pltpu.HBM(a.shape, a.dtype) for a in arrays], jax.ShapeDtypeStruct((1, D_MODEL), F32)),
        in_specs=[HBM_SPEC] * n,
        out_specs=(SEM_SPEC, SEM_SPEC, *[HBM_SPEC] * n, pl.BlockSpec(memory_space=pltpu.VMEM)),
        input_output_aliases={i: 2 + i for i in range(n)},
        compiler_params=pltpu.CompilerParams(has_side_effects=DATAFLOW),
    )(*[pltpu.with_memory_space_constraint(a, pltpu.HBM) for a in arrays])
    return (out[0], out[1]), list(out[2:2 + n]), out[2 + n]


def _exchange_wait(name, arrays, sems, after, plan):
    n = len(arrays)
    after = list(after) if isinstance(after, (list, tuple)) else [after]

    def body(*refs):
        send_sems, recv_sems = refs[n], refs[n + 1]
        for cp in plan(refs[:n], send_sems, recv_sems):
            cp.wait_send()
            cp.wait_recv()

    out = pl.pallas_call(
        body, name=name, out_shape=[pltpu.HBM(a.shape, a.dtype) for a in arrays],
        in_specs=[HBM_SPEC] * n + [SEM_SPEC, SEM_SPEC] + [pl.BlockSpec(memory_space=pl.ANY)] * len(after),
        out_specs=[HBM_SPEC] * n, input_output_aliases={i: i for i in range(n)},
        compiler_params=pltpu.CompilerParams(has_side_effects=DATAFLOW),
    )(*arrays, sems[0], sems[1], *after)
    return list(out)


def _remote(src, dst, send_sems, recv_sems, k, to):
    return pltpu.make_async_remote_copy(src_ref=src, dst_ref=dst, send_sem=send_sems.at[k], recv_sem=recv_sems.at[k],
                                        device_id=to, device_id_type=MESH)


def _plan_gather_direct(refs, send_sems, recv_sems):
    x, y, c = _place()
    me = 4 * x + 2 * y + c
    peers = [(x, y, 1 - c), (1 - x, y, c), (x, 1 - y, c), (1 - x, 1 - y, c)]
    return [_remote(ref.at[me], ref.at[me], send_sems, recv_sems, 4 * a + k, peer)
            for a, ref in enumerate(refs) for k, peer in enumerate(peers)]


def _plan_gather_pass_on(refs, send_sems, recv_sems):
    x, y, c = _place()
    chips = [(1 - x, y), (x, 1 - y), (1 - x, 1 - y)]
    return [_remote(ref.at[4 * px + 2 * py + c], ref.at[4 * px + 2 * py + c], send_sems, recv_sems, 3 * a + k,
                    (x, y, 1 - c))
            for a, ref in enumerate(refs) for k, (px, py) in enumerate(chips)]


def _plan_in_chip(refs, send_sems, recv_sems):
    n = len(refs) // 2
    x, y, c = _place()
    return [_remote(refs[a].at[:, 1 - c], refs[n + a], send_sems, recv_sems, a, (x, y, 1 - c)) for a in range(n)]


def _plan_between_chips(refs, send_sems, recv_sems):
    n = len(refs) // 2
    x, y, c = _place()
    chips = [(1 - x, y), (x, 1 - y), (1 - x, 1 - y)]
    return [_remote(refs[a].at[2 * px + py], refs[n + a].at[2 * x + y], send_sems, recv_sems, 3 * a + k, (px, py, c))
            for a in range(n) for k, (px, py) in enumerate(chips)]


def _plan_late_between(refs, send_sems, recv_sems):
    sums, land, small = refs
    x, y, c = _place()
    me = 4 * x + 2 * y + c
    copies = _plan_between_chips([sums, land], send_sems, recv_sems)
    peers = [(x ^ dx, y ^ dy, c ^ dc) for dx in range(2) for dy in range(2) for dc in range(2) if dx + dy + dc]
    return copies + [_remote(small.at[me], small.at[me], send_sems, recv_sems, 3 + k, peer)
                     for k, peer in enumerate(peers)]


def _adamw_math(w, g, m, v):
    m = ADAM_B1 * m + (1.0 - ADAM_B1) * g
    v = ADAM_B2 * v + (1.0 - ADAM_B2) * jnp.square(g)
    m_hat = m / (1.0 - ADAM_B1 ** ADAM_STEP)
    v_hat = v / (1.0 - ADAM_B2 ** ADAM_STEP)
    delta = -ADAM_LR * (m_hat / (jnp.sqrt(v_hat) + ADAM_EPS) + ADAM_WD * w)
    return delta, m, v


def _adamw_small(parts, w, m, v):
    n_parts = parts.shape[0]
    n_rows = len(SMALL_ROWS)
    names = SMALL_ROWS + ("b_forget", "swa_sinks", "rel_bias")
    shapes = [(1, D_MODEL)] * n_rows + [(1, N_HEADS), (1, N_HEADS), (N_HEADS, N_BUCKETS)]

    def body(p_ref, w_ref, m_ref, v_ref, *outs):
        g = p_ref[0]
        for k in range(1, n_parts):
            g = g + p_ref[k]
        delta, m_new, v_new = _adamw_math(w_ref[...], g, m_ref[...], v_ref[...])
        for kind, val in enumerate((g, delta, m_new, v_new)):
            o = outs[kind * len(names):(kind + 1) * len(names)]
            for i in range(n_rows):
                o[i][...] = val[i:i + 1]
            misc = val[n_rows:n_rows + 1]
            o[n_rows][...] = misc[:, :N_HEADS]
            o[n_rows + 1][...] = misc[:, N_HEADS:2 * N_HEADS]
            for h in range(N_HEADS):
                first = 2 * N_HEADS + h * N_BUCKETS
                o[n_rows + 2][h:h + 1, :] = misc[:, first:first + N_BUCKETS]
        outs[-1][...] = g[n_rows + 1:n_rows + 2, 0:1]

    out = pl.pallas_call(
        body, name="adamw_small", in_specs=[_resident()] * 4, out_specs=[_resident()] * (4 * len(names) + 1),
        out_shape=[jax.ShapeDtypeStruct(s, F32) for s in shapes * 4] + [jax.ShapeDtypeStruct((1, 1), F32)],
        compiler_params=_params(),
    )(parts, w, m, v)
    kinds =[dict(zip(names, out[kind * len(names):(kind + 1) * len(names)])) for kind in range(4)]
    kinds[0]["loss"] = out[-1]
    return kinds


def _adamw_chips(parts, sums, w, m, v, name):
    _, r, cdim = parts.shape
    tr = r // 4 if r % 64 == 0 else r
    apart = w.ndim == 3

    def body(chip_ref, p_ref, own_ref, w_ref, m_ref, v_ref, g_out, d_out, m_out, v_out):
        g = None
        for k in range(4):
            term = jnp.where(chip_ref[0] == k, own_ref[...], p_ref[k]).astype(F32)
            g = term if g is None else g + term
        get = (lambda ref: ref[:, 0, :]) if apart else (lambda ref: ref[...])
        delta, m_new, v_new = _adamw_math(get(w_ref), g, get(m_ref), get(v_ref))
        for ref, val in ((g_out, g), (d_out, delta), (m_out, m_new), (v_out, v_new)):
            if apart:
                ref[:, 0, :] = val
            else:
                ref[...] = val

    if apart:
        blk = pl.BlockSpec((tr, 1, cdim), lambda i, chip: (i, 0, 0))
        shape = (r, 1, cdim)
    else:
        blk = pl.BlockSpec((tr, cdim), lambda i, chip: (i, 0))
        shape = (r, cdim)
    my_chip = (2 * lax.axis_index("x") + lax.axis_index("y")).astype(jnp.int32).reshape(1)
    return pl.pallas_call(
        body, name=name,
        grid_spec=pltpu.PrefetchScalarGridSpec(
            num_scalar_prefetch=1, grid=(r // tr,),
            in_specs=[pl.BlockSpec((4, tr, cdim), lambda i, chip: (0, i, 0)),
                      pl.BlockSpec((None, tr, cdim), lambda i, chip: (chip[0], i, 0)), blk, blk, blk],
            out_specs=[blk] * 4),
        out_shape=[jax.ShapeDtypeStruct(shape, F32)] * 4,
        compiler_params=_params("parallel"),
    )(my_chip, parts, sums, w, m, v)


class _NoExchange:
    def __init__(self, weights):
        self.weights = weights

    def before_pre_attn(self):
        return None

    def after_fox_fwd(self, fox_o):
        return None

    def after_attention(self, swa_o):
        return self.weights

    def after_early_grads(self, grads):
        return None

    def after_swa_bwd(self, dsq):
        return None

    def after_w_in_grad(self, d_win):
        return None


def _slot_order(t, axis):
    shp = t.shape
    t = t.reshape(shp[:axis] + (2, 4, shp[axis] // N_HEADS) + shp[axis + 1:])
    return jnp.swapaxes(t, axis, axis + 1).reshape(shp)


def _head_order(t, axis):
    shp = t.shape
    t = t.reshape(shp[:axis] + (4, 2, shp[axis] // N_HEADS) + shp[axis + 1:])
    return jnp.swapaxes(t, axis, axis + 1).reshape(shp)


def _forward_backward(x, p, target, win_t, hooks, b_forget, rel_bias, sinks, g1, g2, g3, g4, g5):
    S = x.shape[0]
    tm = 512
    tm_mlp = 512
    t = 256
    q0 = 3 * D_ATT + N_HEADS
    win_t = win_t.reshape(D_IN, D_MODEL)
    wt_rest = jnp.concatenate(
        [win_t[WT_FOX:q0], jnp.zeros((8, D_MODEL), MM), _slot_order(win_t[q0:q0 + D_ATT], 0), win_t[q0 + D_ATT:]],
        axis=0)
    bcol = jnp.pad(b_forget.reshape(N_HEADS, 1), ((0, 8), (0, 0)))
    rel_bias_slot = rel_bias[:, np.array(SLOT_HEAD)]
    sinks_slot = sinks.reshape(N_HEADS)[np.array(SLOT_HEAD)]
    bucket = jnp.asarray(_swa_bucket_map())

    a, fqkv, sqkv, fft = _pre_attn(x, g1, win_t, wt_rest, tm, after=hooks.before_pre_attn())
    c_row = _forget_cumsum(fft, bcol)
    c_col = c_row[:N_HEADS].T
    c_row3 = c_row[:N_HEADS].reshape(N_HEADS, S // t, t)
    fox_o, fox_lse = _fox_fwd(fqkv, c_row3, tq=512, tk=t)
    biasm = _swa_bias(rel_bias_slot, bucket)
    swa_o, swa_lse = _swa_fwd(sqkv, biasm, sinks_slot, after=hooks.after_fox_fwd(fox_o))
    wout, w1, w2, wple, wg = hooks.after_attention(swa_o)
    wout_fox = wout[:D_ATT]
    wout_swa = _slot_order(wout[D_ATT:], 0)
    mix, h1, m = _post_attn(x, fox_o, swa_o, wout_fox, wout_swa, g2, g3, tm)
    u, y, h2 = _mlp_fwd(m, h1, w1, w2, g4, tm_mlp)
    dh2, dpe, dgl, dg5, loss = _ple_loss(h2, p, target, wg, wple, g5, tm)

    d_wple = _weight_grad(p, dpe, "grad_w_ple", tk=D_PLE, n_chunks=N_DEV)
    d_wg = _weight_grad(h2, dgl, "grad_w_ple_gate", tk=256)
    dh1, dy, du, dg4, dg3 = _mlp_bwd(dh2, y, h1, u, w1, w2, g4, g3, tm)
    d_w2 = _weight_grad(u, dy, "grad_w_ff2", tk=256, relu2=True)
    d_w1 = _weight_grad(m, du, "grad_w_ff1", tk=256, n_chunks=N_DEV)
    head = np.arange(D_ATT) // HEAD_DIM
    head_rows = jnp.asarray((head[None, :] == np.arange(N_HEADS)[:, None]).astype(np.float32))
    dmix, dcat, d_row, d_swa, dg2 = _attn_out_bwd(dh1, mix, fox_o, swa_o, wout_fox, wout_swa, g2, head_rows, tm)
    d_col = d_swa.T
    d_wout = _weight_grad_two(fox_o, swa_o, dmix, "grad_w_out", tk=256)
    d_wout = jnp.concatenate([d_wout[:D_ATT], _head_order(d_wout[D_ATT:], 0)], axis=0)
    d_wout = d_wout.reshape(N_DEV, D_MODEL // N_DEV, D_MODEL)
    early = dict(w_ff1=d_w1, w_ff2=d_w2.reshape(N_DEV, FF_CHUNK, D_MODEL), w_ple=d_wple,
                 w_ple_gate=d_wg.reshape(N_DEV, D_MODEL // N_DEV, D_MODEL), w_out=d_wout)

    dsq, dsk, dsv, d_rb_slot, d_sink_slot = _swa_bwd(sqkv, dcat, biasm, sinks_slot, bucket, swa_lse, d_col,
                                                     after=hooks.after_early_grads(early))
    lse_row3 = fox_lse.T.reshape(N_HEADS, S // t, t)
    d_row3 = d_row.reshape(N_HEADS, S // t, t)
    dq_fox, dk_fox, dv_fox, dc_col, dcq = _fox_bwd(fqkv, dcat, lse_row3, d_row3, c_col, tq=t, tk=512,
                                                  after=hooks.after_swa_bwd(dsq))
    dc_row = jnp.pad(dc_col.T + dcq.reshape(N_HEADS, S), ((0, 8), (0, 0)))
    dff_t, db, d_wff_t = _forget_bwd(dc_row, fft, bcol, a)
    dz = [dq_fox, dk_fox, dv_fox, dsq, dsk, dsv]
    d_wmain = _weight_grad_pieces(dz, a, "grad_w_in", tk=256)

    sq0 = 3 * D_ATT
    d_win = jnp.concatenate(
        [d_wmain[:sq0], d_wff_t[:N_HEADS].astype(MM), _head_order(d_wmain[sq0:sq0 + D_ATT], 0),
         d_wmain[sq0 + D_ATT:]], axis=0)
    d_win = d_win.reshape(N_DEV, D_IN // N_DEV, D_MODEL)
    grad_x, dg1 = _pre_attn_bwd(x, dh1, dz, dff_t, win_t, wt_rest, g1, tm, after=hooks.after_w_in_grad(d_win))
    big = dict(early, w_in=d_win)
    small = dict(b_forget=db[:N_HEADS].reshape(1, N_HEADS), rel_bias=d_rb_slot[:, np.array(HEAD_SLOT)],
                 swa_sinks=d_sink_slot[:, np.array(HEAD_SLOT)], g_attn_pre=dg1, g_attn_post=dg2, g_ff_pre=dg3,
                 g_ff_post=dg4, g_ple_post=dg5)
    return loss, grad_x, big, small


BIG = ("w_in", "w_out", "w_ff1", "w_ff2", "w_ple", "w_ple_gate")
SMALL_ROWS = ("g_attn_pre", "g_attn_post", "g_ff_pre", "g_ff_post", "g_ple_post")
WEIGHTS =("w_in", "b_forget", "w_out", "rel_bias", "swa_sinks", "g_attn_pre", "g_attn_post", "w_ff1", "w_ff2",
           "g_ff_pre", "g_ff_post", "w_ple", "w_ple_gate", "g_ple_post")


EARLY = ("w_ff1", "w_ff2", "w_ple", "w_ple_gate", "w_out")


class _Overlap:
    def __init__(self, later):
        self.later = later

    def before_pre_attn(self):
        self.gather_sems, self.later, token = _exchange_start("gather_rest_start", self.later, 4 * 5, _plan_gather_direct)
        return token

    def after_fox_fwd(self, fox_o):
        later = _exchange_wait("gather_rest_wait", self.later, self.gather_sems, fox_o, _plan_gather_direct)
        self.pass_sems, self.later, token = _exchange_start("gather_pass_on_start", later, 3 * 5, _plan_gather_pass_on)
        return token

    def after_attention(self, swa_o):
        wout_g, w1_g, w2_g, wple_g, wg_g = _exchange_wait("gather_pass_on_wait", self.later, self.pass_sems, swa_o,
                                                         _plan_gather_pass_on)
        return (wout_g.reshape(D_MODEL, D_MODEL), w1_g, w2_g.reshape(D_FF, D_MODEL),
                jnp.moveaxis(wple_g, 0, 1).reshape(D_PLE, D_MODEL), wg_g.reshape(D_MODEL, D_MODEL))

    def after_early_grads(self, grads):
        views = [grads[k].reshape((4, 2) + grads[k].shape[1:]) for k in EARLY]
        lands = [lax.empty((4,) + grads[k].shape[1:], MM) for k in EARLY]
        self.in_chip_sems, self.in_chip, token = _exchange_start("grads_in_chip_start", views + lands, len(EARLY),
                                                                 _plan_in_chip)
        return token

    def after_swa_bwd(self, dsq):
        arrays = _exchange_wait("grads_in_chip_wait", self.in_chip, self.in_chip_sems, dsq, _plan_in_chip)
        n = len(EARLY)
        sums = list(_chip_sums(arrays[:n], arrays[n:], "chip_sums_early"))
        lands = [lax.empty(s.shape, s.dtype) for s in sums]
        self.between_sems, self.between, token = _exchange_start("grads_between_chips_start", sums + lands, 3 * n,
                                                                 _plan_between_chips)
        return token

    def after_w_in_grad(self, d_win):
        self.late_in_chip_sems, self.late_in_chip, token = _exchange_start(
            "late_in_chip_start", [d_win.reshape((4, 2) + d_win.shape[1:]), lax.empty((4,) + d_win.shape[1:], MM)],
            1, _plan_in_chip)
        return token

    def finish(self, after):
        arrays = _exchange_wait("grads_between_chips_wait", self.between, self.between_sems, after,
                                _plan_between_chips)
        n = len(EARLY)
        self.sums = arrays[:n]
        return arrays[n:]


def _pack_small(t):
    rows = [t[k].reshape(1, D_MODEL) for k in SMALL_ROWS]
    misc = jnp.concatenate([t["b_forget"].reshape(-1), t["swa_sinks"].reshape(-1), t["rel_bias"].T.reshape(-1)])
    rows.append(jnp.pad(misc, (0, D_MODEL - misc.shape[0])).reshape(1, D_MODEL))
    rows.append(jnp.pad(t["loss"].reshape(-1), (0, D_MODEL - 1)).reshape(1, D_MODEL))
    rows.append(jnp.zeros((1, D_MODEL), F32))
    return jnp.concatenate(rows, axis=0).astype(F32)


def kernel(x, p, w_in, b_forget, w_out, rel_bias, swa_sinks, g_attn_pre, g_attn_post, w_ff1, w_ff2, g_ff_pre, g_ff_post, w_ple, w_ple_gate, g_ple_post, loss_target, m_w_in, m_b_forget, m_w_out, m_rel_bias, m_swa_sinks, m_g_attn_pre, m_g_attn_post, m_w_ff1, m_w_ff2, m_g_ff_pre, m_g_ff_post, m_w_ple, m_w_ple_gate, m_g_ple_post, v_w_in, v_b_forget, v_w_out, v_rel_bias, v_swa_sinks, v_g_attn_pre, v_g_attn_post, v_w_ff1, v_w_ff2, v_g_ff_pre, v_g_ff_post, v_w_ple, v_w_ple_gate, v_g_ple_post):
    w = dict(w_in=w_in, b_forget=b_forget, w_out=w_out, rel_bias=rel_bias, swa_sinks=swa_sinks,
             g_attn_pre=g_attn_pre, g_attn_post=g_attn_post, w_ff1=w_ff1, w_ff2=w_ff2, g_ff_pre=g_ff_pre,
             g_ff_post=g_ff_post, w_ple=w_ple, w_ple_gate=w_ple_gate, g_ple_post=g_ple_post)
    mom = dict(w_in=m_w_in, b_forget=m_b_forget, w_out=m_w_out, rel_bias=m_rel_bias, swa_sinks=m_swa_sinks,
               g_attn_pre=m_g_attn_pre, g_attn_post=m_g_attn_post, w_ff1=m_w_ff1, w_ff2=m_w_ff2,
               g_ff_pre=m_g_ff_pre, g_ff_post=m_g_ff_post, w_ple=m_w_ple, w_ple_gate=m_w_ple_gate,
               g_ple_post=m_g_ple_post)
    var = dict(w_in=v_w_in, b_forget=v_b_forget, w_out=v_w_out, rel_bias=v_rel_bias, swa_sinks=v_swa_sinks,
               g_attn_pre=v_g_attn_pre, g_attn_post=v_g_attn_post, w_ff1=v_w_ff1, w_ff2=v_w_ff2,
               g_ff_pre=v_g_ff_pre, g_ff_post=v_g_ff_post, w_ple=v_w_ple, w_ple_gate=v_w_ple_gate,
               g_ple_post=v_g_ple_post)

    turn = lambda t, k: t.T if k == "w_in" else t
    me = 4 * lax.axis_index("x") + 2 * lax.axis_index("y") + lax.axis_index("c")

    def stack(block):
        return lax.dynamic_update_slice_in_dim(lax.empty((N_DEV,) + block.shape, block.dtype), block[None], me, 0)

    stacks = [stack(turn(w[k][0], k).astype(MM)) for k in BIG]
    (win_g,), later = _all_gather_sequencer(stacks[:1], "all_gather_sequencer", 1), stacks[1:]
    hooks = _Overlap(later)
    loss, grad_x, big, small = _forward_backward(
        x[0], p[0, 0], loss_target[0], win_g, hooks, b_forget, rel_bias, swa_sinks,
        g_attn_pre, g_attn_post, g_ff_pre, g_ff_post, g_ple_post)
    out_g, out_d, out_m, out_v = {}, {}, {}, {}

    def update(k, part, own):
        if k == "w_in":
            there, back = (lambda t: jnp.transpose(t, (2, 0, 1))), (lambda t: jnp.transpose(t, (1, 2, 0)))
        else:
            there, back = (lambda t: t[0]), (lambda t: t[None])
        g, d, m_new, v_new = _adamw_chips(part, own, there(w[k]), there(mom[k]), there(var[k]), "adamw_" + k)
        out_g[k], out_d[k], out_m[k], out_v[k] = back(g), back(d), back(m_new), back(v_new)
        return d

    view, other = _exchange_wait("late_in_chip_wait", hooks.late_in_chip, hooks.late_in_chip_sems, grad_x,
                                 _plan_in_chip)
    (chip_sum,) = _chip_sums([view], [other], "chip_sum_w_in")
    small["loss"] = loss
    between_sems, between, token = _exchange_start(
        "late_between_chips_start", [chip_sum, lax.empty(chip_sum.shape, MM), stack(_pack_small(small))], 3 + 7,
        _plan_late_between)
    early_parts = hooks.finish(token)
    done = [update(k, part, own) for k, part, own in zip(EARLY, early_parts, hooks.sums)]
    chip_sum, part, small_all = _exchange_wait("late_between_chips_wait", between, between_sems, done,
                                               _plan_late_between)
    update("w_in", part, chip_sum)
    rep = {k: w[k] for k in w if k not in BIG}
    rep["loss"] = jnp.zeros((), F32)
    rep_m = {k: mom[k] for k in mom if k not in BIG}
    rep_m["loss"] = jnp.zeros((), F32)
    rep_v = {k: var[k] for k in var if k not in BIG}
    rep_v["loss"] = jnp.ones((), F32)
    g_s, d_s, m_s, v_s = _adamw_small(small_all, _pack_small(rep), _pack_small(rep_m), _pack_small(rep_v))
    for k in w:
        if k not in BIG:
            natural = (lambda t: t.T) if k == "rel_bias" else (lambda t: t)
            out_g[k], out_d[k], out_m[k], out_v[k] = natural(g_s[k]), natural(d_s[k]), natural(m_s[k]), natural(v_s[k])
    return (g_s["loss"].reshape(()), grad_x[None], *[out_g[k] for k in WEIGHTS], *[out_d[k] for k in WEIGHTS],
            *[out_m[k] for k in WEIGHTS], *[out_v[k] for k in WEIGHTS])
```

```python
import functools

import numpy as np
import jax
import jax.numpy as jnp
from jax import lax
from jax.experimental import pallas as pl
from jax.experimental.pallas import tpu as pltpu
from jax.experimental.pallas import tpu_sc as plsc

F32 = jnp.float32
MM = jnp.bfloat16

D_MODEL = 1024
HEAD_DIM = 64
N_HEADS = 8
D_ATT = N_HEADS * HEAD_DIM
D_KV = 128
D_FF = 4096
D_PLE = 256
D_IN = 3 * D_ATT + N_HEADS + D_ATT + 2 * D_KV
N_DEV = 8
FF_CHUNK = D_FF // N_DEV
WINDOW = 128
N_BUCKETS = 32
MAX_DISTANCE = 128
RMS_EPS = 1e-6
Q_SCALE = HEAD_DIM ** -0.5
NEG = -1e30

ADAM_LR = 0.001
ADAM_B1 = 0.9
ADAM_B2 = 0.999
ADAM_EPS = 1e-08
ADAM_WD = 0.01
ADAM_STEP = 10

SLOT_HEAD = (0, 4, 1, 5, 2, 6, 3, 7)
HEAD_SLOT = (0, 2, 4, 6, 1, 3, 5, 7)

VMEM_LIMIT = 60 * 1024 * 1024
MESH = pl.DeviceIdType.MESH

NT = (((1,), (1,)), ((), ()))
TN = (((0,), (0,)), ((), ()))


def _params(*semantics):
    return pltpu.CompilerParams(dimension_semantics=semantics, vmem_limit_bytes=VMEM_LIMIT)


def _resident():
    return pl.BlockSpec(memory_space=pltpu.VMEM)


def _rows(tm, width):
    return pl.BlockSpec((tm, width), lambda i: (i, 0))


def _const(shape):
    return pl.BlockSpec(shape, lambda i: (0,) * len(shape))


def _dot(a, b):
    return jnp.dot(a, b, preferred_element_type=F32)


def _dot_nt(a, b):
    return lax.dot_general(a, b, NT, preferred_element_type=F32)


def _dot_tn(a, b):
    return lax.dot_general(a, b, TN, preferred_element_type=F32)


def _rms(xf):
    r = lax.rsqrt(jnp.mean(xf * xf, axis=-1, keepdims=True) + RMS_EPS)
    return xf * r, r


def _rms_bwd(dout, n, r, g):
    dg = jnp.sum(dout * n, axis=0, keepdims=True)
    dn = dout * g
    dx = r * (dn - n * jnp.mean(dn * n, axis=-1, keepdims=True))
    return dx, dg


def _run_after(after, body, in_specs, operands):
    if after is None:
        return body, list(in_specs), tuple(operands)
    n = len(operands)
    return ((lambda *refs: body(*refs[:n], *refs[n + 1:])), list(in_specs) + [pl.BlockSpec(memory_space=pl.ANY)],
            tuple(operands) + (after,))


def _accumulate(ref, value, step):
    @pl.when(step == 0)
    def _():
        ref[...] = value

    @pl.when(step != 0)
    def _():
        ref[...] += value


def _t5_bucket(n):
    max_exact = N_BUCKETS // 2
    large = max_exact + (np.log(np.maximum(n, 1) / max_exact) / np.log(MAX_DISTANCE / max_exact)
                         * (N_BUCKETS - max_exact)).astype(np.int32)
    large = np.minimum(large, N_BUCKETS - 1)
    return np.where(n < max_exact, n, large).astype(np.int32)


def _swa_bucket_map():
    i = np.arange(WINDOW)[:, None]
    j = np.arange(2 * WINDOW)[None, :]
    dist = i + WINDOW - j
    ok = (dist >= 0) & (dist < WINDOW)
    return np.where(ok, _t5_bucket(np.clip(dist, 0, None)), -1).astype(np.int32)


WT_FOX = 3 * D_ATT
WT_SQ = 16
WT_SKV = WT_SQ + D_ATT
WT_REST = WT_SKV + 2 * D_KV


def _pre_attn(x, g1, win_t, wt_rest, tm, after=None):
    S = x.shape[0]

    def body(x_ref, g_ref, wf_ref, wr_ref, a_ref, fqkv_ref, sqkv_ref, fft_ref):
        n, _ = _rms(x_ref[...])
        a = (n * g_ref[...]).astype(MM)
        a_ref[...] = a
        fqkv_ref[:, :D_ATT] = (_dot_nt(a, wf_ref[0:D_ATT]) * Q_SCALE).astype(MM)
        fqkv_ref[:, D_ATT:] = _dot_nt(a, wf_ref[D_ATT:WT_FOX]).astype(MM)
        sqkv_ref[:, :D_ATT] = (_dot_nt(a, wr_ref[WT_SQ:WT_SKV]) * Q_SCALE).astype(MM)
        sqkv_ref[:, D_ATT:] = _dot_nt(a, wr_ref[WT_SKV:WT_REST]).astype(MM)
        fft_ref[...] = _dot_nt(wr_ref[0:WT_SQ], a)

    body, in_specs, operands = _run_after(
        after, body, [_rows(tm, D_MODEL), _const((1, D_MODEL)), _const((WT_FOX, D_MODEL)), _resident()],
        (x, g1, win_t, wt_rest))
    return pl.pallas_call(
        body, name="pre_attn", grid=(S // tm,), in_specs=in_specs,
        out_specs=[_rows(tm, D_MODEL), _rows(tm, 3 * D_ATT), _rows(tm, D_ATT + 2 * D_KV),
                   pl.BlockSpec((16, tm), lambda i: (0, i))],
        out_shape=[jax.ShapeDtypeStruct((S, D_MODEL), MM), jax.ShapeDtypeStruct((S, 3 * D_ATT), MM),
                   jax.ShapeDtypeStruct((S, D_ATT + 2 * D_KV), MM), jax.ShapeDtypeStruct((16, S), F32)],
        compiler_params=_params("parallel"),
    )(*operands)


def _lane_scan(v, reverse):
    S = v.shape[1]
    lane = lax.broadcasted_iota(jnp.int32, v.shape, 1)
    k = 1
    while k < S:
        if reverse:
            v = v + jnp.where(lane < S - k, pltpu.roll(v, S - k, axis=1), 0.0)
        else:
            v = v + jnp.where(lane >= k, pltpu.roll(v, k, axis=1), 0.0)
        k *= 2
    return v


def _forget_cumsum(fft, bcol):
    def body(f_ref, b_ref, c_ref):
        z = f_ref[...] + b_ref[...]
        log_f = jnp.minimum(z, 0.0) - jnp.log1p(jnp.exp(-jnp.abs(z)))
        c_ref[...] = _lane_scan(log_f, reverse=False)

    return pl.pallas_call(
        body, name="forget_cumsum", out_shape=jax.ShapeDtypeStruct(fft.shape, F32),
        in_specs=[_resident(), _resident()], out_specs=_resident(),
    )(fft, bcol)


def _forget_bwd(dc_row, fft, bcol, a):
    def body(dc_ref, f_ref, b_ref, a_ref, dff_ref, db_ref, dw_ref):
        z = f_ref[...] + b_ref[...]
        dlog_f = _lane_scan(dc_ref[...], reverse=True)
        dff = dlog_f * (1.0 / (1.0 + jnp.exp(z)))
        dff_ref[...] = dff
        db_ref[...] = jnp.sum(dff, axis=1, keepdims=True)
        dw_ref[...] = _dot(dff.astype(MM), a_ref[...])

    return pl.pallas_call(
        body, name="forget_bwd",
        out_shape=[jax.ShapeDtypeStruct(fft.shape, F32), jax.ShapeDtypeStruct((fft.shape[0], 1), F32),
                   jax.ShapeDtypeStruct((fft.shape[0], D_MODEL), F32)],
        in_specs=[_resident()] * 4, out_specs=[_resident()] * 3,
    )(dc_row, fft, bcol, a)


def _head_select(shape, upper):
    lane = lax.broadcasted_iota(jnp.int32, shape, 1)
    return lane >= HEAD_DIM if upper else lane < HEAD_DIM


def _fox_fwd(fqkv, c_row3, tq, tk, pairs_per_loop=2, row_chunks=1):
    S = fqkv.shape[0]
    rq = tq // row_chunks
    n_band = tq // tk

    def body(q_ref, k_ref, v_ref, ck_ref, o_ref, lse_ref):
        qi = pl.program_id(0)
        row = lax.broadcasted_iota(jnp.int32, (rq, tk), 0)
        col = lax.broadcasted_iota(jnp.int32, (rq, tk), 1)
        low = _head_select((rq, 128), 0)
        for first in range(0, N_HEADS // 2, pairs_per_loop):
            pairs = range(first, first + pairs_per_loop)
            chains = [(pr, hh, rc) for pr in pairs for hh in range(2) for rc in range(row_chunks)]
            qh = {}
            for pr in pairs:
                for rc in range(row_chunks):
                    q2 = q_ref[rc * rq:(rc + 1) * rq, pr * 128:(pr + 1) * 128]
                    qh[pr, 0, rc] = jnp.where(low, q2, jnp.zeros_like(q2))
                    qh[pr, 1, rc] = jnp.where(low, jnp.zeros_like(q2), q2)

            def block(kb, carry, band, chains=chains, qh=qh):
                rows = pl.ds(pl.multiple_of(kb * tk, tk), tk)
                out = []
                for (pr, hh, rc), (m, l, acc) in zip(chains, carry):
                    if band is not None and (rc + 1) * rq <= band * tk:
                        out.append((m, l, acc))
                        continue
                    lanes = slice(pr * 128, (pr + 1) * 128)
                    s = _dot_nt(qh[pr, hh, rc], k_ref[rows, lanes]) - ck_ref[2 * pr + hh, pl.ds(kb, 1), :]
                    if band is not None:
                        s = jnp.where(row + rc * rq >= col + band * tk, s, NEG)
                    m_new = jnp.maximum(m, jnp.max(s, axis=-1, keepdims=True))
                    p = jnp.exp(s - m_new)
                    alpha = jnp.exp(m - m_new)
                    l = alpha * l + jnp.sum(p, axis=-1, keepdims=True)
                    acc = alpha * acc + _dot(p.astype(MM), v_ref[rows, lanes])
                    out.append((m_new, l, acc))
                return tuple(out)

            carry = tuple((jnp.full((rq, 1), NEG, F32), jnp.zeros((rq, 1), F32), jnp.zeros((rq, 128), F32))
                          for _ in chains)
            carry = lax.fori_loop(0, qi * n_band, functools.partial(block, band=None), carry)
            for band in range(n_band):
                carry = block(qi * n_band + band, carry, band=band)
            res = {}
            for (pr, hh, rc), (m, l, acc) in zip(chains, carry):
                res[pr, hh, rc] = acc / l
                lse_ref[rc * rq:(rc + 1) * rq, 2 * pr + hh:2 * pr + hh + 1] = m + jnp.log(l)
            for pr in pairs:
                for rc in range(row_chunks):
                    o_ref[rc * rq:(rc + 1) * rq, pr * 128:(pr + 1) * 128] = jnp.where(
                        low, res[pr, 0, rc], res[pr, 1, rc]).astype(MM)

    return pl.pallas_call(
        body, name="fox_fwd", grid=(S // tq,),
        in_specs=[pl.BlockSpec((tq, D_ATT), lambda i: (i, 0)), pl.BlockSpec((S, D_ATT), lambda i: (0, 1)),
                  pl.BlockSpec((S, D_ATT), lambda i: (0, 2)), _resident()],
        out_specs=[_rows(tq, D_ATT), _rows(tq, N_HEADS)],
        out_shape=[jax.ShapeDtypeStruct((S, D_ATT), MM), jax.ShapeDtypeStruct((S, N_HEADS), F32)],
        compiler_params=_params("parallel"),
    )(fqkv, fqkv, fqkv, c_row3)


def _swa_bias(rel_bias_slot, bucket, after=None):
    def body(rb_ref, bk_ref, out_ref):
        bk = bk_ref[...]
        for s in range(N_HEADS):
            acc = jnp.where(bk < 0, NEG, 0.0).astype(F32)
            for b in range(N_BUCKETS):
                acc = jnp.where(bk == b, rb_ref[b, s], acc)
            out_ref[s] = acc

    body, in_specs, operands = _run_after(after, body, [pl.BlockSpec(memory_space=pltpu.SMEM), _resident()],
                                          (rel_bias_slot, bucket))
    return pl.pallas_call(
        body, name="swa_bias", out_shape=jax.ShapeDtypeStruct((N_HEADS, WINDOW, 2 * WINDOW), F32),
        in_specs=in_specs, out_specs=_resident(),
    )(*operands)


def _stack4(piece):
    return jnp.concatenate([piece(j) for j in range(4)], axis=0)


def _swa_specs(S):
    q = pl.BlockSpec((WINDOW, D_ATT), lambda n: (n, 0))
    kp = pl.BlockSpec((WINDOW, D_KV), lambda n: (jnp.maximum(n - 1, 0), 4))
    kc = pl.BlockSpec((WINDOW, D_KV), lambda n: (n, 4))
    vp = pl.BlockSpec((WINDOW, D_KV), lambda n: (jnp.maximum(n - 1, 0), 5))
    vc = pl.BlockSpec((WINDOW, D_KV), lambda n: (n, 5))
    return [q, kp, kc, vp, vc]


def _swa_fwd(sqkv, biasm, sinks_slot, after=None):
    S = sqkv.shape[0]

    def body(q_ref, kp_ref, kc_ref, vp_ref, vc_ref, bias_ref, sink_ref, o_ref, lse_ref):
        n = pl.program_id(0)
        no_prev = jnp.where(n > 0, 0.0, NEG)
        low = _head_select((WINDOW, 128), 0)
        res = []
        for g in range(2):
            sel = low if g == 0 else jnp.logical_not(low)
            qg = _stack4(lambda j: jnp.where(sel, q_ref[:, j * 128:(j + 1) * 128], jnp.zeros((WINDOW, 128), MM)))
            sink = _stack4(lambda j: jnp.full((WINDOW, 1), sink_ref[2 * j + g], F32))
            sp = _dot_nt(qg, kp_ref[...]) + _stack4(lambda j: bias_ref[2 * j + g, :, :WINDOW]) + no_prev
            sc = _dot_nt(qg, kc_ref[...]) + _stack4(lambda j: bias_ref[2 * j + g, :, WINDOW:])
            m = jnp.maximum(jnp.maximum(jnp.max(sp, axis=-1, keepdims=True),
                                        jnp.max(sc, axis=-1, keepdims=True)), sink)
            ep = jnp.exp(sp - m)
            ec = jnp.exp(sc - m)
            den = jnp.sum(ep, axis=-1, keepdims=True) + jnp.sum(ec, axis=-1, keepdims=True) + jnp.exp(sink - m)
            res.append((_dot(ep.astype(MM), vp_ref[...]) + _dot(ec.astype(MM), vc_ref[...])) / den)
            lse = m + jnp.log(den)
            for j in range(4):
                lse_ref[:, 2 * j + g:2 * j + g + 1] = lse[j * WINDOW:(j + 1) * WINDOW]
        for j in range(4):
            rows = slice(j * WINDOW, (j + 1) * WINDOW)
            o_ref[:, j * 128:(j + 1) * 128] = jnp.where(low, res[0][rows], res[1][rows]).astype(MM)

    body, in_specs, operands = _run_after(
        after, body, _swa_specs(S) + [_resident(), pl.BlockSpec(memory_space=pltpu.SMEM)],
        (sqkv, sqkv, sqkv, sqkv, sqkv, biasm, sinks_slot))
    return pl.pallas_call(
        body, name="swa_fwd", grid=(S // WINDOW,), in_specs=in_specs,
        out_specs=[_rows(WINDOW, D_ATT), _rows(WINDOW, N_HEADS)],
        out_shape=[jax.ShapeDtypeStruct((S, D_ATT), MM), jax.ShapeDtypeStruct((S, N_HEADS), F32)],
        compiler_params=_params("parallel"),
    )(*operands)


def _post_attn(x, fox_o, swa_o, wout_fox, wout_swa, g2, g3, tm):
    S = x.shape[0]

    def body(x_ref, fo_ref, so_ref, wf_ref, ws_ref, g2_ref, g3_ref, mix_ref, h1_ref, m_ref):
        mix = _dot(fo_ref[...], wf_ref[...]) + _dot(so_ref[...], ws_ref[...])
        mix_ref[...] = mix
        n2, _ = _rms(mix)
        h1 = x_ref[...] + n2 * g2_ref[...]
        h1_ref[...] = h1
        n3, _ = _rms(h1)
        m_ref[...] = (n3 * g3_ref[...]).astype(MM)

    return pl.pallas_call(
        body, name="post_attn", grid=(S // tm,),
        in_specs=[_rows(tm, D_MODEL), _rows(tm, D_ATT), _rows(tm, D_ATT), _resident(), _resident(),
                  _const((1, D_MODEL)), _const((1, D_MODEL))],
        out_specs=[_rows(tm, D_MODEL)] * 3,
        out_shape=[jax.ShapeDtypeStruct((S, D_MODEL), F32), jax.ShapeDtypeStruct((S, D_MODEL), F32),
                   jax.ShapeDtypeStruct((S, D_MODEL), MM)],
        compiler_params=_params("parallel"),
    )(x, fox_o, swa_o, wout_fox, wout_swa, g2, g3)


def _mlp_fwd(m, h1, w1, w2, g4, tm):
    S = m.shape[0]

    def body(m_ref, h1_ref, w1_ref, w2_ref, g4_ref, u_ref, y_ref, h2_ref):
        mb = m_ref[...]
        y = jnp.zeros((tm, D_MODEL), F32)
        for j in range(N_DEV):
            cols = slice(j * FF_CHUNK, (j + 1) * FF_CHUNK)
            u = _dot(mb, w1_ref[j])
            u_ref[:, cols] = u.astype(MM)
            y = y + _dot(jnp.square(jnp.maximum(u, 0.0)).astype(MM), w2_ref[cols, :])
        y_ref[...] = y
        n4, _ = _rms(y)
        h2_ref[...] = h1_ref[...] + n4 * g4_ref[...]

    return pl.pallas_call(
        body, name="mlp_fwd", grid=(S // tm,),
        in_specs=[_rows(tm, D_MODEL), _rows(tm, D_MODEL), _resident(), _resident(), _const((1, D_MODEL))],
        out_specs=[_rows(tm, D_FF), _rows(tm, D_MODEL), _rows(tm, D_MODEL)],
        out_shape=[jax.ShapeDtypeStruct((S, D_FF), MM), jax.ShapeDtypeStruct((S, D_MODEL), F32),
                   jax.ShapeDtypeStruct((S, D_MODEL), F32)],
        compiler_params=_params("parallel"),
    )(m, h1, w1, w2, g4)


def _ple_loss(h2, p, target, wg, wple, g5, tm):
    S = h2.shape[0]

    def body(h2_ref, p_ref, t_ref, wg_ref, wp_ref, g5_ref, dh2_ref, dpe_ref, dgl_ref, dg5_ref, loss_ref):
        i = pl.program_id(0)
        h2 = h2_ref[...]
        gate = jax.nn.sigmoid(_dot(h2.astype(MM), wg_ref[...]))
        pe = _dot(p_ref[...].astype(MM), wp_ref[...])
        n5, r5 = _rms(pe * gate)
        g5 = g5_ref[...]
        diff = h2 + n5 * g5 - t_ref[...]
        per_token = jnp.mean(jnp.square(diff), axis=-1, keepdims=True)
        _accumulate(loss_ref, 0.5 * jnp.sum(per_token, axis=0, keepdims=True), i)
        dh3 = diff * (1.0 / D_MODEL)
        de, dg5 = _rms_bwd(dh3, n5, r5, g5)
        _accumulate(dg5_ref, dg5, i)
        dpe_ref[...] = (de * gate).astype(MM)
        dgl = (de * pe * gate * (1.0 - gate)).astype(MM)
        dgl_ref[...] = dgl
        dh2_ref[...] = dh3 + _dot_nt(dgl, wg_ref[...])

    return pl.pallas_call(
        body, name="ple_loss", grid=(S // tm,),
        in_specs=[_rows(tm, D_MODEL), _rows(tm, D_PLE), _rows(tm, D_MODEL), _resident(), _resident(),
                  _const((1, D_MODEL))],
        out_specs=[_rows(tm, D_MODEL), _rows(tm, D_MODEL), _rows(tm, D_MODEL), _const((1, D_MODEL)), _const((1, 1))],
        out_shape=[jax.ShapeDtypeStruct((S, D_MODEL), F32), jax.ShapeDtypeStruct((S, D_MODEL), MM),
                   jax.ShapeDtypeStruct((S, D_MODEL), MM), jax.ShapeDtypeStruct((1, D_MODEL), F32),
                   jax.ShapeDtypeStruct((1, 1), F32)],
        compiler_params=_params("arbitrary"),
    )(h2, p, target, wg, wple, g5)


def _mlp_bwd(dh2, y, h1, u, w1, w2, g4, g3, tm):
    S = dh2.shape[0]

    def body(dh2_ref, y_ref, h1_ref, u_ref, w1_ref, w2_ref, g4_ref, g3_ref,
             dh1_ref, dy_ref, du_ref, dg4_ref, dg3_ref):
        i = pl.program_id(0)
        dh2 = dh2_ref[...]
        n4, r4 = _rms(y_ref[...])
        dy, dg4 = _rms_bwd(dh2, n4, r4, g4_ref[...])
        _accumulate(dg4_ref, dg4, i)
        dyb = dy.astype(MM)
        dy_ref[...] = dyb
        dm = jnp.zeros((tm, D_MODEL), F32)
        for j in range(N_DEV):
            cols = slice(j * FF_CHUNK, (j + 1) * FF_CHUNK)
            dact = _dot_nt(dyb, w2_ref[cols, :])
            du = (dact * (2.0 * jnp.maximum(u_ref[:, cols].astype(F32), 0.0))).astype(MM)
            du_ref[:, cols] = du
            dm = dm + _dot_nt(du, w1_ref[j])
        n3, r3 = _rms(h1_ref[...])
        dx, dg3 = _rms_bwd(dm, n3, r3, g3_ref[...])
        _accumulate(dg3_ref, dg3, i)
        dh1_ref[...] = dh2 + dx

    return pl.pallas_call(
        body, name="mlp_bwd", grid=(S // tm,),
        in_specs=[_rows(tm, D_MODEL), _rows(tm, D_MODEL), _rows(tm, D_MODEL), _rows(tm, D_FF),
                  _resident(), _resident(), _const((1, D_MODEL)), _const((1, D_MODEL))],
        out_specs=[_rows(tm, D_MODEL), _rows(tm, D_MODEL), _rows(tm, D_FF), _const((1, D_MODEL)),
                   _const((1, D_MODEL))],
        out_shape=[jax.ShapeDtypeStruct((S, D_MODEL), F32), jax.ShapeDtypeStruct((S, D_MODEL), MM),
                   jax.ShapeDtypeStruct((S, D_FF), MM), jax.ShapeDtypeStruct((1, D_MODEL), F32),
                   jax.ShapeDtypeStruct((1, D_MODEL), F32)],
        compiler_params=_params("arbitrary"),
    )(dh2, y, h1, u, w1, w2, g4, g3)


def _attn_out_bwd(dh1, mix, fox_o, swa_o, wout_fox, wout_swa, g2, head_rows, tm):
    S = dh1.shape[0]

    def body(dh1_ref, mix_ref, fo_ref, so_ref, wf_ref, ws_ref, g2_ref, er_ref,
             dmix_ref, dcat_ref, drow_ref, dswa_ref, dg2_ref):
        i = pl.program_id(0)
        n2, r2 = _rms(mix_ref[...])
        dmix, dg2 = _rms_bwd(dh1_ref[...], n2, r2, g2_ref[...])
        _accumulate(dg2_ref, dg2, i)
        dmb = dmix.astype(MM)
        dmix_ref[...] = dmb
        dfo = _dot_nt(dmb, wf_ref[...]).astype(MM)
        dso = _dot_nt(dmb, ws_ref[...]).astype(MM)
        dcat_ref[:, :D_ATT] = dfo
        dcat_ref[:, D_ATT:] = dso
        hi = lax.Precision.HIGHEST
        prod_f = dfo.astype(F32) * fo_ref[...].astype(F32)
        prod_s = dso.astype(F32) * so_ref[...].astype(F32)
        drow_ref[...] = lax.dot_general(er_ref[...], prod_f, NT, precision=hi, preferred_element_type=F32)
        dswa_ref[...] = lax.dot_general(er_ref[...], prod_s, NT, precision=hi, preferred_element_type=F32)

    return pl.pallas_call(
        body, name="attn_out_bwd", grid=(S // tm,),
        in_specs=[_rows(tm, D_MODEL), _rows(tm, D_MODEL), _rows(tm, D_ATT), _rows(tm, D_ATT), _resident(),
                  _resident(), _const((1, D_MODEL)), _resident()],
        out_specs=[_rows(tm, D_MODEL), _rows(tm, D_MODEL), pl.BlockSpec((N_HEADS, tm), lambda i: (0, i)),
                   pl.BlockSpec((N_HEADS, tm), lambda i: (0, i)), _const((1, D_MODEL))],
        out_shape=[jax.ShapeDtypeStruct((S, D_MODEL), MM), jax.ShapeDtypeStruct((S, D_MODEL), MM),
                   jax.ShapeDtypeStruct((N_HEADS, S), F32), jax.ShapeDtypeStruct((N_HEADS, S), F32),
                   jax.ShapeDtypeStruct((1, D_MODEL), F32)],
        compiler_params=_params("arbitrary"),
    )(dh1, mix, fox_o, swa_o, wout_fox, wout_swa, g2, head_rows)


def _fox_bwd(fqkv, dcat, lse_row3, d_row3, c_col, tq, tk, pairs_per_loop=2, after=None):
    S = fqkv.shape[0]
    n_blk = S // tk
    n_qblk = S // tq
    n_band = tk // tq

    def body(q_ref, k_ref, v_ref, do_ref, lse_ref, dd_ref, ck_ref, dq_ref, dk_ref, dv_ref, dc_ref, dcq_ref):
        kb = pl.program_id(0)

        @pl.when(kb == 0)
        def _():
            dq_ref[...] = jnp.zeros_like(dq_ref)
            dcq_ref[...] = jnp.zeros_like(dcq_ref)

        key = lax.broadcasted_iota(jnp.int32, (tk, tq), 0)
        qry = lax.broadcasted_iota(jnp.int32, (tk, tq), 1)
        low = _head_select((tk, 128), 0)
        for first in range(0, N_HEADS // 2, pairs_per_loop):
            pairs = range(first, first + pairs_per_loop)
            heads = [(pr, hh) for pr in pairs for hh in range(2)]
            kh, vh, ck = {}, {}, {}
            for pr in pairs:
                k2 = k_ref[:, pr * 128:(pr + 1) * 128]
                v2 = v_ref[:, pr * 128:(pr + 1) * 128]
                zero = jnp.zeros_like(k2)
                kh[pr, 0], kh[pr, 1] = jnp.where(low, k2, zero), jnp.where(low, zero, k2)
                vh[pr, 0], vh[pr, 1] = jnp.where(low, v2, zero), jnp.where(low, zero, v2)
                for hh in range(2):
                    ck[pr, hh] = ck_ref[:, 2 * pr + hh:2 * pr + hh + 1]

            def block(qb, carry, band, pairs=pairs, kh=kh, vh=vh, ck=ck):
                rows = pl.ds(pl.multiple_of(qb * tq, tq), tq)
                k1 = tk if band is None else (band + 1) * tq
                out = []
                it = iter(carry)
                for pr in pairs:
                    lanes = slice(pr * 128, (pr + 1) * 128)
                    q2 = q_ref[rows, lanes]
                    do2 = do_ref[rows, lanes]
                    dq = None
                    for hh in range(2):
                        h = 2 * pr + hh
                        dk, dv, dc = next(it)
                        s_t = _dot_nt(kh[pr, hh][:k1], q2) - ck[pr, hh][:k1]
                        p_t = jnp.exp(s_t - lse_ref[h, pl.ds(qb, 1), :])
                        if band is not None:
                            p_t = jnp.where(qry[:k1] + band * tq >= key[:k1], p_t, 0.0)
                        ds_t = p_t * (_dot_nt(vh[pr, hh][:k1], do2) - dd_ref[h, pl.ds(qb, 1), :])
                        dsb = ds_t.astype(MM)
                        dv_new = dv[:k1] + _dot(p_t.astype(MM), do2)
                        dk_new = dk[:k1] + _dot(dsb, q2)
                        dc_new = dc[:k1] - jnp.sum(ds_t, axis=1, keepdims=True)
                        if k1 < tk:
                            dv_new = jnp.concatenate([dv_new, dv[k1:]], axis=0)
                            dk_new = jnp.concatenate([dk_new, dk[k1:]], axis=0)
                            dc_new = jnp.concatenate([dc_new, dc[k1:]], axis=0)
                        part = _dot_tn(dsb, kh[pr, hh][:k1])
                        dq = part if dq is None else dq + part
                        dcq_ref[h, pl.ds(qb, 1), :] += jnp.sum(ds_t, axis=0, keepdims=True)
                        out.append((dk_new, dv_new, dc_new))
                    dq_ref[rows, lanes] += dq
                return tuple(out)

            carry = tuple((jnp.zeros((tk, 128), F32), jnp.zeros((tk, 128), F32), jnp.zeros((tk, 1), F32))
                          for _ in heads)
            for band in range(n_band):
                carry = block(kb * n_band + band, carry, band=band)
            carry = lax.fori_loop((kb + 1) * n_band, n_qblk, functools.partial(block, band=None), carry)
            grads = dict(zip(heads, carry))
            for pr in pairs:
                lanes = slice(pr * 128, (pr + 1) * 128)
                dk_ref[:, lanes] = jnp.where(low, grads[pr, 0][0], grads[pr, 1][0]).astype(MM)
                dv_ref[:, lanes] = jnp.where(low, grads[pr, 0][1], grads[pr, 1][1]).astype(MM)
                for hh in range(2):
                    dc_ref[:, 2 * pr + hh:2 * pr + hh + 1] = grads[pr, hh][2]

        @pl.when(kb == n_blk - 1)
        def _():
            dq_ref[...] = dq_ref[...] * Q_SCALE

    body, in_specs, operands = _run_after(
        after, body,
        [pl.BlockSpec((S, D_ATT), lambda i: (0, 0)), pl.BlockSpec((tk, D_ATT), lambda i: (i, 1)),
         pl.BlockSpec((tk, D_ATT), lambda i: (i, 2)), pl.BlockSpec((S, D_ATT), lambda i: (0, 0)),
         _resident(), _resident(), _rows(tk, N_HEADS)],
        (fqkv, fqkv, fqkv, dcat, lse_row3, d_row3, c_col))
    return pl.pallas_call(
        body, name="fox_bwd", grid=(n_blk,), in_specs=in_specs,
        out_specs=[_const((S, D_ATT)), _rows(tk, D_ATT), _rows(tk, D_ATT), _rows(tk, N_HEADS),
                   _const((N_HEADS, n_qblk, tq))],
        out_shape=[jax.ShapeDtypeStruct((S, D_ATT), F32), jax.ShapeDtypeStruct((S, D_ATT), MM),
                   jax.ShapeDtypeStruct((S, D_ATT), MM), jax.ShapeDtypeStruct((S, N_HEADS), F32),
                   jax.ShapeDtypeStruct((N_HEADS, n_qblk, tq), F32)],
        compiler_params=_params("arbitrary"),
    )(*operands)


def _swa_bwd(sqkv, dcat, biasm, sinks_slot, bucket, lse, d_col, after=None):
    S = sqkv.shape[0]
    n_blk = S // WINDOW

    def body(q_ref, kp_ref, kc_ref, vp_ref, vc_ref, do_ref, bias_ref, sink_ref, bk_ref, lse_ref, dd_ref,
             dq_ref, dk_ref, dv_ref, drb_ref, dsink_ref, ds_acc):
        n = pl.program_id(0)

        @pl.when(n == 0)
        def _():
            dk_ref[...] = jnp.zeros_like(dk_ref)
            dv_ref[...] = jnp.zeros_like(dv_ref)
            ds_acc[...] = jnp.zeros_like(ds_acc)
            dsink_ref[...] = jnp.zeros_like(dsink_ref)

        no_prev = jnp.where(n > 0, 0.0, NEG)
        prev = pl.ds(pl.multiple_of(jnp.maximum(n - 1, 0) * WINDOW, WINDOW), WINDOW)
        cur = pl.ds(pl.multiple_of(n * WINDOW, WINDOW), WINDOW)
        lane8 = lax.broadcasted_iota(jnp.int32, (1, N_HEADS), 1)
        dkp = jnp.zeros((WINDOW, D_KV), F32)
        dkc = jnp.zeros((WINDOW, D_KV), F32)
        dvp = jnp.zeros((WINDOW, D_KV), F32)
        dvc = jnp.zeros((WINDOW, D_KV), F32)
        dsink = jnp.zeros((1, N_HEADS), F32)
        low = _head_select((WINDOW, 128), 0)
        zero = jnp.zeros((WINDOW, 128), MM)
        dqs = []
        for g in range(2):
            sel = low if g == 0 else jnp.logical_not(low)
            qg = _stack4(lambda j: jnp.where(sel, q_ref[:, j * 128:(j + 1) * 128], zero))
            dog = _stack4(lambda j: jnp.where(sel, do_ref[:, j * 128:(j + 1) * 128], zero))
            lse_g = _stack4(lambda j: lse_ref[:, 2 * j + g:2 * j + g + 1])
            dd = _stack4(lambda j: dd_ref[:, 2 * j + g:2 * j + g + 1])
            sink = _stack4(lambda j: jnp.full((WINDOW, 1), sink_ref[2 * j + g], F32))
            pp = jnp.exp(_dot_nt(qg, kp_ref[...]) + _stack4(lambda j: bias_ref[2 * j + g, :, :WINDOW]) + no_prev - lse_g)
            pc = jnp.exp(_dot_nt(qg, kc_ref[...]) + _stack4(lambda j: bias_ref[2 * j + g, :, WINDOW:]) - lse_g)
            sink_term = jnp.exp(sink - lse_g) * dd
            dsp = pp * (_dot_nt(dog, vp_ref[...]) - dd)
            dsc = pc * (_dot_nt(dog, vc_ref[...]) - dd)
            for j in range(4):
                rows = slice(j * WINDOW, (j + 1) * WINDOW)
                dsink = dsink + jnp.where(lane8 == 2 * j + g, -jnp.sum(sink_term[rows]), 0.0)
                ds_acc[2 * j + g, :, :WINDOW] += dsp[rows]
                ds_acc[2 * j + g, :, WINDOW:] += dsc[rows]
            dspb, dscb = dsp.astype(MM), dsc.astype(MM)
            dqs.append(_dot(dspb, kp_ref[...]) + _dot(dscb, kc_ref[...]))
            dkp = dkp + _dot_tn(dspb, qg)
            dkc = dkc + _dot_tn(dscb, qg)
            dvp = dvp + _dot_tn(pp.astype(MM), dog)
            dvc = dvc + _dot_tn(pc.astype(MM), dog)
        for j in range(4):
            rows = slice(j * WINDOW, (j + 1) * WINDOW)
            dq_ref[:, j * 128:(j + 1) * 128] = (jnp.where(low, dqs[0][rows], dqs[1][rows]) * Q_SCALE).astype(MM)
        dk_ref[prev, :] += dkp
        dk_ref[cur, :] += dkc
        dv_ref[prev, :] += dvp
        dv_ref[cur, :] += dvc
        dsink_ref[...] += dsink

        @pl.when(n == n_blk - 1)
        def _():
            bk = bk_ref[...]
            rb = lax.broadcasted_iota(jnp.int32, (N_BUCKETS, N_HEADS), 0)
            cb = lax.broadcasted_iota(jnp.int32, (N_BUCKETS, N_HEADS), 1)
            out = jnp.zeros((N_BUCKETS, N_HEADS), F32)
            for s in range(N_HEADS):
                acc = ds_acc[s]
                for b in range(N_BUCKETS):
                    out = out + jnp.where((rb == b) & (cb == s), jnp.sum(jnp.where(bk == b, acc, 0.0)), 0.0)
            drb_ref[...] = out

    do_spec = pl.BlockSpec((WINDOW, D_ATT), lambda n: (n, 1))
    body, in_specs, operands = _run_after(
        after, body, _swa_specs(S) + [do_spec, _resident(), pl.BlockSpec(memory_space=pltpu.SMEM), _resident(),
                                      _rows(WINDOW, N_HEADS), _rows(WINDOW, N_HEADS)],
        (sqkv, sqkv, sqkv, sqkv, sqkv, dcat, biasm, sinks_slot, bucket, lse, d_col))
    return pl.pallas_call(
        body, name="swa_bwd", grid=(n_blk,), in_specs=in_specs,
        out_specs=[_rows(WINDOW, D_ATT), _const((S, D_KV)), _const((S, D_KV)), _const((N_BUCKETS, N_HEADS)),
                   _const((1, N_HEADS))],
        out_shape=[jax.ShapeDtypeStruct((S, D_ATT), MM), jax.ShapeDtypeStruct((S, D_KV), F32),
                   jax.ShapeDtypeStruct((S, D_KV), F32), jax.ShapeDtypeStruct((N_BUCKETS, N_HEADS), F32),
                   jax.ShapeDtypeStruct((1, N_HEADS), F32)],
        scratch_shapes=[pltpu.VMEM((N_HEADS, WINDOW, 2 * WINDOW), F32)],
        compiler_params=_params("arbitrary"),
    )(*operands)


def _weight_grad_pieces(pieces, b, name, tk):
    S, N = b.shape
    wide = [p for p in pieces if p.shape[1] % tk == 0]
    narrow = pieces[len(wide):]
    assert sum(p.shape[1] for p in narrow) == tk
    counts = [p.shape[1] // tk for p in wide]
    firsts = [sum(counts[:j]) for j in range(len(wide))]
    last = sum(counts)

    def body(*refs):
        b_ref, out_ref = refs[len(pieces)], refs[len(pieces) + 1]
        i = pl.program_id(0)

        def whole_steps(ref, first, count):
            @pl.when((i >= first) & (i < first + count))
            def _():
                out_ref[...] = _dot_tn(ref[...].astype(MM), b_ref[...]).astype(MM)

        for ref, first, count in zip(refs[:len(wide)], firsts, counts):
            whole_steps(ref, first, count)

        @pl.when(i == last)
        def _():
            row = 0
            for ref in refs[len(wide):len(pieces)]:
                k = ref.shape[1]
                out_ref[row:row + k] = _dot_tn(ref[...].astype(MM), b_ref[...]).astype(MM)
                row += k

    def steps_of(first, count):
        return pl.BlockSpec((S, tk), lambda i: (0, jnp.clip(i - first, 0, count - 1)))

    return pl.pallas_call(
        body, name=name, grid=(last + 1,),
        in_specs=[steps_of(first, count) for first, count in zip(firsts, counts)]
        + [pl.BlockSpec((S, p.shape[1]), lambda i: (0, 0)) for p in narrow] + [_resident()],
        out_specs=pl.BlockSpec((tk, N), lambda i: (i, 0)), out_shape=jax.ShapeDtypeStruct(((last + 1) * tk, N), MM),
        compiler_params=_params("parallel"),
    )(*pieces, b)


def _pre_attn_bwd(x, dh1, dz, dff_t, win_t, wt_rest, g1, tm, after=None):
    S = x.shape[0]

    def body(x_ref, dh1_ref, dq_ref, dk_ref, dv_ref, dsq_ref, dsk_ref, dsv_ref, dff_ref, wf_ref, wr_ref, g1_ref,
             dx_ref, dg1_ref):
        i = pl.program_id(0)
        dz_fox = jnp.concatenate([dq_ref[...].astype(MM), dk_ref[...], dv_ref[...]], axis=1)
        dz_swa = jnp.concatenate([dsq_ref[...], dsk_ref[...].astype(MM), dsv_ref[...].astype(MM)], axis=1)
        da = (_dot(dz_fox, wf_ref[...]) + _dot(dz_swa, wr_ref[WT_SQ:WT_REST])
              + _dot_tn(dff_ref[...].astype(MM), wr_ref[0:WT_SQ]))
        n1, r1 = _rms(x_ref[...])
        dx, dg1 = _rms_bwd(da, n1, r1, g1_ref[...])
        _accumulate(dg1_ref, dg1, i)
        dx_ref[...] = dh1_ref[...] + dx

    body, in_specs, operands = _run_after(
        after, body,
        [_rows(tm, D_MODEL), _rows(tm, D_MODEL), *[_rows(tm, d.shape[1]) for d in dz],
         pl.BlockSpec((16, tm), lambda i: (0, i)), _const((WT_FOX, D_MODEL)), _resident(), _const((1, D_MODEL))],
        (x, dh1, *dz, dff_t, win_t, wt_rest, g1))
    return pl.pallas_call(
        body, name="pre_attn_bwd", grid=(S // tm,), in_specs=in_specs,
        out_specs=[_rows(tm, D_MODEL), _const((1, D_MODEL))],
        out_shape=[jax.ShapeDtypeStruct((S, D_MODEL), F32), jax.ShapeDtypeStruct((1, D_MODEL), F32)],
        compiler_params=_params("arbitrary"),
    )(*operands)


def _weight_grad(a, b, name, tk, n_chunks=1, relu2=False):
    S, K = a.shape
    N = b.shape[1]
    cn = N // n_chunks

    def body(a_ref, b_ref, out_ref):
        av = a_ref[...]
        if relu2:
            av = jnp.square(jnp.maximum(av.astype(F32), 0.0))
        av = av.astype(MM)
        for j in range(n_chunks):
            val = _dot_tn(av, b_ref[:, j * cn:(j + 1) * cn].astype(MM)).astype(MM)
            if n_chunks > 1:
                out_ref[j] = val
            else:
                out_ref[...] = val

    if n_chunks > 1:
        out_spec = pl.BlockSpec((n_chunks, tk, cn), lambda i: (0, i, 0))
        out_shape = jax.ShapeDtypeStruct((n_chunks, K, cn), MM)
    else:
        out_spec = pl.BlockSpec((tk, N), lambda i: (i, 0))
        out_shape = jax.ShapeDtypeStruct((K, N), MM)
    return pl.pallas_call(
        body, name=name, grid=(K // tk,),
        in_specs=[pl.BlockSpec((S, tk), lambda i: (0, i)), _resident()],
        out_specs=out_spec, out_shape=out_shape, compiler_params=_params("parallel"),
    )(a, b)


def _weight_grad_two(a1, a2, b, name, tk):
    S, K1 = a1.shape
    K2 = a2.shape[1]
    N = b.shape[1]
    n1 = K1 // tk

    def body(a1_ref, a2_ref, b_ref, out_ref):
        av = jnp.where(pl.program_id(0) < n1, a1_ref[...], a2_ref[...])
        out_ref[...] = _dot_tn(av, b_ref[...]).astype(MM)

    return pl.pallas_call(
        body, name=name, grid=((K1 + K2) // tk,),
        in_specs=[pl.BlockSpec((S, tk), lambda i: (0, jnp.minimum(i, n1 - 1))),
                  pl.BlockSpec((S, tk), lambda i: (0, jnp.maximum(i - n1, 0))), _resident()],
        out_specs=pl.BlockSpec((tk, N), lambda i: (i, 0)), out_shape=jax.ShapeDtypeStruct((K1 + K2, N), MM),
        compiler_params=_params("parallel"),
    )(a1, a2, b)


def _place():
    return lax.axis_index("x"), lax.axis_index("y"), lax.axis_index("c")


def _all_gather_sequencer(stacks, name, collective_id):
    refs = [jax.new_ref(s, memory_space=pltpu.MemorySpace.HBM) for s in stacks]
    n = len(refs)

    @pl.kernel(mesh=plsc.ScalarSubcoreMesh(axis_name="sequencer", num_cores=1), name=name,
               scratch_types=(pltpu.SemaphoreType.DMA((7 * n,)), pltpu.SemaphoreType.DMA((7 * n,))),
               compiler_params=pltpu.CompilerParams(collective_id=collective_id))
    def launch(send_sems, recv_sems):
        x, y, c = _place()
        sibling = (x, y, 1 - c)
        chips = [(1 - x, y), (x, 1 - y), (1 - x, 1 - y)]
        peers = [sibling] + [(px, py, c) for px, py in chips]
        barrier = pltpu.get_barrier_semaphore()
        for peer in peers:
            pl.semaphore_signal(barrier, inc=1, device_id=peer, device_id_type=MESH)
        pl.semaphore_wait(barrier, len(peers))

        def copy(a, k, block, to):
            px, py, pc = block
            slot = refs[a].at[4 * px + 2 * py + pc]
            return _remote(slot, slot, send_sems, recv_sems, 7 * a + k, to)

        first = [copy(a, k, (x, y, c), peer) for a in range(n) for k, peer in enumerate(peers)]
        for cp in first:
            cp.start()
        passed = []
        for j, (px, py) in enumerate(chips):
            for a in range(n):
                copy(a, 1 + j, (px, py, c), sibling).wait_recv()
                passed.append(copy(a, 4 + j, (px, py, c), sibling))
                passed[-1].start()
        for a in range(n):
            copy(a, 0, (x, y, 1 - c), sibling).wait_recv()
            for j, (px, py) in enumerate(chips):
                copy(a, 4 + j, (px, py, 1 - c), sibling).wait_recv()
        for cp in first + passed:
            cp.wait_send()

    launch()
    return [ref[...] for ref in refs]


def _chip_sums(grads, others, name):
    n = len(grads)

    def body(c_ref, *refs):
        for g_ref, o_ref, out_ref in zip(refs[:n], refs[n:2 * n], refs[2 * n:]):
            out_ref[...] = (g_ref[...].astype(F32) + o_ref[...].astype(F32)).astype(out_ref.dtype)

    own = [pl.BlockSpec((None, None) + g.shape[2:], lambda k, c_ref: (k, c_ref[0], 0, 0)) for g in grads]
    chip = [pl.BlockSpec((None,) + g.shape[2:], lambda k, c_ref: (k, 0, 0)) for g in grads]
    return pl.pallas_call(
        body, name=name,
        grid_spec=pltpu.PrefetchScalarGridSpec(num_scalar_prefetch=1, grid=(4,), in_specs=own + chip, out_specs=chip),
        out_shape=[jax.ShapeDtypeStruct((4,) + g.shape[2:], MM) for g in grads],
        compiler_params=_params("parallel"),
    )(lax.axis_index("c").astype(jnp.int32).reshape(1), *grads, *others)


HBM_SPEC = pl.BlockSpec(memory_space=pltpu.HBM)
SEM_SPEC = pl.BlockSpec(memory_space=pltpu.SEMAPHORE)
DATAFLOW = pltpu.SideEffectType.DATAFLOW_SIDE_EFFECTING


def _exchange_start(name, arrays, n_copies, plan):
    n = len(arrays)

    def body(*refs):
        send_sems, recv_sems, token = refs[n], refs[n + 1], refs[2 * n + 2]
        for cp in plan(refs[:n], send_sems, recv_sems):
            cp.start()
        token[...] = jnp.zeros_like(token)

    out = pl.pallas_call(
        body, name=name,
        out_shape=(pltpu.SemaphoreType.DMA((n_copies,)), pltpu.SemaphoreType.DMA((n_copies,)),
                   *[pltpu.HBM(a.shape, a.dtype) for a in arrays], jax.ShapeDtypeStruct((1, D_MODEL), F32)),
        in_specs=[HBM_SPEC] * n,
        out_specs=(SEM_SPEC, SEM_SPEC, *[HBM_SPEC] * n, pl.BlockSpec(memory_space=pltpu.VMEM)),
        input_output_aliases={i: 2 + i for i in range(n)},
        compiler_params=pltpu.CompilerParams(has_side_effects=DATAFLOW),
    )(*[pltpu.with_memory_space_constraint(a, pltpu.HBM) for a in arrays])
    return (out[0], out[1]), list(out[2:2 + n]), out[2 + n]


def _exchange_wait(name, arrays, sems, after, plan):
    n = len(arrays)
    after = list(after) if isinstance(after, (list, tuple)) else [after]

    def body(*refs):
        send_sems, recv_sems = refs[n], refs[n + 1]
        for cp in plan(refs[:n], send_sems, recv_sems):
            cp.wait_send()
            cp.wait_recv()

    out = pl.pallas_call(
        body, name=name, out_shape=[pltpu.HBM(a.shape, a.dtype) for a in arrays],
        in_specs=[HBM_SPEC] * n + [SEM_SPEC, SEM_SPEC] + [pl.BlockSpec(memory_space=pl.ANY)] * len(after),
        out_specs=[HBM_SPEC] * n, input_output_aliases={i: i for i in range(n)},
        compiler_params=pltpu.CompilerParams(has_side_effects=DATAFLOW),
    )(*arrays, sems[0], sems[1], *after)
    return list(out)


def _remote(src, dst, send_sems, recv_sems, k, to):
    return pltpu.make_async_remote_copy(src_ref=src, dst_ref=dst, send_sem=send_sems.at[k], recv_sem=recv_sems.at[k],
                                        device_id=to, device_id_type=MESH)


def _plan_gather_direct(refs, send_sems, recv_sems):
    x, y, c = _place()
    me = 4 * x + 2 * y + c
    peers = [(x, y, 1 - c), (1 - x, y, c), (x, 1 - y, c), (1 - x, 1 - y, c)]
    return [_remote(ref.at[me], ref.at[me], send_sems, recv_sems, 4 * a + k, peer)
            for a, ref in enumerate(refs) for k, peer in enumerate(peers)]


def _plan_gather_pass_on(refs, send_sems, recv_sems):
    x, y, c = _place()
    chips = [(1 - x, y), (x, 1 - y), (1 - x, 1 - y)]
    return [_remote(ref.at[4 * px + 2 * py + c], ref.at[4 * px + 2 * py + c], send_sems, recv_sems, 3 * a + k,
                    (x, y, 1 - c))
            for a, ref in enumerate(refs) for k, (px, py) in enumerate(chips)]


def _plan_in_chip(refs, send_sems, recv_sems):
    n = len(refs) // 2
    x, y, c = _place()
    return [_remote(refs[a].at[:, 1 - c], refs[n + a], send_sems, recv_sems, a, (x, y, 1 - c)) for a in range(n)]


def _plan_between_chips(refs, send_sems, recv_sems):
    n = len(refs) // 2
    x, y, c = _place()
    chips = [(1 - x, y), (x, 1 - y), (1 - x, 1 - y)]
    return [_remote(refs[a].at[2 * px + py], refs[n + a].at[2 * x + y], send_sems, recv_sems, 3 * a + k, (px, py, c))
            for a in range(n) for k, (px, py) in enumerate(chips)]


def _plan_late_between(refs, send_sems, recv_sems):
    sums, land, small = refs
    x, y, c = _place()
    me = 4 * x + 2 * y + c
    copies = _plan_between_chips([sums, land], send_sems, recv_sems)
    peers = [(x ^ dx, y ^ dy, c ^ dc) for dx in range(2) for dy in range(2) for dc in range(2) if dx + dy + dc]
    return copies + [_remote(small.at[me], small.at[me], send_sems, recv_sems, 3 + k, peer)
                     for k, peer in enumerate(peers)]


def _adamw_math(w, g, m, v):
    m = ADAM_B1 * m + (1.0 - ADAM_B1) * g
    v = ADAM_B2 * v + (1.0 - ADAM_B2) * jnp.square(g)
    m_hat = m / (1.0 - ADAM_B1 ** ADAM_STEP)
    v_hat = v / (1.0 - ADAM_B2 ** ADAM_STEP)
    delta = -ADAM_LR * (m_hat / (jnp.sqrt(v_hat) + ADAM_EPS) + ADAM_WD * w)
    return delta, m, v


def _adamw_small(parts, w, m, v):
    n_parts = parts.shape[0]
    n_rows = len(SMALL_ROWS)
    names = SMALL_ROWS + ("b_forget", "swa_sinks", "rel_bias")
    shapes = [(1, D_MODEL)] * n_rows + [(1, N_HEADS), (1, N_HEADS), (N_HEADS, N_BUCKETS)]

    def body(p_ref, w_ref, m_ref, v_ref, *outs):
        g = p_ref[0]
        for k in range(1, n_parts):
            g = g + p_ref[k]
        delta, m_new, v_new = _adamw_math(w_ref[...], g, m_ref[...], v_ref[...])
        for kind, val in enumerate((g, delta, m_new, v_new)):
            o = outs[kind * len(names):(kind + 1) * len(names)]
            for i in range(n_rows):
                o[i][...] = val[i:i + 1]
            misc = val[n_rows:n_rows + 1]
            o[n_rows][...] = misc[:, :N_HEADS]
            o[n_rows + 1][...] = misc[:, N_HEADS:2 * N_HEADS]
            for h in range(N_HEADS):
                first = 2 * N_HEADS + h * N_BUCKETS
                o[n_rows + 2][h:h + 1, :] = misc[:, first:first + N_BUCKETS]
        outs[-1][...] = g[n_rows + 1:n_rows + 2, 0:1]

    out = pl.pallas_call(
        body, name="adamw_small", in_specs=[_resident()] * 4, out_specs=[_resident()] * (4 * len(names) + 1),
        out_shape=[jax.ShapeDtypeStruct(s, F32) for s in shapes * 4] + [jax.ShapeDtypeStruct((1, 1), F32)],
        compiler_params=_params(),
    )(parts, w, m, v)
    kinds =[dict(zip(names, out[kind * len(names):(kind + 1) * len(names)])) for kind in range(4)]
    kinds[0]["loss"] = out[-1]
    return kinds


def _adamw_chips(parts, sums, w, m, v, name):
    _, r, cdim = parts.shape
    tr = 256 if r % 256 == 0 else r
    apart = w.ndim == 3

    def body(chip_ref, p_ref, own_ref, w_ref, m_ref, v_ref, g_out, d_out, m_out, v_out):
        g = None
        for k in range(4):
            term = jnp.where(chip_ref[0] == k, own_ref[...], p_ref[k]).astype(F32)
            g = term if g is None else g + term
        get = (lambda ref: ref[:, 0, :]) if apart else (lambda ref: ref[...])
        delta, m_new, v_new = _adamw_math(get(w_ref), g, get(m_ref), get(v_ref))
        for ref, val in ((g_out, g), (d_out, delta), (m_out, m_new), (v_out, v_new)):
            if apart:
                ref[:, 0, :] = val
            else:
                ref[...] = val

    if apart:
        blk = pl.BlockSpec((tr, 1, cdim), lambda i, chip: (i, 0, 0))
        shape = (r, 1, cdim)
    else:
        blk = pl.BlockSpec((tr, cdim), lambda i, chip: (i, 0))
        shape = (r, cdim)
    my_chip = (2 * lax.axis_index("x") + lax.axis_index("y")).astype(jnp.int32).reshape(1)
    return pl.pallas_call(
        body, name=name,
        grid_spec=pltpu.PrefetchScalarGridSpec(
            num_scalar_prefetch=1, grid=(r // tr,),
            in_specs=[pl.BlockSpec((4, tr, cdim), lambda i, chip: (0, i, 0)),
                      pl.BlockSpec((None, tr, cdim), lambda i, chip: (chip[0], i, 0)), blk, blk, blk],
            out_specs=[blk] * 4),
        out_shape=[jax.ShapeDtypeStruct(shape, F32)] * 4,
        compiler_params=_params("parallel"),
    )(my_chip, parts, sums, w, m, v)


class _NoExchange:
    def __init__(self, weights):
        self.weights = weights

    def before_pre_attn(self):
        return None

    def after_fox_fwd(self, fox_o):
        return None

    def after_attention(self, swa_o):
        return self.weights

    def after_early_grads(self, grads):
        return None

    def after_swa_bwd(self, dsq):
        return None

    def after_w_in_grad(self, d_win):
        return None


def _slot_order(t, axis):
    shp = t.shape
    t = t.reshape(shp[:axis] + (2, 4, shp[axis] // N_HEADS) + shp[axis + 1:])
    return jnp.swapaxes(t, axis, axis + 1).reshape(shp)


def _head_order(t, axis):
    shp = t.shape
    t = t.reshape(shp[:axis] + (4, 2, shp[axis] // N_HEADS) + shp[axis + 1:])
    return jnp.swapaxes(t, axis, axis + 1).reshape(shp)


def _forward_backward(x, p, target, win_t, hooks, b_forget, rel_bias, sinks, g1, g2, g3, g4, g5):
    S = x.shape[0]
    tm = 512
    tm_mlp = 512
    t = 256
    q0 = 3 * D_ATT + N_HEADS
    win_t = win_t.reshape(D_IN, D_MODEL)
    wt_rest = jnp.concatenate(
        [win_t[WT_FOX:q0], jnp.zeros((8, D_MODEL), MM), _slot_order(win_t[q0:q0 + D_ATT], 0), win_t[q0 + D_ATT:]],
        axis=0)
    bcol = jnp.pad(b_forget.reshape(N_HEADS, 1), ((0, 8), (0, 0)))
    rel_bias_slot = rel_bias[:, np.array(SLOT_HEAD)]
    sinks_slot = sinks.reshape(N_HEADS)[np.array(SLOT_HEAD)]
    bucket = jnp.asarray(_swa_bucket_map())

    started = hooks.before_pre_attn()
    biasm = _swa_bias(rel_bias_slot, bucket, after=started)
    a, fqkv, sqkv, fft = _pre_attn(x, g1, win_t, wt_rest, tm, after=started)
    c_row = _forget_cumsum(fft, bcol)
    c_col = c_row[:N_HEADS].T
    c_row3 = c_row[:N_HEADS].reshape(N_HEADS, S // t, t)
    fox_o, fox_lse = _fox_fwd(fqkv, c_row3, tq=512, tk=t)
    swa_o, swa_lse = _swa_fwd(sqkv, biasm, sinks_slot, after=hooks.after_fox_fwd(fox_o))
    wout, w1, w2, wple, wg = hooks.after_attention(swa_o)
    wout_fox = wout[:D_ATT]
    wout_swa = _slot_order(wout[D_ATT:], 0)
    mix, h1, m = _post_attn(x, fox_o, swa_o, wout_fox, wout_swa, g2, g3, tm)
    u, y, h2 = _mlp_fwd(m, h1, w1, w2, g4, tm_mlp)
    dh2, dpe, dgl, dg5, loss = _ple_loss(h2, p, target, wg, wple, g5, tm)

    d_wple = _weight_grad(p, dpe, "grad_w_ple", tk=D_PLE, n_chunks=N_DEV)
    d_wg = _weight_grad(h2, dgl, "grad_w_ple_gate", tk=256)
    dh1, dy, du, dg4, dg3 = _mlp_bwd(dh2, y, h1, u, w1, w2, g4, g3, tm)
    d_w2 = _weight_grad(u, dy, "grad_w_ff2", tk=256, relu2=True)
    d_w1 = _weight_grad(m, du, "grad_w_ff1", tk=256, n_chunks=N_DEV)
    head = np.arange(D_ATT) // HEAD_DIM
    head_rows = jnp.asarray((head[None, :] == np.arange(N_HEADS)[:, None]).astype(np.float32))
    dmix, dcat, d_row, d_swa, dg2 = _attn_out_bwd(dh1, mix, fox_o, swa_o, wout_fox, wout_swa, g2, head_rows, tm)
    d_col = d_swa.T
    d_wout = _weight_grad_two(fox_o, swa_o, dmix, "grad_w_out", tk=256)
    d_wout = jnp.concatenate([d_wout[:D_ATT], _head_order(d_wout[D_ATT:], 0)], axis=0)
    d_wout = d_wout.reshape(N_DEV, D_MODEL // N_DEV, D_MODEL)
    early = dict(w_ff1=d_w1, w_ff2=d_w2.reshape(N_DEV, FF_CHUNK, D_MODEL), w_ple=d_wple,
                 w_ple_gate=d_wg.reshape(N_DEV, D_MODEL // N_DEV, D_MODEL), w_out=d_wout)

    dsq, dsk, dsv, d_rb_slot, d_sink_slot = _swa_bwd(sqkv, dcat, biasm, sinks_slot, bucket, swa_lse, d_col,
                                                     after=hooks.after_early_grads(early))
    lse_row3 = fox_lse.T.reshape(N_HEADS, S // t, t)
    d_row3 = d_row.reshape(N_HEADS, S // t, t)
    dq_fox, dk_fox, dv_fox, dc_col, dcq = _fox_bwd(fqkv, dcat, lse_row3, d_row3, c_col, tq=t, tk=512,
                                                  after=hooks.after_swa_bwd(dsq))
    dc_row = jnp.pad(dc_col.T + dcq.reshape(N_HEADS, S), ((0, 8), (0, 0)))
    dff_t, db, d_wff_t = _forget_bwd(dc_row, fft, bcol, a)
    dz = [dq_fox, dk_fox, dv_fox, dsq, dsk, dsv]
    d_wmain = _weight_grad_pieces(dz, a, "grad_w_in", tk=256)

    sq0 = 3 * D_ATT
    d_win = jnp.concatenate(
        [d_wmain[:sq0], d_wff_t[:N_HEADS].astype(MM), _head_order(d_wmain[sq0:sq0 + D_ATT], 0),
         d_wmain[sq0 + D_ATT:]], axis=0)
    d_win = d_win.reshape(N_DEV, D_IN // N_DEV, D_MODEL)
    grad_x, dg1 = _pre_attn_bwd(x, dh1, dz, dff_t, win_t, wt_rest, g1, tm, after=hooks.after_w_in_grad(d_win))
    big = dict(early, w_in=d_win)
    small = dict(b_forget=db[:N_HEADS].reshape(1, N_HEADS), rel_bias=d_rb_slot[:, np.array(HEAD_SLOT)],
                 swa_sinks=d_sink_slot[:, np.array(HEAD_SLOT)], g_attn_pre=dg1, g_attn_post=dg2, g_ff_pre=dg3,
                 g_ff_post=dg4, g_ple_post=dg5)
    return loss, grad_x, big, small


BIG = ("w_in", "w_out", "w_ff1", "w_ff2", "w_ple", "w_ple_gate")
SMALL_ROWS = ("g_attn_pre", "g_attn_post", "g_ff_pre", "g_ff_post", "g_ple_post")
WEIGHTS =("w_in", "b_forget", "w_out", "rel_bias", "swa_sinks", "g_attn_pre", "g_attn_post", "w_ff1", "w_ff2",
           "g_ff_pre", "g_ff_post", "w_ple", "w_ple_gate", "g_ple_post")


EARLY = ("w_ff1", "w_ff2", "w_ple", "w_ple_gate", "w_out")


class _Overlap:
    def __init__(self, later):
        self.later = later

    def before_pre_attn(self):
        self.gather_sems, self.later, token = _exchange_start("gather_rest_start", self.later, 4 * 5, _plan_gather_direct)
        return token

    def after_fox_fwd(self, fox_o):
        later = _exchange_wait("gather_rest_wait", self.later, self.gather_sems, fox_o, _plan_gather_direct)
        self.pass_sems, self.later, token = _exchange_start("gather_pass_on_start", later, 3 * 5, _plan_gather_pass_on)
        return token

    def after_attention(self, swa_o):
        wout_g, w1_g, w2_g, wple_g, wg_g = _exchange_wait("gather_pass_on_wait", self.later, self.pass_sems, swa_o,
                                                         _plan_gather_pass_on)
        return (wout_g.reshape(D_MODEL, D_MODEL), w1_g, w2_g.reshape(D_FF, D_MODEL),
                jnp.moveaxis(wple_g, 0, 1).reshape(D_PLE, D_MODEL), wg_g.reshape(D_MODEL, D_MODEL))

    def after_early_grads(self, grads):
        views = [grads[k].reshape((4, 2) + grads[k].shape[1:]) for k in EARLY]
        lands = [lax.empty((4,) + grads[k].shape[1:], MM) for k in EARLY]
        self.in_chip_sems, self.in_chip, token = _exchange_start("grads_in_chip_start", views + lands, len(EARLY),
                                                                 _plan_in_chip)
        return token

    def after_swa_bwd(self, dsq):
        arrays = _exchange_wait("grads_in_chip_wait", self.in_chip, self.in_chip_sems, dsq, _plan_in_chip)
        n = len(EARLY)
        sums = list(_chip_sums(arrays[:n], arrays[n:], "chip_sums_early"))
        lands = [lax.empty(s.shape, s.dtype) for s in sums]
        self.between_sems, self.between, token = _exchange_start("grads_between_chips_start", sums + lands, 3 * n,
                                                                 _plan_between_chips)
        return token

    def after_w_in_grad(self, d_win):
        self.late_in_chip_sems, self.late_in_chip, token = _exchange_start(
            "late_in_chip_start", [d_win.reshape((4, 2) + d_win.shape[1:]), lax.empty((4,) + d_win.shape[1:], MM)],
            1, _plan_in_chip)
        return token

    def finish(self, after):
        arrays = _exchange_wait("grads_between_chips_wait", self.between, self.between_sems, after,
                                _plan_between_chips)
        n = len(EARLY)
        self.sums = arrays[:n]
        return arrays[n:]


def _pack_small(t):
    rows = [t[k].reshape(1, D_MODEL) for k in SMALL_ROWS]
    misc = jnp.concatenate([t["b_forget"].reshape(-1), t["swa_sinks"].reshape(-1), t["rel_bias"].T.reshape(-1)])
    rows.append(jnp.pad(misc, (0, D_MODEL - misc.shape[0])).reshape(1, D_MODEL))
    rows.append(jnp.pad(t["loss"].reshape(-1), (0, D_MODEL - 1)).reshape(1, D_MODEL))
    rows.append(jnp.zeros((1, D_MODEL), F32))
    return jnp.concatenate(rows, axis=0).astype(F32)


def kernel(x, p, w_in, b_forget, w_out, rel_bias, swa_sinks, g_attn_pre, g_attn_post, w_ff1, w_ff2, g_ff_pre, g_ff_post, w_ple, w_ple_gate, g_ple_post, loss_target, m_w_in, m_b_forget, m_w_out, m_rel_bias, m_swa_sinks, m_g_attn_pre, m_g_attn_post, m_w_ff1, m_w_ff2, m_g_ff_pre, m_g_ff_post, m_w_ple, m_w_ple_gate, m_g_ple_post, v_w_in, v_b_forget, v_w_out, v_rel_bias, v_swa_sinks, v_g_attn_pre, v_g_attn_post, v_w_ff1, v_w_ff2, v_g_ff_pre, v_g_ff_post, v_w_ple, v_w_ple_gate, v_g_ple_post):
    w = dict(w_in=w_in, b_forget=b_forget, w_out=w_out, rel_bias=rel_bias, swa_sinks=swa_sinks,
             g_attn_pre=g_attn_pre, g_attn_post=g_attn_post, w_ff1=w_ff1, w_ff2=w_ff2, g_ff_pre=g_ff_pre,
             g_ff_post=g_ff_post, w_ple=w_ple, w_ple_gate=w_ple_gate, g_ple_post=g_ple_post)
    mom = dict(w_in=m_w_in, b_forget=m_b_forget, w_out=m_w_out, rel_bias=m_rel_bias, swa_sinks=m_swa_sinks,
               g_attn_pre=m_g_attn_pre, g_attn_post=m_g_attn_post, w_ff1=m_w_ff1, w_ff2=m_w_ff2,
               g_ff_pre=m_g_ff_pre, g_ff_post=m_g_ff_post, w_ple=m_w_ple, w_ple_gate=m_w_ple_gate,
               g_ple_post=m_g_ple_post)
    var = dict(w_in=v_w_in, b_forget=v_b_forget, w_out=v_w_out, rel_bias=v_rel_bias, swa_sinks=v_swa_sinks,
               g_attn_pre=v_g_attn_pre, g_attn_post=v_g_attn_post, w_ff1=v_w_ff1, w_ff2=v_w_ff2,
               g_ff_pre=v_g_ff_pre, g_ff_post=v_g_ff_post, w_ple=v_w_ple, w_ple_gate=v_w_ple_gate,
               g_ple_post=v_g_ple_post)

    turn = lambda t, k: t.T if k == "w_in" else t
    me = 4 * lax.axis_index("x") + 2 * lax.axis_index("y") + lax.axis_index("c")

    def stack(block):
        return lax.dynamic_update_slice_in_dim(lax.empty((N_DEV,) + block.shape, block.dtype), block[None], me, 0)

    stacks = [stack(turn(w[k][0], k).astype(MM)) for k in BIG]
    (win_g,), later = _all_gather_sequencer(stacks[:1], "all_gather_sequencer", 1), stacks[1:]
    hooks = _Overlap(later)
    loss, grad_x, big, small = _forward_backward(
        x[0], p[0, 0], loss_target[0], win_g, hooks, b_forget, rel_bias, swa_sinks,
        g_attn_pre, g_attn_post, g_ff_pre, g_ff_post, g_ple_post)
    out_g, out_d, out_m, out_v = {}, {}, {}, {}

    def update(k, part, own):
        if k == "w_in":
            there, back = (lambda t: jnp.transpose(t, (2, 0, 1))), (lambda t: jnp.transpose(t, (1, 2, 0)))
        else:
            there, back = (lambda t: t[0]), (lambda t: t[None])
        g, d, m_new, v_new = _adamw_chips(part, own, there(w[k]), there(mom[k]), there(var[k]), "adamw_" + k)
        out_g[k], out_d[k], out_m[k], out_v[k] = back(g), back(d), back(m_new), back(v_new)
        return d

    view, other = _exchange_wait("late_in_chip_wait", hooks.late_in_chip, hooks.late_in_chip_sems, grad_x,
                                 _plan_in_chip)
    (chip_sum,) = _chip_sums([view], [other], "chip_sum_w_in")
    small["loss"] = loss
    between_sems, between, token = _exchange_start(
        "late_between_chips_start", [chip_sum, lax.empty(chip_sum.shape, MM), stack(_pack_small(small))], 3 + 7,
        _plan_late_between)
    early_parts = hooks.finish(token)
    done = [update(k, part, own) for k, part, own in zip(EARLY, early_parts, hooks.sums)]
    chip_sum, part, small_all = _exchange_wait("late_between_chips_wait", between, between_sems, done,
                                               _plan_late_between)
    update("w_in", part, chip_sum)
    rep = {k: w[k] for k in w if k not in BIG}
    rep["loss"] = jnp.zeros((), F32)
    rep_m = {k: mom[k] for k in mom if k not in BIG}
    rep_m["loss"] = jnp.zeros((), F32)
    rep_v = {k: var[k] for k in var if k not in BIG}
    rep_v["loss"] = jnp.ones((), F32)
    g_s, d_s, m_s, v_s = _adamw_small(small_all, _pack_small(rep), _pack_small(rep_m), _pack_small(rep_v))
    for k in w:
        if k not in BIG:
            natural = (lambda t: t.T) if k == "rel_bias" else (lambda t: t)
            out_g[k], out_d[k], out_m[k], out_v[k] = natural(g_s[k]), natural(d_s[k]), natural(m_s[k]), natural(v_s[k])
    return (g_s["loss"].reshape(()), grad_x[None], *[out_g[k] for k in WEIGHTS], *[out_d[k] for k in WEIGHTS],
            *[out_m[k] for k in WEIGHTS], *[out_v[k] for k in WEIGHTS])
```

```python
import functools

import numpy as np
import jax
import jax.numpy as jnp
from jax import lax
from jax.experimental import pallas as pl
from jax.experimental.pallas import tpu as pltpu
from jax.experimental.pallas import tpu_sc as plsc

F32 = jnp.float32
MM = jnp.bfloat16

D_MODEL = 1024
HEAD_DIM = 64
N_HEADS = 8
D_ATT = N_HEADS * HEAD_DIM
D_KV = 128
D_FF = 4096
D_PLE = 256
D_IN = 3 * D_ATT + N_HEADS + D_ATT + 2 * D_KV
N_DEV = 8
FF_CHUNK = D_FF // N_DEV
WINDOW = 128
N_BUCKETS = 32
MAX_DISTANCE = 128
RMS_EPS = 1e-6
Q_SCALE = HEAD_DIM ** -0.5
NEG = -1e30

ADAM_LR = 0.001
ADAM_B1 = 0.9
ADAM_B2 = 0.999
ADAM_EPS = 1e-08
ADAM_WD = 0.01
ADAM_STEP = 10

SLOT_HEAD = (0, 4, 1, 5, 2, 6, 3, 7)
HEAD_SLOT = (0, 2, 4, 6, 1, 3, 5, 7)

VMEM_LIMIT = 60 * 1024 * 1024
MESH = pl.DeviceIdType.MESH

NT = (((1,), (1,)), ((), ()))
TN = (((0,), (0,)), ((), ()))


def _params(*semantics):
    return pltpu.CompilerParams(dimension_semantics=semantics, vmem_limit_bytes=VMEM_LIMIT)


def _resident():
    return pl.BlockSpec(memory_space=pltpu.VMEM)


def _rows(tm, width):
    return pl.BlockSpec((tm, width), lambda i: (i, 0))


def _const(shape):
    return pl.BlockSpec(shape, lambda i: (0,) * len(shape))


def _dot(a, b):
    return jnp.dot(a, b, preferred_element_type=F32)


def _dot_nt(a, b):
    return lax.dot_general(a, b, NT, preferred_element_type=F32)


def _dot_tn(a, b):
    return lax.dot_general(a, b, TN, preferred_element_type=F32)


def _rms(xf):
    r = lax.rsqrt(jnp.mean(xf * xf, axis=-1, keepdims=True) + RMS_EPS)
    return xf * r, r


def _rms_bwd(dout, n, r, g):
    dg = jnp.sum(dout * n, axis=0, keepdims=True)
    dn = dout * g
    dx = r * (dn - n * jnp.mean(dn * n, axis=-1, keepdims=True))
    return dx, dg


def _run_after(after, body, in_specs, operands):
    if after is None:
        return body, list(in_specs), tuple(operands)
    n = len(operands)
    return ((lambda *refs: body(*refs[:n], *refs[n + 1:])), list(in_specs) + [pl.BlockSpec(memory_space=pl.ANY)],
            tuple(operands) + (after,))


def _accumulate(ref, value, step):
    @pl.when(step == 0)
    def _():
        ref[...] = value

    @pl.when(step != 0)
    def _():
        ref[...] += value


def _t5_bucket(n):
    max_exact = N_BUCKETS // 2
    large = max_exact + (np.log(np.maximum(n, 1) / max_exact) / np.log(MAX_DISTANCE / max_exact)
                         * (N_BUCKETS - max_exact)).astype(np.int32)
    large = np.minimum(large, N_BUCKETS - 1)
    return np.where(n < max_exact, n, large).astype(np.int32)


def _swa_bucket_map():
    i = np.arange(WINDOW)[:, None]
    j = np.arange(2 * WINDOW)[None, :]
    dist = i + WINDOW - j
    ok = (dist >= 0) & (dist < WINDOW)
    return np.where(ok, _t5_bucket(np.clip(dist, 0, None)), -1).astype(np.int32)


WT_FOX = 3 * D_ATT
WT_SQ = 16
WT_SKV = WT_SQ + D_ATT
WT_REST = WT_SKV + 2 * D_KV


def _pre_attn(x, g1, win_t, wt_rest, tm, after=None):
    S = x.shape[0]

    def body(x_ref, g_ref, wf_ref, wr_ref, a_ref, fqkv_ref, sqkv_ref, fft_ref):
        n, _ = _rms(x_ref[...])
        a = (n * g_ref[...]).astype(MM)
        a_ref[...] = a
        fqkv_ref[:, :D_ATT] = (_dot_nt(a, wf_ref[0:D_ATT]) * Q_SCALE).astype(MM)
        fqkv_ref[:, D_ATT:] = _dot_nt(a, wf_ref[D_ATT:WT_FOX]).astype(MM)
        sqkv_ref[:, :D_ATT] = (_dot_nt(a, wr_ref[WT_SQ:WT_SKV]) * Q_SCALE).astype(MM)
        sqkv_ref[:, D_ATT:] = _dot_nt(a, wr_ref[WT_SKV:WT_REST]).astype(MM)
        fft_ref[...] = _dot_nt(wr_ref[0:WT_SQ], a)

    body, in_specs, operands = _run_after(
        after, body, [_rows(tm, D_MODEL), _const((1, D_MODEL)), _const((WT_FOX, D_MODEL)), _resident()],
        (x, g1, win_t, wt_rest))
    return pl.pallas_call(
        body, name="pre_attn", grid=(S // tm,), in_specs=in_specs,
        out_specs=[_rows(tm, D_MODEL), _rows(tm, 3 * D_ATT), _rows(tm, D_ATT + 2 * D_KV),
                   pl.BlockSpec((16, tm), lambda i: (0, i))],
        out_shape=[jax.ShapeDtypeStruct((S, D_MODEL), MM), jax.ShapeDtypeStruct((S, 3 * D_ATT), MM),
                   jax.ShapeDtypeStruct((S, D_ATT + 2 * D_KV), MM), jax.ShapeDtypeStruct((16, S), F32)],
        compiler_params=_params("parallel"),
    )(*operands)


def _lane_scan(v, reverse):
    S = v.shape[1]
    lane = lax.broadcasted_iota(jnp.int32, v.shape, 1)
    k = 1
    while k < S:
        if reverse:
            v = v + jnp.where(lane < S - k, pltpu.roll(v, S - k, axis=1), 0.0)
        else:
            v = v + jnp.where(lane >= k, pltpu.roll(v, k, axis=1), 0.0)
        k *= 2
    return v


def _forget_cumsum(fft, bcol):
    def body(f_ref, b_ref, c_ref):
        z = f_ref[...] + b_ref[...]
        log_f = jnp.minimum(z, 0.0) - jnp.log1p(jnp.exp(-jnp.abs(z)))
        c_ref[...] = _lane_scan(log_f, reverse=False)

    return pl.pallas_call(
        body, name="forget_cumsum", out_shape=jax.ShapeDtypeStruct(fft.shape, F32),
        in_specs=[_resident(), _resident()], out_specs=_resident(),
    )(fft, bcol)


def _forget_bwd(dc_row, fft, bcol, a):
    def body(dc_ref, f_ref, b_ref, a_ref, dff_ref, db_ref, dw_ref):
        z = f_ref[...] + b_ref[...]
        dlog_f = _lane_scan(dc_ref[...], reverse=True)
        dff = dlog_f * (1.0 / (1.0 + jnp.exp(z)))
        dff_ref[...] = dff
        db_ref[...] = jnp.sum(dff, axis=1, keepdims=True)
        dw_ref[...] = _dot(dff.astype(MM), a_ref[...])

    return pl.pallas_call(
        body, name="forget_bwd",
        out_shape=[jax.ShapeDtypeStruct(fft.shape, F32), jax.ShapeDtypeStruct((fft.shape[0], 1), F32),
                   jax.ShapeDtypeStruct((fft.shape[0], D_MODEL), F32)],
        in_specs=[_resident()] * 4, out_specs=[_resident()] * 3,
    )(dc_row, fft, bcol, a)


def _head_select(shape, upper):
    lane = lax.broadcasted_iota(jnp.int32, shape, 1)
    return lane >= HEAD_DIM if upper else lane < HEAD_DIM


def _fox_fwd(fqkv, c_row3, tq, tk, pairs_per_loop=2, row_chunks=1):
    S = fqkv.shape[0]
    rq = tq // row_chunks
    n_band = tq // tk

    def body(q_ref, k_ref, v_ref, ck_ref, o_ref, lse_ref):
        qi = pl.program_id(0)
        row = lax.broadcasted_iota(jnp.int32, (rq, tk), 0)
        col = lax.broadcasted_iota(jnp.int32, (rq, tk), 1)
        low = _head_select((rq, 128), 0)
        for first in range(0, N_HEADS // 2, pairs_per_loop):
            pairs = range(first, first + pairs_per_loop)
            chains = [(pr, hh, rc) for pr in pairs for hh in range(2) for rc in range(row_chunks)]
            qh = {}
            for pr in pairs:
                for rc in range(row_chunks):
                    q2 = q_ref[rc * rq:(rc + 1) * rq, pr * 128:(pr + 1) * 128]
                    qh[pr, 0, rc] = jnp.where(low, q2, jnp.zeros_like(q2))
                    qh[pr, 1, rc] = jnp.where(low, jnp.zeros_like(q2), q2)

            def block(kb, carry, band, chains=chains, qh=qh):
                rows = pl.ds(pl.multiple_of(kb * tk, tk), tk)
                out = []
                for (pr, hh, rc), (m, l, acc) in zip(chains, carry):
                    if band is not None and (rc + 1) * rq <= band * tk:
                        out.append((m, l, acc))
                        continue
                    lanes = slice(pr * 128, (pr + 1) * 128)
                    s = _dot_nt(qh[pr, hh, rc], k_ref[rows, lanes]) - ck_ref[2 * pr + hh, pl.ds(kb, 1), :]
                    if band is not None:
                        s = jnp.where(row + rc * rq >= col + band * tk, s, NEG)
                    m_new = jnp.maximum(m, jnp.max(s, axis=-1, keepdims=True))
                    p = jnp.exp(s - m_new)
                    alpha = jnp.exp(m - m_new)
                    l = alpha * l + jnp.sum(p, axis=-1, keepdims=True)
                    acc = alpha * acc + _dot(p.astype(MM), v_ref[rows, lanes])
                    out.append((m_new, l, acc))
                return tuple(out)

            carry = tuple((jnp.full((rq, 1), NEG, F32), jnp.zeros((rq, 1), F32), jnp.zeros((rq, 128), F32))
                          for _ in chains)
            carry = lax.fori_loop(0, qi * n_band, functools.partial(block, band=None), carry)
            for band in range(n_band):
                carry = block(qi * n_band + band, carry, band=band)
            res = {}
            for (pr, hh, rc), (m, l, acc) in zip(chains, carry):
                res[pr, hh, rc] = acc / l
                lse_ref[rc * rq:(rc + 1) * rq, 2 * pr + hh:2 * pr + hh + 1] = m + jnp.log(l)
            for pr in pairs:
                for rc in range(row_chunks):
                    o_ref[rc * rq:(rc + 1) * rq, pr * 128:(pr + 1) * 128] = jnp.where(
                        low, res[pr, 0, rc], res[pr, 1, rc]).astype(MM)

    return pl.pallas_call(
        body, name="fox_fwd", grid=(S // tq,),
        in_specs=[pl.BlockSpec((tq, D_ATT), lambda i: (i, 0)), pl.BlockSpec((S, D_ATT), lambda i: (0, 1)),
                  pl.BlockSpec((S, D_ATT), lambda i: (0, 2)), _resident()],
        out_specs=[_rows(tq, D_ATT), _rows(tq, N_HEADS)],
        out_shape=[jax.ShapeDtypeStruct((S, D_ATT), MM), jax.ShapeDtypeStruct((S, N_HEADS), F32)],
        compiler_params=_params("parallel"),
    )(fqkv, fqkv, fqkv, c_row3)


def _swa_bias(rel_bias_slot, bucket):
    def body(rb_ref, bk_ref, out_ref):
        bk = bk_ref[...]
        for s in range(N_HEADS):
            acc = jnp.where(bk < 0, NEG, 0.0).astype(F32)
            for b in range(N_BUCKETS):
                acc = jnp.where(bk == b, rb_ref[b, s], acc)
            out_ref[s] = acc

    return pl.pallas_call(
        body, name="swa_bias", out_shape=jax.ShapeDtypeStruct((N_HEADS, WINDOW, 2 * WINDOW), F32),
        in_specs=[pl.BlockSpec(memory_space=pltpu.SMEM), _resident()], out_specs=_resident(),
    )(rel_bias_slot, bucket)


def _stack4(piece):
    return jnp.concatenate([piece(j) for j in range(4)], axis=0)


def _swa_specs(S):
    q = pl.BlockSpec((WINDOW, D_ATT), lambda n: (n, 0))
    kp = pl.BlockSpec((WINDOW, D_KV), lambda n: (jnp.maximum(n - 1, 0), 4))
    kc = pl.BlockSpec((WINDOW, D_KV), lambda n: (n, 4))
    vp = pl.BlockSpec((WINDOW, D_KV), lambda n: (jnp.maximum(n - 1, 0), 5))
    vc = pl.BlockSpec((WINDOW, D_KV), lambda n: (n, 5))
    return [q, kp, kc, vp, vc]


def _swa_fwd(sqkv, biasm, sinks_slot, after=None):
    S = sqkv.shape[0]

    def body(q_ref, kp_ref, kc_ref, vp_ref, vc_ref, bias_ref, sink_ref, o_ref, lse_ref):
        n = pl.program_id(0)
        no_prev = jnp.where(n > 0, 0.0, NEG)
        low = _head_select((WINDOW, 128), 0)
        res = []
        for g in range(2):
            sel = low if g == 0 else jnp.logical_not(low)
            qg = _stack4(lambda j: jnp.where(sel, q_ref[:, j * 128:(j + 1) * 128], jnp.zeros((WINDOW, 128), MM)))
            sink = _stack4(lambda j: jnp.full((WINDOW, 1), sink_ref[2 * j + g], F32))
            sp = _dot_nt(qg, kp_ref[...]) + _stack4(lambda j: bias_ref[2 * j + g, :, :WINDOW]) + no_prev
            sc = _dot_nt(qg, kc_ref[...]) + _stack4(lambda j: bias_ref[2 * j + g, :, WINDOW:])
            m = jnp.maximum(jnp.maximum(jnp.max(sp, axis=-1, keepdims=True),
                                        jnp.max(sc, axis=-1, keepdims=True)), sink)
            ep = jnp.exp(sp - m)
            ec = jnp.exp(sc - m)
            den = jnp.sum(ep, axis=-1, keepdims=True) + jnp.sum(ec, axis=-1, keepdims=True) + jnp.exp(sink - m)
            res.append((_dot(ep.astype(MM), vp_ref[...]) + _dot(ec.astype(MM), vc_ref[...])) / den)
            lse = m + jnp.log(den)
            for j in range(4):
                lse_ref[:, 2 * j + g:2 * j + g + 1] = lse[j * WINDOW:(j + 1) * WINDOW]
        for j in range(4):
            rows = slice(j * WINDOW, (j + 1) * WINDOW)
            o_ref[:, j * 128:(j + 1) * 128] = jnp.where(low, res[0][rows], res[1][rows]).astype(MM)

    body, in_specs, operands = _run_after(
        after, body, _swa_specs(S) + [_resident(), pl.BlockSpec(memory_space=pltpu.SMEM)],
        (sqkv, sqkv, sqkv, sqkv, sqkv, biasm, sinks_slot))
    return pl.pallas_call(
        body, name="swa_fwd", grid=(S // WINDOW,), in_specs=in_specs,
        out_specs=[_rows(WINDOW, D_ATT), _rows(WINDOW, N_HEADS)],
        out_shape=[jax.ShapeDtypeStruct((S, D_ATT), MM), jax.ShapeDtypeStruct((S, N_HEADS), F32)],
        compiler_params=_params("parallel"),
    )(*operands)


def _post_attn(x, fox_o, swa_o, wout_fox, wout_swa, g2, g3, tm, after=None):
    S = x.shape[0]

    def body(x_ref, fo_ref, so_ref, wf_ref, ws_ref, g2_ref, g3_ref, mix_ref, h1_ref, m_ref):
        mix = _dot(fo_ref[...], wf_ref[...]) + _dot(so_ref[...], ws_ref[...])
        mix_ref[...] = mix
        n2, _ = _rms(mix)
        h1 = x_ref[...] + n2 * g2_ref[...]
        h1_ref[...] = h1
        n3, _ = _rms(h1)
        m_ref[...] = (n3 * g3_ref[...]).astype(MM)

    body, in_specs, operands = _run_after(
        after, body, [_rows(tm, D_MODEL), _rows(tm, D_ATT), _rows(tm, D_ATT), _resident(), _resident(),
                      _const((1, D_MODEL)), _const((1, D_MODEL))], (x, fox_o, swa_o, wout_fox, wout_swa, g2, g3))
    return pl.pallas_call(
        body, name="post_attn", grid=(S // tm,), in_specs=in_specs,
        out_specs=[_rows(tm, D_MODEL)] * 3,
        out_shape=[jax.ShapeDtypeStruct((S, D_MODEL), F32), jax.ShapeDtypeStruct((S, D_MODEL), F32),
                   jax.ShapeDtypeStruct((S, D_MODEL), MM)],
        compiler_params=_params("parallel"),
    )(*operands)


def _mlp_fwd(m, h1, w1, w2, g4, tm):
    S = m.shape[0]

    def body(m_ref, h1_ref, w1_ref, w2_ref, g4_ref, u_ref, y_ref, h2_ref):
        mb = m_ref[...]
        y = jnp.zeros((tm, D_MODEL), F32)
        for j in range(N_DEV):
            cols = slice(j * FF_CHUNK, (j + 1) * FF_CHUNK)
            u = _dot(mb, w1_ref[j])
            u_ref[:, cols] = u.astype(MM)
            y = y + _dot(jnp.square(jnp.maximum(u, 0.0)).astype(MM), w2_ref[cols, :])
        y_ref[...] = y
        n4, _ = _rms(y)
        h2_ref[...] = h1_ref[...] + n4 * g4_ref[...]

    return pl.pallas_call(
        body, name="mlp_fwd", grid=(S // tm,),
        in_specs=[_rows(tm, D_MODEL), _rows(tm, D_MODEL), _resident(), _resident(), _const((1, D_MODEL))],
        out_specs=[_rows(tm, D_FF), _rows(tm, D_MODEL), _rows(tm, D_MODEL)],
        out_shape=[jax.ShapeDtypeStruct((S, D_FF), MM), jax.ShapeDtypeStruct((S, D_MODEL), F32),
                   jax.ShapeDtypeStruct((S, D_MODEL), F32)],
        compiler_params=_params("parallel"),
    )(m, h1, w1, w2, g4)


def _ple_loss(h2, p, target, wg, wple, g5, tm):
    S = h2.shape[0]

    def body(h2_ref, p_ref, t_ref, wg_ref, wp_ref, g5_ref, dh2_ref, dpe_ref, dgl_ref, dg5_ref, loss_ref):
        i = pl.program_id(0)
        h2 = h2_ref[...]
        gate = jax.nn.sigmoid(_dot(h2.astype(MM), wg_ref[...]))
        pe = _dot(p_ref[...].astype(MM), wp_ref[...])
        n5, r5 = _rms(pe * gate)
        g5 = g5_ref[...]
        diff = h2 + n5 * g5 - t_ref[...]
        per_token = jnp.mean(jnp.square(diff), axis=-1, keepdims=True)
        _accumulate(loss_ref, 0.5 * jnp.sum(per_token, axis=0, keepdims=True), i)
        dh3 = diff * (1.0 / D_MODEL)
        de, dg5 = _rms_bwd(dh3, n5, r5, g5)
        _accumulate(dg5_ref, dg5, i)
        dpe_ref[...] = (de * gate).astype(MM)
        dgl = (de * pe * gate * (1.0 - gate)).astype(MM)
        dgl_ref[...] = dgl
        dh2_ref[...] = dh3 + _dot_nt(dgl, wg_ref[...])

    return pl.pallas_call(
        body, name="ple_loss", grid=(S // tm,),
        in_specs=[_rows(tm, D_MODEL), _rows(tm, D_PLE), _rows(tm, D_MODEL), _resident(), _resident(),
                  _const((1, D_MODEL))],
        out_specs=[_rows(tm, D_MODEL), _rows(tm, D_MODEL), _rows(tm, D_MODEL), _const((1, D_MODEL)), _const((1, 1))],
        out_shape=[jax.ShapeDtypeStruct((S, D_MODEL), F32), jax.ShapeDtypeStruct((S, D_MODEL), MM),
                   jax.ShapeDtypeStruct((S, D_MODEL), MM), jax.ShapeDtypeStruct((1, D_MODEL), F32),
                   jax.ShapeDtypeStruct((1, 1), F32)],
        compiler_params=_params("arbitrary"),
    )(h2, p, target, wg, wple, g5)


def _mlp_bwd(dh2, y, h1, u, w1, w2, g4, g3, tm):
    S = dh2.shape[0]

    def body(dh2_ref, y_ref, h1_ref, u_ref, w1_ref, w2_ref, g4_ref, g3_ref,
             dh1_ref, dy_ref, du_ref, dg4_ref, dg3_ref):
        i = pl.program_id(0)
        dh2 = dh2_ref[...]
        n4, r4 = _rms(y_ref[...])
        dy, dg4 = _rms_bwd(dh2, n4, r4, g4_ref[...])
        _accumulate(dg4_ref, dg4, i)
        dyb = dy.astype(MM)
        dy_ref[...] = dyb
        dm = jnp.zeros((tm, D_MODEL), F32)
        for j in range(N_DEV):
            cols = slice(j * FF_CHUNK, (j + 1) * FF_CHUNK)
            dact = _dot_nt(dyb, w2_ref[cols, :])
            du = (dact * (2.0 * jnp.maximum(u_ref[:, cols].astype(F32), 0.0))).astype(MM)
            du_ref[:, cols] = du
            dm = dm + _dot_nt(du, w1_ref[j])
        n3, r3 = _rms(h1_ref[...])
        dx, dg3 = _rms_bwd(dm, n3, r3, g3_ref[...])
        _accumulate(dg3_ref, dg3, i)
        dh1_ref[...] = dh2 + dx

    return pl.pallas_call(
        body, name="mlp_bwd", grid=(S // tm,),
        in_specs=[_rows(tm, D_MODEL), _rows(tm, D_MODEL), _rows(tm, D_MODEL), _rows(tm, D_FF),
                  _resident(), _resident(), _const((1, D_MODEL)), _const((1, D_MODEL))],
        out_specs=[_rows(tm, D_MODEL), _rows(tm, D_MODEL), _rows(tm, D_FF), _const((1, D_MODEL)),
                   _const((1, D_MODEL))],
        out_shape=[jax.ShapeDtypeStruct((S, D_MODEL), F32), jax.ShapeDtypeStruct((S, D_MODEL), MM),
                   jax.ShapeDtypeStruct((S, D_FF), MM), jax.ShapeDtypeStruct((1, D_MODEL), F32),
                   jax.ShapeDtypeStruct((1, D_MODEL), F32)],
        compiler_params=_params("arbitrary"),
    )(dh2, y, h1, u, w1, w2, g4, g3)


def _attn_out_bwd(dh1, mix, fox_o, swa_o, wout_fox, wout_swa, g2, head_rows, tm):
    S = dh1.shape[0]

    def body(dh1_ref, mix_ref, fo_ref, so_ref, wf_ref, ws_ref, g2_ref, er_ref,
             dmix_ref, dcat_ref, drow_ref, dswa_ref, dg2_ref):
        i = pl.program_id(0)
        n2, r2 = _rms(mix_ref[...])
        dmix, dg2 = _rms_bwd(dh1_ref[...], n2, r2, g2_ref[...])
        _accumulate(dg2_ref, dg2, i)
        dmb = dmix.astype(MM)
        dmix_ref[...] = dmb
        dfo = _dot_nt(dmb, wf_ref[...]).astype(MM)
        dso = _dot_nt(dmb, ws_ref[...]).astype(MM)
        dcat_ref[:, :D_ATT] = dfo
        dcat_ref[:, D_ATT:] = dso
        hi = lax.Precision.HIGHEST
        prod_f = dfo.astype(F32) * fo_ref[...].astype(F32)
        prod_s = dso.astype(F32) * so_ref[...].astype(F32)
        drow_ref[...] = lax.dot_general(er_ref[...], prod_f, NT, precision=hi, preferred_element_type=F32)
        dswa_ref[...] = lax.dot_general(er_ref[...], prod_s, NT, precision=hi, preferred_element_type=F32)

    return pl.pallas_call(
        body, name="attn_out_bwd", grid=(S // tm,),
        in_specs=[_rows(tm, D_MODEL), _rows(tm, D_MODEL), _rows(tm, D_ATT), _rows(tm, D_ATT), _resident(),
                  _resident(), _const((1, D_MODEL)), _resident()],
        out_specs=[_rows(tm, D_MODEL), _rows(tm, D_MODEL), pl.BlockSpec((N_HEADS, tm), lambda i: (0, i)),
                   pl.BlockSpec((N_HEADS, tm), lambda i: (0, i)), _const((1, D_MODEL))],
        out_shape=[jax.ShapeDtypeStruct((S, D_MODEL), MM), jax.ShapeDtypeStruct((S, D_MODEL), MM),
                   jax.ShapeDtypeStruct((N_HEADS, S), F32), jax.ShapeDtypeStruct((N_HEADS, S), F32),
                   jax.ShapeDtypeStruct((1, D_MODEL), F32)],
        compiler_params=_params("arbitrary"),
    )(dh1, mix, fox_o, swa_o, wout_fox, wout_swa, g2, head_rows)


def _fox_bwd(fqkv, dcat, lse_row3, d_row3, c_col, tq, tk, pairs_per_loop=2, after=None):
    S = fqkv.shape[0]
    n_blk = S // tk
    n_qblk = S // tq
    n_band = tk // tq

    def body(q_ref, k_ref, v_ref, do_ref, lse_ref, dd_ref, ck_ref, dq_ref, dk_ref, dv_ref, dc_ref, dcq_ref):
        kb = pl.program_id(0)

        @pl.when(kb == 0)
        def _():
            dq_ref[...] = jnp.zeros_like(dq_ref)
            dcq_ref[...] = jnp.zeros_like(dcq_ref)

        key = lax.broadcasted_iota(jnp.int32, (tk, tq), 0)
        qry = lax.broadcasted_iota(jnp.int32, (tk, tq), 1)
        low = _head_select((tk, 128), 0)
        for first in range(0, N_HEADS // 2, pairs_per_loop):
            pairs = range(first, first + pairs_per_loop)
            heads = [(pr, hh) for pr in pairs for hh in range(2)]
            kh, vh, ck = {}, {}, {}
            for pr in pairs:
                k2 = k_ref[:, pr * 128:(pr + 1) * 128]
                v2 = v_ref[:, pr * 128:(pr + 1) * 128]
                zero = jnp.zeros_like(k2)
                kh[pr, 0], kh[pr, 1] = jnp.where(low, k2, zero), jnp.where(low, zero, k2)
                vh[pr, 0], vh[pr, 1] = jnp.where(low, v2, zero), jnp.where(low, zero, v2)
                for hh in range(2):
                    ck[pr, hh] = ck_ref[:, 2 * pr + hh:2 * pr + hh + 1]

            def block(qb, carry, band, pairs=pairs, kh=kh, vh=vh, ck=ck):
                rows = pl.ds(pl.multiple_of(qb * tq, tq), tq)
                k1 = tk if band is None else (band + 1) * tq
                out = []
                it = iter(carry)
                for pr in pairs:
                    lanes = slice(pr * 128, (pr + 1) * 128)
                    q2 = q_ref[rows, lanes]
                    do2 = do_ref[rows, lanes]
                    dq = None
                    for hh in range(2):
                        h = 2 * pr + hh
                        dk, dv, dc = next(it)
                        s_t = _dot_nt(kh[pr, hh][:k1], q2) - ck[pr, hh][:k1]
                        p_t = jnp.exp(s_t - lse_ref[h, pl.ds(qb, 1), :])
                        if band is not None:
                            p_t = jnp.where(qry[:k1] + band * tq >= key[:k1], p_t, 0.0)
                        ds_t = p_t * (_dot_nt(vh[pr, hh][:k1], do2) - dd_ref[h, pl.ds(qb, 1), :])
                        dsb = ds_t.astype(MM)
                        dv_new = dv[:k1] + _dot(p_t.astype(MM), do2)
                        dk_new = dk[:k1] + _dot(dsb, q2)
                        dc_new = dc[:k1] - jnp.sum(ds_t, axis=1, keepdims=True)
                        if k1 < tk:
                            dv_new = jnp.concatenate([dv_new, dv[k1:]], axis=0)
                            dk_new = jnp.concatenate([dk_new, dk[k1:]], axis=0)
                            dc_new = jnp.concatenate([dc_new, dc[k1:]], axis=0)
                        part = _dot_tn(dsb, kh[pr, hh][:k1])
                        dq = part if dq is None else dq + part
                        dcq_ref[h, pl.ds(qb, 1), :] += jnp.sum(ds_t, axis=0, keepdims=True)
                        out.append((dk_new, dv_new, dc_new))
                    dq_ref[rows, lanes] += dq
                return tuple(out)

            carry = tuple((jnp.zeros((tk, 128), F32), jnp.zeros((tk, 128), F32), jnp.zeros((tk, 1), F32))
                          for _ in heads)
            for band in range(n_band):
                carry = block(kb * n_band + band, carry, band=band)
            carry = lax.fori_loop((kb + 1) * n_band, n_qblk, functools.partial(block, band=None), carry)
            grads = dict(zip(heads, carry))
            for pr in pairs:
                lanes = slice(pr * 128, (pr + 1) * 128)
                dk_ref[:, lanes] = jnp.where(low, grads[pr, 0][0], grads[pr, 1][0]).astype(MM)
                dv_ref[:, lanes] = jnp.where(low, grads[pr, 0][1], grads[pr, 1][1]).astype(MM)
                for hh in range(2):
                    dc_ref[:, 2 * pr + hh:2 * pr + hh + 1] = grads[pr, hh][2]

        @pl.when(kb == n_blk - 1)
        def _():
            dq_ref[...] = dq_ref[...] * Q_SCALE

    body, in_specs, operands = _run_after(
        after, body,
        [pl.BlockSpec((S, D_ATT), lambda i: (0, 0)), pl.BlockSpec((tk, D_ATT), lambda i: (i, 1)),
         pl.BlockSpec((tk, D_ATT), lambda i: (i, 2)), pl.BlockSpec((S, D_ATT), lambda i: (0, 0)),
         _resident(), _resident(), _rows(tk, N_HEADS)],
        (fqkv, fqkv, fqkv, dcat, lse_row3, d_row3, c_col))
    return pl.pallas_call(
        body, name="fox_bwd", grid=(n_blk,), in_specs=in_specs,
        out_specs=[_const((S, D_ATT)), _rows(tk, D_ATT), _rows(tk, D_ATT), _rows(tk, N_HEADS),
                   _const((N_HEADS, n_qblk, tq))],
        out_shape=[jax.ShapeDtypeStruct((S, D_ATT), F32), jax.ShapeDtypeStruct((S, D_ATT), MM),
                   jax.ShapeDtypeStruct((S, D_ATT), MM), jax.ShapeDtypeStruct((S, N_HEADS), F32),
                   jax.ShapeDtypeStruct((N_HEADS, n_qblk, tq), F32)],
        compiler_params=_params("arbitrary"),
    )(*operands)


def _swa_bwd(sqkv, dcat, biasm, sinks_slot, bucket, lse, d_col, after=None):
    S = sqkv.shape[0]
    n_blk = S // WINDOW

    def body(q_ref, kp_ref, kc_ref, vp_ref, vc_ref, do_ref, bias_ref, sink_ref, bk_ref, lse_ref, dd_ref,
             dq_ref, dk_ref, dv_ref, drb_ref, dsink_ref, ds_acc):
        n = pl.program_id(0)

        @pl.when(n == 0)
        def _():
            dk_ref[...] = jnp.zeros_like(dk_ref)
            dv_ref[...] = jnp.zeros_like(dv_ref)
            ds_acc[...] = jnp.zeros_like(ds_acc)
            dsink_ref[...] = jnp.zeros_like(dsink_ref)

        no_prev = jnp.where(n > 0, 0.0, NEG)
        prev = pl.ds(pl.multiple_of(jnp.maximum(n - 1, 0) * WINDOW, WINDOW), WINDOW)
        cur = pl.ds(pl.multiple_of(n * WINDOW, WINDOW), WINDOW)
        lane8 = lax.broadcasted_iota(jnp.int32, (1, N_HEADS), 1)
        dkp = jnp.zeros((WINDOW, D_KV), F32)
        dkc = jnp.zeros((WINDOW, D_KV), F32)
        dvp = jnp.zeros((WINDOW, D_KV), F32)
        dvc = jnp.zeros((WINDOW, D_KV), F32)
        dsink = jnp.zeros((1, N_HEADS), F32)
        low = _head_select((WINDOW, 128), 0)
        zero = jnp.zeros((WINDOW, 128), MM)
        dqs = []
        for g in range(2):
            sel = low if g == 0 else jnp.logical_not(low)
            qg = _stack4(lambda j: jnp.where(sel, q_ref[:, j * 128:(j + 1) * 128], zero))
            dog = _stack4(lambda j: jnp.where(sel, do_ref[:, j * 128:(j + 1) * 128], zero))
            lse_g = _stack4(lambda j: lse_ref[:, 2 * j + g:2 * j + g + 1])
            dd = _stack4(lambda j: dd_ref[:, 2 * j + g:2 * j + g + 1])
            sink = _stack4(lambda j: jnp.full((WINDOW, 1), sink_ref[2 * j + g], F32))
            pp = jnp.exp(_dot_nt(qg, kp_ref[...]) + _stack4(lambda j: bias_ref[2 * j + g, :, :WINDOW]) + no_prev - lse_g)
            pc = jnp.exp(_dot_nt(qg, kc_ref[...]) + _stack4(lambda j: bias_ref[2 * j + g, :, WINDOW:]) - lse_g)
            sink_term = jnp.exp(sink - lse_g) * dd
            dsp = pp * (_dot_nt(dog, vp_ref[...]) - dd)
            dsc = pc * (_dot_nt(dog, vc_ref[...]) - dd)
            for j in range(4):
                rows = slice(j * WINDOW, (j + 1) * WINDOW)
                dsink = dsink + jnp.where(lane8 == 2 * j + g, -jnp.sum(sink_term[rows]), 0.0)
                ds_acc[2 * j + g, :, :WINDOW] += dsp[rows]
                ds_acc[2 * j + g, :, WINDOW:] += dsc[rows]
            dspb, dscb = dsp.astype(MM), dsc.astype(MM)
            dqs.append(_dot(dspb, kp_ref[...]) + _dot(dscb, kc_ref[...]))
            dkp = dkp + _dot_tn(dspb, qg)
            dkc = dkc + _dot_tn(dscb, qg)
            dvp = dvp + _dot_tn(pp.astype(MM), dog)
            dvc = dvc + _dot_tn(pc.astype(MM), dog)
        for j in range(4):
            rows = slice(j * WINDOW, (j + 1) * WINDOW)
            dq_ref[:, j * 128:(j + 1) * 128] = (jnp.where(low, dqs[0][rows], dqs[1][rows]) * Q_SCALE).astype(MM)
        dk_ref[prev, :] += dkp
        dk_ref[cur, :] += dkc
        dv_ref[prev, :] += dvp
        dv_ref[cur, :] += dvc
        dsink_ref[...] += dsink

        @pl.when(n == n_blk - 1)
        def _():
            bk = bk_ref[...]
            rb = lax.broadcasted_iota(jnp.int32, (N_BUCKETS, N_HEADS), 0)
            cb = lax.broadcasted_iota(jnp.int32, (N_BUCKETS, N_HEADS), 1)
            out = jnp.zeros((N_BUCKETS, N_HEADS), F32)
            for s in range(N_HEADS):
                acc = ds_acc[s]
                for b in range(N_BUCKETS):
                    out = out + jnp.where((rb == b) & (cb == s), jnp.sum(jnp.where(bk == b, acc, 0.0)), 0.0)
            drb_ref[...] = out

    do_spec = pl.BlockSpec((WINDOW, D_ATT), lambda n: (n, 1))
    body, in_specs, operands = _run_after(
        after, body, _swa_specs(S) + [do_spec, _resident(), pl.BlockSpec(memory_space=pltpu.SMEM), _resident(),
                                      _rows(WINDOW, N_HEADS), _rows(WINDOW, N_HEADS)],
        (sqkv, sqkv, sqkv, sqkv, sqkv, dcat, biasm, sinks_slot, bucket, lse, d_col))
    return pl.pallas_call(
        body, name="swa_bwd", grid=(n_blk,), in_specs=in_specs,
        out_specs=[_rows(WINDOW, D_ATT), _const((S, D_KV)), _const((S, D_KV)), _const((N_BUCKETS, N_HEADS)),
                   _const((1, N_HEADS))],
        out_shape=[jax.ShapeDtypeStruct((S, D_ATT), MM), jax.ShapeDtypeStruct((S, D_KV), F32),
                   jax.ShapeDtypeStruct((S, D_KV), F32), jax.ShapeDtypeStruct((N_BUCKETS, N_HEADS), F32),
                   jax.ShapeDtypeStruct((1, N_HEADS), F32)],
        scratch_shapes=[pltpu.VMEM((N_HEADS, WINDOW, 2 * WINDOW), F32)],
        compiler_params=_params("arbitrary"),
    )(*operands)


def _weight_grad_pieces(pieces, b, name, tk):
    S, N = b.shape
    wide = [p for p in pieces if p.shape[1] % tk == 0]
    narrow = pieces[len(wide):]
    assert sum(p.shape[1] for p in narrow) == tk
    counts = [p.shape[1] // tk for p in wide]
    firsts = [sum(counts[:j]) for j in range(len(wide))]
    last = sum(counts)

    def body(*refs):
        b_ref, out_ref = refs[len(pieces)], refs[len(pieces) + 1]
        i = pl.program_id(0)

        def whole_steps(ref, first, count):
            @pl.when((i >= first) & (i < first + count))
            def _():
                out_ref[...] = _dot_tn(ref[...].astype(MM), b_ref[...]).astype(MM)

        for ref, first, count in zip(refs[:len(wide)], firsts, counts):
            whole_steps(ref, first, count)

        @pl.when(i == last)
        def _():
            row = 0
            for ref in refs[len(wide):len(pieces)]:
                k = ref.shape[1]
                out_ref[row:row + k] = _dot_tn(ref[...].astype(MM), b_ref[...]).astype(MM)
                row += k

    def steps_of(first, count):
        return pl.BlockSpec((S, tk), lambda i: (0, jnp.clip(i - first, 0, count - 1)))

    return pl.pallas_call(
        body, name=name, grid=(last + 1,),
        in_specs=[steps_of(first, count) for first, count in zip(firsts, counts)]
        + [pl.BlockSpec((S, p.shape[1]), lambda i: (0, 0)) for p in narrow] + [_resident()],
        out_specs=pl.BlockSpec((tk, N), lambda i: (i, 0)), out_shape=jax.ShapeDtypeStruct(((last + 1) * tk, N), MM),
        compiler_params=_params("parallel"),
    )(*pieces, b)


def _pre_attn_bwd(x, dh1, dz, dff_t, win_t, wt_rest, g1, tm, after=None):
    S = x.shape[0]

    def body(x_ref, dh1_ref, dq_ref, dk_ref, dv_ref, dsq_ref, dsk_ref, dsv_ref, dff_ref, wf_ref, wr_ref, g1_ref,
             dx_ref, dg1_ref):
        i = pl.program_id(0)
        dz_fox = jnp.concatenate([dq_ref[...].astype(MM), dk_ref[...], dv_ref[...]], axis=1)
        dz_swa = jnp.concatenate([dsq_ref[...], dsk_ref[...].astype(MM), dsv_ref[...].astype(MM)], axis=1)
        da = (_dot(dz_fox, wf_ref[...]) + _dot(dz_swa, wr_ref[WT_SQ:WT_REST])
              + _dot_tn(dff_ref[...].astype(MM), wr_ref[0:WT_SQ]))
        n1, r1 = _rms(x_ref[...])
        dx, dg1 = _rms_bwd(da, n1, r1, g1_ref[...])
        _accumulate(dg1_ref, dg1, i)
        dx_ref[...] = dh1_ref[...] + dx

    body, in_specs, operands = _run_after(
        after, body,
        [_rows(tm, D_MODEL), _rows(tm, D_MODEL), *[_rows(tm, d.shape[1]) for d in dz],
         pl.BlockSpec((16, tm), lambda i: (0, i)), _const((WT_FOX, D_MODEL)), _resident(), _const((1, D_MODEL))],
        (x, dh1, *dz, dff_t, win_t, wt_rest, g1))
    return pl.pallas_call(
        body, name="pre_attn_bwd", grid=(S // tm,), in_specs=in_specs,
        out_specs=[_rows(tm, D_MODEL), _const((1, D_MODEL))],
        out_shape=[jax.ShapeDtypeStruct((S, D_MODEL), F32), jax.ShapeDtypeStruct((1, D_MODEL), F32)],
        compiler_params=_params("arbitrary"),
    )(*operands)


def _weight_grad(a, b, name, tk, n_chunks=1, relu2=False):
    S, K = a.shape
    N = b.shape[1]
    cn = N // n_chunks

    def body(a_ref, b_ref, out_ref):
        av = a_ref[...]
        if relu2:
            av = jnp.square(jnp.maximum(av.astype(F32), 0.0))
        av = av.astype(MM)
        for j in range(n_chunks):
            val = _dot_tn(av, b_ref[:, j * cn:(j + 1) * cn].astype(MM)).astype(MM)
            if n_chunks > 1:
                out_ref[j] = val
            else:
                out_ref[...] = val

    if n_chunks > 1:
        out_spec = pl.BlockSpec((n_chunks, tk, cn), lambda i: (0, i, 0))
        out_shape = jax.ShapeDtypeStruct((n_chunks, K, cn), MM)
    else:
        out_spec = pl.BlockSpec((tk, N), lambda i: (i, 0))
        out_shape = jax.ShapeDtypeStruct((K, N), MM)
    return pl.pallas_call(
        body, name=name, grid=(K // tk,),
        in_specs=[pl.BlockSpec((S, tk), lambda i: (0, i)), _resident()],
        out_specs=out_spec, out_shape=out_shape, compiler_params=_params("parallel"),
    )(a, b)


def _weight_grad_two(a1, a2, b, name, tk):
    S, K1 = a1.shape
    K2 = a2.shape[1]
    N = b.shape[1]
    n1 = K1 // tk

    def body(a1_ref, a2_ref, b_ref, out_ref):
        av = jnp.where(pl.program_id(0) < n1, a1_ref[...], a2_ref[...])
        out_ref[...] = _dot_tn(av, b_ref[...]).astype(MM)

    return pl.pallas_call(
        body, name=name, grid=((K1 + K2) // tk,),
        in_specs=[pl.BlockSpec((S, tk), lambda i: (0, jnp.minimum(i, n1 - 1))),
                  pl.BlockSpec((S, tk), lambda i: (0, jnp.maximum(i - n1, 0))), _resident()],
        out_specs=pl.BlockSpec((tk, N), lambda i: (i, 0)), out_shape=jax.ShapeDtypeStruct((K1 + K2, N), MM),
        compiler_params=_params("parallel"),
    )(a1, a2, b)


def _place():
    return lax.axis_index("x"), lax.axis_index("y"), lax.axis_index("c")


def _all_gather_sequencer(stacks, name, collective_id):
    refs = [jax.new_ref(s, memory_space=pltpu.MemorySpace.HBM) for s in stacks]
    n = len(refs)

    @pl.kernel(mesh=plsc.ScalarSubcoreMesh(axis_name="sequencer", num_cores=1), name=name,
               scratch_types=(pltpu.SemaphoreType.DMA((7 * n,)), pltpu.SemaphoreType.DMA((7 * n,))),
               compiler_params=pltpu.CompilerParams(collective_id=collective_id))
    def launch(send_sems, recv_sems):
        x, y, c = _place()
        sibling = (x, y, 1 - c)
        chips = [(1 - x, y), (x, 1 - y), (1 - x, 1 - y)]
        peers = [sibling] + [(px, py, c) for px, py in chips]
        barrier = pltpu.get_barrier_semaphore()
        for peer in peers:
            pl.semaphore_signal(barrier, inc=1, device_id=peer, device_id_type=MESH)
        pl.semaphore_wait(barrier, len(peers))

        def copy(a, k, block, to):
            px, py, pc = block
            slot = refs[a].at[4 * px + 2 * py + pc]
            return _remote(slot, slot, send_sems, recv_sems, 7 * a + k, to)

        first = [copy(a, k, (x, y, c), peer) for a in range(n) for k, peer in enumerate(peers)]
        for cp in first:
            cp.start()
        passed = []
        for j, (px, py) in enumerate(chips):
            for a in range(n):
                copy(a, 1 + j, (px, py, c), sibling).wait_recv()
                passed.append(copy(a, 4 + j, (px, py, c), sibling))
                passed[-1].start()
        for a in range(n):
            copy(a, 0, (x, y, 1 - c), sibling).wait_recv()
            for j, (px, py) in enumerate(chips):
                copy(a, 4 + j, (px, py, 1 - c), sibling).wait_recv()
        for cp in first + passed:
            cp.wait_send()

    launch()
    return [ref[...] for ref in refs]


def _chip_sums(grads, others, name):
    n = len(grads)

    def body(c_ref, *refs):
        for g_ref, o_ref, out_ref in zip(refs[:n], refs[n:2 * n], refs[2 * n:]):
            out_ref[...] = (g_ref[...].astype(F32) + o_ref[...].astype(F32)).astype(out_ref.dtype)

    own = [pl.BlockSpec((None, None) + g.shape[2:], lambda k, c_ref: (k, c_ref[0], 0, 0)) for g in grads]
    chip = [pl.BlockSpec((None,) + g.shape[2:], lambda k, c_ref: (k, 0, 0)) for g in grads]
    return pl.pallas_call(
        body, name=name,
        grid_spec=pltpu.PrefetchScalarGridSpec(num_scalar_prefetch=1, grid=(4,), in_specs=own + chip, out_specs=chip),
        out_shape=[jax.ShapeDtypeStruct((4,) + g.shape[2:], MM) for g in grads],
        compiler_params=_params("parallel"),
    )(lax.axis_index("c").astype(jnp.int32).reshape(1), *grads, *others)


HBM_SPEC = pl.BlockSpec(memory_space=pltpu.HBM)
SEM_SPEC = pl.BlockSpec(memory_space=pltpu.SEMAPHORE)
DATAFLOW = pltpu.SideEffectType.DATAFLOW_SIDE_EFFECTING


def _exchange_start(name, arrays, n_copies, plan):
    n = len(arrays)

    def body(*refs):
        send_sems, recv_sems, token = refs[n], refs[n + 1], refs[2 * n + 2]
        for cp in plan(refs[:n], send_sems, recv_sems):
            cp.start()
        token[...] = jnp.zeros_like(token)

    out = pl.pallas_call(
        body, name=name,
        out_shape=(pltpu.SemaphoreType.DMA((n_copies,)), pltpu.SemaphoreType.DMA((n_copies,)),
                   *[pltpu.HBM(a.shape, a.dtype) for a in arrays], jax.ShapeDtypeStruct((1, D_MODEL), F32)),
        in_specs=[HBM_SPEC] * n,
        out_specs=(SEM_SPEC, SEM_SPEC, *[HBM_SPEC] * n, pl.BlockSpec(memory_space=pltpu.VMEM)),
        input_output_aliases={i: 2 + i for i in range(n)},
        compiler_params=pltpu.CompilerParams(has_side_effects=DATAFLOW),
    )(*[pltpu.with_memory_space_constraint(a, pltpu.HBM) for a in arrays])
    return (out[0], out[1]), list(out[2:2 + n]), out[2 + n]


def _exchange_wait(name, arrays, sems, after, plan):
    n = len(arrays)
    after = list(after) if isinstance(after, (list, tuple)) else [after]

    def body(*refs):
        send_sems, recv_sems = refs[n], refs[n + 1]
        for cp in plan(refs[:n], send_sems, recv_sems):
            cp.wait_send()
            cp.wait_recv()

    out = pl.pallas_call(
        body, name=name, out_shape=[pltpu.HBM(a.shape, a.dtype) for a in arrays],
        in_specs=[HBM_SPEC] * n + [SEM_SPEC, SEM_SPEC] + [pl.BlockSpec(memory_space=pl.ANY)] * len(after),
        out_specs=[HBM_SPEC] * n, input_output_aliases={i: i for i in range(n)},
        compiler_params=pltpu.CompilerParams(has_side_effects=DATAFLOW),
    )(*arrays, sems[0], sems[1], *after)
    return list(out)


def _remote(src, dst, send_sems, recv_sems, k, to):
    return pltpu.make_async_remote_copy(src_ref=src, dst_ref=dst, send_sem=send_sems.at[k], recv_sem=recv_sems.at[k],
                                        device_id=to, device_id_type=MESH)


def _plan_gather_near(refs, send_sems, recv_sems):
    x, y, c = _place()
    me = 4 * x + 2 * y + c
    peers = [(x, y, 1 - c), (1 - x, y, c), (x, 1 - y, c)]
    copies = [_remote(ref.at[me], ref.at[me], send_sems, recv_sems, 3 * a + k, peer)
              for a, ref in enumerate(refs) for k, peer in enumerate(peers)]
    return copies + [_remote(refs[0].at[me], refs[0].at[me], send_sems, recv_sems, 3 * len(refs),
                             (1 - x, 1 - y, c))]


def _plan_gather_far(refs, send_sems, recv_sems):
    x, y, c = _place()
    south = 1 - c
    from_x, from_y = x + south - 2 * x * south, y + c - 2 * y * c
    to_x, to_y = x + c - 2 * x * c, y + south - 2 * y * south
    carried = 4 * from_x + 2 * from_y + c
    diagonal = 4 * (1 - x) + 2 * (1 - y) + c
    copies = []
    for a, ref in enumerate(refs):
        if a == 0:
            copies.append(_remote(ref.at[diagonal], ref.at[diagonal], send_sems, recv_sems, 0, (x, y, 1 - c)))
        else:
            copies.append(_remote(ref.at[carried], ref.at[carried], send_sems, recv_sems, 3 * a, (to_x, to_y, c)))
        for k, (px, py) in enumerate([(1 - x, y), (x, 1 - y)]):
            block = 4 * px + 2 * py + c
            copies.append(_remote(ref.at[block], ref.at[block], send_sems, recv_sems, 3 * a + 1 + k, (x, y, 1 - c)))
    return copies


def _plan_gather_last(refs, send_sems, recv_sems):
    x, y, c = _place()
    block = 4 * (1 - x) + 2 * (1 - y) + c
    return [_remote(ref.at[block], ref.at[block], send_sems, recv_sems, a, (x, y, 1 - c))
            for a, ref in enumerate(refs)]


def _plan_in_chip(refs, send_sems, recv_sems):
    n = len(refs) // 2
    x, y, c = _place()
    return [_remote(refs[a].at[:, 1 - c], refs[n + a], send_sems, recv_sems, a, (x, y, 1 - c)) for a in range(n)]


def _plan_between_chips(refs, send_sems, recv_sems):
    n = len(refs) // 2
    x, y, c = _place()
    chips = [(1 - x, y), (x, 1 - y), (1 - x, 1 - y)]
    return [_remote(refs[a].at[2 * px + py], refs[n + a].at[2 * x + y], send_sems, recv_sems, 3 * a + k, (px, py, c))
            for a in range(n) for k, (px, py) in enumerate(chips)]


def _plan_late_between(refs, send_sems, recv_sems):
    sums, land, small = refs
    x, y, c = _place()
    me = 4 * x + 2 * y + c
    copies = _plan_between_chips([sums, land], send_sems, recv_sems)
    peers = [(x ^ dx, y ^ dy, c ^ dc) for dx in range(2) for dy in range(2) for dc in range(2) if dx + dy + dc]
    return copies + [_remote(small.at[me], small.at[me], send_sems, recv_sems, 3 + k, peer)
                     for k, peer in enumerate(peers)]


def _adamw_math(w, g, m, v):
    m = ADAM_B1 * m + (1.0 - ADAM_B1) * g
    v = ADAM_B2 * v + (1.0 - ADAM_B2) * jnp.square(g)
    m_hat = m / (1.0 - ADAM_B1 ** ADAM_STEP)
    v_hat = v / (1.0 - ADAM_B2 ** ADAM_STEP)
    delta = -ADAM_LR * (m_hat / (jnp.sqrt(v_hat) + ADAM_EPS) + ADAM_WD * w)
    return delta, m, v


def _adamw_small(parts, w, m, v):
    n_parts = parts.shape[0]
    n_rows = len(SMALL_ROWS)
    names = SMALL_ROWS + ("b_forget", "swa_sinks", "rel_bias")
    shapes = [(1, D_MODEL)] * n_rows + [(1, N_HEADS), (1, N_HEADS), (N_HEADS, N_BUCKETS)]

    def body(p_ref, w_ref, m_ref, v_ref, *outs):
        g = p_ref[0]
        for k in range(1, n_parts):
            g = g + p_ref[k]
        delta, m_new, v_new = _adamw_math(w_ref[...], g, m_ref[...], v_ref[...])
        for kind, val in enumerate((g, delta, m_new, v_new)):
            o = outs[kind * len(names):(kind + 1) * len(names)]
            for i in range(n_rows):
                o[i][...] = val[i:i + 1]
            misc = val[n_rows:n_rows + 1]
            o[n_rows][...] = misc[:, :N_HEADS]
            o[n_rows + 1][...] = misc[:, N_HEADS:2 * N_HEADS]
            for h in range(N_HEADS):
                first = 2 * N_HEADS + h * N_BUCKETS
                o[n_rows + 2][h:h + 1, :] = misc[:, first:first + N_BUCKETS]
        outs[-1][...] = g[n_rows + 1:n_rows + 2, 0:1]

    out = pl.pallas_call(
        body, name="adamw_small", in_specs=[_resident()] * 4, out_specs=[_resident()] * (4 * len(names) + 1),
        out_shape=[jax.ShapeDtypeStruct(s, F32) for s in shapes * 4] + [jax.ShapeDtypeStruct((1, 1), F32)],
        compiler_params=_params(),
    )(parts, w, m, v)
    kinds =[dict(zip(names, out[kind * len(names):(kind + 1) * len(names)])) for kind in range(4)]
    kinds[0]["loss"] = out[-1]
    return kinds


def _adamw_chips(parts, sums, w, m, v, name):
    _, r, cdim = parts.shape
    tr = 256 if r % 256 == 0 else r
    apart = w.ndim == 3

    def body(chip_ref, p_ref, own_ref, w_ref, m_ref, v_ref, g_out, d_out, m_out, v_out):
        g = None
        for k in range(4):
            term = jnp.where(chip_ref[0] == k, own_ref[...], p_ref[k]).astype(F32)
            g = term if g is None else g + term
        get = (lambda ref: ref[:, 0, :]) if apart else (lambda ref: ref[...])
        delta, m_new, v_new = _adamw_math(get(w_ref), g, get(m_ref), get(v_ref))
        for ref, val in ((g_out, g), (d_out, delta), (m_out, m_new), (v_out, v_new)):
            if apart:
                ref[:, 0, :] = val
            else:
                ref[...] = val

    if apart:
        blk = pl.BlockSpec((tr, 1, cdim), lambda i, chip: (i, 0, 0))
        shape = (r, 1, cdim)
    else:
        blk = pl.BlockSpec((tr, cdim), lambda i, chip: (i, 0))
        shape = (r, cdim)
    my_chip = (2 * lax.axis_index("x") + lax.axis_index("y")).astype(jnp.int32).reshape(1)
    return pl.pallas_call(
        body, name=name,
        grid_spec=pltpu.PrefetchScalarGridSpec(
            num_scalar_prefetch=1, grid=(r // tr,),
            in_specs=[pl.BlockSpec((4, tr, cdim), lambda i, chip: (0, i, 0)),
                      pl.BlockSpec((None, tr, cdim), lambda i, chip: (chip[0], i, 0)), blk, blk, blk],
            out_specs=[blk] * 4),
        out_shape=[jax.ShapeDtypeStruct(shape, F32)] * 4,
        compiler_params=_params("parallel"),
    )(my_chip, parts, sums, w, m, v)


class _NoExchange:
    def __init__(self, weights):
        self.weights = weights

    def before_pre_attn(self):
        return None

    def after_fox_fwd(self, fox_o):
        return None

    def after_attention(self, swa_o):
        return self.weights[0], None

    def before_mlp(self, m):
        return self.weights[1:]

    def after_early_grads(self, grads):
        return None

    def after_swa_bwd(self, dsq):
        return None

    def after_w_in_grad(self, d_win):
        return None


def _slot_order(t, axis):
    shp = t.shape
    t = t.reshape(shp[:axis] + (2, 4, shp[axis] // N_HEADS) + shp[axis + 1:])
    return jnp.swapaxes(t, axis, axis + 1).reshape(shp)


def _head_order(t, axis):
    shp = t.shape
    t = t.reshape(shp[:axis] + (4, 2, shp[axis] // N_HEADS) + shp[axis + 1:])
    return jnp.swapaxes(t, axis, axis + 1).reshape(shp)


def _forward_backward(x, p, target, win_t, hooks, b_forget, rel_bias, sinks, g1, g2, g3, g4, g5):
    S = x.shape[0]
    tm = 512
    tm_mlp = 512
    t = 256
    q0 = 3 * D_ATT + N_HEADS
    win_t = win_t.reshape(D_IN, D_MODEL)
    wt_rest = jnp.concatenate(
        [win_t[WT_FOX:q0], jnp.zeros((8, D_MODEL), MM), _slot_order(win_t[q0:q0 + D_ATT], 0), win_t[q0 + D_ATT:]],
        axis=0)
    bcol = jnp.pad(b_forget.reshape(N_HEADS, 1), ((0, 8), (0, 0)))
    rel_bias_slot = rel_bias[:, np.array(SLOT_HEAD)]
    sinks_slot = sinks.reshape(N_HEADS)[np.array(SLOT_HEAD)]
    bucket = jnp.asarray(_swa_bucket_map())

    a, fqkv, sqkv, fft = _pre_attn(x, g1, win_t, wt_rest, tm, after=hooks.before_pre_attn())
    c_row = _forget_cumsum(fft, bcol)
    c_col = c_row[:N_HEADS].T
    c_row3 = c_row[:N_HEADS].reshape(N_HEADS, S // t, t)
    fox_o, fox_lse = _fox_fwd(fqkv, c_row3, tq=512, tk=t)
    biasm = _swa_bias(rel_bias_slot, bucket)
    swa_o, swa_lse = _swa_fwd(sqkv, biasm, sinks_slot, after=hooks.after_fox_fwd(fox_o))
    wout, started = hooks.after_attention(swa_o)
    wout_fox = wout[:D_ATT]
    wout_swa = _slot_order(wout[D_ATT:], 0)
    mix, h1, m = _post_attn(x, fox_o, swa_o, wout_fox, wout_swa, g2, g3, tm, after=started)
    w1, w2, wple, wg = hooks.before_mlp(m)
    u, y, h2 = _mlp_fwd(m, h1, w1, w2, g4, tm_mlp)
    dh2, dpe, dgl, dg5, loss = _ple_loss(h2, p, target, wg, wple, g5, tm)

    d_wple = _weight_grad(p, dpe, "grad_w_ple", tk=D_PLE, n_chunks=N_DEV)
    d_wg = _weight_grad(h2, dgl, "grad_w_ple_gate", tk=256)
    dh1, dy, du, dg4, dg3 = _mlp_bwd(dh2, y, h1, u, w1, w2, g4, g3, tm)
    d_w2 = _weight_grad(u, dy, "grad_w_ff2", tk=256, relu2=True)
    d_w1 = _weight_grad(m, du, "grad_w_ff1", tk=256, n_chunks=N_DEV)
    head = np.arange(D_ATT) // HEAD_DIM
    head_rows = jnp.asarray((head[None, :] == np.arange(N_HEADS)[:, None]).astype(np.float32))
    dmix, dcat, d_row, d_swa, dg2 = _attn_out_bwd(dh1, mix, fox_o, swa_o, wout_fox, wout_swa, g2, head_rows, tm)
    d_col = d_swa.T
    d_wout = _weight_grad_two(fox_o, swa_o, dmix, "grad_w_out", tk=256)
    d_wout = jnp.concatenate([d_wout[:D_ATT], _head_order(d_wout[D_ATT:], 0)], axis=0)
    d_wout = d_wout.reshape(N_DEV, D_MODEL // N_DEV, D_MODEL)
    early = dict(w_ff1=d_w1, w_ff2=d_w2.reshape(N_DEV, FF_CHUNK, D_MODEL), w_ple=d_wple,
                 w_ple_gate=d_wg.reshape(N_DEV, D_MODEL // N_DEV, D_MODEL), w_out=d_wout)

    dsq, dsk, dsv, d_rb_slot, d_sink_slot = _swa_bwd(sqkv, dcat, biasm, sinks_slot, bucket, swa_lse, d_col,
                                                     after=hooks.after_early_grads(early))
    lse_row3 = fox_lse.T.reshape(N_HEADS, S // t, t)
    d_row3 = d_row.reshape(N_HEADS, S // t, t)
    dq_fox, dk_fox, dv_fox, dc_col, dcq = _fox_bwd(fqkv, dcat, lse_row3, d_row3, c_col, tq=t, tk=512,
                                                  after=hooks.after_swa_bwd(dsq))
    dc_row = jnp.pad(dc_col.T + dcq.reshape(N_HEADS, S), ((0, 8), (0, 0)))
    dff_t, db, d_wff_t = _forget_bwd(dc_row, fft, bcol, a)
    dz = [dq_fox, dk_fox, dv_fox, dsq, dsk, dsv]
    d_wmain = _weight_grad_pieces(dz, a, "grad_w_in", tk=256)

    sq0 = 3 * D_ATT
    d_win = jnp.concatenate(
        [d_wmain[:sq0], d_wff_t[:N_HEADS].astype(MM), _head_order(d_wmain[sq0:sq0 + D_ATT], 0),
         d_wmain[sq0 + D_ATT:]], axis=0)
    d_win = d_win.reshape(N_DEV, D_IN // N_DEV, D_MODEL)
    grad_x, dg1 = _pre_attn_bwd(x, dh1, dz, dff_t, win_t, wt_rest, g1, tm, after=hooks.after_w_in_grad(d_win))
    big = dict(early, w_in=d_win)
    small = dict(b_forget=db[:N_HEADS].reshape(1, N_HEADS), rel_bias=d_rb_slot[:, np.array(HEAD_SLOT)],
                 swa_sinks=d_sink_slot[:, np.array(HEAD_SLOT)], g_attn_pre=dg1, g_attn_post=dg2, g_ff_pre=dg3,
                 g_ff_post=dg4, g_ple_post=dg5)
    return loss, grad_x, big, small


BIG = ("w_in", "w_out", "w_ff1", "w_ff2", "w_ple", "w_ple_gate")
SMALL_ROWS = ("g_attn_pre", "g_attn_post", "g_ff_pre", "g_ff_post", "g_ple_post")
WEIGHTS =("w_in", "b_forget", "w_out", "rel_bias", "swa_sinks", "g_attn_pre", "g_attn_post", "w_ff1", "w_ff2",
           "g_ff_pre", "g_ff_post", "w_ple", "w_ple_gate", "g_ple_post")


EARLY = ("w_ff1", "w_ff2", "w_ple", "w_ple_gate", "w_out")


class _Overlap:
    def __init__(self, later):
        self.later = later

    def before_pre_attn(self):
        self.near_sems, self.later, token = _exchange_start("gather_near_start", self.later, 3 * 5 + 1, _plan_gather_near)
        return token

    def after_fox_fwd(self, fox_o):
        later = _exchange_wait("gather_near_wait", self.later, self.near_sems, fox_o, _plan_gather_near)
        self.far_sems, self.later, token = _exchange_start("gather_far_start", later, 3 * 5, _plan_gather_far)
        return token

    def after_attention(self, swa_o):
        later = _exchange_wait("gather_far_wait", self.later, self.far_sems, swa_o, _plan_gather_far)
        self.last_sems, self.later, token = _exchange_start("gather_last_start", later[1:], 4, _plan_gather_last)
        return later[0].reshape(D_MODEL, D_MODEL), token

    def before_mlp(self, m):
        w1_g, w2_g, wple_g, wg_g = _exchange_wait("gather_last_wait", self.later, self.last_sems, m, _plan_gather_last)
        return (w1_g, w2_g.reshape(D_FF, D_MODEL), jnp.moveaxis(wple_g, 0, 1).reshape(D_PLE, D_MODEL),
                wg_g.reshape(D_MODEL, D_MODEL))

    def after_early_grads(self, grads):
        views = [grads[k].reshape((4, 2) + grads[k].shape[1:]) for k in EARLY]
        lands = [lax.empty((4,) + grads[k].shape[1:], MM) for k in EARLY]
        self.in_chip_sems, self.in_chip, token = _exchange_start("grads_in_chip_start", views + lands, len(EARLY),
                                                                 _plan_in_chip)
        return token

    def after_swa_bwd(self, dsq):
        arrays = _exchange_wait("grads_in_chip_wait", self.in_chip, self.in_chip_sems, dsq, _plan_in_chip)
        n = len(EARLY)
        sums = list(_chip_sums(arrays[:n], arrays[n:], "chip_sums_early"))
        lands = [lax.empty(s.shape, s.dtype) for s in sums]
        self.between_sems, self.between, token = _exchange_start("grads_between_chips_start", sums + lands, 3 * n,
                                                                 _plan_between_chips)
        return token

    def after_w_in_grad(self, d_win):
        self.late_in_chip_sems, self.late_in_chip, token = _exchange_start(
            "late_in_chip_start", [d_win.reshape((4, 2) + d_win.shape[1:]), lax.empty((4,) + d_win.shape[1:], MM)],
            1, _plan_in_chip)
        return token

    def finish(self, after):
        arrays = _exchange_wait("grads_between_chips_wait", self.between, self.between_sems, after,
                                _plan_between_chips)
        n = len(EARLY)
        self.sums = arrays[:n]
        return arrays[n:]


def _pack_small(t):
    rows = [t[k].reshape(1, D_MODEL) for k in SMALL_ROWS]
    misc = jnp.concatenate([t["b_forget"].reshape(-1), t["swa_sinks"].reshape(-1), t["rel_bias"].T.reshape(-1)])
    rows.append(jnp.pad(misc, (0, D_MODEL - misc.shape[0])).reshape(1, D_MODEL))
    rows.append(jnp.pad(t["loss"].reshape(-1), (0, D_MODEL - 1)).reshape(1, D_MODEL))
    rows.append(jnp.zeros((1, D_MODEL), F32))
    return jnp.concatenate(rows, axis=0).astype(F32)


def kernel(x, p, w_in, b_forget, w_out, rel_bias, swa_sinks, g_attn_pre, g_attn_post, w_ff1, w_ff2, g_ff_pre, g_ff_post, w_ple, w_ple_gate, g_ple_post, loss_target, m_w_in, m_b_forget, m_w_out, m_rel_bias, m_swa_sinks, m_g_attn_pre, m_g_attn_post, m_w_ff1, m_w_ff2, m_g_ff_pre, m_g_ff_post, m_w_ple, m_w_ple_gate, m_g_ple_post, v_w_in, v_b_forget, v_w_out, v_rel_bias, v_swa_sinks, v_g_attn_pre, v_g_attn_post, v_w_ff1, v_w_ff2, v_g_ff_pre, v_g_ff_post, v_w_ple, v_w_ple_gate, v_g_ple_post):
    w = dict(w_in=w_in, b_forget=b_forget, w_out=w_out, rel_bias=rel_bias, swa_sinks=swa_sinks,
             g_attn_pre=g_attn_pre, g_attn_post=g_attn_post, w_ff1=w_ff1, w_ff2=w_ff2, g_ff_pre=g_ff_pre,
             g_ff_post=g_ff_post, w_ple=w_ple, w_ple_gate=w_ple_gate, g_ple_post=g_ple_post)
    mom = dict(w_in=m_w_in, b_forget=m_b_forget, w_out=m_w_out, rel_bias=m_rel_bias, swa_sinks=m_swa_sinks,
               g_attn_pre=m_g_attn_pre, g_attn_post=m_g_attn_post, w_ff1=m_w_ff1, w_ff2=m_w_ff2,
               g_ff_pre=m_g_ff_pre, g_ff_post=m_g_ff_post, w_ple=m_w_ple, w_ple_gate=m_w_ple_gate,
               g_ple_post=m_g_ple_post)
    var = dict(w_in=v_w_in, b_forget=v_b_forget, w_out=v_w_out, rel_bias=v_rel_bias, swa_sinks=v_swa_sinks,
               g_attn_pre=v_g_attn_pre, g_attn_post=v_g_attn_post, w_ff1=v_w_ff1, w_ff2=v_w_ff2,
               g_ff_pre=v_g_ff_pre, g_ff_post=v_g_ff_post, w_ple=v_w_ple, w_ple_gate=v_w_ple_gate,
               g_ple_post=v_g_ple_post)

    turn = lambda t, k: t.T if k == "w_in" else t
    me = 4 * lax.axis_index("x") + 2 * lax.axis_index("y") + lax.axis_index("c")

    def stack(block):
        return lax.dynamic_update_slice_in_dim(lax.empty((N_DEV,) + block.shape, block.dtype), block[None], me, 0)

    stacks = [stack(turn(w[k][0], k).astype(MM)) for k in BIG]
    (win_g,), later = _all_gather_sequencer(stacks[:1], "all_gather_sequencer", 1), stacks[1:]
    hooks = _Overlap(later)
    loss, grad_x, big, small = _forward_backward(
        x[0], p[0, 0], loss_target[0], win_g, hooks, b_forget, rel_bias, swa_sinks,
        g_attn_pre, g_attn_post, g_ff_pre, g_ff_post, g_ple_post)
    out_g, out_d, out_m, out_v = {}, {}, {}, {}

    def update(k, part, own):
        if k == "w_in":
            there, back = (lambda t: jnp.transpose(t, (2, 0, 1))), (lambda t: jnp.transpose(t, (1, 2, 0)))
        else:
            there, back = (lambda t: t[0]), (lambda t: t[None])
        g, d, m_new, v_new = _adamw_chips(part, own, there(w[k]), there(mom[k]), there(var[k]), "adamw_" + k)
        out_g[k], out_d[k], out_m[k], out_v[k] = back(g), back(d), back(m_new), back(v_new)
        return d

    view, other = _exchange_wait("late_in_chip_wait", hooks.late_in_chip, hooks.late_in_chip_sems, grad_x,
                                 _plan_in_chip)
    (chip_sum,) = _chip_sums([view], [other], "chip_sum_w_in")
    small["loss"] = loss
    between_sems, between, token = _exchange_start(
        "late_between_chips_start", [chip_sum, lax.empty(chip_sum.shape, MM), stack(_pack_small(small))], 3 + 7,
        _plan_late_between)
    early_parts = hooks.finish(token)
    done = [update(k, part, own) for k, part, own in zip(EARLY, early_parts, hooks.sums)]
    chip_sum, part, small_all = _exchange_wait("late_between_chips_wait", between, between_sems, done,
                                               _plan_late_between)
    update("w_in", part, chip_sum)
    rep = {k: w[k] for k in w if k not in BIG}
    rep["loss"] = jnp.zeros((), F32)
    rep_m = {k: mom[k] for k in mom if k not in BIG}
    rep_m["loss"] = jnp.zeros((), F32)
    rep_v = {k: var[k] for k in var if k not in BIG}
    rep_v["loss"] = jnp.ones((), F32)
    g_s, d_s, m_s, v_s = _adamw_small(small_all, _pack_small(rep), _pack_small(rep_m), _pack_small(rep_v))
    for k in w:
        if k not in BIG:
            natural = (lambda t: t.T) if k == "rel_bias" else (lambda t: t)
            out_g[k], out_d[k], out_m[k], out_v[k] = natural(g_s[k]), natural(d_s[k]), natural(m_s[k]), natural(v_s[k])
    return (g_s["loss"].reshape(()), grad_x[None], *[out_g[k] for k in WEIGHTS], *[out_d[k] for k in WEIGHTS],
            *[out_m[k] for k in WEIGHTS], *[out_v[k] for k in WEIGHTS])
```

```python
import functools

import numpy as np
import jax
import jax.numpy as jnp
from jax import lax
from jax.experimental import pallas as pl
from jax.experimental.pallas import tpu as pltpu
from jax.experimental.pallas import tpu_sc as plsc

F32 = jnp.float32
MM = jnp.bfloat16

D_MODEL = 1024
HEAD_DIM = 64
N_HEADS = 8
D_ATT = N_HEADS * HEAD_DIM
D_KV = 128
D_FF = 4096
D_PLE = 256
D_IN = 3 * D_ATT + N_HEADS + D_ATT + 2 * D_KV
N_DEV = 8
FF_CHUNK = D_FF // N_DEV
WINDOW = 128
N_BUCKETS = 32
MAX_DISTANCE = 128
RMS_EPS = 1e-6
Q_SCALE = HEAD_DIM ** -0.5
NEG = -1e30

ADAM_LR = 0.001
ADAM_B1 = 0.9
ADAM_B2 = 0.999
ADAM_EPS = 1e-08
ADAM_WD = 0.01
ADAM_STEP = 10

SLOT_HEAD = (0, 4, 1, 5, 2, 6, 3, 7)
HEAD_SLOT = (0, 2, 4, 6, 1, 3, 5, 7)

VMEM_LIMIT = 60 * 1024 * 1024
MESH = pl.DeviceIdType.MESH

NT = (((1,), (1,)), ((), ()))
TN = (((0,), (0,)), ((), ()))


def _params(*semantics):
    return pltpu.CompilerParams(dimension_semantics=semantics, vmem_limit_bytes=VMEM_LIMIT)


def _resident():
    return pl.BlockSpec(memory_space=pltpu.VMEM)


def _rows(tm, width):
    return pl.BlockSpec((tm, width), lambda i: (i, 0))


def _const(shape):
    return pl.BlockSpec(shape, lambda i: (0,) * len(shape))


def _dot(a, b):
    return jnp.dot(a, b, preferred_element_type=F32)


def _dot_nt(a, b):
    return lax.dot_general(a, b, NT, preferred_element_type=F32)


def _dot_tn(a, b):
    return lax.dot_general(a, b, TN, preferred_element_type=F32)


def _rms(xf):
    r = lax.rsqrt(jnp.mean(xf * xf, axis=-1, keepdims=True) + RMS_EPS)
    return xf * r, r


def _rms_bwd(dout, n, r, g):
    dg = jnp.sum(dout * n, axis=0, keepdims=True)
    dn = dout * g
    dx = r * (dn - n * jnp.mean(dn * n, axis=-1, keepdims=True))
    return dx, dg


def _run_after(after, body, in_specs, operands):
    if after is None:
        return body, list(in_specs), tuple(operands)
    n = len(operands)
    return ((lambda *refs: body(*refs[:n], *refs[n + 1:])), list(in_specs) + [pl.BlockSpec(memory_space=pl.ANY)],
            tuple(operands) + (after,))


def _accumulate(ref, value, step):
    @pl.when(step == 0)
    def _():
        ref[...] = value

    @pl.when(step != 0)
    def _():
        ref[...] += value


def _t5_bucket(n):
    max_exact = N_BUCKETS // 2
    large = max_exact + (np.log(np.maximum(n, 1) / max_exact) / np.log(MAX_DISTANCE / max_exact)
                         * (N_BUCKETS - max_exact)).astype(np.int32)
    large = np.minimum(large, N_BUCKETS - 1)
    return np.where(n < max_exact, n, large).astype(np.int32)


def _swa_bucket_map():
    i = np.arange(WINDOW)[:, None]
    j = np.arange(2 * WINDOW)[None, :]
    dist = i + WINDOW - j
    ok = (dist >= 0) & (dist < WINDOW)
    return np.where(ok, _t5_bucket(np.clip(dist, 0, None)), -1).astype(np.int32)


WT_FOX = 3 * D_ATT
WT_SQ = 16
WT_SKV = WT_SQ + D_ATT
WT_REST = WT_SKV + 2 * D_KV


def _pre_attn(x, g1, win_t, wt_rest, tm, after=None):
    S = x.shape[0]

    def body(x_ref, g_ref, wf_ref, wr_ref, a_ref, fqkv_ref, sqkv_ref, fft_ref):
        n, _ = _rms(x_ref[...])
        a = (n * g_ref[...]).astype(MM)
        a_ref[...] = a
        fqkv_ref[:, :D_ATT] = (_dot_nt(a, wf_ref[0:D_ATT]) * Q_SCALE).astype(MM)
        fqkv_ref[:, D_ATT:] = _dot_nt(a, wf_ref[D_ATT:WT_FOX]).astype(MM)
        sqkv_ref[:, :D_ATT] = (_dot_nt(a, wr_ref[WT_SQ:WT_SKV]) * Q_SCALE).astype(MM)
        sqkv_ref[:, D_ATT:] = _dot_nt(a, wr_ref[WT_SKV:WT_REST]).astype(MM)
        fft_ref[...] = _dot_nt(wr_ref[0:WT_SQ], a)

    body, in_specs, operands = _run_after(
        after, body, [_rows(tm, D_MODEL), _const((1, D_MODEL)), _const((WT_FOX, D_MODEL)), _resident()],
        (x, g1, win_t, wt_rest))
    return pl.pallas_call(
        body, name="pre_attn", grid=(S // tm,), in_specs=in_specs,
        out_specs=[_rows(tm, D_MODEL), _rows(tm, 3 * D_ATT), _rows(tm, D_ATT + 2 * D_KV),
                   pl.BlockSpec((16, tm), lambda i: (0, i))],
        out_shape=[jax.ShapeDtypeStruct((S, D_MODEL), MM), jax.ShapeDtypeStruct((S, 3 * D_ATT), MM),
                   jax.ShapeDtypeStruct((S, D_ATT + 2 * D_KV), MM), jax.ShapeDtypeStruct((16, S), F32)],
        compiler_params=_params("parallel"),
    )(*operands)


def _lane_scan(v, reverse):
    S = v.shape[1]
    lane = lax.broadcasted_iota(jnp.int32, v.shape, 1)
    k = 1
    while k < S:
        if reverse:
            v = v + jnp.where(lane < S - k, pltpu.roll(v, S - k, axis=1), 0.0)
        else:
            v = v + jnp.where(lane >= k, pltpu.roll(v, k, axis=1), 0.0)
        k *= 2
    return v


def _forget_cumsum(fft, bcol):
    def body(f_ref, b_ref, c_ref):
        z = f_ref[...] + b_ref[...]
        log_f = jnp.minimum(z, 0.0) - jnp.log1p(jnp.exp(-jnp.abs(z)))
        c_ref[...] = _lane_scan(log_f, reverse=False)

    return pl.pallas_call(
        body, name="forget_cumsum", out_shape=jax.ShapeDtypeStruct(fft.shape, F32),
        in_specs=[_resident(), _resident()], out_specs=_resident(),
    )(fft, bcol)


def _forget_bwd(dc_row, fft, bcol, a):
    def body(dc_ref, f_ref, b_ref, a_ref, dff_ref, db_ref, dw_ref):
        z = f_ref[...] + b_ref[...]
        dlog_f = _lane_scan(dc_ref[...], reverse=True)
        dff = dlog_f * (1.0 / (1.0 + jnp.exp(z)))
        dff_ref[...] = dff
        db_ref[...] = jnp.sum(dff, axis=1, keepdims=True)
        dw_ref[...] = _dot(dff.astype(MM), a_ref[...])

    return pl.pallas_call(
        body, name="forget_bwd",
        out_shape=[jax.ShapeDtypeStruct(fft.shape, F32), jax.ShapeDtypeStruct((fft.shape[0], 1), F32),
                   jax.ShapeDtypeStruct((fft.shape[0], D_MODEL), F32)],
        in_specs=[_resident()] * 4, out_specs=[_resident()] * 3,
    )(dc_row, fft, bcol, a)


def _head_select(shape, upper):
    lane = lax.broadcasted_iota(jnp.int32, shape, 1)
    return lane >= HEAD_DIM if upper else lane < HEAD_DIM


def _fox_fwd(fqkv, c_row3, tq, tk, pairs_per_loop=2, row_chunks=1):
    S = fqkv.shape[0]
    rq = tq // row_chunks
    n_band = tq // tk

    def body(q_ref, k_ref, v_ref, ck_ref, o_ref, lse_ref):
        qi = pl.program_id(0)
        row = lax.broadcasted_iota(jnp.int32, (rq, tk), 0)
        col = lax.broadcasted_iota(jnp.int32, (rq, tk), 1)
        low = _head_select((rq, 128), 0)
        for first in range(0, N_HEADS // 2, pairs_per_loop):
            pairs = range(first, first + pairs_per_loop)
            chains = [(pr, hh, rc) for pr in pairs for hh in range(2) for rc in range(row_chunks)]
            qh = {}
            for pr in pairs:
                for rc in range(row_chunks):
                    q2 = q_ref[rc * rq:(rc + 1) * rq, pr * 128:(pr + 1) * 128]
                    qh[pr, 0, rc] = jnp.where(low, q2, jnp.zeros_like(q2))
                    qh[pr, 1, rc] = jnp.where(low, jnp.zeros_like(q2), q2)

            def block(kb, carry, band, chains=chains, qh=qh):
                rows = pl.ds(pl.multiple_of(kb * tk, tk), tk)
                out = []
                for (pr, hh, rc), (m, l, acc) in zip(chains, carry):
                    if band is not None and (rc + 1) * rq <= band * tk:
                        out.append((m, l, acc))
                        continue
                    lanes = slice(pr * 128, (pr + 1) * 128)
                    s = _dot_nt(qh[pr, hh, rc], k_ref[rows, lanes]) - ck_ref[2 * pr + hh, pl.ds(kb, 1), :]
                    if band is not None:
                        s = jnp.where(row + rc * rq >= col + band * tk, s, NEG)
                    m_new = jnp.maximum(m, jnp.max(s, axis=-1, keepdims=True))
                    p = jnp.exp(s - m_new)
                    alpha = jnp.exp(m - m_new)
                    l = alpha * l + jnp.sum(p, axis=-1, keepdims=True)
                    acc = alpha * acc + _dot(p.astype(MM), v_ref[rows, lanes])
                    out.append((m_new, l, acc))
                return tuple(out)

            carry = tuple((jnp.full((rq, 1), NEG, F32), jnp.zeros((rq, 1), F32), jnp.zeros((rq, 128), F32))
                          for _ in chains)
            carry = lax.fori_loop(0, qi * n_band, functools.partial(block, band=None), carry)
            for band in range(n_band):
                carry = block(qi * n_band + band, carry, band=band)
            res = {}
            for (pr, hh, rc), (m, l, acc) in zip(chains, carry):
                res[pr, hh, rc] = acc / l
                lse_ref[rc * rq:(rc + 1) * rq, 2 * pr + hh:2 * pr + hh + 1] = m + jnp.log(l)
            for pr in pairs:
                for rc in range(row_chunks):
                    o_ref[rc * rq:(rc + 1) * rq, pr * 128:(pr + 1) * 128] = jnp.where(
                        low, res[pr, 0, rc], res[pr, 1, rc]).astype(MM)

    return pl.pallas_call(
        body, name="fox_fwd", grid=(S // tq,),
        in_specs=[pl.BlockSpec((tq, D_ATT), lambda i: (i, 0)), pl.BlockSpec((S, D_ATT), lambda i: (0, 1)),
                  pl.BlockSpec((S, D_ATT), lambda i: (0, 2)), _resident()],
        out_specs=[_rows(tq, D_ATT), _rows(tq, N_HEADS)],
        out_shape=[jax.ShapeDtypeStruct((S, D_ATT), MM), jax.ShapeDtypeStruct((S, N_HEADS), F32)],
        compiler_params=_params("parallel"),
    )(fqkv, fqkv, fqkv, c_row3)


def _swa_bias(rel_bias_slot, bucket):
    def body(rb_ref, bk_ref, out_ref):
        bk = bk_ref[...]
        for s in range(N_HEADS):
            acc = jnp.where(bk < 0, NEG, 0.0).astype(F32)
            for b in range(N_BUCKETS):
                acc = jnp.where(bk == b, rb_ref[b, s], acc)
            out_ref[s] = acc

    return pl.pallas_call(
        body, name="swa_bias", out_shape=jax.ShapeDtypeStruct((N_HEADS, WINDOW, 2 * WINDOW), F32),
        in_specs=[pl.BlockSpec(memory_space=pltpu.SMEM), _resident()], out_specs=_resident(),
    )(rel_bias_slot, bucket)


def _stack4(piece):
    return jnp.concatenate([piece(j) for j in range(4)], axis=0)


def _swa_specs(S):
    q = pl.BlockSpec((WINDOW, D_ATT), lambda n: (n, 0))
    kp = pl.BlockSpec((WINDOW, D_KV), lambda n: (jnp.maximum(n - 1, 0), 4))
    kc = pl.BlockSpec((WINDOW, D_KV), lambda n: (n, 4))
    vp = pl.BlockSpec((WINDOW, D_KV), lambda n: (jnp.maximum(n - 1, 0), 5))
    vc = pl.BlockSpec((WINDOW, D_KV), lambda n: (n, 5))
    return [q, kp, kc, vp, vc]


def _swa_fwd(sqkv, biasm, sinks_slot, after=None):
    S = sqkv.shape[0]

    def body(q_ref, kp_ref, kc_ref, vp_ref, vc_ref, bias_ref, sink_ref, o_ref, lse_ref):
        n = pl.program_id(0)
        no_prev = jnp.where(n > 0, 0.0, NEG)
        low = _head_select((WINDOW, 128), 0)
        res = []
        for g in range(2):
            sel = low if g == 0 else jnp.logical_not(low)
            qg = _stack4(lambda j: jnp.where(sel, q_ref[:, j * 128:(j + 1) * 128], jnp.zeros((WINDOW, 128), MM)))
            sink = _stack4(lambda j: jnp.full((WINDOW, 1), sink_ref[2 * j + g], F32))
            sp = _dot_nt(qg, kp_ref[...]) + _stack4(lambda j: bias_ref[2 * j + g, :, :WINDOW]) + no_prev
            sc = _dot_nt(qg, kc_ref[...]) + _stack4(lambda j: bias_ref[2 * j + g, :, WINDOW:])
            m = jnp.maximum(jnp.maximum(jnp.max(sp, axis=-1, keepdims=True),
                                        jnp.max(sc, axis=-1, keepdims=True)), sink)
            ep = jnp.exp(sp - m)
            ec = jnp.exp(sc - m)
            den = jnp.sum(ep, axis=-1, keepdims=True) + jnp.sum(ec, axis=-1, keepdims=True) + jnp.exp(sink - m)
            res.append((_dot(ep.astype(MM), vp_ref[...]) + _dot(ec.astype(MM), vc_ref[...])) / den)
            lse = m + jnp.log(den)
            for j in range(4):
                lse_ref[:, 2 * j + g:2 * j + g + 1] = lse[j * WINDOW:(j + 1) * WINDOW]
        for j in range(4):
            rows = slice(j * WINDOW, (j + 1) * WINDOW)
            o_ref[:, j * 128:(j + 1) * 128] = jnp.where(low, res[0][rows], res[1][rows]).astype(MM)

    body, in_specs, operands = _run_after(
        after, body, _swa_specs(S) + [_resident(), pl.BlockSpec(memory_space=pltpu.SMEM)],
        (sqkv, sqkv, sqkv, sqkv, sqkv, biasm, sinks_slot))
    return pl.pallas_call(
        body, name="swa_fwd", grid=(S // WINDOW,), in_specs=in_specs,
        out_specs=[_rows(WINDOW, D_ATT), _rows(WINDOW, N_HEADS)],
        out_shape=[jax.ShapeDtypeStruct((S, D_ATT), MM), jax.ShapeDtypeStruct((S, N_HEADS), F32)],
        compiler_params=_params("parallel"),
    )(*operands)


def _post_attn(x, fox_o, swa_o, wout_fox, wout_swa, g2, g3, tm):
    S = x.shape[0]

    def body(x_ref, fo_ref, so_ref, wf_ref, ws_ref, g2_ref, g3_ref, mix_ref, h1_ref, m_ref):
        mix = _dot(fo_ref[...], wf_ref[...]) + _dot(so_ref[...], ws_ref[...])
        mix_ref[...] = mix
        n2, _ = _rms(mix)
        h1 = x_ref[...] + n2 * g2_ref[...]
        h1_ref[...] = h1
        n3, _ = _rms(h1)
        m_ref[...] = (n3 * g3_ref[...]).astype(MM)

    return pl.pallas_call(
        body, name="post_attn", grid=(S // tm,),
        in_specs=[_rows(tm, D_MODEL), _rows(tm, D_ATT), _rows(tm, D_ATT), _resident(), _resident(),
                  _const((1, D_MODEL)), _const((1, D_MODEL))],
        out_specs=[_rows(tm, D_MODEL)] * 3,
        out_shape=[jax.ShapeDtypeStruct((S, D_MODEL), F32), jax.ShapeDtypeStruct((S, D_MODEL), F32),
                   jax.ShapeDtypeStruct((S, D_MODEL), MM)],
        compiler_params=_params("parallel"),
    )(x, fox_o, swa_o, wout_fox, wout_swa, g2, g3)


def _mlp_fwd(m, h1, w1, w2, g4, tm):
    S = m.shape[0]

    def body(m_ref, h1_ref, w1_ref, w2_ref, g4_ref, u_ref, y_ref, h2_ref):
        mb = m_ref[...]
        y = jnp.zeros((tm, D_MODEL), F32)
        for j in range(N_DEV):
            cols = slice(j * FF_CHUNK, (j + 1) * FF_CHUNK)
            u = _dot(mb, w1_ref[j])
            u_ref[:, cols] = u.astype(MM)
            y = y + _dot(jnp.square(jnp.maximum(u, 0.0)).astype(MM), w2_ref[cols, :])
        y_ref[...] = y
        n4, _ = _rms(y)
        h2_ref[...] = h1_ref[...] + n4 * g4_ref[...]

    return pl.pallas_call(
        body, name="mlp_fwd", grid=(S // tm,),
        in_specs=[_rows(tm, D_MODEL), _rows(tm, D_MODEL), _resident(), _resident(), _const((1, D_MODEL))],
        out_specs=[_rows(tm, D_FF), _rows(tm, D_MODEL), _rows(tm, D_MODEL)],
        out_shape=[jax.ShapeDtypeStruct((S, D_FF), MM), jax.ShapeDtypeStruct((S, D_MODEL), F32),
                   jax.ShapeDtypeStruct((S, D_MODEL), F32)],
        compiler_params=_params("parallel"),
    )(m, h1, w1, w2, g4)


def _ple_loss(h2, p, target, wg, wple, g5, tm):
    S = h2.shape[0]

    def body(h2_ref, p_ref, t_ref, wg_ref, wp_ref, g5_ref, dh2_ref, dpe_ref, dgl_ref, dg5_ref, loss_ref):
        i = pl.program_id(0)
        h2 = h2_ref[...]
        gate = jax.nn.sigmoid(_dot(h2.astype(MM), wg_ref[...]))
        pe = _dot(p_ref[...].astype(MM), wp_ref[...])
        n5, r5 = _rms(pe * gate)
        g5 = g5_ref[...]
        diff = h2 + n5 * g5 - t_ref[...]
        per_token = jnp.mean(jnp.square(diff), axis=-1, keepdims=True)
        _accumulate(loss_ref, 0.5 * jnp.sum(per_token, axis=0, keepdims=True), i)
        dh3 = diff * (1.0 / D_MODEL)
        de, dg5 = _rms_bwd(dh3, n5, r5, g5)
        _accumulate(dg5_ref, dg5, i)
        dpe_ref[...] = (de * gate).astype(MM)
        dgl = (de * pe * gate * (1.0 - gate)).astype(MM)
        dgl_ref[...] = dgl
        dh2_ref[...] = dh3 + _dot_nt(dgl, wg_ref[...])

    return pl.pallas_call(
        body, name="ple_loss", grid=(S // tm,),
        in_specs=[_rows(tm, D_MODEL), _rows(tm, D_PLE), _rows(tm, D_MODEL), _resident(), _resident(),
                  _const((1, D_MODEL))],
        out_specs=[_rows(tm, D_MODEL), _rows(tm, D_MODEL), _rows(tm, D_MODEL), _const((1, D_MODEL)), _const((1, 1))],
        out_shape=[jax.ShapeDtypeStruct((S, D_MODEL), F32), jax.ShapeDtypeStruct((S, D_MODEL), MM),
                   jax.ShapeDtypeStruct((S, D_MODEL), MM), jax.ShapeDtypeStruct((1, D_MODEL), F32),
                   jax.ShapeDtypeStruct((1, 1), F32)],
        compiler_params=_params("arbitrary"),
    )(h2, p, target, wg, wple, g5)


def _mlp_bwd(dh2, y, h1, u, w1, w2, g4, g3, tm):
    S = dh2.shape[0]

    def body(dh2_ref, y_ref, h1_ref, u_ref, w1_ref, w2_ref, g4_ref, g3_ref,
             dh1_ref, dy_ref, du_ref, dg4_ref, dg3_ref):
        i = pl.program_id(0)
        dh2 = dh2_ref[...]
        n4, r4 = _rms(y_ref[...])
        dy, dg4 = _rms_bwd(dh2, n4, r4, g4_ref[...])
        _accumulate(dg4_ref, dg4, i)
        dyb = dy.astype(MM)
        dy_ref[...] = dyb
        dm = jnp.zeros((tm, D_MODEL), F32)
        for j in range(N_DEV):
            cols = slice(j * FF_CHUNK, (j + 1) * FF_CHUNK)
            dact = _dot_nt(dyb, w2_ref[cols, :])
            du = (dact * (2.0 * jnp.maximum(u_ref[:, cols].astype(F32), 0.0))).astype(MM)
            du_ref[:, cols] = du
            dm = dm + _dot_nt(du, w1_ref[j])
        n3, r3 = _rms(h1_ref[...])
        dx, dg3 = _rms_bwd(dm, n3, r3, g3_ref[...])
        _accumulate(dg3_ref, dg3, i)
        dh1_ref[...] = dh2 + dx

    return pl.pallas_call(
        body, name="mlp_bwd", grid=(S // tm,),
        in_specs=[_rows(tm, D_MODEL), _rows(tm, D_MODEL), _rows(tm, D_MODEL), _rows(tm, D_FF),
                  _resident(), _resident(), _const((1, D_MODEL)), _const((1, D_MODEL))],
        out_specs=[_rows(tm, D_MODEL), _rows(tm, D_MODEL), _rows(tm, D_FF), _const((1, D_MODEL)),
                   _const((1, D_MODEL))],
        out_shape=[jax.ShapeDtypeStruct((S, D_MODEL), F32), jax.ShapeDtypeStruct((S, D_MODEL), MM),
                   jax.ShapeDtypeStruct((S, D_FF), MM), jax.ShapeDtypeStruct((1, D_MODEL), F32),
                   jax.ShapeDtypeStruct((1, D_MODEL), F32)],
        compiler_params=_params("arbitrary"),
    )(dh2, y, h1, u, w1, w2, g4, g3)


def _attn_out_bwd(dh1, mix, fox_o, swa_o, wout_fox, wout_swa, g2, head_rows, tm):
    S = dh1.shape[0]

    def body(dh1_ref, mix_ref, fo_ref, so_ref, wf_ref, ws_ref, g2_ref, er_ref,
             dmix_ref, dcat_ref, drow_ref, dswa_ref, dg2_ref):
        i = pl.program_id(0)
        n2, r2 = _rms(mix_ref[...])
        dmix, dg2 = _rms_bwd(dh1_ref[...], n2, r2, g2_ref[...])
        _accumulate(dg2_ref, dg2, i)
        dmb = dmix.astype(MM)
        dmix_ref[...] = dmb
        dfo = _dot_nt(dmb, wf_ref[...]).astype(MM)
        dso = _dot_nt(dmb, ws_ref[...]).astype(MM)
        dcat_ref[:, :D_ATT] = dfo
        dcat_ref[:, D_ATT:] = dso
        hi = lax.Precision.HIGH
        prod_f = dfo.astype(F32) * fo_ref[...].astype(F32)
        prod_s = dso.astype(F32) * so_ref[...].astype(F32)
        drow_ref[...] = lax.dot_general(er_ref[...], prod_f, NT, precision=hi, preferred_element_type=F32)
        dswa_ref[...] = lax.dot_general(er_ref[...], prod_s, NT, precision=hi, preferred_element_type=F32)

    return pl.pallas_call(
        body, name="attn_out_bwd", grid=(S // tm,),
        in_specs=[_rows(tm, D_MODEL), _rows(tm, D_MODEL), _rows(tm, D_ATT), _rows(tm, D_ATT), _resident(),
                  _resident(), _const((1, D_MODEL)), _resident()],
        out_specs=[_rows(tm, D_MODEL), _rows(tm, D_MODEL), pl.BlockSpec((N_HEADS, tm), lambda i: (0, i)),
                   pl.BlockSpec((N_HEADS, tm), lambda i: (0, i)), _const((1, D_MODEL))],
        out_shape=[jax.ShapeDtypeStruct((S, D_MODEL), MM), jax.ShapeDtypeStruct((S, D_MODEL), MM),
                   jax.ShapeDtypeStruct((N_HEADS, S), F32), jax.ShapeDtypeStruct((N_HEADS, S), F32),
                   jax.ShapeDtypeStruct((1, D_MODEL), F32)],
        compiler_params=_params("arbitrary"),
    )(dh1, mix, fox_o, swa_o, wout_fox, wout_swa, g2, head_rows)


def _fox_bwd(fqkv, dcat, lse_row3, d_row3, c_col, tq, tk, pairs_per_loop=2, after=None):
    S = fqkv.shape[0]
    n_blk = S // tk
    n_qblk = S // tq
    n_band = tk // tq

    def body(q_ref, k_ref, v_ref, do_ref, lse_ref, dd_ref, ck_ref, dq_ref, dk_ref, dv_ref, dc_ref, dcq_ref):
        kb = pl.program_id(0)

        @pl.when(kb == 0)
        def _():
            dq_ref[...] = jnp.zeros_like(dq_ref)
            dcq_ref[...] = jnp.zeros_like(dcq_ref)

        key = lax.broadcasted_iota(jnp.int32, (tk, tq), 0)
        qry = lax.broadcasted_iota(jnp.int32, (tk, tq), 1)
        low = _head_select((tk, 128), 0)
        for first in range(0, N_HEADS // 2, pairs_per_loop):
            pairs = range(first, first + pairs_per_loop)
            heads = [(pr, hh) for pr in pairs for hh in range(2)]
            kh, vh, ck = {}, {}, {}
            for pr in pairs:
                k2 = k_ref[:, pr * 128:(pr + 1) * 128]
                v2 = v_ref[:, pr * 128:(pr + 1) * 128]
                zero = jnp.zeros_like(k2)
                kh[pr, 0], kh[pr, 1] = jnp.where(low, k2, zero), jnp.where(low, zero, k2)
                vh[pr, 0], vh[pr, 1] = jnp.where(low, v2, zero), jnp.where(low, zero, v2)
                for hh in range(2):
                    ck[pr, hh] = ck_ref[:, 2 * pr + hh:2 * pr + hh + 1]

            def block(qb, carry, band, pairs=pairs, kh=kh, vh=vh, ck=ck):
                rows = pl.ds(pl.multiple_of(qb * tq, tq), tq)
                k1 = tk if band is None else (band + 1) * tq
                out = []
                it = iter(carry)
                for pr in pairs:
                    lanes = slice(pr * 128, (pr + 1) * 128)
                    q2 = q_ref[rows, lanes]
                    do2 = do_ref[rows, lanes]
                    dq = None
                    for hh in range(2):
                        h = 2 * pr + hh
                        dk, dv, dc = next(it)
                        s_t = _dot_nt(kh[pr, hh][:k1], q2) - ck[pr, hh][:k1]
                        p_t = jnp.exp(s_t - lse_ref[h, pl.ds(qb, 1), :])
                        if band is not None:
                            p_t = jnp.where(qry[:k1] + band * tq >= key[:k1], p_t, 0.0)
                        ds_t = p_t * (_dot_nt(vh[pr, hh][:k1], do2) - dd_ref[h, pl.ds(qb, 1), :])
                        dsb = ds_t.astype(MM)
                        dv_new = dv[:k1] + _dot(p_t.astype(MM), do2)
                        dk_new = dk[:k1] + _dot(dsb, q2)
                        dc_new = dc[:k1] - jnp.sum(ds_t, axis=1, keepdims=True)
                        if k1 < tk:
                            dv_new = jnp.concatenate([dv_new, dv[k1:]], axis=0)
                            dk_new = jnp.concatenate([dk_new, dk[k1:]], axis=0)
                            dc_new = jnp.concatenate([dc_new, dc[k1:]], axis=0)
                        part = _dot_tn(dsb, kh[pr, hh][:k1])
                        dq = part if dq is None else dq + part
                        dcq_ref[h, pl.ds(qb, 1), :] += jnp.sum(ds_t, axis=0, keepdims=True)
                        out.append((dk_new, dv_new, dc_new))
                    dq_ref[rows, lanes] += dq
                return tuple(out)

            carry = tuple((jnp.zeros((tk, 128), F32), jnp.zeros((tk, 128), F32), jnp.zeros((tk, 1), F32))
                          for _ in heads)
            for band in range(n_band):
                carry = block(kb * n_band + band, carry, band=band)
            carry = lax.fori_loop((kb + 1) * n_band, n_qblk, functools.partial(block, band=None), carry)
            grads = dict(zip(heads, carry))
            for pr in pairs:
                lanes = slice(pr * 128, (pr + 1) * 128)
                dk_ref[:, lanes] = jnp.where(low, grads[pr, 0][0], grads[pr, 1][0]).astype(MM)
                dv_ref[:, lanes] = jnp.where(low, grads[pr, 0][1], grads[pr, 1][1]).astype(MM)
                for hh in range(2):
                    dc_ref[:, 2 * pr + hh:2 * pr + hh + 1] = grads[pr, hh][2]

        @pl.when(kb == n_blk - 1)
        def _():
            dq_ref[...] = dq_ref[...] * Q_SCALE

    body, in_specs, operands = _run_after(
        after, body,
        [pl.BlockSpec((S, D_ATT), lambda i: (0, 0)), pl.BlockSpec((tk, D_ATT), lambda i: (i, 1)),
         pl.BlockSpec((tk, D_ATT), lambda i: (i, 2)), pl.BlockSpec((S, D_ATT), lambda i: (0, 0)),
         _resident(), _resident(), _rows(tk, N_HEADS)],
        (fqkv, fqkv, fqkv, dcat, lse_row3, d_row3, c_col))
    return pl.pallas_call(
        body, name="fox_bwd", grid=(n_blk,), in_specs=in_specs,
        out_specs=[_const((S, D_ATT)), _rows(tk, D_ATT), _rows(tk, D_ATT), _rows(tk, N_HEADS),
                   _const((N_HEADS, n_qblk, tq))],
        out_shape=[jax.ShapeDtypeStruct((S, D_ATT), F32), jax.ShapeDtypeStruct((S, D_ATT), MM),
                   jax.ShapeDtypeStruct((S, D_ATT), MM), jax.ShapeDtypeStruct((S, N_HEADS), F32),
                   jax.ShapeDtypeStruct((N_HEADS, n_qblk, tq), F32)],
        compiler_params=_params("arbitrary"),
    )(*operands)


def _swa_bwd(sqkv, dcat, biasm, sinks_slot, bucket, lse, d_col, after=None):
    S = sqkv.shape[0]
    n_blk = S // WINDOW

    def body(q_ref, kp_ref, kc_ref, vp_ref, vc_ref, do_ref, bias_ref, sink_ref, bk_ref, lse_ref, dd_ref,
             dq_ref, dk_ref, dv_ref, drb_ref, dsink_ref, ds_acc):
        n = pl.program_id(0)

        @pl.when(n == 0)
        def _():
            dk_ref[...] = jnp.zeros_like(dk_ref)
            dv_ref[...] = jnp.zeros_like(dv_ref)
            ds_acc[...] = jnp.zeros_like(ds_acc)
            dsink_ref[...] = jnp.zeros_like(dsink_ref)

        no_prev = jnp.where(n > 0, 0.0, NEG)
        prev = pl.ds(pl.multiple_of(jnp.maximum(n - 1, 0) * WINDOW, WINDOW), WINDOW)
        cur = pl.ds(pl.multiple_of(n * WINDOW, WINDOW), WINDOW)
        lane8 = lax.broadcasted_iota(jnp.int32, (1, N_HEADS), 1)
        dkp = jnp.zeros((WINDOW, D_KV), F32)
        dkc = jnp.zeros((WINDOW, D_KV), F32)
        dvp = jnp.zeros((WINDOW, D_KV), F32)
        dvc = jnp.zeros((WINDOW, D_KV), F32)
        dsink = jnp.zeros((1, N_HEADS), F32)
        low = _head_select((WINDOW, 128), 0)
        zero = jnp.zeros((WINDOW, 128), MM)
        dqs = []
        for g in range(2):
            sel = low if g == 0 else jnp.logical_not(low)
            qg = _stack4(lambda j: jnp.where(sel, q_ref[:, j * 128:(j + 1) * 128], zero))
            dog = _stack4(lambda j: jnp.where(sel, do_ref[:, j * 128:(j + 1) * 128], zero))
            lse_g = _stack4(lambda j: lse_ref[:, 2 * j + g:2 * j + g + 1])
            dd = _stack4(lambda j: dd_ref[:, 2 * j + g:2 * j + g + 1])
            sink = _stack4(lambda j: jnp.full((WINDOW, 1), sink_ref[2 * j + g], F32))
            pp = jnp.exp(_dot_nt(qg, kp_ref[...]) + _stack4(lambda j: bias_ref[2 * j + g, :, :WINDOW]) + no_prev - lse_g)
            pc = jnp.exp(_dot_nt(qg, kc_ref[...]) + _stack4(lambda j: bias_ref[2 * j + g, :, WINDOW:]) - lse_g)
            sink_term = jnp.exp(sink - lse_g) * dd
            dsp = pp * (_dot_nt(dog, vp_ref[...]) - dd)
            dsc = pc * (_dot_nt(dog, vc_ref[...]) - dd)
            for j in range(4):
                rows = slice(j * WINDOW, (j + 1) * WINDOW)
                dsink = dsink + jnp.where(lane8 == 2 * j + g, -jnp.sum(sink_term[rows]), 0.0)
                ds_acc[2 * j + g, :, :WINDOW] += dsp[rows]
                ds_acc[2 * j + g, :, WINDOW:] += dsc[rows]
            dspb, dscb = dsp.astype(MM), dsc.astype(MM)
            dqs.append(_dot(dspb, kp_ref[...]) + _dot(dscb, kc_ref[...]))
            dkp = dkp + _dot_tn(dspb, qg)
            dkc = dkc + _dot_tn(dscb, qg)
            dvp = dvp + _dot_tn(pp.astype(MM), dog)
            dvc = dvc + _dot_tn(pc.astype(MM), dog)
        for j in range(4):
            rows = slice(j * WINDOW, (j + 1) * WINDOW)
            dq_ref[:, j * 128:(j + 1) * 128] = (jnp.where(low, dqs[0][rows], dqs[1][rows]) * Q_SCALE).astype(MM)
        dk_ref[prev, :] += dkp
        dk_ref[cur, :] += dkc
        dv_ref[prev, :] += dvp
        dv_ref[cur, :] += dvc
        dsink_ref[...] += dsink

        @pl.when(n == n_blk - 1)
        def _():
            bk = bk_ref[...]
            rb = lax.broadcasted_iota(jnp.int32, (N_BUCKETS, N_HEADS), 0)
            cb = lax.broadcasted_iota(jnp.int32, (N_BUCKETS, N_HEADS), 1)
            out = jnp.zeros((N_BUCKETS, N_HEADS), F32)
            for s in range(N_HEADS):
                acc = ds_acc[s]
                for b in range(N_BUCKETS):
                    out = out + jnp.where((rb == b) & (cb == s), jnp.sum(jnp.where(bk == b, acc, 0.0)), 0.0)
            drb_ref[...] = out

    do_spec = pl.BlockSpec((WINDOW, D_ATT), lambda n: (n, 1))
    body, in_specs, operands = _run_after(
        after, body, _swa_specs(S) + [do_spec, _resident(), pl.BlockSpec(memory_space=pltpu.SMEM), _resident(),
                                      _rows(WINDOW, N_HEADS), _rows(WINDOW, N_HEADS)],
        (sqkv, sqkv, sqkv, sqkv, sqkv, dcat, biasm, sinks_slot, bucket, lse, d_col))
    return pl.pallas_call(
        body, name="swa_bwd", grid=(n_blk,), in_specs=in_specs,
        out_specs=[_rows(WINDOW, D_ATT), _const((S, D_KV)), _const((S, D_KV)), _const((N_BUCKETS, N_HEADS)),
                   _const((1, N_HEADS))],
        out_shape=[jax.ShapeDtypeStruct((S, D_ATT), MM), jax.ShapeDtypeStruct((S, D_KV), F32),
                   jax.ShapeDtypeStruct((S, D_KV), F32), jax.ShapeDtypeStruct((N_BUCKETS, N_HEADS), F32),
                   jax.ShapeDtypeStruct((1, N_HEADS), F32)],
        scratch_shapes=[pltpu.VMEM((N_HEADS, WINDOW, 2 * WINDOW), F32)],
        compiler_params=_params("arbitrary"),
    )(*operands)


def _weight_grad_pieces(pieces, b, name, tk):
    S, N = b.shape
    wide = [p for p in pieces if p.shape[1] % tk == 0]
    narrow = pieces[len(wide):]
    assert sum(p.shape[1] for p in narrow) == tk
    counts = [p.shape[1] // tk for p in wide]
    firsts = [sum(counts[:j]) for j in range(len(wide))]
    last = sum(counts)

    def body(*refs):
        b_ref, out_ref = refs[len(pieces)], refs[len(pieces) + 1]
        i = pl.program_id(0)

        def whole_steps(ref, first, count):
            @pl.when((i >= first) & (i < first + count))
            def _():
                out_ref[...] = _dot_tn(ref[...].astype(MM), b_ref[...]).astype(MM)

        for ref, first, count in zip(refs[:len(wide)], firsts, counts):
            whole_steps(ref, first, count)

        @pl.when(i == last)
        def _():
            row = 0
            for ref in refs[len(wide):len(pieces)]:
                k = ref.shape[1]
                out_ref[row:row + k] = _dot_tn(ref[...].astype(MM), b_ref[...]).astype(MM)
                row += k

    def steps_of(first, count):
        return pl.BlockSpec((S, tk), lambda i: (0, jnp.clip(i - first, 0, count - 1)))

    return pl.pallas_call(
        body, name=name, grid=(last + 1,),
        in_specs=[steps_of(first, count) for first, count in zip(firsts, counts)]
        + [pl.BlockSpec((S, p.shape[1]), lambda i: (0, 0)) for p in narrow] + [_resident()],
        out_specs=pl.BlockSpec((tk, N), lambda i: (i, 0)), out_shape=jax.ShapeDtypeStruct(((last + 1) * tk, N), MM),
        compiler_params=_params("parallel"),
    )(*pieces, b)


def _pre_attn_bwd(x, dh1, dz, dff_t, win_t, wt_rest, g1, tm, after=None):
    S = x.shape[0]

    def body(x_ref, dh1_ref, dq_ref, dk_ref, dv_ref, dsq_ref, dsk_ref, dsv_ref, dff_ref, wf_ref, wr_ref, g1_ref,
             dx_ref, dg1_ref):
        i = pl.program_id(0)
        dz_fox = jnp.concatenate([dq_ref[...].astype(MM), dk_ref[...], dv_ref[...]], axis=1)
        dz_swa = jnp.concatenate([dsq_ref[...], dsk_ref[...].astype(MM), dsv_ref[...].astype(MM)], axis=1)
        da = (_dot(dz_fox, wf_ref[...]) + _dot(dz_swa, wr_ref[WT_SQ:WT_REST])
              + _dot_tn(dff_ref[...].astype(MM), wr_ref[0:WT_SQ]))
        n1, r1 = _rms(x_ref[...])
        dx, dg1 = _rms_bwd(da, n1, r1, g1_ref[...])
        _accumulate(dg1_ref, dg1, i)
        dx_ref[...] = dh1_ref[...] + dx

    body, in_specs, operands = _run_after(
        after, body,
        [_rows(tm, D_MODEL), _rows(tm, D_MODEL), *[_rows(tm, d.shape[1]) for d in dz],
         pl.BlockSpec((16, tm), lambda i: (0, i)), _const((WT_FOX, D_MODEL)), _resident(), _const((1, D_MODEL))],
        (x, dh1, *dz, dff_t, win_t, wt_rest, g1))
    return pl.pallas_call(
        body, name="pre_attn_bwd", grid=(S // tm,), in_specs=in_specs,
        out_specs=[_rows(tm, D_MODEL), _const((1, D_MODEL))],
        out_shape=[jax.ShapeDtypeStruct((S, D_MODEL), F32), jax.ShapeDtypeStruct((1, D_MODEL), F32)],
        compiler_params=_params("arbitrary"),
    )(*operands)


def _weight_grad(a, b, name, tk, n_chunks=1, relu2=False):
    S, K = a.shape
    N = b.shape[1]
    cn = N // n_chunks

    def body(a_ref, b_ref, out_ref):
        av = a_ref[...]
        if relu2:
            av = jnp.square(jnp.maximum(av.astype(F32), 0.0))
        av = av.astype(MM)
        for j in range(n_chunks):
            val = _dot_tn(av, b_ref[:, j * cn:(j + 1) * cn].astype(MM)).astype(MM)
            if n_chunks > 1:
                out_ref[j] = val
            else:
                out_ref[...] = val

    if n_chunks > 1:
        out_spec = pl.BlockSpec((n_chunks, tk, cn), lambda i: (0, i, 0))
        out_shape = jax.ShapeDtypeStruct((n_chunks, K, cn), MM)
    else:
        out_spec = pl.BlockSpec((tk, N), lambda i: (i, 0))
        out_shape = jax.ShapeDtypeStruct((K, N), MM)
    return pl.pallas_call(
        body, name=name, grid=(K // tk,),
        in_specs=[pl.BlockSpec((S, tk), lambda i: (0, i)), _resident()],
        out_specs=out_spec, out_shape=out_shape, compiler_params=_params("parallel"),
    )(a, b)


def _weight_grad_two(a1, a2, b, name, tk):
    S, K1 = a1.shape
    K2 = a2.shape[1]
    N = b.shape[1]
    n1 = K1 // tk

    def body(a1_ref, a2_ref, b_ref, out_ref):
        av = jnp.where(pl.program_id(0) < n1, a1_ref[...], a2_ref[...])
        out_ref[...] = _dot_tn(av, b_ref[...]).astype(MM)

    return pl.pallas_call(
        body, name=name, grid=((K1 + K2) // tk,),
        in_specs=[pl.BlockSpec((S, tk), lambda i: (0, jnp.minimum(i, n1 - 1))),
                  pl.BlockSpec((S, tk), lambda i: (0, jnp.maximum(i - n1, 0))), _resident()],
        out_specs=pl.BlockSpec((tk, N), lambda i: (i, 0)), out_shape=jax.ShapeDtypeStruct((K1 + K2, N), MM),
        compiler_params=_params("parallel"),
    )(a1, a2, b)


def _place():
    return lax.axis_index("x"), lax.axis_index("y"), lax.axis_index("c")


def _all_gather_sequencer(stacks, name, collective_id):
    refs = [jax.new_ref(s, memory_space=pltpu.MemorySpace.HBM) for s in stacks]
    n = len(refs)

    @pl.kernel(mesh=plsc.ScalarSubcoreMesh(axis_name="sequencer", num_cores=1), name=name,
               scratch_types=(pltpu.SemaphoreType.DMA((7 * n,)), pltpu.SemaphoreType.DMA((7 * n,))),
               compiler_params=pltpu.CompilerParams(collective_id=collective_id))
    def launch(send_sems, recv_sems):
        x, y, c = _place()
        sibling = (x, y, 1 - c)
        chips = [(1 - x, y), (x, 1 - y), (1 - x, 1 - y)]
        peers = [sibling] + [(px, py, c) for px, py in chips]
        barrier = pltpu.get_barrier_semaphore()
        for peer in peers:
            pl.semaphore_signal(barrier, inc=1, device_id=peer, device_id_type=MESH)
        pl.semaphore_wait(barrier, len(peers))

        def copy(a, k, block, to):
            px, py, pc = block
            slot = refs[a].at[4 * px + 2 * py + pc]
            return _remote(slot, slot, send_sems, recv_sems, 7 * a + k, to)

        first = [copy(a, k, (x, y, c), peer) for a in range(n) for k, peer in enumerate(peers)]
        for cp in first:
            cp.start()
        passed = []
        for j, (px, py) in enumerate(chips):
            for a in range(n):
                copy(a, 1 + j, (px, py, c), sibling).wait_recv()
                passed.append(copy(a, 4 + j, (px, py, c), sibling))
                passed[-1].start()
        for a in range(n):
            copy(a, 0, (x, y, 1 - c), sibling).wait_recv()
            for j, (px, py) in enumerate(chips):
                copy(a, 4 + j, (px, py, 1 - c), sibling).wait_recv()
        for cp in first + passed:
            cp.wait_send()

    launch()
    return [ref[...] for ref in refs]


def _chip_sums(grads, others, name):
    n = len(grads)

    def body(c_ref, *refs):
        for g_ref, o_ref, out_ref in zip(refs[:n], refs[n:2 * n], refs[2 * n:]):
            out_ref[...] = (g_ref[...].astype(F32) + o_ref[...].astype(F32)).astype(out_ref.dtype)

    own = [pl.BlockSpec((None, None) + g.shape[2:], lambda k, c_ref: (k, c_ref[0], 0, 0)) for g in grads]
    chip = [pl.BlockSpec((None,) + g.shape[2:], lambda k, c_ref: (k, 0, 0)) for g in grads]
    return pl.pallas_call(
        body, name=name,
        grid_spec=pltpu.PrefetchScalarGridSpec(num_scalar_prefetch=1, grid=(4,), in_specs=own + chip, out_specs=chip),
        out_shape=[jax.ShapeDtypeStruct((4,) + g.shape[2:], MM) for g in grads],
        compiler_params=_params("parallel"),
    )(lax.axis_index("c").astype(jnp.int32).reshape(1), *grads, *others)


HBM_SPEC = pl.BlockSpec(memory_space=pltpu.HBM)
SEM_SPEC = pl.BlockSpec(memory_space=pltpu.SEMAPHORE)
DATAFLOW = pltpu.SideEffectType.DATAFLOW_SIDE_EFFECTING


def _exchange_start(name, arrays, n_copies, plan):
    n = len(arrays)

    def body(*refs):
        send_sems, recv_sems, token = refs[n], refs[n + 1], refs[2 * n + 2]
        for cp in plan(refs[:n], send_sems, recv_sems):
            cp.start()
        token[...] = jnp.zeros_like(token)

    out = pl.pallas_call(
        body, name=name,
        out_shape=(pltpu.SemaphoreType.DMA((n_copies,)), pltpu.SemaphoreType.DMA((n_copies,)),
                   *[pltpu.HBM(a.shape, a.dtype) for a in arrays], jax.ShapeDtypeStruct((1, D_MODEL), F32)),
        in_specs=[HBM_SPEC] * n,
        out_specs=(SEM_SPEC, SEM_SPEC, *[HBM_SPEC] * n, pl.BlockSpec(memory_space=pltpu.VMEM)),
        input_output_aliases={i: 2 + i for i in range(n)},
        compiler_params=pltpu.CompilerParams(has_side_effects=DATAFLOW),
    )(*[pltpu.with_memory_space_constraint(a, pltpu.HBM) for a in arrays])
    return (out[0], out[1]), list(out[2:2 + n]), out[2 + n]


def _exchange_wait(name, arrays, sems, after, plan):
    n = len(arrays)
    after = list(after) if isinstance(after, (list, tuple)) else [after]

    def body(*refs):
        send_sems, recv_sems = refs[n], refs[n + 1]
        for cp in plan(refs[:n], send_sems, recv_sems):
            cp.wait_send()
            cp.wait_recv()

    out = pl.pallas_call(
        body, name=name, out_shape=[pltpu.HBM(a.shape, a.dtype) for a in arrays],
        in_specs=[HBM_SPEC] * n + [SEM_SPEC, SEM_SPEC] + [pl.BlockSpec(memory_space=pl.ANY)] * len(after),
        out_specs=[HBM_SPEC] * n, input_output_aliases={i: i for i in range(n)},
        compiler_params=pltpu.CompilerParams(has_side_effects=DATAFLOW),
    )(*arrays, sems[0], sems[1], *after)
    return list(out)


def _remote(src, dst, send_sems, recv_sems, k, to):
    return pltpu.make_async_remote_copy(src_ref=src, dst_ref=dst, send_sem=send_sems.at[k], recv_sem=recv_sems.at[k],
                                        device_id=to, device_id_type=MESH)


def _plan_gather_direct(refs, send_sems, recv_sems):
    x, y, c = _place()
    me = 4 * x + 2 * y + c
    peers = [(x, y, 1 - c), (1 - x, y, c), (x, 1 - y, c), (1 - x, 1 - y, c)]
    return [_remote(ref.at[me], ref.at[me], send_sems, recv_sems, 4 * a + k, peer)
            for a, ref in enumerate(refs) for k, peer in enumerate(peers)]


def _plan_gather_pass_on(refs, send_sems, recv_sems):
    x, y, c = _place()
    chips = [(1 - x, y), (x, 1 - y), (1 - x, 1 - y)]
    return [_remote(ref.at[4 * px + 2 * py + c], ref.at[4 * px + 2 * py + c], send_sems, recv_sems, 3 * a + k,
                    (x, y, 1 - c))
            for a, ref in enumerate(refs) for k, (px, py) in enumerate(chips)]


def _plan_in_chip(refs, send_sems, recv_sems):
    n = len(refs) // 2
    x, y, c = _place()
    return [_remote(refs[a].at[:, 1 - c], refs[n + a], send_sems, recv_sems, a, (x, y, 1 - c)) for a in range(n)]


def _plan_between_chips(refs, send_sems, recv_sems):
    n = len(refs) // 2
    x, y, c = _place()
    chips = [(1 - x, y), (x, 1 - y), (1 - x, 1 - y)]
    return [_remote(refs[a].at[2 * px + py], refs[n + a].at[2 * x + y], send_sems, recv_sems, 3 * a + k, (px, py, c))
            for a in range(n) for k, (px, py) in enumerate(chips)]


def _plan_late_between(refs, send_sems, recv_sems):
    sums, land, small = refs
    x, y, c = _place()
    me = 4 * x + 2 * y + c
    copies = _plan_between_chips([sums, land], send_sems, recv_sems)
    peers = [(x ^ dx, y ^ dy, c ^ dc) for dx in range(2) for dy in range(2) for dc in range(2) if dx + dy + dc]
    return copies + [_remote(small.at[me], small.at[me], send_sems, recv_sems, 3 + k, peer)
                     for k, peer in enumerate(peers)]


def _adamw_math(w, g, m, v):
    m = ADAM_B1 * m + (1.0 - ADAM_B1) * g
    v = ADAM_B2 * v + (1.0 - ADAM_B2) * jnp.square(g)
    m_hat = m / (1.0 - ADAM_B1 ** ADAM_STEP)
    v_hat = v / (1.0 - ADAM_B2 ** ADAM_STEP)
    delta = -ADAM_LR * (m_hat / (jnp.sqrt(v_hat) + ADAM_EPS) + ADAM_WD * w)
    return delta, m, v


def _adamw_small(parts, w, m, v):
    n_parts = parts.shape[0]
    n_rows = len(SMALL_ROWS)
    names = SMALL_ROWS + ("b_forget", "swa_sinks", "rel_bias")
    shapes = [(1, D_MODEL)] * n_rows + [(1, N_HEADS), (1, N_HEADS), (N_HEADS, N_BUCKETS)]

    def body(p_ref, w_ref, m_ref, v_ref, *outs):
        g = p_ref[0]
        for k in range(1, n_parts):
            g = g + p_ref[k]
        delta, m_new, v_new = _adamw_math(w_ref[...], g, m_ref[...], v_ref[...])
        for kind, val in enumerate((g, delta, m_new, v_new)):
            o = outs[kind * len(names):(kind + 1) * len(names)]
            for i in range(n_rows):
                o[i][...] = val[i:i + 1]
            misc = val[n_rows:n_rows + 1]
            o[n_rows][...] = misc[:, :N_HEADS]
            o[n_rows + 1][...] = misc[:, N_HEADS:2 * N_HEADS]
            for h in range(N_HEADS):
                first = 2 * N_HEADS + h * N_BUCKETS
                o[n_rows + 2][h:h + 1, :] = misc[:, first:first + N_BUCKETS]
        outs[-1][...] = g[n_rows + 1:n_rows + 2, 0:1]

    out = pl.pallas_call(
        body, name="adamw_small", in_specs=[_resident()] * 4, out_specs=[_resident()] * (4 * len(names) + 1),
        out_shape=[jax.ShapeDtypeStruct(s, F32) for s in shapes * 4] + [jax.ShapeDtypeStruct((1, 1), F32)],
        compiler_params=_params(),
    )(parts, w, m, v)
    kinds =[dict(zip(names, out[kind * len(names):(kind + 1) * len(names)])) for kind in range(4)]
    kinds[0]["loss"] = out[-1]
    return kinds


def _adamw_chips(parts, sums, w, m, v, name):
    _, r, cdim = parts.shape
    tr = 256 if r % 256 == 0 else r
    apart = w.ndim == 3

    def body(chip_ref, p_ref, own_ref, w_ref, m_ref, v_ref, g_out, d_out, m_out, v_out):
        g = None
        for k in range(4):
            term = jnp.where(chip_ref[0] == k, own_ref[...], p_ref[k]).astype(F32)
            g = term if g is None else g + term
        get = (lambda ref: ref[:, 0, :]) if apart else (lambda ref: ref[...])
        delta, m_new, v_new = _adamw_math(get(w_ref), g, get(m_ref), get(v_ref))
        for ref, val in ((g_out, g), (d_out, delta), (m_out, m_new), (v_out, v_new)):
            if apart:
                ref[:, 0, :] = val
            else:
                ref[...] = val

    if apart:
        blk = pl.BlockSpec((tr, 1, cdim), lambda i, chip: (i, 0, 0))
        shape = (r, 1, cdim)
    else:
        blk = pl.BlockSpec((tr, cdim), lambda i, chip: (i, 0))
        shape = (r, cdim)
    my_chip = (2 * lax.axis_index("x") + lax.axis_index("y")).astype(jnp.int32).reshape(1)
    return pl.pallas_call(
        body, name=name,
        grid_spec=pltpu.PrefetchScalarGridSpec(
            num_scalar_prefetch=1, grid=(r // tr,),
            in_specs=[pl.BlockSpec((4, tr, cdim), lambda i, chip: (0, i, 0)),
                      pl.BlockSpec((None, tr, cdim), lambda i, chip: (chip[0], i, 0)), blk, blk, blk],
            out_specs=[blk] * 4),
        out_shape=[jax.ShapeDtypeStruct(shape, F32)] * 4,
        compiler_params=_params("parallel"),
    )(my_chip, parts, sums, w, m, v)


class _NoExchange:
    def __init__(self, weights):
        self.weights = weights

    def before_pre_attn(self):
        return None

    def after_fox_fwd(self, fox_o):
        return None

    def after_attention(self, swa_o):
        return self.weights

    def after_early_grads(self, grads):
        return None

    def after_swa_bwd(self, dsq):
        return None

    def after_w_in_grad(self, d_win):
        return None


def _slot_order(t, axis):
    shp = t.shape
    t = t.reshape(shp[:axis] + (2, 4, shp[axis] // N_HEADS) + shp[axis + 1:])
    return jnp.swapaxes(t, axis, axis + 1).reshape(shp)


def _head_order(t, axis):
    shp = t.shape
    t = t.reshape(shp[:axis] + (4, 2, shp[axis] // N_HEADS) + shp[axis + 1:])
    return jnp.swapaxes(t, axis, axis + 1).reshape(shp)


def _forward_backward(x, p, target, win_t, hooks, b_forget, rel_bias, sinks, g1, g2, g3, g4, g5):
    S = x.shape[0]
    tm = 512
    tm_mlp = 512
    t = 256
    q0 = 3 * D_ATT + N_HEADS
    win_t = win_t.reshape(D_IN, D_MODEL)
    wt_rest = jnp.concatenate(
        [win_t[WT_FOX:q0], jnp.zeros((8, D_MODEL), MM), _slot_order(win_t[q0:q0 + D_ATT], 0), win_t[q0 + D_ATT:]],
        axis=0)
    bcol = jnp.pad(b_forget.reshape(N_HEADS, 1), ((0, 8), (0, 0)))
    rel_bias_slot = rel_bias[:, np.array(SLOT_HEAD)]
    sinks_slot = sinks.reshape(N_HEADS)[np.array(SLOT_HEAD)]
    bucket = jnp.asarray(_swa_bucket_map())

    a, fqkv, sqkv, fft = _pre_attn(x, g1, win_t, wt_rest, tm, after=hooks.before_pre_attn())
    c_row = _forget_cumsum(fft, bcol)
    c_col = c_row[:N_HEADS].T
    c_row3 = c_row[:N_HEADS].reshape(N_HEADS, S // t, t)
    fox_o, fox_lse = _fox_fwd(fqkv, c_row3, tq=512, tk=t)
    biasm = _swa_bias(rel_bias_slot, bucket)
    swa_o, swa_lse = _swa_fwd(sqkv, biasm, sinks_slot, after=hooks.after_fox_fwd(fox_o))
    wout, w1, w2, wple, wg = hooks.after_attention(swa_o)
    wout_fox = wout[:D_ATT]
    wout_swa = _slot_order(wout[D_ATT:], 0)
    mix, h1, m = _post_attn(x, fox_o, swa_o, wout_fox, wout_swa, g2, g3, tm)
    u, y, h2 = _mlp_fwd(m, h1, w1, w2, g4, tm_mlp)
    dh2, dpe, dgl, dg5, loss = _ple_loss(h2, p, target, wg, wple, g5, tm)

    d_wple = _weight_grad(p, dpe, "grad_w_ple", tk=D_PLE, n_chunks=N_DEV)
    d_wg = _weight_grad(h2, dgl, "grad_w_ple_gate", tk=256)
    dh1, dy, du, dg4, dg3 = _mlp_bwd(dh2, y, h1, u, w1, w2, g4, g3, tm)
    d_w2 = _weight_grad(u, dy, "grad_w_ff2", tk=256, relu2=True)
    d_w1 = _weight_grad(m, du, "grad_w_ff1", tk=256, n_chunks=N_DEV)
    head = np.arange(D_ATT) // HEAD_DIM
    head_rows = jnp.asarray((head[None, :] == np.arange(N_HEADS)[:, None]).astype(np.float32))
    dmix, dcat, d_row, d_swa, dg2 = _attn_out_bwd(dh1, mix, fox_o, swa_o, wout_fox, wout_swa, g2, head_rows, tm)
    d_col = d_swa.T
    d_wout = _weight_grad_two(fox_o, swa_o, dmix, "grad_w_out", tk=256)
    d_wout = jnp.concatenate([d_wout[:D_ATT], _head_order(d_wout[D_ATT:], 0)], axis=0)
    d_wout = d_wout.reshape(N_DEV, D_MODEL // N_DEV, D_MODEL)
    early = dict(w_ff1=d_w1, w_ff2=d_w2.reshape(N_DEV, FF_CHUNK, D_MODEL), w_ple=d_wple,
                 w_ple_gate=d_wg.reshape(N_DEV, D_MODEL // N_DEV, D_MODEL), w_out=d_wout)

    dsq, dsk, dsv, d_rb_slot, d_sink_slot = _swa_bwd(sqkv, dcat, biasm, sinks_slot, bucket, swa_lse, d_col,
                                                     after=hooks.after_early_grads(early))
    lse_row3 = fox_lse.T.reshape(N_HEADS, S // t, t)
    d_row3 = d_row.reshape(N_HEADS, S // t, t)
    dq_fox, dk_fox, dv_fox, dc_col, dcq = _fox_bwd(fqkv, dcat, lse_row3, d_row3, c_col, tq=t, tk=512,
                                                  after=hooks.after_swa_bwd(dsq))
    dc_row = jnp.pad(dc_col.T + dcq.reshape(N_HEADS, S), ((0, 8), (0, 0)))
    dff_t, db, d_wff_t = _forget_bwd(dc_row, fft, bcol, a)
    dz = [dq_fox, dk_fox, dv_fox, dsq, dsk, dsv]
    d_wmain = _weight_grad_pieces(dz, a, "grad_w_in", tk=256)

    sq0 = 3 * D_ATT
    d_win = jnp.concatenate(
        [d_wmain[:sq0], d_wff_t[:N_HEADS].astype(MM), _head_order(d_wmain[sq0:sq0 + D_ATT], 0),
         d_wmain[sq0 + D_ATT:]], axis=0)
    d_win = d_win.reshape(N_DEV, D_IN // N_DEV, D_MODEL)
    grad_x, dg1 = _pre_attn_bwd(x, dh1, dz, dff_t, win_t, wt_rest, g1, tm, after=hooks.after_w_in_grad(d_win))
    big = dict(early, w_in=d_win)
    small = dict(b_forget=db[:N_HEADS].reshape(1, N_HEADS), rel_bias=d_rb_slot[:, np.array(HEAD_SLOT)],
                 swa_sinks=d_sink_slot[:, np.array(HEAD_SLOT)], g_attn_pre=dg1, g_attn_post=dg2, g_ff_pre=dg3,
                 g_ff_post=dg4, g_ple_post=dg5)
    return loss, grad_x, big, small


BIG = ("w_in", "w_out", "w_ff1", "w_ff2", "w_ple", "w_ple_gate")
SMALL_ROWS = ("g_attn_pre", "g_attn_post", "g_ff_pre", "g_ff_post", "g_ple_post")
WEIGHTS =("w_in", "b_forget", "w_out", "rel_bias", "swa_sinks", "g_attn_pre", "g_attn_post", "w_ff1", "w_ff2",
           "g_ff_pre", "g_ff_post", "w_ple", "w_ple_gate", "g_ple_post")


EARLY = ("w_ff1", "w_ff2", "w_ple", "w_ple_gate", "w_out")


class _Overlap:
    def __init__(self, later):
        self.later = later

    def before_pre_attn(self):
        self.gather_sems, self.later, token = _exchange_start("gather_rest_start", self.later, 4 * 5, _plan_gather_direct)
        return token

    def after_fox_fwd(self, fox_o):
        later = _exchange_wait("gather_rest_wait", self.later, self.gather_sems, fox_o, _plan_gather_direct)
        self.pass_sems, self.later, token = _exchange_start("gather_pass_on_start", later, 3 * 5, _plan_gather_pass_on)
        return token

    def after_attention(self, swa_o):
        wout_g, w1_g, w2_g, wple_g, wg_g = _exchange_wait("gather_pass_on_wait", self.later, self.pass_sems, swa_o,
                                                         _plan_gather_pass_on)
        return (wout_g.reshape(D_MODEL, D_MODEL), w1_g, w2_g.reshape(D_FF, D_MODEL),
                jnp.moveaxis(wple_g, 0, 1).reshape(D_PLE, D_MODEL), wg_g.reshape(D_MODEL, D_MODEL))

    def after_early_grads(self, grads):
        views = [grads[k].reshape((4, 2) + grads[k].shape[1:]) for k in EARLY]
        lands = [lax.empty((4,) + grads[k].shape[1:], MM) for k in EARLY]
        self.in_chip_sems, self.in_chip, token = _exchange_start("grads_in_chip_start", views + lands, len(EARLY),
                                                                 _plan_in_chip)
        return token

    def after_swa_bwd(self, dsq):
        arrays = _exchange_wait("grads_in_chip_wait", self.in_chip, self.in_chip_sems, dsq, _plan_in_chip)
        n = len(EARLY)
        sums = list(_chip_sums(arrays[:n], arrays[n:], "chip_sums_early"))
        lands = [lax.empty(s.shape, s.dtype) for s in sums]
        self.between_sems, self.between, token = _exchange_start("grads_between_chips_start", sums + lands, 3 * n,
                                                                 _plan_between_chips)
        return token

    def after_w_in_grad(self, d_win):
        self.late_in_chip_sems, self.late_in_chip, token = _exchange_start(
            "late_in_chip_start", [d_win.reshape((4, 2) + d_win.shape[1:]), lax.empty((4,) + d_win.shape[1:], MM)],
            1, _plan_in_chip)
        return token

    def finish(self, after):
        arrays = _exchange_wait("grads_between_chips_wait", self.between, self.between_sems, after,
                                _plan_between_chips)
        n = len(EARLY)
        self.sums = arrays[:n]
        return arrays[n:]


def _pack_small(t):
    rows = [t[k].reshape(1, D_MODEL) for k in SMALL_ROWS]
    misc = jnp.concatenate([t["b_forget"].reshape(-1), t["swa_sinks"].reshape(-1), t["rel_bias"].T.reshape(-1)])
    rows.append(jnp.pad(misc, (0, D_MODEL - misc.shape[0])).reshape(1, D_MODEL))
    rows.append(jnp.pad(t["loss"].reshape(-1), (0, D_MODEL - 1)).reshape(1, D_MODEL))
    rows.append(jnp.zeros((1, D_MODEL), F32))
    return jnp.concatenate(rows, axis=0).astype(F32)


def kernel(x, p, w_in, b_forget, w_out, rel_bias, swa_sinks, g_attn_pre, g_attn_post, w_ff1, w_ff2, g_ff_pre, g_ff_post, w_ple, w_ple_gate, g_ple_post, loss_target, m_w_in, m_b_forget, m_w_out, m_rel_bias, m_swa_sinks, m_g_attn_pre, m_g_attn_post, m_w_ff1, m_w_ff2, m_g_ff_pre, m_g_ff_post, m_w_ple, m_w_ple_gate, m_g_ple_post, v_w_in, v_b_forget, v_w_out, v_rel_bias, v_swa_sinks, v_g_attn_pre, v_g_attn_post, v_w_ff1, v_w_ff2, v_g_ff_pre, v_g_ff_post, v_w_ple, v_w_ple_gate, v_g_ple_post):
    w = dict(w_in=w_in, b_forget=b_forget, w_out=w_out, rel_bias=rel_bias, swa_sinks=swa_sinks,
             g_attn_pre=g_attn_pre, g_attn_post=g_attn_post, w_ff1=w_ff1, w_ff2=w_ff2, g_ff_pre=g_ff_pre,
             g_ff_post=g_ff_post, w_ple=w_ple, w_ple_gate=w_ple_gate, g_ple_post=g_ple_post)
    mom = dict(w_in=m_w_in, b_forget=m_b_forget, w_out=m_w_out, rel_bias=m_rel_bias, swa_sinks=m_swa_sinks,
               g_attn_pre=m_g_attn_pre, g_attn_post=m_g_attn_post, w_ff1=m_w_ff1, w_ff2=m_w_ff2,
               g_ff_pre=m_g_ff_pre, g_ff_post=m_g_ff_post, w_ple=m_w_ple, w_ple_gate=m_w_ple_gate,
               g_ple_post=m_g_ple_post)
    var = dict(w_in=v_w_in, b_forget=v_b_forget, w_out=v_w_out, rel_bias=v_rel_bias, swa_sinks=v_swa_sinks,
               g_attn_pre=v_g_attn_pre, g_attn_post=v_g_attn_post, w_ff1=v_w_ff1, w_ff2=v_w_ff2,
               g_ff_pre=v_g_ff_pre, g_ff_post=v_g_ff_post, w_ple=v_w_ple, w_ple_gate=v_w_ple_gate,
               g_ple_post=v_g_ple_post)

    turn = lambda t, k: t.T if k == "w_in" else t
    me = 4 * lax.axis_index("x") + 2 * lax.axis_index("y") + lax.axis_index("c")

    def stack(block):
        return lax.dynamic_update_slice_in_dim(lax.empty((N_DEV,) + block.shape, block.dtype), block[None], me, 0)

    stacks = [stack(turn(w[k][0], k).astype(MM)) for k in BIG]
    (win_g,), later = _all_gather_sequencer(stacks[:1], "all_gather_sequencer", 1), stacks[1:]
    hooks = _Overlap(later)
    loss, grad_x, big, small = _forward_backward(
        x[0], p[0, 0], loss_target[0], win_g, hooks, b_forget, rel_bias, swa_sinks,
        g_attn_pre, g_attn_post, g_ff_pre, g_ff_post, g_ple_post)
    out_g, out_d, out_m, out_v = {}, {}, {}, {}

    def update(k, part, own):
        if k == "w_in":
            there, back = (lambda t: jnp.transpose(t, (2, 0, 1))), (lambda t: jnp.transpose(t, (1, 2, 0)))
        else:
            there, back = (lambda t: t[0]), (lambda t: t[None])
        g, d, m_new, v_new = _adamw_chips(part, own, there(w[k]), there(mom[k]), there(var[k]), "adamw_" + k)
        out_g[k], out_d[k], out_m[k], out_v[k] = back(g), back(d), back(m_new), back(v_new)
        return d

    view, other = _exchange_wait("late_in_chip_wait", hooks.late_in_chip, hooks.late_in_chip_sems, grad_x,
                                 _plan_in_chip)
    (chip_sum,) = _chip_sums([view], [other], "chip_sum_w_in")
    small["loss"] = loss
    between_sems, between, token = _exchange_start(
        "late_between_chips_start", [chip_sum, lax.empty(chip_sum.shape, MM), stack(_pack_small(small))], 3 + 7,
        _plan_late_between)
    early_parts = hooks.finish(token)
    done = [update(k, part, own) for k, part, own in zip(EARLY, early_parts, hooks.sums)]
    chip_sum, part, small_all = _exchange_wait("late_between_chips_wait", between, between_sems, done,
                                               _plan_late_between)
    update("w_in", part, chip_sum)
    rep = {k: w[k] for k in w if k not in BIG}
    rep["loss"] = jnp.zeros((), F32)
    rep_m = {k: mom[k] for k in mom if k not in BIG}
    rep_m["loss"] = jnp.zeros((), F32)
    rep_v = {k: var[k] for k in var if k not in BIG}
    rep_v["loss"] = jnp.ones((), F32)
    g_s, d_s, m_s, v_s = _adamw_small(small_all, _pack_small(rep), _pack_small(rep_m), _pack_small(rep_v))
    for k in w:
        if k not in BIG:
            natural = (lambda t: t.T) if k == "rel_bias" else (lambda t: t)
            out_g[k], out_d[k], out_m[k], out_v[k] = natural(g_s[k]), natural(d_s[k]), natural(m_s[k]), natural(v_s[k])
    return (g_s["loss"].reshape(()), grad_x[None], *[out_g[k] for k in WEIGHTS], *[out_d[k] for k in WEIGHTS],
            *[out_m[k] for k in WEIGHTS], *[out_v[k] for k in WEIGHTS])
```
